```python
import jax, jax.numpy as jnp
from jax import lax
import numpy as np

D_MODEL = 1024
BATCH = 8
SEQ = 4096
DEPTH = 1

MEM_LEN = 256
EPS = 1e-6

SB_HEADS = 16
SB_HEAD_DIM = 64
SB_WIDTH = SB_HEADS * SB_HEAD_DIM
SB_BLOCK = 128

SSD_EXPAND = 2
SSD_INNER = SSD_EXPAND * D_MODEL
SSD_HEAD_DIM = 64
SSD_HEADS = SSD_INNER // SSD_HEAD_DIM
SSD_GROUPS = 4
SSD_HEADS_PER_GROUP = SSD_HEADS // SSD_GROUPS
SSD_STATE = 128
SSD_CONV = 4
SSD_CHUNK = 128
SSD_CONV_DIM = SSD_INNER + 2 * SSD_GROUPS * SSD_STATE

MEM_HEADS = 4
MEM_HEAD_DIM = 256
MEM_WIDTH = MEM_HEADS * MEM_HEAD_DIM

N_BRANCHES = 3
D_FF = 4 * D_MODEL

IN_SIZES = (3 * SB_WIDTH, SSD_INNER, SSD_CONV_DIM, SSD_HEADS, MEM_WIDTH, N_BRANCHES * D_MODEL)
D_IN_PROJ = int(sum(IN_SIZES))
SPLIT_POINTS = tuple(int(v) for v in np.cumsum(IN_SIZES)[:-1])

kernel_name = "hybrid_gated_stickbreak_ssd_memxattn_block"


def rms_norm(x, gain):
    xf = x.astype(jnp.float32)
    y = xf * lax.rsqrt(jnp.mean(xf * xf, axis=-1, keepdims=True) + EPS)
    return (y * gain.astype(jnp.float32)).astype(x.dtype)


def stick_breaking_attention(q, k, v):
    b, h, s, dh = q.shape
    nb = s // SB_BLOCK
    scale = dh ** -0.5
    qb = q.reshape(b, h, nb, SB_BLOCK, dh).transpose(2, 0, 1, 3, 4)
    key_pos = jnp.arange(s)

    def block(args):
        q_blk, blk_idx = args
        q_pos = blk_idx * SB_BLOCK + jnp.arange(SB_BLOCK)
        z = jnp.einsum("bhqd,bhkd->bhqk", q_blk, k).astype(jnp.float32) * scale
        causal = key_pos[None, :] < q_pos[:, None]
        log_beta = jax.nn.log_sigmoid(z)
        log_keep = jnp.where(causal, jax.nn.log_sigmoid(-z), 0.0)
        suffix = lax.cumsum(log_keep, axis=3, reverse=True) - log_keep
        w = jnp.where(causal, jnp.exp(log_beta + suffix), 0.0)
        return jnp.einsum("bhqk,bhkd->bhqd", w.astype(v.dtype), v)

    out = lax.map(block, (qb, jnp.arange(nb)))
    return out.transpose(1, 2, 0, 3, 4).reshape(b, h, s, dh)


def causal_depthwise_conv(x, w, bias):
    kw, c = w.shape
    y = lax.conv_general_dilated(
        x, w[:, None, :].astype(x.dtype), window_strides=(1,), padding=((kw - 1, 0),),
        dimension_numbers=("NWC", "WIO", "NWC"), feature_group_count=c)
    return y + bias.astype(x.dtype)


def ssd_chunked(xh, dt, a, bmat, cmat):
    b, s, g, r, p = xh.shape
    n = bmat.shape[-1]
    l = SSD_CHUNK
    c = s // l
    x = (xh * dt[..., None]).reshape(b, c, l, g, r, p)
    da = (dt.astype(jnp.float32) * a.astype(jnp.float32)).reshape(b, c, l, g, r)
    bm = bmat.reshape(b, c, l, g, n)
    cm = cmat.reshape(b, c, l, g, n)
    a_cs = jnp.cumsum(da, axis=2).transpose(0, 1, 3, 4, 2)

    tri = jnp.tril(jnp.ones((l, l), dtype=bool))
    seg = a_cs[..., :, None] - a_cs[..., None, :]
    decay = jnp.exp(jnp.where(tri, seg, -jnp.inf))
    cb = jnp.einsum("bclgn,bcsgn->bcgls", cm, bm).astype(jnp.float32)
    w_intra = cb[:, :, :, None] * decay
    y_diag = jnp.einsum("bcgrls,bcsgrp->bclgrp", w_intra, x.astype(jnp.float32))

    decay_to_end = jnp.exp(a_cs[..., -1:] - a_cs)
    states = jnp.einsum("bcsgn,bcgrs,bcsgrp->bcgrpn", bm.astype(jnp.float32), decay_to_end,
                        x.astype(jnp.float32))
    chunk_decay = jnp.exp(a_cs[..., -1])

    def step(hstate, inp):
        st, dec = inp
        return hstate * dec[..., None, None] + st, hstate

    h0 = jnp.zeros((b, g, r, p, n), jnp.float32)
    _, prev = lax.scan(step, h0, (states.transpose(1, 0, 2, 3, 4, 5), chunk_decay.transpose(1, 0, 2, 3)))
    prev = prev.transpose(1, 0, 2, 3, 4, 5)
    y_off = jnp.einsum("bclgn,bcgrpn,bcgrl->bclgrp", cm.astype(jnp.float32), prev, jnp.exp(a_cs))
    return (y_diag + y_off).reshape(b, s, g, r, p).astype(xh.dtype)


def ssd_branch(z, xbc_raw, dt_raw, conv_w, conv_b, dt_bias, a_log, d_skip, ssd_norm):
    b, s, _ = z.shape
    xbc = jax.nn.silu(causal_depthwise_conv(xbc_raw, conv_w, conv_b))
    xs, bmat, cmat = jnp.split(xbc, (SSD_INNER, SSD_INNER + SSD_GROUPS * SSD_STATE), axis=-1)
    xh = xs.reshape(b, s, SSD_GROUPS, SSD_HEADS_PER_GROUP, SSD_HEAD_DIM)
    bmat = bmat.reshape(b, s, SSD_GROUPS, SSD_STATE)
    cmat = cmat.reshape(b, s, SSD_GROUPS, SSD_STATE)
    dt = jax.nn.softplus((dt_raw + dt_bias).astype(jnp.float32)).astype(xs.dtype)
    dt = dt.reshape(b, s, SSD_GROUPS, SSD_HEADS_PER_GROUP)
    a = -jnp.exp(a_log.astype(jnp.float32)).reshape(SSD_GROUPS, SSD_HEADS_PER_GROUP)
    d = d_skip.reshape(SSD_GROUPS, SSD_HEADS_PER_GROUP)
    y = ssd_chunked(xh, dt, a, bmat, cmat) + d[..., None] * xh
    y = y.reshape(b, s, SSD_INNER) * jax.nn.silu(z)
    yg = y.reshape(b, s, SSD_GROUPS, SSD_INNER // SSD_GROUPS).astype(jnp.float32)
    yg = yg * lax.rsqrt(jnp.mean(yg * yg, axis=-1, keepdims=True) + EPS)
    return (yg.reshape(b, s, SSD_INNER) * ssd_norm.astype(jnp.float32)).astype(z.dtype)


def mem_cross_attention(q, mem, norm_mem, w_mem_kv):
    b, s, _ = q.shape
    m = mem.shape[1]
    kv = rms_norm(mem, norm_mem) @ w_mem_kv
    k, v = jnp.split(kv, 2, axis=-1)
    qh = q.reshape(b, s, MEM_HEADS, MEM_HEAD_DIM)
    kh = k.reshape(b, m, MEM_HEADS, MEM_HEAD_DIM)
    vh = v.reshape(b, m, MEM_HEADS, MEM_HEAD_DIM)
    scores = jnp.einsum("bshd,bmhd->bhsm", qh, kh).astype(jnp.float32) * (MEM_HEAD_DIM ** -0.5)
    probs = jax.nn.softmax(scores, axis=-1).astype(v.dtype)
    return jnp.einsum("bhsm,bmhd->bshd", probs, vh).reshape(b, s, MEM_WIDTH)


def _fwd_setup_inputs(seed: int = 0) -> dict:
    key = jax.random.key(seed)
    ks = jax.random.split(key, 24)
    nrm = jax.random.normal
    f32 = jnp.float32

    def gain(k, n):
        return 1.0 + 0.05 * nrm(k, (DEPTH, n), f32)

    dt0 = jnp.exp(jax.random.uniform(ks[6], (DEPTH, SSD_HEADS), f32, minval=np.log(1e-3), maxval=np.log(1e-1)))
    dt_bias = dt0 + jnp.log(-jnp.expm1(-dt0))
    return {
        "x": nrm(ks[0], (BATCH, SEQ, D_MODEL), f32),
        "mem": nrm(ks[1], (BATCH, MEM_LEN, D_MODEL), f32),
        "norm_mix_pre": gain(ks[2], D_MODEL),
        "w_in": nrm(ks[3], (DEPTH, D_MODEL, D_IN_PROJ), f32) * D_MODEL ** -0.5,
        "conv_w": nrm(ks[4], (DEPTH, SSD_CONV, SSD_CONV_DIM), f32) * SSD_CONV ** -0.5,
        "conv_b": 0.02 * nrm(ks[5], (DEPTH, SSD_CONV_DIM), f32),
        "dt_bias": dt_bias,
        "a_log": jnp.log(jax.random.uniform(ks[7], (DEPTH, SSD_HEADS), f32, minval=1.0, maxval=16.0)),
        "d_skip": 1.0 + 0.1 * nrm(ks[8], (DEPTH, SSD_HEADS), f32),
        "ssd_norm": gain(ks[9], SSD_INNER),
        "norm_mem": gain(ks[10], D_MODEL),
        "w_mem_kv": nrm(ks[11], (DEPTH, D_MODEL, 2 * MEM_WIDTH), f32) * D_MODEL ** -0.5,
        "w_sb_out": nrm(ks[12], (DEPTH, SB_WIDTH, D_MODEL), f32) * SB_WIDTH ** -0.5,
        "w_ssd_out": nrm(ks[13], (DEPTH, SSD_INNER, D_MODEL), f32) * SSD_INNER ** -0.5,
        "w_mem_out": nrm(ks[14], (DEPTH, MEM_WIDTH, D_MODEL), f32) * MEM_WIDTH ** -0.5,
        "w_o": nrm(ks[15], (DEPTH, D_MODEL, D_MODEL), f32) * D_MODEL ** -0.5,
        "norm_mix_post": gain(ks[16], D_MODEL),
        "norm_mlp_pre": gain(ks[17], D_MODEL),
        "w_up": nrm(ks[18], (DEPTH, D_MODEL, D_FF), f32) * D_MODEL ** -0.5,
        "w_down": nrm(ks[19], (DEPTH, D_FF, D_MODEL), f32) * D_FF ** -0.5,
        "norm_mlp_post": gain(ks[20], D_MODEL),
    }


def _fwd_reference(x, mem, norm_mix_pre, w_in, conv_w, conv_b, dt_bias, a_log, d_skip, ssd_norm,
              norm_mem, w_mem_kv, w_sb_out, w_ssd_out, w_mem_out, w_o, norm_mix_post,
              norm_mlp_pre, w_up, w_down, norm_mlp_post):
    b, s, _ = x.shape
    h = x
    for layer in range(DEPTH):
        u = rms_norm(h, norm_mix_pre[layer])
        proj = u @ w_in[layer]
        sb_qkv, z, xbc_raw, dt_raw, mem_q, gate_logits = jnp.split(proj, SPLIT_POINTS, axis=-1)

        q, k, v = jnp.split(sb_qkv, 3, axis=-1)
        to_heads = lambda t: t.reshape(b, s, SB_HEADS, SB_HEAD_DIM).transpose(0, 2, 1, 3)
        y_sb = stick_breaking_attention(to_heads(q), to_heads(k), to_heads(v))
        y_sb = y_sb.transpose(0, 2, 1, 3).reshape(b, s, SB_WIDTH)

        y_ssd = ssd_branch(z, xbc_raw, dt_raw, conv_w[layer], conv_b[layer], dt_bias[layer],
                           a_log[layer], d_skip[layer], ssd_norm[layer])

        y_mem = mem_cross_attention(mem_q, mem, norm_mem[layer], w_mem_kv[layer])

        gates = jax.nn.sigmoid(gate_logits.astype(jnp.float32)).astype(h.dtype)
        gates = gates.reshape(b, s, N_BRANCHES, D_MODEL)
        merged = (gates[:, :, 0] * (y_sb @ w_sb_out[layer])
                  + gates[:, :, 1] * (y_ssd @ w_ssd_out[layer])
                  + gates[:, :, 2] * (y_mem @ w_mem_out[layer]))
        mix = merged @ w_o[layer]
        h = h + rms_norm(mix, norm_mix_post[layer])

        u = rms_norm(h, norm_mlp_pre[layer])
        ff = jnp.square(jax.nn.relu(u @ w_up[layer])) @ w_down[layer]
        h = h + rms_norm(ff, norm_mlp_post[layer])
    return h


import jax as _jax
import jax.numpy as _jnp

TWIN_FORMAT = 'train_step'
FWD_PARAMS = ['x', 'mem', 'norm_mix_pre', 'w_in', 'conv_w', 'conv_b', 'dt_bias', 'a_log', 'd_skip', 'ssd_norm', 'norm_mem', 'w_mem_kv', 'w_sb_out', 'w_ssd_out', 'w_mem_out', 'w_o', 'norm_mix_post', 'norm_mlp_pre', 'w_up', 'w_down', 'norm_mlp_post']
TWIN_WEIGHTS = ['norm_mix_pre', 'w_in', 'conv_w', 'conv_b', 'dt_bias', 'a_log', 'd_skip', 'ssd_norm', 'norm_mem', 'w_mem_kv', 'w_sb_out', 'w_ssd_out', 'w_mem_out', 'w_o', 'norm_mix_post', 'norm_mlp_pre', 'w_up', 'w_down', 'norm_mlp_post']
TWIN_DIFF_INPUT = 'x'
TWIN_INPUTS = ['x', 'mem', 'norm_mix_pre', 'w_in', 'conv_w', 'conv_b', 'dt_bias', 'a_log', 'd_skip', 'ssd_norm', 'norm_mem', 'w_mem_kv', 'w_sb_out', 'w_ssd_out', 'w_mem_out', 'w_o', 'norm_mix_post', 'norm_mlp_pre', 'w_up', 'w_down', 'norm_mlp_post', 'loss_target', 'm_norm_mix_pre', 'm_w_in', 'm_conv_w', 'm_conv_b', 'm_dt_bias', 'm_a_log', 'm_d_skip', 'm_ssd_norm', 'm_norm_mem', 'm_w_mem_kv', 'm_w_sb_out', 'm_w_ssd_out', 'm_w_mem_out', 'm_w_o', 'm_norm_mix_post', 'm_norm_mlp_pre', 'm_w_up', 'm_w_down', 'm_norm_mlp_post', 'v_norm_mix_pre', 'v_w_in', 'v_conv_w', 'v_conv_b', 'v_dt_bias', 'v_a_log', 'v_d_skip', 'v_ssd_norm', 'v_norm_mem', 'v_w_mem_kv', 'v_w_sb_out', 'v_w_ssd_out', 'v_w_mem_out', 'v_w_o', 'v_norm_mix_post', 'v_norm_mlp_pre', 'v_w_up', 'v_w_down', 'v_norm_mlp_post']
TWIN_OUTPUTS = ['loss', 'grad_x', 'grad_norm_mix_pre', 'grad_w_in', 'grad_conv_w', 'grad_conv_b', 'grad_dt_bias', 'grad_a_log', 'grad_d_skip', 'grad_ssd_norm', 'grad_norm_mem', 'grad_w_mem_kv', 'grad_w_sb_out', 'grad_w_ssd_out', 'grad_w_mem_out', 'grad_w_o', 'grad_norm_mix_post', 'grad_norm_mlp_pre', 'grad_w_up', 'grad_w_down', 'grad_norm_mlp_post', 'delta_norm_mix_pre', 'delta_w_in', 'delta_conv_w', 'delta_conv_b', 'delta_dt_bias', 'delta_a_log', 'delta_d_skip', 'delta_ssd_norm', 'delta_norm_mem', 'delta_w_mem_kv', 'delta_w_sb_out', 'delta_w_ssd_out', 'delta_w_mem_out', 'delta_w_o', 'delta_norm_mix_post', 'delta_norm_mlp_pre', 'delta_w_up', 'delta_w_down', 'delta_norm_mlp_post', 'new_m_norm_mix_pre', 'new_m_w_in', 'new_m_conv_w', 'new_m_conv_b', 'new_m_dt_bias', 'new_m_a_log', 'new_m_d_skip', 'new_m_ssd_norm', 'new_m_norm_mem', 'new_m_w_mem_kv', 'new_m_w_sb_out', 'new_m_w_ssd_out', 'new_m_w_mem_out', 'new_m_w_o', 'new_m_norm_mix_post', 'new_m_norm_mlp_pre', 'new_m_w_up', 'new_m_w_down', 'new_m_norm_mlp_post', 'new_v_norm_mix_pre', 'new_v_w_in', 'new_v_conv_w', 'new_v_conv_b', 'new_v_dt_bias', 'new_v_a_log', 'new_v_d_skip', 'new_v_ssd_norm', 'new_v_norm_mem', 'new_v_w_mem_kv', 'new_v_w_sb_out', 'new_v_w_ssd_out', 'new_v_w_mem_out', 'new_v_w_o', 'new_v_norm_mix_post', 'new_v_norm_mlp_pre', 'new_v_w_up', 'new_v_w_down', 'new_v_norm_mlp_post']
TWIN_LEAF_KINDS = {'loss': 'loss', 'grad_x': 'grad_x', 'grad_norm_mix_pre': 'grad_w', 'grad_w_in': 'grad_w', 'grad_conv_w': 'grad_w', 'grad_conv_b': 'grad_w', 'grad_dt_bias': 'grad_w', 'grad_a_log': 'grad_w', 'grad_d_skip': 'grad_w', 'grad_ssd_norm': 'grad_w', 'grad_norm_mem': 'grad_w', 'grad_w_mem_kv': 'grad_w', 'grad_w_sb_out': 'grad_w', 'grad_w_ssd_out': 'grad_w', 'grad_w_mem_out': 'grad_w', 'grad_w_o': 'grad_w', 'grad_norm_mix_post': 'grad_w', 'grad_norm_mlp_pre': 'grad_w', 'grad_w_up': 'grad_w', 'grad_w_down': 'grad_w', 'grad_norm_mlp_post': 'grad_w', 'delta_norm_mix_pre': 'delta_w', 'delta_w_in': 'delta_w', 'delta_conv_w': 'delta_w', 'delta_conv_b': 'delta_w', 'delta_dt_bias': 'delta_w', 'delta_a_log': 'delta_w', 'delta_d_skip': 'delta_w', 'delta_ssd_norm': 'delta_w', 'delta_norm_mem': 'delta_w', 'delta_w_mem_kv': 'delta_w', 'delta_w_sb_out': 'delta_w', 'delta_w_ssd_out': 'delta_w', 'delta_w_mem_out': 'delta_w', 'delta_w_o': 'delta_w', 'delta_norm_mix_post': 'delta_w', 'delta_norm_mlp_pre': 'delta_w', 'delta_w_up': 'delta_w', 'delta_w_down': 'delta_w', 'delta_norm_mlp_post': 'delta_w', 'new_m_norm_mix_pre': 'new_m', 'new_m_w_in': 'new_m', 'new_m_conv_w': 'new_m', 'new_m_conv_b': 'new_m', 'new_m_dt_bias': 'new_m', 'new_m_a_log': 'new_m', 'new_m_d_skip': 'new_m', 'new_m_ssd_norm': 'new_m', 'new_m_norm_mem': 'new_m', 'new_m_w_mem_kv': 'new_m', 'new_m_w_sb_out': 'new_m', 'new_m_w_ssd_out': 'new_m', 'new_m_w_mem_out': 'new_m', 'new_m_w_o': 'new_m', 'new_m_norm_mix_post': 'new_m', 'new_m_norm_mlp_pre': 'new_m', 'new_m_w_up': 'new_m', 'new_m_w_down': 'new_m', 'new_m_norm_mlp_post': 'new_m', 'new_v_norm_mix_pre': 'new_v', 'new_v_w_in': 'new_v', 'new_v_conv_w': 'new_v', 'new_v_conv_b': 'new_v', 'new_v_dt_bias': 'new_v', 'new_v_a_log': 'new_v', 'new_v_d_skip': 'new_v', 'new_v_ssd_norm': 'new_v', 'new_v_norm_mem': 'new_v', 'new_v_w_mem_kv': 'new_v', 'new_v_w_sb_out': 'new_v', 'new_v_w_ssd_out': 'new_v', 'new_v_w_mem_out': 'new_v', 'new_v_w_o': 'new_v', 'new_v_norm_mix_post': 'new_v', 'new_v_norm_mlp_pre': 'new_v', 'new_v_w_up': 'new_v', 'new_v_w_down': 'new_v', 'new_v_norm_mlp_post': 'new_v'}


def _forward(args):
    return _fwd_reference(*[args[k] for k in FWD_PARAMS])


def _output_shape():
    def fwd():
        inp = _fwd_setup_inputs(0)
        return _fwd_reference(*[inp[k] for k in FWD_PARAMS])
    out = _jax.eval_shape(fwd)
    return out.shape, out.dtype

N_MICROBATCH = 1
ADAM_LR = 0.001
ADAM_B1 = 0.9
ADAM_B2 = 0.999
ADAM_EPS = 1e-08
ADAM_WD = 0.01
ADAM_STEP = 10
PER_EXAMPLE_BATCH_AXIS = {'x': 0, 'mem': 0, 'loss_target': 0}
SHARED_INPUTS = []
_WEIGHT_DTYPES = {'norm_mix_pre': _jnp.float32, 'w_in': _jnp.float32, 'conv_w': _jnp.float32, 'conv_b': _jnp.float32, 'dt_bias': _jnp.float32, 'a_log': _jnp.float32, 'd_skip': _jnp.float32, 'ssd_norm': _jnp.float32, 'norm_mem': _jnp.float32, 'w_mem_kv': _jnp.float32, 'w_sb_out': _jnp.float32, 'w_ssd_out': _jnp.float32, 'w_mem_out': _jnp.float32, 'w_o': _jnp.float32, 'norm_mix_post': _jnp.float32, 'norm_mlp_pre': _jnp.float32, 'w_up': _jnp.float32, 'w_down': _jnp.float32, 'norm_mlp_post': _jnp.float32}
MOMENT_SCALE = {'norm_mix_pre': 7.422692e-01, 'w_in': 2.008346e-01, 'conv_w': 7.914909e-01, 'conv_b': 2.675172e+00, 'dt_bias': 6.324938e-01, 'a_log': 3.960618e+00, 'd_skip': 4.373996e+00, 'ssd_norm': 1.484239e+00, 'norm_mem': 2.097454e-01, 'w_mem_kv': 1.463560e-01, 'w_sb_out': 2.684503e-01, 'w_ssd_out': 2.144633e+00, 'w_mem_out': 2.197193e-01, 'w_o': 2.216379e+00, 'norm_mix_post': 3.223175e+01, 'norm_mlp_pre': 1.066959e+00, 'w_up': 5.340385e-01, 'w_down': 2.390174e+00, 'norm_mlp_post': 3.313018e+01}


def _to_microbatches(a, axis):
    t = _jnp.moveaxis(a, axis, 0)
    t = t.reshape((N_MICROBATCH, t.shape[0] // N_MICROBATCH) + t.shape[1:])
    return _jnp.moveaxis(t, 1, axis + 1)


def setup_inputs(seed: int = 0) -> dict:
    inp = _fwd_setup_inputs(seed)
    key = _jax.random.fold_in(_jax.random.key(seed), 7919)
    shape, _ = _output_shape()
    out = dict(inp)
    out["loss_target"] = _jax.random.normal(_jax.random.fold_in(key, 0), shape, _jnp.float32)
    for i, name in enumerate(TWIN_WEIGHTS):
        w = inp[name].astype(_jnp.float32)
        if MOMENT_SCALE is None:
            s = _jnp.sqrt(_jnp.mean(_jnp.square(w)) + 1e-30)
        else:
            s = MOMENT_SCALE[name]
        km, kv = _jax.random.split(_jax.random.fold_in(key, i + 1))
        out[name] = w
        out["m_" + name] = s * _jax.random.normal(km, w.shape, _jnp.float32)
        out["v_" + name] = (s * s) * _jax.random.uniform(kv, w.shape, _jnp.float32, 0.5, 1.5)
    if N_MICROBATCH > 1:
        for name, axis in PER_EXAMPLE_BATCH_AXIS.items():
            out[name] = _to_microbatches(out[name], axis)
    return {'x': out['x'], 'mem': out['mem'], 'norm_mix_pre': out['norm_mix_pre'], 'w_in': out['w_in'], 'conv_w': out['conv_w'], 'conv_b': out['conv_b'], 'dt_bias': out['dt_bias'], 'a_log': out['a_log'], 'd_skip': out['d_skip'], 'ssd_norm': out['ssd_norm'], 'norm_mem': out['norm_mem'], 'w_mem_kv': out['w_mem_kv'], 'w_sb_out': out['w_sb_out'], 'w_ssd_out': out['w_ssd_out'], 'w_mem_out': out['w_mem_out'], 'w_o': out['w_o'], 'norm_mix_post': out['norm_mix_post'], 'norm_mlp_pre': out['norm_mlp_pre'], 'w_up': out['w_up'], 'w_down': out['w_down'], 'norm_mlp_post': out['norm_mlp_post'], 'loss_target': out['loss_target'], 'm_norm_mix_pre': out['m_norm_mix_pre'], 'm_w_in': out['m_w_in'], 'm_conv_w': out['m_conv_w'], 'm_conv_b': out['m_conv_b'], 'm_dt_bias': out['m_dt_bias'], 'm_a_log': out['m_a_log'], 'm_d_skip': out['m_d_skip'], 'm_ssd_norm': out['m_ssd_norm'], 'm_norm_mem': out['m_norm_mem'], 'm_w_mem_kv': out['m_w_mem_kv'], 'm_w_sb_out': out['m_w_sb_out'], 'm_w_ssd_out': out['m_w_ssd_out'], 'm_w_mem_out': out['m_w_mem_out'], 'm_w_o': out['m_w_o'], 'm_norm_mix_post': out['m_norm_mix_post'], 'm_norm_mlp_pre': out['m_norm_mlp_pre'], 'm_w_up': out['m_w_up'], 'm_w_down': out['m_w_down'], 'm_norm_mlp_post': out['m_norm_mlp_post'], 'v_norm_mix_pre': out['v_norm_mix_pre'], 'v_w_in': out['v_w_in'], 'v_conv_w': out['v_conv_w'], 'v_conv_b': out['v_conv_b'], 'v_dt_bias': out['v_dt_bias'], 'v_a_log': out['v_a_log'], 'v_d_skip': out['v_d_skip'], 'v_ssd_norm': out['v_ssd_norm'], 'v_norm_mem': out['v_norm_mem'], 'v_w_mem_kv': out['v_w_mem_kv'], 'v_w_sb_out': out['v_w_sb_out'], 'v_w_ssd_out': out['v_w_ssd_out'], 'v_w_mem_out': out['v_w_mem_out'], 'v_w_o': out['v_w_o'], 'v_norm_mix_post': out['v_norm_mix_post'], 'v_norm_mlp_pre': out['v_norm_mlp_pre'], 'v_w_up': out['v_w_up'], 'v_w_down': out['v_w_down'], 'v_norm_mlp_post': out['v_norm_mlp_post']}


def _loss(weights, diff, rest, loss_target):
    with _jax.named_scope("forward"):
        args = {**rest, TWIN_DIFF_INPUT: diff, **{k: w.astype(_WEIGHT_DTYPES[k]) for k, w in weights.items()}}
        y = _forward(args)
    with _jax.named_scope("loss_head"):
        err = _jnp.square(y.astype(_jnp.float32) - loss_target)
        return 0.5 * _jnp.sum(_jnp.mean(err, axis=-1)) if err.ndim else 0.5 * err


def _adamw(w, g, m, v):
    m = ADAM_B1 * m + (1.0 - ADAM_B1) * g
    v = ADAM_B2 * v + (1.0 - ADAM_B2) * _jnp.square(g)
    m_hat = m / (1.0 - ADAM_B1 ** ADAM_STEP)
    v_hat = v / (1.0 - ADAM_B2 ** ADAM_STEP)
    delta = -ADAM_LR * (m_hat / (_jnp.sqrt(v_hat) + ADAM_EPS) + ADAM_WD * w)
    return delta, m, v


def reference(x, mem, norm_mix_pre, w_in, conv_w, conv_b, dt_bias, a_log, d_skip, ssd_norm, norm_mem, w_mem_kv, w_sb_out, w_ssd_out, w_mem_out, w_o, norm_mix_post, norm_mlp_pre, w_up, w_down, norm_mlp_post, loss_target, m_norm_mix_pre, m_w_in, m_conv_w, m_conv_b, m_dt_bias, m_a_log, m_d_skip, m_ssd_norm, m_norm_mem, m_w_mem_kv, m_w_sb_out, m_w_ssd_out, m_w_mem_out, m_w_o, m_norm_mix_post, m_norm_mlp_pre, m_w_up, m_w_down, m_norm_mlp_post, v_norm_mix_pre, v_w_in, v_conv_w, v_conv_b, v_dt_bias, v_a_log, v_d_skip, v_ssd_norm, v_norm_mem, v_w_mem_kv, v_w_sb_out, v_w_ssd_out, v_w_mem_out, v_w_o, v_norm_mix_post, v_norm_mlp_pre, v_w_up, v_w_down, v_norm_mlp_post):
    given = dict(x=x, mem=mem, norm_mix_pre=norm_mix_pre, w_in=w_in, conv_w=conv_w, conv_b=conv_b, dt_bias=dt_bias, a_log=a_log, d_skip=d_skip, ssd_norm=ssd_norm, norm_mem=norm_mem, w_mem_kv=w_mem_kv, w_sb_out=w_sb_out, w_ssd_out=w_ssd_out, w_mem_out=w_mem_out, w_o=w_o, norm_mix_post=norm_mix_post, norm_mlp_pre=norm_mlp_pre, w_up=w_up, w_down=w_down, norm_mlp_post=norm_mlp_post, loss_target=loss_target, m_norm_mix_pre=m_norm_mix_pre, m_w_in=m_w_in, m_conv_w=m_conv_w, m_conv_b=m_conv_b, m_dt_bias=m_dt_bias, m_a_log=m_a_log, m_d_skip=m_d_skip, m_ssd_norm=m_ssd_norm, m_norm_mem=m_norm_mem, m_w_mem_kv=m_w_mem_kv, m_w_sb_out=m_w_sb_out, m_w_ssd_out=m_w_ssd_out, m_w_mem_out=m_w_mem_out, m_w_o=m_w_o, m_norm_mix_post=m_norm_mix_post, m_norm_mlp_pre=m_norm_mlp_pre, m_w_up=m_w_up, m_w_down=m_w_down, m_norm_mlp_post=m_norm_mlp_post, v_norm_mix_pre=v_norm_mix_pre, v_w_in=v_w_in, v_conv_w=v_conv_w, v_conv_b=v_conv_b, v_dt_bias=v_dt_bias, v_a_log=v_a_log, v_d_skip=v_d_skip, v_ssd_norm=v_ssd_norm, v_norm_mem=v_norm_mem, v_w_mem_kv=v_w_mem_kv, v_w_sb_out=v_w_sb_out, v_w_ssd_out=v_w_ssd_out, v_w_mem_out=v_w_mem_out, v_w_o=v_w_o, v_norm_mix_post=v_norm_mix_post, v_norm_mlp_pre=v_norm_mlp_pre, v_w_up=v_w_up, v_w_down=v_w_down, v_norm_mlp_post=v_norm_mlp_post)
    weights = {n: given[n] for n in TWIN_WEIGHTS}
    shared = {n: given[n] for n in SHARED_INPUTS}
    per_example = {n: given[n] for n in ['x', 'mem']}
    grad_fn = _jax.value_and_grad(_loss, argnums=(0, 1))

    def one_microbatch(ex, loss_target):
        ex = dict(ex)
        diff = ex.pop(TWIN_DIFF_INPUT)
        return grad_fn(weights, diff, {**shared, **ex}, loss_target)

    if N_MICROBATCH == 1:
        loss, (grad_w, grad_x) = one_microbatch(per_example, given["loss_target"])
    else:
        def body(carry, xs):
            loss_sum, grad_sum = carry
            l_k, (gw_k, gx_k) = one_microbatch(xs[0], xs[1])
            with _jax.named_scope("update"):
                return (loss_sum + l_k, _jax.tree.map(_jnp.add, grad_sum, gw_k)), gx_k

        init = (_jnp.zeros((), _jnp.float32), _jax.tree.map(_jnp.zeros_like, weights))
        (loss, grad_w), grad_x = _jax.lax.scan(body, init, (per_example, given["loss_target"]))
    with _jax.named_scope("update"):
        delta_w, new_m, new_v = {}, {}, {}
        for n in TWIN_WEIGHTS:
            delta_w[n], new_m[n], new_v[n] = _adamw(weights[n], grad_w[n], given["m_" + n], given["v_" + n])
    return (loss, grad_x, *[grad_w[n] for n in TWIN_WEIGHTS], *[delta_w[n] for n in TWIN_WEIGHTS],
            *[new_m[n] for n in TWIN_WEIGHTS], *[new_v[n] for n in TWIN_WEIGHTS])
```

```python
import functools

import jax
import jax.numpy as jnp
from jax import lax
from jax.experimental import pallas as pl
from jax.experimental.pallas import tpu as pltpu

f32 = jnp.float32
bf16 = jnp.bfloat16

D = 1024
EPS = 1e-6
SB_HD = 64
SSD_INNER = 2048
SSD_HEADS = 32
SSD_GROUPS = 4
SSD_N = 128
SSD_L = 128
CONV_K = 4
CONV_DIM = 3072
MEM_HEADS = 4
MEM_HD = 256
D_FF = 4096
D_IN = 12320
N_SHARD = 4
N_DEV = 8

P_QKV, P_XBC, P_GATE, P_MEMQ, P_Z, P_DT, P_TOT = 0, 3072, 6144, 9216, 10240, 12288, 12416
R_QKV, R_Z, R_XBC, R_DT, R_MEMQ, R_GATE = (0, 3072), (3072, 5120), (5120, 8192), (8192, 8224), (8224, 9248), (9248, 12320)

ADAM_LR = 0.001
ADAM_B1 = 0.9
ADAM_B2 = 0.999
ADAM_EPS = 1e-08
ADAM_WD = 0.01
ADAM_STEP = 10

VMEM_LIMIT = 56 * 1024 * 1024

NN = (((1,), (0,)), ((), ()))
NT = (((1,), (1,)), ((), ()))
TN = (((0,), (0,)), ((), ()))


def _dot(a, b, dims=NN):
    return lax.dot_general(a, b, dims, preferred_element_type=f32)


def _params(sem=None):
    return pltpu.CompilerParams(dimension_semantics=sem, vmem_limit_bytes=VMEM_LIMIT)


def _sigmoid(x):
    return 1.0 / (1.0 + jnp.exp(-x))


def _split2(x):
    hi = x.astype(bf16)
    lo = (x - hi.astype(f32)).astype(bf16)
    return hi, lo


def _split3(x):
    hi = x.astype(bf16)
    r = x - hi.astype(f32)
    mid = r.astype(bf16)
    lo = (r - mid.astype(f32)).astype(bf16)
    return hi, mid, lo


def _mm(a, b, mode, *, tm, tn, name, out_dtypes=(f32,), epi=None, extras=()):
    M = a.shape[1] if mode == "tn" else a.shape[0]
    N = b.shape[0] if mode == "nt" else b.shape[1]
    tm, tn = min(tm, M), min(tn, N)
    if mode == "nn":
        (M, K), N = a.shape, b.shape[1]
        a_spec = pl.BlockSpec((tm, K), lambda i, j: (i, 0))
        b_spec = pl.BlockSpec((K, tn), lambda i, j: (0, j))
        dims = NN
    elif mode == "nt":
        (M, K), N = a.shape, b.shape[0]
        a_spec = pl.BlockSpec((tm, K), lambda i, j: (i, 0))
        b_spec = pl.BlockSpec((tn, K), lambda i, j: (j, 0))
        dims = NT
    else:
        (K, M), N = a.shape, b.shape[1]
        a_spec = pl.BlockSpec((K, tm), lambda i, j: (0, i))
        b_spec = pl.BlockSpec((K, tn), lambda i, j: (0, j))
        dims = TN
    assert M % tm == 0 and N % tn == 0, (name, M, N, tm, tn)
    n_ex = len(extras)
    o_spec = pl.BlockSpec((tm, tn), lambda i, j: (i, j))

    def body(a_ref, b_ref, *rest):
        acc = _dot(a_ref[...].astype(bf16), b_ref[...].astype(bf16), dims)
        res = (acc,) if epi is None else epi(acc, *[e[...] for e in rest[:n_ex]])
        for o_ref, r in zip(rest[n_ex:], res):
            o_ref[...] = r.astype(o_ref.dtype)

    out = pl.pallas_call(
        body, name=name, grid=(M // tm, N // tn),
        in_specs=[a_spec, b_spec] + [o_spec] * n_ex,
        out_specs=[o_spec] * len(out_dtypes),
        out_shape=[jax.ShapeDtypeStruct((M, N), dt) for dt in out_dtypes],
        compiler_params=_params(("parallel", "parallel")),
    )(a, b, *extras)
    return out[0] if len(out_dtypes) == 1 else out


def _rms_fwd(x, g, *, name, out_dtype, residual=None, tm=512):
    S, C = x.shape
    tm = min(tm, S)
    has_res = residual is not None

    def body(x_ref, g_ref, *rest):
        xv = x_ref[...]
        r = lax.rsqrt(jnp.mean(xv * xv, axis=1, keepdims=True) + EPS)
        y = xv * r * g_ref[...]
        if has_res:
            y = y + rest[0][...]
        rest[-1][...] = y.astype(out_dtype)

    row = pl.BlockSpec((tm, C), lambda i: (i, 0))
    vec = pl.BlockSpec((1, C), lambda i: (0, 0))
    args = (x, g) + ((residual,) if has_res else ())
    return pl.pallas_call(
        body, name=name, grid=(S // tm,),
        in_specs=[row, vec] + ([row] if has_res else []),
        out_specs=row, out_shape=jax.ShapeDtypeStruct((S, C), out_dtype),
        compiler_params=_params(("parallel",)),
    )(*args)


def _rms_bwd(x, dy, g, *, name, out_dtype, add=None, tm=512):
    S, C = x.shape
    tm = min(tm, S)
    has_add = add is not None

    def body(x_ref, dy_ref, g_ref, *rest):
        dx_ref, dg_ref = rest[-2], rest[-1]
        xv = x_ref[...]
        dyv = dy_ref[...].astype(f32)
        r = lax.rsqrt(jnp.mean(xv * xv, axis=1, keepdims=True) + EPS)
        xh = xv * r
        dxh = dyv * g_ref[...]
        dx = r * (dxh - xh * jnp.mean(dxh * xh, axis=1, keepdims=True))
        if has_add:
            dx = dx + rest[0][...]
        dx_ref[...] = dx.astype(out_dtype)

        @pl.when(pl.program_id(0) == 0)
        def _():
            dg_ref[...] = jnp.zeros_like(dg_ref)

        dg_ref[...] += jnp.sum(dyv * xh, axis=0, keepdims=True)

    row = pl.BlockSpec((tm, C), lambda i: (i, 0))
    vec = pl.BlockSpec((1, C), lambda i: (0, 0))
    args = (x, dy, g) + ((add,) if has_add else ())
    return pl.pallas_call(
        body, name=name, grid=(S // tm,),
        in_specs=[row, row, vec] + ([row] if has_add else []),
        out_specs=[row, vec],
        out_shape=[jax.ShapeDtypeStruct((S, C), out_dtype), jax.ShapeDtypeStruct((1, C), f32)],
        compiler_params=_params(("arbitrary",)),
    )(*args)


SB_T = 128


def _sb_masks():
    lane = lax.broadcasted_iota(jnp.int32, (1, 128), 1)
    m_a = (lane < SB_HD).astype(f32)
    return m_a, 1.0 - m_a


def _sb_tile(z, carry, u_mat, mask, sign=1.0):
    l1p = jnp.log(1.0 + jnp.exp(-jnp.abs(z)))
    lb = jnp.minimum(z, 0.0) - l1p
    lk = lb - z
    if mask is not None:
        lk = jnp.where(mask, lk, 0.0)
    hi, lo = _split2(lk)
    part = _dot(hi, u_mat) + _dot(lo, u_mat)
    w = jnp.exp(lb + carry + sign * part)
    if mask is not None:
        w = jnp.where(mask, w, 0.0)
    return w, lb, jnp.sum(lk, axis=1, keepdims=True)


def _sb_fwd(proj, S):
    nq = S // SB_T
    n_pairs = D // 128
    scale = SB_HD ** -0.5

    def body(q_ref, k_ref, v_ref, o_ref, t_ref):
        i = pl.program_id(1)
        m_a, m_b = _sb_masks()
        r_i = lax.broadcasted_iota(jnp.int32, (SB_T, SB_T), 0)
        c_i = lax.broadcasted_iota(jnp.int32, (SB_T, SB_T), 1)
        u_mat = (r_i > c_i).astype(bf16)
        causal = c_i < r_i
        q = q_ref[...] * scale
        q_h = ((q * m_a).astype(bf16), (q * m_b).astype(bf16))

        def tile(j, carry, mask):
            acc, c_a, c_b = carry
            rows = pl.ds(pl.multiple_of(j * SB_T, SB_T), SB_T)
            k = k_ref[rows, :].astype(bf16)
            v = v_ref[rows, :]
            cs = []
            for hd, (m, c) in enumerate(((m_a, c_a), (m_b, c_b))):
                z = _dot(q_h[hd], k, NT)
                w, _, tot = _sb_tile(z, c, u_mat, mask)
                acc = acc + _dot(w.astype(bf16), (v * m).astype(bf16))
                cs.append(c + tot)
            return acc, cs[0], cs[1]

        zero_c = jnp.zeros((SB_T, 1), f32)
        carry = tile(i, (jnp.zeros((SB_T, 128), f32), zero_c, zero_c), causal)
        carry = lax.fori_loop(0, i, lambda jj, cr: tile(i - 1 - jj, cr, None), carry)
        o_ref[...] = carry[0]
        lane = lax.broadcasted_iota(jnp.int32, (1, 128), 1)
        t_ref[...] = jnp.where(lane == 0, carry[1], 0.0) + jnp.where(lane == SB_HD, carry[2], 0.0)

    qs = pl.BlockSpec((SB_T, 128), lambda h, i: (i, h))
    return pl.pallas_call(
        body, name="sb_fwd", grid=(n_pairs, nq),
        in_specs=[qs,
                  pl.BlockSpec((S, 128), lambda h, i: (0, n_pairs + h)),
                  pl.BlockSpec((S, 128), lambda h, i: (0, 2 * n_pairs + h))],
        out_specs=[qs, qs], out_shape=[jax.ShapeDtypeStruct((S, D), f32)] * 2,
        compiler_params=_params(("parallel", "arbitrary")),
    )(proj, proj, proj)


def _sb_bwd(proj, tot_lk, do, S):
    nq = S // SB_T
    n_pairs = D // 128
    scale = SB_HD ** -0.5

    def body(q_ref, k_ref, v_ref, t_ref, do_ref, dq_ref, dk_ref, dv_ref, dk_acc, dv_acc):
        i = pl.program_id(1)
        m_a, m_b = _sb_masks()
        r_i = lax.broadcasted_iota(jnp.int32, (SB_T, SB_T), 0)
        c_i = lax.broadcasted_iota(jnp.int32, (SB_T, SB_T), 1)
        u_inc = (r_i <= c_i).astype(bf16)
        u_exc = (r_i < c_i).astype(bf16)
        causal = c_i < r_i

        @pl.when(i == 0)
        def _():
            dk_acc[...] = jnp.zeros_like(dk_acc)
            dv_acc[...] = jnp.zeros_like(dv_acc)

        q = q_ref[...] * scale
        dov = do_ref[...]
        tv = t_ref[...]
        heads = []
        for m in (m_a, m_b):
            heads.append(((q * m).astype(bf16), (dov * m).astype(bf16), jnp.sum(tv * m, axis=1, keepdims=True), m))

        def tile(j, carry, mask):
            dq_acc, cp_a, cp_b, ce_a, ce_b = carry
            rows = pl.ds(pl.multiple_of(j * SB_T, SB_T), SB_T)
            k_f = k_ref[rows, :]
            k = k_f.astype(bf16)
            v = v_ref[rows, :].astype(bf16)
            dk_t = jnp.zeros((SB_T, 128), f32)
            dv_t = jnp.zeros((SB_T, 128), f32)
            cps, ces = [], []
            for (q_b, do_b, tot, m), cp, ce in zip(heads, (cp_a, cp_b), (ce_a, ce_b)):
                z = _dot(q_b, k, NT)
                w, lb, blk = _sb_tile(z, tot - cp, u_inc, mask, sign=-1.0)
                e = _dot(do_b, v, NT) * w
                big_e = ce + _dot(e.astype(bf16), u_exc)
                sig = jnp.exp(lb)
                dz = e * (1.0 - sig) - big_e * sig
                if mask is not None:
                    dz = jnp.where(mask, dz, 0.0)
                dz_b = dz.astype(bf16)
                dq_acc = dq_acc + _dot(dz_b, (k_f * m).astype(bf16))
                dk_t = dk_t + _dot(dz_b, q_b, TN)
                dv_t = dv_t + _dot(w.astype(bf16), do_b, TN)
                cps.append(cp + blk)
                ces.append(ce + jnp.sum(e, axis=1, keepdims=True))
            dk_acc[rows, :] += dk_t
            dv_acc[rows, :] += dv_t
            return dq_acc, cps[0], cps[1], ces[0], ces[1]

        zc = jnp.zeros((SB_T, 1), f32)
        carry = (jnp.zeros((SB_T, 128), f32), zc, zc, zc, zc)
        carry = lax.fori_loop(0, i, lambda j, cr: tile(j, cr, None), carry)
        carry = tile(i, carry, causal)
        dq_ref[...] = (carry[0] * scale).astype(bf16)

        @pl.when(i == nq - 1)
        def _():
            dk_ref[...] = dk_acc[...].astype(bf16)
            dv_ref[...] = dv_acc[...].astype(bf16)

    qs = pl.BlockSpec((SB_T, 128), lambda h, i: (i, h))
    full = pl.BlockSpec((S, 128), lambda h, i: (0, h))
    dq, dk, dv = pl.pallas_call(
        body, name="sb_bwd", grid=(n_pairs, nq),
        in_specs=[qs,
                  pl.BlockSpec((S, 128), lambda h, i: (0, n_pairs + h)),
                  pl.BlockSpec((S, 128), lambda h, i: (0, 2 * n_pairs + h)),
                  qs, qs],
        out_specs=[qs, full, full],
        out_shape=[jax.ShapeDtypeStruct((S, D), bf16)] * 3,
        scratch_shapes=[pltpu.VMEM((S, 128), f32), pltpu.VMEM((S, 128), f32)],
        compiler_params=_params(("parallel", "arbitrary")),
    )(proj, proj, proj, tot_lk, do)
    return dq, dk, dv


CONV_CB = 256
HALO = 8


def _conv_fwd(proj, conv_w, conv_b, S):
    tr = min(512, S)

    def body(x_ref, w_ref, b_ref, xc_ref, xbc_ref):
        w = w_ref[...]
        for t in range(S // tr):
            cur = x_ref[t * tr:(t + 1) * tr, :]
            halo = x_ref[t * tr - HALO:t * tr, :] if t else jnp.zeros((HALO, CONV_CB), f32)
            win = jnp.concatenate([halo, cur], axis=0)
            acc = b_ref[...] + w[CONV_K - 1:CONV_K, :] * cur
            for k in range(CONV_K - 1):
                acc = acc + w[k:k + 1, :] * pltpu.roll(win, CONV_K - 1 - k, 0)[HALO:, :]
            xc_ref[t * tr:(t + 1) * tr, :] = acc
            xbc_ref[t * tr:(t + 1) * tr, :] = acc * _sigmoid(acc)

    col = pl.BlockSpec((S, CONV_CB), lambda c: (0, c))
    return pl.pallas_call(
        body, name="conv_fwd", grid=(CONV_DIM // CONV_CB,),
        in_specs=[pl.BlockSpec((S, CONV_CB), lambda c: (0, P_XBC // CONV_CB + c)),
                  pl.BlockSpec((CONV_K, CONV_CB), lambda c: (0, c)),
                  pl.BlockSpec((1, CONV_CB), lambda c: (0, c))],
        out_specs=[col, col], out_shape=[jax.ShapeDtypeStruct((S, CONV_DIM), f32)] * 2,
        compiler_params=_params(("parallel",)),
    )(proj, conv_w, conv_b)


def _conv_bwd(proj, xc, dxbc, conv_w, S):
    tr = min(512, S)

    def body(x_ref, xc_ref, dy_ref, w_ref, dx_ref, dw_ref, db_ref, dxc_s):
        w = w_ref[...]
        xcv = xc_ref[...]
        sg = _sigmoid(xcv)
        dxc_s[0:S, :] = dy_ref[...] * (sg * (1.0 + xcv * (1.0 - sg)))
        dxc_s[S:S + HALO, :] = jnp.zeros((HALO, CONV_CB), f32)
        dws = [jnp.zeros((1, CONV_CB), f32) for _ in range(CONV_K)]
        db = jnp.zeros((1, CONV_CB), f32)
        for t in range(S // tr):
            cur = x_ref[t * tr:(t + 1) * tr, :]
            halo = x_ref[t * tr - HALO:t * tr, :] if t else jnp.zeros((HALO, CONV_CB), f32)
            win = jnp.concatenate([halo, cur], axis=0)
            dwin = dxc_s[t * tr:(t + 1) * tr + HALO, :]
            dcur = dwin[0:tr, :]
            db = db + jnp.sum(dcur, axis=0, keepdims=True)
            dws[CONV_K - 1] = dws[CONV_K - 1] + jnp.sum(dcur * cur, axis=0, keepdims=True)
            dx = w[CONV_K - 1:CONV_K, :] * dcur
            for k in range(CONV_K - 1):
                sh = CONV_K - 1 - k
                dws[k] = dws[k] + jnp.sum(dcur * pltpu.roll(win, sh, 0)[HALO:, :], axis=0, keepdims=True)
                dx = dx + w[k:k + 1, :] * pltpu.roll(dwin, tr + HALO - sh, 0)[0:tr, :]
            dx_ref[t * tr:(t + 1) * tr, :] = dx.astype(bf16)
        dw_ref[...] = jnp.concatenate(dws + [jnp.zeros((8 - CONV_K, CONV_CB), f32)], axis=0)
        db_ref[...] = db

    col = pl.BlockSpec((S, CONV_CB), lambda c: (0, c))
    return pl.pallas_call(
        body, name="conv_bwd", grid=(CONV_DIM // CONV_CB,),
        in_specs=[pl.BlockSpec((S, CONV_CB), lambda c: (0, P_XBC // CONV_CB + c)), col, col,
                  pl.BlockSpec((CONV_K, CONV_CB), lambda c: (0, c))],
        out_specs=[col, pl.BlockSpec((8, CONV_CB), lambda c: (0, c)), pl.BlockSpec((1, CONV_CB), lambda c: (0, c))],
        out_shape=[jax.ShapeDtypeStruct((S, CONV_DIM), bf16), jax.ShapeDtypeStruct((8, CONV_DIM), f32),
                   jax.ShapeDtypeStruct((1, CONV_DIM), f32)],
        scratch_shapes=[pltpu.VMEM((S + HALO, CONV_CB), f32)],
        compiler_params=_params(("parallel",)),
    )(proj, xc, dxbc, conv_w)


N_PAIR = SSD_HEADS // 2
NEG = -1e30


def _softplus(x):
    return jnp.maximum(x, 0.0) + jnp.log(1.0 + jnp.exp(-jnp.abs(x)))


def _ssd_common(dtr, dtb, alog):
    L = SSD_L
    r_i = lax.broadcasted_iota(jnp.int32, (L, L), 0)
    c_i = lax.broadcasted_iota(jnp.int32, (L, L), 1)
    dt = _softplus(dtr + dtb)
    a = -jnp.exp(alog)
    da = dt * a
    lower = (r_i >= c_i).astype(bf16)
    upper = (r_i <= c_i).astype(bf16)
    parts = _split3(da)
    a_cs = sum(_dot(lower, p) for p in parts)
    a_cs_t = sum(_dot(p, upper, TN) for p in parts)
    return dt, a, a_cs, a_cs_t, r_i >= c_i


def _pair_vec(lane, v, h):
    return jnp.where(lane < SB_HD, v[:, h:h + 1], v[:, h + 1:h + 2])


def _decay_mat(a_cs, a_cs_t, h, tril):
    return jnp.exp(jnp.where(tril, a_cs[:, h:h + 1] - a_cs_t[h:h + 1, :], NEG))


def _ssd_fwd(xbc, proj, pdt, dt_bias_p, a_log_p, d_skip_c, ssd_norm, S):
    L = SSD_L
    nc = S // L

    def body(xbc_ref, dt_ref, z_ref, dtb_ref, alog_ref, dsk_ref, gn_ref, y_ref, yn_ref, hp_ref, state):
        c = pl.program_id(0)

        @pl.when(c == 0)
        def _():
            state[...] = jnp.zeros_like(state)

        hp_ref[0] = state[...]
        lane = lax.broadcasted_iota(jnp.int32, (1, 128), 1)
        row128 = lax.broadcasted_iota(jnp.int32, (128, 1), 0)
        m_a, m_b = _sb_masks()
        dt, a, a_cs, a_cs_t, tril = _ssd_common(dt_ref[...], dtb_ref[...], alog_ref[...])
        a_last = a_cs[L - 1:L, :]
        for g in range(SSD_GROUPS):
            b_g = xbc_ref[:, SSD_INNER + g * SSD_N:SSD_INNER + (g + 1) * SSD_N].astype(bf16)
            c_g = xbc_ref[:, SSD_INNER + (SSD_GROUPS + g) * SSD_N:SSD_INNER + (SSD_GROUPS + g + 1) * SSD_N].astype(bf16)
            cb = _dot(c_g, b_g, NT)
            for pr in range(4):
                h = 8 * g + 2 * pr
                pi = h // 2
                cols = slice(pi * 128, (pi + 1) * 128)
                xs = xbc_ref[:, cols]
                x = xs * _pair_vec(lane, dt, h)
                acs = _pair_vec(lane, a_cs, h)
                al = _pair_vec(lane, a_last, h)
                w_a = (cb * _decay_mat(a_cs, a_cs_t, h, tril)).astype(bf16)
                w_b = (cb * _decay_mat(a_cs, a_cs_t, h + 1, tril)).astype(bf16)
                yd = _dot(w_a, (x * m_a).astype(bf16)) + _dot(w_b, (x * m_b).astype(bf16))
                hp = state[pi]
                yo = _dot(c_g, hp.astype(bf16), NT) * jnp.exp(acs)
                y_ref[:, cols] = yd + yo + dsk_ref[:, cols] * xs
                dec = jnp.exp(jnp.where(row128 < SB_HD, a_last[:, h:h + 1], a_last[:, h + 1:h + 2]))
                state[pi] = hp * dec + _dot((x * jnp.exp(al - acs)).astype(bf16), b_g, TN)
        zz = z_ref[...]
        y2 = y_ref[...] * (zz * _sigmoid(zz))
        gw = SSD_INNER // SSD_GROUPS
        for g in range(SSD_GROUPS):
            yg = y2[:, g * gw:(g + 1) * gw]
            rg = lax.rsqrt(jnp.mean(yg * yg, axis=1, keepdims=True) + EPS)
            yn_ref[:, g * gw:(g + 1) * gw] = (yg * rg * gn_ref[:, g * gw:(g + 1) * gw]).astype(bf16)

    vec128 = pl.BlockSpec((1, 128), lambda c: (0, 0))
    vecin = pl.BlockSpec((1, SSD_INNER), lambda c: (0, 0))
    rows = pl.BlockSpec((L, SSD_INNER), lambda c: (c, 0))
    return pl.pallas_call(
        body, name="ssd_fwd", grid=(nc,),
        in_specs=[pl.BlockSpec((L, CONV_DIM), lambda c: (c, 0)),
                  pl.BlockSpec((L, 128), lambda c: (c, 0)),
                  pl.BlockSpec((L, SSD_INNER), lambda c: (c, P_Z // SSD_INNER)),
                  vec128, vec128, vecin, vecin],
        out_specs=[rows, rows, pl.BlockSpec((1, N_PAIR, 128, SSD_N), lambda c: (c, 0, 0, 0))],
        out_shape=[jax.ShapeDtypeStruct((S, SSD_INNER), f32), jax.ShapeDtypeStruct((S, SSD_INNER), bf16),
                   jax.ShapeDtypeStruct((nc, N_PAIR, 128, SSD_N), f32)],
        scratch_shapes=[pltpu.VMEM((N_PAIR, 128, SSD_N), f32)],
        compiler_params=_params(("arbitrary",)),
    )(xbc, pdt, proj, dt_bias_p, a_log_p, d_skip_c, ssd_norm)


def _sum_all(v):
    return jnp.sum(jnp.sum(v, axis=1, keepdims=True), axis=0, keepdims=True)


def _ssd_bwd(dyn, y, xbc, proj, pdt, hprev, dt_bias_p, a_log_p, d_skip_c, ssd_norm, S):
    L = SSD_L
    nc = S // L

    def body(dyn_ref, y_ref, xbc_ref, dt_ref, z_ref, hp_ref, dtb_ref, alog_ref, dsk_ref, gn_ref,
             dz_ref, dxbc_ref, ddt_ref, dgn_ref, dsk_out, dalog_ref, ddtb_ref, dstate, dy_s):
        c = pl.program_id(0)

        @pl.when(c == 0)
        def _():
            dstate[...] = jnp.zeros_like(dstate)
            dgn_ref[...] = jnp.zeros_like(dgn_ref)
            dsk_out[...] = jnp.zeros_like(dsk_out)
            dalog_ref[...] = jnp.zeros_like(dalog_ref)
            ddtb_ref[...] = jnp.zeros_like(ddtb_ref)

        lane = lax.broadcasted_iota(jnp.int32, (1, 128), 1)
        row128 = lax.broadcasted_iota(jnp.int32, (128, 1), 0)
        rowl = lax.broadcasted_iota(jnp.int32, (L, 1), 0)
        m_a, m_b = _sb_masks()
        dtr = dt_ref[...]
        dt, a, a_cs, a_cs_t, tril = _ssd_common(dtr, dtb_ref[...], alog_ref[...])
        a_last = a_cs[L - 1:L, :]

        zz = z_ref[...]
        sg = _sigmoid(zz)
        silu = zz * sg
        yv = y_ref[...]
        y2 = yv * silu
        gw = SSD_INNER // SSD_GROUPS
        for g in range(SSD_GROUPS):
            sl = slice(g * gw, (g + 1) * gw)
            yg = y2[:, sl]
            rg = lax.rsqrt(jnp.mean(yg * yg, axis=1, keepdims=True) + EPS)
            yh = yg * rg
            dyn_g = dyn_ref[:, sl]
            dgn_ref[:, sl] += jnp.sum(dyn_g * yh, axis=0, keepdims=True)
            dyh = dyn_g * gn_ref[:, sl]
            dy2 = rg * (dyh - yh * jnp.mean(dyh * yh, axis=1, keepdims=True))
            dy_s[:, sl] = dy2 * silu[:, sl]
            dz_ref[:, sl] = (dy2 * yv[:, sl] * (sg[:, sl] * (1.0 + zz[:, sl] * (1.0 - sg[:, sl])))).astype(bf16)

        d_acs = jnp.zeros((L, 128), f32)
        ddt_x = jnp.zeros((L, 128), f32)
        dsk_acc = jnp.zeros((1, 128), f32)
        for g in range(SSD_GROUPS):
            bsl = slice(SSD_INNER + g * SSD_N, SSD_INNER + (g + 1) * SSD_N)
            csl = slice(SSD_INNER + (SSD_GROUPS + g) * SSD_N, SSD_INNER + (SSD_GROUPS + g + 1) * SSD_N)
            b_g = xbc_ref[:, bsl].astype(bf16)
            c_g = xbc_ref[:, csl].astype(bf16)
            cb = _dot(c_g, b_g, NT)
            dcb = jnp.zeros((L, L), f32)
            dc_g = jnp.zeros((L, SSD_N), f32)
            db_g = jnp.zeros((L, SSD_N), f32)
            for pr in range(4):
                h = 8 * g + 2 * pr
                pi = h // 2
                cols = slice(pi * 128, (pi + 1) * 128)
                xs = xbc_ref[:, cols]
                dt_p = _pair_vec(lane, dt, h)
                x = xs * dt_p
                acs = _pair_vec(lane, a_cs, h)
                al = _pair_vec(lane, a_last, h)
                e_a = jnp.exp(acs)
                dte = jnp.exp(al - acs)
                m_mat_a = _decay_mat(a_cs, a_cs_t, h, tril)
                m_mat_b = _decay_mat(a_cs, a_cs_t, h + 1, tril)
                dyp = dy_s[:, cols]
                dsk = dsk_ref[:, cols]
                d_hn = dstate[pi]
                hp = hp_ref[0, pi]
                dy_a = (dyp * m_a).astype(bf16)
                dy_b = (dyp * m_b).astype(bf16)
                x_b = x.astype(bf16)
                gm_a = _dot(dy_a, x_b, NT) * m_mat_a
                gm_b = _dot(dy_b, x_b, NT) * m_mat_b
                dcb = dcb + gm_a + gm_b
                dx_d = _dot((cb * m_mat_a).astype(bf16), dy_a, TN) + _dot((cb * m_mat_b).astype(bf16), dy_b, TN)
                dx_s = _dot(b_g, d_hn.astype(bf16), NT) * dte
                dx = dx_d + dx_s
                dxbc_ref[:, cols] = dx * dt_p + dsk * dyp
                xdxs = x * dx_s
                u = dyp * (_dot(c_g, hp.astype(bf16), NT) * e_a) - xdxs
                hh = d_hn * hp
                dec = jnp.exp(jnp.where(row128 < SB_HD, a_last[:, h:h + 1], a_last[:, h + 1:h + 2]))
                for hd, m, gm in ((h, m_a, gm_a), (h + 1, m_b, gm_b)):
                    half = slice(0, SB_HD) if hd == h else slice(SB_HD, 128)
                    last = _sum_all(xdxs * m) + jnp.exp(a_last[:, hd:hd + 1]) * _sum_all(hh[half, :])
                    qm = gm * cb
                    col = jnp.sum(qm - qm.T, axis=1, keepdims=True) + jnp.sum(u * m, axis=1, keepdims=True)
                    col = col + jnp.where(rowl == L - 1, last, 0.0)
                    d_acs = jnp.where(lane == hd, col, d_acs)
                    ddt_x = jnp.where(lane == hd, jnp.sum(dx * xs * m, axis=1, keepdims=True), ddt_x)
                    dsk_acc = jnp.where(lane == hd, _sum_all(dyp * xs * m), dsk_acc)
                dye = (dyp * e_a).astype(bf16)
                dc_g = dc_g + _dot(dye, hp.astype(bf16))
                db_g = db_g + _dot((x * dte).astype(bf16), d_hn.astype(bf16))
                dstate[pi] = dec * d_hn + _dot(dye, c_g, TN)
            dcb_b = dcb.astype(bf16)
            dxbc_ref[:, csl] = dc_g + _dot(dcb_b, b_g)
            dxbc_ref[:, bsl] = db_g + _dot(dcb_b, c_g, TN)

        r_i = lax.broadcasted_iota(jnp.int32, (L, L), 0)
        c_i = lax.broadcasted_iota(jnp.int32, (L, L), 1)
        rev = (r_i <= c_i).astype(bf16)
        dda = sum(_dot(rev, p) for p in _split3(d_acs))
        ddt = ddt_x + dda * a
        dalog_ref[...] += jnp.sum(dda * dt, axis=0, keepdims=True) * a
        ddtr = jnp.where(lane < SSD_HEADS, ddt * _sigmoid(dtr + dtb_ref[...]), 0.0)
        ddt_ref[...] = ddtr.astype(bf16)
        ddtb_ref[...] += jnp.sum(ddtr, axis=0, keepdims=True)
        dsk_out[...] += dsk_acc

    rv = lambda c: nc - 1 - c
    vec128 = pl.BlockSpec((1, 128), lambda c: (0, 0))
    vecin = pl.BlockSpec((1, SSD_INNER), lambda c: (0, 0))
    rows = pl.BlockSpec((L, SSD_INNER), lambda c: (rv(c), 0))
    return pl.pallas_call(
        body, name="ssd_bwd", grid=(nc,),
        in_specs=[rows, rows,
                  pl.BlockSpec((L, CONV_DIM), lambda c: (rv(c), 0)),
                  pl.BlockSpec((L, 128), lambda c: (rv(c), 0)),
                  pl.BlockSpec((L, SSD_INNER), lambda c: (rv(c), P_Z // SSD_INNER)),
                  pl.BlockSpec((1, N_PAIR, 128, SSD_N), lambda c: (rv(c), 0, 0, 0)),
                  vec128, vec128, vecin, vecin],
        out_specs=[rows, pl.BlockSpec((L, CONV_DIM), lambda c: (rv(c), 0)),
                   pl.BlockSpec((L, 128), lambda c: (rv(c), 0)), vecin, vec128, vec128, vec128],
        out_shape=[jax.ShapeDtypeStruct((S, SSD_INNER), bf16), jax.ShapeDtypeStruct((S, CONV_DIM), f32),
                   jax.ShapeDtypeStruct((S, 128), bf16), jax.ShapeDtypeStruct((1, SSD_INNER), f32),
                   jax.ShapeDtypeStruct((1, 128), f32), jax.ShapeDtypeStruct((1, 128), f32),
                   jax.ShapeDtypeStruct((1, 128), f32)],
        scratch_shapes=[pltpu.VMEM((N_PAIR, 128, SSD_N), f32), pltpu.VMEM((L, SSD_INNER), f32)],
        compiler_params=_params(("arbitrary",)),
    )(dyn, y, xbc, pdt, proj, hprev, dt_bias_p, a_log_p, d_skip_c, ssd_norm)


MEM_W = MEM_HEADS * MEM_HD


def _mem_probs(q, k):
    s = _dot(q, k, NT) * (MEM_HD ** -0.5)
    s = s - jnp.max(s, axis=1, keepdims=True)
    p = jnp.exp(s)
    return p / jnp.sum(p, axis=1, keepdims=True)


def _mem_fwd(proj, kv, S, tm=512):
    tm = min(tm, S)
    M = kv.shape[0]

    def body(q_ref, kv_ref, o_ref):
        for h in range(MEM_HEADS):
            sl = slice(h * MEM_HD, (h + 1) * MEM_HD)
            vsl = slice(MEM_W + h * MEM_HD, MEM_W + (h + 1) * MEM_HD)
            p = _mem_probs(q_ref[:, sl].astype(bf16), kv_ref[:, sl].astype(bf16))
            o_ref[:, sl] = _dot(p.astype(bf16), kv_ref[:, vsl].astype(bf16)).astype(bf16)

    return pl.pallas_call(
        body, name="mem_fwd", grid=(S // tm,),
        in_specs=[pl.BlockSpec((tm, MEM_W), lambda i: (i, P_MEMQ // MEM_W)),
                  pl.BlockSpec((M, 2 * MEM_W), lambda i: (0, 0))],
        out_specs=pl.BlockSpec((tm, MEM_W), lambda i: (i, 0)),
        out_shape=jax.ShapeDtypeStruct((S, MEM_W), bf16),
        compiler_params=_params(("parallel",)),
    )(proj, kv)


def _mem_bwd(proj, kv, dy, S, tm=512):
    tm = min(tm, S)
    M = kv.shape[0]
    scale = MEM_HD ** -0.5

    def body(q_ref, kv_ref, dy_ref, dq_ref, dkv_ref):
        @pl.when(pl.program_id(0) == 0)
        def _():
            dkv_ref[...] = jnp.zeros_like(dkv_ref)

        for h in range(MEM_HEADS):
            sl = slice(h * MEM_HD, (h + 1) * MEM_HD)
            vsl = slice(MEM_W + h * MEM_HD, MEM_W + (h + 1) * MEM_HD)
            q = q_ref[:, sl].astype(bf16)
            k = kv_ref[:, sl].astype(bf16)
            v = kv_ref[:, vsl].astype(bf16)
            dyh = dy_ref[:, sl].astype(bf16)
            p = _mem_probs(q, k)
            dp = _dot(dyh, v, NT)
            ds = (p * (dp - jnp.sum(dp * p, axis=1, keepdims=True)) * scale).astype(bf16)
            dq_ref[:, sl] = _dot(ds, k).astype(bf16)
            dkv_ref[:, sl] += _dot(ds, q, TN)
            dkv_ref[:, vsl] += _dot(p.astype(bf16), dyh, TN)

    return pl.pallas_call(
        body, name="mem_bwd", grid=(S // tm,),
        in_specs=[pl.BlockSpec((tm, MEM_W), lambda i: (i, P_MEMQ // MEM_W)),
                  pl.BlockSpec((M, 2 * MEM_W), lambda i: (0, 0)),
                  pl.BlockSpec((tm, MEM_W), lambda i: (i, 0))],
        out_specs=[pl.BlockSpec((tm, MEM_W), lambda i: (i, 0)), pl.BlockSpec((M, 2 * MEM_W), lambda i: (0, 0))],
        out_shape=[jax.ShapeDtypeStruct((S, MEM_W), bf16), jax.ShapeDtypeStruct((M, 2 * MEM_W), f32)],
        compiler_params=_params(("arbitrary",)),
    )(proj, kv, dy)


def _merge_fwd(proj, t0, t1, t2, S, tm=512):
    tm = min(tm, S)

    def body(g_ref, t0_ref, t1_ref, t2_ref, o_ref):
        acc = jnp.zeros((tm, D), f32)
        for b, t_ref in enumerate((t0_ref, t1_ref, t2_ref)):
            acc = acc + _sigmoid(g_ref[:, b * D:(b + 1) * D]) * t_ref[...]
        o_ref[...] = acc.astype(bf16)

    row = pl.BlockSpec((tm, D), lambda i: (i, 0))
    return pl.pallas_call(
        body, name="merge_fwd", grid=(S // tm,),
        in_specs=[pl.BlockSpec((tm, 3 * D), lambda i: (i, P_GATE // (3 * D))), row, row, row],
        out_specs=row, out_shape=jax.ShapeDtypeStruct((S, D), bf16),
        compiler_params=_params(("parallel",)),
    )(proj, t0, t1, t2)


def _merge_bwd(proj, t0, t1, t2, dm, S, tm=512):
    tm = min(tm, S)

    def body(g_ref, t0_ref, t1_ref, t2_ref, dm_ref, d0_ref, d1_ref, d2_ref, dg_ref):
        dmv = dm_ref[...]
        for b, (t_ref, d_ref) in enumerate(((t0_ref, d0_ref), (t1_ref, d1_ref), (t2_ref, d2_ref))):
            sg = _sigmoid(g_ref[:, b * D:(b + 1) * D])
            d_ref[...] = (dmv * sg).astype(bf16)
            dg_ref[:, b * D:(b + 1) * D] = (dmv * t_ref[...] * sg * (1.0 - sg)).astype(bf16)

    row = pl.BlockSpec((tm, D), lambda i: (i, 0))
    return pl.pallas_call(
        body, name="merge_bwd", grid=(S // tm,),
        in_specs=[pl.BlockSpec((tm, 3 * D), lambda i: (i, P_GATE // (3 * D))), row, row, row, row],
        out_specs=[row, row, row, pl.BlockSpec((tm, 3 * D), lambda i: (i, 0))],
        out_shape=[jax.ShapeDtypeStruct((S, D), bf16)] * 3 + [jax.ShapeDtypeStruct((S, 3 * D), bf16)],
        compiler_params=_params(("parallel",)),
    )(proj, t0, t1, t2, dm)


def _loss_head(ff, g, h1, target, S, tm=512):
    tm = min(tm, S)

    def body(ff_ref, g_ref, h1_ref, t_ref, dh_ref, loss_ref):
        xv = ff_ref[...]
        r = lax.rsqrt(jnp.mean(xv * xv, axis=1, keepdims=True) + EPS)
        err = h1_ref[...] + xv * r * g_ref[...] - t_ref[...]
        dh_ref[...] = err * (1.0 / D)

        @pl.when(pl.program_id(0) == 0)
        def _():
            loss_ref[...] = jnp.zeros_like(loss_ref)

        loss_ref[...] += 0.5 * _sum_all(jnp.mean(err * err, axis=1, keepdims=True)) * jnp.ones((1, 128), f32)

    row = pl.BlockSpec((tm, D), lambda i: (i, 0))
    return pl.pallas_call(
        body, name="loss_head", grid=(S // tm,),
        in_specs=[row, pl.BlockSpec((1, D), lambda i: (0, 0)), row, row],
        out_specs=[row, pl.BlockSpec((1, 128), lambda i: (0, 0))],
        out_shape=[jax.ShapeDtypeStruct((S, D), f32), jax.ShapeDtypeStruct((1, 128), f32)],
        compiler_params=_params(("arbitrary",)),
    )(ff, g, h1, target)


def _local_step(x, mem, target, wts, small):
    S = x.shape[0]
    M = mem.shape[0]
    pad = lambda v: jnp.pad(v, ((0, 0), (0, 128 - SSD_HEADS)))
    dtb_p, alog_p = pad(small["dt_bias"]), pad(small["a_log"])
    dsk_c = jnp.repeat(small["d_skip"], SB_HD, axis=1)

    u = _rms_fwd(x, small["norm_mix_pre"], name="norm_pre", out_dtype=bf16)
    proj = _mm(u, wts["w_main"], "nn", tm=1024, tn=1024, name="in_proj")
    pdt = _mm(u, wts["w_dt"], "nn", tm=1024, tn=128, name="in_proj_dt")
    y_sb, tot_lk = _sb_fwd(proj, S)
    xc, xbc = _conv_fwd(proj, small["conv_w"], small["conv_b"], S)
    y_ssd, yn, hprev = _ssd_fwd(xbc, proj, pdt, dtb_p, alog_p, dsk_c, small["ssd_norm"], S)
    mn = _rms_fwd(mem, small["norm_mem"], name="norm_mem", out_dtype=bf16, tm=min(512, M))
    kv = _mm(mn, wts["w_mem_kv"], "nn", tm=M, tn=1024, name="mem_kv")
    y_mem = _mem_fwd(proj, kv, S)
    t0 = _mm(y_sb, wts["w_sb_out"], "nn", tm=1024, tn=1024, name="sb_out")
    t1 = _mm(yn, wts["w_ssd_out"], "nn", tm=1024, tn=1024, name="ssd_out")
    t2 = _mm(y_mem, wts["w_mem_out"], "nn", tm=1024, tn=1024, name="mem_out")
    merged = _merge_fwd(proj, t0, t1, t2, S)
    mix = _mm(merged, wts["w_o"], "nn", tm=1024, tn=1024, name="w_o")
    h1 = _rms_fwd(mix, small["norm_mix_post"], name="norm_mix_post", out_dtype=f32, residual=x)
    u2 = _rms_fwd(h1, small["norm_mlp_pre"], name="norm_mlp_pre", out_dtype=bf16)
    a_up, hrelu = _mm(u2, wts["w_up"], "nn", tm=1024, tn=1024, name="mlp_up", out_dtypes=(f32, bf16),
                      epi=lambda acc: (acc, jnp.square(jnp.maximum(acc, 0.0))))
    ff = _mm(hrelu, wts["w_down"], "nn", tm=1024, tn=1024, name="mlp_down")
    dh2, loss = _loss_head(ff, small["norm_mlp_post"], h1, target, S)

    g = {}
    dff, g["norm_mlp_post"] = _rms_bwd(ff, dh2, small["norm_mlp_post"], name="norm_mlp_post_bwd", out_dtype=bf16)
    da = _mm(dff, wts["w_down"], "nt", tm=1024, tn=1024, name="mlp_down_dx", out_dtypes=(bf16,),
             epi=lambda acc, a: (acc * (2.0 * jnp.maximum(a, 0.0)),), extras=(a_up,))
    g["w_down"] = _mm(hrelu, dff, "tn", tm=1024, tn=1024, name="mlp_down_dw")
    du2 = _mm(da, wts["w_up"], "nt", tm=1024, tn=1024, name="mlp_up_dx")
    g["w_up"] = _mm(u2, da, "tn", tm=1024, tn=1024, name="mlp_up_dw")
    dh1, g["norm_mlp_pre"] = _rms_bwd(h1, du2, small["norm_mlp_pre"], name="norm_mlp_pre_bwd", out_dtype=f32, add=dh2)
    dmix, g["norm_mix_post"] = _rms_bwd(mix, dh1, small["norm_mix_post"], name="norm_mix_post_bwd", out_dtype=bf16)
    dmerged = _mm(dmix, wts["w_o"], "nt", tm=1024, tn=1024, name="w_o_dx")
    g["w_o"] = _mm(merged, dmix, "tn", tm=1024, tn=1024, name="w_o_dw")
    dt0, dt1, dt2, dgl = _merge_bwd(proj, t0, t1, t2, dmerged, S)
    dy_sb = _mm(dt0, wts["w_sb_out"], "nt", tm=1024, tn=1024, name="sb_out_dx")
    g["w_sb_out"] = _mm(y_sb, dt0, "tn", tm=1024, tn=1024, name="sb_out_dw")
    dy_ssd = _mm(dt1, wts["w_ssd_out"], "nt", tm=1024, tn=1024, name="ssd_out_dx")
    g["w_ssd_out"] = _mm(yn, dt1, "tn", tm=1024, tn=1024, name="ssd_out_dw")
    dy_mem = _mm(dt2, wts["w_mem_out"], "nt", tm=1024, tn=1024, name="mem_out_dx")
    g["w_mem_out"] = _mm(y_mem, dt2, "tn", tm=1024, tn=1024, name="mem_out_dw")
    dmemq, dkv = _mem_bwd(proj, kv, dy_mem, S)
    g["w_mem_kv"] = _mm(mn, dkv, "tn", tm=1024, tn=1024, name="mem_kv_dw")
    dmn = _mm(dkv, wts["w_mem_kv"], "nt", tm=M, tn=1024, name="mem_kv_dx")
    _, g["norm_mem"] = _rms_bwd(mem, dmn, small["norm_mem"], name="norm_mem_bwd", out_dtype=bf16, tm=min(512, M))
    dz, dxbc, ddt, g["ssd_norm"], dsk, dalog, ddtb = _ssd_bwd(
        dy_ssd, y_ssd, xbc, proj, pdt, hprev, dtb_p, alog_p, dsk_c, small["ssd_norm"], S)
    g["d_skip"], g["a_log"], g["dt_bias"] = dsk[:, :SSD_HEADS], dalog[:, :SSD_HEADS], ddtb[:, :SSD_HEADS]
    dxbc_raw, dcw, g["conv_b"] = _conv_bwd(proj, xc, dxbc, small["conv_w"], S)
    g["conv_w"] = dcw[:CONV_K]
    dq, dk, dv = _sb_bwd(proj, tot_lk, dy_sb, S)
    dproj = jnp.concatenate([dq, dk, dv, dxbc_raw, dgl, dmemq, dz], axis=1)
    du_dt = _mm(ddt, wts["w_dt"], "nt", tm=1024, tn=1024, name="in_proj_dt_dx")
    du = _mm(dproj, wts["w_main"], "nt", tm=512, tn=256, name="in_proj_dx",
             epi=lambda acc, e: (acc + e,), extras=(du_dt,))
    g["w_main"] = _mm(u, dproj, "tn", tm=1024, tn=1024, name="in_proj_dw")
    g["w_dt"] = _mm(u, ddt, "tn", tm=1024, tn=128, name="in_proj_dt_dw")
    grad_x, g["norm_mix_pre"] = _rms_bwd(x, du, small["norm_mix_pre"], name="norm_pre_bwd", out_dtype=f32, add=dh1)
    return loss, grad_x, g


def _to_internal(w_in):
    sec = lambda r: w_in[:, r[0]:r[1]]
    w_main = jnp.concatenate([sec(R_QKV), sec(R_XBC), sec(R_GATE), sec(R_MEMQ), sec(R_Z)], axis=1)
    w_dt = jnp.pad(sec(R_DT), ((0, 0), (0, 128 - SSD_HEADS)))
    return w_main, w_dt


def _from_internal(g_main, g_dt):
    sec = lambda p, n: g_main[:, p:p + n]
    return jnp.concatenate([sec(P_QKV, 3072), sec(P_Z, 2048), sec(P_XBC, 3072), g_dt[:, :SSD_HEADS],
                            sec(P_MEMQ, 1024), sec(P_GATE, 3072)], axis=1)


MESH = pl.DeviceIdType.MESH
ANY = pl.BlockSpec(memory_space=pl.ANY)


def _place():
    x, y, c = lax.axis_index("x"), lax.axis_index("y"), lax.axis_index("c")
    return (x, y, c), [(1 - x, y, c), (x, 1 - y, c), (1 - x, 1 - y, c)]


def _gather_shards(shards):
    n = len(shards)

    def body(*refs):
        ins, outs = refs[:n], refs[n:2 * n]
        send, recv, loc = refs[2 * n:]
        (x, y, c), peers = _place()
        me = 2 * x + y
        own = [pltpu.make_async_copy(ins[a], outs[a].at[me], loc.at[a]) for a in range(n)]
        for cp in own:
            cp.start()

        def copy(a, k, slot):
            return pltpu.make_async_remote_copy(
                src_ref=ins[a], dst_ref=outs[a].at[slot], send_sem=send.at[a * 3 + k], recv_sem=recv.at[a * 3 + k],
                device_id=peers[k], device_id_type=MESH)

        sent = [copy(a, k, me) for a in range(n) for k in range(3)]
        for cp in sent:
            cp.start()
        for a in range(n):
            for k, p in enumerate(peers):
                copy(a, k, 2 * p[0] + p[1]).wait_recv()
        for cp in sent:
            cp.wait_send()
        for cp in own:
            cp.wait()

    return pl.pallas_call(
        body, name="gather_weights",
        in_specs=[ANY] * n, out_specs=[ANY] * n,
        out_shape=[jax.ShapeDtypeStruct((N_SHARD,) + s.shape, s.dtype) for s in shards],
        scratch_shapes=[pltpu.SemaphoreType.DMA((3 * n,)), pltpu.SemaphoreType.DMA((3 * n,)),
                        pltpu.SemaphoreType.DMA((n,))],
    )(*shards)


def _scatter_grads(slabs, packet):
    n = len(slabs)

    def body(*refs):
        ins, pk = refs[:n], refs[n]
        outs, pk_out = refs[n + 1:2 * n + 1], refs[2 * n + 1]
        send, recv, loc = refs[2 * n + 2:]
        (x, y, c), peers = _place()
        lin = 4 * x + 2 * y + c
        own = pltpu.make_async_copy(pk, pk_out.at[lin], loc.at[0])
        own.start()

        def copy(a, k):
            p = peers[k]
            return pltpu.make_async_remote_copy(
                src_ref=ins[a].at[2 * p[0] + p[1]], dst_ref=outs[a].at[k],
                send_sem=send.at[a * 3 + k], recv_sem=recv.at[a * 3 + k], device_id=p, device_id_type=MESH)

        def pk_copy(m, slot):
            dev = (x ^ ((m >> 2) & 1), y ^ ((m >> 1) & 1), c ^ (m & 1))
            return pltpu.make_async_remote_copy(
                src_ref=pk, dst_ref=pk_out.at[slot], send_sem=send.at[3 * n + m - 1], recv_sem=recv.at[3 * n + m - 1],
                device_id=dev, device_id_type=MESH)

        sent = [pk_copy(m, lin) for m in range(1, N_DEV)] + [copy(a, k) for a in range(n) for k in range(3)]
        for cp in sent:
            cp.start()
        for m in range(1, N_DEV):
            pk_copy(m, lin ^ m).wait_recv()
        for a in range(n):
            for k in range(3):
                copy(a, k).wait_recv()
        for cp in sent:
            cp.wait_send()
        own.wait()

    n_sem = 3 * n + N_DEV - 1
    return pl.pallas_call(
        body, name="scatter_grads",
        in_specs=[ANY] * (n + 1), out_specs=[ANY] * (n + 1),
        out_shape=[jax.ShapeDtypeStruct((3,) + s.shape[1:], s.dtype) for s in slabs]
        + [jax.ShapeDtypeStruct((N_DEV,) + packet.shape, packet.dtype)],
        scratch_shapes=[pltpu.SemaphoreType.DMA((n_sem,)), pltpu.SemaphoreType.DMA((n_sem,)),
                        pltpu.SemaphoreType.DMA((1,))],
    )(*slabs, packet)


def _swap_sibling(parts):
    n = len(parts)

    def body(*refs):
        ins, outs = refs[:n], refs[n:2 * n]
        send, recv = refs[2 * n:]
        x, y, c = lax.axis_index("x"), lax.axis_index("y"), lax.axis_index("c")
        cps = [pltpu.make_async_remote_copy(
            src_ref=ins[a], dst_ref=outs[a], send_sem=send.at[a], recv_sem=recv.at[a],
            device_id=(x, y, 1 - c), device_id_type=MESH) for a in range(n)]
        for cp in cps:
            cp.start()
        for cp in cps:
            cp.wait_recv()
        for cp in cps:
            cp.wait_send()

    return pl.pallas_call(
        body, name="swap_sibling",
        in_specs=[ANY] * n, out_specs=[ANY] * n,
        out_shape=[jax.ShapeDtypeStruct(p.shape, p.dtype) for p in parts],
        scratch_shapes=[pltpu.SemaphoreType.DMA((n,)), pltpu.SemaphoreType.DMA((n,))],
    )(*parts)


BLOCK_ELEMS = 256 * 1024


def _row_tile(R, C):
    tr = max(8, (BLOCK_ELEMS // C) // 8 * 8)
    while R % tr:
        tr -= 8
    return min(tr, R)


def _sum_parts(own, stack, name):
    k = stack.shape[0]
    R, C = stack.shape[1:]
    tr = _row_tile(R, C)

    def body(*refs):
        o_ref = refs[-1]
        acc = refs[0][...]
        for r in refs[1:-1]:
            acc = acc + r[...]
        o_ref[...] = acc

    row = pl.BlockSpec((tr, C), lambda i: (i, 0))
    specs = ([row] if own is not None else []) + [
        pl.BlockSpec((None, tr, C), functools.partial(lambda i, j: (j, i, 0), j=j)) for j in range(k)]
    args = ([own] if own is not None else []) + [stack] * k
    return pl.pallas_call(
        body, name=name, grid=(R // tr,), in_specs=specs, out_specs=row,
        out_shape=jax.ShapeDtypeStruct((R, C), f32), compiler_params=_params(("parallel",)),
    )(*args)


def _adamw(w, m, v, g_parts, name):
    R, C = w.shape
    tr = _row_tile(R, C)
    n_g = len(g_parts)

    def body(w_ref, m_ref, v_ref, *rest):
        g = rest[0][...]
        for r in rest[1:n_g]:
            g = g + r[...]
        g_ref, d_ref, nm_ref, nv_ref = rest[n_g:]
        nm = ADAM_B1 * m_ref[...] + (1.0 - ADAM_B1) * g
        nv = ADAM_B2 * v_ref[...] + (1.0 - ADAM_B2) * jnp.square(g)
        m_hat = nm / (1.0 - ADAM_B1 ** ADAM_STEP)
        v_hat = nv / (1.0 - ADAM_B2 ** ADAM_STEP)
        g_ref[...] = g
        d_ref[...] = -ADAM_LR * (m_hat / (jnp.sqrt(v_hat) + ADAM_EPS) + ADAM_WD * w_ref[...])
        nm_ref[...] = nm
        nv_ref[...] = nv

    row = pl.BlockSpec((tr, C), lambda i: (i, 0))
    return pl.pallas_call(
        body, name=name, grid=(R // tr,), in_specs=[row] * (3 + n_g), out_specs=[row] * 4,
        out_shape=[jax.ShapeDtypeStruct((R, C), f32)] * 4, compiler_params=_params(("parallel",)),
    )(w, m, v, *g_parts)


BIG = ("w_in", "w_mem_kv", "w_sb_out", "w_ssd_out", "w_mem_out", "w_o", "w_up", "w_down")
COL_SHARDED = ("w_in", "w_mem_kv", "w_up")
SMALL = ("norm_mix_pre", "conv_w", "conv_b", "dt_bias", "a_log", "d_skip", "ssd_norm", "norm_mem",
         "norm_mix_post", "norm_mlp_pre", "norm_mlp_post")
WEIGHTS = ("norm_mix_pre", "w_in", "conv_w", "conv_b", "dt_bias", "a_log", "d_skip", "ssd_norm", "norm_mem",
           "w_mem_kv", "w_sb_out", "w_ssd_out", "w_mem_out", "w_o", "norm_mix_post", "norm_mlp_pre", "w_up",
           "w_down", "norm_mlp_post")
PK_ROWS = 184


def _pack(vecs):
    flat = jnp.concatenate([v.reshape(-1) for v in vecs])
    return jnp.pad(flat, (0, PK_ROWS * 128 - flat.shape[0])).reshape(PK_ROWS, 128)


def _unpack(pk, shapes):
    flat = pk.reshape(-1)
    out, off = [], 0
    for s in shapes:
        n = 1
        for d in s:
            n *= d
        out.append(flat[off:off + n].reshape(s))
        off += n
    return out


def _full_from_slabs(name, slabs):
    if name in COL_SHARDED:
        return slabs.transpose(1, 0, 2).reshape(slabs.shape[1], -1)
    return slabs.reshape(-1, slabs.shape[2])


def _slabs_from_full(name, g):
    if name in COL_SHARDED:
        return g.reshape(g.shape[0], N_SHARD, -1).transpose(1, 0, 2)
    return g.reshape(N_SHARD, -1, g.shape[1])


def kernel(x, mem, norm_mix_pre, w_in, conv_w, conv_b, dt_bias, a_log, d_skip, ssd_norm, norm_mem, w_mem_kv, w_sb_out, w_ssd_out, w_mem_out, w_o, norm_mix_post, norm_mlp_pre, w_up, w_down, norm_mlp_post, loss_target, m_norm_mix_pre, m_w_in, m_conv_w, m_conv_b, m_dt_bias, m_a_log, m_d_skip, m_ssd_norm, m_norm_mem, m_w_mem_kv, m_w_sb_out, m_w_ssd_out, m_w_mem_out, m_w_o, m_norm_mix_post, m_norm_mlp_pre, m_w_up, m_w_down, m_norm_mlp_post, v_norm_mix_pre, v_w_in, v_conv_w, v_conv_b, v_dt_bias, v_a_log, v_d_skip, v_ssd_norm, v_norm_mem, v_w_mem_kv, v_w_sb_out, v_w_ssd_out, v_w_mem_out, v_w_o, v_norm_mix_post, v_norm_mlp_pre, v_w_up, v_w_down, v_norm_mlp_post):
    env = dict(locals())
    w = {n: env[n] for n in WEIGHTS}
    mo = {n: env["m_" + n] for n in WEIGHTS}
    vo = {n: env["v_" + n] for n in WEIGHTS}
    shard = 2 * lax.axis_index("x") + lax.axis_index("y")

    gathered = _gather_shards([w[n][0].astype(bf16) for n in BIG] + [w["conv_w"][0]])
    full = {n: _full_from_slabs(n, s) for n, s in zip(BIG, gathered[:-1])}
    wts = {n: full[n] for n in BIG if n != "w_in"}
    wts["w_main"], wts["w_dt"] = _to_internal(full["w_in"])
    small = {n: w[n] for n in SMALL if n != "conv_w"}
    small["conv_w"] = gathered[-1].transpose(1, 0, 2).reshape(CONV_K, CONV_DIM)

    loss, grad_x, g = _local_step(x[0], mem[0], loss_target[0], wts, small)
    g["w_in"] = _from_internal(g.pop("w_main"), g.pop("w_dt"))

    packet = _pack([g[n] for n in SMALL] + [loss[:, :1]])
    slabs = [_slabs_from_full(n, g[n]) for n in BIG]
    got = _scatter_grads(slabs, packet)
    packets = got[-1]
    partial = []
    for n, s, r in zip(BIG, slabs, got[:-1]):
        own = lax.dynamic_index_in_dim(s, shard, 0, keepdims=False)
        partial.append(_sum_parts(own, r, name="sum_chips_" + n))
    other = _swap_sibling(partial)

    out_g, out_d, out_m, out_v = {}, {}, {}, {}
    for n, p, q in zip(BIG, partial, other):
        res = _adamw(w[n][0], mo[n][0], vo[n][0], [p, q], name="adamw_" + n)
        out_g[n], out_d[n], out_m[n], out_v[n] = [r[None] for r in res]
    tot = _sum_parts(None, packets, name="sum_packets")
    shapes = [g[n].shape for n in SMALL] + [(1, 1)]
    sm = dict(zip(SMALL + ("loss",), _unpack(tot, shapes)))
    sm["conv_w"] = lax.dynamic_slice_in_dim(sm["conv_w"], shard * (CONV_DIM // N_SHARD), CONV_DIM // N_SHARD, axis=1)
    own_small = lambda d: _pack([d[n].reshape(sm[n].shape) for n in SMALL])
    res = _adamw(own_small(w), own_small(mo), own_small(vo), [own_small(sm)], name="adamw_small")
    own_shapes = [sm[n].shape for n in SMALL]
    for store, r in zip((out_g, out_d, out_m, out_v), res):
        for n, val in zip(SMALL, _unpack(r, own_shapes)):
            store[n] = val.reshape(w[n].shape)

    outs = [sm["loss"].reshape(()), grad_x[None]]
    for store in (out_g, out_d, out_m, out_v):
        outs += [store[n] for n in WEIGHTS]
    return tuple(outs)
```

```python
import functools

import jax
import jax.numpy as jnp
from jax import lax
from jax.experimental import pallas as pl
from jax.experimental.pallas import tpu as pltpu

f32 = jnp.float32
bf16 = jnp.bfloat16

D = 1024
EPS = 1e-6
SB_HD = 64
SSD_INNER = 2048
SSD_HEADS = 32
SSD_GROUPS = 4
SSD_N = 128
SSD_L = 128
CONV_K = 4
CONV_DIM = 3072
MEM_HEADS = 4
MEM_HD = 256
D_FF = 4096
D_IN = 12320
N_SHARD = 4
N_DEV = 8

P_QKV, P_XBC, P_GATE, P_MEMQ, P_Z, P_DT, P_TOT = 0, 3072, 6144, 9216, 10240, 12288, 12416
R_QKV, R_Z, R_XBC, R_DT, R_MEMQ, R_GATE = (0, 3072), (3072, 5120), (5120, 8192), (8192, 8224), (8224, 9248), (9248, 12320)

ADAM_LR = 0.001
ADAM_B1 = 0.9
ADAM_B2 = 0.999
ADAM_EPS = 1e-08
ADAM_WD = 0.01
ADAM_STEP = 10

VMEM_LIMIT = 56 * 1024 * 1024

NN = (((1,), (0,)), ((), ()))
NT = (((1,), (1,)), ((), ()))
TN = (((0,), (0,)), ((), ()))


def _dot(a, b, dims=NN):
    return lax.dot_general(a, b, dims, preferred_element_type=f32)


def _params(sem=None):
    return pltpu.CompilerParams(dimension_semantics=sem, vmem_limit_bytes=VMEM_LIMIT)


def _sigmoid(x):
    return 1.0 / (1.0 + jnp.exp(-x))


def _split2(x):
    hi = x.astype(bf16)
    lo = (x - hi.astype(f32)).astype(bf16)
    return hi, lo


def _split3(x):
    hi = x.astype(bf16)
    r = x - hi.astype(f32)
    mid = r.astype(bf16)
    lo = (r - mid.astype(f32)).astype(bf16)
    return hi, mid, lo


def _mm(a, b, mode, *, tm, tn, name, out_dtypes=(f32,), epi=None, extras=()):
    M = a.shape[1] if mode == "tn" else a.shape[0]
    N = b.shape[0] if mode == "nt" else b.shape[1]
    tm, tn = min(tm, M), min(tn, N)
    if mode == "nn":
        (M, K), N = a.shape, b.shape[1]
        a_spec = pl.BlockSpec((tm, K), lambda i, j: (i, 0))
        b_spec = pl.BlockSpec((K, tn), lambda i, j: (0, j))
        dims = NN
    elif mode == "nt":
        (M, K), N = a.shape, b.shape[0]
        a_spec = pl.BlockSpec((tm, K), lambda i, j: (i, 0))
        b_spec = pl.BlockSpec((tn, K), lambda i, j: (j, 0))
        dims = NT
    else:
        (K, M), N = a.shape, b.shape[1]
        a_spec = pl.BlockSpec((K, tm), lambda i, j: (0, i))
        b_spec = pl.BlockSpec((K, tn), lambda i, j: (0, j))
        dims = TN
    assert M % tm == 0 and N % tn == 0, (name, M, N, tm, tn)
    n_ex = len(extras)
    o_spec = pl.BlockSpec((tm, tn), lambda i, j: (i, j))

    def body(a_ref, b_ref, *rest):
        acc = _dot(a_ref[...].astype(bf16), b_ref[...].astype(bf16), dims)
        res = (acc,) if epi is None else epi(acc, *[e[...] for e in rest[:n_ex]])
        for o_ref, r in zip(rest[n_ex:], res):
            o_ref[...] = r.astype(o_ref.dtype)

    out = pl.pallas_call(
        body, name=name, grid=(M // tm, N // tn),
        in_specs=[a_spec, b_spec] + [o_spec] * n_ex,
        out_specs=[o_spec] * len(out_dtypes),
        out_shape=[jax.ShapeDtypeStruct((M, N), dt) for dt in out_dtypes],
        compiler_params=_params(("parallel", "parallel")),
    )(a, b, *extras)
    return out[0] if len(out_dtypes) == 1 else out


def _rms_fwd(x, g, *, name, out_dtype, residual=None, tm=512):
    S, C = x.shape
    tm = min(tm, S)
    has_res = residual is not None

    def body(x_ref, g_ref, *rest):
        xv = x_ref[...]
        r = lax.rsqrt(jnp.mean(xv * xv, axis=1, keepdims=True) + EPS)
        y = xv * r * g_ref[...]
        if has_res:
            y = y + rest[0][...]
        rest[-1][...] = y.astype(out_dtype)

    row = pl.BlockSpec((tm, C), lambda i: (i, 0))
    vec = pl.BlockSpec((1, C), lambda i: (0, 0))
    args = (x, g) + ((residual,) if has_res else ())
    return pl.pallas_call(
        body, name=name, grid=(S // tm,),
        in_specs=[row, vec] + ([row] if has_res else []),
        out_specs=row, out_shape=jax.ShapeDtypeStruct((S, C), out_dtype),
        compiler_params=_params(("parallel",)),
    )(*args)


def _rms_bwd(x, dy, g, *, name, out_dtype, add=None, tm=512):
    S, C = x.shape
    tm = min(tm, S)
    has_add = add is not None

    def body(x_ref, dy_ref, g_ref, *rest):
        dx_ref, dg_ref = rest[-2], rest[-1]
        xv = x_ref[...]
        dyv = dy_ref[...].astype(f32)
        r = lax.rsqrt(jnp.mean(xv * xv, axis=1, keepdims=True) + EPS)
        xh = xv * r
        dxh = dyv * g_ref[...]
        dx = r * (dxh - xh * jnp.mean(dxh * xh, axis=1, keepdims=True))
        if has_add:
            dx = dx + rest[0][...]
        dx_ref[...] = dx.astype(out_dtype)

        @pl.when(pl.program_id(0) == 0)
        def _():
            dg_ref[...] = jnp.zeros_like(dg_ref)

        dg_ref[...] += jnp.sum(dyv * xh, axis=0, keepdims=True)

    row = pl.BlockSpec((tm, C), lambda i: (i, 0))
    vec = pl.BlockSpec((1, C), lambda i: (0, 0))
    args = (x, dy, g) + ((add,) if has_add else ())
    return pl.pallas_call(
        body, name=name, grid=(S // tm,),
        in_specs=[row, row, vec] + ([row] if has_add else []),
        out_specs=[row, vec],
        out_shape=[jax.ShapeDtypeStruct((S, C), out_dtype), jax.ShapeDtypeStruct((1, C), f32)],
        compiler_params=_params(("arbitrary",)),
    )(*args)


SB_T = 128
SB_GROUPS = (8, 4, 2, 1)
SB_GROUPS_BWD = (4, 2, 1)


def _sb_masks():
    lane = lax.broadcasted_iota(jnp.int32, (1, 128), 1)
    m_a = (lane < SB_HD).astype(f32)
    return m_a, 1.0 - m_a


def _sb_logits(z, mask):
    l1p = jnp.log(1.0 + jnp.exp(-jnp.abs(z)))
    lb = jnp.minimum(z, 0.0) - l1p
    lk = lb - z
    if mask is not None:
        lk = jnp.where(mask, lk, 0.0)
    return lb, lk


def _chunks(a, n):
    return [a[:, u * SB_T:(u + 1) * SB_T] for u in range(n)]


def _cat(parts, axis):
    return parts[0] if len(parts) == 1 else jnp.concatenate(parts, axis=axis)


def _chunk_matmul(parts_list, u_mat):
    out = _dot(_cat(parts_list, 0), u_mat)
    return [out[u * SB_T:(u + 1) * SB_T] for u in range(len(parts_list))]


def _chunk_cumsum(lk, n, u_mat):
    hi = lk.astype(bf16)
    lo = (lk - hi.astype(f32)).astype(bf16)
    out = _chunk_matmul(_chunks(hi, n) + _chunks(lo, n), u_mat)
    return [out[u] + out[n + u] for u in range(n)]


def _sb_fwd(proj, S):
    nq = S // SB_T
    n_pairs = D // 128
    scale = SB_HD ** -0.5

    def body(q_ref, k_ref, v_ref, o_ref, t_ref):
        i = pl.program_id(1)
        m_a, m_b = _sb_masks()
        r_i = lax.broadcasted_iota(jnp.int32, (SB_T, SB_T), 0)
        c_i = lax.broadcasted_iota(jnp.int32, (SB_T, SB_T), 1)
        u_mat = (r_i > c_i).astype(bf16)
        causal = c_i < r_i
        q = q_ref[...] * scale
        q_h = ((q * m_a).astype(bf16), (q * m_b).astype(bf16))

        def group(j_lo, n, carry, mask):
            acc, c_a, c_b = carry
            rows = pl.ds(pl.multiple_of(j_lo * SB_T, SB_T), n * SB_T)
            k = k_ref[rows, :].astype(bf16)
            v = v_ref[rows, :]
            zs = [_dot(q_b, k, NT) for q_b in q_h]
            lbk = [_sb_logits(z, mask) for z in zs]
            parts = [_chunk_cumsum(lk, n, u_mat) for _, lk in lbk]
            ws, cs = [], []
            for (lb, lk), part, c in zip(lbk, parts, (c_a, c_b)):
                lb_c, lk_c = _chunks(lb, n), _chunks(lk, n)
                w_c = [None] * n
                for u in reversed(range(n)):
                    w_c[u] = jnp.exp(lb_c[u] + c + part[u])
                    c = c + jnp.sum(lk_c[u], axis=1, keepdims=True)
                w = _cat(w_c, 1)
                if mask is not None:
                    w = jnp.where(mask, w, 0.0)
                ws.append(w.astype(bf16))
                cs.append(c)
            for w, m in zip(ws, (m_a, m_b)):
                acc = acc + _dot(w, (v * m).astype(bf16))
            return acc, cs[0], cs[1]

        zero_c = jnp.zeros((SB_T, 1), f32)
        carry = group(i, 1, (jnp.zeros((SB_T, 128), f32), zero_c, zero_c), causal)
        left = i
        for n in SB_GROUPS:
            trips = left // n
            carry = lax.fori_loop(
                0, trips, functools.partial(lambda gi, cr, n, left: group(left - (gi + 1) * n, n, cr, None), n=n, left=left),
                carry)
            left = left - trips * n
        o_ref[...] = carry[0]
        lane = lax.broadcasted_iota(jnp.int32, (1, 128), 1)
        t_ref[...] = jnp.where(lane == 0, carry[1], 0.0) + jnp.where(lane == SB_HD, carry[2], 0.0)

    qs = pl.BlockSpec((SB_T, 128), lambda h, i: (i, h))
    return pl.pallas_call(
        body, name="sb_fwd", grid=(n_pairs, nq),
        in_specs=[qs,
                  pl.BlockSpec((S, 128), lambda h, i: (0, n_pairs + h)),
                  pl.BlockSpec((S, 128), lambda h, i: (0, 2 * n_pairs + h))],
        out_specs=[qs, qs], out_shape=[jax.ShapeDtypeStruct((S, D), f32)] * 2,
        compiler_params=_params(("parallel", "arbitrary")),
    )(proj, proj, proj)


def _sb_bwd(proj, tot_lk, do, S):
    nq = S // SB_T
    n_pairs = D // 128
    scale = SB_HD ** -0.5

    def body(q_ref, k_ref, v_ref, t_ref, do_ref, dq_ref, dk_ref, dv_ref, dk_acc, dv_acc):
        i = pl.program_id(1)
        m_a, m_b = _sb_masks()
        r_i = lax.broadcasted_iota(jnp.int32, (SB_T, SB_T), 0)
        c_i = lax.broadcasted_iota(jnp.int32, (SB_T, SB_T), 1)
        u_inc = (r_i <= c_i).astype(bf16)
        u_exc = (r_i < c_i).astype(bf16)
        causal = c_i < r_i

        @pl.when(i == 0)
        def _():
            dk_acc[...] = jnp.zeros_like(dk_acc)
            dv_acc[...] = jnp.zeros_like(dv_acc)

        q = q_ref[...] * scale
        dov = do_ref[...]
        tv = t_ref[...]
        heads = []
        for m in (m_a, m_b):
            heads.append(((q * m).astype(bf16), (dov * m).astype(bf16), jnp.sum(tv * m, axis=1, keepdims=True), m))

        def group(j_lo, n, carry, mask):
            dq_acc, cp_a, cp_b, ce_a, ce_b = carry
            rows = pl.ds(pl.multiple_of(j_lo * SB_T, SB_T), n * SB_T)
            k_f = k_ref[rows, :]
            k = k_f.astype(bf16)
            v = v_ref[rows, :].astype(bf16)
            zs = [_dot(h[0], k, NT) for h in heads]
            dws = [_dot(h[1], v, NT) for h in heads]
            lbk = [_sb_logits(z, mask) for z in zs]
            parts = [_chunk_cumsum(lk, n, u_inc) for _, lk in lbk]
            ws, es, cps = [], [], []
            for (lb, lk), part, dw, h, cp in zip(lbk, parts, dws, heads, (cp_a, cp_b)):
                lb_c, lk_c = _chunks(lb, n), _chunks(lk, n)
                w_c = []
                for u in range(n):
                    w_c.append(jnp.exp(lb_c[u] + (h[2] - cp) - part[u]))
                    cp = cp + jnp.sum(lk_c[u], axis=1, keepdims=True)
                w = _cat(w_c, 1)
                if mask is not None:
                    w = jnp.where(mask, w, 0.0)
                ws.append(w)
                es.append(dw * w)
                cps.append(cp)
            e_parts = [_chunk_matmul(_chunks(e.astype(bf16), n), u_exc) for e in es]
            dzs, ces = [], []
            for (lb, _), e, e_part, ce in zip(lbk, es, e_parts, (ce_a, ce_b)):
                e_c = _chunks(e, n)
                big_c = []
                for u in range(n):
                    big_c.append(ce + e_part[u])
                    ce = ce + jnp.sum(e_c[u], axis=1, keepdims=True)
                sig = jnp.exp(lb)
                dz = e * (1.0 - sig) - _cat(big_c, 1) * sig
                if mask is not None:
                    dz = jnp.where(mask, dz, 0.0)
                dzs.append(dz.astype(bf16))
                ces.append(ce)
            dk_t = jnp.zeros((n * SB_T, 128), f32)
            dv_t = jnp.zeros((n * SB_T, 128), f32)
            for dz_b, w, h in zip(dzs, ws, heads):
                dq_acc = dq_acc + _dot(dz_b, (k_f * h[3]).astype(bf16))
                dk_t = dk_t + _dot(dz_b, h[0], TN)
                dv_t = dv_t + _dot(w.astype(bf16), h[1], TN)
            dk_acc[rows, :] += dk_t
            dv_acc[rows, :] += dv_t
            return dq_acc, cps[0], cps[1], ces[0], ces[1]

        zc = jnp.zeros((SB_T, 1), f32)
        carry = (jnp.zeros((SB_T, 128), f32), zc, zc, zc, zc)
        done = 0
        for n in SB_GROUPS_BWD:
            trips = (i - done) // n
            carry = lax.fori_loop(
                0, trips, functools.partial(lambda gi, cr, n, done: group(done + gi * n, n, cr, None), n=n, done=done),
                carry)
            done = done + trips * n
        carry = group(i, 1, carry, causal)
        dq_ref[...] = (carry[0] * scale).astype(bf16)

        @pl.when(i == nq - 1)
        def _():
            dk_ref[...] = dk_acc[...].astype(bf16)
            dv_ref[...] = dv_acc[...].astype(bf16)

    qs = pl.BlockSpec((SB_T, 128), lambda h, i: (i, h))
    full = pl.BlockSpec((S, 128), lambda h, i: (0, h))
    dq, dk, dv = pl.pallas_call(
        body, name="sb_bwd", grid=(n_pairs, nq),
        in_specs=[qs,
                  pl.BlockSpec((S, 128), lambda h, i: (0, n_pairs + h)),
                  pl.BlockSpec((S, 128), lambda h, i: (0, 2 * n_pairs + h)),
                  qs, qs],
        out_specs=[qs, full, full],
        out_shape=[jax.ShapeDtypeStruct((S, D), bf16)] * 3,
        scratch_shapes=[pltpu.VMEM((S, 128), f32), pltpu.VMEM((S, 128), f32)],
        compiler_params=_params(("parallel", "arbitrary")),
    )(proj, proj, proj, tot_lk, do)
    return dq, dk, dv


CONV_CB = 256
HALO = 8


def _conv_fwd(proj, conv_w, conv_b, S):
    tr = min(512, S)

    def body(x_ref, w_ref, b_ref, xc_ref, xbc_ref):
        w = w_ref[...]
        for t in range(S // tr):
            cur = x_ref[t * tr:(t + 1) * tr, :]
            halo = x_ref[t * tr - HALO:t * tr, :] if t else jnp.zeros((HALO, CONV_CB), f32)
            win = jnp.concatenate([halo, cur], axis=0)
            acc = b_ref[...] + w[CONV_K - 1:CONV_K, :] * cur
            for k in range(CONV_K - 1):
                acc = acc + w[k:k + 1, :] * pltpu.roll(win, CONV_K - 1 - k, 0)[HALO:, :]
            xc_ref[t * tr:(t + 1) * tr, :] = acc
            xbc_ref[t * tr:(t + 1) * tr, :] = acc * _sigmoid(acc)

    col = pl.BlockSpec((S, CONV_CB), lambda c: (0, c))
    return pl.pallas_call(
        body, name="conv_fwd", grid=(CONV_DIM // CONV_CB,),
        in_specs=[pl.BlockSpec((S, CONV_CB), lambda c: (0, P_XBC // CONV_CB + c)),
                  pl.BlockSpec((CONV_K, CONV_CB), lambda c: (0, c)),
                  pl.BlockSpec((1, CONV_CB), lambda c: (0, c))],
        out_specs=[col, col], out_shape=[jax.ShapeDtypeStruct((S, CONV_DIM), f32)] * 2,
        compiler_params=_params(("parallel",)),
    )(proj, conv_w, conv_b)


def _conv_bwd(proj, xc, dxbc, conv_w, S):
    tr = min(512, S)

    def body(x_ref, xc_ref, dy_ref, w_ref, dx_ref, dw_ref, db_ref, dxc_s):
        w = w_ref[...]
        xcv = xc_ref[...]
        sg = _sigmoid(xcv)
        dxc_s[0:S, :] = dy_ref[...] * (sg * (1.0 + xcv * (1.0 - sg)))
        dxc_s[S:S + HALO, :] = jnp.zeros((HALO, CONV_CB), f32)
        dws = [jnp.zeros((1, CONV_CB), f32) for _ in range(CONV_K)]
        db = jnp.zeros((1, CONV_CB), f32)
        for t in range(S // tr):
            cur = x_ref[t * tr:(t + 1) * tr, :]
            halo = x_ref[t * tr - HALO:t * tr, :] if t else jnp.zeros((HALO, CONV_CB), f32)
            win = jnp.concatenate([halo, cur], axis=0)
            dwin = dxc_s[t * tr:(t + 1) * tr + HALO, :]
            dcur = dwin[0:tr, :]
            db = db + jnp.sum(dcur, axis=0, keepdims=True)
            dws[CONV_K - 1] = dws[CONV_K - 1] + jnp.sum(dcur * cur, axis=0, keepdims=True)
            dx = w[CONV_K - 1:CONV_K, :] * dcur
            for k in range(CONV_K - 1):
                sh = CONV_K - 1 - k
                dws[k] = dws[k] + jnp.sum(dcur * pltpu.roll(win, sh, 0)[HALO:, :], axis=0, keepdims=True)
                dx = dx + w[k:k + 1, :] * pltpu.roll(dwin, tr + HALO - sh, 0)[0:tr, :]
            dx_ref[t * tr:(t + 1) * tr, :] = dx.astype(bf16)
        dw_ref[...] = jnp.concatenate(dws + [jnp.zeros((8 - CONV_K, CONV_CB), f32)], axis=0)
        db_ref[...] = db

    col = pl.BlockSpec((S, CONV_CB), lambda c: (0, c))
    return pl.pallas_call(
        body, name="conv_bwd", grid=(CONV_DIM // CONV_CB,),
        in_specs=[pl.BlockSpec((S, CONV_CB), lambda c: (0, P_XBC // CONV_CB + c)), col, col,
                  pl.BlockSpec((CONV_K, CONV_CB), lambda c: (0, c))],
        out_specs=[col, pl.BlockSpec((8, CONV_CB), lambda c: (0, c)), pl.BlockSpec((1, CONV_CB), lambda c: (0, c))],
        out_shape=[jax.ShapeDtypeStruct((S, CONV_DIM), bf16), jax.ShapeDtypeStruct((8, CONV_DIM), f32),
                   jax.ShapeDtypeStruct((1, CONV_DIM), f32)],
        scratch_shapes=[pltpu.VMEM((S + HALO, CONV_CB), f32)],
        compiler_params=_params(("parallel",)),
    )(proj, xc, dxbc, conv_w)


N_PAIR = SSD_HEADS // 2
NEG = -1e30


def _softplus(x):
    return jnp.maximum(x, 0.0) + jnp.log(1.0 + jnp.exp(-jnp.abs(x)))


def _ssd_common(dtr, dtb, alog):
    L = SSD_L
    r_i = lax.broadcasted_iota(jnp.int32, (L, L), 0)
    c_i = lax.broadcasted_iota(jnp.int32, (L, L), 1)
    dt = _softplus(dtr + dtb)
    a = -jnp.exp(alog)
    da = dt * a
    lower = (r_i >= c_i).astype(bf16)
    upper = (r_i <= c_i).astype(bf16)
    parts = _split3(da)
    a_cs = sum(_dot(lower, p) for p in parts)
    a_cs_t = sum(_dot(p, upper, TN) for p in parts)
    return dt, a, a_cs, a_cs_t, r_i >= c_i


def _pair_vec(lane, v, h):
    return jnp.where(lane < SB_HD, v[:, h:h + 1], v[:, h + 1:h + 2])


def _decay_mat(a_cs, a_cs_t, h, tril):
    return jnp.exp(jnp.where(tril, a_cs[:, h:h + 1] - a_cs_t[h:h + 1, :], NEG))


def _ssd_fwd(xbc, proj, pdt, dt_bias_p, a_log_p, d_skip_c, ssd_norm, S):
    L = SSD_L
    nc = S // L

    def body(xbc_ref, dt_ref, z_ref, dtb_ref, alog_ref, dsk_ref, gn_ref, y_ref, yn_ref, hp_ref, state):
        c = pl.program_id(0)

        @pl.when(c == 0)
        def _():
            state[...] = jnp.zeros_like(state)

        hp_ref[0] = state[...]
        lane = lax.broadcasted_iota(jnp.int32, (1, 128), 1)
        row128 = lax.broadcasted_iota(jnp.int32, (128, 1), 0)
        m_a, m_b = _sb_masks()
        dt, a, a_cs, a_cs_t, tril = _ssd_common(dt_ref[...], dtb_ref[...], alog_ref[...])
        a_last = a_cs[L - 1:L, :]
        for g in range(SSD_GROUPS):
            b_g = xbc_ref[:, SSD_INNER + g * SSD_N:SSD_INNER + (g + 1) * SSD_N].astype(bf16)
            c_g = xbc_ref[:, SSD_INNER + (SSD_GROUPS + g) * SSD_N:SSD_INNER + (SSD_GROUPS + g + 1) * SSD_N].astype(bf16)
            cb = _dot(c_g, b_g, NT)
            for pr in range(4):
                h = 8 * g + 2 * pr
                pi = h // 2
                cols = slice(pi * 128, (pi + 1) * 128)
                xs = xbc_ref[:, cols]
                x = xs * _pair_vec(lane, dt, h)
                acs = _pair_vec(lane, a_cs, h)
                al = _pair_vec(lane, a_last, h)
                w_a = (cb * _decay_mat(a_cs, a_cs_t, h, tril)).astype(bf16)
                w_b = (cb * _decay_mat(a_cs, a_cs_t, h + 1, tril)).astype(bf16)
                yd = _dot(w_a, (x * m_a).astype(bf16)) + _dot(w_b, (x * m_b).astype(bf16))
                hp = state[pi]
                yo = _dot(c_g, hp.astype(bf16), NT) * jnp.exp(acs)
                y_ref[:, cols] = yd + yo + dsk_ref[:, cols] * xs
                dec = jnp.exp(jnp.where(row128 < SB_HD, a_last[:, h:h + 1], a_last[:, h + 1:h + 2]))
                state[pi] = hp * dec + _dot((x * jnp.exp(al - acs)).astype(bf16), b_g, TN)
        zz = z_ref[...]
        y2 = y_ref[...] * (zz * _sigmoid(zz))
        gw = SSD_INNER // SSD_GROUPS
        for g in range(SSD_GROUPS):
            yg = y2[:, g * gw:(g + 1) * gw]
            rg = lax.rsqrt(jnp.mean(yg * yg, axis=1, keepdims=True) + EPS)
            yn_ref[:, g * gw:(g + 1) * gw] = (yg * rg * gn_ref[:, g * gw:(g + 1) * gw]).astype(bf16)

    vec128 = pl.BlockSpec((1, 128), lambda c: (0, 0))
    vecin = pl.BlockSpec((1, SSD_INNER), lambda c: (0, 0))
    rows = pl.BlockSpec((L, SSD_INNER), lambda c: (c, 0))
    return pl.pallas_call(
        body, name="ssd_fwd", grid=(nc,),
        in_specs=[pl.BlockSpec((L, CONV_DIM), lambda c: (c, 0)),
                  pl.BlockSpec((L, 128), lambda c: (c, 0)),
                  pl.BlockSpec((L, SSD_INNER), lambda c: (c, P_Z // SSD_INNER)),
                  vec128, vec128, vecin, vecin],
        out_specs=[rows, rows, pl.BlockSpec((1, N_PAIR, 128, SSD_N), lambda c: (c, 0, 0, 0))],
        out_shape=[jax.ShapeDtypeStruct((S, SSD_INNER), f32), jax.ShapeDtypeStruct((S, SSD_INNER), bf16),
                   jax.ShapeDtypeStruct((nc, N_PAIR, 128, SSD_N), f32)],
        scratch_shapes=[pltpu.VMEM((N_PAIR, 128, SSD_N), f32)],
        compiler_params=_params(("arbitrary",)),
    )(xbc, pdt, proj, dt_bias_p, a_log_p, d_skip_c, ssd_norm)


def _sum_all(v):
    return jnp.sum(jnp.sum(v, axis=1, keepdims=True), axis=0, keepdims=True)


def _ssd_bwd(dyn, y, xbc, proj, pdt, hprev, dt_bias_p, a_log_p, d_skip_c, ssd_norm, S):
    L = SSD_L
    nc = S // L

    def body(dyn_ref, y_ref, xbc_ref, dt_ref, z_ref, hp_ref, dtb_ref, alog_ref, dsk_ref, gn_ref,
             dz_ref, dxbc_ref, ddt_ref, dgn_ref, dsk_out, dalog_ref, ddtb_ref, dstate, dy_s):
        c = pl.program_id(0)

        @pl.when(c == 0)
        def _():
            dstate[...] = jnp.zeros_like(dstate)
            dgn_ref[...] = jnp.zeros_like(dgn_ref)
            dsk_out[...] = jnp.zeros_like(dsk_out)
            dalog_ref[...] = jnp.zeros_like(dalog_ref)
            ddtb_ref[...] = jnp.zeros_like(ddtb_ref)

        lane = lax.broadcasted_iota(jnp.int32, (1, 128), 1)
        row128 = lax.broadcasted_iota(jnp.int32, (128, 1), 0)
        rowl = lax.broadcasted_iota(jnp.int32, (L, 1), 0)
        m_a, m_b = _sb_masks()
        dtr = dt_ref[...]
        dt, a, a_cs, a_cs_t, tril = _ssd_common(dtr, dtb_ref[...], alog_ref[...])
        a_last = a_cs[L - 1:L, :]

        zz = z_ref[...]
        sg = _sigmoid(zz)
        silu = zz * sg
        yv = y_ref[...]
        y2 = yv * silu
        gw = SSD_INNER // SSD_GROUPS
        for g in range(SSD_GROUPS):
            sl = slice(g * gw, (g + 1) * gw)
            yg = y2[:, sl]
            rg = lax.rsqrt(jnp.mean(yg * yg, axis=1, keepdims=True) + EPS)
            yh = yg * rg
            dyn_g = dyn_ref[:, sl]
            dgn_ref[:, sl] += jnp.sum(dyn_g * yh, axis=0, keepdims=True)
            dyh = dyn_g * gn_ref[:, sl]
            dy2 = rg * (dyh - yh * jnp.mean(dyh * yh, axis=1, keepdims=True))
            dy_s[:, sl] = dy2 * silu[:, sl]
            dz_ref[:, sl] = (dy2 * yv[:, sl] * (sg[:, sl] * (1.0 + zz[:, sl] * (1.0 - sg[:, sl])))).astype(bf16)

        d_acs = jnp.zeros((L, 128), f32)
        ddt_x = jnp.zeros((L, 128), f32)
        dsk_acc = jnp.zeros((1, 128), f32)
        for g in range(SSD_GROUPS):
            bsl = slice(SSD_INNER + g * SSD_N, SSD_INNER + (g + 1) * SSD_N)
            csl = slice(SSD_INNER + (SSD_GROUPS + g) * SSD_N, SSD_INNER + (SSD_GROUPS + g + 1) * SSD_N)
            b_g = xbc_ref[:, bsl].astype(bf16)
            c_g = xbc_ref[:, csl].astype(bf16)
            cb = _dot(c_g, b_g, NT)
            dcb = jnp.zeros((L, L), f32)
            dc_g = jnp.zeros((L, SSD_N), f32)
            db_g = jnp.zeros((L, SSD_N), f32)
            for pr in range(4):
                h = 8 * g + 2 * pr
                pi = h // 2
                cols = slice(pi * 128, (pi + 1) * 128)
                xs = xbc_ref[:, cols]
                dt_p = _pair_vec(lane, dt, h)
                x = xs * dt_p
                acs = _pair_vec(lane, a_cs, h)
                al = _pair_vec(lane, a_last, h)
                e_a = jnp.exp(acs)
                dte = jnp.exp(al - acs)
                m_mat_a = _decay_mat(a_cs, a_cs_t, h, tril)
                m_mat_b = _decay_mat(a_cs, a_cs_t, h + 1, tril)
                dyp = dy_s[:, cols]
                dsk = dsk_ref[:, cols]
                d_hn = dstate[pi]
                hp = hp_ref[0, pi]
                dy_a = (dyp * m_a).astype(bf16)
                dy_b = (dyp * m_b).astype(bf16)
                x_b = x.astype(bf16)
                gm_a = _dot(dy_a, x_b, NT) * m_mat_a
                gm_b = _dot(dy_b, x_b, NT) * m_mat_b
                dcb = dcb + gm_a + gm_b
                dx_d = _dot((cb * m_mat_a).astype(bf16), dy_a, TN) + _dot((cb * m_mat_b).astype(bf16), dy_b, TN)
                dx_s = _dot(b_g, d_hn.astype(bf16), NT) * dte
                dx = dx_d + dx_s
                dxbc_ref[:, cols] = dx * dt_p + dsk * dyp
                xdxs = x * dx_s
                u = dyp * (_dot(c_g, hp.astype(bf16), NT) * e_a) - xdxs
                hh = d_hn * hp
                dec = jnp.exp(jnp.where(row128 < SB_HD, a_last[:, h:h + 1], a_last[:, h + 1:h + 2]))
                for hd, m, gm in ((h, m_a, gm_a), (h + 1, m_b, gm_b)):
                    half = slice(0, SB_HD) if hd == h else slice(SB_HD, 128)
                    last = _sum_all(xdxs * m) + jnp.exp(a_last[:, hd:hd + 1]) * _sum_all(hh[half, :])
                    qm = gm * cb
                    col = jnp.sum(qm - qm.T, axis=1, keepdims=True) + jnp.sum(u * m, axis=1, keepdims=True)
                    col = col + jnp.where(rowl == L - 1, last, 0.0)
                    d_acs = jnp.where(lane == hd, col, d_acs)
                    ddt_x = jnp.where(lane == hd, jnp.sum(dx * xs * m, axis=1, keepdims=True), ddt_x)
                    dsk_acc = jnp.where(lane == hd, _sum_all(dyp * xs * m), dsk_acc)
                dye = (dyp * e_a).astype(bf16)
                dc_g = dc_g + _dot(dye, hp.astype(bf16))
                db_g = db_g + _dot((x * dte).astype(bf16), d_hn.astype(bf16))
                dstate[pi] = dec * d_hn + _dot(dye, c_g, TN)
            dcb_b = dcb.astype(bf16)
            dxbc_ref[:, csl] = dc_g + _dot(dcb_b, b_g)
            dxbc_ref[:, bsl] = db_g + _dot(dcb_b, c_g, TN)

        r_i = lax.broadcasted_iota(jnp.int32, (L, L), 0)
        c_i = lax.broadcasted_iota(jnp.int32, (L, L), 1)
        rev = (r_i <= c_i).astype(bf16)
        dda = sum(_dot(rev, p) for p in _split3(d_acs))
        ddt = ddt_x + dda * a
        dalog_ref[...] += jnp.sum(dda * dt, axis=0, keepdims=True) * a
        ddtr = jnp.where(lane < SSD_HEADS, ddt * _sigmoid(dtr + dtb_ref[...]), 0.0)
        ddt_ref[...] = ddtr.astype(bf16)
        ddtb_ref[...] += jnp.sum(ddtr, axis=0, keepdims=True)
        dsk_out[...] += dsk_acc

    rv = lambda c: nc - 1 - c
    vec128 = pl.BlockSpec((1, 128), lambda c: (0, 0))
    vecin = pl.BlockSpec((1, SSD_INNER), lambda c: (0, 0))
    rows = pl.BlockSpec((L, SSD_INNER), lambda c: (rv(c), 0))
    return pl.pallas_call(
        body, name="ssd_bwd", grid=(nc,),
        in_specs=[rows, rows,
                  pl.BlockSpec((L, CONV_DIM), lambda c: (rv(c), 0)),
                  pl.BlockSpec((L, 128), lambda c: (rv(c), 0)),
                  pl.BlockSpec((L, SSD_INNER), lambda c: (rv(c), P_Z // SSD_INNER)),
                  pl.BlockSpec((1, N_PAIR, 128, SSD_N), lambda c: (rv(c), 0, 0, 0)),
                  vec128, vec128, vecin, vecin],
        out_specs=[rows, pl.BlockSpec((L, CONV_DIM), lambda c: (rv(c), 0)),
                   pl.BlockSpec((L, 128), lambda c: (rv(c), 0)), vecin, vec128, vec128, vec128],
        out_shape=[jax.ShapeDtypeStruct((S, SSD_INNER), bf16), jax.ShapeDtypeStruct((S, CONV_DIM), f32),
                   jax.ShapeDtypeStruct((S, 128), bf16), jax.ShapeDtypeStruct((1, SSD_INNER), f32),
                   jax.ShapeDtypeStruct((1, 128), f32), jax.ShapeDtypeStruct((1, 128), f32),
                   jax.ShapeDtypeStruct((1, 128), f32)],
        scratch_shapes=[pltpu.VMEM((N_PAIR, 128, SSD_N), f32), pltpu.VMEM((L, SSD_INNER), f32)],
        compiler_params=_params(("arbitrary",)),
    )(dyn, y, xbc, pdt, proj, hprev, dt_bias_p, a_log_p, d_skip_c, ssd_norm)


MEM_W = MEM_HEADS * MEM_HD


def _mem_probs(q, k):
    s = _dot(q, k, NT) * (MEM_HD ** -0.5)
    s = s - jnp.max(s, axis=1, keepdims=True)
    p = jnp.exp(s)
    return p / jnp.sum(p, axis=1, keepdims=True)


def _mem_fwd(proj, kv, S, tm=512):
    tm = min(tm, S)
    M = kv.shape[0]

    def body(q_ref, kv_ref, o_ref):
        for h in range(MEM_HEADS):
            sl = slice(h * MEM_HD, (h + 1) * MEM_HD)
            vsl = slice(MEM_W + h * MEM_HD, MEM_W + (h + 1) * MEM_HD)
            p = _mem_probs(q_ref[:, sl].astype(bf16), kv_ref[:, sl].astype(bf16))
            o_ref[:, sl] = _dot(p.astype(bf16), kv_ref[:, vsl].astype(bf16)).astype(bf16)

    return pl.pallas_call(
        body, name="mem_fwd", grid=(S // tm,),
        in_specs=[pl.BlockSpec((tm, MEM_W), lambda i: (i, P_MEMQ // MEM_W)),
                  pl.BlockSpec((M, 2 * MEM_W), lambda i: (0, 0))],
        out_specs=pl.BlockSpec((tm, MEM_W), lambda i: (i, 0)),
        out_shape=jax.ShapeDtypeStruct((S, MEM_W), bf16),
        compiler_params=_params(("parallel",)),
    )(proj, kv)


def _mem_bwd(proj, kv, dy, S, tm=512):
    tm = min(tm, S)
    M = kv.shape[0]
    scale = MEM_HD ** -0.5

    def body(q_ref, kv_ref, dy_ref, dq_ref, dkv_ref):
        @pl.when(pl.program_id(0) == 0)
        def _():
            dkv_ref[...] = jnp.zeros_like(dkv_ref)

        for h in range(MEM_HEADS):
            sl = slice(h * MEM_HD, (h + 1) * MEM_HD)
            vsl = slice(MEM_W + h * MEM_HD, MEM_W + (h + 1) * MEM_HD)
            q = q_ref[:, sl].astype(bf16)
            k = kv_ref[:, sl].astype(bf16)
            v = kv_ref[:, vsl].astype(bf16)
            dyh = dy_ref[:, sl].astype(bf16)
            p = _mem_probs(q, k)
            dp = _dot(dyh, v, NT)
            ds = (p * (dp - jnp.sum(dp * p, axis=1, keepdims=True)) * scale).astype(bf16)
            dq_ref[:, sl] = _dot(ds, k).astype(bf16)
            dkv_ref[:, sl] += _dot(ds, q, TN)
            dkv_ref[:, vsl] += _dot(p.astype(bf16), dyh, TN)

    return pl.pallas_call(
        body, name="mem_bwd", grid=(S // tm,),
        in_specs=[pl.BlockSpec((tm, MEM_W), lambda i: (i, P_MEMQ // MEM_W)),
                  pl.BlockSpec((M, 2 * MEM_W), lambda i: (0, 0)),
                  pl.BlockSpec((tm, MEM_W), lambda i: (i, 0))],
        out_specs=[pl.BlockSpec((tm, MEM_W), lambda i: (i, 0)), pl.BlockSpec((M, 2 * MEM_W), lambda i: (0, 0))],
        out_shape=[jax.ShapeDtypeStruct((S, MEM_W), bf16), jax.ShapeDtypeStruct((M, 2 * MEM_W), f32)],
        compiler_params=_params(("arbitrary",)),
    )(proj, kv, dy)


def _merge_fwd(proj, t0, t1, t2, S, tm=512):
    tm = min(tm, S)

    def body(g_ref, t0_ref, t1_ref, t2_ref, o_ref):
        acc = jnp.zeros((tm, D), f32)
        for b, t_ref in enumerate((t0_ref, t1_ref, t2_ref)):
            acc = acc + _sigmoid(g_ref[:, b * D:(b + 1) * D]) * t_ref[...]
        o_ref[...] = acc.astype(bf16)

    row = pl.BlockSpec((tm, D), lambda i: (i, 0))
    return pl.pallas_call(
        body, name="merge_fwd", grid=(S // tm,),
        in_specs=[pl.BlockSpec((tm, 3 * D), lambda i: (i, P_GATE // (3 * D))), row, row, row],
        out_specs=row, out_shape=jax.ShapeDtypeStruct((S, D), bf16),
        compiler_params=_params(("parallel",)),
    )(proj, t0, t1, t2)


def _merge_bwd(proj, t0, t1, t2, dm, S, tm=512):
    tm = min(tm, S)

    def body(g_ref, t0_ref, t1_ref, t2_ref, dm_ref, d0_ref, d1_ref, d2_ref, dg_ref):
        dmv = dm_ref[...]
        for b, (t_ref, d_ref) in enumerate(((t0_ref, d0_ref), (t1_ref, d1_ref), (t2_ref, d2_ref))):
            sg = _sigmoid(g_ref[:, b * D:(b + 1) * D])
            d_ref[...] = (dmv * sg).astype(bf16)
            dg_ref[:, b * D:(b + 1) * D] = (dmv * t_ref[...] * sg * (1.0 - sg)).astype(bf16)

    row = pl.BlockSpec((tm, D), lambda i: (i, 0))
    return pl.pallas_call(
        body, name="merge_bwd", grid=(S // tm,),
        in_specs=[pl.BlockSpec((tm, 3 * D), lambda i: (i, P_GATE // (3 * D))), row, row, row, row],
        out_specs=[row, row, row, pl.BlockSpec((tm, 3 * D), lambda i: (i, 0))],
        out_shape=[jax.ShapeDtypeStruct((S, D), bf16)] * 3 + [jax.ShapeDtypeStruct((S, 3 * D), bf16)],
        compiler_params=_params(("parallel",)),
    )(proj, t0, t1, t2, dm)


def _loss_head(ff, g, h1, target, S, tm=512):
    tm = min(tm, S)

    def body(ff_ref, g_ref, h1_ref, t_ref, dh_ref, loss_ref):
        xv = ff_ref[...]
        r = lax.rsqrt(jnp.mean(xv * xv, axis=1, keepdims=True) + EPS)
        err = h1_ref[...] + xv * r * g_ref[...] - t_ref[...]
        dh_ref[...] = err * (1.0 / D)

        @pl.when(pl.program_id(0) == 0)
        def _():
            loss_ref[...] = jnp.zeros_like(loss_ref)

        loss_ref[...] += 0.5 * _sum_all(jnp.mean(err * err, axis=1, keepdims=True)) * jnp.ones((1, 128), f32)

    row = pl.BlockSpec((tm, D), lambda i: (i, 0))
    return pl.pallas_call(
        body, name="loss_head", grid=(S // tm,),
        in_specs=[row, pl.BlockSpec((1, D), lambda i: (0, 0)), row, row],
        out_specs=[row, pl.BlockSpec((1, 128), lambda i: (0, 0))],
        out_shape=[jax.ShapeDtypeStruct((S, D), f32), jax.ShapeDtypeStruct((1, 128), f32)],
        compiler_params=_params(("arbitrary",)),
    )(ff, g, h1, target)


def _local_step(x, mem, target, wts, small):
    S = x.shape[0]
    M = mem.shape[0]
    pad = lambda v: jnp.pad(v, ((0, 0), (0, 128 - SSD_HEADS)))
    dtb_p, alog_p = pad(small["dt_bias"]), pad(small["a_log"])
    dsk_c = jnp.repeat(small["d_skip"], SB_HD, axis=1)

    u = _rms_fwd(x, small["norm_mix_pre"], name="norm_pre", out_dtype=bf16)
    proj = _mm(u, wts["w_main"], "nn", tm=1024, tn=1024, name="in_proj")
    pdt = _mm(u, wts["w_dt"], "nn", tm=1024, tn=128, name="in_proj_dt")
    y_sb, tot_lk = _sb_fwd(proj, S)
    xc, xbc = _conv_fwd(proj, small["conv_w"], small["conv_b"], S)
    y_ssd, yn, hprev = _ssd_fwd(xbc, proj, pdt, dtb_p, alog_p, dsk_c, small["ssd_norm"], S)
    mn = _rms_fwd(mem, small["norm_mem"], name="norm_mem", out_dtype=bf16, tm=min(512, M))
    kv = _mm(mn, wts["w_mem_kv"], "nn", tm=M, tn=1024, name="mem_kv")
    y_mem = _mem_fwd(proj, kv, S)
    t0 = _mm(y_sb, wts["w_sb_out"], "nn", tm=1024, tn=1024, name="sb_out")
    t1 = _mm(yn, wts["w_ssd_out"], "nn", tm=1024, tn=1024, name="ssd_out")
    t2 = _mm(y_mem, wts["w_mem_out"], "nn", tm=1024, tn=1024, name="mem_out")
    merged = _merge_fwd(proj, t0, t1, t2, S)
    mix = _mm(merged, wts["w_o"], "nn", tm=1024, tn=1024, name="w_o")
    h1 = _rms_fwd(mix, small["norm_mix_post"], name="norm_mix_post", out_dtype=f32, residual=x)
    u2 = _rms_fwd(h1, small["norm_mlp_pre"], name="norm_mlp_pre", out_dtype=bf16)
    a_up, hrelu = _mm(u2, wts["w_up"], "nn", tm=1024, tn=1024, name="mlp_up", out_dtypes=(f32, bf16),
                      epi=lambda acc: (acc, jnp.square(jnp.maximum(acc, 0.0))))
    ff = _mm(hrelu, wts["w_down"], "nn", tm=1024, tn=1024, name="mlp_down")
    dh2, loss = _loss_head(ff, small["norm_mlp_post"], h1, target, S)

    g = {}
    dff, g["norm_mlp_post"] = _rms_bwd(ff, dh2, small["norm_mlp_post"], name="norm_mlp_post_bwd", out_dtype=bf16)
    da = _mm(dff, wts["w_down"], "nt", tm=1024, tn=1024, name="mlp_down_dx", out_dtypes=(bf16,),
             epi=lambda acc, a: (acc * (2.0 * jnp.maximum(a, 0.0)),), extras=(a_up,))
    g["w_down"] = _mm(hrelu, dff, "tn", tm=1024, tn=1024, name="mlp_down_dw")
    du2 = _mm(da, wts["w_up"], "nt", tm=1024, tn=1024, name="mlp_up_dx")
    g["w_up"] = _mm(u2, da, "tn", tm=1024, tn=1024, name="mlp_up_dw")
    dh1, g["norm_mlp_pre"] = _rms_bwd(h1, du2, small["norm_mlp_pre"], name="norm_mlp_pre_bwd", out_dtype=f32, add=dh2)
    dmix, g["norm_mix_post"] = _rms_bwd(mix, dh1, small["norm_mix_post"], name="norm_mix_post_bwd", out_dtype=bf16)
    dmerged = _mm(dmix, wts["w_o"], "nt", tm=1024, tn=1024, name="w_o_dx")
    g["w_o"] = _mm(merged, dmix, "tn", tm=1024, tn=1024, name="w_o_dw")
    dt0, dt1, dt2, dgl = _merge_bwd(proj, t0, t1, t2, dmerged, S)
    dy_sb = _mm(dt0, wts["w_sb_out"], "nt", tm=1024, tn=1024, name="sb_out_dx")
    g["w_sb_out"] = _mm(y_sb, dt0, "tn", tm=1024, tn=1024, name="sb_out_dw")
    dy_ssd = _mm(dt1, wts["w_ssd_out"], "nt", tm=1024, tn=1024, name="ssd_out_dx")
    g["w_ssd_out"] = _mm(yn, dt1, "tn", tm=1024, tn=1024, name="ssd_out_dw")
    dy_mem = _mm(dt2, wts["w_mem_out"], "nt", tm=1024, tn=1024, name="mem_out_dx")
    g["w_mem_out"] = _mm(y_mem, dt2, "tn", tm=1024, tn=1024, name="mem_out_dw")
    dmemq, dkv = _mem_bwd(proj, kv, dy_mem, S)
    g["w_mem_kv"] = _mm(mn, dkv, "tn", tm=1024, tn=1024, name="mem_kv_dw")
    dmn = _mm(dkv, wts["w_mem_kv"], "nt", tm=M, tn=1024, name="mem_kv_dx")
    _, g["norm_mem"] = _rms_bwd(mem, dmn, small["norm_mem"], name="norm_mem_bwd", out_dtype=bf16, tm=min(512, M))
    dz, dxbc, ddt, g["ssd_norm"], dsk, dalog, ddtb = _ssd_bwd(
        dy_ssd, y_ssd, xbc, proj, pdt, hprev, dtb_p, alog_p, dsk_c, small["ssd_norm"], S)
    g["d_skip"], g["a_log"], g["dt_bias"] = dsk[:, :SSD_HEADS], dalog[:, :SSD_HEADS], ddtb[:, :SSD_HEADS]
    dxbc_raw, dcw, g["conv_b"] = _conv_bwd(proj, xc, dxbc, small["conv_w"], S)
    g["conv_w"] = dcw[:CONV_K]
    dq, dk, dv = _sb_bwd(proj, tot_lk, dy_sb, S)
    dproj = jnp.concatenate([dq, dk, dv, dxbc_raw, dgl, dmemq, dz], axis=1)
    du_dt = _mm(ddt, wts["w_dt"], "nt", tm=1024, tn=1024, name="in_proj_dt_dx")
    du = _mm(dproj, wts["w_main"], "nt", tm=512, tn=256, name="in_proj_dx",
             epi=lambda acc, e: (acc + e,), extras=(du_dt,))
    g["w_main"] = _mm(u, dproj, "tn", tm=1024, tn=1024, name="in_proj_dw")
    g["w_dt"] = _mm(u, ddt, "tn", tm=1024, tn=128, name="in_proj_dt_dw")
    grad_x, g["norm_mix_pre"] = _rms_bwd(x, du, small["norm_mix_pre"], name="norm_pre_bwd", out_dtype=f32, add=dh1)
    return loss, grad_x, g


def _to_internal(w_in):
    sec = lambda r: w_in[:, r[0]:r[1]]
    w_main = jnp.concatenate([sec(R_QKV), sec(R_XBC), sec(R_GATE), sec(R_MEMQ), sec(R_Z)], axis=1)
    w_dt = jnp.pad(sec(R_DT), ((0, 0), (0, 128 - SSD_HEADS)))
    return w_main, w_dt


def _from_internal(g_main, g_dt):
    sec = lambda p, n: g_main[:, p:p + n]
    return jnp.concatenate([sec(P_QKV, 3072), sec(P_Z, 2048), sec(P_XBC, 3072), g_dt[:, :SSD_HEADS],
                            sec(P_MEMQ, 1024), sec(P_GATE, 3072)], axis=1)


MESH = pl.DeviceIdType.MESH
ANY = pl.BlockSpec(memory_space=pl.ANY)


def _place():
    x, y, c = lax.axis_index("x"), lax.axis_index("y"), lax.axis_index("c")
    return (x, y, c), [(1 - x, y, c), (x, 1 - y, c), (1 - x, 1 - y, c)]


def _gather_shards(shards):
    n = len(shards)

    def body(*refs):
        ins, outs = refs[:n], refs[n:2 * n]
        send, recv, loc = refs[2 * n:]
        (x, y, c), peers = _place()
        me = 2 * x + y
        own = [pltpu.make_async_copy(ins[a], outs[a].at[me], loc.at[a]) for a in range(n)]
        for cp in own:
            cp.start()

        def copy(a, k, slot):
            return pltpu.make_async_remote_copy(
                src_ref=ins[a], dst_ref=outs[a].at[slot], send_sem=send.at[a * 3 + k], recv_sem=recv.at[a * 3 + k],
                device_id=peers[k], device_id_type=MESH)

        sent = [copy(a, k, me) for a in range(n) for k in range(3)]
        for cp in sent:
            cp.start()
        for a in range(n):
            for k, p in enumerate(peers):
                copy(a, k, 2 * p[0] + p[1]).wait_recv()
        for cp in sent:
            cp.wait_send()
        for cp in own:
            cp.wait()

    return pl.pallas_call(
        body, name="gather_weights",
        in_specs=[ANY] * n, out_specs=[ANY] * n,
        out_shape=[jax.ShapeDtypeStruct((N_SHARD,) + s.shape, s.dtype) for s in shards],
        scratch_shapes=[pltpu.SemaphoreType.DMA((3 * n,)), pltpu.SemaphoreType.DMA((3 * n,)),
                        pltpu.SemaphoreType.DMA((n,))],
    )(*shards)


def _scatter_grads(slabs, packet):
    n = len(slabs)

    def body(*refs):
        ins, pk = refs[:n], refs[n]
        outs, pk_out = refs[n + 1:2 * n + 1], refs[2 * n + 1]
        send, recv, loc = refs[2 * n + 2:]
        (x, y, c), peers = _place()
        lin = 4 * x + 2 * y + c
        own = pltpu.make_async_copy(pk, pk_out.at[lin], loc.at[0])
        own.start()

        def copy(a, k):
            p = peers[k]
            return pltpu.make_async_remote_copy(
                src_ref=ins[a].at[2 * p[0] + p[1]], dst_ref=outs[a].at[k],
                send_sem=send.at[a * 3 + k], recv_sem=recv.at[a * 3 + k], device_id=p, device_id_type=MESH)

        def pk_copy(m, slot):
            dev = (x ^ ((m >> 2) & 1), y ^ ((m >> 1) & 1), c ^ (m & 1))
            return pltpu.make_async_remote_copy(
                src_ref=pk, dst_ref=pk_out.at[slot], send_sem=send.at[3 * n + m - 1], recv_sem=recv.at[3 * n + m - 1],
                device_id=dev, device_id_type=MESH)

        sent = [pk_copy(m, lin) for m in range(1, N_DEV)] + [copy(a, k) for a in range(n) for k in range(3)]
        for cp in sent:
            cp.start()
        for m in range(1, N_DEV):
            pk_copy(m, lin ^ m).wait_recv()
        for a in range(n):
            for k in range(3):
                copy(a, k).wait_recv()
        for cp in sent:
            cp.wait_send()
        own.wait()

    n_sem = 3 * n + N_DEV - 1
    return pl.pallas_call(
        body, name="scatter_grads",
        in_specs=[ANY] * (n + 1), out_specs=[ANY] * (n + 1),
        out_shape=[jax.ShapeDtypeStruct((3,) + s.shape[1:], s.dtype) for s in slabs]
        + [jax.ShapeDtypeStruct((N_DEV,) + packet.shape, packet.dtype)],
        scratch_shapes=[pltpu.SemaphoreType.DMA((n_sem,)), pltpu.SemaphoreType.DMA((n_sem,)),
                        pltpu.SemaphoreType.DMA((1,))],
    )(*slabs, packet)


def _swap_sibling(parts):
    n = len(parts)

    def body(*refs):
        ins, outs = refs[:n], refs[n:2 * n]
        send, recv = refs[2 * n:]
        x, y, c = lax.axis_index("x"), lax.axis_index("y"), lax.axis_index("c")
        cps = [pltpu.make_async_remote_copy(
            src_ref=ins[a], dst_ref=outs[a], send_sem=send.at[a], recv_sem=recv.at[a],
            device_id=(x, y, 1 - c), device_id_type=MESH) for a in range(n)]
        for cp in cps:
            cp.start()
        for cp in cps:
            cp.wait_recv()
        for cp in cps:
            cp.wait_send()

    return pl.pallas_call(
        body, name="swap_sibling",
        in_specs=[ANY] * n, out_specs=[ANY] * n,
        out_shape=[jax.ShapeDtypeStruct(p.shape, p.dtype) for p in parts],
        scratch_shapes=[pltpu.SemaphoreType.DMA((n,)), pltpu.SemaphoreType.DMA((n,))],
    )(*parts)


BLOCK_ELEMS = 256 * 1024


def _row_tile(R, C):
    tr = max(8, (BLOCK_ELEMS // C) // 8 * 8)
    while R % tr:
        tr -= 8
    return min(tr, R)


def _sum_parts(own, stack, name):
    k = stack.shape[0]
    R, C = stack.shape[1:]
    tr = _row_tile(R, C)

    def body(*refs):
        o_ref = refs[-1]
        acc = refs[0][...]
        for r in refs[1:-1]:
            acc = acc + r[...]
        o_ref[...] = acc

    row = pl.BlockSpec((tr, C), lambda i: (i, 0))
    specs = ([row] if own is not None else []) + [
        pl.BlockSpec((None, tr, C), functools.partial(lambda i, j: (j, i, 0), j=j)) for j in range(k)]
    args = ([own] if own is not None else []) + [stack] * k
    return pl.pallas_call(
        body, name=name, grid=(R // tr,), in_specs=specs, out_specs=row,
        out_shape=jax.ShapeDtypeStruct((R, C), f32), compiler_params=_params(("parallel",)),
    )(*args)


def _adamw(w, m, v, g_parts, name):
    R, C = w.shape
    tr = _row_tile(R, C)
    n_g = len(g_parts)

    def body(w_ref, m_ref, v_ref, *rest):
        g = rest[0][...]
        for r in rest[1:n_g]:
            g = g + r[...]
        g_ref, d_ref, nm_ref, nv_ref = rest[n_g:]
        nm = ADAM_B1 * m_ref[...] + (1.0 - ADAM_B1) * g
        nv = ADAM_B2 * v_ref[...] + (1.0 - ADAM_B2) * jnp.square(g)
        m_hat = nm / (1.0 - ADAM_B1 ** ADAM_STEP)
        v_hat = nv / (1.0 - ADAM_B2 ** ADAM_STEP)
        g_ref[...] = g
        d_ref[...] = -ADAM_LR * (m_hat / (jnp.sqrt(v_hat) + ADAM_EPS) + ADAM_WD * w_ref[...])
        nm_ref[...] = nm
        nv_ref[...] = nv

    row = pl.BlockSpec((tr, C), lambda i: (i, 0))
    return pl.pallas_call(
        body, name=name, grid=(R // tr,), in_specs=[row] * (3 + n_g), out_specs=[row] * 4,
        out_shape=[jax.ShapeDtypeStruct((R, C), f32)] * 4, compiler_params=_params(("parallel",)),
    )(w, m, v, *g_parts)


BIG = ("w_in", "w_mem_kv", "w_sb_out", "w_ssd_out", "w_mem_out", "w_o", "w_up", "w_down")
COL_SHARDED = ("w_in", "w_mem_kv", "w_up")
SMALL = ("norm_mix_pre", "conv_w", "conv_b", "dt_bias", "a_log", "d_skip", "ssd_norm", "norm_mem",
         "norm_mix_post", "norm_mlp_pre", "norm_mlp_post")
WEIGHTS = ("norm_mix_pre", "w_in", "conv_w", "conv_b", "dt_bias", "a_log", "d_skip", "ssd_norm", "norm_mem",
           "w_mem_kv", "w_sb_out", "w_ssd_out", "w_mem_out", "w_o", "norm_mix_post", "norm_mlp_pre", "w_up",
           "w_down", "norm_mlp_post")
PK_ROWS = 184


def _pack(vecs):
    flat = jnp.concatenate([v.reshape(-1) for v in vecs])
    return jnp.pad(flat, (0, PK_ROWS * 128 - flat.shape[0])).reshape(PK_ROWS, 128)


def _unpack(pk, shapes):
    flat = pk.reshape(-1)
    out, off = [], 0
    for s in shapes:
        n = 1
        for d in s:
            n *= d
        out.append(flat[off:off + n].reshape(s))
        off += n
    return out


def _full_from_slabs(name, slabs):
    if name in COL_SHARDED:
        return slabs.transpose(1, 0, 2).reshape(slabs.shape[1], -1)
    return slabs.reshape(-1, slabs.shape[2])


def _slabs_from_full(name, g):
    if name in COL_SHARDED:
        return g.reshape(g.shape[0], N_SHARD, -1).transpose(1, 0, 2)
    return g.reshape(N_SHARD, -1, g.shape[1])


def kernel(x, mem, norm_mix_pre, w_in, conv_w, conv_b, dt_bias, a_log, d_skip, ssd_norm, norm_mem, w_mem_kv, w_sb_out, w_ssd_out, w_mem_out, w_o, norm_mix_post, norm_mlp_pre, w_up, w_down, norm_mlp_post, loss_target, m_norm_mix_pre, m_w_in, m_conv_w, m_conv_b, m_dt_bias, m_a_log, m_d_skip, m_ssd_norm, m_norm_mem, m_w_mem_kv, m_w_sb_out, m_w_ssd_out, m_w_mem_out, m_w_o, m_norm_mix_post, m_norm_mlp_pre, m_w_up, m_w_down, m_norm_mlp_post, v_norm_mix_pre, v_w_in, v_conv_w, v_conv_b, v_dt_bias, v_a_log, v_d_skip, v_ssd_norm, v_norm_mem, v_w_mem_kv, v_w_sb_out, v_w_ssd_out, v_w_mem_out, v_w_o, v_norm_mix_post, v_norm_mlp_pre, v_w_up, v_w_down, v_norm_mlp_post):
    env = dict(locals())
    w = {n: env[n] for n in WEIGHTS}
    mo = {n: env["m_" + n] for n in WEIGHTS}
    vo = {n: env["v_" + n] for n in WEIGHTS}
    shard = 2 * lax.axis_index("x") + lax.axis_index("y")

    gathered = _gather_shards([w[n][0].astype(bf16) for n in BIG] + [w["conv_w"][0]])
    full = {n: _full_from_slabs(n, s) for n, s in zip(BIG, gathered[:-1])}
    wts = {n: full[n] for n in BIG if n != "w_in"}
    wts["w_main"], wts["w_dt"] = _to_internal(full["w_in"])
    small = {n: w[n] for n in SMALL if n != "conv_w"}
    small["conv_w"] = gathered[-1].transpose(1, 0, 2).reshape(CONV_K, CONV_DIM)

    loss, grad_x, g = _local_step(x[0], mem[0], loss_target[0], wts, small)
    g["w_in"] = _from_internal(g.pop("w_main"), g.pop("w_dt"))

    packet = _pack([g[n] for n in SMALL] + [loss[:, :1]])
    slabs = [_slabs_from_full(n, g[n]) for n in BIG]
    got = _scatter_grads(slabs, packet)
    packets = got[-1]
    partial = []
    for n, s, r in zip(BIG, slabs, got[:-1]):
        own = lax.dynamic_index_in_dim(s, shard, 0, keepdims=False)
        partial.append(_sum_parts(own, r, name="sum_chips_" + n))
    other = _swap_sibling(partial)

    out_g, out_d, out_m, out_v = {}, {}, {}, {}
    for n, p, q in zip(BIG, partial, other):
        res = _adamw(w[n][0], mo[n][0], vo[n][0], [p, q], name="adamw_" + n)
        out_g[n], out_d[n], out_m[n], out_v[n] = [r[None] for r in res]
    tot = _sum_parts(None, packets, name="sum_packets")
    shapes = [g[n].shape for n in SMALL] + [(1, 1)]
    sm = dict(zip(SMALL + ("loss",), _unpack(tot, shapes)))
    sm["conv_w"] = lax.dynamic_slice_in_dim(sm["conv_w"], shard * (CONV_DIM // N_SHARD), CONV_DIM // N_SHARD, axis=1)
    own_small = lambda d: _pack([d[n].reshape(sm[n].shape) for n in SMALL])
    res = _adamw(own_small(w), own_small(mo), own_small(vo), [own_small(sm)], name="adamw_small")
    own_shapes = [sm[n].shape for n in SMALL]
    for store, r in zip((out_g, out_d, out_m, out_v), res):
        for n, val in zip(SMALL, _unpack(r, own_shapes)):
            store[n] = val.reshape(w[n].shape)

    outs = [sm["loss"].reshape(()), grad_x[None]]
    for store in (out_g, out_d, out_m, out_v):
        outs += [store[n] for n in WEIGHTS]
    return tuple(outs)
```

```python
import functools

import jax
import jax.numpy as jnp
from jax import lax
from jax.experimental import pallas as pl
from jax.experimental.pallas import tpu as pltpu

f32 = jnp.float32
bf16 = jnp.bfloat16

D = 1024
EPS = 1e-6
SB_HD = 64
SSD_INNER = 2048
SSD_HEADS = 32
SSD_GROUPS = 4
SSD_N = 128
SSD_L = 128
CONV_K = 4
CONV_DIM = 3072
MEM_HEADS = 4
MEM_HD = 256
D_FF = 4096
D_IN = 12320
N_SHARD = 4
N_DEV = 8

P_QKV, P_XBC, P_GATE, P_MEMQ, P_Z, P_DT, P_TOT = 0, 3072, 6144, 9216, 10240, 12288, 12416
R_QKV, R_Z, R_XBC, R_DT, R_MEMQ, R_GATE = (0, 3072), (3072, 5120), (5120, 8192), (8192, 8224), (8224, 9248), (9248, 12320)

ADAM_LR = 0.001
ADAM_B1 = 0.9
ADAM_B2 = 0.999
ADAM_EPS = 1e-08
ADAM_WD = 0.01
ADAM_STEP = 10

VMEM_LIMIT = 56 * 1024 * 1024

NN = (((1,), (0,)), ((), ()))
NT = (((1,), (1,)), ((), ()))
TN = (((0,), (0,)), ((), ()))


def _dot(a, b, dims=NN):
    return lax.dot_general(a, b, dims, preferred_element_type=f32)


def _params(sem=None):
    return pltpu.CompilerParams(dimension_semantics=sem, vmem_limit_bytes=VMEM_LIMIT)


def _sigmoid(x):
    return 1.0 / (1.0 + jnp.exp(-x))


def _split2(x):
    hi = x.astype(bf16)
    lo = (x - hi.astype(f32)).astype(bf16)
    return hi, lo


def _split3(x):
    hi = x.astype(bf16)
    r = x - hi.astype(f32)
    mid = r.astype(bf16)
    lo = (r - mid.astype(f32)).astype(bf16)
    return hi, mid, lo


def _mm(a, b, mode, *, tm, tn, name, out_dtypes=(f32,), epi=None, extras=()):
    M = a.shape[1] if mode == "tn" else a.shape[0]
    N = b.shape[0] if mode == "nt" else b.shape[1]
    tm, tn = min(tm, M), min(tn, N)
    if mode == "nn":
        (M, K), N = a.shape, b.shape[1]
        a_spec = pl.BlockSpec((tm, K), lambda i, j: (i, 0))
        b_spec = pl.BlockSpec((K, tn), lambda i, j: (0, j))
        dims = NN
    elif mode == "nt":
        (M, K), N = a.shape, b.shape[0]
        a_spec = pl.BlockSpec((tm, K), lambda i, j: (i, 0))
        b_spec = pl.BlockSpec((tn, K), lambda i, j: (j, 0))
        dims = NT
    else:
        (K, M), N = a.shape, b.shape[1]
        a_spec = pl.BlockSpec((K, tm), lambda i, j: (0, i))
        b_spec = pl.BlockSpec((K, tn), lambda i, j: (0, j))
        dims = TN
    assert M % tm == 0 and N % tn == 0, (name, M, N, tm, tn)
    n_ex = len(extras)
    o_spec = pl.BlockSpec((tm, tn), lambda i, j: (i, j))

    def body(a_ref, b_ref, *rest):
        acc = _dot(a_ref[...].astype(bf16), b_ref[...].astype(bf16), dims)
        res = (acc,) if epi is None else epi(acc, *[e[...] for e in rest[:n_ex]])
        for o_ref, r in zip(rest[n_ex:], res):
            o_ref[...] = r.astype(o_ref.dtype)

    out = pl.pallas_call(
        body, name=name, grid=(M // tm, N // tn),
        in_specs=[a_spec, b_spec] + [o_spec] * n_ex,
        out_specs=[o_spec] * len(out_dtypes),
        out_shape=[jax.ShapeDtypeStruct((M, N), dt) for dt in out_dtypes],
        compiler_params=_params(("parallel", "parallel")),
    )(a, b, *extras)
    return out[0] if len(out_dtypes) == 1 else out


def _rms_fwd(x, g, *, name, out_dtype, residual=None, tm=512):
    S, C = x.shape
    tm = min(tm, S)
    has_res = residual is not None

    def body(x_ref, g_ref, *rest):
        xv = x_ref[...]
        r = lax.rsqrt(jnp.mean(xv * xv, axis=1, keepdims=True) + EPS)
        y = xv * r * g_ref[...]
        if has_res:
            y = y + rest[0][...]
        rest[-1][...] = y.astype(out_dtype)

    row = pl.BlockSpec((tm, C), lambda i: (i, 0))
    vec = pl.BlockSpec((1, C), lambda i: (0, 0))
    args = (x, g) + ((residual,) if has_res else ())
    return pl.pallas_call(
        body, name=name, grid=(S // tm,),
        in_specs=[row, vec] + ([row] if has_res else []),
        out_specs=row, out_shape=jax.ShapeDtypeStruct((S, C), out_dtype),
        compiler_params=_params(("parallel",)),
    )(*args)


def _rms_bwd(x, dy, g, *, name, out_dtype, add=None, tm=512):
    S, C = x.shape
    tm = min(tm, S)
    has_add = add is not None

    def body(x_ref, dy_ref, g_ref, *rest):
        dx_ref, dg_ref = rest[-2], rest[-1]
        xv = x_ref[...]
        dyv = dy_ref[...].astype(f32)
        r = lax.rsqrt(jnp.mean(xv * xv, axis=1, keepdims=True) + EPS)
        xh = xv * r
        dxh = dyv * g_ref[...]
        dx = r * (dxh - xh * jnp.mean(dxh * xh, axis=1, keepdims=True))
        if has_add:
            dx = dx + rest[0][...]
        dx_ref[...] = dx.astype(out_dtype)

        @pl.when(pl.program_id(0) == 0)
        def _():
            dg_ref[...] = jnp.zeros_like(dg_ref)

        dg_ref[...] += jnp.sum(dyv * xh, axis=0, keepdims=True)

    row = pl.BlockSpec((tm, C), lambda i: (i, 0))
    vec = pl.BlockSpec((1, C), lambda i: (0, 0))
    args = (x, dy, g) + ((add,) if has_add else ())
    return pl.pallas_call(
        body, name=name, grid=(S // tm,),
        in_specs=[row, row, vec] + ([row] if has_add else []),
        out_specs=[row, vec],
        out_shape=[jax.ShapeDtypeStruct((S, C), out_dtype), jax.ShapeDtypeStruct((1, C), f32)],
        compiler_params=_params(("arbitrary",)),
    )(*args)


SB_T = 128
SB_GROUPS = (8, 4, 2, 1)
SB_GROUPS_BWD = (4, 2, 1)


def _sb_masks():
    lane = lax.broadcasted_iota(jnp.int32, (1, 128), 1)
    m_a = (lane < SB_HD).astype(f32)
    return m_a, 1.0 - m_a


def _sb_logits(z, mask):
    l1p = jnp.log(1.0 + jnp.exp(-jnp.abs(z)))
    lb = jnp.minimum(z, 0.0) - l1p
    lk = lb - z
    if mask is not None:
        lk = jnp.where(mask, lk, 0.0)
    return lb, lk


def _chunks(a, n):
    return [a[:, u * SB_T:(u + 1) * SB_T] for u in range(n)]


def _cat(parts, axis):
    return parts[0] if len(parts) == 1 else jnp.concatenate(parts, axis=axis)


def _chunk_matmul(parts_list, u_mat):
    out = _dot(_cat(parts_list, 0), u_mat)
    return [out[u * SB_T:(u + 1) * SB_T] for u in range(len(parts_list))]


def _chunk_cumsum(lk, n, u_mat):
    hi = lk.astype(bf16)
    lo = (lk - hi.astype(f32)).astype(bf16)
    out = _chunk_matmul(_chunks(hi, n) + _chunks(lo, n), u_mat)
    return [out[u] + out[n + u] for u in range(n)]


def _sb_fwd(proj, S):
    nq = S // SB_T
    n_pairs = D // 128
    scale = SB_HD ** -0.5

    def body(q_ref, k_ref, v_ref, o_ref, t_ref):
        i = pl.program_id(1)
        m_a, m_b = _sb_masks()
        r_i = lax.broadcasted_iota(jnp.int32, (SB_T, SB_T), 0)
        c_i = lax.broadcasted_iota(jnp.int32, (SB_T, SB_T), 1)
        u_mat = (r_i > c_i).astype(bf16)
        causal = c_i < r_i
        q = q_ref[...] * scale
        q_h = ((q * m_a).astype(bf16), (q * m_b).astype(bf16))

        def group(j_lo, n, carry, mask):
            acc, c_a, c_b = carry
            rows = pl.ds(pl.multiple_of(j_lo * SB_T, SB_T), n * SB_T)
            k = k_ref[rows, :].astype(bf16)
            v = v_ref[rows, :]
            zs = [_dot(q_b, k, NT) for q_b in q_h]
            lbk = [_sb_logits(z, mask) for z in zs]
            parts = [_chunk_cumsum(lk, n, u_mat) for _, lk in lbk]
            ws, cs = [], []
            for (lb, lk), part, c in zip(lbk, parts, (c_a, c_b)):
                lb_c, lk_c = _chunks(lb, n), _chunks(lk, n)
                w_c = [None] * n
                for u in reversed(range(n)):
                    w_c[u] = jnp.exp(lb_c[u] + c + part[u])
                    c = c + jnp.sum(lk_c[u], axis=1, keepdims=True)
                w = _cat(w_c, 1)
                if mask is not None:
                    w = jnp.where(mask, w, 0.0)
                ws.append(w.astype(bf16))
                cs.append(c)
            for w, m in zip(ws, (m_a, m_b)):
                acc = acc + _dot(w, (v * m).astype(bf16))
            return acc, cs[0], cs[1]

        zero_c = jnp.zeros((SB_T, 1), f32)
        carry = group(i, 1, (jnp.zeros((SB_T, 128), f32), zero_c, zero_c), causal)
        left = i
        for n in SB_GROUPS:
            trips = left // n
            carry = lax.fori_loop(
                0, trips, functools.partial(lambda gi, cr, n, left: group(left - (gi + 1) * n, n, cr, None), n=n, left=left),
                carry)
            left = left - trips * n
        o_ref[...] = carry[0]
        lane = lax.broadcasted_iota(jnp.int32, (1, 128), 1)
        t_ref[...] = jnp.where(lane == 0, carry[1], 0.0) + jnp.where(lane == SB_HD, carry[2], 0.0)

    qs = pl.BlockSpec((SB_T, 128), lambda h, i: (i, h))
    return pl.pallas_call(
        body, name="sb_fwd", grid=(n_pairs, nq),
        in_specs=[qs,
                  pl.BlockSpec((S, 128), lambda h, i: (0, n_pairs + h)),
                  pl.BlockSpec((S, 128), lambda h, i: (0, 2 * n_pairs + h))],
        out_specs=[qs, qs], out_shape=[jax.ShapeDtypeStruct((S, D), f32)] * 2,
        compiler_params=_params(("parallel", "arbitrary")),
    )(proj, proj, proj)


def _sb_bwd(proj, tot_lk, do, S):
    nq = S // SB_T
    n_pairs = D // 128
    scale = SB_HD ** -0.5

    def body(q_ref, k_ref, v_ref, t_ref, do_ref, dq_ref, dk_ref, dv_ref, dk_acc, dv_acc):
        i = pl.program_id(1)
        m_a, m_b = _sb_masks()
        r_i = lax.broadcasted_iota(jnp.int32, (SB_T, SB_T), 0)
        c_i = lax.broadcasted_iota(jnp.int32, (SB_T, SB_T), 1)
        u_inc = (r_i <= c_i).astype(bf16)
        u_exc = (r_i < c_i).astype(bf16)
        causal = c_i < r_i

        @pl.when(i == 0)
        def _():
            dk_acc[...] = jnp.zeros_like(dk_acc)
            dv_acc[...] = jnp.zeros_like(dv_acc)

        q = q_ref[...] * scale
        dov = do_ref[...]
        tv = t_ref[...]
        heads = []
        for m in (m_a, m_b):
            heads.append(((q * m).astype(bf16), (dov * m).astype(bf16), jnp.sum(tv * m, axis=1, keepdims=True), m))

        def group(j_lo, n, carry, mask):
            dq_acc, cp_a, cp_b, ce_a, ce_b = carry
            rows = pl.ds(pl.multiple_of(j_lo * SB_T, SB_T), n * SB_T)
            k_f = k_ref[rows, :]
            k = k_f.astype(bf16)
            v = v_ref[rows, :].astype(bf16)
            zs = [_dot(h[0], k, NT) for h in heads]
            dws = [_dot(h[1], v, NT) for h in heads]
            lbk = [_sb_logits(z, mask) for z in zs]
            parts = [_chunk_cumsum(lk, n, u_inc) for _, lk in lbk]
            ws, es, cps = [], [], []
            for (lb, lk), part, dw, h, cp in zip(lbk, parts, dws, heads, (cp_a, cp_b)):
                lb_c, lk_c = _chunks(lb, n), _chunks(lk, n)
                w_c = []
                for u in range(n):
                    w_c.append(jnp.exp(lb_c[u] + (h[2] - cp) - part[u]))
                    cp = cp + jnp.sum(lk_c[u], axis=1, keepdims=True)
                w = _cat(w_c, 1)
                if mask is not None:
                    w = jnp.where(mask, w, 0.0)
                ws.append(w)
                es.append(dw * w)
                cps.append(cp)
            e_parts = [_chunk_matmul(_chunks(e.astype(bf16), n), u_exc) for e in es]
            dzs, ces = [], []
            for (lb, _), e, e_part, ce in zip(lbk, es, e_parts, (ce_a, ce_b)):
                e_c = _chunks(e, n)
                big_c = []
                for u in range(n):
                    big_c.append(ce + e_part[u])
                    ce = ce + jnp.sum(e_c[u], axis=1, keepdims=True)
                sig = jnp.exp(lb)
                dz = e * (1.0 - sig) - _cat(big_c, 1) * sig
                if mask is not None:
                    dz = jnp.where(mask, dz, 0.0)
                dzs.append(dz.astype(bf16))
                ces.append(ce)
            dk_t = jnp.zeros((n * SB_T, 128), f32)
            dv_t = jnp.zeros((n * SB_T, 128), f32)
            for dz_b, w, h in zip(dzs, ws, heads):
                dq_acc = dq_acc + _dot(dz_b, (k_f * h[3]).astype(bf16))
                dk_t = dk_t + _dot(dz_b, h[0], TN)
                dv_t = dv_t + _dot(w.astype(bf16), h[1], TN)
            dk_acc[rows, :] += dk_t
            dv_acc[rows, :] += dv_t
            return dq_acc, cps[0], cps[1], ces[0], ces[1]

        zc = jnp.zeros((SB_T, 1), f32)
        carry = (jnp.zeros((SB_T, 128), f32), zc, zc, zc, zc)
        done = 0
        for n in SB_GROUPS_BWD:
            trips = (i - done) // n
            carry = lax.fori_loop(
                0, trips, functools.partial(lambda gi, cr, n, done: group(done + gi * n, n, cr, None), n=n, done=done),
                carry)
            done = done + trips * n
        carry = group(i, 1, carry, causal)
        dq_ref[...] = (carry[0] * scale).astype(bf16)

        @pl.when(i == nq - 1)
        def _():
            dk_ref[...] = dk_acc[...].astype(bf16)
            dv_ref[...] = dv_acc[...].astype(bf16)

    qs = pl.BlockSpec((SB_T, 128), lambda h, i: (i, h))
    full = pl.BlockSpec((S, 128), lambda h, i: (0, h))
    dq, dk, dv = pl.pallas_call(
        body, name="sb_bwd", grid=(n_pairs, nq),
        in_specs=[qs,
                  pl.BlockSpec((S, 128), lambda h, i: (0, n_pairs + h)),
                  pl.BlockSpec((S, 128), lambda h, i: (0, 2 * n_pairs + h)),
                  qs, qs],
        out_specs=[qs, full, full],
        out_shape=[jax.ShapeDtypeStruct((S, D), bf16)] * 3,
        scratch_shapes=[pltpu.VMEM((S, 128), f32), pltpu.VMEM((S, 128), f32)],
        compiler_params=_params(("parallel", "arbitrary")),
    )(proj, proj, proj, tot_lk, do)
    return dq, dk, dv


CONV_CB = 256
HALO = 8


def _conv_fwd(proj, conv_w, conv_b, S):
    tr = min(512, S)

    def body(x_ref, w_ref, b_ref, xc_ref, xbc_ref):
        w = w_ref[...]
        for t in range(S // tr):
            cur = x_ref[t * tr:(t + 1) * tr, :]
            halo = x_ref[t * tr - HALO:t * tr, :] if t else jnp.zeros((HALO, CONV_CB), f32)
            win = jnp.concatenate([halo, cur], axis=0)
            acc = b_ref[...] + w[CONV_K - 1:CONV_K, :] * cur
            for k in range(CONV_K - 1):
                acc = acc + w[k:k + 1, :] * pltpu.roll(win, CONV_K - 1 - k, 0)[HALO:, :]
            xc_ref[t * tr:(t + 1) * tr, :] = acc
            xbc_ref[t * tr:(t + 1) * tr, :] = acc * _sigmoid(acc)

    col = pl.BlockSpec((S, CONV_CB), lambda c: (0, c))
    return pl.pallas_call(
        body, name="conv_fwd", grid=(CONV_DIM // CONV_CB,),
        in_specs=[pl.BlockSpec((S, CONV_CB), lambda c: (0, P_XBC // CONV_CB + c)),
                  pl.BlockSpec((CONV_K, CONV_CB), lambda c: (0, c)),
                  pl.BlockSpec((1, CONV_CB), lambda c: (0, c))],
        out_specs=[col, col], out_shape=[jax.ShapeDtypeStruct((S, CONV_DIM), f32)] * 2,
        compiler_params=_params(("parallel",)),
    )(proj, conv_w, conv_b)


def _conv_bwd(proj, xc, dxbc, conv_w, S):
    tr = min(512, S)

    def body(x_ref, xc_ref, dy_ref, w_ref, dx_ref, dw_ref, db_ref, dxc_s):
        w = w_ref[...]
        xcv = xc_ref[...]
        sg = _sigmoid(xcv)
        dxc_s[0:S, :] = dy_ref[...] * (sg * (1.0 + xcv * (1.0 - sg)))
        dxc_s[S:S + HALO, :] = jnp.zeros((HALO, CONV_CB), f32)
        dws = [jnp.zeros((1, CONV_CB), f32) for _ in range(CONV_K)]
        db = jnp.zeros((1, CONV_CB), f32)
        for t in range(S // tr):
            cur = x_ref[t * tr:(t + 1) * tr, :]
            halo = x_ref[t * tr - HALO:t * tr, :] if t else jnp.zeros((HALO, CONV_CB), f32)
            win = jnp.concatenate([halo, cur], axis=0)
            dwin = dxc_s[t * tr:(t + 1) * tr + HALO, :]
            dcur = dwin[0:tr, :]
            db = db + jnp.sum(dcur, axis=0, keepdims=True)
            dws[CONV_K - 1] = dws[CONV_K - 1] + jnp.sum(dcur * cur, axis=0, keepdims=True)
            dx = w[CONV_K - 1:CONV_K, :] * dcur
            for k in range(CONV_K - 1):
                sh = CONV_K - 1 - k
                dws[k] = dws[k] + jnp.sum(dcur * pltpu.roll(win, sh, 0)[HALO:, :], axis=0, keepdims=True)
                dx = dx + w[k:k + 1, :] * pltpu.roll(dwin, tr + HALO - sh, 0)[0:tr, :]
            dx_ref[t * tr:(t + 1) * tr, :] = dx.astype(bf16)
        dw_ref[...] = jnp.concatenate(dws + [jnp.zeros((8 - CONV_K, CONV_CB), f32)], axis=0)
        db_ref[...] = db

    col = pl.BlockSpec((S, CONV_CB), lambda c: (0, c))
    return pl.pallas_call(
        body, name="conv_bwd", grid=(CONV_DIM // CONV_CB,),
        in_specs=[pl.BlockSpec((S, CONV_CB), lambda c: (0, P_XBC // CONV_CB + c)), col, col,
                  pl.BlockSpec((CONV_K, CONV_CB), lambda c: (0, c))],
        out_specs=[col, pl.BlockSpec((8, CONV_CB), lambda c: (0, c)), pl.BlockSpec((1, CONV_CB), lambda c: (0, c))],
        out_shape=[jax.ShapeDtypeStruct((S, CONV_DIM), bf16), jax.ShapeDtypeStruct((8, CONV_DIM), f32),
                   jax.ShapeDtypeStruct((1, CONV_DIM), f32)],
        scratch_shapes=[pltpu.VMEM((S + HALO, CONV_CB), f32)],
        compiler_params=_params(("parallel",)),
    )(proj, xc, dxbc, conv_w)


N_PAIR = SSD_HEADS // 2
NEG = -1e30


def _softplus(x):
    return jnp.maximum(x, 0.0) + jnp.log(1.0 + jnp.exp(-jnp.abs(x)))


def _ssd_common(dtr, dtb, alog):
    L = SSD_L
    r_i = lax.broadcasted_iota(jnp.int32, (L, L), 0)
    c_i = lax.broadcasted_iota(jnp.int32, (L, L), 1)
    dt = _softplus(dtr + dtb)
    a = -jnp.exp(alog)
    da = dt * a
    lower = (r_i >= c_i).astype(bf16)
    upper = (r_i <= c_i).astype(bf16)
    parts = _split3(da)
    a_cs = sum(_dot(lower, p) for p in parts)
    a_cs_t = sum(_dot(p, upper, TN) for p in parts)
    return dt, a, a_cs, a_cs_t, r_i >= c_i


def _pair_vec(lane, v, h):
    return jnp.where(lane < SB_HD, v[:, h:h + 1], v[:, h + 1:h + 2])


def _decay_mat(a_cs, a_cs_t, h, tril):
    return jnp.exp(jnp.where(tril, a_cs[:, h:h + 1] - a_cs_t[h:h + 1, :], NEG))


def _ssd_fwd(xbc, proj, pdt, dt_bias_p, a_log_p, d_skip_c, ssd_norm, S):
    L = SSD_L
    nc = S // L

    def body(xbc_ref, dt_ref, z_ref, dtb_ref, alog_ref, dsk_ref, gn_ref, y_ref, yn_ref, hp_ref, state):
        c = pl.program_id(0)

        @pl.when(c == 0)
        def _():
            state[...] = jnp.zeros_like(state)

        hp_ref[0] = state[...]
        lane = lax.broadcasted_iota(jnp.int32, (1, 128), 1)
        row128 = lax.broadcasted_iota(jnp.int32, (128, 1), 0)
        m_a, m_b = _sb_masks()
        dt, a, a_cs, a_cs_t, tril = _ssd_common(dt_ref[...], dtb_ref[...], alog_ref[...])
        a_last = a_cs[L - 1:L, :]
        for g in range(SSD_GROUPS):
            b_g = xbc_ref[:, SSD_INNER + g * SSD_N:SSD_INNER + (g + 1) * SSD_N].astype(bf16)
            c_g = xbc_ref[:, SSD_INNER + (SSD_GROUPS + g) * SSD_N:SSD_INNER + (SSD_GROUPS + g + 1) * SSD_N].astype(bf16)
            cb = _dot(c_g, b_g, NT)
            for pr in range(4):
                h = 8 * g + 2 * pr
                pi = h // 2
                cols = slice(pi * 128, (pi + 1) * 128)
                xs = xbc_ref[:, cols]
                x = xs * _pair_vec(lane, dt, h)
                acs = _pair_vec(lane, a_cs, h)
                al = _pair_vec(lane, a_last, h)
                w_a = (cb * _decay_mat(a_cs, a_cs_t, h, tril)).astype(bf16)
                w_b = (cb * _decay_mat(a_cs, a_cs_t, h + 1, tril)).astype(bf16)
                yd = _dot(w_a, (x * m_a).astype(bf16)) + _dot(w_b, (x * m_b).astype(bf16))
                hp = state[pi]
                yo = _dot(c_g, hp.astype(bf16), NT) * jnp.exp(acs)
                y_ref[:, cols] = yd + yo + dsk_ref[:, cols] * xs
                dec = jnp.exp(jnp.where(row128 < SB_HD, a_last[:, h:h + 1], a_last[:, h + 1:h + 2]))
                state[pi] = hp * dec + _dot((x * jnp.exp(al - acs)).astype(bf16), b_g, TN)
        zz = z_ref[...]
        y2 = y_ref[...] * (zz * _sigmoid(zz))
        gw = SSD_INNER // SSD_GROUPS
        for g in range(SSD_GROUPS):
            yg = y2[:, g * gw:(g + 1) * gw]
            rg = lax.rsqrt(jnp.mean(yg * yg, axis=1, keepdims=True) + EPS)
            yn_ref[:, g * gw:(g + 1) * gw] = (yg * rg * gn_ref[:, g * gw:(g + 1) * gw]).astype(bf16)

    vec128 = pl.BlockSpec((1, 128), lambda c: (0, 0))
    vecin = pl.BlockSpec((1, SSD_INNER), lambda c: (0, 0))
    rows = pl.BlockSpec((L, SSD_INNER), lambda c: (c, 0))
    return pl.pallas_call(
        body, name="ssd_fwd", grid=(nc,),
        in_specs=[pl.BlockSpec((L, CONV_DIM), lambda c: (c, 0)),
                  pl.BlockSpec((L, 128), lambda c: (c, 0)),
                  pl.BlockSpec((L, SSD_INNER), lambda c: (c, P_Z // SSD_INNER)),
                  vec128, vec128, vecin, vecin],
        out_specs=[rows, rows, pl.BlockSpec((1, N_PAIR, 128, SSD_N), lambda c: (c, 0, 0, 0))],
        out_shape=[jax.ShapeDtypeStruct((S, SSD_INNER), f32), jax.ShapeDtypeStruct((S, SSD_INNER), bf16),
                   jax.ShapeDtypeStruct((nc, N_PAIR, 128, SSD_N), f32)],
        scratch_shapes=[pltpu.VMEM((N_PAIR, 128, SSD_N), f32)],
        compiler_params=_params(("arbitrary",)),
    )(xbc, pdt, proj, dt_bias_p, a_log_p, d_skip_c, ssd_norm)


def _sum_all(v):
    return jnp.sum(jnp.sum(v, axis=1, keepdims=True), axis=0, keepdims=True)


def _ssd_bwd(dyn, y, xbc, proj, pdt, hprev, dt_bias_p, a_log_p, d_skip_c, ssd_norm, S):
    L = SSD_L
    nc = S // L

    def body(dyn_ref, y_ref, xbc_ref, dt_ref, z_ref, hp_ref, dtb_ref, alog_ref, dsk_ref, gn_ref,
             dz_ref, dxbc_ref, ddt_ref, dgn_ref, dsk_out, dalog_ref, ddtb_ref, dstate, dy_s):
        c = pl.program_id(0)

        @pl.when(c == 0)
        def _():
            dstate[...] = jnp.zeros_like(dstate)
            dgn_ref[...] = jnp.zeros_like(dgn_ref)
            dsk_out[...] = jnp.zeros_like(dsk_out)
            dalog_ref[...] = jnp.zeros_like(dalog_ref)
            ddtb_ref[...] = jnp.zeros_like(ddtb_ref)

        lane = lax.broadcasted_iota(jnp.int32, (1, 128), 1)
        row128 = lax.broadcasted_iota(jnp.int32, (128, 1), 0)
        rowl = lax.broadcasted_iota(jnp.int32, (L, 1), 0)
        m_a, m_b = _sb_masks()
        dtr = dt_ref[...]
        dt, a, a_cs, a_cs_t, tril = _ssd_common(dtr, dtb_ref[...], alog_ref[...])
        a_last = a_cs[L - 1:L, :]

        zz = z_ref[...]
        sg = _sigmoid(zz)
        silu = zz * sg
        yv = y_ref[...]
        y2 = yv * silu
        gw = SSD_INNER // SSD_GROUPS
        for g in range(SSD_GROUPS):
            sl = slice(g * gw, (g + 1) * gw)
            yg = y2[:, sl]
            rg = lax.rsqrt(jnp.mean(yg * yg, axis=1, keepdims=True) + EPS)
            yh = yg * rg
            dyn_g = dyn_ref[:, sl]
            dgn_ref[:, sl] += jnp.sum(dyn_g * yh, axis=0, keepdims=True)
            dyh = dyn_g * gn_ref[:, sl]
            dy2 = rg * (dyh - yh * jnp.mean(dyh * yh, axis=1, keepdims=True))
            dy_s[:, sl] = dy2 * silu[:, sl]
            dz_ref[:, sl] = (dy2 * yv[:, sl] * (sg[:, sl] * (1.0 + zz[:, sl] * (1.0 - sg[:, sl])))).astype(bf16)

        d_acs = jnp.zeros((L, 128), f32)
        ddt_x = jnp.zeros((L, 128), f32)
        dsk_acc = jnp.zeros((1, 128), f32)
        for g in range(SSD_GROUPS):
            bsl = slice(SSD_INNER + g * SSD_N, SSD_INNER + (g + 1) * SSD_N)
            csl = slice(SSD_INNER + (SSD_GROUPS + g) * SSD_N, SSD_INNER + (SSD_GROUPS + g + 1) * SSD_N)
            b_g = xbc_ref[:, bsl].astype(bf16)
            c_g = xbc_ref[:, csl].astype(bf16)
            cb = _dot(c_g, b_g, NT)
            dcb = jnp.zeros((L, L), f32)
            dc_g = jnp.zeros((L, SSD_N), f32)
            db_g = jnp.zeros((L, SSD_N), f32)
            for pr in range(4):
                h = 8 * g + 2 * pr
                pi = h // 2
                cols = slice(pi * 128, (pi + 1) * 128)
                xs = xbc_ref[:, cols]
                dt_p = _pair_vec(lane, dt, h)
                x = xs * dt_p
                acs = _pair_vec(lane, a_cs, h)
                al = _pair_vec(lane, a_last, h)
                e_a = jnp.exp(acs)
                dte = jnp.exp(al - acs)
                m_mat_a = _decay_mat(a_cs, a_cs_t, h, tril)
                m_mat_b = _decay_mat(a_cs, a_cs_t, h + 1, tril)
                dyp = dy_s[:, cols]
                dsk = dsk_ref[:, cols]
                d_hn = dstate[pi]
                hp = hp_ref[0, pi]
                dy_a = (dyp * m_a).astype(bf16)
                dy_b = (dyp * m_b).astype(bf16)
                x_b = x.astype(bf16)
                gm_a = _dot(dy_a, x_b, NT) * m_mat_a
                gm_b = _dot(dy_b, x_b, NT) * m_mat_b
                dcb = dcb + gm_a + gm_b
                dx_d = _dot((cb * m_mat_a).astype(bf16), dy_a, TN) + _dot((cb * m_mat_b).astype(bf16), dy_b, TN)
                dx_s = _dot(b_g, d_hn.astype(bf16), NT) * dte
                dx = dx_d + dx_s
                dxbc_ref[:, cols] = dx * dt_p + dsk * dyp
                xdxs = x * dx_s
                u = dyp * (_dot(c_g, hp.astype(bf16), NT) * e_a) - xdxs
                hh = d_hn * hp
                dec = jnp.exp(jnp.where(row128 < SB_HD, a_last[:, h:h + 1], a_last[:, h + 1:h + 2]))
                for hd, m, gm in ((h, m_a, gm_a), (h + 1, m_b, gm_b)):
                    half = slice(0, SB_HD) if hd == h else slice(SB_HD, 128)
                    last = _sum_all(xdxs * m) + jnp.exp(a_last[:, hd:hd + 1]) * _sum_all(hh[half, :])
                    qm = gm * cb
                    col = jnp.sum(qm - qm.T, axis=1, keepdims=True) + jnp.sum(u * m, axis=1, keepdims=True)
                    col = col + jnp.where(rowl == L - 1, last, 0.0)
                    d_acs = jnp.where(lane == hd, col, d_acs)
                    ddt_x = jnp.where(lane == hd, jnp.sum(dx * xs * m, axis=1, keepdims=True), ddt_x)
                    dsk_acc = jnp.where(lane == hd, _sum_all(dyp * xs * m), dsk_acc)
                dye = (dyp * e_a).astype(bf16)
                dc_g = dc_g + _dot(dye, hp.astype(bf16))
                db_g = db_g + _dot((x * dte).astype(bf16), d_hn.astype(bf16))
                dstate[pi] = dec * d_hn + _dot(dye, c_g, TN)
            dcb_b = dcb.astype(bf16)
            dxbc_ref[:, csl] = dc_g + _dot(dcb_b, b_g)
            dxbc_ref[:, bsl] = db_g + _dot(dcb_b, c_g, TN)

        r_i = lax.broadcasted_iota(jnp.int32, (L, L), 0)
        c_i = lax.broadcasted_iota(jnp.int32, (L, L), 1)
        rev = (r_i <= c_i).astype(bf16)
        dda = sum(_dot(rev, p) for p in _split3(d_acs))
        ddt = ddt_x + dda * a
        dalog_ref[...] += jnp.sum(dda * dt, axis=0, keepdims=True) * a
        ddtr = jnp.where(lane < SSD_HEADS, ddt * _sigmoid(dtr + dtb_ref[...]), 0.0)
        ddt_ref[...] = ddtr.astype(bf16)
        ddtb_ref[...] += jnp.sum(ddtr, axis=0, keepdims=True)
        dsk_out[...] += dsk_acc

    rv = lambda c: nc - 1 - c
    vec128 = pl.BlockSpec((1, 128), lambda c: (0, 0))
    vecin = pl.BlockSpec((1, SSD_INNER), lambda c: (0, 0))
    rows = pl.BlockSpec((L, SSD_INNER), lambda c: (rv(c), 0))
    return pl.pallas_call(
        body, name="ssd_bwd", grid=(nc,),
        in_specs=[rows, rows,
                  pl.BlockSpec((L, CONV_DIM), lambda c: (rv(c), 0)),
                  pl.BlockSpec((L, 128), lambda c: (rv(c), 0)),
                  pl.BlockSpec((L, SSD_INNER), lambda c: (rv(c), P_Z // SSD_INNER)),
                  pl.BlockSpec((1, N_PAIR, 128, SSD_N), lambda c: (rv(c), 0, 0, 0)),
                  vec128, vec128, vecin, vecin],
        out_specs=[rows, pl.BlockSpec((L, CONV_DIM), lambda c: (rv(c), 0)),
                   pl.BlockSpec((L, 128), lambda c: (rv(c), 0)), vecin, vec128, vec128, vec128],
        out_shape=[jax.ShapeDtypeStruct((S, SSD_INNER), bf16), jax.ShapeDtypeStruct((S, CONV_DIM), f32),
                   jax.ShapeDtypeStruct((S, 128), bf16), jax.ShapeDtypeStruct((1, SSD_INNER), f32),
                   jax.ShapeDtypeStruct((1, 128), f32), jax.ShapeDtypeStruct((1, 128), f32),
                   jax.ShapeDtypeStruct((1, 128), f32)],
        scratch_shapes=[pltpu.VMEM((N_PAIR, 128, SSD_N), f32), pltpu.VMEM((L, SSD_INNER), f32)],
        compiler_params=_params(("arbitrary",)),
    )(dyn, y, xbc, pdt, proj, hprev, dt_bias_p, a_log_p, d_skip_c, ssd_norm)


MEM_W = MEM_HEADS * MEM_HD


def _mem_probs(q, k):
    s = _dot(q, k, NT) * (MEM_HD ** -0.5)
    s = s - jnp.max(s, axis=1, keepdims=True)
    p = jnp.exp(s)
    return p / jnp.sum(p, axis=1, keepdims=True)


def _mem_fwd(proj, kv, S, tm=512):
    tm = min(tm, S)
    M = kv.shape[0]

    def body(q_ref, kv_ref, o_ref):
        for h in range(MEM_HEADS):
            sl = slice(h * MEM_HD, (h + 1) * MEM_HD)
            vsl = slice(MEM_W + h * MEM_HD, MEM_W + (h + 1) * MEM_HD)
            p = _mem_probs(q_ref[:, sl].astype(bf16), kv_ref[:, sl].astype(bf16))
            o_ref[:, sl] = _dot(p.astype(bf16), kv_ref[:, vsl].astype(bf16)).astype(bf16)

    return pl.pallas_call(
        body, name="mem_fwd", grid=(S // tm,),
        in_specs=[pl.BlockSpec((tm, MEM_W), lambda i: (i, P_MEMQ // MEM_W)),
                  pl.BlockSpec((M, 2 * MEM_W), lambda i: (0, 0))],
        out_specs=pl.BlockSpec((tm, MEM_W), lambda i: (i, 0)),
        out_shape=jax.ShapeDtypeStruct((S, MEM_W), bf16),
        compiler_params=_params(("parallel",)),
    )(proj, kv)


def _mem_bwd(proj, kv, dy, S, tm=512):
    tm = min(tm, S)
    M = kv.shape[0]
    scale = MEM_HD ** -0.5

    def body(q_ref, kv_ref, dy_ref, dq_ref, dkv_ref):
        @pl.when(pl.program_id(0) == 0)
        def _():
            dkv_ref[...] = jnp.zeros_like(dkv_ref)

        for h in range(MEM_HEADS):
            sl = slice(h * MEM_HD, (h + 1) * MEM_HD)
            vsl = slice(MEM_W + h * MEM_HD, MEM_W + (h + 1) * MEM_HD)
            q = q_ref[:, sl].astype(bf16)
            k = kv_ref[:, sl].astype(bf16)
            v = kv_ref[:, vsl].astype(bf16)
            dyh = dy_ref[:, sl].astype(bf16)
            p = _mem_probs(q, k)
            dp = _dot(dyh, v, NT)
            ds = (p * (dp - jnp.sum(dp * p, axis=1, keepdims=True)) * scale).astype(bf16)
            dq_ref[:, sl] = _dot(ds, k).astype(bf16)
            dkv_ref[:, sl] += _dot(ds, q, TN)
            dkv_ref[:, vsl] += _dot(p.astype(bf16), dyh, TN)

    return pl.pallas_call(
        body, name="mem_bwd", grid=(S // tm,),
        in_specs=[pl.BlockSpec((tm, MEM_W), lambda i: (i, P_MEMQ // MEM_W)),
                  pl.BlockSpec((M, 2 * MEM_W), lambda i: (0, 0)),
                  pl.BlockSpec((tm, MEM_W), lambda i: (i, 0))],
        out_specs=[pl.BlockSpec((tm, MEM_W), lambda i: (i, 0)), pl.BlockSpec((M, 2 * MEM_W), lambda i: (0, 0))],
        out_shape=[jax.ShapeDtypeStruct((S, MEM_W), bf16), jax.ShapeDtypeStruct((M, 2 * MEM_W), f32)],
        compiler_params=_params(("arbitrary",)),
    )(proj, kv, dy)


def _merge_fwd(proj, t0, t1, t2, S, tm=512):
    tm = min(tm, S)

    def body(g_ref, t0_ref, t1_ref, t2_ref, o_ref):
        acc = jnp.zeros((tm, D), f32)
        for b, t_ref in enumerate((t0_ref, t1_ref, t2_ref)):
            acc = acc + _sigmoid(g_ref[:, b * D:(b + 1) * D]) * t_ref[...]
        o_ref[...] = acc.astype(bf16)

    row = pl.BlockSpec((tm, D), lambda i: (i, 0))
    return pl.pallas_call(
        body, name="merge_fwd", grid=(S // tm,),
        in_specs=[pl.BlockSpec((tm, 3 * D), lambda i: (i, P_GATE // (3 * D))), row, row, row],
        out_specs=row, out_shape=jax.ShapeDtypeStruct((S, D), bf16),
        compiler_params=_params(("parallel",)),
    )(proj, t0, t1, t2)


def _merge_bwd(proj, t0, t1, t2, dm, S, tm=512):
    tm = min(tm, S)

    def body(g_ref, t0_ref, t1_ref, t2_ref, dm_ref, d0_ref, d1_ref, d2_ref, dg_ref):
        dmv = dm_ref[...]
        for b, (t_ref, d_ref) in enumerate(((t0_ref, d0_ref), (t1_ref, d1_ref), (t2_ref, d2_ref))):
            sg = _sigmoid(g_ref[:, b * D:(b + 1) * D])
            d_ref[...] = (dmv * sg).astype(bf16)
            dg_ref[:, b * D:(b + 1) * D] = (dmv * t_ref[...] * sg * (1.0 - sg)).astype(bf16)

    row = pl.BlockSpec((tm, D), lambda i: (i, 0))
    return pl.pallas_call(
        body, name="merge_bwd", grid=(S // tm,),
        in_specs=[pl.BlockSpec((tm, 3 * D), lambda i: (i, P_GATE // (3 * D))), row, row, row, row],
        out_specs=[row, row, row, pl.BlockSpec((tm, 3 * D), lambda i: (i, 0))],
        out_shape=[jax.ShapeDtypeStruct((S, D), bf16)] * 3 + [jax.ShapeDtypeStruct((S, 3 * D), bf16)],
        compiler_params=_params(("parallel",)),
    )(proj, t0, t1, t2, dm)


def _loss_head(ff, g, h1, target, S, tm=512):
    tm = min(tm, S)

    def body(ff_ref, g_ref, h1_ref, t_ref, dh_ref, loss_ref):
        xv = ff_ref[...]
        r = lax.rsqrt(jnp.mean(xv * xv, axis=1, keepdims=True) + EPS)
        err = h1_ref[...] + xv * r * g_ref[...] - t_ref[...]
        dh_ref[...] = err * (1.0 / D)

        @pl.when(pl.program_id(0) == 0)
        def _():
            loss_ref[...] = jnp.zeros_like(loss_ref)

        loss_ref[...] += 0.5 * _sum_all(jnp.mean(err * err, axis=1, keepdims=True)) * jnp.ones((1, 128), f32)

    row = pl.BlockSpec((tm, D), lambda i: (i, 0))
    return pl.pallas_call(
        body, name="loss_head", grid=(S // tm,),
        in_specs=[row, pl.BlockSpec((1, D), lambda i: (0, 0)), row, row],
        out_specs=[row, pl.BlockSpec((1, 128), lambda i: (0, 0))],
        out_shape=[jax.ShapeDtypeStruct((S, D), f32), jax.ShapeDtypeStruct((1, 128), f32)],
        compiler_params=_params(("arbitrary",)),
    )(ff, g, h1, target)


def _local_step(x, mem, target, first_weights, late_weights, small, grads_ready):
    S = x.shape[0]
    M = mem.shape[0]
    pad = lambda v: jnp.pad(v, ((0, 0), (0, 128 - SSD_HEADS)))
    dtb_p, alog_p = pad(small["dt_bias"]), pad(small["a_log"])
    dsk_c = jnp.repeat(small["d_skip"], SB_HD, axis=1)

    u = _rms_fwd(x, small["norm_mix_pre"], name="norm_pre", out_dtype=bf16)
    wts = dict(first_weights(u))
    small = dict(small, conv_w=wts.pop("conv_w"))
    proj = _mm(u, wts["w_main"], "nn", tm=1024, tn=1024, name="in_proj")
    pdt = _mm(u, wts["w_dt"], "nn", tm=1024, tn=128, name="in_proj_dt")
    y_sb, tot_lk = _sb_fwd(proj, S)
    xc, xbc = _conv_fwd(proj, small["conv_w"], small["conv_b"], S)
    y_ssd, yn, hprev = _ssd_fwd(xbc, proj, pdt, dtb_p, alog_p, dsk_c, small["ssd_norm"], S)
    mn = _rms_fwd(mem, small["norm_mem"], name="norm_mem", out_dtype=bf16, tm=min(512, M))
    kv = _mm(mn, wts["w_mem_kv"], "nn", tm=M, tn=1024, name="mem_kv")
    y_mem = _mem_fwd(proj, kv, S)
    wts.update(late_weights(y_mem))
    t0 = _mm(y_sb, wts["w_sb_out"], "nn", tm=1024, tn=1024, name="sb_out")
    t1 = _mm(yn, wts["w_ssd_out"], "nn", tm=1024, tn=1024, name="ssd_out")
    t2 = _mm(y_mem, wts["w_mem_out"], "nn", tm=1024, tn=1024, name="mem_out")
    merged = _merge_fwd(proj, t0, t1, t2, S)
    mix = _mm(merged, wts["w_o"], "nn", tm=1024, tn=1024, name="w_o")
    h1 = _rms_fwd(mix, small["norm_mix_post"], name="norm_mix_post", out_dtype=f32, residual=x)
    u2 = _rms_fwd(h1, small["norm_mlp_pre"], name="norm_mlp_pre", out_dtype=bf16)
    a_up, hrelu = _mm(u2, wts["w_up"], "nn", tm=1024, tn=1024, name="mlp_up", out_dtypes=(f32, bf16),
                      epi=lambda acc: (acc, jnp.square(jnp.maximum(acc, 0.0))))
    ff = _mm(hrelu, wts["w_down"], "nn", tm=1024, tn=1024, name="mlp_down")
    dh2, loss = _loss_head(ff, small["norm_mlp_post"], h1, target, S)

    g = {}
    dff, g["norm_mlp_post"] = _rms_bwd(ff, dh2, small["norm_mlp_post"], name="norm_mlp_post_bwd", out_dtype=bf16)
    da = _mm(dff, wts["w_down"], "nt", tm=1024, tn=1024, name="mlp_down_dx", out_dtypes=(bf16,),
             epi=lambda acc, a: (acc * (2.0 * jnp.maximum(a, 0.0)),), extras=(a_up,))
    g["w_down"] = _mm(hrelu, dff, "tn", tm=1024, tn=1024, name="mlp_down_dw")
    du2 = _mm(da, wts["w_up"], "nt", tm=1024, tn=1024, name="mlp_up_dx")
    g["w_up"] = _mm(u2, da, "tn", tm=1024, tn=1024, name="mlp_up_dw")
    dh1, g["norm_mlp_pre"] = _rms_bwd(h1, du2, small["norm_mlp_pre"], name="norm_mlp_pre_bwd", out_dtype=f32, add=dh2)
    dmix, g["norm_mix_post"] = _rms_bwd(mix, dh1, small["norm_mix_post"], name="norm_mix_post_bwd", out_dtype=bf16)
    dmerged = _mm(dmix, wts["w_o"], "nt", tm=1024, tn=1024, name="w_o_dx")
    g["w_o"] = _mm(merged, dmix, "tn", tm=1024, tn=1024, name="w_o_dw")
    dt0, dt1, dt2, dgl = _merge_bwd(proj, t0, t1, t2, dmerged, S)
    dy_sb = _mm(dt0, wts["w_sb_out"], "nt", tm=1024, tn=1024, name="sb_out_dx")
    g["w_sb_out"] = _mm(y_sb, dt0, "tn", tm=1024, tn=1024, name="sb_out_dw")
    dy_ssd = _mm(dt1, wts["w_ssd_out"], "nt", tm=1024, tn=1024, name="ssd_out_dx")
    g["w_ssd_out"] = _mm(yn, dt1, "tn", tm=1024, tn=1024, name="ssd_out_dw")
    dy_mem = _mm(dt2, wts["w_mem_out"], "nt", tm=1024, tn=1024, name="mem_out_dx")
    g["w_mem_out"] = _mm(y_mem, dt2, "tn", tm=1024, tn=1024, name="mem_out_dw")
    dmemq, dkv = _mem_bwd(proj, kv, dy_mem, S)
    g["w_mem_kv"] = _mm(mn, dkv, "tn", tm=1024, tn=1024, name="mem_kv_dw")
    dmn = _mm(dkv, wts["w_mem_kv"], "nt", tm=M, tn=1024, name="mem_kv_dx")
    _, g["norm_mem"] = _rms_bwd(mem, dmn, small["norm_mem"], name="norm_mem_bwd", out_dtype=bf16, tm=min(512, M))
    tick = grads_ready(g)
    dz, dxbc, ddt, g["ssd_norm"], dsk, dalog, ddtb = _ssd_bwd(
        dy_ssd, y_ssd, xbc, proj, pdt, hprev, dtb_p + tick, alog_p, dsk_c, small["ssd_norm"], S)
    g["d_skip"], g["a_log"], g["dt_bias"] = dsk[:, :SSD_HEADS], dalog[:, :SSD_HEADS], ddtb[:, :SSD_HEADS]
    dxbc_raw, dcw, g["conv_b"] = _conv_bwd(proj, xc, dxbc, small["conv_w"], S)
    g["conv_w"] = dcw[:CONV_K]
    dq, dk, dv = _sb_bwd(proj, tot_lk, dy_sb, S)
    dproj = jnp.concatenate([dq, dk, dv, dxbc_raw, dgl, dmemq, dz], axis=1)
    du_dt = _mm(ddt, wts["w_dt"], "nt", tm=1024, tn=1024, name="in_proj_dt_dx")
    du = _mm(dproj, wts["w_main"], "nt", tm=512, tn=256, name="in_proj_dx",
             epi=lambda acc, e: (acc + e,), extras=(du_dt,))
    g["w_main"] = _mm(u, dproj, "tn", tm=1024, tn=1024, name="in_proj_dw")
    g["w_dt"] = _mm(u, ddt, "tn", tm=1024, tn=128, name="in_proj_dt_dw")
    grad_x, g["norm_mix_pre"] = _rms_bwd(x, du, small["norm_mix_pre"], name="norm_pre_bwd", out_dtype=f32, add=dh1)
    return loss, grad_x, g


def _to_internal(w_in):
    sec = lambda r: w_in[:, r[0]:r[1]]
    w_main = jnp.concatenate([sec(R_QKV), sec(R_XBC), sec(R_GATE), sec(R_MEMQ), sec(R_Z)], axis=1)
    w_dt = jnp.pad(sec(R_DT), ((0, 0), (0, 128 - SSD_HEADS)))
    return w_main, w_dt


def _from_internal(g_main, g_dt):
    sec = lambda p, n: g_main[:, p:p + n]
    return jnp.concatenate([sec(P_QKV, 3072), sec(P_Z, 2048), sec(P_XBC, 3072), g_dt[:, :SSD_HEADS],
                            sec(P_MEMQ, 1024), sec(P_GATE, 3072)], axis=1)


MESH = pl.DeviceIdType.MESH
ANY = pl.BlockSpec(memory_space=pl.ANY)


def _place():
    x, y, c = lax.axis_index("x"), lax.axis_index("y"), lax.axis_index("c")
    return (x, y, c), [(1 - x, y, c), (x, 1 - y, c), (1 - x, 1 - y, c)]


HBM = pl.BlockSpec(memory_space=pltpu.HBM)
SEM = pl.BlockSpec(memory_space=pltpu.SEMAPHORE)
EFFECT = pltpu.SideEffectType.DATAFLOW_SIDE_EFFECTING


def _in_hbm(a):
    return pltpu.with_memory_space_constraint(a, pltpu.HBM)


def _exchange_copy(mode, ins, lands, send, recv, a, k, me, peers, arriving):
    p = peers[k]
    theirs = 2 * p[0] + p[1]
    if mode == "gather":
        src, dst = ins[a], lands[a].at[theirs if arriving else me]
    else:
        src, dst = ins[a].at[theirs], lands[a].at[k]
    return pltpu.make_async_remote_copy(src_ref=src, dst_ref=dst, send_sem=send.at[a * 3 + k],
                                        recv_sem=recv.at[a * 3 + k], device_id=p, device_id_type=MESH)


def _exchange_start(srcs, mode, name):
    n = len(srcs)
    lands = [lax.empty((N_SHARD,) + s.shape if mode == "gather" else (3,) + s.shape[1:], s.dtype) for s in srcs]

    def body(*refs):
        ins, lnd = refs[:n], refs[n:2 * n]
        send, recv, token = refs[2 * n], refs[2 * n + 1], refs[-1]
        (x, y, c), peers = _place()
        me = 2 * x + y
        for a in range(n):
            for k in range(3):
                _exchange_copy(mode, ins, lnd, send, recv, a, k, me, peers, False).start()
        token[...] = jnp.zeros_like(token)

    through = [pltpu.HBM(t.shape, t.dtype) for t in list(srcs) + lands]
    out = pl.pallas_call(
        body, name=name,
        out_shape=(pltpu.SemaphoreType.DMA((3 * n,)), pltpu.SemaphoreType.DMA((3 * n,)), *through,
                   jax.ShapeDtypeStruct((8, 128), f32)),
        in_specs=[HBM] * (2 * n),
        out_specs=(SEM, SEM, *([HBM] * (2 * n)), pl.BlockSpec(memory_space=pltpu.VMEM)),
        input_output_aliases={i: 2 + i for i in range(2 * n)},
        compiler_params=pltpu.CompilerParams(has_side_effects=EFFECT),
    )(*[_in_hbm(t) for t in list(srcs) + lands])
    return out[0], out[1], list(out[2:2 + n]), list(out[2 + n:2 + 2 * n]), out[-1]


def _exchange_wait(started, after, mode, name):
    send, recv, srcs, lands, _ = started
    n = len(srcs)

    def body(*refs):
        ins, lnd = refs[:n], refs[n:2 * n]
        send_r, recv_r, token = refs[2 * n], refs[2 * n + 1], refs[-1]
        (x, y, c), peers = _place()
        me = 2 * x + y
        for a in range(n):
            for k in range(3):
                _exchange_copy(mode, ins, lnd, send_r, recv_r, a, k, me, peers, False).wait_send()
                _exchange_copy(mode, ins, lnd, send_r, recv_r, a, k, me, peers, True).wait_recv()
        token[...] = jnp.zeros_like(token)

    through = [pltpu.HBM(t.shape, t.dtype) for t in srcs + lands]
    out = pl.pallas_call(
        body, name=name,
        out_shape=(*through, jax.ShapeDtypeStruct((8, 128), f32)),
        in_specs=[HBM] * (2 * n) + [SEM, SEM, ANY],
        out_specs=(*([HBM] * (2 * n)), pl.BlockSpec(memory_space=pltpu.VMEM)),
        input_output_aliases={i: i for i in range(2 * n)},
        compiler_params=pltpu.CompilerParams(has_side_effects=EFFECT),
    )(*srcs, *lands, send, recv, after)
    return list(out[:n]), list(out[n:2 * n]), out[-1]


def _exchange_packets(packet):
    def body(pk, pk_out, send, recv, loc):
        x, y, c = lax.axis_index("x"), lax.axis_index("y"), lax.axis_index("c")
        lin = 4 * x + 2 * y + c
        own = pltpu.make_async_copy(pk, pk_out.at[lin], loc.at[0])
        own.start()

        def pk_copy(m, slot):
            dev = (x ^ ((m >> 2) & 1), y ^ ((m >> 1) & 1), c ^ (m & 1))
            return pltpu.make_async_remote_copy(
                src_ref=pk, dst_ref=pk_out.at[slot], send_sem=send.at[m - 1], recv_sem=recv.at[m - 1],
                device_id=dev, device_id_type=MESH)

        sent = [pk_copy(m, lin) for m in range(1, N_DEV)]
        for cp in sent:
            cp.start()
        for m in range(1, N_DEV):
            pk_copy(m, lin ^ m).wait_recv()
        for cp in sent:
            cp.wait_send()
        own.wait()

    return pl.pallas_call(
        body, name="exchange_packets", in_specs=[ANY], out_specs=ANY,
        out_shape=jax.ShapeDtypeStruct((N_DEV,) + packet.shape, packet.dtype),
        scratch_shapes=[pltpu.SemaphoreType.DMA((N_DEV - 1,)), pltpu.SemaphoreType.DMA((N_DEV - 1,)),
                        pltpu.SemaphoreType.DMA((1,))],
    )(packet)


def _swap_sibling(parts, name):
    n = len(parts)

    def body(*refs):
        ins, outs = refs[:n], refs[n:2 * n]
        send, recv = refs[2 * n:]
        x, y, c = lax.axis_index("x"), lax.axis_index("y"), lax.axis_index("c")
        cps = [pltpu.make_async_remote_copy(
            src_ref=ins[a], dst_ref=outs[a], send_sem=send.at[a], recv_sem=recv.at[a],
            device_id=(x, y, 1 - c), device_id_type=MESH) for a in range(n)]
        for cp in cps:
            cp.start()
        for cp in cps:
            cp.wait_recv()
        for cp in cps:
            cp.wait_send()

    return pl.pallas_call(
        body, name=name,
        in_specs=[ANY] * n, out_specs=[ANY] * n,
        out_shape=[jax.ShapeDtypeStruct(p.shape, p.dtype) for p in parts],
        scratch_shapes=[pltpu.SemaphoreType.DMA((n,)), pltpu.SemaphoreType.DMA((n,))],
    )(*parts)


BLOCK_ELEMS = 256 * 1024


def _row_tile(R, C):
    tr = max(8, (BLOCK_ELEMS // C) // 8 * 8)
    while R % tr:
        tr -= 8
    return min(tr, R)


def _sum_parts(own, stack, name):
    k = stack.shape[0]
    R, C = stack.shape[1:]
    tr = _row_tile(R, C)

    def body(*refs):
        o_ref = refs[-1]
        acc = refs[0][...]
        for r in refs[1:-1]:
            acc = acc + r[...]
        o_ref[...] = acc

    row = pl.BlockSpec((tr, C), lambda i: (i, 0))
    specs = ([row] if own is not None else []) + [
        pl.BlockSpec((None, tr, C), functools.partial(lambda i, j: (j, i, 0), j=j)) for j in range(k)]
    args = ([own] if own is not None else []) + [stack] * k
    return pl.pallas_call(
        body, name=name, grid=(R // tr,), in_specs=specs, out_specs=row,
        out_shape=jax.ShapeDtypeStruct((R, C), f32), compiler_params=_params(("parallel",)),
    )(*args)


def _adamw(w, m, v, g_parts, name):
    R, C = w.shape
    tr = _row_tile(R, C)
    n_g = len(g_parts)

    def body(w_ref, m_ref, v_ref, *rest):
        g = rest[0][...]
        for r in rest[1:n_g]:
            g = g + r[...]
        g_ref, d_ref, nm_ref, nv_ref = rest[n_g:]
        nm = ADAM_B1 * m_ref[...] + (1.0 - ADAM_B1) * g
        nv = ADAM_B2 * v_ref[...] + (1.0 - ADAM_B2) * jnp.square(g)
        m_hat = nm / (1.0 - ADAM_B1 ** ADAM_STEP)
        v_hat = nv / (1.0 - ADAM_B2 ** ADAM_STEP)
        g_ref[...] = g
        d_ref[...] = -ADAM_LR * (m_hat / (jnp.sqrt(v_hat) + ADAM_EPS) + ADAM_WD * w_ref[...])
        nm_ref[...] = nm
        nv_ref[...] = nv

    row = pl.BlockSpec((tr, C), lambda i: (i, 0))
    return pl.pallas_call(
        body, name=name, grid=(R // tr,), in_specs=[row] * (3 + n_g), out_specs=[row] * 4,
        out_shape=[jax.ShapeDtypeStruct((R, C), f32)] * 4, compiler_params=_params(("parallel",)),
    )(w, m, v, *g_parts)


BIG = ("w_in", "w_mem_kv", "w_sb_out", "w_ssd_out", "w_mem_out", "w_o", "w_up", "w_down")
FIRST = ("w_in", "w_mem_kv")
LATE = ("w_sb_out", "w_ssd_out", "w_mem_out", "w_o", "w_up", "w_down")
REST = BIG[1:]
COL_SHARDED = ("w_in", "w_mem_kv", "w_up")
SMALL = ("norm_mix_pre", "conv_w", "conv_b", "dt_bias", "a_log", "d_skip", "ssd_norm", "norm_mem",
         "norm_mix_post", "norm_mlp_pre", "norm_mlp_post")
WEIGHTS = ("norm_mix_pre", "w_in", "conv_w", "conv_b", "dt_bias", "a_log", "d_skip", "ssd_norm", "norm_mem",
           "w_mem_kv", "w_sb_out", "w_ssd_out", "w_mem_out", "w_o", "norm_mix_post", "norm_mlp_pre", "w_up",
           "w_down", "norm_mlp_post")
PK_ROWS = 184


def _pack(vecs):
    flat = jnp.concatenate([v.reshape(-1) for v in vecs])
    return jnp.pad(flat, (0, PK_ROWS * 128 - flat.shape[0])).reshape(PK_ROWS, 128)


def _unpack(pk, shapes):
    flat = pk.reshape(-1)
    out, off = [], 0
    for s in shapes:
        n = 1
        for d in s:
            n *= d
        out.append(flat[off:off + n].reshape(s))
        off += n
    return out


def _full_from_slabs(name, slabs):
    if name in COL_SHARDED:
        return slabs.transpose(1, 0, 2).reshape(slabs.shape[1], -1)
    return slabs.reshape(-1, slabs.shape[2])


def _slabs_from_full(name, g):
    if name in COL_SHARDED:
        return g.reshape(g.shape[0], N_SHARD, -1).transpose(1, 0, 2)
    return g.reshape(N_SHARD, -1, g.shape[1])


def kernel(x, mem, norm_mix_pre, w_in, conv_w, conv_b, dt_bias, a_log, d_skip, ssd_norm, norm_mem, w_mem_kv, w_sb_out, w_ssd_out, w_mem_out, w_o, norm_mix_post, norm_mlp_pre, w_up, w_down, norm_mlp_post, loss_target, m_norm_mix_pre, m_w_in, m_conv_w, m_conv_b, m_dt_bias, m_a_log, m_d_skip, m_ssd_norm, m_norm_mem, m_w_mem_kv, m_w_sb_out, m_w_ssd_out, m_w_mem_out, m_w_o, m_norm_mix_post, m_norm_mlp_pre, m_w_up, m_w_down, m_norm_mlp_post, v_norm_mix_pre, v_w_in, v_conv_w, v_conv_b, v_dt_bias, v_a_log, v_d_skip, v_ssd_norm, v_norm_mem, v_w_mem_kv, v_w_sb_out, v_w_ssd_out, v_w_mem_out, v_w_o, v_norm_mix_post, v_norm_mlp_pre, v_w_up, v_w_down, v_norm_mlp_post):
    env = dict(locals())
    w = {n: env[n] for n in WEIGHTS}
    mo = {n: env["m_" + n] for n in WEIGHTS}
    vo = {n: env["v_" + n] for n in WEIGHTS}
    shard = 2 * lax.axis_index("x") + lax.axis_index("y")
    flight = {}

    def gathered(started, after, name):
        srcs, lands, tok = _exchange_wait(started, after, "gather", name)
        return [lax.dynamic_update_index_in_dim(l, s, shard, 0) for s, l in zip(srcs, lands)], tok

    flight["first"] = _exchange_start([w[n][0].astype(bf16) for n in FIRST] + [w["conv_w"][0]],
                                      "gather", "gather_first_start")

    def first_weights(u):
        full, tok = gathered(flight["first"], u, "gather_first_wait")
        flight["late"] = _exchange_start([(w[n][0] + tok[0, 0]).astype(bf16) for n in LATE],
                                         "gather", "gather_late_start")
        w_main, w_dt = _to_internal(_full_from_slabs("w_in", full[0]))
        return dict(w_main=w_main + flight["late"][4][0, 0].astype(bf16), w_dt=w_dt,
                    w_mem_kv=_full_from_slabs("w_mem_kv", full[1]),
                    conv_w=full[2].transpose(1, 0, 2).reshape(CONV_K, CONV_DIM))

    def late_weights(after):
        full, _ = gathered(flight["late"], after, "gather_late_wait")
        return {n: _full_from_slabs(n, s) for n, s in zip(LATE, full)}

    def grads_ready(g):
        flight["rest"] = _exchange_start([_slabs_from_full(n, g[n]).astype(bf16) for n in REST],
                                         "scatter", "scatter_rest_start")
        return flight["rest"][4][0:1, :]

    small = {n: w[n] for n in SMALL if n != "conv_w"}
    loss, grad_x, g = _local_step(x[0], mem[0], loss_target[0], first_weights, late_weights, small, grads_ready)
    g["w_in"] = _from_internal(g.pop("w_main"), g.pop("w_dt"))

    flight["w_in"] = _exchange_start([_slabs_from_full("w_in", g["w_in"]).astype(bf16)], "scatter",
                                     "scatter_w_in_start")
    packets = _exchange_packets(_pack([g[n] for n in SMALL] + [loss[:, :1]]))
    out_g, out_d, out_m, out_v = {}, {}, {}, {}

    def finish(names, started, after, tag):
        _, lands, _ = _exchange_wait(started, after, "scatter", "scatter_" + tag + "_wait")
        partial = []
        for n, r in zip(names, lands):
            own = lax.dynamic_index_in_dim(_slabs_from_full(n, g[n]), shard, 0, keepdims=False)
            partial.append(_sum_parts(own, r, name="sum_chips_" + n))
        other = _swap_sibling(partial, "swap_sibling_" + tag)
        for n, p, q in zip(names, partial, other):
            res = _adamw(w[n][0], mo[n][0], vo[n][0], [p, q], name="adamw_" + n)
            out_g[n], out_d[n], out_m[n], out_v[n] = [r[None] for r in res]
        return res[1]

    last = finish(REST, flight["rest"], flight["w_in"][4], "rest")
    finish(("w_in",), flight["w_in"], last, "w_in")
    tot = _sum_parts(None, packets, name="sum_packets")
    shapes = [g[n].shape for n in SMALL] + [(1, 1)]
    sm = dict(zip(SMALL + ("loss",), _unpack(tot, shapes)))
    sm["conv_w"] = lax.dynamic_slice_in_dim(sm["conv_w"], shard * (CONV_DIM // N_SHARD), CONV_DIM // N_SHARD, axis=1)
    own_small = lambda d: _pack([d[n].reshape(sm[n].shape) for n in SMALL])
    res = _adamw(own_small(w), own_small(mo), own_small(vo), [own_small(sm)], name="adamw_small")
    own_shapes = [sm[n].shape for n in SMALL]
    for store, r in zip((out_g, out_d, out_m, out_v), res):
        for n, val in zip(SMALL, _unpack(r, own_shapes)):
            store[n] = val.reshape(w[n].shape)

    outs = [sm["loss"].reshape(()), grad_x[None]]
    for store in (out_g, out_d, out_m, out_v):
        outs += [store[n] for n in WEIGHTS]
    return tuple(outs)
```

```python
import functools

import jax
import jax.numpy as jnp
from jax import lax
from jax.experimental import pallas as pl
from jax.experimental.pallas import tpu as pltpu

f32 = jnp.float32
bf16 = jnp.bfloat16

D = 1024
EPS = 1e-6
SB_HD = 64
SSD_INNER = 2048
SSD_HEADS = 32
SSD_GROUPS = 4
SSD_N = 128
SSD_L = 128
CONV_K = 4
CONV_DIM = 3072
MEM_HEADS = 4
MEM_HD = 256
D_FF = 4096
D_IN = 12320
N_SHARD = 4
N_DEV = 8

P_QKV, P_XBC, P_GATE, P_MEMQ, P_Z, P_DT, P_TOT = 0, 3072, 6144, 9216, 10240, 12288, 12416
R_QKV, R_Z, R_XBC, R_DT, R_MEMQ, R_GATE = (0, 3072), (3072, 5120), (5120, 8192), (8192, 8224), (8224, 9248), (9248, 12320)

ADAM_LR = 0.001
ADAM_B1 = 0.9
ADAM_B2 = 0.999
ADAM_EPS = 1e-08
ADAM_WD = 0.01
ADAM_STEP = 10

VMEM_LIMIT = 56 * 1024 * 1024

NN = (((1,), (0,)), ((), ()))
NT = (((1,), (1,)), ((), ()))
TN = (((0,), (0,)), ((), ()))


def _dot(a, b, dims=NN):
    return lax.dot_general(a, b, dims, preferred_element_type=f32)


def _params(sem=None):
    return pltpu.CompilerParams(dimension_semantics=sem, vmem_limit_bytes=VMEM_LIMIT)


def _sigmoid(x):
    return 1.0 / (1.0 + jnp.exp(-x))


def _split2(x):
    hi = x.astype(bf16)
    lo = (x - hi.astype(f32)).astype(bf16)
    return hi, lo


def _split3(x):
    hi = x.astype(bf16)
    r = x - hi.astype(f32)
    mid = r.astype(bf16)
    lo = (r - mid.astype(f32)).astype(bf16)
    return hi, mid, lo


def _mm(a, b, mode, *, tm, tn, name, out_dtypes=(f32,), epi=None, extras=()):
    M = a.shape[1] if mode == "tn" else a.shape[0]
    N = b.shape[0] if mode == "nt" else b.shape[1]
    tm, tn = min(tm, M), min(tn, N)
    if mode == "nn":
        (M, K), N = a.shape, b.shape[1]
        a_spec = pl.BlockSpec((tm, K), lambda i, j: (i, 0))
        b_spec = pl.BlockSpec((K, tn), lambda i, j: (0, j))
        dims = NN
    elif mode == "nt":
        (M, K), N = a.shape, b.shape[0]
        a_spec = pl.BlockSpec((tm, K), lambda i, j: (i, 0))
        b_spec = pl.BlockSpec((tn, K), lambda i, j: (j, 0))
        dims = NT
    else:
        (K, M), N = a.shape, b.shape[1]
        a_spec = pl.BlockSpec((K, tm), lambda i, j: (0, i))
        b_spec = pl.BlockSpec((K, tn), lambda i, j: (0, j))
        dims = TN
    assert M % tm == 0 and N % tn == 0, (name, M, N, tm, tn)
    n_ex = len(extras)
    o_spec = pl.BlockSpec((tm, tn), lambda i, j: (i, j))

    def body(a_ref, b_ref, *rest):
        acc = _dot(a_ref[...].astype(bf16), b_ref[...].astype(bf16), dims)
        res = (acc,) if epi is None else epi(acc, *[e[...] for e in rest[:n_ex]])
        for o_ref, r in zip(rest[n_ex:], res):
            o_ref[...] = r.astype(o_ref.dtype)

    out = pl.pallas_call(
        body, name=name, grid=(M // tm, N // tn),
        in_specs=[a_spec, b_spec] + [o_spec] * n_ex,
        out_specs=[o_spec] * len(out_dtypes),
        out_shape=[jax.ShapeDtypeStruct((M, N), dt) for dt in out_dtypes],
        compiler_params=_params(("parallel", "parallel")),
    )(a, b, *extras)
    return out[0] if len(out_dtypes) == 1 else out


def _rms_fwd(x, g, *, name, out_dtype, residual=None, tm=512):
    S, C = x.shape
    tm = min(tm, S)
    has_res = residual is not None

    def body(x_ref, g_ref, *rest):
        xv = x_ref[...]
        r = lax.rsqrt(jnp.mean(xv * xv, axis=1, keepdims=True) + EPS)
        y = xv * r * g_ref[...]
        if has_res:
            y = y + rest[0][...]
        rest[-1][...] = y.astype(out_dtype)

    row = pl.BlockSpec((tm, C), lambda i: (i, 0))
    vec = pl.BlockSpec((1, C), lambda i: (0, 0))
    args = (x, g) + ((residual,) if has_res else ())
    return pl.pallas_call(
        body, name=name, grid=(S // tm,),
        in_specs=[row, vec] + ([row] if has_res else []),
        out_specs=row, out_shape=jax.ShapeDtypeStruct((S, C), out_dtype),
        compiler_params=_params(("parallel",)),
    )(*args)


def _rms_bwd(x, dy, g, *, name, out_dtype, add=None, tm=512):
    S, C = x.shape
    tm = min(tm, S)
    has_add = add is not None

    def body(x_ref, dy_ref, g_ref, *rest):
        dx_ref, dg_ref = rest[-2], rest[-1]
        xv = x_ref[...]
        dyv = dy_ref[...].astype(f32)
        r = lax.rsqrt(jnp.mean(xv * xv, axis=1, keepdims=True) + EPS)
        xh = xv * r
        dxh = dyv * g_ref[...]
        dx = r * (dxh - xh * jnp.mean(dxh * xh, axis=1, keepdims=True))
        if has_add:
            dx = dx + rest[0][...]
        dx_ref[...] = dx.astype(out_dtype)

        @pl.when(pl.program_id(0) == 0)
        def _():
            dg_ref[...] = jnp.zeros_like(dg_ref)

        dg_ref[...] += jnp.sum(dyv * xh, axis=0, keepdims=True)

    row = pl.BlockSpec((tm, C), lambda i: (i, 0))
    vec = pl.BlockSpec((1, C), lambda i: (0, 0))
    args = (x, dy, g) + ((add,) if has_add else ())
    return pl.pallas_call(
        body, name=name, grid=(S // tm,),
        in_specs=[row, row, vec] + ([row] if has_add else []),
        out_specs=[row, vec],
        out_shape=[jax.ShapeDtypeStruct((S, C), out_dtype), jax.ShapeDtypeStruct((1, C), f32)],
        compiler_params=_params(("arbitrary",)),
    )(*args)


SB_T = 128
SB_SPENT = -120.0
SB_GROUPS = (4, 2, 1)
SB_GROUPS_BWD = (4, 2, 1)


def _sb_masks():
    lane = lax.broadcasted_iota(jnp.int32, (1, 128), 1)
    m_a = (lane < SB_HD).astype(f32)
    return m_a, 1.0 - m_a


def _sb_logits(z, mask):
    l1p = jnp.log(1.0 + jnp.exp(-jnp.abs(z)))
    lb = jnp.minimum(z, 0.0) - l1p
    lk = lb - z
    if mask is not None:
        lk = jnp.where(mask, lk, 0.0)
    return lb, lk


def _chunks(a, n):
    return [a[:, u * SB_T:(u + 1) * SB_T] for u in range(n)]


def _cat(parts, axis):
    return parts[0] if len(parts) == 1 else jnp.concatenate(parts, axis=axis)


def _chunk_matmul(parts_list, u_mat):
    out = _dot(_cat(parts_list, 0), u_mat)
    return [out[u * SB_T:(u + 1) * SB_T] for u in range(len(parts_list))]


def _chunk_cumsum(lk, n, u_mat):
    hi = lk.astype(bf16)
    lo = (lk - hi.astype(f32)).astype(bf16)
    out = _chunk_matmul(_chunks(hi, n) + _chunks(lo, n), u_mat)
    return [out[u] + out[n + u] for u in range(n)]


def _sb_fwd(proj, S):
    nq = S // SB_T
    n_pairs = D // 128
    scale = SB_HD ** -0.5

    def body(q_ref, k_ref, v_ref, o_ref, t_ref):
        i = pl.program_id(1)
        m_a, m_b = _sb_masks()
        r_i = lax.broadcasted_iota(jnp.int32, (SB_T, SB_T), 0)
        c_i = lax.broadcasted_iota(jnp.int32, (SB_T, SB_T), 1)
        u_mat = (r_i > c_i).astype(bf16)
        causal = c_i < r_i
        q = q_ref[...] * scale
        q_h = ((q * m_a).astype(bf16), (q * m_b).astype(bf16))

        def group(j_lo, n, carry, mask):
            acc, c_a, c_b = carry
            rows = pl.ds(pl.multiple_of(j_lo * SB_T, SB_T), n * SB_T)
            k = k_ref[rows, :].astype(bf16)
            v = v_ref[rows, :]
            zs = [_dot(q_b, k, NT) for q_b in q_h]
            lbk = [_sb_logits(z, mask) for z in zs]
            parts = [_chunk_cumsum(lk, n, u_mat) for _, lk in lbk]
            ws, cs = [], []
            for (lb, lk), part, c in zip(lbk, parts, (c_a, c_b)):
                lb_c, lk_c = _chunks(lb, n), _chunks(lk, n)
                w_c = [None] * n
                for u in reversed(range(n)):
                    w_c[u] = jnp.exp(lb_c[u] + c + part[u])
                    c = c + jnp.sum(lk_c[u], axis=1, keepdims=True)
                w = _cat(w_c, 1)
                if mask is not None:
                    w = jnp.where(mask, w, 0.0)
                ws.append(w.astype(bf16))
                cs.append(c)
            for w, m in zip(ws, (m_a, m_b)):
                acc = acc + _dot(w, (v * m).astype(bf16))
            return acc, cs[0], cs[1]

        zero_c = jnp.zeros((SB_T, 1), f32)
        carry = group(i, 1, (jnp.zeros((SB_T, 128), f32), zero_c, zero_c), causal)

        def spent(cr):
            return (jnp.max(jnp.maximum(cr[1], cr[2])) < SB_SPENT).astype(jnp.int32)

        state = (i, spent(carry), carry)
        for n in SB_GROUPS:
            def step(st, n=n):
                left, _, cr = st
                cr = group(left - n, n, cr, None)
                return left - n, spent(cr), cr

            state = lax.while_loop(lambda st, n=n: (st[0] >= n) & (st[1] == 0), step, state)
        left, _, carry = state
        o_ref[...] = carry[0]
        lane = lax.broadcasted_iota(jnp.int32, (1, 128), 1)
        t_ref[...] = (jnp.where(lane == 0, carry[1], 0.0) + jnp.where(lane == SB_HD, carry[2], 0.0)
                      + jnp.where(lane == 1, left.astype(f32), 0.0))

    qs = pl.BlockSpec((SB_T, 128), lambda h, i: (i, h))
    return pl.pallas_call(
        body, name="sb_fwd", grid=(n_pairs, nq),
        in_specs=[qs,
                  pl.BlockSpec((S, 128), lambda h, i: (0, n_pairs + h)),
                  pl.BlockSpec((S, 128), lambda h, i: (0, 2 * n_pairs + h))],
        out_specs=[qs, qs], out_shape=[jax.ShapeDtypeStruct((S, D), f32)] * 2,
        compiler_params=_params(("parallel", "arbitrary")),
    )(proj, proj, proj)


def _sb_bwd(proj, tot_lk, do, S):
    nq = S // SB_T
    n_pairs = D // 128
    scale = SB_HD ** -0.5

    def body(q_ref, k_ref, v_ref, t_ref, do_ref, dq_ref, dk_ref, dv_ref, dk_acc, dv_acc):
        i = pl.program_id(1)
        m_a, m_b = _sb_masks()
        r_i = lax.broadcasted_iota(jnp.int32, (SB_T, SB_T), 0)
        c_i = lax.broadcasted_iota(jnp.int32, (SB_T, SB_T), 1)
        u_inc = (r_i <= c_i).astype(bf16)
        u_exc = (r_i < c_i).astype(bf16)
        causal = c_i < r_i

        @pl.when(i == 0)
        def _():
            dk_acc[...] = jnp.zeros_like(dk_acc)
            dv_acc[...] = jnp.zeros_like(dv_acc)

        q = q_ref[...] * scale
        dov = do_ref[...]
        tv = t_ref[...]
        lane = lax.broadcasted_iota(jnp.int32, (1, 128), 1)
        heads = []
        for m, first in ((m_a, 0), (m_b, SB_HD)):
            tot = jnp.sum(jnp.where(lane == first, tv, 0.0), axis=1, keepdims=True)
            heads.append(((q * m).astype(bf16), (dov * m).astype(bf16), tot, m))
        lowest = jnp.clip(jnp.max(jnp.where(lane == 1, tv, 0.0)).astype(jnp.int32), 0, i)

        def group(j_lo, n, carry, mask):
            dq_acc, cp_a, cp_b, ce_a, ce_b = carry
            rows = pl.ds(pl.multiple_of(j_lo * SB_T, SB_T), n * SB_T)
            k_f = k_ref[rows, :]
            k = k_f.astype(bf16)
            v = v_ref[rows, :].astype(bf16)
            zs = [_dot(h[0], k, NT) for h in heads]
            dws = [_dot(h[1], v, NT) for h in heads]
            lbk = [_sb_logits(z, mask) for z in zs]
            parts = [_chunk_cumsum(lk, n, u_inc) for _, lk in lbk]
            ws, es, cps = [], [], []
            for (lb, lk), part, dw, h, cp in zip(lbk, parts, dws, heads, (cp_a, cp_b)):
                lb_c, lk_c = _chunks(lb, n), _chunks(lk, n)
                w_c = []
                for u in range(n):
                    w_c.append(jnp.exp(lb_c[u] + (h[2] - cp) - part[u]))
                    cp = cp + jnp.sum(lk_c[u], axis=1, keepdims=True)
                w = _cat(w_c, 1)
                if mask is not None:
                    w = jnp.where(mask, w, 0.0)
                ws.append(w)
                es.append(dw * w)
                cps.append(cp)
            e_parts = [_chunk_matmul(_chunks(e.astype(bf16), n), u_exc) for e in es]
            dzs, ces = [], []
            for (lb, _), e, e_part, ce in zip(lbk, es, e_parts, (ce_a, ce_b)):
                e_c = _chunks(e, n)
                big_c = []
                for u in range(n):
                    big_c.append(ce + e_part[u])
                    ce = ce + jnp.sum(e_c[u], axis=1, keepdims=True)
                sig = jnp.exp(lb)
                dz = e * (1.0 - sig) - _cat(big_c, 1) * sig
                if mask is not None:
                    dz = jnp.where(mask, dz, 0.0)
                dzs.append(dz.astype(bf16))
                ces.append(ce)
            dk_t = jnp.zeros((n * SB_T, 128), f32)
            dv_t = jnp.zeros((n * SB_T, 128), f32)
            for dz_b, w, h in zip(dzs, ws, heads):
                dq_acc = dq_acc + _dot(dz_b, (k_f * h[3]).astype(bf16))
                dk_t = dk_t + _dot(dz_b, h[0], TN)
                dv_t = dv_t + _dot(w.astype(bf16), h[1], TN)
            dk_acc[rows, :] += dk_t
            dv_acc[rows, :] += dv_t
            return dq_acc, cps[0], cps[1], ces[0], ces[1]

        zc = jnp.zeros((SB_T, 1), f32)
        carry = (jnp.zeros((SB_T, 128), f32), zc, zc, zc, zc)
        done = lowest
        for n in SB_GROUPS_BWD:
            trips = (i - done) // n
            carry = lax.fori_loop(
                0, trips, functools.partial(lambda gi, cr, n, done: group(done + gi * n, n, cr, None), n=n, done=done),
                carry)
            done = done + trips * n
        carry = group(i, 1, carry, causal)
        dq_ref[...] = (carry[0] * scale).astype(bf16)

        @pl.when(i == nq - 1)
        def _():
            dk_ref[...] = dk_acc[...].astype(bf16)
            dv_ref[...] = dv_acc[...].astype(bf16)

    qs = pl.BlockSpec((SB_T, 128), lambda h, i: (i, h))
    full = pl.BlockSpec((S, 128), lambda h, i: (0, h))
    dq, dk, dv = pl.pallas_call(
        body, name="sb_bwd", grid=(n_pairs, nq),
        in_specs=[qs,
                  pl.BlockSpec((S, 128), lambda h, i: (0, n_pairs + h)),
                  pl.BlockSpec((S, 128), lambda h, i: (0, 2 * n_pairs + h)),
                  qs, qs],
        out_specs=[qs, full, full],
        out_shape=[jax.ShapeDtypeStruct((S, D), bf16)] * 3,
        scratch_shapes=[pltpu.VMEM((S, 128), f32), pltpu.VMEM((S, 128), f32)],
        compiler_params=_params(("parallel", "arbitrary")),
    )(proj, proj, proj, tot_lk, do)
    return dq, dk, dv


CONV_CB = 256
HALO = 8


def _conv_fwd(proj, conv_w, conv_b, S):
    tr = min(512, S)

    def body(x_ref, w_ref, b_ref, xc_ref, xbc_ref):
        w = w_ref[...]
        for t in range(S // tr):
            cur = x_ref[t * tr:(t + 1) * tr, :]
            halo = x_ref[t * tr - HALO:t * tr, :] if t else jnp.zeros((HALO, CONV_CB), f32)
            win = jnp.concatenate([halo, cur], axis=0)
            acc = b_ref[...] + w[CONV_K - 1:CONV_K, :] * cur
            for k in range(CONV_K - 1):
                acc = acc + w[k:k + 1, :] * pltpu.roll(win, CONV_K - 1 - k, 0)[HALO:, :]
            xc_ref[t * tr:(t + 1) * tr, :] = acc
            xbc_ref[t * tr:(t + 1) * tr, :] = acc * _sigmoid(acc)

    col = pl.BlockSpec((S, CONV_CB), lambda c: (0, c))
    return pl.pallas_call(
        body, name="conv_fwd", grid=(CONV_DIM // CONV_CB,),
        in_specs=[pl.BlockSpec((S, CONV_CB), lambda c: (0, P_XBC // CONV_CB + c)),
                  pl.BlockSpec((CONV_K, CONV_CB), lambda c: (0, c)),
                  pl.BlockSpec((1, CONV_CB), lambda c: (0, c))],
        out_specs=[col, col], out_shape=[jax.ShapeDtypeStruct((S, CONV_DIM), f32)] * 2,
        compiler_params=_params(("parallel",)),
    )(proj, conv_w, conv_b)


def _conv_bwd(proj, xc, dxbc, conv_w, S):
    tr = min(512, S)

    def body(x_ref, xc_ref, dy_ref, w_ref, dx_ref, dw_ref, db_ref, dxc_s):
        w = w_ref[...]
        xcv = xc_ref[...]
        sg = _sigmoid(xcv)
        dxc_s[0:S, :] = dy_ref[...] * (sg * (1.0 + xcv * (1.0 - sg)))
        dxc_s[S:S + HALO, :] = jnp.zeros((HALO, CONV_CB), f32)
        dws = [jnp.zeros((1, CONV_CB), f32) for _ in range(CONV_K)]
        db = jnp.zeros((1, CONV_CB), f32)
        for t in range(S // tr):
            cur = x_ref[t * tr:(t + 1) * tr, :]
            halo = x_ref[t * tr - HALO:t * tr, :] if t else jnp.zeros((HALO, CONV_CB), f32)
            win = jnp.concatenate([halo, cur], axis=0)
            dwin = dxc_s[t * tr:(t + 1) * tr + HALO, :]
            dcur = dwin[0:tr, :]
            db = db + jnp.sum(dcur, axis=0, keepdims=True)
            dws[CONV_K - 1] = dws[CONV_K - 1] + jnp.sum(dcur * cur, axis=0, keepdims=True)
            dx = w[CONV_K - 1:CONV_K, :] * dcur
            for k in range(CONV_K - 1):
                sh = CONV_K - 1 - k
                dws[k] = dws[k] + jnp.sum(dcur * pltpu.roll(win, sh, 0)[HALO:, :], axis=0, keepdims=True)
                dx = dx + w[k:k + 1, :] * pltpu.roll(dwin, tr + HALO - sh, 0)[0:tr, :]
            dx_ref[t * tr:(t + 1) * tr, :] = dx.astype(bf16)
        dw_ref[...] = jnp.concatenate(dws + [jnp.zeros((8 - CONV_K, CONV_CB), f32)], axis=0)
        db_ref[...] = db

    col = pl.BlockSpec((S, CONV_CB), lambda c: (0, c))
    return pl.pallas_call(
        body, name="conv_bwd", grid=(CONV_DIM // CONV_CB,),
        in_specs=[pl.BlockSpec((S, CONV_CB), lambda c: (0, P_XBC // CONV_CB + c)), col, col,
                  pl.BlockSpec((CONV_K, CONV_CB), lambda c: (0, c))],
        out_specs=[col, pl.BlockSpec((8, CONV_CB), lambda c: (0, c)), pl.BlockSpec((1, CONV_CB), lambda c: (0, c))],
        out_shape=[jax.ShapeDtypeStruct((S, CONV_DIM), bf16), jax.ShapeDtypeStruct((8, CONV_DIM), f32),
                   jax.ShapeDtypeStruct((1, CONV_DIM), f32)],
        scratch_shapes=[pltpu.VMEM((S + HALO, CONV_CB), f32)],
        compiler_params=_params(("parallel",)),
    )(proj, xc, dxbc, conv_w)


N_PAIR = SSD_HEADS // 2
NEG = -1e30


def _softplus(x):
    return jnp.maximum(x, 0.0) + jnp.log(1.0 + jnp.exp(-jnp.abs(x)))


def _ssd_common(dtr, dtb, alog):
    L = SSD_L
    r_i = lax.broadcasted_iota(jnp.int32, (L, L), 0)
    c_i = lax.broadcasted_iota(jnp.int32, (L, L), 1)
    dt = _softplus(dtr + dtb)
    a = -jnp.exp(alog)
    da = dt * a
    lower = (r_i >= c_i).astype(bf16)
    upper = (r_i <= c_i).astype(bf16)
    parts = _split3(da)
    a_cs = sum(_dot(lower, p) for p in parts)
    a_cs_t = sum(_dot(p, upper, TN) for p in parts)
    return dt, a, a_cs, a_cs_t, r_i >= c_i


def _pair_vec(lane, v, h):
    return jnp.where(lane < SB_HD, v[:, h:h + 1], v[:, h + 1:h + 2])


def _decay_mat(a_cs, a_cs_t, h, tril):
    return jnp.exp(jnp.where(tril, a_cs[:, h:h + 1] - a_cs_t[h:h + 1, :], NEG))


def _ssd_fwd(xbc, proj, pdt, dt_bias_p, a_log_p, d_skip_c, ssd_norm, S):
    L = SSD_L
    nc = S // L

    def body(xbc_ref, dt_ref, z_ref, dtb_ref, alog_ref, dsk_ref, gn_ref, y_ref, yn_ref, hp_ref, state):
        c = pl.program_id(0)

        @pl.when(c == 0)
        def _():
            state[...] = jnp.zeros_like(state)

        hp_ref[0] = state[...]
        lane = lax.broadcasted_iota(jnp.int32, (1, 128), 1)
        row128 = lax.broadcasted_iota(jnp.int32, (128, 1), 0)
        m_a, m_b = _sb_masks()
        dt, a, a_cs, a_cs_t, tril = _ssd_common(dt_ref[...], dtb_ref[...], alog_ref[...])
        a_last = a_cs[L - 1:L, :]
        for g in range(SSD_GROUPS):
            b_g = xbc_ref[:, SSD_INNER + g * SSD_N:SSD_INNER + (g + 1) * SSD_N].astype(bf16)
            c_g = xbc_ref[:, SSD_INNER + (SSD_GROUPS + g) * SSD_N:SSD_INNER + (SSD_GROUPS + g + 1) * SSD_N].astype(bf16)
            cb = _dot(c_g, b_g, NT)
            for pr in range(4):
                h = 8 * g + 2 * pr
                pi = h // 2
                cols = slice(pi * 128, (pi + 1) * 128)
                xs = xbc_ref[:, cols]
                x = xs * _pair_vec(lane, dt, h)
                acs = _pair_vec(lane, a_cs, h)
                al = _pair_vec(lane, a_last, h)
                w_a = (cb * _decay_mat(a_cs, a_cs_t, h, tril)).astype(bf16)
                w_b = (cb * _decay_mat(a_cs, a_cs_t, h + 1, tril)).astype(bf16)
                yd = _dot(w_a, (x * m_a).astype(bf16)) + _dot(w_b, (x * m_b).astype(bf16))
                hp = state[pi]
                yo = _dot(c_g, hp.astype(bf16), NT) * jnp.exp(acs)
                y_ref[:, cols] = yd + yo + dsk_ref[:, cols] * xs
                dec = jnp.exp(jnp.where(row128 < SB_HD, a_last[:, h:h + 1], a_last[:, h + 1:h + 2]))
                state[pi] = hp * dec + _dot((x * jnp.exp(al - acs)).astype(bf16), b_g, TN)
        zz = z_ref[...]
        y2 = y_ref[...] * (zz * _sigmoid(zz))
        gw = SSD_INNER // SSD_GROUPS
        for g in range(SSD_GROUPS):
            yg = y2[:, g * gw:(g + 1) * gw]
            rg = lax.rsqrt(jnp.mean(yg * yg, axis=1, keepdims=True) + EPS)
            yn_ref[:, g * gw:(g + 1) * gw] = (yg * rg * gn_ref[:, g * gw:(g + 1) * gw]).astype(bf16)

    vec128 = pl.BlockSpec((1, 128), lambda c: (0, 0))
    vecin = pl.BlockSpec((1, SSD_INNER), lambda c: (0, 0))
    rows = pl.BlockSpec((L, SSD_INNER), lambda c: (c, 0))
    return pl.pallas_call(
        body, name="ssd_fwd", grid=(nc,),
        in_specs=[pl.BlockSpec((L, CONV_DIM), lambda c: (c, 0)),
                  pl.BlockSpec((L, 128), lambda c: (c, 0)),
                  pl.BlockSpec((L, SSD_INNER), lambda c: (c, P_Z // SSD_INNER)),
                  vec128, vec128, vecin, vecin],
        out_specs=[rows, rows, pl.BlockSpec((1, N_PAIR, 128, SSD_N), lambda c: (c, 0, 0, 0))],
        out_shape=[jax.ShapeDtypeStruct((S, SSD_INNER), f32), jax.ShapeDtypeStruct((S, SSD_INNER), bf16),
                   jax.ShapeDtypeStruct((nc, N_PAIR, 128, SSD_N), f32)],
        scratch_shapes=[pltpu.VMEM((N_PAIR, 128, SSD_N), f32)],
        compiler_params=_params(("arbitrary",)),
    )(xbc, pdt, proj, dt_bias_p, a_log_p, d_skip_c, ssd_norm)


def _sum_all(v):
    return jnp.sum(jnp.sum(v, axis=1, keepdims=True), axis=0, keepdims=True)


def _ssd_bwd(dyn, y, xbc, proj, pdt, hprev, dt_bias_p, a_log_p, d_skip_c, ssd_norm, S):
    L = SSD_L
    nc = S // L

    def body(dyn_ref, y_ref, xbc_ref, dt_ref, z_ref, hp_ref, dtb_ref, alog_ref, dsk_ref, gn_ref,
             dz_ref, dxbc_ref, ddt_ref, dgn_ref, dsk_out, dalog_ref, ddtb_ref, dstate, dy_s):
        c = pl.program_id(0)

        @pl.when(c == 0)
        def _():
            dstate[...] = jnp.zeros_like(dstate)
            dgn_ref[...] = jnp.zeros_like(dgn_ref)
            dsk_out[...] = jnp.zeros_like(dsk_out)
            dalog_ref[...] = jnp.zeros_like(dalog_ref)
            ddtb_ref[...] = jnp.zeros_like(ddtb_ref)

        lane = lax.broadcasted_iota(jnp.int32, (1, 128), 1)
        row128 = lax.broadcasted_iota(jnp.int32, (128, 1), 0)
        rowl = lax.broadcasted_iota(jnp.int32, (L, 1), 0)
        m_a, m_b = _sb_masks()
        dtr = dt_ref[...]
        dt, a, a_cs, a_cs_t, tril = _ssd_common(dtr, dtb_ref[...], alog_ref[...])
        a_last = a_cs[L - 1:L, :]

        zz = z_ref[...]
        sg = _sigmoid(zz)
        silu = zz * sg
        yv = y_ref[...]
        y2 = yv * silu
        gw = SSD_INNER // SSD_GROUPS
        for g in range(SSD_GROUPS):
            sl = slice(g * gw, (g + 1) * gw)
            yg = y2[:, sl]
            rg = lax.rsqrt(jnp.mean(yg * yg, axis=1, keepdims=True) + EPS)
            yh = yg * rg
            dyn_g = dyn_ref[:, sl]
            dgn_ref[:, sl] += jnp.sum(dyn_g * yh, axis=0, keepdims=True)
            dyh = dyn_g * gn_ref[:, sl]
            dy2 = rg * (dyh - yh * jnp.mean(dyh * yh, axis=1, keepdims=True))
            dy_s[:, sl] = dy2 * silu[:, sl]
            dz_ref[:, sl] = (dy2 * yv[:, sl] * (sg[:, sl] * (1.0 + zz[:, sl] * (1.0 - sg[:, sl])))).astype(bf16)

        d_acs = jnp.zeros((L, 128), f32)
        ddt_x = jnp.zeros((L, 128), f32)
        dsk_acc = jnp.zeros((1, 128), f32)
        for g in range(SSD_GROUPS):
            bsl = slice(SSD_INNER + g * SSD_N, SSD_INNER + (g + 1) * SSD_N)
            csl = slice(SSD_INNER + (SSD_GROUPS + g) * SSD_N, SSD_INNER + (SSD_GROUPS + g + 1) * SSD_N)
            b_g = xbc_ref[:, bsl].astype(bf16)
            c_g = xbc_ref[:, csl].astype(bf16)
            cb = _dot(c_g, b_g, NT)
            dcb = jnp.zeros((L, L), f32)
            dc_g = jnp.zeros((L, SSD_N), f32)
            db_g = jnp.zeros((L, SSD_N), f32)
            for pr in range(4):
                h = 8 * g + 2 * pr
                pi = h // 2
                cols = slice(pi * 128, (pi + 1) * 128)
                xs = xbc_ref[:, cols]
                dt_p = _pair_vec(lane, dt, h)
                x = xs * dt_p
                acs = _pair_vec(lane, a_cs, h)
                al = _pair_vec(lane, a_last, h)
                e_a = jnp.exp(acs)
                dte = jnp.exp(al - acs)
                m_mat_a = _decay_mat(a_cs, a_cs_t, h, tril)
                m_mat_b = _decay_mat(a_cs, a_cs_t, h + 1, tril)
                dyp = dy_s[:, cols]
                dsk = dsk_ref[:, cols]
                d_hn = dstate[pi]
                hp = hp_ref[0, pi]
                dy_a = (dyp * m_a).astype(bf16)
                dy_b = (dyp * m_b).astype(bf16)
                x_b = x.astype(bf16)
                gm_a = _dot(dy_a, x_b, NT) * m_mat_a
                gm_b = _dot(dy_b, x_b, NT) * m_mat_b
                dcb = dcb + gm_a + gm_b
                dx_d = _dot((cb * m_mat_a).astype(bf16), dy_a, TN) + _dot((cb * m_mat_b).astype(bf16), dy_b, TN)
                dx_s = _dot(b_g, d_hn.astype(bf16), NT) * dte
                dx = dx_d + dx_s
                dxbc_ref[:, cols] = dx * dt_p + dsk * dyp
                xdxs = x * dx_s
                u = dyp * (_dot(c_g, hp.astype(bf16), NT) * e_a) - xdxs
                hh = d_hn * hp
                dec = jnp.exp(jnp.where(row128 < SB_HD, a_last[:, h:h + 1], a_last[:, h + 1:h + 2]))
                for hd, m, gm in ((h, m_a, gm_a), (h + 1, m_b, gm_b)):
                    half = slice(0, SB_HD) if hd == h else slice(SB_HD, 128)
                    last = _sum_all(xdxs * m) + jnp.exp(a_last[:, hd:hd + 1]) * _sum_all(hh[half, :])
                    qm = gm * cb
                    col = jnp.sum(qm - qm.T, axis=1, keepdims=True) + jnp.sum(u * m, axis=1, keepdims=True)
                    col = col + jnp.where(rowl == L - 1, last, 0.0)
                    d_acs = jnp.where(lane == hd, col, d_acs)
                    ddt_x = jnp.where(lane == hd, jnp.sum(dx * xs * m, axis=1, keepdims=True), ddt_x)
                    dsk_acc = jnp.where(lane == hd, _sum_all(dyp * xs * m), dsk_acc)
                dye = (dyp * e_a).astype(bf16)
                dc_g = dc_g + _dot(dye, hp.astype(bf16))
                db_g = db_g + _dot((x * dte).astype(bf16), d_hn.astype(bf16))
                dstate[pi] = dec * d_hn + _dot(dye, c_g, TN)
            dcb_b = dcb.astype(bf16)
            dxbc_ref[:, csl] = dc_g + _dot(dcb_b, b_g)
            dxbc_ref[:, bsl] = db_g + _dot(dcb_b, c_g, TN)

        r_i = lax.broadcasted_iota(jnp.int32, (L, L), 0)
        c_i = lax.broadcasted_iota(jnp.int32, (L, L), 1)
        rev = (r_i <= c_i).astype(bf16)
        dda = sum(_dot(rev, p) for p in _split3(d_acs))
        ddt = ddt_x + dda * a
        dalog_ref[...] += jnp.sum(dda * dt, axis=0, keepdims=True) * a
        ddtr = jnp.where(lane < SSD_HEADS, ddt * _sigmoid(dtr + dtb_ref[...]), 0.0)
        ddt_ref[...] = ddtr.astype(bf16)
        ddtb_ref[...] += jnp.sum(ddtr, axis=0, keepdims=True)
        dsk_out[...] += dsk_acc

    rv = lambda c: nc - 1 - c
    vec128 = pl.BlockSpec((1, 128), lambda c: (0, 0))
    vecin = pl.BlockSpec((1, SSD_INNER), lambda c: (0, 0))
    rows = pl.BlockSpec((L, SSD_INNER), lambda c: (rv(c), 0))
    return pl.pallas_call(
        body, name="ssd_bwd", grid=(nc,),
        in_specs=[rows, rows,
                  pl.BlockSpec((L, CONV_DIM), lambda c: (rv(c), 0)),
                  pl.BlockSpec((L, 128), lambda c: (rv(c), 0)),
                  pl.BlockSpec((L, SSD_INNER), lambda c: (rv(c), P_Z // SSD_INNER)),
                  pl.BlockSpec((1, N_PAIR, 128, SSD_N), lambda c: (rv(c), 0, 0, 0)),
                  vec128, vec128, vecin, vecin],
        out_specs=[rows, pl.BlockSpec((L, CONV_DIM), lambda c: (rv(c), 0)),
                   pl.BlockSpec((L, 128), lambda c: (rv(c), 0)), vecin, vec128, vec128, vec128],
        out_shape=[jax.ShapeDtypeStruct((S, SSD_INNER), bf16), jax.ShapeDtypeStruct((S, CONV_DIM), f32),
                   jax.ShapeDtypeStruct((S, 128), bf16), jax.ShapeDtypeStruct((1, SSD_INNER), f32),
                   jax.ShapeDtypeStruct((1, 128), f32), jax.ShapeDtypeStruct((1, 128), f32),
                   jax.ShapeDtypeStruct((1, 128), f32)],
        scratch_shapes=[pltpu.VMEM((N_PAIR, 128, SSD_N), f32), pltpu.VMEM((L, SSD_INNER), f32)],
        compiler_params=_params(("arbitrary",)),
    )(dyn, y, xbc, pdt, proj, hprev, dt_bias_p, a_log_p, d_skip_c, ssd_norm)


MEM_W = MEM_HEADS * MEM_HD


def _mem_probs(q, k):
    s = _dot(q, k, NT) * (MEM_HD ** -0.5)
    s = s - jnp.max(s, axis=1, keepdims=True)
    p = jnp.exp(s)
    return p / jnp.sum(p, axis=1, keepdims=True)


def _mem_fwd(proj, kv, S, tm=512):
    tm = min(tm, S)
    M = kv.shape[0]

    def body(q_ref, kv_ref, o_ref):
        for h in range(MEM_HEADS):
            sl = slice(h * MEM_HD, (h + 1) * MEM_HD)
            vsl = slice(MEM_W + h * MEM_HD, MEM_W + (h + 1) * MEM_HD)
            p = _mem_probs(q_ref[:, sl].astype(bf16), kv_ref[:, sl].astype(bf16))
            o_ref[:, sl] = _dot(p.astype(bf16), kv_ref[:, vsl].astype(bf16)).astype(bf16)

    return pl.pallas_call(
        body, name="mem_fwd", grid=(S // tm,),
        in_specs=[pl.BlockSpec((tm, MEM_W), lambda i: (i, P_MEMQ // MEM_W)),
                  pl.BlockSpec((M, 2 * MEM_W), lambda i: (0, 0))],
        out_specs=pl.BlockSpec((tm, MEM_W), lambda i: (i, 0)),
        out_shape=jax.ShapeDtypeStruct((S, MEM_W), bf16),
        compiler_params=_params(("parallel",)),
    )(proj, kv)


def _mem_bwd(proj, kv, dy, S, tm=512):
    tm = min(tm, S)
    M = kv.shape[0]
    scale = MEM_HD ** -0.5

    def body(q_ref, kv_ref, dy_ref, dq_ref, dkv_ref):
        @pl.when(pl.program_id(0) == 0)
        def _():
            dkv_ref[...] = jnp.zeros_like(dkv_ref)

        for h in range(MEM_HEADS):
            sl = slice(h * MEM_HD, (h + 1) * MEM_HD)
            vsl = slice(MEM_W + h * MEM_HD, MEM_W + (h + 1) * MEM_HD)
            q = q_ref[:, sl].astype(bf16)
            k = kv_ref[:, sl].astype(bf16)
            v = kv_ref[:, vsl].astype(bf16)
            dyh = dy_ref[:, sl].astype(bf16)
            p = _mem_probs(q, k)
            dp = _dot(dyh, v, NT)
            ds = (p * (dp - jnp.sum(dp * p, axis=1, keepdims=True)) * scale).astype(bf16)
            dq_ref[:, sl] = _dot(ds, k).astype(bf16)
            dkv_ref[:, sl] += _dot(ds, q, TN)
            dkv_ref[:, vsl] += _dot(p.astype(bf16), dyh, TN)

    return pl.pallas_call(
        body, name="mem_bwd", grid=(S // tm,),
        in_specs=[pl.BlockSpec((tm, MEM_W), lambda i: (i, P_MEMQ // MEM_W)),
                  pl.BlockSpec((M, 2 * MEM_W), lambda i: (0, 0)),
                  pl.BlockSpec((tm, MEM_W), lambda i: (i, 0))],
        out_specs=[pl.BlockSpec((tm, MEM_W), lambda i: (i, 0)), pl.BlockSpec((M, 2 * MEM_W), lambda i: (0, 0))],
        out_shape=[jax.ShapeDtypeStruct((S, MEM_W), bf16), jax.ShapeDtypeStruct((M, 2 * MEM_W), f32)],
        compiler_params=_params(("arbitrary",)),
    )(proj, kv, dy)


def _merge_fwd(proj, t0, t1, t2, S, tm=512):
    tm = min(tm, S)

    def body(g_ref, t0_ref, t1_ref, t2_ref, o_ref):
        acc = jnp.zeros((tm, D), f32)
        for b, t_ref in enumerate((t0_ref, t1_ref, t2_ref)):
            acc = acc + _sigmoid(g_ref[:, b * D:(b + 1) * D]) * t_ref[...]
        o_ref[...] = acc.astype(bf16)

    row = pl.BlockSpec((tm, D), lambda i: (i, 0))
    return pl.pallas_call(
        body, name="merge_fwd", grid=(S // tm,),
        in_specs=[pl.BlockSpec((tm, 3 * D), lambda i: (i, P_GATE // (3 * D))), row, row, row],
        out_specs=row, out_shape=jax.ShapeDtypeStruct((S, D), bf16),
        compiler_params=_params(("parallel",)),
    )(proj, t0, t1, t2)


def _merge_bwd(proj, t0, t1, t2, dm, S, tm=512):
    tm = min(tm, S)

    def body(g_ref, t0_ref, t1_ref, t2_ref, dm_ref, d0_ref, d1_ref, d2_ref, dg_ref):
        dmv = dm_ref[...]
        for b, (t_ref, d_ref) in enumerate(((t0_ref, d0_ref), (t1_ref, d1_ref), (t2_ref, d2_ref))):
            sg = _sigmoid(g_ref[:, b * D:(b + 1) * D])
            d_ref[...] = (dmv * sg).astype(bf16)
            dg_ref[:, b * D:(b + 1) * D] = (dmv * t_ref[...] * sg * (1.0 - sg)).astype(bf16)

    row = pl.BlockSpec((tm, D), lambda i: (i, 0))
    return pl.pallas_call(
        body, name="merge_bwd", grid=(S // tm,),
        in_specs=[pl.BlockSpec((tm, 3 * D), lambda i: (i, P_GATE // (3 * D))), row, row, row, row],
        out_specs=[row, row, row, pl.BlockSpec((tm, 3 * D), lambda i: (i, 0))],
        out_shape=[jax.ShapeDtypeStruct((S, D), bf16)] * 3 + [jax.ShapeDtypeStruct((S, 3 * D), bf16)],
        compiler_params=_params(("parallel",)),
    )(proj, t0, t1, t2, dm)


def _loss_head(ff, g, h1, target, S, tm=512):
    tm = min(tm, S)

    def body(ff_ref, g_ref, h1_ref, t_ref, dh_ref, loss_ref):
        xv = ff_ref[...]
        r = lax.rsqrt(jnp.mean(xv * xv, axis=1, keepdims=True) + EPS)
        err = h1_ref[...] + xv * r * g_ref[...] - t_ref[...]
        dh_ref[...] = err * (1.0 / D)

        @pl.when(pl.program_id(0) == 0)
        def _():
            loss_ref[...] = jnp.zeros_like(loss_ref)

        loss_ref[...] += 0.5 * _sum_all(jnp.mean(err * err, axis=1, keepdims=True)) * jnp.ones((1, 128), f32)

    row = pl.BlockSpec((tm, D), lambda i: (i, 0))
    return pl.pallas_call(
        body, name="loss_head", grid=(S // tm,),
        in_specs=[row, pl.BlockSpec((1, D), lambda i: (0, 0)), row, row],
        out_specs=[row, pl.BlockSpec((1, 128), lambda i: (0, 0))],
        out_shape=[jax.ShapeDtypeStruct((S, D), f32), jax.ShapeDtypeStruct((1, 128), f32)],
        compiler_params=_params(("arbitrary",)),
    )(ff, g, h1, target)


def _local_step(x, mem, target, first_weights, late_weights, small, grads_ready):
    S = x.shape[0]
    M = mem.shape[0]
    pad = lambda v: jnp.pad(v, ((0, 0), (0, 128 - SSD_HEADS)))
    dtb_p, alog_p = pad(small["dt_bias"]), pad(small["a_log"])
    dsk_c = jnp.repeat(small["d_skip"], SB_HD, axis=1)

    u = _rms_fwd(x, small["norm_mix_pre"], name="norm_pre", out_dtype=bf16)
    wts = dict(first_weights(u))
    small = dict(small, conv_w=wts.pop("conv_w"))
    proj = _mm(u, wts["w_main"], "nn", tm=1024, tn=1024, name="in_proj")
    pdt = _mm(u, wts["w_dt"], "nn", tm=1024, tn=128, name="in_proj_dt")
    y_sb, tot_lk = _sb_fwd(proj, S)
    xc, xbc = _conv_fwd(proj, small["conv_w"], small["conv_b"], S)
    y_ssd, yn, hprev = _ssd_fwd(xbc, proj, pdt, dtb_p, alog_p, dsk_c, small["ssd_norm"], S)
    mn = _rms_fwd(mem, small["norm_mem"], name="norm_mem", out_dtype=bf16, tm=min(512, M))
    kv = _mm(mn, wts["w_mem_kv"], "nn", tm=M, tn=1024, name="mem_kv")
    y_mem = _mem_fwd(proj, kv, S)
    wts.update(late_weights(y_mem))
    t0 = _mm(y_sb, wts["w_sb_out"], "nn", tm=1024, tn=1024, name="sb_out")
    t1 = _mm(yn, wts["w_ssd_out"], "nn", tm=1024, tn=1024, name="ssd_out")
    t2 = _mm(y_mem, wts["w_mem_out"], "nn", tm=1024, tn=1024, name="mem_out")
    merged = _merge_fwd(proj, t0, t1, t2, S)
    mix = _mm(merged, wts["w_o"], "nn", tm=1024, tn=1024, name="w_o")
    h1 = _rms_fwd(mix, small["norm_mix_post"], name="norm_mix_post", out_dtype=f32, residual=x)
    u2 = _rms_fwd(h1, small["norm_mlp_pre"], name="norm_mlp_pre", out_dtype=bf16)
    a_up, hrelu = _mm(u2, wts["w_up"], "nn", tm=1024, tn=1024, name="mlp_up", out_dtypes=(f32, bf16),
                      epi=lambda acc: (acc, jnp.square(jnp.maximum(acc, 0.0))))
    ff = _mm(hrelu, wts["w_down"], "nn", tm=1024, tn=1024, name="mlp_down")
    dh2, loss = _loss_head(ff, small["norm_mlp_post"], h1, target, S)

    g = {}
    dff, g["norm_mlp_post"] = _rms_bwd(ff, dh2, small["norm_mlp_post"], name="norm_mlp_post_bwd", out_dtype=bf16)
    da = _mm(dff, wts["w_down"], "nt", tm=1024, tn=1024, name="mlp_down_dx", out_dtypes=(bf16,),
             epi=lambda acc, a: (acc * (2.0 * jnp.maximum(a, 0.0)),), extras=(a_up,))
    g["w_down"] = _mm(hrelu, dff, "tn", tm=1024, tn=1024, name="mlp_down_dw")
    du2 = _mm(da, wts["w_up"], "nt", tm=1024, tn=1024, name="mlp_up_dx")
    g["w_up"] = _mm(u2, da, "tn", tm=1024, tn=1024, name="mlp_up_dw")
    dh1, g["norm_mlp_pre"] = _rms_bwd(h1, du2, small["norm_mlp_pre"], name="norm_mlp_pre_bwd", out_dtype=f32, add=dh2)
    dmix, g["norm_mix_post"] = _rms_bwd(mix, dh1, small["norm_mix_post"], name="norm_mix_post_bwd", out_dtype=bf16)
    dmerged = _mm(dmix, wts["w_o"], "nt", tm=1024, tn=1024, name="w_o_dx")
    g["w_o"] = _mm(merged, dmix, "tn", tm=1024, tn=1024, name="w_o_dw")
    dt0, dt1, dt2, dgl = _merge_bwd(proj, t0, t1, t2, dmerged, S)
    dy_sb = _mm(dt0, wts["w_sb_out"], "nt", tm=1024, tn=1024, name="sb_out_dx")
    g["w_sb_out"] = _mm(y_sb, dt0, "tn", tm=1024, tn=1024, name="sb_out_dw")
    dy_ssd = _mm(dt1, wts["w_ssd_out"], "nt", tm=1024, tn=1024, name="ssd_out_dx")
    g["w_ssd_out"] = _mm(yn, dt1, "tn", tm=1024, tn=1024, name="ssd_out_dw")
    dy_mem = _mm(dt2, wts["w_mem_out"], "nt", tm=1024, tn=1024, name="mem_out_dx")
    g["w_mem_out"] = _mm(y_mem, dt2, "tn", tm=1024, tn=1024, name="mem_out_dw")
    dmemq, dkv = _mem_bwd(proj, kv, dy_mem, S)
    g["w_mem_kv"] = _mm(mn, dkv, "tn", tm=1024, tn=1024, name="mem_kv_dw")
    dmn = _mm(dkv, wts["w_mem_kv"], "nt", tm=M, tn=1024, name="mem_kv_dx")
    _, g["norm_mem"] = _rms_bwd(mem, dmn, small["norm_mem"], name="norm_mem_bwd", out_dtype=bf16, tm=min(512, M))
    tick = grads_ready(g)
    dz, dxbc, ddt, g["ssd_norm"], dsk, dalog, ddtb = _ssd_bwd(
        dy_ssd, y_ssd, xbc, proj, pdt, hprev, dtb_p + tick, alog_p, dsk_c, small["ssd_norm"], S)
    g["d_skip"], g["a_log"], g["dt_bias"] = dsk[:, :SSD_HEADS], dalog[:, :SSD_HEADS], ddtb[:, :SSD_HEADS]
    dxbc_raw, dcw, g["conv_b"] = _conv_bwd(proj, xc, dxbc, small["conv_w"], S)
    g["conv_w"] = dcw[:CONV_K]
    dq, dk, dv = _sb_bwd(proj, tot_lk, dy_sb, S)
    dproj = jnp.concatenate([dq, dk, dv, dxbc_raw, dgl, dmemq, dz], axis=1)
    du_dt = _mm(ddt, wts["w_dt"], "nt", tm=1024, tn=1024, name="in_proj_dt_dx")
    du = _mm(dproj, wts["w_main"], "nt", tm=512, tn=256, name="in_proj_dx",
             epi=lambda acc, e: (acc + e,), extras=(du_dt,))
    g["w_main"] = _mm(u, dproj, "tn", tm=1024, tn=1024, name="in_proj_dw")
    g["w_dt"] = _mm(u, ddt, "tn", tm=1024, tn=128, name="in_proj_dt_dw")
    grad_x, g["norm_mix_pre"] = _rms_bwd(x, du, small["norm_mix_pre"], name="norm_pre_bwd", out_dtype=f32, add=dh1)
    return loss, grad_x, g


def _to_internal(w_in):
    sec = lambda r: w_in[:, r[0]:r[1]]
    w_main = jnp.concatenate([sec(R_QKV), sec(R_XBC), sec(R_GATE), sec(R_MEMQ), sec(R_Z)], axis=1)
    w_dt = jnp.pad(sec(R_DT), ((0, 0), (0, 128 - SSD_HEADS)))
    return w_main, w_dt


def _from_internal(g_main, g_dt):
    sec = lambda p, n: g_main[:, p:p + n]
    return jnp.concatenate([sec(P_QKV, 3072), sec(P_Z, 2048), sec(P_XBC, 3072), g_dt[:, :SSD_HEADS],
                            sec(P_MEMQ, 1024), sec(P_GATE, 3072)], axis=1)


MESH = pl.DeviceIdType.MESH
ANY = pl.BlockSpec(memory_space=pl.ANY)


def _place():
    x, y, c = lax.axis_index("x"), lax.axis_index("y"), lax.axis_index("c")
    return (x, y, c), [(1 - x, y, c), (x, 1 - y, c), (1 - x, 1 - y, c)]


def _exchange_copy(mode, ins, lands, send, recv, a, k, me, peers, arriving):
    p = peers[k]
    theirs = 2 * p[0] + p[1]
    if mode == "gather":
        src, dst = ins[a], lands[a].at[theirs if arriving else me]
    else:
        src, dst = ins[a].at[theirs], lands[a].at[k]
    return pltpu.make_async_remote_copy(src_ref=src, dst_ref=dst, send_sem=send.at[a * 3 + k],
                                        recv_sem=recv.at[a * 3 + k], device_id=p, device_id_type=MESH)


def _exchange(srcs, mode, name):
    n = len(srcs)

    def body(*refs):
        ins, lnd = refs[:n], refs[n:2 * n]
        send, recv, loc = refs[2 * n:]
        (x, y, c), peers = _place()
        me = 2 * x + y
        own = [pltpu.make_async_copy(ins[a], lnd[a].at[me], loc.at[a]) for a in range(n)] if mode == "gather" else []
        sent = [_exchange_copy(mode, ins, lnd, send, recv, a, k, me, peers, False) for a in range(n) for k in range(3)]
        for cp in own + sent:
            cp.start()
        for a in range(n):
            for k in range(3):
                _exchange_copy(mode, ins, lnd, send, recv, a, k, me, peers, True).wait_recv()
        for cp in sent:
            cp.wait_send()
        for cp in own:
            cp.wait()

    return pl.pallas_call(
        body, name=name, in_specs=[ANY] * n, out_specs=[ANY] * n,
        out_shape=[jax.ShapeDtypeStruct((N_SHARD,) + s.shape if mode == "gather" else (3,) + s.shape[1:], s.dtype)
                   for s in srcs],
        scratch_shapes=[pltpu.SemaphoreType.DMA((3 * n,)), pltpu.SemaphoreType.DMA((3 * n,)),
                        pltpu.SemaphoreType.DMA((n,))],
    )(*srcs)


def _exchange_packets(packet):
    def body(pk, pk_out, send, recv, loc):
        x, y, c = lax.axis_index("x"), lax.axis_index("y"), lax.axis_index("c")
        lin = 4 * x + 2 * y + c
        own = pltpu.make_async_copy(pk, pk_out.at[lin], loc.at[0])
        own.start()

        def pk_copy(m, slot):
            dev = (x ^ ((m >> 2) & 1), y ^ ((m >> 1) & 1), c ^ (m & 1))
            return pltpu.make_async_remote_copy(
                src_ref=pk, dst_ref=pk_out.at[slot], send_sem=send.at[m - 1], recv_sem=recv.at[m - 1],
                device_id=dev, device_id_type=MESH)

        sent = [pk_copy(m, lin) for m in range(1, N_DEV)]
        for cp in sent:
            cp.start()
        for m in range(1, N_DEV):
            pk_copy(m, lin ^ m).wait_recv()
        for cp in sent:
            cp.wait_send()
        own.wait()

    return pl.pallas_call(
        body, name="exchange_packets", in_specs=[ANY], out_specs=ANY,
        out_shape=jax.ShapeDtypeStruct((N_DEV,) + packet.shape, packet.dtype),
        scratch_shapes=[pltpu.SemaphoreType.DMA((N_DEV - 1,)), pltpu.SemaphoreType.DMA((N_DEV - 1,)),
                        pltpu.SemaphoreType.DMA((1,))],
    )(packet)


def _swap_sibling(parts, name):
    n = len(parts)

    def body(*refs):
        ins, outs = refs[:n], refs[n:2 * n]
        send, recv = refs[2 * n:]
        x, y, c = lax.axis_index("x"), lax.axis_index("y"), lax.axis_index("c")
        cps = [pltpu.make_async_remote_copy(
            src_ref=ins[a], dst_ref=outs[a], send_sem=send.at[a], recv_sem=recv.at[a],
            device_id=(x, y, 1 - c), device_id_type=MESH) for a in range(n)]
        for cp in cps:
            cp.start()
        for cp in cps:
            cp.wait_recv()
        for cp in cps:
            cp.wait_send()

    return pl.pallas_call(
        body, name=name,
        in_specs=[ANY] * n, out_specs=[ANY] * n,
        out_shape=[jax.ShapeDtypeStruct(p.shape, p.dtype) for p in parts],
        scratch_shapes=[pltpu.SemaphoreType.DMA((n,)), pltpu.SemaphoreType.DMA((n,))],
    )(*parts)


BLOCK_ELEMS = 256 * 1024


def _row_tile(R, C):
    tr = max(8, (BLOCK_ELEMS // C) // 8 * 8)
    while R % tr:
        tr -= 8
    return min(tr, R)


def _sum_parts(own, stack, name):
    k = stack.shape[0]
    R, C = stack.shape[1:]
    tr = _row_tile(R, C)

    def body(*refs):
        o_ref = refs[-1]
        acc = refs[0][...]
        for r in refs[1:-1]:
            acc = acc + r[...]
        o_ref[...] = acc

    row = pl.BlockSpec((tr, C), lambda i: (i, 0))
    specs = ([row] if own is not None else []) + [
        pl.BlockSpec((None, tr, C), functools.partial(lambda i, j: (j, i, 0), j=j)) for j in range(k)]
    args = ([own] if own is not None else []) + [stack] * k
    return pl.pallas_call(
        body, name=name, grid=(R // tr,), in_specs=specs, out_specs=row,
        out_shape=jax.ShapeDtypeStruct((R, C), f32), compiler_params=_params(("parallel",)),
    )(*args)


def _adamw(w, m, v, g_parts, name):
    R, C = w.shape
    tr = _row_tile(R, C)
    n_g = len(g_parts)

    def body(w_ref, m_ref, v_ref, *rest):
        g = rest[0][...]
        for r in rest[1:n_g]:
            g = g + r[...]
        g_ref, d_ref, nm_ref, nv_ref = rest[n_g:]
        nm = ADAM_B1 * m_ref[...] + (1.0 - ADAM_B1) * g
        nv = ADAM_B2 * v_ref[...] + (1.0 - ADAM_B2) * jnp.square(g)
        m_hat = nm / (1.0 - ADAM_B1 ** ADAM_STEP)
        v_hat = nv / (1.0 - ADAM_B2 ** ADAM_STEP)
        g_ref[...] = g
        d_ref[...] = -ADAM_LR * (m_hat / (jnp.sqrt(v_hat) + ADAM_EPS) + ADAM_WD * w_ref[...])
        nm_ref[...] = nm
        nv_ref[...] = nv

    row = pl.BlockSpec((tr, C), lambda i: (i, 0))
    return pl.pallas_call(
        body, name=name, grid=(R // tr,), in_specs=[row] * (3 + n_g), out_specs=[row] * 4,
        out_shape=[jax.ShapeDtypeStruct((R, C), f32)] * 4, compiler_params=_params(("parallel",)),
    )(w, m, v, *g_parts)


BIG = ("w_in", "w_mem_kv", "w_sb_out", "w_ssd_out", "w_mem_out", "w_o", "w_up", "w_down")
FIRST = ("w_in", "w_mem_kv")
LATE = ("w_sb_out", "w_ssd_out", "w_mem_out", "w_o", "w_up", "w_down")
REST = BIG[1:]
COL_SHARDED = ("w_in", "w_mem_kv", "w_up")
SMALL = ("norm_mix_pre", "conv_w", "conv_b", "dt_bias", "a_log", "d_skip", "ssd_norm", "norm_mem",
         "norm_mix_post", "norm_mlp_pre", "norm_mlp_post")
WEIGHTS = ("norm_mix_pre", "w_in", "conv_w", "conv_b", "dt_bias", "a_log", "d_skip", "ssd_norm", "norm_mem",
           "w_mem_kv", "w_sb_out", "w_ssd_out", "w_mem_out", "w_o", "norm_mix_post", "norm_mlp_pre", "w_up",
           "w_down", "norm_mlp_post")
PK_ROWS = 184


def _pack(vecs):
    flat = jnp.concatenate([v.reshape(-1) for v in vecs])
    return jnp.pad(flat, (0, PK_ROWS * 128 - flat.shape[0])).reshape(PK_ROWS, 128)


def _unpack(pk, shapes):
    flat = pk.reshape(-1)
    out, off = [], 0
    for s in shapes:
        n = 1
        for d in s:
            n *= d
        out.append(flat[off:off + n].reshape(s))
        off += n
    return out


def _full_from_slabs(name, slabs):
    if name in COL_SHARDED:
        return slabs.transpose(1, 0, 2).reshape(slabs.shape[1], -1)
    return slabs.reshape(-1, slabs.shape[2])


def _slabs_from_full(name, g):
    if name in COL_SHARDED:
        return g.reshape(g.shape[0], N_SHARD, -1).transpose(1, 0, 2)
    return g.reshape(N_SHARD, -1, g.shape[1])


def kernel(x, mem, norm_mix_pre, w_in, conv_w, conv_b, dt_bias, a_log, d_skip, ssd_norm, norm_mem, w_mem_kv, w_sb_out, w_ssd_out, w_mem_out, w_o, norm_mix_post, norm_mlp_pre, w_up, w_down, norm_mlp_post, loss_target, m_norm_mix_pre, m_w_in, m_conv_w, m_conv_b, m_dt_bias, m_a_log, m_d_skip, m_ssd_norm, m_norm_mem, m_w_mem_kv, m_w_sb_out, m_w_ssd_out, m_w_mem_out, m_w_o, m_norm_mix_post, m_norm_mlp_pre, m_w_up, m_w_down, m_norm_mlp_post, v_norm_mix_pre, v_w_in, v_conv_w, v_conv_b, v_dt_bias, v_a_log, v_d_skip, v_ssd_norm, v_norm_mem, v_w_mem_kv, v_w_sb_out, v_w_ssd_out, v_w_mem_out, v_w_o, v_norm_mix_post, v_norm_mlp_pre, v_w_up, v_w_down, v_norm_mlp_post):
    env = dict(locals())
    w = {n: env[n] for n in WEIGHTS}
    mo = {n: env["m_" + n] for n in WEIGHTS}
    vo = {n: env["v_" + n] for n in WEIGHTS}
    shard = 2 * lax.axis_index("x") + lax.axis_index("y")

    first = _exchange([w[n][0].astype(bf16) for n in FIRST] + [w["conv_w"][0]], "gather", "gather_first")
    late = _exchange([w[n][0].astype(bf16) for n in LATE], "gather", "gather_late")

    def first_weights(u):
        w_main, w_dt = _to_internal(_full_from_slabs("w_in", first[0]))
        return dict(w_main=w_main, w_dt=w_dt, w_mem_kv=_full_from_slabs("w_mem_kv", first[1]),
                    conv_w=first[2].transpose(1, 0, 2).reshape(CONV_K, CONV_DIM))

    def late_weights(after):
        return {n: _full_from_slabs(n, s) for n, s in zip(LATE, late)}

    small = {n: w[n] for n in SMALL if n != "conv_w"}
    loss, grad_x, g = _local_step(x[0], mem[0], loss_target[0], first_weights, late_weights, small,
                                  lambda g: jnp.zeros((1, 128), f32))
    g["w_in"] = _from_internal(g.pop("w_main"), g.pop("w_dt"))

    lands = _exchange([_slabs_from_full(n, g[n]).astype(bf16) for n in BIG], "scatter", "scatter_grads")
    packets = _exchange_packets(_pack([g[n] for n in SMALL] + [loss[:, :1]]))
    partial = []
    for n, r in zip(BIG, lands):
        own = lax.dynamic_index_in_dim(_slabs_from_full(n, g[n]), shard, 0, keepdims=False)
        partial.append(_sum_parts(own, r, name="sum_chips_" + n))
    other = _swap_sibling(partial, "swap_sibling")

    out_g, out_d, out_m, out_v = {}, {}, {}, {}
    for n, p, q in zip(BIG, partial, other):
        res = _adamw(w[n][0], mo[n][0], vo[n][0], [p, q], name="adamw_" + n)
        out_g[n], out_d[n], out_m[n], out_v[n] = [r[None] for r in res]
    tot = _sum_parts(None, packets, name="sum_packets")
    shapes = [g[n].shape for n in SMALL] + [(1, 1)]
    sm = dict(zip(SMALL + ("loss",), _unpack(tot, shapes)))
    sm["conv_w"] = lax.dynamic_slice_in_dim(sm["conv_w"], shard * (CONV_DIM // N_SHARD), CONV_DIM // N_SHARD, axis=1)
    own_small = lambda d: _pack([d[n].reshape(sm[n].shape) for n in SMALL])
    res = _adamw(own_small(w), own_small(mo), own_small(vo), [own_small(sm)], name="adamw_small")
    own_shapes = [sm[n].shape for n in SMALL]
    for store, r in zip((out_g, out_d, out_m, out_v), res):
        for n, val in zip(SMALL, _unpack(r, own_shapes)):
            store[n] = val.reshape(w[n].shape)

    outs = [sm["loss"].reshape(()), grad_x[None]]
    for store in (out_g, out_d, out_m, out_v):
        outs += [store[n] for n in WEIGHTS]
    return tuple(outs)
```

```python
import functools

import jax
import jax.numpy as jnp
from jax import lax
from jax.experimental import pallas as pl
from jax.experimental.pallas import tpu as pltpu

f32 = jnp.float32
bf16 = jnp.bfloat16

D = 1024
EPS = 1e-6
SB_HD = 64
SSD_INNER = 2048
SSD_HEADS = 32
SSD_GROUPS = 4
SSD_N = 128
SSD_L = 128
CONV_K = 4
CONV_DIM = 3072
MEM_HEADS = 4
MEM_HD = 256
D_FF = 4096
D_IN = 12320
N_SHARD = 4
N_DEV = 8

P_QKV, P_XBC, P_GATE, P_MEMQ, P_Z, P_DT, P_TOT = 0, 3072, 6144, 9216, 10240, 12288, 12416
R_QKV, R_Z, R_XBC, R_DT, R_MEMQ, R_GATE = (0, 3072), (3072, 5120), (5120, 8192), (8192, 8224), (8224, 9248), (9248, 12320)

ADAM_LR = 0.001
ADAM_B1 = 0.9
ADAM_B2 = 0.999
ADAM_EPS = 1e-08
ADAM_WD = 0.01
ADAM_STEP = 10

VMEM_LIMIT = 56 * 1024 * 1024

NN = (((1,), (0,)), ((), ()))
NT = (((1,), (1,)), ((), ()))
TN = (((0,), (0,)), ((), ()))


def _dot(a, b, dims=NN):
    return lax.dot_general(a, b, dims, preferred_element_type=f32)


def _params(sem=None):
    return pltpu.CompilerParams(dimension_semantics=sem, vmem_limit_bytes=VMEM_LIMIT)


def _sigmoid(x):
    return 1.0 / (1.0 + jnp.exp(-x))


def _split2(x):
    hi = x.astype(bf16)
    lo = (x - hi.astype(f32)).astype(bf16)
    return hi, lo


def _split3(x):
    hi = x.astype(bf16)
    r = x - hi.astype(f32)
    mid = r.astype(bf16)
    lo = (r - mid.astype(f32)).astype(bf16)
    return hi, mid, lo


def _mm(a, b, mode, *, tm, tn, name, out_dtypes=(f32,), epi=None, extras=()):
    M = a.shape[1] if mode == "tn" else a.shape[0]
    N = b.shape[0] if mode == "nt" else b.shape[1]
    tm, tn = min(tm, M), min(tn, N)
    if mode == "nn":
        (M, K), N = a.shape, b.shape[1]
        a_spec = pl.BlockSpec((tm, K), lambda i, j: (i, 0))
        b_spec = pl.BlockSpec((K, tn), lambda i, j: (0, j))
        dims = NN
    elif mode == "nt":
        (M, K), N = a.shape, b.shape[0]
        a_spec = pl.BlockSpec((tm, K), lambda i, j: (i, 0))
        b_spec = pl.BlockSpec((tn, K), lambda i, j: (j, 0))
        dims = NT
    else:
        (K, M), N = a.shape, b.shape[1]
        a_spec = pl.BlockSpec((K, tm), lambda i, j: (0, i))
        b_spec = pl.BlockSpec((K, tn), lambda i, j: (0, j))
        dims = TN
    assert M % tm == 0 and N % tn == 0, (name, M, N, tm, tn)
    n_ex = len(extras)
    o_spec = pl.BlockSpec((tm, tn), lambda i, j: (i, j))

    def body(a_ref, b_ref, *rest):
        acc = _dot(a_ref[...].astype(bf16), b_ref[...].astype(bf16), dims)
        res = (acc,) if epi is None else epi(acc, *[e[...] for e in rest[:n_ex]])
        for o_ref, r in zip(rest[n_ex:], res):
            o_ref[...] = r.astype(o_ref.dtype)

    out = pl.pallas_call(
        body, name=name, grid=(M // tm, N // tn),
        in_specs=[a_spec, b_spec] + [o_spec] * n_ex,
        out_specs=[o_spec] * len(out_dtypes),
        out_shape=[jax.ShapeDtypeStruct((M, N), dt) for dt in out_dtypes],
        compiler_params=_params(("parallel", "parallel")),
    )(a, b, *extras)
    return out[0] if len(out_dtypes) == 1 else out


def _rms_fwd(x, g, *, name, out_dtype, residual=None, tm=512):
    S, C = x.shape
    tm = min(tm, S)
    has_res = residual is not None

    def body(x_ref, g_ref, *rest):
        xv = x_ref[...]
        r = lax.rsqrt(jnp.mean(xv * xv, axis=1, keepdims=True) + EPS)
        y = xv * r * g_ref[...]
        if has_res:
            y = y + rest[0][...]
        rest[-1][...] = y.astype(out_dtype)

    row = pl.BlockSpec((tm, C), lambda i: (i, 0))
    vec = pl.BlockSpec((1, C), lambda i: (0, 0))
    args = (x, g) + ((residual,) if has_res else ())
    return pl.pallas_call(
        body, name=name, grid=(S // tm,),
        in_specs=[row, vec] + ([row] if has_res else []),
        out_specs=row, out_shape=jax.ShapeDtypeStruct((S, C), out_dtype),
        compiler_params=_params(("parallel",)),
    )(*args)


def _rms_bwd(x, dy, g, *, name, out_dtype, add=None, tm=512):
    S, C = x.shape
    tm = min(tm, S)
    has_add = add is not None

    def body(x_ref, dy_ref, g_ref, *rest):
        dx_ref, dg_ref = rest[-2], rest[-1]
        xv = x_ref[...]
        dyv = dy_ref[...].astype(f32)
        r = lax.rsqrt(jnp.mean(xv * xv, axis=1, keepdims=True) + EPS)
        xh = xv * r
        dxh = dyv * g_ref[...]
        dx = r * (dxh - xh * jnp.mean(dxh * xh, axis=1, keepdims=True))
        if has_add:
            dx = dx + rest[0][...]
        dx_ref[...] = dx.astype(out_dtype)

        @pl.when(pl.program_id(0) == 0)
        def _():
            dg_ref[...] = jnp.zeros_like(dg_ref)

        dg_ref[...] += jnp.sum(dyv * xh, axis=0, keepdims=True)

    row = pl.BlockSpec((tm, C), lambda i: (i, 0))
    vec = pl.BlockSpec((1, C), lambda i: (0, 0))
    args = (x, dy, g) + ((add,) if has_add else ())
    return pl.pallas_call(
        body, name=name, grid=(S // tm,),
        in_specs=[row, row, vec] + ([row] if has_add else []),
        out_specs=[row, vec],
        out_shape=[jax.ShapeDtypeStruct((S, C), out_dtype), jax.ShapeDtypeStruct((1, C), f32)],
        compiler_params=_params(("arbitrary",)),
    )(*args)


SB_T = 128
SB_SPENT = -120.0
SB_GROUPS = (4, 2, 1)
SB_GROUPS_BWD = (4, 2, 1)


def _sb_masks():
    lane = lax.broadcasted_iota(jnp.int32, (1, 128), 1)
    m_a = (lane < SB_HD).astype(f32)
    return m_a, 1.0 - m_a


def _sb_logits(z, mask):
    l1p = jnp.log(1.0 + jnp.exp(-jnp.abs(z)))
    lb = jnp.minimum(z, 0.0) - l1p
    lk = lb - z
    if mask is not None:
        lk = jnp.where(mask, lk, 0.0)
    return lb, lk


def _chunks(a, n):
    return [a[:, u * SB_T:(u + 1) * SB_T] for u in range(n)]


def _cat(parts, axis):
    return parts[0] if len(parts) == 1 else jnp.concatenate(parts, axis=axis)


def _chunk_matmul(parts_list, u_mat):
    out = _dot(_cat(parts_list, 0), u_mat)
    return [out[u * SB_T:(u + 1) * SB_T] for u in range(len(parts_list))]


def _chunk_cumsum(lk, n, u_mat):
    hi = lk.astype(bf16)
    lo = (lk - hi.astype(f32)).astype(bf16)
    out = _chunk_matmul(_chunks(hi, n) + _chunks(lo, n), u_mat)
    return [out[u] + out[n + u] for u in range(n)]


def _sb_fwd(proj, S, ride=None):
    nq = S // SB_T
    n_pairs = D // 128
    scale = SB_HD ** -0.5
    n_r = ride.n if ride else 0

    def body(q_ref, k_ref, v_ref, *rest):
        o_ref, t_ref = rest[n_r:n_r + 2]
        i = pl.program_id(1)
        if ride:
            pl.when((pl.program_id(0) == 0) & (i == 0))(
                lambda: ride.start(rest[:n_r], rest[n_r + 2:2 * n_r + 2], rest[2 * n_r + 2:]))
        m_a, m_b = _sb_masks()
        r_i = lax.broadcasted_iota(jnp.int32, (SB_T, SB_T), 0)
        c_i = lax.broadcasted_iota(jnp.int32, (SB_T, SB_T), 1)
        u_mat = (r_i > c_i).astype(bf16)
        causal = c_i < r_i
        q = q_ref[...] * scale
        q_h = ((q * m_a).astype(bf16), (q * m_b).astype(bf16))

        def group(j_lo, n, carry, mask):
            acc, c_a, c_b = carry
            rows = pl.ds(pl.multiple_of(j_lo * SB_T, SB_T), n * SB_T)
            k = k_ref[rows, :].astype(bf16)
            v = v_ref[rows, :]
            zs = [_dot(q_b, k, NT) for q_b in q_h]
            lbk = [_sb_logits(z, mask) for z in zs]
            parts = [_chunk_cumsum(lk, n, u_mat) for _, lk in lbk]
            ws, cs = [], []
            for (lb, lk), part, c in zip(lbk, parts, (c_a, c_b)):
                lb_c, lk_c = _chunks(lb, n), _chunks(lk, n)
                w_c = [None] * n
                for u in reversed(range(n)):
                    w_c[u] = jnp.exp(lb_c[u] + c + part[u])
                    c = c + jnp.sum(lk_c[u], axis=1, keepdims=True)
                w = _cat(w_c, 1)
                if mask is not None:
                    w = jnp.where(mask, w, 0.0)
                ws.append(w.astype(bf16))
                cs.append(c)
            for w, m in zip(ws, (m_a, m_b)):
                acc = acc + _dot(w, (v * m).astype(bf16))
            return acc, cs[0], cs[1]

        zero_c = jnp.zeros((SB_T, 1), f32)
        carry = group(i, 1, (jnp.zeros((SB_T, 128), f32), zero_c, zero_c), causal)

        def spent(cr):
            return (jnp.max(jnp.maximum(cr[1], cr[2])) < SB_SPENT).astype(jnp.int32)

        state = (i, spent(carry), carry)
        for n in SB_GROUPS:
            def step(st, n=n):
                left, _, cr = st
                cr = group(left - n, n, cr, None)
                return left - n, spent(cr), cr

            state = lax.while_loop(lambda st, n=n: (st[0] >= n) & (st[1] == 0), step, state)
        left, _, carry = state
        o_ref[...] = carry[0]
        lane = lax.broadcasted_iota(jnp.int32, (1, 128), 1)
        t_ref[...] = (jnp.where(lane == 0, carry[1], 0.0) + jnp.where(lane == SB_HD, carry[2], 0.0)
                      + jnp.where(lane == 1, left.astype(f32), 0.0))
        if ride:
            pl.when((pl.program_id(0) == n_pairs - 1) & (i == nq - 1))(
                lambda: ride.finish(rest[:n_r], rest[n_r + 2:2 * n_r + 2], rest[2 * n_r + 2:]))

    qs = pl.BlockSpec((SB_T, 128), lambda h, i: (i, h))
    out = pl.pallas_call(
        body, name="sb_fwd", grid=(n_pairs, nq),
        in_specs=[qs,
                  pl.BlockSpec((S, 128), lambda h, i: (0, n_pairs + h)),
                  pl.BlockSpec((S, 128), lambda h, i: (0, 2 * n_pairs + h))] + (ride.in_specs if ride else []),
        out_specs=[qs, qs] + (ride.out_specs if ride else []),
        out_shape=[jax.ShapeDtypeStruct((S, D), f32)] * 2 + (ride.out_shape if ride else []),
        scratch_shapes=ride.scratch if ride else [],
        compiler_params=_params(("arbitrary", "arbitrary")),
    )(proj, proj, proj, *(ride.srcs if ride else []))
    return out[0], out[1], list(out[2:])


def _sb_bwd(proj, tot_lk, do, S):
    nq = S // SB_T
    n_pairs = D // 128
    scale = SB_HD ** -0.5

    def body(q_ref, k_ref, v_ref, t_ref, do_ref, dq_ref, dk_ref, dv_ref, dk_acc, dv_acc):
        i = pl.program_id(1)
        m_a, m_b = _sb_masks()
        r_i = lax.broadcasted_iota(jnp.int32, (SB_T, SB_T), 0)
        c_i = lax.broadcasted_iota(jnp.int32, (SB_T, SB_T), 1)
        u_inc = (r_i <= c_i).astype(bf16)
        u_exc = (r_i < c_i).astype(bf16)
        causal = c_i < r_i

        @pl.when(i == 0)
        def _():
            dk_acc[...] = jnp.zeros_like(dk_acc)
            dv_acc[...] = jnp.zeros_like(dv_acc)

        q = q_ref[...] * scale
        dov = do_ref[...]
        tv = t_ref[...]
        lane = lax.broadcasted_iota(jnp.int32, (1, 128), 1)
        heads = []
        for m, first in ((m_a, 0), (m_b, SB_HD)):
            tot = jnp.sum(jnp.where(lane == first, tv, 0.0), axis=1, keepdims=True)
            heads.append(((q * m).astype(bf16), (dov * m).astype(bf16), tot, m))
        lowest = jnp.clip(jnp.max(jnp.where(lane == 1, tv, 0.0)).astype(jnp.int32), 0, i)

        def group(j_lo, n, carry, mask):
            dq_acc, cp_a, cp_b, ce_a, ce_b = carry
            rows = pl.ds(pl.multiple_of(j_lo * SB_T, SB_T), n * SB_T)
            k_f = k_ref[rows, :]
            k = k_f.astype(bf16)
            v = v_ref[rows, :].astype(bf16)
            zs = [_dot(h[0], k, NT) for h in heads]
            dws = [_dot(h[1], v, NT) for h in heads]
            lbk = [_sb_logits(z, mask) for z in zs]
            parts = [_chunk_cumsum(lk, n, u_inc) for _, lk in lbk]
            ws, es, cps = [], [], []
            for (lb, lk), part, dw, h, cp in zip(lbk, parts, dws, heads, (cp_a, cp_b)):
                lb_c, lk_c = _chunks(lb, n), _chunks(lk, n)
                w_c = []
                for u in range(n):
                    w_c.append(jnp.exp(lb_c[u] + (h[2] - cp) - part[u]))
                    cp = cp + jnp.sum(lk_c[u], axis=1, keepdims=True)
                w = _cat(w_c, 1)
                if mask is not None:
                    w = jnp.where(mask, w, 0.0)
                ws.append(w)
                es.append(dw * w)
                cps.append(cp)
            e_parts = [_chunk_matmul(_chunks(e.astype(bf16), n), u_exc) for e in es]
            dzs, ces = [], []
            for (lb, _), e, e_part, ce in zip(lbk, es, e_parts, (ce_a, ce_b)):
                e_c = _chunks(e, n)
                big_c = []
                for u in range(n):
                    big_c.append(ce + e_part[u])
                    ce = ce + jnp.sum(e_c[u], axis=1, keepdims=True)
                sig = jnp.exp(lb)
                dz = e * (1.0 - sig) - _cat(big_c, 1) * sig
                if mask is not None:
                    dz = jnp.where(mask, dz, 0.0)
                dzs.append(dz.astype(bf16))
                ces.append(ce)
            dk_t = jnp.zeros((n * SB_T, 128), f32)
            dv_t = jnp.zeros((n * SB_T, 128), f32)
            for dz_b, w, h in zip(dzs, ws, heads):
                dq_acc = dq_acc + _dot(dz_b, (k_f * h[3]).astype(bf16))
                dk_t = dk_t + _dot(dz_b, h[0], TN)
                dv_t = dv_t + _dot(w.astype(bf16), h[1], TN)
            dk_acc[rows, :] += dk_t
            dv_acc[rows, :] += dv_t
            return dq_acc, cps[0], cps[1], ces[0], ces[1]

        zc = jnp.zeros((SB_T, 1), f32)
        carry = (jnp.zeros((SB_T, 128), f32), zc, zc, zc, zc)
        done = lowest
        for n in SB_GROUPS_BWD:
            trips = (i - done) // n
            carry = lax.fori_loop(
                0, trips, functools.partial(lambda gi, cr, n, done: group(done + gi * n, n, cr, None), n=n, done=done),
                carry)
            done = done + trips * n
        carry = group(i, 1, carry, causal)
        dq_ref[...] = (carry[0] * scale).astype(bf16)

        @pl.when(i == nq - 1)
        def _():
            dk_ref[...] = dk_acc[...].astype(bf16)
            dv_ref[...] = dv_acc[...].astype(bf16)

    qs = pl.BlockSpec((SB_T, 128), lambda h, i: (i, h))
    full = pl.BlockSpec((S, 128), lambda h, i: (0, h))
    dq, dk, dv = pl.pallas_call(
        body, name="sb_bwd", grid=(n_pairs, nq),
        in_specs=[qs,
                  pl.BlockSpec((S, 128), lambda h, i: (0, n_pairs + h)),
                  pl.BlockSpec((S, 128), lambda h, i: (0, 2 * n_pairs + h)),
                  qs, qs],
        out_specs=[qs, full, full],
        out_shape=[jax.ShapeDtypeStruct((S, D), bf16)] * 3,
        scratch_shapes=[pltpu.VMEM((S, 128), f32), pltpu.VMEM((S, 128), f32)],
        compiler_params=_params(("parallel", "arbitrary")),
    )(proj, proj, proj, tot_lk, do)
    return dq, dk, dv


CONV_CB = 256
HALO = 8


def _conv_fwd(proj, conv_w, conv_b, S):
    tr = min(512, S)

    def body(x_ref, w_ref, b_ref, xc_ref, xbc_ref):
        w = w_ref[...]
        for t in range(S // tr):
            cur = x_ref[t * tr:(t + 1) * tr, :]
            halo = x_ref[t * tr - HALO:t * tr, :] if t else jnp.zeros((HALO, CONV_CB), f32)
            win = jnp.concatenate([halo, cur], axis=0)
            acc = b_ref[...] + w[CONV_K - 1:CONV_K, :] * cur
            for k in range(CONV_K - 1):
                acc = acc + w[k:k + 1, :] * pltpu.roll(win, CONV_K - 1 - k, 0)[HALO:, :]
            xc_ref[t * tr:(t + 1) * tr, :] = acc
            xbc_ref[t * tr:(t + 1) * tr, :] = acc * _sigmoid(acc)

    col = pl.BlockSpec((S, CONV_CB), lambda c: (0, c))
    return pl.pallas_call(
        body, name="conv_fwd", grid=(CONV_DIM // CONV_CB,),
        in_specs=[pl.BlockSpec((S, CONV_CB), lambda c: (0, P_XBC // CONV_CB + c)),
                  pl.BlockSpec((CONV_K, CONV_CB), lambda c: (0, c)),
                  pl.BlockSpec((1, CONV_CB), lambda c: (0, c))],
        out_specs=[col, col], out_shape=[jax.ShapeDtypeStruct((S, CONV_DIM), f32)] * 2,
        compiler_params=_params(("parallel",)),
    )(proj, conv_w, conv_b)


def _conv_bwd(proj, xc, dxbc, conv_w, S):
    tr = min(512, S)

    def body(x_ref, xc_ref, dy_ref, w_ref, dx_ref, dw_ref, db_ref, dxc_s):
        w = w_ref[...]
        xcv = xc_ref[...]
        sg = _sigmoid(xcv)
        dxc_s[0:S, :] = dy_ref[...] * (sg * (1.0 + xcv * (1.0 - sg)))
        dxc_s[S:S + HALO, :] = jnp.zeros((HALO, CONV_CB), f32)
        dws = [jnp.zeros((1, CONV_CB), f32) for _ in range(CONV_K)]
        db = jnp.zeros((1, CONV_CB), f32)
        for t in range(S // tr):
            cur = x_ref[t * tr:(t + 1) * tr, :]
            halo = x_ref[t * tr - HALO:t * tr, :] if t else jnp.zeros((HALO, CONV_CB), f32)
            win = jnp.concatenate([halo, cur], axis=0)
            dwin = dxc_s[t * tr:(t + 1) * tr + HALO, :]
            dcur = dwin[0:tr, :]
            db = db + jnp.sum(dcur, axis=0, keepdims=True)
            dws[CONV_K - 1] = dws[CONV_K - 1] + jnp.sum(dcur * cur, axis=0, keepdims=True)
            dx = w[CONV_K - 1:CONV_K, :] * dcur
            for k in range(CONV_K - 1):
                sh = CONV_K - 1 - k
                dws[k] = dws[k] + jnp.sum(dcur * pltpu.roll(win, sh, 0)[HALO:, :], axis=0, keepdims=True)
                dx = dx + w[k:k + 1, :] * pltpu.roll(dwin, tr + HALO - sh, 0)[0:tr, :]
            dx_ref[t * tr:(t + 1) * tr, :] = dx.astype(bf16)
        dw_ref[...] = jnp.concatenate(dws + [jnp.zeros((8 - CONV_K, CONV_CB), f32)], axis=0)
        db_ref[...] = db

    col = pl.BlockSpec((S, CONV_CB), lambda c: (0, c))
    return pl.pallas_call(
        body, name="conv_bwd", grid=(CONV_DIM // CONV_CB,),
        in_specs=[pl.BlockSpec((S, CONV_CB), lambda c: (0, P_XBC // CONV_CB + c)), col, col,
                  pl.BlockSpec((CONV_K, CONV_CB), lambda c: (0, c))],
        out_specs=[col, pl.BlockSpec((8, CONV_CB), lambda c: (0, c)), pl.BlockSpec((1, CONV_CB), lambda c: (0, c))],
        out_shape=[jax.ShapeDtypeStruct((S, CONV_DIM), bf16), jax.ShapeDtypeStruct((8, CONV_DIM), f32),
                   jax.ShapeDtypeStruct((1, CONV_DIM), f32)],
        scratch_shapes=[pltpu.VMEM((S + HALO, CONV_CB), f32)],
        compiler_params=_params(("parallel",)),
    )(proj, xc, dxbc, conv_w)


N_PAIR = SSD_HEADS // 2
NEG = -1e30


def _softplus(x):
    return jnp.maximum(x, 0.0) + jnp.log(1.0 + jnp.exp(-jnp.abs(x)))


def _ssd_common(dtr, dtb, alog):
    L = SSD_L
    r_i = lax.broadcasted_iota(jnp.int32, (L, L), 0)
    c_i = lax.broadcasted_iota(jnp.int32, (L, L), 1)
    dt = _softplus(dtr + dtb)
    a = -jnp.exp(alog)
    da = dt * a
    lower = (r_i >= c_i).astype(bf16)
    upper = (r_i <= c_i).astype(bf16)
    parts = _split3(da)
    a_cs = sum(_dot(lower, p) for p in parts)
    a_cs_t = sum(_dot(p, upper, TN) for p in parts)
    return dt, a, a_cs, a_cs_t, r_i >= c_i


def _pair_vec(lane, v, h):
    return jnp.where(lane < SB_HD, v[:, h:h + 1], v[:, h + 1:h + 2])


def _decay_mat(a_cs, a_cs_t, h, tril):
    return jnp.exp(jnp.where(tril, a_cs[:, h:h + 1] - a_cs_t[h:h + 1, :], NEG))


def _ssd_fwd(xbc, proj, pdt, dt_bias_p, a_log_p, d_skip_c, ssd_norm, S):
    L = SSD_L
    nc = S // L

    def body(xbc_ref, dt_ref, z_ref, dtb_ref, alog_ref, dsk_ref, gn_ref, y_ref, yn_ref, hp_ref, state):
        c = pl.program_id(0)

        @pl.when(c == 0)
        def _():
            state[...] = jnp.zeros_like(state)

        hp_ref[0] = state[...]
        lane = lax.broadcasted_iota(jnp.int32, (1, 128), 1)
        row128 = lax.broadcasted_iota(jnp.int32, (128, 1), 0)
        m_a, m_b = _sb_masks()
        dt, a, a_cs, a_cs_t, tril = _ssd_common(dt_ref[...], dtb_ref[...], alog_ref[...])
        a_last = a_cs[L - 1:L, :]
        for g in range(SSD_GROUPS):
            b_g = xbc_ref[:, SSD_INNER + g * SSD_N:SSD_INNER + (g + 1) * SSD_N].astype(bf16)
            c_g = xbc_ref[:, SSD_INNER + (SSD_GROUPS + g) * SSD_N:SSD_INNER + (SSD_GROUPS + g + 1) * SSD_N].astype(bf16)
            cb = _dot(c_g, b_g, NT)
            for pr in range(4):
                h = 8 * g + 2 * pr
                pi = h // 2
                cols = slice(pi * 128, (pi + 1) * 128)
                xs = xbc_ref[:, cols]
                x = xs * _pair_vec(lane, dt, h)
                acs = _pair_vec(lane, a_cs, h)
                al = _pair_vec(lane, a_last, h)
                w_a = (cb * _decay_mat(a_cs, a_cs_t, h, tril)).astype(bf16)
                w_b = (cb * _decay_mat(a_cs, a_cs_t, h + 1, tril)).astype(bf16)
                yd = _dot(w_a, (x * m_a).astype(bf16)) + _dot(w_b, (x * m_b).astype(bf16))
                hp = state[pi]
                yo = _dot(c_g, hp.astype(bf16), NT) * jnp.exp(acs)
                y_ref[:, cols] = yd + yo + dsk_ref[:, cols] * xs
                dec = jnp.exp(jnp.where(row128 < SB_HD, a_last[:, h:h + 1], a_last[:, h + 1:h + 2]))
                state[pi] = hp * dec + _dot((x * jnp.exp(al - acs)).astype(bf16), b_g, TN)
        zz = z_ref[...]
        y2 = y_ref[...] * (zz * _sigmoid(zz))
        gw = SSD_INNER // SSD_GROUPS
        for g in range(SSD_GROUPS):
            yg = y2[:, g * gw:(g + 1) * gw]
            rg = lax.rsqrt(jnp.mean(yg * yg, axis=1, keepdims=True) + EPS)
            yn_ref[:, g * gw:(g + 1) * gw] = (yg * rg * gn_ref[:, g * gw:(g + 1) * gw]).astype(bf16)

    vec128 = pl.BlockSpec((1, 128), lambda c: (0, 0))
    vecin = pl.BlockSpec((1, SSD_INNER), lambda c: (0, 0))
    rows = pl.BlockSpec((L, SSD_INNER), lambda c: (c, 0))
    return pl.pallas_call(
        body, name="ssd_fwd", grid=(nc,),
        in_specs=[pl.BlockSpec((L, CONV_DIM), lambda c: (c, 0)),
                  pl.BlockSpec((L, 128), lambda c: (c, 0)),
                  pl.BlockSpec((L, SSD_INNER), lambda c: (c, P_Z // SSD_INNER)),
                  vec128, vec128, vecin, vecin],
        out_specs=[rows, rows, pl.BlockSpec((1, N_PAIR, 128, SSD_N), lambda c: (c, 0, 0, 0))],
        out_shape=[jax.ShapeDtypeStruct((S, SSD_INNER), f32), jax.ShapeDtypeStruct((S, SSD_INNER), bf16),
                   jax.ShapeDtypeStruct((nc, N_PAIR, 128, SSD_N), f32)],
        scratch_shapes=[pltpu.VMEM((N_PAIR, 128, SSD_N), f32)],
        compiler_params=_params(("arbitrary",)),
    )(xbc, pdt, proj, dt_bias_p, a_log_p, d_skip_c, ssd_norm)


def _sum_all(v):
    return jnp.sum(jnp.sum(v, axis=1, keepdims=True), axis=0, keepdims=True)


def _ssd_bwd(dyn, y, xbc, proj, pdt, hprev, dt_bias_p, a_log_p, d_skip_c, ssd_norm, S, ride=None):
    L = SSD_L
    nc = S // L
    n_r = ride.n if ride else 0

    def body(*refs):
        dyn_ref, y_ref, xbc_ref, dt_ref, z_ref, hp_ref, dtb_ref, alog_ref, dsk_ref, gn_ref = refs[:10]
        dz_ref, dxbc_ref, ddt_ref, dgn_ref, dsk_out, dalog_ref, ddtb_ref = refs[10 + n_r:17 + n_r]
        dstate, dy_s = refs[17 + 2 * n_r:19 + 2 * n_r]
        r_ins, r_lnd, r_sems = refs[10:10 + n_r], refs[17 + n_r:17 + 2 * n_r], refs[19 + 2 * n_r:]
        c = pl.program_id(0)
        if ride:
            pl.when(c == 0)(lambda: ride.start(r_ins, r_lnd, r_sems))

        @pl.when(c == 0)
        def _():
            dstate[...] = jnp.zeros_like(dstate)
            dgn_ref[...] = jnp.zeros_like(dgn_ref)
            dsk_out[...] = jnp.zeros_like(dsk_out)
            dalog_ref[...] = jnp.zeros_like(dalog_ref)
            ddtb_ref[...] = jnp.zeros_like(ddtb_ref)

        lane = lax.broadcasted_iota(jnp.int32, (1, 128), 1)
        row128 = lax.broadcasted_iota(jnp.int32, (128, 1), 0)
        rowl = lax.broadcasted_iota(jnp.int32, (L, 1), 0)
        m_a, m_b = _sb_masks()
        dtr = dt_ref[...]
        dt, a, a_cs, a_cs_t, tril = _ssd_common(dtr, dtb_ref[...], alog_ref[...])
        a_last = a_cs[L - 1:L, :]

        zz = z_ref[...]
        sg = _sigmoid(zz)
        silu = zz * sg
        yv = y_ref[...]
        y2 = yv * silu
        gw = SSD_INNER // SSD_GROUPS
        for g in range(SSD_GROUPS):
            sl = slice(g * gw, (g + 1) * gw)
            yg = y2[:, sl]
            rg = lax.rsqrt(jnp.mean(yg * yg, axis=1, keepdims=True) + EPS)
            yh = yg * rg
            dyn_g = dyn_ref[:, sl]
            dgn_ref[:, sl] += jnp.sum(dyn_g * yh, axis=0, keepdims=True)
            dyh = dyn_g * gn_ref[:, sl]
            dy2 = rg * (dyh - yh * jnp.mean(dyh * yh, axis=1, keepdims=True))
            dy_s[:, sl] = dy2 * silu[:, sl]
            dz_ref[:, sl] = (dy2 * yv[:, sl] * (sg[:, sl] * (1.0 + zz[:, sl] * (1.0 - sg[:, sl])))).astype(bf16)

        d_acs = jnp.zeros((L, 128), f32)
        ddt_x = jnp.zeros((L, 128), f32)
        dsk_acc = jnp.zeros((1, 128), f32)
        for g in range(SSD_GROUPS):
            bsl = slice(SSD_INNER + g * SSD_N, SSD_INNER + (g + 1) * SSD_N)
            csl = slice(SSD_INNER + (SSD_GROUPS + g) * SSD_N, SSD_INNER + (SSD_GROUPS + g + 1) * SSD_N)
            b_g = xbc_ref[:, bsl].astype(bf16)
            c_g = xbc_ref[:, csl].astype(bf16)
            cb = _dot(c_g, b_g, NT)
            dcb = jnp.zeros((L, L), f32)
            dc_g = jnp.zeros((L, SSD_N), f32)
            db_g = jnp.zeros((L, SSD_N), f32)
            for pr in range(4):
                h = 8 * g + 2 * pr
                pi = h // 2
                cols = slice(pi * 128, (pi + 1) * 128)
                xs = xbc_ref[:, cols]
                dt_p = _pair_vec(lane, dt, h)
                x = xs * dt_p
                acs = _pair_vec(lane, a_cs, h)
                al = _pair_vec(lane, a_last, h)
                e_a = jnp.exp(acs)
                dte = jnp.exp(al - acs)
                m_mat_a = _decay_mat(a_cs, a_cs_t, h, tril)
                m_mat_b = _decay_mat(a_cs, a_cs_t, h + 1, tril)
                dyp = dy_s[:, cols]
                dsk = dsk_ref[:, cols]
                d_hn = dstate[pi]
                hp = hp_ref[0, pi]
                dy_a = (dyp * m_a).astype(bf16)
                dy_b = (dyp * m_b).astype(bf16)
                x_b = x.astype(bf16)
                gm_a = _dot(dy_a, x_b, NT) * m_mat_a
                gm_b = _dot(dy_b, x_b, NT) * m_mat_b
                dcb = dcb + gm_a + gm_b
                dx_d = _dot((cb * m_mat_a).astype(bf16), dy_a, TN) + _dot((cb * m_mat_b).astype(bf16), dy_b, TN)
                dx_s = _dot(b_g, d_hn.astype(bf16), NT) * dte
                dx = dx_d + dx_s
                dxbc_ref[:, cols] = dx * dt_p + dsk * dyp
                xdxs = x * dx_s
                u = dyp * (_dot(c_g, hp.astype(bf16), NT) * e_a) - xdxs
                hh = d_hn * hp
                dec = jnp.exp(jnp.where(row128 < SB_HD, a_last[:, h:h + 1], a_last[:, h + 1:h + 2]))
                for hd, m, gm in ((h, m_a, gm_a), (h + 1, m_b, gm_b)):
                    half = slice(0, SB_HD) if hd == h else slice(SB_HD, 128)
                    last = _sum_all(xdxs * m) + jnp.exp(a_last[:, hd:hd + 1]) * _sum_all(hh[half, :])
                    qm = gm * cb
                    col = jnp.sum(qm - qm.T, axis=1, keepdims=True) + jnp.sum(u * m, axis=1, keepdims=True)
                    col = col + jnp.where(rowl == L - 1, last, 0.0)
                    d_acs = jnp.where(lane == hd, col, d_acs)
                    ddt_x = jnp.where(lane == hd, jnp.sum(dx * xs * m, axis=1, keepdims=True), ddt_x)
                    dsk_acc = jnp.where(lane == hd, _sum_all(dyp * xs * m), dsk_acc)
                dye = (dyp * e_a).astype(bf16)
                dc_g = dc_g + _dot(dye, hp.astype(bf16))
                db_g = db_g + _dot((x * dte).astype(bf16), d_hn.astype(bf16))
                dstate[pi] = dec * d_hn + _dot(dye, c_g, TN)
            dcb_b = dcb.astype(bf16)
            dxbc_ref[:, csl] = dc_g + _dot(dcb_b, b_g)
            dxbc_ref[:, bsl] = db_g + _dot(dcb_b, c_g, TN)

        r_i = lax.broadcasted_iota(jnp.int32, (L, L), 0)
        c_i = lax.broadcasted_iota(jnp.int32, (L, L), 1)
        rev = (r_i <= c_i).astype(bf16)
        dda = sum(_dot(rev, p) for p in _split3(d_acs))
        ddt = ddt_x + dda * a
        dalog_ref[...] += jnp.sum(dda * dt, axis=0, keepdims=True) * a
        ddtr = jnp.where(lane < SSD_HEADS, ddt * _sigmoid(dtr + dtb_ref[...]), 0.0)
        ddt_ref[...] = ddtr.astype(bf16)
        ddtb_ref[...] += jnp.sum(ddtr, axis=0, keepdims=True)
        dsk_out[...] += dsk_acc
        if ride:
            pl.when(c == nc - 1)(lambda: ride.finish(r_ins, r_lnd, r_sems))

    rv = lambda c: nc - 1 - c
    vec128 = pl.BlockSpec((1, 128), lambda c: (0, 0))
    vecin = pl.BlockSpec((1, SSD_INNER), lambda c: (0, 0))
    rows = pl.BlockSpec((L, SSD_INNER), lambda c: (rv(c), 0))
    return pl.pallas_call(
        body, name="ssd_bwd", grid=(nc,),
        in_specs=[rows, rows,
                  pl.BlockSpec((L, CONV_DIM), lambda c: (rv(c), 0)),
                  pl.BlockSpec((L, 128), lambda c: (rv(c), 0)),
                  pl.BlockSpec((L, SSD_INNER), lambda c: (rv(c), P_Z // SSD_INNER)),
                  pl.BlockSpec((1, N_PAIR, 128, SSD_N), lambda c: (rv(c), 0, 0, 0)),
                  vec128, vec128, vecin, vecin] + (ride.in_specs if ride else []),
        out_specs=[rows, pl.BlockSpec((L, CONV_DIM), lambda c: (rv(c), 0)),
                   pl.BlockSpec((L, 128), lambda c: (rv(c), 0)), vecin, vec128, vec128, vec128]
        + (ride.out_specs if ride else []),
        out_shape=[jax.ShapeDtypeStruct((S, SSD_INNER), bf16), jax.ShapeDtypeStruct((S, CONV_DIM), f32),
                   jax.ShapeDtypeStruct((S, 128), bf16), jax.ShapeDtypeStruct((1, SSD_INNER), f32),
                   jax.ShapeDtypeStruct((1, 128), f32), jax.ShapeDtypeStruct((1, 128), f32),
                   jax.ShapeDtypeStruct((1, 128), f32)] + (ride.out_shape if ride else []),
        scratch_shapes=[pltpu.VMEM((N_PAIR, 128, SSD_N), f32), pltpu.VMEM((L, SSD_INNER), f32)]
        + (ride.scratch if ride else []),
        compiler_params=_params(("arbitrary",)),
    )(dyn, y, xbc, pdt, proj, hprev, dt_bias_p, a_log_p, d_skip_c, ssd_norm, *(ride.srcs if ride else []))


MEM_W = MEM_HEADS * MEM_HD


def _mem_probs(q, k):
    s = _dot(q, k, NT) * (MEM_HD ** -0.5)
    s = s - jnp.max(s, axis=1, keepdims=True)
    p = jnp.exp(s)
    return p / jnp.sum(p, axis=1, keepdims=True)


def _mem_fwd(proj, kv, S, tm=512):
    tm = min(tm, S)
    M = kv.shape[0]

    def body(q_ref, kv_ref, o_ref):
        for h in range(MEM_HEADS):
            sl = slice(h * MEM_HD, (h + 1) * MEM_HD)
            vsl = slice(MEM_W + h * MEM_HD, MEM_W + (h + 1) * MEM_HD)
            p = _mem_probs(q_ref[:, sl].astype(bf16), kv_ref[:, sl].astype(bf16))
            o_ref[:, sl] = _dot(p.astype(bf16), kv_ref[:, vsl].astype(bf16)).astype(bf16)

    return pl.pallas_call(
        body, name="mem_fwd", grid=(S // tm,),
        in_specs=[pl.BlockSpec((tm, MEM_W), lambda i: (i, P_MEMQ // MEM_W)),
                  pl.BlockSpec((M, 2 * MEM_W), lambda i: (0, 0))],
        out_specs=pl.BlockSpec((tm, MEM_W), lambda i: (i, 0)),
        out_shape=jax.ShapeDtypeStruct((S, MEM_W), bf16),
        compiler_params=_params(("parallel",)),
    )(proj, kv)


def _mem_bwd(proj, kv, dy, S, tm=512):
    tm = min(tm, S)
    M = kv.shape[0]
    scale = MEM_HD ** -0.5

    def body(q_ref, kv_ref, dy_ref, dq_ref, dkv_ref):
        @pl.when(pl.program_id(0) == 0)
        def _():
            dkv_ref[...] = jnp.zeros_like(dkv_ref)

        for h in range(MEM_HEADS):
            sl = slice(h * MEM_HD, (h + 1) * MEM_HD)
            vsl = slice(MEM_W + h * MEM_HD, MEM_W + (h + 1) * MEM_HD)
            q = q_ref[:, sl].astype(bf16)
            k = kv_ref[:, sl].astype(bf16)
            v = kv_ref[:, vsl].astype(bf16)
            dyh = dy_ref[:, sl].astype(bf16)
            p = _mem_probs(q, k)
            dp = _dot(dyh, v, NT)
            ds = (p * (dp - jnp.sum(dp * p, axis=1, keepdims=True)) * scale).astype(bf16)
            dq_ref[:, sl] = _dot(ds, k).astype(bf16)
            dkv_ref[:, sl] += _dot(ds, q, TN)
            dkv_ref[:, vsl] += _dot(p.astype(bf16), dyh, TN)

    return pl.pallas_call(
        body, name="mem_bwd", grid=(S // tm,),
        in_specs=[pl.BlockSpec((tm, MEM_W), lambda i: (i, P_MEMQ // MEM_W)),
                  pl.BlockSpec((M, 2 * MEM_W), lambda i: (0, 0)),
                  pl.BlockSpec((tm, MEM_W), lambda i: (i, 0))],
        out_specs=[pl.BlockSpec((tm, MEM_W), lambda i: (i, 0)), pl.BlockSpec((M, 2 * MEM_W), lambda i: (0, 0))],
        out_shape=[jax.ShapeDtypeStruct((S, MEM_W), bf16), jax.ShapeDtypeStruct((M, 2 * MEM_W), f32)],
        compiler_params=_params(("arbitrary",)),
    )(proj, kv, dy)


def _merge_fwd(proj, t0, t1, t2, S, tm=512):
    tm = min(tm, S)

    def body(g_ref, t0_ref, t1_ref, t2_ref, o_ref):
        acc = jnp.zeros((tm, D), f32)
        for b, t_ref in enumerate((t0_ref, t1_ref, t2_ref)):
            acc = acc + _sigmoid(g_ref[:, b * D:(b + 1) * D]) * t_ref[...]
        o_ref[...] = acc.astype(bf16)

    row = pl.BlockSpec((tm, D), lambda i: (i, 0))
    return pl.pallas_call(
        body, name="merge_fwd", grid=(S // tm,),
        in_specs=[pl.BlockSpec((tm, 3 * D), lambda i: (i, P_GATE // (3 * D))), row, row, row],
        out_specs=row, out_shape=jax.ShapeDtypeStruct((S, D), bf16),
        compiler_params=_params(("parallel",)),
    )(proj, t0, t1, t2)


def _merge_bwd(proj, t0, t1, t2, dm, S, tm=512):
    tm = min(tm, S)

    def body(g_ref, t0_ref, t1_ref, t2_ref, dm_ref, d0_ref, d1_ref, d2_ref, dg_ref):
        dmv = dm_ref[...]
        for b, (t_ref, d_ref) in enumerate(((t0_ref, d0_ref), (t1_ref, d1_ref), (t2_ref, d2_ref))):
            sg = _sigmoid(g_ref[:, b * D:(b + 1) * D])
            d_ref[...] = (dmv * sg).astype(bf16)
            dg_ref[:, b * D:(b + 1) * D] = (dmv * t_ref[...] * sg * (1.0 - sg)).astype(bf16)

    row = pl.BlockSpec((tm, D), lambda i: (i, 0))
    return pl.pallas_call(
        body, name="merge_bwd", grid=(S // tm,),
        in_specs=[pl.BlockSpec((tm, 3 * D), lambda i: (i, P_GATE // (3 * D))), row, row, row, row],
        out_specs=[row, row, row, pl.BlockSpec((tm, 3 * D), lambda i: (i, 0))],
        out_shape=[jax.ShapeDtypeStruct((S, D), bf16)] * 3 + [jax.ShapeDtypeStruct((S, 3 * D), bf16)],
        compiler_params=_params(("parallel",)),
    )(proj, t0, t1, t2, dm)


def _loss_head(ff, g, h1, target, S, tm=512):
    tm = min(tm, S)

    def body(ff_ref, g_ref, h1_ref, t_ref, dh_ref, loss_ref):
        xv = ff_ref[...]
        r = lax.rsqrt(jnp.mean(xv * xv, axis=1, keepdims=True) + EPS)
        err = h1_ref[...] + xv * r * g_ref[...] - t_ref[...]
        dh_ref[...] = err * (1.0 / D)

        @pl.when(pl.program_id(0) == 0)
        def _():
            loss_ref[...] = jnp.zeros_like(loss_ref)

        loss_ref[...] += 0.5 * _sum_all(jnp.mean(err * err, axis=1, keepdims=True)) * jnp.ones((1, 128), f32)

    row = pl.BlockSpec((tm, D), lambda i: (i, 0))
    return pl.pallas_call(
        body, name="loss_head", grid=(S // tm,),
        in_specs=[row, pl.BlockSpec((1, D), lambda i: (0, 0)), row, row],
        out_specs=[row, pl.BlockSpec((1, 128), lambda i: (0, 0))],
        out_shape=[jax.ShapeDtypeStruct((S, D), f32), jax.ShapeDtypeStruct((1, 128), f32)],
        compiler_params=_params(("arbitrary",)),
    )(ff, g, h1, target)


def _local_step(x, mem, target, wts, late_ride, late_weights, small, rest_ride):
    S = x.shape[0]
    M = mem.shape[0]
    pad = lambda v: jnp.pad(v, ((0, 0), (0, 128 - SSD_HEADS)))
    dtb_p, alog_p = pad(small["dt_bias"]), pad(small["a_log"])
    dsk_c = jnp.repeat(small["d_skip"], SB_HD, axis=1)

    u = _rms_fwd(x, small["norm_mix_pre"], name="norm_pre", out_dtype=bf16)
    proj = _mm(u, wts["w_main"], "nn", tm=1024, tn=1024, name="in_proj")
    pdt = _mm(u, wts["w_dt"], "nn", tm=1024, tn=128, name="in_proj_dt")
    y_sb, tot_lk, lands = _sb_fwd(proj, S, late_ride)
    wts = dict(wts, **late_weights(lands))
    small = dict(small, conv_w=wts.pop("conv_w"))
    xc, xbc = _conv_fwd(proj, small["conv_w"], small["conv_b"], S)
    y_ssd, yn, hprev = _ssd_fwd(xbc, proj, pdt, dtb_p, alog_p, dsk_c, small["ssd_norm"], S)
    mn = _rms_fwd(mem, small["norm_mem"], name="norm_mem", out_dtype=bf16, tm=min(512, M))
    kv = _mm(mn, wts["w_mem_kv"], "nn", tm=M, tn=1024, name="mem_kv")
    y_mem = _mem_fwd(proj, kv, S)
    t0 = _mm(y_sb, wts["w_sb_out"], "nn", tm=1024, tn=1024, name="sb_out")
    t1 = _mm(yn, wts["w_ssd_out"], "nn", tm=1024, tn=1024, name="ssd_out")
    t2 = _mm(y_mem, wts["w_mem_out"], "nn", tm=1024, tn=1024, name="mem_out")
    merged = _merge_fwd(proj, t0, t1, t2, S)
    mix = _mm(merged, wts["w_o"], "nn", tm=1024, tn=1024, name="w_o")
    h1 = _rms_fwd(mix, small["norm_mix_post"], name="norm_mix_post", out_dtype=f32, residual=x)
    u2 = _rms_fwd(h1, small["norm_mlp_pre"], name="norm_mlp_pre", out_dtype=bf16)
    a_up, hrelu = _mm(u2, wts["w_up"], "nn", tm=1024, tn=1024, name="mlp_up", out_dtypes=(f32, bf16),
                      epi=lambda acc: (acc, jnp.square(jnp.maximum(acc, 0.0))))
    ff = _mm(hrelu, wts["w_down"], "nn", tm=1024, tn=1024, name="mlp_down")
    dh2, loss = _loss_head(ff, small["norm_mlp_post"], h1, target, S)

    g = {}
    dff, g["norm_mlp_post"] = _rms_bwd(ff, dh2, small["norm_mlp_post"], name="norm_mlp_post_bwd", out_dtype=bf16)
    da = _mm(dff, wts["w_down"], "nt", tm=1024, tn=1024, name="mlp_down_dx", out_dtypes=(bf16,),
             epi=lambda acc, a: (acc * (2.0 * jnp.maximum(a, 0.0)),), extras=(a_up,))
    g["w_down"] = _mm(hrelu, dff, "tn", tm=1024, tn=1024, name="mlp_down_dw")
    du2 = _mm(da, wts["w_up"], "nt", tm=1024, tn=1024, name="mlp_up_dx")
    g["w_up"] = _mm(u2, da, "tn", tm=1024, tn=1024, name="mlp_up_dw")
    dh1, g["norm_mlp_pre"] = _rms_bwd(h1, du2, small["norm_mlp_pre"], name="norm_mlp_pre_bwd", out_dtype=f32, add=dh2)
    dmix, g["norm_mix_post"] = _rms_bwd(mix, dh1, small["norm_mix_post"], name="norm_mix_post_bwd", out_dtype=bf16)
    dmerged = _mm(dmix, wts["w_o"], "nt", tm=1024, tn=1024, name="w_o_dx")
    g["w_o"] = _mm(merged, dmix, "tn", tm=1024, tn=1024, name="w_o_dw")
    dt0, dt1, dt2, dgl = _merge_bwd(proj, t0, t1, t2, dmerged, S)
    dy_sb = _mm(dt0, wts["w_sb_out"], "nt", tm=1024, tn=1024, name="sb_out_dx")
    g["w_sb_out"] = _mm(y_sb, dt0, "tn", tm=1024, tn=1024, name="sb_out_dw")
    dy_ssd = _mm(dt1, wts["w_ssd_out"], "nt", tm=1024, tn=1024, name="ssd_out_dx")
    g["w_ssd_out"] = _mm(yn, dt1, "tn", tm=1024, tn=1024, name="ssd_out_dw")
    dy_mem = _mm(dt2, wts["w_mem_out"], "nt", tm=1024, tn=1024, name="mem_out_dx")
    g["w_mem_out"] = _mm(y_mem, dt2, "tn", tm=1024, tn=1024, name="mem_out_dw")
    dmemq, dkv = _mem_bwd(proj, kv, dy_mem, S)
    g["w_mem_kv"] = _mm(mn, dkv, "tn", tm=1024, tn=1024, name="mem_kv_dw")
    dmn = _mm(dkv, wts["w_mem_kv"], "nt", tm=M, tn=1024, name="mem_kv_dx")
    _, g["norm_mem"] = _rms_bwd(mem, dmn, small["norm_mem"], name="norm_mem_bwd", out_dtype=bf16, tm=min(512, M))
    dz, dxbc, ddt, g["ssd_norm"], dsk, dalog, ddtb, *g["rest_lands"] = _ssd_bwd(
        dy_ssd, y_ssd, xbc, proj, pdt, hprev, dtb_p, alog_p, dsk_c, small["ssd_norm"], S,
        rest_ride(g) if rest_ride else None)
    g["d_skip"], g["a_log"], g["dt_bias"] = dsk[:, :SSD_HEADS], dalog[:, :SSD_HEADS], ddtb[:, :SSD_HEADS]
    dxbc_raw, dcw, g["conv_b"] = _conv_bwd(proj, xc, dxbc, small["conv_w"], S)
    g["conv_w"] = dcw[:CONV_K]
    dq, dk, dv = _sb_bwd(proj, tot_lk, dy_sb, S)
    dproj = jnp.concatenate([dq, dk, dv, dxbc_raw, dgl, dmemq, dz], axis=1)
    du_dt = _mm(ddt, wts["w_dt"], "nt", tm=1024, tn=1024, name="in_proj_dt_dx")
    du = _mm(dproj, wts["w_main"], "nt", tm=512, tn=256, name="in_proj_dx",
             epi=lambda acc, e: (acc + e,), extras=(du_dt,))
    g["w_main"] = _mm(u, dproj, "tn", tm=1024, tn=1024, name="in_proj_dw")
    g["w_dt"] = _mm(u, ddt, "tn", tm=1024, tn=128, name="in_proj_dt_dw")
    grad_x, g["norm_mix_pre"] = _rms_bwd(x, du, small["norm_mix_pre"], name="norm_pre_bwd", out_dtype=f32, add=dh1)
    return loss, grad_x, g


def _to_internal(w_in):
    sec = lambda r: w_in[:, r[0]:r[1]]
    w_main = jnp.concatenate([sec(R_QKV), sec(R_XBC), sec(R_GATE), sec(R_MEMQ), sec(R_Z)], axis=1)
    w_dt = jnp.pad(sec(R_DT), ((0, 0), (0, 128 - SSD_HEADS)))
    return w_main, w_dt


def _from_internal(g_main, g_dt):
    sec = lambda p, n: g_main[:, p:p + n]
    return jnp.concatenate([sec(P_QKV, 3072), sec(P_Z, 2048), sec(P_XBC, 3072), g_dt[:, :SSD_HEADS],
                            sec(P_MEMQ, 1024), sec(P_GATE, 3072)], axis=1)


MESH = pl.DeviceIdType.MESH
ANY = pl.BlockSpec(memory_space=pl.ANY)


def _place():
    x, y, c = lax.axis_index("x"), lax.axis_index("y"), lax.axis_index("c")
    return (x, y, c), [(1 - x, y, c), (x, 1 - y, c), (1 - x, 1 - y, c)]


def _exchange_copy(mode, ins, lands, send, recv, a, k, me, peers, arriving):
    p = peers[k]
    theirs = 2 * p[0] + p[1]
    if mode == "gather":
        src, dst = ins[a], lands[a].at[theirs if arriving else me]
    else:
        src, dst = ins[a].at[theirs], lands[a].at[k]
    return pltpu.make_async_remote_copy(src_ref=src, dst_ref=dst, send_sem=send.at[a * 3 + k],
                                        recv_sem=recv.at[a * 3 + k], device_id=p, device_id_type=MESH)


class _Ride:
    def __init__(self, srcs, mode):
        self.srcs, self.mode, self.n = list(srcs), mode, len(srcs)
        n = self.n
        self.in_specs, self.out_specs = [ANY] * n, [ANY] * n
        self.out_shape = [
            jax.ShapeDtypeStruct((N_SHARD,) + s.shape if mode == "gather" else (3,) + s.shape[1:], s.dtype)
            for s in self.srcs]
        self.scratch = [pltpu.SemaphoreType.DMA((3 * n,)), pltpu.SemaphoreType.DMA((3 * n,)),
                        pltpu.SemaphoreType.DMA((n,))]

    def _own(self, ins, lnd, sems):
        if self.mode != "gather":
            return []
        me = 2 * lax.axis_index("x") + lax.axis_index("y")
        return [pltpu.make_async_copy(ins[a], lnd[a].at[me], sems[2].at[a]) for a in range(self.n)]

    def _far(self, ins, lnd, sems, arriving):
        (x, y, c), peers = _place()
        return [_exchange_copy(self.mode, ins, lnd, sems[0], sems[1], a, k, 2 * x + y, peers, arriving)
                for a in range(self.n) for k in range(3)]

    def start(self, ins, lnd, sems):
        for cp in self._own(ins, lnd, sems) + self._far(ins, lnd, sems, False):
            cp.start()

    def finish(self, ins, lnd, sems):
        for cp in self._far(ins, lnd, sems, True):
            cp.wait_recv()
        for cp in self._far(ins, lnd, sems, False):
            cp.wait_send()
        for cp in self._own(ins, lnd, sems):
            cp.wait()


def _exchange(srcs, mode, name):
    ride = _Ride(srcs, mode)
    n = ride.n

    def body(*refs):
        ride.start(refs[:n], refs[n:2 * n], refs[2 * n:])
        ride.finish(refs[:n], refs[n:2 * n], refs[2 * n:])

    return pl.pallas_call(body, name=name, in_specs=ride.in_specs, out_specs=ride.out_specs,
                          out_shape=ride.out_shape, scratch_shapes=ride.scratch)(*srcs)


def _gather_two_level(shards, name):
    n = len(shards)

    def body(*refs):
        ins, lnd = refs[:n], refs[n:2 * n]
        send, recv, loc = refs[2 * n:]
        (x, y, c), peers = _place()
        me = 2 * x + y

        def half(ref, a, core):
            rows = shards[a].shape[0] // 2
            return ref.at[pl.ds(core * rows, rows)]

        def copy(a, j, slot, core, to):
            return pltpu.make_async_remote_copy(
                src_ref=half(ins[a], a, core) if j < 3 else half(lnd[a].at[slot], a, core),
                dst_ref=half(lnd[a].at[slot], a, core), send_sem=send.at[6 * a + j], recv_sem=recv.at[6 * a + j],
                device_id=to, device_id_type=MESH)

        own = [pltpu.make_async_copy(ins[a], lnd[a].at[me], loc.at[a]) for a in range(n)]
        far = [copy(a, k, me, c, peers[k]) for a in range(n) for k in range(3)]
        for cp in own + far:
            cp.start()
        passed = []
        for a in range(n):
            for k, p in enumerate(peers):
                theirs = 2 * p[0] + p[1]
                copy(a, k, theirs, c, p).wait_recv()
                passed.append(copy(a, 3 + k, theirs, c, (x, y, 1 - c)))
                passed[-1].start()
        for a in range(n):
            for k, p in enumerate(peers):
                copy(a, 3 + k, 2 * p[0] + p[1], 1 - c, (x, y, 1 - c)).wait_recv()
        for cp in far + passed:
            cp.wait_send()
        for cp in own:
            cp.wait()

    return pl.pallas_call(
        body, name=name, in_specs=[ANY] * n, out_specs=[ANY] * n,
        out_shape=[jax.ShapeDtypeStruct((N_SHARD,) + s.shape, s.dtype) for s in shards],
        scratch_shapes=[pltpu.SemaphoreType.DMA((6 * n,)), pltpu.SemaphoreType.DMA((6 * n,)),
                        pltpu.SemaphoreType.DMA((n,))],
    )(*shards)


def _exchange_packets(packet):
    def body(pk, pk_out, send, recv, loc):
        x, y, c = lax.axis_index("x"), lax.axis_index("y"), lax.axis_index("c")
        lin = 4 * x + 2 * y + c
        own = pltpu.make_async_copy(pk, pk_out.at[lin], loc.at[0])
        own.start()

        def pk_copy(m, slot):
            dev = (x ^ ((m >> 2) & 1), y ^ ((m >> 1) & 1), c ^ (m & 1))
            return pltpu.make_async_remote_copy(
                src_ref=pk, dst_ref=pk_out.at[slot], send_sem=send.at[m - 1], recv_sem=recv.at[m - 1],
                device_id=dev, device_id_type=MESH)

        sent = [pk_copy(m, lin) for m in range(1, N_DEV)]
        for cp in sent:
            cp.start()
        for m in range(1, N_DEV):
            pk_copy(m, lin ^ m).wait_recv()
        for cp in sent:
            cp.wait_send()
        own.wait()

    return pl.pallas_call(
        body, name="exchange_packets", in_specs=[ANY], out_specs=ANY,
        out_shape=jax.ShapeDtypeStruct((N_DEV,) + packet.shape, packet.dtype),
        scratch_shapes=[pltpu.SemaphoreType.DMA((N_DEV - 1,)), pltpu.SemaphoreType.DMA((N_DEV - 1,)),
                        pltpu.SemaphoreType.DMA((1,))],
    )(packet)


def _swap_sibling(parts, name):
    n = len(parts)

    def body(*refs):
        ins, outs = refs[:n], refs[n:2 * n]
        send, recv = refs[2 * n:]
        x, y, c = lax.axis_index("x"), lax.axis_index("y"), lax.axis_index("c")
        cps = [pltpu.make_async_remote_copy(
            src_ref=ins[a], dst_ref=outs[a], send_sem=send.at[a], recv_sem=recv.at[a],
            device_id=(x, y, 1 - c), device_id_type=MESH) for a in range(n)]
        for cp in cps:
            cp.start()
        for cp in cps:
            cp.wait_recv()
        for cp in cps:
            cp.wait_send()

    return pl.pallas_call(
        body, name=name,
        in_specs=[ANY] * n, out_specs=[ANY] * n,
        out_shape=[jax.ShapeDtypeStruct(p.shape, p.dtype) for p in parts],
        scratch_shapes=[pltpu.SemaphoreType.DMA((n,)), pltpu.SemaphoreType.DMA((n,))],
    )(*parts)


BLOCK_ELEMS = 256 * 1024


def _row_tile(R, C):
    tr = max(8, (BLOCK_ELEMS // C) // 8 * 8)
    while R % tr:
        tr -= 8
    return min(tr, R)


def _sum_parts(own, stack, name):
    k = stack.shape[0]
    R, C = stack.shape[1:]
    tr = _row_tile(R, C)

    def body(*refs):
        o_ref = refs[-1]
        acc = refs[0][...]
        for r in refs[1:-1]:
            acc = acc + r[...]
        o_ref[...] = acc

    row = pl.BlockSpec((tr, C), lambda i: (i, 0))
    specs = ([row] if own is not None else []) + [
        pl.BlockSpec((None, tr, C), functools.partial(lambda i, j: (j, i, 0), j=j)) for j in range(k)]
    args = ([own] if own is not None else []) + [stack] * k
    return pl.pallas_call(
        body, name=name, grid=(R // tr,), in_specs=specs, out_specs=row,
        out_shape=jax.ShapeDtypeStruct((R, C), f32), compiler_params=_params(("parallel",)),
    )(*args)


def _adamw(w, m, v, g_parts, name):
    R, C = w.shape
    tr = _row_tile(R, C)
    n_g = len(g_parts)

    def body(w_ref, m_ref, v_ref, *rest):
        g = rest[0][...]
        for r in rest[1:n_g]:
            g = g + r[...]
        g_ref, d_ref, nm_ref, nv_ref = rest[n_g:]
        nm = ADAM_B1 * m_ref[...] + (1.0 - ADAM_B1) * g
        nv = ADAM_B2 * v_ref[...] + (1.0 - ADAM_B2) * jnp.square(g)
        m_hat = nm / (1.0 - ADAM_B1 ** ADAM_STEP)
        v_hat = nv / (1.0 - ADAM_B2 ** ADAM_STEP)
        g_ref[...] = g
        d_ref[...] = -ADAM_LR * (m_hat / (jnp.sqrt(v_hat) + ADAM_EPS) + ADAM_WD * w_ref[...])
        nm_ref[...] = nm
        nv_ref[...] = nv

    row = pl.BlockSpec((tr, C), lambda i: (i, 0))
    return pl.pallas_call(
        body, name=name, grid=(R // tr,), in_specs=[row] * (3 + n_g), out_specs=[row] * 4,
        out_shape=[jax.ShapeDtypeStruct((R, C), f32)] * 4, compiler_params=_params(("parallel",)),
    )(w, m, v, *g_parts)


BIG = ("w_in", "w_mem_kv", "w_sb_out", "w_ssd_out", "w_mem_out", "w_o", "w_up", "w_down")
FIRST = ("w_in", "w_mem_kv")
LATE = ("w_sb_out", "w_ssd_out", "w_mem_out", "w_o", "w_up", "w_down")
REST = BIG[1:]
COL_SHARDED = ("w_in", "w_mem_kv", "w_up")
SMALL = ("norm_mix_pre", "conv_w", "conv_b", "dt_bias", "a_log", "d_skip", "ssd_norm", "norm_mem",
         "norm_mix_post", "norm_mlp_pre", "norm_mlp_post")
WEIGHTS = ("norm_mix_pre", "w_in", "conv_w", "conv_b", "dt_bias", "a_log", "d_skip", "ssd_norm", "norm_mem",
           "w_mem_kv", "w_sb_out", "w_ssd_out", "w_mem_out", "w_o", "norm_mix_post", "norm_mlp_pre", "w_up",
           "w_down", "norm_mlp_post")
PK_ROWS = 184


def _pack(vecs):
    flat = jnp.concatenate([v.reshape(-1) for v in vecs])
    return jnp.pad(flat, (0, PK_ROWS * 128 - flat.shape[0])).reshape(PK_ROWS, 128)


def _unpack(pk, shapes):
    flat = pk.reshape(-1)
    out, off = [], 0
    for s in shapes:
        n = 1
        for d in s:
            n *= d
        out.append(flat[off:off + n].reshape(s))
        off += n
    return out


def _full_from_slabs(name, slabs):
    if name in COL_SHARDED:
        return slabs.transpose(1, 0, 2).reshape(slabs.shape[1], -1)
    return slabs.reshape(-1, slabs.shape[2])


def _slabs_from_full(name, g):
    if name in COL_SHARDED:
        return g.reshape(g.shape[0], N_SHARD, -1).transpose(1, 0, 2)
    return g.reshape(N_SHARD, -1, g.shape[1])


def kernel(x, mem, norm_mix_pre, w_in, conv_w, conv_b, dt_bias, a_log, d_skip, ssd_norm, norm_mem, w_mem_kv, w_sb_out, w_ssd_out, w_mem_out, w_o, norm_mix_post, norm_mlp_pre, w_up, w_down, norm_mlp_post, loss_target, m_norm_mix_pre, m_w_in, m_conv_w, m_conv_b, m_dt_bias, m_a_log, m_d_skip, m_ssd_norm, m_norm_mem, m_w_mem_kv, m_w_sb_out, m_w_ssd_out, m_w_mem_out, m_w_o, m_norm_mix_post, m_norm_mlp_pre, m_w_up, m_w_down, m_norm_mlp_post, v_norm_mix_pre, v_w_in, v_conv_w, v_conv_b, v_dt_bias, v_a_log, v_d_skip, v_ssd_norm, v_norm_mem, v_w_mem_kv, v_w_sb_out, v_w_ssd_out, v_w_mem_out, v_w_o, v_norm_mix_post, v_norm_mlp_pre, v_w_up, v_w_down, v_norm_mlp_post):
    env = dict(locals())
    w = {n: env[n] for n in WEIGHTS}
    mo = {n: env["m_" + n] for n in WEIGHTS}
    vo = {n: env["v_" + n] for n in WEIGHTS}
    shard = 2 * lax.axis_index("x") + lax.axis_index("y")

    first = _gather_two_level([w[n][0].astype(bf16) for n in FIRST], "gather_first")
    w_main, w_dt = _to_internal(_full_from_slabs("w_in", first[0]))
    wts = dict(w_main=w_main, w_dt=w_dt, w_mem_kv=_full_from_slabs("w_mem_kv", first[1]))
    late_ride = _Ride([w[n][0].astype(bf16) for n in LATE] + [w["conv_w"][0]], "gather")

    def late_weights(lands):
        full = {n: _full_from_slabs(n, s) for n, s in zip(LATE, lands)}
        return dict(full, conv_w=lands[-1].transpose(1, 0, 2).reshape(CONV_K, CONV_DIM))

    small = {n: w[n] for n in SMALL if n != "conv_w"}
    loss, grad_x, g = _local_step(
        x[0], mem[0], loss_target[0], wts, late_ride, late_weights, small,
        lambda g: _Ride([_slabs_from_full(n, g[n]).astype(bf16) for n in REST], "scatter"))
    g["w_in"] = _from_internal(g.pop("w_main"), g.pop("w_dt"))

    lands = list(_exchange([_slabs_from_full("w_in", g["w_in"]).astype(bf16)], "scatter", "scatter_w_in"))
    lands += g["rest_lands"]
    packets = _exchange_packets(_pack([g[n] for n in SMALL] + [loss[:, :1]]))
    partial = []
    for n, r in zip(BIG, lands):
        own = lax.dynamic_index_in_dim(_slabs_from_full(n, g[n]), shard, 0, keepdims=False)
        partial.append(_sum_parts(own, r, name="sum_chips_" + n))
    other = _swap_sibling(partial, "swap_sibling")

    out_g, out_d, out_m, out_v = {}, {}, {}, {}
    for n, p, q in zip(BIG, partial, other):
        res = _adamw(w[n][0], mo[n][0], vo[n][0], [p, q], name="adamw_" + n)
        out_g[n], out_d[n], out_m[n], out_v[n] = [r[None] for r in res]
    tot = _sum_parts(None, packets, name="sum_packets")
    shapes = [g[n].shape for n in SMALL] + [(1, 1)]
    sm = dict(zip(SMALL + ("loss",), _unpack(tot, shapes)))
    sm["conv_w"] = lax.dynamic_slice_in_dim(sm["conv_w"], shard * (CONV_DIM // N_SHARD), CONV_DIM // N_SHARD, axis=1)
    own_small = lambda d: _pack([d[n].reshape(sm[n].shape) for n in SMALL])
    res = _adamw(own_small(w), own_small(mo), own_small(vo), [own_small(sm)], name="adamw_small")
    own_shapes = [sm[n].shape for n in SMALL]
    for store, r in zip((out_g, out_d, out_m, out_v), res):
        for n, val in zip(SMALL, _unpack(r, own_shapes)):
            store[n] = val.reshape(w[n].shape)

    outs = [sm["loss"].reshape(()), grad_x[None]]
    for store in (out_g, out_d, out_m, out_v):
        outs += [store[n] for n in WEIGHTS]
    return tuple(outs)
```

```python
import functools

import jax
import jax.numpy as jnp
from jax import lax
from jax.experimental import pallas as pl
from jax.experimental.pallas import tpu as pltpu

f32 = jnp.float32
bf16 = jnp.bfloat16

D = 1024
EPS = 1e-6
SB_HD = 64
SSD_INNER = 2048
SSD_HEADS = 32
SSD_GROUPS = 4
SSD_N = 128
SSD_L = 128
CONV_K = 4
CONV_DIM = 3072
MEM_HEADS = 4
MEM_HD = 256
D_FF = 4096
D_IN = 12320
N_SHARD = 4
N_DEV = 8

P_QKV, P_XBC, P_GATE, P_MEMQ, P_Z, P_DT, P_TOT = 0, 3072, 6144, 9216, 10240, 12288, 12416
R_QKV, R_Z, R_XBC, R_DT, R_MEMQ, R_GATE = (0, 3072), (3072, 5120), (5120, 8192), (8192, 8224), (8224, 9248), (9248, 12320)

ADAM_LR = 0.001
ADAM_B1 = 0.9
ADAM_B2 = 0.999
ADAM_EPS = 1e-08
ADAM_WD = 0.01
ADAM_STEP = 10

VMEM_LIMIT = 56 * 1024 * 1024

NN = (((1,), (0,)), ((), ()))
NT = (((1,), (1,)), ((), ()))
TN = (((0,), (0,)), ((), ()))


def _dot(a, b, dims=NN):
    return lax.dot_general(a, b, dims, preferred_element_type=f32)


def _params(sem=None):
    return pltpu.CompilerParams(dimension_semantics=sem, vmem_limit_bytes=VMEM_LIMIT)


def _sigmoid(x):
    return 1.0 / (1.0 + jnp.exp(-x))


def _split2(x):
    hi = x.astype(bf16)
    lo = (x - hi.astype(f32)).astype(bf16)
    return hi, lo


def _split3(x):
    hi = x.astype(bf16)
    r = x - hi.astype(f32)
    mid = r.astype(bf16)
    lo = (r - mid.astype(f32)).astype(bf16)
    return hi, mid, lo


def _mm(a, b, mode, *, tm, tn, name, out_dtypes=(f32,), epi=None, extras=()):
    M = a.shape[1] if mode == "tn" else a.shape[0]
    N = b.shape[0] if mode == "nt" else b.shape[1]
    tm, tn = min(tm, M), min(tn, N)
    if mode == "nn":
        (M, K), N = a.shape, b.shape[1]
        a_spec = pl.BlockSpec((tm, K), lambda i, j: (i, 0))
        b_spec = pl.BlockSpec((K, tn), lambda i, j: (0, j))
        dims = NN
    elif mode == "nt":
        (M, K), N = a.shape, b.shape[0]
        a_spec = pl.BlockSpec((tm, K), lambda i, j: (i, 0))
        b_spec = pl.BlockSpec((tn, K), lambda i, j: (j, 0))
        dims = NT
    else:
        (K, M), N = a.shape, b.shape[1]
        a_spec = pl.BlockSpec((K, tm), lambda i, j: (0, i))
        b_spec = pl.BlockSpec((K, tn), lambda i, j: (0, j))
        dims = TN
    assert M % tm == 0 and N % tn == 0, (name, M, N, tm, tn)
    n_ex = len(extras)
    o_spec = pl.BlockSpec((tm, tn), lambda i, j: (i, j))

    def body(a_ref, b_ref, *rest):
        acc = _dot(a_ref[...].astype(bf16), b_ref[...].astype(bf16), dims)
        res = (acc,) if epi is None else epi(acc, *[e[...] for e in rest[:n_ex]])
        for o_ref, r in zip(rest[n_ex:], res):
            o_ref[...] = r.astype(o_ref.dtype)

    out = pl.pallas_call(
        body, name=name, grid=(M // tm, N // tn),
        in_specs=[a_spec, b_spec] + [o_spec] * n_ex,
        out_specs=[o_spec] * len(out_dtypes),
        out_shape=[jax.ShapeDtypeStruct((M, N), dt) for dt in out_dtypes],
        compiler_params=_params(("parallel", "parallel")),
    )(a, b, *extras)
    return out[0] if len(out_dtypes) == 1 else out


def _rms_fwd(x, g, *, name, out_dtype, residual=None, tm=512):
    S, C = x.shape
    tm = min(tm, S)
    has_res = residual is not None

    def body(x_ref, g_ref, *rest):
        xv = x_ref[...]
        r = lax.rsqrt(jnp.mean(xv * xv, axis=1, keepdims=True) + EPS)
        y = xv * r * g_ref[...]
        if has_res:
            y = y + rest[0][...]
        rest[-1][...] = y.astype(out_dtype)

    row = pl.BlockSpec((tm, C), lambda i: (i, 0))
    vec = pl.BlockSpec((1, C), lambda i: (0, 0))
    args = (x, g) + ((residual,) if has_res else ())
    return pl.pallas_call(
        body, name=name, grid=(S // tm,),
        in_specs=[row, vec] + ([row] if has_res else []),
        out_specs=row, out_shape=jax.ShapeDtypeStruct((S, C), out_dtype),
        compiler_params=_params(("parallel",)),
    )(*args)


def _rms_bwd(x, dy, g, *, name, out_dtype, add=None, tm=512):
    S, C = x.shape
    tm = min(tm, S)
    has_add = add is not None

    def body(x_ref, dy_ref, g_ref, *rest):
        dx_ref, dg_ref = rest[-2], rest[-1]
        xv = x_ref[...]
        dyv = dy_ref[...].astype(f32)
        r = lax.rsqrt(jnp.mean(xv * xv, axis=1, keepdims=True) + EPS)
        xh = xv * r
        dxh = dyv * g_ref[...]
        dx = r * (dxh - xh * jnp.mean(dxh * xh, axis=1, keepdims=True))
        if has_add:
            dx = dx + rest[0][...]
        dx_ref[...] = dx.astype(out_dtype)

        @pl.when(pl.program_id(0) == 0)
        def _():
            dg_ref[...] = jnp.zeros_like(dg_ref)

        dg_ref[...] += jnp.sum(dyv * xh, axis=0, keepdims=True)

    row = pl.BlockSpec((tm, C), lambda i: (i, 0))
    vec = pl.BlockSpec((1, C), lambda i: (0, 0))
    args = (x, dy, g) + ((add,) if has_add else ())
    return pl.pallas_call(
        body, name=name, grid=(S // tm,),
        in_specs=[row, row, vec] + ([row] if has_add else []),
        out_specs=[row, vec],
        out_shape=[jax.ShapeDtypeStruct((S, C), out_dtype), jax.ShapeDtypeStruct((1, C), f32)],
        compiler_params=_params(("arbitrary",)),
    )(*args)


SB_T = 128
SB_SPENT = -120.0
SB_TAIL = 4
SB_GROUPS = (4, 2, 1)
SB_GROUPS_BWD = (4, 2, 1)


def _sb_masks():
    lane = lax.broadcasted_iota(jnp.int32, (1, 128), 1)
    m_a = (lane < SB_HD).astype(f32)
    return m_a, 1.0 - m_a


def _chunks(a, n):
    return [a[:, u * SB_T:(u + 1) * SB_T] for u in range(n)]


def _cat(parts, axis):
    return parts[0] if len(parts) == 1 else jnp.concatenate(parts, axis=axis)


def _mask_last(a, n, mask):
    if mask is None:
        return a
    parts = _chunks(a, n)
    return _cat(parts[:-1] + [jnp.where(mask, parts[-1], 0.0)], 1)


def _sb_logits(z, n, mask):
    l1p = jnp.log(1.0 + jnp.exp(-jnp.abs(z)))
    lb = jnp.minimum(z, 0.0) - l1p
    return lb, _mask_last(lb - z, n, mask)


def _by_count(i, most, fn):
    return lax.switch(jnp.minimum(i, most - 1), [functools.partial(fn, n) for n in range(1, most + 1)])


def _chunk_matmul(parts_list, u_mat):
    out = _dot(_cat(parts_list, 0), u_mat)
    return [out[u * SB_T:(u + 1) * SB_T] for u in range(len(parts_list))]


def _chunk_cumsum(lk, n, u_mat):
    hi = lk.astype(bf16)
    lo = (lk - hi.astype(f32)).astype(bf16)
    out = _chunk_matmul(_chunks(hi, n) + _chunks(lo, n), u_mat)
    return [out[u] + out[n + u] for u in range(n)]


def _sb_fwd(proj, S, ride=None):
    nq = S // SB_T
    n_pairs = D // 128
    scale = SB_HD ** -0.5
    n_r = ride.n if ride else 0

    def body(q_ref, k_ref, v_ref, *rest):
        o_ref, t_ref = rest[n_r:n_r + 2]
        i = pl.program_id(1)
        if ride:
            pl.when((pl.program_id(0) == 0) & (i == 0))(
                lambda: ride.start(rest[:n_r], rest[n_r + 2:2 * n_r + 2], rest[2 * n_r + 2:]))
        m_a, m_b = _sb_masks()
        r_i = lax.broadcasted_iota(jnp.int32, (SB_T, SB_T), 0)
        c_i = lax.broadcasted_iota(jnp.int32, (SB_T, SB_T), 1)
        u_mat = (r_i > c_i).astype(bf16)
        causal = c_i < r_i
        q = q_ref[...] * scale
        q_h = ((q * m_a).astype(bf16), (q * m_b).astype(bf16))

        def group(j_lo, n, carry, mask):
            acc, c_a, c_b = carry
            rows = pl.ds(pl.multiple_of(j_lo * SB_T, SB_T), n * SB_T)
            k = k_ref[rows, :].astype(bf16)
            v = v_ref[rows, :]
            zs = [_dot(q_b, k, NT) for q_b in q_h]
            lbk = [_sb_logits(z, n, mask) for z in zs]
            parts = [_chunk_cumsum(lk, n, u_mat) for _, lk in lbk]
            ws, cs = [], []
            for (lb, lk), part, c in zip(lbk, parts, (c_a, c_b)):
                lb_c, lk_c = _chunks(lb, n), _chunks(lk, n)
                w_c = [None] * n
                for u in reversed(range(n)):
                    w_c[u] = jnp.exp(lb_c[u] + c + part[u])
                    c = c + jnp.sum(lk_c[u], axis=1, keepdims=True)
                ws.append(_mask_last(_cat(w_c, 1), n, mask).astype(bf16))
                cs.append(c)
            for w, m in zip(ws, (m_a, m_b)):
                acc = acc + _dot(w, (v * m).astype(bf16))
            return acc, cs[0], cs[1]

        zero_c = jnp.zeros((SB_T, 1), f32)
        init = (jnp.zeros((SB_T, 128), f32), zero_c, zero_c)
        carry = _by_count(i, SB_TAIL, lambda n: group(i - n + 1, n, init, causal))

        def spent(cr):
            return (jnp.max(jnp.maximum(cr[1], cr[2])) < SB_SPENT).astype(jnp.int32)

        state = (i - jnp.minimum(i, SB_TAIL - 1), spent(carry), carry)
        for n in SB_GROUPS:
            def step(st, n=n):
                left, _, cr = st
                cr = group(left - n, n, cr, None)
                return left - n, spent(cr), cr

            state = lax.while_loop(lambda st, n=n: (st[0] >= n) & (st[1] == 0), step, state)
        left, _, carry = state
        o_ref[...] = carry[0]
        lane = lax.broadcasted_iota(jnp.int32, (1, 128), 1)
        t_ref[...] = (jnp.where(lane == 0, carry[1], 0.0) + jnp.where(lane == SB_HD, carry[2], 0.0)
                      + jnp.where(lane == 1, left.astype(f32), 0.0))
        if ride:
            pl.when((pl.program_id(0) == n_pairs - 1) & (i == nq - 1))(
                lambda: ride.finish(rest[:n_r], rest[n_r + 2:2 * n_r + 2], rest[2 * n_r + 2:]))

    qs = pl.BlockSpec((SB_T, 128), lambda h, i: (i, h))
    out = pl.pallas_call(
        body, name="sb_fwd", grid=(n_pairs, nq),
        in_specs=[qs,
                  pl.BlockSpec((S, 128), lambda h, i: (0, n_pairs + h)),
                  pl.BlockSpec((S, 128), lambda h, i: (0, 2 * n_pairs + h))] + (ride.in_specs if ride else []),
        out_specs=[qs, qs] + (ride.out_specs if ride else []),
        out_shape=[jax.ShapeDtypeStruct((S, D), f32)] * 2 + (ride.out_shape if ride else []),
        scratch_shapes=ride.scratch if ride else [],
        compiler_params=_params(("arbitrary", "arbitrary")),
    )(proj, proj, proj, *(ride.srcs if ride else []))
    return out[0], out[1], list(out[2:])


def _sb_bwd(proj, tot_lk, do, S):
    nq = S // SB_T
    n_pairs = D // 128
    scale = SB_HD ** -0.5

    def body(q_ref, k_ref, v_ref, t_ref, do_ref, dq_ref, dk_ref, dv_ref, dk_acc, dv_acc):
        i = pl.program_id(1)
        m_a, m_b = _sb_masks()
        r_i = lax.broadcasted_iota(jnp.int32, (SB_T, SB_T), 0)
        c_i = lax.broadcasted_iota(jnp.int32, (SB_T, SB_T), 1)
        u_inc = (r_i <= c_i).astype(bf16)
        u_exc = (r_i < c_i).astype(bf16)
        causal = c_i < r_i

        @pl.when(i == 0)
        def _():
            dk_acc[...] = jnp.zeros_like(dk_acc)
            dv_acc[...] = jnp.zeros_like(dv_acc)

        q = q_ref[...] * scale
        dov = do_ref[...]
        tv = t_ref[...]
        lane = lax.broadcasted_iota(jnp.int32, (1, 128), 1)
        heads = []
        for m, first in ((m_a, 0), (m_b, SB_HD)):
            tot = jnp.sum(jnp.where(lane == first, tv, 0.0), axis=1, keepdims=True)
            heads.append(((q * m).astype(bf16), (dov * m).astype(bf16), tot, m))
        lowest = jnp.clip(jnp.max(jnp.where(lane == 1, tv, 0.0)).astype(jnp.int32), 0, i)

        def group(j_lo, n, carry, mask):
            dq_acc, cp_a, cp_b, ce_a, ce_b = carry
            rows = pl.ds(pl.multiple_of(j_lo * SB_T, SB_T), n * SB_T)
            k_f = k_ref[rows, :]
            k = k_f.astype(bf16)
            v = v_ref[rows, :].astype(bf16)
            zs = [_dot(h[0], k, NT) for h in heads]
            dws = [_dot(h[1], v, NT) for h in heads]
            lbk = [_sb_logits(z, n, mask) for z in zs]
            parts = [_chunk_cumsum(lk, n, u_inc) for _, lk in lbk]
            ws, es, cps = [], [], []
            for (lb, lk), part, dw, h, cp in zip(lbk, parts, dws, heads, (cp_a, cp_b)):
                lb_c, lk_c = _chunks(lb, n), _chunks(lk, n)
                w_c = []
                for u in range(n):
                    w_c.append(jnp.exp(lb_c[u] + (h[2] - cp) - part[u]))
                    cp = cp + jnp.sum(lk_c[u], axis=1, keepdims=True)
                w = _mask_last(_cat(w_c, 1), n, mask)
                ws.append(w)
                es.append(dw * w)
                cps.append(cp)
            e_parts = [_chunk_matmul(_chunks(e.astype(bf16), n), u_exc) for e in es]
            dzs, ces = [], []
            for (lb, _), e, e_part, ce in zip(lbk, es, e_parts, (ce_a, ce_b)):
                e_c = _chunks(e, n)
                big_c = []
                for u in range(n):
                    big_c.append(ce + e_part[u])
                    ce = ce + jnp.sum(e_c[u], axis=1, keepdims=True)
                sig = jnp.exp(lb)
                dz = _mask_last(e * (1.0 - sig) - _cat(big_c, 1) * sig, n, mask)
                dzs.append(dz.astype(bf16))
                ces.append(ce)
            dk_t = jnp.zeros((n * SB_T, 128), f32)
            dv_t = jnp.zeros((n * SB_T, 128), f32)
            for dz_b, w, h in zip(dzs, ws, heads):
                dq_acc = dq_acc + _dot(dz_b, (k_f * h[3]).astype(bf16))
                dk_t = dk_t + _dot(dz_b, h[0], TN)
                dv_t = dv_t + _dot(w.astype(bf16), h[1], TN)
            dk_acc[rows, :] += dk_t
            dv_acc[rows, :] += dv_t
            return dq_acc, cps[0], cps[1], ces[0], ces[1]

        zc = jnp.zeros((SB_T, 1), f32)
        carry = (jnp.zeros((SB_T, 128), f32), zc, zc, zc, zc)
        done = lowest
        tail_lo = i - jnp.minimum(i, SB_TAIL - 1)
        for n in SB_GROUPS_BWD:
            trips = (tail_lo - done) // n
            carry = lax.fori_loop(
                0, trips, functools.partial(lambda gi, cr, n, done: group(done + gi * n, n, cr, None), n=n, done=done),
                carry)
            done = done + trips * n
        carry = _by_count(i, SB_TAIL, lambda n: group(i - n + 1, n, carry, causal))
        dq_ref[...] = (carry[0] * scale).astype(bf16)

        @pl.when(i == nq - 1)
        def _():
            dk_ref[...] = dk_acc[...].astype(bf16)
            dv_ref[...] = dv_acc[...].astype(bf16)

    qs = pl.BlockSpec((SB_T, 128), lambda h, i: (i, h))
    full = pl.BlockSpec((S, 128), lambda h, i: (0, h))
    dq, dk, dv = pl.pallas_call(
        body, name="sb_bwd", grid=(n_pairs, nq),
        in_specs=[qs,
                  pl.BlockSpec((S, 128), lambda h, i: (0, n_pairs + h)),
                  pl.BlockSpec((S, 128), lambda h, i: (0, 2 * n_pairs + h)),
                  qs, qs],
        out_specs=[qs, full, full],
        out_shape=[jax.ShapeDtypeStruct((S, D), bf16)] * 3,
        scratch_shapes=[pltpu.VMEM((S, 128), f32), pltpu.VMEM((S, 128), f32)],
        compiler_params=_params(("parallel", "arbitrary")),
    )(proj, proj, proj, tot_lk, do)
    return dq, dk, dv


CONV_CB = 256
HALO = 8


def _conv_fwd(proj, conv_w, conv_b, S):
    tr = min(512, S)

    def body(x_ref, w_ref, b_ref, xc_ref, xbc_ref):
        w = w_ref[...]
        for t in range(S // tr):
            cur = x_ref[t * tr:(t + 1) * tr, :]
            halo = x_ref[t * tr - HALO:t * tr, :] if t else jnp.zeros((HALO, CONV_CB), f32)
            win = jnp.concatenate([halo, cur], axis=0)
            acc = b_ref[...] + w[CONV_K - 1:CONV_K, :] * cur
            for k in range(CONV_K - 1):
                acc = acc + w[k:k + 1, :] * pltpu.roll(win, CONV_K - 1 - k, 0)[HALO:, :]
            xc_ref[t * tr:(t + 1) * tr, :] = acc
            xbc_ref[t * tr:(t + 1) * tr, :] = acc * _sigmoid(acc)

    col = pl.BlockSpec((S, CONV_CB), lambda c: (0, c))
    return pl.pallas_call(
        body, name="conv_fwd", grid=(CONV_DIM // CONV_CB,),
        in_specs=[pl.BlockSpec((S, CONV_CB), lambda c: (0, P_XBC // CONV_CB + c)),
                  pl.BlockSpec((CONV_K, CONV_CB), lambda c: (0, c)),
                  pl.BlockSpec((1, CONV_CB), lambda c: (0, c))],
        out_specs=[col, col], out_shape=[jax.ShapeDtypeStruct((S, CONV_DIM), f32)] * 2,
        compiler_params=_params(("parallel",)),
    )(proj, conv_w, conv_b)


def _conv_bwd(proj, xc, dxbc, conv_w, S):
    tr = min(512, S)

    def body(x_ref, xc_ref, dy_ref, w_ref, dx_ref, dw_ref, db_ref, dxc_s):
        w = w_ref[...]
        xcv = xc_ref[...]
        sg = _sigmoid(xcv)
        dxc_s[0:S, :] = dy_ref[...] * (sg * (1.0 + xcv * (1.0 - sg)))
        dxc_s[S:S + HALO, :] = jnp.zeros((HALO, CONV_CB), f32)
        dws = [jnp.zeros((1, CONV_CB), f32) for _ in range(CONV_K)]
        db = jnp.zeros((1, CONV_CB), f32)
        for t in range(S // tr):
            cur = x_ref[t * tr:(t + 1) * tr, :]
            halo = x_ref[t * tr - HALO:t * tr, :] if t else jnp.zeros((HALO, CONV_CB), f32)
            win = jnp.concatenate([halo, cur], axis=0)
            dwin = dxc_s[t * tr:(t + 1) * tr + HALO, :]
            dcur = dwin[0:tr, :]
            db = db + jnp.sum(dcur, axis=0, keepdims=True)
            dws[CONV_K - 1] = dws[CONV_K - 1] + jnp.sum(dcur * cur, axis=0, keepdims=True)
            dx = w[CONV_K - 1:CONV_K, :] * dcur
            for k in range(CONV_K - 1):
                sh = CONV_K - 1 - k
                dws[k] = dws[k] + jnp.sum(dcur * pltpu.roll(win, sh, 0)[HALO:, :], axis=0, keepdims=True)
                dx = dx + w[k:k + 1, :] * pltpu.roll(dwin, tr + HALO - sh, 0)[0:tr, :]
            dx_ref[t * tr:(t + 1) * tr, :] = dx.astype(bf16)
        dw_ref[...] = jnp.concatenate(dws + [jnp.zeros((8 - CONV_K, CONV_CB), f32)], axis=0)
        db_ref[...] = db

    col = pl.BlockSpec((S, CONV_CB), lambda c: (0, c))
    return pl.pallas_call(
        body, name="conv_bwd", grid=(CONV_DIM // CONV_CB,),
        in_specs=[pl.BlockSpec((S, CONV_CB), lambda c: (0, P_XBC // CONV_CB + c)), col, col,
                  pl.BlockSpec((CONV_K, CONV_CB), lambda c: (0, c))],
        out_specs=[col, pl.BlockSpec((8, CONV_CB), lambda c: (0, c)), pl.BlockSpec((1, CONV_CB), lambda c: (0, c))],
        out_shape=[jax.ShapeDtypeStruct((S, CONV_DIM), bf16), jax.ShapeDtypeStruct((8, CONV_DIM), f32),
                   jax.ShapeDtypeStruct((1, CONV_DIM), f32)],
        scratch_shapes=[pltpu.VMEM((S + HALO, CONV_CB), f32)],
        compiler_params=_params(("parallel",)),
    )(proj, xc, dxbc, conv_w)


N_PAIR = SSD_HEADS // 2
NEG = -1e30


def _softplus(x):
    return jnp.maximum(x, 0.0) + jnp.log(1.0 + jnp.exp(-jnp.abs(x)))


def _ssd_common(dtr, dtb, alog):
    L = SSD_L
    r_i = lax.broadcasted_iota(jnp.int32, (L, L), 0)
    c_i = lax.broadcasted_iota(jnp.int32, (L, L), 1)
    dt = _softplus(dtr + dtb)
    a = -jnp.exp(alog)
    da = dt * a
    lower = (r_i >= c_i).astype(bf16)
    upper = (r_i <= c_i).astype(bf16)
    parts = _split3(da)
    a_cs = sum(_dot(lower, p) for p in parts)
    a_cs_t = sum(_dot(p, upper, TN) for p in parts)
    return dt, a, a_cs, a_cs_t, r_i >= c_i


def _pair_vec(lane, v, h):
    return jnp.where(lane < SB_HD, v[:, h:h + 1], v[:, h + 1:h + 2])


def _decay_mat(a_cs, a_cs_t, h, tril):
    return jnp.exp(jnp.where(tril, a_cs[:, h:h + 1] - a_cs_t[h:h + 1, :], NEG))


def _ssd_fwd(xbc, proj, pdt, dt_bias_p, a_log_p, d_skip_c, ssd_norm, S):
    L = SSD_L
    nc = S // L

    def body(xbc_ref, dt_ref, z_ref, dtb_ref, alog_ref, dsk_ref, gn_ref, y_ref, yn_ref, hp_ref, state):
        c = pl.program_id(0)

        @pl.when(c == 0)
        def _():
            state[...] = jnp.zeros_like(state)

        hp_ref[0] = state[...]
        lane = lax.broadcasted_iota(jnp.int32, (1, 128), 1)
        row128 = lax.broadcasted_iota(jnp.int32, (128, 1), 0)
        m_a, m_b = _sb_masks()
        dt, a, a_cs, a_cs_t, tril = _ssd_common(dt_ref[...], dtb_ref[...], alog_ref[...])
        a_last = a_cs[L - 1:L, :]
        for g in range(SSD_GROUPS):
            b_g = xbc_ref[:, SSD_INNER + g * SSD_N:SSD_INNER + (g + 1) * SSD_N].astype(bf16)
            c_g = xbc_ref[:, SSD_INNER + (SSD_GROUPS + g) * SSD_N:SSD_INNER + (SSD_GROUPS + g + 1) * SSD_N].astype(bf16)
            cb = _dot(c_g, b_g, NT)
            for pr in range(4):
                h = 8 * g + 2 * pr
                pi = h // 2
                cols = slice(pi * 128, (pi + 1) * 128)
                xs = xbc_ref[:, cols]
                x = xs * _pair_vec(lane, dt, h)
                acs = _pair_vec(lane, a_cs, h)
                al = _pair_vec(lane, a_last, h)
                w_a = (cb * _decay_mat(a_cs, a_cs_t, h, tril)).astype(bf16)
                w_b = (cb * _decay_mat(a_cs, a_cs_t, h + 1, tril)).astype(bf16)
                yd = _dot(w_a, (x * m_a).astype(bf16)) + _dot(w_b, (x * m_b).astype(bf16))
                hp = state[pi]
                yo = _dot(c_g, hp.astype(bf16), NT) * jnp.exp(acs)
                y_ref[:, cols] = yd + yo + dsk_ref[:, cols] * xs
                dec = jnp.exp(jnp.where(row128 < SB_HD, a_last[:, h:h + 1], a_last[:, h + 1:h + 2]))
                state[pi] = hp * dec + _dot((x * jnp.exp(al - acs)).astype(bf16), b_g, TN)
        zz = z_ref[...]
        y2 = y_ref[...] * (zz * _sigmoid(zz))
        gw = SSD_INNER // SSD_GROUPS
        for g in range(SSD_GROUPS):
            yg = y2[:, g * gw:(g + 1) * gw]
            rg = lax.rsqrt(jnp.mean(yg * yg, axis=1, keepdims=True) + EPS)
            yn_ref[:, g * gw:(g + 1) * gw] = (yg * rg * gn_ref[:, g * gw:(g + 1) * gw]).astype(bf16)

    vec128 = pl.BlockSpec((1, 128), lambda c: (0, 0))
    vecin = pl.BlockSpec((1, SSD_INNER), lambda c: (0, 0))
    rows = pl.BlockSpec((L, SSD_INNER), lambda c: (c, 0))
    return pl.pallas_call(
        body, name="ssd_fwd", grid=(nc,),
        in_specs=[pl.BlockSpec((L, CONV_DIM), lambda c: (c, 0)),
                  pl.BlockSpec((L, 128), lambda c: (c, 0)),
                  pl.BlockSpec((L, SSD_INNER), lambda c: (c, P_Z // SSD_INNER)),
                  vec128, vec128, vecin, vecin],
        out_specs=[rows, rows, pl.BlockSpec((1, N_PAIR, 128, SSD_N), lambda c: (c, 0, 0, 0))],
        out_shape=[jax.ShapeDtypeStruct((S, SSD_INNER), f32), jax.ShapeDtypeStruct((S, SSD_INNER), bf16),
                   jax.ShapeDtypeStruct((nc, N_PAIR, 128, SSD_N), f32)],
        scratch_shapes=[pltpu.VMEM((N_PAIR, 128, SSD_N), f32)],
        compiler_params=_params(("arbitrary",)),
    )(xbc, pdt, proj, dt_bias_p, a_log_p, d_skip_c, ssd_norm)


def _sum_all(v):
    return jnp.sum(jnp.sum(v, axis=1, keepdims=True), axis=0, keepdims=True)


def _ssd_bwd(dyn, y, xbc, proj, pdt, hprev, dt_bias_p, a_log_p, d_skip_c, ssd_norm, S, ride=None):
    L = SSD_L
    nc = S // L
    n_r = ride.n if ride else 0

    col = lax.broadcasted_iota(jnp.int32, (2 * SSD_INNER, 128), 0)
    head = lax.broadcasted_iota(jnp.int32, (2 * SSD_INNER, 128), 1)
    sel_pair = (col[:SSD_INNER] // SB_HD == head[:SSD_INNER]).astype(bf16)
    sel_head = (col // 128 == head).astype(bf16)

    def body(*refs):
        (dyn_ref, y_ref, xbc_ref, dt_ref, z_ref, hp_ref, dtb_ref, alog_ref, dsk_ref, gn_ref,
         selp_ref, selh_ref) = refs[:12]
        dz_ref, dxbc_ref, ddt_ref, dgn_ref, dsk_out, dalog_ref, ddtb_ref = refs[12 + n_r:19 + n_r]
        dstate, dy_s, st_a, st_q, st_d, st_x, dat = refs[19 + 2 * n_r:26 + 2 * n_r]
        r_ins, r_lnd, r_sems = refs[12:12 + n_r], refs[19 + n_r:19 + 2 * n_r], refs[26 + 2 * n_r:]
        c = pl.program_id(0)
        if ride:
            pl.when(c == 0)(lambda: ride.start(r_ins, r_lnd, r_sems))

        @pl.when(c == 0)
        def _():
            dat[...] = jnp.zeros_like(dat)
            dstate[...] = jnp.zeros_like(dstate)
            dgn_ref[...] = jnp.zeros_like(dgn_ref)
            dsk_out[...] = jnp.zeros_like(dsk_out)
            dalog_ref[...] = jnp.zeros_like(dalog_ref)
            ddtb_ref[...] = jnp.zeros_like(ddtb_ref)

        lane = lax.broadcasted_iota(jnp.int32, (1, 128), 1)
        row128 = lax.broadcasted_iota(jnp.int32, (128, 1), 0)
        rowl = lax.broadcasted_iota(jnp.int32, (L, 1), 0)
        m_a, m_b = _sb_masks()
        dtr = dt_ref[...]
        dt, a, a_cs, a_cs_t, tril = _ssd_common(dtr, dtb_ref[...], alog_ref[...])
        a_last = a_cs[L - 1:L, :]

        zz = z_ref[...]
        sg = _sigmoid(zz)
        silu = zz * sg
        yv = y_ref[...]
        y2 = yv * silu
        gw = SSD_INNER // SSD_GROUPS
        for g in range(SSD_GROUPS):
            sl = slice(g * gw, (g + 1) * gw)
            yg = y2[:, sl]
            rg = lax.rsqrt(jnp.mean(yg * yg, axis=1, keepdims=True) + EPS)
            yh = yg * rg
            dyn_g = dyn_ref[:, sl]
            dgn_ref[:, sl] += jnp.sum(dyn_g * yh, axis=0, keepdims=True)
            dyh = dyn_g * gn_ref[:, sl]
            dy2 = rg * (dyh - yh * jnp.mean(dyh * yh, axis=1, keepdims=True))
            dy_s[:, sl] = dy2 * silu[:, sl]
            dz_ref[:, sl] = (dy2 * yv[:, sl] * (sg[:, sl] * (1.0 + zz[:, sl] * (1.0 - sg[:, sl])))).astype(bf16)

        last_row = jnp.zeros((1, 128), f32)
        dsk_acc = jnp.zeros((1, 128), f32)
        for g in range(SSD_GROUPS):
            bsl = slice(SSD_INNER + g * SSD_N, SSD_INNER + (g + 1) * SSD_N)
            csl = slice(SSD_INNER + (SSD_GROUPS + g) * SSD_N, SSD_INNER + (SSD_GROUPS + g + 1) * SSD_N)
            b_g = xbc_ref[:, bsl].astype(bf16)
            c_g = xbc_ref[:, csl].astype(bf16)
            cb = _dot(c_g, b_g, NT)
            dcb = jnp.zeros((L, L), f32)
            dc_g = jnp.zeros((L, SSD_N), f32)
            db_g = jnp.zeros((L, SSD_N), f32)
            for pr in range(4):
                h = 8 * g + 2 * pr
                pi = h // 2
                cols = slice(pi * 128, (pi + 1) * 128)
                xs = xbc_ref[:, cols]
                dt_p = _pair_vec(lane, dt, h)
                x = xs * dt_p
                acs = _pair_vec(lane, a_cs, h)
                al = _pair_vec(lane, a_last, h)
                e_a = jnp.exp(acs)
                dte = jnp.exp(al - acs)
                m_mat_a = _decay_mat(a_cs, a_cs_t, h, tril)
                m_mat_b = _decay_mat(a_cs, a_cs_t, h + 1, tril)
                dyp = dy_s[:, cols]
                dsk = dsk_ref[:, cols]
                d_hn = dstate[pi]
                hp = hp_ref[0, pi]
                dy_a = (dyp * m_a).astype(bf16)
                dy_b = (dyp * m_b).astype(bf16)
                x_b = x.astype(bf16)
                gm_a = _dot(dy_a, x_b, NT) * m_mat_a
                gm_b = _dot(dy_b, x_b, NT) * m_mat_b
                dcb = dcb + gm_a + gm_b
                dx_d = _dot((cb * m_mat_a).astype(bf16), dy_a, TN) + _dot((cb * m_mat_b).astype(bf16), dy_b, TN)
                dx_s = _dot(b_g, d_hn.astype(bf16), NT) * dte
                dx = dx_d + dx_s
                dxbc_ref[:, cols] = dx * dt_p + dsk * dyp
                xdxs = x * dx_s
                st_x[:, cols] = xdxs
                st_a[:, cols] = dyp * (_dot(c_g, hp.astype(bf16), NT) * e_a) - xdxs
                st_d[:, cols] = dx * xs
                hh = d_hn * hp
                dsk_row = jnp.sum(dyp * xs, axis=0, keepdims=True)
                dec = jnp.exp(jnp.where(row128 < SB_HD, a_last[:, h:h + 1], a_last[:, h + 1:h + 2]))
                for hd, m, gm in ((h, m_a, gm_a), (h + 1, m_b, gm_b)):
                    half = slice(0, SB_HD) if hd == h else slice(SB_HD, 128)
                    qm = gm * cb
                    st_q[:, hd * 128:(hd + 1) * 128] = qm
                    dat[hd:hd + 1, :] = jnp.sum(qm, axis=0, keepdims=True)
                    hh_sum = jnp.sum(jnp.sum(hh[half, :], axis=0, keepdims=True), axis=1, keepdims=True)
                    last_row = jnp.where(lane == hd, jnp.exp(a_last[:, hd:hd + 1]) * hh_sum, last_row)
                    dsk_acc = jnp.where(lane == hd, jnp.sum(dsk_row * m, axis=1, keepdims=True), dsk_acc)
                dye = (dyp * e_a).astype(bf16)
                dc_g = dc_g + _dot(dye, hp.astype(bf16))
                db_g = db_g + _dot((x * dte).astype(bf16), d_hn.astype(bf16))
                dstate[pi] = dec * d_hn + _dot(dye, c_g, TN)
            dcb_b = dcb.astype(bf16)
            dxbc_ref[:, csl] = dc_g + _dot(dcb_b, b_g)
            dxbc_ref[:, bsl] = db_g + _dot(dcb_b, c_g, TN)

        r_i = lax.broadcasted_iota(jnp.int32, (L, L), 0)
        c_i = lax.broadcasted_iota(jnp.int32, (L, L), 1)
        rev = (r_i <= c_i).astype(bf16)

        def head_sums(st, sel, split=_split2):
            return sum(_dot(p, sel[...]) for p in split(st[...]))

        last_row = last_row + jnp.sum(head_sums(st_x, selp_ref), axis=0, keepdims=True)
        d_acs = (head_sums(st_a, selp_ref) + head_sums(st_q, selh_ref, _split3)
                 + jnp.where(rowl == L - 1, last_row, 0.0))
        ddt_x = head_sums(st_d, selp_ref)
        dda = sum(_dot(rev, p) for p in _split3(d_acs)) - sum(_dot(rev, p, NT) for p in _split3(dat[...]))
        ddt = ddt_x + dda * a
        dalog_ref[...] += jnp.sum(dda * dt, axis=0, keepdims=True) * a
        ddtr = jnp.where(lane < SSD_HEADS, ddt * _sigmoid(dtr + dtb_ref[...]), 0.0)
        ddt_ref[...] = ddtr.astype(bf16)
        ddtb_ref[...] += jnp.sum(ddtr, axis=0, keepdims=True)
        dsk_out[...] += dsk_acc
        if ride:
            pl.when(c == nc - 1)(lambda: ride.finish(r_ins, r_lnd, r_sems))

    rv = lambda c: nc - 1 - c
    vec128 = pl.BlockSpec((1, 128), lambda c: (0, 0))
    vecin = pl.BlockSpec((1, SSD_INNER), lambda c: (0, 0))
    rows = pl.BlockSpec((L, SSD_INNER), lambda c: (rv(c), 0))
    return pl.pallas_call(
        body, name="ssd_bwd", grid=(nc,),
        in_specs=[rows, rows,
                  pl.BlockSpec((L, CONV_DIM), lambda c: (rv(c), 0)),
                  pl.BlockSpec((L, 128), lambda c: (rv(c), 0)),
                  pl.BlockSpec((L, SSD_INNER), lambda c: (rv(c), P_Z // SSD_INNER)),
                  pl.BlockSpec((1, N_PAIR, 128, SSD_N), lambda c: (rv(c), 0, 0, 0)),
                  vec128, vec128, vecin, vecin,
                  pl.BlockSpec((SSD_INNER, 128), lambda c: (0, 0)),
                  pl.BlockSpec((2 * SSD_INNER, 128), lambda c: (0, 0))] + (ride.in_specs if ride else []),
        out_specs=[rows, pl.BlockSpec((L, CONV_DIM), lambda c: (rv(c), 0)),
                   pl.BlockSpec((L, 128), lambda c: (rv(c), 0)), vecin, vec128, vec128, vec128]
        + (ride.out_specs if ride else []),
        out_shape=[jax.ShapeDtypeStruct((S, SSD_INNER), bf16), jax.ShapeDtypeStruct((S, CONV_DIM), f32),
                   jax.ShapeDtypeStruct((S, 128), bf16), jax.ShapeDtypeStruct((1, SSD_INNER), f32),
                   jax.ShapeDtypeStruct((1, 128), f32), jax.ShapeDtypeStruct((1, 128), f32),
                   jax.ShapeDtypeStruct((1, 128), f32)] + (ride.out_shape if ride else []),
        scratch_shapes=[pltpu.VMEM((N_PAIR, 128, SSD_N), f32), pltpu.VMEM((L, SSD_INNER), f32),
                        pltpu.VMEM((L, SSD_INNER), f32), pltpu.VMEM((L, 2 * SSD_INNER), f32),
                        pltpu.VMEM((L, SSD_INNER), f32), pltpu.VMEM((L, SSD_INNER), f32),
                        pltpu.VMEM((128, L), f32)]
        + (ride.scratch if ride else []),
        compiler_params=_params(("arbitrary",)),
    )(dyn, y, xbc, pdt, proj, hprev, dt_bias_p, a_log_p, d_skip_c, ssd_norm, sel_pair, sel_head,
      *(ride.srcs if ride else []))


MEM_W = MEM_HEADS * MEM_HD


def _mem_probs(q, k):
    s = _dot(q, k, NT) * (MEM_HD ** -0.5)
    s = s - jnp.max(s, axis=1, keepdims=True)
    p = jnp.exp(s)
    return p / jnp.sum(p, axis=1, keepdims=True)


def _mem_fwd(proj, kv, S, tm=512):
    tm = min(tm, S)
    M = kv.shape[0]

    def body(q_ref, kv_ref, o_ref):
        for h in range(MEM_HEADS):
            sl = slice(h * MEM_HD, (h + 1) * MEM_HD)
            vsl = slice(MEM_W + h * MEM_HD, MEM_W + (h + 1) * MEM_HD)
            p = _mem_probs(q_ref[:, sl].astype(bf16), kv_ref[:, sl].astype(bf16))
            o_ref[:, sl] = _dot(p.astype(bf16), kv_ref[:, vsl].astype(bf16)).astype(bf16)

    return pl.pallas_call(
        body, name="mem_fwd", grid=(S // tm,),
        in_specs=[pl.BlockSpec((tm, MEM_W), lambda i: (i, P_MEMQ // MEM_W)),
                  pl.BlockSpec((M, 2 * MEM_W), lambda i: (0, 0))],
        out_specs=pl.BlockSpec((tm, MEM_W), lambda i: (i, 0)),
        out_shape=jax.ShapeDtypeStruct((S, MEM_W), bf16),
        compiler_params=_params(("parallel",)),
    )(proj, kv)


def _mem_bwd(proj, kv, dy, S, tm=512):
    tm = min(tm, S)
    M = kv.shape[0]
    scale = MEM_HD ** -0.5

    def body(q_ref, kv_ref, dy_ref, dq_ref, dkv_ref):
        @pl.when(pl.program_id(0) == 0)
        def _():
            dkv_ref[...] = jnp.zeros_like(dkv_ref)

        for h in range(MEM_HEADS):
            sl = slice(h * MEM_HD, (h + 1) * MEM_HD)
            vsl = slice(MEM_W + h * MEM_HD, MEM_W + (h + 1) * MEM_HD)
            q = q_ref[:, sl].astype(bf16)
            k = kv_ref[:, sl].astype(bf16)
            v = kv_ref[:, vsl].astype(bf16)
            dyh = dy_ref[:, sl].astype(bf16)
            p = _mem_probs(q, k)
            dp = _dot(dyh, v, NT)
            ds = (p * (dp - jnp.sum(dp * p, axis=1, keepdims=True)) * scale).astype(bf16)
            dq_ref[:, sl] = _dot(ds, k).astype(bf16)
            dkv_ref[:, sl] += _dot(ds, q, TN)
            dkv_ref[:, vsl] += _dot(p.astype(bf16), dyh, TN)

    return pl.pallas_call(
        body, name="mem_bwd", grid=(S // tm,),
        in_specs=[pl.BlockSpec((tm, MEM_W), lambda i: (i, P_MEMQ // MEM_W)),
                  pl.BlockSpec((M, 2 * MEM_W), lambda i: (0, 0)),
                  pl.BlockSpec((tm, MEM_W), lambda i: (i, 0))],
        out_specs=[pl.BlockSpec((tm, MEM_W), lambda i: (i, 0)), pl.BlockSpec((M, 2 * MEM_W), lambda i: (0, 0))],
        out_shape=[jax.ShapeDtypeStruct((S, MEM_W), bf16), jax.ShapeDtypeStruct((M, 2 * MEM_W), f32)],
        compiler_params=_params(("arbitrary",)),
    )(proj, kv, dy)


def _merge_fwd(proj, t0, t1, t2, S, tm=512):
    tm = min(tm, S)

    def body(g_ref, t0_ref, t1_ref, t2_ref, o_ref):
        acc = jnp.zeros((tm, D), f32)
        for b, t_ref in enumerate((t0_ref, t1_ref, t2_ref)):
            acc = acc + _sigmoid(g_ref[:, b * D:(b + 1) * D]) * t_ref[...]
        o_ref[...] = acc.astype(bf16)

    row = pl.BlockSpec((tm, D), lambda i: (i, 0))
    return pl.pallas_call(
        body, name="merge_fwd", grid=(S // tm,),
        in_specs=[pl.BlockSpec((tm, 3 * D), lambda i: (i, P_GATE // (3 * D))), row, row, row],
        out_specs=row, out_shape=jax.ShapeDtypeStruct((S, D), bf16),
        compiler_params=_params(("parallel",)),
    )(proj, t0, t1, t2)


def _merge_bwd(proj, t0, t1, t2, dm, S, tm=512):
    tm = min(tm, S)

    def body(g_ref, t0_ref, t1_ref, t2_ref, dm_ref, d0_ref, d1_ref, d2_ref, dg_ref):
        dmv = dm_ref[...]
        for b, (t_ref, d_ref) in enumerate(((t0_ref, d0_ref), (t1_ref, d1_ref), (t2_ref, d2_ref))):
            sg = _sigmoid(g_ref[:, b * D:(b + 1) * D])
            d_ref[...] = (dmv * sg).astype(bf16)
            dg_ref[:, b * D:(b + 1) * D] = (dmv * t_ref[...] * sg * (1.0 - sg)).astype(bf16)

    row = pl.BlockSpec((tm, D), lambda i: (i, 0))
    return pl.pallas_call(
        body, name="merge_bwd", grid=(S // tm,),
        in_specs=[pl.BlockSpec((tm, 3 * D), lambda i: (i, P_GATE // (3 * D))), row, row, row, row],
        out_specs=[row, row, row, pl.BlockSpec((tm, 3 * D), lambda i: (i, 0))],
        out_shape=[jax.ShapeDtypeStruct((S, D), bf16)] * 3 + [jax.ShapeDtypeStruct((S, 3 * D), bf16)],
        compiler_params=_params(("parallel",)),
    )(proj, t0, t1, t2, dm)


def _loss_head(ff, g, h1, target, S, tm=512):
    tm = min(tm, S)

    def body(ff_ref, g_ref, h1_ref, t_ref, dh_ref, loss_ref):
        xv = ff_ref[...]
        r = lax.rsqrt(jnp.mean(xv * xv, axis=1, keepdims=True) + EPS)
        err = h1_ref[...] + xv * r * g_ref[...] - t_ref[...]
        dh_ref[...] = err * (1.0 / D)

        @pl.when(pl.program_id(0) == 0)
        def _():
            loss_ref[...] = jnp.zeros_like(loss_ref)

        loss_ref[...] += 0.5 * _sum_all(jnp.mean(err * err, axis=1, keepdims=True)) * jnp.ones((1, 128), f32)

    row = pl.BlockSpec((tm, D), lambda i: (i, 0))
    return pl.pallas_call(
        body, name="loss_head", grid=(S // tm,),
        in_specs=[row, pl.BlockSpec((1, D), lambda i: (0, 0)), row, row],
        out_specs=[row, pl.BlockSpec((1, 128), lambda i: (0, 0))],
        out_shape=[jax.ShapeDtypeStruct((S, D), f32), jax.ShapeDtypeStruct((1, 128), f32)],
        compiler_params=_params(("arbitrary",)),
    )(ff, g, h1, target)


def _local_step(x, mem, target, wts, late_ride, late_weights, small, rest_ride):
    S = x.shape[0]
    M = mem.shape[0]
    pad = lambda v: jnp.pad(v, ((0, 0), (0, 128 - SSD_HEADS)))
    dtb_p, alog_p = pad(small["dt_bias"]), pad(small["a_log"])
    dsk_c = jnp.repeat(small["d_skip"], SB_HD, axis=1)

    u = _rms_fwd(x, small["norm_mix_pre"], name="norm_pre", out_dtype=bf16)
    proj = _mm(u, wts["w_main"], "nn", tm=1024, tn=1024, name="in_proj")
    pdt = _mm(u, wts["w_dt"], "nn", tm=1024, tn=128, name="in_proj_dt")
    y_sb, tot_lk, lands = _sb_fwd(proj, S, late_ride)
    wts = dict(wts, **late_weights(lands))
    small = dict(small, conv_w=wts.pop("conv_w"))
    xc, xbc = _conv_fwd(proj, small["conv_w"], small["conv_b"], S)
    y_ssd, yn, hprev = _ssd_fwd(xbc, proj, pdt, dtb_p, alog_p, dsk_c, small["ssd_norm"], S)
    mn = _rms_fwd(mem, small["norm_mem"], name="norm_mem", out_dtype=bf16, tm=min(512, M))
    kv = _mm(mn, wts["w_mem_kv"], "nn", tm=M, tn=1024, name="mem_kv")
    y_mem = _mem_fwd(proj, kv, S)
    t0 = _mm(y_sb, wts["w_sb_out"], "nn", tm=1024, tn=1024, name="sb_out")
    t1 = _mm(yn, wts["w_ssd_out"], "nn", tm=1024, tn=1024, name="ssd_out")
    t2 = _mm(y_mem, wts["w_mem_out"], "nn", tm=1024, tn=1024, name="mem_out")
    merged = _merge_fwd(proj, t0, t1, t2, S)
    mix = _mm(merged, wts["w_o"], "nn", tm=1024, tn=1024, name="w_o")
    h1 = _rms_fwd(mix, small["norm_mix_post"], name="norm_mix_post", out_dtype=f32, residual=x)
    u2 = _rms_fwd(h1, small["norm_mlp_pre"], name="norm_mlp_pre", out_dtype=bf16)
    a_up, hrelu = _mm(u2, wts["w_up"], "nn", tm=1024, tn=1024, name="mlp_up", out_dtypes=(f32, bf16),
                      epi=lambda acc: (acc, jnp.square(jnp.maximum(acc, 0.0))))
    ff = _mm(hrelu, wts["w_down"], "nn", tm=1024, tn=1024, name="mlp_down")
    dh2, loss = _loss_head(ff, small["norm_mlp_post"], h1, target, S)

    g = {}
    dff, g["norm_mlp_post"] = _rms_bwd(ff, dh2, small["norm_mlp_post"], name="norm_mlp_post_bwd", out_dtype=bf16)
    da = _mm(dff, wts["w_down"], "nt", tm=1024, tn=1024, name="mlp_down_dx", out_dtypes=(bf16,),
             epi=lambda acc, a: (acc * (2.0 * jnp.maximum(a, 0.0)),), extras=(a_up,))
    g["w_down"] = _mm(hrelu, dff, "tn", tm=1024, tn=1024, name="mlp_down_dw")
    du2 = _mm(da, wts["w_up"], "nt", tm=1024, tn=1024, name="mlp_up_dx")
    g["w_up"] = _mm(u2, da, "tn", tm=1024, tn=1024, name="mlp_up_dw")
    dh1, g["norm_mlp_pre"] = _rms_bwd(h1, du2, small["norm_mlp_pre"], name="norm_mlp_pre_bwd", out_dtype=f32, add=dh2)
    dmix, g["norm_mix_post"] = _rms_bwd(mix, dh1, small["norm_mix_post"], name="norm_mix_post_bwd", out_dtype=bf16)
    dmerged = _mm(dmix, wts["w_o"], "nt", tm=1024, tn=1024, name="w_o_dx")
    g["w_o"] = _mm(merged, dmix, "tn", tm=1024, tn=1024, name="w_o_dw")
    dt0, dt1, dt2, dgl = _merge_bwd(proj, t0, t1, t2, dmerged, S)
    dy_sb = _mm(dt0, wts["w_sb_out"], "nt", tm=1024, tn=1024, name="sb_out_dx")
    g["w_sb_out"] = _mm(y_sb, dt0, "tn", tm=1024, tn=1024, name="sb_out_dw")
    dy_ssd = _mm(dt1, wts["w_ssd_out"], "nt", tm=1024, tn=1024, name="ssd_out_dx")
    g["w_ssd_out"] = _mm(yn, dt1, "tn", tm=1024, tn=1024, name="ssd_out_dw")
    dy_mem = _mm(dt2, wts["w_mem_out"], "nt", tm=1024, tn=1024, name="mem_out_dx")
    g["w_mem_out"] = _mm(y_mem, dt2, "tn", tm=1024, tn=1024, name="mem_out_dw")
    dmemq, dkv = _mem_bwd(proj, kv, dy_mem, S)
    g["w_mem_kv"] = _mm(mn, dkv, "tn", tm=1024, tn=1024, name="mem_kv_dw")
    dmn = _mm(dkv, wts["w_mem_kv"], "nt", tm=M, tn=1024, name="mem_kv_dx")
    _, g["norm_mem"] = _rms_bwd(mem, dmn, small["norm_mem"], name="norm_mem_bwd", out_dtype=bf16, tm=min(512, M))
    dz, dxbc, ddt, g["ssd_norm"], dsk, dalog, ddtb, *g["rest_lands"] = _ssd_bwd(
        dy_ssd, y_ssd, xbc, proj, pdt, hprev, dtb_p, alog_p, dsk_c, small["ssd_norm"], S,
        rest_ride(g) if rest_ride else None)
    g["d_skip"], g["a_log"], g["dt_bias"] = dsk[:, :SSD_HEADS], dalog[:, :SSD_HEADS], ddtb[:, :SSD_HEADS]
    dxbc_raw, dcw, g["conv_b"] = _conv_bwd(proj, xc, dxbc, small["conv_w"], S)
    g["conv_w"] = dcw[:CONV_K]
    dq, dk, dv = _sb_bwd(proj, tot_lk, dy_sb, S)
    dproj = jnp.concatenate([dq, dk, dv, dxbc_raw, dgl, dmemq, dz], axis=1)
    du_dt = _mm(ddt, wts["w_dt"], "nt", tm=1024, tn=1024, name="in_proj_dt_dx")
    du = _mm(dproj, wts["w_main"], "nt", tm=512, tn=256, name="in_proj_dx",
             epi=lambda acc, e: (acc + e,), extras=(du_dt,))
    g["w_main"] = _mm(u, dproj, "tn", tm=1024, tn=1024, name="in_proj_dw")
    g["w_dt"] = _mm(u, ddt, "tn", tm=1024, tn=128, name="in_proj_dt_dw")
    grad_x, g["norm_mix_pre"] = _rms_bwd(x, du, small["norm_mix_pre"], name="norm_pre_bwd", out_dtype=f32, add=dh1)
    return loss, grad_x, g


def _to_internal(w_in):
    sec = lambda r: w_in[:, r[0]:r[1]]
    w_main = jnp.concatenate([sec(R_QKV), sec(R_XBC), sec(R_GATE), sec(R_MEMQ), sec(R_Z)], axis=1)
    w_dt = jnp.pad(sec(R_DT), ((0, 0), (0, 128 - SSD_HEADS)))
    return w_main, w_dt


def _from_internal(g_main, g_dt):
    sec = lambda p, n: g_main[:, p:p + n]
    return jnp.concatenate([sec(P_QKV, 3072), sec(P_Z, 2048), sec(P_XBC, 3072), g_dt[:, :SSD_HEADS],
                            sec(P_MEMQ, 1024), sec(P_GATE, 3072)], axis=1)


MESH = pl.DeviceIdType.MESH
ANY = pl.BlockSpec(memory_space=pl.ANY)


def _place():
    x, y, c = lax.axis_index("x"), lax.axis_index("y"), lax.axis_index("c")
    return (x, y, c), [(1 - x, y, c), (x, 1 - y, c), (1 - x, 1 - y, c)]


def _exchange_copy(mode, ins, lands, send, recv, a, k, me, peers, arriving):
    p = peers[k]
    theirs = 2 * p[0] + p[1]
    if mode == "gather":
        src, dst = ins[a], lands[a].at[theirs if arriving else me]
    else:
        src, dst = ins[a].at[theirs], lands[a].at[k]
    return pltpu.make_async_remote_copy(src_ref=src, dst_ref=dst, send_sem=send.at[a * 3 + k],
                                        recv_sem=recv.at[a * 3 + k], device_id=p, device_id_type=MESH)


class _Ride:
    def __init__(self, srcs, mode):
        self.srcs, self.mode, self.n = list(srcs), mode, len(srcs)
        n = self.n
        self.in_specs, self.out_specs = [ANY] * n, [ANY] * n
        self.out_shape = [
            jax.ShapeDtypeStruct((N_SHARD,) + s.shape if mode == "gather" else (3,) + s.shape[1:], s.dtype)
            for s in self.srcs]
        self.scratch = [pltpu.SemaphoreType.DMA((3 * n,)), pltpu.SemaphoreType.DMA((3 * n,)),
                        pltpu.SemaphoreType.DMA((n,))]

    def _own(self, ins, lnd, sems):
        if self.mode != "gather":
            return []
        me = 2 * lax.axis_index("x") + lax.axis_index("y")
        return [pltpu.make_async_copy(ins[a], lnd[a].at[me], sems[2].at[a]) for a in range(self.n)]

    def _far(self, ins, lnd, sems, arriving):
        (x, y, c), peers = _place()
        return [_exchange_copy(self.mode, ins, lnd, sems[0], sems[1], a, k, 2 * x + y, peers, arriving)
                for a in range(self.n) for k in range(3)]

    def start(self, ins, lnd, sems):
        for cp in self._own(ins, lnd, sems) + self._far(ins, lnd, sems, False):
            cp.start()

    def finish(self, ins, lnd, sems):
        for cp in self._far(ins, lnd, sems, True):
            cp.wait_recv()
        for cp in self._far(ins, lnd, sems, False):
            cp.wait_send()
        for cp in self._own(ins, lnd, sems):
            cp.wait()


def _exchange(srcs, mode, name):
    ride = _Ride(srcs, mode)
    n = ride.n

    def body(*refs):
        ride.start(refs[:n], refs[n:2 * n], refs[2 * n:])
        ride.finish(refs[:n], refs[n:2 * n], refs[2 * n:])

    return pl.pallas_call(body, name=name, in_specs=ride.in_specs, out_specs=ride.out_specs,
                          out_shape=ride.out_shape, scratch_shapes=ride.scratch)(*srcs)


def _gather_two_level(shards, name):
    n = len(shards)

    def body(*refs):
        ins, lnd = refs[:n], refs[n:2 * n]
        send, recv, loc = refs[2 * n:]
        (x, y, c), peers = _place()
        me = 2 * x + y

        def half(ref, a, core):
            rows = shards[a].shape[0] // 2
            return ref.at[pl.ds(core * rows, rows)]

        def copy(a, j, slot, core, to):
            return pltpu.make_async_remote_copy(
                src_ref=half(ins[a], a, core) if j < 3 else half(lnd[a].at[slot], a, core),
                dst_ref=half(lnd[a].at[slot], a, core), send_sem=send.at[6 * a + j], recv_sem=recv.at[6 * a + j],
                device_id=to, device_id_type=MESH)

        own = [pltpu.make_async_copy(ins[a], lnd[a].at[me], loc.at[a]) for a in range(n)]
        far = [copy(a, k, me, c, peers[k]) for a in range(n) for k in range(3)]
        for cp in own + far:
            cp.start()
        passed = []
        for a in range(n):
            for k, p in enumerate(peers):
                theirs = 2 * p[0] + p[1]
                copy(a, k, theirs, c, p).wait_recv()
                passed.append(copy(a, 3 + k, theirs, c, (x, y, 1 - c)))
                passed[-1].start()
        for a in range(n):
            for k, p in enumerate(peers):
                copy(a, 3 + k, 2 * p[0] + p[1], 1 - c, (x, y, 1 - c)).wait_recv()
        for cp in far + passed:
            cp.wait_send()
        for cp in own:
            cp.wait()

    return pl.pallas_call(
        body, name=name, in_specs=[ANY] * n, out_specs=[ANY] * n,
        out_shape=[jax.ShapeDtypeStruct((N_SHARD,) + s.shape, s.dtype) for s in shards],
        scratch_shapes=[pltpu.SemaphoreType.DMA((6 * n,)), pltpu.SemaphoreType.DMA((6 * n,)),
                        pltpu.SemaphoreType.DMA((n,))],
    )(*shards)


def _exchange_packets(packet):
    def body(pk, pk_out, send, recv, loc):
        x, y, c = lax.axis_index("x"), lax.axis_index("y"), lax.axis_index("c")
        lin = 4 * x + 2 * y + c
        own = pltpu.make_async_copy(pk, pk_out.at[lin], loc.at[0])
        own.start()

        def pk_copy(m, slot):
            dev = (x ^ ((m >> 2) & 1), y ^ ((m >> 1) & 1), c ^ (m & 1))
            return pltpu.make_async_remote_copy(
                src_ref=pk, dst_ref=pk_out.at[slot], send_sem=send.at[m - 1], recv_sem=recv.at[m - 1],
                device_id=dev, device_id_type=MESH)

        sent = [pk_copy(m, lin) for m in range(1, N_DEV)]
        for cp in sent:
            cp.start()
        for m in range(1, N_DEV):
            pk_copy(m, lin ^ m).wait_recv()
        for cp in sent:
            cp.wait_send()
        own.wait()

    return pl.pallas_call(
        body, name="exchange_packets", in_specs=[ANY], out_specs=ANY,
        out_shape=jax.ShapeDtypeStruct((N_DEV,) + packet.shape, packet.dtype),
        scratch_shapes=[pltpu.SemaphoreType.DMA((N_DEV - 1,)), pltpu.SemaphoreType.DMA((N_DEV - 1,)),
                        pltpu.SemaphoreType.DMA((1,))],
    )(packet)


def _swap_sibling(parts, name):
    n = len(parts)

    def body(*refs):
        ins, outs = refs[:n], refs[n:2 * n]
        send, recv = refs[2 * n:]
        x, y, c = lax.axis_index("x"), lax.axis_index("y"), lax.axis_index("c")
        cps = [pltpu.make_async_remote_copy(
            src_ref=ins[a], dst_ref=outs[a], send_sem=send.at[a], recv_sem=recv.at[a],
            device_id=(x, y, 1 - c), device_id_type=MESH) for a in range(n)]
        for cp in cps:
            cp.start()
        for cp in cps:
            cp.wait_recv()
        for cp in cps:
            cp.wait_send()

    return pl.pallas_call(
        body, name=name,
        in_specs=[ANY] * n, out_specs=[ANY] * n,
        out_shape=[jax.ShapeDtypeStruct(p.shape, p.dtype) for p in parts],
        scratch_shapes=[pltpu.SemaphoreType.DMA((n,)), pltpu.SemaphoreType.DMA((n,))],
    )(*parts)


BLOCK_ELEMS = 256 * 1024


def _row_tile(R, C):
    tr = max(8, (BLOCK_ELEMS // C) // 8 * 8)
    while R % tr:
        tr -= 8
    return min(tr, R)


def _sum_parts(own, stack, name):
    k = stack.shape[0]
    R, C = stack.shape[1:]
    tr = _row_tile(R, C)

    def body(*refs):
        o_ref = refs[-1]
        acc = refs[0][...]
        for r in refs[1:-1]:
            acc = acc + r[...]
        o_ref[...] = acc

    row = pl.BlockSpec((tr, C), lambda i: (i, 0))
    specs = ([row] if own is not None else []) + [
        pl.BlockSpec((None, tr, C), functools.partial(lambda i, j: (j, i, 0), j=j)) for j in range(k)]
    args = ([own] if own is not None else []) + [stack] * k
    return pl.pallas_call(
        body, name=name, grid=(R // tr,), in_specs=specs, out_specs=row,
        out_shape=jax.ShapeDtypeStruct((R, C), f32), compiler_params=_params(("parallel",)),
    )(*args)


def _adamw(w, m, v, g_parts, name):
    R, C = w.shape
    tr = _row_tile(R, C)
    n_g = len(g_parts)

    def body(w_ref, m_ref, v_ref, *rest):
        g = rest[0][...]
        for r in rest[1:n_g]:
            g = g + r[...]
        g_ref, d_ref, nm_ref, nv_ref = rest[n_g:]
        nm = ADAM_B1 * m_ref[...] + (1.0 - ADAM_B1) * g
        nv = ADAM_B2 * v_ref[...] + (1.0 - ADAM_B2) * jnp.square(g)
        m_hat = nm / (1.0 - ADAM_B1 ** ADAM_STEP)
        v_hat = nv / (1.0 - ADAM_B2 ** ADAM_STEP)
        g_ref[...] = g
        d_ref[...] = -ADAM_LR * (m_hat / (jnp.sqrt(v_hat) + ADAM_EPS) + ADAM_WD * w_ref[...])
        nm_ref[...] = nm
        nv_ref[...] = nv

    row = pl.BlockSpec((tr, C), lambda i: (i, 0))
    return pl.pallas_call(
        body, name=name, grid=(R // tr,), in_specs=[row] * (3 + n_g), out_specs=[row] * 4,
        out_shape=[jax.ShapeDtypeStruct((R, C), f32)] * 4, compiler_params=_params(("parallel",)),
    )(w, m, v, *g_parts)


BIG = ("w_in", "w_mem_kv", "w_sb_out", "w_ssd_out", "w_mem_out", "w_o", "w_up", "w_down")
FIRST = ("w_in", "w_mem_kv")
LATE = ("w_sb_out", "w_ssd_out", "w_mem_out", "w_o", "w_up", "w_down")
REST = BIG[1:]
COL_SHARDED = ("w_in", "w_mem_kv", "w_up")
SMALL = ("norm_mix_pre", "conv_w", "conv_b", "dt_bias", "a_log", "d_skip", "ssd_norm", "norm_mem",
         "norm_mix_post", "norm_mlp_pre", "norm_mlp_post")
WEIGHTS = ("norm_mix_pre", "w_in", "conv_w", "conv_b", "dt_bias", "a_log", "d_skip", "ssd_norm", "norm_mem",
           "w_mem_kv", "w_sb_out", "w_ssd_out", "w_mem_out", "w_o", "norm_mix_post", "norm_mlp_pre", "w_up",
           "w_down", "norm_mlp_post")
PK_ROWS = 184


def _pack(vecs):
    flat = jnp.concatenate([v.reshape(-1) for v in vecs])
    return jnp.pad(flat, (0, PK_ROWS * 128 - flat.shape[0])).reshape(PK_ROWS, 128)


def _unpack(pk, shapes):
    flat = pk.reshape(-1)
    out, off = [], 0
    for s in shapes:
        n = 1
        for d in s:
            n *= d
        out.append(flat[off:off + n].reshape(s))
        off += n
    return out


def _full_from_slabs(name, slabs):
    if name in COL_SHARDED:
        return slabs.transpose(1, 0, 2).reshape(slabs.shape[1], -1)
    return slabs.reshape(-1, slabs.shape[2])


def _slabs_from_full(name, g):
    if name in COL_SHARDED:
        return g.reshape(g.shape[0], N_SHARD, -1).transpose(1, 0, 2)
    return g.reshape(N_SHARD, -1, g.shape[1])


def kernel(x, mem, norm_mix_pre, w_in, conv_w, conv_b, dt_bias, a_log, d_skip, ssd_norm, norm_mem, w_mem_kv, w_sb_out, w_ssd_out, w_mem_out, w_o, norm_mix_post, norm_mlp_pre, w_up, w_down, norm_mlp_post, loss_target, m_norm_mix_pre, m_w_in, m_conv_w, m_conv_b, m_dt_bias, m_a_log, m_d_skip, m_ssd_norm, m_norm_mem, m_w_mem_kv, m_w_sb_out, m_w_ssd_out, m_w_mem_out, m_w_o, m_norm_mix_post, m_norm_mlp_pre, m_w_up, m_w_down, m_norm_mlp_post, v_norm_mix_pre, v_w_in, v_conv_w, v_conv_b, v_dt_bias, v_a_log, v_d_skip, v_ssd_norm, v_norm_mem, v_w_mem_kv, v_w_sb_out, v_w_ssd_out, v_w_mem_out, v_w_o, v_norm_mix_post, v_norm_mlp_pre, v_w_up, v_w_down, v_norm_mlp_post):
    env = dict(locals())
    w = {n: env[n] for n in WEIGHTS}
    mo = {n: env["m_" + n] for n in WEIGHTS}
    vo = {n: env["v_" + n] for n in WEIGHTS}
    shard = 2 * lax.axis_index("x") + lax.axis_index("y")

    first = _gather_two_level([w[n][0].astype(bf16) for n in FIRST], "gather_first")
    w_main, w_dt = _to_internal(_full_from_slabs("w_in", first[0]))
    wts = dict(w_main=w_main, w_dt=w_dt, w_mem_kv=_full_from_slabs("w_mem_kv", first[1]))
    late_ride = _Ride([w[n][0].astype(bf16) for n in LATE] + [w["conv_w"][0]], "gather")

    def late_weights(lands):
        full = {n: _full_from_slabs(n, s) for n, s in zip(LATE, lands)}
        return dict(full, conv_w=lands[-1].transpose(1, 0, 2).reshape(CONV_K, CONV_DIM))

    small = {n: w[n] for n in SMALL if n != "conv_w"}
    loss, grad_x, g = _local_step(
        x[0], mem[0], loss_target[0], wts, late_ride, late_weights, small,
        lambda g: _Ride([_slabs_from_full(n, g[n]).astype(bf16) for n in REST], "scatter"))
    g["w_in"] = _from_internal(g.pop("w_main"), g.pop("w_dt"))

    lands = list(_exchange([_slabs_from_full("w_in", g["w_in"]).astype(bf16)], "scatter", "scatter_w_in"))
    lands += g["rest_lands"]
    packets = _exchange_packets(_pack([g[n] for n in SMALL] + [loss[:, :1]]))
    partial = []
    for n, r in zip(BIG, lands):
        own = lax.dynamic_index_in_dim(_slabs_from_full(n, g[n]), shard, 0, keepdims=False)
        partial.append(_sum_parts(own, r, name="sum_chips_" + n))
    other = _swap_sibling(partial, "swap_sibling")

    out_g, out_d, out_m, out_v = {}, {}, {}, {}
    for n, p, q in zip(BIG, partial, other):
        res = _adamw(w[n][0], mo[n][0], vo[n][0], [p, q], name="adamw_" + n)
        out_g[n], out_d[n], out_m[n], out_v[n] = [r[None] for r in res]
    tot = _sum_parts(None, packets, name="sum_packets")
    shapes = [g[n].shape for n in SMALL] + [(1, 1)]
    sm = dict(zip(SMALL + ("loss",), _unpack(tot, shapes)))
    sm["conv_w"] = lax.dynamic_slice_in_dim(sm["conv_w"], shard * (CONV_DIM // N_SHARD), CONV_DIM // N_SHARD, axis=1)
    own_small = lambda d: _pack([d[n].reshape(sm[n].shape) for n in SMALL])
    res = _adamw(own_small(w), own_small(mo), own_small(vo), [own_small(sm)], name="adamw_small")
    own_shapes = [sm[n].shape for n in SMALL]
    for store, r in zip((out_g, out_d, out_m, out_v), res):
        for n, val in zip(SMALL, _unpack(r, own_shapes)):
            store[n] = val.reshape(w[n].shape)

    outs = [sm["loss"].reshape(()), grad_x[None]]
    for store in (out_g, out_d, out_m, out_v):
        outs += [store[n] for n in WEIGHTS]
    return tuple(outs)
```

```python
import functools

import jax
import jax.numpy as jnp
from jax import lax
from jax.experimental import pallas as pl
from jax.experimental.pallas import tpu as pltpu

f32 = jnp.float32
bf16 = jnp.bfloat16

D = 1024
EPS = 1e-6
SB_HD = 64
SSD_INNER = 2048
SSD_HEADS = 32
SSD_GROUPS = 4
SSD_N = 128
SSD_L = 128
CONV_K = 4
CONV_DIM = 3072
MEM_HEADS = 4
MEM_HD = 256
D_FF = 4096
D_IN = 12320
N_SHARD = 4
N_DEV = 8

P_QKV, P_XBC, P_GATE, P_MEMQ, P_Z, P_DT, P_TOT = 0, 3072, 6144, 9216, 10240, 12288, 12416
R_QKV, R_Z, R_XBC, R_DT, R_MEMQ, R_GATE = (0, 3072), (3072, 5120), (5120, 8192), (8192, 8224), (8224, 9248), (9248, 12320)

ADAM_LR = 0.001
ADAM_B1 = 0.9
ADAM_B2 = 0.999
ADAM_EPS = 1e-08
ADAM_WD = 0.01
ADAM_STEP = 10

VMEM_LIMIT = 56 * 1024 * 1024

NN = (((1,), (0,)), ((), ()))
NT = (((1,), (1,)), ((), ()))
TN = (((0,), (0,)), ((), ()))


def _dot(a, b, dims=NN):
    return lax.dot_general(a, b, dims, preferred_element_type=f32)


def _params(sem=None):
    return pltpu.CompilerParams(dimension_semantics=sem, vmem_limit_bytes=VMEM_LIMIT)


def _sigmoid(x):
    return 1.0 / (1.0 + jnp.exp(-x))


def _split2(x):
    hi = x.astype(bf16)
    lo = (x - hi.astype(f32)).astype(bf16)
    return hi, lo


def _split3(x):
    hi = x.astype(bf16)
    r = x - hi.astype(f32)
    mid = r.astype(bf16)
    lo = (r - mid.astype(f32)).astype(bf16)
    return hi, mid, lo


def _mm(a, b, mode, *, tm, tn, name, out_dtypes=(f32,), epi=None, extras=(), ride=None):
    M = a.shape[1] if mode == "tn" else a.shape[0]
    N = b.shape[0] if mode == "nt" else b.shape[1]
    tm, tn = min(tm, M), min(tn, N)
    if mode == "nn":
        (M, K), N = a.shape, b.shape[1]
        a_spec = pl.BlockSpec((tm, K), lambda i, j: (i, 0))
        b_spec = pl.BlockSpec((K, tn), lambda i, j: (0, j))
        dims = NN
    elif mode == "nt":
        (M, K), N = a.shape, b.shape[0]
        a_spec = pl.BlockSpec((tm, K), lambda i, j: (i, 0))
        b_spec = pl.BlockSpec((tn, K), lambda i, j: (j, 0))
        dims = NT
    else:
        (K, M), N = a.shape, b.shape[1]
        a_spec = pl.BlockSpec((K, tm), lambda i, j: (0, i))
        b_spec = pl.BlockSpec((K, tn), lambda i, j: (0, j))
        dims = TN
    assert M % tm == 0 and N % tn == 0, (name, M, N, tm, tn)
    n_ex, n_out = len(extras), len(out_dtypes)
    n_r = ride.n if ride else 0
    o_spec = pl.BlockSpec((tm, tn), lambda i, j: (i, j))
    grid = (M // tm, N // tn)

    def body(a_ref, b_ref, *rest):
        r_ins = rest[n_ex:n_ex + n_r]
        outs = rest[n_ex + n_r:n_ex + n_r + n_out]
        r_lnd, r_sems = rest[n_ex + n_r + n_out:n_ex + 2 * n_r + n_out], rest[n_ex + 2 * n_r + n_out:]
        i, j = pl.program_id(0), pl.program_id(1)
        if ride:
            pl.when((i == 0) & (j == 0))(lambda: ride.start(r_ins, r_lnd, r_sems))
        acc = _dot(a_ref[...].astype(bf16), b_ref[...].astype(bf16), dims)
        res = (acc,) if epi is None else epi(acc, *[e[...] for e in rest[:n_ex]])
        for o_ref, r in zip(outs, res):
            o_ref[...] = r.astype(o_ref.dtype)
        if ride:
            pl.when((i == grid[0] - 1) & (j == grid[1] - 1))(lambda: ride.finish(r_ins, r_lnd, r_sems))

    out = pl.pallas_call(
        body, name=name, grid=grid,
        in_specs=[a_spec, b_spec] + [o_spec] * n_ex + (ride.in_specs if ride else []),
        out_specs=[o_spec] * n_out + (ride.out_specs if ride else []),
        out_shape=[jax.ShapeDtypeStruct((M, N), dt) for dt in out_dtypes] + (ride.out_shape if ride else []),
        scratch_shapes=ride.scratch if ride else [],
        compiler_params=_params(("arbitrary", "arbitrary") if ride else ("parallel", "parallel")),
    )(a, b, *extras, *(ride.srcs if ride else []))
    if ride:
        return (out[0] if n_out == 1 else out[:n_out]), list(out[n_out:])
    return out[0] if n_out == 1 else out


def _mm_pieces_nt(pieces, b, add, *, tm, tn, name):
    M, N = pieces[0].shape[0], b.shape[0]
    n_p = len(pieces)
    o_spec = pl.BlockSpec((tm, tn), lambda i, j: (i, j))

    def body(*refs):
        b_ref, add_ref, o_ref = refs[n_p:]
        acc, off = add_ref[...], 0
        for r in refs[:n_p]:
            acc = acc + _dot(r[...], b_ref[:, off:off + r.shape[1]], NT)
            off += r.shape[1]
        o_ref[...] = acc

    return pl.pallas_call(
        body, name=name, grid=(M // tm, N // tn),
        in_specs=[pl.BlockSpec((tm, p.shape[1]), lambda i, j: (i, 0)) for p in pieces]
        + [pl.BlockSpec((tn, b.shape[1]), lambda i, j: (j, 0)), o_spec],
        out_specs=o_spec, out_shape=jax.ShapeDtypeStruct((M, N), f32),
        compiler_params=_params(("parallel", "parallel")),
    )(*pieces, b, add)


def _rms_fwd(x, g, *, name, out_dtype, residual=None, tm=512):
    S, C = x.shape
    tm = min(tm, S)
    has_res = residual is not None

    def body(x_ref, g_ref, *rest):
        xv = x_ref[...]
        r = lax.rsqrt(jnp.mean(xv * xv, axis=1, keepdims=True) + EPS)
        y = xv * r * g_ref[...]
        if has_res:
            y = y + rest[0][...]
        rest[-1][...] = y.astype(out_dtype)

    row = pl.BlockSpec((tm, C), lambda i: (i, 0))
    vec = pl.BlockSpec((1, C), lambda i: (0, 0))
    args = (x, g) + ((residual,) if has_res else ())
    return pl.pallas_call(
        body, name=name, grid=(S // tm,),
        in_specs=[row, vec] + ([row] if has_res else []),
        out_specs=row, out_shape=jax.ShapeDtypeStruct((S, C), out_dtype),
        compiler_params=_params(("parallel",)),
    )(*args)


def _rms_bwd(x, dy, g, *, name, out_dtype, add=None, tm=512):
    S, C = x.shape
    tm = min(tm, S)
    has_add = add is not None

    def body(x_ref, dy_ref, g_ref, *rest):
        dx_ref, dg_ref = rest[-2], rest[-1]
        xv = x_ref[...]
        dyv = dy_ref[...].astype(f32)
        r = lax.rsqrt(jnp.mean(xv * xv, axis=1, keepdims=True) + EPS)
        xh = xv * r
        dxh = dyv * g_ref[...]
        dx = r * (dxh - xh * jnp.mean(dxh * xh, axis=1, keepdims=True))
        if has_add:
            dx = dx + rest[0][...]
        dx_ref[...] = dx.astype(out_dtype)

        @pl.when(pl.program_id(0) == 0)
        def _():
            dg_ref[...] = jnp.zeros_like(dg_ref)

        dg_ref[...] += jnp.sum(dyv * xh, axis=0, keepdims=True)

    row = pl.BlockSpec((tm, C), lambda i: (i, 0))
    vec = pl.BlockSpec((1, C), lambda i: (0, 0))
    args = (x, dy, g) + ((add,) if has_add else ())
    return pl.pallas_call(
        body, name=name, grid=(S // tm,),
        in_specs=[row, row, vec] + ([row] if has_add else []),
        out_specs=[row, vec],
        out_shape=[jax.ShapeDtypeStruct((S, C), out_dtype), jax.ShapeDtypeStruct((1, C), f32)],
        compiler_params=_params(("arbitrary",)),
    )(*args)


SB_T = 128
SB_SPENT = -120.0
SB_TAIL = 4
SB_GROUPS = (4, 2, 1)
SB_GROUPS_BWD = (4, 2, 1)


def _sb_masks():
    lane = lax.broadcasted_iota(jnp.int32, (1, 128), 1)
    m_a = (lane < SB_HD).astype(f32)
    return m_a, 1.0 - m_a


def _chunks(a, n):
    return [a[:, u * SB_T:(u + 1) * SB_T] for u in range(n)]


def _cat(parts, axis):
    return parts[0] if len(parts) == 1 else jnp.concatenate(parts, axis=axis)


def _mask_last(a, n, mask):
    if mask is None:
        return a
    parts = _chunks(a, n)
    return _cat(parts[:-1] + [jnp.where(mask, parts[-1], 0.0)], 1)


def _sb_logits(z, n, mask):
    l1p = jnp.log(1.0 + jnp.exp(-jnp.abs(z)))
    lb = jnp.minimum(z, 0.0) - l1p
    return lb, _mask_last(lb - z, n, mask)


def _by_count(i, most, fn):
    return lax.switch(jnp.minimum(i, most - 1), [functools.partial(fn, n) for n in range(1, most + 1)])


def _chunk_matmul(parts_list, u_mat):
    out = _dot(_cat(parts_list, 0), u_mat)
    return [out[u * SB_T:(u + 1) * SB_T] for u in range(len(parts_list))]


def _chunk_cumsum(lk, n, u_mat):
    hi = lk.astype(bf16)
    lo = (lk - hi.astype(f32)).astype(bf16)
    out = _chunk_matmul(_chunks(hi, n) + _chunks(lo, n), u_mat)
    return [out[u] + out[n + u] for u in range(n)]


def _sb_fwd(proj, S, ride=None):
    nq = S // SB_T
    n_pairs = D // 128
    scale = SB_HD ** -0.5
    n_r = ride.n if ride else 0

    def body(q_ref, k_ref, v_ref, *rest):
        o_ref, t_ref = rest[n_r:n_r + 2]
        i = pl.program_id(1)
        if ride:
            pl.when((pl.program_id(0) == 0) & (i == 0))(
                lambda: ride.start(rest[:n_r], rest[n_r + 2:2 * n_r + 2], rest[2 * n_r + 2:]))
        m_a, m_b = _sb_masks()
        r_i = lax.broadcasted_iota(jnp.int32, (SB_T, SB_T), 0)
        c_i = lax.broadcasted_iota(jnp.int32, (SB_T, SB_T), 1)
        u_mat = (r_i > c_i).astype(bf16)
        causal = c_i < r_i
        q = q_ref[...] * scale
        q_h = ((q * m_a).astype(bf16), (q * m_b).astype(bf16))

        def group(j_lo, n, carry, mask):
            acc, c_a, c_b = carry
            rows = pl.ds(pl.multiple_of(j_lo * SB_T, SB_T), n * SB_T)
            k = k_ref[rows, :].astype(bf16)
            v = v_ref[rows, :]
            zs = [_dot(q_b, k, NT) for q_b in q_h]
            lbk = [_sb_logits(z, n, mask) for z in zs]
            parts = [_chunk_cumsum(lk, n, u_mat) for _, lk in lbk]
            ws, cs = [], []
            for (lb, lk), part, c in zip(lbk, parts, (c_a, c_b)):
                lb_c, lk_c = _chunks(lb, n), _chunks(lk, n)
                w_c = [None] * n
                for u in reversed(range(n)):
                    w_c[u] = jnp.exp(lb_c[u] + c + part[u])
                    c = c + jnp.sum(lk_c[u], axis=1, keepdims=True)
                ws.append(_mask_last(_cat(w_c, 1), n, mask).astype(bf16))
                cs.append(c)
            for w, m in zip(ws, (m_a, m_b)):
                acc = acc + _dot(w, (v * m).astype(bf16))
            return acc, cs[0], cs[1]

        zero_c = jnp.zeros((SB_T, 1), f32)
        init = (jnp.zeros((SB_T, 128), f32), zero_c, zero_c)
        carry = _by_count(i, SB_TAIL, lambda n: group(i - n + 1, n, init, causal))

        def spent(cr):
            return (jnp.max(jnp.maximum(cr[1], cr[2])) < SB_SPENT).astype(jnp.int32)

        state = (i - jnp.minimum(i, SB_TAIL - 1), spent(carry), carry)
        for n in SB_GROUPS:
            def step(st, n=n):
                left, _, cr = st
                cr = group(left - n, n, cr, None)
                return left - n, spent(cr), cr

            state = lax.while_loop(lambda st, n=n: (st[0] >= n) & (st[1] == 0), step, state)
        left, _, carry = state
        o_ref[...] = carry[0]
        lane = lax.broadcasted_iota(jnp.int32, (1, 128), 1)
        t_ref[...] = (jnp.where(lane == 0, carry[1], 0.0) + jnp.where(lane == SB_HD, carry[2], 0.0)
                      + jnp.where(lane == 1, left.astype(f32), 0.0))
        if ride:
            pl.when((pl.program_id(0) == n_pairs - 1) & (i == nq - 1))(
                lambda: ride.finish(rest[:n_r], rest[n_r + 2:2 * n_r + 2], rest[2 * n_r + 2:]))

    qs = pl.BlockSpec((SB_T, 128), lambda h, i: (i, h))
    out = pl.pallas_call(
        body, name="sb_fwd", grid=(n_pairs, nq),
        in_specs=[qs,
                  pl.BlockSpec((S, 128), lambda h, i: (0, n_pairs + h)),
                  pl.BlockSpec((S, 128), lambda h, i: (0, 2 * n_pairs + h))] + (ride.in_specs if ride else []),
        out_specs=[qs, qs] + (ride.out_specs if ride else []),
        out_shape=[jax.ShapeDtypeStruct((S, D), f32)] * 2 + (ride.out_shape if ride else []),
        scratch_shapes=ride.scratch if ride else [],
        compiler_params=_params(("arbitrary", "arbitrary")),
    )(proj, proj, proj, *(ride.srcs if ride else []))
    return out[0], out[1], list(out[2:])


def _sb_bwd(proj, tot_lk, do, S, ride=None):
    nq = S // SB_T
    n_pairs = D // 128
    scale = SB_HD ** -0.5
    n_r = ride.n if ride else 0

    def body(q_ref, k_ref, v_ref, t_ref, do_ref, *rest):
        dq_ref, dk_ref, dv_ref = rest[n_r:n_r + 3]
        dk_acc, dv_acc = rest[2 * n_r + 3:2 * n_r + 5]
        r_ins, r_lnd, r_sems = rest[:n_r], rest[n_r + 3:2 * n_r + 3], rest[2 * n_r + 5:]
        i = pl.program_id(1)
        if ride:
            pl.when((pl.program_id(0) == 0) & (i == 0))(lambda: ride.start(r_ins, r_lnd, r_sems))
        m_a, m_b = _sb_masks()
        r_i = lax.broadcasted_iota(jnp.int32, (SB_T, SB_T), 0)
        c_i = lax.broadcasted_iota(jnp.int32, (SB_T, SB_T), 1)
        u_inc = (r_i <= c_i).astype(bf16)
        u_exc = (r_i < c_i).astype(bf16)
        causal = c_i < r_i

        @pl.when(i == 0)
        def _():
            dk_acc[...] = jnp.zeros_like(dk_acc)
            dv_acc[...] = jnp.zeros_like(dv_acc)

        q = q_ref[...] * scale
        dov = do_ref[...]
        tv = t_ref[...]
        lane = lax.broadcasted_iota(jnp.int32, (1, 128), 1)
        heads = []
        for m, first in ((m_a, 0), (m_b, SB_HD)):
            tot = jnp.sum(jnp.where(lane == first, tv, 0.0), axis=1, keepdims=True)
            heads.append(((q * m).astype(bf16), (dov * m).astype(bf16), tot, m))
        lowest = jnp.clip(jnp.max(jnp.where(lane == 1, tv, 0.0)).astype(jnp.int32), 0, i)

        def group(j_lo, n, carry, mask):
            dq_acc, cp_a, cp_b, ce_a, ce_b = carry
            rows = pl.ds(pl.multiple_of(j_lo * SB_T, SB_T), n * SB_T)
            k_f = k_ref[rows, :]
            k = k_f.astype(bf16)
            v = v_ref[rows, :].astype(bf16)
            zs = [_dot(h[0], k, NT) for h in heads]
            dws = [_dot(h[1], v, NT) for h in heads]
            lbk = [_sb_logits(z, n, mask) for z in zs]
            parts = [_chunk_cumsum(lk, n, u_inc) for _, lk in lbk]
            ws, es, cps = [], [], []
            for (lb, lk), part, dw, h, cp in zip(lbk, parts, dws, heads, (cp_a, cp_b)):
                lb_c, lk_c = _chunks(lb, n), _chunks(lk, n)
                w_c = []
                for u in range(n):
                    w_c.append(jnp.exp(lb_c[u] + (h[2] - cp) - part[u]))
                    cp = cp + jnp.sum(lk_c[u], axis=1, keepdims=True)
                w = _mask_last(_cat(w_c, 1), n, mask)
                ws.append(w)
                es.append(dw * w)
                cps.append(cp)
            e_parts = [_chunk_matmul(_chunks(e.astype(bf16), n), u_exc) for e in es]
            dzs, ces = [], []
            for (lb, _), e, e_part, ce in zip(lbk, es, e_parts, (ce_a, ce_b)):
                e_c = _chunks(e, n)
                big_c = []
                for u in range(n):
                    big_c.append(ce + e_part[u])
                    ce = ce + jnp.sum(e_c[u], axis=1, keepdims=True)
                sig = jnp.exp(lb)
                dz = _mask_last(e * (1.0 - sig) - _cat(big_c, 1) * sig, n, mask)
                dzs.append(dz.astype(bf16))
                ces.append(ce)
            dk_t = jnp.zeros((n * SB_T, 128), f32)
            dv_t = jnp.zeros((n * SB_T, 128), f32)
            for dz_b, w, h in zip(dzs, ws, heads):
                dq_acc = dq_acc + _dot(dz_b, (k_f * h[3]).astype(bf16))
                dk_t = dk_t + _dot(dz_b, h[0], TN)
                dv_t = dv_t + _dot(w.astype(bf16), h[1], TN)
            dk_acc[rows, :] += dk_t
            dv_acc[rows, :] += dv_t
            return dq_acc, cps[0], cps[1], ces[0], ces[1]

        zc = jnp.zeros((SB_T, 1), f32)
        carry = (jnp.zeros((SB_T, 128), f32), zc, zc, zc, zc)
        done = lowest
        tail_lo = i - jnp.minimum(i, SB_TAIL - 1)
        for n in SB_GROUPS_BWD:
            trips = (tail_lo - done) // n
            carry = lax.fori_loop(
                0, trips, functools.partial(lambda gi, cr, n, done: group(done + gi * n, n, cr, None), n=n, done=done),
                carry)
            done = done + trips * n
        carry = _by_count(i, SB_TAIL, lambda n: group(i - n + 1, n, carry, causal))
        dq_ref[...] = (carry[0] * scale).astype(bf16)

        @pl.when(i == nq - 1)
        def _():
            dk_ref[...] = dk_acc[...].astype(bf16)
            dv_ref[...] = dv_acc[...].astype(bf16)

        if ride:
            pl.when((pl.program_id(0) == n_pairs - 1) & (i == nq - 1))(
                lambda: ride.finish(r_ins, r_lnd, r_sems))

    qs = pl.BlockSpec((SB_T, 128), lambda h, i: (i, h))
    full = pl.BlockSpec((S, 128), lambda h, i: (0, h))
    out = pl.pallas_call(
        body, name="sb_bwd", grid=(n_pairs, nq),
        in_specs=[qs,
                  pl.BlockSpec((S, 128), lambda h, i: (0, n_pairs + h)),
                  pl.BlockSpec((S, 128), lambda h, i: (0, 2 * n_pairs + h)),
                  qs, qs] + (ride.in_specs if ride else []),
        out_specs=[qs, full, full] + (ride.out_specs if ride else []),
        out_shape=[jax.ShapeDtypeStruct((S, D), bf16)] * 3 + (ride.out_shape if ride else []),
        scratch_shapes=[pltpu.VMEM((S, 128), f32), pltpu.VMEM((S, 128), f32)] + (ride.scratch if ride else []),
        compiler_params=_params(("arbitrary", "arbitrary")),
    )(proj, proj, proj, tot_lk, do, *(ride.srcs if ride else []))
    return out[0], out[1], out[2], list(out[3:])


CONV_CB = 256
HALO = 8


def _conv_fwd(proj, conv_w, conv_b, S):
    tr = min(512, S)

    def body(x_ref, w_ref, b_ref, xc_ref, xbc_ref):
        w = w_ref[...]
        for t in range(S // tr):
            cur = x_ref[t * tr:(t + 1) * tr, :]
            halo = x_ref[t * tr - HALO:t * tr, :] if t else jnp.zeros((HALO, CONV_CB), f32)
            win = jnp.concatenate([halo, cur], axis=0)
            acc = b_ref[...] + w[CONV_K - 1:CONV_K, :] * cur
            for k in range(CONV_K - 1):
                acc = acc + w[k:k + 1, :] * pltpu.roll(win, CONV_K - 1 - k, 0)[HALO:, :]
            xc_ref[t * tr:(t + 1) * tr, :] = acc
            xbc_ref[t * tr:(t + 1) * tr, :] = acc * _sigmoid(acc)

    col = pl.BlockSpec((S, CONV_CB), lambda c: (0, c))
    return pl.pallas_call(
        body, name="conv_fwd", grid=(CONV_DIM // CONV_CB,),
        in_specs=[pl.BlockSpec((S, CONV_CB), lambda c: (0, P_XBC // CONV_CB + c)),
                  pl.BlockSpec((CONV_K, CONV_CB), lambda c: (0, c)),
                  pl.BlockSpec((1, CONV_CB), lambda c: (0, c))],
        out_specs=[col, col], out_shape=[jax.ShapeDtypeStruct((S, CONV_DIM), f32)] * 2,
        compiler_params=_params(("parallel",)),
    )(proj, conv_w, conv_b)


def _conv_bwd(proj, xc, dxbc, conv_w, S):
    tr = min(512, S)

    def body(x_ref, xc_ref, dy_ref, w_ref, dx_ref, dw_ref, db_ref, dxc_s):
        w = w_ref[...]
        xcv = xc_ref[...]
        sg = _sigmoid(xcv)
        dxc_s[0:S, :] = dy_ref[...] * (sg * (1.0 + xcv * (1.0 - sg)))
        dxc_s[S:S + HALO, :] = jnp.zeros((HALO, CONV_CB), f32)
        dws = [jnp.zeros((1, CONV_CB), f32) for _ in range(CONV_K)]
        db = jnp.zeros((1, CONV_CB), f32)
        for t in range(S // tr):
            cur = x_ref[t * tr:(t + 1) * tr, :]
            halo = x_ref[t * tr - HALO:t * tr, :] if t else jnp.zeros((HALO, CONV_CB), f32)
            win = jnp.concatenate([halo, cur], axis=0)
            dwin = dxc_s[t * tr:(t + 1) * tr + HALO, :]
            dcur = dwin[0:tr, :]
            db = db + jnp.sum(dcur, axis=0, keepdims=True)
            dws[CONV_K - 1] = dws[CONV_K - 1] + jnp.sum(dcur * cur, axis=0, keepdims=True)
            dx = w[CONV_K - 1:CONV_K, :] * dcur
            for k in range(CONV_K - 1):
                sh = CONV_K - 1 - k
                dws[k] = dws[k] + jnp.sum(dcur * pltpu.roll(win, sh, 0)[HALO:, :], axis=0, keepdims=True)
                dx = dx + w[k:k + 1, :] * pltpu.roll(dwin, tr + HALO - sh, 0)[0:tr, :]
            dx_ref[t * tr:(t + 1) * tr, :] = dx.astype(bf16)
        dw_ref[...] = jnp.concatenate(dws + [jnp.zeros((8 - CONV_K, CONV_CB), f32)], axis=0)
        db_ref[...] = db

    col = pl.BlockSpec((S, CONV_CB), lambda c: (0, c))
    return pl.pallas_call(
        body, name="conv_bwd", grid=(CONV_DIM // CONV_CB,),
        in_specs=[pl.BlockSpec((S, CONV_CB), lambda c: (0, P_XBC // CONV_CB + c)), col, col,
                  pl.BlockSpec((CONV_K, CONV_CB), lambda c: (0, c))],
        out_specs=[col, pl.BlockSpec((8, CONV_CB), lambda c: (0, c)), pl.BlockSpec((1, CONV_CB), lambda c: (0, c))],
        out_shape=[jax.ShapeDtypeStruct((S, CONV_DIM), bf16), jax.ShapeDtypeStruct((8, CONV_DIM), f32),
                   jax.ShapeDtypeStruct((1, CONV_DIM), f32)],
        scratch_shapes=[pltpu.VMEM((S + HALO, CONV_CB), f32)],
        compiler_params=_params(("parallel",)),
    )(proj, xc, dxbc, conv_w)


N_PAIR = SSD_HEADS // 2
NEG = -1e30


def _softplus(x):
    return jnp.maximum(x, 0.0) + jnp.log(1.0 + jnp.exp(-jnp.abs(x)))


def _ssd_common(dtr, dtb, alog):
    L = SSD_L
    r_i = lax.broadcasted_iota(jnp.int32, (L, L), 0)
    c_i = lax.broadcasted_iota(jnp.int32, (L, L), 1)
    dt = _softplus(dtr + dtb)
    a = -jnp.exp(alog)
    da = dt * a
    lower = (r_i >= c_i).astype(bf16)
    upper = (r_i <= c_i).astype(bf16)
    parts = _split3(da)
    a_cs = sum(_dot(lower, p) for p in parts)
    a_cs_t = sum(_dot(p, upper, TN) for p in parts)
    return dt, a, a_cs, a_cs_t, r_i >= c_i


def _pair_vec(lane, v, h):
    return jnp.where(lane < SB_HD, v[:, h:h + 1], v[:, h + 1:h + 2])


def _decay_mat(a_cs, a_cs_t, h, tril):
    return jnp.exp(jnp.where(tril, a_cs[:, h:h + 1] - a_cs_t[h:h + 1, :], NEG))


def _ssd_fwd(xbc, proj, pdt, dt_bias_p, a_log_p, d_skip_c, ssd_norm, S):
    L = SSD_L
    nc = S // L

    def body(xbc_ref, dt_ref, z_ref, dtb_ref, alog_ref, dsk_ref, gn_ref, y_ref, yn_ref, hp_ref, state):
        c = pl.program_id(0)

        @pl.when(c == 0)
        def _():
            state[...] = jnp.zeros_like(state)

        hp_ref[0] = state[...]
        lane = lax.broadcasted_iota(jnp.int32, (1, 128), 1)
        row128 = lax.broadcasted_iota(jnp.int32, (128, 1), 0)
        m_a, m_b = _sb_masks()
        dt, a, a_cs, a_cs_t, tril = _ssd_common(dt_ref[...], dtb_ref[...], alog_ref[...])
        a_last = a_cs[L - 1:L, :]
        for g in range(SSD_GROUPS):
            b_g = xbc_ref[:, SSD_INNER + g * SSD_N:SSD_INNER + (g + 1) * SSD_N].astype(bf16)
            c_g = xbc_ref[:, SSD_INNER + (SSD_GROUPS + g) * SSD_N:SSD_INNER + (SSD_GROUPS + g + 1) * SSD_N].astype(bf16)
            cb = _dot(c_g, b_g, NT)
            for pr in range(4):
                h = 8 * g + 2 * pr
                pi = h // 2
                cols = slice(pi * 128, (pi + 1) * 128)
                xs = xbc_ref[:, cols]
                x = xs * _pair_vec(lane, dt, h)
                acs = _pair_vec(lane, a_cs, h)
                al = _pair_vec(lane, a_last, h)
                w_a = (cb * _decay_mat(a_cs, a_cs_t, h, tril)).astype(bf16)
                w_b = (cb * _decay_mat(a_cs, a_cs_t, h + 1, tril)).astype(bf16)
                yd = _dot(w_a, (x * m_a).astype(bf16)) + _dot(w_b, (x * m_b).astype(bf16))
                hp = state[pi]
                yo = _dot(c_g, hp.astype(bf16), NT) * jnp.exp(acs)
                y_ref[:, cols] = yd + yo + dsk_ref[:, cols] * xs
                dec = jnp.exp(jnp.where(row128 < SB_HD, a_last[:, h:h + 1], a_last[:, h + 1:h + 2]))
                state[pi] = hp * dec + _dot((x * jnp.exp(al - acs)).astype(bf16), b_g, TN)
        zz = z_ref[...]
        y2 = y_ref[...] * (zz * _sigmoid(zz))
        gw = SSD_INNER // SSD_GROUPS
        for g in range(SSD_GROUPS):
            yg = y2[:, g * gw:(g + 1) * gw]
            rg = lax.rsqrt(jnp.mean(yg * yg, axis=1, keepdims=True) + EPS)
            yn_ref[:, g * gw:(g + 1) * gw] = (yg * rg * gn_ref[:, g * gw:(g + 1) * gw]).astype(bf16)

    vec128 = pl.BlockSpec((1, 128), lambda c: (0, 0))
    vecin = pl.BlockSpec((1, SSD_INNER), lambda c: (0, 0))
    rows = pl.BlockSpec((L, SSD_INNER), lambda c: (c, 0))
    return pl.pallas_call(
        body, name="ssd_fwd", grid=(nc,),
        in_specs=[pl.BlockSpec((L, CONV_DIM), lambda c: (c, 0)),
                  pl.BlockSpec((L, 128), lambda c: (c, 0)),
                  pl.BlockSpec((L, SSD_INNER), lambda c: (c, P_Z // SSD_INNER)),
                  vec128, vec128, vecin, vecin],
        out_specs=[rows, rows, pl.BlockSpec((1, N_PAIR, 128, SSD_N), lambda c: (c, 0, 0, 0))],
        out_shape=[jax.ShapeDtypeStruct((S, SSD_INNER), f32), jax.ShapeDtypeStruct((S, SSD_INNER), bf16),
                   jax.ShapeDtypeStruct((nc, N_PAIR, 128, SSD_N), f32)],
        scratch_shapes=[pltpu.VMEM((N_PAIR, 128, SSD_N), f32)],
        compiler_params=_params(("arbitrary",)),
    )(xbc, pdt, proj, dt_bias_p, a_log_p, d_skip_c, ssd_norm)


def _sum_all(v):
    return jnp.sum(jnp.sum(v, axis=1, keepdims=True), axis=0, keepdims=True)


def _ssd_bwd(dyn, y, xbc, proj, pdt, hprev, dt_bias_p, a_log_p, d_skip_c, ssd_norm, S, ride=None):
    L = SSD_L
    nc = S // L
    n_r = ride.n if ride else 0

    col = lax.broadcasted_iota(jnp.int32, (2 * SSD_INNER, 128), 0)
    head = lax.broadcasted_iota(jnp.int32, (2 * SSD_INNER, 128), 1)
    sel_pair = (col[:SSD_INNER] // SB_HD == head[:SSD_INNER]).astype(bf16)
    sel_head = (col // 128 == head).astype(bf16)

    def body(*refs):
        (dyn_ref, y_ref, xbc_ref, dt_ref, z_ref, hp_ref, dtb_ref, alog_ref, dsk_ref, gn_ref,
         selp_ref, selh_ref) = refs[:12]
        dz_ref, dxbc_ref, ddt_ref, dgn_ref, dsk_out, dalog_ref, ddtb_ref = refs[12 + n_r:19 + n_r]
        dstate, dy_s, st_a, st_q, st_d, st_x, dat = refs[19 + 2 * n_r:26 + 2 * n_r]
        r_ins, r_lnd, r_sems = refs[12:12 + n_r], refs[19 + n_r:19 + 2 * n_r], refs[26 + 2 * n_r:]
        c = pl.program_id(0)
        if ride:
            pl.when(c == 0)(lambda: ride.start(r_ins, r_lnd, r_sems))

        @pl.when(c == 0)
        def _():
            dat[...] = jnp.zeros_like(dat)
            dstate[...] = jnp.zeros_like(dstate)
            dgn_ref[...] = jnp.zeros_like(dgn_ref)
            dsk_out[...] = jnp.zeros_like(dsk_out)
            dalog_ref[...] = jnp.zeros_like(dalog_ref)
            ddtb_ref[...] = jnp.zeros_like(ddtb_ref)

        lane = lax.broadcasted_iota(jnp.int32, (1, 128), 1)
        row128 = lax.broadcasted_iota(jnp.int32, (128, 1), 0)
        rowl = lax.broadcasted_iota(jnp.int32, (L, 1), 0)
        m_a, m_b = _sb_masks()
        dtr = dt_ref[...]
        dt, a, a_cs, a_cs_t, tril = _ssd_common(dtr, dtb_ref[...], alog_ref[...])
        a_last = a_cs[L - 1:L, :]

        zz = z_ref[...]
        sg = _sigmoid(zz)
        silu = zz * sg
        yv = y_ref[...]
        y2 = yv * silu
        gw = SSD_INNER // SSD_GROUPS
        for g in range(SSD_GROUPS):
            sl = slice(g * gw, (g + 1) * gw)
            yg = y2[:, sl]
            rg = lax.rsqrt(jnp.mean(yg * yg, axis=1, keepdims=True) + EPS)
            yh = yg * rg
            dyn_g = dyn_ref[:, sl]
            dgn_ref[:, sl] += jnp.sum(dyn_g * yh, axis=0, keepdims=True)
            dyh = dyn_g * gn_ref[:, sl]
            dy2 = rg * (dyh - yh * jnp.mean(dyh * yh, axis=1, keepdims=True))
            dy_s[:, sl] = dy2 * silu[:, sl]
            dz_ref[:, sl] = (dy2 * yv[:, sl] * (sg[:, sl] * (1.0 + zz[:, sl] * (1.0 - sg[:, sl])))).astype(bf16)

        last_row = jnp.zeros((1, 128), f32)
        dsk_acc = jnp.zeros((1, 128), f32)
        for g in range(SSD_GROUPS):
            bsl = slice(SSD_INNER + g * SSD_N, SSD_INNER + (g + 1) * SSD_N)
            csl = slice(SSD_INNER + (SSD_GROUPS + g) * SSD_N, SSD_INNER + (SSD_GROUPS + g + 1) * SSD_N)
            b_g = xbc_ref[:, bsl].astype(bf16)
            c_g = xbc_ref[:, csl].astype(bf16)
            cb = _dot(c_g, b_g, NT)
            dcb = jnp.zeros((L, L), f32)
            dc_g = jnp.zeros((L, SSD_N), f32)
            db_g = jnp.zeros((L, SSD_N), f32)
            for pr in range(4):
                h = 8 * g + 2 * pr
                pi = h // 2
                cols = slice(pi * 128, (pi + 1) * 128)
                xs = xbc_ref[:, cols]
                dt_p = _pair_vec(lane, dt, h)
                x = xs * dt_p
                acs = _pair_vec(lane, a_cs, h)
                al = _pair_vec(lane, a_last, h)
                e_a = jnp.exp(acs)
                dte = jnp.exp(al - acs)
                m_mat_a = _decay_mat(a_cs, a_cs_t, h, tril)
                m_mat_b = _decay_mat(a_cs, a_cs_t, h + 1, tril)
                dyp = dy_s[:, cols]
                dsk = dsk_ref[:, cols]
                d_hn = dstate[pi]
                hp = hp_ref[0, pi]
                dy_a = (dyp * m_a).astype(bf16)
                dy_b = (dyp * m_b).astype(bf16)
                x_b = x.astype(bf16)
                gm_a = _dot(dy_a, x_b, NT) * m_mat_a
                gm_b = _dot(dy_b, x_b, NT) * m_mat_b
                dcb = dcb + gm_a + gm_b
                dx_d = _dot((cb * m_mat_a).astype(bf16), dy_a, TN) + _dot((cb * m_mat_b).astype(bf16), dy_b, TN)
                dx_s = _dot(b_g, d_hn.astype(bf16), NT) * dte
                dx = dx_d + dx_s
                dxbc_ref[:, cols] = dx * dt_p + dsk * dyp
                xdxs = x * dx_s
                st_x[:, cols] = xdxs
                st_a[:, cols] = dyp * (_dot(c_g, hp.astype(bf16), NT) * e_a) - xdxs
                st_d[:, cols] = dx * xs
                hh = d_hn * hp
                dsk_row = jnp.sum(dyp * xs, axis=0, keepdims=True)
                dec = jnp.exp(jnp.where(row128 < SB_HD, a_last[:, h:h + 1], a_last[:, h + 1:h + 2]))
                for hd, m, gm in ((h, m_a, gm_a), (h + 1, m_b, gm_b)):
                    half = slice(0, SB_HD) if hd == h else slice(SB_HD, 128)
                    qm = gm * cb
                    st_q[:, hd * 128:(hd + 1) * 128] = qm
                    dat[hd:hd + 1, :] = jnp.sum(qm, axis=0, keepdims=True)
                    hh_sum = jnp.sum(jnp.sum(hh[half, :], axis=0, keepdims=True), axis=1, keepdims=True)
                    last_row = jnp.where(lane == hd, jnp.exp(a_last[:, hd:hd + 1]) * hh_sum, last_row)
                    dsk_acc = jnp.where(lane == hd, jnp.sum(dsk_row * m, axis=1, keepdims=True), dsk_acc)
                dye = (dyp * e_a).astype(bf16)
                dc_g = dc_g + _dot(dye, hp.astype(bf16))
                db_g = db_g + _dot((x * dte).astype(bf16), d_hn.astype(bf16))
                dstate[pi] = dec * d_hn + _dot(dye, c_g, TN)
            dcb_b = dcb.astype(bf16)
            dxbc_ref[:, csl] = dc_g + _dot(dcb_b, b_g)
            dxbc_ref[:, bsl] = db_g + _dot(dcb_b, c_g, TN)

        r_i = lax.broadcasted_iota(jnp.int32, (L, L), 0)
        c_i = lax.broadcasted_iota(jnp.int32, (L, L), 1)
        rev = (r_i <= c_i).astype(bf16)

        def head_sums(st, sel, split=_split2):
            return sum(_dot(p, sel[...]) for p in split(st[...]))

        last_row = last_row + jnp.sum(head_sums(st_x, selp_ref), axis=0, keepdims=True)
        d_acs = (head_sums(st_a, selp_ref) + head_sums(st_q, selh_ref, _split3)
                 + jnp.where(rowl == L - 1, last_row, 0.0))
        ddt_x = head_sums(st_d, selp_ref)
        dda = sum(_dot(rev, p) for p in _split3(d_acs)) - sum(_dot(rev, p, NT) for p in _split3(dat[...]))
        ddt = ddt_x + dda * a
        dalog_ref[...] += jnp.sum(dda * dt, axis=0, keepdims=True) * a
        ddtr = jnp.where(lane < SSD_HEADS, ddt * _sigmoid(dtr + dtb_ref[...]), 0.0)
        ddt_ref[...] = ddtr.astype(bf16)
        ddtb_ref[...] += jnp.sum(ddtr, axis=0, keepdims=True)
        dsk_out[...] += dsk_acc
        if ride:
            pl.when(c == nc - 1)(lambda: ride.finish(r_ins, r_lnd, r_sems))

    rv = lambda c: nc - 1 - c
    vec128 = pl.BlockSpec((1, 128), lambda c: (0, 0))
    vecin = pl.BlockSpec((1, SSD_INNER), lambda c: (0, 0))
    rows = pl.BlockSpec((L, SSD_INNER), lambda c: (rv(c), 0))
    return pl.pallas_call(
        body, name="ssd_bwd", grid=(nc,),
        in_specs=[rows, rows,
                  pl.BlockSpec((L, CONV_DIM), lambda c: (rv(c), 0)),
                  pl.BlockSpec((L, 128), lambda c: (rv(c), 0)),
                  pl.BlockSpec((L, SSD_INNER), lambda c: (rv(c), P_Z // SSD_INNER)),
                  pl.BlockSpec((1, N_PAIR, 128, SSD_N), lambda c: (rv(c), 0, 0, 0)),
                  vec128, vec128, vecin, vecin,
                  pl.BlockSpec((SSD_INNER, 128), lambda c: (0, 0)),
                  pl.BlockSpec((2 * SSD_INNER, 128), lambda c: (0, 0))] + (ride.in_specs if ride else []),
        out_specs=[rows, pl.BlockSpec((L, CONV_DIM), lambda c: (rv(c), 0)),
                   pl.BlockSpec((L, 128), lambda c: (rv(c), 0)), vecin, vec128, vec128, vec128]
        + (ride.out_specs if ride else []),
        out_shape=[jax.ShapeDtypeStruct((S, SSD_INNER), bf16), jax.ShapeDtypeStruct((S, CONV_DIM), f32),
                   jax.ShapeDtypeStruct((S, 128), bf16), jax.ShapeDtypeStruct((1, SSD_INNER), f32),
                   jax.ShapeDtypeStruct((1, 128), f32), jax.ShapeDtypeStruct((1, 128), f32),
                   jax.ShapeDtypeStruct((1, 128), f32)] + (ride.out_shape if ride else []),
        scratch_shapes=[pltpu.VMEM((N_PAIR, 128, SSD_N), f32), pltpu.VMEM((L, SSD_INNER), f32),
                        pltpu.VMEM((L, SSD_INNER), f32), pltpu.VMEM((L, 2 * SSD_INNER), f32),
                        pltpu.VMEM((L, SSD_INNER), f32), pltpu.VMEM((L, SSD_INNER), f32),
                        pltpu.VMEM((128, L), f32)]
        + (ride.scratch if ride else []),
        compiler_params=_params(("arbitrary",)),
    )(dyn, y, xbc, pdt, proj, hprev, dt_bias_p, a_log_p, d_skip_c, ssd_norm, sel_pair, sel_head,
      *(ride.srcs if ride else []))


MEM_W = MEM_HEADS * MEM_HD


def _mem_probs(q, k):
    s = _dot(q, k, NT) * (MEM_HD ** -0.5)
    s = s - jnp.max(s, axis=1, keepdims=True)
    p = jnp.exp(s)
    return p / jnp.sum(p, axis=1, keepdims=True)


def _mem_fwd(proj, kv, S, tm=512):
    tm = min(tm, S)
    M = kv.shape[0]

    def body(q_ref, kv_ref, o_ref):
        for h in range(MEM_HEADS):
            sl = slice(h * MEM_HD, (h + 1) * MEM_HD)
            vsl = slice(MEM_W + h * MEM_HD, MEM_W + (h + 1) * MEM_HD)
            p = _mem_probs(q_ref[:, sl].astype(bf16), kv_ref[:, sl].astype(bf16))
            o_ref[:, sl] = _dot(p.astype(bf16), kv_ref[:, vsl].astype(bf16)).astype(bf16)

    return pl.pallas_call(
        body, name="mem_fwd", grid=(S // tm,),
        in_specs=[pl.BlockSpec((tm, MEM_W), lambda i: (i, P_MEMQ // MEM_W)),
                  pl.BlockSpec((M, 2 * MEM_W), lambda i: (0, 0))],
        out_specs=pl.BlockSpec((tm, MEM_W), lambda i: (i, 0)),
        out_shape=jax.ShapeDtypeStruct((S, MEM_W), bf16),
        compiler_params=_params(("parallel",)),
    )(proj, kv)


def _mem_bwd(proj, kv, dy, S, tm=512):
    tm = min(tm, S)
    M = kv.shape[0]
    scale = MEM_HD ** -0.5

    def body(q_ref, kv_ref, dy_ref, dq_ref, dkv_ref):
        @pl.when(pl.program_id(0) == 0)
        def _():
            dkv_ref[...] = jnp.zeros_like(dkv_ref)

        for h in range(MEM_HEADS):
            sl = slice(h * MEM_HD, (h + 1) * MEM_HD)
            vsl = slice(MEM_W + h * MEM_HD, MEM_W + (h + 1) * MEM_HD)
            q = q_ref[:, sl].astype(bf16)
            k = kv_ref[:, sl].astype(bf16)
            v = kv_ref[:, vsl].astype(bf16)
            dyh = dy_ref[:, sl].astype(bf16)
            p = _mem_probs(q, k)
            dp = _dot(dyh, v, NT)
            ds = (p * (dp - jnp.sum(dp * p, axis=1, keepdims=True)) * scale).astype(bf16)
            dq_ref[:, sl] = _dot(ds, k).astype(bf16)
            dkv_ref[:, sl] += _dot(ds, q, TN)
            dkv_ref[:, vsl] += _dot(p.astype(bf16), dyh, TN)

    return pl.pallas_call(
        body, name="mem_bwd", grid=(S // tm,),
        in_specs=[pl.BlockSpec((tm, MEM_W), lambda i: (i, P_MEMQ // MEM_W)),
                  pl.BlockSpec((M, 2 * MEM_W), lambda i: (0, 0)),
                  pl.BlockSpec((tm, MEM_W), lambda i: (i, 0))],
        out_specs=[pl.BlockSpec((tm, MEM_W), lambda i: (i, 0)), pl.BlockSpec((M, 2 * MEM_W), lambda i: (0, 0))],
        out_shape=[jax.ShapeDtypeStruct((S, MEM_W), bf16), jax.ShapeDtypeStruct((M, 2 * MEM_W), f32)],
        compiler_params=_params(("arbitrary",)),
    )(proj, kv, dy)


def _merge_fwd(proj, t0, t1, t2, S, tm=512):
    tm = min(tm, S)

    def body(g_ref, t0_ref, t1_ref, t2_ref, o_ref):
        acc = jnp.zeros((tm, D), f32)
        for b, t_ref in enumerate((t0_ref, t1_ref, t2_ref)):
            acc = acc + _sigmoid(g_ref[:, b * D:(b + 1) * D]) * t_ref[...]
        o_ref[...] = acc.astype(bf16)

    row = pl.BlockSpec((tm, D), lambda i: (i, 0))
    return pl.pallas_call(
        body, name="merge_fwd", grid=(S // tm,),
        in_specs=[pl.BlockSpec((tm, 3 * D), lambda i: (i, P_GATE // (3 * D))), row, row, row],
        out_specs=row, out_shape=jax.ShapeDtypeStruct((S, D), bf16),
        compiler_params=_params(("parallel",)),
    )(proj, t0, t1, t2)


def _merge_bwd(proj, t0, t1, t2, dm, S, tm=512):
    tm = min(tm, S)

    def body(g_ref, t0_ref, t1_ref, t2_ref, dm_ref, d0_ref, d1_ref, d2_ref, dg_ref):
        dmv = dm_ref[...]
        for b, (t_ref, d_ref) in enumerate(((t0_ref, d0_ref), (t1_ref, d1_ref), (t2_ref, d2_ref))):
            sg = _sigmoid(g_ref[:, b * D:(b + 1) * D])
            d_ref[...] = (dmv * sg).astype(bf16)
            dg_ref[:, b * D:(b + 1) * D] = (dmv * t_ref[...] * sg * (1.0 - sg)).astype(bf16)

    row = pl.BlockSpec((tm, D), lambda i: (i, 0))
    return pl.pallas_call(
        body, name="merge_bwd", grid=(S // tm,),
        in_specs=[pl.BlockSpec((tm, 3 * D), lambda i: (i, P_GATE // (3 * D))), row, row, row, row],
        out_specs=[row, row, row, pl.BlockSpec((tm, 3 * D), lambda i: (i, 0))],
        out_shape=[jax.ShapeDtypeStruct((S, D), bf16)] * 3 + [jax.ShapeDtypeStruct((S, 3 * D), bf16)],
        compiler_params=_params(("parallel",)),
    )(proj, t0, t1, t2, dm)


def _loss_head(ff, g, h1, target, S, tm=512):
    tm = min(tm, S)

    def body(ff_ref, g_ref, h1_ref, t_ref, dh_ref, loss_ref):
        xv = ff_ref[...]
        r = lax.rsqrt(jnp.mean(xv * xv, axis=1, keepdims=True) + EPS)
        err = h1_ref[...] + xv * r * g_ref[...] - t_ref[...]
        dh_ref[...] = err * (1.0 / D)

        @pl.when(pl.program_id(0) == 0)
        def _():
            loss_ref[...] = jnp.zeros_like(loss_ref)

        loss_ref[...] += 0.5 * _sum_all(jnp.mean(err * err, axis=1, keepdims=True)) * jnp.ones((1, 128), f32)

    row = pl.BlockSpec((tm, D), lambda i: (i, 0))
    return pl.pallas_call(
        body, name="loss_head", grid=(S // tm,),
        in_specs=[row, pl.BlockSpec((1, D), lambda i: (0, 0)), row, row],
        out_specs=[row, pl.BlockSpec((1, 128), lambda i: (0, 0))],
        out_shape=[jax.ShapeDtypeStruct((S, D), f32), jax.ShapeDtypeStruct((1, 128), f32)],
        compiler_params=_params(("arbitrary",)),
    )(ff, g, h1, target)


def _local_step(x, mem, target, wts, late_rides, late_weights, small, rest_rides):
    S = x.shape[0]
    M = mem.shape[0]
    pad = lambda v: jnp.pad(v, ((0, 0), (0, 128 - SSD_HEADS)))
    dtb_p, alog_p = pad(small["dt_bias"]), pad(small["a_log"])
    dsk_c = jnp.repeat(small["d_skip"], SB_HD, axis=1)

    u = _rms_fwd(x, small["norm_mix_pre"], name="norm_pre", out_dtype=bf16)
    if late_rides:
        proj, lands_a = _mm(u, wts["w_main"], "nn", tm=1024, tn=1024, name="in_proj", ride=late_rides[0])
    else:
        proj, lands_a = _mm(u, wts["w_main"], "nn", tm=1024, tn=1024, name="in_proj"), []
    pdt = _mm(u, wts["w_dt"], "nn", tm=1024, tn=128, name="in_proj_dt")
    y_sb, tot_lk, lands_b = _sb_fwd(proj, S, late_rides[1] if late_rides else None)
    wts = dict(wts, **late_weights(lands_a + lands_b))
    small = dict(small, conv_w=wts.pop("conv_w"))
    xc, xbc = _conv_fwd(proj, small["conv_w"], small["conv_b"], S)
    y_ssd, yn, hprev = _ssd_fwd(xbc, proj, pdt, dtb_p, alog_p, dsk_c, small["ssd_norm"], S)
    mn = _rms_fwd(mem, small["norm_mem"], name="norm_mem", out_dtype=bf16, tm=min(512, M))
    kv = _mm(mn, wts["w_mem_kv"], "nn", tm=M, tn=1024, name="mem_kv")
    y_mem = _mem_fwd(proj, kv, S)
    t0 = _mm(y_sb, wts["w_sb_out"], "nn", tm=1024, tn=1024, name="sb_out")
    t1 = _mm(yn, wts["w_ssd_out"], "nn", tm=1024, tn=1024, name="ssd_out")
    t2 = _mm(y_mem, wts["w_mem_out"], "nn", tm=1024, tn=1024, name="mem_out")
    merged = _merge_fwd(proj, t0, t1, t2, S)
    mix = _mm(merged, wts["w_o"], "nn", tm=1024, tn=1024, name="w_o")
    h1 = _rms_fwd(mix, small["norm_mix_post"], name="norm_mix_post", out_dtype=f32, residual=x)
    u2 = _rms_fwd(h1, small["norm_mlp_pre"], name="norm_mlp_pre", out_dtype=bf16)
    a_up, hrelu = _mm(u2, wts["w_up"], "nn", tm=1024, tn=1024, name="mlp_up", out_dtypes=(f32, bf16),
                      epi=lambda acc: (acc, jnp.square(jnp.maximum(acc, 0.0))))
    ff = _mm(hrelu, wts["w_down"], "nn", tm=1024, tn=1024, name="mlp_down")
    dh2, loss = _loss_head(ff, small["norm_mlp_post"], h1, target, S)

    g = {}
    dff, g["norm_mlp_post"] = _rms_bwd(ff, dh2, small["norm_mlp_post"], name="norm_mlp_post_bwd", out_dtype=bf16)
    da = _mm(dff, wts["w_down"], "nt", tm=1024, tn=1024, name="mlp_down_dx", out_dtypes=(bf16,),
             epi=lambda acc, a: (acc * (2.0 * jnp.maximum(a, 0.0)),), extras=(a_up,))
    g["w_down"] = _mm(hrelu, dff, "tn", tm=1024, tn=1024, name="mlp_down_dw")
    du2 = _mm(da, wts["w_up"], "nt", tm=1024, tn=1024, name="mlp_up_dx")
    g["w_up"] = _mm(u2, da, "tn", tm=1024, tn=1024, name="mlp_up_dw")
    dh1, g["norm_mlp_pre"] = _rms_bwd(h1, du2, small["norm_mlp_pre"], name="norm_mlp_pre_bwd", out_dtype=f32, add=dh2)
    dmix, g["norm_mix_post"] = _rms_bwd(mix, dh1, small["norm_mix_post"], name="norm_mix_post_bwd", out_dtype=bf16)
    dmerged = _mm(dmix, wts["w_o"], "nt", tm=1024, tn=1024, name="w_o_dx")
    g["w_o"] = _mm(merged, dmix, "tn", tm=1024, tn=1024, name="w_o_dw")
    dt0, dt1, dt2, dgl = _merge_bwd(proj, t0, t1, t2, dmerged, S)
    dy_sb = _mm(dt0, wts["w_sb_out"], "nt", tm=1024, tn=1024, name="sb_out_dx")
    g["w_sb_out"] = _mm(y_sb, dt0, "tn", tm=1024, tn=1024, name="sb_out_dw")
    dy_ssd = _mm(dt1, wts["w_ssd_out"], "nt", tm=1024, tn=1024, name="ssd_out_dx")
    g["w_ssd_out"] = _mm(yn, dt1, "tn", tm=1024, tn=1024, name="ssd_out_dw")
    dy_mem = _mm(dt2, wts["w_mem_out"], "nt", tm=1024, tn=1024, name="mem_out_dx")
    g["w_mem_out"] = _mm(y_mem, dt2, "tn", tm=1024, tn=1024, name="mem_out_dw")
    dmemq, dkv = _mem_bwd(proj, kv, dy_mem, S)
    g["w_mem_kv"] = _mm(mn, dkv, "tn", tm=1024, tn=1024, name="mem_kv_dw")
    dmn = _mm(dkv, wts["w_mem_kv"], "nt", tm=M, tn=1024, name="mem_kv_dx")
    _, g["norm_mem"] = _rms_bwd(mem, dmn, small["norm_mem"], name="norm_mem_bwd", out_dtype=bf16, tm=min(512, M))
    rides = rest_rides(g) if rest_rides else (None, None)
    dz, dxbc, ddt, g["ssd_norm"], dsk, dalog, ddtb, *lands_a = _ssd_bwd(
        dy_ssd, y_ssd, xbc, proj, pdt, hprev, dtb_p, alog_p, dsk_c, small["ssd_norm"], S, rides[0])
    g["d_skip"], g["a_log"], g["dt_bias"] = dsk[:, :SSD_HEADS], dalog[:, :SSD_HEADS], ddtb[:, :SSD_HEADS]
    dxbc_raw, dcw, g["conv_b"] = _conv_bwd(proj, xc, dxbc, small["conv_w"], S)
    g["conv_w"] = dcw[:CONV_K]
    dq, dk, dv, lands_b = _sb_bwd(proj, tot_lk, dy_sb, S, rides[1])
    g["rest_lands"] = lands_b + lands_a
    dproj = (dq, dk, dv, dxbc_raw, dgl, dmemq, dz)
    du_dt = _mm(ddt, wts["w_dt"], "nt", tm=1024, tn=1024, name="in_proj_dt_dx")
    du = _mm_pieces_nt(dproj, wts["w_main"], du_dt, tm=512, tn=256, name="in_proj_dx")
    g["w_main"] = [_mm(u, p, "tn", tm=1024, tn=1024, name="in_proj_dw_%d" % i) for i, p in enumerate(dproj)]
    g["w_dt"] = _mm(u, ddt, "tn", tm=1024, tn=128, name="in_proj_dt_dw")
    grad_x, g["norm_mix_pre"] = _rms_bwd(x, du, small["norm_mix_pre"], name="norm_pre_bwd", out_dtype=f32, add=dh1)
    return loss, grad_x, g


def _to_internal(w_in):
    sec = lambda r: w_in[:, r[0]:r[1]]
    w_main = jnp.concatenate([sec(R_QKV), sec(R_XBC), sec(R_GATE), sec(R_MEMQ), sec(R_Z)], axis=1)
    w_dt = jnp.pad(sec(R_DT), ((0, 0), (0, 128 - SSD_HEADS)))
    return w_main, w_dt


def _from_internal(pieces, g_dt):
    dq, dk, dv, dxbc, dgate, dmemq, dz = pieces
    return jnp.concatenate([dq, dk, dv, dz, dxbc, g_dt[:, :SSD_HEADS], dmemq, dgate], axis=1)


MESH = pl.DeviceIdType.MESH
ANY = pl.BlockSpec(memory_space=pl.ANY)


def _place():
    x, y, c = lax.axis_index("x"), lax.axis_index("y"), lax.axis_index("c")
    return (x, y, c), [(1 - x, y, c), (x, 1 - y, c), (1 - x, 1 - y, c)]


def _exchange_copy(mode, ins, lands, send, recv, a, k, me, peers, arriving):
    p = peers[k]
    theirs = 2 * p[0] + p[1]
    if mode == "gather":
        src, dst = ins[a], lands[a].at[theirs if arriving else me]
    else:
        src, dst = ins[a].at[theirs], lands[a].at[k]
    return pltpu.make_async_remote_copy(src_ref=src, dst_ref=dst, send_sem=send.at[a * 3 + k],
                                        recv_sem=recv.at[a * 3 + k], device_id=p, device_id_type=MESH)


class _Ride:
    def __init__(self, srcs, mode):
        self.srcs, self.mode, self.n = list(srcs), mode, len(srcs)
        n = self.n
        self.in_specs, self.out_specs = [ANY] * n, [ANY] * n
        self.out_shape = [
            jax.ShapeDtypeStruct((N_SHARD,) + s.shape if mode == "gather" else (3,) + s.shape[1:], s.dtype)
            for s in self.srcs]
        self.scratch = [pltpu.SemaphoreType.DMA((3 * n,)), pltpu.SemaphoreType.DMA((3 * n,)),
                        pltpu.SemaphoreType.DMA((n,))]

    def _own(self, ins, lnd, sems):
        if self.mode != "gather":
            return []
        me = 2 * lax.axis_index("x") + lax.axis_index("y")
        return [pltpu.make_async_copy(ins[a], lnd[a].at[me], sems[2].at[a]) for a in range(self.n)]

    def _far(self, ins, lnd, sems, arriving):
        (x, y, c), peers = _place()
        return [_exchange_copy(self.mode, ins, lnd, sems[0], sems[1], a, k, 2 * x + y, peers, arriving)
                for a in range(self.n) for k in range(3)]

    def start(self, ins, lnd, sems):
        for cp in self._own(ins, lnd, sems) + self._far(ins, lnd, sems, False):
            cp.start()

    def finish(self, ins, lnd, sems):
        for cp in self._far(ins, lnd, sems, True):
            cp.wait_recv()
        for cp in self._far(ins, lnd, sems, False):
            cp.wait_send()
        for cp in self._own(ins, lnd, sems):
            cp.wait()


def _exchange(srcs, mode, name):
    ride = _Ride(srcs, mode)
    n = ride.n

    def body(*refs):
        ride.start(refs[:n], refs[n:2 * n], refs[2 * n:])
        ride.finish(refs[:n], refs[n:2 * n], refs[2 * n:])

    return pl.pallas_call(body, name=name, in_specs=ride.in_specs, out_specs=ride.out_specs,
                          out_shape=ride.out_shape, scratch_shapes=ride.scratch)(*srcs)


def _gather_two_level(shards, name):
    n = len(shards)

    def body(*refs):
        ins, lnd = refs[:n], refs[n:2 * n]
        send, recv, loc = refs[2 * n:]
        (x, y, c), peers = _place()
        me = 2 * x + y

        def half(ref, a, core):
            rows = shards[a].shape[0] // 2
            return ref.at[pl.ds(core * rows, rows)]

        def copy(a, j, slot, core, to):
            return pltpu.make_async_remote_copy(
                src_ref=half(ins[a], a, core) if j < 3 else half(lnd[a].at[slot], a, core),
                dst_ref=half(lnd[a].at[slot], a, core), send_sem=send.at[6 * a + j], recv_sem=recv.at[6 * a + j],
                device_id=to, device_id_type=MESH)

        own = [pltpu.make_async_copy(ins[a], lnd[a].at[me], loc.at[a]) for a in range(n)]
        far = [copy(a, k, me, c, peers[k]) for a in range(n) for k in range(3)]
        for cp in own + far:
            cp.start()
        passed = []
        for a in range(n):
            for k, p in enumerate(peers):
                theirs = 2 * p[0] + p[1]
                copy(a, k, theirs, c, p).wait_recv()
                passed.append(copy(a, 3 + k, theirs, c, (x, y, 1 - c)))
                passed[-1].start()
        for a in range(n):
            for k, p in enumerate(peers):
                copy(a, 3 + k, 2 * p[0] + p[1], 1 - c, (x, y, 1 - c)).wait_recv()
        for cp in far + passed:
            cp.wait_send()
        for cp in own:
            cp.wait()

    return pl.pallas_call(
        body, name=name, in_specs=[ANY] * n, out_specs=[ANY] * n,
        out_shape=[jax.ShapeDtypeStruct((N_SHARD,) + s.shape, s.dtype) for s in shards],
        scratch_shapes=[pltpu.SemaphoreType.DMA((6 * n,)), pltpu.SemaphoreType.DMA((6 * n,)),
                        pltpu.SemaphoreType.DMA((n,))],
    )(*shards)


def _exchange_packets(packet):
    def body(pk, pk_out, send, recv, loc):
        x, y, c = lax.axis_index("x"), lax.axis_index("y"), lax.axis_index("c")
        lin = 4 * x + 2 * y + c
        own = pltpu.make_async_copy(pk, pk_out.at[lin], loc.at[0])
        own.start()

        def pk_copy(m, slot):
            dev = (x ^ ((m >> 2) & 1), y ^ ((m >> 1) & 1), c ^ (m & 1))
            return pltpu.make_async_remote_copy(
                src_ref=pk, dst_ref=pk_out.at[slot], send_sem=send.at[m - 1], recv_sem=recv.at[m - 1],
                device_id=dev, device_id_type=MESH)

        sent = [pk_copy(m, lin) for m in range(1, N_DEV)]
        for cp in sent:
            cp.start()
        for m in range(1, N_DEV):
            pk_copy(m, lin ^ m).wait_recv()
        for cp in sent:
            cp.wait_send()
        own.wait()

    return pl.pallas_call(
        body, name="exchange_packets", in_specs=[ANY], out_specs=ANY,
        out_shape=jax.ShapeDtypeStruct((N_DEV,) + packet.shape, packet.dtype),
        scratch_shapes=[pltpu.SemaphoreType.DMA((N_DEV - 1,)), pltpu.SemaphoreType.DMA((N_DEV - 1,)),
                        pltpu.SemaphoreType.DMA((1,))],
    )(packet)


def _swap_sibling(parts, name):
    n = len(parts)

    def body(*refs):
        ins, outs = refs[:n], refs[n:2 * n]
        send, recv = refs[2 * n:]
        x, y, c = lax.axis_index("x"), lax.axis_index("y"), lax.axis_index("c")
        cps = [pltpu.make_async_remote_copy(
            src_ref=ins[a], dst_ref=outs[a], send_sem=send.at[a], recv_sem=recv.at[a],
            device_id=(x, y, 1 - c), device_id_type=MESH) for a in range(n)]
        for cp in cps:
            cp.start()
        for cp in cps:
            cp.wait_recv()
        for cp in cps:
            cp.wait_send()

    return pl.pallas_call(
        body, name=name,
        in_specs=[ANY] * n, out_specs=[ANY] * n,
        out_shape=[jax.ShapeDtypeStruct(p.shape, p.dtype) for p in parts],
        scratch_shapes=[pltpu.SemaphoreType.DMA((n,)), pltpu.SemaphoreType.DMA((n,))],
    )(*parts)


BLOCK_ELEMS = 256 * 1024


def _row_tile(R, C):
    tr = max(8, (BLOCK_ELEMS // C) // 8 * 8)
    while R % tr:
        tr -= 8
    return min(tr, R)


def _sum_parts(own, stack, name):
    k = stack.shape[0]
    R, C = stack.shape[1:]
    tr = _row_tile(R, C)

    def body(*refs):
        o_ref = refs[-1]
        acc = refs[0][...]
        for r in refs[1:-1]:
            acc = acc + r[...]
        o_ref[...] = acc

    row = pl.BlockSpec((tr, C), lambda i: (i, 0))
    specs = ([row] if own is not None else []) + [
        pl.BlockSpec((None, tr, C), functools.partial(lambda i, j: (j, i, 0), j=j)) for j in range(k)]
    args = ([own] if own is not None else []) + [stack] * k
    return pl.pallas_call(
        body, name=name, grid=(R // tr,), in_specs=specs, out_specs=row,
        out_shape=jax.ShapeDtypeStruct((R, C), f32), compiler_params=_params(("parallel",)),
    )(*args)


def _adamw(w, m, v, g_parts, name):
    R, C = w.shape
    tr = _row_tile(R, C)
    n_g = len(g_parts)

    def body(w_ref, m_ref, v_ref, *rest):
        g = rest[0][...]
        for r in rest[1:n_g]:
            g = g + r[...]
        g_ref, d_ref, nm_ref, nv_ref = rest[n_g:]
        nm = ADAM_B1 * m_ref[...] + (1.0 - ADAM_B1) * g
        nv = ADAM_B2 * v_ref[...] + (1.0 - ADAM_B2) * jnp.square(g)
        m_hat = nm / (1.0 - ADAM_B1 ** ADAM_STEP)
        v_hat = nv / (1.0 - ADAM_B2 ** ADAM_STEP)
        g_ref[...] = g
        d_ref[...] = -ADAM_LR * (m_hat / (jnp.sqrt(v_hat) + ADAM_EPS) + ADAM_WD * w_ref[...])
        nm_ref[...] = nm
        nv_ref[...] = nv

    row = pl.BlockSpec((tr, C), lambda i: (i, 0))
    return pl.pallas_call(
        body, name=name, grid=(R // tr,), in_specs=[row] * (3 + n_g), out_specs=[row] * 4,
        out_shape=[jax.ShapeDtypeStruct((R, C), f32)] * 4, compiler_params=_params(("parallel",)),
    )(w, m, v, *g_parts)


BIG = ("w_in", "w_mem_kv", "w_sb_out", "w_ssd_out", "w_mem_out", "w_o", "w_up", "w_down")
FIRST = ("w_in", "w_mem_kv")
LATE = ("w_sb_out", "w_ssd_out", "w_mem_out", "w_o", "w_up", "w_down")
REST = BIG[1:]
COL_SHARDED = ("w_in", "w_mem_kv", "w_up")
SMALL = ("norm_mix_pre", "conv_w", "conv_b", "dt_bias", "a_log", "d_skip", "ssd_norm", "norm_mem",
         "norm_mix_post", "norm_mlp_pre", "norm_mlp_post")
WEIGHTS = ("norm_mix_pre", "w_in", "conv_w", "conv_b", "dt_bias", "a_log", "d_skip", "ssd_norm", "norm_mem",
           "w_mem_kv", "w_sb_out", "w_ssd_out", "w_mem_out", "w_o", "norm_mix_post", "norm_mlp_pre", "w_up",
           "w_down", "norm_mlp_post")
PK_ROWS = 184


def _pack(vecs):
    flat = jnp.concatenate([v.reshape(-1) for v in vecs])
    return jnp.pad(flat, (0, PK_ROWS * 128 - flat.shape[0])).reshape(PK_ROWS, 128)


def _unpack(pk, shapes):
    flat = pk.reshape(-1)
    out, off = [], 0
    for s in shapes:
        n = 1
        for d in s:
            n *= d
        out.append(flat[off:off + n].reshape(s))
        off += n
    return out


def _full_from_slabs(name, slabs):
    if name in COL_SHARDED:
        return slabs.transpose(1, 0, 2).reshape(slabs.shape[1], -1)
    return slabs.reshape(-1, slabs.shape[2])


def _slabs_from_full(name, g):
    if name in COL_SHARDED:
        return g.reshape(g.shape[0], N_SHARD, -1).transpose(1, 0, 2)
    return g.reshape(N_SHARD, -1, g.shape[1])


def kernel(x, mem, norm_mix_pre, w_in, conv_w, conv_b, dt_bias, a_log, d_skip, ssd_norm, norm_mem, w_mem_kv, w_sb_out, w_ssd_out, w_mem_out, w_o, norm_mix_post, norm_mlp_pre, w_up, w_down, norm_mlp_post, loss_target, m_norm_mix_pre, m_w_in, m_conv_w, m_conv_b, m_dt_bias, m_a_log, m_d_skip, m_ssd_norm, m_norm_mem, m_w_mem_kv, m_w_sb_out, m_w_ssd_out, m_w_mem_out, m_w_o, m_norm_mix_post, m_norm_mlp_pre, m_w_up, m_w_down, m_norm_mlp_post, v_norm_mix_pre, v_w_in, v_conv_w, v_conv_b, v_dt_bias, v_a_log, v_d_skip, v_ssd_norm, v_norm_mem, v_w_mem_kv, v_w_sb_out, v_w_ssd_out, v_w_mem_out, v_w_o, v_norm_mix_post, v_norm_mlp_pre, v_w_up, v_w_down, v_norm_mlp_post):
    env = dict(locals())
    w = {n: env[n] for n in WEIGHTS}
    mo = {n: env["m_" + n] for n in WEIGHTS}
    vo = {n: env["v_" + n] for n in WEIGHTS}
    shard = 2 * lax.axis_index("x") + lax.axis_index("y")

    first = _gather_two_level([w[n][0].astype(bf16) for n in FIRST], "gather_first")
    w_main, w_dt = _to_internal(_full_from_slabs("w_in", first[0]))
    wts = dict(w_main=w_main, w_dt=w_dt, w_mem_kv=_full_from_slabs("w_mem_kv", first[1]))
    late_rides = (_Ride([w[n][0].astype(bf16) for n in LATE[:4]] + [w["conv_w"][0]], "gather"),
                  _Ride([w[n][0].astype(bf16) for n in LATE[4:]], "gather"))

    def late_weights(lands):
        full = {n: _full_from_slabs(n, s) for n, s in zip(LATE, lands[:4] + lands[5:])}
        return dict(full, conv_w=lands[4].transpose(1, 0, 2).reshape(CONV_K, CONV_DIM))

    def rest_rides(g):
        slabs = [_slabs_from_full(n, g[n]).astype(bf16) for n in REST]
        return _Ride(slabs[5:], "scatter"), _Ride(slabs[:5], "scatter")

    small = {n: w[n] for n in SMALL if n != "conv_w"}
    loss, grad_x, g = _local_step(x[0], mem[0], loss_target[0], wts, late_rides, late_weights, small, rest_rides)
    g["w_in"] = _from_internal(g.pop("w_main"), g.pop("w_dt"))

    lands = list(_exchange([_slabs_from_full("w_in", g["w_in"]).astype(bf16)], "scatter", "scatter_w_in"))
    lands += g["rest_lands"]
    packets = _exchange_packets(_pack([g[n] for n in SMALL] + [loss[:, :1]]))
    partial = []
    for n, r in zip(BIG, lands):
        own = lax.dynamic_index_in_dim(_slabs_from_full(n, g[n]), shard, 0, keepdims=False)
        partial.append(_sum_parts(own, r, name="sum_chips_" + n))
    other = _swap_sibling(partial, "swap_sibling")

    out_g, out_d, out_m, out_v = {}, {}, {}, {}
    for n, p, q in zip(BIG, partial, other):
        res = _adamw(w[n][0], mo[n][0], vo[n][0], [p, q], name="adamw_" + n)
        out_g[n], out_d[n], out_m[n], out_v[n] = [r[None] for r in res]
    tot = _sum_parts(None, packets, name="sum_packets")
    shapes = [g[n].shape for n in SMALL] + [(1, 1)]
    sm = dict(zip(SMALL + ("loss",), _unpack(tot, shapes)))
    sm["conv_w"] = lax.dynamic_slice_in_dim(sm["conv_w"], shard * (CONV_DIM // N_SHARD), CONV_DIM // N_SHARD, axis=1)
    own_small = lambda d: _pack([d[n].reshape(sm[n].shape) for n in SMALL])
    res = _adamw(own_small(w), own_small(mo), own_small(vo), [own_small(sm)], name="adamw_small")
    own_shapes = [sm[n].shape for n in SMALL]
    for store, r in zip((out_g, out_d, out_m, out_v), res):
        for n, val in zip(SMALL, _unpack(r, own_shapes)):
            store[n] = val.reshape(w[n].shape)

    outs = [sm["loss"].reshape(()), grad_x[None]]
    for store in (out_g, out_d, out_m, out_v):
        outs += [store[n] for n in WEIGHTS]
    return tuple(outs)
```

```python
import functools

import jax
import jax.numpy as jnp
from jax import lax
from jax.experimental import pallas as pl
from jax.experimental.pallas import tpu as pltpu

f32 = jnp.float32
bf16 = jnp.bfloat16

D = 1024
EPS = 1e-6
SB_HD = 64
SSD_INNER = 2048
SSD_HEADS = 32
SSD_GROUPS = 4
SSD_N = 128
SSD_L = 128
CONV_K = 4
CONV_DIM = 3072
MEM_HEADS = 4
MEM_HD = 256
D_FF = 4096
D_IN = 12320
N_SHARD = 4
N_DEV = 8

P_QKV, P_XBC, P_GATE, P_MEMQ, P_Z, P_DT, P_TOT = 0, 3072, 6144, 9216, 10240, 12288, 12416
R_QKV, R_Z, R_XBC, R_DT, R_MEMQ, R_GATE = (0, 3072), (3072, 5120), (5120, 8192), (8192, 8224), (8224, 9248), (9248, 12320)

ADAM_LR = 0.001
ADAM_B1 = 0.9
ADAM_B2 = 0.999
ADAM_EPS = 1e-08
ADAM_WD = 0.01
ADAM_STEP = 10

VMEM_LIMIT = 56 * 1024 * 1024

NN = (((1,), (0,)), ((), ()))
NT = (((1,), (1,)), ((), ()))
TN = (((0,), (0,)), ((), ()))


def _dot(a, b, dims=NN):
    return lax.dot_general(a, b, dims, preferred_element_type=f32)


def _params(sem=None):
    return pltpu.CompilerParams(dimension_semantics=sem, vmem_limit_bytes=VMEM_LIMIT)


def _sigmoid(x):
    return 1.0 / (1.0 + jnp.exp(-x))


def _split2(x):
    hi = x.astype(bf16)
    lo = (x - hi.astype(f32)).astype(bf16)
    return hi, lo


def _split3(x):
    hi = x.astype(bf16)
    r = x - hi.astype(f32)
    mid = r.astype(bf16)
    lo = (r - mid.astype(f32)).astype(bf16)
    return hi, mid, lo


def _mm(a, b, mode, *, tm, tn, name, out_dtypes=(f32,), epi=None, extras=(), ride=None):
    M = a.shape[1] if mode == "tn" else a.shape[0]
    N = b.shape[0] if mode == "nt" else b.shape[1]
    tm, tn = min(tm, M), min(tn, N)
    if mode == "nn":
        (M, K), N = a.shape, b.shape[1]
        a_spec = pl.BlockSpec((tm, K), lambda i, j: (i, 0))
        b_spec = pl.BlockSpec((K, tn), lambda i, j: (0, j))
        dims = NN
    elif mode == "nt":
        (M, K), N = a.shape, b.shape[0]
        a_spec = pl.BlockSpec((tm, K), lambda i, j: (i, 0))
        b_spec = pl.BlockSpec((tn, K), lambda i, j: (j, 0))
        dims = NT
    else:
        (K, M), N = a.shape, b.shape[1]
        a_spec = pl.BlockSpec((K, tm), lambda i, j: (0, i))
        b_spec = pl.BlockSpec((K, tn), lambda i, j: (0, j))
        dims = TN
    assert M % tm == 0 and N % tn == 0, (name, M, N, tm, tn)
    n_ex, n_out = len(extras), len(out_dtypes)
    n_r = ride.n if ride else 0
    o_spec = pl.BlockSpec((tm, tn), lambda i, j: (i, j))
    grid = (M // tm, N // tn)

    def body(a_ref, b_ref, *rest):
        r_ins = rest[n_ex:n_ex + n_r]
        outs = rest[n_ex + n_r:n_ex + n_r + n_out]
        r_lnd, r_sems = rest[n_ex + n_r + n_out:n_ex + 2 * n_r + n_out], rest[n_ex + 2 * n_r + n_out:]
        i, j = pl.program_id(0), pl.program_id(1)
        if ride:
            pl.when((i == 0) & (j == 0))(lambda: ride.start(r_ins, r_lnd, r_sems))
        acc = _dot(a_ref[...].astype(bf16), b_ref[...].astype(bf16), dims)
        res = (acc,) if epi is None else epi(acc, *[e[...] for e in rest[:n_ex]])
        for o_ref, r in zip(outs, res):
            o_ref[...] = r.astype(o_ref.dtype)
        if ride:
            pl.when((i == grid[0] - 1) & (j == grid[1] - 1))(lambda: ride.finish(r_ins, r_lnd, r_sems))

    out = pl.pallas_call(
        body, name=name, grid=grid,
        in_specs=[a_spec, b_spec] + [o_spec] * n_ex + (ride.in_specs if ride else []),
        out_specs=[o_spec] * n_out + (ride.out_specs if ride else []),
        out_shape=[jax.ShapeDtypeStruct((M, N), dt) for dt in out_dtypes] + (ride.out_shape if ride else []),
        scratch_shapes=ride.scratch if ride else [],
        compiler_params=_params(("arbitrary", "arbitrary") if ride else ("parallel", "parallel")),
    )(a, b, *extras, *(ride.srcs if ride else []))
    if ride:
        return (out[0] if n_out == 1 else out[:n_out]), list(out[n_out:])
    return out[0] if n_out == 1 else out


def _mm_pieces_nt(pieces, b, add, *, tm, tn, name, ride):
    M, N = pieces[0].shape[0], b.shape[0]
    n_p, n_r = len(pieces), (ride.n if ride else 0)
    o_spec = pl.BlockSpec((tm, tn), lambda i, j: (i, j))
    grid = (M // tm, N // tn)

    def body(*refs):
        b_ref, add_ref = refs[n_p:n_p + 2]
        r_ins, o_ref = refs[n_p + 2:n_p + 2 + n_r], refs[n_p + 2 + n_r]
        r_lnd, r_sems = refs[n_p + 3 + n_r:n_p + 3 + 2 * n_r], refs[n_p + 3 + 2 * n_r:]
        i, j = pl.program_id(0), pl.program_id(1)
        if ride:
            pl.when((i == 0) & (j == 0))(lambda: ride.start(r_ins, r_lnd, r_sems))
        acc, off = add_ref[...], 0
        for r in refs[:n_p]:
            acc = acc + _dot(r[...], b_ref[:, off:off + r.shape[1]], NT)
            off += r.shape[1]
        o_ref[...] = acc
        if ride:
            pl.when((i == grid[0] - 1) & (j == grid[1] - 1))(lambda: ride.finish(r_ins, r_lnd, r_sems))

    out = pl.pallas_call(
        body, name=name, grid=grid,
        in_specs=[pl.BlockSpec((tm, p.shape[1]), lambda i, j: (i, 0)) for p in pieces]
        + [pl.BlockSpec((tn, b.shape[1]), lambda i, j: (j, 0)), o_spec] + (ride.in_specs if ride else []),
        out_specs=[o_spec] + (ride.out_specs if ride else []),
        out_shape=[jax.ShapeDtypeStruct((M, N), f32)] + (ride.out_shape if ride else []),
        scratch_shapes=ride.scratch if ride else [],
        compiler_params=_params(("arbitrary", "arbitrary")),
    )(*pieces, b, add, *(ride.srcs if ride else []))
    return out[0], list(out[1:])


def _rms_fwd(x, g, *, name, out_dtype, residual=None, tm=512):
    S, C = x.shape
    tm = min(tm, S)
    has_res = residual is not None

    def body(x_ref, g_ref, *rest):
        xv = x_ref[...]
        r = lax.rsqrt(jnp.mean(xv * xv, axis=1, keepdims=True) + EPS)
        y = xv * r * g_ref[...]
        if has_res:
            y = y + rest[0][...]
        rest[-1][...] = y.astype(out_dtype)

    row = pl.BlockSpec((tm, C), lambda i: (i, 0))
    vec = pl.BlockSpec((1, C), lambda i: (0, 0))
    args = (x, g) + ((residual,) if has_res else ())
    return pl.pallas_call(
        body, name=name, grid=(S // tm,),
        in_specs=[row, vec] + ([row] if has_res else []),
        out_specs=row, out_shape=jax.ShapeDtypeStruct((S, C), out_dtype),
        compiler_params=_params(("parallel",)),
    )(*args)


def _rms_bwd(x, dy, g, *, name, out_dtype, add=None, tm=512):
    S, C = x.shape
    tm = min(tm, S)
    has_add = add is not None

    def body(x_ref, dy_ref, g_ref, *rest):
        dx_ref, dg_ref = rest[-2], rest[-1]
        xv = x_ref[...]
        dyv = dy_ref[...].astype(f32)
        r = lax.rsqrt(jnp.mean(xv * xv, axis=1, keepdims=True) + EPS)
        xh = xv * r
        dxh = dyv * g_ref[...]
        dx = r * (dxh - xh * jnp.mean(dxh * xh, axis=1, keepdims=True))
        if has_add:
            dx = dx + rest[0][...]
        dx_ref[...] = dx.astype(out_dtype)

        @pl.when(pl.program_id(0) == 0)
        def _():
            dg_ref[...] = jnp.zeros_like(dg_ref)

        dg_ref[...] += jnp.sum(dyv * xh, axis=0, keepdims=True)

    row = pl.BlockSpec((tm, C), lambda i: (i, 0))
    vec = pl.BlockSpec((1, C), lambda i: (0, 0))
    args = (x, dy, g) + ((add,) if has_add else ())
    return pl.pallas_call(
        body, name=name, grid=(S // tm,),
        in_specs=[row, row, vec] + ([row] if has_add else []),
        out_specs=[row, vec],
        out_shape=[jax.ShapeDtypeStruct((S, C), out_dtype), jax.ShapeDtypeStruct((1, C), f32)],
        compiler_params=_params(("arbitrary",)),
    )(*args)


SB_T = 128
SB_SPENT = -120.0
SB_TAIL = 3
SB_GROUPS = (4, 2, 1)
SB_GROUPS_BWD = (4, 2, 1)


def _sb_masks():
    lane = lax.broadcasted_iota(jnp.int32, (1, 128), 1)
    m_a = (lane < SB_HD).astype(f32)
    return m_a, 1.0 - m_a


def _chunks(a, n):
    return [a[:, u * SB_T:(u + 1) * SB_T] for u in range(n)]


def _cat(parts, axis):
    return parts[0] if len(parts) == 1 else jnp.concatenate(parts, axis=axis)


def _mask_last(a, n, mask):
    if mask is None:
        return a
    parts = _chunks(a, n)
    return _cat(parts[:-1] + [jnp.where(mask, parts[-1], 0.0)], 1)


def _sb_logits(z, n, mask):
    l1p = jnp.log(1.0 + jnp.exp(-jnp.abs(z)))
    lb = jnp.minimum(z, 0.0) - l1p
    return lb, _mask_last(lb - z, n, mask)


def _by_count(i, most, fn):
    return lax.switch(jnp.minimum(i, most - 1), [functools.partial(fn, n) for n in range(1, most + 1)])


def _chunk_matmul(parts_list, u_mat):
    out = _dot(_cat(parts_list, 0), u_mat)
    return [out[u * SB_T:(u + 1) * SB_T] for u in range(len(parts_list))]


def _chunk_cumsum(lk, n, u_mat):
    hi = lk.astype(bf16)
    lo = (lk - hi.astype(f32)).astype(bf16)
    out = _chunk_matmul(_chunks(hi, n) + _chunks(lo, n), u_mat)
    return [out[u] + out[n + u] for u in range(n)]


def _sb_fwd(proj, S, ride=None):
    nq = S // SB_T
    n_pairs = D // 128
    scale = SB_HD ** -0.5
    n_r = ride.n if ride else 0

    def body(q_ref, k_ref, v_ref, *rest):
        o_ref, t_ref = rest[n_r:n_r + 2]
        i = pl.program_id(1)
        if ride:
            pl.when((pl.program_id(0) == 0) & (i == 0))(
                lambda: ride.start(rest[:n_r], rest[n_r + 2:2 * n_r + 2], rest[2 * n_r + 2:]))
        m_a, m_b = _sb_masks()
        r_i = lax.broadcasted_iota(jnp.int32, (SB_T, SB_T), 0)
        c_i = lax.broadcasted_iota(jnp.int32, (SB_T, SB_T), 1)
        u_mat = (r_i > c_i).astype(bf16)
        causal = c_i < r_i
        q = q_ref[...] * scale
        q_h = ((q * m_a).astype(bf16), (q * m_b).astype(bf16))

        def group(j_lo, n, carry, mask):
            acc, c_a, c_b = carry
            rows = pl.ds(pl.multiple_of(j_lo * SB_T, SB_T), n * SB_T)
            k = k_ref[rows, :].astype(bf16)
            v = v_ref[rows, :]
            zs = [_dot(q_b, k, NT) for q_b in q_h]
            lbk = [_sb_logits(z, n, mask) for z in zs]
            parts = [_chunk_cumsum(lk, n, u_mat) for _, lk in lbk]
            ws, cs = [], []
            for (lb, lk), part, c in zip(lbk, parts, (c_a, c_b)):
                lb_c, lk_c = _chunks(lb, n), _chunks(lk, n)
                w_c = [None] * n
                for u in reversed(range(n)):
                    w_c[u] = jnp.exp(lb_c[u] + c + part[u])
                    c = c + jnp.sum(lk_c[u], axis=1, keepdims=True)
                ws.append(_mask_last(_cat(w_c, 1), n, mask).astype(bf16))
                cs.append(c)
            for w, m in zip(ws, (m_a, m_b)):
                acc = acc + _dot(w, (v * m).astype(bf16))
            return acc, cs[0], cs[1]

        zero_c = jnp.zeros((SB_T, 1), f32)
        init = (jnp.zeros((SB_T, 128), f32), zero_c, zero_c)
        carry = _by_count(i, SB_TAIL, lambda n: group(i - n + 1, n, init, causal))

        def spent(cr):
            return (jnp.max(jnp.maximum(cr[1], cr[2])) < SB_SPENT).astype(jnp.int32)

        state = (i - jnp.minimum(i, SB_TAIL - 1), spent(carry), carry)
        for n in SB_GROUPS:
            def step(st, n=n):
                left, _, cr = st
                cr = group(left - n, n, cr, None)
                return left - n, spent(cr), cr

            state = lax.while_loop(lambda st, n=n: (st[0] >= n) & (st[1] == 0), step, state)
        left, _, carry = state
        o_ref[...] = carry[0]
        lane = lax.broadcasted_iota(jnp.int32, (1, 128), 1)
        t_ref[...] = (jnp.where(lane == 0, carry[1], 0.0) + jnp.where(lane == SB_HD, carry[2], 0.0)
                      + jnp.where(lane == 1, left.astype(f32), 0.0))
        if ride:
            pl.when((pl.program_id(0) == n_pairs - 1) & (i == nq - 1))(
                lambda: ride.finish(rest[:n_r], rest[n_r + 2:2 * n_r + 2], rest[2 * n_r + 2:]))

    qs = pl.BlockSpec((SB_T, 128), lambda h, i: (i, h))
    out = pl.pallas_call(
        body, name="sb_fwd", grid=(n_pairs, nq),
        in_specs=[qs,
                  pl.BlockSpec((S, 128), lambda h, i: (0, n_pairs + h)),
                  pl.BlockSpec((S, 128), lambda h, i: (0, 2 * n_pairs + h))] + (ride.in_specs if ride else []),
        out_specs=[qs, qs] + (ride.out_specs if ride else []),
        out_shape=[jax.ShapeDtypeStruct((S, D), f32)] * 2 + (ride.out_shape if ride else []),
        scratch_shapes=ride.scratch if ride else [],
        compiler_params=_params(("arbitrary", "arbitrary")),
    )(proj, proj, proj, *(ride.srcs if ride else []))
    return out[0], out[1], list(out[2:])


def _sb_bwd(proj, tot_lk, do, S, ride=None):
    nq = S // SB_T
    n_pairs = D // 128
    scale = SB_HD ** -0.5
    n_r = ride.n if ride else 0

    def body(q_ref, k_ref, v_ref, t_ref, do_ref, *rest):
        dq_ref, dk_ref, dv_ref = rest[n_r:n_r + 3]
        dk_acc, dv_acc = rest[2 * n_r + 3:2 * n_r + 5]
        r_ins, r_lnd, r_sems = rest[:n_r], rest[n_r + 3:2 * n_r + 3], rest[2 * n_r + 5:]
        i = pl.program_id(1)
        if ride:
            pl.when((pl.program_id(0) == 0) & (i == 0))(lambda: ride.start(r_ins, r_lnd, r_sems))
        m_a, m_b = _sb_masks()
        r_i = lax.broadcasted_iota(jnp.int32, (SB_T, SB_T), 0)
        c_i = lax.broadcasted_iota(jnp.int32, (SB_T, SB_T), 1)
        u_inc = (r_i <= c_i).astype(bf16)
        u_exc = (r_i < c_i).astype(bf16)
        causal = c_i < r_i

        @pl.when(i == 0)
        def _():
            dk_acc[...] = jnp.zeros_like(dk_acc)
            dv_acc[...] = jnp.zeros_like(dv_acc)

        q = q_ref[...] * scale
        dov = do_ref[...]
        tv = t_ref[...]
        lane = lax.broadcasted_iota(jnp.int32, (1, 128), 1)
        heads = []
        for m, first in ((m_a, 0), (m_b, SB_HD)):
            tot = jnp.sum(jnp.where(lane == first, tv, 0.0), axis=1, keepdims=True)
            heads.append(((q * m).astype(bf16), (dov * m).astype(bf16), tot, m))
        lowest = jnp.clip(jnp.max(jnp.where(lane == 1, tv, 0.0)).astype(jnp.int32), 0, i)

        def group(j_lo, n, carry, mask):
            dq_acc, cp_a, cp_b, ce_a, ce_b = carry
            rows = pl.ds(pl.multiple_of(j_lo * SB_T, SB_T), n * SB_T)
            k_f = k_ref[rows, :]
            k = k_f.astype(bf16)
            v = v_ref[rows, :].astype(bf16)
            zs = [_dot(h[0], k, NT) for h in heads]
            dws = [_dot(h[1], v, NT) for h in heads]
            lbk = [_sb_logits(z, n, mask) for z in zs]
            parts = [_chunk_cumsum(lk, n, u_inc) for _, lk in lbk]
            ws, es, cps = [], [], []
            for (lb, lk), part, dw, h, cp in zip(lbk, parts, dws, heads, (cp_a, cp_b)):
                lb_c, lk_c = _chunks(lb, n), _chunks(lk, n)
                w_c = []
                for u in range(n):
                    w_c.append(jnp.exp(lb_c[u] + (h[2] - cp) - part[u]))
                    cp = cp + jnp.sum(lk_c[u], axis=1, keepdims=True)
                w = _mask_last(_cat(w_c, 1), n, mask)
                ws.append(w)
                es.append(dw * w)
                cps.append(cp)
            e_parts = [_chunk_matmul(_chunks(e.astype(bf16), n), u_exc) for e in es]
            dzs, ces = [], []
            for (lb, _), e, e_part, ce in zip(lbk, es, e_parts, (ce_a, ce_b)):
                e_c = _chunks(e, n)
                big_c = []
                for u in range(n):
                    big_c.append(ce + e_part[u])
                    ce = ce + jnp.sum(e_c[u], axis=1, keepdims=True)
                sig = jnp.exp(lb)
                dz = _mask_last(e * (1.0 - sig) - _cat(big_c, 1) * sig, n, mask)
                dzs.append(dz.astype(bf16))
                ces.append(ce)
            dk_t = jnp.zeros((n * SB_T, 128), f32)
            dv_t = jnp.zeros((n * SB_T, 128), f32)
            for dz_b, w, h in zip(dzs, ws, heads):
                dq_acc = dq_acc + _dot(dz_b, (k_f * h[3]).astype(bf16))
                dk_t = dk_t + _dot(dz_b, h[0], TN)
                dv_t = dv_t + _dot(w.astype(bf16), h[1], TN)
            dk_acc[rows, :] += dk_t
            dv_acc[rows, :] += dv_t
            return dq_acc, cps[0], cps[1], ces[0], ces[1]

        zc = jnp.zeros((SB_T, 1), f32)
        carry = (jnp.zeros((SB_T, 128), f32), zc, zc, zc, zc)
        done = lowest
        tail_lo = i - jnp.minimum(i, SB_TAIL - 1)
        for n in SB_GROUPS_BWD:
            trips = (tail_lo - done) // n
            carry = lax.fori_loop(
                0, trips, functools.partial(lambda gi, cr, n, done: group(done + gi * n, n, cr, None), n=n, done=done),
                carry)
            done = done + trips * n
        carry = _by_count(i, SB_TAIL, lambda n: group(i - n + 1, n, carry, causal))
        dq_ref[...] = (carry[0] * scale).astype(bf16)

        @pl.when(i == nq - 1)
        def _():
            dk_ref[...] = dk_acc[...].astype(bf16)
            dv_ref[...] = dv_acc[...].astype(bf16)

        if ride:
            pl.when((pl.program_id(0) == n_pairs - 1) & (i == nq - 1))(
                lambda: ride.finish(r_ins, r_lnd, r_sems))

    qs = pl.BlockSpec((SB_T, 128), lambda h, i: (i, h))
    full = pl.BlockSpec((S, 128), lambda h, i: (0, h))
    out = pl.pallas_call(
        body, name="sb_bwd", grid=(n_pairs, nq),
        in_specs=[qs,
                  pl.BlockSpec((S, 128), lambda h, i: (0, n_pairs + h)),
                  pl.BlockSpec((S, 128), lambda h, i: (0, 2 * n_pairs + h)),
                  qs, qs] + (ride.in_specs if ride else []),
        out_specs=[qs, full, full] + (ride.out_specs if ride else []),
        out_shape=[jax.ShapeDtypeStruct((S, D), bf16)] * 3 + (ride.out_shape if ride else []),
        scratch_shapes=[pltpu.VMEM((S, 128), f32), pltpu.VMEM((S, 128), f32)] + (ride.scratch if ride else []),
        compiler_params=_params(("arbitrary", "arbitrary")),
    )(proj, proj, proj, tot_lk, do, *(ride.srcs if ride else []))
    return out[0], out[1], out[2], list(out[3:])


CONV_CB = 256
HALO = 8


def _conv_fwd(proj, conv_w, conv_b, S):
    tr = min(512, S)

    def body(x_ref, w_ref, b_ref, xc_ref, xbc_ref):
        w = w_ref[...]
        for t in range(S // tr):
            cur = x_ref[t * tr:(t + 1) * tr, :]
            halo = x_ref[t * tr - HALO:t * tr, :] if t else jnp.zeros((HALO, CONV_CB), f32)
            win = jnp.concatenate([halo, cur], axis=0)
            acc = b_ref[...] + w[CONV_K - 1:CONV_K, :] * cur
            for k in range(CONV_K - 1):
                acc = acc + w[k:k + 1, :] * pltpu.roll(win, CONV_K - 1 - k, 0)[HALO:, :]
            xc_ref[t * tr:(t + 1) * tr, :] = acc
            xbc_ref[t * tr:(t + 1) * tr, :] = acc * _sigmoid(acc)

    col = pl.BlockSpec((S, CONV_CB), lambda c: (0, c))
    return pl.pallas_call(
        body, name="conv_fwd", grid=(CONV_DIM // CONV_CB,),
        in_specs=[pl.BlockSpec((S, CONV_CB), lambda c: (0, P_XBC // CONV_CB + c)),
                  pl.BlockSpec((CONV_K, CONV_CB), lambda c: (0, c)),
                  pl.BlockSpec((1, CONV_CB), lambda c: (0, c))],
        out_specs=[col, col], out_shape=[jax.ShapeDtypeStruct((S, CONV_DIM), f32)] * 2,
        compiler_params=_params(("parallel",)),
    )(proj, conv_w, conv_b)


def _conv_bwd(proj, xc, dxbc, conv_w, S):
    tr = min(512, S)

    def body(x_ref, xc_ref, dy_ref, w_ref, dx_ref, dw_ref, db_ref, dxc_s):
        w = w_ref[...]
        xcv = xc_ref[...]
        sg = _sigmoid(xcv)
        dxc_s[0:S, :] = dy_ref[...] * (sg * (1.0 + xcv * (1.0 - sg)))
        dxc_s[S:S + HALO, :] = jnp.zeros((HALO, CONV_CB), f32)
        dws = [jnp.zeros((1, CONV_CB), f32) for _ in range(CONV_K)]
        db = jnp.zeros((1, CONV_CB), f32)
        for t in range(S // tr):
            cur = x_ref[t * tr:(t + 1) * tr, :]
            halo = x_ref[t * tr - HALO:t * tr, :] if t else jnp.zeros((HALO, CONV_CB), f32)
            win = jnp.concatenate([halo, cur], axis=0)
            dwin = dxc_s[t * tr:(t + 1) * tr + HALO, :]
            dcur = dwin[0:tr, :]
            db = db + jnp.sum(dcur, axis=0, keepdims=True)
            dws[CONV_K - 1] = dws[CONV_K - 1] + jnp.sum(dcur * cur, axis=0, keepdims=True)
            dx = w[CONV_K - 1:CONV_K, :] * dcur
            for k in range(CONV_K - 1):
                sh = CONV_K - 1 - k
                dws[k] = dws[k] + jnp.sum(dcur * pltpu.roll(win, sh, 0)[HALO:, :], axis=0, keepdims=True)
                dx = dx + w[k:k + 1, :] * pltpu.roll(dwin, tr + HALO - sh, 0)[0:tr, :]
            dx_ref[t * tr:(t + 1) * tr, :] = dx.astype(bf16)
        dw_ref[...] = jnp.concatenate(dws + [jnp.zeros((8 - CONV_K, CONV_CB), f32)], axis=0)
        db_ref[...] = db

    col = pl.BlockSpec((S, CONV_CB), lambda c: (0, c))
    return pl.pallas_call(
        body, name="conv_bwd", grid=(CONV_DIM // CONV_CB,),
        in_specs=[pl.BlockSpec((S, CONV_CB), lambda c: (0, P_XBC // CONV_CB + c)), col, col,
                  pl.BlockSpec((CONV_K, CONV_CB), lambda c: (0, c))],
        out_specs=[col, pl.BlockSpec((8, CONV_CB), lambda c: (0, c)), pl.BlockSpec((1, CONV_CB), lambda c: (0, c))],
        out_shape=[jax.ShapeDtypeStruct((S, CONV_DIM), bf16), jax.ShapeDtypeStruct((8, CONV_DIM), f32),
                   jax.ShapeDtypeStruct((1, CONV_DIM), f32)],
        scratch_shapes=[pltpu.VMEM((S + HALO, CONV_CB), f32)],
        compiler_params=_params(("parallel",)),
    )(proj, xc, dxbc, conv_w)


N_PAIR = SSD_HEADS // 2
NEG = -1e30


def _softplus(x):
    return jnp.maximum(x, 0.0) + jnp.log(1.0 + jnp.exp(-jnp.abs(x)))


def _ssd_common(dtr, dtb, alog):
    L = SSD_L
    r_i = lax.broadcasted_iota(jnp.int32, (L, L), 0)
    c_i = lax.broadcasted_iota(jnp.int32, (L, L), 1)
    dt = _softplus(dtr + dtb)
    a = -jnp.exp(alog)
    da = dt * a
    lower = (r_i >= c_i).astype(bf16)
    upper = (r_i <= c_i).astype(bf16)
    parts = _split3(da)
    a_cs = sum(_dot(lower, p) for p in parts)
    a_cs_t = sum(_dot(p, upper, TN) for p in parts)
    return dt, a, a_cs, a_cs_t, r_i >= c_i


def _pair_vec(lane, v, h):
    return jnp.where(lane < SB_HD, v[:, h:h + 1], v[:, h + 1:h + 2])


def _decay_mat(a_cs, a_cs_t, h, tril):
    return jnp.exp(jnp.where(tril, a_cs[:, h:h + 1] - a_cs_t[h:h + 1, :], NEG))


def _ssd_fwd(xbc, proj, pdt, dt_bias_p, a_log_p, d_skip_c, ssd_norm, S):
    L = SSD_L
    nc = S // L

    def body(xbc_ref, dt_ref, z_ref, dtb_ref, alog_ref, dsk_ref, gn_ref, y_ref, yn_ref, hp_ref, state):
        c = pl.program_id(0)

        @pl.when(c == 0)
        def _():
            state[...] = jnp.zeros_like(state)

        hp_ref[0] = state[...]
        lane = lax.broadcasted_iota(jnp.int32, (1, 128), 1)
        row128 = lax.broadcasted_iota(jnp.int32, (128, 1), 0)
        m_a, m_b = _sb_masks()
        dt, a, a_cs, a_cs_t, tril = _ssd_common(dt_ref[...], dtb_ref[...], alog_ref[...])
        a_last = a_cs[L - 1:L, :]
        for g in range(SSD_GROUPS):
            b_g = xbc_ref[:, SSD_INNER + g * SSD_N:SSD_INNER + (g + 1) * SSD_N].astype(bf16)
            c_g = xbc_ref[:, SSD_INNER + (SSD_GROUPS + g) * SSD_N:SSD_INNER + (SSD_GROUPS + g + 1) * SSD_N].astype(bf16)
            cb = _dot(c_g, b_g, NT)
            for pr in range(4):
                h = 8 * g + 2 * pr
                pi = h // 2
                cols = slice(pi * 128, (pi + 1) * 128)
                xs = xbc_ref[:, cols]
                x = xs * _pair_vec(lane, dt, h)
                acs = _pair_vec(lane, a_cs, h)
                al = _pair_vec(lane, a_last, h)
                w_a = (cb * _decay_mat(a_cs, a_cs_t, h, tril)).astype(bf16)
                w_b = (cb * _decay_mat(a_cs, a_cs_t, h + 1, tril)).astype(bf16)
                yd = _dot(w_a, (x * m_a).astype(bf16)) + _dot(w_b, (x * m_b).astype(bf16))
                hp = state[pi]
                yo = _dot(c_g, hp.astype(bf16), NT) * jnp.exp(acs)
                y_ref[:, cols] = yd + yo + dsk_ref[:, cols] * xs
                dec = jnp.exp(jnp.where(row128 < SB_HD, a_last[:, h:h + 1], a_last[:, h + 1:h + 2]))
                state[pi] = hp * dec + _dot((x * jnp.exp(al - acs)).astype(bf16), b_g, TN)
        zz = z_ref[...]
        y2 = y_ref[...] * (zz * _sigmoid(zz))
        gw = SSD_INNER // SSD_GROUPS
        for g in range(SSD_GROUPS):
            yg = y2[:, g * gw:(g + 1) * gw]
            rg = lax.rsqrt(jnp.mean(yg * yg, axis=1, keepdims=True) + EPS)
            yn_ref[:, g * gw:(g + 1) * gw] = (yg * rg * gn_ref[:, g * gw:(g + 1) * gw]).astype(bf16)

    vec128 = pl.BlockSpec((1, 128), lambda c: (0, 0))
    vecin = pl.BlockSpec((1, SSD_INNER), lambda c: (0, 0))
    rows = pl.BlockSpec((L, SSD_INNER), lambda c: (c, 0))
    return pl.pallas_call(
        body, name="ssd_fwd", grid=(nc,),
        in_specs=[pl.BlockSpec((L, CONV_DIM), lambda c: (c, 0)),
                  pl.BlockSpec((L, 128), lambda c: (c, 0)),
                  pl.BlockSpec((L, SSD_INNER), lambda c: (c, P_Z // SSD_INNER)),
                  vec128, vec128, vecin, vecin],
        out_specs=[rows, rows, pl.BlockSpec((1, N_PAIR, 128, SSD_N), lambda c: (c, 0, 0, 0))],
        out_shape=[jax.ShapeDtypeStruct((S, SSD_INNER), f32), jax.ShapeDtypeStruct((S, SSD_INNER), bf16),
                   jax.ShapeDtypeStruct((nc, N_PAIR, 128, SSD_N), f32)],
        scratch_shapes=[pltpu.VMEM((N_PAIR, 128, SSD_N), f32)],
        compiler_params=_params(("arbitrary",)),
    )(xbc, pdt, proj, dt_bias_p, a_log_p, d_skip_c, ssd_norm)


def _sum_all(v):
    return jnp.sum(jnp.sum(v, axis=1, keepdims=True), axis=0, keepdims=True)


def _ssd_bwd(dyn, y, xbc, proj, pdt, hprev, dt_bias_p, a_log_p, d_skip_c, ssd_norm, S, ride=None):
    L = SSD_L
    nc = S // L
    n_r = ride.n if ride else 0

    col = lax.broadcasted_iota(jnp.int32, (2 * SSD_INNER, 128), 0)
    head = lax.broadcasted_iota(jnp.int32, (2 * SSD_INNER, 128), 1)
    sel_pair = (col[:SSD_INNER] // SB_HD == head[:SSD_INNER]).astype(bf16)
    sel_head = (col // 128 == head).astype(bf16)

    def body(*refs):
        (dyn_ref, y_ref, xbc_ref, dt_ref, z_ref, hp_ref, dtb_ref, alog_ref, dsk_ref, gn_ref,
         selp_ref, selh_ref) = refs[:12]
        dz_ref, dxbc_ref, ddt_ref, dgn_ref, dsk_out, dalog_ref, ddtb_ref = refs[12 + n_r:19 + n_r]
        dstate, dy_s, st_a, st_q, st_d, st_x, dat = refs[19 + 2 * n_r:26 + 2 * n_r]
        r_ins, r_lnd, r_sems = refs[12:12 + n_r], refs[19 + n_r:19 + 2 * n_r], refs[26 + 2 * n_r:]
        c = pl.program_id(0)
        if ride:
            pl.when(c == 0)(lambda: ride.start(r_ins, r_lnd, r_sems))

        @pl.when(c == 0)
        def _():
            dat[...] = jnp.zeros_like(dat)
            dstate[...] = jnp.zeros_like(dstate)
            dgn_ref[...] = jnp.zeros_like(dgn_ref)
            dsk_out[...] = jnp.zeros_like(dsk_out)
            dalog_ref[...] = jnp.zeros_like(dalog_ref)
            ddtb_ref[...] = jnp.zeros_like(ddtb_ref)

        lane = lax.broadcasted_iota(jnp.int32, (1, 128), 1)
        row128 = lax.broadcasted_iota(jnp.int32, (128, 1), 0)
        rowl = lax.broadcasted_iota(jnp.int32, (L, 1), 0)
        m_a, m_b = _sb_masks()
        dtr = dt_ref[...]
        dt, a, a_cs, a_cs_t, tril = _ssd_common(dtr, dtb_ref[...], alog_ref[...])
        a_last = a_cs[L - 1:L, :]

        zz = z_ref[...]
        sg = _sigmoid(zz)
        silu = zz * sg
        yv = y_ref[...]
        y2 = yv * silu
        gw = SSD_INNER // SSD_GROUPS
        for g in range(SSD_GROUPS):
            sl = slice(g * gw, (g + 1) * gw)
            yg = y2[:, sl]
            rg = lax.rsqrt(jnp.mean(yg * yg, axis=1, keepdims=True) + EPS)
            yh = yg * rg
            dyn_g = dyn_ref[:, sl]
            dgn_ref[:, sl] += jnp.sum(dyn_g * yh, axis=0, keepdims=True)
            dyh = dyn_g * gn_ref[:, sl]
            dy2 = rg * (dyh - yh * jnp.mean(dyh * yh, axis=1, keepdims=True))
            dy_s[:, sl] = dy2 * silu[:, sl]
            dz_ref[:, sl] = (dy2 * yv[:, sl] * (sg[:, sl] * (1.0 + zz[:, sl] * (1.0 - sg[:, sl])))).astype(bf16)

        last_row = jnp.zeros((1, 128), f32)
        dsk_acc = jnp.zeros((1, 128), f32)
        for g in range(SSD_GROUPS):
            bsl = slice(SSD_INNER + g * SSD_N, SSD_INNER + (g + 1) * SSD_N)
            csl = slice(SSD_INNER + (SSD_GROUPS + g) * SSD_N, SSD_INNER + (SSD_GROUPS + g + 1) * SSD_N)
            b_g = xbc_ref[:, bsl].astype(bf16)
            c_g = xbc_ref[:, csl].astype(bf16)
            cb = _dot(c_g, b_g, NT)
            dcb = jnp.zeros((L, L), f32)
            dc_g = jnp.zeros((L, SSD_N), f32)
            db_g = jnp.zeros((L, SSD_N), f32)
            for pr in range(4):
                h = 8 * g + 2 * pr
                pi = h // 2
                cols = slice(pi * 128, (pi + 1) * 128)
                xs = xbc_ref[:, cols]
                dt_p = _pair_vec(lane, dt, h)
                x = xs * dt_p
                acs = _pair_vec(lane, a_cs, h)
                al = _pair_vec(lane, a_last, h)
                e_a = jnp.exp(acs)
                dte = jnp.exp(al - acs)
                m_mat_a = _decay_mat(a_cs, a_cs_t, h, tril)
                m_mat_b = _decay_mat(a_cs, a_cs_t, h + 1, tril)
                dyp = dy_s[:, cols]
                dsk = dsk_ref[:, cols]
                d_hn = dstate[pi]
                hp = hp_ref[0, pi]
                dy_a = (dyp * m_a).astype(bf16)
                dy_b = (dyp * m_b).astype(bf16)
                x_b = x.astype(bf16)
                gm_a = _dot(dy_a, x_b, NT) * m_mat_a
                gm_b = _dot(dy_b, x_b, NT) * m_mat_b
                dcb = dcb + gm_a + gm_b
                dx_d = _dot((cb * m_mat_a).astype(bf16), dy_a, TN) + _dot((cb * m_mat_b).astype(bf16), dy_b, TN)
                dx_s = _dot(b_g, d_hn.astype(bf16), NT) * dte
                dx = dx_d + dx_s
                dxbc_ref[:, cols] = dx * dt_p + dsk * dyp
                xdxs = x * dx_s
                st_x[:, cols] = xdxs
                st_a[:, cols] = dyp * (_dot(c_g, hp.astype(bf16), NT) * e_a) - xdxs
                st_d[:, cols] = dx * xs
                hh = d_hn * hp
                dsk_row = jnp.sum(dyp * xs, axis=0, keepdims=True)
                dec = jnp.exp(jnp.where(row128 < SB_HD, a_last[:, h:h + 1], a_last[:, h + 1:h + 2]))
                for hd, m, gm in ((h, m_a, gm_a), (h + 1, m_b, gm_b)):
                    half = slice(0, SB_HD) if hd == h else slice(SB_HD, 128)
                    qm = gm * cb
                    st_q[:, hd * 128:(hd + 1) * 128] = qm
                    dat[hd:hd + 1, :] = jnp.sum(qm, axis=0, keepdims=True)
                    hh_sum = jnp.sum(jnp.sum(hh[half, :], axis=0, keepdims=True), axis=1, keepdims=True)
                    last_row = jnp.where(lane == hd, jnp.exp(a_last[:, hd:hd + 1]) * hh_sum, last_row)
                    dsk_acc = jnp.where(lane == hd, jnp.sum(dsk_row * m, axis=1, keepdims=True), dsk_acc)
                dye = (dyp * e_a).astype(bf16)
                dc_g = dc_g + _dot(dye, hp.astype(bf16))
                db_g = db_g + _dot((x * dte).astype(bf16), d_hn.astype(bf16))
                dstate[pi] = dec * d_hn + _dot(dye, c_g, TN)
            dcb_b = dcb.astype(bf16)
            dxbc_ref[:, csl] = dc_g + _dot(dcb_b, b_g)
            dxbc_ref[:, bsl] = db_g + _dot(dcb_b, c_g, TN)

        r_i = lax.broadcasted_iota(jnp.int32, (L, L), 0)
        c_i = lax.broadcasted_iota(jnp.int32, (L, L), 1)
        rev = (r_i <= c_i).astype(bf16)

        def head_sums(st, sel, split=_split2):
            return sum(_dot(p, sel[...]) for p in split(st[...]))

        last_row = last_row + jnp.sum(head_sums(st_x, selp_ref), axis=0, keepdims=True)
        d_acs = (head_sums(st_a, selp_ref) + head_sums(st_q, selh_ref, _split3)
                 + jnp.where(rowl == L - 1, last_row, 0.0))
        ddt_x = head_sums(st_d, selp_ref)
        dda = sum(_dot(rev, p) for p in _split3(d_acs)) - sum(_dot(rev, p, NT) for p in _split3(dat[...]))
        ddt = ddt_x + dda * a
        dalog_ref[...] += jnp.sum(dda * dt, axis=0, keepdims=True) * a
        ddtr = jnp.where(lane < SSD_HEADS, ddt * _sigmoid(dtr + dtb_ref[...]), 0.0)
        ddt_ref[...] = ddtr.astype(bf16)
        ddtb_ref[...] += jnp.sum(ddtr, axis=0, keepdims=True)
        dsk_out[...] += dsk_acc
        if ride:
            pl.when(c == nc - 1)(lambda: ride.finish(r_ins, r_lnd, r_sems))

    rv = lambda c: nc - 1 - c
    vec128 = pl.BlockSpec((1, 128), lambda c: (0, 0))
    vecin = pl.BlockSpec((1, SSD_INNER), lambda c: (0, 0))
    rows = pl.BlockSpec((L, SSD_INNER), lambda c: (rv(c), 0))
    return pl.pallas_call(
        body, name="ssd_bwd", grid=(nc,),
        in_specs=[rows, rows,
                  pl.BlockSpec((L, CONV_DIM), lambda c: (rv(c), 0)),
                  pl.BlockSpec((L, 128), lambda c: (rv(c), 0)),
                  pl.BlockSpec((L, SSD_INNER), lambda c: (rv(c), P_Z // SSD_INNER)),
                  pl.BlockSpec((1, N_PAIR, 128, SSD_N), lambda c: (rv(c), 0, 0, 0)),
                  vec128, vec128, vecin, vecin,
                  pl.BlockSpec((SSD_INNER, 128), lambda c: (0, 0)),
                  pl.BlockSpec((2 * SSD_INNER, 128), lambda c: (0, 0))] + (ride.in_specs if ride else []),
        out_specs=[rows, pl.BlockSpec((L, CONV_DIM), lambda c: (rv(c), 0)),
                   pl.BlockSpec((L, 128), lambda c: (rv(c), 0)), vecin, vec128, vec128, vec128]
        + (ride.out_specs if ride else []),
        out_shape=[jax.ShapeDtypeStruct((S, SSD_INNER), bf16), jax.ShapeDtypeStruct((S, CONV_DIM), f32),
                   jax.ShapeDtypeStruct((S, 128), bf16), jax.ShapeDtypeStruct((1, SSD_INNER), f32),
                   jax.ShapeDtypeStruct((1, 128), f32), jax.ShapeDtypeStruct((1, 128), f32),
                   jax.ShapeDtypeStruct((1, 128), f32)] + (ride.out_shape if ride else []),
        scratch_shapes=[pltpu.VMEM((N_PAIR, 128, SSD_N), f32), pltpu.VMEM((L, SSD_INNER), f32),
                        pltpu.VMEM((L, SSD_INNER), f32), pltpu.VMEM((L, 2 * SSD_INNER), f32),
                        pltpu.VMEM((L, SSD_INNER), f32), pltpu.VMEM((L, SSD_INNER), f32),
                        pltpu.VMEM((128, L), f32)]
        + (ride.scratch if ride else []),
        compiler_params=_params(("arbitrary",)),
    )(dyn, y, xbc, pdt, proj, hprev, dt_bias_p, a_log_p, d_skip_c, ssd_norm, sel_pair, sel_head,
      *(ride.srcs if ride else []))


MEM_W = MEM_HEADS * MEM_HD


def _mem_probs(q, k):
    s = _dot(q, k, NT) * (MEM_HD ** -0.5)
    s = s - jnp.max(s, axis=1, keepdims=True)
    p = jnp.exp(s)
    return p / jnp.sum(p, axis=1, keepdims=True)


def _mem_fwd(proj, kv, S, tm=512):
    tm = min(tm, S)
    M = kv.shape[0]

    def body(q_ref, kv_ref, o_ref):
        for h in range(MEM_HEADS):
            sl = slice(h * MEM_HD, (h + 1) * MEM_HD)
            vsl = slice(MEM_W + h * MEM_HD, MEM_W + (h + 1) * MEM_HD)
            p = _mem_probs(q_ref[:, sl].astype(bf16), kv_ref[:, sl].astype(bf16))
            o_ref[:, sl] = _dot(p.astype(bf16), kv_ref[:, vsl].astype(bf16)).astype(bf16)

    return pl.pallas_call(
        body, name="mem_fwd", grid=(S // tm,),
        in_specs=[pl.BlockSpec((tm, MEM_W), lambda i: (i, P_MEMQ // MEM_W)),
                  pl.BlockSpec((M, 2 * MEM_W), lambda i: (0, 0))],
        out_specs=pl.BlockSpec((tm, MEM_W), lambda i: (i, 0)),
        out_shape=jax.ShapeDtypeStruct((S, MEM_W), bf16),
        compiler_params=_params(("parallel",)),
    )(proj, kv)


def _mem_bwd(proj, kv, dy, S, tm=512):
    tm = min(tm, S)
    M = kv.shape[0]
    scale = MEM_HD ** -0.5

    def body(q_ref, kv_ref, dy_ref, dq_ref, dkv_ref):
        @pl.when(pl.program_id(0) == 0)
        def _():
            dkv_ref[...] = jnp.zeros_like(dkv_ref)

        for h in range(MEM_HEADS):
            sl = slice(h * MEM_HD, (h + 1) * MEM_HD)
            vsl = slice(MEM_W + h * MEM_HD, MEM_W + (h + 1) * MEM_HD)
            q = q_ref[:, sl].astype(bf16)
            k = kv_ref[:, sl].astype(bf16)
            v = kv_ref[:, vsl].astype(bf16)
            dyh = dy_ref[:, sl].astype(bf16)
            p = _mem_probs(q, k)
            dp = _dot(dyh, v, NT)
            ds = (p * (dp - jnp.sum(dp * p, axis=1, keepdims=True)) * scale).astype(bf16)
            dq_ref[:, sl] = _dot(ds, k).astype(bf16)
            dkv_ref[:, sl] += _dot(ds, q, TN)
            dkv_ref[:, vsl] += _dot(p.astype(bf16), dyh, TN)

    return pl.pallas_call(
        body, name="mem_bwd", grid=(S // tm,),
        in_specs=[pl.BlockSpec((tm, MEM_W), lambda i: (i, P_MEMQ // MEM_W)),
                  pl.BlockSpec((M, 2 * MEM_W), lambda i: (0, 0)),
                  pl.BlockSpec((tm, MEM_W), lambda i: (i, 0))],
        out_specs=[pl.BlockSpec((tm, MEM_W), lambda i: (i, 0)), pl.BlockSpec((M, 2 * MEM_W), lambda i: (0, 0))],
        out_shape=[jax.ShapeDtypeStruct((S, MEM_W), bf16), jax.ShapeDtypeStruct((M, 2 * MEM_W), f32)],
        compiler_params=_params(("arbitrary",)),
    )(proj, kv, dy)


def _merge_fwd(proj, t0, t1, t2, S, tm=512):
    tm = min(tm, S)

    def body(g_ref, t0_ref, t1_ref, t2_ref, o_ref):
        acc = jnp.zeros((tm, D), f32)
        for b, t_ref in enumerate((t0_ref, t1_ref, t2_ref)):
            acc = acc + _sigmoid(g_ref[:, b * D:(b + 1) * D]) * t_ref[...]
        o_ref[...] = acc.astype(bf16)

    row = pl.BlockSpec((tm, D), lambda i: (i, 0))
    return pl.pallas_call(
        body, name="merge_fwd", grid=(S // tm,),
        in_specs=[pl.BlockSpec((tm, 3 * D), lambda i: (i, P_GATE // (3 * D))), row, row, row],
        out_specs=row, out_shape=jax.ShapeDtypeStruct((S, D), bf16),
        compiler_params=_params(("parallel",)),
    )(proj, t0, t1, t2)


def _merge_bwd(proj, t0, t1, t2, dm, S, tm=512):
    tm = min(tm, S)

    def body(g_ref, t0_ref, t1_ref, t2_ref, dm_ref, d0_ref, d1_ref, d2_ref, dg_ref):
        dmv = dm_ref[...]
        for b, (t_ref, d_ref) in enumerate(((t0_ref, d0_ref), (t1_ref, d1_ref), (t2_ref, d2_ref))):
            sg = _sigmoid(g_ref[:, b * D:(b + 1) * D])
            d_ref[...] = (dmv * sg).astype(bf16)
            dg_ref[:, b * D:(b + 1) * D] = (dmv * t_ref[...] * sg * (1.0 - sg)).astype(bf16)

    row = pl.BlockSpec((tm, D), lambda i: (i, 0))
    return pl.pallas_call(
        body, name="merge_bwd", grid=(S // tm,),
        in_specs=[pl.BlockSpec((tm, 3 * D), lambda i: (i, P_GATE // (3 * D))), row, row, row, row],
        out_specs=[row, row, row, pl.BlockSpec((tm, 3 * D), lambda i: (i, 0))],
        out_shape=[jax.ShapeDtypeStruct((S, D), bf16)] * 3 + [jax.ShapeDtypeStruct((S, 3 * D), bf16)],
        compiler_params=_params(("parallel",)),
    )(proj, t0, t1, t2, dm)


def _loss_head(ff, g, h1, target, S, tm=512):
    tm = min(tm, S)

    def body(ff_ref, g_ref, h1_ref, t_ref, dh_ref, loss_ref):
        xv = ff_ref[...]
        r = lax.rsqrt(jnp.mean(xv * xv, axis=1, keepdims=True) + EPS)
        err = h1_ref[...] + xv * r * g_ref[...] - t_ref[...]
        dh_ref[...] = err * (1.0 / D)

        @pl.when(pl.program_id(0) == 0)
        def _():
            loss_ref[...] = jnp.zeros_like(loss_ref)

        loss_ref[...] += 0.5 * _sum_all(jnp.mean(err * err, axis=1, keepdims=True)) * jnp.ones((1, 128), f32)

    row = pl.BlockSpec((tm, D), lambda i: (i, 0))
    return pl.pallas_call(
        body, name="loss_head", grid=(S // tm,),
        in_specs=[row, pl.BlockSpec((1, D), lambda i: (0, 0)), row, row],
        out_specs=[row, pl.BlockSpec((1, 128), lambda i: (0, 0))],
        out_shape=[jax.ShapeDtypeStruct((S, D), f32), jax.ShapeDtypeStruct((1, 128), f32)],
        compiler_params=_params(("arbitrary",)),
    )(ff, g, h1, target)


def _local_step(x, mem, target, wts, late_rides, late_weights, small, rest_rides, w_in_ride):
    S = x.shape[0]
    M = mem.shape[0]
    pad = lambda v: jnp.pad(v, ((0, 0), (0, 128 - SSD_HEADS)))
    dtb_p, alog_p = pad(small["dt_bias"]), pad(small["a_log"])
    dsk_c = jnp.repeat(small["d_skip"], SB_HD, axis=1)

    u = _rms_fwd(x, small["norm_mix_pre"], name="norm_pre", out_dtype=bf16)
    if late_rides:
        proj, lands_a = _mm(u, wts["w_main"], "nn", tm=1024, tn=1024, name="in_proj", ride=late_rides[0])
    else:
        proj, lands_a = _mm(u, wts["w_main"], "nn", tm=1024, tn=1024, name="in_proj"), []
    pdt = _mm(u, wts["w_dt"], "nn", tm=1024, tn=128, name="in_proj_dt")
    y_sb, tot_lk, lands_b = _sb_fwd(proj, S, late_rides[1] if late_rides else None)
    wts = dict(wts, **late_weights(lands_a + lands_b))
    small = dict(small, conv_w=wts.pop("conv_w"))
    xc, xbc = _conv_fwd(proj, small["conv_w"], small["conv_b"], S)
    y_ssd, yn, hprev = _ssd_fwd(xbc, proj, pdt, dtb_p, alog_p, dsk_c, small["ssd_norm"], S)
    mn = _rms_fwd(mem, small["norm_mem"], name="norm_mem", out_dtype=bf16, tm=min(512, M))
    kv = _mm(mn, wts["w_mem_kv"], "nn", tm=M, tn=1024, name="mem_kv")
    y_mem = _mem_fwd(proj, kv, S)
    t0 = _mm(y_sb, wts["w_sb_out"], "nn", tm=1024, tn=1024, name="sb_out")
    t1 = _mm(yn, wts["w_ssd_out"], "nn", tm=1024, tn=1024, name="ssd_out")
    t2 = _mm(y_mem, wts["w_mem_out"], "nn", tm=1024, tn=1024, name="mem_out")
    merged = _merge_fwd(proj, t0, t1, t2, S)
    mix = _mm(merged, wts["w_o"], "nn", tm=1024, tn=1024, name="w_o")
    h1 = _rms_fwd(mix, small["norm_mix_post"], name="norm_mix_post", out_dtype=f32, residual=x)
    u2 = _rms_fwd(h1, small["norm_mlp_pre"], name="norm_mlp_pre", out_dtype=bf16)
    a_up, hrelu = _mm(u2, wts["w_up"], "nn", tm=1024, tn=1024, name="mlp_up", out_dtypes=(f32, bf16),
                      epi=lambda acc: (acc, jnp.square(jnp.maximum(acc, 0.0))))
    ff = _mm(hrelu, wts["w_down"], "nn", tm=1024, tn=1024, name="mlp_down")
    dh2, loss = _loss_head(ff, small["norm_mlp_post"], h1, target, S)

    g = {}
    dff, g["norm_mlp_post"] = _rms_bwd(ff, dh2, small["norm_mlp_post"], name="norm_mlp_post_bwd", out_dtype=bf16)
    da = _mm(dff, wts["w_down"], "nt", tm=1024, tn=1024, name="mlp_down_dx", out_dtypes=(bf16,),
             epi=lambda acc, a: (acc * (2.0 * jnp.maximum(a, 0.0)),), extras=(a_up,))
    g["w_down"] = _mm(hrelu, dff, "tn", tm=1024, tn=1024, name="mlp_down_dw")
    du2 = _mm(da, wts["w_up"], "nt", tm=1024, tn=1024, name="mlp_up_dx")
    g["w_up"] = _mm(u2, da, "tn", tm=1024, tn=1024, name="mlp_up_dw")
    dh1, g["norm_mlp_pre"] = _rms_bwd(h1, du2, small["norm_mlp_pre"], name="norm_mlp_pre_bwd", out_dtype=f32, add=dh2)
    dmix, g["norm_mix_post"] = _rms_bwd(mix, dh1, small["norm_mix_post"], name="norm_mix_post_bwd", out_dtype=bf16)
    dmerged = _mm(dmix, wts["w_o"], "nt", tm=1024, tn=1024, name="w_o_dx")
    g["w_o"] = _mm(merged, dmix, "tn", tm=1024, tn=1024, name="w_o_dw")
    dt0, dt1, dt2, dgl = _merge_bwd(proj, t0, t1, t2, dmerged, S)
    dy_sb = _mm(dt0, wts["w_sb_out"], "nt", tm=1024, tn=1024, name="sb_out_dx")
    g["w_sb_out"] = _mm(y_sb, dt0, "tn", tm=1024, tn=1024, name="sb_out_dw")
    dy_ssd = _mm(dt1, wts["w_ssd_out"], "nt", tm=1024, tn=1024, name="ssd_out_dx")
    g["w_ssd_out"] = _mm(yn, dt1, "tn", tm=1024, tn=1024, name="ssd_out_dw")
    dy_mem = _mm(dt2, wts["w_mem_out"], "nt", tm=1024, tn=1024, name="mem_out_dx")
    g["w_mem_out"] = _mm(y_mem, dt2, "tn", tm=1024, tn=1024, name="mem_out_dw")
    dmemq, dkv = _mem_bwd(proj, kv, dy_mem, S)
    g["w_mem_kv"] = _mm(mn, dkv, "tn", tm=1024, tn=1024, name="mem_kv_dw")
    dmn = _mm(dkv, wts["w_mem_kv"], "nt", tm=M, tn=1024, name="mem_kv_dx")
    _, g["norm_mem"] = _rms_bwd(mem, dmn, small["norm_mem"], name="norm_mem_bwd", out_dtype=bf16, tm=min(512, M))
    rides = rest_rides(g) if rest_rides else (None, None)
    dz, dxbc, ddt, g["ssd_norm"], dsk, dalog, ddtb, *lands_a = _ssd_bwd(
        dy_ssd, y_ssd, xbc, proj, pdt, hprev, dtb_p, alog_p, dsk_c, small["ssd_norm"], S, rides[0])
    g["d_skip"], g["a_log"], g["dt_bias"] = dsk[:, :SSD_HEADS], dalog[:, :SSD_HEADS], ddtb[:, :SSD_HEADS]
    dxbc_raw, dcw, g["conv_b"] = _conv_bwd(proj, xc, dxbc, small["conv_w"], S)
    g["conv_w"] = dcw[:CONV_K]
    dq, dk, dv, lands_b = _sb_bwd(proj, tot_lk, dy_sb, S, rides[1])
    g["rest_lands"] = lands_b + lands_a
    dproj = (dq, dk, dv, dxbc_raw, dgl, dmemq, dz)
    g["w_main"] = [_mm(u, p, "tn", tm=1024, tn=1024, name="in_proj_dw_%d" % i) for i, p in enumerate(dproj)]
    g["w_dt"] = _mm(u, ddt, "tn", tm=1024, tn=128, name="in_proj_dt_dw")
    du_dt = _mm(ddt, wts["w_dt"], "nt", tm=1024, tn=1024, name="in_proj_dt_dx")
    du, g["w_in_lands"] = _mm_pieces_nt(dproj, wts["w_main"], du_dt, tm=512, tn=256, name="in_proj_dx",
                                        ride=w_in_ride(g) if w_in_ride else None)
    grad_x, g["norm_mix_pre"] = _rms_bwd(x, du, small["norm_mix_pre"], name="norm_pre_bwd", out_dtype=f32, add=dh1)
    return loss, grad_x, g


def _to_internal(w_in):
    sec = lambda r: w_in[:, r[0]:r[1]]
    w_main = jnp.concatenate([sec(R_QKV), sec(R_XBC), sec(R_GATE), sec(R_MEMQ), sec(R_Z)], axis=1)
    w_dt = jnp.pad(sec(R_DT), ((0, 0), (0, 128 - SSD_HEADS)))
    return w_main, w_dt


def _from_internal(pieces, g_dt):
    dq, dk, dv, dxbc, dgate, dmemq, dz = pieces
    return jnp.concatenate([dq, dk, dv, dz, dxbc, g_dt[:, :SSD_HEADS], dmemq, dgate], axis=1)


MESH = pl.DeviceIdType.MESH
ANY = pl.BlockSpec(memory_space=pl.ANY)


def _place():
    x, y, c = lax.axis_index("x"), lax.axis_index("y"), lax.axis_index("c")
    return (x, y, c), [(1 - x, y, c), (x, 1 - y, c), (1 - x, 1 - y, c)]


def _exchange_copy(mode, ins, lands, send, recv, a, k, me, peers, arriving):
    p = peers[k]
    theirs = 2 * p[0] + p[1]
    if mode == "gather":
        src, dst = ins[a], lands[a].at[theirs if arriving else me]
    else:
        src, dst = ins[a].at[theirs], lands[a].at[k]
    return pltpu.make_async_remote_copy(src_ref=src, dst_ref=dst, send_sem=send.at[a * 3 + k],
                                        recv_sem=recv.at[a * 3 + k], device_id=p, device_id_type=MESH)


class _Ride:
    def __init__(self, srcs, mode):
        self.srcs, self.mode, self.n = list(srcs), mode, len(srcs)
        n = self.n
        self.in_specs, self.out_specs = [ANY] * n, [ANY] * n
        self.out_shape = [
            jax.ShapeDtypeStruct((N_SHARD,) + s.shape if mode == "gather" else (3,) + s.shape[1:], s.dtype)
            for s in self.srcs]
        self.scratch = [pltpu.SemaphoreType.DMA((3 * n,)), pltpu.SemaphoreType.DMA((3 * n,)),
                        pltpu.SemaphoreType.DMA((n,))]

    def _own(self, ins, lnd, sems):
        if self.mode != "gather":
            return []
        me = 2 * lax.axis_index("x") + lax.axis_index("y")
        return [pltpu.make_async_copy(ins[a], lnd[a].at[me], sems[2].at[a]) for a in range(self.n)]

    def _far(self, ins, lnd, sems, arriving):
        (x, y, c), peers = _place()
        return [_exchange_copy(self.mode, ins, lnd, sems[0], sems[1], a, k, 2 * x + y, peers, arriving)
                for a in range(self.n) for k in range(3)]

    def start(self, ins, lnd, sems):
        for cp in self._own(ins, lnd, sems) + self._far(ins, lnd, sems, False):
            cp.start()

    def finish(self, ins, lnd, sems):
        for cp in self._far(ins, lnd, sems, True):
            cp.wait_recv()
        for cp in self._far(ins, lnd, sems, False):
            cp.wait_send()
        for cp in self._own(ins, lnd, sems):
            cp.wait()


def _gather_two_level(shards, name):
    n = len(shards)

    def body(*refs):
        ins, lnd = refs[:n], refs[n:2 * n]
        send, recv, loc = refs[2 * n:]
        (x, y, c), peers = _place()
        me = 2 * x + y

        def half(ref, a, core):
            rows = shards[a].shape[0] // 2
            return ref.at[pl.ds(core * rows, rows)]

        def copy(a, j, slot, core, to):
            return pltpu.make_async_remote_copy(
                src_ref=half(ins[a], a, core) if j < 3 else half(lnd[a].at[slot], a, core),
                dst_ref=half(lnd[a].at[slot], a, core), send_sem=send.at[6 * a + j], recv_sem=recv.at[6 * a + j],
                device_id=to, device_id_type=MESH)

        own = [pltpu.make_async_copy(ins[a], lnd[a].at[me], loc.at[a]) for a in range(n)]
        far = [copy(a, k, me, c, peers[k]) for a in range(n) for k in range(3)]
        for cp in own + far:
            cp.start()
        passed = []
        for a in range(n):
            for k, p in enumerate(peers):
                theirs = 2 * p[0] + p[1]
                copy(a, k, theirs, c, p).wait_recv()
                passed.append(copy(a, 3 + k, theirs, c, (x, y, 1 - c)))
                passed[-1].start()
        for a in range(n):
            for k, p in enumerate(peers):
                copy(a, 3 + k, 2 * p[0] + p[1], 1 - c, (x, y, 1 - c)).wait_recv()
        for cp in far + passed:
            cp.wait_send()
        for cp in own:
            cp.wait()

    return pl.pallas_call(
        body, name=name, in_specs=[ANY] * n, out_specs=[ANY] * n,
        out_shape=[jax.ShapeDtypeStruct((N_SHARD,) + s.shape, s.dtype) for s in shards],
        scratch_shapes=[pltpu.SemaphoreType.DMA((6 * n,)), pltpu.SemaphoreType.DMA((6 * n,)),
                        pltpu.SemaphoreType.DMA((n,))],
    )(*shards)


def _exchange_packets(packet):
    def body(pk, pk_out, send, recv, loc):
        x, y, c = lax.axis_index("x"), lax.axis_index("y"), lax.axis_index("c")
        lin = 4 * x + 2 * y + c
        own = pltpu.make_async_copy(pk, pk_out.at[lin], loc.at[0])
        own.start()

        def pk_copy(m, slot):
            dev = (x ^ ((m >> 2) & 1), y ^ ((m >> 1) & 1), c ^ (m & 1))
            return pltpu.make_async_remote_copy(
                src_ref=pk, dst_ref=pk_out.at[slot], send_sem=send.at[m - 1], recv_sem=recv.at[m - 1],
                device_id=dev, device_id_type=MESH)

        sent = [pk_copy(m, lin) for m in range(1, N_DEV)]
        for cp in sent:
            cp.start()
        for m in range(1, N_DEV):
            pk_copy(m, lin ^ m).wait_recv()
        for cp in sent:
            cp.wait_send()
        own.wait()

    return pl.pallas_call(
        body, name="exchange_packets", in_specs=[ANY], out_specs=ANY,
        out_shape=jax.ShapeDtypeStruct((N_DEV,) + packet.shape, packet.dtype),
        scratch_shapes=[pltpu.SemaphoreType.DMA((N_DEV - 1,)), pltpu.SemaphoreType.DMA((N_DEV - 1,)),
                        pltpu.SemaphoreType.DMA((1,))],
    )(packet)


def _swap_sibling(parts, name):
    n = len(parts)

    def body(*refs):
        ins, outs = refs[:n], refs[n:2 * n]
        send, recv = refs[2 * n:]
        x, y, c = lax.axis_index("x"), lax.axis_index("y"), lax.axis_index("c")
        cps = [pltpu.make_async_remote_copy(
            src_ref=ins[a], dst_ref=outs[a], send_sem=send.at[a], recv_sem=recv.at[a],
            device_id=(x, y, 1 - c), device_id_type=MESH) for a in range(n)]
        for cp in cps:
            cp.start()
        for cp in cps:
            cp.wait_recv()
        for cp in cps:
            cp.wait_send()

    return pl.pallas_call(
        body, name=name,
        in_specs=[ANY] * n, out_specs=[ANY] * n,
        out_shape=[jax.ShapeDtypeStruct(p.shape, p.dtype) for p in parts],
        scratch_shapes=[pltpu.SemaphoreType.DMA((n,)), pltpu.SemaphoreType.DMA((n,))],
    )(*parts)


BLOCK_ELEMS = 256 * 1024


def _row_tile(R, C):
    tr = max(8, (BLOCK_ELEMS // C) // 8 * 8)
    while R % tr:
        tr -= 8
    return min(tr, R)


def _sum_parts(own, stack, name):
    k = stack.shape[0]
    R, C = stack.shape[1:]
    tr = _row_tile(R, C)

    def body(*refs):
        o_ref = refs[-1]
        acc = refs[0][...]
        for r in refs[1:-1]:
            acc = acc + r[...]
        o_ref[...] = acc

    row = pl.BlockSpec((tr, C), lambda i: (i, 0))
    specs = ([row] if own is not None else []) + [
        pl.BlockSpec((None, tr, C), functools.partial(lambda i, j: (j, i, 0), j=j)) for j in range(k)]
    args = ([own] if own is not None else []) + [stack] * k
    return pl.pallas_call(
        body, name=name, grid=(R // tr,), in_specs=specs, out_specs=row,
        out_shape=jax.ShapeDtypeStruct((R, C), f32), compiler_params=_params(("parallel",)),
    )(*args)


def _adamw(w, m, v, g_parts, name):
    R, C = w.shape
    tr = _row_tile(R, C)
    n_g = len(g_parts)

    def body(w_ref, m_ref, v_ref, *rest):
        g = rest[0][...]
        for r in rest[1:n_g]:
            g = g + r[...]
        g_ref, d_ref, nm_ref, nv_ref = rest[n_g:]
        nm = ADAM_B1 * m_ref[...] + (1.0 - ADAM_B1) * g
        nv = ADAM_B2 * v_ref[...] + (1.0 - ADAM_B2) * jnp.square(g)
        m_hat = nm / (1.0 - ADAM_B1 ** ADAM_STEP)
        v_hat = nv / (1.0 - ADAM_B2 ** ADAM_STEP)
        g_ref[...] = g
        d_ref[...] = -ADAM_LR * (m_hat / (jnp.sqrt(v_hat) + ADAM_EPS) + ADAM_WD * w_ref[...])
        nm_ref[...] = nm
        nv_ref[...] = nv

    row = pl.BlockSpec((tr, C), lambda i: (i, 0))
    return pl.pallas_call(
        body, name=name, grid=(R // tr,), in_specs=[row] * (3 + n_g), out_specs=[row] * 4,
        out_shape=[jax.ShapeDtypeStruct((R, C), f32)] * 4, compiler_params=_params(("parallel",)),
    )(w, m, v, *g_parts)


BIG = ("w_in", "w_mem_kv", "w_sb_out", "w_ssd_out", "w_mem_out", "w_o", "w_up", "w_down")
FIRST = ("w_in", "w_mem_kv")
LATE = ("w_sb_out", "w_ssd_out", "w_mem_out", "w_o", "w_up", "w_down")
REST = BIG[1:]
COL_SHARDED = ("w_in", "w_mem_kv", "w_up")
SMALL = ("norm_mix_pre", "conv_w", "conv_b", "dt_bias", "a_log", "d_skip", "ssd_norm", "norm_mem",
         "norm_mix_post", "norm_mlp_pre", "norm_mlp_post")
WEIGHTS = ("norm_mix_pre", "w_in", "conv_w", "conv_b", "dt_bias", "a_log", "d_skip", "ssd_norm", "norm_mem",
           "w_mem_kv", "w_sb_out", "w_ssd_out", "w_mem_out", "w_o", "norm_mix_post", "norm_mlp_pre", "w_up",
           "w_down", "norm_mlp_post")
PK_ROWS = 184


def _pack(vecs):
    flat = jnp.concatenate([v.reshape(-1) for v in vecs])
    return jnp.pad(flat, (0, PK_ROWS * 128 - flat.shape[0])).reshape(PK_ROWS, 128)


def _unpack(pk, shapes):
    flat = pk.reshape(-1)
    out, off = [], 0
    for s in shapes:
        n = 1
        for d in s:
            n *= d
        out.append(flat[off:off + n].reshape(s))
        off += n
    return out


def _full_from_slabs(name, slabs):
    if name in COL_SHARDED:
        return slabs.transpose(1, 0, 2).reshape(slabs.shape[1], -1)
    return slabs.reshape(-1, slabs.shape[2])


def _slabs_from_full(name, g):
    if name in COL_SHARDED:
        return g.reshape(g.shape[0], N_SHARD, -1).transpose(1, 0, 2)
    return g.reshape(N_SHARD, -1, g.shape[1])


def kernel(x, mem, norm_mix_pre, w_in, conv_w, conv_b, dt_bias, a_log, d_skip, ssd_norm, norm_mem, w_mem_kv, w_sb_out, w_ssd_out, w_mem_out, w_o, norm_mix_post, norm_mlp_pre, w_up, w_down, norm_mlp_post, loss_target, m_norm_mix_pre, m_w_in, m_conv_w, m_conv_b, m_dt_bias, m_a_log, m_d_skip, m_ssd_norm, m_norm_mem, m_w_mem_kv, m_w_sb_out, m_w_ssd_out, m_w_mem_out, m_w_o, m_norm_mix_post, m_norm_mlp_pre, m_w_up, m_w_down, m_norm_mlp_post, v_norm_mix_pre, v_w_in, v_conv_w, v_conv_b, v_dt_bias, v_a_log, v_d_skip, v_ssd_norm, v_norm_mem, v_w_mem_kv, v_w_sb_out, v_w_ssd_out, v_w_mem_out, v_w_o, v_norm_mix_post, v_norm_mlp_pre, v_w_up, v_w_down, v_norm_mlp_post):
    env = dict(locals())
    w = {n: env[n] for n in WEIGHTS}
    mo = {n: env["m_" + n] for n in WEIGHTS}
    vo = {n: env["v_" + n] for n in WEIGHTS}
    shard = 2 * lax.axis_index("x") + lax.axis_index("y")

    first = _gather_two_level([w[n][0].astype(bf16) for n in FIRST], "gather_first")
    w_main, w_dt = _to_internal(_full_from_slabs("w_in", first[0]))
    wts = dict(w_main=w_main, w_dt=w_dt, w_mem_kv=_full_from_slabs("w_mem_kv", first[1]))
    late_rides = (_Ride([w[n][0].astype(bf16) for n in LATE[:4]] + [w["conv_w"][0]], "gather"),
                  _Ride([w[n][0].astype(bf16) for n in LATE[4:]], "gather"))

    def late_weights(lands):
        full = {n: _full_from_slabs(n, s) for n, s in zip(LATE, lands[:4] + lands[5:])}
        return dict(full, conv_w=lands[4].transpose(1, 0, 2).reshape(CONV_K, CONV_DIM))

    def rest_rides(g):
        slabs = [_slabs_from_full(n, g[n]).astype(bf16) for n in REST]
        return _Ride(slabs[5:], "scatter"), _Ride(slabs[:5], "scatter")

    def w_in_ride(g):
        g["w_in"] = _from_internal(g["w_main"], g["w_dt"])
        return _Ride([_slabs_from_full("w_in", g["w_in"]).astype(bf16)], "scatter")

    small = {n: w[n] for n in SMALL if n != "conv_w"}
    loss, grad_x, g = _local_step(x[0], mem[0], loss_target[0], wts, late_rides, late_weights, small,
                                  rest_rides, w_in_ride)

    lands = g["w_in_lands"] + g["rest_lands"]
    packets = _exchange_packets(_pack([g[n] for n in SMALL] + [loss[:, :1]]))
    partial = []
    for n, r in zip(BIG, lands):
        own = lax.dynamic_index_in_dim(_slabs_from_full(n, g[n]), shard, 0, keepdims=False)
        partial.append(_sum_parts(own, r, name="sum_chips_" + n))
    other = _swap_sibling(partial, "swap_sibling")

    out_g, out_d, out_m, out_v = {}, {}, {}, {}
    for n, p, q in zip(BIG, partial, other):
        res = _adamw(w[n][0], mo[n][0], vo[n][0], [p, q], name="adamw_" + n)
        out_g[n], out_d[n], out_m[n], out_v[n] = [r[None] for r in res]
    tot = _sum_parts(None, packets, name="sum_packets")
    shapes = [g[n].shape for n in SMALL] + [(1, 1)]
    sm = dict(zip(SMALL + ("loss",), _unpack(tot, shapes)))
    sm["conv_w"] = lax.dynamic_slice_in_dim(sm["conv_w"], shard * (CONV_DIM // N_SHARD), CONV_DIM // N_SHARD, axis=1)
    own_small = lambda d: _pack([d[n].reshape(sm[n].shape) for n in SMALL])
    res = _adamw(own_small(w), own_small(mo), own_small(vo), [own_small(sm)], name="adamw_small")
    own_shapes = [sm[n].shape for n in SMALL]
    for store, r in zip((out_g, out_d, out_m, out_v), res):
        for n, val in zip(SMALL, _unpack(r, own_shapes)):
            store[n] = val.reshape(w[n].shape)

    outs = [sm["loss"].reshape(()), grad_x[None]]
    for store in (out_g, out_d, out_m, out_v):
        outs += [store[n] for n in WEIGHTS]
    return tuple(outs)
```

```python
import functools

import jax
import jax.numpy as jnp
from jax import lax
from jax.experimental import pallas as pl
from jax.experimental.pallas import tpu as pltpu

f32 = jnp.float32
bf16 = jnp.bfloat16

D = 1024
EPS = 1e-6
SB_HD = 64
SSD_INNER = 2048
SSD_HEADS = 32
SSD_GROUPS = 4
SSD_N = 128
SSD_L = 128
CONV_K = 4
CONV_DIM = 3072
MEM_HEADS = 4
MEM_HD = 256
D_FF = 4096
D_IN = 12320
N_SHARD = 4
N_DEV = 8

P_QKV, P_XBC, P_GATE, P_MEMQ, P_Z, P_DT, P_TOT = 0, 3072, 6144, 9216, 10240, 12288, 12416
R_QKV, R_Z, R_XBC, R_DT, R_MEMQ, R_GATE = (0, 3072), (3072, 5120), (5120, 8192), (8192, 8224), (8224, 9248), (9248, 12320)

ADAM_LR = 0.001
ADAM_B1 = 0.9
ADAM_B2 = 0.999
ADAM_EPS = 1e-08
ADAM_WD = 0.01
ADAM_STEP = 10

VMEM_LIMIT = 56 * 1024 * 1024

NN = (((1,), (0,)), ((), ()))
NT = (((1,), (1,)), ((), ()))
TN = (((0,), (0,)), ((), ()))


def _dot(a, b, dims=NN):
    return lax.dot_general(a, b, dims, preferred_element_type=f32)


def _params(sem=None):
    return pltpu.CompilerParams(dimension_semantics=sem, vmem_limit_bytes=VMEM_LIMIT)


def _sigmoid(x):
    return 1.0 / (1.0 + jnp.exp(-x))


def _split2(x):
    hi = x.astype(bf16)
    lo = (x - hi.astype(f32)).astype(bf16)
    return hi, lo


def _split3(x):
    hi = x.astype(bf16)
    r = x - hi.astype(f32)
    mid = r.astype(bf16)
    lo = (r - mid.astype(f32)).astype(bf16)
    return hi, mid, lo


def _mm(a, b, mode, *, tm, tn, name, out_dtypes=(f32,), epi=None, extras=(), ride=None):
    M = a.shape[1] if mode == "tn" else a.shape[0]
    N = b.shape[0] if mode == "nt" else b.shape[1]
    tm, tn = min(tm, M), min(tn, N)
    if mode == "nn":
        (M, K), N = a.shape, b.shape[1]
        a_spec = pl.BlockSpec((tm, K), lambda i, j: (i, 0))
        b_spec = pl.BlockSpec((K, tn), lambda i, j: (0, j))
        dims = NN
    elif mode == "nt":
        (M, K), N = a.shape, b.shape[0]
        a_spec = pl.BlockSpec((tm, K), lambda i, j: (i, 0))
        b_spec = pl.BlockSpec((tn, K), lambda i, j: (j, 0))
        dims = NT
    else:
        (K, M), N = a.shape, b.shape[1]
        a_spec = pl.BlockSpec((K, tm), lambda i, j: (0, i))
        b_spec = pl.BlockSpec((K, tn), lambda i, j: (0, j))
        dims = TN
    assert M % tm == 0 and N % tn == 0, (name, M, N, tm, tn)
    n_ex, n_out = len(extras), len(out_dtypes)
    n_r = ride.n if ride else 0
    o_spec = pl.BlockSpec((tm, tn), lambda i, j: (i, j))
    grid = (M // tm, N // tn)

    def body(a_ref, b_ref, *rest):
        r_ins = rest[n_ex:n_ex + n_r]
        outs = rest[n_ex + n_r:n_ex + n_r + n_out]
        r_lnd, r_sems = rest[n_ex + n_r + n_out:n_ex + 2 * n_r + n_out], rest[n_ex + 2 * n_r + n_out:]
        i, j = pl.program_id(0), pl.program_id(1)
        if ride:
            pl.when((i == 0) & (j == 0))(lambda: ride.start(r_ins, r_lnd, r_sems))
        acc = _dot(a_ref[...].astype(bf16), b_ref[...].astype(bf16), dims)
        res = (acc,) if epi is None else epi(acc, *[e[...] for e in rest[:n_ex]])
        for o_ref, r in zip(outs, res):
            o_ref[...] = r.astype(o_ref.dtype)
        if ride:
            pl.when((i == grid[0] - 1) & (j == grid[1] - 1))(lambda: ride.finish(r_ins, r_lnd, r_sems))

    out = pl.pallas_call(
        body, name=name, grid=grid,
        in_specs=[a_spec, b_spec] + [o_spec] * n_ex + (ride.in_specs if ride else []),
        out_specs=[o_spec] * n_out + (ride.out_specs if ride else []),
        out_shape=[jax.ShapeDtypeStruct((M, N), dt) for dt in out_dtypes] + (ride.out_shape if ride else []),
        scratch_shapes=ride.scratch if ride else [],
        compiler_params=_params(("arbitrary", "arbitrary") if ride else ("parallel", "parallel")),
    )(a, b, *extras, *(ride.srcs if ride else []))
    if ride:
        return (out[0] if n_out == 1 else out[:n_out]), list(out[n_out:])
    return out[0] if n_out == 1 else out


def _mm_pieces_nt(pieces, b, add, *, tm, tn, name, ride):
    M, N = pieces[0].shape[0], b.shape[0]
    n_p, n_r = len(pieces), (ride.n if ride else 0)
    o_spec = pl.BlockSpec((tm, tn), lambda i, j: (i, j))
    grid = (M // tm, N // tn)

    def body(*refs):
        b_ref, add_ref = refs[n_p:n_p + 2]
        r_ins, o_ref = refs[n_p + 2:n_p + 2 + n_r], refs[n_p + 2 + n_r]
        r_lnd, r_sems = refs[n_p + 3 + n_r:n_p + 3 + 2 * n_r], refs[n_p + 3 + 2 * n_r:]
        i, j = pl.program_id(0), pl.program_id(1)
        if ride:
            pl.when((i == 0) & (j == 0))(lambda: ride.start(r_ins, r_lnd, r_sems))
        acc, off = add_ref[...], 0
        for r in refs[:n_p]:
            acc = acc + _dot(r[...], b_ref[:, off:off + r.shape[1]], NT)
            off += r.shape[1]
        o_ref[...] = acc
        if ride:
            pl.when((i == grid[0] - 1) & (j == grid[1] - 1))(lambda: ride.finish(r_ins, r_lnd, r_sems))

    out = pl.pallas_call(
        body, name=name, grid=grid,
        in_specs=[pl.BlockSpec((tm, p.shape[1]), lambda i, j: (i, 0)) for p in pieces]
        + [pl.BlockSpec((tn, b.shape[1]), lambda i, j: (j, 0)), o_spec] + (ride.in_specs if ride else []),
        out_specs=[o_spec] + (ride.out_specs if ride else []),
        out_shape=[jax.ShapeDtypeStruct((M, N), f32)] + (ride.out_shape if ride else []),
        scratch_shapes=ride.scratch if ride else [],
        compiler_params=_params(("arbitrary", "arbitrary")),
    )(*pieces, b, add, *(ride.srcs if ride else []))
    return out[0], list(out[1:])


def _rms_fwd(x, g, *, name, out_dtype, residual=None, tm=512):
    S, C = x.shape
    tm = min(tm, S)
    has_res = residual is not None

    def body(x_ref, g_ref, *rest):
        xv = x_ref[...]
        r = lax.rsqrt(jnp.mean(xv * xv, axis=1, keepdims=True) + EPS)
        y = xv * r * g_ref[...]
        if has_res:
            y = y + rest[0][...]
        rest[-1][...] = y.astype(out_dtype)

    row = pl.BlockSpec((tm, C), lambda i: (i, 0))
    vec = pl.BlockSpec((1, C), lambda i: (0, 0))
    args = (x, g) + ((residual,) if has_res else ())
    return pl.pallas_call(
        body, name=name, grid=(S // tm,),
        in_specs=[row, vec] + ([row] if has_res else []),
        out_specs=row, out_shape=jax.ShapeDtypeStruct((S, C), out_dtype),
        compiler_params=_params(("parallel",)),
    )(*args)


def _rms_bwd(x, dy, g, *, name, out_dtype, add=None, tm=512):
    S, C = x.shape
    tm = min(tm, S)
    has_add = add is not None

    def body(x_ref, dy_ref, g_ref, *rest):
        dx_ref, dg_ref = rest[-2], rest[-1]
        xv = x_ref[...]
        dyv = dy_ref[...].astype(f32)
        r = lax.rsqrt(jnp.mean(xv * xv, axis=1, keepdims=True) + EPS)
        xh = xv * r
        dxh = dyv * g_ref[...]
        dx = r * (dxh - xh * jnp.mean(dxh * xh, axis=1, keepdims=True))
        if has_add:
            dx = dx + rest[0][...]
        dx_ref[...] = dx.astype(out_dtype)

        @pl.when(pl.program_id(0) == 0)
        def _():
            dg_ref[...] = jnp.zeros_like(dg_ref)

        dg_ref[...] += jnp.sum(dyv * xh, axis=0, keepdims=True)

    row = pl.BlockSpec((tm, C), lambda i: (i, 0))
    vec = pl.BlockSpec((1, C), lambda i: (0, 0))
    args = (x, dy, g) + ((add,) if has_add else ())
    return pl.pallas_call(
        body, name=name, grid=(S // tm,),
        in_specs=[row, row, vec] + ([row] if has_add else []),
        out_specs=[row, vec],
        out_shape=[jax.ShapeDtypeStruct((S, C), out_dtype), jax.ShapeDtypeStruct((1, C), f32)],
        compiler_params=_params(("arbitrary",)),
    )(*args)


SB_T = 128
SB_SPENT = -120.0
SB_TAIL = 3
SB_GROUPS = (4, 2, 1)
SB_GROUPS_BWD = (4, 2, 1)


def _sb_masks():
    lane = lax.broadcasted_iota(jnp.int32, (1, 128), 1)
    m_a = (lane < SB_HD).astype(f32)
    return m_a, 1.0 - m_a


def _chunks(a, n):
    return [a[:, u * SB_T:(u + 1) * SB_T] for u in range(n)]


def _cat(parts, axis):
    return parts[0] if len(parts) == 1 else jnp.concatenate(parts, axis=axis)


def _mask_last(a, n, mask):
    if mask is None:
        return a
    parts = _chunks(a, n)
    return _cat(parts[:-1] + [jnp.where(mask, parts[-1], 0.0)], 1)


def _sb_logits(z, n, mask):
    l1p = jnp.log(1.0 + jnp.exp(-jnp.abs(z)))
    lb = jnp.minimum(z, 0.0) - l1p
    return lb, _mask_last(lb - z, n, mask)


def _by_count(i, most, fn):
    return lax.switch(jnp.minimum(i, most - 1), [functools.partial(fn, n) for n in range(1, most + 1)])


def _chunk_matmul(parts_list, u_mat):
    out = _dot(_cat(parts_list, 0), u_mat)
    return [out[u * SB_T:(u + 1) * SB_T] for u in range(len(parts_list))]


def _chunk_cumsum(lk, n, u_mat):
    hi = lk.astype(bf16)
    lo = (lk - hi.astype(f32)).astype(bf16)
    out = _chunk_matmul(_chunks(hi, n) + _chunks(lo, n), u_mat)
    return [out[u] + out[n + u] for u in range(n)]


def _sb_fwd(proj, S, ride=None):
    nq = S // SB_T
    n_pairs = D // 128
    scale = SB_HD ** -0.5
    n_r = ride.n if ride else 0

    def body(q_ref, k_ref, v_ref, *rest):
        o_ref, t_ref = rest[n_r:n_r + 2]
        i = pl.program_id(1)
        if ride:
            pl.when((pl.program_id(0) == 0) & (i == 0))(
                lambda: ride.start(rest[:n_r], rest[n_r + 2:2 * n_r + 2], rest[2 * n_r + 2:]))
        m_a, m_b = _sb_masks()
        r_i = lax.broadcasted_iota(jnp.int32, (SB_T, SB_T), 0)
        c_i = lax.broadcasted_iota(jnp.int32, (SB_T, SB_T), 1)
        u_mat = (r_i > c_i).astype(bf16)
        causal = c_i < r_i
        q = q_ref[...] * scale
        q_h = ((q * m_a).astype(bf16), (q * m_b).astype(bf16))

        def group(j_lo, n, carry, mask):
            acc, c_a, c_b = carry
            rows = pl.ds(pl.multiple_of(j_lo * SB_T, SB_T), n * SB_T)
            k = k_ref[rows, :].astype(bf16)
            v = v_ref[rows, :]
            zs = [_dot(q_b, k, NT) for q_b in q_h]
            lbk = [_sb_logits(z, n, mask) for z in zs]
            parts = [_chunk_cumsum(lk, n, u_mat) for _, lk in lbk]
            ws, cs = [], []
            for (lb, lk), part, c in zip(lbk, parts, (c_a, c_b)):
                lb_c, lk_c = _chunks(lb, n), _chunks(lk, n)
                w_c = [None] * n
                for u in reversed(range(n)):
                    w_c[u] = jnp.exp(lb_c[u] + c + part[u])
                    c = c + jnp.sum(lk_c[u], axis=1, keepdims=True)
                ws.append(_mask_last(_cat(w_c, 1), n, mask).astype(bf16))
                cs.append(c)
            for w, m in zip(ws, (m_a, m_b)):
                acc = acc + _dot(w, (v * m).astype(bf16))
            return acc, cs[0], cs[1]

        zero_c = jnp.zeros((SB_T, 1), f32)
        init = (jnp.zeros((SB_T, 128), f32), zero_c, zero_c)
        carry = _by_count(i, SB_TAIL, lambda n: group(i - n + 1, n, init, causal))

        def spent(cr):
            return (jnp.max(jnp.maximum(cr[1], cr[2])) < SB_SPENT).astype(jnp.int32)

        state = (i - jnp.minimum(i, SB_TAIL - 1), spent(carry), carry)
        for n in SB_GROUPS:
            def step(st, n=n):
                left, _, cr = st
                cr = group(left - n, n, cr, None)
                return left - n, spent(cr), cr

            state = lax.while_loop(lambda st, n=n: (st[0] >= n) & (st[1] == 0), step, state)
        left, _, carry = state
        o_ref[...] = carry[0]
        lane = lax.broadcasted_iota(jnp.int32, (1, 128), 1)
        t_ref[...] = (jnp.where(lane == 0, carry[1], 0.0) + jnp.where(lane == SB_HD, carry[2], 0.0)
                      + jnp.where(lane == 1, left.astype(f32), 0.0))
        if ride:
            pl.when((pl.program_id(0) == n_pairs - 1) & (i == nq - 1))(
                lambda: ride.finish(rest[:n_r], rest[n_r + 2:2 * n_r + 2], rest[2 * n_r + 2:]))

    qs = pl.BlockSpec((SB_T, 128), lambda h, i: (i, h))
    out = pl.pallas_call(
        body, name="sb_fwd", grid=(n_pairs, nq),
        in_specs=[qs,
                  pl.BlockSpec((S, 128), lambda h, i: (0, n_pairs + h)),
                  pl.BlockSpec((S, 128), lambda h, i: (0, 2 * n_pairs + h))] + (ride.in_specs if ride else []),
        out_specs=[qs, qs] + (ride.out_specs if ride else []),
        out_shape=[jax.ShapeDtypeStruct((S, D), f32)] * 2 + (ride.out_shape if ride else []),
        scratch_shapes=ride.scratch if ride else [],
        compiler_params=_params(("arbitrary", "arbitrary")),
    )(proj, proj, proj, *(ride.srcs if ride else []))
    return out[0], out[1], list(out[2:])


def _sb_bwd(proj, tot_lk, do, S, ride=None):
    nq = S // SB_T
    n_pairs = D // 128
    scale = SB_HD ** -0.5
    n_r = ride.n if ride else 0

    def body(q_ref, k_ref, v_ref, t_ref, do_ref, *rest):
        dq_ref, dk_ref, dv_ref = rest[n_r:n_r + 3]
        dk_acc, dv_acc = rest[2 * n_r + 3:2 * n_r + 5]
        r_ins, r_lnd, r_sems = rest[:n_r], rest[n_r + 3:2 * n_r + 3], rest[2 * n_r + 5:]
        i = pl.program_id(1)
        if ride:
            pl.when((pl.program_id(0) == 0) & (i == 0))(lambda: ride.start(r_ins, r_lnd, r_sems))
        m_a, m_b = _sb_masks()
        r_i = lax.broadcasted_iota(jnp.int32, (SB_T, SB_T), 0)
        c_i = lax.broadcasted_iota(jnp.int32, (SB_T, SB_T), 1)
        u_inc = (r_i <= c_i).astype(bf16)
        u_exc = (r_i < c_i).astype(bf16)
        causal = c_i < r_i

        @pl.when(i == 0)
        def _():
            dk_acc[...] = jnp.zeros_like(dk_acc)
            dv_acc[...] = jnp.zeros_like(dv_acc)

        q = q_ref[...] * scale
        dov = do_ref[...]
        tv = t_ref[...]
        lane = lax.broadcasted_iota(jnp.int32, (1, 128), 1)
        heads = []
        for m, first in ((m_a, 0), (m_b, SB_HD)):
            tot = jnp.sum(jnp.where(lane == first, tv, 0.0), axis=1, keepdims=True)
            heads.append(((q * m).astype(bf16), (dov * m).astype(bf16), tot, m))
        lowest = jnp.clip(jnp.max(jnp.where(lane == 1, tv, 0.0)).astype(jnp.int32), 0, i)

        def group(j_lo, n, carry, mask):
            dq_acc, cp_a, cp_b, ce_a, ce_b = carry
            rows = pl.ds(pl.multiple_of(j_lo * SB_T, SB_T), n * SB_T)
            k_f = k_ref[rows, :]
            k = k_f.astype(bf16)
            v = v_ref[rows, :].astype(bf16)
            zs = [_dot(h[0], k, NT) for h in heads]
            dws = [_dot(h[1], v, NT) for h in heads]
            lbk = [_sb_logits(z, n, mask) for z in zs]
            parts = [_chunk_cumsum(lk, n, u_inc) for _, lk in lbk]
            ws, es, cps = [], [], []
            for (lb, lk), part, dw, h, cp in zip(lbk, parts, dws, heads, (cp_a, cp_b)):
                lb_c, lk_c = _chunks(lb, n), _chunks(lk, n)
                w_c = []
                for u in range(n):
                    w_c.append(jnp.exp(lb_c[u] + (h[2] - cp) - part[u]))
                    cp = cp + jnp.sum(lk_c[u], axis=1, keepdims=True)
                w = _mask_last(_cat(w_c, 1), n, mask)
                ws.append(w)
                es.append(dw * w)
                cps.append(cp)
            e_parts = [_chunk_matmul(_chunks(e.astype(bf16), n), u_exc) for e in es]
            dzs, ces = [], []
            for (lb, _), e, e_part, ce in zip(lbk, es, e_parts, (ce_a, ce_b)):
                e_c = _chunks(e, n)
                big_c = []
                for u in range(n):
                    big_c.append(ce + e_part[u])
                    ce = ce + jnp.sum(e_c[u], axis=1, keepdims=True)
                sig = jnp.exp(lb)
                dz = _mask_last(e * (1.0 - sig) - _cat(big_c, 1) * sig, n, mask)
                dzs.append(dz.astype(bf16))
                ces.append(ce)
            dk_t = jnp.zeros((n * SB_T, 128), f32)
            dv_t = jnp.zeros((n * SB_T, 128), f32)
            for dz_b, w, h in zip(dzs, ws, heads):
                dq_acc = dq_acc + _dot(dz_b, (k_f * h[3]).astype(bf16))
                dk_t = dk_t + _dot(dz_b, h[0], TN)
                dv_t = dv_t + _dot(w.astype(bf16), h[1], TN)
            dk_acc[rows, :] += dk_t
            dv_acc[rows, :] += dv_t
            return dq_acc, cps[0], cps[1], ces[0], ces[1]

        zc = jnp.zeros((SB_T, 1), f32)
        carry = (jnp.zeros((SB_T, 128), f32), zc, zc, zc, zc)
        done = lowest
        tail_lo = i - jnp.minimum(i, SB_TAIL - 1)
        for n in SB_GROUPS_BWD:
            trips = (tail_lo - done) // n
            carry = lax.fori_loop(
                0, trips, functools.partial(lambda gi, cr, n, done: group(done + gi * n, n, cr, None), n=n, done=done),
                carry)
            done = done + trips * n
        carry = _by_count(i, SB_TAIL, lambda n: group(i - n + 1, n, carry, causal))
        dq_ref[...] = (carry[0] * scale).astype(bf16)

        @pl.when(i == nq - 1)
        def _():
            dk_ref[...] = dk_acc[...].astype(bf16)
            dv_ref[...] = dv_acc[...].astype(bf16)

        if ride:
            pl.when((pl.program_id(0) == n_pairs - 1) & (i == nq - 1))(
                lambda: ride.finish(r_ins, r_lnd, r_sems))

    qs = pl.BlockSpec((SB_T, 128), lambda h, i: (i, h))
    full = pl.BlockSpec((S, 128), lambda h, i: (0, h))
    out = pl.pallas_call(
        body, name="sb_bwd", grid=(n_pairs, nq),
        in_specs=[qs,
                  pl.BlockSpec((S, 128), lambda h, i: (0, n_pairs + h)),
                  pl.BlockSpec((S, 128), lambda h, i: (0, 2 * n_pairs + h)),
                  qs, qs] + (ride.in_specs if ride else []),
        out_specs=[qs, full, full] + (ride.out_specs if ride else []),
        out_shape=[jax.ShapeDtypeStruct((S, D), bf16)] * 3 + (ride.out_shape if ride else []),
        scratch_shapes=[pltpu.VMEM((S, 128), f32), pltpu.VMEM((S, 128), f32)] + (ride.scratch if ride else []),
        compiler_params=_params(("arbitrary", "arbitrary")),
    )(proj, proj, proj, tot_lk, do, *(ride.srcs if ride else []))
    return out[0], out[1], out[2], list(out[3:])


CONV_CB = 256
HALO = 8


def _conv_fwd(proj, conv_w, conv_b, S):
    tr = min(512, S)

    def body(x_ref, w_ref, b_ref, xc_ref, xbc_ref):
        w = w_ref[...]
        for t in range(S // tr):
            cur = x_ref[t * tr:(t + 1) * tr, :]
            halo = x_ref[t * tr - HALO:t * tr, :] if t else jnp.zeros((HALO, CONV_CB), f32)
            win = jnp.concatenate([halo, cur], axis=0)
            acc = b_ref[...] + w[CONV_K - 1:CONV_K, :] * cur
            for k in range(CONV_K - 1):
                acc = acc + w[k:k + 1, :] * pltpu.roll(win, CONV_K - 1 - k, 0)[HALO:, :]
            xc_ref[t * tr:(t + 1) * tr, :] = acc
            xbc_ref[t * tr:(t + 1) * tr, :] = acc * _sigmoid(acc)

    col = pl.BlockSpec((S, CONV_CB), lambda c: (0, c))
    return pl.pallas_call(
        body, name="conv_fwd", grid=(CONV_DIM // CONV_CB,),
        in_specs=[pl.BlockSpec((S, CONV_CB), lambda c: (0, P_XBC // CONV_CB + c)),
                  pl.BlockSpec((CONV_K, CONV_CB), lambda c: (0, c)),
                  pl.BlockSpec((1, CONV_CB), lambda c: (0, c))],
        out_specs=[col, col], out_shape=[jax.ShapeDtypeStruct((S, CONV_DIM), f32)] * 2,
        compiler_params=_params(("parallel",)),
    )(proj, conv_w, conv_b)


def _conv_bwd(proj, xc, dxbc, conv_w, S):
    tr = min(512, S)

    def body(x_ref, xc_ref, dy_ref, w_ref, dx_ref, dw_ref, db_ref, dxc_s):
        w = w_ref[...]
        xcv = xc_ref[...]
        sg = _sigmoid(xcv)
        dxc_s[0:S, :] = dy_ref[...] * (sg * (1.0 + xcv * (1.0 - sg)))
        dxc_s[S:S + HALO, :] = jnp.zeros((HALO, CONV_CB), f32)
        dws = [jnp.zeros((1, CONV_CB), f32) for _ in range(CONV_K)]
        db = jnp.zeros((1, CONV_CB), f32)
        for t in range(S // tr):
            cur = x_ref[t * tr:(t + 1) * tr, :]
            halo = x_ref[t * tr - HALO:t * tr, :] if t else jnp.zeros((HALO, CONV_CB), f32)
            win = jnp.concatenate([halo, cur], axis=0)
            dwin = dxc_s[t * tr:(t + 1) * tr + HALO, :]
            dcur = dwin[0:tr, :]
            db = db + jnp.sum(dcur, axis=0, keepdims=True)
            dws[CONV_K - 1] = dws[CONV_K - 1] + jnp.sum(dcur * cur, axis=0, keepdims=True)
            dx = w[CONV_K - 1:CONV_K, :] * dcur
            for k in range(CONV_K - 1):
                sh = CONV_K - 1 - k
                dws[k] = dws[k] + jnp.sum(dcur * pltpu.roll(win, sh, 0)[HALO:, :], axis=0, keepdims=True)
                dx = dx + w[k:k + 1, :] * pltpu.roll(dwin, tr + HALO - sh, 0)[0:tr, :]
            dx_ref[t * tr:(t + 1) * tr, :] = dx.astype(bf16)
        dw_ref[...] = jnp.concatenate(dws + [jnp.zeros((8 - CONV_K, CONV_CB), f32)], axis=0)
        db_ref[...] = db

    col = pl.BlockSpec((S, CONV_CB), lambda c: (0, c))
    return pl.pallas_call(
        body, name="conv_bwd", grid=(CONV_DIM // CONV_CB,),
        in_specs=[pl.BlockSpec((S, CONV_CB), lambda c: (0, P_XBC // CONV_CB + c)), col, col,
                  pl.BlockSpec((CONV_K, CONV_CB), lambda c: (0, c))],
        out_specs=[col, pl.BlockSpec((8, CONV_CB), lambda c: (0, c)), pl.BlockSpec((1, CONV_CB), lambda c: (0, c))],
        out_shape=[jax.ShapeDtypeStruct((S, CONV_DIM), bf16), jax.ShapeDtypeStruct((8, CONV_DIM), f32),
                   jax.ShapeDtypeStruct((1, CONV_DIM), f32)],
        scratch_shapes=[pltpu.VMEM((S + HALO, CONV_CB), f32)],
        compiler_params=_params(("parallel",)),
    )(proj, xc, dxbc, conv_w)


N_PAIR = SSD_HEADS // 2
NEG = -1e30


def _softplus(x):
    return jnp.maximum(x, 0.0) + jnp.log(1.0 + jnp.exp(-jnp.abs(x)))


def _ssd_common(dtr, dtb, alog):
    L = SSD_L
    r_i = lax.broadcasted_iota(jnp.int32, (L, L), 0)
    c_i = lax.broadcasted_iota(jnp.int32, (L, L), 1)
    dt = _softplus(dtr + dtb)
    a = -jnp.exp(alog)
    da = dt * a
    lower = (r_i >= c_i).astype(bf16)
    upper = (r_i <= c_i).astype(bf16)
    parts = _split3(da)
    a_cs = sum(_dot(lower, p) for p in parts)
    a_cs_t = sum(_dot(p, upper, TN) for p in parts)
    return dt, a, a_cs, a_cs_t, r_i >= c_i


def _pair_vec(lane, v, h):
    return jnp.where(lane < SB_HD, v[:, h:h + 1], v[:, h + 1:h + 2])


def _decay_mat(a_cs, a_cs_t, h, tril):
    return jnp.exp(jnp.where(tril, a_cs[:, h:h + 1] - a_cs_t[h:h + 1, :], NEG))


def _ssd_fwd(xbc, proj, pdt, dt_bias_p, a_log_p, d_skip_c, ssd_norm, S):
    L = SSD_L
    nc = S // L

    def body(xbc_ref, dt_ref, z_ref, dtb_ref, alog_ref, dsk_ref, gn_ref, y_ref, yn_ref, hp_ref, state):
        c = pl.program_id(0)

        @pl.when(c == 0)
        def _():
            state[...] = jnp.zeros_like(state)

        hp_ref[0] = state[...]
        lane = lax.broadcasted_iota(jnp.int32, (1, 128), 1)
        row128 = lax.broadcasted_iota(jnp.int32, (128, 1), 0)
        m_a, m_b = _sb_masks()
        dt, a, a_cs, a_cs_t, tril = _ssd_common(dt_ref[...], dtb_ref[...], alog_ref[...])
        a_last = a_cs[L - 1:L, :]
        for g in range(SSD_GROUPS):
            b_g = xbc_ref[:, SSD_INNER + g * SSD_N:SSD_INNER + (g + 1) * SSD_N].astype(bf16)
            c_g = xbc_ref[:, SSD_INNER + (SSD_GROUPS + g) * SSD_N:SSD_INNER + (SSD_GROUPS + g + 1) * SSD_N].astype(bf16)
            cb = _dot(c_g, b_g, NT)
            for pr in range(4):
                h = 8 * g + 2 * pr
                pi = h // 2
                cols = slice(pi * 128, (pi + 1) * 128)
                xs = xbc_ref[:, cols]
                x = xs * _pair_vec(lane, dt, h)
                acs = _pair_vec(lane, a_cs, h)
                al = _pair_vec(lane, a_last, h)
                w_a = (cb * _decay_mat(a_cs, a_cs_t, h, tril)).astype(bf16)
                w_b = (cb * _decay_mat(a_cs, a_cs_t, h + 1, tril)).astype(bf16)
                yd = _dot(w_a, (x * m_a).astype(bf16)) + _dot(w_b, (x * m_b).astype(bf16))
                hp = state[pi]
                yo = _dot(c_g, hp.astype(bf16), NT) * jnp.exp(acs)
                y_ref[:, cols] = yd + yo + dsk_ref[:, cols] * xs
                dec = jnp.exp(jnp.where(row128 < SB_HD, a_last[:, h:h + 1], a_last[:, h + 1:h + 2]))
                state[pi] = hp * dec + _dot((x * jnp.exp(al - acs)).astype(bf16), b_g, TN)
        zz = z_ref[...]
        y2 = y_ref[...] * (zz * _sigmoid(zz))
        gw = SSD_INNER // SSD_GROUPS
        for g in range(SSD_GROUPS):
            yg = y2[:, g * gw:(g + 1) * gw]
            rg = lax.rsqrt(jnp.mean(yg * yg, axis=1, keepdims=True) + EPS)
            yn_ref[:, g * gw:(g + 1) * gw] = (yg * rg * gn_ref[:, g * gw:(g + 1) * gw]).astype(bf16)

    vec128 = pl.BlockSpec((1, 128), lambda c: (0, 0))
    vecin = pl.BlockSpec((1, SSD_INNER), lambda c: (0, 0))
    rows = pl.BlockSpec((L, SSD_INNER), lambda c: (c, 0))
    return pl.pallas_call(
        body, name="ssd_fwd", grid=(nc,),
        in_specs=[pl.BlockSpec((L, CONV_DIM), lambda c: (c, 0)),
                  pl.BlockSpec((L, 128), lambda c: (c, 0)),
                  pl.BlockSpec((L, SSD_INNER), lambda c: (c, P_Z // SSD_INNER)),
                  vec128, vec128, vecin, vecin],
        out_specs=[rows, rows, pl.BlockSpec((1, N_PAIR, 128, SSD_N), lambda c: (c, 0, 0, 0))],
        out_shape=[jax.ShapeDtypeStruct((S, SSD_INNER), f32), jax.ShapeDtypeStruct((S, SSD_INNER), bf16),
                   jax.ShapeDtypeStruct((nc, N_PAIR, 128, SSD_N), f32)],
        scratch_shapes=[pltpu.VMEM((N_PAIR, 128, SSD_N), f32)],
        compiler_params=_params(("arbitrary",)),
    )(xbc, pdt, proj, dt_bias_p, a_log_p, d_skip_c, ssd_norm)


def _sum_all(v):
    return jnp.sum(jnp.sum(v, axis=1, keepdims=True), axis=0, keepdims=True)


def _ssd_bwd(dyn, y, xbc, proj, pdt, hprev, dt_bias_p, a_log_p, d_skip_c, ssd_norm, S, ride=None):
    L = SSD_L
    nc = S // L
    n_r = ride.n if ride else 0

    col = lax.broadcasted_iota(jnp.int32, (2 * SSD_INNER, 128), 0)
    head = lax.broadcasted_iota(jnp.int32, (2 * SSD_INNER, 128), 1)
    sel_pair = (col[:SSD_INNER] // SB_HD == head[:SSD_INNER]).astype(bf16)
    sel_head = (col // 128 == head).astype(bf16)

    def body(*refs):
        (dyn_ref, y_ref, xbc_ref, dt_ref, z_ref, hp_ref, dtb_ref, alog_ref, dsk_ref, gn_ref,
         selp_ref, selh_ref) = refs[:12]
        dz_ref, dxbc_ref, ddt_ref, dgn_ref, dsk_out, dalog_ref, ddtb_ref = refs[12 + n_r:19 + n_r]
        dstate, dy_s, st_a, st_q, st_d, st_x, dat = refs[19 + 2 * n_r:26 + 2 * n_r]
        r_ins, r_lnd, r_sems = refs[12:12 + n_r], refs[19 + n_r:19 + 2 * n_r], refs[26 + 2 * n_r:]
        c = pl.program_id(0)
        if ride:
            pl.when(c == 0)(lambda: ride.start(r_ins, r_lnd, r_sems))

        @pl.when(c == 0)
        def _():
            dat[...] = jnp.zeros_like(dat)
            dstate[...] = jnp.zeros_like(dstate)
            dgn_ref[...] = jnp.zeros_like(dgn_ref)
            dsk_out[...] = jnp.zeros_like(dsk_out)
            dalog_ref[...] = jnp.zeros_like(dalog_ref)
            ddtb_ref[...] = jnp.zeros_like(ddtb_ref)

        lane = lax.broadcasted_iota(jnp.int32, (1, 128), 1)
        row128 = lax.broadcasted_iota(jnp.int32, (128, 1), 0)
        rowl = lax.broadcasted_iota(jnp.int32, (L, 1), 0)
        m_a, m_b = _sb_masks()
        dtr = dt_ref[...]
        dt, a, a_cs, a_cs_t, tril = _ssd_common(dtr, dtb_ref[...], alog_ref[...])
        a_last = a_cs[L - 1:L, :]

        zz = z_ref[...]
        sg = _sigmoid(zz)
        silu = zz * sg
        yv = y_ref[...]
        y2 = yv * silu
        gw = SSD_INNER // SSD_GROUPS
        for g in range(SSD_GROUPS):
            sl = slice(g * gw, (g + 1) * gw)
            yg = y2[:, sl]
            rg = lax.rsqrt(jnp.mean(yg * yg, axis=1, keepdims=True) + EPS)
            yh = yg * rg
            dyn_g = dyn_ref[:, sl]
            dgn_ref[:, sl] += jnp.sum(dyn_g * yh, axis=0, keepdims=True)
            dyh = dyn_g * gn_ref[:, sl]
            dy2 = rg * (dyh - yh * jnp.mean(dyh * yh, axis=1, keepdims=True))
            dy_s[:, sl] = dy2 * silu[:, sl]
            dz_ref[:, sl] = (dy2 * yv[:, sl] * (sg[:, sl] * (1.0 + zz[:, sl] * (1.0 - sg[:, sl])))).astype(bf16)

        last_row = jnp.zeros((1, 128), f32)
        dsk_acc = jnp.zeros((1, 128), f32)
        for g in range(SSD_GROUPS):
            bsl = slice(SSD_INNER + g * SSD_N, SSD_INNER + (g + 1) * SSD_N)
            csl = slice(SSD_INNER + (SSD_GROUPS + g) * SSD_N, SSD_INNER + (SSD_GROUPS + g + 1) * SSD_N)
            b_g = xbc_ref[:, bsl].astype(bf16)
            c_g = xbc_ref[:, csl].astype(bf16)
            cb = _dot(c_g, b_g, NT)
            dcb = jnp.zeros((L, L), f32)
            dc_g = jnp.zeros((L, SSD_N), f32)
            db_g = jnp.zeros((L, SSD_N), f32)
            for pr in range(4):
                h = 8 * g + 2 * pr
                pi = h // 2
                cols = slice(pi * 128, (pi + 1) * 128)
                xs = xbc_ref[:, cols]
                dt_p = _pair_vec(lane, dt, h)
                x = xs * dt_p
                acs = _pair_vec(lane, a_cs, h)
                al = _pair_vec(lane, a_last, h)
                e_a = jnp.exp(acs)
                dte = jnp.exp(al - acs)
                m_mat_a = _decay_mat(a_cs, a_cs_t, h, tril)
                m_mat_b = _decay_mat(a_cs, a_cs_t, h + 1, tril)
                dyp = dy_s[:, cols]
                dsk = dsk_ref[:, cols]
                d_hn = dstate[pi]
                hp = hp_ref[0, pi]
                dy_a = (dyp * m_a).astype(bf16)
                dy_b = (dyp * m_b).astype(bf16)
                x_b = x.astype(bf16)
                gm_a = _dot(dy_a, x_b, NT) * m_mat_a
                gm_b = _dot(dy_b, x_b, NT) * m_mat_b
                dcb = dcb + gm_a + gm_b
                dx_d = _dot((cb * m_mat_a).astype(bf16), dy_a, TN) + _dot((cb * m_mat_b).astype(bf16), dy_b, TN)
                dx_s = _dot(b_g, d_hn.astype(bf16), NT) * dte
                dx = dx_d + dx_s
                dxbc_ref[:, cols] = dx * dt_p + dsk * dyp
                xdxs = x * dx_s
                st_x[:, cols] = xdxs
                st_a[:, cols] = dyp * (_dot(c_g, hp.astype(bf16), NT) * e_a) - xdxs
                st_d[:, cols] = dx * xs
                hh = d_hn * hp
                dsk_row = jnp.sum(dyp * xs, axis=0, keepdims=True)
                dec = jnp.exp(jnp.where(row128 < SB_HD, a_last[:, h:h + 1], a_last[:, h + 1:h + 2]))
                for hd, m, gm in ((h, m_a, gm_a), (h + 1, m_b, gm_b)):
                    half = slice(0, SB_HD) if hd == h else slice(SB_HD, 128)
                    qm = gm * cb
                    st_q[:, hd * 128:(hd + 1) * 128] = qm
                    dat[hd:hd + 1, :] = jnp.sum(qm, axis=0, keepdims=True)
                    hh_sum = jnp.sum(jnp.sum(hh[half, :], axis=0, keepdims=True), axis=1, keepdims=True)
                    last_row = jnp.where(lane == hd, jnp.exp(a_last[:, hd:hd + 1]) * hh_sum, last_row)
                    dsk_acc = jnp.where(lane == hd, jnp.sum(dsk_row * m, axis=1, keepdims=True), dsk_acc)
                dye = (dyp * e_a).astype(bf16)
                dc_g = dc_g + _dot(dye, hp.astype(bf16))
                db_g = db_g + _dot((x * dte).astype(bf16), d_hn.astype(bf16))
                dstate[pi] = dec * d_hn + _dot(dye, c_g, TN)
            dcb_b = dcb.astype(bf16)
            dxbc_ref[:, csl] = dc_g + _dot(dcb_b, b_g)
            dxbc_ref[:, bsl] = db_g + _dot(dcb_b, c_g, TN)

        r_i = lax.broadcasted_iota(jnp.int32, (L, L), 0)
        c_i = lax.broadcasted_iota(jnp.int32, (L, L), 1)
        rev = (r_i <= c_i).astype(bf16)

        def head_sums(st, sel, split=_split2):
            return sum(_dot(p, sel[...]) for p in split(st[...]))

        last_row = last_row + jnp.sum(head_sums(st_x, selp_ref), axis=0, keepdims=True)
        d_acs = (head_sums(st_a, selp_ref) + head_sums(st_q, selh_ref, _split3)
                 + jnp.where(rowl == L - 1, last_row, 0.0))
        ddt_x = head_sums(st_d, selp_ref)
        dda = sum(_dot(rev, p) for p in _split3(d_acs)) - sum(_dot(rev, p, NT) for p in _split3(dat[...]))
        ddt = ddt_x + dda * a
        dalog_ref[...] += jnp.sum(dda * dt, axis=0, keepdims=True) * a
        ddtr = jnp.where(lane < SSD_HEADS, ddt * _sigmoid(dtr + dtb_ref[...]), 0.0)
        ddt_ref[...] = ddtr.astype(bf16)
        ddtb_ref[...] += jnp.sum(ddtr, axis=0, keepdims=True)
        dsk_out[...] += dsk_acc
        if ride:
            pl.when(c == nc - 1)(lambda: ride.finish(r_ins, r_lnd, r_sems))

    rv = lambda c: nc - 1 - c
    vec128 = pl.BlockSpec((1, 128), lambda c: (0, 0))
    vecin = pl.BlockSpec((1, SSD_INNER), lambda c: (0, 0))
    rows = pl.BlockSpec((L, SSD_INNER), lambda c: (rv(c), 0))
    return pl.pallas_call(
        body, name="ssd_bwd", grid=(nc,),
        in_specs=[rows, rows,
                  pl.BlockSpec((L, CONV_DIM), lambda c: (rv(c), 0)),
                  pl.BlockSpec((L, 128), lambda c: (rv(c), 0)),
                  pl.BlockSpec((L, SSD_INNER), lambda c: (rv(c), P_Z // SSD_INNER)),
                  pl.BlockSpec((1, N_PAIR, 128, SSD_N), lambda c: (rv(c), 0, 0, 0)),
                  vec128, vec128, vecin, vecin,
                  pl.BlockSpec((SSD_INNER, 128), lambda c: (0, 0)),
                  pl.BlockSpec((2 * SSD_INNER, 128), lambda c: (0, 0))] + (ride.in_specs if ride else []),
        out_specs=[rows, pl.BlockSpec((L, CONV_DIM), lambda c: (rv(c), 0)),
                   pl.BlockSpec((L, 128), lambda c: (rv(c), 0)), vecin, vec128, vec128, vec128]
        + (ride.out_specs if ride else []),
        out_shape=[jax.ShapeDtypeStruct((S, SSD_INNER), bf16), jax.ShapeDtypeStruct((S, CONV_DIM), f32),
                   jax.ShapeDtypeStruct((S, 128), bf16), jax.ShapeDtypeStruct((1, SSD_INNER), f32),
                   jax.ShapeDtypeStruct((1, 128), f32), jax.ShapeDtypeStruct((1, 128), f32),
                   jax.ShapeDtypeStruct((1, 128), f32)] + (ride.out_shape if ride else []),
        scratch_shapes=[pltpu.VMEM((N_PAIR, 128, SSD_N), f32), pltpu.VMEM((L, SSD_INNER), f32),
                        pltpu.VMEM((L, SSD_INNER), f32), pltpu.VMEM((L, 2 * SSD_INNER), f32),
                        pltpu.VMEM((L, SSD_INNER), f32), pltpu.VMEM((L, SSD_INNER), f32),
                        pltpu.VMEM((128, L), f32)]
        + (ride.scratch if ride else []),
        compiler_params=_params(("arbitrary",)),
    )(dyn, y, xbc, pdt, proj, hprev, dt_bias_p, a_log_p, d_skip_c, ssd_norm, sel_pair, sel_head,
      *(ride.srcs if ride else []))


MEM_W = MEM_HEADS * MEM_HD


def _mem_probs(q, k):
    s = _dot(q, k, NT) * (MEM_HD ** -0.5)
    s = s - jnp.max(s, axis=1, keepdims=True)
    p = jnp.exp(s)
    return p / jnp.sum(p, axis=1, keepdims=True)


def _mem_fwd(proj, kv, S, tm=512):
    tm = min(tm, S)
    M = kv.shape[0]

    def body(q_ref, kv_ref, o_ref):
        for h in range(MEM_HEADS):
            sl = slice(h * MEM_HD, (h + 1) * MEM_HD)
            vsl = slice(MEM_W + h * MEM_HD, MEM_W + (h + 1) * MEM_HD)
            p = _mem_probs(q_ref[:, sl].astype(bf16), kv_ref[:, sl].astype(bf16))
            o_ref[:, sl] = _dot(p.astype(bf16), kv_ref[:, vsl].astype(bf16)).astype(bf16)

    return pl.pallas_call(
        body, name="mem_fwd", grid=(S // tm,),
        in_specs=[pl.BlockSpec((tm, MEM_W), lambda i: (i, P_MEMQ // MEM_W)),
                  pl.BlockSpec((M, 2 * MEM_W), lambda i: (0, 0))],
        out_specs=pl.BlockSpec((tm, MEM_W), lambda i: (i, 0)),
        out_shape=jax.ShapeDtypeStruct((S, MEM_W), bf16),
        compiler_params=_params(("parallel",)),
    )(proj, kv)


def _mem_bwd(proj, kv, dy, S, tm=512):
    tm = min(tm, S)
    M = kv.shape[0]
    scale = MEM_HD ** -0.5

    def body(q_ref, kv_ref, dy_ref, dq_ref, dkv_ref):
        @pl.when(pl.program_id(0) == 0)
        def _():
            dkv_ref[...] = jnp.zeros_like(dkv_ref)

        for h in range(MEM_HEADS):
            sl = slice(h * MEM_HD, (h + 1) * MEM_HD)
            vsl = slice(MEM_W + h * MEM_HD, MEM_W + (h + 1) * MEM_HD)
            q = q_ref[:, sl].astype(bf16)
            k = kv_ref[:, sl].astype(bf16)
            v = kv_ref[:, vsl].astype(bf16)
            dyh = dy_ref[:, sl].astype(bf16)
            p = _mem_probs(q, k)
            dp = _dot(dyh, v, NT)
            ds = (p * (dp - jnp.sum(dp * p, axis=1, keepdims=True)) * scale).astype(bf16)
            dq_ref[:, sl] = _dot(ds, k).astype(bf16)
            dkv_ref[:, sl] += _dot(ds, q, TN)
            dkv_ref[:, vsl] += _dot(p.astype(bf16), dyh, TN)

    return pl.pallas_call(
        body, name="mem_bwd", grid=(S // tm,),
        in_specs=[pl.BlockSpec((tm, MEM_W), lambda i: (i, P_MEMQ // MEM_W)),
                  pl.BlockSpec((M, 2 * MEM_W), lambda i: (0, 0)),
                  pl.BlockSpec((tm, MEM_W), lambda i: (i, 0))],
        out_specs=[pl.BlockSpec((tm, MEM_W), lambda i: (i, 0)), pl.BlockSpec((M, 2 * MEM_W), lambda i: (0, 0))],
        out_shape=[jax.ShapeDtypeStruct((S, MEM_W), bf16), jax.ShapeDtypeStruct((M, 2 * MEM_W), f32)],
        compiler_params=_params(("arbitrary",)),
    )(proj, kv, dy)


def _merge_fwd(proj, t0, t1, t2, S, tm=512):
    tm = min(tm, S)

    def body(g_ref, t0_ref, t1_ref, t2_ref, o_ref):
        acc = jnp.zeros((tm, D), f32)
        for b, t_ref in enumerate((t0_ref, t1_ref, t2_ref)):
            acc = acc + _sigmoid(g_ref[:, b * D:(b + 1) * D]) * t_ref[...]
        o_ref[...] = acc.astype(bf16)

    row = pl.BlockSpec((tm, D), lambda i: (i, 0))
    return pl.pallas_call(
        body, name="merge_fwd", grid=(S // tm,),
        in_specs=[pl.BlockSpec((tm, 3 * D), lambda i: (i, P_GATE // (3 * D))), row, row, row],
        out_specs=row, out_shape=jax.ShapeDtypeStruct((S, D), bf16),
        compiler_params=_params(("parallel",)),
    )(proj, t0, t1, t2)


def _merge_bwd(proj, t0, t1, t2, dm, S, tm=512):
    tm = min(tm, S)

    def body(g_ref, t0_ref, t1_ref, t2_ref, dm_ref, d0_ref, d1_ref, d2_ref, dg_ref):
        dmv = dm_ref[...]
        for b, (t_ref, d_ref) in enumerate(((t0_ref, d0_ref), (t1_ref, d1_ref), (t2_ref, d2_ref))):
            sg = _sigmoid(g_ref[:, b * D:(b + 1) * D])
            d_ref[...] = (dmv * sg).astype(bf16)
            dg_ref[:, b * D:(b + 1) * D] = (dmv * t_ref[...] * sg * (1.0 - sg)).astype(bf16)

    row = pl.BlockSpec((tm, D), lambda i: (i, 0))
    return pl.pallas_call(
        body, name="merge_bwd", grid=(S // tm,),
        in_specs=[pl.BlockSpec((tm, 3 * D), lambda i: (i, P_GATE // (3 * D))), row, row, row, row],
        out_specs=[row, row, row, pl.BlockSpec((tm, 3 * D), lambda i: (i, 0))],
        out_shape=[jax.ShapeDtypeStruct((S, D), bf16)] * 3 + [jax.ShapeDtypeStruct((S, 3 * D), bf16)],
        compiler_params=_params(("parallel",)),
    )(proj, t0, t1, t2, dm)


def _loss_head(ff, g, h1, target, S, tm=512):
    tm = min(tm, S)

    def body(ff_ref, g_ref, h1_ref, t_ref, dh_ref, loss_ref):
        xv = ff_ref[...]
        r = lax.rsqrt(jnp.mean(xv * xv, axis=1, keepdims=True) + EPS)
        err = h1_ref[...] + xv * r * g_ref[...] - t_ref[...]
        dh_ref[...] = err * (1.0 / D)

        @pl.when(pl.program_id(0) == 0)
        def _():
            loss_ref[...] = jnp.zeros_like(loss_ref)

        loss_ref[...] += 0.5 * _sum_all(jnp.mean(err * err, axis=1, keepdims=True)) * jnp.ones((1, 128), f32)

    row = pl.BlockSpec((tm, D), lambda i: (i, 0))
    return pl.pallas_call(
        body, name="loss_head", grid=(S // tm,),
        in_specs=[row, pl.BlockSpec((1, D), lambda i: (0, 0)), row, row],
        out_specs=[row, pl.BlockSpec((1, 128), lambda i: (0, 0))],
        out_shape=[jax.ShapeDtypeStruct((S, D), f32), jax.ShapeDtypeStruct((1, 128), f32)],
        compiler_params=_params(("arbitrary",)),
    )(ff, g, h1, target)


def _local_step(x, mem, target, wts, late_rides, late_weights, small, rest_rides, w_in_ride):
    S = x.shape[0]
    M = mem.shape[0]
    pad = lambda v: jnp.pad(v, ((0, 0), (0, 128 - SSD_HEADS)))
    dtb_p, alog_p = pad(small["dt_bias"]), pad(small["a_log"])
    dsk_c = jnp.repeat(small["d_skip"], SB_HD, axis=1)

    u = _rms_fwd(x, small["norm_mix_pre"], name="norm_pre", out_dtype=bf16)
    if late_rides:
        proj, lands_a = _mm(u, wts["w_main"], "nn", tm=1024, tn=1024, name="in_proj", ride=late_rides[0])
    else:
        proj, lands_a = _mm(u, wts["w_main"], "nn", tm=1024, tn=1024, name="in_proj"), []
    pdt = _mm(u, wts["w_dt"], "nn", tm=1024, tn=128, name="in_proj_dt")
    y_sb, tot_lk, lands_b = _sb_fwd(proj, S, late_rides[1] if late_rides else None)
    wts = dict(wts, **late_weights(lands_a + lands_b))
    small = dict(small, conv_w=wts.pop("conv_w"))
    xc, xbc = _conv_fwd(proj, small["conv_w"], small["conv_b"], S)
    y_ssd, yn, hprev = _ssd_fwd(xbc, proj, pdt, dtb_p, alog_p, dsk_c, small["ssd_norm"], S)
    mn = _rms_fwd(mem, small["norm_mem"], name="norm_mem", out_dtype=bf16, tm=min(512, M))
    kv = _mm(mn, wts["w_mem_kv"], "nn", tm=M, tn=1024, name="mem_kv")
    y_mem = _mem_fwd(proj, kv, S)
    t0 = _mm(y_sb, wts["w_sb_out"], "nn", tm=1024, tn=1024, name="sb_out")
    t1 = _mm(yn, wts["w_ssd_out"], "nn", tm=1024, tn=1024, name="ssd_out")
    t2 = _mm(y_mem, wts["w_mem_out"], "nn", tm=1024, tn=1024, name="mem_out")
    merged = _merge_fwd(proj, t0, t1, t2, S)
    mix = _mm(merged, wts["w_o"], "nn", tm=1024, tn=1024, name="w_o")
    h1 = _rms_fwd(mix, small["norm_mix_post"], name="norm_mix_post", out_dtype=f32, residual=x)
    u2 = _rms_fwd(h1, small["norm_mlp_pre"], name="norm_mlp_pre", out_dtype=bf16)
    a_up, hrelu = _mm(u2, wts["w_up"], "nn", tm=1024, tn=1024, name="mlp_up", out_dtypes=(f32, bf16),
                      epi=lambda acc: (acc, jnp.square(jnp.maximum(acc, 0.0))))
    ff = _mm(hrelu, wts["w_down"], "nn", tm=1024, tn=1024, name="mlp_down")
    dh2, loss = _loss_head(ff, small["norm_mlp_post"], h1, target, S)

    g = {}
    dff, g["norm_mlp_post"] = _rms_bwd(ff, dh2, small["norm_mlp_post"], name="norm_mlp_post_bwd", out_dtype=bf16)
    da = _mm(dff, wts["w_down"], "nt", tm=1024, tn=1024, name="mlp_down_dx", out_dtypes=(bf16,),
             epi=lambda acc, a: (acc * (2.0 * jnp.maximum(a, 0.0)),), extras=(a_up,))
    g["w_down"] = _mm(hrelu, dff, "tn", tm=1024, tn=1024, name="mlp_down_dw")
    du2 = _mm(da, wts["w_up"], "nt", tm=1024, tn=1024, name="mlp_up_dx")
    g["w_up"] = _mm(u2, da, "tn", tm=1024, tn=1024, name="mlp_up_dw")
    dh1, g["norm_mlp_pre"] = _rms_bwd(h1, du2, small["norm_mlp_pre"], name="norm_mlp_pre_bwd", out_dtype=f32, add=dh2)
    dmix, g["norm_mix_post"] = _rms_bwd(mix, dh1, small["norm_mix_post"], name="norm_mix_post_bwd", out_dtype=bf16)
    dmerged = _mm(dmix, wts["w_o"], "nt", tm=1024, tn=1024, name="w_o_dx")
    g["w_o"] = _mm(merged, dmix, "tn", tm=1024, tn=1024, name="w_o_dw")
    dt0, dt1, dt2, dgl = _merge_bwd(proj, t0, t1, t2, dmerged, S)
    dy_sb = _mm(dt0, wts["w_sb_out"], "nt", tm=1024, tn=1024, name="sb_out_dx")
    g["w_sb_out"] = _mm(y_sb, dt0, "tn", tm=1024, tn=1024, name="sb_out_dw")
    dy_ssd = _mm(dt1, wts["w_ssd_out"], "nt", tm=1024, tn=1024, name="ssd_out_dx")
    g["w_ssd_out"] = _mm(yn, dt1, "tn", tm=1024, tn=1024, name="ssd_out_dw")
    dy_mem = _mm(dt2, wts["w_mem_out"], "nt", tm=1024, tn=1024, name="mem_out_dx")
    g["w_mem_out"] = _mm(y_mem, dt2, "tn", tm=1024, tn=1024, name="mem_out_dw")
    dmemq, dkv = _mem_bwd(proj, kv, dy_mem, S)
    g["w_mem_kv"] = _mm(mn, dkv, "tn", tm=1024, tn=1024, name="mem_kv_dw")
    dmn = _mm(dkv, wts["w_mem_kv"], "nt", tm=M, tn=1024, name="mem_kv_dx")
    _, g["norm_mem"] = _rms_bwd(mem, dmn, small["norm_mem"], name="norm_mem_bwd", out_dtype=bf16, tm=min(512, M))
    rides = rest_rides(g) if rest_rides else (None, None)
    dz, dxbc, ddt, g["ssd_norm"], dsk, dalog, ddtb, *lands_a = _ssd_bwd(
        dy_ssd, y_ssd, xbc, proj, pdt, hprev, dtb_p, alog_p, dsk_c, small["ssd_norm"], S, rides[0])
    g["d_skip"], g["a_log"], g["dt_bias"] = dsk[:, :SSD_HEADS], dalog[:, :SSD_HEADS], ddtb[:, :SSD_HEADS]
    dxbc_raw, dcw, g["conv_b"] = _conv_bwd(proj, xc, dxbc, small["conv_w"], S)
    g["conv_w"] = dcw[:CONV_K]
    dq, dk, dv, lands_b = _sb_bwd(proj, tot_lk, dy_sb, S, rides[1])
    g["rest_lands"] = lands_b + lands_a
    dproj = (dq, dk, dv, dxbc_raw, dgl, dmemq, dz)
    g["w_main"] = [_mm(u, p, "tn", tm=1024, tn=1024, name="in_proj_dw_%d" % i) for i, p in enumerate(dproj)]
    g["w_dt"] = _mm(u, ddt, "tn", tm=1024, tn=128, name="in_proj_dt_dw")
    du_dt = _mm(ddt, wts["w_dt"], "nt", tm=1024, tn=1024, name="in_proj_dt_dx")
    du, g["w_in_lands"] = _mm_pieces_nt(dproj, wts["w_main"], du_dt, tm=512, tn=256, name="in_proj_dx",
                                        ride=w_in_ride(g) if w_in_ride else None)
    grad_x, g["norm_mix_pre"] = _rms_bwd(x, du, small["norm_mix_pre"], name="norm_pre_bwd", out_dtype=f32, add=dh1)
    return loss, grad_x, g


def _to_internal(w_in):
    sec = lambda r: w_in[:, r[0]:r[1]]
    w_main = jnp.concatenate([sec(R_QKV), sec(R_XBC), sec(R_GATE), sec(R_MEMQ), sec(R_Z)], axis=1)
    w_dt = jnp.pad(sec(R_DT), ((0, 0), (0, 128 - SSD_HEADS)))
    return w_main, w_dt


def _from_internal(pieces, g_dt):
    dq, dk, dv, dxbc, dgate, dmemq, dz = pieces
    return [dq, dk, dv, dz, dxbc, g_dt[:, :SSD_HEADS], dmemq, dgate]


def _w_in_slab(ordered, s, dtype):
    width = D_IN // N_SHARD
    lo, hi, off, parts = s * width, (s + 1) * width, 0, []
    for p in ordered:
        a, b = max(lo, off), min(hi, off + p.shape[1])
        if a < b:
            parts.append(p[:, a - off:b - off].astype(dtype))
        off += p.shape[1]
    return jnp.concatenate(parts, axis=1)


MESH = pl.DeviceIdType.MESH
ANY = pl.BlockSpec(memory_space=pl.ANY)


def _place():
    x, y, c = lax.axis_index("x"), lax.axis_index("y"), lax.axis_index("c")
    return (x, y, c), [(1 - x, y, c), (x, 1 - y, c), (1 - x, 1 - y, c)]


def _exchange_copy(mode, ins, lands, send, recv, a, k, me, peers, arriving):
    p = peers[k]
    theirs = 2 * p[0] + p[1]
    if mode == "gather":
        src, dst = ins[a], lands[a].at[theirs if arriving else me]
    else:
        src, dst = ins[a].at[theirs], lands[a].at[k]
    return pltpu.make_async_remote_copy(src_ref=src, dst_ref=dst, send_sem=send.at[a * 3 + k],
                                        recv_sem=recv.at[a * 3 + k], device_id=p, device_id_type=MESH)


class _Ride:
    def __init__(self, srcs, mode):
        self.srcs, self.mode, self.n = list(srcs), mode, len(srcs)
        n = self.n
        self.in_specs, self.out_specs = [ANY] * n, [ANY] * n
        self.out_shape = [
            jax.ShapeDtypeStruct((N_SHARD,) + s.shape if mode == "gather" else (3,) + s.shape[1:], s.dtype)
            for s in self.srcs]
        self.scratch = [pltpu.SemaphoreType.DMA((3 * n,)), pltpu.SemaphoreType.DMA((3 * n,)),
                        pltpu.SemaphoreType.DMA((n,))]

    def _own(self, ins, lnd, sems):
        if self.mode != "gather":
            return []
        me = 2 * lax.axis_index("x") + lax.axis_index("y")
        return [pltpu.make_async_copy(ins[a], lnd[a].at[me], sems[2].at[a]) for a in range(self.n)]

    def _far(self, ins, lnd, sems, arriving):
        (x, y, c), peers = _place()
        return [_exchange_copy(self.mode, ins, lnd, sems[0], sems[1], a, k, 2 * x + y, peers, arriving)
                for a in range(self.n) for k in range(3)]

    def start(self, ins, lnd, sems):
        for cp in self._own(ins, lnd, sems) + self._far(ins, lnd, sems, False):
            cp.start()

    def finish(self, ins, lnd, sems):
        for cp in self._far(ins, lnd, sems, True):
            cp.wait_recv()
        for cp in self._far(ins, lnd, sems, False):
            cp.wait_send()
        for cp in self._own(ins, lnd, sems):
            cp.wait()


def _gather_two_level(shards, name):
    n = len(shards)

    def body(*refs):
        ins, lnd = refs[:n], refs[n:2 * n]
        send, recv, loc = refs[2 * n:]
        (x, y, c), peers = _place()
        me = 2 * x + y

        def half(ref, a, core):
            rows = shards[a].shape[0] // 2
            return ref.at[pl.ds(core * rows, rows)]

        def copy(a, j, slot, core, to):
            return pltpu.make_async_remote_copy(
                src_ref=half(ins[a], a, core) if j < 3 else half(lnd[a].at[slot], a, core),
                dst_ref=half(lnd[a].at[slot], a, core), send_sem=send.at[6 * a + j], recv_sem=recv.at[6 * a + j],
                device_id=to, device_id_type=MESH)

        own = [pltpu.make_async_copy(ins[a], lnd[a].at[me], loc.at[a]) for a in range(n)]
        far = [copy(a, k, me, c, peers[k]) for a in range(n) for k in range(3)]
        for cp in own + far:
            cp.start()
        passed = []
        for a in range(n):
            for k, p in enumerate(peers):
                theirs = 2 * p[0] + p[1]
                copy(a, k, theirs, c, p).wait_recv()
                passed.append(copy(a, 3 + k, theirs, c, (x, y, 1 - c)))
                passed[-1].start()
        for a in range(n):
            for k, p in enumerate(peers):
                copy(a, 3 + k, 2 * p[0] + p[1], 1 - c, (x, y, 1 - c)).wait_recv()
        for cp in far + passed:
            cp.wait_send()
        for cp in own:
            cp.wait()

    return pl.pallas_call(
        body, name=name, in_specs=[ANY] * n, out_specs=[ANY] * n,
        out_shape=[jax.ShapeDtypeStruct((N_SHARD,) + s.shape, s.dtype) for s in shards],
        scratch_shapes=[pltpu.SemaphoreType.DMA((6 * n,)), pltpu.SemaphoreType.DMA((6 * n,)),
                        pltpu.SemaphoreType.DMA((n,))],
    )(*shards)


def _exchange_packets(packet):
    def body(pk, pk_out, send, recv, loc):
        x, y, c = lax.axis_index("x"), lax.axis_index("y"), lax.axis_index("c")
        lin = 4 * x + 2 * y + c
        own = pltpu.make_async_copy(pk, pk_out.at[lin], loc.at[0])
        own.start()

        def pk_copy(m, slot):
            dev = (x ^ ((m >> 2) & 1), y ^ ((m >> 1) & 1), c ^ (m & 1))
            return pltpu.make_async_remote_copy(
                src_ref=pk, dst_ref=pk_out.at[slot], send_sem=send.at[m - 1], recv_sem=recv.at[m - 1],
                device_id=dev, device_id_type=MESH)

        sent = [pk_copy(m, lin) for m in range(1, N_DEV)]
        for cp in sent:
            cp.start()
        for m in range(1, N_DEV):
            pk_copy(m, lin ^ m).wait_recv()
        for cp in sent:
            cp.wait_send()
        own.wait()

    return pl.pallas_call(
        body, name="exchange_packets", in_specs=[ANY], out_specs=ANY,
        out_shape=jax.ShapeDtypeStruct((N_DEV,) + packet.shape, packet.dtype),
        scratch_shapes=[pltpu.SemaphoreType.DMA((N_DEV - 1,)), pltpu.SemaphoreType.DMA((N_DEV - 1,)),
                        pltpu.SemaphoreType.DMA((1,))],
    )(packet)


def _swap_sibling(parts, name):
    n = len(parts)

    def body(*refs):
        ins, outs = refs[:n], refs[n:2 * n]
        send, recv = refs[2 * n:]
        x, y, c = lax.axis_index("x"), lax.axis_index("y"), lax.axis_index("c")
        cps = [pltpu.make_async_remote_copy(
            src_ref=ins[a], dst_ref=outs[a], send_sem=send.at[a], recv_sem=recv.at[a],
            device_id=(x, y, 1 - c), device_id_type=MESH) for a in range(n)]
        for cp in cps:
            cp.start()
        for cp in cps:
            cp.wait_recv()
        for cp in cps:
            cp.wait_send()

    return pl.pallas_call(
        body, name=name,
        in_specs=[ANY] * n, out_specs=[ANY] * n,
        out_shape=[jax.ShapeDtypeStruct(p.shape, p.dtype) for p in parts],
        scratch_shapes=[pltpu.SemaphoreType.DMA((n,)), pltpu.SemaphoreType.DMA((n,))],
    )(*parts)


BLOCK_ELEMS = 256 * 1024


def _row_tile(R, C):
    tr = max(8, (BLOCK_ELEMS // C) // 8 * 8)
    while R % tr:
        tr -= 8
    return min(tr, R)


def _sum_parts(own, stack, name):
    k = stack.shape[0]
    R, C = stack.shape[1:]
    tr = _row_tile(R, C)

    def body(*refs):
        o_ref = refs[-1]
        acc = refs[0][...]
        for r in refs[1:-1]:
            acc = acc + r[...]
        o_ref[...] = acc

    row = pl.BlockSpec((tr, C), lambda i: (i, 0))
    specs = ([row] if own is not None else []) + [
        pl.BlockSpec((None, tr, C), functools.partial(lambda i, j: (j, i, 0), j=j)) for j in range(k)]
    args = ([own] if own is not None else []) + [stack] * k
    return pl.pallas_call(
        body, name=name, grid=(R // tr,), in_specs=specs, out_specs=row,
        out_shape=jax.ShapeDtypeStruct((R, C), f32), compiler_params=_params(("parallel",)),
    )(*args)


def _adamw(w, m, v, g_parts, name):
    R, C = w.shape
    tr = _row_tile(R, C)
    n_g = len(g_parts)

    def body(w_ref, m_ref, v_ref, *rest):
        g = rest[0][...]
        for r in rest[1:n_g]:
            g = g + r[...]
        g_ref, d_ref, nm_ref, nv_ref = rest[n_g:]
        nm = ADAM_B1 * m_ref[...] + (1.0 - ADAM_B1) * g
        nv = ADAM_B2 * v_ref[...] + (1.0 - ADAM_B2) * jnp.square(g)
        m_hat = nm / (1.0 - ADAM_B1 ** ADAM_STEP)
        v_hat = nv / (1.0 - ADAM_B2 ** ADAM_STEP)
        g_ref[...] = g
        d_ref[...] = -ADAM_LR * (m_hat / (jnp.sqrt(v_hat) + ADAM_EPS) + ADAM_WD * w_ref[...])
        nm_ref[...] = nm
        nv_ref[...] = nv

    row = pl.BlockSpec((tr, C), lambda i: (i, 0))
    return pl.pallas_call(
        body, name=name, grid=(R // tr,), in_specs=[row] * (3 + n_g), out_specs=[row] * 4,
        out_shape=[jax.ShapeDtypeStruct((R, C), f32)] * 4, compiler_params=_params(("parallel",)),
    )(w, m, v, *g_parts)


BIG = ("w_in", "w_mem_kv", "w_sb_out", "w_ssd_out", "w_mem_out", "w_o", "w_up", "w_down")
FIRST = ("w_in", "w_mem_kv")
LATE = ("w_sb_out", "w_ssd_out", "w_mem_out", "w_o", "w_up", "w_down")
REST = BIG[1:]
COL_SHARDED = ("w_in", "w_mem_kv", "w_up")
SMALL = ("norm_mix_pre", "conv_w", "conv_b", "dt_bias", "a_log", "d_skip", "ssd_norm", "norm_mem",
         "norm_mix_post", "norm_mlp_pre", "norm_mlp_post")
WEIGHTS = ("norm_mix_pre", "w_in", "conv_w", "conv_b", "dt_bias", "a_log", "d_skip", "ssd_norm", "norm_mem",
           "w_mem_kv", "w_sb_out", "w_ssd_out", "w_mem_out", "w_o", "norm_mix_post", "norm_mlp_pre", "w_up",
           "w_down", "norm_mlp_post")
PK_ROWS = 184


def _pack(vecs):
    flat = jnp.concatenate([v.reshape(-1) for v in vecs])
    return jnp.pad(flat, (0, PK_ROWS * 128 - flat.shape[0])).reshape(PK_ROWS, 128)


def _unpack(pk, shapes):
    flat = pk.reshape(-1)
    out, off = [], 0
    for s in shapes:
        n = 1
        for d in s:
            n *= d
        out.append(flat[off:off + n].reshape(s))
        off += n
    return out


def _full_from_slabs(name, slabs):
    if name in COL_SHARDED:
        return slabs.transpose(1, 0, 2).reshape(slabs.shape[1], -1)
    return slabs.reshape(-1, slabs.shape[2])


def _slabs_from_full(name, g):
    if name in COL_SHARDED:
        return g.reshape(g.shape[0], N_SHARD, -1).transpose(1, 0, 2)
    return g.reshape(N_SHARD, -1, g.shape[1])


def kernel(x, mem, norm_mix_pre, w_in, conv_w, conv_b, dt_bias, a_log, d_skip, ssd_norm, norm_mem, w_mem_kv, w_sb_out, w_ssd_out, w_mem_out, w_o, norm_mix_post, norm_mlp_pre, w_up, w_down, norm_mlp_post, loss_target, m_norm_mix_pre, m_w_in, m_conv_w, m_conv_b, m_dt_bias, m_a_log, m_d_skip, m_ssd_norm, m_norm_mem, m_w_mem_kv, m_w_sb_out, m_w_ssd_out, m_w_mem_out, m_w_o, m_norm_mix_post, m_norm_mlp_pre, m_w_up, m_w_down, m_norm_mlp_post, v_norm_mix_pre, v_w_in, v_conv_w, v_conv_b, v_dt_bias, v_a_log, v_d_skip, v_ssd_norm, v_norm_mem, v_w_mem_kv, v_w_sb_out, v_w_ssd_out, v_w_mem_out, v_w_o, v_norm_mix_post, v_norm_mlp_pre, v_w_up, v_w_down, v_norm_mlp_post):
    env = dict(locals())
    w = {n: env[n] for n in WEIGHTS}
    mo = {n: env["m_" + n] for n in WEIGHTS}
    vo = {n: env["v_" + n] for n in WEIGHTS}
    shard = 2 * lax.axis_index("x") + lax.axis_index("y")

    first = _gather_two_level([w[n][0].astype(bf16) for n in FIRST], "gather_first")
    w_main, w_dt = _to_internal(_full_from_slabs("w_in", first[0]))
    wts = dict(w_main=w_main, w_dt=w_dt, w_mem_kv=_full_from_slabs("w_mem_kv", first[1]))
    late_rides = (_Ride([w[n][0].astype(bf16) for n in LATE[:4]] + [w["conv_w"][0]], "gather"),
                  _Ride([w[n][0].astype(bf16) for n in LATE[4:]], "gather"))

    def late_weights(lands):
        full = {n: _full_from_slabs(n, s) for n, s in zip(LATE, lands[:4] + lands[5:])}
        return dict(full, conv_w=lands[4].transpose(1, 0, 2).reshape(CONV_K, CONV_DIM))

    def rest_rides(g):
        slabs = [_slabs_from_full(n, g[n]).astype(bf16) for n in REST]
        return _Ride(slabs[5:], "scatter"), _Ride(slabs[:5], "scatter")

    def w_in_ride(g):
        g["w_in"] = _from_internal(g["w_main"], g["w_dt"])
        return _Ride([jnp.stack([_w_in_slab(g["w_in"], s, bf16) for s in range(N_SHARD)])], "scatter")

    def own_slab(n):
        if n == "w_in":
            return lax.switch(shard, [functools.partial(_w_in_slab, g["w_in"], s, f32) for s in range(N_SHARD)])
        return lax.dynamic_index_in_dim(_slabs_from_full(n, g[n]), shard, 0, keepdims=False)

    small = {n: w[n] for n in SMALL if n != "conv_w"}
    loss, grad_x, g = _local_step(x[0], mem[0], loss_target[0], wts, late_rides, late_weights, small,
                                  rest_rides, w_in_ride)

    lands = g["w_in_lands"] + g["rest_lands"]
    packets = _exchange_packets(_pack([g[n] for n in SMALL] + [loss[:, :1]]))
    partial = []
    for n, r in zip(BIG, lands):
        partial.append(_sum_parts(own_slab(n), r, name="sum_chips_" + n))
    other = _swap_sibling(partial, "swap_sibling")

    out_g, out_d, out_m, out_v = {}, {}, {}, {}
    for n, p, q in zip(BIG, partial, other):
        res = _adamw(w[n][0], mo[n][0], vo[n][0], [p, q], name="adamw_" + n)
        out_g[n], out_d[n], out_m[n], out_v[n] = [r[None] for r in res]
    tot = _sum_parts(None, packets, name="sum_packets")
    shapes = [g[n].shape for n in SMALL] + [(1, 1)]
    sm = dict(zip(SMALL + ("loss",), _unpack(tot, shapes)))
    sm["conv_w"] = lax.dynamic_slice_in_dim(sm["conv_w"], shard * (CONV_DIM // N_SHARD), CONV_DIM // N_SHARD, axis=1)
    own_small = lambda d: _pack([d[n].reshape(sm[n].shape) for n in SMALL])
    res = _adamw(own_small(w), own_small(mo), own_small(vo), [own_small(sm)], name="adamw_small")
    own_shapes = [sm[n].shape for n in SMALL]
    for store, r in zip((out_g, out_d, out_m, out_v), res):
        for n, val in zip(SMALL, _unpack(r, own_shapes)):
            store[n] = val.reshape(w[n].shape)

    outs = [sm["loss"].reshape(()), grad_x[None]]
    for store in (out_g, out_d, out_m, out_v):
        outs += [store[n] for n in WEIGHTS]
    return tuple(outs)
```

```python
import functools

import jax
import jax.numpy as jnp
from jax import lax
from jax.experimental import pallas as pl
from jax.experimental.pallas import tpu as pltpu

f32 = jnp.float32
bf16 = jnp.bfloat16

D = 1024
EPS = 1e-6
SB_HD = 64
SSD_INNER = 2048
SSD_HEADS = 32
SSD_GROUPS = 4
SSD_N = 128
SSD_L = 128
CONV_K = 4
CONV_DIM = 3072
MEM_HEADS = 4
MEM_HD = 256
D_FF = 4096
D_IN = 12320
N_SHARD = 4
N_DEV = 8

P_QKV, P_XBC, P_GATE, P_MEMQ, P_Z, P_DT, P_TOT = 0, 3072, 6144, 9216, 10240, 12288, 12416
R_QKV, R_Z, R_XBC, R_DT, R_MEMQ, R_GATE = (0, 3072), (3072, 5120), (5120, 8192), (8192, 8224), (8224, 9248), (9248, 12320)

ADAM_LR = 0.001
ADAM_B1 = 0.9
ADAM_B2 = 0.999
ADAM_EPS = 1e-08
ADAM_WD = 0.01
ADAM_STEP = 10

VMEM_LIMIT = 56 * 1024 * 1024

NN = (((1,), (0,)), ((), ()))
NT = (((1,), (1,)), ((), ()))
TN = (((0,), (0,)), ((), ()))


def _dot(a, b, dims=NN):
    return lax.dot_general(a, b, dims, preferred_element_type=f32)


def _params(sem=None):
    return pltpu.CompilerParams(dimension_semantics=sem, vmem_limit_bytes=VMEM_LIMIT)


def _sigmoid(x):
    return 1.0 / (1.0 + jnp.exp(-x))


def _split2(x):
    hi = x.astype(bf16)
    lo = (x - hi.astype(f32)).astype(bf16)
    return hi, lo


def _split3(x):
    hi = x.astype(bf16)
    r = x - hi.astype(f32)
    mid = r.astype(bf16)
    lo = (r - mid.astype(f32)).astype(bf16)
    return hi, mid, lo


def _mm(a, b, mode, *, tm, tn, name, out_dtypes=(f32,), epi=None, extras=(), ride=None):
    M = a.shape[1] if mode == "tn" else a.shape[0]
    N = b.shape[0] if mode == "nt" else b.shape[1]
    tm, tn = min(tm, M), min(tn, N)
    if mode == "nn":
        (M, K), N = a.shape, b.shape[1]
        a_spec = pl.BlockSpec((tm, K), lambda i, j: (i, 0))
        b_spec = pl.BlockSpec((K, tn), lambda i, j: (0, j))
        dims = NN
    elif mode == "nt":
        (M, K), N = a.shape, b.shape[0]
        a_spec = pl.BlockSpec((tm, K), lambda i, j: (i, 0))
        b_spec = pl.BlockSpec((tn, K), lambda i, j: (j, 0))
        dims = NT
    else:
        (K, M), N = a.shape, b.shape[1]
        a_spec = pl.BlockSpec((K, tm), lambda i, j: (0, i))
        b_spec = pl.BlockSpec((K, tn), lambda i, j: (0, j))
        dims = TN
    assert M % tm == 0 and N % tn == 0, (name, M, N, tm, tn)
    n_ex, n_out = len(extras), len(out_dtypes)
    n_r = ride.n if ride else 0
    o_spec = pl.BlockSpec((tm, tn), lambda i, j: (i, j))
    grid = (M // tm, N // tn)

    def body(a_ref, b_ref, *rest):
        r_ins = rest[n_ex:n_ex + n_r]
        outs = rest[n_ex + n_r:n_ex + n_r + n_out]
        r_lnd, r_sems = rest[n_ex + n_r + n_out:n_ex + 2 * n_r + n_out], rest[n_ex + 2 * n_r + n_out:]
        i, j = pl.program_id(0), pl.program_id(1)
        if ride:
            pl.when((i == 0) & (j == 0))(lambda: ride.start(r_ins, r_lnd, r_sems))
        acc = _dot(a_ref[...].astype(bf16), b_ref[...].astype(bf16), dims)
        res = (acc,) if epi is None else epi(acc, *[e[...] for e in rest[:n_ex]])
        for o_ref, r in zip(outs, res):
            o_ref[...] = r.astype(o_ref.dtype)
        if ride:
            pl.when((i == grid[0] - 1) & (j == grid[1] - 1))(lambda: ride.finish(r_ins, r_lnd, r_sems))

    out = pl.pallas_call(
        body, name=name, grid=grid,
        in_specs=[a_spec, b_spec] + [o_spec] * n_ex + (ride.in_specs if ride else []),
        out_specs=[o_spec] * n_out + (ride.out_specs if ride else []),
        out_shape=[jax.ShapeDtypeStruct((M, N), dt) for dt in out_dtypes] + (ride.out_shape if ride else []),
        scratch_shapes=ride.scratch if ride else [],
        compiler_params=_params(("arbitrary", "arbitrary") if ride else ("parallel", "parallel")),
    )(a, b, *extras, *(ride.srcs if ride else []))
    if ride:
        return (out[0] if n_out == 1 else out[:n_out]), list(out[n_out:])
    return out[0] if n_out == 1 else out


def _mm_pieces_nt(pieces, b, add, *, tm, tn, name, ride):
    M, N = pieces[0].shape[0], b.shape[0]
    n_p, n_r = len(pieces), (ride.n if ride else 0)
    o_spec = pl.BlockSpec((tm, tn), lambda i, j: (i, j))
    grid = (M // tm, N // tn)

    def body(*refs):
        b_ref, add_ref = refs[n_p:n_p + 2]
        r_ins, o_ref = refs[n_p + 2:n_p + 2 + n_r], refs[n_p + 2 + n_r]
        r_lnd, r_sems = refs[n_p + 3 + n_r:n_p + 3 + 2 * n_r], refs[n_p + 3 + 2 * n_r:]
        i, j = pl.program_id(0), pl.program_id(1)
        if ride:
            pl.when((i == 0) & (j == 0))(lambda: ride.start(r_ins, r_lnd, r_sems))
        acc, off = add_ref[...], 0
        for r in refs[:n_p]:
            acc = acc + _dot(r[...], b_ref[:, off:off + r.shape[1]], NT)
            off += r.shape[1]
        o_ref[...] = acc
        if ride:
            pl.when((i == grid[0] - 1) & (j == grid[1] - 1))(lambda: ride.finish(r_ins, r_lnd, r_sems))

    out = pl.pallas_call(
        body, name=name, grid=grid,
        in_specs=[pl.BlockSpec((tm, p.shape[1]), lambda i, j: (i, 0)) for p in pieces]
        + [pl.BlockSpec((tn, b.shape[1]), lambda i, j: (j, 0)), o_spec] + (ride.in_specs if ride else []),
        out_specs=[o_spec] + (ride.out_specs if ride else []),
        out_shape=[jax.ShapeDtypeStruct((M, N), f32)] + (ride.out_shape if ride else []),
        scratch_shapes=ride.scratch if ride else [],
        compiler_params=_params(("arbitrary", "arbitrary")),
    )(*pieces, b, add, *(ride.srcs if ride else []))
    return out[0], list(out[1:])


def _rms_fwd(x, g, *, name, out_dtype, residual=None, tm=512):
    S, C = x.shape
    tm = min(tm, S)
    has_res = residual is not None

    def body(x_ref, g_ref, *rest):
        xv = x_ref[...]
        r = lax.rsqrt(jnp.mean(xv * xv, axis=1, keepdims=True) + EPS)
        y = xv * r * g_ref[...]
        if has_res:
            y = y + rest[0][...]
        rest[-1][...] = y.astype(out_dtype)

    row = pl.BlockSpec((tm, C), lambda i: (i, 0))
    vec = pl.BlockSpec((1, C), lambda i: (0, 0))
    args = (x, g) + ((residual,) if has_res else ())
    return pl.pallas_call(
        body, name=name, grid=(S // tm,),
        in_specs=[row, vec] + ([row] if has_res else []),
        out_specs=row, out_shape=jax.ShapeDtypeStruct((S, C), out_dtype),
        compiler_params=_params(("parallel",)),
    )(*args)


def _rms_bwd(x, dy, g, *, name, out_dtype, add=None, tm=512):
    S, C = x.shape
    tm = min(tm, S)
    has_add = add is not None

    def body(x_ref, dy_ref, g_ref, *rest):
        dx_ref, dg_ref = rest[-2], rest[-1]
        xv = x_ref[...]
        dyv = dy_ref[...].astype(f32)
        r = lax.rsqrt(jnp.mean(xv * xv, axis=1, keepdims=True) + EPS)
        xh = xv * r
        dxh = dyv * g_ref[...]
        dx = r * (dxh - xh * jnp.mean(dxh * xh, axis=1, keepdims=True))
        if has_add:
            dx = dx + rest[0][...]
        dx_ref[...] = dx.astype(out_dtype)

        @pl.when(pl.program_id(0) == 0)
        def _():
            dg_ref[...] = jnp.zeros_like(dg_ref)

        dg_ref[...] += jnp.sum(dyv * xh, axis=0, keepdims=True)

    row = pl.BlockSpec((tm, C), lambda i: (i, 0))
    vec = pl.BlockSpec((1, C), lambda i: (0, 0))
    args = (x, dy, g) + ((add,) if has_add else ())
    return pl.pallas_call(
        body, name=name, grid=(S // tm,),
        in_specs=[row, row, vec] + ([row] if has_add else []),
        out_specs=[row, vec],
        out_shape=[jax.ShapeDtypeStruct((S, C), out_dtype), jax.ShapeDtypeStruct((1, C), f32)],
        compiler_params=_params(("arbitrary",)),
    )(*args)


SB_T = 128
SB_SPENT = -120.0
SB_TAIL = 3
SB_GROUPS = (4, 2, 1)
SB_GROUPS_BWD = (4, 2, 1)


def _sb_masks():
    lane = lax.broadcasted_iota(jnp.int32, (1, 128), 1)
    m_a = (lane < SB_HD).astype(f32)
    return m_a, 1.0 - m_a


def _chunks(a, n):
    return [a[:, u * SB_T:(u + 1) * SB_T] for u in range(n)]


def _cat(parts, axis):
    return parts[0] if len(parts) == 1 else jnp.concatenate(parts, axis=axis)


def _mask_last(a, n, mask):
    if mask is None:
        return a
    parts = _chunks(a, n)
    return _cat(parts[:-1] + [jnp.where(mask, parts[-1], 0.0)], 1)


def _sb_logits(z, n, mask):
    l1p = jnp.log(1.0 + jnp.exp(-jnp.abs(z)))
    lb = jnp.minimum(z, 0.0) - l1p
    return lb, _mask_last(lb - z, n, mask)


def _by_count(i, most, fn):
    return lax.switch(jnp.minimum(i, most - 1), [functools.partial(fn, n) for n in range(1, most + 1)])


def _chunk_matmul(parts_list, u_mat):
    out = _dot(_cat(parts_list, 0), u_mat)
    return [out[u * SB_T:(u + 1) * SB_T] for u in range(len(parts_list))]


def _chunk_cumsum(lk, n, u_mat):
    hi = lk.astype(bf16)
    lo = (lk - hi.astype(f32)).astype(bf16)
    out = _chunk_matmul(_chunks(hi, n) + _chunks(lo, n), u_mat)
    return [out[u] + out[n + u] for u in range(n)]


def _sb_fwd(proj, S, ride=None):
    nq = S // SB_T
    n_pairs = D // 128
    scale = SB_HD ** -0.5
    n_r = ride.n if ride else 0

    def body(q_ref, k_ref, v_ref, *rest):
        o_ref, t_ref = rest[n_r:n_r + 2]
        i = pl.program_id(1)
        if ride:
            pl.when((pl.program_id(0) == 0) & (i == 0))(
                lambda: ride.start(rest[:n_r], rest[n_r + 2:2 * n_r + 2], rest[2 * n_r + 2:]))
        m_a, m_b = _sb_masks()
        r_i = lax.broadcasted_iota(jnp.int32, (SB_T, SB_T), 0)
        c_i = lax.broadcasted_iota(jnp.int32, (SB_T, SB_T), 1)
        u_mat = (r_i > c_i).astype(bf16)
        causal = c_i < r_i
        q = q_ref[...] * scale
        q_h = ((q * m_a).astype(bf16), (q * m_b).astype(bf16))

        def group(j_lo, n, carry, mask):
            acc, c_a, c_b = carry
            rows = pl.ds(pl.multiple_of(j_lo * SB_T, SB_T), n * SB_T)
            k = k_ref[rows, :].astype(bf16)
            v = v_ref[rows, :]
            zs = [_dot(q_b, k, NT) for q_b in q_h]
            lbk = [_sb_logits(z, n, mask) for z in zs]
            parts = [_chunk_cumsum(lk, n, u_mat) for _, lk in lbk]
            ws, cs = [], []
            for (lb, lk), part, c in zip(lbk, parts, (c_a, c_b)):
                lb_c, lk_c = _chunks(lb, n), _chunks(lk, n)
                w_c = [None] * n
                for u in reversed(range(n)):
                    w_c[u] = jnp.exp(lb_c[u] + c + part[u])
                    c = c + jnp.sum(lk_c[u], axis=1, keepdims=True)
                ws.append(_mask_last(_cat(w_c, 1), n, mask).astype(bf16))
                cs.append(c)
            for w, m in zip(ws, (m_a, m_b)):
                acc = acc + _dot(w, (v * m).astype(bf16))
            return acc, cs[0], cs[1]

        zero_c = jnp.zeros((SB_T, 1), f32)
        init = (jnp.zeros((SB_T, 128), f32), zero_c, zero_c)
        carry = _by_count(i, SB_TAIL, lambda n: group(i - n + 1, n, init, causal))

        def spent(cr):
            return (jnp.max(jnp.maximum(cr[1], cr[2])) < SB_SPENT).astype(jnp.int32)

        state = (i - jnp.minimum(i, SB_TAIL - 1), spent(carry), carry)
        for n in SB_GROUPS:
            def step(st, n=n):
                left, _, cr = st
                cr = group(left - n, n, cr, None)
                return left - n, spent(cr), cr

            state = lax.while_loop(lambda st, n=n: (st[0] >= n) & (st[1] == 0), step, state)
        left, _, carry = state
        o_ref[...] = carry[0]
        lane = lax.broadcasted_iota(jnp.int32, (1, 128), 1)
        t_ref[...] = (jnp.where(lane == 0, carry[1], 0.0) + jnp.where(lane == SB_HD, carry[2], 0.0)
                      + jnp.where(lane == 1, left.astype(f32), 0.0))
        if ride:
            pl.when((pl.program_id(0) == n_pairs - 1) & (i == nq - 1))(
                lambda: ride.finish(rest[:n_r], rest[n_r + 2:2 * n_r + 2], rest[2 * n_r + 2:]))

    qs = pl.BlockSpec((SB_T, 128), lambda h, i: (i, h))
    out = pl.pallas_call(
        body, name="sb_fwd", grid=(n_pairs, nq),
        in_specs=[qs,
                  pl.BlockSpec((S, 128), lambda h, i: (0, n_pairs + h)),
                  pl.BlockSpec((S, 128), lambda h, i: (0, 2 * n_pairs + h))] + (ride.in_specs if ride else []),
        out_specs=[qs, qs] + (ride.out_specs if ride else []),
        out_shape=[jax.ShapeDtypeStruct((S, D), f32)] * 2 + (ride.out_shape if ride else []),
        scratch_shapes=ride.scratch if ride else [],
        compiler_params=_params(("arbitrary", "arbitrary")),
    )(proj, proj, proj, *(ride.srcs if ride else []))
    return out[0], out[1], list(out[2:])


def _sb_bwd(proj, tot_lk, do, S, ride=None):
    nq = S // SB_T
    n_pairs = D // 128
    scale = SB_HD ** -0.5
    n_r = ride.n if ride else 0

    def body(q_ref, k_ref, v_ref, t_ref, do_ref, *rest):
        dq_ref, dk_ref, dv_ref = rest[n_r:n_r + 3]
        dk_acc, dv_acc = rest[2 * n_r + 3:2 * n_r + 5]
        r_ins, r_lnd, r_sems = rest[:n_r], rest[n_r + 3:2 * n_r + 3], rest[2 * n_r + 5:]
        i = pl.program_id(1)
        if ride:
            pl.when((pl.program_id(0) == 0) & (i == 0))(lambda: ride.start(r_ins, r_lnd, r_sems))
        m_a, m_b = _sb_masks()
        r_i = lax.broadcasted_iota(jnp.int32, (SB_T, SB_T), 0)
        c_i = lax.broadcasted_iota(jnp.int32, (SB_T, SB_T), 1)
        u_inc = (r_i <= c_i).astype(bf16)
        u_exc = (r_i < c_i).astype(bf16)
        causal = c_i < r_i

        @pl.when(i == 0)
        def _():
            dk_acc[...] = jnp.zeros_like(dk_acc)
            dv_acc[...] = jnp.zeros_like(dv_acc)

        q = q_ref[...] * scale
        dov = do_ref[...]
        tv = t_ref[...]
        lane = lax.broadcasted_iota(jnp.int32, (1, 128), 1)
        heads = []
        for m, first in ((m_a, 0), (m_b, SB_HD)):
            tot = jnp.sum(jnp.where(lane == first, tv, 0.0), axis=1, keepdims=True)
            heads.append(((q * m).astype(bf16), (dov * m).astype(bf16), tot, m))
        lowest = jnp.clip(jnp.max(jnp.where(lane == 1, tv, 0.0)).astype(jnp.int32), 0, i)

        def group(j_lo, n, carry, mask):
            dq_acc, cp_a, cp_b, ce_a, ce_b = carry
            rows = pl.ds(pl.multiple_of(j_lo * SB_T, SB_T), n * SB_T)
            k_f = k_ref[rows, :]
            k = k_f.astype(bf16)
            v = v_ref[rows, :].astype(bf16)
            zs = [_dot(h[0], k, NT) for h in heads]
            dws = [_dot(h[1], v, NT) for h in heads]
            lbk = [_sb_logits(z, n, mask) for z in zs]
            parts = [_chunk_cumsum(lk, n, u_inc) for _, lk in lbk]
            ws, es, cps = [], [], []
            for (lb, lk), part, dw, h, cp in zip(lbk, parts, dws, heads, (cp_a, cp_b)):
                lb_c, lk_c = _chunks(lb, n), _chunks(lk, n)
                w_c = []
                for u in range(n):
                    w_c.append(jnp.exp(lb_c[u] + (h[2] - cp) - part[u]))
                    cp = cp + jnp.sum(lk_c[u], axis=1, keepdims=True)
                w = _mask_last(_cat(w_c, 1), n, mask)
                ws.append(w)
                es.append(dw * w)
                cps.append(cp)
            e_parts = [_chunk_matmul(_chunks(e.astype(bf16), n), u_exc) for e in es]
            dzs, ces = [], []
            for (lb, _), e, e_part, ce in zip(lbk, es, e_parts, (ce_a, ce_b)):
                e_c = _chunks(e, n)
                big_c = []
                for u in range(n):
                    big_c.append(ce + e_part[u])
                    ce = ce + jnp.sum(e_c[u], axis=1, keepdims=True)
                sig = jnp.exp(lb)
                dz = _mask_last(e * (1.0 - sig) - _cat(big_c, 1) * sig, n, mask)
                dzs.append(dz.astype(bf16))
                ces.append(ce)
            dk_t = jnp.zeros((n * SB_T, 128), f32)
            dv_t = jnp.zeros((n * SB_T, 128), f32)
            for dz_b, w, h in zip(dzs, ws, heads):
                dq_acc = dq_acc + _dot(dz_b, (k_f * h[3]).astype(bf16))
                dk_t = dk_t + _dot(dz_b, h[0], TN)
                dv_t = dv_t + _dot(w.astype(bf16), h[1], TN)
            dk_acc[rows, :] += dk_t
            dv_acc[rows, :] += dv_t
            return dq_acc, cps[0], cps[1], ces[0], ces[1]

        zc = jnp.zeros((SB_T, 1), f32)
        carry = (jnp.zeros((SB_T, 128), f32), zc, zc, zc, zc)
        done = lowest
        tail_lo = i - jnp.minimum(i, SB_TAIL - 1)
        for n in SB_GROUPS_BWD:
            trips = (tail_lo - done) // n
            carry = lax.fori_loop(
                0, trips, functools.partial(lambda gi, cr, n, done: group(done + gi * n, n, cr, None), n=n, done=done),
                carry)
            done = done + trips * n
        carry = _by_count(i, SB_TAIL, lambda n: group(i - n + 1, n, carry, causal))
        dq_ref[...] = (carry[0] * scale).astype(bf16)

        @pl.when(i == nq - 1)
        def _():
            dk_ref[...] = dk_acc[...].astype(bf16)
            dv_ref[...] = dv_acc[...].astype(bf16)

        if ride:
            pl.when((pl.program_id(0) == n_pairs - 1) & (i == nq - 1))(
                lambda: ride.finish(r_ins, r_lnd, r_sems))

    qs = pl.BlockSpec((SB_T, 128), lambda h, i: (i, h))
    full = pl.BlockSpec((S, 128), lambda h, i: (0, h))
    out = pl.pallas_call(
        body, name="sb_bwd", grid=(n_pairs, nq),
        in_specs=[qs,
                  pl.BlockSpec((S, 128), lambda h, i: (0, n_pairs + h)),
                  pl.BlockSpec((S, 128), lambda h, i: (0, 2 * n_pairs + h)),
                  qs, qs] + (ride.in_specs if ride else []),
        out_specs=[qs, full, full] + (ride.out_specs if ride else []),
        out_shape=[jax.ShapeDtypeStruct((S, D), bf16)] * 3 + (ride.out_shape if ride else []),
        scratch_shapes=[pltpu.VMEM((S, 128), f32), pltpu.VMEM((S, 128), f32)] + (ride.scratch if ride else []),
        compiler_params=_params(("arbitrary", "arbitrary")),
    )(proj, proj, proj, tot_lk, do, *(ride.srcs if ride else []))
    return out[0], out[1], out[2], list(out[3:])


CONV_CB = 256
HALO = 8


def _conv_fwd(proj, conv_w, conv_b, S):
    tr = min(512, S)

    def body(x_ref, w_ref, b_ref, xc_ref, xbc_ref):
        w = w_ref[...]
        for t in range(S // tr):
            cur = x_ref[t * tr:(t + 1) * tr, :]
            halo = x_ref[t * tr - HALO:t * tr, :] if t else jnp.zeros((HALO, CONV_CB), f32)
            win = jnp.concatenate([halo, cur], axis=0)
            acc = b_ref[...] + w[CONV_K - 1:CONV_K, :] * cur
            for k in range(CONV_K - 1):
                acc = acc + w[k:k + 1, :] * pltpu.roll(win, CONV_K - 1 - k, 0)[HALO:, :]
            xc_ref[t * tr:(t + 1) * tr, :] = acc
            xbc_ref[t * tr:(t + 1) * tr, :] = acc * _sigmoid(acc)

    col = pl.BlockSpec((S, CONV_CB), lambda c: (0, c))
    return pl.pallas_call(
        body, name="conv_fwd", grid=(CONV_DIM // CONV_CB,),
        in_specs=[pl.BlockSpec((S, CONV_CB), lambda c: (0, P_XBC // CONV_CB + c)),
                  pl.BlockSpec((CONV_K, CONV_CB), lambda c: (0, c)),
                  pl.BlockSpec((1, CONV_CB), lambda c: (0, c))],
        out_specs=[col, col], out_shape=[jax.ShapeDtypeStruct((S, CONV_DIM), f32)] * 2,
        compiler_params=_params(("parallel",)),
    )(proj, conv_w, conv_b)


def _conv_bwd(proj, xc, dxbc, conv_w, S):
    tr = min(512, S)

    def body(x_ref, xc_ref, dy_ref, w_ref, dx_ref, dw_ref, db_ref, dxc_s):
        w = w_ref[...]
        xcv = xc_ref[...]
        sg = _sigmoid(xcv)
        dxc_s[0:S, :] = dy_ref[...] * (sg * (1.0 + xcv * (1.0 - sg)))
        dxc_s[S:S + HALO, :] = jnp.zeros((HALO, CONV_CB), f32)
        dws = [jnp.zeros((1, CONV_CB), f32) for _ in range(CONV_K)]
        db = jnp.zeros((1, CONV_CB), f32)
        for t in range(S // tr):
            cur = x_ref[t * tr:(t + 1) * tr, :]
            halo = x_ref[t * tr - HALO:t * tr, :] if t else jnp.zeros((HALO, CONV_CB), f32)
            win = jnp.concatenate([halo, cur], axis=0)
            dwin = dxc_s[t * tr:(t + 1) * tr + HALO, :]
            dcur = dwin[0:tr, :]
            db = db + jnp.sum(dcur, axis=0, keepdims=True)
            dws[CONV_K - 1] = dws[CONV_K - 1] + jnp.sum(dcur * cur, axis=0, keepdims=True)
            dx = w[CONV_K - 1:CONV_K, :] * dcur
            for k in range(CONV_K - 1):
                sh = CONV_K - 1 - k
                dws[k] = dws[k] + jnp.sum(dcur * pltpu.roll(win, sh, 0)[HALO:, :], axis=0, keepdims=True)
                dx = dx + w[k:k + 1, :] * pltpu.roll(dwin, tr + HALO - sh, 0)[0:tr, :]
            dx_ref[t * tr:(t + 1) * tr, :] = dx.astype(bf16)
        dw_ref[...] = jnp.concatenate(dws + [jnp.zeros((8 - CONV_K, CONV_CB), f32)], axis=0)
        db_ref[...] = db

    col = pl.BlockSpec((S, CONV_CB), lambda c: (0, c))
    return pl.pallas_call(
        body, name="conv_bwd", grid=(CONV_DIM // CONV_CB,),
        in_specs=[pl.BlockSpec((S, CONV_CB), lambda c: (0, P_XBC // CONV_CB + c)), col, col,
                  pl.BlockSpec((CONV_K, CONV_CB), lambda c: (0, c))],
        out_specs=[col, pl.BlockSpec((8, CONV_CB), lambda c: (0, c)), pl.BlockSpec((1, CONV_CB), lambda c: (0, c))],
        out_shape=[jax.ShapeDtypeStruct((S, CONV_DIM), bf16), jax.ShapeDtypeStruct((8, CONV_DIM), f32),
                   jax.ShapeDtypeStruct((1, CONV_DIM), f32)],
        scratch_shapes=[pltpu.VMEM((S + HALO, CONV_CB), f32)],
        compiler_params=_params(("parallel",)),
    )(proj, xc, dxbc, conv_w)


N_PAIR = SSD_HEADS // 2
NEG = -1e30


def _softplus(x):
    return jnp.maximum(x, 0.0) + jnp.log(1.0 + jnp.exp(-jnp.abs(x)))


def _ssd_common(dtr, dtb, alog):
    L = SSD_L
    r_i = lax.broadcasted_iota(jnp.int32, (L, L), 0)
    c_i = lax.broadcasted_iota(jnp.int32, (L, L), 1)
    dt = _softplus(dtr + dtb)
    a = -jnp.exp(alog)
    da = dt * a
    lower = (r_i >= c_i).astype(bf16)
    upper = (r_i <= c_i).astype(bf16)
    parts = _split3(da)
    a_cs = sum(_dot(lower, p) for p in parts)
    a_cs_t = sum(_dot(p, upper, TN) for p in parts)
    return dt, a, a_cs, a_cs_t, r_i >= c_i


def _pair_vec(lane, v, h):
    return jnp.where(lane < SB_HD, v[:, h:h + 1], v[:, h + 1:h + 2])


def _decay_mat(a_cs, a_cs_t, h, tril):
    return jnp.exp(jnp.where(tril, a_cs[:, h:h + 1] - a_cs_t[h:h + 1, :], NEG))


def _ssd_fwd(xbc, proj, pdt, dt_bias_p, a_log_p, d_skip_c, ssd_norm, S, ride=None):
    L = SSD_L
    nc = S // L
    n_r = ride.n if ride else 0

    def body(xbc_ref, dt_ref, z_ref, dtb_ref, alog_ref, dsk_ref, gn_ref, *rest):
        y_ref, yn_ref, hp_ref = rest[n_r:n_r + 3]
        state = rest[2 * n_r + 3]
        r_ins, r_lnd, r_sems = rest[:n_r], rest[n_r + 3:2 * n_r + 3], rest[2 * n_r + 4:]
        c = pl.program_id(0)
        if ride:
            pl.when(c == 0)(lambda: ride.start(r_ins, r_lnd, r_sems))

        @pl.when(c == 0)
        def _():
            state[...] = jnp.zeros_like(state)

        hp_ref[0] = state[...]
        lane = lax.broadcasted_iota(jnp.int32, (1, 128), 1)
        row128 = lax.broadcasted_iota(jnp.int32, (128, 1), 0)
        m_a, m_b = _sb_masks()
        dt, a, a_cs, a_cs_t, tril = _ssd_common(dt_ref[...], dtb_ref[...], alog_ref[...])
        a_last = a_cs[L - 1:L, :]
        for g in range(SSD_GROUPS):
            b_g = xbc_ref[:, SSD_INNER + g * SSD_N:SSD_INNER + (g + 1) * SSD_N].astype(bf16)
            c_g = xbc_ref[:, SSD_INNER + (SSD_GROUPS + g) * SSD_N:SSD_INNER + (SSD_GROUPS + g + 1) * SSD_N].astype(bf16)
            cb = _dot(c_g, b_g, NT)
            for pr in range(4):
                h = 8 * g + 2 * pr
                pi = h // 2
                cols = slice(pi * 128, (pi + 1) * 128)
                xs = xbc_ref[:, cols]
                x = xs * _pair_vec(lane, dt, h)
                acs = _pair_vec(lane, a_cs, h)
                al = _pair_vec(lane, a_last, h)
                w_a = (cb * _decay_mat(a_cs, a_cs_t, h, tril)).astype(bf16)
                w_b = (cb * _decay_mat(a_cs, a_cs_t, h + 1, tril)).astype(bf16)
                yd = _dot(w_a, (x * m_a).astype(bf16)) + _dot(w_b, (x * m_b).astype(bf16))
                hp = state[pi]
                yo = _dot(c_g, hp.astype(bf16), NT) * jnp.exp(acs)
                y_ref[:, cols] = yd + yo + dsk_ref[:, cols] * xs
                dec = jnp.exp(jnp.where(row128 < SB_HD, a_last[:, h:h + 1], a_last[:, h + 1:h + 2]))
                state[pi] = hp * dec + _dot((x * jnp.exp(al - acs)).astype(bf16), b_g, TN)
        zz = z_ref[...]
        y2 = y_ref[...] * (zz * _sigmoid(zz))
        gw = SSD_INNER // SSD_GROUPS
        for g in range(SSD_GROUPS):
            yg = y2[:, g * gw:(g + 1) * gw]
            rg = lax.rsqrt(jnp.mean(yg * yg, axis=1, keepdims=True) + EPS)
            yn_ref[:, g * gw:(g + 1) * gw] = (yg * rg * gn_ref[:, g * gw:(g + 1) * gw]).astype(bf16)
        if ride:
            pl.when(c == nc - 1)(lambda: ride.finish(r_ins, r_lnd, r_sems))

    vec128 = pl.BlockSpec((1, 128), lambda c: (0, 0))
    vecin = pl.BlockSpec((1, SSD_INNER), lambda c: (0, 0))
    rows = pl.BlockSpec((L, SSD_INNER), lambda c: (c, 0))
    out = pl.pallas_call(
        body, name="ssd_fwd", grid=(nc,),
        in_specs=[pl.BlockSpec((L, CONV_DIM), lambda c: (c, 0)),
                  pl.BlockSpec((L, 128), lambda c: (c, 0)),
                  pl.BlockSpec((L, SSD_INNER), lambda c: (c, P_Z // SSD_INNER)),
                  vec128, vec128, vecin, vecin] + (ride.in_specs if ride else []),
        out_specs=[rows, rows, pl.BlockSpec((1, N_PAIR, 128, SSD_N), lambda c: (c, 0, 0, 0))]
        + (ride.out_specs if ride else []),
        out_shape=[jax.ShapeDtypeStruct((S, SSD_INNER), f32), jax.ShapeDtypeStruct((S, SSD_INNER), bf16),
                   jax.ShapeDtypeStruct((nc, N_PAIR, 128, SSD_N), f32)] + (ride.out_shape if ride else []),
        scratch_shapes=[pltpu.VMEM((N_PAIR, 128, SSD_N), f32)] + (ride.scratch if ride else []),
        compiler_params=_params(("arbitrary",)),
    )(xbc, pdt, proj, dt_bias_p, a_log_p, d_skip_c, ssd_norm, *(ride.srcs if ride else []))
    return out[0], out[1], out[2], list(out[3:])


def _sum_all(v):
    return jnp.sum(jnp.sum(v, axis=1, keepdims=True), axis=0, keepdims=True)


def _ssd_bwd(dyn, y, xbc, proj, pdt, hprev, dt_bias_p, a_log_p, d_skip_c, ssd_norm, S, ride=None):
    L = SSD_L
    nc = S // L
    n_r = ride.n if ride else 0

    col = lax.broadcasted_iota(jnp.int32, (2 * SSD_INNER, 128), 0)
    head = lax.broadcasted_iota(jnp.int32, (2 * SSD_INNER, 128), 1)
    sel_pair = (col[:SSD_INNER] // SB_HD == head[:SSD_INNER]).astype(bf16)
    sel_head = (col // 128 == head).astype(bf16)

    def body(*refs):
        (dyn_ref, y_ref, xbc_ref, dt_ref, z_ref, hp_ref, dtb_ref, alog_ref, dsk_ref, gn_ref,
         selp_ref, selh_ref) = refs[:12]
        dz_ref, dxbc_ref, ddt_ref, dgn_ref, dsk_out, dalog_ref, ddtb_ref = refs[12 + n_r:19 + n_r]
        dstate, dy_s, st_a, st_q, st_d, st_x, dat = refs[19 + 2 * n_r:26 + 2 * n_r]
        r_ins, r_lnd, r_sems = refs[12:12 + n_r], refs[19 + n_r:19 + 2 * n_r], refs[26 + 2 * n_r:]
        c = pl.program_id(0)
        if ride:
            pl.when(c == 0)(lambda: ride.start(r_ins, r_lnd, r_sems))

        @pl.when(c == 0)
        def _():
            dat[...] = jnp.zeros_like(dat)
            dstate[...] = jnp.zeros_like(dstate)
            dgn_ref[...] = jnp.zeros_like(dgn_ref)
            dsk_out[...] = jnp.zeros_like(dsk_out)
            dalog_ref[...] = jnp.zeros_like(dalog_ref)
            ddtb_ref[...] = jnp.zeros_like(ddtb_ref)

        lane = lax.broadcasted_iota(jnp.int32, (1, 128), 1)
        row128 = lax.broadcasted_iota(jnp.int32, (128, 1), 0)
        rowl = lax.broadcasted_iota(jnp.int32, (L, 1), 0)
        m_a, m_b = _sb_masks()
        dtr = dt_ref[...]
        dt, a, a_cs, a_cs_t, tril = _ssd_common(dtr, dtb_ref[...], alog_ref[...])
        a_last = a_cs[L - 1:L, :]

        zz = z_ref[...]
        sg = _sigmoid(zz)
        silu = zz * sg
        yv = y_ref[...]
        y2 = yv * silu
        gw = SSD_INNER // SSD_GROUPS
        for g in range(SSD_GROUPS):
            sl = slice(g * gw, (g + 1) * gw)
            yg = y2[:, sl]
            rg = lax.rsqrt(jnp.mean(yg * yg, axis=1, keepdims=True) + EPS)
            yh = yg * rg
            dyn_g = dyn_ref[:, sl]
            dgn_ref[:, sl] += jnp.sum(dyn_g * yh, axis=0, keepdims=True)
            dyh = dyn_g * gn_ref[:, sl]
            dy2 = rg * (dyh - yh * jnp.mean(dyh * yh, axis=1, keepdims=True))
            dy_s[:, sl] = dy2 * silu[:, sl]
            dz_ref[:, sl] = (dy2 * yv[:, sl] * (sg[:, sl] * (1.0 + zz[:, sl] * (1.0 - sg[:, sl])))).astype(bf16)

        last_row = jnp.zeros((1, 128), f32)
        dsk_acc = jnp.zeros((1, 128), f32)
        for g in range(SSD_GROUPS):
            bsl = slice(SSD_INNER + g * SSD_N, SSD_INNER + (g + 1) * SSD_N)
            csl = slice(SSD_INNER + (SSD_GROUPS + g) * SSD_N, SSD_INNER + (SSD_GROUPS + g + 1) * SSD_N)
            b_g = xbc_ref[:, bsl].astype(bf16)
            c_g = xbc_ref[:, csl].astype(bf16)
            cb = _dot(c_g, b_g, NT)
            dcb = jnp.zeros((L, L), f32)
            dc_g = jnp.zeros((L, SSD_N), f32)
            db_g = jnp.zeros((L, SSD_N), f32)
            for pr in range(4):
                h = 8 * g + 2 * pr
                pi = h // 2
                cols = slice(pi * 128, (pi + 1) * 128)
                xs = xbc_ref[:, cols]
                dt_p = _pair_vec(lane, dt, h)
                x = xs * dt_p
                acs = _pair_vec(lane, a_cs, h)
                al = _pair_vec(lane, a_last, h)
                e_a = jnp.exp(acs)
                dte = jnp.exp(al - acs)
                m_mat_a = _decay_mat(a_cs, a_cs_t, h, tril)
                m_mat_b = _decay_mat(a_cs, a_cs_t, h + 1, tril)
                dyp = dy_s[:, cols]
                dsk = dsk_ref[:, cols]
                d_hn = dstate[pi]
                hp = hp_ref[0, pi]
                dy_a = (dyp * m_a).astype(bf16)
                dy_b = (dyp * m_b).astype(bf16)
                x_b = x.astype(bf16)
                gm_a = _dot(dy_a, x_b, NT) * m_mat_a
                gm_b = _dot(dy_b, x_b, NT) * m_mat_b
                dcb = dcb + gm_a + gm_b
                dx_d = _dot((cb * m_mat_a).astype(bf16), dy_a, TN) + _dot((cb * m_mat_b).astype(bf16), dy_b, TN)
                dx_s = _dot(b_g, d_hn.astype(bf16), NT) * dte
                dx = dx_d + dx_s
                dxbc_ref[:, cols] = dx * dt_p + dsk * dyp
                xdxs = x * dx_s
                st_x[:, cols] = xdxs
                st_a[:, cols] = dyp * (_dot(c_g, hp.astype(bf16), NT) * e_a) - xdxs
                st_d[:, cols] = dx * xs
                hh = d_hn * hp
                dsk_row = jnp.sum(dyp * xs, axis=0, keepdims=True)
                dec = jnp.exp(jnp.where(row128 < SB_HD, a_last[:, h:h + 1], a_last[:, h + 1:h + 2]))
                for hd, m, gm in ((h, m_a, gm_a), (h + 1, m_b, gm_b)):
                    half = slice(0, SB_HD) if hd == h else slice(SB_HD, 128)
                    qm = gm * cb
                    st_q[:, hd * 128:(hd + 1) * 128] = qm
                    dat[hd:hd + 1, :] = jnp.sum(qm, axis=0, keepdims=True)
                    hh_sum = jnp.sum(jnp.sum(hh[half, :], axis=0, keepdims=True), axis=1, keepdims=True)
                    last_row = jnp.where(lane == hd, jnp.exp(a_last[:, hd:hd + 1]) * hh_sum, last_row)
                    dsk_acc = jnp.where(lane == hd, jnp.sum(dsk_row * m, axis=1, keepdims=True), dsk_acc)
                dye = (dyp * e_a).astype(bf16)
                dc_g = dc_g + _dot(dye, hp.astype(bf16))
                db_g = db_g + _dot((x * dte).astype(bf16), d_hn.astype(bf16))
                dstate[pi] = dec * d_hn + _dot(dye, c_g, TN)
            dcb_b = dcb.astype(bf16)
            dxbc_ref[:, csl] = dc_g + _dot(dcb_b, b_g)
            dxbc_ref[:, bsl] = db_g + _dot(dcb_b, c_g, TN)

        r_i = lax.broadcasted_iota(jnp.int32, (L, L), 0)
        c_i = lax.broadcasted_iota(jnp.int32, (L, L), 1)
        rev = (r_i <= c_i).astype(bf16)

        def head_sums(st, sel, split=_split2):
            return sum(_dot(p, sel[...]) for p in split(st[...]))

        last_row = last_row + jnp.sum(head_sums(st_x, selp_ref), axis=0, keepdims=True)
        d_acs = (head_sums(st_a, selp_ref) + head_sums(st_q, selh_ref, _split3)
                 + jnp.where(rowl == L - 1, last_row, 0.0))
        ddt_x = head_sums(st_d, selp_ref)
        dda = sum(_dot(rev, p) for p in _split3(d_acs)) - sum(_dot(rev, p, NT) for p in _split3(dat[...]))
        ddt = ddt_x + dda * a
        dalog_ref[...] += jnp.sum(dda * dt, axis=0, keepdims=True) * a
        ddtr = jnp.where(lane < SSD_HEADS, ddt * _sigmoid(dtr + dtb_ref[...]), 0.0)
        ddt_ref[...] = ddtr.astype(bf16)
        ddtb_ref[...] += jnp.sum(ddtr, axis=0, keepdims=True)
        dsk_out[...] += dsk_acc
        if ride:
            pl.when(c == nc - 1)(lambda: ride.finish(r_ins, r_lnd, r_sems))

    rv = lambda c: nc - 1 - c
    vec128 = pl.BlockSpec((1, 128), lambda c: (0, 0))
    vecin = pl.BlockSpec((1, SSD_INNER), lambda c: (0, 0))
    rows = pl.BlockSpec((L, SSD_INNER), lambda c: (rv(c), 0))
    return pl.pallas_call(
        body, name="ssd_bwd", grid=(nc,),
        in_specs=[rows, rows,
                  pl.BlockSpec((L, CONV_DIM), lambda c: (rv(c), 0)),
                  pl.BlockSpec((L, 128), lambda c: (rv(c), 0)),
                  pl.BlockSpec((L, SSD_INNER), lambda c: (rv(c), P_Z // SSD_INNER)),
                  pl.BlockSpec((1, N_PAIR, 128, SSD_N), lambda c: (rv(c), 0, 0, 0)),
                  vec128, vec128, vecin, vecin,
                  pl.BlockSpec((SSD_INNER, 128), lambda c: (0, 0)),
                  pl.BlockSpec((2 * SSD_INNER, 128), lambda c: (0, 0))] + (ride.in_specs if ride else []),
        out_specs=[rows, pl.BlockSpec((L, CONV_DIM), lambda c: (rv(c), 0)),
                   pl.BlockSpec((L, 128), lambda c: (rv(c), 0)), vecin, vec128, vec128, vec128]
        + (ride.out_specs if ride else []),
        out_shape=[jax.ShapeDtypeStruct((S, SSD_INNER), bf16), jax.ShapeDtypeStruct((S, CONV_DIM), f32),
                   jax.ShapeDtypeStruct((S, 128), bf16), jax.ShapeDtypeStruct((1, SSD_INNER), f32),
                   jax.ShapeDtypeStruct((1, 128), f32), jax.ShapeDtypeStruct((1, 128), f32),
                   jax.ShapeDtypeStruct((1, 128), f32)] + (ride.out_shape if ride else []),
        scratch_shapes=[pltpu.VMEM((N_PAIR, 128, SSD_N), f32), pltpu.VMEM((L, SSD_INNER), f32),
                        pltpu.VMEM((L, SSD_INNER), f32), pltpu.VMEM((L, 2 * SSD_INNER), f32),
                        pltpu.VMEM((L, SSD_INNER), f32), pltpu.VMEM((L, SSD_INNER), f32),
                        pltpu.VMEM((128, L), f32)]
        + (ride.scratch if ride else []),
        compiler_params=_params(("arbitrary",)),
    )(dyn, y, xbc, pdt, proj, hprev, dt_bias_p, a_log_p, d_skip_c, ssd_norm, sel_pair, sel_head,
      *(ride.srcs if ride else []))


MEM_W = MEM_HEADS * MEM_HD


def _mem_probs(q, k):
    s = _dot(q, k, NT) * (MEM_HD ** -0.5)
    s = s - jnp.max(s, axis=1, keepdims=True)
    p = jnp.exp(s)
    return p / jnp.sum(p, axis=1, keepdims=True)


def _mem_fwd(proj, kv, S, tm=512):
    tm = min(tm, S)
    M = kv.shape[0]

    def body(q_ref, kv_ref, o_ref):
        for h in range(MEM_HEADS):
            sl = slice(h * MEM_HD, (h + 1) * MEM_HD)
            vsl = slice(MEM_W + h * MEM_HD, MEM_W + (h + 1) * MEM_HD)
            p = _mem_probs(q_ref[:, sl].astype(bf16), kv_ref[:, sl].astype(bf16))
            o_ref[:, sl] = _dot(p.astype(bf16), kv_ref[:, vsl].astype(bf16)).astype(bf16)

    return pl.pallas_call(
        body, name="mem_fwd", grid=(S // tm,),
        in_specs=[pl.BlockSpec((tm, MEM_W), lambda i: (i, P_MEMQ // MEM_W)),
                  pl.BlockSpec((M, 2 * MEM_W), lambda i: (0, 0))],
        out_specs=pl.BlockSpec((tm, MEM_W), lambda i: (i, 0)),
        out_shape=jax.ShapeDtypeStruct((S, MEM_W), bf16),
        compiler_params=_params(("parallel",)),
    )(proj, kv)


def _mem_bwd(proj, kv, dy, S, tm=512):
    tm = min(tm, S)
    M = kv.shape[0]
    scale = MEM_HD ** -0.5

    def body(q_ref, kv_ref, dy_ref, dq_ref, dkv_ref):
        @pl.when(pl.program_id(0) == 0)
        def _():
            dkv_ref[...] = jnp.zeros_like(dkv_ref)

        for h in range(MEM_HEADS):
            sl = slice(h * MEM_HD, (h + 1) * MEM_HD)
            vsl = slice(MEM_W + h * MEM_HD, MEM_W + (h + 1) * MEM_HD)
            q = q_ref[:, sl].astype(bf16)
            k = kv_ref[:, sl].astype(bf16)
            v = kv_ref[:, vsl].astype(bf16)
            dyh = dy_ref[:, sl].astype(bf16)
            p = _mem_probs(q, k)
            dp = _dot(dyh, v, NT)
            ds = (p * (dp - jnp.sum(dp * p, axis=1, keepdims=True)) * scale).astype(bf16)
            dq_ref[:, sl] = _dot(ds, k).astype(bf16)
            dkv_ref[:, sl] += _dot(ds, q, TN)
            dkv_ref[:, vsl] += _dot(p.astype(bf16), dyh, TN)

    return pl.pallas_call(
        body, name="mem_bwd", grid=(S // tm,),
        in_specs=[pl.BlockSpec((tm, MEM_W), lambda i: (i, P_MEMQ // MEM_W)),
                  pl.BlockSpec((M, 2 * MEM_W), lambda i: (0, 0)),
                  pl.BlockSpec((tm, MEM_W), lambda i: (i, 0))],
        out_specs=[pl.BlockSpec((tm, MEM_W), lambda i: (i, 0)), pl.BlockSpec((M, 2 * MEM_W), lambda i: (0, 0))],
        out_shape=[jax.ShapeDtypeStruct((S, MEM_W), bf16), jax.ShapeDtypeStruct((M, 2 * MEM_W), f32)],
        compiler_params=_params(("arbitrary",)),
    )(proj, kv, dy)


def _merge_fwd(proj, t0, t1, t2, S, tm=512):
    tm = min(tm, S)

    def body(g_ref, t0_ref, t1_ref, t2_ref, o_ref):
        acc = jnp.zeros((tm, D), f32)
        for b, t_ref in enumerate((t0_ref, t1_ref, t2_ref)):
            acc = acc + _sigmoid(g_ref[:, b * D:(b + 1) * D]) * t_ref[...]
        o_ref[...] = acc.astype(bf16)

    row = pl.BlockSpec((tm, D), lambda i: (i, 0))
    return pl.pallas_call(
        body, name="merge_fwd", grid=(S // tm,),
        in_specs=[pl.BlockSpec((tm, 3 * D), lambda i: (i, P_GATE // (3 * D))), row, row, row],
        out_specs=row, out_shape=jax.ShapeDtypeStruct((S, D), bf16),
        compiler_params=_params(("parallel",)),
    )(proj, t0, t1, t2)


def _merge_bwd(proj, t0, t1, t2, dm, S, tm=512):
    tm = min(tm, S)

    def body(g_ref, t0_ref, t1_ref, t2_ref, dm_ref, d0_ref, d1_ref, d2_ref, dg_ref):
        dmv = dm_ref[...]
        for b, (t_ref, d_ref) in enumerate(((t0_ref, d0_ref), (t1_ref, d1_ref), (t2_ref, d2_ref))):
            sg = _sigmoid(g_ref[:, b * D:(b + 1) * D])
            d_ref[...] = (dmv * sg).astype(bf16)
            dg_ref[:, b * D:(b + 1) * D] = (dmv * t_ref[...] * sg * (1.0 - sg)).astype(bf16)

    row = pl.BlockSpec((tm, D), lambda i: (i, 0))
    return pl.pallas_call(
        body, name="merge_bwd", grid=(S // tm,),
        in_specs=[pl.BlockSpec((tm, 3 * D), lambda i: (i, P_GATE // (3 * D))), row, row, row, row],
        out_specs=[row, row, row, pl.BlockSpec((tm, 3 * D), lambda i: (i, 0))],
        out_shape=[jax.ShapeDtypeStruct((S, D), bf16)] * 3 + [jax.ShapeDtypeStruct((S, 3 * D), bf16)],
        compiler_params=_params(("parallel",)),
    )(proj, t0, t1, t2, dm)


def _loss_head(ff, g, h1, target, S, tm=512):
    tm = min(tm, S)

    def body(ff_ref, g_ref, h1_ref, t_ref, dh_ref, loss_ref):
        xv = ff_ref[...]
        r = lax.rsqrt(jnp.mean(xv * xv, axis=1, keepdims=True) + EPS)
        err = h1_ref[...] + xv * r * g_ref[...] - t_ref[...]
        dh_ref[...] = err * (1.0 / D)

        @pl.when(pl.program_id(0) == 0)
        def _():
            loss_ref[...] = jnp.zeros_like(loss_ref)

        loss_ref[...] += 0.5 * _sum_all(jnp.mean(err * err, axis=1, keepdims=True)) * jnp.ones((1, 128), f32)

    row = pl.BlockSpec((tm, D), lambda i: (i, 0))
    return pl.pallas_call(
        body, name="loss_head", grid=(S // tm,),
        in_specs=[row, pl.BlockSpec((1, D), lambda i: (0, 0)), row, row],
        out_specs=[row, pl.BlockSpec((1, 128), lambda i: (0, 0))],
        out_shape=[jax.ShapeDtypeStruct((S, D), f32), jax.ShapeDtypeStruct((1, 128), f32)],
        compiler_params=_params(("arbitrary",)),
    )(ff, g, h1, target)


def _local_step(x, mem, target, wts, late_rides, late_weights, small, rest_rides, w_in_ride):
    S = x.shape[0]
    M = mem.shape[0]
    pad = lambda v: jnp.pad(v, ((0, 0), (0, 128 - SSD_HEADS)))
    dtb_p, alog_p = pad(small["dt_bias"]), pad(small["a_log"])
    dsk_c = jnp.repeat(small["d_skip"], SB_HD, axis=1)

    u = _rms_fwd(x, small["norm_mix_pre"], name="norm_pre", out_dtype=bf16)
    rides = late_rides or (None, None, None)
    if late_rides:
        proj, lands_a = _mm(u, wts["w_main"], "nn", tm=1024, tn=1024, name="in_proj", ride=rides[0])
    else:
        proj, lands_a = _mm(u, wts["w_main"], "nn", tm=1024, tn=1024, name="in_proj"), []
    pdt = _mm(u, wts["w_dt"], "nn", tm=1024, tn=128, name="in_proj_dt")
    y_sb, tot_lk, lands_b = _sb_fwd(proj, S, rides[1])
    wts = dict(wts, **late_weights(0, lands_a))
    small = dict(small, conv_w=wts.pop("conv_w"))
    xc, xbc = _conv_fwd(proj, small["conv_w"], small["conv_b"], S)
    y_ssd, yn, hprev, lands_c = _ssd_fwd(xbc, proj, pdt, dtb_p, alog_p, dsk_c, small["ssd_norm"], S, rides[2])
    wts = dict(wts, **late_weights(1, lands_b), **late_weights(2, lands_c))
    mn = _rms_fwd(mem, small["norm_mem"], name="norm_mem", out_dtype=bf16, tm=min(512, M))
    kv = _mm(mn, wts["w_mem_kv"], "nn", tm=M, tn=1024, name="mem_kv")
    y_mem = _mem_fwd(proj, kv, S)
    t0 = _mm(y_sb, wts["w_sb_out"], "nn", tm=1024, tn=1024, name="sb_out")
    t1 = _mm(yn, wts["w_ssd_out"], "nn", tm=1024, tn=1024, name="ssd_out")
    t2 = _mm(y_mem, wts["w_mem_out"], "nn", tm=1024, tn=1024, name="mem_out")
    merged = _merge_fwd(proj, t0, t1, t2, S)
    mix = _mm(merged, wts["w_o"], "nn", tm=1024, tn=1024, name="w_o")
    h1 = _rms_fwd(mix, small["norm_mix_post"], name="norm_mix_post", out_dtype=f32, residual=x)
    u2 = _rms_fwd(h1, small["norm_mlp_pre"], name="norm_mlp_pre", out_dtype=bf16)
    a_up, hrelu = _mm(u2, wts["w_up"], "nn", tm=1024, tn=1024, name="mlp_up", out_dtypes=(f32, bf16),
                      epi=lambda acc: (acc, jnp.square(jnp.maximum(acc, 0.0))))
    ff = _mm(hrelu, wts["w_down"], "nn", tm=1024, tn=1024, name="mlp_down")
    dh2, loss = _loss_head(ff, small["norm_mlp_post"], h1, target, S)

    g = {}
    dff, g["norm_mlp_post"] = _rms_bwd(ff, dh2, small["norm_mlp_post"], name="norm_mlp_post_bwd", out_dtype=bf16)
    da = _mm(dff, wts["w_down"], "nt", tm=1024, tn=1024, name="mlp_down_dx", out_dtypes=(bf16,),
             epi=lambda acc, a: (acc * (2.0 * jnp.maximum(a, 0.0)),), extras=(a_up,))
    g["w_down"] = _mm(hrelu, dff, "tn", tm=1024, tn=1024, name="mlp_down_dw")
    du2 = _mm(da, wts["w_up"], "nt", tm=1024, tn=1024, name="mlp_up_dx")
    g["w_up"] = _mm(u2, da, "tn", tm=1024, tn=1024, name="mlp_up_dw")
    dh1, g["norm_mlp_pre"] = _rms_bwd(h1, du2, small["norm_mlp_pre"], name="norm_mlp_pre_bwd", out_dtype=f32, add=dh2)
    dmix, g["norm_mix_post"] = _rms_bwd(mix, dh1, small["norm_mix_post"], name="norm_mix_post_bwd", out_dtype=bf16)
    dmerged = _mm(dmix, wts["w_o"], "nt", tm=1024, tn=1024, name="w_o_dx")
    g["w_o"] = _mm(merged, dmix, "tn", tm=1024, tn=1024, name="w_o_dw")
    dt0, dt1, dt2, dgl = _merge_bwd(proj, t0, t1, t2, dmerged, S)
    dy_sb = _mm(dt0, wts["w_sb_out"], "nt", tm=1024, tn=1024, name="sb_out_dx")
    g["w_sb_out"] = _mm(y_sb, dt0, "tn", tm=1024, tn=1024, name="sb_out_dw")
    dy_ssd = _mm(dt1, wts["w_ssd_out"], "nt", tm=1024, tn=1024, name="ssd_out_dx")
    g["w_ssd_out"] = _mm(yn, dt1, "tn", tm=1024, tn=1024, name="ssd_out_dw")
    dy_mem = _mm(dt2, wts["w_mem_out"], "nt", tm=1024, tn=1024, name="mem_out_dx")
    g["w_mem_out"] = _mm(y_mem, dt2, "tn", tm=1024, tn=1024, name="mem_out_dw")
    dmemq, dkv = _mem_bwd(proj, kv, dy_mem, S)
    g["w_mem_kv"] = _mm(mn, dkv, "tn", tm=1024, tn=1024, name="mem_kv_dw")
    dmn = _mm(dkv, wts["w_mem_kv"], "nt", tm=M, tn=1024, name="mem_kv_dx")
    _, g["norm_mem"] = _rms_bwd(mem, dmn, small["norm_mem"], name="norm_mem_bwd", out_dtype=bf16, tm=min(512, M))
    rides = rest_rides(g) if rest_rides else (None, None)
    dz, dxbc, ddt, g["ssd_norm"], dsk, dalog, ddtb, *lands_a = _ssd_bwd(
        dy_ssd, y_ssd, xbc, proj, pdt, hprev, dtb_p, alog_p, dsk_c, small["ssd_norm"], S, rides[0])
    g["d_skip"], g["a_log"], g["dt_bias"] = dsk[:, :SSD_HEADS], dalog[:, :SSD_HEADS], ddtb[:, :SSD_HEADS]
    dxbc_raw, dcw, g["conv_b"] = _conv_bwd(proj, xc, dxbc, small["conv_w"], S)
    g["conv_w"] = dcw[:CONV_K]
    dq, dk, dv, lands_b = _sb_bwd(proj, tot_lk, dy_sb, S, rides[1])
    g["rest_lands"] = lands_b + lands_a
    dproj = (dq, dk, dv, dxbc_raw, dgl, dmemq, dz)
    g["w_main"] = [_mm(u, p, "tn", tm=1024, tn=1024, name="in_proj_dw_%d" % i) for i, p in enumerate(dproj)]
    g["w_dt"] = _mm(u, ddt, "tn", tm=1024, tn=128, name="in_proj_dt_dw")
    du_dt = _mm(ddt, wts["w_dt"], "nt", tm=1024, tn=1024, name="in_proj_dt_dx")
    du, g["w_in_lands"] = _mm_pieces_nt(dproj, wts["w_main"], du_dt, tm=512, tn=256, name="in_proj_dx",
                                        ride=w_in_ride(g) if w_in_ride else None)
    grad_x, g["norm_mix_pre"] = _rms_bwd(x, du, small["norm_mix_pre"], name="norm_pre_bwd", out_dtype=f32, add=dh1)
    return loss, grad_x, g


def _to_internal(w_in):
    sec = lambda r: w_in[:, r[0]:r[1]]
    w_main = jnp.concatenate([sec(R_QKV), sec(R_XBC), sec(R_GATE), sec(R_MEMQ), sec(R_Z)], axis=1)
    w_dt = jnp.pad(sec(R_DT), ((0, 0), (0, 128 - SSD_HEADS)))
    return w_main, w_dt


def _from_internal(pieces, g_dt):
    dq, dk, dv, dxbc, dgate, dmemq, dz = pieces
    return [dq, dk, dv, dz, dxbc, g_dt[:, :SSD_HEADS], dmemq, dgate]


def _w_in_slab(ordered, s, dtype):
    width = D_IN // N_SHARD
    lo, hi, off, parts = s * width, (s + 1) * width, 0, []
    for p in ordered:
        a, b = max(lo, off), min(hi, off + p.shape[1])
        if a < b:
            parts.append(p[:, a - off:b - off].astype(dtype))
        off += p.shape[1]
    return jnp.concatenate(parts, axis=1)


MESH = pl.DeviceIdType.MESH
ANY = pl.BlockSpec(memory_space=pl.ANY)


def _place():
    x, y, c = lax.axis_index("x"), lax.axis_index("y"), lax.axis_index("c")
    return (x, y, c), [(1 - x, y, c), (x, 1 - y, c), (1 - x, 1 - y, c)]


def _exchange_copy(mode, ins, lands, send, recv, a, k, me, peers, arriving):
    p = peers[k]
    theirs = 2 * p[0] + p[1]
    if mode == "gather":
        src, dst = ins[a], lands[a].at[theirs if arriving else me]
    else:
        src, dst = ins[a].at[theirs], lands[a].at[k]
    return pltpu.make_async_remote_copy(src_ref=src, dst_ref=dst, send_sem=send.at[a * 3 + k],
                                        recv_sem=recv.at[a * 3 + k], device_id=p, device_id_type=MESH)


class _Ride:
    def __init__(self, srcs, mode):
        self.srcs, self.mode, self.n = list(srcs), mode, len(srcs)
        n = self.n
        self.in_specs, self.out_specs = [ANY] * n, [ANY] * n
        self.out_shape = [
            jax.ShapeDtypeStruct((N_SHARD,) + s.shape if mode == "gather" else (3,) + s.shape[1:], s.dtype)
            for s in self.srcs]
        self.scratch = [pltpu.SemaphoreType.DMA((3 * n,)), pltpu.SemaphoreType.DMA((3 * n,)),
                        pltpu.SemaphoreType.DMA((n,))]

    def _own(self, ins, lnd, sems):
        if self.mode != "gather":
            return []
        me = 2 * lax.axis_index("x") + lax.axis_index("y")
        return [pltpu.make_async_copy(ins[a], lnd[a].at[me], sems[2].at[a]) for a in range(self.n)]

    def _far(self, ins, lnd, sems, arriving):
        (x, y, c), peers = _place()
        return [_exchange_copy(self.mode, ins, lnd, sems[0], sems[1], a, k, 2 * x + y, peers, arriving)
                for a in range(self.n) for k in range(3)]

    def start(self, ins, lnd, sems):
        for cp in self._own(ins, lnd, sems) + self._far(ins, lnd, sems, False):
            cp.start()

    def finish(self, ins, lnd, sems):
        for cp in self._far(ins, lnd, sems, True):
            cp.wait_recv()
        for cp in self._far(ins, lnd, sems, False):
            cp.wait_send()
        for cp in self._own(ins, lnd, sems):
            cp.wait()


def _gather_two_level(shards, name):
    n = len(shards)

    def body(*refs):
        ins, lnd = refs[:n], refs[n:2 * n]
        send, recv, loc = refs[2 * n:]
        (x, y, c), peers = _place()
        me = 2 * x + y

        def half(ref, a, core):
            rows = shards[a].shape[0] // 2
            return ref.at[pl.ds(core * rows, rows)]

        def copy(a, j, slot, core, to):
            return pltpu.make_async_remote_copy(
                src_ref=half(ins[a], a, core) if j < 3 else half(lnd[a].at[slot], a, core),
                dst_ref=half(lnd[a].at[slot], a, core), send_sem=send.at[6 * a + j], recv_sem=recv.at[6 * a + j],
                device_id=to, device_id_type=MESH)

        own = [pltpu.make_async_copy(ins[a], lnd[a].at[me], loc.at[a]) for a in range(n)]
        far = [copy(a, k, me, c, peers[k]) for a in range(n) for k in range(3)]
        for cp in own + far:
            cp.start()
        passed = []
        for a in range(n):
            for k, p in enumerate(peers):
                theirs = 2 * p[0] + p[1]
                copy(a, k, theirs, c, p).wait_recv()
                passed.append(copy(a, 3 + k, theirs, c, (x, y, 1 - c)))
                passed[-1].start()
        for a in range(n):
            for k, p in enumerate(peers):
                copy(a, 3 + k, 2 * p[0] + p[1], 1 - c, (x, y, 1 - c)).wait_recv()
        for cp in far + passed:
            cp.wait_send()
        for cp in own:
            cp.wait()

    return pl.pallas_call(
        body, name=name, in_specs=[ANY] * n, out_specs=[ANY] * n,
        out_shape=[jax.ShapeDtypeStruct((N_SHARD,) + s.shape, s.dtype) for s in shards],
        scratch_shapes=[pltpu.SemaphoreType.DMA((6 * n,)), pltpu.SemaphoreType.DMA((6 * n,)),
                        pltpu.SemaphoreType.DMA((n,))],
    )(*shards)


def _exchange_packets(packet):
    def body(pk, pk_out, send, recv, loc):
        x, y, c = lax.axis_index("x"), lax.axis_index("y"), lax.axis_index("c")
        lin = 4 * x + 2 * y + c
        own = pltpu.make_async_copy(pk, pk_out.at[lin], loc.at[0])
        own.start()

        def pk_copy(m, slot):
            dev = (x ^ ((m >> 2) & 1), y ^ ((m >> 1) & 1), c ^ (m & 1))
            return pltpu.make_async_remote_copy(
                src_ref=pk, dst_ref=pk_out.at[slot], send_sem=send.at[m - 1], recv_sem=recv.at[m - 1],
                device_id=dev, device_id_type=MESH)

        sent = [pk_copy(m, lin) for m in range(1, N_DEV)]
        for cp in sent:
            cp.start()
        for m in range(1, N_DEV):
            pk_copy(m, lin ^ m).wait_recv()
        for cp in sent:
            cp.wait_send()
        own.wait()

    return pl.pallas_call(
        body, name="exchange_packets", in_specs=[ANY], out_specs=ANY,
        out_shape=jax.ShapeDtypeStruct((N_DEV,) + packet.shape, packet.dtype),
        scratch_shapes=[pltpu.SemaphoreType.DMA((N_DEV - 1,)), pltpu.SemaphoreType.DMA((N_DEV - 1,)),
                        pltpu.SemaphoreType.DMA((1,))],
    )(packet)


def _swap_sibling(parts, name):
    n = len(parts)

    def body(*refs):
        ins, outs = refs[:n], refs[n:2 * n]
        send, recv = refs[2 * n:]
        x, y, c = lax.axis_index("x"), lax.axis_index("y"), lax.axis_index("c")
        cps = [pltpu.make_async_remote_copy(
            src_ref=ins[a], dst_ref=outs[a], send_sem=send.at[a], recv_sem=recv.at[a],
            device_id=(x, y, 1 - c), device_id_type=MESH) for a in range(n)]
        for cp in cps:
            cp.start()
        for cp in cps:
            cp.wait_recv()
        for cp in cps:
            cp.wait_send()

    return pl.pallas_call(
        body, name=name,
        in_specs=[ANY] * n, out_specs=[ANY] * n,
        out_shape=[jax.ShapeDtypeStruct(p.shape, p.dtype) for p in parts],
        scratch_shapes=[pltpu.SemaphoreType.DMA((n,)), pltpu.SemaphoreType.DMA((n,))],
    )(*parts)


BLOCK_ELEMS = 256 * 1024


def _row_tile(R, C):
    tr = max(8, (BLOCK_ELEMS // C) // 8 * 8)
    while R % tr:
        tr -= 8
    return min(tr, R)


def _sum_parts(own, stack, name, out_dtype=f32):
    k = stack.shape[0]
    R, C = stack.shape[1:]
    tr = _row_tile(R, C)

    def body(*refs):
        o_ref = refs[-1]
        acc = refs[0][...].astype(f32)
        for r in refs[1:-1]:
            acc = acc + r[...].astype(f32)
        o_ref[...] = acc.astype(out_dtype)

    row = pl.BlockSpec((tr, C), lambda i: (i, 0))
    specs = ([row] if own is not None else []) + [
        pl.BlockSpec((None, tr, C), functools.partial(lambda i, j: (j, i, 0), j=j)) for j in range(k)]
    args = ([own] if own is not None else []) + [stack] * k
    return pl.pallas_call(
        body, name=name, grid=(R // tr,), in_specs=specs, out_specs=row,
        out_shape=jax.ShapeDtypeStruct((R, C), out_dtype), compiler_params=_params(("parallel",)),
    )(*args)


def _adamw(w, m, v, g_parts, name):
    R, C = w.shape
    tr = _row_tile(R, C)
    n_g = len(g_parts)

    def body(w_ref, m_ref, v_ref, *rest):
        g = rest[0][...]
        for r in rest[1:n_g]:
            g = g + r[...]
        g_ref, d_ref, nm_ref, nv_ref = rest[n_g:]
        nm = ADAM_B1 * m_ref[...] + (1.0 - ADAM_B1) * g
        nv = ADAM_B2 * v_ref[...] + (1.0 - ADAM_B2) * jnp.square(g)
        m_hat = nm / (1.0 - ADAM_B1 ** ADAM_STEP)
        v_hat = nv / (1.0 - ADAM_B2 ** ADAM_STEP)
        g_ref[...] = g
        d_ref[...] = -ADAM_LR * (m_hat / (jnp.sqrt(v_hat) + ADAM_EPS) + ADAM_WD * w_ref[...])
        nm_ref[...] = nm
        nv_ref[...] = nv

    row = pl.BlockSpec((tr, C), lambda i: (i, 0))
    return pl.pallas_call(
        body, name=name, grid=(R // tr,), in_specs=[row] * (3 + n_g), out_specs=[row] * 4,
        out_shape=[jax.ShapeDtypeStruct((R, C), f32)] * 4, compiler_params=_params(("parallel",)),
    )(w, m, v, *g_parts)


BIG = ("w_in", "w_mem_kv", "w_sb_out", "w_ssd_out", "w_mem_out", "w_o", "w_up", "w_down")
FIRST = ("w_in", "w_mem_kv")
LATE = ("w_sb_out", "w_ssd_out", "w_mem_out", "w_o", "w_up", "w_down")
REST = BIG[1:]
COL_SHARDED = ("w_in", "w_mem_kv", "w_up")
SMALL = ("norm_mix_pre", "conv_w", "conv_b", "dt_bias", "a_log", "d_skip", "ssd_norm", "norm_mem",
         "norm_mix_post", "norm_mlp_pre", "norm_mlp_post")
WEIGHTS = ("norm_mix_pre", "w_in", "conv_w", "conv_b", "dt_bias", "a_log", "d_skip", "ssd_norm", "norm_mem",
           "w_mem_kv", "w_sb_out", "w_ssd_out", "w_mem_out", "w_o", "norm_mix_post", "norm_mlp_pre", "w_up",
           "w_down", "norm_mlp_post")
PK_ROWS = 184


def _pack(vecs):
    flat = jnp.concatenate([v.reshape(-1) for v in vecs])
    return jnp.pad(flat, (0, PK_ROWS * 128 - flat.shape[0])).reshape(PK_ROWS, 128)


def _unpack(pk, shapes):
    flat = pk.reshape(-1)
    out, off = [], 0
    for s in shapes:
        n = 1
        for d in s:
            n *= d
        out.append(flat[off:off + n].reshape(s))
        off += n
    return out


def _full_from_slabs(name, slabs):
    if name in COL_SHARDED:
        return slabs.transpose(1, 0, 2).reshape(slabs.shape[1], -1)
    return slabs.reshape(-1, slabs.shape[2])


def _slabs_from_full(name, g):
    if name in COL_SHARDED:
        return g.reshape(g.shape[0], N_SHARD, -1).transpose(1, 0, 2)
    return g.reshape(N_SHARD, -1, g.shape[1])


def kernel(x, mem, norm_mix_pre, w_in, conv_w, conv_b, dt_bias, a_log, d_skip, ssd_norm, norm_mem, w_mem_kv, w_sb_out, w_ssd_out, w_mem_out, w_o, norm_mix_post, norm_mlp_pre, w_up, w_down, norm_mlp_post, loss_target, m_norm_mix_pre, m_w_in, m_conv_w, m_conv_b, m_dt_bias, m_a_log, m_d_skip, m_ssd_norm, m_norm_mem, m_w_mem_kv, m_w_sb_out, m_w_ssd_out, m_w_mem_out, m_w_o, m_norm_mix_post, m_norm_mlp_pre, m_w_up, m_w_down, m_norm_mlp_post, v_norm_mix_pre, v_w_in, v_conv_w, v_conv_b, v_dt_bias, v_a_log, v_d_skip, v_ssd_norm, v_norm_mem, v_w_mem_kv, v_w_sb_out, v_w_ssd_out, v_w_mem_out, v_w_o, v_norm_mix_post, v_norm_mlp_pre, v_w_up, v_w_down, v_norm_mlp_post):
    env = dict(locals())
    w = {n: env[n] for n in WEIGHTS}
    mo = {n: env["m_" + n] for n in WEIGHTS}
    vo = {n: env["v_" + n] for n in WEIGHTS}
    shard = 2 * lax.axis_index("x") + lax.axis_index("y")

    first = _gather_two_level([w[n][0].astype(bf16) for n in FIRST], "gather_first")
    w_main, w_dt = _to_internal(_full_from_slabs("w_in", first[0]))
    wts = dict(w_main=w_main, w_dt=w_dt, w_mem_kv=_full_from_slabs("w_mem_kv", first[1]))
    ride_names = (LATE[:4], LATE[4:5], LATE[5:])
    late_rides = tuple(_Ride([w[n][0].astype(bf16) for n in names] + ([w["conv_w"][0]] if i == 0 else []), "gather")
                       for i, names in enumerate(ride_names))

    def late_weights(i, lands):
        full = {n: _full_from_slabs(n, s) for n, s in zip(ride_names[i], lands)}
        if i == 0:
            full["conv_w"] = lands[-1].transpose(1, 0, 2).reshape(CONV_K, CONV_DIM)
        return full

    def rest_rides(g):
        slabs = [_slabs_from_full(n, g[n]).astype(bf16) for n in REST]
        return _Ride(slabs[5:], "scatter"), _Ride(slabs[:5], "scatter")

    core = lax.axis_index("c")
    half = D // 2

    def w_in_ride(g):
        ordered = _from_internal(g["w_main"], g["w_dt"])
        stack = jnp.stack([_w_in_slab(ordered, s, bf16) for s in range(N_SHARD)])
        keep = lax.dynamic_slice_in_dim(stack, core * half, half, axis=1)
        away = lax.dynamic_slice_in_dim(stack, (1 - core) * half, half, axis=1)
        (got,) = _swap_sibling([away], "w_in_halves_out")
        wide = lambda a: a.reshape(N_SHARD * half, -1)
        chip = _sum_parts(wide(keep), wide(got)[None], "sum_cores_w_in", bf16).reshape(N_SHARD, half, -1)
        own = lax.switch(shard, [functools.partial(_w_in_slab, ordered, s, f32) for s in range(N_SHARD)])
        own = lax.dynamic_slice_in_dim(own, core * half, half, axis=0)
        g["w_in_own"] = _sum_parts(own, lax.dynamic_index_in_dim(got, shard, 0, keepdims=True), "sum_cores_w_in_own")
        return _Ride([chip], "scatter")

    small = {n: w[n] for n in SMALL if n != "conv_w"}
    loss, grad_x, g = _local_step(x[0], mem[0], loss_target[0], wts, late_rides, late_weights, small,
                                  rest_rides, w_in_ride)
    out_g, out_d, out_m, out_v = {}, {}, {}, {}

    def apply(n, g_parts):
        res = _adamw(w[n][0], mo[n][0], vo[n][0], g_parts, name="adamw_" + n)
        out_g[n], out_d[n], out_m[n], out_v[n] = [r[None] for r in res]

    mine = _sum_parts(g["w_in_own"], g["w_in_lands"][0], name="sum_chips_w_in")
    (theirs,) = _swap_sibling([mine], "w_in_halves_back")
    g_w_in = lax.dynamic_update_slice_in_dim(jnp.zeros((D, D_IN // N_SHARD), f32), mine, core * half, axis=0)
    apply("w_in", [lax.dynamic_update_slice_in_dim(g_w_in, theirs, (1 - core) * half, axis=0)])

    packets = _exchange_packets(_pack([g[n] for n in SMALL] + [loss[:, :1]]))
    partial = []
    for n, r in zip(REST, g["rest_lands"]):
        own = lax.dynamic_index_in_dim(_slabs_from_full(n, g[n]), shard, 0, keepdims=False)
        partial.append(_sum_parts(own, r, name="sum_chips_" + n))
    other = _swap_sibling(partial, "swap_sibling")

    for n, p, q in zip(REST, partial, other):
        apply(n, [p, q])
    tot = _sum_parts(None, packets, name="sum_packets")
    shapes = [g[n].shape for n in SMALL] + [(1, 1)]
    sm = dict(zip(SMALL + ("loss",), _unpack(tot, shapes)))
    sm["conv_w"] = lax.dynamic_slice_in_dim(sm["conv_w"], shard * (CONV_DIM // N_SHARD), CONV_DIM // N_SHARD, axis=1)
    own_small = lambda d: _pack([d[n].reshape(sm[n].shape) for n in SMALL])
    res = _adamw(own_small(w), own_small(mo), own_small(vo), [own_small(sm)], name="adamw_small")
    own_shapes = [sm[n].shape for n in SMALL]
    for store, r in zip((out_g, out_d, out_m, out_v), res):
        for n, val in zip(SMALL, _unpack(r, own_shapes)):
            store[n] = val.reshape(w[n].shape)

    outs = [sm["loss"].reshape(()), grad_x[None]]
    for store in (out_g, out_d, out_m, out_v):
        outs += [store[n] for n in WEIGHTS]
    return tuple(outs)
```

```python
import functools

import jax
import jax.numpy as jnp
from jax import lax
from jax.experimental import pallas as pl
from jax.experimental.pallas import tpu as pltpu

f32 = jnp.float32
bf16 = jnp.bfloat16

D = 1024
EPS = 1e-6
SB_HD = 64
SSD_INNER = 2048
SSD_HEADS = 32
SSD_GROUPS = 4
SSD_N = 128
SSD_L = 128
CONV_K = 4
CONV_DIM = 3072
MEM_HEADS = 4
MEM_HD = 256
D_FF = 4096
D_IN = 12320
N_SHARD = 4
N_DEV = 8

P_QKV, P_XBC, P_GATE, P_MEMQ, P_Z, P_DT, P_TOT = 0, 3072, 6144, 9216, 10240, 12288, 12416
R_QKV, R_Z, R_XBC, R_DT, R_MEMQ, R_GATE = (0, 3072), (3072, 5120), (5120, 8192), (8192, 8224), (8224, 9248), (9248, 12320)

ADAM_LR = 0.001
ADAM_B1 = 0.9
ADAM_B2 = 0.999
ADAM_EPS = 1e-08
ADAM_WD = 0.01
ADAM_STEP = 10

VMEM_LIMIT = 56 * 1024 * 1024
TN_TILE = 512

NN = (((1,), (0,)), ((), ()))
NT = (((1,), (1,)), ((), ()))
TN = (((0,), (0,)), ((), ()))


def _dot(a, b, dims=NN):
    return lax.dot_general(a, b, dims, preferred_element_type=f32)


def _params(sem=None):
    return pltpu.CompilerParams(dimension_semantics=sem, vmem_limit_bytes=VMEM_LIMIT)


def _sigmoid(x):
    return 1.0 / (1.0 + jnp.exp(-x))


def _split2(x):
    hi = x.astype(bf16)
    lo = (x - hi.astype(f32)).astype(bf16)
    return hi, lo


def _split3(x):
    hi = x.astype(bf16)
    r = x - hi.astype(f32)
    mid = r.astype(bf16)
    lo = (r - mid.astype(f32)).astype(bf16)
    return hi, mid, lo


def _mm(a, b, mode, *, tm, tn, name, out_dtypes=(f32,), epi=None, extras=(), ride=None):
    M = a.shape[1] if mode == "tn" else a.shape[0]
    N = b.shape[0] if mode == "nt" else b.shape[1]
    tm, tn = min(tm, M), min(tn, N)
    if mode == "tn":
        tm, tn = min(tm, TN_TILE), min(tn, TN_TILE)
    if mode == "nn":
        (M, K), N = a.shape, b.shape[1]
        a_spec = pl.BlockSpec((tm, K), lambda i, j: (i, 0))
        b_spec = pl.BlockSpec((K, tn), lambda i, j: (0, j))
        dims = NN
    elif mode == "nt":
        (M, K), N = a.shape, b.shape[0]
        a_spec = pl.BlockSpec((tm, K), lambda i, j: (i, 0))
        b_spec = pl.BlockSpec((tn, K), lambda i, j: (j, 0))
        dims = NT
    else:
        (K, M), N = a.shape, b.shape[1]
        a_spec = pl.BlockSpec((K, tm), lambda i, j: (0, i))
        b_spec = pl.BlockSpec((K, tn), lambda i, j: (0, j))
        dims = TN
    assert M % tm == 0 and N % tn == 0, (name, M, N, tm, tn)
    n_ex, n_out = len(extras), len(out_dtypes)
    n_r = ride.n if ride else 0
    o_spec = pl.BlockSpec((tm, tn), lambda i, j: (i, j))
    grid = (M // tm, N // tn)

    def body(a_ref, b_ref, *rest):
        r_ins = rest[n_ex:n_ex + n_r]
        outs = rest[n_ex + n_r:n_ex + n_r + n_out]
        r_lnd, r_sems = rest[n_ex + n_r + n_out:n_ex + 2 * n_r + n_out], rest[n_ex + 2 * n_r + n_out:]
        i, j = pl.program_id(0), pl.program_id(1)
        if ride:
            pl.when((i == 0) & (j == 0))(lambda: ride.start(r_ins, r_lnd, r_sems))
        acc = _dot(a_ref[...].astype(bf16), b_ref[...].astype(bf16), dims)
        res = (acc,) if epi is None else epi(acc, *[e[...] for e in rest[:n_ex]])
        for o_ref, r in zip(outs, res):
            o_ref[...] = r.astype(o_ref.dtype)
        if ride:
            pl.when((i == grid[0] - 1) & (j == grid[1] - 1))(lambda: ride.finish(r_ins, r_lnd, r_sems))

    out = pl.pallas_call(
        body, name=name, grid=grid,
        in_specs=[a_spec, b_spec] + [o_spec] * n_ex + (ride.in_specs if ride else []),
        out_specs=[o_spec] * n_out + (ride.out_specs if ride else []),
        out_shape=[jax.ShapeDtypeStruct((M, N), dt) for dt in out_dtypes] + (ride.out_shape if ride else []),
        scratch_shapes=ride.scratch if ride else [],
        compiler_params=_params(("arbitrary", "arbitrary") if ride else ("parallel", "parallel")),
    )(a, b, *extras, *(ride.srcs if ride else []))
    if ride:
        return (out[0] if n_out == 1 else out[:n_out]), list(out[n_out:])
    return out[0] if n_out == 1 else out


def _mm_pieces_nt(pieces, b, add, *, tm, tn, name, ride):
    M, N = pieces[0].shape[0], b.shape[0]
    n_p, n_r = len(pieces), (ride.n if ride else 0)
    o_spec = pl.BlockSpec((tm, tn), lambda i, j: (i, j))
    grid = (M // tm, N // tn)

    def body(*refs):
        b_ref, add_ref = refs[n_p:n_p + 2]
        r_ins, o_ref = refs[n_p + 2:n_p + 2 + n_r], refs[n_p + 2 + n_r]
        r_lnd, r_sems = refs[n_p + 3 + n_r:n_p + 3 + 2 * n_r], refs[n_p + 3 + 2 * n_r:]
        i, j = pl.program_id(0), pl.program_id(1)
        if ride:
            pl.when((i == 0) & (j == 0))(lambda: ride.start(r_ins, r_lnd, r_sems))
        acc, off = add_ref[...], 0
        for r in refs[:n_p]:
            acc = acc + _dot(r[...], b_ref[:, off:off + r.shape[1]], NT)
            off += r.shape[1]
        o_ref[...] = acc
        if ride:
            pl.when((i == grid[0] - 1) & (j == grid[1] - 1))(lambda: ride.finish(r_ins, r_lnd, r_sems))

    out = pl.pallas_call(
        body, name=name, grid=grid,
        in_specs=[pl.BlockSpec((tm, p.shape[1]), lambda i, j: (i, 0)) for p in pieces]
        + [pl.BlockSpec((tn, b.shape[1]), lambda i, j: (j, 0)), o_spec] + (ride.in_specs if ride else []),
        out_specs=[o_spec] + (ride.out_specs if ride else []),
        out_shape=[jax.ShapeDtypeStruct((M, N), f32)] + (ride.out_shape if ride else []),
        scratch_shapes=ride.scratch if ride else [],
        compiler_params=_params(("arbitrary", "arbitrary")),
    )(*pieces, b, add, *(ride.srcs if ride else []))
    return out[0], list(out[1:])


def _rms_fwd(x, g, *, name, out_dtype, residual=None, tm=512):
    S, C = x.shape
    tm = min(tm, S)
    has_res = residual is not None

    def body(x_ref, g_ref, *rest):
        xv = x_ref[...]
        r = lax.rsqrt(jnp.mean(xv * xv, axis=1, keepdims=True) + EPS)
        y = xv * r * g_ref[...]
        if has_res:
            y = y + rest[0][...]
        rest[-1][...] = y.astype(out_dtype)

    row = pl.BlockSpec((tm, C), lambda i: (i, 0))
    vec = pl.BlockSpec((1, C), lambda i: (0, 0))
    args = (x, g) + ((residual,) if has_res else ())
    return pl.pallas_call(
        body, name=name, grid=(S // tm,),
        in_specs=[row, vec] + ([row] if has_res else []),
        out_specs=row, out_shape=jax.ShapeDtypeStruct((S, C), out_dtype),
        compiler_params=_params(("parallel",)),
    )(*args)


def _rms_bwd(x, dy, g, *, name, out_dtype, add=None, tm=512):
    S, C = x.shape
    tm = min(tm, S)
    has_add = add is not None

    def body(x_ref, dy_ref, g_ref, *rest):
        dx_ref, dg_ref = rest[-2], rest[-1]
        xv = x_ref[...]
        dyv = dy_ref[...].astype(f32)
        r = lax.rsqrt(jnp.mean(xv * xv, axis=1, keepdims=True) + EPS)
        xh = xv * r
        dxh = dyv * g_ref[...]
        dx = r * (dxh - xh * jnp.mean(dxh * xh, axis=1, keepdims=True))
        if has_add:
            dx = dx + rest[0][...]
        dx_ref[...] = dx.astype(out_dtype)

        @pl.when(pl.program_id(0) == 0)
        def _():
            dg_ref[...] = jnp.zeros_like(dg_ref)

        dg_ref[...] += jnp.sum(dyv * xh, axis=0, keepdims=True)

    row = pl.BlockSpec((tm, C), lambda i: (i, 0))
    vec = pl.BlockSpec((1, C), lambda i: (0, 0))
    args = (x, dy, g) + ((add,) if has_add else ())
    return pl.pallas_call(
        body, name=name, grid=(S // tm,),
        in_specs=[row, row, vec] + ([row] if has_add else []),
        out_specs=[row, vec],
        out_shape=[jax.ShapeDtypeStruct((S, C), out_dtype), jax.ShapeDtypeStruct((1, C), f32)],
        compiler_params=_params(("arbitrary",)),
    )(*args)


SB_T = 128
SB_SPENT = -120.0
SB_TAIL = 3
SB_GROUPS = (4, 2, 1)
SB_GROUPS_BWD = (4, 2, 1)


def _sb_masks():
    lane = lax.broadcasted_iota(jnp.int32, (1, 128), 1)
    m_a = (lane < SB_HD).astype(f32)
    return m_a, 1.0 - m_a


def _chunks(a, n):
    return [a[:, u * SB_T:(u + 1) * SB_T] for u in range(n)]


def _cat(parts, axis):
    return parts[0] if len(parts) == 1 else jnp.concatenate(parts, axis=axis)


def _mask_last(a, n, mask):
    if mask is None:
        return a
    parts = _chunks(a, n)
    return _cat(parts[:-1] + [jnp.where(mask, parts[-1], 0.0)], 1)


def _sb_logits(z, n, mask):
    l1p = jnp.log(1.0 + jnp.exp(-jnp.abs(z)))
    lb = jnp.minimum(z, 0.0) - l1p
    return lb, _mask_last(lb - z, n, mask)


def _by_count(i, most, fn):
    return lax.switch(jnp.minimum(i, most - 1), [functools.partial(fn, n) for n in range(1, most + 1)])


def _chunk_matmul(parts_list, u_mat):
    out = _dot(_cat(parts_list, 0), u_mat)
    return [out[u * SB_T:(u + 1) * SB_T] for u in range(len(parts_list))]


def _chunk_cumsum(lk, n, u_mat):
    hi = lk.astype(bf16)
    lo = (lk - hi.astype(f32)).astype(bf16)
    out = _chunk_matmul(_chunks(hi, n) + _chunks(lo, n), u_mat)
    return [out[u] + out[n + u] for u in range(n)]


def _sb_fwd(proj, S, ride=None):
    nq = S // SB_T
    n_pairs = D // 128
    scale = SB_HD ** -0.5
    n_r = ride.n if ride else 0

    def body(q_ref, k_ref, v_ref, *rest):
        o_ref, t_ref = rest[n_r:n_r + 2]
        i = pl.program_id(1)
        if ride:
            pl.when((pl.program_id(0) == 0) & (i == 0))(
                lambda: ride.start(rest[:n_r], rest[n_r + 2:2 * n_r + 2], rest[2 * n_r + 2:]))
        m_a, m_b = _sb_masks()
        r_i = lax.broadcasted_iota(jnp.int32, (SB_T, SB_T), 0)
        c_i = lax.broadcasted_iota(jnp.int32, (SB_T, SB_T), 1)
        u_mat = (r_i > c_i).astype(bf16)
        causal = c_i < r_i
        q = q_ref[...] * scale
        q_h = ((q * m_a).astype(bf16), (q * m_b).astype(bf16))

        def group(j_lo, n, carry, mask):
            acc, c_a, c_b = carry
            rows = pl.ds(pl.multiple_of(j_lo * SB_T, SB_T), n * SB_T)
            k = k_ref[rows, :].astype(bf16)
            v = v_ref[rows, :]
            zs = [_dot(q_b, k, NT) for q_b in q_h]
            lbk = [_sb_logits(z, n, mask) for z in zs]
            parts = [_chunk_cumsum(lk, n, u_mat) for _, lk in lbk]
            ws, cs = [], []
            for (lb, lk), part, c in zip(lbk, parts, (c_a, c_b)):
                lb_c, lk_c = _chunks(lb, n), _chunks(lk, n)
                w_c = [None] * n
                for u in reversed(range(n)):
                    w_c[u] = jnp.exp(lb_c[u] + c + part[u])
                    c = c + jnp.sum(lk_c[u], axis=1, keepdims=True)
                ws.append(_mask_last(_cat(w_c, 1), n, mask).astype(bf16))
                cs.append(c)
            for w, m in zip(ws, (m_a, m_b)):
                acc = acc + _dot(w, (v * m).astype(bf16))
            return acc, cs[0], cs[1]

        zero_c = jnp.zeros((SB_T, 1), f32)
        init = (jnp.zeros((SB_T, 128), f32), zero_c, zero_c)
        carry = _by_count(i, SB_TAIL, lambda n: group(i - n + 1, n, init, causal))

        def spent(cr):
            return (jnp.max(jnp.maximum(cr[1], cr[2])) < SB_SPENT).astype(jnp.int32)

        state = (i - jnp.minimum(i, SB_TAIL - 1), spent(carry), carry)
        for n in SB_GROUPS:
            def step(st, n=n):
                left, _, cr = st
                cr = group(left - n, n, cr, None)
                return left - n, spent(cr), cr

            state = lax.while_loop(lambda st, n=n: (st[0] >= n) & (st[1] == 0), step, state)
        left, _, carry = state
        o_ref[...] = carry[0]
        lane = lax.broadcasted_iota(jnp.int32, (1, 128), 1)
        t_ref[...] = (jnp.where(lane == 0, carry[1], 0.0) + jnp.where(lane == SB_HD, carry[2], 0.0)
                      + jnp.where(lane == 1, left.astype(f32), 0.0))
        if ride:
            pl.when((pl.program_id(0) == n_pairs - 1) & (i == nq - 1))(
                lambda: ride.finish(rest[:n_r], rest[n_r + 2:2 * n_r + 2], rest[2 * n_r + 2:]))

    qs = pl.BlockSpec((SB_T, 128), lambda h, i: (i, h))
    out = pl.pallas_call(
        body, name="sb_fwd", grid=(n_pairs, nq),
        in_specs=[qs,
                  pl.BlockSpec((S, 128), lambda h, i: (0, n_pairs + h)),
                  pl.BlockSpec((S, 128), lambda h, i: (0, 2 * n_pairs + h))] + (ride.in_specs if ride else []),
        out_specs=[qs, qs] + (ride.out_specs if ride else []),
        out_shape=[jax.ShapeDtypeStruct((S, D), f32)] * 2 + (ride.out_shape if ride else []),
        scratch_shapes=ride.scratch if ride else [],
        compiler_params=_params(("arbitrary", "arbitrary")),
    )(proj, proj, proj, *(ride.srcs if ride else []))
    return out[0], out[1], list(out[2:])


def _sb_bwd(proj, tot_lk, do, S, ride=None):
    nq = S // SB_T
    n_pairs = D // 128
    scale = SB_HD ** -0.5
    n_r = ride.n if ride else 0

    def body(q_ref, k_ref, v_ref, t_ref, do_ref, *rest):
        dq_ref, dk_ref, dv_ref = rest[n_r:n_r + 3]
        dk_acc, dv_acc = rest[2 * n_r + 3:2 * n_r + 5]
        r_ins, r_lnd, r_sems = rest[:n_r], rest[n_r + 3:2 * n_r + 3], rest[2 * n_r + 5:]
        i = pl.program_id(1)
        if ride:
            pl.when((pl.program_id(0) == 0) & (i == 0))(lambda: ride.start(r_ins, r_lnd, r_sems))
        m_a, m_b = _sb_masks()
        r_i = lax.broadcasted_iota(jnp.int32, (SB_T, SB_T), 0)
        c_i = lax.broadcasted_iota(jnp.int32, (SB_T, SB_T), 1)
        u_inc = (r_i <= c_i).astype(bf16)
        u_exc = (r_i < c_i).astype(bf16)
        causal = c_i < r_i

        @pl.when(i == 0)
        def _():
            dk_acc[...] = jnp.zeros_like(dk_acc)
            dv_acc[...] = jnp.zeros_like(dv_acc)

        q = q_ref[...] * scale
        dov = do_ref[...]
        tv = t_ref[...]
        lane = lax.broadcasted_iota(jnp.int32, (1, 128), 1)
        heads = []
        for m, first in ((m_a, 0), (m_b, SB_HD)):
            tot = jnp.sum(jnp.where(lane == first, tv, 0.0), axis=1, keepdims=True)
            heads.append(((q * m).astype(bf16), (dov * m).astype(bf16), tot, m))
        lowest = jnp.clip(jnp.max(jnp.where(lane == 1, tv, 0.0)).astype(jnp.int32), 0, i)

        def group(j_lo, n, carry, mask):
            dq_acc, cp_a, cp_b, ce_a, ce_b = carry
            rows = pl.ds(pl.multiple_of(j_lo * SB_T, SB_T), n * SB_T)
            k_f = k_ref[rows, :]
            k = k_f.astype(bf16)
            v = v_ref[rows, :].astype(bf16)
            zs = [_dot(h[0], k, NT) for h in heads]
            dws = [_dot(h[1], v, NT) for h in heads]
            lbk = [_sb_logits(z, n, mask) for z in zs]
            parts = [_chunk_cumsum(lk, n, u_inc) for _, lk in lbk]
            ws, es, cps = [], [], []
            for (lb, lk), part, dw, h, cp in zip(lbk, parts, dws, heads, (cp_a, cp_b)):
                lb_c, lk_c = _chunks(lb, n), _chunks(lk, n)
                w_c = []
                for u in range(n):
                    w_c.append(jnp.exp(lb_c[u] + (h[2] - cp) - part[u]))
                    cp = cp + jnp.sum(lk_c[u], axis=1, keepdims=True)
                w = _mask_last(_cat(w_c, 1), n, mask)
                ws.append(w)
                es.append(dw * w)
                cps.append(cp)
            e_parts = [_chunk_matmul(_chunks(e.astype(bf16), n), u_exc) for e in es]
            dzs, ces = [], []
            for (lb, _), e, e_part, ce in zip(lbk, es, e_parts, (ce_a, ce_b)):
                e_c = _chunks(e, n)
                big_c = []
                for u in range(n):
                    big_c.append(ce + e_part[u])
                    ce = ce + jnp.sum(e_c[u], axis=1, keepdims=True)
                sig = jnp.exp(lb)
                dz = _mask_last(e * (1.0 - sig) - _cat(big_c, 1) * sig, n, mask)
                dzs.append(dz.astype(bf16))
                ces.append(ce)
            dk_t = jnp.zeros((n * SB_T, 128), f32)
            dv_t = jnp.zeros((n * SB_T, 128), f32)
            for dz_b, w, h in zip(dzs, ws, heads):
                dq_acc = dq_acc + _dot(dz_b, (k_f * h[3]).astype(bf16))
                dk_t = dk_t + _dot(dz_b, h[0], TN)
                dv_t = dv_t + _dot(w.astype(bf16), h[1], TN)
            dk_acc[rows, :] += dk_t
            dv_acc[rows, :] += dv_t
            return dq_acc, cps[0], cps[1], ces[0], ces[1]

        zc = jnp.zeros((SB_T, 1), f32)
        carry = (jnp.zeros((SB_T, 128), f32), zc, zc, zc, zc)
        done = lowest
        tail_lo = i - jnp.minimum(i, SB_TAIL - 1)
        for n in SB_GROUPS_BWD:
            trips = (tail_lo - done) // n
            carry = lax.fori_loop(
                0, trips, functools.partial(lambda gi, cr, n, done: group(done + gi * n, n, cr, None), n=n, done=done),
                carry)
            done = done + trips * n
        carry = _by_count(i, SB_TAIL, lambda n: group(i - n + 1, n, carry, causal))
        dq_ref[...] = (carry[0] * scale).astype(bf16)

        @pl.when(i == nq - 1)
        def _():
            dk_ref[...] = dk_acc[...].astype(bf16)
            dv_ref[...] = dv_acc[...].astype(bf16)

        if ride:
            pl.when((pl.program_id(0) == n_pairs - 1) & (i == nq - 1))(
                lambda: ride.finish(r_ins, r_lnd, r_sems))

    qs = pl.BlockSpec((SB_T, 128), lambda h, i: (i, h))
    full = pl.BlockSpec((S, 128), lambda h, i: (0, h))
    out = pl.pallas_call(
        body, name="sb_bwd", grid=(n_pairs, nq),
        in_specs=[qs,
                  pl.BlockSpec((S, 128), lambda h, i: (0, n_pairs + h)),
                  pl.BlockSpec((S, 128), lambda h, i: (0, 2 * n_pairs + h)),
                  qs, qs] + (ride.in_specs if ride else []),
        out_specs=[qs, full, full] + (ride.out_specs if ride else []),
        out_shape=[jax.ShapeDtypeStruct((S, D), bf16)] * 3 + (ride.out_shape if ride else []),
        scratch_shapes=[pltpu.VMEM((S, 128), f32), pltpu.VMEM((S, 128), f32)] + (ride.scratch if ride else []),
        compiler_params=_params(("arbitrary", "arbitrary")),
    )(proj, proj, proj, tot_lk, do, *(ride.srcs if ride else []))
    return out[0], out[1], out[2], list(out[3:])


CONV_CB = 256
HALO = 8


def _conv_fwd(proj, conv_w, conv_b, S):
    tr = min(512, S)

    def body(x_ref, w_ref, b_ref, xc_ref, xbc_ref):
        w = w_ref[...]
        for t in range(S // tr):
            cur = x_ref[t * tr:(t + 1) * tr, :]
            halo = x_ref[t * tr - HALO:t * tr, :] if t else jnp.zeros((HALO, CONV_CB), f32)
            win = jnp.concatenate([halo, cur], axis=0)
            acc = b_ref[...] + w[CONV_K - 1:CONV_K, :] * cur
            for k in range(CONV_K - 1):
                acc = acc + w[k:k + 1, :] * pltpu.roll(win, CONV_K - 1 - k, 0)[HALO:, :]
            xc_ref[t * tr:(t + 1) * tr, :] = acc
            xbc_ref[t * tr:(t + 1) * tr, :] = acc * _sigmoid(acc)

    col = pl.BlockSpec((S, CONV_CB), lambda c: (0, c))
    return pl.pallas_call(
        body, name="conv_fwd", grid=(CONV_DIM // CONV_CB,),
        in_specs=[pl.BlockSpec((S, CONV_CB), lambda c: (0, P_XBC // CONV_CB + c)),
                  pl.BlockSpec((CONV_K, CONV_CB), lambda c: (0, c)),
                  pl.BlockSpec((1, CONV_CB), lambda c: (0, c))],
        out_specs=[col, col], out_shape=[jax.ShapeDtypeStruct((S, CONV_DIM), f32)] * 2,
        compiler_params=_params(("parallel",)),
    )(proj, conv_w, conv_b)


def _conv_bwd(proj, xc, dxbc, conv_w, S):
    tr = min(512, S)

    def body(x_ref, xc_ref, dy_ref, w_ref, dx_ref, dw_ref, db_ref, dxc_s):
        w = w_ref[...]
        xcv = xc_ref[...]
        sg = _sigmoid(xcv)
        dxc_s[0:S, :] = dy_ref[...] * (sg * (1.0 + xcv * (1.0 - sg)))
        dxc_s[S:S + HALO, :] = jnp.zeros((HALO, CONV_CB), f32)
        dws = [jnp.zeros((1, CONV_CB), f32) for _ in range(CONV_K)]
        db = jnp.zeros((1, CONV_CB), f32)
        for t in range(S // tr):
            cur = x_ref[t * tr:(t + 1) * tr, :]
            halo = x_ref[t * tr - HALO:t * tr, :] if t else jnp.zeros((HALO, CONV_CB), f32)
            win = jnp.concatenate([halo, cur], axis=0)
            dwin = dxc_s[t * tr:(t + 1) * tr + HALO, :]
            dcur = dwin[0:tr, :]
            db = db + jnp.sum(dcur, axis=0, keepdims=True)
            dws[CONV_K - 1] = dws[CONV_K - 1] + jnp.sum(dcur * cur, axis=0, keepdims=True)
            dx = w[CONV_K - 1:CONV_K, :] * dcur
            for k in range(CONV_K - 1):
                sh = CONV_K - 1 - k
                dws[k] = dws[k] + jnp.sum(dcur * pltpu.roll(win, sh, 0)[HALO:, :], axis=0, keepdims=True)
                dx = dx + w[k:k + 1, :] * pltpu.roll(dwin, tr + HALO - sh, 0)[0:tr, :]
            dx_ref[t * tr:(t + 1) * tr, :] = dx.astype(bf16)
        dw_ref[...] = jnp.concatenate(dws + [jnp.zeros((8 - CONV_K, CONV_CB), f32)], axis=0)
        db_ref[...] = db

    col = pl.BlockSpec((S, CONV_CB), lambda c: (0, c))
    return pl.pallas_call(
        body, name="conv_bwd", grid=(CONV_DIM // CONV_CB,),
        in_specs=[pl.BlockSpec((S, CONV_CB), lambda c: (0, P_XBC // CONV_CB + c)), col, col,
                  pl.BlockSpec((CONV_K, CONV_CB), lambda c: (0, c))],
        out_specs=[col, pl.BlockSpec((8, CONV_CB), lambda c: (0, c)), pl.BlockSpec((1, CONV_CB), lambda c: (0, c))],
        out_shape=[jax.ShapeDtypeStruct((S, CONV_DIM), bf16), jax.ShapeDtypeStruct((8, CONV_DIM), f32),
                   jax.ShapeDtypeStruct((1, CONV_DIM), f32)],
        scratch_shapes=[pltpu.VMEM((S + HALO, CONV_CB), f32)],
        compiler_params=_params(("parallel",)),
    )(proj, xc, dxbc, conv_w)


N_PAIR = SSD_HEADS // 2
NEG = -1e30


def _softplus(x):
    return jnp.maximum(x, 0.0) + jnp.log(1.0 + jnp.exp(-jnp.abs(x)))


def _ssd_common(dtr, dtb, alog):
    L = SSD_L
    r_i = lax.broadcasted_iota(jnp.int32, (L, L), 0)
    c_i = lax.broadcasted_iota(jnp.int32, (L, L), 1)
    dt = _softplus(dtr + dtb)
    a = -jnp.exp(alog)
    da = dt * a
    lower = (r_i >= c_i).astype(bf16)
    upper = (r_i <= c_i).astype(bf16)
    parts = _split3(da)
    a_cs = sum(_dot(lower, p) for p in parts)
    a_cs_t = sum(_dot(p, upper, TN) for p in parts)
    return dt, a, a_cs, a_cs_t, r_i >= c_i


def _pair_vec(lane, v, h):
    return jnp.where(lane < SB_HD, v[:, h:h + 1], v[:, h + 1:h + 2])


def _decay_mat(a_cs, a_cs_t, h, tril):
    return jnp.exp(jnp.where(tril, a_cs[:, h:h + 1] - a_cs_t[h:h + 1, :], NEG))


def _ssd_fwd(xbc, proj, pdt, dt_bias_p, a_log_p, d_skip_c, ssd_norm, S, ride=None):
    L = SSD_L
    nc = S // L
    n_r = ride.n if ride else 0

    def body(xbc_ref, dt_ref, z_ref, dtb_ref, alog_ref, dsk_ref, gn_ref, *rest):
        y_ref, yn_ref, hp_ref = rest[n_r:n_r + 3]
        state = rest[2 * n_r + 3]
        r_ins, r_lnd, r_sems = rest[:n_r], rest[n_r + 3:2 * n_r + 3], rest[2 * n_r + 4:]
        c = pl.program_id(0)
        if ride:
            pl.when(c == 0)(lambda: ride.start(r_ins, r_lnd, r_sems))

        @pl.when(c == 0)
        def _():
            state[...] = jnp.zeros_like(state)

        hp_ref[0] = state[...]
        lane = lax.broadcasted_iota(jnp.int32, (1, 128), 1)
        row128 = lax.broadcasted_iota(jnp.int32, (128, 1), 0)
        m_a, m_b = _sb_masks()
        dt, a, a_cs, a_cs_t, tril = _ssd_common(dt_ref[...], dtb_ref[...], alog_ref[...])
        a_last = a_cs[L - 1:L, :]
        for g in range(SSD_GROUPS):
            b_g = xbc_ref[:, SSD_INNER + g * SSD_N:SSD_INNER + (g + 1) * SSD_N].astype(bf16)
            c_g = xbc_ref[:, SSD_INNER + (SSD_GROUPS + g) * SSD_N:SSD_INNER + (SSD_GROUPS + g + 1) * SSD_N].astype(bf16)
            cb = _dot(c_g, b_g, NT)
            for pr in range(4):
                h = 8 * g + 2 * pr
                pi = h // 2
                cols = slice(pi * 128, (pi + 1) * 128)
                xs = xbc_ref[:, cols]
                x = xs * _pair_vec(lane, dt, h)
                acs = _pair_vec(lane, a_cs, h)
                al = _pair_vec(lane, a_last, h)
                w_a = (cb * _decay_mat(a_cs, a_cs_t, h, tril)).astype(bf16)
                w_b = (cb * _decay_mat(a_cs, a_cs_t, h + 1, tril)).astype(bf16)
                yd = _dot(w_a, (x * m_a).astype(bf16)) + _dot(w_b, (x * m_b).astype(bf16))
                hp = state[pi]
                yo = _dot(c_g, hp.astype(bf16), NT) * jnp.exp(acs)
                y_ref[:, cols] = yd + yo + dsk_ref[:, cols] * xs
                dec = jnp.exp(jnp.where(row128 < SB_HD, a_last[:, h:h + 1], a_last[:, h + 1:h + 2]))
                state[pi] = hp * dec + _dot((x * jnp.exp(al - acs)).astype(bf16), b_g, TN)
        zz = z_ref[...]
        y2 = y_ref[...] * (zz * _sigmoid(zz))
        gw = SSD_INNER // SSD_GROUPS
        for g in range(SSD_GROUPS):
            yg = y2[:, g * gw:(g + 1) * gw]
            rg = lax.rsqrt(jnp.mean(yg * yg, axis=1, keepdims=True) + EPS)
            yn_ref[:, g * gw:(g + 1) * gw] = (yg * rg * gn_ref[:, g * gw:(g + 1) * gw]).astype(bf16)
        if ride:
            pl.when(c == nc - 1)(lambda: ride.finish(r_ins, r_lnd, r_sems))

    vec128 = pl.BlockSpec((1, 128), lambda c: (0, 0))
    vecin = pl.BlockSpec((1, SSD_INNER), lambda c: (0, 0))
    rows = pl.BlockSpec((L, SSD_INNER), lambda c: (c, 0))
    out = pl.pallas_call(
        body, name="ssd_fwd", grid=(nc,),
        in_specs=[pl.BlockSpec((L, CONV_DIM), lambda c: (c, 0)),
                  pl.BlockSpec((L, 128), lambda c: (c, 0)),
                  pl.BlockSpec((L, SSD_INNER), lambda c: (c, P_Z // SSD_INNER)),
                  vec128, vec128, vecin, vecin] + (ride.in_specs if ride else []),
        out_specs=[rows, rows, pl.BlockSpec((1, N_PAIR, 128, SSD_N), lambda c: (c, 0, 0, 0))]
        + (ride.out_specs if ride else []),
        out_shape=[jax.ShapeDtypeStruct((S, SSD_INNER), f32), jax.ShapeDtypeStruct((S, SSD_INNER), bf16),
                   jax.ShapeDtypeStruct((nc, N_PAIR, 128, SSD_N), f32)] + (ride.out_shape if ride else []),
        scratch_shapes=[pltpu.VMEM((N_PAIR, 128, SSD_N), f32)] + (ride.scratch if ride else []),
        compiler_params=_params(("arbitrary",)),
    )(xbc, pdt, proj, dt_bias_p, a_log_p, d_skip_c, ssd_norm, *(ride.srcs if ride else []))
    return out[0], out[1], out[2], list(out[3:])


def _sum_all(v):
    return jnp.sum(jnp.sum(v, axis=1, keepdims=True), axis=0, keepdims=True)


def _ssd_bwd(dyn, y, xbc, proj, pdt, hprev, dt_bias_p, a_log_p, d_skip_c, ssd_norm, S, ride=None):
    L = SSD_L
    nc = S // L
    n_r = ride.n if ride else 0

    col = lax.broadcasted_iota(jnp.int32, (2 * SSD_INNER, 128), 0)
    head = lax.broadcasted_iota(jnp.int32, (2 * SSD_INNER, 128), 1)
    sel_pair = (col[:SSD_INNER] // SB_HD == head[:SSD_INNER]).astype(bf16)
    sel_head = (col // 128 == head).astype(bf16)

    def body(*refs):
        (dyn_ref, y_ref, xbc_ref, dt_ref, z_ref, hp_ref, dtb_ref, alog_ref, dsk_ref, gn_ref,
         selp_ref, selh_ref) = refs[:12]
        dz_ref, dxbc_ref, ddt_ref, dgn_ref, dsk_out, dalog_ref, ddtb_ref = refs[12 + n_r:19 + n_r]
        dstate, dy_s, st_a, st_q, st_d, st_x, dat = refs[19 + 2 * n_r:26 + 2 * n_r]
        r_ins, r_lnd, r_sems = refs[12:12 + n_r], refs[19 + n_r:19 + 2 * n_r], refs[26 + 2 * n_r:]
        c = pl.program_id(0)
        if ride:
            pl.when(c == 0)(lambda: ride.start(r_ins, r_lnd, r_sems))

        @pl.when(c == 0)
        def _():
            dat[...] = jnp.zeros_like(dat)
            dstate[...] = jnp.zeros_like(dstate)
            dgn_ref[...] = jnp.zeros_like(dgn_ref)
            dsk_out[...] = jnp.zeros_like(dsk_out)
            dalog_ref[...] = jnp.zeros_like(dalog_ref)
            ddtb_ref[...] = jnp.zeros_like(ddtb_ref)

        lane = lax.broadcasted_iota(jnp.int32, (1, 128), 1)
        row128 = lax.broadcasted_iota(jnp.int32, (128, 1), 0)
        rowl = lax.broadcasted_iota(jnp.int32, (L, 1), 0)
        m_a, m_b = _sb_masks()
        dtr = dt_ref[...]
        dt, a, a_cs, a_cs_t, tril = _ssd_common(dtr, dtb_ref[...], alog_ref[...])
        a_last = a_cs[L - 1:L, :]

        zz = z_ref[...]
        sg = _sigmoid(zz)
        silu = zz * sg
        yv = y_ref[...]
        y2 = yv * silu
        gw = SSD_INNER // SSD_GROUPS
        for g in range(SSD_GROUPS):
            sl = slice(g * gw, (g + 1) * gw)
            yg = y2[:, sl]
            rg = lax.rsqrt(jnp.mean(yg * yg, axis=1, keepdims=True) + EPS)
            yh = yg * rg
            dyn_g = dyn_ref[:, sl]
            dgn_ref[:, sl] += jnp.sum(dyn_g * yh, axis=0, keepdims=True)
            dyh = dyn_g * gn_ref[:, sl]
            dy2 = rg * (dyh - yh * jnp.mean(dyh * yh, axis=1, keepdims=True))
            dy_s[:, sl] = dy2 * silu[:, sl]
            dz_ref[:, sl] = (dy2 * yv[:, sl] * (sg[:, sl] * (1.0 + zz[:, sl] * (1.0 - sg[:, sl])))).astype(bf16)

        last_row = jnp.zeros((1, 128), f32)
        dsk_acc = jnp.zeros((1, 128), f32)
        for g in range(SSD_GROUPS):
            bsl = slice(SSD_INNER + g * SSD_N, SSD_INNER + (g + 1) * SSD_N)
            csl = slice(SSD_INNER + (SSD_GROUPS + g) * SSD_N, SSD_INNER + (SSD_GROUPS + g + 1) * SSD_N)
            b_g = xbc_ref[:, bsl].astype(bf16)
            c_g = xbc_ref[:, csl].astype(bf16)
            cb = _dot(c_g, b_g, NT)
            dcb = jnp.zeros((L, L), f32)
            dc_g = jnp.zeros((L, SSD_N), f32)
            db_g = jnp.zeros((L, SSD_N), f32)
            for pr in range(4):
                h = 8 * g + 2 * pr
                pi = h // 2
                cols = slice(pi * 128, (pi + 1) * 128)
                xs = xbc_ref[:, cols]
                dt_p = _pair_vec(lane, dt, h)
                x = xs * dt_p
                acs = _pair_vec(lane, a_cs, h)
                al = _pair_vec(lane, a_last, h)
                e_a = jnp.exp(acs)
                dte = jnp.exp(al - acs)
                m_mat_a = _decay_mat(a_cs, a_cs_t, h, tril)
                m_mat_b = _decay_mat(a_cs, a_cs_t, h + 1, tril)
                dyp = dy_s[:, cols]
                dsk = dsk_ref[:, cols]
                d_hn = dstate[pi]
                hp = hp_ref[0, pi]
                dy_a = (dyp * m_a).astype(bf16)
                dy_b = (dyp * m_b).astype(bf16)
                x_b = x.astype(bf16)
                gm_a = _dot(dy_a, x_b, NT) * m_mat_a
                gm_b = _dot(dy_b, x_b, NT) * m_mat_b
                dcb = dcb + gm_a + gm_b
                dx_d = _dot((cb * m_mat_a).astype(bf16), dy_a, TN) + _dot((cb * m_mat_b).astype(bf16), dy_b, TN)
                dx_s = _dot(b_g, d_hn.astype(bf16), NT) * dte
                dx = dx_d + dx_s
                dxbc_ref[:, cols] = dx * dt_p + dsk * dyp
                xdxs = x * dx_s
                st_x[:, cols] = xdxs
                st_a[:, cols] = dyp * (_dot(c_g, hp.astype(bf16), NT) * e_a) - xdxs
                st_d[:, cols] = dx * xs
                hh = d_hn * hp
                dsk_row = jnp.sum(dyp * xs, axis=0, keepdims=True)
                dec = jnp.exp(jnp.where(row128 < SB_HD, a_last[:, h:h + 1], a_last[:, h + 1:h + 2]))
                for hd, m, gm in ((h, m_a, gm_a), (h + 1, m_b, gm_b)):
                    half = slice(0, SB_HD) if hd == h else slice(SB_HD, 128)
                    qm = gm * cb
                    st_q[:, hd * 128:(hd + 1) * 128] = qm
                    dat[hd:hd + 1, :] = jnp.sum(qm, axis=0, keepdims=True)
                    hh_sum = jnp.sum(jnp.sum(hh[half, :], axis=0, keepdims=True), axis=1, keepdims=True)
                    last_row = jnp.where(lane == hd, jnp.exp(a_last[:, hd:hd + 1]) * hh_sum, last_row)
                    dsk_acc = jnp.where(lane == hd, jnp.sum(dsk_row * m, axis=1, keepdims=True), dsk_acc)
                dye = (dyp * e_a).astype(bf16)
                dc_g = dc_g + _dot(dye, hp.astype(bf16))
                db_g = db_g + _dot((x * dte).astype(bf16), d_hn.astype(bf16))
                dstate[pi] = dec * d_hn + _dot(dye, c_g, TN)
            dcb_b = dcb.astype(bf16)
            dxbc_ref[:, csl] = dc_g + _dot(dcb_b, b_g)
            dxbc_ref[:, bsl] = db_g + _dot(dcb_b, c_g, TN)

        r_i = lax.broadcasted_iota(jnp.int32, (L, L), 0)
        c_i = lax.broadcasted_iota(jnp.int32, (L, L), 1)
        rev = (r_i <= c_i).astype(bf16)

        def head_sums(st, sel, split=_split2):
            return sum(_dot(p, sel[...]) for p in split(st[...]))

        last_row = last_row + jnp.sum(head_sums(st_x, selp_ref), axis=0, keepdims=True)
        d_acs = (head_sums(st_a, selp_ref) + head_sums(st_q, selh_ref, _split3)
                 + jnp.where(rowl == L - 1, last_row, 0.0))
        ddt_x = head_sums(st_d, selp_ref)
        dda = sum(_dot(rev, p) for p in _split3(d_acs)) - sum(_dot(rev, p, NT) for p in _split3(dat[...]))
        ddt = ddt_x + dda * a
        dalog_ref[...] += jnp.sum(dda * dt, axis=0, keepdims=True) * a
        ddtr = jnp.where(lane < SSD_HEADS, ddt * _sigmoid(dtr + dtb_ref[...]), 0.0)
        ddt_ref[...] = ddtr.astype(bf16)
        ddtb_ref[...] += jnp.sum(ddtr, axis=0, keepdims=True)
        dsk_out[...] += dsk_acc
        if ride:
            pl.when(c == nc - 1)(lambda: ride.finish(r_ins, r_lnd, r_sems))

    rv = lambda c: nc - 1 - c
    vec128 = pl.BlockSpec((1, 128), lambda c: (0, 0))
    vecin = pl.BlockSpec((1, SSD_INNER), lambda c: (0, 0))
    rows = pl.BlockSpec((L, SSD_INNER), lambda c: (rv(c), 0))
    return pl.pallas_call(
        body, name="ssd_bwd", grid=(nc,),
        in_specs=[rows, rows,
                  pl.BlockSpec((L, CONV_DIM), lambda c: (rv(c), 0)),
                  pl.BlockSpec((L, 128), lambda c: (rv(c), 0)),
                  pl.BlockSpec((L, SSD_INNER), lambda c: (rv(c), P_Z // SSD_INNER)),
                  pl.BlockSpec((1, N_PAIR, 128, SSD_N), lambda c: (rv(c), 0, 0, 0)),
                  vec128, vec128, vecin, vecin,
                  pl.BlockSpec((SSD_INNER, 128), lambda c: (0, 0)),
                  pl.BlockSpec((2 * SSD_INNER, 128), lambda c: (0, 0))] + (ride.in_specs if ride else []),
        out_specs=[rows, pl.BlockSpec((L, CONV_DIM), lambda c: (rv(c), 0)),
                   pl.BlockSpec((L, 128), lambda c: (rv(c), 0)), vecin, vec128, vec128, vec128]
        + (ride.out_specs if ride else []),
        out_shape=[jax.ShapeDtypeStruct((S, SSD_INNER), bf16), jax.ShapeDtypeStruct((S, CONV_DIM), f32),
                   jax.ShapeDtypeStruct((S, 128), bf16), jax.ShapeDtypeStruct((1, SSD_INNER), f32),
                   jax.ShapeDtypeStruct((1, 128), f32), jax.ShapeDtypeStruct((1, 128), f32),
                   jax.ShapeDtypeStruct((1, 128), f32)] + (ride.out_shape if ride else []),
        scratch_shapes=[pltpu.VMEM((N_PAIR, 128, SSD_N), f32), pltpu.VMEM((L, SSD_INNER), f32),
                        pltpu.VMEM((L, SSD_INNER), f32), pltpu.VMEM((L, 2 * SSD_INNER), f32),
                        pltpu.VMEM((L, SSD_INNER), f32), pltpu.VMEM((L, SSD_INNER), f32),
                        pltpu.VMEM((128, L), f32)]
        + (ride.scratch if ride else []),
        compiler_params=_params(("arbitrary",)),
    )(dyn, y, xbc, pdt, proj, hprev, dt_bias_p, a_log_p, d_skip_c, ssd_norm, sel_pair, sel_head,
      *(ride.srcs if ride else []))


MEM_W = MEM_HEADS * MEM_HD


def _mem_probs(q, k):
    s = _dot(q, k, NT) * (MEM_HD ** -0.5)
    s = s - jnp.max(s, axis=1, keepdims=True)
    p = jnp.exp(s)
    return p / jnp.sum(p, axis=1, keepdims=True)


def _mem_fwd(proj, kv, S, tm=512):
    tm = min(tm, S)
    M = kv.shape[0]

    def body(q_ref, kv_ref, o_ref):
        for h in range(MEM_HEADS):
            sl = slice(h * MEM_HD, (h + 1) * MEM_HD)
            vsl = slice(MEM_W + h * MEM_HD, MEM_W + (h + 1) * MEM_HD)
            p = _mem_probs(q_ref[:, sl].astype(bf16), kv_ref[:, sl].astype(bf16))
            o_ref[:, sl] = _dot(p.astype(bf16), kv_ref[:, vsl].astype(bf16)).astype(bf16)

    return pl.pallas_call(
        body, name="mem_fwd", grid=(S // tm,),
        in_specs=[pl.BlockSpec((tm, MEM_W), lambda i: (i, P_MEMQ // MEM_W)),
                  pl.BlockSpec((M, 2 * MEM_W), lambda i: (0, 0))],
        out_specs=pl.BlockSpec((tm, MEM_W), lambda i: (i, 0)),
        out_shape=jax.ShapeDtypeStruct((S, MEM_W), bf16),
        compiler_params=_params(("parallel",)),
    )(proj, kv)


def _mem_bwd(proj, kv, dy, S, tm=512):
    tm = min(tm, S)
    M = kv.shape[0]
    scale = MEM_HD ** -0.5

    def body(q_ref, kv_ref, dy_ref, dq_ref, dkv_ref):
        @pl.when(pl.program_id(0) == 0)
        def _():
            dkv_ref[...] = jnp.zeros_like(dkv_ref)

        for h in range(MEM_HEADS):
            sl = slice(h * MEM_HD, (h + 1) * MEM_HD)
            vsl = slice(MEM_W + h * MEM_HD, MEM_W + (h + 1) * MEM_HD)
            q = q_ref[:, sl].astype(bf16)
            k = kv_ref[:, sl].astype(bf16)
            v = kv_ref[:, vsl].astype(bf16)
            dyh = dy_ref[:, sl].astype(bf16)
            p = _mem_probs(q, k)
            dp = _dot(dyh, v, NT)
            ds = (p * (dp - jnp.sum(dp * p, axis=1, keepdims=True)) * scale).astype(bf16)
            dq_ref[:, sl] = _dot(ds, k).astype(bf16)
            dkv_ref[:, sl] += _dot(ds, q, TN)
            dkv_ref[:, vsl] += _dot(p.astype(bf16), dyh, TN)

    return pl.pallas_call(
        body, name="mem_bwd", grid=(S // tm,),
        in_specs=[pl.BlockSpec((tm, MEM_W), lambda i: (i, P_MEMQ // MEM_W)),
                  pl.BlockSpec((M, 2 * MEM_W), lambda i: (0, 0)),
                  pl.BlockSpec((tm, MEM_W), lambda i: (i, 0))],
        out_specs=[pl.BlockSpec((tm, MEM_W), lambda i: (i, 0)), pl.BlockSpec((M, 2 * MEM_W), lambda i: (0, 0))],
        out_shape=[jax.ShapeDtypeStruct((S, MEM_W), bf16), jax.ShapeDtypeStruct((M, 2 * MEM_W), f32)],
        compiler_params=_params(("arbitrary",)),
    )(proj, kv, dy)


def _merge_fwd(proj, t0, t1, t2, S, tm=512):
    tm = min(tm, S)

    def body(g_ref, t0_ref, t1_ref, t2_ref, o_ref):
        acc = jnp.zeros((tm, D), f32)
        for b, t_ref in enumerate((t0_ref, t1_ref, t2_ref)):
            acc = acc + _sigmoid(g_ref[:, b * D:(b + 1) * D]) * t_ref[...]
        o_ref[...] = acc.astype(bf16)

    row = pl.BlockSpec((tm, D), lambda i: (i, 0))
    return pl.pallas_call(
        body, name="merge_fwd", grid=(S // tm,),
        in_specs=[pl.BlockSpec((tm, 3 * D), lambda i: (i, P_GATE // (3 * D))), row, row, row],
        out_specs=row, out_shape=jax.ShapeDtypeStruct((S, D), bf16),
        compiler_params=_params(("parallel",)),
    )(proj, t0, t1, t2)


def _merge_bwd(proj, t0, t1, t2, dm, S, tm=512):
    tm = min(tm, S)

    def body(g_ref, t0_ref, t1_ref, t2_ref, dm_ref, d0_ref, d1_ref, d2_ref, dg_ref):
        dmv = dm_ref[...]
        for b, (t_ref, d_ref) in enumerate(((t0_ref, d0_ref), (t1_ref, d1_ref), (t2_ref, d2_ref))):
            sg = _sigmoid(g_ref[:, b * D:(b + 1) * D])
            d_ref[...] = (dmv * sg).astype(bf16)
            dg_ref[:, b * D:(b + 1) * D] = (dmv * t_ref[...] * sg * (1.0 - sg)).astype(bf16)

    row = pl.BlockSpec((tm, D), lambda i: (i, 0))
    return pl.pallas_call(
        body, name="merge_bwd", grid=(S // tm,),
        in_specs=[pl.BlockSpec((tm, 3 * D), lambda i: (i, P_GATE // (3 * D))), row, row, row, row],
        out_specs=[row, row, row, pl.BlockSpec((tm, 3 * D), lambda i: (i, 0))],
        out_shape=[jax.ShapeDtypeStruct((S, D), bf16)] * 3 + [jax.ShapeDtypeStruct((S, 3 * D), bf16)],
        compiler_params=_params(("parallel",)),
    )(proj, t0, t1, t2, dm)


def _loss_head(ff, g, h1, target, S, tm=512):
    tm = min(tm, S)

    def body(ff_ref, g_ref, h1_ref, t_ref, dh_ref, loss_ref):
        xv = ff_ref[...]
        r = lax.rsqrt(jnp.mean(xv * xv, axis=1, keepdims=True) + EPS)
        err = h1_ref[...] + xv * r * g_ref[...] - t_ref[...]
        dh_ref[...] = err * (1.0 / D)

        @pl.when(pl.program_id(0) == 0)
        def _():
            loss_ref[...] = jnp.zeros_like(loss_ref)

        loss_ref[...] += 0.5 * _sum_all(jnp.mean(err * err, axis=1, keepdims=True)) * jnp.ones((1, 128), f32)

    row = pl.BlockSpec((tm, D), lambda i: (i, 0))
    return pl.pallas_call(
        body, name="loss_head", grid=(S // tm,),
        in_specs=[row, pl.BlockSpec((1, D), lambda i: (0, 0)), row, row],
        out_specs=[row, pl.BlockSpec((1, 128), lambda i: (0, 0))],
        out_shape=[jax.ShapeDtypeStruct((S, D), f32), jax.ShapeDtypeStruct((1, 128), f32)],
        compiler_params=_params(("arbitrary",)),
    )(ff, g, h1, target)


def _local_step(x, mem, target, wts, late_rides, late_weights, small, rest_rides, w_in_ride):
    S = x.shape[0]
    M = mem.shape[0]
    pad = lambda v: jnp.pad(v, ((0, 0), (0, 128 - SSD_HEADS)))
    dtb_p, alog_p = pad(small["dt_bias"]), pad(small["a_log"])
    dsk_c = jnp.repeat(small["d_skip"], SB_HD, axis=1)

    u = _rms_fwd(x, small["norm_mix_pre"], name="norm_pre", out_dtype=bf16)
    rides = late_rides or (None, None, None)
    if late_rides:
        proj, lands_a = _mm(u, wts["w_main"], "nn", tm=1024, tn=1024, name="in_proj", ride=rides[0])
    else:
        proj, lands_a = _mm(u, wts["w_main"], "nn", tm=1024, tn=1024, name="in_proj"), []
    pdt = _mm(u, wts["w_dt"], "nn", tm=1024, tn=128, name="in_proj_dt")
    y_sb, tot_lk, lands_b = _sb_fwd(proj, S, rides[1])
    wts = dict(wts, **late_weights(0, lands_a))
    small = dict(small, conv_w=wts.pop("conv_w"))
    xc, xbc = _conv_fwd(proj, small["conv_w"], small["conv_b"], S)
    y_ssd, yn, hprev, lands_c = _ssd_fwd(xbc, proj, pdt, dtb_p, alog_p, dsk_c, small["ssd_norm"], S, rides[2])
    wts = dict(wts, **late_weights(1, lands_b), **late_weights(2, lands_c))
    mn = _rms_fwd(mem, small["norm_mem"], name="norm_mem", out_dtype=bf16, tm=min(512, M))
    kv = _mm(mn, wts["w_mem_kv"], "nn", tm=M, tn=1024, name="mem_kv")
    y_mem = _mem_fwd(proj, kv, S)
    t0 = _mm(y_sb, wts["w_sb_out"], "nn", tm=1024, tn=1024, name="sb_out")
    t1 = _mm(yn, wts["w_ssd_out"], "nn", tm=1024, tn=1024, name="ssd_out")
    t2 = _mm(y_mem, wts["w_mem_out"], "nn", tm=1024, tn=1024, name="mem_out")
    merged = _merge_fwd(proj, t0, t1, t2, S)
    mix = _mm(merged, wts["w_o"], "nn", tm=1024, tn=1024, name="w_o")
    h1 = _rms_fwd(mix, small["norm_mix_post"], name="norm_mix_post", out_dtype=f32, residual=x)
    u2 = _rms_fwd(h1, small["norm_mlp_pre"], name="norm_mlp_pre", out_dtype=bf16)
    a_up, hrelu = _mm(u2, wts["w_up"], "nn", tm=1024, tn=1024, name="mlp_up", out_dtypes=(f32, bf16),
                      epi=lambda acc: (acc, jnp.square(jnp.maximum(acc, 0.0))))
    ff = _mm(hrelu, wts["w_down"], "nn", tm=1024, tn=1024, name="mlp_down")
    dh2, loss = _loss_head(ff, small["norm_mlp_post"], h1, target, S)

    g = {}
    dff, g["norm_mlp_post"] = _rms_bwd(ff, dh2, small["norm_mlp_post"], name="norm_mlp_post_bwd", out_dtype=bf16)
    da = _mm(dff, wts["w_down"], "nt", tm=1024, tn=1024, name="mlp_down_dx", out_dtypes=(bf16,),
             epi=lambda acc, a: (acc * (2.0 * jnp.maximum(a, 0.0)),), extras=(a_up,))
    g["w_down"] = _mm(hrelu, dff, "tn", tm=1024, tn=1024, name="mlp_down_dw")
    du2 = _mm(da, wts["w_up"], "nt", tm=1024, tn=1024, name="mlp_up_dx")
    g["w_up"] = _mm(u2, da, "tn", tm=1024, tn=1024, name="mlp_up_dw")
    dh1, g["norm_mlp_pre"] = _rms_bwd(h1, du2, small["norm_mlp_pre"], name="norm_mlp_pre_bwd", out_dtype=f32, add=dh2)
    dmix, g["norm_mix_post"] = _rms_bwd(mix, dh1, small["norm_mix_post"], name="norm_mix_post_bwd", out_dtype=bf16)
    dmerged = _mm(dmix, wts["w_o"], "nt", tm=1024, tn=1024, name="w_o_dx")
    g["w_o"] = _mm(merged, dmix, "tn", tm=1024, tn=1024, name="w_o_dw")
    dt0, dt1, dt2, dgl = _merge_bwd(proj, t0, t1, t2, dmerged, S)
    dy_sb = _mm(dt0, wts["w_sb_out"], "nt", tm=1024, tn=1024, name="sb_out_dx")
    g["w_sb_out"] = _mm(y_sb, dt0, "tn", tm=1024, tn=1024, name="sb_out_dw")
    dy_ssd = _mm(dt1, wts["w_ssd_out"], "nt", tm=1024, tn=1024, name="ssd_out_dx")
    g["w_ssd_out"] = _mm(yn, dt1, "tn", tm=1024, tn=1024, name="ssd_out_dw")
    dy_mem = _mm(dt2, wts["w_mem_out"], "nt", tm=1024, tn=1024, name="mem_out_dx")
    g["w_mem_out"] = _mm(y_mem, dt2, "tn", tm=1024, tn=1024, name="mem_out_dw")
    dmemq, dkv = _mem_bwd(proj, kv, dy_mem, S)
    g["w_mem_kv"] = _mm(mn, dkv, "tn", tm=1024, tn=1024, name="mem_kv_dw")
    dmn = _mm(dkv, wts["w_mem_kv"], "nt", tm=M, tn=1024, name="mem_kv_dx")
    _, g["norm_mem"] = _rms_bwd(mem, dmn, small["norm_mem"], name="norm_mem_bwd", out_dtype=bf16, tm=min(512, M))
    rides = rest_rides(g) if rest_rides else (None, None)
    dz, dxbc, ddt, g["ssd_norm"], dsk, dalog, ddtb, *lands_a = _ssd_bwd(
        dy_ssd, y_ssd, xbc, proj, pdt, hprev, dtb_p, alog_p, dsk_c, small["ssd_norm"], S, rides[0])
    g["d_skip"], g["a_log"], g["dt_bias"] = dsk[:, :SSD_HEADS], dalog[:, :SSD_HEADS], ddtb[:, :SSD_HEADS]
    dxbc_raw, dcw, g["conv_b"] = _conv_bwd(proj, xc, dxbc, small["conv_w"], S)
    g["conv_w"] = dcw[:CONV_K]
    dq, dk, dv, lands_b = _sb_bwd(proj, tot_lk, dy_sb, S, rides[1])
    g["rest_lands"] = lands_b + lands_a
    dproj = (dq, dk, dv, dxbc_raw, dgl, dmemq, dz)
    g["w_main"] = [_mm(u, p, "tn", tm=1024, tn=1024, name="in_proj_dw_%d" % i) for i, p in enumerate(dproj)]
    g["w_dt"] = _mm(u, ddt, "tn", tm=1024, tn=128, name="in_proj_dt_dw")
    du_dt = _mm(ddt, wts["w_dt"], "nt", tm=1024, tn=1024, name="in_proj_dt_dx")
    du, g["w_in_lands"] = _mm_pieces_nt(dproj, wts["w_main"], du_dt, tm=512, tn=256, name="in_proj_dx",
                                        ride=w_in_ride(g) if w_in_ride else None)
    grad_x, g["norm_mix_pre"] = _rms_bwd(x, du, small["norm_mix_pre"], name="norm_pre_bwd", out_dtype=f32, add=dh1)
    return loss, grad_x, g


def _to_internal(w_in):
    sec = lambda r: w_in[:, r[0]:r[1]]
    w_main = jnp.concatenate([sec(R_QKV), sec(R_XBC), sec(R_GATE), sec(R_MEMQ), sec(R_Z)], axis=1)
    w_dt = jnp.pad(sec(R_DT), ((0, 0), (0, 128 - SSD_HEADS)))
    return w_main, w_dt


def _from_internal(pieces, g_dt):
    dq, dk, dv, dxbc, dgate, dmemq, dz = pieces
    return [dq, dk, dv, dz, dxbc, g_dt[:, :SSD_HEADS], dmemq, dgate]


def _w_in_slab(ordered, s, dtype):
    width = D_IN // N_SHARD
    lo, hi, off, parts = s * width, (s + 1) * width, 0, []
    for p in ordered:
        a, b = max(lo, off), min(hi, off + p.shape[1])
        if a < b:
            parts.append(p[:, a - off:b - off].astype(dtype))
        off += p.shape[1]
    return jnp.concatenate(parts, axis=1)


MESH = pl.DeviceIdType.MESH
ANY = pl.BlockSpec(memory_space=pl.ANY)


def _place():
    x, y, c = lax.axis_index("x"), lax.axis_index("y"), lax.axis_index("c")
    return (x, y, c), [(1 - x, y, c), (x, 1 - y, c), (1 - x, 1 - y, c)]


def _exchange_copy(mode, ins, lands, send, recv, a, k, me, peers, arriving):
    p = peers[k]
    theirs = 2 * p[0] + p[1]
    if mode == "gather":
        src, dst = ins[a], lands[a].at[theirs if arriving else me]
    else:
        src, dst = ins[a].at[theirs], lands[a].at[k]
    return pltpu.make_async_remote_copy(src_ref=src, dst_ref=dst, send_sem=send.at[a * 3 + k],
                                        recv_sem=recv.at[a * 3 + k], device_id=p, device_id_type=MESH)


class _Ride:
    def __init__(self, srcs, mode):
        self.srcs, self.mode, self.n = list(srcs), mode, len(srcs)
        n = self.n
        self.in_specs, self.out_specs = [ANY] * n, [ANY] * n
        self.out_shape = [
            jax.ShapeDtypeStruct((N_SHARD,) + s.shape if mode == "gather" else (3,) + s.shape[1:], s.dtype)
            for s in self.srcs]
        self.scratch = [pltpu.SemaphoreType.DMA((3 * n,)), pltpu.SemaphoreType.DMA((3 * n,)),
                        pltpu.SemaphoreType.DMA((n,))]

    def _own(self, ins, lnd, sems):
        if self.mode != "gather":
            return []
        me = 2 * lax.axis_index("x") + lax.axis_index("y")
        return [pltpu.make_async_copy(ins[a], lnd[a].at[me], sems[2].at[a]) for a in range(self.n)]

    def _far(self, ins, lnd, sems, arriving):
        (x, y, c), peers = _place()
        return [_exchange_copy(self.mode, ins, lnd, sems[0], sems[1], a, k, 2 * x + y, peers, arriving)
                for a in range(self.n) for k in range(3)]

    def start(self, ins, lnd, sems):
        for cp in self._own(ins, lnd, sems) + self._far(ins, lnd, sems, False):
            cp.start()

    def finish(self, ins, lnd, sems):
        for cp in self._far(ins, lnd, sems, True):
            cp.wait_recv()
        for cp in self._far(ins, lnd, sems, False):
            cp.wait_send()
        for cp in self._own(ins, lnd, sems):
            cp.wait()


def _gather_two_level(shards, name):
    n = len(shards)

    def body(*refs):
        ins, lnd = refs[:n], refs[n:2 * n]
        send, recv, loc = refs[2 * n:]
        (x, y, c), peers = _place()
        me = 2 * x + y

        def half(ref, a, core):
            rows = shards[a].shape[0] // 2
            return ref.at[pl.ds(core * rows, rows)]

        def copy(a, j, slot, core, to):
            return pltpu.make_async_remote_copy(
                src_ref=half(ins[a], a, core) if j < 3 else half(lnd[a].at[slot], a, core),
                dst_ref=half(lnd[a].at[slot], a, core), send_sem=send.at[6 * a + j], recv_sem=recv.at[6 * a + j],
                device_id=to, device_id_type=MESH)

        own = [pltpu.make_async_copy(ins[a], lnd[a].at[me], loc.at[a]) for a in range(n)]
        far = [copy(a, k, me, c, peers[k]) for a in range(n) for k in range(3)]
        for cp in own + far:
            cp.start()
        passed = []
        for a in range(n):
            for k, p in enumerate(peers):
                theirs = 2 * p[0] + p[1]
                copy(a, k, theirs, c, p).wait_recv()
                passed.append(copy(a, 3 + k, theirs, c, (x, y, 1 - c)))
                passed[-1].start()
        for a in range(n):
            for k, p in enumerate(peers):
                copy(a, 3 + k, 2 * p[0] + p[1], 1 - c, (x, y, 1 - c)).wait_recv()
        for cp in far + passed:
            cp.wait_send()
        for cp in own:
            cp.wait()

    return pl.pallas_call(
        body, name=name, in_specs=[ANY] * n, out_specs=[ANY] * n,
        out_shape=[jax.ShapeDtypeStruct((N_SHARD,) + s.shape, s.dtype) for s in shards],
        scratch_shapes=[pltpu.SemaphoreType.DMA((6 * n,)), pltpu.SemaphoreType.DMA((6 * n,)),
                        pltpu.SemaphoreType.DMA((n,))],
    )(*shards)


def _exchange_packets(packet):
    def body(pk, pk_out, send, recv, loc):
        x, y, c = lax.axis_index("x"), lax.axis_index("y"), lax.axis_index("c")
        lin = 4 * x + 2 * y + c
        own = pltpu.make_async_copy(pk, pk_out.at[lin], loc.at[0])
        own.start()

        def pk_copy(m, slot):
            dev = (x ^ ((m >> 2) & 1), y ^ ((m >> 1) & 1), c ^ (m & 1))
            return pltpu.make_async_remote_copy(
                src_ref=pk, dst_ref=pk_out.at[slot], send_sem=send.at[m - 1], recv_sem=recv.at[m - 1],
                device_id=dev, device_id_type=MESH)

        sent = [pk_copy(m, lin) for m in range(1, N_DEV)]
        for cp in sent:
            cp.start()
        for m in range(1, N_DEV):
            pk_copy(m, lin ^ m).wait_recv()
        for cp in sent:
            cp.wait_send()
        own.wait()

    return pl.pallas_call(
        body, name="exchange_packets", in_specs=[ANY], out_specs=ANY,
        out_shape=jax.ShapeDtypeStruct((N_DEV,) + packet.shape, packet.dtype),
        scratch_shapes=[pltpu.SemaphoreType.DMA((N_DEV - 1,)), pltpu.SemaphoreType.DMA((N_DEV - 1,)),
                        pltpu.SemaphoreType.DMA((1,))],
    )(packet)


def _swap_sibling(parts, name):
    n = len(parts)

    def body(*refs):
        ins, outs = refs[:n], refs[n:2 * n]
        send, recv = refs[2 * n:]
        x, y, c = lax.axis_index("x"), lax.axis_index("y"), lax.axis_index("c")
        cps = [pltpu.make_async_remote_copy(
            src_ref=ins[a], dst_ref=outs[a], send_sem=send.at[a], recv_sem=recv.at[a],
            device_id=(x, y, 1 - c), device_id_type=MESH) for a in range(n)]
        for cp in cps:
            cp.start()
        for cp in cps:
            cp.wait_recv()
        for cp in cps:
            cp.wait_send()

    return pl.pallas_call(
        body, name=name,
        in_specs=[ANY] * n, out_specs=[ANY] * n,
        out_shape=[jax.ShapeDtypeStruct(p.shape, p.dtype) for p in parts],
        scratch_shapes=[pltpu.SemaphoreType.DMA((n,)), pltpu.SemaphoreType.DMA((n,))],
    )(*parts)


BLOCK_ELEMS = 256 * 1024


def _row_tile(R, C):
    tr = max(8, (BLOCK_ELEMS // C) // 8 * 8)
    while R % tr:
        tr -= 8
    return min(tr, R)


def _sum_parts(own, stack, name, out_dtype=f32):
    k = stack.shape[0]
    R, C = stack.shape[1:]
    tr = _row_tile(R, C)

    def body(*refs):
        o_ref = refs[-1]
        acc = refs[0][...].astype(f32)
        for r in refs[1:-1]:
            acc = acc + r[...].astype(f32)
        o_ref[...] = acc.astype(out_dtype)

    row = pl.BlockSpec((tr, C), lambda i: (i, 0))
    specs = ([row] if own is not None else []) + [
        pl.BlockSpec((None, tr, C), functools.partial(lambda i, j: (j, i, 0), j=j)) for j in range(k)]
    args = ([own] if own is not None else []) + [stack] * k
    return pl.pallas_call(
        body, name=name, grid=(R // tr,), in_specs=specs, out_specs=row,
        out_shape=jax.ShapeDtypeStruct((R, C), out_dtype), compiler_params=_params(("parallel",)),
    )(*args)


def _adamw(w, m, v, g_parts, name):
    R, C = w.shape
    tr = _row_tile(R, C)
    n_g = len(g_parts)

    def body(w_ref, m_ref, v_ref, *rest):
        g = rest[0][...]
        for r in rest[1:n_g]:
            g = g + r[...]
        g_ref, d_ref, nm_ref, nv_ref = rest[n_g:]
        nm = ADAM_B1 * m_ref[...] + (1.0 - ADAM_B1) * g
        nv = ADAM_B2 * v_ref[...] + (1.0 - ADAM_B2) * jnp.square(g)
        m_hat = nm / (1.0 - ADAM_B1 ** ADAM_STEP)
        v_hat = nv / (1.0 - ADAM_B2 ** ADAM_STEP)
        g_ref[...] = g
        d_ref[...] = -ADAM_LR * (m_hat / (jnp.sqrt(v_hat) + ADAM_EPS) + ADAM_WD * w_ref[...])
        nm_ref[...] = nm
        nv_ref[...] = nv

    row = pl.BlockSpec((tr, C), lambda i: (i, 0))
    return pl.pallas_call(
        body, name=name, grid=(R // tr,), in_specs=[row] * (3 + n_g), out_specs=[row] * 4,
        out_shape=[jax.ShapeDtypeStruct((R, C), f32)] * 4, compiler_params=_params(("parallel",)),
    )(w, m, v, *g_parts)


BIG = ("w_in", "w_mem_kv", "w_sb_out", "w_ssd_out", "w_mem_out", "w_o", "w_up", "w_down")
LATE = ("w_sb_out", "w_ssd_out", "w_mem_out", "w_o", "w_up", "w_down")
REST = BIG[1:]
COL_SHARDED = ("w_in", "w_mem_kv", "w_up")
SMALL = ("norm_mix_pre", "conv_w", "conv_b", "dt_bias", "a_log", "d_skip", "ssd_norm", "norm_mem",
         "norm_mix_post", "norm_mlp_pre", "norm_mlp_post")
WEIGHTS = ("norm_mix_pre", "w_in", "conv_w", "conv_b", "dt_bias", "a_log", "d_skip", "ssd_norm", "norm_mem",
           "w_mem_kv", "w_sb_out", "w_ssd_out", "w_mem_out", "w_o", "norm_mix_post", "norm_mlp_pre", "w_up",
           "w_down", "norm_mlp_post")
PK_ROWS = 184


def _pack(vecs):
    flat = jnp.concatenate([v.reshape(-1) for v in vecs])
    return jnp.pad(flat, (0, PK_ROWS * 128 - flat.shape[0])).reshape(PK_ROWS, 128)


def _unpack(pk, shapes):
    flat = pk.reshape(-1)
    out, off = [], 0
    for s in shapes:
        n = 1
        for d in s:
            n *= d
        out.append(flat[off:off + n].reshape(s))
        off += n
    return out


def _full_from_slabs(name, slabs):
    if name in COL_SHARDED:
        return slabs.transpose(1, 0, 2).reshape(slabs.shape[1], -1)
    return slabs.reshape(-1, slabs.shape[2])


def _slabs_from_full(name, g):
    if name in COL_SHARDED:
        return g.reshape(g.shape[0], N_SHARD, -1).transpose(1, 0, 2)
    return g.reshape(N_SHARD, -1, g.shape[1])


def kernel(x, mem, norm_mix_pre, w_in, conv_w, conv_b, dt_bias, a_log, d_skip, ssd_norm, norm_mem, w_mem_kv, w_sb_out, w_ssd_out, w_mem_out, w_o, norm_mix_post, norm_mlp_pre, w_up, w_down, norm_mlp_post, loss_target, m_norm_mix_pre, m_w_in, m_conv_w, m_conv_b, m_dt_bias, m_a_log, m_d_skip, m_ssd_norm, m_norm_mem, m_w_mem_kv, m_w_sb_out, m_w_ssd_out, m_w_mem_out, m_w_o, m_norm_mix_post, m_norm_mlp_pre, m_w_up, m_w_down, m_norm_mlp_post, v_norm_mix_pre, v_w_in, v_conv_w, v_conv_b, v_dt_bias, v_a_log, v_d_skip, v_ssd_norm, v_norm_mem, v_w_mem_kv, v_w_sb_out, v_w_ssd_out, v_w_mem_out, v_w_o, v_norm_mix_post, v_norm_mlp_pre, v_w_up, v_w_down, v_norm_mlp_post):
    env = dict(locals())
    w = {n: env[n] for n in WEIGHTS}
    mo = {n: env["m_" + n] for n in WEIGHTS}
    vo = {n: env["v_" + n] for n in WEIGHTS}
    shard = 2 * lax.axis_index("x") + lax.axis_index("y")

    first = _gather_two_level([w["w_in"][0].astype(bf16)], "gather_first")
    w_main, w_dt = _to_internal(_full_from_slabs("w_in", first[0]))
    wts = dict(w_main=w_main, w_dt=w_dt)
    ride_names = (("w_mem_kv",) + LATE[:4], LATE[4:5], LATE[5:])
    late_rides = tuple(_Ride([w[n][0].astype(bf16) for n in names] + ([w["conv_w"][0]] if i == 0 else []), "gather")
                       for i, names in enumerate(ride_names))

    def late_weights(i, lands):
        full = {n: _full_from_slabs(n, s) for n, s in zip(ride_names[i], lands)}
        if i == 0:
            full["conv_w"] = lands[-1].transpose(1, 0, 2).reshape(CONV_K, CONV_DIM)
        return full

    def rest_rides(g):
        slabs = [_slabs_from_full(n, g[n]).astype(bf16) for n in REST]
        return _Ride(slabs[5:], "scatter"), _Ride(slabs[:5], "scatter")

    core = lax.axis_index("c")
    half = D // 2

    def w_in_ride(g):
        ordered = _from_internal(g["w_main"], g["w_dt"])
        stack = jnp.stack([_w_in_slab(ordered, s, bf16) for s in range(N_SHARD)])
        keep = lax.dynamic_slice_in_dim(stack, core * half, half, axis=1)
        away = lax.dynamic_slice_in_dim(stack, (1 - core) * half, half, axis=1)
        (got,) = _swap_sibling([away], "w_in_halves_out")
        wide = lambda a: a.reshape(N_SHARD * half, -1)
        chip = _sum_parts(wide(keep), wide(got)[None], "sum_cores_w_in", bf16).reshape(N_SHARD, half, -1)
        own = lax.switch(shard, [functools.partial(_w_in_slab, ordered, s, f32) for s in range(N_SHARD)])
        own = lax.dynamic_slice_in_dim(own, core * half, half, axis=0)
        g["w_in_own"] = _sum_parts(own, lax.dynamic_index_in_dim(got, shard, 0, keepdims=True), "sum_cores_w_in_own")
        return _Ride([chip], "scatter")

    small = {n: w[n] for n in SMALL if n != "conv_w"}
    loss, grad_x, g = _local_step(x[0], mem[0], loss_target[0], wts, late_rides, late_weights, small,
                                  rest_rides, w_in_ride)
    out_g, out_d, out_m, out_v = {}, {}, {}, {}

    def apply(n, g_parts):
        res = _adamw(w[n][0], mo[n][0], vo[n][0], g_parts, name="adamw_" + n)
        out_g[n], out_d[n], out_m[n], out_v[n] = [r[None] for r in res]

    mine = _sum_parts(g["w_in_own"], g["w_in_lands"][0], name="sum_chips_w_in")
    (theirs,) = _swap_sibling([mine], "w_in_halves_back")
    g_w_in = lax.dynamic_update_slice_in_dim(jnp.zeros((D, D_IN // N_SHARD), f32), mine, core * half, axis=0)
    apply("w_in", [lax.dynamic_update_slice_in_dim(g_w_in, theirs, (1 - core) * half, axis=0)])

    packets = _exchange_packets(_pack([g[n] for n in SMALL] + [loss[:, :1]]))
    partial = []
    for n, r in zip(REST, g["rest_lands"]):
        own = lax.dynamic_index_in_dim(_slabs_from_full(n, g[n]), shard, 0, keepdims=False)
        partial.append(_sum_parts(own, r, name="sum_chips_" + n))
    other = _swap_sibling(partial, "swap_sibling")

    for n, p, q in zip(REST, partial, other):
        apply(n, [p, q])
    tot = _sum_parts(None, packets, name="sum_packets")
    shapes = [g[n].shape for n in SMALL] + [(1, 1)]
    sm = dict(zip(SMALL + ("loss",), _unpack(tot, shapes)))
    sm["conv_w"] = lax.dynamic_slice_in_dim(sm["conv_w"], shard * (CONV_DIM // N_SHARD), CONV_DIM // N_SHARD, axis=1)
    own_small = lambda d: _pack([d[n].reshape(sm[n].shape) for n in SMALL])
    res = _adamw(own_small(w), own_small(mo), own_small(vo), [own_small(sm)], name="adamw_small")
    own_shapes = [sm[n].shape for n in SMALL]
    for store, r in zip((out_g, out_d, out_m, out_v), res):
        for n, val in zip(SMALL, _unpack(r, own_shapes)):
            store[n] = val.reshape(w[n].shape)

    outs = [sm["loss"].reshape(()), grad_x[None]]
    for store in (out_g, out_d, out_m, out_v):
        outs += [store[n] for n in WEIGHTS]
    return tuple(outs)
```

```python
import functools

import jax
import jax.numpy as jnp
from jax import lax
from jax.experimental import pallas as pl
from jax.experimental.pallas import tpu as pltpu

f32 = jnp.float32
bf16 = jnp.bfloat16

D = 1024
EPS = 1e-6
SB_HD = 64
SSD_INNER = 2048
SSD_HEADS = 32
SSD_GROUPS = 4
SSD_N = 128
SSD_L = 128
CONV_K = 4
CONV_DIM = 3072
MEM_HEADS = 4
MEM_HD = 256
D_FF = 4096
D_IN = 12320
N_SHARD = 4
N_DEV = 8

P_QKV, P_XBC, P_GATE, P_MEMQ, P_Z, P_DT, P_TOT = 0, 3072, 6144, 9216, 10240, 12288, 12416
R_QKV, R_Z, R_XBC, R_DT, R_MEMQ, R_GATE = (0, 3072), (3072, 5120), (5120, 8192), (8192, 8224), (8224, 9248), (9248, 12320)

ADAM_LR = 0.001
ADAM_B1 = 0.9
ADAM_B2 = 0.999
ADAM_EPS = 1e-08
ADAM_WD = 0.01
ADAM_STEP = 10

VMEM_LIMIT = 56 * 1024 * 1024

NN = (((1,), (0,)), ((), ()))
NT = (((1,), (1,)), ((), ()))
TN = (((0,), (0,)), ((), ()))


def _dot(a, b, dims=NN):
    return lax.dot_general(a, b, dims, preferred_element_type=f32)


def _params(sem=None):
    return pltpu.CompilerParams(dimension_semantics=sem, vmem_limit_bytes=VMEM_LIMIT)


def _sigmoid(x):
    return 1.0 / (1.0 + jnp.exp(-x))


def _split2(x):
    hi = x.astype(bf16)
    lo = (x - hi.astype(f32)).astype(bf16)
    return hi, lo


def _split3(x):
    hi = x.astype(bf16)
    r = x - hi.astype(f32)
    mid = r.astype(bf16)
    lo = (r - mid.astype(f32)).astype(bf16)
    return hi, mid, lo


def _mm(a, b, mode, *, tm, tn, name, out_dtypes=(f32,), epi=None, extras=(), ride=None):
    M = a.shape[1] if mode == "tn" else a.shape[0]
    N = b.shape[0] if mode == "nt" else b.shape[1]
    tm, tn = min(tm, M), min(tn, N)
    if mode == "nn":
        (M, K), N = a.shape, b.shape[1]
        a_spec = pl.BlockSpec((tm, K), lambda i, j: (i, 0))
        b_spec = pl.BlockSpec((K, tn), lambda i, j: (0, j))
        dims = NN
    elif mode == "nt":
        (M, K), N = a.shape, b.shape[0]
        a_spec = pl.BlockSpec((tm, K), lambda i, j: (i, 0))
        b_spec = pl.BlockSpec((tn, K), lambda i, j: (j, 0))
        dims = NT
    else:
        (K, M), N = a.shape, b.shape[1]
        a_spec = pl.BlockSpec((K, tm), lambda i, j: (0, i))
        b_spec = pl.BlockSpec((K, tn), lambda i, j: (0, j))
        dims = TN
    assert M % tm == 0 and N % tn == 0, (name, M, N, tm, tn)
    n_ex, n_out = len(extras), len(out_dtypes)
    n_r = ride.n if ride else 0
    o_spec = pl.BlockSpec((tm, tn), lambda i, j: (i, j))
    grid = (M // tm, N // tn)

    def body(a_ref, b_ref, *rest):
        r_ins = rest[n_ex:n_ex + n_r]
        outs = rest[n_ex + n_r:n_ex + n_r + n_out]
        r_lnd, r_sems = rest[n_ex + n_r + n_out:n_ex + 2 * n_r + n_out], rest[n_ex + 2 * n_r + n_out:]
        i, j = pl.program_id(0), pl.program_id(1)
        if ride:
            pl.when((i == 0) & (j == 0))(lambda: ride.start(r_ins, r_lnd, r_sems))
        acc = _dot(a_ref[...].astype(bf16), b_ref[...].astype(bf16), dims)
        res = (acc,) if epi is None else epi(acc, *[e[...] for e in rest[:n_ex]])
        for o_ref, r in zip(outs, res):
            o_ref[...] = r.astype(o_ref.dtype)
        if ride:
            pl.when((i == grid[0] - 1) & (j == grid[1] - 1))(lambda: ride.finish(r_ins, r_lnd, r_sems))

    out = pl.pallas_call(
        body, name=name, grid=grid,
        in_specs=[a_spec, b_spec] + [o_spec] * n_ex + (ride.in_specs if ride else []),
        out_specs=[o_spec] * n_out + (ride.out_specs if ride else []),
        out_shape=[jax.ShapeDtypeStruct((M, N), dt) for dt in out_dtypes] + (ride.out_shape if ride else []),
        scratch_shapes=ride.scratch if ride else [],
        compiler_params=_params(("arbitrary", "arbitrary") if ride else ("parallel", "parallel")),
    )(a, b, *extras, *(ride.srcs if ride else []))
    if ride:
        return (out[0] if n_out == 1 else out[:n_out]), list(out[n_out:])
    return out[0] if n_out == 1 else out


def _mm_pieces_nt(pieces, b, add, *, tm, tn, name, ride):
    M, N = pieces[0].shape[0], b.shape[0]
    n_p, n_r = len(pieces), (ride.n if ride else 0)
    o_spec = pl.BlockSpec((tm, tn), lambda i, j: (i, j))
    grid = (M // tm, N // tn)

    def body(*refs):
        b_ref, add_ref = refs[n_p:n_p + 2]
        r_ins, o_ref = refs[n_p + 2:n_p + 2 + n_r], refs[n_p + 2 + n_r]
        r_lnd, r_sems = refs[n_p + 3 + n_r:n_p + 3 + 2 * n_r], refs[n_p + 3 + 2 * n_r:]
        i, j = pl.program_id(0), pl.program_id(1)
        if ride:
            pl.when((i == 0) & (j == 0))(lambda: ride.start(r_ins, r_lnd, r_sems))
        acc, off = add_ref[...], 0
        for r in refs[:n_p]:
            acc = acc + _dot(r[...], b_ref[:, off:off + r.shape[1]], NT)
            off += r.shape[1]
        o_ref[...] = acc
        if ride:
            pl.when((i == grid[0] - 1) & (j == grid[1] - 1))(lambda: ride.finish(r_ins, r_lnd, r_sems))

    out = pl.pallas_call(
        body, name=name, grid=grid,
        in_specs=[pl.BlockSpec((tm, p.shape[1]), lambda i, j: (i, 0)) for p in pieces]
        + [pl.BlockSpec((tn, b.shape[1]), lambda i, j: (j, 0)), o_spec] + (ride.in_specs if ride else []),
        out_specs=[o_spec] + (ride.out_specs if ride else []),
        out_shape=[jax.ShapeDtypeStruct((M, N), f32)] + (ride.out_shape if ride else []),
        scratch_shapes=ride.scratch if ride else [],
        compiler_params=_params(("arbitrary", "arbitrary")),
    )(*pieces, b, add, *(ride.srcs if ride else []))
    return out[0], list(out[1:])


def _rms_fwd(x, g, *, name, out_dtype, residual=None, tm=512):
    S, C = x.shape
    tm = min(tm, S)
    has_res = residual is not None

    def body(x_ref, g_ref, *rest):
        xv = x_ref[...]
        r = lax.rsqrt(jnp.mean(xv * xv, axis=1, keepdims=True) + EPS)
        y = xv * r * g_ref[...]
        if has_res:
            y = y + rest[0][...]
        rest[-1][...] = y.astype(out_dtype)

    row = pl.BlockSpec((tm, C), lambda i: (i, 0))
    vec = pl.BlockSpec((1, C), lambda i: (0, 0))
    args = (x, g) + ((residual,) if has_res else ())
    return pl.pallas_call(
        body, name=name, grid=(S // tm,),
        in_specs=[row, vec] + ([row] if has_res else []),
        out_specs=row, out_shape=jax.ShapeDtypeStruct((S, C), out_dtype),
        compiler_params=_params(("parallel",)),
    )(*args)


def _rms_bwd(x, dy, g, *, name, out_dtype, add=None, tm=512):
    S, C = x.shape
    tm = min(tm, S)
    has_add = add is not None

    def body(x_ref, dy_ref, g_ref, *rest):
        dx_ref, dg_ref = rest[-2], rest[-1]
        xv = x_ref[...]
        dyv = dy_ref[...].astype(f32)
        r = lax.rsqrt(jnp.mean(xv * xv, axis=1, keepdims=True) + EPS)
        xh = xv * r
        dxh = dyv * g_ref[...]
        dx = r * (dxh - xh * jnp.mean(dxh * xh, axis=1, keepdims=True))
        if has_add:
            dx = dx + rest[0][...]
        dx_ref[...] = dx.astype(out_dtype)

        @pl.when(pl.program_id(0) == 0)
        def _():
            dg_ref[...] = jnp.zeros_like(dg_ref)

        dg_ref[...] += jnp.sum(dyv * xh, axis=0, keepdims=True)

    row = pl.BlockSpec((tm, C), lambda i: (i, 0))
    vec = pl.BlockSpec((1, C), lambda i: (0, 0))
    args = (x, dy, g) + ((add,) if has_add else ())
    return pl.pallas_call(
        body, name=name, grid=(S // tm,),
        in_specs=[row, row, vec] + ([row] if has_add else []),
        out_specs=[row, vec],
        out_shape=[jax.ShapeDtypeStruct((S, C), out_dtype), jax.ShapeDtypeStruct((1, C), f32)],
        compiler_params=_params(("arbitrary",)),
    )(*args)


SB_T = 128
SB_SPENT = -120.0
SB_TAIL = 3
SB_GROUPS = (4, 2, 1)
SB_GROUPS_BWD = (4, 2, 1)


def _sb_masks():
    lane = lax.broadcasted_iota(jnp.int32, (1, 128), 1)
    m_a = (lane < SB_HD).astype(f32)
    return m_a, 1.0 - m_a


def _chunks(a, n):
    return [a[:, u * SB_T:(u + 1) * SB_T] for u in range(n)]


def _cat(parts, axis):
    return parts[0] if len(parts) == 1 else jnp.concatenate(parts, axis=axis)


def _mask_last(a, n, mask):
    if mask is None:
        return a
    parts = _chunks(a, n)
    return _cat(parts[:-1] + [jnp.where(mask, parts[-1], 0.0)], 1)


def _sb_logits(z, n, mask):
    l1p = jnp.log(1.0 + jnp.exp(-jnp.abs(z)))
    lb = jnp.minimum(z, 0.0) - l1p
    return lb, _mask_last(lb - z, n, mask)


def _by_count(i, most, fn):
    return lax.switch(jnp.minimum(i, most - 1), [functools.partial(fn, n) for n in range(1, most + 1)])


def _chunk_matmul(parts_list, u_mat):
    out = _dot(_cat(parts_list, 0), u_mat)
    return [out[u * SB_T:(u + 1) * SB_T] for u in range(len(parts_list))]


def _chunk_cumsum(lk, n, u_mat):
    hi = lk.astype(bf16)
    lo = (lk - hi.astype(f32)).astype(bf16)
    out = _chunk_matmul(_chunks(hi, n) + _chunks(lo, n), u_mat)
    return [out[u] + out[n + u] for u in range(n)]


def _sb_fwd(proj, S, ride=None):
    nq = S // SB_T
    n_pairs = D // 128
    scale = SB_HD ** -0.5
    n_r = ride.n if ride else 0

    def body(q_ref, k_ref, v_ref, *rest):
        o_ref, t_ref = rest[n_r:n_r + 2]
        i = pl.program_id(1)
        if ride:
            pl.when((pl.program_id(0) == 0) & (i == 0))(
                lambda: ride.start(rest[:n_r], rest[n_r + 2:2 * n_r + 2], rest[2 * n_r + 2:]))
        m_a, m_b = _sb_masks()
        r_i = lax.broadcasted_iota(jnp.int32, (SB_T, SB_T), 0)
        c_i = lax.broadcasted_iota(jnp.int32, (SB_T, SB_T), 1)
        u_mat = (r_i > c_i).astype(bf16)
        causal = c_i < r_i
        q = q_ref[...] * scale
        q_h = ((q * m_a).astype(bf16), (q * m_b).astype(bf16))

        def group(j_lo, n, carry, mask):
            acc, c_a, c_b = carry
            rows = pl.ds(pl.multiple_of(j_lo * SB_T, SB_T), n * SB_T)
            k = k_ref[rows, :].astype(bf16)
            v = v_ref[rows, :]
            zs = [_dot(q_b, k, NT) for q_b in q_h]
            lbk = [_sb_logits(z, n, mask) for z in zs]
            parts = [_chunk_cumsum(lk, n, u_mat) for _, lk in lbk]
            ws, cs = [], []
            for (lb, lk), part, c in zip(lbk, parts, (c_a, c_b)):
                lb_c, lk_c = _chunks(lb, n), _chunks(lk, n)
                w_c = [None] * n
                for u in reversed(range(n)):
                    w_c[u] = jnp.exp(lb_c[u] + c + part[u])
                    c = c + jnp.sum(lk_c[u], axis=1, keepdims=True)
                ws.append(_mask_last(_cat(w_c, 1), n, mask).astype(bf16))
                cs.append(c)
            for w, m in zip(ws, (m_a, m_b)):
                acc = acc + _dot(w, (v * m).astype(bf16))
            return acc, cs[0], cs[1]

        zero_c = jnp.zeros((SB_T, 1), f32)
        init = (jnp.zeros((SB_T, 128), f32), zero_c, zero_c)
        carry = _by_count(i, SB_TAIL, lambda n: group(i - n + 1, n, init, causal))

        def spent(cr):
            return (jnp.max(jnp.maximum(cr[1], cr[2])) < SB_SPENT).astype(jnp.int32)

        state = (i - jnp.minimum(i, SB_TAIL - 1), spent(carry), carry)
        for n in SB_GROUPS:
            def step(st, n=n):
                left, _, cr = st
                cr = group(left - n, n, cr, None)
                return left - n, spent(cr), cr

            state = lax.while_loop(lambda st, n=n: (st[0] >= n) & (st[1] == 0), step, state)
        left, _, carry = state
        o_ref[...] = carry[0]
        lane = lax.broadcasted_iota(jnp.int32, (1, 128), 1)
        t_ref[...] = (jnp.where(lane == 0, carry[1], 0.0) + jnp.where(lane == SB_HD, carry[2], 0.0)
                      + jnp.where(lane == 1, left.astype(f32), 0.0))
        if ride:
            pl.when((pl.program_id(0) == n_pairs - 1) & (i == nq - 1))(
                lambda: ride.finish(rest[:n_r], rest[n_r + 2:2 * n_r + 2], rest[2 * n_r + 2:]))

    qs = pl.BlockSpec((SB_T, 128), lambda h, i: (i, h))
    out = pl.pallas_call(
        body, name="sb_fwd", grid=(n_pairs, nq),
        in_specs=[qs,
                  pl.BlockSpec((S, 128), lambda h, i: (0, n_pairs + h)),
                  pl.BlockSpec((S, 128), lambda h, i: (0, 2 * n_pairs + h))] + (ride.in_specs if ride else []),
        out_specs=[qs, qs] + (ride.out_specs if ride else []),
        out_shape=[jax.ShapeDtypeStruct((S, D), f32)] * 2 + (ride.out_shape if ride else []),
        scratch_shapes=ride.scratch if ride else [],
        compiler_params=_params(("arbitrary", "arbitrary")),
    )(proj, proj, proj, *(ride.srcs if ride else []))
    return out[0], out[1], list(out[2:])


def _sb_bwd(proj, tot_lk, do, S, ride=None):
    nq = S // SB_T
    n_pairs = D // 128
    scale = SB_HD ** -0.5
    n_r = ride.n if ride else 0

    def body(q_ref, k_ref, v_ref, t_ref, do_ref, *rest):
        dq_ref, dk_ref, dv_ref = rest[n_r:n_r + 3]
        dk_acc, dv_acc = rest[2 * n_r + 3:2 * n_r + 5]
        r_ins, r_lnd, r_sems = rest[:n_r], rest[n_r + 3:2 * n_r + 3], rest[2 * n_r + 5:]
        i = pl.program_id(1)
        if ride:
            pl.when((pl.program_id(0) == 0) & (i == 0))(lambda: ride.start(r_ins, r_lnd, r_sems))
        m_a, m_b = _sb_masks()
        r_i = lax.broadcasted_iota(jnp.int32, (SB_T, SB_T), 0)
        c_i = lax.broadcasted_iota(jnp.int32, (SB_T, SB_T), 1)
        u_inc = (r_i <= c_i).astype(bf16)
        u_exc = (r_i < c_i).astype(bf16)
        causal = c_i < r_i

        @pl.when(i == 0)
        def _():
            dk_acc[...] = jnp.zeros_like(dk_acc)
            dv_acc[...] = jnp.zeros_like(dv_acc)

        q = q_ref[...] * scale
        dov = do_ref[...]
        tv = t_ref[...]
        lane = lax.broadcasted_iota(jnp.int32, (1, 128), 1)
        heads = []
        for m, first in ((m_a, 0), (m_b, SB_HD)):
            tot = jnp.sum(jnp.where(lane == first, tv, 0.0), axis=1, keepdims=True)
            heads.append(((q * m).astype(bf16), (dov * m).astype(bf16), tot, m))
        lowest = jnp.clip(jnp.max(jnp.where(lane == 1, tv, 0.0)).astype(jnp.int32), 0, i)

        def group(j_lo, n, carry, mask):
            dq_acc, cp_a, cp_b, ce_a, ce_b = carry
            rows = pl.ds(pl.multiple_of(j_lo * SB_T, SB_T), n * SB_T)
            k_f = k_ref[rows, :]
            k = k_f.astype(bf16)
            v = v_ref[rows, :].astype(bf16)
            zs = [_dot(h[0], k, NT) for h in heads]
            dws = [_dot(h[1], v, NT) for h in heads]
            lbk = [_sb_logits(z, n, mask) for z in zs]
            parts = [_chunk_cumsum(lk, n, u_inc) for _, lk in lbk]
            ws, es, cps = [], [], []
            for (lb, lk), part, dw, h, cp in zip(lbk, parts, dws, heads, (cp_a, cp_b)):
                lb_c, lk_c = _chunks(lb, n), _chunks(lk, n)
                w_c = []
                for u in range(n):
                    w_c.append(jnp.exp(lb_c[u] + (h[2] - cp) - part[u]))
                    cp = cp + jnp.sum(lk_c[u], axis=1, keepdims=True)
                w = _mask_last(_cat(w_c, 1), n, mask)
                ws.append(w)
                es.append(dw * w)
                cps.append(cp)
            e_parts = [_chunk_matmul(_chunks(e.astype(bf16), n), u_exc) for e in es]
            dzs, ces = [], []
            for (lb, _), e, e_part, ce in zip(lbk, es, e_parts, (ce_a, ce_b)):
                e_c = _chunks(e, n)
                big_c = []
                for u in range(n):
                    big_c.append(ce + e_part[u])
                    ce = ce + jnp.sum(e_c[u], axis=1, keepdims=True)
                sig = jnp.exp(lb)
                dz = _mask_last(e * (1.0 - sig) - _cat(big_c, 1) * sig, n, mask)
                dzs.append(dz.astype(bf16))
                ces.append(ce)
            dk_t = jnp.zeros((n * SB_T, 128), f32)
            dv_t = jnp.zeros((n * SB_T, 128), f32)
            for dz_b, w, h in zip(dzs, ws, heads):
                dq_acc = dq_acc + _dot(dz_b, (k_f * h[3]).astype(bf16))
                dk_t = dk_t + _dot(dz_b, h[0], TN)
                dv_t = dv_t + _dot(w.astype(bf16), h[1], TN)
            dk_acc[rows, :] += dk_t
            dv_acc[rows, :] += dv_t
            return dq_acc, cps[0], cps[1], ces[0], ces[1]

        zc = jnp.zeros((SB_T, 1), f32)
        carry = (jnp.zeros((SB_T, 128), f32), zc, zc, zc, zc)
        done = lowest
        tail_lo = i - jnp.minimum(i, SB_TAIL - 1)
        for n in SB_GROUPS_BWD:
            trips = (tail_lo - done) // n
            carry = lax.fori_loop(
                0, trips, functools.partial(lambda gi, cr, n, done: group(done + gi * n, n, cr, None), n=n, done=done),
                carry)
            done = done + trips * n
        carry = _by_count(i, SB_TAIL, lambda n: group(i - n + 1, n, carry, causal))
        dq_ref[...] = (carry[0] * scale).astype(bf16)

        @pl.when(i == nq - 1)
        def _():
            dk_ref[...] = dk_acc[...].astype(bf16)
            dv_ref[...] = dv_acc[...].astype(bf16)

        if ride:
            pl.when((pl.program_id(0) == n_pairs - 1) & (i == nq - 1))(
                lambda: ride.finish(r_ins, r_lnd, r_sems))

    qs = pl.BlockSpec((SB_T, 128), lambda h, i: (i, h))
    full = pl.BlockSpec((S, 128), lambda h, i: (0, h))
    out = pl.pallas_call(
        body, name="sb_bwd", grid=(n_pairs, nq),
        in_specs=[qs,
                  pl.BlockSpec((S, 128), lambda h, i: (0, n_pairs + h)),
                  pl.BlockSpec((S, 128), lambda h, i: (0, 2 * n_pairs + h)),
                  qs, qs] + (ride.in_specs if ride else []),
        out_specs=[qs, full, full] + (ride.out_specs if ride else []),
        out_shape=[jax.ShapeDtypeStruct((S, D), bf16)] * 3 + (ride.out_shape if ride else []),
        scratch_shapes=[pltpu.VMEM((S, 128), f32), pltpu.VMEM((S, 128), f32)] + (ride.scratch if ride else []),
        compiler_params=_params(("arbitrary", "arbitrary")),
    )(proj, proj, proj, tot_lk, do, *(ride.srcs if ride else []))
    return out[0], out[1], out[2], list(out[3:])


CONV_CB = 256
HALO = 8


def _conv_fwd(proj, conv_w, conv_b, S):
    tr = min(512, S)

    def body(x_ref, w_ref, b_ref, xc_ref, xbc_ref):
        w = w_ref[...]
        for t in range(S // tr):
            cur = x_ref[t * tr:(t + 1) * tr, :]
            halo = x_ref[t * tr - HALO:t * tr, :] if t else jnp.zeros((HALO, CONV_CB), f32)
            win = jnp.concatenate([halo, cur], axis=0)
            acc = b_ref[...] + w[CONV_K - 1:CONV_K, :] * cur
            for k in range(CONV_K - 1):
                acc = acc + w[k:k + 1, :] * pltpu.roll(win, CONV_K - 1 - k, 0)[HALO:, :]
            xc_ref[t * tr:(t + 1) * tr, :] = acc
            xbc_ref[t * tr:(t + 1) * tr, :] = acc * _sigmoid(acc)

    col = pl.BlockSpec((S, CONV_CB), lambda c: (0, c))
    return pl.pallas_call(
        body, name="conv_fwd", grid=(CONV_DIM // CONV_CB,),
        in_specs=[pl.BlockSpec((S, CONV_CB), lambda c: (0, P_XBC // CONV_CB + c)),
                  pl.BlockSpec((CONV_K, CONV_CB), lambda c: (0, c)),
                  pl.BlockSpec((1, CONV_CB), lambda c: (0, c))],
        out_specs=[col, col], out_shape=[jax.ShapeDtypeStruct((S, CONV_DIM), f32)] * 2,
        compiler_params=_params(("parallel",)),
    )(proj, conv_w, conv_b)


def _conv_bwd(proj, xc, dxbc, conv_w, S):
    tr = min(512, S)

    def body(x_ref, xc_ref, dy_ref, w_ref, dx_ref, dw_ref, db_ref, dxc_s):
        w = w_ref[...]
        xcv = xc_ref[...]
        sg = _sigmoid(xcv)
        dxc_s[0:S, :] = dy_ref[...] * (sg * (1.0 + xcv * (1.0 - sg)))
        dxc_s[S:S + HALO, :] = jnp.zeros((HALO, CONV_CB), f32)
        dws = [jnp.zeros((1, CONV_CB), f32) for _ in range(CONV_K)]
        db = jnp.zeros((1, CONV_CB), f32)
        for t in range(S // tr):
            cur = x_ref[t * tr:(t + 1) * tr, :]
            halo = x_ref[t * tr - HALO:t * tr, :] if t else jnp.zeros((HALO, CONV_CB), f32)
            win = jnp.concatenate([halo, cur], axis=0)
            dwin = dxc_s[t * tr:(t + 1) * tr + HALO, :]
            dcur = dwin[0:tr, :]
            db = db + jnp.sum(dcur, axis=0, keepdims=True)
            dws[CONV_K - 1] = dws[CONV_K - 1] + jnp.sum(dcur * cur, axis=0, keepdims=True)
            dx = w[CONV_K - 1:CONV_K, :] * dcur
            for k in range(CONV_K - 1):
                sh = CONV_K - 1 - k
                dws[k] = dws[k] + jnp.sum(dcur * pltpu.roll(win, sh, 0)[HALO:, :], axis=0, keepdims=True)
                dx = dx + w[k:k + 1, :] * pltpu.roll(dwin, tr + HALO - sh, 0)[0:tr, :]
            dx_ref[t * tr:(t + 1) * tr, :] = dx.astype(bf16)
        dw_ref[...] = jnp.concatenate(dws + [jnp.zeros((8 - CONV_K, CONV_CB), f32)], axis=0)
        db_ref[...] = db

    col = pl.BlockSpec((S, CONV_CB), lambda c: (0, c))
    return pl.pallas_call(
        body, name="conv_bwd", grid=(CONV_DIM // CONV_CB,),
        in_specs=[pl.BlockSpec((S, CONV_CB), lambda c: (0, P_XBC // CONV_CB + c)), col, col,
                  pl.BlockSpec((CONV_K, CONV_CB), lambda c: (0, c))],
        out_specs=[col, pl.BlockSpec((8, CONV_CB), lambda c: (0, c)), pl.BlockSpec((1, CONV_CB), lambda c: (0, c))],
        out_shape=[jax.ShapeDtypeStruct((S, CONV_DIM), bf16), jax.ShapeDtypeStruct((8, CONV_DIM), f32),
                   jax.ShapeDtypeStruct((1, CONV_DIM), f32)],
        scratch_shapes=[pltpu.VMEM((S + HALO, CONV_CB), f32)],
        compiler_params=_params(("parallel",)),
    )(proj, xc, dxbc, conv_w)


N_PAIR = SSD_HEADS // 2
NEG = -1e30


def _softplus(x):
    return jnp.maximum(x, 0.0) + jnp.log(1.0 + jnp.exp(-jnp.abs(x)))


def _ssd_common(dtr, dtb, alog):
    L = SSD_L
    r_i = lax.broadcasted_iota(jnp.int32, (L, L), 0)
    c_i = lax.broadcasted_iota(jnp.int32, (L, L), 1)
    dt = _softplus(dtr + dtb)
    a = -jnp.exp(alog)
    da = dt * a
    lower = (r_i >= c_i).astype(bf16)
    upper = (r_i <= c_i).astype(bf16)
    parts = _split3(da)
    a_cs = sum(_dot(lower, p) for p in parts)
    a_cs_t = sum(_dot(p, upper, TN) for p in parts)
    return dt, a, a_cs, a_cs_t, r_i >= c_i


def _pair_vec(lane, v, h):
    return jnp.where(lane < SB_HD, v[:, h:h + 1], v[:, h + 1:h + 2])


def _decay_mat(a_cs, a_cs_t, h, tril):
    return jnp.exp(jnp.where(tril, a_cs[:, h:h + 1] - a_cs_t[h:h + 1, :], NEG))


def _ssd_fwd(xbc, proj, pdt, dt_bias_p, a_log_p, d_skip_c, ssd_norm, S, ride=None):
    L = SSD_L
    nc = S // L
    n_r = ride.n if ride else 0

    def body(xbc_ref, dt_ref, z_ref, dtb_ref, alog_ref, dsk_ref, gn_ref, *rest):
        y_ref, yn_ref, hp_ref = rest[n_r:n_r + 3]
        state = rest[2 * n_r + 3]
        r_ins, r_lnd, r_sems = rest[:n_r], rest[n_r + 3:2 * n_r + 3], rest[2 * n_r + 4:]
        c = pl.program_id(0)
        if ride:
            pl.when(c == 0)(lambda: ride.start(r_ins, r_lnd, r_sems))

        @pl.when(c == 0)
        def _():
            state[...] = jnp.zeros_like(state)

        hp_ref[0] = state[...]
        lane = lax.broadcasted_iota(jnp.int32, (1, 128), 1)
        row128 = lax.broadcasted_iota(jnp.int32, (128, 1), 0)
        m_a, m_b = _sb_masks()
        dt, a, a_cs, a_cs_t, tril = _ssd_common(dt_ref[...], dtb_ref[...], alog_ref[...])
        a_last = a_cs[L - 1:L, :]
        for g in range(SSD_GROUPS):
            b_g = xbc_ref[:, SSD_INNER + g * SSD_N:SSD_INNER + (g + 1) * SSD_N].astype(bf16)
            c_g = xbc_ref[:, SSD_INNER + (SSD_GROUPS + g) * SSD_N:SSD_INNER + (SSD_GROUPS + g + 1) * SSD_N].astype(bf16)
            cb = _dot(c_g, b_g, NT)
            for pr in range(4):
                h = 8 * g + 2 * pr
                pi = h // 2
                cols = slice(pi * 128, (pi + 1) * 128)
                xs = xbc_ref[:, cols]
                x = xs * _pair_vec(lane, dt, h)
                acs = _pair_vec(lane, a_cs, h)
                al = _pair_vec(lane, a_last, h)
                w_a = (cb * _decay_mat(a_cs, a_cs_t, h, tril)).astype(bf16)
                w_b = (cb * _decay_mat(a_cs, a_cs_t, h + 1, tril)).astype(bf16)
                yd = _dot(w_a, (x * m_a).astype(bf16)) + _dot(w_b, (x * m_b).astype(bf16))
                hp = state[pi]
                yo = _dot(c_g, hp.astype(bf16), NT) * jnp.exp(acs)
                y_ref[:, cols] = yd + yo + dsk_ref[:, cols] * xs
                dec = jnp.exp(jnp.where(row128 < SB_HD, a_last[:, h:h + 1], a_last[:, h + 1:h + 2]))
                state[pi] = hp * dec + _dot((x * jnp.exp(al - acs)).astype(bf16), b_g, TN)
        zz = z_ref[...]
        y2 = y_ref[...] * (zz * _sigmoid(zz))
        gw = SSD_INNER // SSD_GROUPS
        for g in range(SSD_GROUPS):
            yg = y2[:, g * gw:(g + 1) * gw]
            rg = lax.rsqrt(jnp.mean(yg * yg, axis=1, keepdims=True) + EPS)
            yn_ref[:, g * gw:(g + 1) * gw] = (yg * rg * gn_ref[:, g * gw:(g + 1) * gw]).astype(bf16)
        if ride:
            pl.when(c == nc - 1)(lambda: ride.finish(r_ins, r_lnd, r_sems))

    vec128 = pl.BlockSpec((1, 128), lambda c: (0, 0))
    vecin = pl.BlockSpec((1, SSD_INNER), lambda c: (0, 0))
    rows = pl.BlockSpec((L, SSD_INNER), lambda c: (c, 0))
    out = pl.pallas_call(
        body, name="ssd_fwd", grid=(nc,),
        in_specs=[pl.BlockSpec((L, CONV_DIM), lambda c: (c, 0)),
                  pl.BlockSpec((L, 128), lambda c: (c, 0)),
                  pl.BlockSpec((L, SSD_INNER), lambda c: (c, P_Z // SSD_INNER)),
                  vec128, vec128, vecin, vecin] + (ride.in_specs if ride else []),
        out_specs=[rows, rows, pl.BlockSpec((1, N_PAIR, 128, SSD_N), lambda c: (c, 0, 0, 0))]
        + (ride.out_specs if ride else []),
        out_shape=[jax.ShapeDtypeStruct((S, SSD_INNER), f32), jax.ShapeDtypeStruct((S, SSD_INNER), bf16),
                   jax.ShapeDtypeStruct((nc, N_PAIR, 128, SSD_N), f32)] + (ride.out_shape if ride else []),
        scratch_shapes=[pltpu.VMEM((N_PAIR, 128, SSD_N), f32)] + (ride.scratch if ride else []),
        compiler_params=_params(("arbitrary",)),
    )(xbc, pdt, proj, dt_bias_p, a_log_p, d_skip_c, ssd_norm, *(ride.srcs if ride else []))
    return out[0], out[1], out[2], list(out[3:])


def _sum_all(v):
    return jnp.sum(jnp.sum(v, axis=1, keepdims=True), axis=0, keepdims=True)


def _ssd_bwd(dyn, y, xbc, proj, pdt, hprev, dt_bias_p, a_log_p, d_skip_c, ssd_norm, S, ride=None):
    L = SSD_L
    nc = S // L
    n_r = ride.n if ride else 0

    col = lax.broadcasted_iota(jnp.int32, (2 * SSD_INNER, 128), 0)
    head = lax.broadcasted_iota(jnp.int32, (2 * SSD_INNER, 128), 1)
    sel_pair = (col[:SSD_INNER] // SB_HD == head[:SSD_INNER]).astype(bf16)
    sel_head = (col // 128 == head).astype(bf16)

    def body(*refs):
        (dyn_ref, y_ref, xbc_ref, dt_ref, z_ref, hp_ref, dtb_ref, alog_ref, dsk_ref, gn_ref,
         selp_ref, selh_ref) = refs[:12]
        dz_ref, dxbc_ref, ddt_ref, dgn_ref, dsk_out, dalog_ref, ddtb_ref = refs[12 + n_r:19 + n_r]
        dstate, dy_s, st_a, st_q, st_d, st_x, dat = refs[19 + 2 * n_r:26 + 2 * n_r]
        r_ins, r_lnd, r_sems = refs[12:12 + n_r], refs[19 + n_r:19 + 2 * n_r], refs[26 + 2 * n_r:]
        c = pl.program_id(0)
        if ride:
            pl.when(c == 0)(lambda: ride.start(r_ins, r_lnd, r_sems))

        @pl.when(c == 0)
        def _():
            dat[...] = jnp.zeros_like(dat)
            dstate[...] = jnp.zeros_like(dstate)
            dgn_ref[...] = jnp.zeros_like(dgn_ref)
            dsk_out[...] = jnp.zeros_like(dsk_out)
            dalog_ref[...] = jnp.zeros_like(dalog_ref)
            ddtb_ref[...] = jnp.zeros_like(ddtb_ref)

        lane = lax.broadcasted_iota(jnp.int32, (1, 128), 1)
        row128 = lax.broadcasted_iota(jnp.int32, (128, 1), 0)
        rowl = lax.broadcasted_iota(jnp.int32, (L, 1), 0)
        m_a, m_b = _sb_masks()
        dtr = dt_ref[...]
        dt, a, a_cs, a_cs_t, tril = _ssd_common(dtr, dtb_ref[...], alog_ref[...])
        a_last = a_cs[L - 1:L, :]

        zz = z_ref[...]
        sg = _sigmoid(zz)
        silu = zz * sg
        yv = y_ref[...]
        y2 = yv * silu
        gw = SSD_INNER // SSD_GROUPS
        for g in range(SSD_GROUPS):
            sl = slice(g * gw, (g + 1) * gw)
            yg = y2[:, sl]
            rg = lax.rsqrt(jnp.mean(yg * yg, axis=1, keepdims=True) + EPS)
            yh = yg * rg
            dyn_g = dyn_ref[:, sl]
            dgn_ref[:, sl] += jnp.sum(dyn_g * yh, axis=0, keepdims=True)
            dyh = dyn_g * gn_ref[:, sl]
            dy2 = rg * (dyh - yh * jnp.mean(dyh * yh, axis=1, keepdims=True))
            dy_s[:, sl] = dy2 * silu[:, sl]
            dz_ref[:, sl] = (dy2 * yv[:, sl] * (sg[:, sl] * (1.0 + zz[:, sl] * (1.0 - sg[:, sl])))).astype(bf16)

        last_row = jnp.zeros((1, 128), f32)
        dsk_acc = jnp.zeros((1, 128), f32)
        for g in range(SSD_GROUPS):
            bsl = slice(SSD_INNER + g * SSD_N, SSD_INNER + (g + 1) * SSD_N)
            csl = slice(SSD_INNER + (SSD_GROUPS + g) * SSD_N, SSD_INNER + (SSD_GROUPS + g + 1) * SSD_N)
            b_g = xbc_ref[:, bsl].astype(bf16)
            c_g = xbc_ref[:, csl].astype(bf16)
            cb = _dot(c_g, b_g, NT)
            dcb = jnp.zeros((L, L), f32)
            dc_g = jnp.zeros((L, SSD_N), f32)
            db_g = jnp.zeros((L, SSD_N), f32)
            for pr in range(4):
                h = 8 * g + 2 * pr
                pi = h // 2
                cols = slice(pi * 128, (pi + 1) * 128)
                xs = xbc_ref[:, cols]
                dt_p = _pair_vec(lane, dt, h)
                x = xs * dt_p
                acs = _pair_vec(lane, a_cs, h)
                al = _pair_vec(lane, a_last, h)
                e_a = jnp.exp(acs)
                dte = jnp.exp(al - acs)
                m_mat_a = _decay_mat(a_cs, a_cs_t, h, tril)
                m_mat_b = _decay_mat(a_cs, a_cs_t, h + 1, tril)
                dyp = dy_s[:, cols]
                dsk = dsk_ref[:, cols]
                d_hn = dstate[pi]
                hp = hp_ref[0, pi]
                dy_a = (dyp * m_a).astype(bf16)
                dy_b = (dyp * m_b).astype(bf16)
                x_b = x.astype(bf16)
                gm_a = _dot(dy_a, x_b, NT) * m_mat_a
                gm_b = _dot(dy_b, x_b, NT) * m_mat_b
                dcb = dcb + gm_a + gm_b
                dx_d = _dot((cb * m_mat_a).astype(bf16), dy_a, TN) + _dot((cb * m_mat_b).astype(bf16), dy_b, TN)
                dx_s = _dot(b_g, d_hn.astype(bf16), NT) * dte
                dx = dx_d + dx_s
                dxbc_ref[:, cols] = dx * dt_p + dsk * dyp
                xdxs = x * dx_s
                st_x[:, cols] = xdxs
                st_a[:, cols] = dyp * (_dot(c_g, hp.astype(bf16), NT) * e_a) - xdxs
                st_d[:, cols] = dx * xs
                hh = d_hn * hp
                dsk_row = jnp.sum(dyp * xs, axis=0, keepdims=True)
                dec = jnp.exp(jnp.where(row128 < SB_HD, a_last[:, h:h + 1], a_last[:, h + 1:h + 2]))
                for hd, m, gm in ((h, m_a, gm_a), (h + 1, m_b, gm_b)):
                    half = slice(0, SB_HD) if hd == h else slice(SB_HD, 128)
                    qm = gm * cb
                    st_q[:, hd * 128:(hd + 1) * 128] = qm
                    dat[hd:hd + 1, :] = jnp.sum(qm, axis=0, keepdims=True)
                    hh_sum = jnp.sum(jnp.sum(hh[half, :], axis=0, keepdims=True), axis=1, keepdims=True)
                    last_row = jnp.where(lane == hd, jnp.exp(a_last[:, hd:hd + 1]) * hh_sum, last_row)
                    dsk_acc = jnp.where(lane == hd, jnp.sum(dsk_row * m, axis=1, keepdims=True), dsk_acc)
                dye = (dyp * e_a).astype(bf16)
                dc_g = dc_g + _dot(dye, hp.astype(bf16))
                db_g = db_g + _dot((x * dte).astype(bf16), d_hn.astype(bf16))
                dstate[pi] = dec * d_hn + _dot(dye, c_g, TN)
            dcb_b = dcb.astype(bf16)
            dxbc_ref[:, csl] = dc_g + _dot(dcb_b, b_g)
            dxbc_ref[:, bsl] = db_g + _dot(dcb_b, c_g, TN)

        r_i = lax.broadcasted_iota(jnp.int32, (L, L), 0)
        c_i = lax.broadcasted_iota(jnp.int32, (L, L), 1)
        rev = (r_i <= c_i).astype(bf16)

        def head_sums(st, sel, split=_split2):
            return sum(_dot(p, sel[...]) for p in split(st[...]))

        last_row = last_row + jnp.sum(head_sums(st_x, selp_ref), axis=0, keepdims=True)
        d_acs = (head_sums(st_a, selp_ref) + head_sums(st_q, selh_ref, _split3)
                 + jnp.where(rowl == L - 1, last_row, 0.0))
        ddt_x = head_sums(st_d, selp_ref)
        dda = sum(_dot(rev, p) for p in _split3(d_acs)) - sum(_dot(rev, p, NT) for p in _split3(dat[...]))
        ddt = ddt_x + dda * a
        dalog_ref[...] += jnp.sum(dda * dt, axis=0, keepdims=True) * a
        ddtr = jnp.where(lane < SSD_HEADS, ddt * _sigmoid(dtr + dtb_ref[...]), 0.0)
        ddt_ref[...] = ddtr.astype(bf16)
        ddtb_ref[...] += jnp.sum(ddtr, axis=0, keepdims=True)
        dsk_out[...] += dsk_acc
        if ride:
            pl.when(c == nc - 1)(lambda: ride.finish(r_ins, r_lnd, r_sems))

    rv = lambda c: nc - 1 - c
    vec128 = pl.BlockSpec((1, 128), lambda c: (0, 0))
    vecin = pl.BlockSpec((1, SSD_INNER), lambda c: (0, 0))
    rows = pl.BlockSpec((L, SSD_INNER), lambda c: (rv(c), 0))
    return pl.pallas_call(
        body, name="ssd_bwd", grid=(nc,),
        in_specs=[rows, rows,
                  pl.BlockSpec((L, CONV_DIM), lambda c: (rv(c), 0)),
                  pl.BlockSpec((L, 128), lambda c: (rv(c), 0)),
                  pl.BlockSpec((L, SSD_INNER), lambda c: (rv(c), P_Z // SSD_INNER)),
                  pl.BlockSpec((1, N_PAIR, 128, SSD_N), lambda c: (rv(c), 0, 0, 0)),
                  vec128, vec128, vecin, vecin,
                  pl.BlockSpec((SSD_INNER, 128), lambda c: (0, 0)),
                  pl.BlockSpec((2 * SSD_INNER, 128), lambda c: (0, 0))] + (ride.in_specs if ride else []),
        out_specs=[rows, pl.BlockSpec((L, CONV_DIM), lambda c: (rv(c), 0)),
                   pl.BlockSpec((L, 128), lambda c: (rv(c), 0)), vecin, vec128, vec128, vec128]
        + (ride.out_specs if ride else []),
        out_shape=[jax.ShapeDtypeStruct((S, SSD_INNER), bf16), jax.ShapeDtypeStruct((S, CONV_DIM), f32),
                   jax.ShapeDtypeStruct((S, 128), bf16), jax.ShapeDtypeStruct((1, SSD_INNER), f32),
                   jax.ShapeDtypeStruct((1, 128), f32), jax.ShapeDtypeStruct((1, 128), f32),
                   jax.ShapeDtypeStruct((1, 128), f32)] + (ride.out_shape if ride else []),
        scratch_shapes=[pltpu.VMEM((N_PAIR, 128, SSD_N), f32), pltpu.VMEM((L, SSD_INNER), f32),
                        pltpu.VMEM((L, SSD_INNER), f32), pltpu.VMEM((L, 2 * SSD_INNER), f32),
                        pltpu.VMEM((L, SSD_INNER), f32), pltpu.VMEM((L, SSD_INNER), f32),
                        pltpu.VMEM((128, L), f32)]
        + (ride.scratch if ride else []),
        compiler_params=_params(("arbitrary",)),
    )(dyn, y, xbc, pdt, proj, hprev, dt_bias_p, a_log_p, d_skip_c, ssd_norm, sel_pair, sel_head,
      *(ride.srcs if ride else []))


MEM_W = MEM_HEADS * MEM_HD


def _mem_probs(q, k):
    s = _dot(q, k, NT) * (MEM_HD ** -0.5)
    s = s - jnp.max(s, axis=1, keepdims=True)
    p = jnp.exp(s)
    return p / jnp.sum(p, axis=1, keepdims=True)


def _mem_fwd(proj, kv, S, tm=512):
    tm = min(tm, S)
    M = kv.shape[0]

    def body(q_ref, kv_ref, o_ref):
        for h in range(MEM_HEADS):
            sl = slice(h * MEM_HD, (h + 1) * MEM_HD)
            vsl = slice(MEM_W + h * MEM_HD, MEM_W + (h + 1) * MEM_HD)
            p = _mem_probs(q_ref[:, sl].astype(bf16), kv_ref[:, sl].astype(bf16))
            o_ref[:, sl] = _dot(p.astype(bf16), kv_ref[:, vsl].astype(bf16)).astype(bf16)

    return pl.pallas_call(
        body, name="mem_fwd", grid=(S // tm,),
        in_specs=[pl.BlockSpec((tm, MEM_W), lambda i: (i, P_MEMQ // MEM_W)),
                  pl.BlockSpec((M, 2 * MEM_W), lambda i: (0, 0))],
        out_specs=pl.BlockSpec((tm, MEM_W), lambda i: (i, 0)),
        out_shape=jax.ShapeDtypeStruct((S, MEM_W), bf16),
        compiler_params=_params(("parallel",)),
    )(proj, kv)


def _mem_bwd(proj, kv, dy, S, tm=512):
    tm = min(tm, S)
    M = kv.shape[0]
    scale = MEM_HD ** -0.5

    def body(q_ref, kv_ref, dy_ref, dq_ref, dkv_ref):
        @pl.when(pl.program_id(0) == 0)
        def _():
            dkv_ref[...] = jnp.zeros_like(dkv_ref)

        for h in range(MEM_HEADS):
            sl = slice(h * MEM_HD, (h + 1) * MEM_HD)
            vsl = slice(MEM_W + h * MEM_HD, MEM_W + (h + 1) * MEM_HD)
            q = q_ref[:, sl].astype(bf16)
            k = kv_ref[:, sl].astype(bf16)
            v = kv_ref[:, vsl].astype(bf16)
            dyh = dy_ref[:, sl].astype(bf16)
            p = _mem_probs(q, k)
            dp = _dot(dyh, v, NT)
            ds = (p * (dp - jnp.sum(dp * p, axis=1, keepdims=True)) * scale).astype(bf16)
            dq_ref[:, sl] = _dot(ds, k).astype(bf16)
            dkv_ref[:, sl] += _dot(ds, q, TN)
            dkv_ref[:, vsl] += _dot(p.astype(bf16), dyh, TN)

    return pl.pallas_call(
        body, name="mem_bwd", grid=(S // tm,),
        in_specs=[pl.BlockSpec((tm, MEM_W), lambda i: (i, P_MEMQ // MEM_W)),
                  pl.BlockSpec((M, 2 * MEM_W), lambda i: (0, 0)),
                  pl.BlockSpec((tm, MEM_W), lambda i: (i, 0))],
        out_specs=[pl.BlockSpec((tm, MEM_W), lambda i: (i, 0)), pl.BlockSpec((M, 2 * MEM_W), lambda i: (0, 0))],
        out_shape=[jax.ShapeDtypeStruct((S, MEM_W), bf16), jax.ShapeDtypeStruct((M, 2 * MEM_W), f32)],
        compiler_params=_params(("arbitrary",)),
    )(proj, kv, dy)


def _merge_fwd(proj, t0, t1, t2, S, tm=512):
    tm = min(tm, S)

    def body(g_ref, t0_ref, t1_ref, t2_ref, o_ref):
        acc = jnp.zeros((tm, D), f32)
        for b, t_ref in enumerate((t0_ref, t1_ref, t2_ref)):
            acc = acc + _sigmoid(g_ref[:, b * D:(b + 1) * D]) * t_ref[...]
        o_ref[...] = acc.astype(bf16)

    row = pl.BlockSpec((tm, D), lambda i: (i, 0))
    return pl.pallas_call(
        body, name="merge_fwd", grid=(S // tm,),
        in_specs=[pl.BlockSpec((tm, 3 * D), lambda i: (i, P_GATE // (3 * D))), row, row, row],
        out_specs=row, out_shape=jax.ShapeDtypeStruct((S, D), bf16),
        compiler_params=_params(("parallel",)),
    )(proj, t0, t1, t2)


def _merge_bwd(proj, t0, t1, t2, dm, S, tm=512):
    tm = min(tm, S)

    def body(g_ref, t0_ref, t1_ref, t2_ref, dm_ref, d0_ref, d1_ref, d2_ref, dg_ref):
        dmv = dm_ref[...]
        for b, (t_ref, d_ref) in enumerate(((t0_ref, d0_ref), (t1_ref, d1_ref), (t2_ref, d2_ref))):
            sg = _sigmoid(g_ref[:, b * D:(b + 1) * D])
            d_ref[...] = (dmv * sg).astype(bf16)
            dg_ref[:, b * D:(b + 1) * D] = (dmv * t_ref[...] * sg * (1.0 - sg)).astype(bf16)

    row = pl.BlockSpec((tm, D), lambda i: (i, 0))
    return pl.pallas_call(
        body, name="merge_bwd", grid=(S // tm,),
        in_specs=[pl.BlockSpec((tm, 3 * D), lambda i: (i, P_GATE // (3 * D))), row, row, row, row],
        out_specs=[row, row, row, pl.BlockSpec((tm, 3 * D), lambda i: (i, 0))],
        out_shape=[jax.ShapeDtypeStruct((S, D), bf16)] * 3 + [jax.ShapeDtypeStruct((S, 3 * D), bf16)],
        compiler_params=_params(("parallel",)),
    )(proj, t0, t1, t2, dm)


def _loss_head(ff, g, h1, target, S, tm=512):
    tm = min(tm, S)

    def body(ff_ref, g_ref, h1_ref, t_ref, dh_ref, loss_ref):
        xv = ff_ref[...]
        r = lax.rsqrt(jnp.mean(xv * xv, axis=1, keepdims=True) + EPS)
        err = h1_ref[...] + xv * r * g_ref[...] - t_ref[...]
        dh_ref[...] = err * (1.0 / D)

        @pl.when(pl.program_id(0) == 0)
        def _():
            loss_ref[...] = jnp.zeros_like(loss_ref)

        loss_ref[...] += 0.5 * _sum_all(jnp.mean(err * err, axis=1, keepdims=True)) * jnp.ones((1, 128), f32)

    row = pl.BlockSpec((tm, D), lambda i: (i, 0))
    return pl.pallas_call(
        body, name="loss_head", grid=(S // tm,),
        in_specs=[row, pl.BlockSpec((1, D), lambda i: (0, 0)), row, row],
        out_specs=[row, pl.BlockSpec((1, 128), lambda i: (0, 0))],
        out_shape=[jax.ShapeDtypeStruct((S, D), f32), jax.ShapeDtypeStruct((1, 128), f32)],
        compiler_params=_params(("arbitrary",)),
    )(ff, g, h1, target)


def _local_step(x, mem, target, wts, late_rides, late_weights, small, rest_rides, w_in_ride):
    S = x.shape[0]
    M = mem.shape[0]
    pad = lambda v: jnp.pad(v, ((0, 0), (0, 128 - SSD_HEADS)))
    dtb_p, alog_p = pad(small["dt_bias"]), pad(small["a_log"])
    dsk_c = jnp.repeat(small["d_skip"], SB_HD, axis=1)

    u = _rms_fwd(x, small["norm_mix_pre"], name="norm_pre", out_dtype=bf16)
    rides = late_rides or (None, None, None)
    if late_rides:
        proj, lands_a = _mm(u, wts["w_main"], "nn", tm=1024, tn=1024, name="in_proj", ride=rides[0])
    else:
        proj, lands_a = _mm(u, wts["w_main"], "nn", tm=1024, tn=1024, name="in_proj"), []
    pdt = _mm(u, wts["w_dt"], "nn", tm=1024, tn=128, name="in_proj_dt")
    y_sb, tot_lk, lands_b = _sb_fwd(proj, S, rides[1])
    wts = dict(wts, **late_weights(0, lands_a))
    small = dict(small, conv_w=wts.pop("conv_w"))
    xc, xbc = _conv_fwd(proj, small["conv_w"], small["conv_b"], S)
    y_ssd, yn, hprev, lands_c = _ssd_fwd(xbc, proj, pdt, dtb_p, alog_p, dsk_c, small["ssd_norm"], S, rides[2])
    wts = dict(wts, **late_weights(1, lands_b), **late_weights(2, lands_c))
    mn = _rms_fwd(mem, small["norm_mem"], name="norm_mem", out_dtype=bf16, tm=min(512, M))
    kv = _mm(mn, wts["w_mem_kv"], "nn", tm=M, tn=1024, name="mem_kv")
    y_mem = _mem_fwd(proj, kv, S)
    t0 = _mm(y_sb, wts["w_sb_out"], "nn", tm=1024, tn=1024, name="sb_out")
    t1 = _mm(yn, wts["w_ssd_out"], "nn", tm=1024, tn=1024, name="ssd_out")
    t2 = _mm(y_mem, wts["w_mem_out"], "nn", tm=1024, tn=1024, name="mem_out")
    merged = _merge_fwd(proj, t0, t1, t2, S)
    mix = _mm(merged, wts["w_o"], "nn", tm=1024, tn=1024, name="w_o")
    h1 = _rms_fwd(mix, small["norm_mix_post"], name="norm_mix_post", out_dtype=f32, residual=x)
    u2 = _rms_fwd(h1, small["norm_mlp_pre"], name="norm_mlp_pre", out_dtype=bf16)
    a_up, hrelu = _mm(u2, wts["w_up"], "nn", tm=1024, tn=1024, name="mlp_up", out_dtypes=(f32, bf16),
                      epi=lambda acc: (acc, jnp.square(jnp.maximum(acc, 0.0))))
    ff = _mm(hrelu, wts["w_down"], "nn", tm=1024, tn=1024, name="mlp_down")
    dh2, loss = _loss_head(ff, small["norm_mlp_post"], h1, target, S)

    g = {}
    dff, g["norm_mlp_post"] = _rms_bwd(ff, dh2, small["norm_mlp_post"], name="norm_mlp_post_bwd", out_dtype=bf16)
    da = _mm(dff, wts["w_down"], "nt", tm=1024, tn=1024, name="mlp_down_dx", out_dtypes=(bf16,),
             epi=lambda acc, a: (acc * (2.0 * jnp.maximum(a, 0.0)),), extras=(a_up,))
    g["w_down"] = _mm(hrelu, dff, "tn", tm=1024, tn=1024, name="mlp_down_dw")
    du2 = _mm(da, wts["w_up"], "nt", tm=1024, tn=1024, name="mlp_up_dx")
    g["w_up"] = _mm(u2, da, "tn", tm=1024, tn=1024, name="mlp_up_dw")
    dh1, g["norm_mlp_pre"] = _rms_bwd(h1, du2, small["norm_mlp_pre"], name="norm_mlp_pre_bwd", out_dtype=f32, add=dh2)
    dmix, g["norm_mix_post"] = _rms_bwd(mix, dh1, small["norm_mix_post"], name="norm_mix_post_bwd", out_dtype=bf16)
    dmerged = _mm(dmix, wts["w_o"], "nt", tm=1024, tn=1024, name="w_o_dx")
    g["w_o"] = _mm(merged, dmix, "tn", tm=1024, tn=1024, name="w_o_dw")
    dt0, dt1, dt2, dgl = _merge_bwd(proj, t0, t1, t2, dmerged, S)
    dy_sb = _mm(dt0, wts["w_sb_out"], "nt", tm=1024, tn=1024, name="sb_out_dx")
    g["w_sb_out"] = _mm(y_sb, dt0, "tn", tm=1024, tn=1024, name="sb_out_dw")
    dy_ssd = _mm(dt1, wts["w_ssd_out"], "nt", tm=1024, tn=1024, name="ssd_out_dx")
    g["w_ssd_out"] = _mm(yn, dt1, "tn", tm=1024, tn=1024, name="ssd_out_dw")
    dy_mem = _mm(dt2, wts["w_mem_out"], "nt", tm=1024, tn=1024, name="mem_out_dx")
    g["w_mem_out"] = _mm(y_mem, dt2, "tn", tm=1024, tn=1024, name="mem_out_dw")
    dmemq, dkv = _mem_bwd(proj, kv, dy_mem, S)
    g["w_mem_kv"] = _mm(mn, dkv, "tn", tm=1024, tn=1024, name="mem_kv_dw")
    dmn = _mm(dkv, wts["w_mem_kv"], "nt", tm=M, tn=1024, name="mem_kv_dx")
    _, g["norm_mem"] = _rms_bwd(mem, dmn, small["norm_mem"], name="norm_mem_bwd", out_dtype=bf16, tm=min(512, M))
    rides = rest_rides(g) if rest_rides else (None, None)
    dz, dxbc, ddt, g["ssd_norm"], dsk, dalog, ddtb, *lands_a = _ssd_bwd(
        dy_ssd, y_ssd, xbc, proj, pdt, hprev, dtb_p, alog_p, dsk_c, small["ssd_norm"], S, rides[0])
    g["d_skip"], g["a_log"], g["dt_bias"] = dsk[:, :SSD_HEADS], dalog[:, :SSD_HEADS], ddtb[:, :SSD_HEADS]
    dxbc_raw, dcw, g["conv_b"] = _conv_bwd(proj, xc, dxbc, small["conv_w"], S)
    g["conv_w"] = dcw[:CONV_K]
    dq, dk, dv, lands_b = _sb_bwd(proj, tot_lk, dy_sb, S, rides[1])
    g["rest_lands"] = lands_b + lands_a
    dproj = (dq, dk, dv, dxbc_raw, dgl, dmemq, dz)
    u_t = u.T
    g["w_main"] = [_mm(u_t, p, "nn", tm=512, tn=1024, name="in_proj_dw_%d" % i) for i, p in enumerate(dproj)]
    g["w_dt"] = _mm(u_t, ddt, "nn", tm=512, tn=128, name="in_proj_dt_dw")
    du_dt = _mm(ddt, wts["w_dt"], "nt", tm=1024, tn=1024, name="in_proj_dt_dx")
    du, g["w_in_lands"] = _mm_pieces_nt(dproj, wts["w_main"], du_dt, tm=512, tn=256, name="in_proj_dx",
                                        ride=w_in_ride(g) if w_in_ride else None)
    grad_x, g["norm_mix_pre"] = _rms_bwd(x, du, small["norm_mix_pre"], name="norm_pre_bwd", out_dtype=f32, add=dh1)
    return loss, grad_x, g


def _to_internal(w_in):
    sec = lambda r: w_in[:, r[0]:r[1]]
    w_main = jnp.concatenate([sec(R_QKV), sec(R_XBC), sec(R_GATE), sec(R_MEMQ), sec(R_Z)], axis=1)
    w_dt = jnp.pad(sec(R_DT), ((0, 0), (0, 128 - SSD_HEADS)))
    return w_main, w_dt


def _from_internal(pieces, g_dt):
    dq, dk, dv, dxbc, dgate, dmemq, dz = pieces
    return [dq, dk, dv, dz, dxbc, g_dt[:, :SSD_HEADS], dmemq, dgate]


def _w_in_slab(ordered, s, dtype):
    width = D_IN // N_SHARD
    lo, hi, off, parts = s * width, (s + 1) * width, 0, []
    for p in ordered:
        a, b = max(lo, off), min(hi, off + p.shape[1])
        if a < b:
            parts.append(p[:, a - off:b - off].astype(dtype))
        off += p.shape[1]
    return jnp.concatenate(parts, axis=1)


MESH = pl.DeviceIdType.MESH
ANY = pl.BlockSpec(memory_space=pl.ANY)


def _place():
    x, y, c = lax.axis_index("x"), lax.axis_index("y"), lax.axis_index("c")
    return (x, y, c), [(1 - x, y, c), (x, 1 - y, c), (1 - x, 1 - y, c)]


def _exchange_copy(mode, ins, lands, send, recv, a, k, me, peers, arriving):
    p = peers[k]
    theirs = 2 * p[0] + p[1]
    if mode == "gather":
        src, dst = ins[a], lands[a].at[theirs if arriving else me]
    else:
        src, dst = ins[a].at[theirs], lands[a].at[k]
    return pltpu.make_async_remote_copy(src_ref=src, dst_ref=dst, send_sem=send.at[a * 3 + k],
                                        recv_sem=recv.at[a * 3 + k], device_id=p, device_id_type=MESH)


class _Ride:
    def __init__(self, srcs, mode):
        self.srcs, self.mode, self.n = list(srcs), mode, len(srcs)
        n = self.n
        self.in_specs, self.out_specs = [ANY] * n, [ANY] * n
        self.out_shape = [
            jax.ShapeDtypeStruct((N_SHARD,) + s.shape if mode == "gather" else (3,) + s.shape[1:], s.dtype)
            for s in self.srcs]
        self.scratch = [pltpu.SemaphoreType.DMA((3 * n,)), pltpu.SemaphoreType.DMA((3 * n,)),
                        pltpu.SemaphoreType.DMA((n,))]

    def _own(self, ins, lnd, sems):
        if self.mode != "gather":
            return []
        me = 2 * lax.axis_index("x") + lax.axis_index("y")
        return [pltpu.make_async_copy(ins[a], lnd[a].at[me], sems[2].at[a]) for a in range(self.n)]

    def _far(self, ins, lnd, sems, arriving):
        (x, y, c), peers = _place()
        return [_exchange_copy(self.mode, ins, lnd, sems[0], sems[1], a, k, 2 * x + y, peers, arriving)
                for a in range(self.n) for k in range(3)]

    def start(self, ins, lnd, sems):
        for cp in self._own(ins, lnd, sems) + self._far(ins, lnd, sems, False):
            cp.start()

    def finish(self, ins, lnd, sems):
        for cp in self._far(ins, lnd, sems, True):
            cp.wait_recv()
        for cp in self._far(ins, lnd, sems, False):
            cp.wait_send()
        for cp in self._own(ins, lnd, sems):
            cp.wait()


def _gather_two_level(shards, name):
    n = len(shards)

    def body(*refs):
        ins, lnd = refs[:n], refs[n:2 * n]
        send, recv, loc = refs[2 * n:]
        (x, y, c), peers = _place()
        me = 2 * x + y

        def half(ref, a, core):
            rows = shards[a].shape[0] // 2
            return ref.at[pl.ds(core * rows, rows)]

        def copy(a, j, slot, core, to):
            return pltpu.make_async_remote_copy(
                src_ref=half(ins[a], a, core) if j < 3 else half(lnd[a].at[slot], a, core),
                dst_ref=half(lnd[a].at[slot], a, core), send_sem=send.at[6 * a + j], recv_sem=recv.at[6 * a + j],
                device_id=to, device_id_type=MESH)

        own = [pltpu.make_async_copy(ins[a], lnd[a].at[me], loc.at[a]) for a in range(n)]
        far = [copy(a, k, me, c, peers[k]) for a in range(n) for k in range(3)]
        for cp in own + far:
            cp.start()
        passed = []
        for a in range(n):
            for k, p in enumerate(peers):
                theirs = 2 * p[0] + p[1]
                copy(a, k, theirs, c, p).wait_recv()
                passed.append(copy(a, 3 + k, theirs, c, (x, y, 1 - c)))
                passed[-1].start()
        for a in range(n):
            for k, p in enumerate(peers):
                copy(a, 3 + k, 2 * p[0] + p[1], 1 - c, (x, y, 1 - c)).wait_recv()
        for cp in far + passed:
            cp.wait_send()
        for cp in own:
            cp.wait()

    return pl.pallas_call(
        body, name=name, in_specs=[ANY] * n, out_specs=[ANY] * n,
        out_shape=[jax.ShapeDtypeStruct((N_SHARD,) + s.shape, s.dtype) for s in shards],
        scratch_shapes=[pltpu.SemaphoreType.DMA((6 * n,)), pltpu.SemaphoreType.DMA((6 * n,)),
                        pltpu.SemaphoreType.DMA((n,))],
    )(*shards)


def _exchange_packets(packet):
    def body(pk, pk_out, send, recv, loc):
        x, y, c = lax.axis_index("x"), lax.axis_index("y"), lax.axis_index("c")
        lin = 4 * x + 2 * y + c
        own = pltpu.make_async_copy(pk, pk_out.at[lin], loc.at[0])
        own.start()

        def pk_copy(m, slot):
            dev = (x ^ ((m >> 2) & 1), y ^ ((m >> 1) & 1), c ^ (m & 1))
            return pltpu.make_async_remote_copy(
                src_ref=pk, dst_ref=pk_out.at[slot], send_sem=send.at[m - 1], recv_sem=recv.at[m - 1],
                device_id=dev, device_id_type=MESH)

        sent = [pk_copy(m, lin) for m in range(1, N_DEV)]
        for cp in sent:
            cp.start()
        for m in range(1, N_DEV):
            pk_copy(m, lin ^ m).wait_recv()
        for cp in sent:
            cp.wait_send()
        own.wait()

    return pl.pallas_call(
        body, name="exchange_packets", in_specs=[ANY], out_specs=ANY,
        out_shape=jax.ShapeDtypeStruct((N_DEV,) + packet.shape, packet.dtype),
        scratch_shapes=[pltpu.SemaphoreType.DMA((N_DEV - 1,)), pltpu.SemaphoreType.DMA((N_DEV - 1,)),
                        pltpu.SemaphoreType.DMA((1,))],
    )(packet)


def _swap_sibling(parts, name):
    n = len(parts)

    def body(*refs):
        ins, outs = refs[:n], refs[n:2 * n]
        send, recv = refs[2 * n:]
        x, y, c = lax.axis_index("x"), lax.axis_index("y"), lax.axis_index("c")
        cps = [pltpu.make_async_remote_copy(
            src_ref=ins[a], dst_ref=outs[a], send_sem=send.at[a], recv_sem=recv.at[a],
            device_id=(x, y, 1 - c), device_id_type=MESH) for a in range(n)]
        for cp in cps:
            cp.start()
        for cp in cps:
            cp.wait_recv()
        for cp in cps:
            cp.wait_send()

    return pl.pallas_call(
        body, name=name,
        in_specs=[ANY] * n, out_specs=[ANY] * n,
        out_shape=[jax.ShapeDtypeStruct(p.shape, p.dtype) for p in parts],
        scratch_shapes=[pltpu.SemaphoreType.DMA((n,)), pltpu.SemaphoreType.DMA((n,))],
    )(*parts)


BLOCK_ELEMS = 256 * 1024


def _row_tile(R, C):
    tr = max(8, (BLOCK_ELEMS // C) // 8 * 8)
    while R % tr:
        tr -= 8
    return min(tr, R)


def _sum_parts(own, stack, name, out_dtype=f32):
    k = stack.shape[0]
    R, C = stack.shape[1:]
    tr = _row_tile(R, C)

    def body(*refs):
        o_ref = refs[-1]
        acc = refs[0][...].astype(f32)
        for r in refs[1:-1]:
            acc = acc + r[...].astype(f32)
        o_ref[...] = acc.astype(out_dtype)

    row = pl.BlockSpec((tr, C), lambda i: (i, 0))
    specs = ([row] if own is not None else []) + [
        pl.BlockSpec((None, tr, C), functools.partial(lambda i, j: (j, i, 0), j=j)) for j in range(k)]
    args = ([own] if own is not None else []) + [stack] * k
    return pl.pallas_call(
        body, name=name, grid=(R // tr,), in_specs=specs, out_specs=row,
        out_shape=jax.ShapeDtypeStruct((R, C), out_dtype), compiler_params=_params(("parallel",)),
    )(*args)


def _adamw(w, m, v, g_parts, name):
    R, C = w.shape
    tr = _row_tile(R, C)
    n_g = len(g_parts)

    def body(w_ref, m_ref, v_ref, *rest):
        g = rest[0][...]
        for r in rest[1:n_g]:
            g = g + r[...]
        g_ref, d_ref, nm_ref, nv_ref = rest[n_g:]
        nm = ADAM_B1 * m_ref[...] + (1.0 - ADAM_B1) * g
        nv = ADAM_B2 * v_ref[...] + (1.0 - ADAM_B2) * jnp.square(g)
        m_hat = nm / (1.0 - ADAM_B1 ** ADAM_STEP)
        v_hat = nv / (1.0 - ADAM_B2 ** ADAM_STEP)
        g_ref[...] = g
        d_ref[...] = -ADAM_LR * (m_hat / (jnp.sqrt(v_hat) + ADAM_EPS) + ADAM_WD * w_ref[...])
        nm_ref[...] = nm
        nv_ref[...] = nv

    row = pl.BlockSpec((tr, C), lambda i: (i, 0))
    return pl.pallas_call(
        body, name=name, grid=(R // tr,), in_specs=[row] * (3 + n_g), out_specs=[row] * 4,
        out_shape=[jax.ShapeDtypeStruct((R, C), f32)] * 4, compiler_params=_params(("parallel",)),
    )(w, m, v, *g_parts)


BIG = ("w_in", "w_mem_kv", "w_sb_out", "w_ssd_out", "w_mem_out", "w_o", "w_up", "w_down")
LATE = ("w_sb_out", "w_ssd_out", "w_mem_out", "w_o", "w_up", "w_down")
REST = BIG[1:]
COL_SHARDED = ("w_in", "w_mem_kv", "w_up")
SMALL = ("norm_mix_pre", "conv_w", "conv_b", "dt_bias", "a_log", "d_skip", "ssd_norm", "norm_mem",
         "norm_mix_post", "norm_mlp_pre", "norm_mlp_post")
WEIGHTS = ("norm_mix_pre", "w_in", "conv_w", "conv_b", "dt_bias", "a_log", "d_skip", "ssd_norm", "norm_mem",
           "w_mem_kv", "w_sb_out", "w_ssd_out", "w_mem_out", "w_o", "norm_mix_post", "norm_mlp_pre", "w_up",
           "w_down", "norm_mlp_post")
PK_ROWS = 184


def _pack(vecs):
    flat = jnp.concatenate([v.reshape(-1) for v in vecs])
    return jnp.pad(flat, (0, PK_ROWS * 128 - flat.shape[0])).reshape(PK_ROWS, 128)


def _unpack(pk, shapes):
    flat = pk.reshape(-1)
    out, off = [], 0
    for s in shapes:
        n = 1
        for d in s:
            n *= d
        out.append(flat[off:off + n].reshape(s))
        off += n
    return out


def _full_from_slabs(name, slabs):
    if name in COL_SHARDED:
        return slabs.transpose(1, 0, 2).reshape(slabs.shape[1], -1)
    return slabs.reshape(-1, slabs.shape[2])


def _slabs_from_full(name, g):
    if name in COL_SHARDED:
        return g.reshape(g.shape[0], N_SHARD, -1).transpose(1, 0, 2)
    return g.reshape(N_SHARD, -1, g.shape[1])


def kernel(x, mem, norm_mix_pre, w_in, conv_w, conv_b, dt_bias, a_log, d_skip, ssd_norm, norm_mem, w_mem_kv, w_sb_out, w_ssd_out, w_mem_out, w_o, norm_mix_post, norm_mlp_pre, w_up, w_down, norm_mlp_post, loss_target, m_norm_mix_pre, m_w_in, m_conv_w, m_conv_b, m_dt_bias, m_a_log, m_d_skip, m_ssd_norm, m_norm_mem, m_w_mem_kv, m_w_sb_out, m_w_ssd_out, m_w_mem_out, m_w_o, m_norm_mix_post, m_norm_mlp_pre, m_w_up, m_w_down, m_norm_mlp_post, v_norm_mix_pre, v_w_in, v_conv_w, v_conv_b, v_dt_bias, v_a_log, v_d_skip, v_ssd_norm, v_norm_mem, v_w_mem_kv, v_w_sb_out, v_w_ssd_out, v_w_mem_out, v_w_o, v_norm_mix_post, v_norm_mlp_pre, v_w_up, v_w_down, v_norm_mlp_post):
    env = dict(locals())
    w = {n: env[n] for n in WEIGHTS}
    mo = {n: env["m_" + n] for n in WEIGHTS}
    vo = {n: env["v_" + n] for n in WEIGHTS}
    shard = 2 * lax.axis_index("x") + lax.axis_index("y")

    first = _gather_two_level([w["w_in"][0].astype(bf16)], "gather_first")
    w_main, w_dt = _to_internal(_full_from_slabs("w_in", first[0]))
    wts = dict(w_main=w_main, w_dt=w_dt)
    ride_names = (("w_mem_kv",) + LATE[:4], LATE[4:5], LATE[5:])
    late_rides = tuple(_Ride([w[n][0].astype(bf16) for n in names] + ([w["conv_w"][0]] if i == 0 else []), "gather")
                       for i, names in enumerate(ride_names))

    def late_weights(i, lands):
        full = {n: _full_from_slabs(n, s) for n, s in zip(ride_names[i], lands)}
        if i == 0:
            full["conv_w"] = lands[-1].transpose(1, 0, 2).reshape(CONV_K, CONV_DIM)
        return full

    def rest_rides(g):
        slabs = [_slabs_from_full(n, g[n]).astype(bf16) for n in REST]
        return _Ride(slabs[5:], "scatter"), _Ride(slabs[:5], "scatter")

    core = lax.axis_index("c")
    half = D // 2

    def w_in_ride(g):
        ordered = _from_internal(g["w_main"], g["w_dt"])
        stack = jnp.stack([_w_in_slab(ordered, s, bf16) for s in range(N_SHARD)])
        keep = lax.dynamic_slice_in_dim(stack, core * half, half, axis=1)
        away = lax.dynamic_slice_in_dim(stack, (1 - core) * half, half, axis=1)
        (got,) = _swap_sibling([away], "w_in_halves_out")
        wide = lambda a: a.reshape(N_SHARD * half, -1)
        chip = _sum_parts(wide(keep), wide(got)[None], "sum_cores_w_in", bf16).reshape(N_SHARD, half, -1)
        own = lax.switch(shard, [functools.partial(_w_in_slab, ordered, s, f32) for s in range(N_SHARD)])
        own = lax.dynamic_slice_in_dim(own, core * half, half, axis=0)
        g["w_in_own"] = _sum_parts(own, lax.dynamic_index_in_dim(got, shard, 0, keepdims=True), "sum_cores_w_in_own")
        return _Ride([chip], "scatter")

    small = {n: w[n] for n in SMALL if n != "conv_w"}
    loss, grad_x, g = _local_step(x[0], mem[0], loss_target[0], wts, late_rides, late_weights, small,
                                  rest_rides, w_in_ride)
    out_g, out_d, out_m, out_v = {}, {}, {}, {}

    def apply(n, g_parts):
        res = _adamw(w[n][0], mo[n][0], vo[n][0], g_parts, name="adamw_" + n)
        out_g[n], out_d[n], out_m[n], out_v[n] = [r[None] for r in res]

    mine = _sum_parts(g["w_in_own"], g["w_in_lands"][0], name="sum_chips_w_in")
    (theirs,) = _swap_sibling([mine], "w_in_halves_back")
    g_w_in = lax.dynamic_update_slice_in_dim(jnp.zeros((D, D_IN // N_SHARD), f32), mine, core * half, axis=0)
    apply("w_in", [lax.dynamic_update_slice_in_dim(g_w_in, theirs, (1 - core) * half, axis=0)])

    packets = _exchange_packets(_pack([g[n] for n in SMALL] + [loss[:, :1]]))
    partial = []
    for n, r in zip(REST, g["rest_lands"]):
        own = lax.dynamic_index_in_dim(_slabs_from_full(n, g[n]), shard, 0, keepdims=False)
        partial.append(_sum_parts(own, r, name="sum_chips_" + n))
    other = _swap_sibling(partial, "swap_sibling")

    for n, p, q in zip(REST, partial, other):
        apply(n, [p, q])
    tot = _sum_parts(None, packets, name="sum_packets")
    shapes = [g[n].shape for n in SMALL] + [(1, 1)]
    sm = dict(zip(SMALL + ("loss",), _unpack(tot, shapes)))
    sm["conv_w"] = lax.dynamic_slice_in_dim(sm["conv_w"], shard * (CONV_DIM // N_SHARD), CONV_DIM // N_SHARD, axis=1)
    own_small = lambda d: _pack([d[n].reshape(sm[n].shape) for n in SMALL])
    res = _adamw(own_small(w), own_small(mo), own_small(vo), [own_small(sm)], name="adamw_small")
    own_shapes = [sm[n].shape for n in SMALL]
    for store, r in zip((out_g, out_d, out_m, out_v), res):
        for n, val in zip(SMALL, _unpack(r, own_shapes)):
            store[n] = val.reshape(w[n].shape)

    outs = [sm["loss"].reshape(()), grad_x[None]]
    for store in (out_g, out_d, out_m, out_v):
        outs += [store[n] for n in WEIGHTS]
    return tuple(outs)
```

```python
import functools

import jax
import jax.numpy as jnp
from jax import lax
from jax.experimental import pallas as pl
from jax.experimental.pallas import tpu as pltpu

f32 = jnp.float32
bf16 = jnp.bfloat16

D = 1024
EPS = 1e-6
SB_HD = 64
SSD_INNER = 2048
SSD_HEADS = 32
SSD_GROUPS = 4
SSD_N = 128
SSD_L = 128
CONV_K = 4
CONV_DIM = 3072
MEM_HEADS = 4
MEM_HD = 256
D_FF = 4096
D_IN = 12320
N_SHARD = 4
N_DEV = 8

P_QKV, P_XBC, P_GATE, P_MEMQ, P_Z, P_DT, P_TOT = 0, 3072, 6144, 9216, 10240, 12288, 12416
R_QKV, R_Z, R_XBC, R_DT, R_MEMQ, R_GATE = (0, 3072), (3072, 5120), (5120, 8192), (8192, 8224), (8224, 9248), (9248, 12320)

ADAM_LR = 0.001
ADAM_B1 = 0.9
ADAM_B2 = 0.999
ADAM_EPS = 1e-08
ADAM_WD = 0.01
ADAM_STEP = 10

VMEM_LIMIT = 56 * 1024 * 1024

NN = (((1,), (0,)), ((), ()))
NT = (((1,), (1,)), ((), ()))
TN = (((0,), (0,)), ((), ()))


def _dot(a, b, dims=NN):
    return lax.dot_general(a, b, dims, preferred_element_type=f32)


def _params(sem=None):
    return pltpu.CompilerParams(dimension_semantics=sem, vmem_limit_bytes=VMEM_LIMIT)


def _sigmoid(x):
    return 1.0 / (1.0 + jnp.exp(-x))


def _split2(x):
    hi = x.astype(bf16)
    lo = (x - hi.astype(f32)).astype(bf16)
    return hi, lo


def _split3(x):
    hi = x.astype(bf16)
    r = x - hi.astype(f32)
    mid = r.astype(bf16)
    lo = (r - mid.astype(f32)).astype(bf16)
    return hi, mid, lo


def _mm(a, b, mode, *, tm, tn, name, out_dtypes=(f32,), epi=None, extras=(), ride=None):
    M = a.shape[1] if mode == "tn" else a.shape[0]
    N = b.shape[0] if mode == "nt" else b.shape[1]
    tm, tn = min(tm, M), min(tn, N)
    if mode == "nn":
        (M, K), N = a.shape, b.shape[1]
        a_spec = pl.BlockSpec((tm, K), lambda i, j: (i, 0))
        b_spec = pl.BlockSpec((K, tn), lambda i, j: (0, j))
        dims = NN
    elif mode == "nt":
        (M, K), N = a.shape, b.shape[0]
        a_spec = pl.BlockSpec((tm, K), lambda i, j: (i, 0))
        b_spec = pl.BlockSpec((tn, K), lambda i, j: (j, 0))
        dims = NT
    else:
        (K, M), N = a.shape, b.shape[1]
        a_spec = pl.BlockSpec((K, tm), lambda i, j: (0, i))
        b_spec = pl.BlockSpec((K, tn), lambda i, j: (0, j))
        dims = TN
    assert M % tm == 0 and N % tn == 0, (name, M, N, tm, tn)
    n_ex, n_out = len(extras), len(out_dtypes)
    n_r = ride.n if ride else 0
    o_spec = pl.BlockSpec((tm, tn), lambda i, j: (i, j))
    grid = (M // tm, N // tn)

    def body(a_ref, b_ref, *rest):
        r_ins = rest[n_ex:n_ex + n_r]
        outs = rest[n_ex + n_r:n_ex + n_r + n_out]
        r_lnd, r_sems = rest[n_ex + n_r + n_out:n_ex + 2 * n_r + n_out], rest[n_ex + 2 * n_r + n_out:]
        i, j = pl.program_id(0), pl.program_id(1)
        if ride:
            pl.when((i == 0) & (j == 0))(lambda: ride.start(r_ins, r_lnd, r_sems))
        acc = _dot(a_ref[...].astype(bf16), b_ref[...].astype(bf16), dims)
        res = (acc,) if epi is None else epi(acc, *[e[...] for e in rest[:n_ex]])
        for o_ref, r in zip(outs, res):
            o_ref[...] = r.astype(o_ref.dtype)
        if ride:
            pl.when((i == grid[0] - 1) & (j == grid[1] - 1))(lambda: ride.finish(r_ins, r_lnd, r_sems))

    out = pl.pallas_call(
        body, name=name, grid=grid,
        in_specs=[a_spec, b_spec] + [o_spec] * n_ex + (ride.in_specs if ride else []),
        out_specs=[o_spec] * n_out + (ride.out_specs if ride else []),
        out_shape=[jax.ShapeDtypeStruct((M, N), dt) for dt in out_dtypes] + (ride.out_shape if ride else []),
        scratch_shapes=ride.scratch if ride else [],
        compiler_params=_params(("arbitrary", "arbitrary") if ride else ("parallel", "parallel")),
    )(a, b, *extras, *(ride.srcs if ride else []))
    if ride:
        return (out[0] if n_out == 1 else out[:n_out]), list(out[n_out:])
    return out[0] if n_out == 1 else out


def _mm_pieces_nt(pieces, b, add, *, tm, tn, name, ride):
    M, N = pieces[0].shape[0], b.shape[0]
    n_p, n_r = len(pieces), (ride.n if ride else 0)
    o_spec = pl.BlockSpec((tm, tn), lambda i, j: (i, j))
    grid = (M // tm, N // tn)

    def body(*refs):
        b_ref, add_ref = refs[n_p:n_p + 2]
        r_ins, o_ref = refs[n_p + 2:n_p + 2 + n_r], refs[n_p + 2 + n_r]
        r_lnd, r_sems = refs[n_p + 3 + n_r:n_p + 3 + 2 * n_r], refs[n_p + 3 + 2 * n_r:]
        i, j = pl.program_id(0), pl.program_id(1)
        if ride:
            pl.when((i == 0) & (j == 0))(lambda: ride.start(r_ins, r_lnd, r_sems))
        acc, off = add_ref[...], 0
        for r in refs[:n_p]:
            acc = acc + _dot(r[...], b_ref[:, off:off + r.shape[1]], NT)
            off += r.shape[1]
        o_ref[...] = acc
        if ride:
            pl.when((i == grid[0] - 1) & (j == grid[1] - 1))(lambda: ride.finish(r_ins, r_lnd, r_sems))

    out = pl.pallas_call(
        body, name=name, grid=grid,
        in_specs=[pl.BlockSpec((tm, p.shape[1]), lambda i, j: (i, 0)) for p in pieces]
        + [pl.BlockSpec((tn, b.shape[1]), lambda i, j: (j, 0)), o_spec] + (ride.in_specs if ride else []),
        out_specs=[o_spec] + (ride.out_specs if ride else []),
        out_shape=[jax.ShapeDtypeStruct((M, N), f32)] + (ride.out_shape if ride else []),
        scratch_shapes=ride.scratch if ride else [],
        compiler_params=_params(("arbitrary", "arbitrary")),
    )(*pieces, b, add, *(ride.srcs if ride else []))
    return out[0], list(out[1:])


def _rms_fwd(x, g, *, name, out_dtype, residual=None, tm=512):
    S, C = x.shape
    tm = min(tm, S)
    has_res = residual is not None

    def body(x_ref, g_ref, *rest):
        xv = x_ref[...]
        r = lax.rsqrt(jnp.mean(xv * xv, axis=1, keepdims=True) + EPS)
        y = xv * r * g_ref[...]
        if has_res:
            y = y + rest[0][...]
        rest[-1][...] = y.astype(out_dtype)

    row = pl.BlockSpec((tm, C), lambda i: (i, 0))
    vec = pl.BlockSpec((1, C), lambda i: (0, 0))
    args = (x, g) + ((residual,) if has_res else ())
    return pl.pallas_call(
        body, name=name, grid=(S // tm,),
        in_specs=[row, vec] + ([row] if has_res else []),
        out_specs=row, out_shape=jax.ShapeDtypeStruct((S, C), out_dtype),
        compiler_params=_params(("parallel",)),
    )(*args)


def _rms_bwd(x, dy, g, *, name, out_dtype, add=None, tm=512):
    S, C = x.shape
    tm = min(tm, S)
    has_add = add is not None

    def body(x_ref, dy_ref, g_ref, *rest):
        dx_ref, dg_ref = rest[-2], rest[-1]
        xv = x_ref[...]
        dyv = dy_ref[...].astype(f32)
        r = lax.rsqrt(jnp.mean(xv * xv, axis=1, keepdims=True) + EPS)
        xh = xv * r
        dxh = dyv * g_ref[...]
        dx = r * (dxh - xh * jnp.mean(dxh * xh, axis=1, keepdims=True))
        if has_add:
            dx = dx + rest[0][...]
        dx_ref[...] = dx.astype(out_dtype)

        @pl.when(pl.program_id(0) == 0)
        def _():
            dg_ref[...] = jnp.zeros_like(dg_ref)

        dg_ref[...] += jnp.sum(dyv * xh, axis=0, keepdims=True)

    row = pl.BlockSpec((tm, C), lambda i: (i, 0))
    vec = pl.BlockSpec((1, C), lambda i: (0, 0))
    args = (x, dy, g) + ((add,) if has_add else ())
    return pl.pallas_call(
        body, name=name, grid=(S // tm,),
        in_specs=[row, row, vec] + ([row] if has_add else []),
        out_specs=[row, vec],
        out_shape=[jax.ShapeDtypeStruct((S, C), out_dtype), jax.ShapeDtypeStruct((1, C), f32)],
        compiler_params=_params(("arbitrary",)),
    )(*args)


SB_T = 128
SB_SPENT = -120.0
SB_QB = 2
SB_TAIL = 3
SB_GROUPS = (4, 2, 1)
SB_GROUPS_BWD = (4, 2, 1)


def _sb_masks():
    lane = lax.broadcasted_iota(jnp.int32, (1, 128), 1)
    m_a = (lane < SB_HD).astype(f32)
    return m_a, 1.0 - m_a


def _chunks(a, n):
    return [a[:, u * SB_T:(u + 1) * SB_T] for u in range(n)]


def _cat(parts, axis):
    return parts[0] if len(parts) == 1 else jnp.concatenate(parts, axis=axis)


def _mask_last(a, n, mask):
    if mask is None:
        return a
    parts = _chunks(a, n)
    return _cat(parts[:-1] + [jnp.where(mask, parts[-1], 0.0)], 1)


def _sb_logits(z, n, mask):
    l1p = jnp.log(1.0 + jnp.exp(-jnp.abs(z)))
    lb = jnp.minimum(z, 0.0) - l1p
    return lb, _mask_last(lb - z, n, mask)


def _by_count(i, most, fn):
    return lax.switch(jnp.minimum(i, most - 1), [functools.partial(fn, n) for n in range(1, most + 1)])


def _chunk_matmul(parts_list, u_mat):
    out = _dot(_cat(parts_list, 0), u_mat)
    return [out[u * SB_T:(u + 1) * SB_T] for u in range(len(parts_list))]


def _chunk_cumsum(lk, n, u_mat):
    hi = lk.astype(bf16)
    lo = (lk - hi.astype(f32)).astype(bf16)
    out = _chunk_matmul(_chunks(hi, n) + _chunks(lo, n), u_mat)
    return [out[u] + out[n + u] for u in range(n)]


def _sb_fwd(proj, S, ride=None):
    nq = S // SB_T
    n_pairs = D // 128
    scale = SB_HD ** -0.5
    n_r = ride.n if ride else 0

    def body(q_ref, k_ref, v_ref, *rest):
        o_ref, t_ref = rest[n_r:n_r + 2]
        step_i = pl.program_id(1)
        if ride:
            pl.when((pl.program_id(0) == 0) & (step_i == 0))(
                lambda: ride.start(rest[:n_r], rest[n_r + 2:2 * n_r + 2], rest[2 * n_r + 2:]))
        m_a, m_b = _sb_masks()
        r_i = lax.broadcasted_iota(jnp.int32, (SB_T, SB_T), 0)
        c_i = lax.broadcasted_iota(jnp.int32, (SB_T, SB_T), 1)
        u_mat = (r_i > c_i).astype(bf16)
        causal = c_i < r_i
        q_all = q_ref[...] * scale
        q_hs = [((q * m_a).astype(bf16), (q * m_b).astype(bf16))
                for q in (q_all[b * SB_T:(b + 1) * SB_T] for b in range(SB_QB))]

        def group(q_h, j_lo, n, carry, mask):
            acc, c_a, c_b = carry
            rows = pl.ds(pl.multiple_of(j_lo * SB_T, SB_T), n * SB_T)
            k = k_ref[rows, :].astype(bf16)
            v = v_ref[rows, :]
            zs = [_dot(q_b, k, NT) for q_b in q_h]
            lbk = [_sb_logits(z, n, mask) for z in zs]
            parts = [_chunk_cumsum(lk, n, u_mat) for _, lk in lbk]
            ws, cs = [], []
            for (lb, lk), part, c in zip(lbk, parts, (c_a, c_b)):
                lb_c, lk_c = _chunks(lb, n), _chunks(lk, n)
                w_c = [None] * n
                for u in reversed(range(n)):
                    w_c[u] = jnp.exp(lb_c[u] + c + part[u])
                    c = c + jnp.sum(lk_c[u], axis=1, keepdims=True)
                ws.append(_mask_last(_cat(w_c, 1), n, mask).astype(bf16))
                cs.append(c)
            for w, m in zip(ws, (m_a, m_b)):
                acc = acc + _dot(w, (v * m).astype(bf16))
            return acc, cs[0], cs[1]

        zero_c = jnp.zeros((SB_T, 1), f32)
        init = (jnp.zeros((SB_T, 128), f32), zero_c, zero_c)
        blocks = [(step_i * SB_QB + b, q_hs[b]) for b in range(SB_QB)]

        def whole_tails():
            return tuple(group(q_h, i - SB_TAIL + 1, SB_TAIL, init, causal) for i, q_h in blocks)

        def short_tails():
            return tuple(_by_count(i, SB_TAIL, functools.partial(
                lambda n, i, q_h: group(q_h, i - n + 1, n, init, causal), i=i, q_h=q_h)) for i, q_h in blocks)

        carries = lax.cond(step_i * SB_QB >= SB_TAIL - 1, whole_tails, short_tails)

        def spent(cr):
            return (jnp.max(jnp.maximum(cr[1], cr[2])) < SB_SPENT).astype(jnp.int32)

        lane = lax.broadcasted_iota(jnp.int32, (1, 128), 1)
        for b, ((i, q_h), carry) in enumerate(zip(blocks, carries)):
            state = (i - jnp.minimum(i, SB_TAIL - 1), spent(carry), carry)
            for n in SB_GROUPS:
                def step(st, n=n, q_h=q_h):
                    left, _, cr = st
                    cr = group(q_h, left - n, n, cr, None)
                    return left - n, spent(cr), cr

                state = lax.while_loop(lambda st, n=n: (st[0] >= n) & (st[1] == 0), step, state)
            left, _, carry = state
            rows = slice(b * SB_T, (b + 1) * SB_T)
            o_ref[rows, :] = carry[0]
            t_ref[rows, :] = (jnp.where(lane == 0, carry[1], 0.0) + jnp.where(lane == SB_HD, carry[2], 0.0)
                              + jnp.where(lane == 1, left.astype(f32), 0.0))
        if ride:
            pl.when((pl.program_id(0) == n_pairs - 1) & (step_i == nq // SB_QB - 1))(
                lambda: ride.finish(rest[:n_r], rest[n_r + 2:2 * n_r + 2], rest[2 * n_r + 2:]))

    qs = pl.BlockSpec((SB_QB * SB_T, 128), lambda h, i: (i, h))
    out = pl.pallas_call(
        body, name="sb_fwd", grid=(n_pairs, nq // SB_QB),
        in_specs=[qs,
                  pl.BlockSpec((S, 128), lambda h, i: (0, n_pairs + h)),
                  pl.BlockSpec((S, 128), lambda h, i: (0, 2 * n_pairs + h))] + (ride.in_specs if ride else []),
        out_specs=[qs, qs] + (ride.out_specs if ride else []),
        out_shape=[jax.ShapeDtypeStruct((S, D), f32)] * 2 + (ride.out_shape if ride else []),
        scratch_shapes=ride.scratch if ride else [],
        compiler_params=_params(("arbitrary", "arbitrary")),
    )(proj, proj, proj, *(ride.srcs if ride else []))
    return out[0], out[1], list(out[2:])


def _sb_bwd(proj, tot_lk, do, S, ride=None):
    nq = S // SB_T
    n_pairs = D // 128
    scale = SB_HD ** -0.5
    n_r = ride.n if ride else 0

    def body(q_ref, k_ref, v_ref, t_ref, do_ref, *rest):
        dq_ref, dk_ref, dv_ref = rest[n_r:n_r + 3]
        dk_acc, dv_acc = rest[2 * n_r + 3:2 * n_r + 5]
        r_ins, r_lnd, r_sems = rest[:n_r], rest[n_r + 3:2 * n_r + 3], rest[2 * n_r + 5:]
        step_i = pl.program_id(1)
        if ride:
            pl.when((pl.program_id(0) == 0) & (step_i == 0))(lambda: ride.start(r_ins, r_lnd, r_sems))
        m_a, m_b = _sb_masks()
        r_i = lax.broadcasted_iota(jnp.int32, (SB_T, SB_T), 0)
        c_i = lax.broadcasted_iota(jnp.int32, (SB_T, SB_T), 1)
        u_inc = (r_i <= c_i).astype(bf16)
        u_exc = (r_i < c_i).astype(bf16)
        causal = c_i < r_i

        @pl.when(step_i == 0)
        def _():
            dk_acc[...] = jnp.zeros_like(dk_acc)
            dv_acc[...] = jnp.zeros_like(dv_acc)

        lane = lax.broadcasted_iota(jnp.int32, (1, 128), 1)
        blocks = []
        for b in range(SB_QB):
            rows_b = slice(b * SB_T, (b + 1) * SB_T)
            i = step_i * SB_QB + b
            q = q_ref[rows_b, :] * scale
            dov = do_ref[rows_b, :]
            tv = t_ref[rows_b, :]
            heads = []
            for m, first in ((m_a, 0), (m_b, SB_HD)):
                tot = jnp.sum(jnp.where(lane == first, tv, 0.0), axis=1, keepdims=True)
                heads.append(((q * m).astype(bf16), (dov * m).astype(bf16), tot, m))
            lowest = jnp.clip(jnp.max(jnp.where(lane == 1, tv, 0.0)).astype(jnp.int32), 0, i)
            blocks.append((i, heads, lowest))

        def group(heads, j_lo, n, carry, mask):
            dq_acc, cp_a, cp_b, ce_a, ce_b = carry
            rows = pl.ds(pl.multiple_of(j_lo * SB_T, SB_T), n * SB_T)
            k_f = k_ref[rows, :]
            k = k_f.astype(bf16)
            v = v_ref[rows, :].astype(bf16)
            zs = [_dot(h[0], k, NT) for h in heads]
            dws = [_dot(h[1], v, NT) for h in heads]
            lbk = [_sb_logits(z, n, mask) for z in zs]
            parts = [_chunk_cumsum(lk, n, u_inc) for _, lk in lbk]
            ws, es, cps = [], [], []
            for (lb, lk), part, dw, h, cp in zip(lbk, parts, dws, heads, (cp_a, cp_b)):
                lb_c, lk_c = _chunks(lb, n), _chunks(lk, n)
                w_c = []
                for u in range(n):
                    w_c.append(jnp.exp(lb_c[u] + (h[2] - cp) - part[u]))
                    cp = cp + jnp.sum(lk_c[u], axis=1, keepdims=True)
                w = _mask_last(_cat(w_c, 1), n, mask)
                ws.append(w)
                es.append(dw * w)
                cps.append(cp)
            e_parts = [_chunk_matmul(_chunks(e.astype(bf16), n), u_exc) for e in es]
            dzs, ces = [], []
            for (lb, _), e, e_part, ce in zip(lbk, es, e_parts, (ce_a, ce_b)):
                e_c = _chunks(e, n)
                big_c = []
                for u in range(n):
                    big_c.append(ce + e_part[u])
                    ce = ce + jnp.sum(e_c[u], axis=1, keepdims=True)
                sig = jnp.exp(lb)
                dz = _mask_last(e * (1.0 - sig) - _cat(big_c, 1) * sig, n, mask)
                dzs.append(dz.astype(bf16))
                ces.append(ce)
            dk_t = jnp.zeros((n * SB_T, 128), f32)
            dv_t = jnp.zeros((n * SB_T, 128), f32)
            for dz_b, w, h in zip(dzs, ws, heads):
                dq_acc = dq_acc + _dot(dz_b, (k_f * h[3]).astype(bf16))
                dk_t = dk_t + _dot(dz_b, h[0], TN)
                dv_t = dv_t + _dot(w.astype(bf16), h[1], TN)
            dk_acc[rows, :] += dk_t
            dv_acc[rows, :] += dv_t
            return dq_acc, cps[0], cps[1], ces[0], ces[1]

        zc = jnp.zeros((SB_T, 1), f32)
        carries = []
        for i, heads, lowest in blocks:
            carry = (jnp.zeros((SB_T, 128), f32), zc, zc, zc, zc)
            done = lowest
            tail_lo = i - jnp.minimum(i, SB_TAIL - 1)
            for n in SB_GROUPS_BWD:
                trips = (tail_lo - done) // n
                carry = lax.fori_loop(
                    0, trips, functools.partial(
                        lambda gi, cr, n, done, heads: group(heads, done + gi * n, n, cr, None),
                        n=n, done=done, heads=heads),
                    carry)
                done = done + trips * n
            carries.append(carry)

        def whole_tails():
            return tuple(group(heads, i - SB_TAIL + 1, SB_TAIL, cr, causal)
                         for (i, heads, _), cr in zip(blocks, carries))

        def short_tails():
            return tuple(_by_count(i, SB_TAIL, functools.partial(
                lambda n, i, heads, cr: group(heads, i - n + 1, n, cr, causal), i=i, heads=heads, cr=cr))
                for (i, heads, _), cr in zip(blocks, carries))

        carries = lax.cond(step_i * SB_QB >= SB_TAIL - 1, whole_tails, short_tails)
        for b, carry in enumerate(carries):
            dq_ref[b * SB_T:(b + 1) * SB_T, :] = (carry[0] * scale).astype(bf16)

        @pl.when(step_i == nq // SB_QB - 1)
        def _():
            dk_ref[...] = dk_acc[...].astype(bf16)
            dv_ref[...] = dv_acc[...].astype(bf16)

        if ride:
            pl.when((pl.program_id(0) == n_pairs - 1) & (step_i == nq // SB_QB - 1))(
                lambda: ride.finish(r_ins, r_lnd, r_sems))

    qs = pl.BlockSpec((SB_QB * SB_T, 128), lambda h, i: (i, h))
    full = pl.BlockSpec((S, 128), lambda h, i: (0, h))
    out = pl.pallas_call(
        body, name="sb_bwd", grid=(n_pairs, nq // SB_QB),
        in_specs=[qs,
                  pl.BlockSpec((S, 128), lambda h, i: (0, n_pairs + h)),
                  pl.BlockSpec((S, 128), lambda h, i: (0, 2 * n_pairs + h)),
                  qs, qs] + (ride.in_specs if ride else []),
        out_specs=[qs, full, full] + (ride.out_specs if ride else []),
        out_shape=[jax.ShapeDtypeStruct((S, D), bf16)] * 3 + (ride.out_shape if ride else []),
        scratch_shapes=[pltpu.VMEM((S, 128), f32), pltpu.VMEM((S, 128), f32)] + (ride.scratch if ride else []),
        compiler_params=_params(("arbitrary", "arbitrary")),
    )(proj, proj, proj, tot_lk, do, *(ride.srcs if ride else []))
    return out[0], out[1], out[2], list(out[3:])


CONV_CB = 256
HALO = 8


def _conv_fwd(proj, conv_w, conv_b, S):
    tr = min(512, S)

    def body(x_ref, w_ref, b_ref, xc_ref, xbc_ref):
        w = w_ref[...]
        for t in range(S // tr):
            cur = x_ref[t * tr:(t + 1) * tr, :]
            halo = x_ref[t * tr - HALO:t * tr, :] if t else jnp.zeros((HALO, CONV_CB), f32)
            win = jnp.concatenate([halo, cur], axis=0)
            acc = b_ref[...] + w[CONV_K - 1:CONV_K, :] * cur
            for k in range(CONV_K - 1):
                acc = acc + w[k:k + 1, :] * pltpu.roll(win, CONV_K - 1 - k, 0)[HALO:, :]
            xc_ref[t * tr:(t + 1) * tr, :] = acc
            xbc_ref[t * tr:(t + 1) * tr, :] = acc * _sigmoid(acc)

    col = pl.BlockSpec((S, CONV_CB), lambda c: (0, c))
    return pl.pallas_call(
        body, name="conv_fwd", grid=(CONV_DIM // CONV_CB,),
        in_specs=[pl.BlockSpec((S, CONV_CB), lambda c: (0, P_XBC // CONV_CB + c)),
                  pl.BlockSpec((CONV_K, CONV_CB), lambda c: (0, c)),
                  pl.BlockSpec((1, CONV_CB), lambda c: (0, c))],
        out_specs=[col, col], out_shape=[jax.ShapeDtypeStruct((S, CONV_DIM), f32)] * 2,
        compiler_params=_params(("parallel",)),
    )(proj, conv_w, conv_b)


def _conv_bwd(proj, xc, dxbc, conv_w, S):
    tr = min(512, S)

    def body(x_ref, xc_ref, dy_ref, w_ref, dx_ref, dw_ref, db_ref, dxc_s):
        w = w_ref[...]
        xcv = xc_ref[...]
        sg = _sigmoid(xcv)
        dxc_s[0:S, :] = dy_ref[...] * (sg * (1.0 + xcv * (1.0 - sg)))
        dxc_s[S:S + HALO, :] = jnp.zeros((HALO, CONV_CB), f32)
        dws = [jnp.zeros((1, CONV_CB), f32) for _ in range(CONV_K)]
        db = jnp.zeros((1, CONV_CB), f32)
        for t in range(S // tr):
            cur = x_ref[t * tr:(t + 1) * tr, :]
            halo = x_ref[t * tr - HALO:t * tr, :] if t else jnp.zeros((HALO, CONV_CB), f32)
            win = jnp.concatenate([halo, cur], axis=0)
            dwin = dxc_s[t * tr:(t + 1) * tr + HALO, :]
            dcur = dwin[0:tr, :]
            db = db + jnp.sum(dcur, axis=0, keepdims=True)
            dws[CONV_K - 1] = dws[CONV_K - 1] + jnp.sum(dcur * cur, axis=0, keepdims=True)
            dx = w[CONV_K - 1:CONV_K, :] * dcur
            for k in range(CONV_K - 1):
                sh = CONV_K - 1 - k
                dws[k] = dws[k] + jnp.sum(dcur * pltpu.roll(win, sh, 0)[HALO:, :], axis=0, keepdims=True)
                dx = dx + w[k:k + 1, :] * pltpu.roll(dwin, tr + HALO - sh, 0)[0:tr, :]
            dx_ref[t * tr:(t + 1) * tr, :] = dx.astype(bf16)
        dw_ref[...] = jnp.concatenate(dws + [jnp.zeros((8 - CONV_K, CONV_CB), f32)], axis=0)
        db_ref[...] = db

    col = pl.BlockSpec((S, CONV_CB), lambda c: (0, c))
    return pl.pallas_call(
        body, name="conv_bwd", grid=(CONV_DIM // CONV_CB,),
        in_specs=[pl.BlockSpec((S, CONV_CB), lambda c: (0, P_XBC // CONV_CB + c)), col, col,
                  pl.BlockSpec((CONV_K, CONV_CB), lambda c: (0, c))],
        out_specs=[col, pl.BlockSpec((8, CONV_CB), lambda c: (0, c)), pl.BlockSpec((1, CONV_CB), lambda c: (0, c))],
        out_shape=[jax.ShapeDtypeStruct((S, CONV_DIM), bf16), jax.ShapeDtypeStruct((8, CONV_DIM), f32),
                   jax.ShapeDtypeStruct((1, CONV_DIM), f32)],
        scratch_shapes=[pltpu.VMEM((S + HALO, CONV_CB), f32)],
        compiler_params=_params(("parallel",)),
    )(proj, xc, dxbc, conv_w)


N_PAIR = SSD_HEADS // 2
NEG = -1e30


def _softplus(x):
    return jnp.maximum(x, 0.0) + jnp.log(1.0 + jnp.exp(-jnp.abs(x)))


def _ssd_common(dtr, dtb, alog):
    L = SSD_L
    r_i = lax.broadcasted_iota(jnp.int32, (L, L), 0)
    c_i = lax.broadcasted_iota(jnp.int32, (L, L), 1)
    dt = _softplus(dtr + dtb)
    a = -jnp.exp(alog)
    da = dt * a
    lower = (r_i >= c_i).astype(bf16)
    upper = (r_i <= c_i).astype(bf16)
    parts = _split3(da)
    a_cs = sum(_dot(lower, p) for p in parts)
    a_cs_t = sum(_dot(p, upper, TN) for p in parts)
    return dt, a, a_cs, a_cs_t, r_i >= c_i


def _pair_vec(lane, v, h):
    return jnp.where(lane < SB_HD, v[:, h:h + 1], v[:, h + 1:h + 2])


def _decay_mat(a_cs, a_cs_t, h, tril):
    return jnp.exp(jnp.where(tril, a_cs[:, h:h + 1] - a_cs_t[h:h + 1, :], NEG))


def _ssd_fwd(xbc, proj, pdt, dt_bias_p, a_log_p, d_skip_c, ssd_norm, S, ride=None):
    L = SSD_L
    nc = S // L
    n_r = ride.n if ride else 0

    def body(xbc_ref, dt_ref, z_ref, dtb_ref, alog_ref, dsk_ref, gn_ref, *rest):
        y_ref, yn_ref, hp_ref = rest[n_r:n_r + 3]
        state = rest[2 * n_r + 3]
        r_ins, r_lnd, r_sems = rest[:n_r], rest[n_r + 3:2 * n_r + 3], rest[2 * n_r + 4:]
        c = pl.program_id(0)
        if ride:
            pl.when(c == 0)(lambda: ride.start(r_ins, r_lnd, r_sems))

        @pl.when(c == 0)
        def _():
            state[...] = jnp.zeros_like(state)

        hp_ref[0] = state[...]
        lane = lax.broadcasted_iota(jnp.int32, (1, 128), 1)
        row128 = lax.broadcasted_iota(jnp.int32, (128, 1), 0)
        m_a, m_b = _sb_masks()
        dt, a, a_cs, a_cs_t, tril = _ssd_common(dt_ref[...], dtb_ref[...], alog_ref[...])
        a_last = a_cs[L - 1:L, :]
        for g in range(SSD_GROUPS):
            b_g = xbc_ref[:, SSD_INNER + g * SSD_N:SSD_INNER + (g + 1) * SSD_N].astype(bf16)
            c_g = xbc_ref[:, SSD_INNER + (SSD_GROUPS + g) * SSD_N:SSD_INNER + (SSD_GROUPS + g + 1) * SSD_N].astype(bf16)
            cb = _dot(c_g, b_g, NT)
            for pr in range(4):
                h = 8 * g + 2 * pr
                pi = h // 2
                cols = slice(pi * 128, (pi + 1) * 128)
                xs = xbc_ref[:, cols]
                x = xs * _pair_vec(lane, dt, h)
                acs = _pair_vec(lane, a_cs, h)
                al = _pair_vec(lane, a_last, h)
                w_a = (cb * _decay_mat(a_cs, a_cs_t, h, tril)).astype(bf16)
                w_b = (cb * _decay_mat(a_cs, a_cs_t, h + 1, tril)).astype(bf16)
                yd = _dot(w_a, (x * m_a).astype(bf16)) + _dot(w_b, (x * m_b).astype(bf16))
                hp = state[pi]
                yo = _dot(c_g, hp.astype(bf16), NT) * jnp.exp(acs)
                y_ref[:, cols] = yd + yo + dsk_ref[:, cols] * xs
                dec = jnp.exp(jnp.where(row128 < SB_HD, a_last[:, h:h + 1], a_last[:, h + 1:h + 2]))
                state[pi] = hp * dec + _dot((x * jnp.exp(al - acs)).astype(bf16), b_g, TN)
        zz = z_ref[...]
        y2 = y_ref[...] * (zz * _sigmoid(zz))
        gw = SSD_INNER // SSD_GROUPS
        for g in range(SSD_GROUPS):
            yg = y2[:, g * gw:(g + 1) * gw]
            rg = lax.rsqrt(jnp.mean(yg * yg, axis=1, keepdims=True) + EPS)
            yn_ref[:, g * gw:(g + 1) * gw] = (yg * rg * gn_ref[:, g * gw:(g + 1) * gw]).astype(bf16)
        if ride:
            pl.when(c == nc - 1)(lambda: ride.finish(r_ins, r_lnd, r_sems))

    vec128 = pl.BlockSpec((1, 128), lambda c: (0, 0))
    vecin = pl.BlockSpec((1, SSD_INNER), lambda c: (0, 0))
    rows = pl.BlockSpec((L, SSD_INNER), lambda c: (c, 0))
    out = pl.pallas_call(
        body, name="ssd_fwd", grid=(nc,),
        in_specs=[pl.BlockSpec((L, CONV_DIM), lambda c: (c, 0)),
                  pl.BlockSpec((L, 128), lambda c: (c, 0)),
                  pl.BlockSpec((L, SSD_INNER), lambda c: (c, P_Z // SSD_INNER)),
                  vec128, vec128, vecin, vecin] + (ride.in_specs if ride else []),
        out_specs=[rows, rows, pl.BlockSpec((1, N_PAIR, 128, SSD_N), lambda c: (c, 0, 0, 0))]
        + (ride.out_specs if ride else []),
        out_shape=[jax.ShapeDtypeStruct((S, SSD_INNER), f32), jax.ShapeDtypeStruct((S, SSD_INNER), bf16),
                   jax.ShapeDtypeStruct((nc, N_PAIR, 128, SSD_N), f32)] + (ride.out_shape if ride else []),
        scratch_shapes=[pltpu.VMEM((N_PAIR, 128, SSD_N), f32)] + (ride.scratch if ride else []),
        compiler_params=_params(("arbitrary",)),
    )(xbc, pdt, proj, dt_bias_p, a_log_p, d_skip_c, ssd_norm, *(ride.srcs if ride else []))
    return out[0], out[1], out[2], list(out[3:])


def _sum_all(v):
    return jnp.sum(jnp.sum(v, axis=1, keepdims=True), axis=0, keepdims=True)


def _ssd_bwd(dyn, y, xbc, proj, pdt, hprev, dt_bias_p, a_log_p, d_skip_c, ssd_norm, S, ride=None):
    L = SSD_L
    nc = S // L
    n_r = ride.n if ride else 0

    col = lax.broadcasted_iota(jnp.int32, (2 * SSD_INNER, 128), 0)
    head = lax.broadcasted_iota(jnp.int32, (2 * SSD_INNER, 128), 1)
    sel_pair = (col[:SSD_INNER] // SB_HD == head[:SSD_INNER]).astype(bf16)
    sel_head = (col // 128 == head).astype(bf16)

    def body(*refs):
        (dyn_ref, y_ref, xbc_ref, dt_ref, z_ref, hp_ref, dtb_ref, alog_ref, dsk_ref, gn_ref,
         selp_ref, selh_ref) = refs[:12]
        dz_ref, dxbc_ref, ddt_ref, dgn_ref, dsk_out, dalog_ref, ddtb_ref = refs[12 + n_r:19 + n_r]
        dstate, dy_s, st_a, st_q, st_d, st_x, dat = refs[19 + 2 * n_r:26 + 2 * n_r]
        r_ins, r_lnd, r_sems = refs[12:12 + n_r], refs[19 + n_r:19 + 2 * n_r], refs[26 + 2 * n_r:]
        c = pl.program_id(0)
        if ride:
            pl.when(c == 0)(lambda: ride.start(r_ins, r_lnd, r_sems))

        @pl.when(c == 0)
        def _():
            dat[...] = jnp.zeros_like(dat)
            dstate[...] = jnp.zeros_like(dstate)
            dgn_ref[...] = jnp.zeros_like(dgn_ref)
            dsk_out[...] = jnp.zeros_like(dsk_out)
            dalog_ref[...] = jnp.zeros_like(dalog_ref)
            ddtb_ref[...] = jnp.zeros_like(ddtb_ref)

        lane = lax.broadcasted_iota(jnp.int32, (1, 128), 1)
        row128 = lax.broadcasted_iota(jnp.int32, (128, 1), 0)
        rowl = lax.broadcasted_iota(jnp.int32, (L, 1), 0)
        m_a, m_b = _sb_masks()
        dtr = dt_ref[...]
        dt, a, a_cs, a_cs_t, tril = _ssd_common(dtr, dtb_ref[...], alog_ref[...])
        a_last = a_cs[L - 1:L, :]

        zz = z_ref[...]
        sg = _sigmoid(zz)
        silu = zz * sg
        yv = y_ref[...]
        y2 = yv * silu
        gw = SSD_INNER // SSD_GROUPS
        for g in range(SSD_GROUPS):
            sl = slice(g * gw, (g + 1) * gw)
            yg = y2[:, sl]
            rg = lax.rsqrt(jnp.mean(yg * yg, axis=1, keepdims=True) + EPS)
            yh = yg * rg
            dyn_g = dyn_ref[:, sl]
            dgn_ref[:, sl] += jnp.sum(dyn_g * yh, axis=0, keepdims=True)
            dyh = dyn_g * gn_ref[:, sl]
            dy2 = rg * (dyh - yh * jnp.mean(dyh * yh, axis=1, keepdims=True))
            dy_s[:, sl] = dy2 * silu[:, sl]
            dz_ref[:, sl] = (dy2 * yv[:, sl] * (sg[:, sl] * (1.0 + zz[:, sl] * (1.0 - sg[:, sl])))).astype(bf16)

        last_row = jnp.zeros((1, 128), f32)
        dsk_acc = jnp.zeros((1, 128), f32)
        for g in range(SSD_GROUPS):
            bsl = slice(SSD_INNER + g * SSD_N, SSD_INNER + (g + 1) * SSD_N)
            csl = slice(SSD_INNER + (SSD_GROUPS + g) * SSD_N, SSD_INNER + (SSD_GROUPS + g + 1) * SSD_N)
            b_g = xbc_ref[:, bsl].astype(bf16)
            c_g = xbc_ref[:, csl].astype(bf16)
            cb = _dot(c_g, b_g, NT)
            dcb = jnp.zeros((L, L), f32)
            dc_g = jnp.zeros((L, SSD_N), f32)
            db_g = jnp.zeros((L, SSD_N), f32)
            for pr in range(4):
                h = 8 * g + 2 * pr
                pi = h // 2
                cols = slice(pi * 128, (pi + 1) * 128)
                xs = xbc_ref[:, cols]
                dt_p = _pair_vec(lane, dt, h)
                x = xs * dt_p
                acs = _pair_vec(lane, a_cs, h)
                al = _pair_vec(lane, a_last, h)
                e_a = jnp.exp(acs)
                dte = jnp.exp(al - acs)
                m_mat_a = _decay_mat(a_cs, a_cs_t, h, tril)
                m_mat_b = _decay_mat(a_cs, a_cs_t, h + 1, tril)
                dyp = dy_s[:, cols]
                dsk = dsk_ref[:, cols]
                d_hn = dstate[pi]
                hp = hp_ref[0, pi]
                dy_a = (dyp * m_a).astype(bf16)
                dy_b = (dyp * m_b).astype(bf16)
                x_b = x.astype(bf16)
                gm_a = _dot(dy_a, x_b, NT) * m_mat_a
                gm_b = _dot(dy_b, x_b, NT) * m_mat_b
                dcb = dcb + gm_a + gm_b
                dx_d = _dot((cb * m_mat_a).astype(bf16), dy_a, TN) + _dot((cb * m_mat_b).astype(bf16), dy_b, TN)
                dx_s = _dot(b_g, d_hn.astype(bf16), NT) * dte
                dx = dx_d + dx_s
                dxbc_ref[:, cols] = dx * dt_p + dsk * dyp
                xdxs = x * dx_s
                st_x[:, cols] = xdxs
                st_a[:, cols] = dyp * (_dot(c_g, hp.astype(bf16), NT) * e_a) - xdxs
                st_d[:, cols] = dx * xs
                hh = d_hn * hp
                dsk_row = jnp.sum(dyp * xs, axis=0, keepdims=True)
                dec = jnp.exp(jnp.where(row128 < SB_HD, a_last[:, h:h + 1], a_last[:, h + 1:h + 2]))
                for hd, m, gm in ((h, m_a, gm_a), (h + 1, m_b, gm_b)):
                    half = slice(0, SB_HD) if hd == h else slice(SB_HD, 128)
                    qm = gm * cb
                    st_q[:, hd * 128:(hd + 1) * 128] = qm
                    dat[hd:hd + 1, :] = jnp.sum(qm, axis=0, keepdims=True)
                    hh_sum = jnp.sum(jnp.sum(hh[half, :], axis=0, keepdims=True), axis=1, keepdims=True)
                    last_row = jnp.where(lane == hd, jnp.exp(a_last[:, hd:hd + 1]) * hh_sum, last_row)
                    dsk_acc = jnp.where(lane == hd, jnp.sum(dsk_row * m, axis=1, keepdims=True), dsk_acc)
                dye = (dyp * e_a).astype(bf16)
                dc_g = dc_g + _dot(dye, hp.astype(bf16))
                db_g = db_g + _dot((x * dte).astype(bf16), d_hn.astype(bf16))
                dstate[pi] = dec * d_hn + _dot(dye, c_g, TN)
            dcb_b = dcb.astype(bf16)
            dxbc_ref[:, csl] = dc_g + _dot(dcb_b, b_g)
            dxbc_ref[:, bsl] = db_g + _dot(dcb_b, c_g, TN)

        r_i = lax.broadcasted_iota(jnp.int32, (L, L), 0)
        c_i = lax.broadcasted_iota(jnp.int32, (L, L), 1)
        rev = (r_i <= c_i).astype(bf16)

        def head_sums(st, sel, split=_split2):
            return sum(_dot(p, sel[...]) for p in split(st[...]))

        last_row = last_row + jnp.sum(head_sums(st_x, selp_ref), axis=0, keepdims=True)
        d_acs = (head_sums(st_a, selp_ref) + head_sums(st_q, selh_ref, _split3)
                 + jnp.where(rowl == L - 1, last_row, 0.0))
        ddt_x = head_sums(st_d, selp_ref)
        dda = sum(_dot(rev, p) for p in _split3(d_acs)) - sum(_dot(rev, p, NT) for p in _split3(dat[...]))
        ddt = ddt_x + dda * a
        dalog_ref[...] += jnp.sum(dda * dt, axis=0, keepdims=True) * a
        ddtr = jnp.where(lane < SSD_HEADS, ddt * _sigmoid(dtr + dtb_ref[...]), 0.0)
        ddt_ref[...] = ddtr.astype(bf16)
        ddtb_ref[...] += jnp.sum(ddtr, axis=0, keepdims=True)
        dsk_out[...] += dsk_acc
        if ride:
            pl.when(c == nc - 1)(lambda: ride.finish(r_ins, r_lnd, r_sems))

    rv = lambda c: nc - 1 - c
    vec128 = pl.BlockSpec((1, 128), lambda c: (0, 0))
    vecin = pl.BlockSpec((1, SSD_INNER), lambda c: (0, 0))
    rows = pl.BlockSpec((L, SSD_INNER), lambda c: (rv(c), 0))
    return pl.pallas_call(
        body, name="ssd_bwd", grid=(nc,),
        in_specs=[rows, rows,
                  pl.BlockSpec((L, CONV_DIM), lambda c: (rv(c), 0)),
                  pl.BlockSpec((L, 128), lambda c: (rv(c), 0)),
                  pl.BlockSpec((L, SSD_INNER), lambda c: (rv(c), P_Z // SSD_INNER)),
                  pl.BlockSpec((1, N_PAIR, 128, SSD_N), lambda c: (rv(c), 0, 0, 0)),
                  vec128, vec128, vecin, vecin,
                  pl.BlockSpec((SSD_INNER, 128), lambda c: (0, 0)),
                  pl.BlockSpec((2 * SSD_INNER, 128), lambda c: (0, 0))] + (ride.in_specs if ride else []),
        out_specs=[rows, pl.BlockSpec((L, CONV_DIM), lambda c: (rv(c), 0)),
                   pl.BlockSpec((L, 128), lambda c: (rv(c), 0)), vecin, vec128, vec128, vec128]
        + (ride.out_specs if ride else []),
        out_shape=[jax.ShapeDtypeStruct((S, SSD_INNER), bf16), jax.ShapeDtypeStruct((S, CONV_DIM), f32),
                   jax.ShapeDtypeStruct((S, 128), bf16), jax.ShapeDtypeStruct((1, SSD_INNER), f32),
                   jax.ShapeDtypeStruct((1, 128), f32), jax.ShapeDtypeStruct((1, 128), f32),
                   jax.ShapeDtypeStruct((1, 128), f32)] + (ride.out_shape if ride else []),
        scratch_shapes=[pltpu.VMEM((N_PAIR, 128, SSD_N), f32), pltpu.VMEM((L, SSD_INNER), f32),
                        pltpu.VMEM((L, SSD_INNER), f32), pltpu.VMEM((L, 2 * SSD_INNER), f32),
                        pltpu.VMEM((L, SSD_INNER), f32), pltpu.VMEM((L, SSD_INNER), f32),
                        pltpu.VMEM((128, L), f32)]
        + (ride.scratch if ride else []),
        compiler_params=_params(("arbitrary",)),
    )(dyn, y, xbc, pdt, proj, hprev, dt_bias_p, a_log_p, d_skip_c, ssd_norm, sel_pair, sel_head,
      *(ride.srcs if ride else []))


MEM_W = MEM_HEADS * MEM_HD


def _mem_probs(q, k):
    s = _dot(q, k, NT) * (MEM_HD ** -0.5)
    s = s - jnp.max(s, axis=1, keepdims=True)
    p = jnp.exp(s)
    return p / jnp.sum(p, axis=1, keepdims=True)


def _mem_fwd(proj, kv, S, tm=512):
    tm = min(tm, S)
    M = kv.shape[0]

    def body(q_ref, kv_ref, o_ref):
        for h in range(MEM_HEADS):
            sl = slice(h * MEM_HD, (h + 1) * MEM_HD)
            vsl = slice(MEM_W + h * MEM_HD, MEM_W + (h + 1) * MEM_HD)
            p = _mem_probs(q_ref[:, sl].astype(bf16), kv_ref[:, sl].astype(bf16))
            o_ref[:, sl] = _dot(p.astype(bf16), kv_ref[:, vsl].astype(bf16)).astype(bf16)

    return pl.pallas_call(
        body, name="mem_fwd", grid=(S // tm,),
        in_specs=[pl.BlockSpec((tm, MEM_W), lambda i: (i, P_MEMQ // MEM_W)),
                  pl.BlockSpec((M, 2 * MEM_W), lambda i: (0, 0))],
        out_specs=pl.BlockSpec((tm, MEM_W), lambda i: (i, 0)),
        out_shape=jax.ShapeDtypeStruct((S, MEM_W), bf16),
        compiler_params=_params(("parallel",)),
    )(proj, kv)


def _mem_bwd(proj, kv, dy, S, tm=512):
    tm = min(tm, S)
    M = kv.shape[0]
    scale = MEM_HD ** -0.5

    def body(q_ref, kv_ref, dy_ref, dq_ref, dkv_ref):
        @pl.when(pl.program_id(0) == 0)
        def _():
            dkv_ref[...] = jnp.zeros_like(dkv_ref)

        for h in range(MEM_HEADS):
            sl = slice(h * MEM_HD, (h + 1) * MEM_HD)
            vsl = slice(MEM_W + h * MEM_HD, MEM_W + (h + 1) * MEM_HD)
            q = q_ref[:, sl].astype(bf16)
            k = kv_ref[:, sl].astype(bf16)
            v = kv_ref[:, vsl].astype(bf16)
            dyh = dy_ref[:, sl].astype(bf16)
            p = _mem_probs(q, k)
            dp = _dot(dyh, v, NT)
            ds = (p * (dp - jnp.sum(dp * p, axis=1, keepdims=True)) * scale).astype(bf16)
            dq_ref[:, sl] = _dot(ds, k).astype(bf16)
            dkv_ref[:, sl] += _dot(ds, q, TN)
            dkv_ref[:, vsl] += _dot(p.astype(bf16), dyh, TN)

    return pl.pallas_call(
        body, name="mem_bwd", grid=(S // tm,),
        in_specs=[pl.BlockSpec((tm, MEM_W), lambda i: (i, P_MEMQ // MEM_W)),
                  pl.BlockSpec((M, 2 * MEM_W), lambda i: (0, 0)),
                  pl.BlockSpec((tm, MEM_W), lambda i: (i, 0))],
        out_specs=[pl.BlockSpec((tm, MEM_W), lambda i: (i, 0)), pl.BlockSpec((M, 2 * MEM_W), lambda i: (0, 0))],
        out_shape=[jax.ShapeDtypeStruct((S, MEM_W), bf16), jax.ShapeDtypeStruct((M, 2 * MEM_W), f32)],
        compiler_params=_params(("arbitrary",)),
    )(proj, kv, dy)


def _merge_fwd(proj, t0, t1, t2, S, tm=512):
    tm = min(tm, S)

    def body(g_ref, t0_ref, t1_ref, t2_ref, o_ref):
        acc = jnp.zeros((tm, D), f32)
        for b, t_ref in enumerate((t0_ref, t1_ref, t2_ref)):
            acc = acc + _sigmoid(g_ref[:, b * D:(b + 1) * D]) * t_ref[...]
        o_ref[...] = acc.astype(bf16)

    row = pl.BlockSpec((tm, D), lambda i: (i, 0))
    return pl.pallas_call(
        body, name="merge_fwd", grid=(S // tm,),
        in_specs=[pl.BlockSpec((tm, 3 * D), lambda i: (i, P_GATE // (3 * D))), row, row, row],
        out_specs=row, out_shape=jax.ShapeDtypeStruct((S, D), bf16),
        compiler_params=_params(("parallel",)),
    )(proj, t0, t1, t2)


def _merge_bwd(proj, t0, t1, t2, dm, S, tm=512):
    tm = min(tm, S)

    def body(g_ref, t0_ref, t1_ref, t2_ref, dm_ref, d0_ref, d1_ref, d2_ref, dg_ref):
        dmv = dm_ref[...]
        for b, (t_ref, d_ref) in enumerate(((t0_ref, d0_ref), (t1_ref, d1_ref), (t2_ref, d2_ref))):
            sg = _sigmoid(g_ref[:, b * D:(b + 1) * D])
            d_ref[...] = (dmv * sg).astype(bf16)
            dg_ref[:, b * D:(b + 1) * D] = (dmv * t_ref[...] * sg * (1.0 - sg)).astype(bf16)

    row = pl.BlockSpec((tm, D), lambda i: (i, 0))
    return pl.pallas_call(
        body, name="merge_bwd", grid=(S // tm,),
        in_specs=[pl.BlockSpec((tm, 3 * D), lambda i: (i, P_GATE // (3 * D))), row, row, row, row],
        out_specs=[row, row, row, pl.BlockSpec((tm, 3 * D), lambda i: (i, 0))],
        out_shape=[jax.ShapeDtypeStruct((S, D), bf16)] * 3 + [jax.ShapeDtypeStruct((S, 3 * D), bf16)],
        compiler_params=_params(("parallel",)),
    )(proj, t0, t1, t2, dm)


def _loss_head(ff, g, h1, target, S, tm=512):
    tm = min(tm, S)

    def body(ff_ref, g_ref, h1_ref, t_ref, dh_ref, loss_ref):
        xv = ff_ref[...]
        r = lax.rsqrt(jnp.mean(xv * xv, axis=1, keepdims=True) + EPS)
        err = h1_ref[...] + xv * r * g_ref[...] - t_ref[...]
        dh_ref[...] = err * (1.0 / D)

        @pl.when(pl.program_id(0) == 0)
        def _():
            loss_ref[...] = jnp.zeros_like(loss_ref)

        loss_ref[...] += 0.5 * _sum_all(jnp.mean(err * err, axis=1, keepdims=True)) * jnp.ones((1, 128), f32)

    row = pl.BlockSpec((tm, D), lambda i: (i, 0))
    return pl.pallas_call(
        body, name="loss_head", grid=(S // tm,),
        in_specs=[row, pl.BlockSpec((1, D), lambda i: (0, 0)), row, row],
        out_specs=[row, pl.BlockSpec((1, 128), lambda i: (0, 0))],
        out_shape=[jax.ShapeDtypeStruct((S, D), f32), jax.ShapeDtypeStruct((1, 128), f32)],
        compiler_params=_params(("arbitrary",)),
    )(ff, g, h1, target)


def _local_step(x, mem, target, wts, late_rides, late_weights, small, rest_rides, w_in_ride):
    S = x.shape[0]
    M = mem.shape[0]
    pad = lambda v: jnp.pad(v, ((0, 0), (0, 128 - SSD_HEADS)))
    dtb_p, alog_p = pad(small["dt_bias"]), pad(small["a_log"])
    dsk_c = jnp.repeat(small["d_skip"], SB_HD, axis=1)

    u = _rms_fwd(x, small["norm_mix_pre"], name="norm_pre", out_dtype=bf16)
    rides = late_rides or (None, None, None)
    if late_rides:
        proj, lands_a = _mm(u, wts["w_main"], "nn", tm=1024, tn=1024, name="in_proj", ride=rides[0])
    else:
        proj, lands_a = _mm(u, wts["w_main"], "nn", tm=1024, tn=1024, name="in_proj"), []
    pdt = _mm(u, wts["w_dt"], "nn", tm=1024, tn=128, name="in_proj_dt")
    y_sb, tot_lk, lands_b = _sb_fwd(proj, S, rides[1])
    wts = dict(wts, **late_weights(0, lands_a))
    small = dict(small, conv_w=wts.pop("conv_w"))
    xc, xbc = _conv_fwd(proj, small["conv_w"], small["conv_b"], S)
    y_ssd, yn, hprev, lands_c = _ssd_fwd(xbc, proj, pdt, dtb_p, alog_p, dsk_c, small["ssd_norm"], S, rides[2])
    wts = dict(wts, **late_weights(1, lands_b), **late_weights(2, lands_c))
    mn = _rms_fwd(mem, small["norm_mem"], name="norm_mem", out_dtype=bf16, tm=min(512, M))
    kv = _mm(mn, wts["w_mem_kv"], "nn", tm=M, tn=1024, name="mem_kv")
    y_mem = _mem_fwd(proj, kv, S)
    t0 = _mm(y_sb, wts["w_sb_out"], "nn", tm=1024, tn=1024, name="sb_out")
    t1 = _mm(yn, wts["w_ssd_out"], "nn", tm=1024, tn=1024, name="ssd_out")
    t2 = _mm(y_mem, wts["w_mem_out"], "nn", tm=1024, tn=1024, name="mem_out")
    merged = _merge_fwd(proj, t0, t1, t2, S)
    mix = _mm(merged, wts["w_o"], "nn", tm=1024, tn=1024, name="w_o")
    h1 = _rms_fwd(mix, small["norm_mix_post"], name="norm_mix_post", out_dtype=f32, residual=x)
    u2 = _rms_fwd(h1, small["norm_mlp_pre"], name="norm_mlp_pre", out_dtype=bf16)
    a_up, hrelu = _mm(u2, wts["w_up"], "nn", tm=1024, tn=1024, name="mlp_up", out_dtypes=(f32, bf16),
                      epi=lambda acc: (acc, jnp.square(jnp.maximum(acc, 0.0))))
    ff = _mm(hrelu, wts["w_down"], "nn", tm=1024, tn=1024, name="mlp_down")
    dh2, loss = _loss_head(ff, small["norm_mlp_post"], h1, target, S)

    g = {}
    dff, g["norm_mlp_post"] = _rms_bwd(ff, dh2, small["norm_mlp_post"], name="norm_mlp_post_bwd", out_dtype=bf16)
    da = _mm(dff, wts["w_down"], "nt", tm=1024, tn=1024, name="mlp_down_dx", out_dtypes=(bf16,),
             epi=lambda acc, a: (acc * (2.0 * jnp.maximum(a, 0.0)),), extras=(a_up,))
    g["w_down"] = _mm(hrelu, dff, "tn", tm=1024, tn=1024, name="mlp_down_dw")
    du2 = _mm(da, wts["w_up"], "nt", tm=1024, tn=1024, name="mlp_up_dx")
    g["w_up"] = _mm(u2, da, "tn", tm=1024, tn=1024, name="mlp_up_dw")
    dh1, g["norm_mlp_pre"] = _rms_bwd(h1, du2, small["norm_mlp_pre"], name="norm_mlp_pre_bwd", out_dtype=f32, add=dh2)
    dmix, g["norm_mix_post"] = _rms_bwd(mix, dh1, small["norm_mix_post"], name="norm_mix_post_bwd", out_dtype=bf16)
    dmerged = _mm(dmix, wts["w_o"], "nt", tm=1024, tn=1024, name="w_o_dx")
    g["w_o"] = _mm(merged, dmix, "tn", tm=1024, tn=1024, name="w_o_dw")
    dt0, dt1, dt2, dgl = _merge_bwd(proj, t0, t1, t2, dmerged, S)
    dy_sb = _mm(dt0, wts["w_sb_out"], "nt", tm=1024, tn=1024, name="sb_out_dx")
    g["w_sb_out"] = _mm(y_sb, dt0, "tn", tm=1024, tn=1024, name="sb_out_dw")
    dy_ssd = _mm(dt1, wts["w_ssd_out"], "nt", tm=1024, tn=1024, name="ssd_out_dx")
    g["w_ssd_out"] = _mm(yn, dt1, "tn", tm=1024, tn=1024, name="ssd_out_dw")
    dy_mem = _mm(dt2, wts["w_mem_out"], "nt", tm=1024, tn=1024, name="mem_out_dx")
    g["w_mem_out"] = _mm(y_mem, dt2, "tn", tm=1024, tn=1024, name="mem_out_dw")
    dmemq, dkv = _mem_bwd(proj, kv, dy_mem, S)
    g["w_mem_kv"] = _mm(mn, dkv, "tn", tm=1024, tn=1024, name="mem_kv_dw")
    dmn = _mm(dkv, wts["w_mem_kv"], "nt", tm=M, tn=1024, name="mem_kv_dx")
    _, g["norm_mem"] = _rms_bwd(mem, dmn, small["norm_mem"], name="norm_mem_bwd", out_dtype=bf16, tm=min(512, M))
    rides = rest_rides(g) if rest_rides else (None, None)
    dz, dxbc, ddt, g["ssd_norm"], dsk, dalog, ddtb, *lands_a = _ssd_bwd(
        dy_ssd, y_ssd, xbc, proj, pdt, hprev, dtb_p, alog_p, dsk_c, small["ssd_norm"], S, rides[0])
    g["d_skip"], g["a_log"], g["dt_bias"] = dsk[:, :SSD_HEADS], dalog[:, :SSD_HEADS], ddtb[:, :SSD_HEADS]
    dxbc_raw, dcw, g["conv_b"] = _conv_bwd(proj, xc, dxbc, small["conv_w"], S)
    g["conv_w"] = dcw[:CONV_K]
    dq, dk, dv, lands_b = _sb_bwd(proj, tot_lk, dy_sb, S, rides[1])
    g["rest_lands"] = lands_b + lands_a
    dproj = (dq, dk, dv, dxbc_raw, dgl, dmemq, dz)
    u_t = u.T
    g["w_main"] = [_mm(u_t, p, "nn", tm=512, tn=1024, name="in_proj_dw_%d" % i) for i, p in enumerate(dproj)]
    g["w_dt"] = _mm(u_t, ddt, "nn", tm=512, tn=128, name="in_proj_dt_dw")
    du_dt = _mm(ddt, wts["w_dt"], "nt", tm=1024, tn=1024, name="in_proj_dt_dx")
    du, g["w_in_lands"] = _mm_pieces_nt(dproj, wts["w_main"], du_dt, tm=512, tn=256, name="in_proj_dx",
                                        ride=w_in_ride(g) if w_in_ride else None)
    grad_x, g["norm_mix_pre"] = _rms_bwd(x, du, small["norm_mix_pre"], name="norm_pre_bwd", out_dtype=f32, add=dh1)
    return loss, grad_x, g


def _to_internal(w_in):
    sec = lambda r: w_in[:, r[0]:r[1]]
    w_main = jnp.concatenate([sec(R_QKV), sec(R_XBC), sec(R_GATE), sec(R_MEMQ), sec(R_Z)], axis=1)
    w_dt = jnp.pad(sec(R_DT), ((0, 0), (0, 128 - SSD_HEADS)))
    return w_main, w_dt


def _from_internal(pieces, g_dt):
    dq, dk, dv, dxbc, dgate, dmemq, dz = pieces
    return [dq, dk, dv, dz, dxbc, g_dt[:, :SSD_HEADS], dmemq, dgate]


def _w_in_slab(ordered, s, dtype):
    width = D_IN // N_SHARD
    lo, hi, off, parts = s * width, (s + 1) * width, 0, []
    for p in ordered:
        a, b = max(lo, off), min(hi, off + p.shape[1])
        if a < b:
            parts.append(p[:, a - off:b - off].astype(dtype))
        off += p.shape[1]
    return jnp.concatenate(parts, axis=1)


MESH = pl.DeviceIdType.MESH
ANY = pl.BlockSpec(memory_space=pl.ANY)


def _place():
    x, y, c = lax.axis_index("x"), lax.axis_index("y"), lax.axis_index("c")
    return (x, y, c), [(1 - x, y, c), (x, 1 - y, c), (1 - x, 1 - y, c)]


def _exchange_copy(mode, ins, lands, send, recv, a, k, me, peers, arriving):
    p = peers[k]
    theirs = 2 * p[0] + p[1]
    if mode == "gather":
        src, dst = ins[a], lands[a].at[theirs if arriving else me]
    else:
        src, dst = ins[a].at[theirs], lands[a].at[k]
    return pltpu.make_async_remote_copy(src_ref=src, dst_ref=dst, send_sem=send.at[a * 3 + k],
                                        recv_sem=recv.at[a * 3 + k], device_id=p, device_id_type=MESH)


class _Ride:
    def __init__(self, srcs, mode):
        self.srcs, self.mode, self.n = list(srcs), mode, len(srcs)
        n = self.n
        self.in_specs, self.out_specs = [ANY] * n, [ANY] * n
        self.out_shape = [
            jax.ShapeDtypeStruct((N_SHARD,) + s.shape if mode == "gather" else (3,) + s.shape[1:], s.dtype)
            for s in self.srcs]
        self.scratch = [pltpu.SemaphoreType.DMA((3 * n,)), pltpu.SemaphoreType.DMA((3 * n,)),
                        pltpu.SemaphoreType.DMA((n,))]

    def _own(self, ins, lnd, sems):
        if self.mode != "gather":
            return []
        me = 2 * lax.axis_index("x") + lax.axis_index("y")
        return [pltpu.make_async_copy(ins[a], lnd[a].at[me], sems[2].at[a]) for a in range(self.n)]

    def _far(self, ins, lnd, sems, arriving):
        (x, y, c), peers = _place()
        return [_exchange_copy(self.mode, ins, lnd, sems[0], sems[1], a, k, 2 * x + y, peers, arriving)
                for a in range(self.n) for k in range(3)]

    def start(self, ins, lnd, sems):
        for cp in self._own(ins, lnd, sems) + self._far(ins, lnd, sems, False):
            cp.start()

    def finish(self, ins, lnd, sems):
        for cp in self._far(ins, lnd, sems, True):
            cp.wait_recv()
        for cp in self._far(ins, lnd, sems, False):
            cp.wait_send()
        for cp in self._own(ins, lnd, sems):
            cp.wait()


def _gather_two_level(shards, name):
    n = len(shards)

    def body(*refs):
        ins, lnd = refs[:n], refs[n:2 * n]
        send, recv, loc = refs[2 * n:]
        (x, y, c), peers = _place()
        me = 2 * x + y

        def half(ref, a, core):
            rows = shards[a].shape[0] // 2
            return ref.at[pl.ds(core * rows, rows)]

        def copy(a, j, slot, core, to):
            return pltpu.make_async_remote_copy(
                src_ref=half(ins[a], a, core) if j < 3 else half(lnd[a].at[slot], a, core),
                dst_ref=half(lnd[a].at[slot], a, core), send_sem=send.at[6 * a + j], recv_sem=recv.at[6 * a + j],
                device_id=to, device_id_type=MESH)

        own = [pltpu.make_async_copy(ins[a], lnd[a].at[me], loc.at[a]) for a in range(n)]
        far = [copy(a, k, me, c, peers[k]) for a in range(n) for k in range(3)]
        for cp in own + far:
            cp.start()
        passed = []
        for a in range(n):
            for k, p in enumerate(peers):
                theirs = 2 * p[0] + p[1]
                copy(a, k, theirs, c, p).wait_recv()
                passed.append(copy(a, 3 + k, theirs, c, (x, y, 1 - c)))
                passed[-1].start()
        for a in range(n):
            for k, p in enumerate(peers):
                copy(a, 3 + k, 2 * p[0] + p[1], 1 - c, (x, y, 1 - c)).wait_recv()
        for cp in far + passed:
            cp.wait_send()
        for cp in own:
            cp.wait()

    return pl.pallas_call(
        body, name=name, in_specs=[ANY] * n, out_specs=[ANY] * n,
        out_shape=[jax.ShapeDtypeStruct((N_SHARD,) + s.shape, s.dtype) for s in shards],
        scratch_shapes=[pltpu.SemaphoreType.DMA((6 * n,)), pltpu.SemaphoreType.DMA((6 * n,)),
                        pltpu.SemaphoreType.DMA((n,))],
    )(*shards)


def _exchange_packets(packet):
    def body(pk, pk_out, send, recv, loc):
        x, y, c = lax.axis_index("x"), lax.axis_index("y"), lax.axis_index("c")
        lin = 4 * x + 2 * y + c
        own = pltpu.make_async_copy(pk, pk_out.at[lin], loc.at[0])
        own.start()

        def pk_copy(m, slot):
            dev = (x ^ ((m >> 2) & 1), y ^ ((m >> 1) & 1), c ^ (m & 1))
            return pltpu.make_async_remote_copy(
                src_ref=pk, dst_ref=pk_out.at[slot], send_sem=send.at[m - 1], recv_sem=recv.at[m - 1],
                device_id=dev, device_id_type=MESH)

        sent = [pk_copy(m, lin) for m in range(1, N_DEV)]
        for cp in sent:
            cp.start()
        for m in range(1, N_DEV):
            pk_copy(m, lin ^ m).wait_recv()
        for cp in sent:
            cp.wait_send()
        own.wait()

    return pl.pallas_call(
        body, name="exchange_packets", in_specs=[ANY], out_specs=ANY,
        out_shape=jax.ShapeDtypeStruct((N_DEV,) + packet.shape, packet.dtype),
        scratch_shapes=[pltpu.SemaphoreType.DMA((N_DEV - 1,)), pltpu.SemaphoreType.DMA((N_DEV - 1,)),
                        pltpu.SemaphoreType.DMA((1,))],
    )(packet)


def _swap_sibling(parts, name):
    n = len(parts)

    def body(*refs):
        ins, outs = refs[:n], refs[n:2 * n]
        send, recv = refs[2 * n:]
        x, y, c = lax.axis_index("x"), lax.axis_index("y"), lax.axis_index("c")
        cps = [pltpu.make_async_remote_copy(
            src_ref=ins[a], dst_ref=outs[a], send_sem=send.at[a], recv_sem=recv.at[a],
            device_id=(x, y, 1 - c), device_id_type=MESH) for a in range(n)]
        for cp in cps:
            cp.start()
        for cp in cps:
            cp.wait_recv()
        for cp in cps:
            cp.wait_send()

    return pl.pallas_call(
        body, name=name,
        in_specs=[ANY] * n, out_specs=[ANY] * n,
        out_shape=[jax.ShapeDtypeStruct(p.shape, p.dtype) for p in parts],
        scratch_shapes=[pltpu.SemaphoreType.DMA((n,)), pltpu.SemaphoreType.DMA((n,))],
    )(*parts)


BLOCK_ELEMS = 256 * 1024


def _row_tile(R, C):
    tr = max(8, (BLOCK_ELEMS // C) // 8 * 8)
    while R % tr:
        tr -= 8
    return min(tr, R)


def _sum_parts(own, stack, name, out_dtype=f32):
    k = stack.shape[0]
    R, C = stack.shape[1:]
    tr = _row_tile(R, C)

    def body(*refs):
        o_ref = refs[-1]
        acc = refs[0][...].astype(f32)
        for r in refs[1:-1]:
            acc = acc + r[...].astype(f32)
        o_ref[...] = acc.astype(out_dtype)

    row = pl.BlockSpec((tr, C), lambda i: (i, 0))
    specs = ([row] if own is not None else []) + [
        pl.BlockSpec((None, tr, C), functools.partial(lambda i, j: (j, i, 0), j=j)) for j in range(k)]
    args = ([own] if own is not None else []) + [stack] * k
    return pl.pallas_call(
        body, name=name, grid=(R // tr,), in_specs=specs, out_specs=row,
        out_shape=jax.ShapeDtypeStruct((R, C), out_dtype), compiler_params=_params(("parallel",)),
    )(*args)


def _adamw(w, m, v, g_parts, name):
    R, C = w.shape
    tr = _row_tile(R, C)
    n_g = len(g_parts)

    def body(w_ref, m_ref, v_ref, *rest):
        g = rest[0][...]
        for r in rest[1:n_g]:
            g = g + r[...]
        g_ref, d_ref, nm_ref, nv_ref = rest[n_g:]
        nm = ADAM_B1 * m_ref[...] + (1.0 - ADAM_B1) * g
        nv = ADAM_B2 * v_ref[...] + (1.0 - ADAM_B2) * jnp.square(g)
        m_hat = nm / (1.0 - ADAM_B1 ** ADAM_STEP)
        v_hat = nv / (1.0 - ADAM_B2 ** ADAM_STEP)
        g_ref[...] = g
        d_ref[...] = -ADAM_LR * (m_hat / (jnp.sqrt(v_hat) + ADAM_EPS) + ADAM_WD * w_ref[...])
        nm_ref[...] = nm
        nv_ref[...] = nv

    row = pl.BlockSpec((tr, C), lambda i: (i, 0))
    return pl.pallas_call(
        body, name=name, grid=(R // tr,), in_specs=[row] * (3 + n_g), out_specs=[row] * 4,
        out_shape=[jax.ShapeDtypeStruct((R, C), f32)] * 4, compiler_params=_params(("parallel",)),
    )(w, m, v, *g_parts)


BIG = ("w_in", "w_mem_kv", "w_sb_out", "w_ssd_out", "w_mem_out", "w_o", "w_up", "w_down")
LATE = ("w_sb_out", "w_ssd_out", "w_mem_out", "w_o", "w_up", "w_down")
REST = BIG[1:]
COL_SHARDED = ("w_in", "w_mem_kv", "w_up")
SMALL = ("norm_mix_pre", "conv_w", "conv_b", "dt_bias", "a_log", "d_skip", "ssd_norm", "norm_mem",
         "norm_mix_post", "norm_mlp_pre", "norm_mlp_post")
WEIGHTS = ("norm_mix_pre", "w_in", "conv_w", "conv_b", "dt_bias", "a_log", "d_skip", "ssd_norm", "norm_mem",
           "w_mem_kv", "w_sb_out", "w_ssd_out", "w_mem_out", "w_o", "norm_mix_post", "norm_mlp_pre", "w_up",
           "w_down", "norm_mlp_post")
PK_ROWS = 184


def _pack(vecs):
    flat = jnp.concatenate([v.reshape(-1) for v in vecs])
    return jnp.pad(flat, (0, PK_ROWS * 128 - flat.shape[0])).reshape(PK_ROWS, 128)


def _unpack(pk, shapes):
    flat = pk.reshape(-1)
    out, off = [], 0
    for s in shapes:
        n = 1
        for d in s:
            n *= d
        out.append(flat[off:off + n].reshape(s))
        off += n
    return out


def _full_from_slabs(name, slabs):
    if name in COL_SHARDED:
        return slabs.transpose(1, 0, 2).reshape(slabs.shape[1], -1)
    return slabs.reshape(-1, slabs.shape[2])


def _slabs_from_full(name, g):
    if name in COL_SHARDED:
        return g.reshape(g.shape[0], N_SHARD, -1).transpose(1, 0, 2)
    return g.reshape(N_SHARD, -1, g.shape[1])


def kernel(x, mem, norm_mix_pre, w_in, conv_w, conv_b, dt_bias, a_log, d_skip, ssd_norm, norm_mem, w_mem_kv, w_sb_out, w_ssd_out, w_mem_out, w_o, norm_mix_post, norm_mlp_pre, w_up, w_down, norm_mlp_post, loss_target, m_norm_mix_pre, m_w_in, m_conv_w, m_conv_b, m_dt_bias, m_a_log, m_d_skip, m_ssd_norm, m_norm_mem, m_w_mem_kv, m_w_sb_out, m_w_ssd_out, m_w_mem_out, m_w_o, m_norm_mix_post, m_norm_mlp_pre, m_w_up, m_w_down, m_norm_mlp_post, v_norm_mix_pre, v_w_in, v_conv_w, v_conv_b, v_dt_bias, v_a_log, v_d_skip, v_ssd_norm, v_norm_mem, v_w_mem_kv, v_w_sb_out, v_w_ssd_out, v_w_mem_out, v_w_o, v_norm_mix_post, v_norm_mlp_pre, v_w_up, v_w_down, v_norm_mlp_post):
    env = dict(locals())
    w = {n: env[n] for n in WEIGHTS}
    mo = {n: env["m_" + n] for n in WEIGHTS}
    vo = {n: env["v_" + n] for n in WEIGHTS}
    shard = 2 * lax.axis_index("x") + lax.axis_index("y")

    first = _gather_two_level([w["w_in"][0].astype(bf16)], "gather_first")
    w_main, w_dt = _to_internal(_full_from_slabs("w_in", first[0]))
    wts = dict(w_main=w_main, w_dt=w_dt)
    ride_names = (("w_mem_kv",) + LATE[:4], LATE[4:5], LATE[5:])
    late_rides = tuple(_Ride([w[n][0].astype(bf16) for n in names] + ([w["conv_w"][0]] if i == 0 else []), "gather")
                       for i, names in enumerate(ride_names))

    def late_weights(i, lands):
        full = {n: _full_from_slabs(n, s) for n, s in zip(ride_names[i], lands)}
        if i == 0:
            full["conv_w"] = lands[-1].transpose(1, 0, 2).reshape(CONV_K, CONV_DIM)
        return full

    def rest_rides(g):
        slabs = [_slabs_from_full(n, g[n]).astype(bf16) for n in REST]
        return _Ride(slabs[5:], "scatter"), _Ride(slabs[:5], "scatter")

    core = lax.axis_index("c")
    half = D // 2

    def w_in_ride(g):
        ordered = _from_internal(g["w_main"], g["w_dt"])
        stack = jnp.stack([_w_in_slab(ordered, s, bf16) for s in range(N_SHARD)])
        keep = lax.dynamic_slice_in_dim(stack, core * half, half, axis=1)
        away = lax.dynamic_slice_in_dim(stack, (1 - core) * half, half, axis=1)
        (got,) = _swap_sibling([away], "w_in_halves_out")
        wide = lambda a: a.reshape(N_SHARD * half, -1)
        chip = _sum_parts(wide(keep), wide(got)[None], "sum_cores_w_in", bf16).reshape(N_SHARD, half, -1)
        own = lax.switch(shard, [functools.partial(_w_in_slab, ordered, s, f32) for s in range(N_SHARD)])
        own = lax.dynamic_slice_in_dim(own, core * half, half, axis=0)
        g["w_in_own"] = _sum_parts(own, lax.dynamic_index_in_dim(got, shard, 0, keepdims=True), "sum_cores_w_in_own")
        return _Ride([chip], "scatter")

    small = {n: w[n] for n in SMALL if n != "conv_w"}
    loss, grad_x, g = _local_step(x[0], mem[0], loss_target[0], wts, late_rides, late_weights, small,
                                  rest_rides, w_in_ride)
    out_g, out_d, out_m, out_v = {}, {}, {}, {}

    def apply(n, g_parts):
        res = _adamw(w[n][0], mo[n][0], vo[n][0], g_parts, name="adamw_" + n)
        out_g[n], out_d[n], out_m[n], out_v[n] = [r[None] for r in res]

    mine = _sum_parts(g["w_in_own"], g["w_in_lands"][0], name="sum_chips_w_in")
    (theirs,) = _swap_sibling([mine], "w_in_halves_back")
    g_w_in = lax.dynamic_update_slice_in_dim(jnp.zeros((D, D_IN // N_SHARD), f32), mine, core * half, axis=0)
    apply("w_in", [lax.dynamic_update_slice_in_dim(g_w_in, theirs, (1 - core) * half, axis=0)])

    packets = _exchange_packets(_pack([g[n] for n in SMALL] + [loss[:, :1]]))
    partial = []
    for n, r in zip(REST, g["rest_lands"]):
        own = lax.dynamic_index_in_dim(_slabs_from_full(n, g[n]), shard, 0, keepdims=False)
        partial.append(_sum_parts(own, r, name="sum_chips_" + n))
    other = _swap_sibling(partial, "swap_sibling")

    for n, p, q in zip(REST, partial, other):
        apply(n, [p, q])
    tot = _sum_parts(None, packets, name="sum_packets")
    shapes = [g[n].shape for n in SMALL] + [(1, 1)]
    sm = dict(zip(SMALL + ("loss",), _unpack(tot, shapes)))
    sm["conv_w"] = lax.dynamic_slice_in_dim(sm["conv_w"], shard * (CONV_DIM // N_SHARD), CONV_DIM // N_SHARD, axis=1)
    own_small = lambda d: _pack([d[n].reshape(sm[n].shape) for n in SMALL])
    res = _adamw(own_small(w), own_small(mo), own_small(vo), [own_small(sm)], name="adamw_small")
    own_shapes = [sm[n].shape for n in SMALL]
    for store, r in zip((out_g, out_d, out_m, out_v), res):
        for n, val in zip(SMALL, _unpack(r, own_shapes)):
            store[n] = val.reshape(w[n].shape)

    outs = [sm["loss"].reshape(()), grad_x[None]]
    for store in (out_g, out_d, out_m, out_v):
        outs += [store[n] for n in WEIGHTS]
    return tuple(outs)
```

```python
import functools

import jax
import jax.numpy as jnp
from jax import lax
from jax.experimental import pallas as pl
from jax.experimental.pallas import tpu as pltpu

f32 = jnp.float32
bf16 = jnp.bfloat16

D = 1024
EPS = 1e-6
SB_HD = 64
SSD_INNER = 2048
SSD_HEADS = 32
SSD_GROUPS = 4
SSD_N = 128
SSD_L = 128
CONV_K = 4
CONV_DIM = 3072
MEM_HEADS = 4
MEM_HD = 256
D_FF = 4096
D_IN = 12320
N_SHARD = 4
N_DEV = 8

P_QKV, P_XBC, P_GATE, P_MEMQ, P_Z, P_DT, P_TOT = 0, 3072, 6144, 9216, 10240, 12288, 12416
R_QKV, R_Z, R_XBC, R_DT, R_MEMQ, R_GATE = (0, 3072), (3072, 5120), (5120, 8192), (8192, 8224), (8224, 9248), (9248, 12320)

ADAM_LR = 0.001
ADAM_B1 = 0.9
ADAM_B2 = 0.999
ADAM_EPS = 1e-08
ADAM_WD = 0.01
ADAM_STEP = 10

VMEM_LIMIT = 56 * 1024 * 1024

NN = (((1,), (0,)), ((), ()))
NT = (((1,), (1,)), ((), ()))
TN = (((0,), (0,)), ((), ()))


def _dot(a, b, dims=NN):
    return lax.dot_general(a, b, dims, preferred_element_type=f32)


def _params(sem=None):
    return pltpu.CompilerParams(dimension_semantics=sem, vmem_limit_bytes=VMEM_LIMIT)


def _sigmoid(x):
    return 1.0 / (1.0 + jnp.exp(-x))


def _split2(x):
    hi = x.astype(bf16)
    lo = (x - hi.astype(f32)).astype(bf16)
    return hi, lo


def _split3(x):
    hi = x.astype(bf16)
    r = x - hi.astype(f32)
    mid = r.astype(bf16)
    lo = (r - mid.astype(f32)).astype(bf16)
    return hi, mid, lo


def _mm(a, b, mode, *, tm, tn, name, out_dtypes=(f32,), epi=None, extras=(), ride=None):
    M = a.shape[1] if mode == "tn" else a.shape[0]
    N = b.shape[0] if mode == "nt" else b.shape[1]
    tm, tn = min(tm, M), min(tn, N)
    if mode == "nn":
        (M, K), N = a.shape, b.shape[1]
        a_spec = pl.BlockSpec((tm, K), lambda i, j: (i, 0))
        b_spec = pl.BlockSpec((K, tn), lambda i, j: (0, j))
        dims = NN
    elif mode == "nt":
        (M, K), N = a.shape, b.shape[0]
        a_spec = pl.BlockSpec((tm, K), lambda i, j: (i, 0))
        b_spec = pl.BlockSpec((tn, K), lambda i, j: (j, 0))
        dims = NT
    else:
        (K, M), N = a.shape, b.shape[1]
        a_spec = pl.BlockSpec((K, tm), lambda i, j: (0, i))
        b_spec = pl.BlockSpec((K, tn), lambda i, j: (0, j))
        dims = TN
    assert M % tm == 0 and N % tn == 0, (name, M, N, tm, tn)
    n_ex, n_out = len(extras), len(out_dtypes)
    n_r = ride.n if ride else 0
    o_spec = pl.BlockSpec((tm, tn), lambda i, j: (i, j))
    grid = (M // tm, N // tn)

    def body(a_ref, b_ref, *rest):
        r_ins = rest[n_ex:n_ex + n_r]
        outs = rest[n_ex + n_r:n_ex + n_r + n_out]
        r_lnd, r_sems = rest[n_ex + n_r + n_out:n_ex + 2 * n_r + n_out], rest[n_ex + 2 * n_r + n_out:]
        i, j = pl.program_id(0), pl.program_id(1)
        if ride:
            pl.when((i == 0) & (j == 0))(lambda: ride.start(r_ins, r_lnd, r_sems))
        acc = _dot(a_ref[...].astype(bf16), b_ref[...].astype(bf16), dims)
        res = (acc,) if epi is None else epi(acc, *[e[...] for e in rest[:n_ex]])
        for o_ref, r in zip(outs, res):
            o_ref[...] = r.astype(o_ref.dtype)
        if ride:
            pl.when((i == grid[0] - 1) & (j == grid[1] - 1))(lambda: ride.finish(r_ins, r_lnd, r_sems))

    out = pl.pallas_call(
        body, name=name, grid=grid,
        in_specs=[a_spec, b_spec] + [o_spec] * n_ex + (ride.in_specs if ride else []),
        out_specs=[o_spec] * n_out + (ride.out_specs if ride else []),
        out_shape=[jax.ShapeDtypeStruct((M, N), dt) for dt in out_dtypes] + (ride.out_shape if ride else []),
        scratch_shapes=ride.scratch if ride else [],
        compiler_params=_params(("arbitrary", "arbitrary") if ride else ("parallel", "parallel")),
    )(a, b, *extras, *(ride.srcs if ride else []))
    if ride:
        return (out[0] if n_out == 1 else out[:n_out]), list(out[n_out:])
    return out[0] if n_out == 1 else out


def _mm_pieces_nt(pieces, b, add, *, tm, tn, name, ride):
    M, N = pieces[0].shape[0], b.shape[0]
    n_p, n_r = len(pieces), (ride.n if ride else 0)
    o_spec = pl.BlockSpec((tm, tn), lambda i, j: (i, j))
    grid = (M // tm, N // tn)

    def body(*refs):
        b_ref, add_ref = refs[n_p:n_p + 2]
        r_ins, o_ref = refs[n_p + 2:n_p + 2 + n_r], refs[n_p + 2 + n_r]
        r_lnd, r_sems = refs[n_p + 3 + n_r:n_p + 3 + 2 * n_r], refs[n_p + 3 + 2 * n_r:]
        i, j = pl.program_id(0), pl.program_id(1)
        if ride:
            pl.when((i == 0) & (j == 0))(lambda: ride.start(r_ins, r_lnd, r_sems))
        acc, off = add_ref[...], 0
        for r in refs[:n_p]:
            acc = acc + _dot(r[...], b_ref[:, off:off + r.shape[1]], NT)
            off += r.shape[1]
        o_ref[...] = acc
        if ride:
            pl.when((i == grid[0] - 1) & (j == grid[1] - 1))(lambda: ride.finish(r_ins, r_lnd, r_sems))

    out = pl.pallas_call(
        body, name=name, grid=grid,
        in_specs=[pl.BlockSpec((tm, p.shape[1]), lambda i, j: (i, 0)) for p in pieces]
        + [pl.BlockSpec((tn, b.shape[1]), lambda i, j: (j, 0)), o_spec] + (ride.in_specs if ride else []),
        out_specs=[o_spec] + (ride.out_specs if ride else []),
        out_shape=[jax.ShapeDtypeStruct((M, N), f32)] + (ride.out_shape if ride else []),
        scratch_shapes=ride.scratch if ride else [],
        compiler_params=_params(("arbitrary", "arbitrary")),
    )(*pieces, b, add, *(ride.srcs if ride else []))
    return out[0], list(out[1:])


def _rms_fwd(x, g, *, name, out_dtype, residual=None, tm=512):
    S, C = x.shape
    tm = min(tm, S)
    has_res = residual is not None

    def body(x_ref, g_ref, *rest):
        xv = x_ref[...]
        r = lax.rsqrt(jnp.mean(xv * xv, axis=1, keepdims=True) + EPS)
        y = xv * r * g_ref[...]
        if has_res:
            y = y + rest[0][...]
        rest[-1][...] = y.astype(out_dtype)

    row = pl.BlockSpec((tm, C), lambda i: (i, 0))
    vec = pl.BlockSpec((1, C), lambda i: (0, 0))
    args = (x, g) + ((residual,) if has_res else ())
    return pl.pallas_call(
        body, name=name, grid=(S // tm,),
        in_specs=[row, vec] + ([row] if has_res else []),
        out_specs=row, out_shape=jax.ShapeDtypeStruct((S, C), out_dtype),
        compiler_params=_params(("parallel",)),
    )(*args)


def _rms_bwd(x, dy, g, *, name, out_dtype, add=None, tm=512):
    S, C = x.shape
    tm = min(tm, S)
    has_add = add is not None

    def body(x_ref, dy_ref, g_ref, *rest):
        dx_ref, dg_ref = rest[-2], rest[-1]
        xv = x_ref[...]
        dyv = dy_ref[...].astype(f32)
        r = lax.rsqrt(jnp.mean(xv * xv, axis=1, keepdims=True) + EPS)
        xh = xv * r
        dxh = dyv * g_ref[...]
        dx = r * (dxh - xh * jnp.mean(dxh * xh, axis=1, keepdims=True))
        if has_add:
            dx = dx + rest[0][...]
        dx_ref[...] = dx.astype(out_dtype)

        @pl.when(pl.program_id(0) == 0)
        def _():
            dg_ref[...] = jnp.zeros_like(dg_ref)

        dg_ref[...] += jnp.sum(dyv * xh, axis=0, keepdims=True)

    row = pl.BlockSpec((tm, C), lambda i: (i, 0))
    vec = pl.BlockSpec((1, C), lambda i: (0, 0))
    args = (x, dy, g) + ((add,) if has_add else ())
    return pl.pallas_call(
        body, name=name, grid=(S // tm,),
        in_specs=[row, row, vec] + ([row] if has_add else []),
        out_specs=[row, vec],
        out_shape=[jax.ShapeDtypeStruct((S, C), out_dtype), jax.ShapeDtypeStruct((1, C), f32)],
        compiler_params=_params(("arbitrary",)),
    )(*args)


SB_T = 128
SB_SPENT = -120.0
SB_QB = 4
SB_TAIL = 3
SB_GROUPS = (4, 2, 1)
SB_GROUPS_BWD = (4, 2, 1)


def _sb_masks():
    lane = lax.broadcasted_iota(jnp.int32, (1, 128), 1)
    m_a = (lane < SB_HD).astype(f32)
    return m_a, 1.0 - m_a


def _chunks(a, n):
    return [a[:, u * SB_T:(u + 1) * SB_T] for u in range(n)]


def _cat(parts, axis):
    return parts[0] if len(parts) == 1 else jnp.concatenate(parts, axis=axis)


def _mask_last(a, n, mask):
    if mask is None:
        return a
    parts = _chunks(a, n)
    return _cat(parts[:-1] + [jnp.where(mask, parts[-1], 0.0)], 1)


def _sb_logits(z, n, mask):
    l1p = jnp.log(1.0 + jnp.exp(-jnp.abs(z)))
    lb = jnp.minimum(z, 0.0) - l1p
    return lb, _mask_last(lb - z, n, mask)


def _by_count(i, most, fn):
    return lax.switch(jnp.minimum(i, most - 1), [functools.partial(fn, n) for n in range(1, most + 1)])


def _chunk_matmul(parts_list, u_mat):
    out = _dot(_cat(parts_list, 0), u_mat)
    return [out[u * SB_T:(u + 1) * SB_T] for u in range(len(parts_list))]


def _chunk_cumsum(lk, n, u_mat):
    hi = lk.astype(bf16)
    lo = (lk - hi.astype(f32)).astype(bf16)
    out = _chunk_matmul(_chunks(hi, n) + _chunks(lo, n), u_mat)
    return [out[u] + out[n + u] for u in range(n)]


def _sb_fwd(proj, S, ride=None):
    nq = S // SB_T
    n_pairs = D // 128
    scale = SB_HD ** -0.5
    n_r = ride.n if ride else 0

    def body(q_ref, k_ref, v_ref, *rest):
        o_ref, t_ref = rest[n_r:n_r + 2]
        step_i = pl.program_id(1)
        if ride:
            pl.when((pl.program_id(0) == 0) & (step_i == 0))(
                lambda: ride.start(rest[:n_r], rest[n_r + 2:2 * n_r + 2], rest[2 * n_r + 2:]))
        m_a, m_b = _sb_masks()
        r_i = lax.broadcasted_iota(jnp.int32, (SB_T, SB_T), 0)
        c_i = lax.broadcasted_iota(jnp.int32, (SB_T, SB_T), 1)
        u_mat = (r_i > c_i).astype(bf16)
        causal = c_i < r_i
        q_all = q_ref[...] * scale
        q_hs = [((q * m_a).astype(bf16), (q * m_b).astype(bf16))
                for q in (q_all[b * SB_T:(b + 1) * SB_T] for b in range(SB_QB))]

        def group(q_h, j_lo, n, carry, mask):
            acc, c_a, c_b = carry
            rows = pl.ds(pl.multiple_of(j_lo * SB_T, SB_T), n * SB_T)
            k = k_ref[rows, :].astype(bf16)
            v = v_ref[rows, :]
            zs = [_dot(q_b, k, NT) for q_b in q_h]
            lbk = [_sb_logits(z, n, mask) for z in zs]
            parts = [_chunk_cumsum(lk, n, u_mat) for _, lk in lbk]
            ws, cs = [], []
            for (lb, lk), part, c in zip(lbk, parts, (c_a, c_b)):
                lb_c, lk_c = _chunks(lb, n), _chunks(lk, n)
                w_c = [None] * n
                for u in reversed(range(n)):
                    w_c[u] = jnp.exp(lb_c[u] + c + part[u])
                    c = c + jnp.sum(lk_c[u], axis=1, keepdims=True)
                ws.append(_mask_last(_cat(w_c, 1), n, mask).astype(bf16))
                cs.append(c)
            for w, m in zip(ws, (m_a, m_b)):
                acc = acc + _dot(w, (v * m).astype(bf16))
            return acc, cs[0], cs[1]

        zero_c = jnp.zeros((SB_T, 1), f32)
        init = (jnp.zeros((SB_T, 128), f32), zero_c, zero_c)
        blocks = [(step_i * SB_QB + b, q_hs[b]) for b in range(SB_QB)]

        def whole_tails():
            return tuple(group(q_h, i - SB_TAIL + 1, SB_TAIL, init, causal) for i, q_h in blocks)

        def short_tails():
            return tuple(_by_count(i, SB_TAIL, functools.partial(
                lambda n, i, q_h: group(q_h, i - n + 1, n, init, causal), i=i, q_h=q_h)) for i, q_h in blocks)

        carries = lax.cond(step_i * SB_QB >= SB_TAIL - 1, whole_tails, short_tails)

        def spent(cr):
            return (jnp.max(jnp.maximum(cr[1], cr[2])) < SB_SPENT).astype(jnp.int32)

        lane = lax.broadcasted_iota(jnp.int32, (1, 128), 1)
        for b, ((i, q_h), carry) in enumerate(zip(blocks, carries)):
            state = (i - jnp.minimum(i, SB_TAIL - 1), spent(carry), carry)
            for n in SB_GROUPS:
                def step(st, n=n, q_h=q_h):
                    left, _, cr = st
                    cr = group(q_h, left - n, n, cr, None)
                    return left - n, spent(cr), cr

                state = lax.while_loop(lambda st, n=n: (st[0] >= n) & (st[1] == 0), step, state)
            left, _, carry = state
            rows = slice(b * SB_T, (b + 1) * SB_T)
            o_ref[rows, :] = carry[0]
            t_ref[rows, :] = (jnp.where(lane == 0, carry[1], 0.0) + jnp.where(lane == SB_HD, carry[2], 0.0)
                              + jnp.where(lane == 1, left.astype(f32), 0.0))
        if ride:
            pl.when((pl.program_id(0) == n_pairs - 1) & (step_i == nq // SB_QB - 1))(
                lambda: ride.finish(rest[:n_r], rest[n_r + 2:2 * n_r + 2], rest[2 * n_r + 2:]))

    qs = pl.BlockSpec((SB_QB * SB_T, 128), lambda h, i: (i, h))
    out = pl.pallas_call(
        body, name="sb_fwd", grid=(n_pairs, nq // SB_QB),
        in_specs=[qs,
                  pl.BlockSpec((S, 128), lambda h, i: (0, n_pairs + h)),
                  pl.BlockSpec((S, 128), lambda h, i: (0, 2 * n_pairs + h))] + (ride.in_specs if ride else []),
        out_specs=[qs, qs] + (ride.out_specs if ride else []),
        out_shape=[jax.ShapeDtypeStruct((S, D), f32)] * 2 + (ride.out_shape if ride else []),
        scratch_shapes=ride.scratch if ride else [],
        compiler_params=_params(("arbitrary", "arbitrary")),
    )(proj, proj, proj, *(ride.srcs if ride else []))
    return out[0], out[1], list(out[2:])


def _sb_bwd(proj, tot_lk, do, S, ride=None):
    nq = S // SB_T
    n_pairs = D // 128
    scale = SB_HD ** -0.5
    n_r = ride.n if ride else 0

    def body(q_ref, k_ref, v_ref, t_ref, do_ref, *rest):
        dq_ref, dk_ref, dv_ref = rest[n_r:n_r + 3]
        dk_acc, dv_acc = rest[2 * n_r + 3:2 * n_r + 5]
        r_ins, r_lnd, r_sems = rest[:n_r], rest[n_r + 3:2 * n_r + 3], rest[2 * n_r + 5:]
        step_i = pl.program_id(1)
        if ride:
            pl.when((pl.program_id(0) == 0) & (step_i == 0))(lambda: ride.start(r_ins, r_lnd, r_sems))
        m_a, m_b = _sb_masks()
        r_i = lax.broadcasted_iota(jnp.int32, (SB_T, SB_T), 0)
        c_i = lax.broadcasted_iota(jnp.int32, (SB_T, SB_T), 1)
        u_inc = (r_i <= c_i).astype(bf16)
        u_exc = (r_i < c_i).astype(bf16)
        causal = c_i < r_i

        @pl.when(step_i == 0)
        def _():
            dk_acc[...] = jnp.zeros_like(dk_acc)
            dv_acc[...] = jnp.zeros_like(dv_acc)

        lane = lax.broadcasted_iota(jnp.int32, (1, 128), 1)
        blocks = []
        for b in range(SB_QB):
            rows_b = slice(b * SB_T, (b + 1) * SB_T)
            i = step_i * SB_QB + b
            q = q_ref[rows_b, :] * scale
            dov = do_ref[rows_b, :]
            tv = t_ref[rows_b, :]
            heads = []
            for m, first in ((m_a, 0), (m_b, SB_HD)):
                tot = jnp.sum(jnp.where(lane == first, tv, 0.0), axis=1, keepdims=True)
                heads.append(((q * m).astype(bf16), (dov * m).astype(bf16), tot, m))
            lowest = jnp.clip(jnp.max(jnp.where(lane == 1, tv, 0.0)).astype(jnp.int32), 0, i)
            blocks.append((i, heads, lowest))

        def group(heads, j_lo, n, carry, mask):
            dq_acc, cp_a, cp_b, ce_a, ce_b = carry
            rows = pl.ds(pl.multiple_of(j_lo * SB_T, SB_T), n * SB_T)
            k_f = k_ref[rows, :]
            k = k_f.astype(bf16)
            v = v_ref[rows, :].astype(bf16)
            zs = [_dot(h[0], k, NT) for h in heads]
            dws = [_dot(h[1], v, NT) for h in heads]
            lbk = [_sb_logits(z, n, mask) for z in zs]
            parts = [_chunk_cumsum(lk, n, u_inc) for _, lk in lbk]
            ws, es, cps = [], [], []
            for (lb, lk), part, dw, h, cp in zip(lbk, parts, dws, heads, (cp_a, cp_b)):
                lb_c, lk_c = _chunks(lb, n), _chunks(lk, n)
                w_c = []
                for u in range(n):
                    w_c.append(jnp.exp(lb_c[u] + (h[2] - cp) - part[u]))
                    cp = cp + jnp.sum(lk_c[u], axis=1, keepdims=True)
                w = _mask_last(_cat(w_c, 1), n, mask)
                ws.append(w)
                es.append(dw * w)
                cps.append(cp)
            e_parts = [_chunk_matmul(_chunks(e.astype(bf16), n), u_exc) for e in es]
            dzs, ces = [], []
            for (lb, _), e, e_part, ce in zip(lbk, es, e_parts, (ce_a, ce_b)):
                e_c = _chunks(e, n)
                big_c = []
                for u in range(n):
                    big_c.append(ce + e_part[u])
                    ce = ce + jnp.sum(e_c[u], axis=1, keepdims=True)
                sig = jnp.exp(lb)
                dz = _mask_last(e * (1.0 - sig) - _cat(big_c, 1) * sig, n, mask)
                dzs.append(dz.astype(bf16))
                ces.append(ce)
            dk_t = jnp.zeros((n * SB_T, 128), f32)
            dv_t = jnp.zeros((n * SB_T, 128), f32)
            for dz_b, w, h in zip(dzs, ws, heads):
                dq_acc = dq_acc + _dot(dz_b, (k_f * h[3]).astype(bf16))
                dk_t = dk_t + _dot(dz_b, h[0], TN)
                dv_t = dv_t + _dot(w.astype(bf16), h[1], TN)
            dk_acc[rows, :] += dk_t
            dv_acc[rows, :] += dv_t
            return dq_acc, cps[0], cps[1], ces[0], ces[1]

        zc = jnp.zeros((SB_T, 1), f32)
        carries = []
        for i, heads, lowest in blocks:
            carry = (jnp.zeros((SB_T, 128), f32), zc, zc, zc, zc)
            done = lowest
            tail_lo = i - jnp.minimum(i, SB_TAIL - 1)
            for n in SB_GROUPS_BWD:
                trips = (tail_lo - done) // n
                carry = lax.fori_loop(
                    0, trips, functools.partial(
                        lambda gi, cr, n, done, heads: group(heads, done + gi * n, n, cr, None),
                        n=n, done=done, heads=heads),
                    carry)
                done = done + trips * n
            carries.append(carry)

        def whole_tails():
            return tuple(group(heads, i - SB_TAIL + 1, SB_TAIL, cr, causal)
                         for (i, heads, _), cr in zip(blocks, carries))

        def short_tails():
            return tuple(_by_count(i, SB_TAIL, functools.partial(
                lambda n, i, heads, cr: group(heads, i - n + 1, n, cr, causal), i=i, heads=heads, cr=cr))
                for (i, heads, _), cr in zip(blocks, carries))

        carries = lax.cond(step_i * SB_QB >= SB_TAIL - 1, whole_tails, short_tails)
        for b, carry in enumerate(carries):
            dq_ref[b * SB_T:(b + 1) * SB_T, :] = (carry[0] * scale).astype(bf16)

        @pl.when(step_i == nq // SB_QB - 1)
        def _():
            dk_ref[...] = dk_acc[...].astype(bf16)
            dv_ref[...] = dv_acc[...].astype(bf16)

        if ride:
            pl.when((pl.program_id(0) == n_pairs - 1) & (step_i == nq // SB_QB - 1))(
                lambda: ride.finish(r_ins, r_lnd, r_sems))

    qs = pl.BlockSpec((SB_QB * SB_T, 128), lambda h, i: (i, h))
    full = pl.BlockSpec((S, 128), lambda h, i: (0, h))
    out = pl.pallas_call(
        body, name="sb_bwd", grid=(n_pairs, nq // SB_QB),
        in_specs=[qs,
                  pl.BlockSpec((S, 128), lambda h, i: (0, n_pairs + h)),
                  pl.BlockSpec((S, 128), lambda h, i: (0, 2 * n_pairs + h)),
                  qs, qs] + (ride.in_specs if ride else []),
        out_specs=[qs, full, full] + (ride.out_specs if ride else []),
        out_shape=[jax.ShapeDtypeStruct((S, D), bf16)] * 3 + (ride.out_shape if ride else []),
        scratch_shapes=[pltpu.VMEM((S, 128), f32), pltpu.VMEM((S, 128), f32)] + (ride.scratch if ride else []),
        compiler_params=_params(("arbitrary", "arbitrary")),
    )(proj, proj, proj, tot_lk, do, *(ride.srcs if ride else []))
    return out[0], out[1], out[2], list(out[3:])


CONV_CB = 256
HALO = 8


def _conv_fwd(proj, conv_w, conv_b, S):
    tr = min(512, S)

    def body(x_ref, w_ref, b_ref, xc_ref, xbc_ref):
        w = w_ref[...]
        for t in range(S // tr):
            cur = x_ref[t * tr:(t + 1) * tr, :]
            halo = x_ref[t * tr - HALO:t * tr, :] if t else jnp.zeros((HALO, CONV_CB), f32)
            win = jnp.concatenate([halo, cur], axis=0)
            acc = b_ref[...] + w[CONV_K - 1:CONV_K, :] * cur
            for k in range(CONV_K - 1):
                acc = acc + w[k:k + 1, :] * pltpu.roll(win, CONV_K - 1 - k, 0)[HALO:, :]
            xc_ref[t * tr:(t + 1) * tr, :] = acc
            xbc_ref[t * tr:(t + 1) * tr, :] = acc * _sigmoid(acc)

    col = pl.BlockSpec((S, CONV_CB), lambda c: (0, c))
    return pl.pallas_call(
        body, name="conv_fwd", grid=(CONV_DIM // CONV_CB,),
        in_specs=[pl.BlockSpec((S, CONV_CB), lambda c: (0, P_XBC // CONV_CB + c)),
                  pl.BlockSpec((CONV_K, CONV_CB), lambda c: (0, c)),
                  pl.BlockSpec((1, CONV_CB), lambda c: (0, c))],
        out_specs=[col, col], out_shape=[jax.ShapeDtypeStruct((S, CONV_DIM), f32)] * 2,
        compiler_params=_params(("parallel",)),
    )(proj, conv_w, conv_b)


def _conv_bwd(proj, xc, dxbc, conv_w, S):
    tr = min(512, S)

    def body(x_ref, xc_ref, dy_ref, w_ref, dx_ref, dw_ref, db_ref, dxc_s):
        w = w_ref[...]
        xcv = xc_ref[...]
        sg = _sigmoid(xcv)
        dxc_s[0:S, :] = dy_ref[...] * (sg * (1.0 + xcv * (1.0 - sg)))
        dxc_s[S:S + HALO, :] = jnp.zeros((HALO, CONV_CB), f32)
        dws = [jnp.zeros((1, CONV_CB), f32) for _ in range(CONV_K)]
        db = jnp.zeros((1, CONV_CB), f32)
        for t in range(S // tr):
            cur = x_ref[t * tr:(t + 1) * tr, :]
            halo = x_ref[t * tr - HALO:t * tr, :] if t else jnp.zeros((HALO, CONV_CB), f32)
            win = jnp.concatenate([halo, cur], axis=0)
            dwin = dxc_s[t * tr:(t + 1) * tr + HALO, :]
            dcur = dwin[0:tr, :]
            db = db + jnp.sum(dcur, axis=0, keepdims=True)
            dws[CONV_K - 1] = dws[CONV_K - 1] + jnp.sum(dcur * cur, axis=0, keepdims=True)
            dx = w[CONV_K - 1:CONV_K, :] * dcur
            for k in range(CONV_K - 1):
                sh = CONV_K - 1 - k
                dws[k] = dws[k] + jnp.sum(dcur * pltpu.roll(win, sh, 0)[HALO:, :], axis=0, keepdims=True)
                dx = dx + w[k:k + 1, :] * pltpu.roll(dwin, tr + HALO - sh, 0)[0:tr, :]
            dx_ref[t * tr:(t + 1) * tr, :] = dx.astype(bf16)
        dw_ref[...] = jnp.concatenate(dws + [jnp.zeros((8 - CONV_K, CONV_CB), f32)], axis=0)
        db_ref[...] = db

    col = pl.BlockSpec((S, CONV_CB), lambda c: (0, c))
    return pl.pallas_call(
        body, name="conv_bwd", grid=(CONV_DIM // CONV_CB,),
        in_specs=[pl.BlockSpec((S, CONV_CB), lambda c: (0, P_XBC // CONV_CB + c)), col, col,
                  pl.BlockSpec((CONV_K, CONV_CB), lambda c: (0, c))],
        out_specs=[col, pl.BlockSpec((8, CONV_CB), lambda c: (0, c)), pl.BlockSpec((1, CONV_CB), lambda c: (0, c))],
        out_shape=[jax.ShapeDtypeStruct((S, CONV_DIM), bf16), jax.ShapeDtypeStruct((8, CONV_DIM), f32),
                   jax.ShapeDtypeStruct((1, CONV_DIM), f32)],
        scratch_shapes=[pltpu.VMEM((S + HALO, CONV_CB), f32)],
        compiler_params=_params(("parallel",)),
    )(proj, xc, dxbc, conv_w)


N_PAIR = SSD_HEADS // 2
NEG = -1e30


def _softplus(x):
    return jnp.maximum(x, 0.0) + jnp.log(1.0 + jnp.exp(-jnp.abs(x)))


def _ssd_common(dtr, dtb, alog):
    L = SSD_L
    r_i = lax.broadcasted_iota(jnp.int32, (L, L), 0)
    c_i = lax.broadcasted_iota(jnp.int32, (L, L), 1)
    dt = _softplus(dtr + dtb)
    a = -jnp.exp(alog)
    da = dt * a
    lower = (r_i >= c_i).astype(bf16)
    upper = (r_i <= c_i).astype(bf16)
    parts = _split3(da)
    a_cs = sum(_dot(lower, p) for p in parts)
    a_cs_t = sum(_dot(p, upper, TN) for p in parts)
    return dt, a, a_cs, a_cs_t, r_i >= c_i


def _pair_vec(lane, v, h):
    return jnp.where(lane < SB_HD, v[:, h:h + 1], v[:, h + 1:h + 2])


def _decay_mat(a_cs, a_cs_t, h, tril):
    return jnp.exp(jnp.where(tril, a_cs[:, h:h + 1] - a_cs_t[h:h + 1, :], NEG))


def _ssd_fwd(xbc, proj, pdt, dt_bias_p, a_log_p, d_skip_c, ssd_norm, S, ride=None):
    L = SSD_L
    nc = S // L
    n_r = ride.n if ride else 0

    def body(xbc_ref, dt_ref, z_ref, dtb_ref, alog_ref, dsk_ref, gn_ref, *rest):
        y_ref, yn_ref, hp_ref = rest[n_r:n_r + 3]
        state = rest[2 * n_r + 3]
        r_ins, r_lnd, r_sems = rest[:n_r], rest[n_r + 3:2 * n_r + 3], rest[2 * n_r + 4:]
        c = pl.program_id(0)
        if ride:
            pl.when(c == 0)(lambda: ride.start(r_ins, r_lnd, r_sems))

        @pl.when(c == 0)
        def _():
            state[...] = jnp.zeros_like(state)

        hp_ref[0] = state[...]
        lane = lax.broadcasted_iota(jnp.int32, (1, 128), 1)
        row128 = lax.broadcasted_iota(jnp.int32, (128, 1), 0)
        m_a, m_b = _sb_masks()
        dt, a, a_cs, a_cs_t, tril = _ssd_common(dt_ref[...], dtb_ref[...], alog_ref[...])
        a_last = a_cs[L - 1:L, :]
        for g in range(SSD_GROUPS):
            b_g = xbc_ref[:, SSD_INNER + g * SSD_N:SSD_INNER + (g + 1) * SSD_N].astype(bf16)
            c_g = xbc_ref[:, SSD_INNER + (SSD_GROUPS + g) * SSD_N:SSD_INNER + (SSD_GROUPS + g + 1) * SSD_N].astype(bf16)
            cb = _dot(c_g, b_g, NT)
            for pr in range(4):
                h = 8 * g + 2 * pr
                pi = h // 2
                cols = slice(pi * 128, (pi + 1) * 128)
                xs = xbc_ref[:, cols]
                x = xs * _pair_vec(lane, dt, h)
                acs = _pair_vec(lane, a_cs, h)
                al = _pair_vec(lane, a_last, h)
                w_a = (cb * _decay_mat(a_cs, a_cs_t, h, tril)).astype(bf16)
                w_b = (cb * _decay_mat(a_cs, a_cs_t, h + 1, tril)).astype(bf16)
                yd = _dot(w_a, (x * m_a).astype(bf16)) + _dot(w_b, (x * m_b).astype(bf16))
                hp = state[pi]
                yo = _dot(c_g, hp.astype(bf16), NT) * jnp.exp(acs)
                y_ref[:, cols] = yd + yo + dsk_ref[:, cols] * xs
                dec = jnp.exp(jnp.where(row128 < SB_HD, a_last[:, h:h + 1], a_last[:, h + 1:h + 2]))
                state[pi] = hp * dec + _dot((x * jnp.exp(al - acs)).astype(bf16), b_g, TN)
        zz = z_ref[...]
        y2 = y_ref[...] * (zz * _sigmoid(zz))
        gw = SSD_INNER // SSD_GROUPS
        for g in range(SSD_GROUPS):
            yg = y2[:, g * gw:(g + 1) * gw]
            rg = lax.rsqrt(jnp.mean(yg * yg, axis=1, keepdims=True) + EPS)
            yn_ref[:, g * gw:(g + 1) * gw] = (yg * rg * gn_ref[:, g * gw:(g + 1) * gw]).astype(bf16)
        if ride:
            pl.when(c == nc - 1)(lambda: ride.finish(r_ins, r_lnd, r_sems))

    vec128 = pl.BlockSpec((1, 128), lambda c: (0, 0))
    vecin = pl.BlockSpec((1, SSD_INNER), lambda c: (0, 0))
    rows = pl.BlockSpec((L, SSD_INNER), lambda c: (c, 0))
    out = pl.pallas_call(
        body, name="ssd_fwd", grid=(nc,),
        in_specs=[pl.BlockSpec((L, CONV_DIM), lambda c: (c, 0)),
                  pl.BlockSpec((L, 128), lambda c: (c, 0)),
                  pl.BlockSpec((L, SSD_INNER), lambda c: (c, P_Z // SSD_INNER)),
                  vec128, vec128, vecin, vecin] + (ride.in_specs if ride else []),
        out_specs=[rows, rows, pl.BlockSpec((1, N_PAIR, 128, SSD_N), lambda c: (c, 0, 0, 0))]
        + (ride.out_specs if ride else []),
        out_shape=[jax.ShapeDtypeStruct((S, SSD_INNER), f32), jax.ShapeDtypeStruct((S, SSD_INNER), bf16),
                   jax.ShapeDtypeStruct((nc, N_PAIR, 128, SSD_N), f32)] + (ride.out_shape if ride else []),
        scratch_shapes=[pltpu.VMEM((N_PAIR, 128, SSD_N), f32)] + (ride.scratch if ride else []),
        compiler_params=_params(("arbitrary",)),
    )(xbc, pdt, proj, dt_bias_p, a_log_p, d_skip_c, ssd_norm, *(ride.srcs if ride else []))
    return out[0], out[1], out[2], list(out[3:])


def _sum_all(v):
    return jnp.sum(jnp.sum(v, axis=1, keepdims=True), axis=0, keepdims=True)


def _ssd_bwd(dyn, y, xbc, proj, pdt, hprev, dt_bias_p, a_log_p, d_skip_c, ssd_norm, S, ride=None):
    L = SSD_L
    nc = S // L
    n_r = ride.n if ride else 0

    col = lax.broadcasted_iota(jnp.int32, (2 * SSD_INNER, 128), 0)
    head = lax.broadcasted_iota(jnp.int32, (2 * SSD_INNER, 128), 1)
    sel_pair = (col[:SSD_INNER] // SB_HD == head[:SSD_INNER]).astype(bf16)
    sel_head = (col // 128 == head).astype(bf16)

    def body(*refs):
        (dyn_ref, y_ref, xbc_ref, dt_ref, z_ref, hp_ref, dtb_ref, alog_ref, dsk_ref, gn_ref,
         selp_ref, selh_ref) = refs[:12]
        dz_ref, dxbc_ref, ddt_ref, dgn_ref, dsk_out, dalog_ref, ddtb_ref = refs[12 + n_r:19 + n_r]
        dstate, dy_s, st_a, st_q, st_d, st_x, dat = refs[19 + 2 * n_r:26 + 2 * n_r]
        r_ins, r_lnd, r_sems = refs[12:12 + n_r], refs[19 + n_r:19 + 2 * n_r], refs[26 + 2 * n_r:]
        c = pl.program_id(0)
        if ride:
            pl.when(c == 0)(lambda: ride.start(r_ins, r_lnd, r_sems))

        @pl.when(c == 0)
        def _():
            dat[...] = jnp.zeros_like(dat)
            dstate[...] = jnp.zeros_like(dstate)
            dgn_ref[...] = jnp.zeros_like(dgn_ref)
            dsk_out[...] = jnp.zeros_like(dsk_out)
            dalog_ref[...] = jnp.zeros_like(dalog_ref)
            ddtb_ref[...] = jnp.zeros_like(ddtb_ref)

        lane = lax.broadcasted_iota(jnp.int32, (1, 128), 1)
        row128 = lax.broadcasted_iota(jnp.int32, (128, 1), 0)
        rowl = lax.broadcasted_iota(jnp.int32, (L, 1), 0)
        m_a, m_b = _sb_masks()
        dtr = dt_ref[...]
        dt, a, a_cs, a_cs_t, tril = _ssd_common(dtr, dtb_ref[...], alog_ref[...])
        a_last = a_cs[L - 1:L, :]

        zz = z_ref[...]
        sg = _sigmoid(zz)
        silu = zz * sg
        yv = y_ref[...]
        y2 = yv * silu
        gw = SSD_INNER // SSD_GROUPS
        for g in range(SSD_GROUPS):
            sl = slice(g * gw, (g + 1) * gw)
            yg = y2[:, sl]
            rg = lax.rsqrt(jnp.mean(yg * yg, axis=1, keepdims=True) + EPS)
            yh = yg * rg
            dyn_g = dyn_ref[:, sl]
            dgn_ref[:, sl] += jnp.sum(dyn_g * yh, axis=0, keepdims=True)
            dyh = dyn_g * gn_ref[:, sl]
            dy2 = rg * (dyh - yh * jnp.mean(dyh * yh, axis=1, keepdims=True))
            dy_s[:, sl] = dy2 * silu[:, sl]
            dz_ref[:, sl] = (dy2 * yv[:, sl] * (sg[:, sl] * (1.0 + zz[:, sl] * (1.0 - sg[:, sl])))).astype(bf16)

        last_row = jnp.zeros((1, 128), f32)
        dsk_acc = jnp.zeros((1, 128), f32)
        for g in range(SSD_GROUPS):
            bsl = slice(SSD_INNER + g * SSD_N, SSD_INNER + (g + 1) * SSD_N)
            csl = slice(SSD_INNER + (SSD_GROUPS + g) * SSD_N, SSD_INNER + (SSD_GROUPS + g + 1) * SSD_N)
            b_g = xbc_ref[:, bsl].astype(bf16)
            c_g = xbc_ref[:, csl].astype(bf16)
            cb = _dot(c_g, b_g, NT)
            dcb = jnp.zeros((L, L), f32)
            dc_g = jnp.zeros((L, SSD_N), f32)
            db_g = jnp.zeros((L, SSD_N), f32)
            for pr in range(4):
                h = 8 * g + 2 * pr
                pi = h // 2
                cols = slice(pi * 128, (pi + 1) * 128)
                xs = xbc_ref[:, cols]
                dt_p = _pair_vec(lane, dt, h)
                x = xs * dt_p
                acs = _pair_vec(lane, a_cs, h)
                al = _pair_vec(lane, a_last, h)
                e_a = jnp.exp(acs)
                dte = jnp.exp(al - acs)
                m_mat_a = _decay_mat(a_cs, a_cs_t, h, tril)
                m_mat_b = _decay_mat(a_cs, a_cs_t, h + 1, tril)
                dyp = dy_s[:, cols]
                dsk = dsk_ref[:, cols]
                d_hn = dstate[pi]
                hp = hp_ref[0, pi]
                dy_a = (dyp * m_a).astype(bf16)
                dy_b = (dyp * m_b).astype(bf16)
                x_b = x.astype(bf16)
                gm_a = _dot(dy_a, x_b, NT) * m_mat_a
                gm_b = _dot(dy_b, x_b, NT) * m_mat_b
                dcb = dcb + gm_a + gm_b
                dx_d = _dot((cb * m_mat_a).astype(bf16), dy_a, TN) + _dot((cb * m_mat_b).astype(bf16), dy_b, TN)
                dx_s = _dot(b_g, d_hn.astype(bf16), NT) * dte
                dx = dx_d + dx_s
                dxbc_ref[:, cols] = dx * dt_p + dsk * dyp
                xdxs = x * dx_s
                st_x[:, cols] = xdxs
                st_a[:, cols] = dyp * (_dot(c_g, hp.astype(bf16), NT) * e_a) - xdxs
                st_d[:, cols] = dx * xs
                hh = d_hn * hp
                dsk_row = jnp.sum(dyp * xs, axis=0, keepdims=True)
                dec = jnp.exp(jnp.where(row128 < SB_HD, a_last[:, h:h + 1], a_last[:, h + 1:h + 2]))
                for hd, m, gm in ((h, m_a, gm_a), (h + 1, m_b, gm_b)):
                    half = slice(0, SB_HD) if hd == h else slice(SB_HD, 128)
                    qm = gm * cb
                    st_q[:, hd * 128:(hd + 1) * 128] = qm
                    dat[hd:hd + 1, :] = jnp.sum(qm, axis=0, keepdims=True)
                    hh_sum = jnp.sum(jnp.sum(hh[half, :], axis=0, keepdims=True), axis=1, keepdims=True)
                    last_row = jnp.where(lane == hd, jnp.exp(a_last[:, hd:hd + 1]) * hh_sum, last_row)
                    dsk_acc = jnp.where(lane == hd, jnp.sum(dsk_row * m, axis=1, keepdims=True), dsk_acc)
                dye = (dyp * e_a).astype(bf16)
                dc_g = dc_g + _dot(dye, hp.astype(bf16))
                db_g = db_g + _dot((x * dte).astype(bf16), d_hn.astype(bf16))
                dstate[pi] = dec * d_hn + _dot(dye, c_g, TN)
            dcb_b = dcb.astype(bf16)
            dxbc_ref[:, csl] = dc_g + _dot(dcb_b, b_g)
            dxbc_ref[:, bsl] = db_g + _dot(dcb_b, c_g, TN)

        r_i = lax.broadcasted_iota(jnp.int32, (L, L), 0)
        c_i = lax.broadcasted_iota(jnp.int32, (L, L), 1)
        rev = (r_i <= c_i).astype(bf16)

        def head_sums(st, sel, split=_split2):
            return sum(_dot(p, sel[...]) for p in split(st[...]))

        last_row = last_row + jnp.sum(head_sums(st_x, selp_ref), axis=0, keepdims=True)
        d_acs = (head_sums(st_a, selp_ref) + head_sums(st_q, selh_ref, _split3)
                 + jnp.where(rowl == L - 1, last_row, 0.0))
        ddt_x = head_sums(st_d, selp_ref)
        dda = sum(_dot(rev, p) for p in _split3(d_acs)) - sum(_dot(rev, p, NT) for p in _split3(dat[...]))
        ddt = ddt_x + dda * a
        dalog_ref[...] += jnp.sum(dda * dt, axis=0, keepdims=True) * a
        ddtr = jnp.where(lane < SSD_HEADS, ddt * _sigmoid(dtr + dtb_ref[...]), 0.0)
        ddt_ref[...] = ddtr.astype(bf16)
        ddtb_ref[...] += jnp.sum(ddtr, axis=0, keepdims=True)
        dsk_out[...] += dsk_acc
        if ride:
            pl.when(c == nc - 1)(lambda: ride.finish(r_ins, r_lnd, r_sems))

    rv = lambda c: nc - 1 - c
    vec128 = pl.BlockSpec((1, 128), lambda c: (0, 0))
    vecin = pl.BlockSpec((1, SSD_INNER), lambda c: (0, 0))
    rows = pl.BlockSpec((L, SSD_INNER), lambda c: (rv(c), 0))
    return pl.pallas_call(
        body, name="ssd_bwd", grid=(nc,),
        in_specs=[rows, rows,
                  pl.BlockSpec((L, CONV_DIM), lambda c: (rv(c), 0)),
                  pl.BlockSpec((L, 128), lambda c: (rv(c), 0)),
                  pl.BlockSpec((L, SSD_INNER), lambda c: (rv(c), P_Z // SSD_INNER)),
                  pl.BlockSpec((1, N_PAIR, 128, SSD_N), lambda c: (rv(c), 0, 0, 0)),
                  vec128, vec128, vecin, vecin,
                  pl.BlockSpec((SSD_INNER, 128), lambda c: (0, 0)),
                  pl.BlockSpec((2 * SSD_INNER, 128), lambda c: (0, 0))] + (ride.in_specs if ride else []),
        out_specs=[rows, pl.BlockSpec((L, CONV_DIM), lambda c: (rv(c), 0)),
                   pl.BlockSpec((L, 128), lambda c: (rv(c), 0)), vecin, vec128, vec128, vec128]
        + (ride.out_specs if ride else []),
        out_shape=[jax.ShapeDtypeStruct((S, SSD_INNER), bf16), jax.ShapeDtypeStruct((S, CONV_DIM), f32),
                   jax.ShapeDtypeStruct((S, 128), bf16), jax.ShapeDtypeStruct((1, SSD_INNER), f32),
                   jax.ShapeDtypeStruct((1, 128), f32), jax.ShapeDtypeStruct((1, 128), f32),
                   jax.ShapeDtypeStruct((1, 128), f32)] + (ride.out_shape if ride else []),
        scratch_shapes=[pltpu.VMEM((N_PAIR, 128, SSD_N), f32), pltpu.VMEM((L, SSD_INNER), f32),
                        pltpu.VMEM((L, SSD_INNER), f32), pltpu.VMEM((L, 2 * SSD_INNER), f32),
                        pltpu.VMEM((L, SSD_INNER), f32), pltpu.VMEM((L, SSD_INNER), f32),
                        pltpu.VMEM((128, L), f32)]
        + (ride.scratch if ride else []),
        compiler_params=_params(("arbitrary",)),
    )(dyn, y, xbc, pdt, proj, hprev, dt_bias_p, a_log_p, d_skip_c, ssd_norm, sel_pair, sel_head,
      *(ride.srcs if ride else []))


MEM_W = MEM_HEADS * MEM_HD


def _mem_probs(q, k):
    s = _dot(q, k, NT) * (MEM_HD ** -0.5)
    s = s - jnp.max(s, axis=1, keepdims=True)
    p = jnp.exp(s)
    return p / jnp.sum(p, axis=1, keepdims=True)


def _mem_fwd(proj, kv, S, tm=512):
    tm = min(tm, S)
    M = kv.shape[0]

    def body(q_ref, kv_ref, o_ref):
        for h in range(MEM_HEADS):
            sl = slice(h * MEM_HD, (h + 1) * MEM_HD)
            vsl = slice(MEM_W + h * MEM_HD, MEM_W + (h + 1) * MEM_HD)
            p = _mem_probs(q_ref[:, sl].astype(bf16), kv_ref[:, sl].astype(bf16))
            o_ref[:, sl] = _dot(p.astype(bf16), kv_ref[:, vsl].astype(bf16)).astype(bf16)

    return pl.pallas_call(
        body, name="mem_fwd", grid=(S // tm,),
        in_specs=[pl.BlockSpec((tm, MEM_W), lambda i: (i, P_MEMQ // MEM_W)),
                  pl.BlockSpec((M, 2 * MEM_W), lambda i: (0, 0))],
        out_specs=pl.BlockSpec((tm, MEM_W), lambda i: (i, 0)),
        out_shape=jax.ShapeDtypeStruct((S, MEM_W), bf16),
        compiler_params=_params(("parallel",)),
    )(proj, kv)


def _mem_bwd(proj, kv, dy, S, tm=512):
    tm = min(tm, S)
    M = kv.shape[0]
    scale = MEM_HD ** -0.5

    def body(q_ref, kv_ref, dy_ref, dq_ref, dkv_ref):
        @pl.when(pl.program_id(0) == 0)
        def _():
            dkv_ref[...] = jnp.zeros_like(dkv_ref)

        for h in range(MEM_HEADS):
            sl = slice(h * MEM_HD, (h + 1) * MEM_HD)
            vsl = slice(MEM_W + h * MEM_HD, MEM_W + (h + 1) * MEM_HD)
            q = q_ref[:, sl].astype(bf16)
            k = kv_ref[:, sl].astype(bf16)
            v = kv_ref[:, vsl].astype(bf16)
            dyh = dy_ref[:, sl].astype(bf16)
            p = _mem_probs(q, k)
            dp = _dot(dyh, v, NT)
            ds = (p * (dp - jnp.sum(dp * p, axis=1, keepdims=True)) * scale).astype(bf16)
            dq_ref[:, sl] = _dot(ds, k).astype(bf16)
            dkv_ref[:, sl] += _dot(ds, q, TN)
            dkv_ref[:, vsl] += _dot(p.astype(bf16), dyh, TN)

    return pl.pallas_call(
        body, name="mem_bwd", grid=(S // tm,),
        in_specs=[pl.BlockSpec((tm, MEM_W), lambda i: (i, P_MEMQ // MEM_W)),
                  pl.BlockSpec((M, 2 * MEM_W), lambda i: (0, 0)),
                  pl.BlockSpec((tm, MEM_W), lambda i: (i, 0))],
        out_specs=[pl.BlockSpec((tm, MEM_W), lambda i: (i, 0)), pl.BlockSpec((M, 2 * MEM_W), lambda i: (0, 0))],
        out_shape=[jax.ShapeDtypeStruct((S, MEM_W), bf16), jax.ShapeDtypeStruct((M, 2 * MEM_W), f32)],
        compiler_params=_params(("arbitrary",)),
    )(proj, kv, dy)


def _merge_fwd(proj, t0, t1, t2, S, tm=512):
    tm = min(tm, S)

    def body(g_ref, t0_ref, t1_ref, t2_ref, o_ref):
        acc = jnp.zeros((tm, D), f32)
        for b, t_ref in enumerate((t0_ref, t1_ref, t2_ref)):
            acc = acc + _sigmoid(g_ref[:, b * D:(b + 1) * D]) * t_ref[...]
        o_ref[...] = acc.astype(bf16)

    row = pl.BlockSpec((tm, D), lambda i: (i, 0))
    return pl.pallas_call(
        body, name="merge_fwd", grid=(S // tm,),
        in_specs=[pl.BlockSpec((tm, 3 * D), lambda i: (i, P_GATE // (3 * D))), row, row, row],
        out_specs=row, out_shape=jax.ShapeDtypeStruct((S, D), bf16),
        compiler_params=_params(("parallel",)),
    )(proj, t0, t1, t2)


def _merge_bwd(proj, t0, t1, t2, dm, S, tm=512):
    tm = min(tm, S)

    def body(g_ref, t0_ref, t1_ref, t2_ref, dm_ref, d0_ref, d1_ref, d2_ref, dg_ref):
        dmv = dm_ref[...]
        for b, (t_ref, d_ref) in enumerate(((t0_ref, d0_ref), (t1_ref, d1_ref), (t2_ref, d2_ref))):
            sg = _sigmoid(g_ref[:, b * D:(b + 1) * D])
            d_ref[...] = (dmv * sg).astype(bf16)
            dg_ref[:, b * D:(b + 1) * D] = (dmv * t_ref[...] * sg * (1.0 - sg)).astype(bf16)

    row = pl.BlockSpec((tm, D), lambda i: (i, 0))
    return pl.pallas_call(
        body, name="merge_bwd", grid=(S // tm,),
        in_specs=[pl.BlockSpec((tm, 3 * D), lambda i: (i, P_GATE // (3 * D))), row, row, row, row],
        out_specs=[row, row, row, pl.BlockSpec((tm, 3 * D), lambda i: (i, 0))],
        out_shape=[jax.ShapeDtypeStruct((S, D), bf16)] * 3 + [jax.ShapeDtypeStruct((S, 3 * D), bf16)],
        compiler_params=_params(("parallel",)),
    )(proj, t0, t1, t2, dm)


def _loss_head(ff, g, h1, target, S, tm=512):
    tm = min(tm, S)

    def body(ff_ref, g_ref, h1_ref, t_ref, dh_ref, loss_ref):
        xv = ff_ref[...]
        r = lax.rsqrt(jnp.mean(xv * xv, axis=1, keepdims=True) + EPS)
        err = h1_ref[...] + xv * r * g_ref[...] - t_ref[...]
        dh_ref[...] = err * (1.0 / D)

        @pl.when(pl.program_id(0) == 0)
        def _():
            loss_ref[...] = jnp.zeros_like(loss_ref)

        loss_ref[...] += 0.5 * _sum_all(jnp.mean(err * err, axis=1, keepdims=True)) * jnp.ones((1, 128), f32)

    row = pl.BlockSpec((tm, D), lambda i: (i, 0))
    return pl.pallas_call(
        body, name="loss_head", grid=(S // tm,),
        in_specs=[row, pl.BlockSpec((1, D), lambda i: (0, 0)), row, row],
        out_specs=[row, pl.BlockSpec((1, 128), lambda i: (0, 0))],
        out_shape=[jax.ShapeDtypeStruct((S, D), f32), jax.ShapeDtypeStruct((1, 128), f32)],
        compiler_params=_params(("arbitrary",)),
    )(ff, g, h1, target)


def _local_step(x, mem, target, wts, late_rides, late_weights, small, rest_rides, w_in_ride):
    S = x.shape[0]
    M = mem.shape[0]
    pad = lambda v: jnp.pad(v, ((0, 0), (0, 128 - SSD_HEADS)))
    dtb_p, alog_p = pad(small["dt_bias"]), pad(small["a_log"])
    dsk_c = jnp.repeat(small["d_skip"], SB_HD, axis=1)

    u = _rms_fwd(x, small["norm_mix_pre"], name="norm_pre", out_dtype=bf16)
    rides = late_rides or (None, None, None)
    if late_rides:
        proj, lands_a = _mm(u, wts["w_main"], "nn", tm=1024, tn=1024, name="in_proj", ride=rides[0])
    else:
        proj, lands_a = _mm(u, wts["w_main"], "nn", tm=1024, tn=1024, name="in_proj"), []
    pdt = _mm(u, wts["w_dt"], "nn", tm=1024, tn=128, name="in_proj_dt")
    y_sb, tot_lk, lands_b = _sb_fwd(proj, S, rides[1])
    wts = dict(wts, **late_weights(0, lands_a))
    small = dict(small, conv_w=wts.pop("conv_w"))
    xc, xbc = _conv_fwd(proj, small["conv_w"], small["conv_b"], S)
    y_ssd, yn, hprev, lands_c = _ssd_fwd(xbc, proj, pdt, dtb_p, alog_p, dsk_c, small["ssd_norm"], S, rides[2])
    wts = dict(wts, **late_weights(1, lands_b), **late_weights(2, lands_c))
    mn = _rms_fwd(mem, small["norm_mem"], name="norm_mem", out_dtype=bf16, tm=min(512, M))
    kv = _mm(mn, wts["w_mem_kv"], "nn", tm=M, tn=1024, name="mem_kv")
    y_mem = _mem_fwd(proj, kv, S)
    t0 = _mm(y_sb, wts["w_sb_out"], "nn", tm=1024, tn=1024, name="sb_out")
    t1 = _mm(yn, wts["w_ssd_out"], "nn", tm=1024, tn=1024, name="ssd_out")
    t2 = _mm(y_mem, wts["w_mem_out"], "nn", tm=1024, tn=1024, name="mem_out")
    merged = _merge_fwd(proj, t0, t1, t2, S)
    mix = _mm(merged, wts["w_o"], "nn", tm=1024, tn=1024, name="w_o")
    h1 = _rms_fwd(mix, small["norm_mix_post"], name="norm_mix_post", out_dtype=f32, residual=x)
    u2 = _rms_fwd(h1, small["norm_mlp_pre"], name="norm_mlp_pre", out_dtype=bf16)
    a_up, hrelu = _mm(u2, wts["w_up"], "nn", tm=1024, tn=1024, name="mlp_up", out_dtypes=(f32, bf16),
                      epi=lambda acc: (acc, jnp.square(jnp.maximum(acc, 0.0))))
    ff = _mm(hrelu, wts["w_down"], "nn", tm=1024, tn=1024, name="mlp_down")
    dh2, loss = _loss_head(ff, small["norm_mlp_post"], h1, target, S)

    g = {}
    dff, g["norm_mlp_post"] = _rms_bwd(ff, dh2, small["norm_mlp_post"], name="norm_mlp_post_bwd", out_dtype=bf16)
    da = _mm(dff, wts["w_down"], "nt", tm=1024, tn=1024, name="mlp_down_dx", out_dtypes=(bf16,),
             epi=lambda acc, a: (acc * (2.0 * jnp.maximum(a, 0.0)),), extras=(a_up,))
    g["w_down"] = _mm(hrelu, dff, "tn", tm=1024, tn=1024, name="mlp_down_dw")
    du2 = _mm(da, wts["w_up"], "nt", tm=1024, tn=1024, name="mlp_up_dx")
    g["w_up"] = _mm(u2, da, "tn", tm=1024, tn=1024, name="mlp_up_dw")
    dh1, g["norm_mlp_pre"] = _rms_bwd(h1, du2, small["norm_mlp_pre"], name="norm_mlp_pre_bwd", out_dtype=f32, add=dh2)
    dmix, g["norm_mix_post"] = _rms_bwd(mix, dh1, small["norm_mix_post"], name="norm_mix_post_bwd", out_dtype=bf16)
    dmerged = _mm(dmix, wts["w_o"], "nt", tm=1024, tn=1024, name="w_o_dx")
    g["w_o"] = _mm(merged, dmix, "tn", tm=1024, tn=1024, name="w_o_dw")
    dt0, dt1, dt2, dgl = _merge_bwd(proj, t0, t1, t2, dmerged, S)
    dy_sb = _mm(dt0, wts["w_sb_out"], "nt", tm=1024, tn=1024, name="sb_out_dx")
    g["w_sb_out"] = _mm(y_sb, dt0, "tn", tm=1024, tn=1024, name="sb_out_dw")
    dy_ssd = _mm(dt1, wts["w_ssd_out"], "nt", tm=1024, tn=1024, name="ssd_out_dx")
    g["w_ssd_out"] = _mm(yn, dt1, "tn", tm=1024, tn=1024, name="ssd_out_dw")
    dy_mem = _mm(dt2, wts["w_mem_out"], "nt", tm=1024, tn=1024, name="mem_out_dx")
    g["w_mem_out"] = _mm(y_mem, dt2, "tn", tm=1024, tn=1024, name="mem_out_dw")
    dmemq, dkv = _mem_bwd(proj, kv, dy_mem, S)
    g["w_mem_kv"] = _mm(mn, dkv, "tn", tm=1024, tn=1024, name="mem_kv_dw")
    dmn = _mm(dkv, wts["w_mem_kv"], "nt", tm=M, tn=1024, name="mem_kv_dx")
    _, g["norm_mem"] = _rms_bwd(mem, dmn, small["norm_mem"], name="norm_mem_bwd", out_dtype=bf16, tm=min(512, M))
    rides = rest_rides(g) if rest_rides else (None, None)
    dz, dxbc, ddt, g["ssd_norm"], dsk, dalog, ddtb, *lands_a = _ssd_bwd(
        dy_ssd, y_ssd, xbc, proj, pdt, hprev, dtb_p, alog_p, dsk_c, small["ssd_norm"], S, rides[0])
    g["d_skip"], g["a_log"], g["dt_bias"] = dsk[:, :SSD_HEADS], dalog[:, :SSD_HEADS], ddtb[:, :SSD_HEADS]
    dxbc_raw, dcw, g["conv_b"] = _conv_bwd(proj, xc, dxbc, small["conv_w"], S)
    g["conv_w"] = dcw[:CONV_K]
    dq, dk, dv, lands_b = _sb_bwd(proj, tot_lk, dy_sb, S, rides[1])
    g["rest_lands"] = lands_b + lands_a
    dproj = (dq, dk, dv, dxbc_raw, dgl, dmemq, dz)
    u_t = u.T
    g["w_main"] = [_mm(u_t, p, "nn", tm=512, tn=1024, name="in_proj_dw_%d" % i) for i, p in enumerate(dproj)]
    g["w_dt"] = _mm(u_t, ddt, "nn", tm=512, tn=128, name="in_proj_dt_dw")
    du_dt = _mm(ddt, wts["w_dt"], "nt", tm=1024, tn=1024, name="in_proj_dt_dx")
    du, g["w_in_lands"] = _mm_pieces_nt(dproj, wts["w_main"], du_dt, tm=512, tn=256, name="in_proj_dx",
                                        ride=w_in_ride(g) if w_in_ride else None)
    grad_x, g["norm_mix_pre"] = _rms_bwd(x, du, small["norm_mix_pre"], name="norm_pre_bwd", out_dtype=f32, add=dh1)
    return loss, grad_x, g


def _to_internal(w_in):
    sec = lambda r: w_in[:, r[0]:r[1]]
    w_main = jnp.concatenate([sec(R_QKV), sec(R_XBC), sec(R_GATE), sec(R_MEMQ), sec(R_Z)], axis=1)
    w_dt = jnp.pad(sec(R_DT), ((0, 0), (0, 128 - SSD_HEADS)))
    return w_main, w_dt


def _from_internal(pieces, g_dt):
    dq, dk, dv, dxbc, dgate, dmemq, dz = pieces
    return [dq, dk, dv, dz, dxbc, g_dt[:, :SSD_HEADS], dmemq, dgate]


def _w_in_slab(ordered, s, dtype):
    width = D_IN // N_SHARD
    lo, hi, off, parts = s * width, (s + 1) * width, 0, []
    for p in ordered:
        a, b = max(lo, off), min(hi, off + p.shape[1])
        if a < b:
            parts.append(p[:, a - off:b - off].astype(dtype))
        off += p.shape[1]
    return jnp.concatenate(parts, axis=1)


MESH = pl.DeviceIdType.MESH
ANY = pl.BlockSpec(memory_space=pl.ANY)


def _place():
    x, y, c = lax.axis_index("x"), lax.axis_index("y"), lax.axis_index("c")
    return (x, y, c), [(1 - x, y, c), (x, 1 - y, c), (1 - x, 1 - y, c)]


def _exchange_copy(mode, ins, lands, send, recv, a, k, me, peers, arriving):
    p = peers[k]
    theirs = 2 * p[0] + p[1]
    if mode == "gather":
        src, dst = ins[a], lands[a].at[theirs if arriving else me]
    else:
        src, dst = ins[a].at[theirs], lands[a].at[k]
    return pltpu.make_async_remote_copy(src_ref=src, dst_ref=dst, send_sem=send.at[a * 3 + k],
                                        recv_sem=recv.at[a * 3 + k], device_id=p, device_id_type=MESH)


class _Ride:
    def __init__(self, srcs, mode):
        self.srcs, self.mode, self.n = list(srcs), mode, len(srcs)
        n = self.n
        self.in_specs, self.out_specs = [ANY] * n, [ANY] * n
        self.out_shape = [
            jax.ShapeDtypeStruct((N_SHARD,) + s.shape if mode == "gather" else (3,) + s.shape[1:], s.dtype)
            for s in self.srcs]
        self.scratch = [pltpu.SemaphoreType.DMA((3 * n,)), pltpu.SemaphoreType.DMA((3 * n,)),
                        pltpu.SemaphoreType.DMA((n,))]

    def _own(self, ins, lnd, sems):
        if self.mode != "gather":
            return []
        me = 2 * lax.axis_index("x") + lax.axis_index("y")
        return [pltpu.make_async_copy(ins[a], lnd[a].at[me], sems[2].at[a]) for a in range(self.n)]

    def _far(self, ins, lnd, sems, arriving):
        (x, y, c), peers = _place()
        return [_exchange_copy(self.mode, ins, lnd, sems[0], sems[1], a, k, 2 * x + y, peers, arriving)
                for a in range(self.n) for k in range(3)]

    def start(self, ins, lnd, sems):
        for cp in self._own(ins, lnd, sems) + self._far(ins, lnd, sems, False):
            cp.start()

    def finish(self, ins, lnd, sems):
        for cp in self._far(ins, lnd, sems, True):
            cp.wait_recv()
        for cp in self._far(ins, lnd, sems, False):
            cp.wait_send()
        for cp in self._own(ins, lnd, sems):
            cp.wait()


def _gather_two_level(shards, name):
    n = len(shards)

    def body(*refs):
        ins, lnd = refs[:n], refs[n:2 * n]
        send, recv, loc = refs[2 * n:]
        (x, y, c), peers = _place()
        me = 2 * x + y

        def half(ref, a, core):
            rows = shards[a].shape[0] // 2
            return ref.at[pl.ds(core * rows, rows)]

        def copy(a, j, slot, core, to):
            return pltpu.make_async_remote_copy(
                src_ref=half(ins[a], a, core) if j < 3 else half(lnd[a].at[slot], a, core),
                dst_ref=half(lnd[a].at[slot], a, core), send_sem=send.at[6 * a + j], recv_sem=recv.at[6 * a + j],
                device_id=to, device_id_type=MESH)

        own = [pltpu.make_async_copy(ins[a], lnd[a].at[me], loc.at[a]) for a in range(n)]
        far = [copy(a, k, me, c, peers[k]) for a in range(n) for k in range(3)]
        for cp in own + far:
            cp.start()
        passed = []
        for a in range(n):
            for k, p in enumerate(peers):
                theirs = 2 * p[0] + p[1]
                copy(a, k, theirs, c, p).wait_recv()
                passed.append(copy(a, 3 + k, theirs, c, (x, y, 1 - c)))
                passed[-1].start()
        for a in range(n):
            for k, p in enumerate(peers):
                copy(a, 3 + k, 2 * p[0] + p[1], 1 - c, (x, y, 1 - c)).wait_recv()
        for cp in far + passed:
            cp.wait_send()
        for cp in own:
            cp.wait()

    return pl.pallas_call(
        body, name=name, in_specs=[ANY] * n, out_specs=[ANY] * n,
        out_shape=[jax.ShapeDtypeStruct((N_SHARD,) + s.shape, s.dtype) for s in shards],
        scratch_shapes=[pltpu.SemaphoreType.DMA((6 * n,)), pltpu.SemaphoreType.DMA((6 * n,)),
                        pltpu.SemaphoreType.DMA((n,))],
    )(*shards)


def _exchange_packets(packet):
    def body(pk, pk_out, send, recv, loc):
        x, y, c = lax.axis_index("x"), lax.axis_index("y"), lax.axis_index("c")
        lin = 4 * x + 2 * y + c
        own = pltpu.make_async_copy(pk, pk_out.at[lin], loc.at[0])
        own.start()

        def pk_copy(m, slot):
            dev = (x ^ ((m >> 2) & 1), y ^ ((m >> 1) & 1), c ^ (m & 1))
            return pltpu.make_async_remote_copy(
                src_ref=pk, dst_ref=pk_out.at[slot], send_sem=send.at[m - 1], recv_sem=recv.at[m - 1],
                device_id=dev, device_id_type=MESH)

        sent = [pk_copy(m, lin) for m in range(1, N_DEV)]
        for cp in sent:
            cp.start()
        for m in range(1, N_DEV):
            pk_copy(m, lin ^ m).wait_recv()
        for cp in sent:
            cp.wait_send()
        own.wait()

    return pl.pallas_call(
        body, name="exchange_packets", in_specs=[ANY], out_specs=ANY,
        out_shape=jax.ShapeDtypeStruct((N_DEV,) + packet.shape, packet.dtype),
        scratch_shapes=[pltpu.SemaphoreType.DMA((N_DEV - 1,)), pltpu.SemaphoreType.DMA((N_DEV - 1,)),
                        pltpu.SemaphoreType.DMA((1,))],
    )(packet)


def _swap_sibling(parts, name):
    n = len(parts)

    def body(*refs):
        ins, outs = refs[:n], refs[n:2 * n]
        send, recv = refs[2 * n:]
        x, y, c = lax.axis_index("x"), lax.axis_index("y"), lax.axis_index("c")
        cps = [pltpu.make_async_remote_copy(
            src_ref=ins[a], dst_ref=outs[a], send_sem=send.at[a], recv_sem=recv.at[a],
            device_id=(x, y, 1 - c), device_id_type=MESH) for a in range(n)]
        for cp in cps:
            cp.start()
        for cp in cps:
            cp.wait_recv()
        for cp in cps:
            cp.wait_send()

    return pl.pallas_call(
        body, name=name,
        in_specs=[ANY] * n, out_specs=[ANY] * n,
        out_shape=[jax.ShapeDtypeStruct(p.shape, p.dtype) for p in parts],
        scratch_shapes=[pltpu.SemaphoreType.DMA((n,)), pltpu.SemaphoreType.DMA((n,))],
    )(*parts)


BLOCK_ELEMS = 256 * 1024


def _row_tile(R, C):
    tr = max(8, (BLOCK_ELEMS // C) // 8 * 8)
    while R % tr:
        tr -= 8
    return min(tr, R)


def _sum_parts(own, stack, name, out_dtype=f32):
    k = stack.shape[0]
    R, C = stack.shape[1:]
    tr = _row_tile(R, C)

    def body(*refs):
        o_ref = refs[-1]
        acc = refs[0][...].astype(f32)
        for r in refs[1:-1]:
            acc = acc + r[...].astype(f32)
        o_ref[...] = acc.astype(out_dtype)

    row = pl.BlockSpec((tr, C), lambda i: (i, 0))
    specs = ([row] if own is not None else []) + [
        pl.BlockSpec((None, tr, C), functools.partial(lambda i, j: (j, i, 0), j=j)) for j in range(k)]
    args = ([own] if own is not None else []) + [stack] * k
    return pl.pallas_call(
        body, name=name, grid=(R // tr,), in_specs=specs, out_specs=row,
        out_shape=jax.ShapeDtypeStruct((R, C), out_dtype), compiler_params=_params(("parallel",)),
    )(*args)


def _adamw(w, m, v, g_parts, name):
    R, C = w.shape
    tr = _row_tile(R, C)
    n_g = len(g_parts)

    def body(w_ref, m_ref, v_ref, *rest):
        g = rest[0][...]
        for r in rest[1:n_g]:
            g = g + r[...]
        g_ref, d_ref, nm_ref, nv_ref = rest[n_g:]
        nm = ADAM_B1 * m_ref[...] + (1.0 - ADAM_B1) * g
        nv = ADAM_B2 * v_ref[...] + (1.0 - ADAM_B2) * jnp.square(g)
        m_hat = nm / (1.0 - ADAM_B1 ** ADAM_STEP)
        v_hat = nv / (1.0 - ADAM_B2 ** ADAM_STEP)
        g_ref[...] = g
        d_ref[...] = -ADAM_LR * (m_hat / (jnp.sqrt(v_hat) + ADAM_EPS) + ADAM_WD * w_ref[...])
        nm_ref[...] = nm
        nv_ref[...] = nv

    row = pl.BlockSpec((tr, C), lambda i: (i, 0))
    return pl.pallas_call(
        body, name=name, grid=(R // tr,), in_specs=[row] * (3 + n_g), out_specs=[row] * 4,
        out_shape=[jax.ShapeDtypeStruct((R, C), f32)] * 4, compiler_params=_params(("parallel",)),
    )(w, m, v, *g_parts)


BIG = ("w_in", "w_mem_kv", "w_sb_out", "w_ssd_out", "w_mem_out", "w_o", "w_up", "w_down")
LATE = ("w_sb_out", "w_ssd_out", "w_mem_out", "w_o", "w_up", "w_down")
REST = BIG[1:]
COL_SHARDED = ("w_in", "w_mem_kv", "w_up")
SMALL = ("norm_mix_pre", "conv_w", "conv_b", "dt_bias", "a_log", "d_skip", "ssd_norm", "norm_mem",
         "norm_mix_post", "norm_mlp_pre", "norm_mlp_post")
WEIGHTS = ("norm_mix_pre", "w_in", "conv_w", "conv_b", "dt_bias", "a_log", "d_skip", "ssd_norm", "norm_mem",
           "w_mem_kv", "w_sb_out", "w_ssd_out", "w_mem_out", "w_o", "norm_mix_post", "norm_mlp_pre", "w_up",
           "w_down", "norm_mlp_post")
PK_ROWS = 184


def _pack(vecs):
    flat = jnp.concatenate([v.reshape(-1) for v in vecs])
    return jnp.pad(flat, (0, PK_ROWS * 128 - flat.shape[0])).reshape(PK_ROWS, 128)


def _unpack(pk, shapes):
    flat = pk.reshape(-1)
    out, off = [], 0
    for s in shapes:
        n = 1
        for d in s:
            n *= d
        out.append(flat[off:off + n].reshape(s))
        off += n
    return out


def _full_from_slabs(name, slabs):
    if name in COL_SHARDED:
        return slabs.transpose(1, 0, 2).reshape(slabs.shape[1], -1)
    return slabs.reshape(-1, slabs.shape[2])


def _slabs_from_full(name, g):
    if name in COL_SHARDED:
        return g.reshape(g.shape[0], N_SHARD, -1).transpose(1, 0, 2)
    return g.reshape(N_SHARD, -1, g.shape[1])


def kernel(x, mem, norm_mix_pre, w_in, conv_w, conv_b, dt_bias, a_log, d_skip, ssd_norm, norm_mem, w_mem_kv, w_sb_out, w_ssd_out, w_mem_out, w_o, norm_mix_post, norm_mlp_pre, w_up, w_down, norm_mlp_post, loss_target, m_norm_mix_pre, m_w_in, m_conv_w, m_conv_b, m_dt_bias, m_a_log, m_d_skip, m_ssd_norm, m_norm_mem, m_w_mem_kv, m_w_sb_out, m_w_ssd_out, m_w_mem_out, m_w_o, m_norm_mix_post, m_norm_mlp_pre, m_w_up, m_w_down, m_norm_mlp_post, v_norm_mix_pre, v_w_in, v_conv_w, v_conv_b, v_dt_bias, v_a_log, v_d_skip, v_ssd_norm, v_norm_mem, v_w_mem_kv, v_w_sb_out, v_w_ssd_out, v_w_mem_out, v_w_o, v_norm_mix_post, v_norm_mlp_pre, v_w_up, v_w_down, v_norm_mlp_post):
    env = dict(locals())
    w = {n: env[n] for n in WEIGHTS}
    mo = {n: env["m_" + n] for n in WEIGHTS}
    vo = {n: env["v_" + n] for n in WEIGHTS}
    shard = 2 * lax.axis_index("x") + lax.axis_index("y")

    first = _gather_two_level([w["w_in"][0].astype(bf16)], "gather_first")
    w_main, w_dt = _to_internal(_full_from_slabs("w_in", first[0]))
    wts = dict(w_main=w_main, w_dt=w_dt)
    ride_names = (("w_mem_kv",) + LATE[:4], LATE[4:5], LATE[5:])
    late_rides = tuple(_Ride([w[n][0].astype(bf16) for n in names] + ([w["conv_w"][0]] if i == 0 else []), "gather")
                       for i, names in enumerate(ride_names))

    def late_weights(i, lands):
        full = {n: _full_from_slabs(n, s) for n, s in zip(ride_names[i], lands)}
        if i == 0:
            full["conv_w"] = lands[-1].transpose(1, 0, 2).reshape(CONV_K, CONV_DIM)
        return full

    def rest_rides(g):
        slabs = [_slabs_from_full(n, g[n]).astype(bf16) for n in REST]
        return _Ride(slabs[5:], "scatter"), _Ride(slabs[:5], "scatter")

    core = lax.axis_index("c")
    half = D // 2

    def w_in_ride(g):
        ordered = _from_internal(g["w_main"], g["w_dt"])
        stack = jnp.stack([_w_in_slab(ordered, s, bf16) for s in range(N_SHARD)])
        keep = lax.dynamic_slice_in_dim(stack, core * half, half, axis=1)
        away = lax.dynamic_slice_in_dim(stack, (1 - core) * half, half, axis=1)
        (got,) = _swap_sibling([away], "w_in_halves_out")
        wide = lambda a: a.reshape(N_SHARD * half, -1)
        chip = _sum_parts(wide(keep), wide(got)[None], "sum_cores_w_in", bf16).reshape(N_SHARD, half, -1)
        own = lax.switch(shard, [functools.partial(_w_in_slab, ordered, s, f32) for s in range(N_SHARD)])
        own = lax.dynamic_slice_in_dim(own, core * half, half, axis=0)
        g["w_in_own"] = _sum_parts(own, lax.dynamic_index_in_dim(got, shard, 0, keepdims=True), "sum_cores_w_in_own")
        return _Ride([chip], "scatter")

    small = {n: w[n] for n in SMALL if n != "conv_w"}
    loss, grad_x, g = _local_step(x[0], mem[0], loss_target[0], wts, late_rides, late_weights, small,
                                  rest_rides, w_in_ride)
    out_g, out_d, out_m, out_v = {}, {}, {}, {}

    def apply(n, g_parts):
        res = _adamw(w[n][0], mo[n][0], vo[n][0], g_parts, name="adamw_" + n)
        out_g[n], out_d[n], out_m[n], out_v[n] = [r[None] for r in res]

    mine = _sum_parts(g["w_in_own"], g["w_in_lands"][0], name="sum_chips_w_in")
    (theirs,) = _swap_sibling([mine], "w_in_halves_back")
    g_w_in = lax.dynamic_update_slice_in_dim(jnp.zeros((D, D_IN // N_SHARD), f32), mine, core * half, axis=0)
    apply("w_in", [lax.dynamic_update_slice_in_dim(g_w_in, theirs, (1 - core) * half, axis=0)])

    packets = _exchange_packets(_pack([g[n] for n in SMALL] + [loss[:, :1]]))
    partial = []
    for n, r in zip(REST, g["rest_lands"]):
        own = lax.dynamic_index_in_dim(_slabs_from_full(n, g[n]), shard, 0, keepdims=False)
        partial.append(_sum_parts(own, r, name="sum_chips_" + n))
    other = _swap_sibling(partial, "swap_sibling")

    for n, p, q in zip(REST, partial, other):
        apply(n, [p, q])
    tot = _sum_parts(None, packets, name="sum_packets")
    shapes = [g[n].shape for n in SMALL] + [(1, 1)]
    sm = dict(zip(SMALL + ("loss",), _unpack(tot, shapes)))
    sm["conv_w"] = lax.dynamic_slice_in_dim(sm["conv_w"], shard * (CONV_DIM // N_SHARD), CONV_DIM // N_SHARD, axis=1)
    own_small = lambda d: _pack([d[n].reshape(sm[n].shape) for n in SMALL])
    res = _adamw(own_small(w), own_small(mo), own_small(vo), [own_small(sm)], name="adamw_small")
    own_shapes = [sm[n].shape for n in SMALL]
    for store, r in zip((out_g, out_d, out_m, out_v), res):
        for n, val in zip(SMALL, _unpack(r, own_shapes)):
            store[n] = val.reshape(w[n].shape)

    outs = [sm["loss"].reshape(()), grad_x[None]]
    for store in (out_g, out_d, out_m, out_v):
        outs += [store[n] for n in WEIGHTS]
    return tuple(outs)
```

```python
import functools

import jax
import jax.numpy as jnp
from jax import lax
from jax.experimental import pallas as pl
from jax.experimental.pallas import tpu as pltpu

f32 = jnp.float32
bf16 = jnp.bfloat16

D = 1024
EPS = 1e-6
SB_HD = 64
SSD_INNER = 2048
SSD_HEADS = 32
SSD_GROUPS = 4
SSD_N = 128
SSD_L = 128
CONV_K = 4
CONV_DIM = 3072
MEM_HEADS = 4
MEM_HD = 256
D_FF = 4096
D_IN = 12320
N_SHARD = 4
N_DEV = 8

P_QKV, P_XBC, P_GATE, P_MEMQ, P_Z, P_DT, P_TOT = 0, 3072, 6144, 9216, 10240, 12288, 12416
R_QKV, R_Z, R_XBC, R_DT, R_MEMQ, R_GATE = (0, 3072), (3072, 5120), (5120, 8192), (8192, 8224), (8224, 9248), (9248, 12320)

ADAM_LR = 0.001
ADAM_B1 = 0.9
ADAM_B2 = 0.999
ADAM_EPS = 1e-08
ADAM_WD = 0.01
ADAM_STEP = 10

VMEM_LIMIT = 56 * 1024 * 1024

NN = (((1,), (0,)), ((), ()))
NT = (((1,), (1,)), ((), ()))
TN = (((0,), (0,)), ((), ()))


def _dot(a, b, dims=NN):
    return lax.dot_general(a, b, dims, preferred_element_type=f32)


def _params(sem=None):
    return pltpu.CompilerParams(dimension_semantics=sem, vmem_limit_bytes=VMEM_LIMIT)


def _sigmoid(x):
    return 1.0 / (1.0 + jnp.exp(-x))


def _split2(x):
    hi = x.astype(bf16)
    lo = (x - hi.astype(f32)).astype(bf16)
    return hi, lo


def _split3(x):
    hi = x.astype(bf16)
    r = x - hi.astype(f32)
    mid = r.astype(bf16)
    lo = (r - mid.astype(f32)).astype(bf16)
    return hi, mid, lo


def _mm(a, b, mode, *, tm, tn, name, out_dtypes=(f32,), epi=None, extras=(), ride=None):
    M = a.shape[1] if mode == "tn" else a.shape[0]
    N = b.shape[0] if mode == "nt" else b.shape[1]
    tm, tn = min(tm, M), min(tn, N)
    if mode == "nn":
        (M, K), N = a.shape, b.shape[1]
        a_spec = pl.BlockSpec((tm, K), lambda i, j: (i, 0))
        b_spec = pl.BlockSpec((K, tn), lambda i, j: (0, j))
        dims = NN
    elif mode == "nt":
        (M, K), N = a.shape, b.shape[0]
        a_spec = pl.BlockSpec((tm, K), lambda i, j: (i, 0))
        b_spec = pl.BlockSpec((tn, K), lambda i, j: (j, 0))
        dims = NT
    else:
        (K, M), N = a.shape, b.shape[1]
        a_spec = pl.BlockSpec((K, tm), lambda i, j: (0, i))
        b_spec = pl.BlockSpec((K, tn), lambda i, j: (0, j))
        dims = TN
    assert M % tm == 0 and N % tn == 0, (name, M, N, tm, tn)
    n_ex, n_out = len(extras), len(out_dtypes)
    n_r = ride.n if ride else 0
    o_spec = pl.BlockSpec((tm, tn), lambda i, j: (i, j))
    grid = (M // tm, N // tn)

    def body(a_ref, b_ref, *rest):
        r_ins = rest[n_ex:n_ex + n_r]
        outs = rest[n_ex + n_r:n_ex + n_r + n_out]
        r_lnd, r_sems = rest[n_ex + n_r + n_out:n_ex + 2 * n_r + n_out], rest[n_ex + 2 * n_r + n_out:]
        i, j = pl.program_id(0), pl.program_id(1)
        if ride:
            pl.when((i == 0) & (j == 0))(lambda: ride.start(r_ins, r_lnd, r_sems))
        acc = _dot(a_ref[...].astype(bf16), b_ref[...].astype(bf16), dims)
        res = (acc,) if epi is None else epi(acc, *[e[...] for e in rest[:n_ex]])
        for o_ref, r in zip(outs, res):
            o_ref[...] = r.astype(o_ref.dtype)
        if ride:
            pl.when((i == grid[0] - 1) & (j == grid[1] - 1))(lambda: ride.finish(r_ins, r_lnd, r_sems))

    out = pl.pallas_call(
        body, name=name, grid=grid,
        in_specs=[a_spec, b_spec] + [o_spec] * n_ex + (ride.in_specs if ride else []),
        out_specs=[o_spec] * n_out + (ride.out_specs if ride else []),
        out_shape=[jax.ShapeDtypeStruct((M, N), dt) for dt in out_dtypes] + (ride.out_shape if ride else []),
        scratch_shapes=ride.scratch if ride else [],
        compiler_params=_params(("arbitrary", "arbitrary") if ride else ("parallel", "parallel")),
    )(a, b, *extras, *(ride.srcs if ride else []))
    if ride:
        return (out[0] if n_out == 1 else out[:n_out]), list(out[n_out:])
    return out[0] if n_out == 1 else out


def _mm_pieces_nt(pieces, b, add, *, tm, tn, name, ride):
    M, N = pieces[0].shape[0], b.shape[0]
    n_p, n_r = len(pieces), (ride.n if ride else 0)
    o_spec = pl.BlockSpec((tm, tn), lambda i, j: (i, j))
    grid = (M // tm, N // tn)

    def body(*refs):
        b_ref, add_ref = refs[n_p:n_p + 2]
        r_ins, o_ref = refs[n_p + 2:n_p + 2 + n_r], refs[n_p + 2 + n_r]
        r_lnd, r_sems = refs[n_p + 3 + n_r:n_p + 3 + 2 * n_r], refs[n_p + 3 + 2 * n_r:]
        i, j = pl.program_id(0), pl.program_id(1)
        if ride:
            pl.when((i == 0) & (j == 0))(lambda: ride.start(r_ins, r_lnd, r_sems))
        acc, off = add_ref[...], 0
        for r in refs[:n_p]:
            acc = acc + _dot(r[...], b_ref[:, off:off + r.shape[1]], NT)
            off += r.shape[1]
        o_ref[...] = acc
        if ride:
            pl.when((i == grid[0] - 1) & (j == grid[1] - 1))(lambda: ride.finish(r_ins, r_lnd, r_sems))

    out = pl.pallas_call(
        body, name=name, grid=grid,
        in_specs=[pl.BlockSpec((tm, p.shape[1]), lambda i, j: (i, 0)) for p in pieces]
        + [pl.BlockSpec((tn, b.shape[1]), lambda i, j: (j, 0)), o_spec] + (ride.in_specs if ride else []),
        out_specs=[o_spec] + (ride.out_specs if ride else []),
        out_shape=[jax.ShapeDtypeStruct((M, N), f32)] + (ride.out_shape if ride else []),
        scratch_shapes=ride.scratch if ride else [],
        compiler_params=_params(("arbitrary", "arbitrary")),
    )(*pieces, b, add, *(ride.srcs if ride else []))
    return out[0], list(out[1:])


def _rms_fwd(x, g, *, name, out_dtype, residual=None, tm=512):
    S, C = x.shape
    tm = min(tm, S)
    has_res = residual is not None

    def body(x_ref, g_ref, *rest):
        xv = x_ref[...]
        r = lax.rsqrt(jnp.mean(xv * xv, axis=1, keepdims=True) + EPS)
        y = xv * r * g_ref[...]
        if has_res:
            y = y + rest[0][...]
        rest[-1][...] = y.astype(out_dtype)

    row = pl.BlockSpec((tm, C), lambda i: (i, 0))
    vec = pl.BlockSpec((1, C), lambda i: (0, 0))
    args = (x, g) + ((residual,) if has_res else ())
    return pl.pallas_call(
        body, name=name, grid=(S // tm,),
        in_specs=[row, vec] + ([row] if has_res else []),
        out_specs=row, out_shape=jax.ShapeDtypeStruct((S, C), out_dtype),
        compiler_params=_params(("parallel",)),
    )(*args)


def _rms_bwd(x, dy, g, *, name, out_dtype, add=None, tm=512):
    S, C = x.shape
    tm = min(tm, S)
    has_add = add is not None

    def body(x_ref, dy_ref, g_ref, *rest):
        dx_ref, dg_ref = rest[-2], rest[-1]
        xv = x_ref[...]
        dyv = dy_ref[...].astype(f32)
        r = lax.rsqrt(jnp.mean(xv * xv, axis=1, keepdims=True) + EPS)
        xh = xv * r
        dxh = dyv * g_ref[...]
        dx = r * (dxh - xh * jnp.mean(dxh * xh, axis=1, keepdims=True))
        if has_add:
            dx = dx + rest[0][...]
        dx_ref[...] = dx.astype(out_dtype)

        @pl.when(pl.program_id(0) == 0)
        def _():
            dg_ref[...] = jnp.zeros_like(dg_ref)

        dg_ref[...] += jnp.sum(dyv * xh, axis=0, keepdims=True)

    row = pl.BlockSpec((tm, C), lambda i: (i, 0))
    vec = pl.BlockSpec((1, C), lambda i: (0, 0))
    args = (x, dy, g) + ((add,) if has_add else ())
    return pl.pallas_call(
        body, name=name, grid=(S // tm,),
        in_specs=[row, row, vec] + ([row] if has_add else []),
        out_specs=[row, vec],
        out_shape=[jax.ShapeDtypeStruct((S, C), out_dtype), jax.ShapeDtypeStruct((1, C), f32)],
        compiler_params=_params(("arbitrary",)),
    )(*args)


SB_T = 128
SB_SPENT = -120.0
SB_QB = 8
SB_TAIL = 3
SB_GROUPS = (4, 2, 1)
SB_GROUPS_BWD = (4, 2, 1)


def _sb_masks():
    lane = lax.broadcasted_iota(jnp.int32, (1, 128), 1)
    m_a = (lane < SB_HD).astype(f32)
    return m_a, 1.0 - m_a


def _chunks(a, n):
    return [a[:, u * SB_T:(u + 1) * SB_T] for u in range(n)]


def _cat(parts, axis):
    return parts[0] if len(parts) == 1 else jnp.concatenate(parts, axis=axis)


def _mask_last(a, n, mask):
    if mask is None:
        return a
    parts = _chunks(a, n)
    return _cat(parts[:-1] + [jnp.where(mask, parts[-1], 0.0)], 1)


def _sb_logits(z, n, mask):
    l1p = jnp.log(1.0 + jnp.exp(-jnp.abs(z)))
    lb = jnp.minimum(z, 0.0) - l1p
    return lb, _mask_last(lb - z, n, mask)


def _by_count(i, most, fn):
    return lax.switch(jnp.minimum(i, most - 1), [functools.partial(fn, n) for n in range(1, most + 1)])


def _chunk_matmul(parts_list, u_mat):
    out = _dot(_cat(parts_list, 0), u_mat)
    return [out[u * SB_T:(u + 1) * SB_T] for u in range(len(parts_list))]


def _chunk_cumsum(lk, n, u_mat):
    hi = lk.astype(bf16)
    lo = (lk - hi.astype(f32)).astype(bf16)
    out = _chunk_matmul(_chunks(hi, n) + _chunks(lo, n), u_mat)
    return [out[u] + out[n + u] for u in range(n)]


def _sb_fwd(proj, S, ride=None):
    nq = S // SB_T
    n_pairs = D // 128
    scale = SB_HD ** -0.5
    n_r = ride.n if ride else 0

    def body(q_ref, k_ref, v_ref, *rest):
        o_ref, t_ref = rest[n_r:n_r + 2]
        step_i = pl.program_id(1)
        if ride:
            pl.when((pl.program_id(0) == 0) & (step_i == 0))(
                lambda: ride.start(rest[:n_r], rest[n_r + 2:2 * n_r + 2], rest[2 * n_r + 2:]))
        m_a, m_b = _sb_masks()
        r_i = lax.broadcasted_iota(jnp.int32, (SB_T, SB_T), 0)
        c_i = lax.broadcasted_iota(jnp.int32, (SB_T, SB_T), 1)
        u_mat = (r_i > c_i).astype(bf16)
        causal = c_i < r_i
        q_all = q_ref[...] * scale
        q_hs = [((q * m_a).astype(bf16), (q * m_b).astype(bf16))
                for q in (q_all[b * SB_T:(b + 1) * SB_T] for b in range(SB_QB))]

        def group(q_h, j_lo, n, carry, mask):
            acc, c_a, c_b = carry
            rows = pl.ds(pl.multiple_of(j_lo * SB_T, SB_T), n * SB_T)
            k = k_ref[rows, :].astype(bf16)
            v = v_ref[rows, :]
            zs = [_dot(q_b, k, NT) for q_b in q_h]
            lbk = [_sb_logits(z, n, mask) for z in zs]
            parts = [_chunk_cumsum(lk, n, u_mat) for _, lk in lbk]
            ws, cs = [], []
            for (lb, lk), part, c in zip(lbk, parts, (c_a, c_b)):
                lb_c, lk_c = _chunks(lb, n), _chunks(lk, n)
                w_c = [None] * n
                for u in reversed(range(n)):
                    w_c[u] = jnp.exp(lb_c[u] + c + part[u])
                    c = c + jnp.sum(lk_c[u], axis=1, keepdims=True)
                ws.append(_mask_last(_cat(w_c, 1), n, mask).astype(bf16))
                cs.append(c)
            for w, m in zip(ws, (m_a, m_b)):
                acc = acc + _dot(w, (v * m).astype(bf16))
            return acc, cs[0], cs[1]

        zero_c = jnp.zeros((SB_T, 1), f32)
        init = (jnp.zeros((SB_T, 128), f32), zero_c, zero_c)
        blocks = [(step_i * SB_QB + b, q_hs[b]) for b in range(SB_QB)]

        def whole_tails():
            return tuple(group(q_h, i - SB_TAIL + 1, SB_TAIL, init, causal) for i, q_h in blocks)

        def short_tails():
            return tuple(_by_count(i, SB_TAIL, functools.partial(
                lambda n, i, q_h: group(q_h, i - n + 1, n, init, causal), i=i, q_h=q_h)) for i, q_h in blocks)

        carries = lax.cond(step_i * SB_QB >= SB_TAIL - 1, whole_tails, short_tails)

        def spent(cr):
            return (jnp.max(jnp.maximum(cr[1], cr[2])) < SB_SPENT).astype(jnp.int32)

        lane = lax.broadcasted_iota(jnp.int32, (1, 128), 1)
        for b, ((i, q_h), carry) in enumerate(zip(blocks, carries)):
            state = (i - jnp.minimum(i, SB_TAIL - 1), spent(carry), carry)
            for n in SB_GROUPS:
                def step(st, n=n, q_h=q_h):
                    left, _, cr = st
                    cr = group(q_h, left - n, n, cr, None)
                    return left - n, spent(cr), cr

                state = lax.while_loop(lambda st, n=n: (st[0] >= n) & (st[1] == 0), step, state)
            left, _, carry = state
            rows = slice(b * SB_T, (b + 1) * SB_T)
            o_ref[rows, :] = carry[0]
            t_ref[rows, :] = (jnp.where(lane == 0, carry[1], 0.0) + jnp.where(lane == SB_HD, carry[2], 0.0)
                              + jnp.where(lane == 1, left.astype(f32), 0.0))
        if ride:
            pl.when((pl.program_id(0) == n_pairs - 1) & (step_i == nq // SB_QB - 1))(
                lambda: ride.finish(rest[:n_r], rest[n_r + 2:2 * n_r + 2], rest[2 * n_r + 2:]))

    qs = pl.BlockSpec((SB_QB * SB_T, 128), lambda h, i: (i, h))
    out = pl.pallas_call(
        body, name="sb_fwd", grid=(n_pairs, nq // SB_QB),
        in_specs=[qs,
                  pl.BlockSpec((S, 128), lambda h, i: (0, n_pairs + h)),
                  pl.BlockSpec((S, 128), lambda h, i: (0, 2 * n_pairs + h))] + (ride.in_specs if ride else []),
        out_specs=[qs, qs] + (ride.out_specs if ride else []),
        out_shape=[jax.ShapeDtypeStruct((S, D), f32)] * 2 + (ride.out_shape if ride else []),
        scratch_shapes=ride.scratch if ride else [],
        compiler_params=_params(("arbitrary", "arbitrary")),
    )(proj, proj, proj, *(ride.srcs if ride else []))
    return out[0], out[1], list(out[2:])


def _sb_bwd(proj, tot_lk, do, S, ride=None):
    nq = S // SB_T
    n_pairs = D // 128
    scale = SB_HD ** -0.5
    n_r = ride.n if ride else 0

    def body(q_ref, k_ref, v_ref, t_ref, do_ref, *rest):
        dq_ref, dk_ref, dv_ref = rest[n_r:n_r + 3]
        dk_acc, dv_acc = rest[2 * n_r + 3:2 * n_r + 5]
        r_ins, r_lnd, r_sems = rest[:n_r], rest[n_r + 3:2 * n_r + 3], rest[2 * n_r + 5:]
        step_i = pl.program_id(1)
        if ride:
            pl.when((pl.program_id(0) == 0) & (step_i == 0))(lambda: ride.start(r_ins, r_lnd, r_sems))
        m_a, m_b = _sb_masks()
        r_i = lax.broadcasted_iota(jnp.int32, (SB_T, SB_T), 0)
        c_i = lax.broadcasted_iota(jnp.int32, (SB_T, SB_T), 1)
        u_inc = (r_i <= c_i).astype(bf16)
        u_exc = (r_i < c_i).astype(bf16)
        causal = c_i < r_i

        @pl.when(step_i == 0)
        def _():
            dk_acc[...] = jnp.zeros_like(dk_acc)
            dv_acc[...] = jnp.zeros_like(dv_acc)

        lane = lax.broadcasted_iota(jnp.int32, (1, 128), 1)
        blocks = []
        for b in range(SB_QB):
            rows_b = slice(b * SB_T, (b + 1) * SB_T)
            i = step_i * SB_QB + b
            q = q_ref[rows_b, :] * scale
            dov = do_ref[rows_b, :]
            tv = t_ref[rows_b, :]
            heads = []
            for m, first in ((m_a, 0), (m_b, SB_HD)):
                tot = jnp.sum(jnp.where(lane == first, tv, 0.0), axis=1, keepdims=True)
                heads.append(((q * m).astype(bf16), (dov * m).astype(bf16), tot, m))
            lowest = jnp.clip(jnp.max(jnp.where(lane == 1, tv, 0.0)).astype(jnp.int32), 0, i)
            blocks.append((i, heads, lowest))

        def group(heads, j_lo, n, carry, mask):
            dq_acc, cp_a, cp_b, ce_a, ce_b = carry
            rows = pl.ds(pl.multiple_of(j_lo * SB_T, SB_T), n * SB_T)
            k_f = k_ref[rows, :]
            k = k_f.astype(bf16)
            v = v_ref[rows, :].astype(bf16)
            zs = [_dot(h[0], k, NT) for h in heads]
            dws = [_dot(h[1], v, NT) for h in heads]
            lbk = [_sb_logits(z, n, mask) for z in zs]
            parts = [_chunk_cumsum(lk, n, u_inc) for _, lk in lbk]
            ws, es, cps = [], [], []
            for (lb, lk), part, dw, h, cp in zip(lbk, parts, dws, heads, (cp_a, cp_b)):
                lb_c, lk_c = _chunks(lb, n), _chunks(lk, n)
                w_c = []
                for u in range(n):
                    w_c.append(jnp.exp(lb_c[u] + (h[2] - cp) - part[u]))
                    cp = cp + jnp.sum(lk_c[u], axis=1, keepdims=True)
                w = _mask_last(_cat(w_c, 1), n, mask)
                ws.append(w)
                es.append(dw * w)
                cps.append(cp)
            e_parts = [_chunk_matmul(_chunks(e.astype(bf16), n), u_exc) for e in es]
            dzs, ces = [], []
            for (lb, _), e, e_part, ce in zip(lbk, es, e_parts, (ce_a, ce_b)):
                e_c = _chunks(e, n)
                big_c = []
                for u in range(n):
                    big_c.append(ce + e_part[u])
                    ce = ce + jnp.sum(e_c[u], axis=1, keepdims=True)
                sig = jnp.exp(lb)
                dz = _mask_last(e * (1.0 - sig) - _cat(big_c, 1) * sig, n, mask)
                dzs.append(dz.astype(bf16))
                ces.append(ce)
            dk_t = jnp.zeros((n * SB_T, 128), f32)
            dv_t = jnp.zeros((n * SB_T, 128), f32)
            for dz_b, w, h in zip(dzs, ws, heads):
                dq_acc = dq_acc + _dot(dz_b, (k_f * h[3]).astype(bf16))
                dk_t = dk_t + _dot(dz_b, h[0], TN)
                dv_t = dv_t + _dot(w.astype(bf16), h[1], TN)
            dk_acc[rows, :] += dk_t
            dv_acc[rows, :] += dv_t
            return dq_acc, cps[0], cps[1], ces[0], ces[1]

        zc = jnp.zeros((SB_T, 1), f32)
        carries = []
        for i, heads, lowest in blocks:
            carry = (jnp.zeros((SB_T, 128), f32), zc, zc, zc, zc)
            done = lowest
            tail_lo = i - jnp.minimum(i, SB_TAIL - 1)
            for n in SB_GROUPS_BWD:
                trips = (tail_lo - done) // n
                carry = lax.fori_loop(
                    0, trips, functools.partial(
                        lambda gi, cr, n, done, heads: group(heads, done + gi * n, n, cr, None),
                        n=n, done=done, heads=heads),
                    carry)
                done = done + trips * n
            carries.append(carry)

        def whole_tails():
            return tuple(group(heads, i - SB_TAIL + 1, SB_TAIL, cr, causal)
                         for (i, heads, _), cr in zip(blocks, carries))

        def short_tails():
            return tuple(_by_count(i, SB_TAIL, functools.partial(
                lambda n, i, heads, cr: group(heads, i - n + 1, n, cr, causal), i=i, heads=heads, cr=cr))
                for (i, heads, _), cr in zip(blocks, carries))

        carries = lax.cond(step_i * SB_QB >= SB_TAIL - 1, whole_tails, short_tails)
        for b, carry in enumerate(carries):
            dq_ref[b * SB_T:(b + 1) * SB_T, :] = (carry[0] * scale).astype(bf16)

        @pl.when(step_i == nq // SB_QB - 1)
        def _():
            dk_ref[...] = dk_acc[...].astype(bf16)
            dv_ref[...] = dv_acc[...].astype(bf16)

        if ride:
            pl.when((pl.program_id(0) == n_pairs - 1) & (step_i == nq // SB_QB - 1))(
                lambda: ride.finish(r_ins, r_lnd, r_sems))

    qs = pl.BlockSpec((SB_QB * SB_T, 128), lambda h, i: (i, h))
    full = pl.BlockSpec((S, 128), lambda h, i: (0, h))
    out = pl.pallas_call(
        body, name="sb_bwd", grid=(n_pairs, nq // SB_QB),
        in_specs=[qs,
                  pl.BlockSpec((S, 128), lambda h, i: (0, n_pairs + h)),
                  pl.BlockSpec((S, 128), lambda h, i: (0, 2 * n_pairs + h)),
                  qs, qs] + (ride.in_specs if ride else []),
        out_specs=[qs, full, full] + (ride.out_specs if ride else []),
        out_shape=[jax.ShapeDtypeStruct((S, D), bf16)] * 3 + (ride.out_shape if ride else []),
        scratch_shapes=[pltpu.VMEM((S, 128), f32), pltpu.VMEM((S, 128), f32)] + (ride.scratch if ride else []),
        compiler_params=_params(("arbitrary", "arbitrary")),
    )(proj, proj, proj, tot_lk, do, *(ride.srcs if ride else []))
    return out[0], out[1], out[2], list(out[3:])


CONV_CB = 256
HALO = 8


def _conv_fwd(proj, conv_w, conv_b, S):
    tr = min(512, S)

    def body(x_ref, w_ref, b_ref, xc_ref, xbc_ref):
        w = w_ref[...]
        for t in range(S // tr):
            cur = x_ref[t * tr:(t + 1) * tr, :]
            halo = x_ref[t * tr - HALO:t * tr, :] if t else jnp.zeros((HALO, CONV_CB), f32)
            win = jnp.concatenate([halo, cur], axis=0)
            acc = b_ref[...] + w[CONV_K - 1:CONV_K, :] * cur
            for k in range(CONV_K - 1):
                acc = acc + w[k:k + 1, :] * pltpu.roll(win, CONV_K - 1 - k, 0)[HALO:, :]
            xc_ref[t * tr:(t + 1) * tr, :] = acc
            xbc_ref[t * tr:(t + 1) * tr, :] = acc * _sigmoid(acc)

    col = pl.BlockSpec((S, CONV_CB), lambda c: (0, c))
    return pl.pallas_call(
        body, name="conv_fwd", grid=(CONV_DIM // CONV_CB,),
        in_specs=[pl.BlockSpec((S, CONV_CB), lambda c: (0, P_XBC // CONV_CB + c)),
                  pl.BlockSpec((CONV_K, CONV_CB), lambda c: (0, c)),
                  pl.BlockSpec((1, CONV_CB), lambda c: (0, c))],
        out_specs=[col, col], out_shape=[jax.ShapeDtypeStruct((S, CONV_DIM), f32)] * 2,
        compiler_params=_params(("parallel",)),
    )(proj, conv_w, conv_b)


def _conv_bwd(proj, xc, dxbc, conv_w, S):
    tr = min(512, S)

    def body(x_ref, xc_ref, dy_ref, w_ref, dx_ref, dw_ref, db_ref, dxc_s):
        w = w_ref[...]
        xcv = xc_ref[...]
        sg = _sigmoid(xcv)
        dxc_s[0:S, :] = dy_ref[...] * (sg * (1.0 + xcv * (1.0 - sg)))
        dxc_s[S:S + HALO, :] = jnp.zeros((HALO, CONV_CB), f32)
        dws = [jnp.zeros((1, CONV_CB), f32) for _ in range(CONV_K)]
        db = jnp.zeros((1, CONV_CB), f32)
        for t in range(S // tr):
            cur = x_ref[t * tr:(t + 1) * tr, :]
            halo = x_ref[t * tr - HALO:t * tr, :] if t else jnp.zeros((HALO, CONV_CB), f32)
            win = jnp.concatenate([halo, cur], axis=0)
            dwin = dxc_s[t * tr:(t + 1) * tr + HALO, :]
            dcur = dwin[0:tr, :]
            db = db + jnp.sum(dcur, axis=0, keepdims=True)
            dws[CONV_K - 1] = dws[CONV_K - 1] + jnp.sum(dcur * cur, axis=0, keepdims=True)
            dx = w[CONV_K - 1:CONV_K, :] * dcur
            for k in range(CONV_K - 1):
                sh = CONV_K - 1 - k
                dws[k] = dws[k] + jnp.sum(dcur * pltpu.roll(win, sh, 0)[HALO:, :], axis=0, keepdims=True)
                dx = dx + w[k:k + 1, :] * pltpu.roll(dwin, tr + HALO - sh, 0)[0:tr, :]
            dx_ref[t * tr:(t + 1) * tr, :] = dx.astype(bf16)
        dw_ref[...] = jnp.concatenate(dws + [jnp.zeros((8 - CONV_K, CONV_CB), f32)], axis=0)
        db_ref[...] = db

    col = pl.BlockSpec((S, CONV_CB), lambda c: (0, c))
    return pl.pallas_call(
        body, name="conv_bwd", grid=(CONV_DIM // CONV_CB,),
        in_specs=[pl.BlockSpec((S, CONV_CB), lambda c: (0, P_XBC // CONV_CB + c)), col, col,
                  pl.BlockSpec((CONV_K, CONV_CB), lambda c: (0, c))],
        out_specs=[col, pl.BlockSpec((8, CONV_CB), lambda c: (0, c)), pl.BlockSpec((1, CONV_CB), lambda c: (0, c))],
        out_shape=[jax.ShapeDtypeStruct((S, CONV_DIM), bf16), jax.ShapeDtypeStruct((8, CONV_DIM), f32),
                   jax.ShapeDtypeStruct((1, CONV_DIM), f32)],
        scratch_shapes=[pltpu.VMEM((S + HALO, CONV_CB), f32)],
        compiler_params=_params(("parallel",)),
    )(proj, xc, dxbc, conv_w)


N_PAIR = SSD_HEADS // 2
NEG = -1e30


def _softplus(x):
    return jnp.maximum(x, 0.0) + jnp.log(1.0 + jnp.exp(-jnp.abs(x)))


def _ssd_common(dtr, dtb, alog):
    L = SSD_L
    r_i = lax.broadcasted_iota(jnp.int32, (L, L), 0)
    c_i = lax.broadcasted_iota(jnp.int32, (L, L), 1)
    dt = _softplus(dtr + dtb)
    a = -jnp.exp(alog)
    da = dt * a
    lower = (r_i >= c_i).astype(bf16)
    upper = (r_i <= c_i).astype(bf16)
    parts = _split3(da)
    a_cs = sum(_dot(lower, p) for p in parts)
    a_cs_t = sum(_dot(p, upper, TN) for p in parts)
    return dt, a, a_cs, a_cs_t, r_i >= c_i


def _pair_vec(lane, v, h):
    return jnp.where(lane < SB_HD, v[:, h:h + 1], v[:, h + 1:h + 2])


def _decay_mat(a_cs, a_cs_t, h, tril):
    return jnp.exp(jnp.where(tril, a_cs[:, h:h + 1] - a_cs_t[h:h + 1, :], NEG))


def _ssd_fwd(xbc, proj, pdt, dt_bias_p, a_log_p, d_skip_c, ssd_norm, S, ride=None):
    L = SSD_L
    nc = S // L
    n_r = ride.n if ride else 0

    def body(xbc_ref, dt_ref, z_ref, dtb_ref, alog_ref, dsk_ref, gn_ref, *rest):
        y_ref, yn_ref, hp_ref = rest[n_r:n_r + 3]
        state = rest[2 * n_r + 3]
        r_ins, r_lnd, r_sems = rest[:n_r], rest[n_r + 3:2 * n_r + 3], rest[2 * n_r + 4:]
        c = pl.program_id(0)
        if ride:
            pl.when(c == 0)(lambda: ride.start(r_ins, r_lnd, r_sems))

        @pl.when(c == 0)
        def _():
            state[...] = jnp.zeros_like(state)

        hp_ref[0] = state[...]
        lane = lax.broadcasted_iota(jnp.int32, (1, 128), 1)
        row128 = lax.broadcasted_iota(jnp.int32, (128, 1), 0)
        m_a, m_b = _sb_masks()
        dt, a, a_cs, a_cs_t, tril = _ssd_common(dt_ref[...], dtb_ref[...], alog_ref[...])
        a_last = a_cs[L - 1:L, :]
        for g in range(SSD_GROUPS):
            b_g = xbc_ref[:, SSD_INNER + g * SSD_N:SSD_INNER + (g + 1) * SSD_N].astype(bf16)
            c_g = xbc_ref[:, SSD_INNER + (SSD_GROUPS + g) * SSD_N:SSD_INNER + (SSD_GROUPS + g + 1) * SSD_N].astype(bf16)
            cb = _dot(c_g, b_g, NT)
            for pr in range(4):
                h = 8 * g + 2 * pr
                pi = h // 2
                cols = slice(pi * 128, (pi + 1) * 128)
                xs = xbc_ref[:, cols]
                x = xs * _pair_vec(lane, dt, h)
                acs = _pair_vec(lane, a_cs, h)
                al = _pair_vec(lane, a_last, h)
                w_a = (cb * _decay_mat(a_cs, a_cs_t, h, tril)).astype(bf16)
                w_b = (cb * _decay_mat(a_cs, a_cs_t, h + 1, tril)).astype(bf16)
                yd = _dot(w_a, (x * m_a).astype(bf16)) + _dot(w_b, (x * m_b).astype(bf16))
                hp = state[pi]
                yo = _dot(c_g, hp.astype(bf16), NT) * jnp.exp(acs)
                y_ref[:, cols] = yd + yo + dsk_ref[:, cols] * xs
                dec = jnp.exp(jnp.where(row128 < SB_HD, a_last[:, h:h + 1], a_last[:, h + 1:h + 2]))
                state[pi] = hp * dec + _dot((x * jnp.exp(al - acs)).astype(bf16), b_g, TN)
        zz = z_ref[...]
        y2 = y_ref[...] * (zz * _sigmoid(zz))
        gw = SSD_INNER // SSD_GROUPS
        for g in range(SSD_GROUPS):
            yg = y2[:, g * gw:(g + 1) * gw]
            rg = lax.rsqrt(jnp.mean(yg * yg, axis=1, keepdims=True) + EPS)
            yn_ref[:, g * gw:(g + 1) * gw] = (yg * rg * gn_ref[:, g * gw:(g + 1) * gw]).astype(bf16)
        if ride:
            pl.when(c == nc - 1)(lambda: ride.finish(r_ins, r_lnd, r_sems))

    vec128 = pl.BlockSpec((1, 128), lambda c: (0, 0))
    vecin = pl.BlockSpec((1, SSD_INNER), lambda c: (0, 0))
    rows = pl.BlockSpec((L, SSD_INNER), lambda c: (c, 0))
    out = pl.pallas_call(
        body, name="ssd_fwd", grid=(nc,),
        in_specs=[pl.BlockSpec((L, CONV_DIM), lambda c: (c, 0)),
                  pl.BlockSpec((L, 128), lambda c: (c, 0)),
                  pl.BlockSpec((L, SSD_INNER), lambda c: (c, P_Z // SSD_INNER)),
                  vec128, vec128, vecin, vecin] + (ride.in_specs if ride else []),
        out_specs=[rows, rows, pl.BlockSpec((1, N_PAIR, 128, SSD_N), lambda c: (c, 0, 0, 0))]
        + (ride.out_specs if ride else []),
        out_shape=[jax.ShapeDtypeStruct((S, SSD_INNER), f32), jax.ShapeDtypeStruct((S, SSD_INNER), bf16),
                   jax.ShapeDtypeStruct((nc, N_PAIR, 128, SSD_N), f32)] + (ride.out_shape if ride else []),
        scratch_shapes=[pltpu.VMEM((N_PAIR, 128, SSD_N), f32)] + (ride.scratch if ride else []),
        compiler_params=_params(("arbitrary",)),
    )(xbc, pdt, proj, dt_bias_p, a_log_p, d_skip_c, ssd_norm, *(ride.srcs if ride else []))
    return out[0], out[1], out[2], list(out[3:])


def _sum_all(v):
    return jnp.sum(jnp.sum(v, axis=1, keepdims=True), axis=0, keepdims=True)


def _ssd_bwd(dyn, y, xbc, proj, pdt, hprev, dt_bias_p, a_log_p, d_skip_c, ssd_norm, S, ride=None):
    L = SSD_L
    nc = S // L
    n_r = ride.n if ride else 0

    col = lax.broadcasted_iota(jnp.int32, (2 * SSD_INNER, 128), 0)
    head = lax.broadcasted_iota(jnp.int32, (2 * SSD_INNER, 128), 1)
    sel_pair = (col[:SSD_INNER] // SB_HD == head[:SSD_INNER]).astype(bf16)
    sel_head = (col // 128 == head).astype(bf16)

    def body(*refs):
        (dyn_ref, y_ref, xbc_ref, dt_ref, z_ref, hp_ref, dtb_ref, alog_ref, dsk_ref, gn_ref,
         selp_ref, selh_ref) = refs[:12]
        dz_ref, dxbc_ref, ddt_ref, dgn_ref, dsk_out, dalog_ref, ddtb_ref = refs[12 + n_r:19 + n_r]
        dstate, dy_s, st_a, st_q, st_d, st_x, dat = refs[19 + 2 * n_r:26 + 2 * n_r]
        r_ins, r_lnd, r_sems = refs[12:12 + n_r], refs[19 + n_r:19 + 2 * n_r], refs[26 + 2 * n_r:]
        c = pl.program_id(0)
        if ride:
            pl.when(c == 0)(lambda: ride.start(r_ins, r_lnd, r_sems))

        @pl.when(c == 0)
        def _():
            dat[...] = jnp.zeros_like(dat)
            dstate[...] = jnp.zeros_like(dstate)
            dgn_ref[...] = jnp.zeros_like(dgn_ref)
            dsk_out[...] = jnp.zeros_like(dsk_out)
            dalog_ref[...] = jnp.zeros_like(dalog_ref)
            ddtb_ref[...] = jnp.zeros_like(ddtb_ref)

        lane = lax.broadcasted_iota(jnp.int32, (1, 128), 1)
        row128 = lax.broadcasted_iota(jnp.int32, (128, 1), 0)
        rowl = lax.broadcasted_iota(jnp.int32, (L, 1), 0)
        m_a, m_b = _sb_masks()
        dtr = dt_ref[...]
        dt, a, a_cs, a_cs_t, tril = _ssd_common(dtr, dtb_ref[...], alog_ref[...])
        a_last = a_cs[L - 1:L, :]

        zz = z_ref[...]
        sg = _sigmoid(zz)
        silu = zz * sg
        yv = y_ref[...]
        y2 = yv * silu
        gw = SSD_INNER // SSD_GROUPS
        for g in range(SSD_GROUPS):
            sl = slice(g * gw, (g + 1) * gw)
            yg = y2[:, sl]
            rg = lax.rsqrt(jnp.mean(yg * yg, axis=1, keepdims=True) + EPS)
            yh = yg * rg
            dyn_g = dyn_ref[:, sl]
            dgn_ref[:, sl] += jnp.sum(dyn_g * yh, axis=0, keepdims=True)
            dyh = dyn_g * gn_ref[:, sl]
            dy2 = rg * (dyh - yh * jnp.mean(dyh * yh, axis=1, keepdims=True))
            dy_s[:, sl] = dy2 * silu[:, sl]
            dz_ref[:, sl] = (dy2 * yv[:, sl] * (sg[:, sl] * (1.0 + zz[:, sl] * (1.0 - sg[:, sl])))).astype(bf16)

        last_row = jnp.zeros((1, 128), f32)
        dsk_acc = jnp.zeros((1, 128), f32)
        for g in range(SSD_GROUPS):
            bsl = slice(SSD_INNER + g * SSD_N, SSD_INNER + (g + 1) * SSD_N)
            csl = slice(SSD_INNER + (SSD_GROUPS + g) * SSD_N, SSD_INNER + (SSD_GROUPS + g + 1) * SSD_N)
            b_g = xbc_ref[:, bsl].astype(bf16)
            c_g = xbc_ref[:, csl].astype(bf16)
            cb = _dot(c_g, b_g, NT)
            dcb = jnp.zeros((L, L), f32)
            dc_g = jnp.zeros((L, SSD_N), f32)
            db_g = jnp.zeros((L, SSD_N), f32)
            for pr in range(4):
                h = 8 * g + 2 * pr
                pi = h // 2
                cols = slice(pi * 128, (pi + 1) * 128)
                xs = xbc_ref[:, cols]
                dt_p = _pair_vec(lane, dt, h)
                x = xs * dt_p
                acs = _pair_vec(lane, a_cs, h)
                al = _pair_vec(lane, a_last, h)
                e_a = jnp.exp(acs)
                dte = jnp.exp(al - acs)
                m_mat_a = _decay_mat(a_cs, a_cs_t, h, tril)
                m_mat_b = _decay_mat(a_cs, a_cs_t, h + 1, tril)
                dyp = dy_s[:, cols]
                dsk = dsk_ref[:, cols]
                d_hn = dstate[pi]
                hp = hp_ref[0, pi]
                dy_a = (dyp * m_a).astype(bf16)
                dy_b = (dyp * m_b).astype(bf16)
                x_b = x.astype(bf16)
                gm_a = _dot(dy_a, x_b, NT) * m_mat_a
                gm_b = _dot(dy_b, x_b, NT) * m_mat_b
                dcb = dcb + gm_a + gm_b
                dx_d = _dot((cb * m_mat_a).astype(bf16), dy_a, TN) + _dot((cb * m_mat_b).astype(bf16), dy_b, TN)
                dx_s = _dot(b_g, d_hn.astype(bf16), NT) * dte
                dx = dx_d + dx_s
                dxbc_ref[:, cols] = dx * dt_p + dsk * dyp
                xdxs = x * dx_s
                st_x[:, cols] = xdxs
                st_a[:, cols] = dyp * (_dot(c_g, hp.astype(bf16), NT) * e_a) - xdxs
                st_d[:, cols] = dx * xs
                hh = d_hn * hp
                dsk_row = jnp.sum(dyp * xs, axis=0, keepdims=True)
                dec = jnp.exp(jnp.where(row128 < SB_HD, a_last[:, h:h + 1], a_last[:, h + 1:h + 2]))
                for hd, m, gm in ((h, m_a, gm_a), (h + 1, m_b, gm_b)):
                    half = slice(0, SB_HD) if hd == h else slice(SB_HD, 128)
                    qm = gm * cb
                    st_q[:, hd * 128:(hd + 1) * 128] = qm
                    dat[hd:hd + 1, :] = jnp.sum(qm, axis=0, keepdims=True)
                    hh_sum = jnp.sum(jnp.sum(hh[half, :], axis=0, keepdims=True), axis=1, keepdims=True)
                    last_row = jnp.where(lane == hd, jnp.exp(a_last[:, hd:hd + 1]) * hh_sum, last_row)
                    dsk_acc = jnp.where(lane == hd, jnp.sum(dsk_row * m, axis=1, keepdims=True), dsk_acc)
                dye = (dyp * e_a).astype(bf16)
                dc_g = dc_g + _dot(dye, hp.astype(bf16))
                db_g = db_g + _dot((x * dte).astype(bf16), d_hn.astype(bf16))
                dstate[pi] = dec * d_hn + _dot(dye, c_g, TN)
            dcb_b = dcb.astype(bf16)
            dxbc_ref[:, csl] = dc_g + _dot(dcb_b, b_g)
            dxbc_ref[:, bsl] = db_g + _dot(dcb_b, c_g, TN)

        r_i = lax.broadcasted_iota(jnp.int32, (L, L), 0)
        c_i = lax.broadcasted_iota(jnp.int32, (L, L), 1)
        rev = (r_i <= c_i).astype(bf16)

        def head_sums(st, sel, split=_split2):
            return sum(_dot(p, sel[...]) for p in split(st[...]))

        last_row = last_row + jnp.sum(head_sums(st_x, selp_ref), axis=0, keepdims=True)
        d_acs = (head_sums(st_a, selp_ref) + head_sums(st_q, selh_ref, _split3)
                 + jnp.where(rowl == L - 1, last_row, 0.0))
        ddt_x = head_sums(st_d, selp_ref)
        dda = sum(_dot(rev, p) for p in _split3(d_acs)) - sum(_dot(rev, p, NT) for p in _split3(dat[...]))
        ddt = ddt_x + dda * a
        dalog_ref[...] += jnp.sum(dda * dt, axis=0, keepdims=True) * a
        ddtr = jnp.where(lane < SSD_HEADS, ddt * _sigmoid(dtr + dtb_ref[...]), 0.0)
        ddt_ref[...] = ddtr.astype(bf16)
        ddtb_ref[...] += jnp.sum(ddtr, axis=0, keepdims=True)
        dsk_out[...] += dsk_acc
        if ride:
            pl.when(c == nc - 1)(lambda: ride.finish(r_ins, r_lnd, r_sems))

    rv = lambda c: nc - 1 - c
    vec128 = pl.BlockSpec((1, 128), lambda c: (0, 0))
    vecin = pl.BlockSpec((1, SSD_INNER), lambda c: (0, 0))
    rows = pl.BlockSpec((L, SSD_INNER), lambda c: (rv(c), 0))
    return pl.pallas_call(
        body, name="ssd_bwd", grid=(nc,),
        in_specs=[rows, rows,
                  pl.BlockSpec((L, CONV_DIM), lambda c: (rv(c), 0)),
                  pl.BlockSpec((L, 128), lambda c: (rv(c), 0)),
                  pl.BlockSpec((L, SSD_INNER), lambda c: (rv(c), P_Z // SSD_INNER)),
                  pl.BlockSpec((1, N_PAIR, 128, SSD_N), lambda c: (rv(c), 0, 0, 0)),
                  vec128, vec128, vecin, vecin,
                  pl.BlockSpec((SSD_INNER, 128), lambda c: (0, 0)),
                  pl.BlockSpec((2 * SSD_INNER, 128), lambda c: (0, 0))] + (ride.in_specs if ride else []),
        out_specs=[rows, pl.BlockSpec((L, CONV_DIM), lambda c: (rv(c), 0)),
                   pl.BlockSpec((L, 128), lambda c: (rv(c), 0)), vecin, vec128, vec128, vec128]
        + (ride.out_specs if ride else []),
        out_shape=[jax.ShapeDtypeStruct((S, SSD_INNER), bf16), jax.ShapeDtypeStruct((S, CONV_DIM), f32),
                   jax.ShapeDtypeStruct((S, 128), bf16), jax.ShapeDtypeStruct((1, SSD_INNER), f32),
                   jax.ShapeDtypeStruct((1, 128), f32), jax.ShapeDtypeStruct((1, 128), f32),
                   jax.ShapeDtypeStruct((1, 128), f32)] + (ride.out_shape if ride else []),
        scratch_shapes=[pltpu.VMEM((N_PAIR, 128, SSD_N), f32), pltpu.VMEM((L, SSD_INNER), f32),
                        pltpu.VMEM((L, SSD_INNER), f32), pltpu.VMEM((L, 2 * SSD_INNER), f32),
                        pltpu.VMEM((L, SSD_INNER), f32), pltpu.VMEM((L, SSD_INNER), f32),
                        pltpu.VMEM((128, L), f32)]
        + (ride.scratch if ride else []),
        compiler_params=_params(("arbitrary",)),
    )(dyn, y, xbc, pdt, proj, hprev, dt_bias_p, a_log_p, d_skip_c, ssd_norm, sel_pair, sel_head,
      *(ride.srcs if ride else []))


MEM_W = MEM_HEADS * MEM_HD


def _mem_probs(q, k):
    s = _dot(q, k, NT) * (MEM_HD ** -0.5)
    s = s - jnp.max(s, axis=1, keepdims=True)
    p = jnp.exp(s)
    return p / jnp.sum(p, axis=1, keepdims=True)


def _mem_fwd(proj, kv, S, tm=512):
    tm = min(tm, S)
    M = kv.shape[0]

    def body(q_ref, kv_ref, o_ref):
        for h in range(MEM_HEADS):
            sl = slice(h * MEM_HD, (h + 1) * MEM_HD)
            vsl = slice(MEM_W + h * MEM_HD, MEM_W + (h + 1) * MEM_HD)
            p = _mem_probs(q_ref[:, sl].astype(bf16), kv_ref[:, sl].astype(bf16))
            o_ref[:, sl] = _dot(p.astype(bf16), kv_ref[:, vsl].astype(bf16)).astype(bf16)

    return pl.pallas_call(
        body, name="mem_fwd", grid=(S // tm,),
        in_specs=[pl.BlockSpec((tm, MEM_W), lambda i: (i, P_MEMQ // MEM_W)),
                  pl.BlockSpec((M, 2 * MEM_W), lambda i: (0, 0))],
        out_specs=pl.BlockSpec((tm, MEM_W), lambda i: (i, 0)),
        out_shape=jax.ShapeDtypeStruct((S, MEM_W), bf16),
        compiler_params=_params(("parallel",)),
    )(proj, kv)


def _mem_bwd(proj, kv, dy, S, tm=512):
    tm = min(tm, S)
    M = kv.shape[0]
    scale = MEM_HD ** -0.5

    def body(q_ref, kv_ref, dy_ref, dq_ref, dkv_ref):
        @pl.when(pl.program_id(0) == 0)
        def _():
            dkv_ref[...] = jnp.zeros_like(dkv_ref)

        for h in range(MEM_HEADS):
            sl = slice(h * MEM_HD, (h + 1) * MEM_HD)
            vsl = slice(MEM_W + h * MEM_HD, MEM_W + (h + 1) * MEM_HD)
            q = q_ref[:, sl].astype(bf16)
            k = kv_ref[:, sl].astype(bf16)
            v = kv_ref[:, vsl].astype(bf16)
            dyh = dy_ref[:, sl].astype(bf16)
            p = _mem_probs(q, k)
            dp = _dot(dyh, v, NT)
            ds = (p * (dp - jnp.sum(dp * p, axis=1, keepdims=True)) * scale).astype(bf16)
            dq_ref[:, sl] = _dot(ds, k).astype(bf16)
            dkv_ref[:, sl] += _dot(ds, q, TN)
            dkv_ref[:, vsl] += _dot(p.astype(bf16), dyh, TN)

    return pl.pallas_call(
        body, name="mem_bwd", grid=(S // tm,),
        in_specs=[pl.BlockSpec((tm, MEM_W), lambda i: (i, P_MEMQ // MEM_W)),
                  pl.BlockSpec((M, 2 * MEM_W), lambda i: (0, 0)),
                  pl.BlockSpec((tm, MEM_W), lambda i: (i, 0))],
        out_specs=[pl.BlockSpec((tm, MEM_W), lambda i: (i, 0)), pl.BlockSpec((M, 2 * MEM_W), lambda i: (0, 0))],
        out_shape=[jax.ShapeDtypeStruct((S, MEM_W), bf16), jax.ShapeDtypeStruct((M, 2 * MEM_W), f32)],
        compiler_params=_params(("arbitrary",)),
    )(proj, kv, dy)


def _merge_fwd(proj, t0, t1, t2, S, tm=512):
    tm = min(tm, S)

    def body(g_ref, t0_ref, t1_ref, t2_ref, o_ref):
        acc = jnp.zeros((tm, D), f32)
        for b, t_ref in enumerate((t0_ref, t1_ref, t2_ref)):
            acc = acc + _sigmoid(g_ref[:, b * D:(b + 1) * D]) * t_ref[...]
        o_ref[...] = acc.astype(bf16)

    row = pl.BlockSpec((tm, D), lambda i: (i, 0))
    return pl.pallas_call(
        body, name="merge_fwd", grid=(S // tm,),
        in_specs=[pl.BlockSpec((tm, 3 * D), lambda i: (i, P_GATE // (3 * D))), row, row, row],
        out_specs=row, out_shape=jax.ShapeDtypeStruct((S, D), bf16),
        compiler_params=_params(("parallel",)),
    )(proj, t0, t1, t2)


def _merge_bwd(proj, t0, t1, t2, dm, S, tm=512):
    tm = min(tm, S)

    def body(g_ref, t0_ref, t1_ref, t2_ref, dm_ref, d0_ref, d1_ref, d2_ref, dg_ref):
        dmv = dm_ref[...]
        for b, (t_ref, d_ref) in enumerate(((t0_ref, d0_ref), (t1_ref, d1_ref), (t2_ref, d2_ref))):
            sg = _sigmoid(g_ref[:, b * D:(b + 1) * D])
            d_ref[...] = (dmv * sg).astype(bf16)
            dg_ref[:, b * D:(b + 1) * D] = (dmv * t_ref[...] * sg * (1.0 - sg)).astype(bf16)

    row = pl.BlockSpec((tm, D), lambda i: (i, 0))
    return pl.pallas_call(
        body, name="merge_bwd", grid=(S // tm,),
        in_specs=[pl.BlockSpec((tm, 3 * D), lambda i: (i, P_GATE // (3 * D))), row, row, row, row],
        out_specs=[row, row, row, pl.BlockSpec((tm, 3 * D), lambda i: (i, 0))],
        out_shape=[jax.ShapeDtypeStruct((S, D), bf16)] * 3 + [jax.ShapeDtypeStruct((S, 3 * D), bf16)],
        compiler_params=_params(("parallel",)),
    )(proj, t0, t1, t2, dm)


def _loss_head(ff, g, h1, target, S, tm=512):
    tm = min(tm, S)

    def body(ff_ref, g_ref, h1_ref, t_ref, dh_ref, loss_ref):
        xv = ff_ref[...]
        r = lax.rsqrt(jnp.mean(xv * xv, axis=1, keepdims=True) + EPS)
        err = h1_ref[...] + xv * r * g_ref[...] - t_ref[...]
        dh_ref[...] = err * (1.0 / D)

        @pl.when(pl.program_id(0) == 0)
        def _():
            loss_ref[...] = jnp.zeros_like(loss_ref)

        loss_ref[...] += 0.5 * _sum_all(jnp.mean(err * err, axis=1, keepdims=True)) * jnp.ones((1, 128), f32)

    row = pl.BlockSpec((tm, D), lambda i: (i, 0))
    return pl.pallas_call(
        body, name="loss_head", grid=(S // tm,),
        in_specs=[row, pl.BlockSpec((1, D), lambda i: (0, 0)), row, row],
        out_specs=[row, pl.BlockSpec((1, 128), lambda i: (0, 0))],
        out_shape=[jax.ShapeDtypeStruct((S, D), f32), jax.ShapeDtypeStruct((1, 128), f32)],
        compiler_params=_params(("arbitrary",)),
    )(ff, g, h1, target)


def _local_step(x, mem, target, wts, late_rides, late_weights, small, rest_rides, w_in_ride):
    S = x.shape[0]
    M = mem.shape[0]
    pad = lambda v: jnp.pad(v, ((0, 0), (0, 128 - SSD_HEADS)))
    dtb_p, alog_p = pad(small["dt_bias"]), pad(small["a_log"])
    dsk_c = jnp.repeat(small["d_skip"], SB_HD, axis=1)

    u = _rms_fwd(x, small["norm_mix_pre"], name="norm_pre", out_dtype=bf16)
    rides = late_rides or (None, None, None)
    if late_rides:
        proj, lands_a = _mm(u, wts["w_main"], "nn", tm=1024, tn=1024, name="in_proj", ride=rides[0])
    else:
        proj, lands_a = _mm(u, wts["w_main"], "nn", tm=1024, tn=1024, name="in_proj"), []
    pdt = _mm(u, wts["w_dt"], "nn", tm=1024, tn=128, name="in_proj_dt")
    y_sb, tot_lk, lands_b = _sb_fwd(proj, S, rides[1])
    wts = dict(wts, **late_weights(0, lands_a))
    small = dict(small, conv_w=wts.pop("conv_w"))
    xc, xbc = _conv_fwd(proj, small["conv_w"], small["conv_b"], S)
    y_ssd, yn, hprev, lands_c = _ssd_fwd(xbc, proj, pdt, dtb_p, alog_p, dsk_c, small["ssd_norm"], S, rides[2])
    wts = dict(wts, **late_weights(1, lands_b), **late_weights(2, lands_c))
    mn = _rms_fwd(mem, small["norm_mem"], name="norm_mem", out_dtype=bf16, tm=min(512, M))
    kv = _mm(mn, wts["w_mem_kv"], "nn", tm=M, tn=1024, name="mem_kv")
    y_mem = _mem_fwd(proj, kv, S)
    t0 = _mm(y_sb, wts["w_sb_out"], "nn", tm=1024, tn=1024, name="sb_out")
    t1 = _mm(yn, wts["w_ssd_out"], "nn", tm=1024, tn=1024, name="ssd_out")
    t2 = _mm(y_mem, wts["w_mem_out"], "nn", tm=1024, tn=1024, name="mem_out")
    merged = _merge_fwd(proj, t0, t1, t2, S)
    mix = _mm(merged, wts["w_o"], "nn", tm=1024, tn=1024, name="w_o")
    h1 = _rms_fwd(mix, small["norm_mix_post"], name="norm_mix_post", out_dtype=f32, residual=x)
    u2 = _rms_fwd(h1, small["norm_mlp_pre"], name="norm_mlp_pre", out_dtype=bf16)
    a_up, hrelu = _mm(u2, wts["w_up"], "nn", tm=1024, tn=1024, name="mlp_up", out_dtypes=(f32, bf16),
                      epi=lambda acc: (acc, jnp.square(jnp.maximum(acc, 0.0))))
    ff = _mm(hrelu, wts["w_down"], "nn", tm=1024, tn=1024, name="mlp_down")
    dh2, loss = _loss_head(ff, small["norm_mlp_post"], h1, target, S)

    g = {}
    dff, g["norm_mlp_post"] = _rms_bwd(ff, dh2, small["norm_mlp_post"], name="norm_mlp_post_bwd", out_dtype=bf16)
    da = _mm(dff, wts["w_down"], "nt", tm=1024, tn=1024, name="mlp_down_dx", out_dtypes=(bf16,),
             epi=lambda acc, a: (acc * (2.0 * jnp.maximum(a, 0.0)),), extras=(a_up,))
    g["w_down"] = _mm(hrelu, dff, "tn", tm=1024, tn=1024, name="mlp_down_dw")
    du2 = _mm(da, wts["w_up"], "nt", tm=1024, tn=1024, name="mlp_up_dx")
    g["w_up"] = _mm(u2, da, "tn", tm=1024, tn=1024, name="mlp_up_dw")
    dh1, g["norm_mlp_pre"] = _rms_bwd(h1, du2, small["norm_mlp_pre"], name="norm_mlp_pre_bwd", out_dtype=f32, add=dh2)
    dmix, g["norm_mix_post"] = _rms_bwd(mix, dh1, small["norm_mix_post"], name="norm_mix_post_bwd", out_dtype=bf16)
    dmerged = _mm(dmix, wts["w_o"], "nt", tm=1024, tn=1024, name="w_o_dx")
    g["w_o"] = _mm(merged, dmix, "tn", tm=1024, tn=1024, name="w_o_dw")
    dt0, dt1, dt2, dgl = _merge_bwd(proj, t0, t1, t2, dmerged, S)
    dy_sb = _mm(dt0, wts["w_sb_out"], "nt", tm=1024, tn=1024, name="sb_out_dx")
    g["w_sb_out"] = _mm(y_sb, dt0, "tn", tm=1024, tn=1024, name="sb_out_dw")
    dy_ssd = _mm(dt1, wts["w_ssd_out"], "nt", tm=1024, tn=1024, name="ssd_out_dx")
    g["w_ssd_out"] = _mm(yn, dt1, "tn", tm=1024, tn=1024, name="ssd_out_dw")
    dy_mem = _mm(dt2, wts["w_mem_out"], "nt", tm=1024, tn=1024, name="mem_out_dx")
    g["w_mem_out"] = _mm(y_mem, dt2, "tn", tm=1024, tn=1024, name="mem_out_dw")
    dmemq, dkv = _mem_bwd(proj, kv, dy_mem, S)
    g["w_mem_kv"] = _mm(mn, dkv, "tn", tm=1024, tn=1024, name="mem_kv_dw")
    dmn = _mm(dkv, wts["w_mem_kv"], "nt", tm=M, tn=1024, name="mem_kv_dx")
    _, g["norm_mem"] = _rms_bwd(mem, dmn, small["norm_mem"], name="norm_mem_bwd", out_dtype=bf16, tm=min(512, M))
    rides = rest_rides(g) if rest_rides else (None, None)
    dz, dxbc, ddt, g["ssd_norm"], dsk, dalog, ddtb, *lands_a = _ssd_bwd(
        dy_ssd, y_ssd, xbc, proj, pdt, hprev, dtb_p, alog_p, dsk_c, small["ssd_norm"], S, rides[0])
    g["d_skip"], g["a_log"], g["dt_bias"] = dsk[:, :SSD_HEADS], dalog[:, :SSD_HEADS], ddtb[:, :SSD_HEADS]
    dxbc_raw, dcw, g["conv_b"] = _conv_bwd(proj, xc, dxbc, small["conv_w"], S)
    g["conv_w"] = dcw[:CONV_K]
    dq, dk, dv, lands_b = _sb_bwd(proj, tot_lk, dy_sb, S, rides[1])
    g["rest_lands"] = lands_b + lands_a
    dproj = (dq, dk, dv, dxbc_raw, dgl, dmemq, dz)
    u_t = u.T
    g["w_main"] = [_mm(u_t, p, "nn", tm=512, tn=1024, name="in_proj_dw_%d" % i) for i, p in enumerate(dproj)]
    g["w_dt"] = _mm(u_t, ddt, "nn", tm=512, tn=128, name="in_proj_dt_dw")
    du_dt = _mm(ddt, wts["w_dt"], "nt", tm=1024, tn=1024, name="in_proj_dt_dx")
    du, g["w_in_lands"] = _mm_pieces_nt(dproj, wts["w_main"], du_dt, tm=512, tn=256, name="in_proj_dx",
                                        ride=w_in_ride(g) if w_in_ride else None)
    grad_x, g["norm_mix_pre"] = _rms_bwd(x, du, small["norm_mix_pre"], name="norm_pre_bwd", out_dtype=f32, add=dh1)
    return loss, grad_x, g


def _to_internal(w_in):
    sec = lambda r: w_in[:, r[0]:r[1]]
    w_main = jnp.concatenate([sec(R_QKV), sec(R_XBC), sec(R_GATE), sec(R_MEMQ), sec(R_Z)], axis=1)
    w_dt = jnp.pad(sec(R_DT), ((0, 0), (0, 128 - SSD_HEADS)))
    return w_main, w_dt


def _from_internal(pieces, g_dt):
    dq, dk, dv, dxbc, dgate, dmemq, dz = pieces
    return [dq, dk, dv, dz, dxbc, g_dt[:, :SSD_HEADS], dmemq, dgate]


def _w_in_slab(ordered, s, dtype):
    width = D_IN // N_SHARD
    lo, hi, off, parts = s * width, (s + 1) * width, 0, []
    for p in ordered:
        a, b = max(lo, off), min(hi, off + p.shape[1])
        if a < b:
            parts.append(p[:, a - off:b - off].astype(dtype))
        off += p.shape[1]
    return jnp.concatenate(parts, axis=1)


MESH = pl.DeviceIdType.MESH
ANY = pl.BlockSpec(memory_space=pl.ANY)


def _place():
    x, y, c = lax.axis_index("x"), lax.axis_index("y"), lax.axis_index("c")
    return (x, y, c), [(1 - x, y, c), (x, 1 - y, c), (1 - x, 1 - y, c)]


def _exchange_copy(mode, ins, lands, send, recv, a, k, me, peers, arriving):
    p = peers[k]
    theirs = 2 * p[0] + p[1]
    if mode == "gather":
        src, dst = ins[a], lands[a].at[theirs if arriving else me]
    else:
        src, dst = ins[a].at[theirs], lands[a].at[k]
    return pltpu.make_async_remote_copy(src_ref=src, dst_ref=dst, send_sem=send.at[a * 3 + k],
                                        recv_sem=recv.at[a * 3 + k], device_id=p, device_id_type=MESH)


class _Ride:
    def __init__(self, srcs, mode):
        self.srcs, self.mode, self.n = list(srcs), mode, len(srcs)
        n = self.n
        self.in_specs, self.out_specs = [ANY] * n, [ANY] * n
        self.out_shape = [
            jax.ShapeDtypeStruct((N_SHARD,) + s.shape if mode == "gather" else (3,) + s.shape[1:], s.dtype)
            for s in self.srcs]
        self.scratch = [pltpu.SemaphoreType.DMA((3 * n,)), pltpu.SemaphoreType.DMA((3 * n,)),
                        pltpu.SemaphoreType.DMA((n,))]

    def _own(self, ins, lnd, sems):
        if self.mode != "gather":
            return []
        me = 2 * lax.axis_index("x") + lax.axis_index("y")
        return [pltpu.make_async_copy(ins[a], lnd[a].at[me], sems[2].at[a]) for a in range(self.n)]

    def _far(self, ins, lnd, sems, arriving):
        (x, y, c), peers = _place()
        return [_exchange_copy(self.mode, ins, lnd, sems[0], sems[1], a, k, 2 * x + y, peers, arriving)
                for a in range(self.n) for k in range(3)]

    def start(self, ins, lnd, sems):
        for cp in self._own(ins, lnd, sems) + self._far(ins, lnd, sems, False):
            cp.start()

    def finish(self, ins, lnd, sems):
        for cp in self._far(ins, lnd, sems, True):
            cp.wait_recv()
        for cp in self._far(ins, lnd, sems, False):
            cp.wait_send()
        for cp in self._own(ins, lnd, sems):
            cp.wait()


def _gather_two_level(shards, name):
    n = len(shards)

    def body(*refs):
        ins, lnd = refs[:n], refs[n:2 * n]
        send, recv, loc = refs[2 * n:]
        (x, y, c), peers = _place()
        me = 2 * x + y

        def half(ref, a, core):
            rows = shards[a].shape[0] // 2
            return ref.at[pl.ds(core * rows, rows)]

        def copy(a, j, slot, core, to):
            return pltpu.make_async_remote_copy(
                src_ref=half(ins[a], a, core) if j < 3 else half(lnd[a].at[slot], a, core),
                dst_ref=half(lnd[a].at[slot], a, core), send_sem=send.at[6 * a + j], recv_sem=recv.at[6 * a + j],
                device_id=to, device_id_type=MESH)

        own = [pltpu.make_async_copy(ins[a], lnd[a].at[me], loc.at[a]) for a in range(n)]
        far = [copy(a, k, me, c, peers[k]) for a in range(n) for k in range(3)]
        for cp in own + far:
            cp.start()
        passed = []
        for a in range(n):
            for k, p in enumerate(peers):
                theirs = 2 * p[0] + p[1]
                copy(a, k, theirs, c, p).wait_recv()
                passed.append(copy(a, 3 + k, theirs, c, (x, y, 1 - c)))
                passed[-1].start()
        for a in range(n):
            for k, p in enumerate(peers):
                copy(a, 3 + k, 2 * p[0] + p[1], 1 - c, (x, y, 1 - c)).wait_recv()
        for cp in far + passed:
            cp.wait_send()
        for cp in own:
            cp.wait()

    return pl.pallas_call(
        body, name=name, in_specs=[ANY] * n, out_specs=[ANY] * n,
        out_shape=[jax.ShapeDtypeStruct((N_SHARD,) + s.shape, s.dtype) for s in shards],
        scratch_shapes=[pltpu.SemaphoreType.DMA((6 * n,)), pltpu.SemaphoreType.DMA((6 * n,)),
                        pltpu.SemaphoreType.DMA((n,))],
    )(*shards)


def _exchange_packets(packet):
    def body(pk, pk_out, send, recv, loc):
        x, y, c = lax.axis_index("x"), lax.axis_index("y"), lax.axis_index("c")
        lin = 4 * x + 2 * y + c
        own = pltpu.make_async_copy(pk, pk_out.at[lin], loc.at[0])
        own.start()

        def pk_copy(m, slot):
            dev = (x ^ ((m >> 2) & 1), y ^ ((m >> 1) & 1), c ^ (m & 1))
            return pltpu.make_async_remote_copy(
                src_ref=pk, dst_ref=pk_out.at[slot], send_sem=send.at[m - 1], recv_sem=recv.at[m - 1],
                device_id=dev, device_id_type=MESH)

        sent = [pk_copy(m, lin) for m in range(1, N_DEV)]
        for cp in sent:
            cp.start()
        for m in range(1, N_DEV):
            pk_copy(m, lin ^ m).wait_recv()
        for cp in sent:
            cp.wait_send()
        own.wait()

    return pl.pallas_call(
        body, name="exchange_packets", in_specs=[ANY], out_specs=ANY,
        out_shape=jax.ShapeDtypeStruct((N_DEV,) + packet.shape, packet.dtype),
        scratch_shapes=[pltpu.SemaphoreType.DMA((N_DEV - 1,)), pltpu.SemaphoreType.DMA((N_DEV - 1,)),
                        pltpu.SemaphoreType.DMA((1,))],
    )(packet)


def _swap_sibling(parts, name):
    n = len(parts)

    def body(*refs):
        ins, outs = refs[:n], refs[n:2 * n]
        send, recv = refs[2 * n:]
        x, y, c = lax.axis_index("x"), lax.axis_index("y"), lax.axis_index("c")
        cps = [pltpu.make_async_remote_copy(
            src_ref=ins[a], dst_ref=outs[a], send_sem=send.at[a], recv_sem=recv.at[a],
            device_id=(x, y, 1 - c), device_id_type=MESH) for a in range(n)]
        for cp in cps:
            cp.start()
        for cp in cps:
            cp.wait_recv()
        for cp in cps:
            cp.wait_send()

    return pl.pallas_call(
        body, name=name,
        in_specs=[ANY] * n, out_specs=[ANY] * n,
        out_shape=[jax.ShapeDtypeStruct(p.shape, p.dtype) for p in parts],
        scratch_shapes=[pltpu.SemaphoreType.DMA((n,)), pltpu.SemaphoreType.DMA((n,))],
    )(*parts)


BLOCK_ELEMS = 256 * 1024


def _row_tile(R, C):
    tr = max(8, (BLOCK_ELEMS // C) // 8 * 8)
    while R % tr:
        tr -= 8
    return min(tr, R)


def _sum_parts(own, stack, name, out_dtype=f32):
    k = stack.shape[0]
    R, C = stack.shape[1:]
    tr = _row_tile(R, C)

    def body(*refs):
        o_ref = refs[-1]
        acc = refs[0][...].astype(f32)
        for r in refs[1:-1]:
            acc = acc + r[...].astype(f32)
        o_ref[...] = acc.astype(out_dtype)

    row = pl.BlockSpec((tr, C), lambda i: (i, 0))
    specs = ([row] if own is not None else []) + [
        pl.BlockSpec((None, tr, C), functools.partial(lambda i, j: (j, i, 0), j=j)) for j in range(k)]
    args = ([own] if own is not None else []) + [stack] * k
    return pl.pallas_call(
        body, name=name, grid=(R // tr,), in_specs=specs, out_specs=row,
        out_shape=jax.ShapeDtypeStruct((R, C), out_dtype), compiler_params=_params(("parallel",)),
    )(*args)


def _adamw(w, m, v, g_parts, name):
    R, C = w.shape
    tr = _row_tile(R, C)
    n_g = len(g_parts)

    def body(w_ref, m_ref, v_ref, *rest):
        g = rest[0][...]
        for r in rest[1:n_g]:
            g = g + r[...]
        g_ref, d_ref, nm_ref, nv_ref = rest[n_g:]
        nm = ADAM_B1 * m_ref[...] + (1.0 - ADAM_B1) * g
        nv = ADAM_B2 * v_ref[...] + (1.0 - ADAM_B2) * jnp.square(g)
        m_hat = nm / (1.0 - ADAM_B1 ** ADAM_STEP)
        v_hat = nv / (1.0 - ADAM_B2 ** ADAM_STEP)
        g_ref[...] = g
        d_ref[...] = -ADAM_LR * (m_hat / (jnp.sqrt(v_hat) + ADAM_EPS) + ADAM_WD * w_ref[...])
        nm_ref[...] = nm
        nv_ref[...] = nv

    row = pl.BlockSpec((tr, C), lambda i: (i, 0))
    return pl.pallas_call(
        body, name=name, grid=(R // tr,), in_specs=[row] * (3 + n_g), out_specs=[row] * 4,
        out_shape=[jax.ShapeDtypeStruct((R, C), f32)] * 4, compiler_params=_params(("parallel",)),
    )(w, m, v, *g_parts)


BIG = ("w_in", "w_mem_kv", "w_sb_out", "w_ssd_out", "w_mem_out", "w_o", "w_up", "w_down")
LATE = ("w_sb_out", "w_ssd_out", "w_mem_out", "w_o", "w_up", "w_down")
REST = BIG[1:]
COL_SHARDED = ("w_in", "w_mem_kv", "w_up")
SMALL = ("norm_mix_pre", "conv_w", "conv_b", "dt_bias", "a_log", "d_skip", "ssd_norm", "norm_mem",
         "norm_mix_post", "norm_mlp_pre", "norm_mlp_post")
WEIGHTS = ("norm_mix_pre", "w_in", "conv_w", "conv_b", "dt_bias", "a_log", "d_skip", "ssd_norm", "norm_mem",
           "w_mem_kv", "w_sb_out", "w_ssd_out", "w_mem_out", "w_o", "norm_mix_post", "norm_mlp_pre", "w_up",
           "w_down", "norm_mlp_post")
PK_ROWS = 184


def _pack(vecs):
    flat = jnp.concatenate([v.reshape(-1) for v in vecs])
    return jnp.pad(flat, (0, PK_ROWS * 128 - flat.shape[0])).reshape(PK_ROWS, 128)


def _unpack(pk, shapes):
    flat = pk.reshape(-1)
    out, off = [], 0
    for s in shapes:
        n = 1
        for d in s:
            n *= d
        out.append(flat[off:off + n].reshape(s))
        off += n
    return out


def _full_from_slabs(name, slabs):
    if name in COL_SHARDED:
        return slabs.transpose(1, 0, 2).reshape(slabs.shape[1], -1)
    return slabs.reshape(-1, slabs.shape[2])


def _slabs_from_full(name, g):
    if name in COL_SHARDED:
        return g.reshape(g.shape[0], N_SHARD, -1).transpose(1, 0, 2)
    return g.reshape(N_SHARD, -1, g.shape[1])


def kernel(x, mem, norm_mix_pre, w_in, conv_w, conv_b, dt_bias, a_log, d_skip, ssd_norm, norm_mem, w_mem_kv, w_sb_out, w_ssd_out, w_mem_out, w_o, norm_mix_post, norm_mlp_pre, w_up, w_down, norm_mlp_post, loss_target, m_norm_mix_pre, m_w_in, m_conv_w, m_conv_b, m_dt_bias, m_a_log, m_d_skip, m_ssd_norm, m_norm_mem, m_w_mem_kv, m_w_sb_out, m_w_ssd_out, m_w_mem_out, m_w_o, m_norm_mix_post, m_norm_mlp_pre, m_w_up, m_w_down, m_norm_mlp_post, v_norm_mix_pre, v_w_in, v_conv_w, v_conv_b, v_dt_bias, v_a_log, v_d_skip, v_ssd_norm, v_norm_mem, v_w_mem_kv, v_w_sb_out, v_w_ssd_out, v_w_mem_out, v_w_o, v_norm_mix_post, v_norm_mlp_pre, v_w_up, v_w_down, v_norm_mlp_post):
    env = dict(locals())
    w = {n: env[n] for n in WEIGHTS}
    mo = {n: env["m_" + n] for n in WEIGHTS}
    vo = {n: env["v_" + n] for n in WEIGHTS}
    shard = 2 * lax.axis_index("x") + lax.axis_index("y")

    first = _gather_two_level([w["w_in"][0].astype(bf16)], "gather_first")
    w_main, w_dt = _to_internal(_full_from_slabs("w_in", first[0]))
    wts = dict(w_main=w_main, w_dt=w_dt)
    ride_names = (("w_mem_kv",) + LATE[:4], LATE[4:5], LATE[5:])
    late_rides = tuple(_Ride([w[n][0].astype(bf16) for n in names] + ([w["conv_w"][0]] if i == 0 else []), "gather")
                       for i, names in enumerate(ride_names))

    def late_weights(i, lands):
        full = {n: _full_from_slabs(n, s) for n, s in zip(ride_names[i], lands)}
        if i == 0:
            full["conv_w"] = lands[-1].transpose(1, 0, 2).reshape(CONV_K, CONV_DIM)
        return full

    def rest_rides(g):
        slabs = [_slabs_from_full(n, g[n]).astype(bf16) for n in REST]
        return _Ride(slabs[5:], "scatter"), _Ride(slabs[:5], "scatter")

    core = lax.axis_index("c")
    half = D // 2

    def w_in_ride(g):
        ordered = _from_internal(g["w_main"], g["w_dt"])
        stack = jnp.stack([_w_in_slab(ordered, s, bf16) for s in range(N_SHARD)])
        keep = lax.dynamic_slice_in_dim(stack, core * half, half, axis=1)
        away = lax.dynamic_slice_in_dim(stack, (1 - core) * half, half, axis=1)
        (got,) = _swap_sibling([away], "w_in_halves_out")
        wide = lambda a: a.reshape(N_SHARD * half, -1)
        chip = _sum_parts(wide(keep), wide(got)[None], "sum_cores_w_in", bf16).reshape(N_SHARD, half, -1)
        own = lax.switch(shard, [functools.partial(_w_in_slab, ordered, s, f32) for s in range(N_SHARD)])
        own = lax.dynamic_slice_in_dim(own, core * half, half, axis=0)
        g["w_in_own"] = _sum_parts(own, lax.dynamic_index_in_dim(got, shard, 0, keepdims=True), "sum_cores_w_in_own")
        return _Ride([chip], "scatter")

    small = {n: w[n] for n in SMALL if n != "conv_w"}
    loss, grad_x, g = _local_step(x[0], mem[0], loss_target[0], wts, late_rides, late_weights, small,
                                  rest_rides, w_in_ride)
    out_g, out_d, out_m, out_v = {}, {}, {}, {}

    def apply(n, g_parts):
        res = _adamw(w[n][0], mo[n][0], vo[n][0], g_parts, name="adamw_" + n)
        out_g[n], out_d[n], out_m[n], out_v[n] = [r[None] for r in res]

    mine = _sum_parts(g["w_in_own"], g["w_in_lands"][0], name="sum_chips_w_in")
    (theirs,) = _swap_sibling([mine], "w_in_halves_back")
    g_w_in = lax.dynamic_update_slice_in_dim(jnp.zeros((D, D_IN // N_SHARD), f32), mine, core * half, axis=0)
    apply("w_in", [lax.dynamic_update_slice_in_dim(g_w_in, theirs, (1 - core) * half, axis=0)])

    packets = _exchange_packets(_pack([g[n] for n in SMALL] + [loss[:, :1]]))
    partial = []
    for n, r in zip(REST, g["rest_lands"]):
        own = lax.dynamic_index_in_dim(_slabs_from_full(n, g[n]), shard, 0, keepdims=False)
        partial.append(_sum_parts(own, r, name="sum_chips_" + n))
    other = _swap_sibling(partial, "swap_sibling")

    for n, p, q in zip(REST, partial, other):
        apply(n, [p, q])
    tot = _sum_parts(None, packets, name="sum_packets")
    shapes = [g[n].shape for n in SMALL] + [(1, 1)]
    sm = dict(zip(SMALL + ("loss",), _unpack(tot, shapes)))
    sm["conv_w"] = lax.dynamic_slice_in_dim(sm["conv_w"], shard * (CONV_DIM // N_SHARD), CONV_DIM // N_SHARD, axis=1)
    own_small = lambda d: _pack([d[n].reshape(sm[n].shape) for n in SMALL])
    res = _adamw(own_small(w), own_small(mo), own_small(vo), [own_small(sm)], name="adamw_small")
    own_shapes = [sm[n].shape for n in SMALL]
    for store, r in zip((out_g, out_d, out_m, out_v), res):
        for n, val in zip(SMALL, _unpack(r, own_shapes)):
            store[n] = val.reshape(w[n].shape)

    outs = [sm["loss"].reshape(()), grad_x[None]]
    for store in (out_g, out_d, out_m, out_v):
        outs += [store[n] for n in WEIGHTS]
    return tuple(outs)
```

```python
import functools

import jax
import jax.numpy as jnp
from jax import lax
from jax.experimental import pallas as pl
from jax.experimental.pallas import tpu as pltpu

f32 = jnp.float32
bf16 = jnp.bfloat16

D = 1024
EPS = 1e-6
SB_HD = 64
SSD_INNER = 2048
SSD_HEADS = 32
SSD_GROUPS = 4
SSD_N = 128
SSD_L = 128
CONV_K = 4
CONV_DIM = 3072
MEM_HEADS = 4
MEM_HD = 256
D_FF = 4096
D_IN = 12320
N_SHARD = 4
N_DEV = 8

P_QKV, P_XBC, P_GATE, P_MEMQ, P_Z, P_DT, P_TOT = 0, 3072, 6144, 9216, 10240, 12288, 12416
R_QKV, R_Z, R_XBC, R_DT, R_MEMQ, R_GATE = (0, 3072), (3072, 5120), (5120, 8192), (8192, 8224), (8224, 9248), (9248, 12320)

ADAM_LR = 0.001
ADAM_B1 = 0.9
ADAM_B2 = 0.999
ADAM_EPS = 1e-08
ADAM_WD = 0.01
ADAM_STEP = 10

VMEM_LIMIT = 56 * 1024 * 1024

NN = (((1,), (0,)), ((), ()))
NT = (((1,), (1,)), ((), ()))
TN = (((0,), (0,)), ((), ()))


def _dot(a, b, dims=NN):
    return lax.dot_general(a, b, dims, preferred_element_type=f32)


def _params(sem=None):
    return pltpu.CompilerParams(dimension_semantics=sem, vmem_limit_bytes=VMEM_LIMIT)


def _sigmoid(x):
    return 1.0 / (1.0 + jnp.exp(-x))


def _split2(x):
    hi = x.astype(bf16)
    lo = (x - hi.astype(f32)).astype(bf16)
    return hi, lo


def _split3(x):
    hi = x.astype(bf16)
    r = x - hi.astype(f32)
    mid = r.astype(bf16)
    lo = (r - mid.astype(f32)).astype(bf16)
    return hi, mid, lo


def _mm(a, b, mode, *, tm, tn, name, out_dtypes=(f32,), epi=None, extras=(), ride=None):
    M = a.shape[1] if mode == "tn" else a.shape[0]
    N = b.shape[0] if mode == "nt" else b.shape[1]
    tm, tn = min(tm, M), min(tn, N)
    if mode == "nn":
        (M, K), N = a.shape, b.shape[1]
        a_spec = pl.BlockSpec((tm, K), lambda i, j: (i, 0))
        b_spec = pl.BlockSpec((K, tn), lambda i, j: (0, j))
        dims = NN
    elif mode == "nt":
        (M, K), N = a.shape, b.shape[0]
        a_spec = pl.BlockSpec((tm, K), lambda i, j: (i, 0))
        b_spec = pl.BlockSpec((tn, K), lambda i, j: (j, 0))
        dims = NT
    else:
        (K, M), N = a.shape, b.shape[1]
        a_spec = pl.BlockSpec((K, tm), lambda i, j: (0, i))
        b_spec = pl.BlockSpec((K, tn), lambda i, j: (0, j))
        dims = TN
    assert M % tm == 0 and N % tn == 0, (name, M, N, tm, tn)
    n_ex, n_out = len(extras), len(out_dtypes)
    n_r = ride.n if ride else 0
    o_spec = pl.BlockSpec((tm, tn), lambda i, j: (i, j))
    grid = (M // tm, N // tn)

    def body(a_ref, b_ref, *rest):
        r_ins = rest[n_ex:n_ex + n_r]
        outs = rest[n_ex + n_r:n_ex + n_r + n_out]
        r_lnd, r_sems = rest[n_ex + n_r + n_out:n_ex + 2 * n_r + n_out], rest[n_ex + 2 * n_r + n_out:]
        i, j = pl.program_id(0), pl.program_id(1)
        if ride:
            pl.when((i == 0) & (j == 0))(lambda: ride.start(r_ins, r_lnd, r_sems))
        acc = _dot(a_ref[...].astype(bf16), b_ref[...].astype(bf16), dims)
        res = (acc,) if epi is None else epi(acc, *[e[...] for e in rest[:n_ex]])
        for o_ref, r in zip(outs, res):
            o_ref[...] = r.astype(o_ref.dtype)
        if ride:
            pl.when((i == grid[0] - 1) & (j == grid[1] - 1))(lambda: ride.finish(r_ins, r_lnd, r_sems))

    out = pl.pallas_call(
        body, name=name, grid=grid,
        in_specs=[a_spec, b_spec] + [o_spec] * n_ex + (ride.in_specs if ride else []),
        out_specs=[o_spec] * n_out + (ride.out_specs if ride else []),
        out_shape=[jax.ShapeDtypeStruct((M, N), dt) for dt in out_dtypes] + (ride.out_shape if ride else []),
        scratch_shapes=ride.scratch if ride else [],
        compiler_params=_params(("arbitrary", "arbitrary") if ride else ("parallel", "parallel")),
    )(a, b, *extras, *(ride.srcs if ride else []))
    if ride:
        return (out[0] if n_out == 1 else out[:n_out]), list(out[n_out:])
    return out[0] if n_out == 1 else out


def _mm_pieces_nt(pieces, b, add, *, tm, tn, name, ride):
    M, N = pieces[0].shape[0], b.shape[0]
    n_p, n_r = len(pieces), (ride.n if ride else 0)
    o_spec = pl.BlockSpec((tm, tn), lambda i, j: (i, j))
    grid = (M // tm, N // tn)

    def body(*refs):
        b_ref, add_ref = refs[n_p:n_p + 2]
        r_ins, o_ref = refs[n_p + 2:n_p + 2 + n_r], refs[n_p + 2 + n_r]
        r_lnd, r_sems = refs[n_p + 3 + n_r:n_p + 3 + 2 * n_r], refs[n_p + 3 + 2 * n_r:]
        i, j = pl.program_id(0), pl.program_id(1)
        if ride:
            pl.when((i == 0) & (j == 0))(lambda: ride.start(r_ins, r_lnd, r_sems))
        acc, off = add_ref[...], 0
        for r in refs[:n_p]:
            acc = acc + _dot(r[...], b_ref[:, off:off + r.shape[1]], NT)
            off += r.shape[1]
        o_ref[...] = acc
        if ride:
            pl.when((i == grid[0] - 1) & (j == grid[1] - 1))(lambda: ride.finish(r_ins, r_lnd, r_sems))

    out = pl.pallas_call(
        body, name=name, grid=grid,
        in_specs=[pl.BlockSpec((tm, p.shape[1]), lambda i, j: (i, 0)) for p in pieces]
        + [pl.BlockSpec((tn, b.shape[1]), lambda i, j: (j, 0)), o_spec] + (ride.in_specs if ride else []),
        out_specs=[o_spec] + (ride.out_specs if ride else []),
        out_shape=[jax.ShapeDtypeStruct((M, N), f32)] + (ride.out_shape if ride else []),
        scratch_shapes=ride.scratch if ride else [],
        compiler_params=_params(("arbitrary", "arbitrary")),
    )(*pieces, b, add, *(ride.srcs if ride else []))
    return out[0], list(out[1:])


def _rms_fwd(x, g, *, name, out_dtype, residual=None, tm=512):
    S, C = x.shape
    tm = min(tm, S)
    has_res = residual is not None

    def body(x_ref, g_ref, *rest):
        xv = x_ref[...]
        r = lax.rsqrt(jnp.mean(xv * xv, axis=1, keepdims=True) + EPS)
        y = xv * r * g_ref[...]
        if has_res:
            y = y + rest[0][...]
        rest[-1][...] = y.astype(out_dtype)

    row = pl.BlockSpec((tm, C), lambda i: (i, 0))
    vec = pl.BlockSpec((1, C), lambda i: (0, 0))
    args = (x, g) + ((residual,) if has_res else ())
    return pl.pallas_call(
        body, name=name, grid=(S // tm,),
        in_specs=[row, vec] + ([row] if has_res else []),
        out_specs=row, out_shape=jax.ShapeDtypeStruct((S, C), out_dtype),
        compiler_params=_params(("parallel",)),
    )(*args)


def _rms_bwd(x, dy, g, *, name, out_dtype, add=None, tm=512):
    S, C = x.shape
    tm = min(tm, S)
    has_add = add is not None

    def body(x_ref, dy_ref, g_ref, *rest):
        dx_ref, dg_ref = rest[-2], rest[-1]
        xv = x_ref[...]
        dyv = dy_ref[...].astype(f32)
        r = lax.rsqrt(jnp.mean(xv * xv, axis=1, keepdims=True) + EPS)
        xh = xv * r
        dxh = dyv * g_ref[...]
        dx = r * (dxh - xh * jnp.mean(dxh * xh, axis=1, keepdims=True))
        if has_add:
            dx = dx + rest[0][...]
        dx_ref[...] = dx.astype(out_dtype)

        @pl.when(pl.program_id(0) == 0)
        def _():
            dg_ref[...] = jnp.zeros_like(dg_ref)

        dg_ref[...] += jnp.sum(dyv * xh, axis=0, keepdims=True)

    row = pl.BlockSpec((tm, C), lambda i: (i, 0))
    vec = pl.BlockSpec((1, C), lambda i: (0, 0))
    args = (x, dy, g) + ((add,) if has_add else ())
    return pl.pallas_call(
        body, name=name, grid=(S // tm,),
        in_specs=[row, row, vec] + ([row] if has_add else []),
        out_specs=[row, vec],
        out_shape=[jax.ShapeDtypeStruct((S, C), out_dtype), jax.ShapeDtypeStruct((1, C), f32)],
        compiler_params=_params(("arbitrary",)),
    )(*args)


SB_T = 128
SB_SPENT = -120.0
SB_QB = 4
SB_TAIL = 3
SB_GROUPS = (4, 2, 1)
SB_GROUPS_BWD = (4, 2, 1)


def _sb_masks():
    lane = lax.broadcasted_iota(jnp.int32, (1, 128), 1)
    m_a = (lane < SB_HD).astype(f32)
    return m_a, 1.0 - m_a


def _chunks(a, n):
    return [a[:, u * SB_T:(u + 1) * SB_T] for u in range(n)]


def _cat(parts, axis):
    return parts[0] if len(parts) == 1 else jnp.concatenate(parts, axis=axis)


def _mask_last(a, n, mask):
    if mask is None:
        return a
    parts = _chunks(a, n)
    return _cat(parts[:-1] + [jnp.where(mask, parts[-1], 0.0)], 1)


def _sb_logits(z, n, mask):
    l1p = jnp.log(1.0 + jnp.exp(-jnp.abs(z)))
    lb = jnp.minimum(z, 0.0) - l1p
    return lb, _mask_last(lb - z, n, mask)


def _by_count(i, most, fn):
    return lax.switch(jnp.minimum(i, most - 1), [functools.partial(fn, n) for n in range(1, most + 1)])


def _chunk_matmul(parts_list, u_mat):
    out = _dot(_cat(parts_list, 0), u_mat)
    return [out[u * SB_T:(u + 1) * SB_T] for u in range(len(parts_list))]


def _chunk_cumsum(lk, n, u_mat):
    hi = lk.astype(bf16)
    lo = (lk - hi.astype(f32)).astype(bf16)
    out = _chunk_matmul(_chunks(hi, n) + _chunks(lo, n), u_mat)
    return [out[u] + out[n + u] for u in range(n)]


def _sb_fwd(proj, S, ride=None):
    nq = S // SB_T
    n_pairs = D // 128
    scale = SB_HD ** -0.5
    n_r = ride.n if ride else 0

    def body(q_ref, k_ref, v_ref, *rest):
        o_ref, t_ref = rest[n_r:n_r + 2]
        step_i = pl.program_id(1)
        if ride:
            pl.when((pl.program_id(0) == 0) & (step_i == 0))(
                lambda: ride.start(rest[:n_r], rest[n_r + 2:2 * n_r + 2], rest[2 * n_r + 2:]))
        m_a, m_b = _sb_masks()
        r_i = lax.broadcasted_iota(jnp.int32, (SB_T, SB_T), 0)
        c_i = lax.broadcasted_iota(jnp.int32, (SB_T, SB_T), 1)
        u_mat = (r_i > c_i).astype(bf16)
        causal = c_i < r_i
        q_all = q_ref[...] * scale
        q_hs = [((q * m_a).astype(bf16), (q * m_b).astype(bf16))
                for q in (q_all[b * SB_T:(b + 1) * SB_T] for b in range(SB_QB))]

        def group(q_h, j_lo, n, carry, mask):
            acc, c_a, c_b = carry
            rows = pl.ds(pl.multiple_of(j_lo * SB_T, SB_T), n * SB_T)
            k = k_ref[rows, :].astype(bf16)
            v = v_ref[rows, :]
            zs = [_dot(q_b, k, NT) for q_b in q_h]
            lbk = [_sb_logits(z, n, mask) for z in zs]
            parts = [_chunk_cumsum(lk, n, u_mat) for _, lk in lbk]
            ws, cs = [], []
            for (lb, lk), part, c in zip(lbk, parts, (c_a, c_b)):
                lb_c, lk_c = _chunks(lb, n), _chunks(lk, n)
                w_c = [None] * n
                for u in reversed(range(n)):
                    w_c[u] = jnp.exp(lb_c[u] + c + part[u])
                    c = c + jnp.sum(lk_c[u], axis=1, keepdims=True)
                ws.append(_mask_last(_cat(w_c, 1), n, mask).astype(bf16))
                cs.append(c)
            for w, m in zip(ws, (m_a, m_b)):
                acc = acc + _dot(w, (v * m).astype(bf16))
            return acc, cs[0], cs[1]

        zero_c = jnp.zeros((SB_T, 1), f32)
        init = (jnp.zeros((SB_T, 128), f32), zero_c, zero_c)
        blocks = [(step_i * SB_QB + b, q_hs[b]) for b in range(SB_QB)]

        def whole_tails():
            return tuple(group(q_h, i - SB_TAIL + 1, SB_TAIL, init, causal) for i, q_h in blocks)

        def short_tails():
            return tuple(_by_count(i, SB_TAIL, functools.partial(
                lambda n, i, q_h: group(q_h, i - n + 1, n, init, causal), i=i, q_h=q_h)) for i, q_h in blocks)

        carries = lax.cond(step_i * SB_QB >= SB_TAIL - 1, whole_tails, short_tails)

        def spent(cr):
            return (jnp.max(jnp.maximum(cr[1], cr[2])) < SB_SPENT).astype(jnp.int32)

        lane = lax.broadcasted_iota(jnp.int32, (1, 128), 1)
        for b, ((i, q_h), carry) in enumerate(zip(blocks, carries)):
            state = (i - jnp.minimum(i, SB_TAIL - 1), spent(carry), carry)
            for n in SB_GROUPS:
                def step(st, n=n, q_h=q_h):
                    left, _, cr = st
                    cr = group(q_h, left - n, n, cr, None)
                    return left - n, spent(cr), cr

                state = lax.while_loop(lambda st, n=n: (st[0] >= n) & (st[1] == 0), step, state)
            left, _, carry = state
            rows = slice(b * SB_T, (b + 1) * SB_T)
            o_ref[rows, :] = carry[0]
            t_ref[rows, :] = (jnp.where(lane == 0, carry[1], 0.0) + jnp.where(lane == SB_HD, carry[2], 0.0)
                              + jnp.where(lane == 1, left.astype(f32), 0.0))
        if ride:
            pl.when((pl.program_id(0) == n_pairs - 1) & (step_i == nq // SB_QB - 1))(
                lambda: ride.finish(rest[:n_r], rest[n_r + 2:2 * n_r + 2], rest[2 * n_r + 2:]))

    qs = pl.BlockSpec((SB_QB * SB_T, 128), lambda h, i: (i, h))
    out = pl.pallas_call(
        body, name="sb_fwd", grid=(n_pairs, nq // SB_QB),
        in_specs=[qs,
                  pl.BlockSpec((S, 128), lambda h, i: (0, n_pairs + h)),
                  pl.BlockSpec((S, 128), lambda h, i: (0, 2 * n_pairs + h))] + (ride.in_specs if ride else []),
        out_specs=[qs, qs] + (ride.out_specs if ride else []),
        out_shape=[jax.ShapeDtypeStruct((S, D), f32)] * 2 + (ride.out_shape if ride else []),
        scratch_shapes=ride.scratch if ride else [],
        compiler_params=_params(("arbitrary", "arbitrary")),
    )(proj, proj, proj, *(ride.srcs if ride else []))
    return out[0], out[1], list(out[2:])


def _sb_bwd(proj, tot_lk, do, S, ride=None):
    nq = S // SB_T
    n_pairs = D // 128
    scale = SB_HD ** -0.5
    n_r = ride.n if ride else 0

    def body(q_ref, k_ref, v_ref, t_ref, do_ref, *rest):
        dq_ref, dk_ref, dv_ref = rest[n_r:n_r + 3]
        dk_acc, dv_acc = rest[2 * n_r + 3:2 * n_r + 5]
        r_ins, r_lnd, r_sems = rest[:n_r], rest[n_r + 3:2 * n_r + 3], rest[2 * n_r + 5:]
        step_i = pl.program_id(1)
        if ride:
            pl.when((pl.program_id(0) == 0) & (step_i == 0))(lambda: ride.start(r_ins, r_lnd, r_sems))
        m_a, m_b = _sb_masks()
        r_i = lax.broadcasted_iota(jnp.int32, (SB_T, SB_T), 0)
        c_i = lax.broadcasted_iota(jnp.int32, (SB_T, SB_T), 1)
        u_inc = (r_i <= c_i).astype(bf16)
        u_exc = (r_i < c_i).astype(bf16)
        causal = c_i < r_i

        @pl.when(step_i == 0)
        def _():
            dk_acc[...] = jnp.zeros_like(dk_acc)
            dv_acc[...] = jnp.zeros_like(dv_acc)

        lane = lax.broadcasted_iota(jnp.int32, (1, 128), 1)
        blocks = []
        for b in range(SB_QB):
            rows_b = slice(b * SB_T, (b + 1) * SB_T)
            i = step_i * SB_QB + b
            q = q_ref[rows_b, :] * scale
            dov = do_ref[rows_b, :]
            tv = t_ref[rows_b, :]
            heads = []
            for m, first in ((m_a, 0), (m_b, SB_HD)):
                tot = jnp.sum(jnp.where(lane == first, tv, 0.0), axis=1, keepdims=True)
                heads.append(((q * m).astype(bf16), (dov * m).astype(bf16), tot, m))
            lowest = jnp.clip(jnp.max(jnp.where(lane == 1, tv, 0.0)).astype(jnp.int32), 0, i)
            blocks.append((i, heads, lowest))

        def group(heads, j_lo, n, carry, mask):
            dq_acc, cp_a, cp_b, ce_a, ce_b = carry
            rows = pl.ds(pl.multiple_of(j_lo * SB_T, SB_T), n * SB_T)
            k_f = k_ref[rows, :]
            k = k_f.astype(bf16)
            v = v_ref[rows, :].astype(bf16)
            zs = [_dot(h[0], k, NT) for h in heads]
            dws = [_dot(h[1], v, NT) for h in heads]
            lbk = [_sb_logits(z, n, mask) for z in zs]
            parts = [_chunk_cumsum(lk, n, u_inc) for _, lk in lbk]
            ws, es, cps = [], [], []
            for (lb, lk), part, dw, h, cp in zip(lbk, parts, dws, heads, (cp_a, cp_b)):
                lb_c, lk_c = _chunks(lb, n), _chunks(lk, n)
                w_c = []
                for u in range(n):
                    w_c.append(jnp.exp(lb_c[u] + (h[2] - cp) - part[u]))
                    cp = cp + jnp.sum(lk_c[u], axis=1, keepdims=True)
                w = _mask_last(_cat(w_c, 1), n, mask)
                ws.append(w)
                es.append(dw * w)
                cps.append(cp)
            e_parts = [_chunk_matmul(_chunks(e.astype(bf16), n), u_exc) for e in es]
            dzs, ces = [], []
            for (lb, _), e, e_part, ce in zip(lbk, es, e_parts, (ce_a, ce_b)):
                e_c = _chunks(e, n)
                big_c = []
                for u in range(n):
                    big_c.append(ce + e_part[u])
                    ce = ce + jnp.sum(e_c[u], axis=1, keepdims=True)
                sig = jnp.exp(lb)
                dz = _mask_last(e * (1.0 - sig) - _cat(big_c, 1) * sig, n, mask)
                dzs.append(dz.astype(bf16))
                ces.append(ce)
            dk_t = jnp.zeros((n * SB_T, 128), f32)
            dv_t = jnp.zeros((n * SB_T, 128), f32)
            for dz_b, w, h in zip(dzs, ws, heads):
                dq_acc = dq_acc + _dot(dz_b, (k_f * h[3]).astype(bf16))
                dk_t = dk_t + _dot(dz_b, h[0], TN)
                dv_t = dv_t + _dot(w.astype(bf16), h[1], TN)
            dk_acc[rows, :] += dk_t
            dv_acc[rows, :] += dv_t
            return dq_acc, cps[0], cps[1], ces[0], ces[1]

        zc = jnp.zeros((SB_T, 1), f32)
        carries = []
        for i, heads, lowest in blocks:
            carry = (jnp.zeros((SB_T, 128), f32), zc, zc, zc, zc)
            done = lowest
            tail_lo = i - jnp.minimum(i, SB_TAIL - 1)
            for n in SB_GROUPS_BWD:
                trips = (tail_lo - done) // n
                carry = lax.fori_loop(
                    0, trips, functools.partial(
                        lambda gi, cr, n, done, heads: group(heads, done + gi * n, n, cr, None),
                        n=n, done=done, heads=heads),
                    carry)
                done = done + trips * n
            carries.append(carry)

        def whole_tails():
            return tuple(group(heads, i - SB_TAIL + 1, SB_TAIL, cr, causal)
                         for (i, heads, _), cr in zip(blocks, carries))

        def short_tails():
            return tuple(_by_count(i, SB_TAIL, functools.partial(
                lambda n, i, heads, cr: group(heads, i - n + 1, n, cr, causal), i=i, heads=heads, cr=cr))
                for (i, heads, _), cr in zip(blocks, carries))

        carries = lax.cond(step_i * SB_QB >= SB_TAIL - 1, whole_tails, short_tails)
        for b, carry in enumerate(carries):
            dq_ref[b * SB_T:(b + 1) * SB_T, :] = (carry[0] * scale).astype(bf16)

        @pl.when(step_i == nq // SB_QB - 1)
        def _():
            dk_ref[...] = dk_acc[...].astype(bf16)
            dv_ref[...] = dv_acc[...].astype(bf16)

        if ride:
            pl.when((pl.program_id(0) == n_pairs - 1) & (step_i == nq // SB_QB - 1))(
                lambda: ride.finish(r_ins, r_lnd, r_sems))

    qs = pl.BlockSpec((SB_QB * SB_T, 128), lambda h, i: (i, h))
    full = pl.BlockSpec((S, 128), lambda h, i: (0, h))
    out = pl.pallas_call(
        body, name="sb_bwd", grid=(n_pairs, nq // SB_QB),
        in_specs=[qs,
                  pl.BlockSpec((S, 128), lambda h, i: (0, n_pairs + h)),
                  pl.BlockSpec((S, 128), lambda h, i: (0, 2 * n_pairs + h)),
                  qs, qs] + (ride.in_specs if ride else []),
        out_specs=[qs, full, full] + (ride.out_specs if ride else []),
        out_shape=[jax.ShapeDtypeStruct((S, D), bf16)] * 3 + (ride.out_shape if ride else []),
        scratch_shapes=[pltpu.VMEM((S, 128), f32), pltpu.VMEM((S, 128), f32)] + (ride.scratch if ride else []),
        compiler_params=_params(("arbitrary", "arbitrary")),
    )(proj, proj, proj, tot_lk, do, *(ride.srcs if ride else []))
    return out[0], out[1], out[2], list(out[3:])


CONV_CB = 256
HALO = 8


def _conv_fwd(proj, conv_w, conv_b, S, ride=None):
    tr = min(512, S)
    n_r = ride.n if ride else 0
    n_c = CONV_DIM // CONV_CB

    def body(x_ref, w_ref, b_ref, *rest):
        xc_ref, xbc_ref = rest[n_r:n_r + 2]
        r_ins, r_lnd, r_sems = rest[:n_r], rest[n_r + 2:2 * n_r + 2], rest[2 * n_r + 2:]
        if ride:
            pl.when(pl.program_id(0) == 0)(lambda: ride.start(r_ins, r_lnd, r_sems))
        w = w_ref[...]
        for t in range(S // tr):
            cur = x_ref[t * tr:(t + 1) * tr, :]
            halo = x_ref[t * tr - HALO:t * tr, :] if t else jnp.zeros((HALO, CONV_CB), f32)
            win = jnp.concatenate([halo, cur], axis=0)
            acc = b_ref[...] + w[CONV_K - 1:CONV_K, :] * cur
            for k in range(CONV_K - 1):
                acc = acc + w[k:k + 1, :] * pltpu.roll(win, CONV_K - 1 - k, 0)[HALO:, :]
            xc_ref[t * tr:(t + 1) * tr, :] = acc
            xbc_ref[t * tr:(t + 1) * tr, :] = acc * _sigmoid(acc)
        if ride:
            pl.when(pl.program_id(0) == n_c - 1)(lambda: ride.finish(r_ins, r_lnd, r_sems))

    col = pl.BlockSpec((S, CONV_CB), lambda c: (0, c))
    out = pl.pallas_call(
        body, name="conv_fwd", grid=(n_c,),
        in_specs=[pl.BlockSpec((S, CONV_CB), lambda c: (0, P_XBC // CONV_CB + c)),
                  pl.BlockSpec((CONV_K, CONV_CB), lambda c: (0, c)),
                  pl.BlockSpec((1, CONV_CB), lambda c: (0, c))] + (ride.in_specs if ride else []),
        out_specs=[col, col] + (ride.out_specs if ride else []),
        out_shape=[jax.ShapeDtypeStruct((S, CONV_DIM), f32)] * 2 + (ride.out_shape if ride else []),
        scratch_shapes=ride.scratch if ride else [],
        compiler_params=_params(("arbitrary",)),
    )(proj, conv_w, conv_b, *(ride.srcs if ride else []))
    return out[0], out[1], list(out[2:])


def _conv_bwd(proj, xc, dxbc, conv_w, S):
    tr = min(512, S)

    def body(x_ref, xc_ref, dy_ref, w_ref, dx_ref, dw_ref, db_ref, dxc_s):
        w = w_ref[...]
        xcv = xc_ref[...]
        sg = _sigmoid(xcv)
        dxc_s[0:S, :] = dy_ref[...] * (sg * (1.0 + xcv * (1.0 - sg)))
        dxc_s[S:S + HALO, :] = jnp.zeros((HALO, CONV_CB), f32)
        dws = [jnp.zeros((1, CONV_CB), f32) for _ in range(CONV_K)]
        db = jnp.zeros((1, CONV_CB), f32)
        for t in range(S // tr):
            cur = x_ref[t * tr:(t + 1) * tr, :]
            halo = x_ref[t * tr - HALO:t * tr, :] if t else jnp.zeros((HALO, CONV_CB), f32)
            win = jnp.concatenate([halo, cur], axis=0)
            dwin = dxc_s[t * tr:(t + 1) * tr + HALO, :]
            dcur = dwin[0:tr, :]
            db = db + jnp.sum(dcur, axis=0, keepdims=True)
            dws[CONV_K - 1] = dws[CONV_K - 1] + jnp.sum(dcur * cur, axis=0, keepdims=True)
            dx = w[CONV_K - 1:CONV_K, :] * dcur
            for k in range(CONV_K - 1):
                sh = CONV_K - 1 - k
                dws[k] = dws[k] + jnp.sum(dcur * pltpu.roll(win, sh, 0)[HALO:, :], axis=0, keepdims=True)
                dx = dx + w[k:k + 1, :] * pltpu.roll(dwin, tr + HALO - sh, 0)[0:tr, :]
            dx_ref[t * tr:(t + 1) * tr, :] = dx.astype(bf16)
        dw_ref[...] = jnp.concatenate(dws + [jnp.zeros((8 - CONV_K, CONV_CB), f32)], axis=0)
        db_ref[...] = db

    col = pl.BlockSpec((S, CONV_CB), lambda c: (0, c))
    return pl.pallas_call(
        body, name="conv_bwd", grid=(CONV_DIM // CONV_CB,),
        in_specs=[pl.BlockSpec((S, CONV_CB), lambda c: (0, P_XBC // CONV_CB + c)), col, col,
                  pl.BlockSpec((CONV_K, CONV_CB), lambda c: (0, c))],
        out_specs=[col, pl.BlockSpec((8, CONV_CB), lambda c: (0, c)), pl.BlockSpec((1, CONV_CB), lambda c: (0, c))],
        out_shape=[jax.ShapeDtypeStruct((S, CONV_DIM), bf16), jax.ShapeDtypeStruct((8, CONV_DIM), f32),
                   jax.ShapeDtypeStruct((1, CONV_DIM), f32)],
        scratch_shapes=[pltpu.VMEM((S + HALO, CONV_CB), f32)],
        compiler_params=_params(("parallel",)),
    )(proj, xc, dxbc, conv_w)


N_PAIR = SSD_HEADS // 2
NEG = -1e30


def _softplus(x):
    return jnp.maximum(x, 0.0) + jnp.log(1.0 + jnp.exp(-jnp.abs(x)))


def _ssd_common(dtr, dtb, alog):
    L = SSD_L
    r_i = lax.broadcasted_iota(jnp.int32, (L, L), 0)
    c_i = lax.broadcasted_iota(jnp.int32, (L, L), 1)
    dt = _softplus(dtr + dtb)
    a = -jnp.exp(alog)
    da = dt * a
    lower = (r_i >= c_i).astype(bf16)
    upper = (r_i <= c_i).astype(bf16)
    parts = _split3(da)
    a_cs = sum(_dot(lower, p) for p in parts)
    a_cs_t = sum(_dot(p, upper, TN) for p in parts)
    return dt, a, a_cs, a_cs_t, r_i >= c_i


def _pair_vec(lane, v, h):
    return jnp.where(lane < SB_HD, v[:, h:h + 1], v[:, h + 1:h + 2])


def _decay_mat(a_cs, a_cs_t, h, tril):
    return jnp.exp(jnp.where(tril, a_cs[:, h:h + 1] - a_cs_t[h:h + 1, :], NEG))


def _ssd_fwd(xbc, proj, pdt, dt_bias_p, a_log_p, d_skip_c, ssd_norm, S, ride=None):
    L = SSD_L
    nc = S // L
    n_r = ride.n if ride else 0

    def body(xbc_ref, dt_ref, z_ref, dtb_ref, alog_ref, dsk_ref, gn_ref, *rest):
        y_ref, yn_ref, hp_ref = rest[n_r:n_r + 3]
        state = rest[2 * n_r + 3]
        r_ins, r_lnd, r_sems = rest[:n_r], rest[n_r + 3:2 * n_r + 3], rest[2 * n_r + 4:]
        c = pl.program_id(0)
        if ride:
            pl.when(c == 0)(lambda: ride.start(r_ins, r_lnd, r_sems))

        @pl.when(c == 0)
        def _():
            state[...] = jnp.zeros_like(state)

        hp_ref[0] = state[...]
        lane = lax.broadcasted_iota(jnp.int32, (1, 128), 1)
        row128 = lax.broadcasted_iota(jnp.int32, (128, 1), 0)
        m_a, m_b = _sb_masks()
        dt, a, a_cs, a_cs_t, tril = _ssd_common(dt_ref[...], dtb_ref[...], alog_ref[...])
        a_last = a_cs[L - 1:L, :]
        for g in range(SSD_GROUPS):
            b_g = xbc_ref[:, SSD_INNER + g * SSD_N:SSD_INNER + (g + 1) * SSD_N].astype(bf16)
            c_g = xbc_ref[:, SSD_INNER + (SSD_GROUPS + g) * SSD_N:SSD_INNER + (SSD_GROUPS + g + 1) * SSD_N].astype(bf16)
            cb = _dot(c_g, b_g, NT)
            for pr in range(4):
                h = 8 * g + 2 * pr
                pi = h // 2
                cols = slice(pi * 128, (pi + 1) * 128)
                xs = xbc_ref[:, cols]
                x = xs * _pair_vec(lane, dt, h)
                acs = _pair_vec(lane, a_cs, h)
                al = _pair_vec(lane, a_last, h)
                w_a = (cb * _decay_mat(a_cs, a_cs_t, h, tril)).astype(bf16)
                w_b = (cb * _decay_mat(a_cs, a_cs_t, h + 1, tril)).astype(bf16)
                yd = _dot(w_a, (x * m_a).astype(bf16)) + _dot(w_b, (x * m_b).astype(bf16))
                hp = state[pi]
                yo = _dot(c_g, hp.astype(bf16), NT) * jnp.exp(acs)
                y_ref[:, cols] = yd + yo + dsk_ref[:, cols] * xs
                dec = jnp.exp(jnp.where(row128 < SB_HD, a_last[:, h:h + 1], a_last[:, h + 1:h + 2]))
                state[pi] = hp * dec + _dot((x * jnp.exp(al - acs)).astype(bf16), b_g, TN)
        zz = z_ref[...]
        y2 = y_ref[...] * (zz * _sigmoid(zz))
        gw = SSD_INNER // SSD_GROUPS
        for g in range(SSD_GROUPS):
            yg = y2[:, g * gw:(g + 1) * gw]
            rg = lax.rsqrt(jnp.mean(yg * yg, axis=1, keepdims=True) + EPS)
            yn_ref[:, g * gw:(g + 1) * gw] = (yg * rg * gn_ref[:, g * gw:(g + 1) * gw]).astype(bf16)
        if ride:
            pl.when(c == nc - 1)(lambda: ride.finish(r_ins, r_lnd, r_sems))

    vec128 = pl.BlockSpec((1, 128), lambda c: (0, 0))
    vecin = pl.BlockSpec((1, SSD_INNER), lambda c: (0, 0))
    rows = pl.BlockSpec((L, SSD_INNER), lambda c: (c, 0))
    out = pl.pallas_call(
        body, name="ssd_fwd", grid=(nc,),
        in_specs=[pl.BlockSpec((L, CONV_DIM), lambda c: (c, 0)),
                  pl.BlockSpec((L, 128), lambda c: (c, 0)),
                  pl.BlockSpec((L, SSD_INNER), lambda c: (c, P_Z // SSD_INNER)),
                  vec128, vec128, vecin, vecin] + (ride.in_specs if ride else []),
        out_specs=[rows, rows, pl.BlockSpec((1, N_PAIR, 128, SSD_N), lambda c: (c, 0, 0, 0))]
        + (ride.out_specs if ride else []),
        out_shape=[jax.ShapeDtypeStruct((S, SSD_INNER), f32), jax.ShapeDtypeStruct((S, SSD_INNER), bf16),
                   jax.ShapeDtypeStruct((nc, N_PAIR, 128, SSD_N), f32)] + (ride.out_shape if ride else []),
        scratch_shapes=[pltpu.VMEM((N_PAIR, 128, SSD_N), f32)] + (ride.scratch if ride else []),
        compiler_params=_params(("arbitrary",)),
    )(xbc, pdt, proj, dt_bias_p, a_log_p, d_skip_c, ssd_norm, *(ride.srcs if ride else []))
    return out[0], out[1], out[2], list(out[3:])


def _sum_all(v):
    return jnp.sum(jnp.sum(v, axis=1, keepdims=True), axis=0, keepdims=True)


def _ssd_bwd(dyn, y, xbc, proj, pdt, hprev, dt_bias_p, a_log_p, d_skip_c, ssd_norm, S, ride=None):
    L = SSD_L
    nc = S // L
    n_r = ride.n if ride else 0

    col = lax.broadcasted_iota(jnp.int32, (2 * SSD_INNER, 128), 0)
    head = lax.broadcasted_iota(jnp.int32, (2 * SSD_INNER, 128), 1)
    sel_pair = (col[:SSD_INNER] // SB_HD == head[:SSD_INNER]).astype(bf16)
    sel_head = (col // 128 == head).astype(bf16)

    def body(*refs):
        (dyn_ref, y_ref, xbc_ref, dt_ref, z_ref, hp_ref, dtb_ref, alog_ref, dsk_ref, gn_ref,
         selp_ref, selh_ref) = refs[:12]
        dz_ref, dxbc_ref, ddt_ref, dgn_ref, dsk_out, dalog_ref, ddtb_ref = refs[12 + n_r:19 + n_r]
        dstate, dy_s, st_a, st_q, st_d, st_x, dat = refs[19 + 2 * n_r:26 + 2 * n_r]
        r_ins, r_lnd, r_sems = refs[12:12 + n_r], refs[19 + n_r:19 + 2 * n_r], refs[26 + 2 * n_r:]
        c = pl.program_id(0)
        if ride:
            pl.when(c == 0)(lambda: ride.start(r_ins, r_lnd, r_sems))

        @pl.when(c == 0)
        def _():
            dat[...] = jnp.zeros_like(dat)
            dstate[...] = jnp.zeros_like(dstate)
            dgn_ref[...] = jnp.zeros_like(dgn_ref)
            dsk_out[...] = jnp.zeros_like(dsk_out)
            dalog_ref[...] = jnp.zeros_like(dalog_ref)
            ddtb_ref[...] = jnp.zeros_like(ddtb_ref)

        lane = lax.broadcasted_iota(jnp.int32, (1, 128), 1)
        row128 = lax.broadcasted_iota(jnp.int32, (128, 1), 0)
        rowl = lax.broadcasted_iota(jnp.int32, (L, 1), 0)
        m_a, m_b = _sb_masks()
        dtr = dt_ref[...]
        dt, a, a_cs, a_cs_t, tril = _ssd_common(dtr, dtb_ref[...], alog_ref[...])
        a_last = a_cs[L - 1:L, :]

        zz = z_ref[...]
        sg = _sigmoid(zz)
        silu = zz * sg
        yv = y_ref[...]
        y2 = yv * silu
        gw = SSD_INNER // SSD_GROUPS
        for g in range(SSD_GROUPS):
            sl = slice(g * gw, (g + 1) * gw)
            yg = y2[:, sl]
            rg = lax.rsqrt(jnp.mean(yg * yg, axis=1, keepdims=True) + EPS)
            yh = yg * rg
            dyn_g = dyn_ref[:, sl]
            dgn_ref[:, sl] += jnp.sum(dyn_g * yh, axis=0, keepdims=True)
            dyh = dyn_g * gn_ref[:, sl]
            dy2 = rg * (dyh - yh * jnp.mean(dyh * yh, axis=1, keepdims=True))
            dy_s[:, sl] = dy2 * silu[:, sl]
            dz_ref[:, sl] = (dy2 * yv[:, sl] * (sg[:, sl] * (1.0 + zz[:, sl] * (1.0 - sg[:, sl])))).astype(bf16)

        last_row = jnp.zeros((1, 128), f32)
        dsk_acc = jnp.zeros((1, 128), f32)
        for g in range(SSD_GROUPS):
            bsl = slice(SSD_INNER + g * SSD_N, SSD_INNER + (g + 1) * SSD_N)
            csl = slice(SSD_INNER + (SSD_GROUPS + g) * SSD_N, SSD_INNER + (SSD_GROUPS + g + 1) * SSD_N)
            b_g = xbc_ref[:, bsl].astype(bf16)
            c_g = xbc_ref[:, csl].astype(bf16)
            cb = _dot(c_g, b_g, NT)
            dcb = jnp.zeros((L, L), f32)
            dc_g = jnp.zeros((L, SSD_N), f32)
            db_g = jnp.zeros((L, SSD_N), f32)
            for pr in range(4):
                h = 8 * g + 2 * pr
                pi = h // 2
                cols = slice(pi * 128, (pi + 1) * 128)
                xs = xbc_ref[:, cols]
                dt_p = _pair_vec(lane, dt, h)
                x = xs * dt_p
                acs = _pair_vec(lane, a_cs, h)
                al = _pair_vec(lane, a_last, h)
                e_a = jnp.exp(acs)
                dte = jnp.exp(al - acs)
                m_mat_a = _decay_mat(a_cs, a_cs_t, h, tril)
                m_mat_b = _decay_mat(a_cs, a_cs_t, h + 1, tril)
                dyp = dy_s[:, cols]
                dsk = dsk_ref[:, cols]
                d_hn = dstate[pi]
                hp = hp_ref[0, pi]
                dy_a = (dyp * m_a).astype(bf16)
                dy_b = (dyp * m_b).astype(bf16)
                x_b = x.astype(bf16)
                gm_a = _dot(dy_a, x_b, NT) * m_mat_a
                gm_b = _dot(dy_b, x_b, NT) * m_mat_b
                dcb = dcb + gm_a + gm_b
                dx_d = _dot((cb * m_mat_a).astype(bf16), dy_a, TN) + _dot((cb * m_mat_b).astype(bf16), dy_b, TN)
                dx_s = _dot(b_g, d_hn.astype(bf16), NT) * dte
                dx = dx_d + dx_s
                dxbc_ref[:, cols] = dx * dt_p + dsk * dyp
                xdxs = x * dx_s
                st_x[:, cols] = xdxs
                st_a[:, cols] = dyp * (_dot(c_g, hp.astype(bf16), NT) * e_a) - xdxs
                st_d[:, cols] = dx * xs
                hh = d_hn * hp
                dsk_row = jnp.sum(dyp * xs, axis=0, keepdims=True)
                dec = jnp.exp(jnp.where(row128 < SB_HD, a_last[:, h:h + 1], a_last[:, h + 1:h + 2]))
                for hd, m, gm in ((h, m_a, gm_a), (h + 1, m_b, gm_b)):
                    half = slice(0, SB_HD) if hd == h else slice(SB_HD, 128)
                    qm = gm * cb
                    st_q[:, hd * 128:(hd + 1) * 128] = qm
                    dat[hd:hd + 1, :] = jnp.sum(qm, axis=0, keepdims=True)
                    hh_sum = jnp.sum(jnp.sum(hh[half, :], axis=0, keepdims=True), axis=1, keepdims=True)
                    last_row = jnp.where(lane == hd, jnp.exp(a_last[:, hd:hd + 1]) * hh_sum, last_row)
                    dsk_acc = jnp.where(lane == hd, jnp.sum(dsk_row * m, axis=1, keepdims=True), dsk_acc)
                dye = (dyp * e_a).astype(bf16)
                dc_g = dc_g + _dot(dye, hp.astype(bf16))
                db_g = db_g + _dot((x * dte).astype(bf16), d_hn.astype(bf16))
                dstate[pi] = dec * d_hn + _dot(dye, c_g, TN)
            dcb_b = dcb.astype(bf16)
            dxbc_ref[:, csl] = dc_g + _dot(dcb_b, b_g)
            dxbc_ref[:, bsl] = db_g + _dot(dcb_b, c_g, TN)

        r_i = lax.broadcasted_iota(jnp.int32, (L, L), 0)
        c_i = lax.broadcasted_iota(jnp.int32, (L, L), 1)
        rev = (r_i <= c_i).astype(bf16)

        def head_sums(st, sel, split=_split2):
            return sum(_dot(p, sel[...]) for p in split(st[...]))

        last_row = last_row + jnp.sum(head_sums(st_x, selp_ref), axis=0, keepdims=True)
        d_acs = (head_sums(st_a, selp_ref) + head_sums(st_q, selh_ref, _split3)
                 + jnp.where(rowl == L - 1, last_row, 0.0))
        ddt_x = head_sums(st_d, selp_ref)
        dda = sum(_dot(rev, p) for p in _split3(d_acs)) - sum(_dot(rev, p, NT) for p in _split3(dat[...]))
        ddt = ddt_x + dda * a
        dalog_ref[...] += jnp.sum(dda * dt, axis=0, keepdims=True) * a
        ddtr = jnp.where(lane < SSD_HEADS, ddt * _sigmoid(dtr + dtb_ref[...]), 0.0)
        ddt_ref[...] = ddtr.astype(bf16)
        ddtb_ref[...] += jnp.sum(ddtr, axis=0, keepdims=True)
        dsk_out[...] += dsk_acc
        if ride:
            pl.when(c == nc - 1)(lambda: ride.finish(r_ins, r_lnd, r_sems))

    rv = lambda c: nc - 1 - c
    vec128 = pl.BlockSpec((1, 128), lambda c: (0, 0))
    vecin = pl.BlockSpec((1, SSD_INNER), lambda c: (0, 0))
    rows = pl.BlockSpec((L, SSD_INNER), lambda c: (rv(c), 0))
    return pl.pallas_call(
        body, name="ssd_bwd", grid=(nc,),
        in_specs=[rows, rows,
                  pl.BlockSpec((L, CONV_DIM), lambda c: (rv(c), 0)),
                  pl.BlockSpec((L, 128), lambda c: (rv(c), 0)),
                  pl.BlockSpec((L, SSD_INNER), lambda c: (rv(c), P_Z // SSD_INNER)),
                  pl.BlockSpec((1, N_PAIR, 128, SSD_N), lambda c: (rv(c), 0, 0, 0)),
                  vec128, vec128, vecin, vecin,
                  pl.BlockSpec((SSD_INNER, 128), lambda c: (0, 0)),
                  pl.BlockSpec((2 * SSD_INNER, 128), lambda c: (0, 0))] + (ride.in_specs if ride else []),
        out_specs=[rows, pl.BlockSpec((L, CONV_DIM), lambda c: (rv(c), 0)),
                   pl.BlockSpec((L, 128), lambda c: (rv(c), 0)), vecin, vec128, vec128, vec128]
        + (ride.out_specs if ride else []),
        out_shape=[jax.ShapeDtypeStruct((S, SSD_INNER), bf16), jax.ShapeDtypeStruct((S, CONV_DIM), f32),
                   jax.ShapeDtypeStruct((S, 128), bf16), jax.ShapeDtypeStruct((1, SSD_INNER), f32),
                   jax.ShapeDtypeStruct((1, 128), f32), jax.ShapeDtypeStruct((1, 128), f32),
                   jax.ShapeDtypeStruct((1, 128), f32)] + (ride.out_shape if ride else []),
        scratch_shapes=[pltpu.VMEM((N_PAIR, 128, SSD_N), f32), pltpu.VMEM((L, SSD_INNER), f32),
                        pltpu.VMEM((L, SSD_INNER), f32), pltpu.VMEM((L, 2 * SSD_INNER), f32),
                        pltpu.VMEM((L, SSD_INNER), f32), pltpu.VMEM((L, SSD_INNER), f32),
                        pltpu.VMEM((128, L), f32)]
        + (ride.scratch if ride else []),
        compiler_params=_params(("arbitrary",)),
    )(dyn, y, xbc, pdt, proj, hprev, dt_bias_p, a_log_p, d_skip_c, ssd_norm, sel_pair, sel_head,
      *(ride.srcs if ride else []))


MEM_W = MEM_HEADS * MEM_HD


def _mem_probs(q, k):
    s = _dot(q, k, NT) * (MEM_HD ** -0.5)
    s = s - jnp.max(s, axis=1, keepdims=True)
    p = jnp.exp(s)
    return p / jnp.sum(p, axis=1, keepdims=True)


def _mem_fwd(proj, kv, S, tm=512):
    tm = min(tm, S)
    M = kv.shape[0]

    def body(q_ref, kv_ref, o_ref):
        for h in range(MEM_HEADS):
            sl = slice(h * MEM_HD, (h + 1) * MEM_HD)
            vsl = slice(MEM_W + h * MEM_HD, MEM_W + (h + 1) * MEM_HD)
            p = _mem_probs(q_ref[:, sl].astype(bf16), kv_ref[:, sl].astype(bf16))
            o_ref[:, sl] = _dot(p.astype(bf16), kv_ref[:, vsl].astype(bf16)).astype(bf16)

    return pl.pallas_call(
        body, name="mem_fwd", grid=(S // tm,),
        in_specs=[pl.BlockSpec((tm, MEM_W), lambda i: (i, P_MEMQ // MEM_W)),
                  pl.BlockSpec((M, 2 * MEM_W), lambda i: (0, 0))],
        out_specs=pl.BlockSpec((tm, MEM_W), lambda i: (i, 0)),
        out_shape=jax.ShapeDtypeStruct((S, MEM_W), bf16),
        compiler_params=_params(("parallel",)),
    )(proj, kv)


def _mem_bwd(proj, kv, dy, S, tm=512):
    tm = min(tm, S)
    M = kv.shape[0]
    scale = MEM_HD ** -0.5

    def body(q_ref, kv_ref, dy_ref, dq_ref, dkv_ref):
        @pl.when(pl.program_id(0) == 0)
        def _():
            dkv_ref[...] = jnp.zeros_like(dkv_ref)

        for h in range(MEM_HEADS):
            sl = slice(h * MEM_HD, (h + 1) * MEM_HD)
            vsl = slice(MEM_W + h * MEM_HD, MEM_W + (h + 1) * MEM_HD)
            q = q_ref[:, sl].astype(bf16)
            k = kv_ref[:, sl].astype(bf16)
            v = kv_ref[:, vsl].astype(bf16)
            dyh = dy_ref[:, sl].astype(bf16)
            p = _mem_probs(q, k)
            dp = _dot(dyh, v, NT)
            ds = (p * (dp - jnp.sum(dp * p, axis=1, keepdims=True)) * scale).astype(bf16)
            dq_ref[:, sl] = _dot(ds, k).astype(bf16)
            dkv_ref[:, sl] += _dot(ds, q, TN)
            dkv_ref[:, vsl] += _dot(p.astype(bf16), dyh, TN)

    return pl.pallas_call(
        body, name="mem_bwd", grid=(S // tm,),
        in_specs=[pl.BlockSpec((tm, MEM_W), lambda i: (i, P_MEMQ // MEM_W)),
                  pl.BlockSpec((M, 2 * MEM_W), lambda i: (0, 0)),
                  pl.BlockSpec((tm, MEM_W), lambda i: (i, 0))],
        out_specs=[pl.BlockSpec((tm, MEM_W), lambda i: (i, 0)), pl.BlockSpec((M, 2 * MEM_W), lambda i: (0, 0))],
        out_shape=[jax.ShapeDtypeStruct((S, MEM_W), bf16), jax.ShapeDtypeStruct((M, 2 * MEM_W), f32)],
        compiler_params=_params(("arbitrary",)),
    )(proj, kv, dy)


def _merge_fwd(proj, t0, t1, t2, S, tm=512):
    tm = min(tm, S)

    def body(g_ref, t0_ref, t1_ref, t2_ref, o_ref):
        acc = jnp.zeros((tm, D), f32)
        for b, t_ref in enumerate((t0_ref, t1_ref, t2_ref)):
            acc = acc + _sigmoid(g_ref[:, b * D:(b + 1) * D]) * t_ref[...]
        o_ref[...] = acc.astype(bf16)

    row = pl.BlockSpec((tm, D), lambda i: (i, 0))
    return pl.pallas_call(
        body, name="merge_fwd", grid=(S // tm,),
        in_specs=[pl.BlockSpec((tm, 3 * D), lambda i: (i, P_GATE // (3 * D))), row, row, row],
        out_specs=row, out_shape=jax.ShapeDtypeStruct((S, D), bf16),
        compiler_params=_params(("parallel",)),
    )(proj, t0, t1, t2)


def _merge_bwd(proj, t0, t1, t2, dm, S, tm=512):
    tm = min(tm, S)

    def body(g_ref, t0_ref, t1_ref, t2_ref, dm_ref, d0_ref, d1_ref, d2_ref, dg_ref):
        dmv = dm_ref[...]
        for b, (t_ref, d_ref) in enumerate(((t0_ref, d0_ref), (t1_ref, d1_ref), (t2_ref, d2_ref))):
            sg = _sigmoid(g_ref[:, b * D:(b + 1) * D])
            d_ref[...] = (dmv * sg).astype(bf16)
            dg_ref[:, b * D:(b + 1) * D] = (dmv * t_ref[...] * sg * (1.0 - sg)).astype(bf16)

    row = pl.BlockSpec((tm, D), lambda i: (i, 0))
    return pl.pallas_call(
        body, name="merge_bwd", grid=(S // tm,),
        in_specs=[pl.BlockSpec((tm, 3 * D), lambda i: (i, P_GATE // (3 * D))), row, row, row, row],
        out_specs=[row, row, row, pl.BlockSpec((tm, 3 * D), lambda i: (i, 0))],
        out_shape=[jax.ShapeDtypeStruct((S, D), bf16)] * 3 + [jax.ShapeDtypeStruct((S, 3 * D), bf16)],
        compiler_params=_params(("parallel",)),
    )(proj, t0, t1, t2, dm)


def _loss_head(ff, g, h1, target, S, tm=512):
    tm = min(tm, S)

    def body(ff_ref, g_ref, h1_ref, t_ref, dh_ref, loss_ref):
        xv = ff_ref[...]
        r = lax.rsqrt(jnp.mean(xv * xv, axis=1, keepdims=True) + EPS)
        err = h1_ref[...] + xv * r * g_ref[...] - t_ref[...]
        dh_ref[...] = err * (1.0 / D)

        @pl.when(pl.program_id(0) == 0)
        def _():
            loss_ref[...] = jnp.zeros_like(loss_ref)

        loss_ref[...] += 0.5 * _sum_all(jnp.mean(err * err, axis=1, keepdims=True)) * jnp.ones((1, 128), f32)

    row = pl.BlockSpec((tm, D), lambda i: (i, 0))
    return pl.pallas_call(
        body, name="loss_head", grid=(S // tm,),
        in_specs=[row, pl.BlockSpec((1, D), lambda i: (0, 0)), row, row],
        out_specs=[row, pl.BlockSpec((1, 128), lambda i: (0, 0))],
        out_shape=[jax.ShapeDtypeStruct((S, D), f32), jax.ShapeDtypeStruct((1, 128), f32)],
        compiler_params=_params(("arbitrary",)),
    )(ff, g, h1, target)


def _local_step(x, mem, target, wts, late_rides, late_weights, small, rest_rides, w_in_ride):
    S = x.shape[0]
    M = mem.shape[0]
    pad = lambda v: jnp.pad(v, ((0, 0), (0, 128 - SSD_HEADS)))
    dtb_p, alog_p = pad(small["dt_bias"]), pad(small["a_log"])
    dsk_c = jnp.repeat(small["d_skip"], SB_HD, axis=1)

    u = _rms_fwd(x, small["norm_mix_pre"], name="norm_pre", out_dtype=bf16)
    rides = late_rides or (None, None, None, None)
    if late_rides:
        proj, lands_a = _mm(u, wts["w_main"], "nn", tm=1024, tn=1024, name="in_proj", ride=rides[0])
    else:
        proj, lands_a = _mm(u, wts["w_main"], "nn", tm=1024, tn=1024, name="in_proj"), []
    pdt = _mm(u, wts["w_dt"], "nn", tm=1024, tn=128, name="in_proj_dt")
    y_sb, tot_lk, lands_b = _sb_fwd(proj, S, rides[1])
    wts = dict(wts, **late_weights(0, lands_a))
    small = dict(small, conv_w=wts.pop("conv_w"))
    xc, xbc, lands_d = _conv_fwd(proj, small["conv_w"], small["conv_b"], S, rides[3])
    y_ssd, yn, hprev, lands_c = _ssd_fwd(xbc, proj, pdt, dtb_p, alog_p, dsk_c, small["ssd_norm"], S, rides[2])
    wts = dict(wts, **late_weights(1, lands_b), **late_weights(2, lands_c), **late_weights(3, lands_d))
    mn = _rms_fwd(mem, small["norm_mem"], name="norm_mem", out_dtype=bf16, tm=min(512, M))
    kv = _mm(mn, wts["w_mem_kv"], "nn", tm=M, tn=1024, name="mem_kv")
    y_mem = _mem_fwd(proj, kv, S)
    t0 = _mm(y_sb, wts["w_sb_out"], "nn", tm=1024, tn=1024, name="sb_out")
    t1 = _mm(yn, wts["w_ssd_out"], "nn", tm=1024, tn=1024, name="ssd_out")
    t2 = _mm(y_mem, wts["w_mem_out"], "nn", tm=1024, tn=1024, name="mem_out")
    merged = _merge_fwd(proj, t0, t1, t2, S)
    mix = _mm(merged, wts["w_o"], "nn", tm=1024, tn=1024, name="w_o")
    h1 = _rms_fwd(mix, small["norm_mix_post"], name="norm_mix_post", out_dtype=f32, residual=x)
    u2 = _rms_fwd(h1, small["norm_mlp_pre"], name="norm_mlp_pre", out_dtype=bf16)
    a_up, hrelu = _mm(u2, wts["w_up"], "nn", tm=1024, tn=1024, name="mlp_up", out_dtypes=(f32, bf16),
                      epi=lambda acc: (acc, jnp.square(jnp.maximum(acc, 0.0))))
    ff = _mm(hrelu, wts["w_down"], "nn", tm=1024, tn=1024, name="mlp_down")
    dh2, loss = _loss_head(ff, small["norm_mlp_post"], h1, target, S)

    g = {}
    dff, g["norm_mlp_post"] = _rms_bwd(ff, dh2, small["norm_mlp_post"], name="norm_mlp_post_bwd", out_dtype=bf16)
    da = _mm(dff, wts["w_down"], "nt", tm=1024, tn=1024, name="mlp_down_dx", out_dtypes=(bf16,),
             epi=lambda acc, a: (acc * (2.0 * jnp.maximum(a, 0.0)),), extras=(a_up,))
    g["w_down"] = _mm(hrelu, dff, "tn", tm=1024, tn=1024, name="mlp_down_dw")
    du2 = _mm(da, wts["w_up"], "nt", tm=1024, tn=1024, name="mlp_up_dx")
    g["w_up"] = _mm(u2, da, "tn", tm=1024, tn=1024, name="mlp_up_dw")
    dh1, g["norm_mlp_pre"] = _rms_bwd(h1, du2, small["norm_mlp_pre"], name="norm_mlp_pre_bwd", out_dtype=f32, add=dh2)
    dmix, g["norm_mix_post"] = _rms_bwd(mix, dh1, small["norm_mix_post"], name="norm_mix_post_bwd", out_dtype=bf16)
    dmerged = _mm(dmix, wts["w_o"], "nt", tm=1024, tn=1024, name="w_o_dx")
    g["w_o"] = _mm(merged, dmix, "tn", tm=1024, tn=1024, name="w_o_dw")
    dt0, dt1, dt2, dgl = _merge_bwd(proj, t0, t1, t2, dmerged, S)
    dy_sb = _mm(dt0, wts["w_sb_out"], "nt", tm=1024, tn=1024, name="sb_out_dx")
    g["w_sb_out"] = _mm(y_sb, dt0, "tn", tm=1024, tn=1024, name="sb_out_dw")
    dy_ssd = _mm(dt1, wts["w_ssd_out"], "nt", tm=1024, tn=1024, name="ssd_out_dx")
    g["w_ssd_out"] = _mm(yn, dt1, "tn", tm=1024, tn=1024, name="ssd_out_dw")
    dy_mem = _mm(dt2, wts["w_mem_out"], "nt", tm=1024, tn=1024, name="mem_out_dx")
    g["w_mem_out"] = _mm(y_mem, dt2, "tn", tm=1024, tn=1024, name="mem_out_dw")
    dmemq, dkv = _mem_bwd(proj, kv, dy_mem, S)
    g["w_mem_kv"] = _mm(mn, dkv, "tn", tm=1024, tn=1024, name="mem_kv_dw")
    dmn = _mm(dkv, wts["w_mem_kv"], "nt", tm=M, tn=1024, name="mem_kv_dx")
    _, g["norm_mem"] = _rms_bwd(mem, dmn, small["norm_mem"], name="norm_mem_bwd", out_dtype=bf16, tm=min(512, M))
    rides = rest_rides(g) if rest_rides else (None, None)
    dz, dxbc, ddt, g["ssd_norm"], dsk, dalog, ddtb, *lands_a = _ssd_bwd(
        dy_ssd, y_ssd, xbc, proj, pdt, hprev, dtb_p, alog_p, dsk_c, small["ssd_norm"], S, rides[0])
    g["d_skip"], g["a_log"], g["dt_bias"] = dsk[:, :SSD_HEADS], dalog[:, :SSD_HEADS], ddtb[:, :SSD_HEADS]
    dxbc_raw, dcw, g["conv_b"] = _conv_bwd(proj, xc, dxbc, small["conv_w"], S)
    g["conv_w"] = dcw[:CONV_K]
    dq, dk, dv, lands_b = _sb_bwd(proj, tot_lk, dy_sb, S, rides[1])
    g["rest_lands"] = lands_b + lands_a
    dproj = (dq, dk, dv, dxbc_raw, dgl, dmemq, dz)
    u_t = u.T
    g["w_main"] = [_mm(u_t, p, "nn", tm=512, tn=1024, name="in_proj_dw_%d" % i) for i, p in enumerate(dproj)]
    g["w_dt"] = _mm(u_t, ddt, "nn", tm=512, tn=128, name="in_proj_dt_dw")
    du_dt = _mm(ddt, wts["w_dt"], "nt", tm=1024, tn=1024, name="in_proj_dt_dx")
    du, g["w_in_lands"] = _mm_pieces_nt(dproj, wts["w_main"], du_dt, tm=512, tn=256, name="in_proj_dx",
                                        ride=w_in_ride(g) if w_in_ride else None)
    grad_x, g["norm_mix_pre"] = _rms_bwd(x, du, small["norm_mix_pre"], name="norm_pre_bwd", out_dtype=f32, add=dh1)
    return loss, grad_x, g


def _to_internal(w_in):
    sec = lambda r: w_in[:, r[0]:r[1]]
    w_main = jnp.concatenate([sec(R_QKV), sec(R_XBC), sec(R_GATE), sec(R_MEMQ), sec(R_Z)], axis=1)
    w_dt = jnp.pad(sec(R_DT), ((0, 0), (0, 128 - SSD_HEADS)))
    return w_main, w_dt


def _from_internal(pieces, g_dt):
    dq, dk, dv, dxbc, dgate, dmemq, dz = pieces
    return [dq, dk, dv, dz, dxbc, g_dt[:, :SSD_HEADS], dmemq, dgate]


def _w_in_slab(ordered, s, dtype):
    width = D_IN // N_SHARD
    lo, hi, off, parts = s * width, (s + 1) * width, 0, []
    for p in ordered:
        a, b = max(lo, off), min(hi, off + p.shape[1])
        if a < b:
            parts.append(p[:, a - off:b - off].astype(dtype))
        off += p.shape[1]
    return jnp.concatenate(parts, axis=1)


MESH = pl.DeviceIdType.MESH
ANY = pl.BlockSpec(memory_space=pl.ANY)


def _place():
    x, y, c = lax.axis_index("x"), lax.axis_index("y"), lax.axis_index("c")
    return (x, y, c), [(1 - x, y, c), (x, 1 - y, c), (1 - x, 1 - y, c)]


def _exchange_copy(mode, ins, lands, send, recv, a, k, me, peers, arriving):
    p = peers[k]
    theirs = 2 * p[0] + p[1]
    if mode == "gather":
        src, dst = ins[a], lands[a].at[theirs if arriving else me]
    else:
        src, dst = ins[a].at[theirs], lands[a].at[k]
    return pltpu.make_async_remote_copy(src_ref=src, dst_ref=dst, send_sem=send.at[a * 3 + k],
                                        recv_sem=recv.at[a * 3 + k], device_id=p, device_id_type=MESH)


class _Ride:
    def __init__(self, srcs, mode):
        self.srcs, self.mode, self.n = list(srcs), mode, len(srcs)
        n = self.n
        self.in_specs, self.out_specs = [ANY] * n, [ANY] * n
        self.out_shape = [
            jax.ShapeDtypeStruct((N_SHARD,) + s.shape if mode == "gather" else (3,) + s.shape[1:], s.dtype)
            for s in self.srcs]
        self.scratch = [pltpu.SemaphoreType.DMA((3 * n,)), pltpu.SemaphoreType.DMA((3 * n,)),
                        pltpu.SemaphoreType.DMA((n,))]

    def _own(self, ins, lnd, sems):
        if self.mode != "gather":
            return []
        me = 2 * lax.axis_index("x") + lax.axis_index("y")
        return [pltpu.make_async_copy(ins[a], lnd[a].at[me], sems[2].at[a]) for a in range(self.n)]

    def _far(self, ins, lnd, sems, arriving):
        (x, y, c), peers = _place()
        return [_exchange_copy(self.mode, ins, lnd, sems[0], sems[1], a, k, 2 * x + y, peers, arriving)
                for a in range(self.n) for k in range(3)]

    def start(self, ins, lnd, sems):
        for cp in self._own(ins, lnd, sems) + self._far(ins, lnd, sems, False):
            cp.start()

    def finish(self, ins, lnd, sems):
        for cp in self._far(ins, lnd, sems, True):
            cp.wait_recv()
        for cp in self._far(ins, lnd, sems, False):
            cp.wait_send()
        for cp in self._own(ins, lnd, sems):
            cp.wait()


def _gather_two_level(shards, name):
    n = len(shards)

    def body(*refs):
        ins, lnd = refs[:n], refs[n:2 * n]
        send, recv, loc = refs[2 * n:]
        (x, y, c), peers = _place()
        me = 2 * x + y

        def half(ref, a, core):
            rows = shards[a].shape[0] // 2
            return ref.at[pl.ds(core * rows, rows)]

        def copy(a, j, slot, core, to):
            return pltpu.make_async_remote_copy(
                src_ref=half(ins[a], a, core) if j < 3 else half(lnd[a].at[slot], a, core),
                dst_ref=half(lnd[a].at[slot], a, core), send_sem=send.at[6 * a + j], recv_sem=recv.at[6 * a + j],
                device_id=to, device_id_type=MESH)

        own = [pltpu.make_async_copy(ins[a], lnd[a].at[me], loc.at[a]) for a in range(n)]
        far = [copy(a, k, me, c, peers[k]) for a in range(n) for k in range(3)]
        for cp in own + far:
            cp.start()
        passed = []
        for a in range(n):
            for k, p in enumerate(peers):
                theirs = 2 * p[0] + p[1]
                copy(a, k, theirs, c, p).wait_recv()
                passed.append(copy(a, 3 + k, theirs, c, (x, y, 1 - c)))
                passed[-1].start()
        for a in range(n):
            for k, p in enumerate(peers):
                copy(a, 3 + k, 2 * p[0] + p[1], 1 - c, (x, y, 1 - c)).wait_recv()
        for cp in far + passed:
            cp.wait_send()
        for cp in own:
            cp.wait()

    return pl.pallas_call(
        body, name=name, in_specs=[ANY] * n, out_specs=[ANY] * n,
        out_shape=[jax.ShapeDtypeStruct((N_SHARD,) + s.shape, s.dtype) for s in shards],
        scratch_shapes=[pltpu.SemaphoreType.DMA((6 * n,)), pltpu.SemaphoreType.DMA((6 * n,)),
                        pltpu.SemaphoreType.DMA((n,))],
    )(*shards)


def _exchange_packets(packet):
    def body(pk, pk_out, send, recv, loc):
        x, y, c = lax.axis_index("x"), lax.axis_index("y"), lax.axis_index("c")
        lin = 4 * x + 2 * y + c
        own = pltpu.make_async_copy(pk, pk_out.at[lin], loc.at[0])
        own.start()

        def pk_copy(m, slot):
            dev = (x ^ ((m >> 2) & 1), y ^ ((m >> 1) & 1), c ^ (m & 1))
            return pltpu.make_async_remote_copy(
                src_ref=pk, dst_ref=pk_out.at[slot], send_sem=send.at[m - 1], recv_sem=recv.at[m - 1],
                device_id=dev, device_id_type=MESH)

        sent = [pk_copy(m, lin) for m in range(1, N_DEV)]
        for cp in sent:
            cp.start()
        for m in range(1, N_DEV):
            pk_copy(m, lin ^ m).wait_recv()
        for cp in sent:
            cp.wait_send()
        own.wait()

    return pl.pallas_call(
        body, name="exchange_packets", in_specs=[ANY], out_specs=ANY,
        out_shape=jax.ShapeDtypeStruct((N_DEV,) + packet.shape, packet.dtype),
        scratch_shapes=[pltpu.SemaphoreType.DMA((N_DEV - 1,)), pltpu.SemaphoreType.DMA((N_DEV - 1,)),
                        pltpu.SemaphoreType.DMA((1,))],
    )(packet)


def _halves_to_full(mine, name):
    h, C = mine.shape

    def body(src, out, send, recv, loc):
        x, y, c = lax.axis_index("x"), lax.axis_index("y"), lax.axis_index("c")
        my_rows, their_rows = out.at[pl.ds(c * h, h)], out.at[pl.ds((1 - c) * h, h)]
        local = pltpu.make_async_copy(src, my_rows, loc.at[0])
        local.start()

        def to_sibling(dst):
            return pltpu.make_async_remote_copy(src_ref=src, dst_ref=dst, send_sem=send.at[0], recv_sem=recv.at[0],
                                                device_id=(x, y, 1 - c), device_id_type=MESH)

        to_sibling(my_rows).start()
        to_sibling(their_rows).wait_recv()
        to_sibling(my_rows).wait_send()
        local.wait()

    return pl.pallas_call(
        body, name=name, in_specs=[ANY], out_specs=ANY,
        out_shape=jax.ShapeDtypeStruct((2 * h, C), mine.dtype),
        scratch_shapes=[pltpu.SemaphoreType.DMA((1,)), pltpu.SemaphoreType.DMA((1,)), pltpu.SemaphoreType.DMA((1,))],
    )(mine)


def _swap_sibling(parts, name):
    n = len(parts)

    def body(*refs):
        ins, outs = refs[:n], refs[n:2 * n]
        send, recv = refs[2 * n:]
        x, y, c = lax.axis_index("x"), lax.axis_index("y"), lax.axis_index("c")
        cps = [pltpu.make_async_remote_copy(
            src_ref=ins[a], dst_ref=outs[a], send_sem=send.at[a], recv_sem=recv.at[a],
            device_id=(x, y, 1 - c), device_id_type=MESH) for a in range(n)]
        for cp in cps:
            cp.start()
        for cp in cps:
            cp.wait_recv()
        for cp in cps:
            cp.wait_send()

    return pl.pallas_call(
        body, name=name,
        in_specs=[ANY] * n, out_specs=[ANY] * n,
        out_shape=[jax.ShapeDtypeStruct(p.shape, p.dtype) for p in parts],
        scratch_shapes=[pltpu.SemaphoreType.DMA((n,)), pltpu.SemaphoreType.DMA((n,))],
    )(*parts)


BLOCK_ELEMS = 256 * 1024


def _row_tile(R, C):
    tr = max(8, (BLOCK_ELEMS // C) // 8 * 8)
    while R % tr:
        tr -= 8
    return min(tr, R)


def _sum_parts(own, stack, name, out_dtype=f32):
    k = stack.shape[0]
    R, C = stack.shape[1:]
    tr = _row_tile(R, C)

    def body(*refs):
        o_ref = refs[-1]
        acc = refs[0][...].astype(f32)
        for r in refs[1:-1]:
            acc = acc + r[...].astype(f32)
        o_ref[...] = acc.astype(out_dtype)

    row = pl.BlockSpec((tr, C), lambda i: (i, 0))
    specs = ([row] if own is not None else []) + [
        pl.BlockSpec((None, tr, C), functools.partial(lambda i, j: (j, i, 0), j=j)) for j in range(k)]
    args = ([own] if own is not None else []) + [stack] * k
    return pl.pallas_call(
        body, name=name, grid=(R // tr,), in_specs=specs, out_specs=row,
        out_shape=jax.ShapeDtypeStruct((R, C), out_dtype), compiler_params=_params(("parallel",)),
    )(*args)


def _adamw(w, m, v, g_parts, name):
    R, C = w.shape
    tr = _row_tile(R, C)
    n_g = len(g_parts)

    def body(w_ref, m_ref, v_ref, *rest):
        g = rest[0][...]
        for r in rest[1:n_g]:
            g = g + r[...]
        g_ref, d_ref, nm_ref, nv_ref = rest[n_g:]
        nm = ADAM_B1 * m_ref[...] + (1.0 - ADAM_B1) * g
        nv = ADAM_B2 * v_ref[...] + (1.0 - ADAM_B2) * jnp.square(g)
        m_hat = nm / (1.0 - ADAM_B1 ** ADAM_STEP)
        v_hat = nv / (1.0 - ADAM_B2 ** ADAM_STEP)
        g_ref[...] = g
        d_ref[...] = -ADAM_LR * (m_hat / (jnp.sqrt(v_hat) + ADAM_EPS) + ADAM_WD * w_ref[...])
        nm_ref[...] = nm
        nv_ref[...] = nv

    row = pl.BlockSpec((tr, C), lambda i: (i, 0))
    return pl.pallas_call(
        body, name=name, grid=(R // tr,), in_specs=[row] * (3 + n_g), out_specs=[row] * 4,
        out_shape=[jax.ShapeDtypeStruct((R, C), f32)] * 4, compiler_params=_params(("parallel",)),
    )(w, m, v, *g_parts)


BIG = ("w_in", "w_mem_kv", "w_sb_out", "w_ssd_out", "w_mem_out", "w_o", "w_up", "w_down")
LATE = ("w_sb_out", "w_ssd_out", "w_mem_out", "w_o", "w_up", "w_down")
REST = BIG[1:]
COL_SHARDED = ("w_in", "w_mem_kv", "w_up")
SMALL = ("norm_mix_pre", "conv_w", "conv_b", "dt_bias", "a_log", "d_skip", "ssd_norm", "norm_mem",
         "norm_mix_post", "norm_mlp_pre", "norm_mlp_post")
WEIGHTS = ("norm_mix_pre", "w_in", "conv_w", "conv_b", "dt_bias", "a_log", "d_skip", "ssd_norm", "norm_mem",
           "w_mem_kv", "w_sb_out", "w_ssd_out", "w_mem_out", "w_o", "norm_mix_post", "norm_mlp_pre", "w_up",
           "w_down", "norm_mlp_post")
PK_ROWS = 184


def _pack(vecs):
    flat = jnp.concatenate([v.reshape(-1) for v in vecs])
    return jnp.pad(flat, (0, PK_ROWS * 128 - flat.shape[0])).reshape(PK_ROWS, 128)


def _unpack(pk, shapes):
    flat = pk.reshape(-1)
    out, off = [], 0
    for s in shapes:
        n = 1
        for d in s:
            n *= d
        out.append(flat[off:off + n].reshape(s))
        off += n
    return out


def _full_from_slabs(name, slabs):
    if name in COL_SHARDED:
        return slabs.transpose(1, 0, 2).reshape(slabs.shape[1], -1)
    return slabs.reshape(-1, slabs.shape[2])


def _slabs_from_full(name, g):
    if name in COL_SHARDED:
        return g.reshape(g.shape[0], N_SHARD, -1).transpose(1, 0, 2)
    return g.reshape(N_SHARD, -1, g.shape[1])


def kernel(x, mem, norm_mix_pre, w_in, conv_w, conv_b, dt_bias, a_log, d_skip, ssd_norm, norm_mem, w_mem_kv, w_sb_out, w_ssd_out, w_mem_out, w_o, norm_mix_post, norm_mlp_pre, w_up, w_down, norm_mlp_post, loss_target, m_norm_mix_pre, m_w_in, m_conv_w, m_conv_b, m_dt_bias, m_a_log, m_d_skip, m_ssd_norm, m_norm_mem, m_w_mem_kv, m_w_sb_out, m_w_ssd_out, m_w_mem_out, m_w_o, m_norm_mix_post, m_norm_mlp_pre, m_w_up, m_w_down, m_norm_mlp_post, v_norm_mix_pre, v_w_in, v_conv_w, v_conv_b, v_dt_bias, v_a_log, v_d_skip, v_ssd_norm, v_norm_mem, v_w_mem_kv, v_w_sb_out, v_w_ssd_out, v_w_mem_out, v_w_o, v_norm_mix_post, v_norm_mlp_pre, v_w_up, v_w_down, v_norm_mlp_post):
    env = dict(locals())
    w = {n: env[n] for n in WEIGHTS}
    mo = {n: env["m_" + n] for n in WEIGHTS}
    vo = {n: env["v_" + n] for n in WEIGHTS}
    shard = 2 * lax.axis_index("x") + lax.axis_index("y")

    first = _gather_two_level([w["w_in"][0].astype(bf16)], "gather_first")
    w_main, w_dt = _to_internal(_full_from_slabs("w_in", first[0]))
    wts = dict(w_main=w_main, w_dt=w_dt)
    ride_names = (LATE[:4], LATE[4:5], LATE[5:], ("w_mem_kv",))
    late_rides = tuple(_Ride([w[n][0].astype(bf16) for n in names] + ([w["conv_w"][0]] if i == 0 else []), "gather")
                       for i, names in enumerate(ride_names))

    def late_weights(i, lands):
        full = {n: _full_from_slabs(n, s) for n, s in zip(ride_names[i], lands)}
        if i == 0:
            full["conv_w"] = lands[-1].transpose(1, 0, 2).reshape(CONV_K, CONV_DIM)
        return full

    def rest_rides(g):
        slabs = [_slabs_from_full(n, g[n]).astype(bf16) for n in REST]
        return _Ride(slabs[5:], "scatter"), _Ride(slabs[:5], "scatter")

    core = lax.axis_index("c")
    half = D // 2

    def w_in_ride(g):
        ordered = _from_internal(g["w_main"], g["w_dt"])
        stack = jnp.stack([_w_in_slab(ordered, s, bf16) for s in range(N_SHARD)])
        keep = lax.dynamic_slice_in_dim(stack, core * half, half, axis=1)
        away = lax.dynamic_slice_in_dim(stack, (1 - core) * half, half, axis=1)
        (got,) = _swap_sibling([away], "w_in_halves_out")
        wide = lambda a: a.reshape(N_SHARD * half, -1)
        chip = _sum_parts(wide(keep), wide(got)[None], "sum_cores_w_in", bf16).reshape(N_SHARD, half, -1)
        own = lax.switch(shard, [functools.partial(_w_in_slab, ordered, s, f32) for s in range(N_SHARD)])
        own = lax.dynamic_slice_in_dim(own, core * half, half, axis=0)
        g["w_in_own"] = _sum_parts(own, lax.dynamic_index_in_dim(got, shard, 0, keepdims=True), "sum_cores_w_in_own")
        return _Ride([chip], "scatter")

    small = {n: w[n] for n in SMALL if n != "conv_w"}
    loss, grad_x, g = _local_step(x[0], mem[0], loss_target[0], wts, late_rides, late_weights, small,
                                  rest_rides, w_in_ride)
    out_g, out_d, out_m, out_v = {}, {}, {}, {}

    def apply(n, g_parts):
        res = _adamw(w[n][0], mo[n][0], vo[n][0], g_parts, name="adamw_" + n)
        out_g[n], out_d[n], out_m[n], out_v[n] = [r[None] for r in res]

    mine = _sum_parts(g["w_in_own"], g["w_in_lands"][0], name="sum_chips_w_in")
    apply("w_in", [_halves_to_full(mine, "w_in_halves_back")])

    packets = _exchange_packets(_pack([g[n] for n in SMALL] + [loss[:, :1]]))
    partial = []
    for n, r in zip(REST, g["rest_lands"]):
        own = lax.dynamic_index_in_dim(_slabs_from_full(n, g[n]), shard, 0, keepdims=False)
        partial.append(_sum_parts(own, r, name="sum_chips_" + n))
    other = _swap_sibling(partial, "swap_sibling")

    for n, p, q in zip(REST, partial, other):
        apply(n, [p, q])
    tot = _sum_parts(None, packets, name="sum_packets")
    shapes = [g[n].shape for n in SMALL] + [(1, 1)]
    sm = dict(zip(SMALL + ("loss",), _unpack(tot, shapes)))
    sm["conv_w"] = lax.dynamic_slice_in_dim(sm["conv_w"], shard * (CONV_DIM // N_SHARD), CONV_DIM // N_SHARD, axis=1)
    own_small = lambda d: _pack([d[n].reshape(sm[n].shape) for n in SMALL])
    res = _adamw(own_small(w), own_small(mo), own_small(vo), [own_small(sm)], name="adamw_small")
    own_shapes = [sm[n].shape for n in SMALL]
    for store, r in zip((out_g, out_d, out_m, out_v), res):
        for n, val in zip(SMALL, _unpack(r, own_shapes)):
            store[n] = val.reshape(w[n].shape)

    outs = [sm["loss"].reshape(()), grad_x[None]]
    for store in (out_g, out_d, out_m, out_v):
        outs += [store[n] for n in WEIGHTS]
    return tuple(outs)
```

```python
import functools

import jax
import jax.numpy as jnp
from jax import lax
from jax.experimental import pallas as pl
from jax.experimental.pallas import tpu as pltpu

f32 = jnp.float32
bf16 = jnp.bfloat16

D = 1024
EPS = 1e-6
SB_HD = 64
SSD_INNER = 2048
SSD_HEADS = 32
SSD_GROUPS = 4
SSD_N = 128
SSD_L = 128
CONV_K = 4
CONV_DIM = 3072
MEM_HEADS = 4
MEM_HD = 256
D_FF = 4096
D_IN = 12320
N_SHARD = 4
N_DEV = 8

P_QKV, P_XBC, P_GATE, P_MEMQ, P_Z, P_DT, P_TOT = 0, 3072, 6144, 9216, 10240, 12288, 12416
R_QKV, R_Z, R_XBC, R_DT, R_MEMQ, R_GATE = (0, 3072), (3072, 5120), (5120, 8192), (8192, 8224), (8224, 9248), (9248, 12320)

ADAM_LR = 0.001
ADAM_B1 = 0.9
ADAM_B2 = 0.999
ADAM_EPS = 1e-08
ADAM_WD = 0.01
ADAM_STEP = 10

VMEM_LIMIT = 56 * 1024 * 1024

NN = (((1,), (0,)), ((), ()))
NT = (((1,), (1,)), ((), ()))
TN = (((0,), (0,)), ((), ()))


def _dot(a, b, dims=NN):
    return lax.dot_general(a, b, dims, preferred_element_type=f32)


def _params(sem=None):
    return pltpu.CompilerParams(dimension_semantics=sem, vmem_limit_bytes=VMEM_LIMIT)


def _sigmoid(x):
    return 1.0 / (1.0 + jnp.exp(-x))


def _split2(x):
    hi = x.astype(bf16)
    lo = (x - hi.astype(f32)).astype(bf16)
    return hi, lo


def _split3(x):
    hi = x.astype(bf16)
    r = x - hi.astype(f32)
    mid = r.astype(bf16)
    lo = (r - mid.astype(f32)).astype(bf16)
    return hi, mid, lo


def _mm(a, b, mode, *, tm, tn, name, out_dtypes=(f32,), epi=None, extras=(), ride=None):
    M = a.shape[1] if mode == "tn" else a.shape[0]
    N = b.shape[0] if mode == "nt" else b.shape[1]
    tm, tn = min(tm, M), min(tn, N)
    if mode == "nn":
        (M, K), N = a.shape, b.shape[1]
        a_spec = pl.BlockSpec((tm, K), lambda i, j: (i, 0))
        b_spec = pl.BlockSpec((K, tn), lambda i, j: (0, j))
        dims = NN
    elif mode == "nt":
        (M, K), N = a.shape, b.shape[0]
        a_spec = pl.BlockSpec((tm, K), lambda i, j: (i, 0))
        b_spec = pl.BlockSpec((tn, K), lambda i, j: (j, 0))
        dims = NT
    else:
        (K, M), N = a.shape, b.shape[1]
        a_spec = pl.BlockSpec((K, tm), lambda i, j: (0, i))
        b_spec = pl.BlockSpec((K, tn), lambda i, j: (0, j))
        dims = TN
    assert M % tm == 0 and N % tn == 0, (name, M, N, tm, tn)
    n_ex, n_out = len(extras), len(out_dtypes)
    n_r = ride.n if ride else 0
    o_spec = pl.BlockSpec((tm, tn), lambda i, j: (i, j))
    grid = (M // tm, N // tn)

    def body(a_ref, b_ref, *rest):
        r_ins = rest[n_ex:n_ex + n_r]
        outs = rest[n_ex + n_r:n_ex + n_r + n_out]
        r_lnd, r_sems = rest[n_ex + n_r + n_out:n_ex + 2 * n_r + n_out], rest[n_ex + 2 * n_r + n_out:]
        i, j = pl.program_id(0), pl.program_id(1)
        if ride:
            pl.when((i == 0) & (j == 0))(lambda: ride.start(r_ins, r_lnd, r_sems))
        acc = _dot(a_ref[...].astype(bf16), b_ref[...].astype(bf16), dims)
        res = (acc,) if epi is None else epi(acc, *[e[...] for e in rest[:n_ex]])
        for o_ref, r in zip(outs, res):
            o_ref[...] = r.astype(o_ref.dtype)
        if ride:
            pl.when((i == grid[0] - 1) & (j == grid[1] - 1))(lambda: ride.finish(r_ins, r_lnd, r_sems))

    out = pl.pallas_call(
        body, name=name, grid=grid,
        in_specs=[a_spec, b_spec] + [o_spec] * n_ex + (ride.in_specs if ride else []),
        out_specs=[o_spec] * n_out + (ride.out_specs if ride else []),
        out_shape=[jax.ShapeDtypeStruct((M, N), dt) for dt in out_dtypes] + (ride.out_shape if ride else []),
        scratch_shapes=ride.scratch if ride else [],
        compiler_params=_params(("arbitrary", "arbitrary") if ride else ("parallel", "parallel")),
    )(a, b, *extras, *(ride.srcs if ride else []))
    if ride:
        return (out[0] if n_out == 1 else out[:n_out]), list(out[n_out:])
    return out[0] if n_out == 1 else out


def _mm_pieces_nt(pieces, b, add, *, tm, tn, name, ride):
    M, N = pieces[0].shape[0], b.shape[0]
    n_p, n_r = len(pieces), (ride.n if ride else 0)
    o_spec = pl.BlockSpec((tm, tn), lambda i, j: (i, j))
    grid = (M // tm, N // tn)

    def body(*refs):
        b_ref, add_ref = refs[n_p:n_p + 2]
        r_ins, o_ref = refs[n_p + 2:n_p + 2 + n_r], refs[n_p + 2 + n_r]
        r_lnd, r_sems = refs[n_p + 3 + n_r:n_p + 3 + 2 * n_r], refs[n_p + 3 + 2 * n_r:]
        i, j = pl.program_id(0), pl.program_id(1)
        if ride:
            pl.when((i == 0) & (j == 0))(lambda: ride.start(r_ins, r_lnd, r_sems))
        acc, off = add_ref[...], 0
        for r in refs[:n_p]:
            acc = acc + _dot(r[...], b_ref[:, off:off + r.shape[1]], NT)
            off += r.shape[1]
        o_ref[...] = acc
        if ride:
            pl.when((i == grid[0] - 1) & (j == grid[1] - 1))(lambda: ride.finish(r_ins, r_lnd, r_sems))

    out = pl.pallas_call(
        body, name=name, grid=grid,
        in_specs=[pl.BlockSpec((tm, p.shape[1]), lambda i, j: (i, 0)) for p in pieces]
        + [pl.BlockSpec((tn, b.shape[1]), lambda i, j: (j, 0)), o_spec] + (ride.in_specs if ride else []),
        out_specs=[o_spec] + (ride.out_specs if ride else []),
        out_shape=[jax.ShapeDtypeStruct((M, N), f32)] + (ride.out_shape if ride else []),
        scratch_shapes=ride.scratch if ride else [],
        compiler_params=_params(("arbitrary", "arbitrary")),
    )(*pieces, b, add, *(ride.srcs if ride else []))
    return out[0], list(out[1:])


def _rms_fwd(x, g, *, name, out_dtype, residual=None, tm=512):
    S, C = x.shape
    tm = min(tm, S)
    has_res = residual is not None

    def body(x_ref, g_ref, *rest):
        xv = x_ref[...]
        r = lax.rsqrt(jnp.mean(xv * xv, axis=1, keepdims=True) + EPS)
        y = xv * r * g_ref[...]
        if has_res:
            y = y + rest[0][...]
        rest[-1][...] = y.astype(out_dtype)

    row = pl.BlockSpec((tm, C), lambda i: (i, 0))
    vec = pl.BlockSpec((1, C), lambda i: (0, 0))
    args = (x, g) + ((residual,) if has_res else ())
    return pl.pallas_call(
        body, name=name, grid=(S // tm,),
        in_specs=[row, vec] + ([row] if has_res else []),
        out_specs=row, out_shape=jax.ShapeDtypeStruct((S, C), out_dtype),
        compiler_params=_params(("parallel",)),
    )(*args)


def _rms_bwd(x, dy, g, *, name, out_dtype, add=None, tm=512):
    S, C = x.shape
    tm = min(tm, S)
    has_add = add is not None

    def body(x_ref, dy_ref, g_ref, *rest):
        dx_ref, dg_ref = rest[-2], rest[-1]
        xv = x_ref[...]
        dyv = dy_ref[...].astype(f32)
        r = lax.rsqrt(jnp.mean(xv * xv, axis=1, keepdims=True) + EPS)
        xh = xv * r
        dxh = dyv * g_ref[...]
        dx = r * (dxh - xh * jnp.mean(dxh * xh, axis=1, keepdims=True))
        if has_add:
            dx = dx + rest[0][...]
        dx_ref[...] = dx.astype(out_dtype)

        @pl.when(pl.program_id(0) == 0)
        def _():
            dg_ref[...] = jnp.zeros_like(dg_ref)

        dg_ref[...] += jnp.sum(dyv * xh, axis=0, keepdims=True)

    row = pl.BlockSpec((tm, C), lambda i: (i, 0))
    vec = pl.BlockSpec((1, C), lambda i: (0, 0))
    args = (x, dy, g) + ((add,) if has_add else ())
    return pl.pallas_call(
        body, name=name, grid=(S // tm,),
        in_specs=[row, row, vec] + ([row] if has_add else []),
        out_specs=[row, vec],
        out_shape=[jax.ShapeDtypeStruct((S, C), out_dtype), jax.ShapeDtypeStruct((1, C), f32)],
        compiler_params=_params(("arbitrary",)),
    )(*args)


SB_T = 128
SB_SPENT = -120.0
SB_QB = 4
SB_TAIL = 3
SB_GROUPS = (4, 2, 1)
SB_GROUPS_BWD = (4, 2, 1)


def _sb_masks():
    lane = lax.broadcasted_iota(jnp.int32, (1, 128), 1)
    m_a = (lane < SB_HD).astype(f32)
    return m_a, 1.0 - m_a


def _chunks(a, n):
    return [a[:, u * SB_T:(u + 1) * SB_T] for u in range(n)]


def _cat(parts, axis):
    return parts[0] if len(parts) == 1 else jnp.concatenate(parts, axis=axis)


def _mask_last(a, n, mask):
    if mask is None:
        return a
    parts = _chunks(a, n)
    return _cat(parts[:-1] + [jnp.where(mask, parts[-1], 0.0)], 1)


def _sb_logits(z, n, mask):
    l1p = jnp.log(1.0 + jnp.exp(-jnp.abs(z)))
    lb = jnp.minimum(z, 0.0) - l1p
    return lb, _mask_last(lb - z, n, mask)


def _by_count(i, most, fn):
    return lax.switch(jnp.minimum(i, most - 1), [functools.partial(fn, n) for n in range(1, most + 1)])


def _chunk_matmul(parts_list, u_mat):
    out = _dot(_cat(parts_list, 0), u_mat)
    return [out[u * SB_T:(u + 1) * SB_T] for u in range(len(parts_list))]


def _chunk_cumsum(lk, n, u_mat):
    hi = lk.astype(bf16)
    lo = (lk - hi.astype(f32)).astype(bf16)
    out = _chunk_matmul(_chunks(hi, n) + _chunks(lo, n), u_mat)
    return [out[u] + out[n + u] for u in range(n)]


def _sb_fwd(proj, S, ride=None):
    nq = S // SB_T
    n_pairs = D // 128
    scale = SB_HD ** -0.5
    n_r = ride.n if ride else 0

    def body(q_ref, k_ref, v_ref, *rest):
        o_ref, t_ref = rest[n_r:n_r + 2]
        step_i = pl.program_id(1)
        if ride:
            pl.when((pl.program_id(0) == 0) & (step_i == 0))(
                lambda: ride.start(rest[:n_r], rest[n_r + 2:2 * n_r + 2], rest[2 * n_r + 2:]))
        m_a, m_b = _sb_masks()
        r_i = lax.broadcasted_iota(jnp.int32, (SB_T, SB_T), 0)
        c_i = lax.broadcasted_iota(jnp.int32, (SB_T, SB_T), 1)
        u_mat = (r_i > c_i).astype(bf16)
        causal = c_i < r_i
        q_all = q_ref[...] * scale
        q_hs = [((q * m_a).astype(bf16), (q * m_b).astype(bf16))
                for q in (q_all[b * SB_T:(b + 1) * SB_T] for b in range(SB_QB))]

        def group(q_h, j_lo, n, carry, mask):
            acc, c_a, c_b = carry
            rows = pl.ds(pl.multiple_of(j_lo * SB_T, SB_T), n * SB_T)
            k = k_ref[rows, :].astype(bf16)
            v = v_ref[rows, :]
            zs = [_dot(q_b, k, NT) for q_b in q_h]
            lbk = [_sb_logits(z, n, mask) for z in zs]
            parts = [_chunk_cumsum(lk, n, u_mat) for _, lk in lbk]
            ws, cs = [], []
            for (lb, lk), part, c in zip(lbk, parts, (c_a, c_b)):
                lb_c, lk_c = _chunks(lb, n), _chunks(lk, n)
                w_c = [None] * n
                for u in reversed(range(n)):
                    w_c[u] = jnp.exp(lb_c[u] + c + part[u])
                    c = c + jnp.sum(lk_c[u], axis=1, keepdims=True)
                ws.append(_mask_last(_cat(w_c, 1), n, mask).astype(bf16))
                cs.append(c)
            for w, m in zip(ws, (m_a, m_b)):
                acc = acc + _dot(w, (v * m).astype(bf16))
            return acc, cs[0], cs[1]

        zero_c = jnp.zeros((SB_T, 1), f32)
        init = (jnp.zeros((SB_T, 128), f32), zero_c, zero_c)
        blocks = [(step_i * SB_QB + b, q_hs[b]) for b in range(SB_QB)]

        def whole_tails():
            return tuple(group(q_h, i - SB_TAIL + 1, SB_TAIL, init, causal) for i, q_h in blocks)

        def short_tails():
            return tuple(_by_count(i, SB_TAIL, functools.partial(
                lambda n, i, q_h: group(q_h, i - n + 1, n, init, causal), i=i, q_h=q_h)) for i, q_h in blocks)

        carries = lax.cond(step_i * SB_QB >= SB_TAIL - 1, whole_tails, short_tails)

        def spent(cr):
            return (jnp.max(jnp.maximum(cr[1], cr[2])) < SB_SPENT).astype(jnp.int32)

        lane = lax.broadcasted_iota(jnp.int32, (1, 128), 1)
        for b, ((i, q_h), carry) in enumerate(zip(blocks, carries)):
            state = (i - jnp.minimum(i, SB_TAIL - 1), spent(carry), carry)
            for n in SB_GROUPS:
                def step(st, n=n, q_h=q_h):
                    left, _, cr = st
                    cr = group(q_h, left - n, n, cr, None)
                    return left - n, spent(cr), cr

                state = lax.while_loop(lambda st, n=n: (st[0] >= n) & (st[1] == 0), step, state)
            left, _, carry = state
            rows = slice(b * SB_T, (b + 1) * SB_T)
            o_ref[rows, :] = carry[0]
            t_ref[rows, :] = (jnp.where(lane == 0, carry[1], 0.0) + jnp.where(lane == SB_HD, carry[2], 0.0)
                              + jnp.where(lane == 1, left.astype(f32), 0.0))
        if ride:
            pl.when((pl.program_id(0) == n_pairs - 1) & (step_i == nq // SB_QB - 1))(
                lambda: ride.finish(rest[:n_r], rest[n_r + 2:2 * n_r + 2], rest[2 * n_r + 2:]))

    qs = pl.BlockSpec((SB_QB * SB_T, 128), lambda h, i: (i, h))
    out = pl.pallas_call(
        body, name="sb_fwd", grid=(n_pairs, nq // SB_QB),
        in_specs=[qs,
                  pl.BlockSpec((S, 128), lambda h, i: (0, n_pairs + h)),
                  pl.BlockSpec((S, 128), lambda h, i: (0, 2 * n_pairs + h))] + (ride.in_specs if ride else []),
        out_specs=[qs, qs] + (ride.out_specs if ride else []),
        out_shape=[jax.ShapeDtypeStruct((S, D), f32)] * 2 + (ride.out_shape if ride else []),
        scratch_shapes=ride.scratch if ride else [],
        compiler_params=_params(("arbitrary", "arbitrary")),
    )(proj, proj, proj, *(ride.srcs if ride else []))
    return out[0], out[1], list(out[2:])


def _sb_bwd(proj, tot_lk, do, S, ride=None):
    nq = S // SB_T
    n_pairs = D // 128
    scale = SB_HD ** -0.5
    n_r = ride.n if ride else 0

    def body(q_ref, k_ref, v_ref, t_ref, do_ref, *rest):
        dq_ref, dk_ref, dv_ref = rest[n_r:n_r + 3]
        dk_acc, dv_acc = rest[2 * n_r + 3:2 * n_r + 5]
        r_ins, r_lnd, r_sems = rest[:n_r], rest[n_r + 3:2 * n_r + 3], rest[2 * n_r + 5:]
        step_i = pl.program_id(1)
        if ride:
            pl.when((pl.program_id(0) == 0) & (step_i == 0))(lambda: ride.start(r_ins, r_lnd, r_sems))
        m_a, m_b = _sb_masks()
        r_i = lax.broadcasted_iota(jnp.int32, (SB_T, SB_T), 0)
        c_i = lax.broadcasted_iota(jnp.int32, (SB_T, SB_T), 1)
        u_inc = (r_i <= c_i).astype(bf16)
        u_exc = (r_i < c_i).astype(bf16)
        causal = c_i < r_i

        @pl.when(step_i == 0)
        def _():
            dk_acc[...] = jnp.zeros_like(dk_acc)
            dv_acc[...] = jnp.zeros_like(dv_acc)

        lane = lax.broadcasted_iota(jnp.int32, (1, 128), 1)
        blocks = []
        for b in range(SB_QB):
            rows_b = slice(b * SB_T, (b + 1) * SB_T)
            i = step_i * SB_QB + b
            q = q_ref[rows_b, :] * scale
            dov = do_ref[rows_b, :]
            tv = t_ref[rows_b, :]
            heads = []
            for m, first in ((m_a, 0), (m_b, SB_HD)):
                tot = jnp.sum(jnp.where(lane == first, tv, 0.0), axis=1, keepdims=True)
                heads.append(((q * m).astype(bf16), (dov * m).astype(bf16), tot, m))
            lowest = jnp.clip(jnp.max(jnp.where(lane == 1, tv, 0.0)).astype(jnp.int32), 0, i)
            blocks.append((i, heads, lowest))

        def group(heads, j_lo, n, carry, mask):
            dq_acc, cp_a, cp_b, ce_a, ce_b = carry
            rows = pl.ds(pl.multiple_of(j_lo * SB_T, SB_T), n * SB_T)
            k_f = k_ref[rows, :]
            k = k_f.astype(bf16)
            v = v_ref[rows, :].astype(bf16)
            zs = [_dot(h[0], k, NT) for h in heads]
            dws = [_dot(h[1], v, NT) for h in heads]
            lbk = [_sb_logits(z, n, mask) for z in zs]
            parts = [_chunk_cumsum(lk, n, u_inc) for _, lk in lbk]
            ws, es, cps = [], [], []
            for (lb, lk), part, dw, h, cp in zip(lbk, parts, dws, heads, (cp_a, cp_b)):
                lb_c, lk_c = _chunks(lb, n), _chunks(lk, n)
                w_c = []
                for u in range(n):
                    w_c.append(jnp.exp(lb_c[u] + (h[2] - cp) - part[u]))
                    cp = cp + jnp.sum(lk_c[u], axis=1, keepdims=True)
                w = _mask_last(_cat(w_c, 1), n, mask)
                ws.append(w)
                es.append(dw * w)
                cps.append(cp)
            e_parts = [_chunk_matmul(_chunks(e.astype(bf16), n), u_exc) for e in es]
            dzs, ces = [], []
            for (lb, _), e, e_part, ce in zip(lbk, es, e_parts, (ce_a, ce_b)):
                e_c = _chunks(e, n)
                big_c = []
                for u in range(n):
                    big_c.append(ce + e_part[u])
                    ce = ce + jnp.sum(e_c[u], axis=1, keepdims=True)
                sig = jnp.exp(lb)
                dz = _mask_last(e * (1.0 - sig) - _cat(big_c, 1) * sig, n, mask)
                dzs.append(dz.astype(bf16))
                ces.append(ce)
            dk_t = jnp.zeros((n * SB_T, 128), f32)
            dv_t = jnp.zeros((n * SB_T, 128), f32)
            for dz_b, w, h in zip(dzs, ws, heads):
                dq_acc = dq_acc + _dot(dz_b, (k_f * h[3]).astype(bf16))
                dk_t = dk_t + _dot(dz_b, h[0], TN)
                dv_t = dv_t + _dot(w.astype(bf16), h[1], TN)
            dk_acc[rows, :] += dk_t
            dv_acc[rows, :] += dv_t
            return dq_acc, cps[0], cps[1], ces[0], ces[1]

        zc = jnp.zeros((SB_T, 1), f32)
        carries = []
        for i, heads, lowest in blocks:
            carry = (jnp.zeros((SB_T, 128), f32), zc, zc, zc, zc)
            done = lowest
            tail_lo = i - jnp.minimum(i, SB_TAIL - 1)
            for n in SB_GROUPS_BWD:
                trips = (tail_lo - done) // n
                carry = lax.fori_loop(
                    0, trips, functools.partial(
                        lambda gi, cr, n, done, heads: group(heads, done + gi * n, n, cr, None),
                        n=n, done=done, heads=heads),
                    carry)
                done = done + trips * n
            carries.append(carry)

        def whole_tails():
            return tuple(group(heads, i - SB_TAIL + 1, SB_TAIL, cr, causal)
                         for (i, heads, _), cr in zip(blocks, carries))

        def short_tails():
            return tuple(_by_count(i, SB_TAIL, functools.partial(
                lambda n, i, heads, cr: group(heads, i - n + 1, n, cr, causal), i=i, heads=heads, cr=cr))
                for (i, heads, _), cr in zip(blocks, carries))

        carries = lax.cond(step_i * SB_QB >= SB_TAIL - 1, whole_tails, short_tails)
        for b, carry in enumerate(carries):
            dq_ref[b * SB_T:(b + 1) * SB_T, :] = (carry[0] * scale).astype(bf16)

        @pl.when(step_i == nq // SB_QB - 1)
        def _():
            dk_ref[...] = dk_acc[...].astype(bf16)
            dv_ref[...] = dv_acc[...].astype(bf16)

        if ride:
            pl.when((pl.program_id(0) == n_pairs - 1) & (step_i == nq // SB_QB - 1))(
                lambda: ride.finish(r_ins, r_lnd, r_sems))

    qs = pl.BlockSpec((SB_QB * SB_T, 128), lambda h, i: (i, h))
    full = pl.BlockSpec((S, 128), lambda h, i: (0, h))
    out = pl.pallas_call(
        body, name="sb_bwd", grid=(n_pairs, nq // SB_QB),
        in_specs=[qs,
                  pl.BlockSpec((S, 128), lambda h, i: (0, n_pairs + h)),
                  pl.BlockSpec((S, 128), lambda h, i: (0, 2 * n_pairs + h)),
                  qs, qs] + (ride.in_specs if ride else []),
        out_specs=[qs, full, full] + (ride.out_specs if ride else []),
        out_shape=[jax.ShapeDtypeStruct((S, D), bf16)] * 3 + (ride.out_shape if ride else []),
        scratch_shapes=[pltpu.VMEM((S, 128), f32), pltpu.VMEM((S, 128), f32)] + (ride.scratch if ride else []),
        compiler_params=_params(("arbitrary", "arbitrary")),
    )(proj, proj, proj, tot_lk, do, *(ride.srcs if ride else []))
    return out[0], out[1], out[2], list(out[3:])


CONV_CB = 256
HALO = 8


def _conv_fwd(proj, conv_w, conv_b, S, ride=None):
    tr = min(512, S)
    n_r = ride.n if ride else 0
    n_c = CONV_DIM // CONV_CB

    def body(x_ref, w_ref, b_ref, *rest):
        xc_ref, xbc_ref = rest[n_r:n_r + 2]
        r_ins, r_lnd, r_sems = rest[:n_r], rest[n_r + 2:2 * n_r + 2], rest[2 * n_r + 2:]
        if ride:
            pl.when(pl.program_id(0) == 0)(lambda: ride.start(r_ins, r_lnd, r_sems))
        w = w_ref[...]
        for t in range(S // tr):
            cur = x_ref[t * tr:(t + 1) * tr, :]
            halo = x_ref[t * tr - HALO:t * tr, :] if t else jnp.zeros((HALO, CONV_CB), f32)
            win = jnp.concatenate([halo, cur], axis=0)
            acc = b_ref[...] + w[CONV_K - 1:CONV_K, :] * cur
            for k in range(CONV_K - 1):
                acc = acc + w[k:k + 1, :] * pltpu.roll(win, CONV_K - 1 - k, 0)[HALO:, :]
            xc_ref[t * tr:(t + 1) * tr, :] = acc
            xbc_ref[t * tr:(t + 1) * tr, :] = acc * _sigmoid(acc)
        if ride:
            pl.when(pl.program_id(0) == n_c - 1)(lambda: ride.finish(r_ins, r_lnd, r_sems))

    col = pl.BlockSpec((S, CONV_CB), lambda c: (0, c))
    out = pl.pallas_call(
        body, name="conv_fwd", grid=(n_c,),
        in_specs=[pl.BlockSpec((S, CONV_CB), lambda c: (0, P_XBC // CONV_CB + c)),
                  pl.BlockSpec((CONV_K, CONV_CB), lambda c: (0, c)),
                  pl.BlockSpec((1, CONV_CB), lambda c: (0, c))] + (ride.in_specs if ride else []),
        out_specs=[col, col] + (ride.out_specs if ride else []),
        out_shape=[jax.ShapeDtypeStruct((S, CONV_DIM), f32)] * 2 + (ride.out_shape if ride else []),
        scratch_shapes=ride.scratch if ride else [],
        compiler_params=_params(("arbitrary",)),
    )(proj, conv_w, conv_b, *(ride.srcs if ride else []))
    return out[0], out[1], list(out[2:])


def _conv_bwd(proj, xc, dxbc, conv_w, S):
    tr = min(512, S)

    def body(x_ref, xc_ref, dy_ref, w_ref, dx_ref, dw_ref, db_ref, dxc_s):
        w = w_ref[...]
        xcv = xc_ref[...]
        sg = _sigmoid(xcv)
        dxc_s[0:S, :] = dy_ref[...] * (sg * (1.0 + xcv * (1.0 - sg)))
        dxc_s[S:S + HALO, :] = jnp.zeros((HALO, CONV_CB), f32)
        dws = [jnp.zeros((1, CONV_CB), f32) for _ in range(CONV_K)]
        db = jnp.zeros((1, CONV_CB), f32)
        for t in range(S // tr):
            cur = x_ref[t * tr:(t + 1) * tr, :]
            halo = x_ref[t * tr - HALO:t * tr, :] if t else jnp.zeros((HALO, CONV_CB), f32)
            win = jnp.concatenate([halo, cur], axis=0)
            dwin = dxc_s[t * tr:(t + 1) * tr + HALO, :]
            dcur = dwin[0:tr, :]
            db = db + jnp.sum(dcur, axis=0, keepdims=True)
            dws[CONV_K - 1] = dws[CONV_K - 1] + jnp.sum(dcur * cur, axis=0, keepdims=True)
            dx = w[CONV_K - 1:CONV_K, :] * dcur
            for k in range(CONV_K - 1):
                sh = CONV_K - 1 - k
                dws[k] = dws[k] + jnp.sum(dcur * pltpu.roll(win, sh, 0)[HALO:, :], axis=0, keepdims=True)
                dx = dx + w[k:k + 1, :] * pltpu.roll(dwin, tr + HALO - sh, 0)[0:tr, :]
            dx_ref[t * tr:(t + 1) * tr, :] = dx.astype(bf16)
        dw_ref[...] = jnp.concatenate(dws + [jnp.zeros((8 - CONV_K, CONV_CB), f32)], axis=0)
        db_ref[...] = db

    col = pl.BlockSpec((S, CONV_CB), lambda c: (0, c))
    return pl.pallas_call(
        body, name="conv_bwd", grid=(CONV_DIM // CONV_CB,),
        in_specs=[pl.BlockSpec((S, CONV_CB), lambda c: (0, P_XBC // CONV_CB + c)), col, col,
                  pl.BlockSpec((CONV_K, CONV_CB), lambda c: (0, c))],
        out_specs=[col, pl.BlockSpec((8, CONV_CB), lambda c: (0, c)), pl.BlockSpec((1, CONV_CB), lambda c: (0, c))],
        out_shape=[jax.ShapeDtypeStruct((S, CONV_DIM), bf16), jax.ShapeDtypeStruct((8, CONV_DIM), f32),
                   jax.ShapeDtypeStruct((1, CONV_DIM), f32)],
        scratch_shapes=[pltpu.VMEM((S + HALO, CONV_CB), f32)],
        compiler_params=_params(("parallel",)),
    )(proj, xc, dxbc, conv_w)


N_PAIR = SSD_HEADS // 2
NEG = -1e30


def _softplus(x):
    return jnp.maximum(x, 0.0) + jnp.log(1.0 + jnp.exp(-jnp.abs(x)))


def _ssd_common(dtr, dtb, alog):
    L = SSD_L
    r_i = lax.broadcasted_iota(jnp.int32, (L, L), 0)
    c_i = lax.broadcasted_iota(jnp.int32, (L, L), 1)
    dt = _softplus(dtr + dtb)
    a = -jnp.exp(alog)
    da = dt * a
    lower = (r_i >= c_i).astype(bf16)
    upper = (r_i <= c_i).astype(bf16)
    parts = _split3(da)
    a_cs = sum(_dot(lower, p) for p in parts)
    a_cs_t = sum(_dot(p, upper, TN) for p in parts)
    return dt, a, a_cs, a_cs_t, r_i >= c_i


def _pair_vec(lane, v, h):
    return jnp.where(lane < SB_HD, v[:, h:h + 1], v[:, h + 1:h + 2])


def _decay_mat(a_cs, a_cs_t, h, tril):
    return jnp.exp(jnp.where(tril, a_cs[:, h:h + 1] - a_cs_t[h:h + 1, :], NEG))


def _ssd_fwd(xbc, proj, pdt, dt_bias_p, a_log_p, d_skip_c, ssd_norm, S, ride=None):
    L = SSD_L
    nc = S // L
    n_r = ride.n if ride else 0

    def body(xbc_ref, dt_ref, z_ref, dtb_ref, alog_ref, dsk_ref, gn_ref, *rest):
        y_ref, yn_ref, hp_ref = rest[n_r:n_r + 3]
        state = rest[2 * n_r + 3]
        r_ins, r_lnd, r_sems = rest[:n_r], rest[n_r + 3:2 * n_r + 3], rest[2 * n_r + 4:]
        c = pl.program_id(0)
        if ride:
            pl.when(c == 0)(lambda: ride.start(r_ins, r_lnd, r_sems))

        @pl.when(c == 0)
        def _():
            state[...] = jnp.zeros_like(state)

        hp_ref[0] = state[...]
        lane = lax.broadcasted_iota(jnp.int32, (1, 128), 1)
        row128 = lax.broadcasted_iota(jnp.int32, (128, 1), 0)
        m_a, m_b = _sb_masks()
        dt, a, a_cs, a_cs_t, tril = _ssd_common(dt_ref[...], dtb_ref[...], alog_ref[...])
        a_last = a_cs[L - 1:L, :]
        for g in range(SSD_GROUPS):
            b_g = xbc_ref[:, SSD_INNER + g * SSD_N:SSD_INNER + (g + 1) * SSD_N].astype(bf16)
            c_g = xbc_ref[:, SSD_INNER + (SSD_GROUPS + g) * SSD_N:SSD_INNER + (SSD_GROUPS + g + 1) * SSD_N].astype(bf16)
            cb = _dot(c_g, b_g, NT)
            for pr in range(4):
                h = 8 * g + 2 * pr
                pi = h // 2
                cols = slice(pi * 128, (pi + 1) * 128)
                xs = xbc_ref[:, cols]
                x = xs * _pair_vec(lane, dt, h)
                acs = _pair_vec(lane, a_cs, h)
                al = _pair_vec(lane, a_last, h)
                w_a = (cb * _decay_mat(a_cs, a_cs_t, h, tril)).astype(bf16)
                w_b = (cb * _decay_mat(a_cs, a_cs_t, h + 1, tril)).astype(bf16)
                yd = _dot(w_a, (x * m_a).astype(bf16)) + _dot(w_b, (x * m_b).astype(bf16))
                hp = state[pi]
                yo = _dot(c_g, hp.astype(bf16), NT) * jnp.exp(acs)
                y_ref[:, cols] = yd + yo + dsk_ref[:, cols] * xs
                dec = jnp.exp(jnp.where(row128 < SB_HD, a_last[:, h:h + 1], a_last[:, h + 1:h + 2]))
                state[pi] = hp * dec + _dot((x * jnp.exp(al - acs)).astype(bf16), b_g, TN)
        zz = z_ref[...]
        y2 = y_ref[...] * (zz * _sigmoid(zz))
        gw = SSD_INNER // SSD_GROUPS
        for g in range(SSD_GROUPS):
            yg = y2[:, g * gw:(g + 1) * gw]
            rg = lax.rsqrt(jnp.mean(yg * yg, axis=1, keepdims=True) + EPS)
            yn_ref[:, g * gw:(g + 1) * gw] = (yg * rg * gn_ref[:, g * gw:(g + 1) * gw]).astype(bf16)
        if ride:
            pl.when(c == nc - 1)(lambda: ride.finish(r_ins, r_lnd, r_sems))

    vec128 = pl.BlockSpec((1, 128), lambda c: (0, 0))
    vecin = pl.BlockSpec((1, SSD_INNER), lambda c: (0, 0))
    rows = pl.BlockSpec((L, SSD_INNER), lambda c: (c, 0))
    out = pl.pallas_call(
        body, name="ssd_fwd", grid=(nc,),
        in_specs=[pl.BlockSpec((L, CONV_DIM), lambda c: (c, 0)),
                  pl.BlockSpec((L, 128), lambda c: (c, 0)),
                  pl.BlockSpec((L, SSD_INNER), lambda c: (c, P_Z // SSD_INNER)),
                  vec128, vec128, vecin, vecin] + (ride.in_specs if ride else []),
        out_specs=[rows, rows, pl.BlockSpec((1, N_PAIR, 128, SSD_N), lambda c: (c, 0, 0, 0))]
        + (ride.out_specs if ride else []),
        out_shape=[jax.ShapeDtypeStruct((S, SSD_INNER), f32), jax.ShapeDtypeStruct((S, SSD_INNER), bf16),
                   jax.ShapeDtypeStruct((nc, N_PAIR, 128, SSD_N), f32)] + (ride.out_shape if ride else []),
        scratch_shapes=[pltpu.VMEM((N_PAIR, 128, SSD_N), f32)] + (ride.scratch if ride else []),
        compiler_params=_params(("arbitrary",)),
    )(xbc, pdt, proj, dt_bias_p, a_log_p, d_skip_c, ssd_norm, *(ride.srcs if ride else []))
    return out[0], out[1], out[2], list(out[3:])


def _sum_all(v):
    return jnp.sum(jnp.sum(v, axis=1, keepdims=True), axis=0, keepdims=True)


def _ssd_bwd(dyn, y, xbc, proj, pdt, hprev, dt_bias_p, a_log_p, d_skip_c, ssd_norm, S, ride=None):
    L = SSD_L
    nc = S // L
    n_r = ride.n if ride else 0

    col = lax.broadcasted_iota(jnp.int32, (2 * SSD_INNER, 128), 0)
    head = lax.broadcasted_iota(jnp.int32, (2 * SSD_INNER, 128), 1)
    sel_pair = (col[:SSD_INNER] // SB_HD == head[:SSD_INNER]).astype(bf16)
    sel_head = (col // 128 == head).astype(bf16)

    def body(*refs):
        (dyn_ref, y_ref, xbc_ref, dt_ref, z_ref, hp_ref, dtb_ref, alog_ref, dsk_ref, gn_ref,
         selp_ref, selh_ref) = refs[:12]
        dz_ref, dxbc_ref, ddt_ref, dgn_ref, dsk_out, dalog_ref, ddtb_ref = refs[12 + n_r:19 + n_r]
        dstate, dy_s, st_a, st_q, st_d, st_x, dat = refs[19 + 2 * n_r:26 + 2 * n_r]
        r_ins, r_lnd, r_sems = refs[12:12 + n_r], refs[19 + n_r:19 + 2 * n_r], refs[26 + 2 * n_r:]
        c = pl.program_id(0)
        if ride:
            pl.when(c == 0)(lambda: ride.start(r_ins, r_lnd, r_sems))

        @pl.when(c == 0)
        def _():
            dat[...] = jnp.zeros_like(dat)
            dstate[...] = jnp.zeros_like(dstate)
            dgn_ref[...] = jnp.zeros_like(dgn_ref)
            dsk_out[...] = jnp.zeros_like(dsk_out)
            dalog_ref[...] = jnp.zeros_like(dalog_ref)
            ddtb_ref[...] = jnp.zeros_like(ddtb_ref)

        lane = lax.broadcasted_iota(jnp.int32, (1, 128), 1)
        row128 = lax.broadcasted_iota(jnp.int32, (128, 1), 0)
        rowl = lax.broadcasted_iota(jnp.int32, (L, 1), 0)
        m_a, m_b = _sb_masks()
        dtr = dt_ref[...]
        dt, a, a_cs, a_cs_t, tril = _ssd_common(dtr, dtb_ref[...], alog_ref[...])
        a_last = a_cs[L - 1:L, :]

        zz = z_ref[...]
        sg = _sigmoid(zz)
        silu = zz * sg
        yv = y_ref[...]
        y2 = yv * silu
        gw = SSD_INNER // SSD_GROUPS
        for g in range(SSD_GROUPS):
            sl = slice(g * gw, (g + 1) * gw)
            yg = y2[:, sl]
            rg = lax.rsqrt(jnp.mean(yg * yg, axis=1, keepdims=True) + EPS)
            yh = yg * rg
            dyn_g = dyn_ref[:, sl]
            dgn_ref[:, sl] += jnp.sum(dyn_g * yh, axis=0, keepdims=True)
            dyh = dyn_g * gn_ref[:, sl]
            dy2 = rg * (dyh - yh * jnp.mean(dyh * yh, axis=1, keepdims=True))
            dy_s[:, sl] = dy2 * silu[:, sl]
            dz_ref[:, sl] = (dy2 * yv[:, sl] * (sg[:, sl] * (1.0 + zz[:, sl] * (1.0 - sg[:, sl])))).astype(bf16)

        last_row = jnp.zeros((1, 128), f32)
        dsk_acc = jnp.zeros((1, 128), f32)
        for g in range(SSD_GROUPS):
            bsl = slice(SSD_INNER + g * SSD_N, SSD_INNER + (g + 1) * SSD_N)
            csl = slice(SSD_INNER + (SSD_GROUPS + g) * SSD_N, SSD_INNER + (SSD_GROUPS + g + 1) * SSD_N)
            b_g = xbc_ref[:, bsl].astype(bf16)
            c_g = xbc_ref[:, csl].astype(bf16)
            cb = _dot(c_g, b_g, NT)
            dcb = jnp.zeros((L, L), f32)
            dc_g = jnp.zeros((L, SSD_N), f32)
            db_g = jnp.zeros((L, SSD_N), f32)
            for pr in range(4):
                h = 8 * g + 2 * pr
                pi = h // 2
                cols = slice(pi * 128, (pi + 1) * 128)
                xs = xbc_ref[:, cols]
                dt_p = _pair_vec(lane, dt, h)
                x = xs * dt_p
                acs = _pair_vec(lane, a_cs, h)
                al = _pair_vec(lane, a_last, h)
                e_a = jnp.exp(acs)
                dte = jnp.exp(al - acs)
                m_mat_a = _decay_mat(a_cs, a_cs_t, h, tril)
                m_mat_b = _decay_mat(a_cs, a_cs_t, h + 1, tril)
                dyp = dy_s[:, cols]
                dsk = dsk_ref[:, cols]
                d_hn = dstate[pi]
                hp = hp_ref[0, pi]
                dy_a = (dyp * m_a).astype(bf16)
                dy_b = (dyp * m_b).astype(bf16)
                x_b = x.astype(bf16)
                gm_a = _dot(dy_a, x_b, NT) * m_mat_a
                gm_b = _dot(dy_b, x_b, NT) * m_mat_b
                dcb = dcb + gm_a + gm_b
                dx_d = _dot((cb * m_mat_a).astype(bf16), dy_a, TN) + _dot((cb * m_mat_b).astype(bf16), dy_b, TN)
                dx_s = _dot(b_g, d_hn.astype(bf16), NT) * dte
                dx = dx_d + dx_s
                dxbc_ref[:, cols] = dx * dt_p + dsk * dyp
                xdxs = x * dx_s
                st_x[:, cols] = xdxs
                st_a[:, cols] = dyp * (_dot(c_g, hp.astype(bf16), NT) * e_a) - xdxs
                st_d[:, cols] = dx * xs
                hh = d_hn * hp
                dsk_row = jnp.sum(dyp * xs, axis=0, keepdims=True)
                dec = jnp.exp(jnp.where(row128 < SB_HD, a_last[:, h:h + 1], a_last[:, h + 1:h + 2]))
                for hd, m, gm in ((h, m_a, gm_a), (h + 1, m_b, gm_b)):
                    half = slice(0, SB_HD) if hd == h else slice(SB_HD, 128)
                    qm = gm * cb
                    st_q[:, hd * 128:(hd + 1) * 128] = qm
                    dat[hd:hd + 1, :] = jnp.sum(qm, axis=0, keepdims=True)
                    hh_sum = jnp.sum(jnp.sum(hh[half, :], axis=0, keepdims=True), axis=1, keepdims=True)
                    last_row = jnp.where(lane == hd, jnp.exp(a_last[:, hd:hd + 1]) * hh_sum, last_row)
                    dsk_acc = jnp.where(lane == hd, jnp.sum(dsk_row * m, axis=1, keepdims=True), dsk_acc)
                dye = (dyp * e_a).astype(bf16)
                dc_g = dc_g + _dot(dye, hp.astype(bf16))
                db_g = db_g + _dot((x * dte).astype(bf16), d_hn.astype(bf16))
                dstate[pi] = dec * d_hn + _dot(dye, c_g, TN)
            dcb_b = dcb.astype(bf16)
            dxbc_ref[:, csl] = dc_g + _dot(dcb_b, b_g)
            dxbc_ref[:, bsl] = db_g + _dot(dcb_b, c_g, TN)

        r_i = lax.broadcasted_iota(jnp.int32, (L, L), 0)
        c_i = lax.broadcasted_iota(jnp.int32, (L, L), 1)
        rev = (r_i <= c_i).astype(bf16)

        def head_sums(st, sel, split=_split2):
            return sum(_dot(p, sel[...]) for p in split(st[...]))

        last_row = last_row + jnp.sum(head_sums(st_x, selp_ref), axis=0, keepdims=True)
        d_acs = (head_sums(st_a, selp_ref) + head_sums(st_q, selh_ref, _split3)
                 + jnp.where(rowl == L - 1, last_row, 0.0))
        ddt_x = head_sums(st_d, selp_ref)
        dda = sum(_dot(rev, p) for p in _split3(d_acs)) - sum(_dot(rev, p, NT) for p in _split3(dat[...]))
        ddt = ddt_x + dda * a
        dalog_ref[...] += jnp.sum(dda * dt, axis=0, keepdims=True) * a
        ddtr = jnp.where(lane < SSD_HEADS, ddt * _sigmoid(dtr + dtb_ref[...]), 0.0)
        ddt_ref[...] = ddtr.astype(bf16)
        ddtb_ref[...] += jnp.sum(ddtr, axis=0, keepdims=True)
        dsk_out[...] += dsk_acc
        if ride:
            pl.when(c == nc - 1)(lambda: ride.finish(r_ins, r_lnd, r_sems))

    rv = lambda c: nc - 1 - c
    vec128 = pl.BlockSpec((1, 128), lambda c: (0, 0))
    vecin = pl.BlockSpec((1, SSD_INNER), lambda c: (0, 0))
    rows = pl.BlockSpec((L, SSD_INNER), lambda c: (rv(c), 0))
    return pl.pallas_call(
        body, name="ssd_bwd", grid=(nc,),
        in_specs=[rows, rows,
                  pl.BlockSpec((L, CONV_DIM), lambda c: (rv(c), 0)),
                  pl.BlockSpec((L, 128), lambda c: (rv(c), 0)),
                  pl.BlockSpec((L, SSD_INNER), lambda c: (rv(c), P_Z // SSD_INNER)),
                  pl.BlockSpec((1, N_PAIR, 128, SSD_N), lambda c: (rv(c), 0, 0, 0)),
                  vec128, vec128, vecin, vecin,
                  pl.BlockSpec((SSD_INNER, 128), lambda c: (0, 0)),
                  pl.BlockSpec((2 * SSD_INNER, 128), lambda c: (0, 0))] + (ride.in_specs if ride else []),
        out_specs=[rows, pl.BlockSpec((L, CONV_DIM), lambda c: (rv(c), 0)),
                   pl.BlockSpec((L, 128), lambda c: (rv(c), 0)), vecin, vec128, vec128, vec128]
        + (ride.out_specs if ride else []),
        out_shape=[jax.ShapeDtypeStruct((S, SSD_INNER), bf16), jax.ShapeDtypeStruct((S, CONV_DIM), f32),
                   jax.ShapeDtypeStruct((S, 128), bf16), jax.ShapeDtypeStruct((1, SSD_INNER), f32),
                   jax.ShapeDtypeStruct((1, 128), f32), jax.ShapeDtypeStruct((1, 128), f32),
                   jax.ShapeDtypeStruct((1, 128), f32)] + (ride.out_shape if ride else []),
        scratch_shapes=[pltpu.VMEM((N_PAIR, 128, SSD_N), f32), pltpu.VMEM((L, SSD_INNER), f32),
                        pltpu.VMEM((L, SSD_INNER), f32), pltpu.VMEM((L, 2 * SSD_INNER), f32),
                        pltpu.VMEM((L, SSD_INNER), f32), pltpu.VMEM((L, SSD_INNER), f32),
                        pltpu.VMEM((128, L), f32)]
        + (ride.scratch if ride else []),
        compiler_params=_params(("arbitrary",)),
    )(dyn, y, xbc, pdt, proj, hprev, dt_bias_p, a_log_p, d_skip_c, ssd_norm, sel_pair, sel_head,
      *(ride.srcs if ride else []))


MEM_W = MEM_HEADS * MEM_HD


def _mem_probs(q, k):
    s = _dot(q, k, NT) * (MEM_HD ** -0.5)
    s = s - jnp.max(s, axis=1, keepdims=True)
    p = jnp.exp(s)
    return p / jnp.sum(p, axis=1, keepdims=True)


def _mem_fwd(proj, kv, S, tm=512):
    tm = min(tm, S)
    M = kv.shape[0]

    def body(q_ref, kv_ref, o_ref):
        for h in range(MEM_HEADS):
            sl = slice(h * MEM_HD, (h + 1) * MEM_HD)
            vsl = slice(MEM_W + h * MEM_HD, MEM_W + (h + 1) * MEM_HD)
            p = _mem_probs(q_ref[:, sl].astype(bf16), kv_ref[:, sl].astype(bf16))
            o_ref[:, sl] = _dot(p.astype(bf16), kv_ref[:, vsl].astype(bf16)).astype(bf16)

    return pl.pallas_call(
        body, name="mem_fwd", grid=(S // tm,),
        in_specs=[pl.BlockSpec((tm, MEM_W), lambda i: (i, P_MEMQ // MEM_W)),
                  pl.BlockSpec((M, 2 * MEM_W), lambda i: (0, 0))],
        out_specs=pl.BlockSpec((tm, MEM_W), lambda i: (i, 0)),
        out_shape=jax.ShapeDtypeStruct((S, MEM_W), bf16),
        compiler_params=_params(("parallel",)),
    )(proj, kv)


def _mem_bwd(proj, kv, dy, S, tm=512):
    tm = min(tm, S)
    M = kv.shape[0]
    scale = MEM_HD ** -0.5

    def body(q_ref, kv_ref, dy_ref, dq_ref, dkv_ref):
        @pl.when(pl.program_id(0) == 0)
        def _():
            dkv_ref[...] = jnp.zeros_like(dkv_ref)

        for h in range(MEM_HEADS):
            sl = slice(h * MEM_HD, (h + 1) * MEM_HD)
            vsl = slice(MEM_W + h * MEM_HD, MEM_W + (h + 1) * MEM_HD)
            q = q_ref[:, sl].astype(bf16)
            k = kv_ref[:, sl].astype(bf16)
            v = kv_ref[:, vsl].astype(bf16)
            dyh = dy_ref[:, sl].astype(bf16)
            p = _mem_probs(q, k)
            dp = _dot(dyh, v, NT)
            ds = (p * (dp - jnp.sum(dp * p, axis=1, keepdims=True)) * scale).astype(bf16)
            dq_ref[:, sl] = _dot(ds, k).astype(bf16)
            dkv_ref[:, sl] += _dot(ds, q, TN)
            dkv_ref[:, vsl] += _dot(p.astype(bf16), dyh, TN)

    return pl.pallas_call(
        body, name="mem_bwd", grid=(S // tm,),
        in_specs=[pl.BlockSpec((tm, MEM_W), lambda i: (i, P_MEMQ // MEM_W)),
                  pl.BlockSpec((M, 2 * MEM_W), lambda i: (0, 0)),
                  pl.BlockSpec((tm, MEM_W), lambda i: (i, 0))],
        out_specs=[pl.BlockSpec((tm, MEM_W), lambda i: (i, 0)), pl.BlockSpec((M, 2 * MEM_W), lambda i: (0, 0))],
        out_shape=[jax.ShapeDtypeStruct((S, MEM_W), bf16), jax.ShapeDtypeStruct((M, 2 * MEM_W), f32)],
        compiler_params=_params(("arbitrary",)),
    )(proj, kv, dy)


def _merge_fwd(proj, t0, t1, t2, S, tm=512):
    tm = min(tm, S)

    def body(g_ref, t0_ref, t1_ref, t2_ref, o_ref):
        acc = jnp.zeros((tm, D), f32)
        for b, t_ref in enumerate((t0_ref, t1_ref, t2_ref)):
            acc = acc + _sigmoid(g_ref[:, b * D:(b + 1) * D]) * t_ref[...]
        o_ref[...] = acc.astype(bf16)

    row = pl.BlockSpec((tm, D), lambda i: (i, 0))
    return pl.pallas_call(
        body, name="merge_fwd", grid=(S // tm,),
        in_specs=[pl.BlockSpec((tm, 3 * D), lambda i: (i, P_GATE // (3 * D))), row, row, row],
        out_specs=row, out_shape=jax.ShapeDtypeStruct((S, D), bf16),
        compiler_params=_params(("parallel",)),
    )(proj, t0, t1, t2)


def _merge_bwd(proj, t0, t1, t2, dm, S, tm=512):
    tm = min(tm, S)

    def body(g_ref, t0_ref, t1_ref, t2_ref, dm_ref, d0_ref, d1_ref, d2_ref, dg_ref):
        dmv = dm_ref[...]
        for b, (t_ref, d_ref) in enumerate(((t0_ref, d0_ref), (t1_ref, d1_ref), (t2_ref, d2_ref))):
            sg = _sigmoid(g_ref[:, b * D:(b + 1) * D])
            d_ref[...] = (dmv * sg).astype(bf16)
            dg_ref[:, b * D:(b + 1) * D] = (dmv * t_ref[...] * sg * (1.0 - sg)).astype(bf16)

    row = pl.BlockSpec((tm, D), lambda i: (i, 0))
    return pl.pallas_call(
        body, name="merge_bwd", grid=(S // tm,),
        in_specs=[pl.BlockSpec((tm, 3 * D), lambda i: (i, P_GATE // (3 * D))), row, row, row, row],
        out_specs=[row, row, row, pl.BlockSpec((tm, 3 * D), lambda i: (i, 0))],
        out_shape=[jax.ShapeDtypeStruct((S, D), bf16)] * 3 + [jax.ShapeDtypeStruct((S, 3 * D), bf16)],
        compiler_params=_params(("parallel",)),
    )(proj, t0, t1, t2, dm)


def _loss_head(ff, g, h1, target, S, tm=512):
    tm = min(tm, S)

    def body(ff_ref, g_ref, h1_ref, t_ref, dh_ref, loss_ref):
        xv = ff_ref[...]
        r = lax.rsqrt(jnp.mean(xv * xv, axis=1, keepdims=True) + EPS)
        err = h1_ref[...] + xv * r * g_ref[...] - t_ref[...]
        dh_ref[...] = err * (1.0 / D)

        @pl.when(pl.program_id(0) == 0)
        def _():
            loss_ref[...] = jnp.zeros_like(loss_ref)

        loss_ref[...] += 0.5 * _sum_all(jnp.mean(err * err, axis=1, keepdims=True)) * jnp.ones((1, 128), f32)

    row = pl.BlockSpec((tm, D), lambda i: (i, 0))
    return pl.pallas_call(
        body, name="loss_head", grid=(S // tm,),
        in_specs=[row, pl.BlockSpec((1, D), lambda i: (0, 0)), row, row],
        out_specs=[row, pl.BlockSpec((1, 128), lambda i: (0, 0))],
        out_shape=[jax.ShapeDtypeStruct((S, D), f32), jax.ShapeDtypeStruct((1, 128), f32)],
        compiler_params=_params(("arbitrary",)),
    )(ff, g, h1, target)


def _local_step(x, mem, target, wts, late_rides, late_weights, small, rest_rides, w_in_ride):
    S = x.shape[0]
    M = mem.shape[0]
    pad = lambda v: jnp.pad(v, ((0, 0), (0, 128 - SSD_HEADS)))
    dtb_p, alog_p = pad(small["dt_bias"]), pad(small["a_log"])
    dsk_c = jnp.repeat(small["d_skip"], SB_HD, axis=1)

    u = _rms_fwd(x, small["norm_mix_pre"], name="norm_pre", out_dtype=bf16)
    rides = late_rides or (None, None, None, None)
    if late_rides:
        proj, lands_a = _mm(u, wts["w_main"], "nn", tm=1024, tn=1024, name="in_proj", ride=rides[0])
    else:
        proj, lands_a = _mm(u, wts["w_main"], "nn", tm=1024, tn=1024, name="in_proj"), []
    pdt = _mm(u, wts["w_dt"], "nn", tm=1024, tn=128, name="in_proj_dt")
    y_sb, tot_lk, lands_b = _sb_fwd(proj, S, rides[1])
    wts = dict(wts, **late_weights(0, lands_a))
    small = dict(small, conv_w=wts.pop("conv_w"))
    xc, xbc, lands_d = _conv_fwd(proj, small["conv_w"], small["conv_b"], S, rides[3])
    y_ssd, yn, hprev, lands_c = _ssd_fwd(xbc, proj, pdt, dtb_p, alog_p, dsk_c, small["ssd_norm"], S, rides[2])
    wts = dict(wts, **late_weights(1, lands_b), **late_weights(2, lands_c), **late_weights(3, lands_d))
    mn = _rms_fwd(mem, small["norm_mem"], name="norm_mem", out_dtype=bf16, tm=min(512, M))
    kv = _mm(mn, wts["w_mem_kv"], "nn", tm=M, tn=1024, name="mem_kv")
    y_mem = _mem_fwd(proj, kv, S)
    t0 = _mm(y_sb, wts["w_sb_out"], "nn", tm=1024, tn=1024, name="sb_out")
    t1 = _mm(yn, wts["w_ssd_out"], "nn", tm=1024, tn=1024, name="ssd_out")
    t2 = _mm(y_mem, wts["w_mem_out"], "nn", tm=1024, tn=1024, name="mem_out")
    merged = _merge_fwd(proj, t0, t1, t2, S)
    mix = _mm(merged, wts["w_o"], "nn", tm=1024, tn=1024, name="w_o")
    h1 = _rms_fwd(mix, small["norm_mix_post"], name="norm_mix_post", out_dtype=f32, residual=x)
    u2 = _rms_fwd(h1, small["norm_mlp_pre"], name="norm_mlp_pre", out_dtype=bf16)
    a_up, hrelu = _mm(u2, wts["w_up"], "nn", tm=1024, tn=1024, name="mlp_up", out_dtypes=(f32, bf16),
                      epi=lambda acc: (acc, jnp.square(jnp.maximum(acc, 0.0))))
    ff = _mm(hrelu, wts["w_down"], "nn", tm=1024, tn=1024, name="mlp_down")
    dh2, loss = _loss_head(ff, small["norm_mlp_post"], h1, target, S)

    g = {}
    dff, g["norm_mlp_post"] = _rms_bwd(ff, dh2, small["norm_mlp_post"], name="norm_mlp_post_bwd", out_dtype=bf16)
    da = _mm(dff, wts["w_down"], "nt", tm=1024, tn=1024, name="mlp_down_dx", out_dtypes=(bf16,),
             epi=lambda acc, a: (acc * (2.0 * jnp.maximum(a, 0.0)),), extras=(a_up,))
    g["w_down"] = _mm(hrelu, dff, "tn", tm=1024, tn=1024, name="mlp_down_dw")
    du2 = _mm(da, wts["w_up"], "nt", tm=1024, tn=1024, name="mlp_up_dx")
    g["w_up"] = _mm(u2, da, "tn", tm=1024, tn=1024, name="mlp_up_dw")
    dh1, g["norm_mlp_pre"] = _rms_bwd(h1, du2, small["norm_mlp_pre"], name="norm_mlp_pre_bwd", out_dtype=f32, add=dh2)
    dmix, g["norm_mix_post"] = _rms_bwd(mix, dh1, small["norm_mix_post"], name="norm_mix_post_bwd", out_dtype=bf16)
    dmerged = _mm(dmix, wts["w_o"], "nt", tm=1024, tn=1024, name="w_o_dx")
    g["w_o"] = _mm(merged, dmix, "tn", tm=1024, tn=1024, name="w_o_dw")
    dt0, dt1, dt2, dgl = _merge_bwd(proj, t0, t1, t2, dmerged, S)
    dy_sb = _mm(dt0, wts["w_sb_out"], "nt", tm=1024, tn=1024, name="sb_out_dx")
    g["w_sb_out"] = _mm(y_sb, dt0, "tn", tm=1024, tn=1024, name="sb_out_dw")
    dy_ssd = _mm(dt1, wts["w_ssd_out"], "nt", tm=1024, tn=1024, name="ssd_out_dx")
    g["w_ssd_out"] = _mm(yn, dt1, "tn", tm=1024, tn=1024, name="ssd_out_dw")
    dy_mem = _mm(dt2, wts["w_mem_out"], "nt", tm=1024, tn=1024, name="mem_out_dx")
    g["w_mem_out"] = _mm(y_mem, dt2, "tn", tm=1024, tn=1024, name="mem_out_dw")
    dmemq, dkv = _mem_bwd(proj, kv, dy_mem, S)
    g["w_mem_kv"] = _mm(mn, dkv, "tn", tm=1024, tn=1024, name="mem_kv_dw")
    dmn = _mm(dkv, wts["w_mem_kv"], "nt", tm=M, tn=1024, name="mem_kv_dx")
    _, g["norm_mem"] = _rms_bwd(mem, dmn, small["norm_mem"], name="norm_mem_bwd", out_dtype=bf16, tm=min(512, M))
    rides = rest_rides(g) if rest_rides else (None, None)
    dz, dxbc, ddt, g["ssd_norm"], dsk, dalog, ddtb, *lands_a = _ssd_bwd(
        dy_ssd, y_ssd, xbc, proj, pdt, hprev, dtb_p, alog_p, dsk_c, small["ssd_norm"], S, rides[0])
    g["d_skip"], g["a_log"], g["dt_bias"] = dsk[:, :SSD_HEADS], dalog[:, :SSD_HEADS], ddtb[:, :SSD_HEADS]
    dxbc_raw, dcw, g["conv_b"] = _conv_bwd(proj, xc, dxbc, small["conv_w"], S)
    g["conv_w"] = dcw[:CONV_K]
    dq, dk, dv, lands_b = _sb_bwd(proj, tot_lk, dy_sb, S, rides[1])
    g["rest_lands"] = lands_b + lands_a
    dproj = (dq, dk, dv, dxbc_raw, dgl, dmemq, dz)
    u_t = u.T
    g["w_main"] = [_mm(u_t, p, "nn", tm=512, tn=1024, name="in_proj_dw_%d" % i) for i, p in enumerate(dproj)]
    g["w_dt"] = _mm(u_t, ddt, "nn", tm=512, tn=128, name="in_proj_dt_dw")
    du_dt = _mm(ddt, wts["w_dt"], "nt", tm=1024, tn=1024, name="in_proj_dt_dx")
    du, g["w_in_lands"] = _mm_pieces_nt(dproj, wts["w_main"], du_dt, tm=512, tn=256, name="in_proj_dx",
                                        ride=w_in_ride(g) if w_in_ride else None)
    grad_x, g["norm_mix_pre"] = _rms_bwd(x, du, small["norm_mix_pre"], name="norm_pre_bwd", out_dtype=f32, add=dh1)
    return loss, grad_x, g


def _to_internal(w_in):
    sec = lambda r: w_in[:, r[0]:r[1]]
    w_main = jnp.concatenate([sec(R_QKV), sec(R_XBC), sec(R_GATE), sec(R_MEMQ), sec(R_Z)], axis=1)
    w_dt = jnp.pad(sec(R_DT), ((0, 0), (0, 128 - SSD_HEADS)))
    return w_main, w_dt


def _from_internal(pieces, g_dt):
    dq, dk, dv, dxbc, dgate, dmemq, dz = pieces
    return [dq, dk, dv, dz, dxbc, g_dt[:, :SSD_HEADS], dmemq, dgate]


def _w_in_slab(ordered, s, dtype):
    width = D_IN // N_SHARD
    lo, hi, off, parts = s * width, (s + 1) * width, 0, []
    for p in ordered:
        a, b = max(lo, off), min(hi, off + p.shape[1])
        if a < b:
            parts.append(p[:, a - off:b - off].astype(dtype))
        off += p.shape[1]
    return jnp.concatenate(parts, axis=1)


MESH = pl.DeviceIdType.MESH
ANY = pl.BlockSpec(memory_space=pl.ANY)


def _place():
    x, y, c = lax.axis_index("x"), lax.axis_index("y"), lax.axis_index("c")
    return (x, y, c), [(1 - x, y, c), (x, 1 - y, c), (1 - x, 1 - y, c)]


def _exchange_copy(mode, ins, lands, send, recv, a, k, me, peers, arriving):
    p = peers[k]
    theirs = 2 * p[0] + p[1]
    if mode == "gather":
        src, dst = ins[a], lands[a].at[theirs if arriving else me]
    else:
        src, dst = ins[a].at[theirs], lands[a].at[k]
    return pltpu.make_async_remote_copy(src_ref=src, dst_ref=dst, send_sem=send.at[a * 3 + k],
                                        recv_sem=recv.at[a * 3 + k], device_id=p, device_id_type=MESH)


class _Ride:
    def __init__(self, srcs, mode):
        self.srcs, self.mode, self.n = list(srcs), mode, len(srcs)
        n = self.n
        self.in_specs, self.out_specs = [ANY] * n, [ANY] * n
        self.out_shape = [
            jax.ShapeDtypeStruct((N_SHARD,) + s.shape if mode == "gather" else (3,) + s.shape[1:], s.dtype)
            for s in self.srcs]
        self.scratch = [pltpu.SemaphoreType.DMA((3 * n,)), pltpu.SemaphoreType.DMA((3 * n,)),
                        pltpu.SemaphoreType.DMA((n,))]

    def _own(self, ins, lnd, sems):
        if self.mode != "gather":
            return []
        me = 2 * lax.axis_index("x") + lax.axis_index("y")
        return [pltpu.make_async_copy(ins[a], lnd[a].at[me], sems[2].at[a]) for a in range(self.n)]

    def _far(self, ins, lnd, sems, arriving):
        (x, y, c), peers = _place()
        return [_exchange_copy(self.mode, ins, lnd, sems[0], sems[1], a, k, 2 * x + y, peers, arriving)
                for a in range(self.n) for k in range(3)]

    def start(self, ins, lnd, sems):
        for cp in self._own(ins, lnd, sems) + self._far(ins, lnd, sems, False):
            cp.start()

    def finish(self, ins, lnd, sems):
        for cp in self._far(ins, lnd, sems, True):
            cp.wait_recv()
        for cp in self._far(ins, lnd, sems, False):
            cp.wait_send()
        for cp in self._own(ins, lnd, sems):
            cp.wait()


def _gather_two_level(shards, name):
    n = len(shards)

    def body(*refs):
        ins, lnd = refs[:n], refs[n:2 * n]
        send, recv, loc = refs[2 * n:]
        (x, y, c), peers = _place()
        me = 2 * x + y

        def half(ref, a, core):
            rows = shards[a].shape[0] // 2
            return ref.at[pl.ds(core * rows, rows)]

        def copy(a, j, slot, core, to):
            return pltpu.make_async_remote_copy(
                src_ref=half(ins[a], a, core) if j < 3 else half(lnd[a].at[slot], a, core),
                dst_ref=half(lnd[a].at[slot], a, core), send_sem=send.at[6 * a + j], recv_sem=recv.at[6 * a + j],
                device_id=to, device_id_type=MESH)

        own = [pltpu.make_async_copy(ins[a], lnd[a].at[me], loc.at[a]) for a in range(n)]
        far = [copy(a, k, me, c, peers[k]) for a in range(n) for k in range(3)]
        for cp in own + far:
            cp.start()
        passed = []
        for a in range(n):
            for k, p in enumerate(peers):
                theirs = 2 * p[0] + p[1]
                copy(a, k, theirs, c, p).wait_recv()
                passed.append(copy(a, 3 + k, theirs, c, (x, y, 1 - c)))
                passed[-1].start()
        for a in range(n):
            for k, p in enumerate(peers):
                copy(a, 3 + k, 2 * p[0] + p[1], 1 - c, (x, y, 1 - c)).wait_recv()
        for cp in far + passed:
            cp.wait_send()
        for cp in own:
            cp.wait()

    return pl.pallas_call(
        body, name=name, in_specs=[ANY] * n, out_specs=[ANY] * n,
        out_shape=[jax.ShapeDtypeStruct((N_SHARD,) + s.shape, s.dtype) for s in shards],
        scratch_shapes=[pltpu.SemaphoreType.DMA((6 * n,)), pltpu.SemaphoreType.DMA((6 * n,)),
                        pltpu.SemaphoreType.DMA((n,))],
    )(*shards)


def _exchange_packets(packet):
    def body(pk, pk_out, send, recv, loc):
        x, y, c = lax.axis_index("x"), lax.axis_index("y"), lax.axis_index("c")
        lin = 4 * x + 2 * y + c
        own = pltpu.make_async_copy(pk, pk_out.at[lin], loc.at[0])
        own.start()

        def pk_copy(m, slot):
            dev = (x ^ ((m >> 2) & 1), y ^ ((m >> 1) & 1), c ^ (m & 1))
            return pltpu.make_async_remote_copy(
                src_ref=pk, dst_ref=pk_out.at[slot], send_sem=send.at[m - 1], recv_sem=recv.at[m - 1],
                device_id=dev, device_id_type=MESH)

        sent = [pk_copy(m, lin) for m in range(1, N_DEV)]
        for cp in sent:
            cp.start()
        for m in range(1, N_DEV):
            pk_copy(m, lin ^ m).wait_recv()
        for cp in sent:
            cp.wait_send()
        own.wait()

    return pl.pallas_call(
        body, name="exchange_packets", in_specs=[ANY], out_specs=ANY,
        out_shape=jax.ShapeDtypeStruct((N_DEV,) + packet.shape, packet.dtype),
        scratch_shapes=[pltpu.SemaphoreType.DMA((N_DEV - 1,)), pltpu.SemaphoreType.DMA((N_DEV - 1,)),
                        pltpu.SemaphoreType.DMA((1,))],
    )(packet)


def _swap_sibling(parts, name):
    n = len(parts)

    def body(*refs):
        ins, outs = refs[:n], refs[n:2 * n]
        send, recv = refs[2 * n:]
        x, y, c = lax.axis_index("x"), lax.axis_index("y"), lax.axis_index("c")
        cps = [pltpu.make_async_remote_copy(
            src_ref=ins[a], dst_ref=outs[a], send_sem=send.at[a], recv_sem=recv.at[a],
            device_id=(x, y, 1 - c), device_id_type=MESH) for a in range(n)]
        for cp in cps:
            cp.start()
        for cp in cps:
            cp.wait_recv()
        for cp in cps:
            cp.wait_send()

    return pl.pallas_call(
        body, name=name,
        in_specs=[ANY] * n, out_specs=[ANY] * n,
        out_shape=[jax.ShapeDtypeStruct(p.shape, p.dtype) for p in parts],
        scratch_shapes=[pltpu.SemaphoreType.DMA((n,)), pltpu.SemaphoreType.DMA((n,))],
    )(*parts)


BLOCK_ELEMS = 256 * 1024


def _row_tile(R, C):
    tr = max(8, (BLOCK_ELEMS // C) // 8 * 8)
    while R % tr:
        tr -= 8
    return min(tr, R)


def _sum_parts(own, stack, name, out_dtype=f32):
    k = stack.shape[0]
    R, C = stack.shape[1:]
    tr = _row_tile(R, C)

    def body(*refs):
        o_ref = refs[-1]
        acc = refs[0][...].astype(f32)
        for r in refs[1:-1]:
            acc = acc + r[...].astype(f32)
        o_ref[...] = acc.astype(out_dtype)

    row = pl.BlockSpec((tr, C), lambda i: (i, 0))
    specs = ([row] if own is not None else []) + [
        pl.BlockSpec((None, tr, C), functools.partial(lambda i, j: (j, i, 0), j=j)) for j in range(k)]
    args = ([own] if own is not None else []) + [stack] * k
    return pl.pallas_call(
        body, name=name, grid=(R // tr,), in_specs=specs, out_specs=row,
        out_shape=jax.ShapeDtypeStruct((R, C), out_dtype), compiler_params=_params(("parallel",)),
    )(*args)


def _adamw(w, m, v, g_parts, name):
    R, C = w.shape
    tr = _row_tile(R, C)
    n_g = len(g_parts)

    def body(w_ref, m_ref, v_ref, *rest):
        g = rest[0][...]
        for r in rest[1:n_g]:
            g = g + r[...]
        g_ref, d_ref, nm_ref, nv_ref = rest[n_g:]
        nm = ADAM_B1 * m_ref[...] + (1.0 - ADAM_B1) * g
        nv = ADAM_B2 * v_ref[...] + (1.0 - ADAM_B2) * jnp.square(g)
        m_hat = nm / (1.0 - ADAM_B1 ** ADAM_STEP)
        v_hat = nv / (1.0 - ADAM_B2 ** ADAM_STEP)
        g_ref[...] = g
        d_ref[...] = -ADAM_LR * (m_hat / (jnp.sqrt(v_hat) + ADAM_EPS) + ADAM_WD * w_ref[...])
        nm_ref[...] = nm
        nv_ref[...] = nv

    row = pl.BlockSpec((tr, C), lambda i: (i, 0))
    return pl.pallas_call(
        body, name=name, grid=(R // tr,), in_specs=[row] * (3 + n_g), out_specs=[row] * 4,
        out_shape=[jax.ShapeDtypeStruct((R, C), f32)] * 4, compiler_params=_params(("parallel",)),
    )(w, m, v, *g_parts)


BIG = ("w_in", "w_mem_kv", "w_sb_out", "w_ssd_out", "w_mem_out", "w_o", "w_up", "w_down")
LATE = ("w_sb_out", "w_ssd_out", "w_mem_out", "w_o", "w_up", "w_down")
REST = BIG[1:]
COL_SHARDED = ("w_in", "w_mem_kv", "w_up")
SMALL = ("norm_mix_pre", "conv_w", "conv_b", "dt_bias", "a_log", "d_skip", "ssd_norm", "norm_mem",
         "norm_mix_post", "norm_mlp_pre", "norm_mlp_post")
WEIGHTS = ("norm_mix_pre", "w_in", "conv_w", "conv_b", "dt_bias", "a_log", "d_skip", "ssd_norm", "norm_mem",
           "w_mem_kv", "w_sb_out", "w_ssd_out", "w_mem_out", "w_o", "norm_mix_post", "norm_mlp_pre", "w_up",
           "w_down", "norm_mlp_post")
PK_ROWS = 184


def _pack(vecs):
    flat = jnp.concatenate([v.reshape(-1) for v in vecs])
    return jnp.pad(flat, (0, PK_ROWS * 128 - flat.shape[0])).reshape(PK_ROWS, 128)


def _unpack(pk, shapes):
    flat = pk.reshape(-1)
    out, off = [], 0
    for s in shapes:
        n = 1
        for d in s:
            n *= d
        out.append(flat[off:off + n].reshape(s))
        off += n
    return out


def _full_from_slabs(name, slabs):
    if name in COL_SHARDED:
        return slabs.transpose(1, 0, 2).reshape(slabs.shape[1], -1)
    return slabs.reshape(-1, slabs.shape[2])


def _slabs_from_full(name, g):
    if name in COL_SHARDED:
        return g.reshape(g.shape[0], N_SHARD, -1).transpose(1, 0, 2)
    return g.reshape(N_SHARD, -1, g.shape[1])


def kernel(x, mem, norm_mix_pre, w_in, conv_w, conv_b, dt_bias, a_log, d_skip, ssd_norm, norm_mem, w_mem_kv, w_sb_out, w_ssd_out, w_mem_out, w_o, norm_mix_post, norm_mlp_pre, w_up, w_down, norm_mlp_post, loss_target, m_norm_mix_pre, m_w_in, m_conv_w, m_conv_b, m_dt_bias, m_a_log, m_d_skip, m_ssd_norm, m_norm_mem, m_w_mem_kv, m_w_sb_out, m_w_ssd_out, m_w_mem_out, m_w_o, m_norm_mix_post, m_norm_mlp_pre, m_w_up, m_w_down, m_norm_mlp_post, v_norm_mix_pre, v_w_in, v_conv_w, v_conv_b, v_dt_bias, v_a_log, v_d_skip, v_ssd_norm, v_norm_mem, v_w_mem_kv, v_w_sb_out, v_w_ssd_out, v_w_mem_out, v_w_o, v_norm_mix_post, v_norm_mlp_pre, v_w_up, v_w_down, v_norm_mlp_post):
    env = dict(locals())
    w = {n: env[n] for n in WEIGHTS}
    mo = {n: env["m_" + n] for n in WEIGHTS}
    vo = {n: env["v_" + n] for n in WEIGHTS}
    shard = 2 * lax.axis_index("x") + lax.axis_index("y")

    first = _gather_two_level([w["w_in"][0].astype(bf16)], "gather_first")
    w_main, w_dt = _to_internal(_full_from_slabs("w_in", first[0]))
    wts = dict(w_main=w_main, w_dt=w_dt)
    ride_names = (LATE[:4], LATE[4:5], LATE[5:], ("w_mem_kv",))
    late_rides = tuple(_Ride([w[n][0].astype(bf16) for n in names] + ([w["conv_w"][0]] if i == 0 else []), "gather")
                       for i, names in enumerate(ride_names))

    def late_weights(i, lands):
        full = {n: _full_from_slabs(n, s) for n, s in zip(ride_names[i], lands)}
        if i == 0:
            full["conv_w"] = lands[-1].transpose(1, 0, 2).reshape(CONV_K, CONV_DIM)
        return full

    def rest_rides(g):
        slabs = [_slabs_from_full(n, g[n]).astype(bf16) for n in REST]
        return _Ride(slabs[5:], "scatter"), _Ride(slabs[:5], "scatter")

    core = lax.axis_index("c")
    half = D // 2

    def w_in_ride(g):
        ordered = _from_internal(g["w_main"], g["w_dt"])
        stack = jnp.stack([_w_in_slab(ordered, s, bf16) for s in range(N_SHARD)])
        keep = lax.dynamic_slice_in_dim(stack, core * half, half, axis=1)
        away = lax.dynamic_slice_in_dim(stack, (1 - core) * half, half, axis=1)
        (got,) = _swap_sibling([away], "w_in_halves_out")
        wide = lambda a: a.reshape(N_SHARD * half, -1)
        chip = _sum_parts(wide(keep), wide(got)[None], "sum_cores_w_in", bf16).reshape(N_SHARD, half, -1)
        own = lax.switch(shard, [functools.partial(_w_in_slab, ordered, s, f32) for s in range(N_SHARD)])
        own = lax.dynamic_slice_in_dim(own, core * half, half, axis=0)
        g["w_in_own"] = _sum_parts(own, lax.dynamic_index_in_dim(got, shard, 0, keepdims=True), "sum_cores_w_in_own")
        return _Ride([chip], "scatter")

    small = {n: w[n] for n in SMALL if n != "conv_w"}
    loss, grad_x, g = _local_step(x[0], mem[0], loss_target[0], wts, late_rides, late_weights, small,
                                  rest_rides, w_in_ride)
    out_g, out_d, out_m, out_v = {}, {}, {}, {}

    def apply(n, g_parts):
        res = _adamw(w[n][0], mo[n][0], vo[n][0], g_parts, name="adamw_" + n)
        out_g[n], out_d[n], out_m[n], out_v[n] = [r[None] for r in res]

    mine = _sum_parts(g["w_in_own"], g["w_in_lands"][0], name="sum_chips_w_in")
    (theirs,) = _swap_sibling([mine], "w_in_halves_back")
    g_w_in = lax.dynamic_update_slice_in_dim(jnp.zeros((D, D_IN // N_SHARD), f32), mine, core * half, axis=0)
    apply("w_in", [lax.dynamic_update_slice_in_dim(g_w_in, theirs, (1 - core) * half, axis=0)])

    packets = _exchange_packets(_pack([g[n] for n in SMALL] + [loss[:, :1]]))
    partial = []
    for n, r in zip(REST, g["rest_lands"]):
        own = lax.dynamic_index_in_dim(_slabs_from_full(n, g[n]), shard, 0, keepdims=False)
        partial.append(_sum_parts(own, r, name="sum_chips_" + n))
    other = _swap_sibling(partial, "swap_sibling")

    for n, p, q in zip(REST, partial, other):
        apply(n, [p, q])
    tot = _sum_parts(None, packets, name="sum_packets")
    shapes = [g[n].shape for n in SMALL] + [(1, 1)]
    sm = dict(zip(SMALL + ("loss",), _unpack(tot, shapes)))
    sm["conv_w"] = lax.dynamic_slice_in_dim(sm["conv_w"], shard * (CONV_DIM // N_SHARD), CONV_DIM // N_SHARD, axis=1)
    own_small = lambda d: _pack([d[n].reshape(sm[n].shape) for n in SMALL])
    res = _adamw(own_small(w), own_small(mo), own_small(vo), [own_small(sm)], name="adamw_small")
    own_shapes = [sm[n].shape for n in SMALL]
    for store, r in zip((out_g, out_d, out_m, out_v), res):
        for n, val in zip(SMALL, _unpack(r, own_shapes)):
            store[n] = val.reshape(w[n].shape)

    outs = [sm["loss"].reshape(()), grad_x[None]]
    for store in (out_g, out_d, out_m, out_v):
        outs += [store[n] for n in WEIGHTS]
    return tuple(outs)
```

```python
import functools

import jax
import jax.numpy as jnp
from jax import lax
from jax.experimental import pallas as pl
from jax.experimental.pallas import tpu as pltpu

f32 = jnp.float32
bf16 = jnp.bfloat16

D = 1024
EPS = 1e-6
SB_HD = 64
SSD_INNER = 2048
SSD_HEADS = 32
SSD_GROUPS = 4
SSD_N = 128
SSD_L = 128
CONV_K = 4
CONV_DIM = 3072
MEM_HEADS = 4
MEM_HD = 256
D_FF = 4096
D_IN = 12320
N_SHARD = 4
N_DEV = 8

P_QKV, P_XBC, P_GATE, P_MEMQ, P_Z, P_DT, P_TOT = 0, 3072, 6144, 9216, 10240, 12288, 12416
R_QKV, R_Z, R_XBC, R_DT, R_MEMQ, R_GATE = (0, 3072), (3072, 5120), (5120, 8192), (8192, 8224), (8224, 9248), (9248, 12320)

ADAM_LR = 0.001
ADAM_B1 = 0.9
ADAM_B2 = 0.999
ADAM_EPS = 1e-08
ADAM_WD = 0.01
ADAM_STEP = 10

VMEM_LIMIT = 56 * 1024 * 1024

NN = (((1,), (0,)), ((), ()))
NT = (((1,), (1,)), ((), ()))
TN = (((0,), (0,)), ((), ()))


def _dot(a, b, dims=NN):
    return lax.dot_general(a, b, dims, preferred_element_type=f32)


def _params(sem=None):
    return pltpu.CompilerParams(dimension_semantics=sem, vmem_limit_bytes=VMEM_LIMIT)


def _sigmoid(x):
    return 1.0 / (1.0 + jnp.exp(-x))


def _split2(x):
    hi = x.astype(bf16)
    lo = (x - hi.astype(f32)).astype(bf16)
    return hi, lo


def _split3(x):
    hi = x.astype(bf16)
    r = x - hi.astype(f32)
    mid = r.astype(bf16)
    lo = (r - mid.astype(f32)).astype(bf16)
    return hi, mid, lo


def _mm(a, b, mode, *, tm, tn, name, out_dtypes=(f32,), epi=None, extras=(), ride=None):
    M = a.shape[1] if mode == "tn" else a.shape[0]
    N = b.shape[0] if mode == "nt" else b.shape[1]
    tm, tn = min(tm, M), min(tn, N)
    if mode == "nn":
        (M, K), N = a.shape, b.shape[1]
        a_spec = pl.BlockSpec((tm, K), lambda i, j: (i, 0))
        b_spec = pl.BlockSpec((K, tn), lambda i, j: (0, j))
        dims = NN
    elif mode == "nt":
        (M, K), N = a.shape, b.shape[0]
        a_spec = pl.BlockSpec((tm, K), lambda i, j: (i, 0))
        b_spec = pl.BlockSpec((tn, K), lambda i, j: (j, 0))
        dims = NT
    else:
        (K, M), N = a.shape, b.shape[1]
        a_spec = pl.BlockSpec((K, tm), lambda i, j: (0, i))
        b_spec = pl.BlockSpec((K, tn), lambda i, j: (0, j))
        dims = TN
    assert M % tm == 0 and N % tn == 0, (name, M, N, tm, tn)
    n_ex, n_out = len(extras), len(out_dtypes)
    n_r = ride.n if ride else 0
    o_spec = pl.BlockSpec((tm, tn), lambda i, j: (i, j))
    grid = (M // tm, N // tn)

    def body(a_ref, b_ref, *rest):
        r_ins = rest[n_ex:n_ex + n_r]
        outs = rest[n_ex + n_r:n_ex + n_r + n_out]
        r_lnd, r_sems = rest[n_ex + n_r + n_out:n_ex + 2 * n_r + n_out], rest[n_ex + 2 * n_r + n_out:]
        i, j = pl.program_id(0), pl.program_id(1)
        if ride:
            pl.when((i == 0) & (j == 0))(lambda: ride.start(r_ins, r_lnd, r_sems))
        acc = _dot(a_ref[...].astype(bf16), b_ref[...].astype(bf16), dims)
        res = (acc,) if epi is None else epi(acc, *[e[...] for e in rest[:n_ex]])
        for o_ref, r in zip(outs, res):
            o_ref[...] = r.astype(o_ref.dtype)
        if ride:
            pl.when((i == grid[0] - 1) & (j == grid[1] - 1))(lambda: ride.finish(r_ins, r_lnd, r_sems))

    out = pl.pallas_call(
        body, name=name, grid=grid,
        in_specs=[a_spec, b_spec] + [o_spec] * n_ex + (ride.in_specs if ride else []),
        out_specs=[o_spec] * n_out + (ride.out_specs if ride else []),
        out_shape=[jax.ShapeDtypeStruct((M, N), dt) for dt in out_dtypes] + (ride.out_shape if ride else []),
        scratch_shapes=ride.scratch if ride else [],
        compiler_params=_params(("arbitrary", "arbitrary") if ride else ("parallel", "parallel")),
    )(a, b, *extras, *(ride.srcs if ride else []))
    if ride:
        return (out[0] if n_out == 1 else out[:n_out]), list(out[n_out:])
    return out[0] if n_out == 1 else out


def _mm_pieces_nt(pieces, b, add, *, tm, tn, name, ride):
    M, N = pieces[0].shape[0], b.shape[0]
    n_p, n_r = len(pieces), (ride.n if ride else 0)
    o_spec = pl.BlockSpec((tm, tn), lambda i, j: (i, j))
    grid = (M // tm, N // tn)

    def body(*refs):
        b_ref, add_ref = refs[n_p:n_p + 2]
        r_ins, o_ref = refs[n_p + 2:n_p + 2 + n_r], refs[n_p + 2 + n_r]
        r_lnd, r_sems = refs[n_p + 3 + n_r:n_p + 3 + 2 * n_r], refs[n_p + 3 + 2 * n_r:]
        i, j = pl.program_id(0), pl.program_id(1)
        if ride:
            pl.when((i == 0) & (j == 0))(lambda: ride.start(r_ins, r_lnd, r_sems))
        acc, off = add_ref[...], 0
        for r in refs[:n_p]:
            acc = acc + _dot(r[...], b_ref[:, off:off + r.shape[1]], NT)
            off += r.shape[1]
        o_ref[...] = acc
        if ride:
            pl.when((i == grid[0] - 1) & (j == grid[1] - 1))(lambda: ride.finish(r_ins, r_lnd, r_sems))

    out = pl.pallas_call(
        body, name=name, grid=grid,
        in_specs=[pl.BlockSpec((tm, p.shape[1]), lambda i, j: (i, 0)) for p in pieces]
        + [pl.BlockSpec((tn, b.shape[1]), lambda i, j: (j, 0)), o_spec] + (ride.in_specs if ride else []),
        out_specs=[o_spec] + (ride.out_specs if ride else []),
        out_shape=[jax.ShapeDtypeStruct((M, N), f32)] + (ride.out_shape if ride else []),
        scratch_shapes=ride.scratch if ride else [],
        compiler_params=_params(("arbitrary", "arbitrary")),
    )(*pieces, b, add, *(ride.srcs if ride else []))
    return out[0], list(out[1:])


def _rms_fwd(x, g, *, name, out_dtype, residual=None, tm=512):
    S, C = x.shape
    tm = min(tm, S)
    has_res = residual is not None

    def body(x_ref, g_ref, *rest):
        xv = x_ref[...]
        r = lax.rsqrt(jnp.mean(xv * xv, axis=1, keepdims=True) + EPS)
        y = xv * r * g_ref[...]
        if has_res:
            y = y + rest[0][...]
        rest[-1][...] = y.astype(out_dtype)

    row = pl.BlockSpec((tm, C), lambda i: (i, 0))
    vec = pl.BlockSpec((1, C), lambda i: (0, 0))
    args = (x, g) + ((residual,) if has_res else ())
    return pl.pallas_call(
        body, name=name, grid=(S // tm,),
        in_specs=[row, vec] + ([row] if has_res else []),
        out_specs=row, out_shape=jax.ShapeDtypeStruct((S, C), out_dtype),
        compiler_params=_params(("parallel",)),
    )(*args)


def _rms_bwd(x, dy, g, *, name, out_dtype, add=None, tm=512):
    S, C = x.shape
    tm = min(tm, S)
    has_add = add is not None

    def body(x_ref, dy_ref, g_ref, *rest):
        dx_ref, dg_ref = rest[-2], rest[-1]
        xv = x_ref[...]
        dyv = dy_ref[...].astype(f32)
        r = lax.rsqrt(jnp.mean(xv * xv, axis=1, keepdims=True) + EPS)
        xh = xv * r
        dxh = dyv * g_ref[...]
        dx = r * (dxh - xh * jnp.mean(dxh * xh, axis=1, keepdims=True))
        if has_add:
            dx = dx + rest[0][...]
        dx_ref[...] = dx.astype(out_dtype)

        @pl.when(pl.program_id(0) == 0)
        def _():
            dg_ref[...] = jnp.zeros_like(dg_ref)

        dg_ref[...] += jnp.sum(dyv * xh, axis=0, keepdims=True)

    row = pl.BlockSpec((tm, C), lambda i: (i, 0))
    vec = pl.BlockSpec((1, C), lambda i: (0, 0))
    args = (x, dy, g) + ((add,) if has_add else ())
    return pl.pallas_call(
        body, name=name, grid=(S // tm,),
        in_specs=[row, row, vec] + ([row] if has_add else []),
        out_specs=[row, vec],
        out_shape=[jax.ShapeDtypeStruct((S, C), out_dtype), jax.ShapeDtypeStruct((1, C), f32)],
        compiler_params=_params(("arbitrary",)),
    )(*args)


SB_T = 128
SB_SPENT = -120.0
SB_QB = 4
SB_TAIL = 3
SB_GROUPS = (4, 2, 1)
SB_GROUPS_BWD = (4, 2, 1)


def _sb_masks():
    lane = lax.broadcasted_iota(jnp.int32, (1, 128), 1)
    m_a = (lane < SB_HD).astype(f32)
    return m_a, 1.0 - m_a


def _chunks(a, n):
    return [a[:, u * SB_T:(u + 1) * SB_T] for u in range(n)]


def _cat(parts, axis):
    return parts[0] if len(parts) == 1 else jnp.concatenate(parts, axis=axis)


def _mask_last(a, n, mask):
    if mask is None:
        return a
    parts = _chunks(a, n)
    return _cat(parts[:-1] + [jnp.where(mask, parts[-1], 0.0)], 1)


def _sb_logits(z, n, mask):
    l1p = jnp.log(1.0 + jnp.exp(-jnp.abs(z)))
    lb = jnp.minimum(z, 0.0) - l1p
    return lb, _mask_last(lb - z, n, mask)


def _by_count(i, most, fn):
    return lax.switch(jnp.minimum(i, most - 1), [functools.partial(fn, n) for n in range(1, most + 1)])


def _chunk_matmul(parts_list, u_mat):
    out = _dot(_cat(parts_list, 0), u_mat)
    return [out[u * SB_T:(u + 1) * SB_T] for u in range(len(parts_list))]


def _chunk_cumsum(lk, n, u_mat):
    hi = lk.astype(bf16)
    lo = (lk - hi.astype(f32)).astype(bf16)
    out = _chunk_matmul(_chunks(hi, n) + _chunks(lo, n), u_mat)
    return [out[u] + out[n + u] for u in range(n)]


def _sb_fwd(proj, S, ride=None):
    nq = S // SB_T
    n_pairs = D // 128
    scale = SB_HD ** -0.5
    n_r = ride.n if ride else 0

    def body(q_ref, k_ref, v_ref, *rest):
        o_ref, t_ref = rest[n_r:n_r + 2]
        step_i = pl.program_id(1)
        if ride:
            pl.when((pl.program_id(0) == 0) & (step_i == 0))(
                lambda: ride.start(rest[:n_r], rest[n_r + 2:2 * n_r + 2], rest[2 * n_r + 2:]))
        m_a, m_b = _sb_masks()
        r_i = lax.broadcasted_iota(jnp.int32, (SB_T, SB_T), 0)
        c_i = lax.broadcasted_iota(jnp.int32, (SB_T, SB_T), 1)
        u_mat = (r_i > c_i).astype(bf16)
        causal = c_i < r_i
        q_all = q_ref[...] * scale
        q_hs = [((q * m_a).astype(bf16), (q * m_b).astype(bf16))
                for q in (q_all[b * SB_T:(b + 1) * SB_T] for b in range(SB_QB))]

        def group(q_h, j_lo, n, carry, mask):
            acc, c_a, c_b = carry
            rows = pl.ds(pl.multiple_of(j_lo * SB_T, SB_T), n * SB_T)
            k = k_ref[rows, :].astype(bf16)
            v = v_ref[rows, :]
            zs = [_dot(q_b, k, NT) for q_b in q_h]
            lbk = [_sb_logits(z, n, mask) for z in zs]
            parts = [_chunk_cumsum(lk, n, u_mat) for _, lk in lbk]
            ws, cs = [], []
            for (lb, lk), part, c in zip(lbk, parts, (c_a, c_b)):
                lb_c, lk_c = _chunks(lb, n), _chunks(lk, n)
                w_c = [None] * n
                for u in reversed(range(n)):
                    w_c[u] = jnp.exp(lb_c[u] + c + part[u])
                    c = c + jnp.sum(lk_c[u], axis=1, keepdims=True)
                ws.append(_mask_last(_cat(w_c, 1), n, mask).astype(bf16))
                cs.append(c)
            for w, m in zip(ws, (m_a, m_b)):
                acc = acc + _dot(w, (v * m).astype(bf16))
            return acc, cs[0], cs[1]

        zero_c = jnp.zeros((SB_T, 1), f32)
        init = (jnp.zeros((SB_T, 128), f32), zero_c, zero_c)
        blocks = [(step_i * SB_QB + b, q_hs[b]) for b in range(SB_QB)]

        def whole_tails():
            return tuple(group(q_h, i - SB_TAIL + 1, SB_TAIL, init, causal) for i, q_h in blocks)

        def short_tails():
            return tuple(_by_count(i, SB_TAIL, functools.partial(
                lambda n, i, q_h: group(q_h, i - n + 1, n, init, causal), i=i, q_h=q_h)) for i, q_h in blocks)

        carries = lax.cond(step_i * SB_QB >= SB_TAIL - 1, whole_tails, short_tails)

        def spent(cr):
            return (jnp.max(jnp.maximum(cr[1], cr[2])) < SB_SPENT).astype(jnp.int32)

        lane = lax.broadcasted_iota(jnp.int32, (1, 128), 1)
        for b, ((i, q_h), carry) in enumerate(zip(blocks, carries)):
            state = (i - jnp.minimum(i, SB_TAIL - 1), spent(carry), carry)
            for n in SB_GROUPS:
                def step(st, n=n, q_h=q_h):
                    left, _, cr = st
                    cr = group(q_h, left - n, n, cr, None)
                    return left - n, spent(cr), cr

                state = lax.while_loop(lambda st, n=n: (st[0] >= n) & (st[1] == 0), step, state)
            left, _, carry = state
            rows = slice(b * SB_T, (b + 1) * SB_T)
            o_ref[rows, :] = carry[0]
            t_ref[rows, :] = (jnp.where(lane == 0, carry[1], 0.0) + jnp.where(lane == SB_HD, carry[2], 0.0)
                              + jnp.where(lane == 1, left.astype(f32), 0.0))
        if ride:
            pl.when((pl.program_id(0) == n_pairs - 1) & (step_i == nq // SB_QB - 1))(
                lambda: ride.finish(rest[:n_r], rest[n_r + 2:2 * n_r + 2], rest[2 * n_r + 2:]))

    qs = pl.BlockSpec((SB_QB * SB_T, 128), lambda h, i: (i, h))
    out = pl.pallas_call(
        body, name="sb_fwd", grid=(n_pairs, nq // SB_QB),
        in_specs=[qs,
                  pl.BlockSpec((S, 128), lambda h, i: (0, n_pairs + h)),
                  pl.BlockSpec((S, 128), lambda h, i: (0, 2 * n_pairs + h))] + (ride.in_specs if ride else []),
        out_specs=[qs, qs] + (ride.out_specs if ride else []),
        out_shape=[jax.ShapeDtypeStruct((S, D), f32)] * 2 + (ride.out_shape if ride else []),
        scratch_shapes=ride.scratch if ride else [],
        compiler_params=_params(("arbitrary", "arbitrary")),
    )(proj, proj, proj, *(ride.srcs if ride else []))
    return out[0], out[1], list(out[2:])


def _sb_bwd(proj, tot_lk, do, S, ride=None):
    nq = S // SB_T
    n_pairs = D // 128
    scale = SB_HD ** -0.5
    n_r = ride.n if ride else 0

    def body(q_ref, k_ref, v_ref, t_ref, do_ref, *rest):
        dq_ref, dk_ref, dv_ref = rest[n_r:n_r + 3]
        dk_acc, dv_acc = rest[2 * n_r + 3:2 * n_r + 5]
        r_ins, r_lnd, r_sems = rest[:n_r], rest[n_r + 3:2 * n_r + 3], rest[2 * n_r + 5:]
        step_i = pl.program_id(1)
        if ride:
            pl.when((pl.program_id(0) == 0) & (step_i == 0))(lambda: ride.start(r_ins, r_lnd, r_sems))
        m_a, m_b = _sb_masks()
        r_i = lax.broadcasted_iota(jnp.int32, (SB_T, SB_T), 0)
        c_i = lax.broadcasted_iota(jnp.int32, (SB_T, SB_T), 1)
        u_inc = (r_i <= c_i).astype(bf16)
        u_exc = (r_i < c_i).astype(bf16)
        causal = c_i < r_i

        @pl.when(step_i == 0)
        def _():
            dk_acc[...] = jnp.zeros_like(dk_acc)
            dv_acc[...] = jnp.zeros_like(dv_acc)

        lane = lax.broadcasted_iota(jnp.int32, (1, 128), 1)
        blocks = []
        for b in range(SB_QB):
            rows_b = slice(b * SB_T, (b + 1) * SB_T)
            i = step_i * SB_QB + b
            q = q_ref[rows_b, :] * scale
            dov = do_ref[rows_b, :]
            tv = t_ref[rows_b, :]
            heads = []
            for m, first in ((m_a, 0), (m_b, SB_HD)):
                tot = jnp.sum(jnp.where(lane == first, tv, 0.0), axis=1, keepdims=True)
                heads.append(((q * m).astype(bf16), (dov * m).astype(bf16), tot, m))
            lowest = jnp.clip(jnp.max(jnp.where(lane == 1, tv, 0.0)).astype(jnp.int32), 0, i)
            blocks.append((i, heads, lowest))

        def group(heads, j_lo, n, carry, mask):
            dq_acc, cp_a, cp_b, ce_a, ce_b = carry
            rows = pl.ds(pl.multiple_of(j_lo * SB_T, SB_T), n * SB_T)
            k_f = k_ref[rows, :]
            k = k_f.astype(bf16)
            v = v_ref[rows, :].astype(bf16)
            zs = [_dot(h[0], k, NT) for h in heads]
            dws = [_dot(h[1], v, NT) for h in heads]
            lbk = [_sb_logits(z, n, mask) for z in zs]
            parts = [_chunk_cumsum(lk, n, u_inc) for _, lk in lbk]
            ws, es, cps = [], [], []
            for (lb, lk), part, dw, h, cp in zip(lbk, parts, dws, heads, (cp_a, cp_b)):
                lb_c, lk_c = _chunks(lb, n), _chunks(lk, n)
                w_c = []
                for u in range(n):
                    w_c.append(jnp.exp(lb_c[u] + (h[2] - cp) - part[u]))
                    cp = cp + jnp.sum(lk_c[u], axis=1, keepdims=True)
                w = _mask_last(_cat(w_c, 1), n, mask)
                ws.append(w)
                es.append(dw * w)
                cps.append(cp)
            e_parts = [_chunk_matmul(_chunks(e.astype(bf16), n), u_exc) for e in es]
            dzs, ces = [], []
            for (lb, _), e, e_part, ce in zip(lbk, es, e_parts, (ce_a, ce_b)):
                e_c = _chunks(e, n)
                big_c = []
                for u in range(n):
                    big_c.append(ce + e_part[u])
                    ce = ce + jnp.sum(e_c[u], axis=1, keepdims=True)
                sig = jnp.exp(lb)
                dz = _mask_last(e * (1.0 - sig) - _cat(big_c, 1) * sig, n, mask)
                dzs.append(dz.astype(bf16))
                ces.append(ce)
            dk_t = jnp.zeros((n * SB_T, 128), f32)
            dv_t = jnp.zeros((n * SB_T, 128), f32)
            for dz_b, w, h in zip(dzs, ws, heads):
                dq_acc = dq_acc + _dot(dz_b, (k_f * h[3]).astype(bf16))
                dk_t = dk_t + _dot(dz_b, h[0], TN)
                dv_t = dv_t + _dot(w.astype(bf16), h[1], TN)
            dk_acc[rows, :] += dk_t
            dv_acc[rows, :] += dv_t
            return dq_acc, cps[0], cps[1], ces[0], ces[1]

        zc = jnp.zeros((SB_T, 1), f32)
        carries = []
        for i, heads, lowest in blocks:
            carry = (jnp.zeros((SB_T, 128), f32), zc, zc, zc, zc)
            done = lowest
            tail_lo = i - jnp.minimum(i, SB_TAIL - 1)
            for n in SB_GROUPS_BWD:
                trips = (tail_lo - done) // n
                carry = lax.fori_loop(
                    0, trips, functools.partial(
                        lambda gi, cr, n, done, heads: group(heads, done + gi * n, n, cr, None),
                        n=n, done=done, heads=heads),
                    carry)
                done = done + trips * n
            carries.append(carry)

        def whole_tails():
            return tuple(group(heads, i - SB_TAIL + 1, SB_TAIL, cr, causal)
                         for (i, heads, _), cr in zip(blocks, carries))

        def short_tails():
            return tuple(_by_count(i, SB_TAIL, functools.partial(
                lambda n, i, heads, cr: group(heads, i - n + 1, n, cr, causal), i=i, heads=heads, cr=cr))
                for (i, heads, _), cr in zip(blocks, carries))

        carries = lax.cond(step_i * SB_QB >= SB_TAIL - 1, whole_tails, short_tails)
        for b, carry in enumerate(carries):
            dq_ref[b * SB_T:(b + 1) * SB_T, :] = (carry[0] * scale).astype(bf16)

        @pl.when(step_i == nq // SB_QB - 1)
        def _():
            dk_ref[...] = dk_acc[...].astype(bf16)
            dv_ref[...] = dv_acc[...].astype(bf16)

        if ride:
            pl.when((pl.program_id(0) == n_pairs - 1) & (step_i == nq // SB_QB - 1))(
                lambda: ride.finish(r_ins, r_lnd, r_sems))

    qs = pl.BlockSpec((SB_QB * SB_T, 128), lambda h, i: (i, h))
    full = pl.BlockSpec((S, 128), lambda h, i: (0, h))
    out = pl.pallas_call(
        body, name="sb_bwd", grid=(n_pairs, nq // SB_QB),
        in_specs=[qs,
                  pl.BlockSpec((S, 128), lambda h, i: (0, n_pairs + h)),
                  pl.BlockSpec((S, 128), lambda h, i: (0, 2 * n_pairs + h)),
                  qs, qs] + (ride.in_specs if ride else []),
        out_specs=[qs, full, full] + (ride.out_specs if ride else []),
        out_shape=[jax.ShapeDtypeStruct((S, D), bf16)] * 3 + (ride.out_shape if ride else []),
        scratch_shapes=[pltpu.VMEM((S, 128), f32), pltpu.VMEM((S, 128), f32)] + (ride.scratch if ride else []),
        compiler_params=_params(("arbitrary", "arbitrary")),
    )(proj, proj, proj, tot_lk, do, *(ride.srcs if ride else []))
    return out[0], out[1], out[2], list(out[3:])


CONV_CB = 256
HALO = 8


def _conv_fwd(proj, conv_w, conv_b, S, ride=None):
    tr = min(512, S)
    n_r = ride.n if ride else 0
    n_c = CONV_DIM // CONV_CB

    def body(x_ref, w_ref, b_ref, *rest):
        xc_ref, xbc_ref = rest[n_r:n_r + 2]
        r_ins, r_lnd, r_sems = rest[:n_r], rest[n_r + 2:2 * n_r + 2], rest[2 * n_r + 2:]
        if ride:
            pl.when(pl.program_id(0) == 0)(lambda: ride.start(r_ins, r_lnd, r_sems))
        w = w_ref[...]
        for t in range(S // tr):
            cur = x_ref[t * tr:(t + 1) * tr, :]
            halo = x_ref[t * tr - HALO:t * tr, :] if t else jnp.zeros((HALO, CONV_CB), f32)
            win = jnp.concatenate([halo, cur], axis=0)
            acc = b_ref[...] + w[CONV_K - 1:CONV_K, :] * cur
            for k in range(CONV_K - 1):
                acc = acc + w[k:k + 1, :] * pltpu.roll(win, CONV_K - 1 - k, 0)[HALO:, :]
            xc_ref[t * tr:(t + 1) * tr, :] = acc
            xbc_ref[t * tr:(t + 1) * tr, :] = acc * _sigmoid(acc)
        if ride:
            pl.when(pl.program_id(0) == n_c - 1)(lambda: ride.finish(r_ins, r_lnd, r_sems))

    col = pl.BlockSpec((S, CONV_CB), lambda c: (0, c))
    out = pl.pallas_call(
        body, name="conv_fwd", grid=(n_c,),
        in_specs=[pl.BlockSpec((S, CONV_CB), lambda c: (0, P_XBC // CONV_CB + c)),
                  pl.BlockSpec((CONV_K, CONV_CB), lambda c: (0, c)),
                  pl.BlockSpec((1, CONV_CB), lambda c: (0, c))] + (ride.in_specs if ride else []),
        out_specs=[col, col] + (ride.out_specs if ride else []),
        out_shape=[jax.ShapeDtypeStruct((S, CONV_DIM), f32)] * 2 + (ride.out_shape if ride else []),
        scratch_shapes=ride.scratch if ride else [],
        compiler_params=_params(("arbitrary",)),
    )(proj, conv_w, conv_b, *(ride.srcs if ride else []))
    return out[0], out[1], list(out[2:])


def _conv_bwd(proj, xc, dxbc, conv_w, S):
    tr = min(512, S)

    def body(x_ref, xc_ref, dy_ref, w_ref, dx_ref, dw_ref, db_ref, dxc_s):
        w = w_ref[...]
        xcv = xc_ref[...]
        sg = _sigmoid(xcv)
        dxc_s[0:S, :] = dy_ref[...] * (sg * (1.0 + xcv * (1.0 - sg)))
        dxc_s[S:S + HALO, :] = jnp.zeros((HALO, CONV_CB), f32)
        dws = [jnp.zeros((1, CONV_CB), f32) for _ in range(CONV_K)]
        db = jnp.zeros((1, CONV_CB), f32)
        for t in range(S // tr):
            cur = x_ref[t * tr:(t + 1) * tr, :]
            halo = x_ref[t * tr - HALO:t * tr, :] if t else jnp.zeros((HALO, CONV_CB), f32)
            win = jnp.concatenate([halo, cur], axis=0)
            dwin = dxc_s[t * tr:(t + 1) * tr + HALO, :]
            dcur = dwin[0:tr, :]
            db = db + jnp.sum(dcur, axis=0, keepdims=True)
            dws[CONV_K - 1] = dws[CONV_K - 1] + jnp.sum(dcur * cur, axis=0, keepdims=True)
            dx = w[CONV_K - 1:CONV_K, :] * dcur
            for k in range(CONV_K - 1):
                sh = CONV_K - 1 - k
                dws[k] = dws[k] + jnp.sum(dcur * pltpu.roll(win, sh, 0)[HALO:, :], axis=0, keepdims=True)
                dx = dx + w[k:k + 1, :] * pltpu.roll(dwin, tr + HALO - sh, 0)[0:tr, :]
            dx_ref[t * tr:(t + 1) * tr, :] = dx.astype(bf16)
        dw_ref[...] = jnp.concatenate(dws + [jnp.zeros((8 - CONV_K, CONV_CB), f32)], axis=0)
        db_ref[...] = db

    col = pl.BlockSpec((S, CONV_CB), lambda c: (0, c))
    return pl.pallas_call(
        body, name="conv_bwd", grid=(CONV_DIM // CONV_CB,),
        in_specs=[pl.BlockSpec((S, CONV_CB), lambda c: (0, P_XBC // CONV_CB + c)), col, col,
                  pl.BlockSpec((CONV_K, CONV_CB), lambda c: (0, c))],
        out_specs=[col, pl.BlockSpec((8, CONV_CB), lambda c: (0, c)), pl.BlockSpec((1, CONV_CB), lambda c: (0, c))],
        out_shape=[jax.ShapeDtypeStruct((S, CONV_DIM), bf16), jax.ShapeDtypeStruct((8, CONV_DIM), f32),
                   jax.ShapeDtypeStruct((1, CONV_DIM), f32)],
        scratch_shapes=[pltpu.VMEM((S + HALO, CONV_CB), f32)],
        compiler_params=_params(("parallel",)),
    )(proj, xc, dxbc, conv_w)


N_PAIR = SSD_HEADS // 2
NEG = -1e30


def _softplus(x):
    return jnp.maximum(x, 0.0) + jnp.log(1.0 + jnp.exp(-jnp.abs(x)))


def _ssd_common(dtr, dtb, alog):
    L = SSD_L
    r_i = lax.broadcasted_iota(jnp.int32, (L, L), 0)
    c_i = lax.broadcasted_iota(jnp.int32, (L, L), 1)
    dt = _softplus(dtr + dtb)
    a = -jnp.exp(alog)
    da = dt * a
    lower = (r_i >= c_i).astype(bf16)
    upper = (r_i <= c_i).astype(bf16)
    parts = _split3(da)
    a_cs = sum(_dot(lower, p) for p in parts)
    a_cs_t = sum(_dot(p, upper, TN) for p in parts)
    return dt, a, a_cs, a_cs_t, r_i >= c_i


def _pair_vec(lane, v, h):
    return jnp.where(lane < SB_HD, v[:, h:h + 1], v[:, h + 1:h + 2])


def _decay_mat(a_cs, a_cs_t, h, tril):
    return jnp.exp(jnp.where(tril, a_cs[:, h:h + 1] - a_cs_t[h:h + 1, :], NEG))


def _ssd_fwd(xbc, proj, pdt, dt_bias_p, a_log_p, d_skip_c, ssd_norm, S, ride=None):
    L = SSD_L
    nc = S // L
    n_r = ride.n if ride else 0

    def body(xbc_ref, dt_ref, z_ref, dtb_ref, alog_ref, dsk_ref, gn_ref, *rest):
        y_ref, yn_ref, hp_ref = rest[n_r:n_r + 3]
        state = rest[2 * n_r + 3]
        r_ins, r_lnd, r_sems = rest[:n_r], rest[n_r + 3:2 * n_r + 3], rest[2 * n_r + 4:]
        c = pl.program_id(0)
        if ride:
            pl.when(c == 0)(lambda: ride.start(r_ins, r_lnd, r_sems))

        @pl.when(c == 0)
        def _():
            state[...] = jnp.zeros_like(state)

        hp_ref[0] = state[...]
        lane = lax.broadcasted_iota(jnp.int32, (1, 128), 1)
        row128 = lax.broadcasted_iota(jnp.int32, (128, 1), 0)
        m_a, m_b = _sb_masks()
        dt, a, a_cs, a_cs_t, tril = _ssd_common(dt_ref[...], dtb_ref[...], alog_ref[...])
        a_last = a_cs[L - 1:L, :]
        for g in range(SSD_GROUPS):
            b_g = xbc_ref[:, SSD_INNER + g * SSD_N:SSD_INNER + (g + 1) * SSD_N].astype(bf16)
            c_g = xbc_ref[:, SSD_INNER + (SSD_GROUPS + g) * SSD_N:SSD_INNER + (SSD_GROUPS + g + 1) * SSD_N].astype(bf16)
            cb = _dot(c_g, b_g, NT)
            for pr in range(4):
                h = 8 * g + 2 * pr
                pi = h // 2
                cols = slice(pi * 128, (pi + 1) * 128)
                xs = xbc_ref[:, cols]
                x = xs * _pair_vec(lane, dt, h)
                acs = _pair_vec(lane, a_cs, h)
                al = _pair_vec(lane, a_last, h)
                w_a = (cb * _decay_mat(a_cs, a_cs_t, h, tril)).astype(bf16)
                w_b = (cb * _decay_mat(a_cs, a_cs_t, h + 1, tril)).astype(bf16)
                yd = _dot(w_a, (x * m_a).astype(bf16)) + _dot(w_b, (x * m_b).astype(bf16))
                hp = state[pi]
                yo = _dot(c_g, hp.astype(bf16), NT) * jnp.exp(acs)
                y_ref[:, cols] = yd + yo + dsk_ref[:, cols] * xs
                dec = jnp.exp(jnp.where(row128 < SB_HD, a_last[:, h:h + 1], a_last[:, h + 1:h + 2]))
                state[pi] = hp * dec + _dot((x * jnp.exp(al - acs)).astype(bf16), b_g, TN)
        zz = z_ref[...]
        y2 = y_ref[...] * (zz * _sigmoid(zz))
        gw = SSD_INNER // SSD_GROUPS
        for g in range(SSD_GROUPS):
            yg = y2[:, g * gw:(g + 1) * gw]
            rg = lax.rsqrt(jnp.mean(yg * yg, axis=1, keepdims=True) + EPS)
            yn_ref[:, g * gw:(g + 1) * gw] = (yg * rg * gn_ref[:, g * gw:(g + 1) * gw]).astype(bf16)
        if ride:
            pl.when(c == nc - 1)(lambda: ride.finish(r_ins, r_lnd, r_sems))

    vec128 = pl.BlockSpec((1, 128), lambda c: (0, 0))
    vecin = pl.BlockSpec((1, SSD_INNER), lambda c: (0, 0))
    rows = pl.BlockSpec((L, SSD_INNER), lambda c: (c, 0))
    out = pl.pallas_call(
        body, name="ssd_fwd", grid=(nc,),
        in_specs=[pl.BlockSpec((L, CONV_DIM), lambda c: (c, 0)),
                  pl.BlockSpec((L, 128), lambda c: (c, 0)),
                  pl.BlockSpec((L, SSD_INNER), lambda c: (c, P_Z // SSD_INNER)),
                  vec128, vec128, vecin, vecin] + (ride.in_specs if ride else []),
        out_specs=[rows, rows, pl.BlockSpec((1, N_PAIR, 128, SSD_N), lambda c: (c, 0, 0, 0))]
        + (ride.out_specs if ride else []),
        out_shape=[jax.ShapeDtypeStruct((S, SSD_INNER), f32), jax.ShapeDtypeStruct((S, SSD_INNER), bf16),
                   jax.ShapeDtypeStruct((nc, N_PAIR, 128, SSD_N), f32)] + (ride.out_shape if ride else []),
        scratch_shapes=[pltpu.VMEM((N_PAIR, 128, SSD_N), f32)] + (ride.scratch if ride else []),
        compiler_params=_params(("arbitrary",)),
    )(xbc, pdt, proj, dt_bias_p, a_log_p, d_skip_c, ssd_norm, *(ride.srcs if ride else []))
    return out[0], out[1], out[2], list(out[3:])


def _sum_all(v):
    return jnp.sum(jnp.sum(v, axis=1, keepdims=True), axis=0, keepdims=True)


def _ssd_bwd(dyn, y, xbc, proj, pdt, hprev, dt_bias_p, a_log_p, d_skip_c, ssd_norm, S, ride=None):
    L = SSD_L
    nc = S // L
    n_r = ride.n if ride else 0

    col = lax.broadcasted_iota(jnp.int32, (2 * SSD_INNER, 128), 0)
    head = lax.broadcasted_iota(jnp.int32, (2 * SSD_INNER, 128), 1)
    sel_pair = (col[:SSD_INNER] // SB_HD == head[:SSD_INNER]).astype(bf16)
    sel_head = (col // 128 == head).astype(bf16)

    def body(*refs):
        (dyn_ref, y_ref, xbc_ref, dt_ref, z_ref, hp_ref, dtb_ref, alog_ref, dsk_ref, gn_ref,
         selp_ref, selh_ref) = refs[:12]
        dz_ref, dxbc_ref, ddt_ref, dgn_ref, dsk_out, dalog_ref, ddtb_ref = refs[12 + n_r:19 + n_r]
        dstate, dy_s, st_a, st_q, st_d, st_x, dat = refs[19 + 2 * n_r:26 + 2 * n_r]
        r_ins, r_lnd, r_sems = refs[12:12 + n_r], refs[19 + n_r:19 + 2 * n_r], refs[26 + 2 * n_r:]
        c = pl.program_id(0)
        if ride:
            pl.when(c == 0)(lambda: ride.start(r_ins, r_lnd, r_sems))

        @pl.when(c == 0)
        def _():
            dat[...] = jnp.zeros_like(dat)
            dstate[...] = jnp.zeros_like(dstate)
            dgn_ref[...] = jnp.zeros_like(dgn_ref)
            dsk_out[...] = jnp.zeros_like(dsk_out)
            dalog_ref[...] = jnp.zeros_like(dalog_ref)
            ddtb_ref[...] = jnp.zeros_like(ddtb_ref)

        lane = lax.broadcasted_iota(jnp.int32, (1, 128), 1)
        row128 = lax.broadcasted_iota(jnp.int32, (128, 1), 0)
        rowl = lax.broadcasted_iota(jnp.int32, (L, 1), 0)
        m_a, m_b = _sb_masks()
        dtr = dt_ref[...]
        dt, a, a_cs, a_cs_t, tril = _ssd_common(dtr, dtb_ref[...], alog_ref[...])
        a_last = a_cs[L - 1:L, :]

        zz = z_ref[...]
        sg = _sigmoid(zz)
        silu = zz * sg
        yv = y_ref[...]
        y2 = yv * silu
        gw = SSD_INNER // SSD_GROUPS
        for g in range(SSD_GROUPS):
            sl = slice(g * gw, (g + 1) * gw)
            yg = y2[:, sl]
            rg = lax.rsqrt(jnp.mean(yg * yg, axis=1, keepdims=True) + EPS)
            yh = yg * rg
            dyn_g = dyn_ref[:, sl]
            dgn_ref[:, sl] += jnp.sum(dyn_g * yh, axis=0, keepdims=True)
            dyh = dyn_g * gn_ref[:, sl]
            dy2 = rg * (dyh - yh * jnp.mean(dyh * yh, axis=1, keepdims=True))
            dy_s[:, sl] = dy2 * silu[:, sl]
            dz_ref[:, sl] = (dy2 * yv[:, sl] * (sg[:, sl] * (1.0 + zz[:, sl] * (1.0 - sg[:, sl])))).astype(bf16)

        last_row = jnp.zeros((1, 128), f32)
        dsk_acc = jnp.zeros((1, 128), f32)
        for g in range(SSD_GROUPS):
            bsl = slice(SSD_INNER + g * SSD_N, SSD_INNER + (g + 1) * SSD_N)
            csl = slice(SSD_INNER + (SSD_GROUPS + g) * SSD_N, SSD_INNER + (SSD_GROUPS + g + 1) * SSD_N)
            b_g = xbc_ref[:, bsl].astype(bf16)
            c_g = xbc_ref[:, csl].astype(bf16)
            cb = _dot(c_g, b_g, NT)
            dcb = jnp.zeros((L, L), f32)
            dc_g = jnp.zeros((L, SSD_N), f32)
            db_g = jnp.zeros((L, SSD_N), f32)
            for pr in range(4):
                h = 8 * g + 2 * pr
                pi = h // 2
                cols = slice(pi * 128, (pi + 1) * 128)
                xs = xbc_ref[:, cols]
                dt_p = _pair_vec(lane, dt, h)
                x = xs * dt_p
                acs = _pair_vec(lane, a_cs, h)
                al = _pair_vec(lane, a_last, h)
                e_a = jnp.exp(acs)
                dte = jnp.exp(al - acs)
                m_mat_a = _decay_mat(a_cs, a_cs_t, h, tril)
                m_mat_b = _decay_mat(a_cs, a_cs_t, h + 1, tril)
                dyp = dy_s[:, cols]
                dsk = dsk_ref[:, cols]
                d_hn = dstate[pi]
                hp = hp_ref[0, pi]
                dy_a = (dyp * m_a).astype(bf16)
                dy_b = (dyp * m_b).astype(bf16)
                x_b = x.astype(bf16)
                gm_a = _dot(dy_a, x_b, NT) * m_mat_a
                gm_b = _dot(dy_b, x_b, NT) * m_mat_b
                dcb = dcb + gm_a + gm_b
                dx_d = _dot((cb * m_mat_a).astype(bf16), dy_a, TN) + _dot((cb * m_mat_b).astype(bf16), dy_b, TN)
                dx_s = _dot(b_g, d_hn.astype(bf16), NT) * dte
                dx = dx_d + dx_s
                dxbc_ref[:, cols] = dx * dt_p + dsk * dyp
                xdxs = x * dx_s
                st_x[:, cols] = xdxs
                st_a[:, cols] = dyp * (_dot(c_g, hp.astype(bf16), NT) * e_a) - xdxs
                st_d[:, cols] = dx * xs
                hh = d_hn * hp
                dsk_row = jnp.sum(dyp * xs, axis=0, keepdims=True)
                dec = jnp.exp(jnp.where(row128 < SB_HD, a_last[:, h:h + 1], a_last[:, h + 1:h + 2]))
                for hd, m, gm in ((h, m_a, gm_a), (h + 1, m_b, gm_b)):
                    half = slice(0, SB_HD) if hd == h else slice(SB_HD, 128)
                    qm = gm * cb
                    st_q[:, hd * 128:(hd + 1) * 128] = qm
                    dat[hd:hd + 1, :] = jnp.sum(qm, axis=0, keepdims=True)
                    hh_sum = jnp.sum(jnp.sum(hh[half, :], axis=0, keepdims=True), axis=1, keepdims=True)
                    last_row = jnp.where(lane == hd, jnp.exp(a_last[:, hd:hd + 1]) * hh_sum, last_row)
                    dsk_acc = jnp.where(lane == hd, jnp.sum(dsk_row * m, axis=1, keepdims=True), dsk_acc)
                dye = (dyp * e_a).astype(bf16)
                dc_g = dc_g + _dot(dye, hp.astype(bf16))
                db_g = db_g + _dot((x * dte).astype(bf16), d_hn.astype(bf16))
                dstate[pi] = dec * d_hn + _dot(dye, c_g, TN)
            dcb_b = dcb.astype(bf16)
            dxbc_ref[:, csl] = dc_g + _dot(dcb_b, b_g)
            dxbc_ref[:, bsl] = db_g + _dot(dcb_b, c_g, TN)

        r_i = lax.broadcasted_iota(jnp.int32, (L, L), 0)
        c_i = lax.broadcasted_iota(jnp.int32, (L, L), 1)
        rev = (r_i <= c_i).astype(bf16)

        def head_sums(st, sel, split=_split2):
            return sum(_dot(p, sel[...]) for p in split(st[...]))

        last_row = last_row + jnp.sum(head_sums(st_x, selp_ref), axis=0, keepdims=True)
        d_acs = (head_sums(st_a, selp_ref) + head_sums(st_q, selh_ref, _split3)
                 + jnp.where(rowl == L - 1, last_row, 0.0))
        ddt_x = head_sums(st_d, selp_ref)
        dda = sum(_dot(rev, p) for p in _split3(d_acs)) - sum(_dot(rev, p, NT) for p in _split3(dat[...]))
        ddt = ddt_x + dda * a
        dalog_ref[...] += jnp.sum(dda * dt, axis=0, keepdims=True) * a
        ddtr = jnp.where(lane < SSD_HEADS, ddt * _sigmoid(dtr + dtb_ref[...]), 0.0)
        ddt_ref[...] = ddtr.astype(bf16)
        ddtb_ref[...] += jnp.sum(ddtr, axis=0, keepdims=True)
        dsk_out[...] += dsk_acc
        if ride:
            pl.when(c == nc - 1)(lambda: ride.finish(r_ins, r_lnd, r_sems))

    rv = lambda c: nc - 1 - c
    vec128 = pl.BlockSpec((1, 128), lambda c: (0, 0))
    vecin = pl.BlockSpec((1, SSD_INNER), lambda c: (0, 0))
    rows = pl.BlockSpec((L, SSD_INNER), lambda c: (rv(c), 0))
    return pl.pallas_call(
        body, name="ssd_bwd", grid=(nc,),
        in_specs=[rows, rows,
                  pl.BlockSpec((L, CONV_DIM), lambda c: (rv(c), 0)),
                  pl.BlockSpec((L, 128), lambda c: (rv(c), 0)),
                  pl.BlockSpec((L, SSD_INNER), lambda c: (rv(c), P_Z // SSD_INNER)),
                  pl.BlockSpec((1, N_PAIR, 128, SSD_N), lambda c: (rv(c), 0, 0, 0)),
                  vec128, vec128, vecin, vecin,
                  pl.BlockSpec((SSD_INNER, 128), lambda c: (0, 0)),
                  pl.BlockSpec((2 * SSD_INNER, 128), lambda c: (0, 0))] + (ride.in_specs if ride else []),
        out_specs=[rows, pl.BlockSpec((L, CONV_DIM), lambda c: (rv(c), 0)),
                   pl.BlockSpec((L, 128), lambda c: (rv(c), 0)), vecin, vec128, vec128, vec128]
        + (ride.out_specs if ride else []),
        out_shape=[jax.ShapeDtypeStruct((S, SSD_INNER), bf16), jax.ShapeDtypeStruct((S, CONV_DIM), f32),
                   jax.ShapeDtypeStruct((S, 128), bf16), jax.ShapeDtypeStruct((1, SSD_INNER), f32),
                   jax.ShapeDtypeStruct((1, 128), f32), jax.ShapeDtypeStruct((1, 128), f32),
                   jax.ShapeDtypeStruct((1, 128), f32)] + (ride.out_shape if ride else []),
        scratch_shapes=[pltpu.VMEM((N_PAIR, 128, SSD_N), f32), pltpu.VMEM((L, SSD_INNER), f32),
                        pltpu.VMEM((L, SSD_INNER), f32), pltpu.VMEM((L, 2 * SSD_INNER), f32),
                        pltpu.VMEM((L, SSD_INNER), f32), pltpu.VMEM((L, SSD_INNER), f32),
                        pltpu.VMEM((128, L), f32)]
        + (ride.scratch if ride else []),
        compiler_params=_params(("arbitrary",)),
    )(dyn, y, xbc, pdt, proj, hprev, dt_bias_p, a_log_p, d_skip_c, ssd_norm, sel_pair, sel_head,
      *(ride.srcs if ride else []))


MEM_W = MEM_HEADS * MEM_HD


def _mem_probs(q, k):
    s = _dot(q, k, NT) * (MEM_HD ** -0.5)
    s = s - jnp.max(s, axis=1, keepdims=True)
    p = jnp.exp(s)
    return p / jnp.sum(p, axis=1, keepdims=True)


def _mem_fwd(proj, kv, S, tm=512):
    tm = min(tm, S)
    M = kv.shape[0]

    def body(q_ref, kv_ref, o_ref):
        for h in range(MEM_HEADS):
            sl = slice(h * MEM_HD, (h + 1) * MEM_HD)
            vsl = slice(MEM_W + h * MEM_HD, MEM_W + (h + 1) * MEM_HD)
            p = _mem_probs(q_ref[:, sl].astype(bf16), kv_ref[:, sl].astype(bf16))
            o_ref[:, sl] = _dot(p.astype(bf16), kv_ref[:, vsl].astype(bf16)).astype(bf16)

    return pl.pallas_call(
        body, name="mem_fwd", grid=(S // tm,),
        in_specs=[pl.BlockSpec((tm, MEM_W), lambda i: (i, P_MEMQ // MEM_W)),
                  pl.BlockSpec((M, 2 * MEM_W), lambda i: (0, 0))],
        out_specs=pl.BlockSpec((tm, MEM_W), lambda i: (i, 0)),
        out_shape=jax.ShapeDtypeStruct((S, MEM_W), bf16),
        compiler_params=_params(("parallel",)),
    )(proj, kv)


def _mem_bwd(proj, kv, dy, S, tm=512):
    tm = min(tm, S)
    M = kv.shape[0]
    scale = MEM_HD ** -0.5

    def body(q_ref, kv_ref, dy_ref, dq_ref, dkv_ref):
        @pl.when(pl.program_id(0) == 0)
        def _():
            dkv_ref[...] = jnp.zeros_like(dkv_ref)

        for h in range(MEM_HEADS):
            sl = slice(h * MEM_HD, (h + 1) * MEM_HD)
            vsl = slice(MEM_W + h * MEM_HD, MEM_W + (h + 1) * MEM_HD)
            q = q_ref[:, sl].astype(bf16)
            k = kv_ref[:, sl].astype(bf16)
            v = kv_ref[:, vsl].astype(bf16)
            dyh = dy_ref[:, sl].astype(bf16)
            p = _mem_probs(q, k)
            dp = _dot(dyh, v, NT)
            ds = (p * (dp - jnp.sum(dp * p, axis=1, keepdims=True)) * scale).astype(bf16)
            dq_ref[:, sl] = _dot(ds, k).astype(bf16)
            dkv_ref[:, sl] += _dot(ds, q, TN)
            dkv_ref[:, vsl] += _dot(p.astype(bf16), dyh, TN)

    return pl.pallas_call(
        body, name="mem_bwd", grid=(S // tm,),
        in_specs=[pl.BlockSpec((tm, MEM_W), lambda i: (i, P_MEMQ // MEM_W)),
                  pl.BlockSpec((M, 2 * MEM_W), lambda i: (0, 0)),
                  pl.BlockSpec((tm, MEM_W), lambda i: (i, 0))],
        out_specs=[pl.BlockSpec((tm, MEM_W), lambda i: (i, 0)), pl.BlockSpec((M, 2 * MEM_W), lambda i: (0, 0))],
        out_shape=[jax.ShapeDtypeStruct((S, MEM_W), bf16), jax.ShapeDtypeStruct((M, 2 * MEM_W), f32)],
        compiler_params=_params(("arbitrary",)),
    )(proj, kv, dy)


def _merge_fwd(proj, t0, t1, t2, S, tm=512):
    tm = min(tm, S)

    def body(g_ref, t0_ref, t1_ref, t2_ref, o_ref):
        acc = jnp.zeros((tm, D), f32)
        for b, t_ref in enumerate((t0_ref, t1_ref, t2_ref)):
            acc = acc + _sigmoid(g_ref[:, b * D:(b + 1) * D]) * t_ref[...]
        o_ref[...] = acc.astype(bf16)

    row = pl.BlockSpec((tm, D), lambda i: (i, 0))
    return pl.pallas_call(
        body, name="merge_fwd", grid=(S // tm,),
        in_specs=[pl.BlockSpec((tm, 3 * D), lambda i: (i, P_GATE // (3 * D))), row, row, row],
        out_specs=row, out_shape=jax.ShapeDtypeStruct((S, D), bf16),
        compiler_params=_params(("parallel",)),
    )(proj, t0, t1, t2)


def _merge_bwd(proj, t0, t1, t2, dm, S, tm=512):
    tm = min(tm, S)

    def body(g_ref, t0_ref, t1_ref, t2_ref, dm_ref, d0_ref, d1_ref, d2_ref, dg_ref):
        dmv = dm_ref[...]
        for b, (t_ref, d_ref) in enumerate(((t0_ref, d0_ref), (t1_ref, d1_ref), (t2_ref, d2_ref))):
            sg = _sigmoid(g_ref[:, b * D:(b + 1) * D])
            d_ref[...] = (dmv * sg).astype(bf16)
            dg_ref[:, b * D:(b + 1) * D] = (dmv * t_ref[...] * sg * (1.0 - sg)).astype(bf16)

    row = pl.BlockSpec((tm, D), lambda i: (i, 0))
    return pl.pallas_call(
        body, name="merge_bwd", grid=(S // tm,),
        in_specs=[pl.BlockSpec((tm, 3 * D), lambda i: (i, P_GATE // (3 * D))), row, row, row, row],
        out_specs=[row, row, row, pl.BlockSpec((tm, 3 * D), lambda i: (i, 0))],
        out_shape=[jax.ShapeDtypeStruct((S, D), bf16)] * 3 + [jax.ShapeDtypeStruct((S, 3 * D), bf16)],
        compiler_params=_params(("parallel",)),
    )(proj, t0, t1, t2, dm)


def _loss_head(ff, g, h1, target, S, tm=512):
    tm = min(tm, S)

    def body(ff_ref, g_ref, h1_ref, t_ref, dh_ref, loss_ref):
        xv = ff_ref[...]
        r = lax.rsqrt(jnp.mean(xv * xv, axis=1, keepdims=True) + EPS)
        err = h1_ref[...] + xv * r * g_ref[...] - t_ref[...]
        dh_ref[...] = err * (1.0 / D)

        @pl.when(pl.program_id(0) == 0)
        def _():
            loss_ref[...] = jnp.zeros_like(loss_ref)

        loss_ref[...] += 0.5 * _sum_all(jnp.mean(err * err, axis=1, keepdims=True)) * jnp.ones((1, 128), f32)

    row = pl.BlockSpec((tm, D), lambda i: (i, 0))
    return pl.pallas_call(
        body, name="loss_head", grid=(S // tm,),
        in_specs=[row, pl.BlockSpec((1, D), lambda i: (0, 0)), row, row],
        out_specs=[row, pl.BlockSpec((1, 128), lambda i: (0, 0))],
        out_shape=[jax.ShapeDtypeStruct((S, D), f32), jax.ShapeDtypeStruct((1, 128), f32)],
        compiler_params=_params(("arbitrary",)),
    )(ff, g, h1, target)


def _local_step(x, mem, target, wts, late_rides, late_weights, small, rest_rides, w_in_ride):
    S = x.shape[0]
    M = mem.shape[0]
    pad = lambda v: jnp.pad(v, ((0, 0), (0, 128 - SSD_HEADS)))
    dtb_p, alog_p = pad(small["dt_bias"]), pad(small["a_log"])
    dsk_c = jnp.repeat(small["d_skip"], SB_HD, axis=1)

    u = _rms_fwd(x, small["norm_mix_pre"], name="norm_pre", out_dtype=bf16)
    rides = late_rides or (None, None, None, None)
    if late_rides:
        proj, lands_a = _mm(u, wts["w_main"], "nn", tm=1024, tn=1024, name="in_proj", ride=rides[0])
    else:
        proj, lands_a = _mm(u, wts["w_main"], "nn", tm=1024, tn=1024, name="in_proj"), []
    pdt = _mm(u, wts["w_dt"], "nn", tm=1024, tn=128, name="in_proj_dt")
    y_sb, tot_lk, lands_b = _sb_fwd(proj, S, rides[1])
    wts = dict(wts, **late_weights(0, lands_a))
    small = dict(small, conv_w=wts.pop("conv_w"))
    xc, xbc, lands_d = _conv_fwd(proj, small["conv_w"], small["conv_b"], S, rides[3])
    y_ssd, yn, hprev, lands_c = _ssd_fwd(xbc, proj, pdt, dtb_p, alog_p, dsk_c, small["ssd_norm"], S, rides[2])
    wts = dict(wts, **late_weights(1, lands_b), **late_weights(2, lands_c), **late_weights(3, lands_d))
    mn = _rms_fwd(mem, small["norm_mem"], name="norm_mem", out_dtype=bf16, tm=min(512, M))
    kv = _mm(mn, wts["w_mem_kv"], "nn", tm=M, tn=1024, name="mem_kv")
    y_mem = _mem_fwd(proj, kv, S)
    t0 = _mm(y_sb, wts["w_sb_out"], "nn", tm=1024, tn=1024, name="sb_out")
    t1 = _mm(yn, wts["w_ssd_out"], "nn", tm=1024, tn=1024, name="ssd_out")
    t2 = _mm(y_mem, wts["w_mem_out"], "nn", tm=1024, tn=1024, name="mem_out")
    merged = _merge_fwd(proj, t0, t1, t2, S)
    mix = _mm(merged, wts["w_o"], "nn", tm=1024, tn=1024, name="w_o")
    h1 = _rms_fwd(mix, small["norm_mix_post"], name="norm_mix_post", out_dtype=f32, residual=x)
    u2 = _rms_fwd(h1, small["norm_mlp_pre"], name="norm_mlp_pre", out_dtype=bf16)
    a_up, hrelu = _mm(u2, wts["w_up"], "nn", tm=1024, tn=1024, name="mlp_up", out_dtypes=(f32, bf16),
                      epi=lambda acc: (acc, jnp.square(jnp.maximum(acc, 0.0))))
    ff = _mm(hrelu, wts["w_down"], "nn", tm=1024, tn=1024, name="mlp_down")
    dh2, loss = _loss_head(ff, small["norm_mlp_post"], h1, target, S)

    g = {}
    dff, g["norm_mlp_post"] = _rms_bwd(ff, dh2, small["norm_mlp_post"], name="norm_mlp_post_bwd", out_dtype=bf16)
    da = _mm(dff, wts["w_down"], "nt", tm=1024, tn=1024, name="mlp_down_dx", out_dtypes=(bf16,),
             epi=lambda acc, a: (acc * (2.0 * jnp.maximum(a, 0.0)),), extras=(a_up,))
    g["w_down"] = _mm(hrelu, dff, "tn", tm=1024, tn=1024, name="mlp_down_dw")
    du2 = _mm(da, wts["w_up"], "nt", tm=1024, tn=1024, name="mlp_up_dx")
    g["w_up"] = _mm(u2, da, "tn", tm=1024, tn=1024, name="mlp_up_dw")
    dh1, g["norm_mlp_pre"] = _rms_bwd(h1, du2, small["norm_mlp_pre"], name="norm_mlp_pre_bwd", out_dtype=f32, add=dh2)
    dmix, g["norm_mix_post"] = _rms_bwd(mix, dh1, small["norm_mix_post"], name="norm_mix_post_bwd", out_dtype=bf16)
    dmerged = _mm(dmix, wts["w_o"], "nt", tm=1024, tn=1024, name="w_o_dx")
    g["w_o"] = _mm(merged, dmix, "tn", tm=1024, tn=1024, name="w_o_dw")
    dt0, dt1, dt2, dgl = _merge_bwd(proj, t0, t1, t2, dmerged, S)
    dy_sb = _mm(dt0, wts["w_sb_out"], "nt", tm=1024, tn=1024, name="sb_out_dx")
    g["w_sb_out"] = _mm(y_sb, dt0, "tn", tm=1024, tn=1024, name="sb_out_dw")
    dy_ssd = _mm(dt1, wts["w_ssd_out"], "nt", tm=1024, tn=1024, name="ssd_out_dx")
    g["w_ssd_out"] = _mm(yn, dt1, "tn", tm=1024, tn=1024, name="ssd_out_dw")
    dy_mem = _mm(dt2, wts["w_mem_out"], "nt", tm=1024, tn=1024, name="mem_out_dx")
    g["w_mem_out"] = _mm(y_mem, dt2, "tn", tm=1024, tn=1024, name="mem_out_dw")
    dmemq, dkv = _mem_bwd(proj, kv, dy_mem, S)
    g["w_mem_kv"] = _mm(mn, dkv, "tn", tm=1024, tn=1024, name="mem_kv_dw")
    dmn = _mm(dkv, wts["w_mem_kv"], "nt", tm=M, tn=1024, name="mem_kv_dx")
    _, g["norm_mem"] = _rms_bwd(mem, dmn, small["norm_mem"], name="norm_mem_bwd", out_dtype=bf16, tm=min(512, M))
    rides = rest_rides(g) if rest_rides else (None, None)
    dz, dxbc, ddt, g["ssd_norm"], dsk, dalog, ddtb, *lands_a = _ssd_bwd(
        dy_ssd, y_ssd, xbc, proj, pdt, hprev, dtb_p, alog_p, dsk_c, small["ssd_norm"], S, rides[0])
    g["d_skip"], g["a_log"], g["dt_bias"] = dsk[:, :SSD_HEADS], dalog[:, :SSD_HEADS], ddtb[:, :SSD_HEADS]
    dxbc_raw, dcw, g["conv_b"] = _conv_bwd(proj, xc, dxbc, small["conv_w"], S)
    g["conv_w"] = dcw[:CONV_K]
    dq, dk, dv, lands_b = _sb_bwd(proj, tot_lk, dy_sb, S, rides[1])
    g["rest_lands"] = lands_b + lands_a
    dproj = (dq, dk, dv, dxbc_raw, dgl, dmemq, dz)
    u_t = u.T
    g["w_main"] = [_mm(u_t, p, "nn", tm=512, tn=1024, name="in_proj_dw_%d" % i) for i, p in enumerate(dproj)]
    g["w_dt"] = _mm(u_t, ddt, "nn", tm=512, tn=128, name="in_proj_dt_dw")
    du_dt = _mm(ddt, wts["w_dt"], "nt", tm=1024, tn=1024, name="in_proj_dt_dx")
    du, g["w_in_lands"] = _mm_pieces_nt(dproj, wts["w_main"], du_dt, tm=512, tn=256, name="in_proj_dx",
                                        ride=w_in_ride(g) if w_in_ride else None)
    grad_x, g["norm_mix_pre"] = _rms_bwd(x, du, small["norm_mix_pre"], name="norm_pre_bwd", out_dtype=f32, add=dh1)
    return loss, grad_x, g


def _to_internal(w_in):
    sec = lambda r: w_in[:, r[0]:r[1]]
    w_main = jnp.concatenate([sec(R_QKV), sec(R_XBC), sec(R_GATE), sec(R_MEMQ), sec(R_Z)], axis=1)
    w_dt = jnp.pad(sec(R_DT), ((0, 0), (0, 128 - SSD_HEADS)))
    return w_main, w_dt


def _from_internal(pieces, g_dt):
    dq, dk, dv, dxbc, dgate, dmemq, dz = pieces
    return [dq, dk, dv, dz, dxbc, g_dt[:, :SSD_HEADS], dmemq, dgate]


def _w_in_slab(ordered, s, dtype):
    width = D_IN // N_SHARD
    lo, hi, off, parts = s * width, (s + 1) * width, 0, []
    for p in ordered:
        a, b = max(lo, off), min(hi, off + p.shape[1])
        if a < b:
            parts.append(p[:, a - off:b - off].astype(dtype))
        off += p.shape[1]
    return jnp.concatenate(parts, axis=1)


MESH = pl.DeviceIdType.MESH
ANY = pl.BlockSpec(memory_space=pl.ANY)


def _place():
    x, y, c = lax.axis_index("x"), lax.axis_index("y"), lax.axis_index("c")
    return (x, y, c), [(1 - x, y, c), (x, 1 - y, c), (1 - x, 1 - y, c)]


def _exchange_copy(mode, ins, lands, send, recv, a, k, me, peers, arriving):
    p = peers[k]
    theirs = 2 * p[0] + p[1]
    if mode == "gather":
        src, dst = ins[a], lands[a].at[theirs if arriving else me]
    else:
        src, dst = ins[a].at[theirs], lands[a].at[k]
    return pltpu.make_async_remote_copy(src_ref=src, dst_ref=dst, send_sem=send.at[a * 3 + k],
                                        recv_sem=recv.at[a * 3 + k], device_id=p, device_id_type=MESH)


class _Ride:
    def __init__(self, srcs, mode):
        self.srcs, self.mode, self.n = list(srcs), mode, len(srcs)
        n = self.n
        self.in_specs, self.out_specs = [ANY] * n, [ANY] * n
        self.out_shape = [
            jax.ShapeDtypeStruct((N_SHARD,) + s.shape if mode == "gather" else (3,) + s.shape[1:], s.dtype)
            for s in self.srcs]
        self.scratch = [pltpu.SemaphoreType.DMA((3 * n,)), pltpu.SemaphoreType.DMA((3 * n,)),
                        pltpu.SemaphoreType.DMA((n,))]

    def _own(self, ins, lnd, sems):
        if self.mode != "gather":
            return []
        me = 2 * lax.axis_index("x") + lax.axis_index("y")
        return [pltpu.make_async_copy(ins[a], lnd[a].at[me], sems[2].at[a]) for a in range(self.n)]

    def _far(self, ins, lnd, sems, arriving):
        (x, y, c), peers = _place()
        return [_exchange_copy(self.mode, ins, lnd, sems[0], sems[1], a, k, 2 * x + y, peers, arriving)
                for a in range(self.n) for k in range(3)]

    def start(self, ins, lnd, sems):
        for cp in self._own(ins, lnd, sems) + self._far(ins, lnd, sems, False):
            cp.start()

    def finish(self, ins, lnd, sems):
        for cp in self._far(ins, lnd, sems, True):
            cp.wait_recv()
        for cp in self._far(ins, lnd, sems, False):
            cp.wait_send()
        for cp in self._own(ins, lnd, sems):
            cp.wait()


def _gather_two_level(halves, name):
    def body(in_0, in_1, lnd_0, lnd_1, send, recv, loc):
        ins, lnd = (in_0, in_1), (lnd_0, lnd_1)
        (x, y, c), peers = _place()
        me = 2 * x + y
        sibling = (x, y, 1 - c)
        own = [pltpu.make_async_copy(ins[h], lnd[h].at[me], loc.at[h]) for h in range(2)]
        for cp in own:
            cp.start()

        def copy(h, j, slot, to, src=None):
            return pltpu.make_async_remote_copy(
                src_ref=lnd[h].at[slot] if src is None else src, dst_ref=lnd[h].at[slot],
                send_sem=send.at[j], recv_sem=recv.at[j], device_id=to, device_id_type=MESH)

        for mine in range(2):
            @pl.when(c == mine)
            def _(mine=mine):
                far = [copy(mine, k, me, peers[k], src=ins[mine]) for k in range(3)]
                for cp in far:
                    cp.start()
                passed = []
                for k, p in enumerate(peers):
                    theirs = 2 * p[0] + p[1]
                    copy(mine, k, theirs, p).wait_recv()
                    passed.append(copy(mine, 3 + k, theirs, sibling))
                    passed[-1].start()
                for k, p in enumerate(peers):
                    copy(1 - mine, 3 + k, 2 * p[0] + p[1], sibling).wait_recv()
                for cp in far + passed:
                    cp.wait_send()

        for cp in own:
            cp.wait()

    return pl.pallas_call(
        body, name=name, in_specs=[ANY] * 2, out_specs=[ANY] * 2,
        out_shape=[jax.ShapeDtypeStruct((N_SHARD,) + h.shape, h.dtype) for h in halves],
        scratch_shapes=[pltpu.SemaphoreType.DMA((6,)), pltpu.SemaphoreType.DMA((6,)), pltpu.SemaphoreType.DMA((2,))],
    )(*halves)


def _exchange_packets(packet):
    def body(pk, pk_out, send, recv, loc):
        x, y, c = lax.axis_index("x"), lax.axis_index("y"), lax.axis_index("c")
        lin = 4 * x + 2 * y + c
        own = pltpu.make_async_copy(pk, pk_out.at[lin], loc.at[0])
        own.start()

        def pk_copy(m, slot):
            dev = (x ^ ((m >> 2) & 1), y ^ ((m >> 1) & 1), c ^ (m & 1))
            return pltpu.make_async_remote_copy(
                src_ref=pk, dst_ref=pk_out.at[slot], send_sem=send.at[m - 1], recv_sem=recv.at[m - 1],
                device_id=dev, device_id_type=MESH)

        sent = [pk_copy(m, lin) for m in range(1, N_DEV)]
        for cp in sent:
            cp.start()
        for m in range(1, N_DEV):
            pk_copy(m, lin ^ m).wait_recv()
        for cp in sent:
            cp.wait_send()
        own.wait()

    return pl.pallas_call(
        body, name="exchange_packets", in_specs=[ANY], out_specs=ANY,
        out_shape=jax.ShapeDtypeStruct((N_DEV,) + packet.shape, packet.dtype),
        scratch_shapes=[pltpu.SemaphoreType.DMA((N_DEV - 1,)), pltpu.SemaphoreType.DMA((N_DEV - 1,)),
                        pltpu.SemaphoreType.DMA((1,))],
    )(packet)


def _swap_sibling(parts, name):
    n = len(parts)

    def body(*refs):
        ins, outs = refs[:n], refs[n:2 * n]
        send, recv = refs[2 * n:]
        x, y, c = lax.axis_index("x"), lax.axis_index("y"), lax.axis_index("c")
        cps = [pltpu.make_async_remote_copy(
            src_ref=ins[a], dst_ref=outs[a], send_sem=send.at[a], recv_sem=recv.at[a],
            device_id=(x, y, 1 - c), device_id_type=MESH) for a in range(n)]
        for cp in cps:
            cp.start()
        for cp in cps:
            cp.wait_recv()
        for cp in cps:
            cp.wait_send()

    return pl.pallas_call(
        body, name=name,
        in_specs=[ANY] * n, out_specs=[ANY] * n,
        out_shape=[jax.ShapeDtypeStruct(p.shape, p.dtype) for p in parts],
        scratch_shapes=[pltpu.SemaphoreType.DMA((n,)), pltpu.SemaphoreType.DMA((n,))],
    )(*parts)


BLOCK_ELEMS = 256 * 1024


def _row_tile(R, C):
    tr = max(8, (BLOCK_ELEMS // C) // 8 * 8)
    while R % tr:
        tr -= 8
    return min(tr, R)


def _sum_parts(own, stack, name, out_dtype=f32):
    k = stack.shape[0]
    R, C = stack.shape[1:]
    tr = _row_tile(R, C)

    def body(*refs):
        o_ref = refs[-1]
        acc = refs[0][...].astype(f32)
        for r in refs[1:-1]:
            acc = acc + r[...].astype(f32)
        o_ref[...] = acc.astype(out_dtype)

    row = pl.BlockSpec((tr, C), lambda i: (i, 0))
    specs = ([row] if own is not None else []) + [
        pl.BlockSpec((None, tr, C), functools.partial(lambda i, j: (j, i, 0), j=j)) for j in range(k)]
    args = ([own] if own is not None else []) + [stack] * k
    return pl.pallas_call(
        body, name=name, grid=(R // tr,), in_specs=specs, out_specs=row,
        out_shape=jax.ShapeDtypeStruct((R, C), out_dtype), compiler_params=_params(("parallel",)),
    )(*args)


def _adamw(w, m, v, g_parts, name):
    R, C = w.shape
    tr = _row_tile(R, C)
    n_g = len(g_parts)

    def body(w_ref, m_ref, v_ref, *rest):
        g = rest[0][...]
        for r in rest[1:n_g]:
            g = g + r[...]
        g_ref, d_ref, nm_ref, nv_ref = rest[n_g:]
        nm = ADAM_B1 * m_ref[...] + (1.0 - ADAM_B1) * g
        nv = ADAM_B2 * v_ref[...] + (1.0 - ADAM_B2) * jnp.square(g)
        m_hat = nm / (1.0 - ADAM_B1 ** ADAM_STEP)
        v_hat = nv / (1.0 - ADAM_B2 ** ADAM_STEP)
        g_ref[...] = g
        d_ref[...] = -ADAM_LR * (m_hat / (jnp.sqrt(v_hat) + ADAM_EPS) + ADAM_WD * w_ref[...])
        nm_ref[...] = nm
        nv_ref[...] = nv

    row = pl.BlockSpec((tr, C), lambda i: (i, 0))
    return pl.pallas_call(
        body, name=name, grid=(R // tr,), in_specs=[row] * (3 + n_g), out_specs=[row] * 4,
        out_shape=[jax.ShapeDtypeStruct((R, C), f32)] * 4, compiler_params=_params(("parallel",)),
    )(w, m, v, *g_parts)


BIG = ("w_in", "w_mem_kv", "w_sb_out", "w_ssd_out", "w_mem_out", "w_o", "w_up", "w_down")
LATE = ("w_sb_out", "w_ssd_out", "w_mem_out", "w_o", "w_up", "w_down")
REST = BIG[1:]
COL_SHARDED = ("w_in", "w_mem_kv", "w_up")
SMALL = ("norm_mix_pre", "conv_w", "conv_b", "dt_bias", "a_log", "d_skip", "ssd_norm", "norm_mem",
         "norm_mix_post", "norm_mlp_pre", "norm_mlp_post")
WEIGHTS = ("norm_mix_pre", "w_in", "conv_w", "conv_b", "dt_bias", "a_log", "d_skip", "ssd_norm", "norm_mem",
           "w_mem_kv", "w_sb_out", "w_ssd_out", "w_mem_out", "w_o", "norm_mix_post", "norm_mlp_pre", "w_up",
           "w_down", "norm_mlp_post")
PK_ROWS = 184


def _pack(vecs):
    flat = jnp.concatenate([v.reshape(-1) for v in vecs])
    return jnp.pad(flat, (0, PK_ROWS * 128 - flat.shape[0])).reshape(PK_ROWS, 128)


def _unpack(pk, shapes):
    flat = pk.reshape(-1)
    out, off = [], 0
    for s in shapes:
        n = 1
        for d in s:
            n *= d
        out.append(flat[off:off + n].reshape(s))
        off += n
    return out


def _full_from_slabs(name, slabs):
    if name in COL_SHARDED:
        return slabs.transpose(1, 0, 2).reshape(slabs.shape[1], -1)
    return slabs.reshape(-1, slabs.shape[2])


def _slabs_from_full(name, g):
    if name in COL_SHARDED:
        return g.reshape(g.shape[0], N_SHARD, -1).transpose(1, 0, 2)
    return g.reshape(N_SHARD, -1, g.shape[1])


def kernel(x, mem, norm_mix_pre, w_in, conv_w, conv_b, dt_bias, a_log, d_skip, ssd_norm, norm_mem, w_mem_kv, w_sb_out, w_ssd_out, w_mem_out, w_o, norm_mix_post, norm_mlp_pre, w_up, w_down, norm_mlp_post, loss_target, m_norm_mix_pre, m_w_in, m_conv_w, m_conv_b, m_dt_bias, m_a_log, m_d_skip, m_ssd_norm, m_norm_mem, m_w_mem_kv, m_w_sb_out, m_w_ssd_out, m_w_mem_out, m_w_o, m_norm_mix_post, m_norm_mlp_pre, m_w_up, m_w_down, m_norm_mlp_post, v_norm_mix_pre, v_w_in, v_conv_w, v_conv_b, v_dt_bias, v_a_log, v_d_skip, v_ssd_norm, v_norm_mem, v_w_mem_kv, v_w_sb_out, v_w_ssd_out, v_w_mem_out, v_w_o, v_norm_mix_post, v_norm_mlp_pre, v_w_up, v_w_down, v_norm_mlp_post):
    env = dict(locals())
    w = {n: env[n] for n in WEIGHTS}
    mo = {n: env["m_" + n] for n in WEIGHTS}
    vo = {n: env["v_" + n] for n in WEIGHTS}
    shard = 2 * lax.axis_index("x") + lax.axis_index("y")

    shard_in = w["w_in"][0].astype(bf16)
    first = _gather_two_level((shard_in[:D // 2], shard_in[D // 2:]), "gather_first")
    w_main, w_dt = _to_internal(_full_from_slabs("w_in", jnp.concatenate(first, axis=1)))
    wts = dict(w_main=w_main, w_dt=w_dt)
    ride_names = (LATE[:4], LATE[4:5], LATE[5:], ("w_mem_kv",))
    late_rides = tuple(_Ride([w[n][0].astype(bf16) for n in names] + ([w["conv_w"][0]] if i == 0 else []), "gather")
                       for i, names in enumerate(ride_names))

    def late_weights(i, lands):
        full = {n: _full_from_slabs(n, s) for n, s in zip(ride_names[i], lands)}
        if i == 0:
            full["conv_w"] = lands[-1].transpose(1, 0, 2).reshape(CONV_K, CONV_DIM)
        return full

    def rest_rides(g):
        slabs = [_slabs_from_full(n, g[n]).astype(bf16) for n in REST]
        return _Ride(slabs[5:], "scatter"), _Ride(slabs[:5], "scatter")

    core = lax.axis_index("c")
    half = D // 2

    def w_in_ride(g):
        ordered = _from_internal(g["w_main"], g["w_dt"])
        stack = jnp.stack([_w_in_slab(ordered, s, bf16) for s in range(N_SHARD)])
        keep = lax.dynamic_slice_in_dim(stack, core * half, half, axis=1)
        away = lax.dynamic_slice_in_dim(stack, (1 - core) * half, half, axis=1)
        (got,) = _swap_sibling([away], "w_in_halves_out")
        wide = lambda a: a.reshape(N_SHARD * half, -1)
        chip = _sum_parts(wide(keep), wide(got)[None], "sum_cores_w_in", bf16).reshape(N_SHARD, half, -1)
        own = lax.switch(shard, [functools.partial(_w_in_slab, ordered, s, f32) for s in range(N_SHARD)])
        own = lax.dynamic_slice_in_dim(own, core * half, half, axis=0)
        g["w_in_own"] = _sum_parts(own, lax.dynamic_index_in_dim(got, shard, 0, keepdims=True), "sum_cores_w_in_own")
        return _Ride([chip], "scatter")

    small = {n: w[n] for n in SMALL if n != "conv_w"}
    loss, grad_x, g = _local_step(x[0], mem[0], loss_target[0], wts, late_rides, late_weights, small,
                                  rest_rides, w_in_ride)
    out_g, out_d, out_m, out_v = {}, {}, {}, {}

    def apply(n, g_parts):
        res = _adamw(w[n][0], mo[n][0], vo[n][0], g_parts, name="adamw_" + n)
        out_g[n], out_d[n], out_m[n], out_v[n] = [r[None] for r in res]

    mine = _sum_parts(g["w_in_own"], g["w_in_lands"][0], name="sum_chips_w_in")
    (theirs,) = _swap_sibling([mine], "w_in_halves_back")
    g_w_in = lax.dynamic_update_slice_in_dim(jnp.zeros((D, D_IN // N_SHARD), f32), mine, core * half, axis=0)
    apply("w_in", [lax.dynamic_update_slice_in_dim(g_w_in, theirs, (1 - core) * half, axis=0)])

    packets = _exchange_packets(_pack([g[n] for n in SMALL] + [loss[:, :1]]))
    partial = []
    for n, r in zip(REST, g["rest_lands"]):
        own = lax.dynamic_index_in_dim(_slabs_from_full(n, g[n]), shard, 0, keepdims=False)
        partial.append(_sum_parts(own, r, name="sum_chips_" + n))
    other = _swap_sibling(partial, "swap_sibling")

    for n, p, q in zip(REST, partial, other):
        apply(n, [p, q])
    tot = _sum_parts(None, packets, name="sum_packets")
    shapes = [g[n].shape for n in SMALL] + [(1, 1)]
    sm = dict(zip(SMALL + ("loss",), _unpack(tot, shapes)))
    sm["conv_w"] = lax.dynamic_slice_in_dim(sm["conv_w"], shard * (CONV_DIM // N_SHARD), CONV_DIM // N_SHARD, axis=1)
    own_small = lambda d: _pack([d[n].reshape(sm[n].shape) for n in SMALL])
    res = _adamw(own_small(w), own_small(mo), own_small(vo), [own_small(sm)], name="adamw_small")
    own_shapes = [sm[n].shape for n in SMALL]
    for store, r in zip((out_g, out_d, out_m, out_v), res):
        for n, val in zip(SMALL, _unpack(r, own_shapes)):
            store[n] = val.reshape(w[n].shape)

    outs = [sm["loss"].reshape(()), grad_x[None]]
    for store in (out_g, out_d, out_m, out_v):
        outs += [store[n] for n in WEIGHTS]
    return tuple(outs)
```

```python
import functools

import jax
import jax.numpy as jnp
from jax import lax
from jax.experimental import pallas as pl
from jax.experimental.pallas import tpu as pltpu

f32 = jnp.float32
bf16 = jnp.bfloat16

D = 1024
EPS = 1e-6
SB_HD = 64
SSD_INNER = 2048
SSD_HEADS = 32
SSD_GROUPS = 4
SSD_N = 128
SSD_L = 128
CONV_K = 4
CONV_DIM = 3072
MEM_HEADS = 4
MEM_HD = 256
D_FF = 4096
D_IN = 12320
N_SHARD = 4
N_DEV = 8

P_QKV, P_XBC, P_GATE, P_MEMQ, P_Z, P_DT, P_TOT = 0, 3072, 6144, 9216, 10240, 12288, 12416
R_QKV, R_Z, R_XBC, R_DT, R_MEMQ, R_GATE = (0, 3072), (3072, 5120), (5120, 8192), (8192, 8224), (8224, 9248), (9248, 12320)

ADAM_LR = 0.001
ADAM_B1 = 0.9
ADAM_B2 = 0.999
ADAM_EPS = 1e-08
ADAM_WD = 0.01
ADAM_STEP = 10

VMEM_LIMIT = 56 * 1024 * 1024

NN = (((1,), (0,)), ((), ()))
NT = (((1,), (1,)), ((), ()))
TN = (((0,), (0,)), ((), ()))


def _dot(a, b, dims=NN):
    return lax.dot_general(a, b, dims, preferred_element_type=f32)


def _params(sem=None):
    return pltpu.CompilerParams(dimension_semantics=sem, vmem_limit_bytes=VMEM_LIMIT)


def _sigmoid(x):
    return 1.0 / (1.0 + jnp.exp(-x))


def _split2(x):
    hi = x.astype(bf16)
    lo = (x - hi.astype(f32)).astype(bf16)
    return hi, lo


def _split3(x):
    hi = x.astype(bf16)
    r = x - hi.astype(f32)
    mid = r.astype(bf16)
    lo = (r - mid.astype(f32)).astype(bf16)
    return hi, mid, lo


def _mm(a, b, mode, *, tm, tn, name, out_dtypes=(f32,), epi=None, extras=(), ride=None):
    M = a.shape[1] if mode == "tn" else a.shape[0]
    N = b.shape[0] if mode == "nt" else b.shape[1]
    tm, tn = min(tm, M), min(tn, N)
    if mode == "nn":
        (M, K), N = a.shape, b.shape[1]
        a_spec = pl.BlockSpec((tm, K), lambda i, j: (i, 0))
        b_spec = pl.BlockSpec((K, tn), lambda i, j: (0, j))
        dims = NN
    elif mode == "nt":
        (M, K), N = a.shape, b.shape[0]
        a_spec = pl.BlockSpec((tm, K), lambda i, j: (i, 0))
        b_spec = pl.BlockSpec((tn, K), lambda i, j: (j, 0))
        dims = NT
    else:
        (K, M), N = a.shape, b.shape[1]
        a_spec = pl.BlockSpec((K, tm), lambda i, j: (0, i))
        b_spec = pl.BlockSpec((K, tn), lambda i, j: (0, j))
        dims = TN
    assert M % tm == 0 and N % tn == 0, (name, M, N, tm, tn)
    n_ex, n_out = len(extras), len(out_dtypes)
    n_r = ride.n if ride else 0
    o_spec = pl.BlockSpec((tm, tn), lambda i, j: (i, j))
    grid = (M // tm, N // tn)

    def body(a_ref, b_ref, *rest):
        r_ins = rest[n_ex:n_ex + n_r]
        outs = rest[n_ex + n_r:n_ex + n_r + n_out]
        r_lnd, r_sems = rest[n_ex + n_r + n_out:n_ex + 2 * n_r + n_out], rest[n_ex + 2 * n_r + n_out:]
        i, j = pl.program_id(0), pl.program_id(1)
        if ride:
            pl.when((i == 0) & (j == 0))(lambda: ride.start(r_ins, r_lnd, r_sems))
        acc = _dot(a_ref[...].astype(bf16), b_ref[...].astype(bf16), dims)
        res = (acc,) if epi is None else epi(acc, *[e[...] for e in rest[:n_ex]])
        for o_ref, r in zip(outs, res):
            o_ref[...] = r.astype(o_ref.dtype)
        if ride:
            pl.when((i == grid[0] - 1) & (j == grid[1] - 1))(lambda: ride.finish(r_ins, r_lnd, r_sems))

    out = pl.pallas_call(
        body, name=name, grid=grid,
        in_specs=[a_spec, b_spec] + [o_spec] * n_ex + (ride.in_specs if ride else []),
        out_specs=[o_spec] * n_out + (ride.out_specs if ride else []),
        out_shape=[jax.ShapeDtypeStruct((M, N), dt) for dt in out_dtypes] + (ride.out_shape if ride else []),
        scratch_shapes=ride.scratch if ride else [],
        compiler_params=_params(("arbitrary", "arbitrary") if ride else ("parallel", "parallel")),
    )(a, b, *extras, *(ride.srcs if ride else []))
    if ride:
        return (out[0] if n_out == 1 else out[:n_out]), list(out[n_out:])
    return out[0] if n_out == 1 else out


def _mm_pieces_nt(pieces, b, add, *, tm, tn, name, ride):
    M, N = pieces[0].shape[0], b.shape[0]
    n_p, n_r = len(pieces), (ride.n if ride else 0)
    o_spec = pl.BlockSpec((tm, tn), lambda i, j: (i, j))
    grid = (M // tm, N // tn)

    def body(*refs):
        b_ref, add_ref = refs[n_p:n_p + 2]
        r_ins, o_ref = refs[n_p + 2:n_p + 2 + n_r], refs[n_p + 2 + n_r]
        r_lnd, r_sems = refs[n_p + 3 + n_r:n_p + 3 + 2 * n_r], refs[n_p + 3 + 2 * n_r:]
        i, j = pl.program_id(0), pl.program_id(1)
        if ride:
            pl.when((i == 0) & (j == 0))(lambda: ride.start(r_ins, r_lnd, r_sems))
        acc, off = add_ref[...], 0
        for r in refs[:n_p]:
            acc = acc + _dot(r[...], b_ref[:, off:off + r.shape[1]], NT)
            off += r.shape[1]
        o_ref[...] = acc
        if ride:
            pl.when((i == grid[0] - 1) & (j == grid[1] - 1))(lambda: ride.finish(r_ins, r_lnd, r_sems))

    out = pl.pallas_call(
        body, name=name, grid=grid,
        in_specs=[pl.BlockSpec((tm, p.shape[1]), lambda i, j: (i, 0)) for p in pieces]
        + [pl.BlockSpec((tn, b.shape[1]), lambda i, j: (j, 0)), o_spec] + (ride.in_specs if ride else []),
        out_specs=[o_spec] + (ride.out_specs if ride else []),
        out_shape=[jax.ShapeDtypeStruct((M, N), f32)] + (ride.out_shape if ride else []),
        scratch_shapes=ride.scratch if ride else [],
        compiler_params=_params(("arbitrary", "arbitrary")),
    )(*pieces, b, add, *(ride.srcs if ride else []))
    return out[0], list(out[1:])


def _rms_fwd(x, g, *, name, out_dtype, residual=None, tm=512):
    S, C = x.shape
    tm = min(tm, S)
    has_res = residual is not None

    def body(x_ref, g_ref, *rest):
        xv = x_ref[...]
        r = lax.rsqrt(jnp.mean(xv * xv, axis=1, keepdims=True) + EPS)
        y = xv * r * g_ref[...]
        if has_res:
            y = y + rest[0][...]
        rest[-1][...] = y.astype(out_dtype)

    row = pl.BlockSpec((tm, C), lambda i: (i, 0))
    vec = pl.BlockSpec((1, C), lambda i: (0, 0))
    args = (x, g) + ((residual,) if has_res else ())
    return pl.pallas_call(
        body, name=name, grid=(S // tm,),
        in_specs=[row, vec] + ([row] if has_res else []),
        out_specs=row, out_shape=jax.ShapeDtypeStruct((S, C), out_dtype),
        compiler_params=_params(("parallel",)),
    )(*args)


def _rms_bwd(x, dy, g, *, name, out_dtype, add=None, tm=512):
    S, C = x.shape
    tm = min(tm, S)
    has_add = add is not None

    def body(x_ref, dy_ref, g_ref, *rest):
        dx_ref, dg_ref = rest[-2], rest[-1]
        xv = x_ref[...]
        dyv = dy_ref[...].astype(f32)
        r = lax.rsqrt(jnp.mean(xv * xv, axis=1, keepdims=True) + EPS)
        xh = xv * r
        dxh = dyv * g_ref[...]
        dx = r * (dxh - xh * jnp.mean(dxh * xh, axis=1, keepdims=True))
        if has_add:
            dx = dx + rest[0][...]
        dx_ref[...] = dx.astype(out_dtype)

        @pl.when(pl.program_id(0) == 0)
        def _():
            dg_ref[...] = jnp.zeros_like(dg_ref)

        dg_ref[...] += jnp.sum(dyv * xh, axis=0, keepdims=True)

    row = pl.BlockSpec((tm, C), lambda i: (i, 0))
    vec = pl.BlockSpec((1, C), lambda i: (0, 0))
    args = (x, dy, g) + ((add,) if has_add else ())
    return pl.pallas_call(
        body, name=name, grid=(S // tm,),
        in_specs=[row, row, vec] + ([row] if has_add else []),
        out_specs=[row, vec],
        out_shape=[jax.ShapeDtypeStruct((S, C), out_dtype), jax.ShapeDtypeStruct((1, C), f32)],
        compiler_params=_params(("arbitrary",)),
    )(*args)


SB_T = 128
SB_SPENT = -120.0
SB_QB = 4
SB_TAIL = 3
SB_GROUPS = (4, 2, 1)
SB_GROUPS_BWD = (4, 2, 1)


def _sb_masks():
    lane = lax.broadcasted_iota(jnp.int32, (1, 128), 1)
    m_a = (lane < SB_HD).astype(f32)
    return m_a, 1.0 - m_a


def _chunks(a, n):
    return [a[:, u * SB_T:(u + 1) * SB_T] for u in range(n)]


def _cat(parts, axis):
    return parts[0] if len(parts) == 1 else jnp.concatenate(parts, axis=axis)


def _mask_last(a, n, mask):
    if mask is None:
        return a
    parts = _chunks(a, n)
    return _cat(parts[:-1] + [jnp.where(mask, parts[-1], 0.0)], 1)


def _sb_logits(z, n, mask):
    l1p = jnp.log(1.0 + jnp.exp(-jnp.abs(z)))
    lb = jnp.minimum(z, 0.0) - l1p
    return lb, _mask_last(lb - z, n, mask)


def _by_count(i, most, fn):
    return lax.switch(jnp.minimum(i, most - 1), [functools.partial(fn, n) for n in range(1, most + 1)])


def _chunk_matmul(parts_list, u_mat):
    out = _dot(_cat(parts_list, 0), u_mat)
    return [out[u * SB_T:(u + 1) * SB_T] for u in range(len(parts_list))]


def _chunk_cumsum(lk, n, u_mat):
    hi = lk.astype(bf16)
    lo = (lk - hi.astype(f32)).astype(bf16)
    out = _chunk_matmul(_chunks(hi, n) + _chunks(lo, n), u_mat)
    return [out[u] + out[n + u] for u in range(n)]


def _sb_fwd(proj, S, ride=None):
    nq = S // SB_T
    n_pairs = D // 128
    scale = SB_HD ** -0.5
    n_r = ride.n if ride else 0

    def body(q_ref, k_ref, v_ref, *rest):
        o_ref, t_ref = rest[n_r:n_r + 2]
        step_i = pl.program_id(1)
        if ride:
            pl.when((pl.program_id(0) == 0) & (step_i == 0))(
                lambda: ride.start(rest[:n_r], rest[n_r + 2:2 * n_r + 2], rest[2 * n_r + 2:]))
        m_a, m_b = _sb_masks()
        r_i = lax.broadcasted_iota(jnp.int32, (SB_T, SB_T), 0)
        c_i = lax.broadcasted_iota(jnp.int32, (SB_T, SB_T), 1)
        u_mat = (r_i > c_i).astype(bf16)
        causal = c_i < r_i
        q_all = q_ref[...] * scale
        q_hs = [((q * m_a).astype(bf16), (q * m_b).astype(bf16))
                for q in (q_all[b * SB_T:(b + 1) * SB_T] for b in range(SB_QB))]

        def group(q_h, j_lo, n, carry, mask):
            acc, c_a, c_b = carry
            rows = pl.ds(pl.multiple_of(j_lo * SB_T, SB_T), n * SB_T)
            k = k_ref[rows, :].astype(bf16)
            v = v_ref[rows, :]
            zs = [_dot(q_b, k, NT) for q_b in q_h]
            lbk = [_sb_logits(z, n, mask) for z in zs]
            parts = [_chunk_cumsum(lk, n, u_mat) for _, lk in lbk]
            ws, cs = [], []
            for (lb, lk), part, c in zip(lbk, parts, (c_a, c_b)):
                lb_c, lk_c = _chunks(lb, n), _chunks(lk, n)
                w_c = [None] * n
                for u in reversed(range(n)):
                    w_c[u] = jnp.exp(lb_c[u] + c + part[u])
                    c = c + jnp.sum(lk_c[u], axis=1, keepdims=True)
                ws.append(_mask_last(_cat(w_c, 1), n, mask).astype(bf16))
                cs.append(c)
            for w, m in zip(ws, (m_a, m_b)):
                acc = acc + _dot(w, (v * m).astype(bf16))
            return acc, cs[0], cs[1]

        zero_c = jnp.zeros((SB_T, 1), f32)
        init = (jnp.zeros((SB_T, 128), f32), zero_c, zero_c)
        blocks = [(step_i * SB_QB + b, q_hs[b]) for b in range(SB_QB)]

        def whole_tails():
            return tuple(group(q_h, i - SB_TAIL + 1, SB_TAIL, init, causal) for i, q_h in blocks)

        def short_tails():
            return tuple(_by_count(i, SB_TAIL, functools.partial(
                lambda n, i, q_h: group(q_h, i - n + 1, n, init, causal), i=i, q_h=q_h)) for i, q_h in blocks)

        carries = lax.cond(step_i * SB_QB >= SB_TAIL - 1, whole_tails, short_tails)

        def spent(cr):
            return (jnp.max(jnp.maximum(cr[1], cr[2])) < SB_SPENT).astype(jnp.int32)

        lane = lax.broadcasted_iota(jnp.int32, (1, 128), 1)
        for b, ((i, q_h), carry) in enumerate(zip(blocks, carries)):
            state = (i - jnp.minimum(i, SB_TAIL - 1), spent(carry), carry)
            for n in SB_GROUPS:
                def step(st, n=n, q_h=q_h):
                    left, _, cr = st
                    cr = group(q_h, left - n, n, cr, None)
                    return left - n, spent(cr), cr

                state = lax.while_loop(lambda st, n=n: (st[0] >= n) & (st[1] == 0), step, state)
            left, _, carry = state
            rows = slice(b * SB_T, (b + 1) * SB_T)
            o_ref[rows, :] = carry[0]
            t_ref[rows, :] = (jnp.where(lane == 0, carry[1], 0.0) + jnp.where(lane == SB_HD, carry[2], 0.0)
                              + jnp.where(lane == 1, left.astype(f32), 0.0))
        if ride:
            pl.when((pl.program_id(0) == n_pairs - 1) & (step_i == nq // SB_QB - 1))(
                lambda: ride.finish(rest[:n_r], rest[n_r + 2:2 * n_r + 2], rest[2 * n_r + 2:]))

    qs = pl.BlockSpec((SB_QB * SB_T, 128), lambda h, i: (i, h))
    out = pl.pallas_call(
        body, name="sb_fwd", grid=(n_pairs, nq // SB_QB),
        in_specs=[qs,
                  pl.BlockSpec((S, 128), lambda h, i: (0, n_pairs + h)),
                  pl.BlockSpec((S, 128), lambda h, i: (0, 2 * n_pairs + h))] + (ride.in_specs if ride else []),
        out_specs=[qs, qs] + (ride.out_specs if ride else []),
        out_shape=[jax.ShapeDtypeStruct((S, D), f32)] * 2 + (ride.out_shape if ride else []),
        scratch_shapes=ride.scratch if ride else [],
        compiler_params=_params(("arbitrary", "arbitrary")),
    )(proj, proj, proj, *(ride.srcs if ride else []))
    return out[0], out[1], list(out[2:])


def _sb_bwd(proj, tot_lk, do, S, ride=None):
    nq = S // SB_T
    n_pairs = D // 128
    scale = SB_HD ** -0.5
    n_r = ride.n if ride else 0

    def body(q_ref, k_ref, v_ref, t_ref, do_ref, *rest):
        dq_ref, dk_ref, dv_ref = rest[n_r:n_r + 3]
        dk_acc, dv_acc = rest[2 * n_r + 3:2 * n_r + 5]
        r_ins, r_lnd, r_sems = rest[:n_r], rest[n_r + 3:2 * n_r + 3], rest[2 * n_r + 5:]
        step_i = pl.program_id(1)
        if ride:
            pl.when((pl.program_id(0) == 0) & (step_i == 0))(lambda: ride.start(r_ins, r_lnd, r_sems))
        m_a, m_b = _sb_masks()
        r_i = lax.broadcasted_iota(jnp.int32, (SB_T, SB_T), 0)
        c_i = lax.broadcasted_iota(jnp.int32, (SB_T, SB_T), 1)
        u_inc = (r_i <= c_i).astype(bf16)
        u_exc = (r_i < c_i).astype(bf16)
        causal = c_i < r_i

        @pl.when(step_i == 0)
        def _():
            dk_acc[...] = jnp.zeros_like(dk_acc)
            dv_acc[...] = jnp.zeros_like(dv_acc)

        lane = lax.broadcasted_iota(jnp.int32, (1, 128), 1)
        blocks = []
        for b in range(SB_QB):
            rows_b = slice(b * SB_T, (b + 1) * SB_T)
            i = step_i * SB_QB + b
            q = q_ref[rows_b, :] * scale
            dov = do_ref[rows_b, :]
            tv = t_ref[rows_b, :]
            heads = []
            for m, first in ((m_a, 0), (m_b, SB_HD)):
                tot = jnp.sum(jnp.where(lane == first, tv, 0.0), axis=1, keepdims=True)
                heads.append(((q * m).astype(bf16), (dov * m).astype(bf16), tot, m))
            lowest = jnp.clip(jnp.max(jnp.where(lane == 1, tv, 0.0)).astype(jnp.int32), 0, i)
            blocks.append((i, heads, lowest))

        def group(heads, j_lo, n, carry, mask):
            dq_acc, cp_a, cp_b, ce_a, ce_b = carry
            rows = pl.ds(pl.multiple_of(j_lo * SB_T, SB_T), n * SB_T)
            k_f = k_ref[rows, :]
            k = k_f.astype(bf16)
            v = v_ref[rows, :].astype(bf16)
            zs = [_dot(h[0], k, NT) for h in heads]
            dws = [_dot(h[1], v, NT) for h in heads]
            lbk = [_sb_logits(z, n, mask) for z in zs]
            parts = [_chunk_cumsum(lk, n, u_inc) for _, lk in lbk]
            ws, es, cps = [], [], []
            for (lb, lk), part, dw, h, cp in zip(lbk, parts, dws, heads, (cp_a, cp_b)):
                lb_c, lk_c = _chunks(lb, n), _chunks(lk, n)
                w_c = []
                for u in range(n):
                    w_c.append(jnp.exp(lb_c[u] + (h[2] - cp) - part[u]))
                    cp = cp + jnp.sum(lk_c[u], axis=1, keepdims=True)
                w = _mask_last(_cat(w_c, 1), n, mask)
                ws.append(w)
                es.append(dw * w)
                cps.append(cp)
            e_parts = [_chunk_matmul(_chunks(e.astype(bf16), n), u_exc) for e in es]
            dzs, ces = [], []
            for (lb, _), e, e_part, ce in zip(lbk, es, e_parts, (ce_a, ce_b)):
                e_c = _chunks(e, n)
                big_c = []
                for u in range(n):
                    big_c.append(ce + e_part[u])
                    ce = ce + jnp.sum(e_c[u], axis=1, keepdims=True)
                sig = jnp.exp(lb)
                dz = _mask_last(e * (1.0 - sig) - _cat(big_c, 1) * sig, n, mask)
                dzs.append(dz.astype(bf16))
                ces.append(ce)
            dk_t = jnp.zeros((n * SB_T, 128), f32)
            dv_t = jnp.zeros((n * SB_T, 128), f32)
            for dz_b, w, h in zip(dzs, ws, heads):
                dq_acc = dq_acc + _dot(dz_b, (k_f * h[3]).astype(bf16))
                dk_t = dk_t + _dot(dz_b, h[0], TN)
                dv_t = dv_t + _dot(w.astype(bf16), h[1], TN)
            dk_acc[rows, :] += dk_t
            dv_acc[rows, :] += dv_t
            return dq_acc, cps[0], cps[1], ces[0], ces[1]

        zc = jnp.zeros((SB_T, 1), f32)
        carries = []
        for i, heads, lowest in blocks:
            carry = (jnp.zeros((SB_T, 128), f32), zc, zc, zc, zc)
            done = lowest
            tail_lo = i - jnp.minimum(i, SB_TAIL - 1)
            for n in SB_GROUPS_BWD:
                trips = (tail_lo - done) // n
                carry = lax.fori_loop(
                    0, trips, functools.partial(
                        lambda gi, cr, n, done, heads: group(heads, done + gi * n, n, cr, None),
                        n=n, done=done, heads=heads),
                    carry)
                done = done + trips * n
            carries.append(carry)

        def whole_tails():
            return tuple(group(heads, i - SB_TAIL + 1, SB_TAIL, cr, causal)
                         for (i, heads, _), cr in zip(blocks, carries))

        def short_tails():
            return tuple(_by_count(i, SB_TAIL, functools.partial(
                lambda n, i, heads, cr: group(heads, i - n + 1, n, cr, causal), i=i, heads=heads, cr=cr))
                for (i, heads, _), cr in zip(blocks, carries))

        carries = lax.cond(step_i * SB_QB >= SB_TAIL - 1, whole_tails, short_tails)
        for b, carry in enumerate(carries):
            dq_ref[b * SB_T:(b + 1) * SB_T, :] = (carry[0] * scale).astype(bf16)

        @pl.when(step_i == nq // SB_QB - 1)
        def _():
            dk_ref[...] = dk_acc[...].astype(bf16)
            dv_ref[...] = dv_acc[...].astype(bf16)

        if ride:
            pl.when((pl.program_id(0) == n_pairs - 1) & (step_i == nq // SB_QB - 1))(
                lambda: ride.finish(r_ins, r_lnd, r_sems))

    qs = pl.BlockSpec((SB_QB * SB_T, 128), lambda h, i: (i, h))
    full = pl.BlockSpec((S, 128), lambda h, i: (0, h))
    out = pl.pallas_call(
        body, name="sb_bwd", grid=(n_pairs, nq // SB_QB),
        in_specs=[qs,
                  pl.BlockSpec((S, 128), lambda h, i: (0, n_pairs + h)),
                  pl.BlockSpec((S, 128), lambda h, i: (0, 2 * n_pairs + h)),
                  qs, qs] + (ride.in_specs if ride else []),
        out_specs=[qs, full, full] + (ride.out_specs if ride else []),
        out_shape=[jax.ShapeDtypeStruct((S, D), bf16)] * 3 + (ride.out_shape if ride else []),
        scratch_shapes=[pltpu.VMEM((S, 128), f32), pltpu.VMEM((S, 128), f32)] + (ride.scratch if ride else []),
        compiler_params=_params(("arbitrary", "arbitrary")),
    )(proj, proj, proj, tot_lk, do, *(ride.srcs if ride else []))
    return out[0], out[1], out[2], list(out[3:])


CONV_CB = 256
HALO = 8


def _conv_fwd(proj, conv_w, conv_b, S, ride=None):
    tr = min(512, S)
    n_r = ride.n if ride else 0
    n_c = CONV_DIM // CONV_CB

    def body(x_ref, w_ref, b_ref, *rest):
        xc_ref, xbc_ref = rest[n_r:n_r + 2]
        r_ins, r_lnd, r_sems = rest[:n_r], rest[n_r + 2:2 * n_r + 2], rest[2 * n_r + 2:]
        if ride:
            pl.when(pl.program_id(0) == 0)(lambda: ride.start(r_ins, r_lnd, r_sems))
        w = w_ref[...]
        for t in range(S // tr):
            cur = x_ref[t * tr:(t + 1) * tr, :]
            halo = x_ref[t * tr - HALO:t * tr, :] if t else jnp.zeros((HALO, CONV_CB), f32)
            win = jnp.concatenate([halo, cur], axis=0)
            acc = b_ref[...] + w[CONV_K - 1:CONV_K, :] * cur
            for k in range(CONV_K - 1):
                acc = acc + w[k:k + 1, :] * pltpu.roll(win, CONV_K - 1 - k, 0)[HALO:, :]
            xc_ref[t * tr:(t + 1) * tr, :] = acc
            xbc_ref[t * tr:(t + 1) * tr, :] = acc * _sigmoid(acc)
        if ride:
            pl.when(pl.program_id(0) == n_c - 1)(lambda: ride.finish(r_ins, r_lnd, r_sems))

    col = pl.BlockSpec((S, CONV_CB), lambda c: (0, c))
    out = pl.pallas_call(
        body, name="conv_fwd", grid=(n_c,),
        in_specs=[pl.BlockSpec((S, CONV_CB), lambda c: (0, P_XBC // CONV_CB + c)),
                  pl.BlockSpec((CONV_K, CONV_CB), lambda c: (0, c)),
                  pl.BlockSpec((1, CONV_CB), lambda c: (0, c))] + (ride.in_specs if ride else []),
        out_specs=[col, col] + (ride.out_specs if ride else []),
        out_shape=[jax.ShapeDtypeStruct((S, CONV_DIM), f32)] * 2 + (ride.out_shape if ride else []),
        scratch_shapes=ride.scratch if ride else [],
        compiler_params=_params(("arbitrary",)),
    )(proj, conv_w, conv_b, *(ride.srcs if ride else []))
    return out[0], out[1], list(out[2:])


def _conv_bwd(proj, xc, dxbc, conv_w, S):
    tr = min(512, S)

    def body(x_ref, xc_ref, dy_ref, w_ref, dx_ref, dw_ref, db_ref, dxc_s):
        w = w_ref[...]
        xcv = xc_ref[...]
        sg = _sigmoid(xcv)
        dxc_s[0:S, :] = dy_ref[...] * (sg * (1.0 + xcv * (1.0 - sg)))
        dxc_s[S:S + HALO, :] = jnp.zeros((HALO, CONV_CB), f32)
        dws = [jnp.zeros((1, CONV_CB), f32) for _ in range(CONV_K)]
        db = jnp.zeros((1, CONV_CB), f32)
        for t in range(S // tr):
            cur = x_ref[t * tr:(t + 1) * tr, :]
            halo = x_ref[t * tr - HALO:t * tr, :] if t else jnp.zeros((HALO, CONV_CB), f32)
            win = jnp.concatenate([halo, cur], axis=0)
            dwin = dxc_s[t * tr:(t + 1) * tr + HALO, :]
            dcur = dwin[0:tr, :]
            db = db + jnp.sum(dcur, axis=0, keepdims=True)
            dws[CONV_K - 1] = dws[CONV_K - 1] + jnp.sum(dcur * cur, axis=0, keepdims=True)
            dx = w[CONV_K - 1:CONV_K, :] * dcur
            for k in range(CONV_K - 1):
                sh = CONV_K - 1 - k
                dws[k] = dws[k] + jnp.sum(dcur * pltpu.roll(win, sh, 0)[HALO:, :], axis=0, keepdims=True)
                dx = dx + w[k:k + 1, :] * pltpu.roll(dwin, tr + HALO - sh, 0)[0:tr, :]
            dx_ref[t * tr:(t + 1) * tr, :] = dx.astype(bf16)
        dw_ref[...] = jnp.concatenate(dws + [jnp.zeros((8 - CONV_K, CONV_CB), f32)], axis=0)
        db_ref[...] = db

    col = pl.BlockSpec((S, CONV_CB), lambda c: (0, c))
    return pl.pallas_call(
        body, name="conv_bwd", grid=(CONV_DIM // CONV_CB,),
        in_specs=[pl.BlockSpec((S, CONV_CB), lambda c: (0, P_XBC // CONV_CB + c)), col, col,
                  pl.BlockSpec((CONV_K, CONV_CB), lambda c: (0, c))],
        out_specs=[col, pl.BlockSpec((8, CONV_CB), lambda c: (0, c)), pl.BlockSpec((1, CONV_CB), lambda c: (0, c))],
        out_shape=[jax.ShapeDtypeStruct((S, CONV_DIM), bf16), jax.ShapeDtypeStruct((8, CONV_DIM), f32),
                   jax.ShapeDtypeStruct((1, CONV_DIM), f32)],
        scratch_shapes=[pltpu.VMEM((S + HALO, CONV_CB), f32)],
        compiler_params=_params(("parallel",)),
    )(proj, xc, dxbc, conv_w)


N_PAIR = SSD_HEADS // 2
NEG = -1e30


def _softplus(x):
    return jnp.maximum(x, 0.0) + jnp.log(1.0 + jnp.exp(-jnp.abs(x)))


def _ssd_common(dtr, dtb, alog):
    L = SSD_L
    r_i = lax.broadcasted_iota(jnp.int32, (L, L), 0)
    c_i = lax.broadcasted_iota(jnp.int32, (L, L), 1)
    dt = _softplus(dtr + dtb)
    a = -jnp.exp(alog)
    da = dt * a
    lower = (r_i >= c_i).astype(bf16)
    upper = (r_i <= c_i).astype(bf16)
    parts = _split3(da)
    a_cs = sum(_dot(lower, p) for p in parts)
    a_cs_t = sum(_dot(p, upper, TN) for p in parts)
    return dt, a, a_cs, a_cs_t, r_i >= c_i


def _pair_vec(lane, v, h):
    return jnp.where(lane < SB_HD, v[:, h:h + 1], v[:, h + 1:h + 2])


def _decay_mat(a_cs, a_cs_t, h, tril):
    return jnp.exp(jnp.where(tril, a_cs[:, h:h + 1] - a_cs_t[h:h + 1, :], NEG))


def _ssd_fwd(xbc, proj, pdt, dt_bias_p, a_log_p, d_skip_c, ssd_norm, S, ride=None):
    L = SSD_L
    nc = S // L
    n_r = ride.n if ride else 0

    def body(xbc_ref, dt_ref, z_ref, dtb_ref, alog_ref, dsk_ref, gn_ref, *rest):
        y_ref, yn_ref, hp_ref = rest[n_r:n_r + 3]
        state = rest[2 * n_r + 3]
        r_ins, r_lnd, r_sems = rest[:n_r], rest[n_r + 3:2 * n_r + 3], rest[2 * n_r + 4:]
        c = pl.program_id(0)
        if ride:
            pl.when(c == 0)(lambda: ride.start(r_ins, r_lnd, r_sems))

        @pl.when(c == 0)
        def _():
            state[...] = jnp.zeros_like(state)

        hp_ref[0] = state[...]
        lane = lax.broadcasted_iota(jnp.int32, (1, 128), 1)
        row128 = lax.broadcasted_iota(jnp.int32, (128, 1), 0)
        m_a, m_b = _sb_masks()
        dt, a, a_cs, a_cs_t, tril = _ssd_common(dt_ref[...], dtb_ref[...], alog_ref[...])
        a_last = a_cs[L - 1:L, :]
        for g in range(SSD_GROUPS):
            b_g = xbc_ref[:, SSD_INNER + g * SSD_N:SSD_INNER + (g + 1) * SSD_N].astype(bf16)
            c_g = xbc_ref[:, SSD_INNER + (SSD_GROUPS + g) * SSD_N:SSD_INNER + (SSD_GROUPS + g + 1) * SSD_N].astype(bf16)
            cb = _dot(c_g, b_g, NT)
            for pr in range(4):
                h = 8 * g + 2 * pr
                pi = h // 2
                cols = slice(pi * 128, (pi + 1) * 128)
                xs = xbc_ref[:, cols]
                x = xs * _pair_vec(lane, dt, h)
                acs = _pair_vec(lane, a_cs, h)
                al = _pair_vec(lane, a_last, h)
                w_a = (cb * _decay_mat(a_cs, a_cs_t, h, tril)).astype(bf16)
                w_b = (cb * _decay_mat(a_cs, a_cs_t, h + 1, tril)).astype(bf16)
                yd = _dot(w_a, (x * m_a).astype(bf16)) + _dot(w_b, (x * m_b).astype(bf16))
                hp = state[pi]
                yo = _dot(c_g, hp.astype(bf16), NT) * jnp.exp(acs)
                y_ref[:, cols] = yd + yo + dsk_ref[:, cols] * xs
                dec = jnp.exp(jnp.where(row128 < SB_HD, a_last[:, h:h + 1], a_last[:, h + 1:h + 2]))
                state[pi] = hp * dec + _dot((x * jnp.exp(al - acs)).astype(bf16), b_g, TN)
        zz = z_ref[...]
        y2 = y_ref[...] * (zz * _sigmoid(zz))
        gw = SSD_INNER // SSD_GROUPS
        for g in range(SSD_GROUPS):
            yg = y2[:, g * gw:(g + 1) * gw]
            rg = lax.rsqrt(jnp.mean(yg * yg, axis=1, keepdims=True) + EPS)
            yn_ref[:, g * gw:(g + 1) * gw] = (yg * rg * gn_ref[:, g * gw:(g + 1) * gw]).astype(bf16)
        if ride:
            pl.when(c == nc - 1)(lambda: ride.finish(r_ins, r_lnd, r_sems))

    vec128 = pl.BlockSpec((1, 128), lambda c: (0, 0))
    vecin = pl.BlockSpec((1, SSD_INNER), lambda c: (0, 0))
    rows = pl.BlockSpec((L, SSD_INNER), lambda c: (c, 0))
    out = pl.pallas_call(
        body, name="ssd_fwd", grid=(nc,),
        in_specs=[pl.BlockSpec((L, CONV_DIM), lambda c: (c, 0)),
                  pl.BlockSpec((L, 128), lambda c: (c, 0)),
                  pl.BlockSpec((L, SSD_INNER), lambda c: (c, P_Z // SSD_INNER)),
                  vec128, vec128, vecin, vecin] + (ride.in_specs if ride else []),
        out_specs=[rows, rows, pl.BlockSpec((1, N_PAIR, 128, SSD_N), lambda c: (c, 0, 0, 0))]
        + (ride.out_specs if ride else []),
        out_shape=[jax.ShapeDtypeStruct((S, SSD_INNER), f32), jax.ShapeDtypeStruct((S, SSD_INNER), bf16),
                   jax.ShapeDtypeStruct((nc, N_PAIR, 128, SSD_N), f32)] + (ride.out_shape if ride else []),
        scratch_shapes=[pltpu.VMEM((N_PAIR, 128, SSD_N), f32)] + (ride.scratch if ride else []),
        compiler_params=_params(("arbitrary",)),
    )(xbc, pdt, proj, dt_bias_p, a_log_p, d_skip_c, ssd_norm, *(ride.srcs if ride else []))
    return out[0], out[1], out[2], list(out[3:])


def _sum_all(v):
    return jnp.sum(jnp.sum(v, axis=1, keepdims=True), axis=0, keepdims=True)


def _ssd_bwd(dyn, y, xbc, proj, pdt, hprev, dt_bias_p, a_log_p, d_skip_c, ssd_norm, S, ride=None):
    L = SSD_L
    nc = S // L
    n_r = ride.n if ride else 0

    col = lax.broadcasted_iota(jnp.int32, (2 * SSD_INNER, 128), 0)
    head = lax.broadcasted_iota(jnp.int32, (2 * SSD_INNER, 128), 1)
    sel_pair = (col[:SSD_INNER] // SB_HD == head[:SSD_INNER]).astype(bf16)
    sel_head = (col // 128 == head).astype(bf16)

    def body(*refs):
        (dyn_ref, y_ref, xbc_ref, dt_ref, z_ref, hp_ref, dtb_ref, alog_ref, dsk_ref, gn_ref,
         selp_ref, selh_ref) = refs[:12]
        dz_ref, dxbc_ref, ddt_ref, dgn_ref, dsk_out, dalog_ref, ddtb_ref = refs[12 + n_r:19 + n_r]
        dstate, dy_s, st_a, st_q, st_d, st_x, dat = refs[19 + 2 * n_r:26 + 2 * n_r]
        r_ins, r_lnd, r_sems = refs[12:12 + n_r], refs[19 + n_r:19 + 2 * n_r], refs[26 + 2 * n_r:]
        c = pl.program_id(0)
        if ride:
            pl.when(c == 0)(lambda: ride.start(r_ins, r_lnd, r_sems))

        @pl.when(c == 0)
        def _():
            dat[...] = jnp.zeros_like(dat)
            dstate[...] = jnp.zeros_like(dstate)
            dgn_ref[...] = jnp.zeros_like(dgn_ref)
            dsk_out[...] = jnp.zeros_like(dsk_out)
            dalog_ref[...] = jnp.zeros_like(dalog_ref)
            ddtb_ref[...] = jnp.zeros_like(ddtb_ref)

        lane = lax.broadcasted_iota(jnp.int32, (1, 128), 1)
        row128 = lax.broadcasted_iota(jnp.int32, (128, 1), 0)
        rowl = lax.broadcasted_iota(jnp.int32, (L, 1), 0)
        m_a, m_b = _sb_masks()
        dtr = dt_ref[...]
        dt, a, a_cs, a_cs_t, tril = _ssd_common(dtr, dtb_ref[...], alog_ref[...])
        a_last = a_cs[L - 1:L, :]

        zz = z_ref[...]
        sg = _sigmoid(zz)
        silu = zz * sg
        yv = y_ref[...]
        y2 = yv * silu
        gw = SSD_INNER // SSD_GROUPS
        for g in range(SSD_GROUPS):
            sl = slice(g * gw, (g + 1) * gw)
            yg = y2[:, sl]
            rg = lax.rsqrt(jnp.mean(yg * yg, axis=1, keepdims=True) + EPS)
            yh = yg * rg
            dyn_g = dyn_ref[:, sl]
            dgn_ref[:, sl] += jnp.sum(dyn_g * yh, axis=0, keepdims=True)
            dyh = dyn_g * gn_ref[:, sl]
            dy2 = rg * (dyh - yh * jnp.mean(dyh * yh, axis=1, keepdims=True))
            dy_s[:, sl] = dy2 * silu[:, sl]
            dz_ref[:, sl] = (dy2 * yv[:, sl] * (sg[:, sl] * (1.0 + zz[:, sl] * (1.0 - sg[:, sl])))).astype(bf16)

        last_row = jnp.zeros((1, 128), f32)
        dsk_acc = jnp.zeros((1, 128), f32)
        for g in range(SSD_GROUPS):
            bsl = slice(SSD_INNER + g * SSD_N, SSD_INNER + (g + 1) * SSD_N)
            csl = slice(SSD_INNER + (SSD_GROUPS + g) * SSD_N, SSD_INNER + (SSD_GROUPS + g + 1) * SSD_N)
            b_g = xbc_ref[:, bsl].astype(bf16)
            c_g = xbc_ref[:, csl].astype(bf16)
            cb = _dot(c_g, b_g, NT)
            dcb = jnp.zeros((L, L), f32)
            dc_g = jnp.zeros((L, SSD_N), f32)
            db_g = jnp.zeros((L, SSD_N), f32)
            for pr in range(4):
                h = 8 * g + 2 * pr
                pi = h // 2
                cols = slice(pi * 128, (pi + 1) * 128)
                xs = xbc_ref[:, cols]
                dt_p = _pair_vec(lane, dt, h)
                x = xs * dt_p
                acs = _pair_vec(lane, a_cs, h)
                al = _pair_vec(lane, a_last, h)
                e_a = jnp.exp(acs)
                dte = jnp.exp(al - acs)
                m_mat_a = _decay_mat(a_cs, a_cs_t, h, tril)
                m_mat_b = _decay_mat(a_cs, a_cs_t, h + 1, tril)
                dyp = dy_s[:, cols]
                dsk = dsk_ref[:, cols]
                d_hn = dstate[pi]
                hp = hp_ref[0, pi]
                dy_a = (dyp * m_a).astype(bf16)
                dy_b = (dyp * m_b).astype(bf16)
                x_b = x.astype(bf16)
                gm_a = _dot(dy_a, x_b, NT) * m_mat_a
                gm_b = _dot(dy_b, x_b, NT) * m_mat_b
                dcb = dcb + gm_a + gm_b
                dx_d = _dot((cb * m_mat_a).astype(bf16), dy_a, TN) + _dot((cb * m_mat_b).astype(bf16), dy_b, TN)
                dx_s = _dot(b_g, d_hn.astype(bf16), NT) * dte
                dx = dx_d + dx_s
                dxbc_ref[:, cols] = dx * dt_p + dsk * dyp
                xdxs = x * dx_s
                st_x[:, cols] = xdxs
                st_a[:, cols] = dyp * (_dot(c_g, hp.astype(bf16), NT) * e_a) - xdxs
                st_d[:, cols] = dx * xs
                hh = d_hn * hp
                dsk_row = jnp.sum(dyp * xs, axis=0, keepdims=True)
                dec = jnp.exp(jnp.where(row128 < SB_HD, a_last[:, h:h + 1], a_last[:, h + 1:h + 2]))
                for hd, m, gm in ((h, m_a, gm_a), (h + 1, m_b, gm_b)):
                    half = slice(0, SB_HD) if hd == h else slice(SB_HD, 128)
                    qm = gm * cb
                    st_q[:, hd * 128:(hd + 1) * 128] = qm
                    dat[hd:hd + 1, :] = jnp.sum(qm, axis=0, keepdims=True)
                    hh_sum = jnp.sum(jnp.sum(hh[half, :], axis=0, keepdims=True), axis=1, keepdims=True)
                    last_row = jnp.where(lane == hd, jnp.exp(a_last[:, hd:hd + 1]) * hh_sum, last_row)
                    dsk_acc = jnp.where(lane == hd, jnp.sum(dsk_row * m, axis=1, keepdims=True), dsk_acc)
                dye = (dyp * e_a).astype(bf16)
                dc_g = dc_g + _dot(dye, hp.astype(bf16))
                db_g = db_g + _dot((x * dte).astype(bf16), d_hn.astype(bf16))
                dstate[pi] = dec * d_hn + _dot(dye, c_g, TN)
            dcb_b = dcb.astype(bf16)
            dxbc_ref[:, csl] = dc_g + _dot(dcb_b, b_g)
            dxbc_ref[:, bsl] = db_g + _dot(dcb_b, c_g, TN)

        r_i = lax.broadcasted_iota(jnp.int32, (L, L), 0)
        c_i = lax.broadcasted_iota(jnp.int32, (L, L), 1)
        rev = (r_i <= c_i).astype(bf16)

        def head_sums(st, sel, split=_split2):
            return sum(_dot(p, sel[...]) for p in split(st[...]))

        last_row = last_row + jnp.sum(head_sums(st_x, selp_ref), axis=0, keepdims=True)
        d_acs = (head_sums(st_a, selp_ref) + head_sums(st_q, selh_ref, _split3)
                 + jnp.where(rowl == L - 1, last_row, 0.0))
        ddt_x = head_sums(st_d, selp_ref)
        dda = sum(_dot(rev, p) for p in _split3(d_acs)) - sum(_dot(rev, p, NT) for p in _split3(dat[...]))
        ddt = ddt_x + dda * a
        dalog_ref[...] += jnp.sum(dda * dt, axis=0, keepdims=True) * a
        ddtr = jnp.where(lane < SSD_HEADS, ddt * _sigmoid(dtr + dtb_ref[...]), 0.0)
        ddt_ref[...] = ddtr.astype(bf16)
        ddtb_ref[...] += jnp.sum(ddtr, axis=0, keepdims=True)
        dsk_out[...] += dsk_acc
        if ride:
            pl.when(c == nc - 1)(lambda: ride.finish(r_ins, r_lnd, r_sems))

    rv = lambda c: nc - 1 - c
    vec128 = pl.BlockSpec((1, 128), lambda c: (0, 0))
    vecin = pl.BlockSpec((1, SSD_INNER), lambda c: (0, 0))
    rows = pl.BlockSpec((L, SSD_INNER), lambda c: (rv(c), 0))
    return pl.pallas_call(
        body, name="ssd_bwd", grid=(nc,),
        in_specs=[rows, rows,
                  pl.BlockSpec((L, CONV_DIM), lambda c: (rv(c), 0)),
                  pl.BlockSpec((L, 128), lambda c: (rv(c), 0)),
                  pl.BlockSpec((L, SSD_INNER), lambda c: (rv(c), P_Z // SSD_INNER)),
                  pl.BlockSpec((1, N_PAIR, 128, SSD_N), lambda c: (rv(c), 0, 0, 0)),
                  vec128, vec128, vecin, vecin,
                  pl.BlockSpec((SSD_INNER, 128), lambda c: (0, 0)),
                  pl.BlockSpec((2 * SSD_INNER, 128), lambda c: (0, 0))] + (ride.in_specs if ride else []),
        out_specs=[rows, pl.BlockSpec((L, CONV_DIM), lambda c: (rv(c), 0)),
                   pl.BlockSpec((L, 128), lambda c: (rv(c), 0)), vecin, vec128, vec128, vec128]
        + (ride.out_specs if ride else []),
        out_shape=[jax.ShapeDtypeStruct((S, SSD_INNER), bf16), jax.ShapeDtypeStruct((S, CONV_DIM), f32),
                   jax.ShapeDtypeStruct((S, 128), bf16), jax.ShapeDtypeStruct((1, SSD_INNER), f32),
                   jax.ShapeDtypeStruct((1, 128), f32), jax.ShapeDtypeStruct((1, 128), f32),
                   jax.ShapeDtypeStruct((1, 128), f32)] + (ride.out_shape if ride else []),
        scratch_shapes=[pltpu.VMEM((N_PAIR, 128, SSD_N), f32), pltpu.VMEM((L, SSD_INNER), f32),
                        pltpu.VMEM((L, SSD_INNER), f32), pltpu.VMEM((L, 2 * SSD_INNER), f32),
                        pltpu.VMEM((L, SSD_INNER), f32), pltpu.VMEM((L, SSD_INNER), f32),
                        pltpu.VMEM((128, L), f32)]
        + (ride.scratch if ride else []),
        compiler_params=_params(("arbitrary",)),
    )(dyn, y, xbc, pdt, proj, hprev, dt_bias_p, a_log_p, d_skip_c, ssd_norm, sel_pair, sel_head,
      *(ride.srcs if ride else []))


MEM_W = MEM_HEADS * MEM_HD


def _mem_probs(q, k):
    s = _dot(q, k, NT) * (MEM_HD ** -0.5)
    s = s - jnp.max(s, axis=1, keepdims=True)
    p = jnp.exp(s)
    return p / jnp.sum(p, axis=1, keepdims=True)


def _mem_fwd(proj, kv, S, tm=512):
    tm = min(tm, S)
    M = kv.shape[0]

    def body(q_ref, kv_ref, o_ref):
        for h in range(MEM_HEADS):
            sl = slice(h * MEM_HD, (h + 1) * MEM_HD)
            vsl = slice(MEM_W + h * MEM_HD, MEM_W + (h + 1) * MEM_HD)
            p = _mem_probs(q_ref[:, sl].astype(bf16), kv_ref[:, sl].astype(bf16))
            o_ref[:, sl] = _dot(p.astype(bf16), kv_ref[:, vsl].astype(bf16)).astype(bf16)

    return pl.pallas_call(
        body, name="mem_fwd", grid=(S // tm,),
        in_specs=[pl.BlockSpec((tm, MEM_W), lambda i: (i, P_MEMQ // MEM_W)),
                  pl.BlockSpec((M, 2 * MEM_W), lambda i: (0, 0))],
        out_specs=pl.BlockSpec((tm, MEM_W), lambda i: (i, 0)),
        out_shape=jax.ShapeDtypeStruct((S, MEM_W), bf16),
        compiler_params=_params(("parallel",)),
    )(proj, kv)


def _mem_bwd(proj, kv, dy, S, tm=512):
    tm = min(tm, S)
    M = kv.shape[0]
    scale = MEM_HD ** -0.5

    def body(q_ref, kv_ref, dy_ref, dq_ref, dkv_ref):
        @pl.when(pl.program_id(0) == 0)
        def _():
            dkv_ref[...] = jnp.zeros_like(dkv_ref)

        for h in range(MEM_HEADS):
            sl = slice(h * MEM_HD, (h + 1) * MEM_HD)
            vsl = slice(MEM_W + h * MEM_HD, MEM_W + (h + 1) * MEM_HD)
            q = q_ref[:, sl].astype(bf16)
            k = kv_ref[:, sl].astype(bf16)
            v = kv_ref[:, vsl].astype(bf16)
            dyh = dy_ref[:, sl].astype(bf16)
            p = _mem_probs(q, k)
            dp = _dot(dyh, v, NT)
            ds = (p * (dp - jnp.sum(dp * p, axis=1, keepdims=True)) * scale).astype(bf16)
            dq_ref[:, sl] = _dot(ds, k).astype(bf16)
            dkv_ref[:, sl] += _dot(ds, q, TN)
            dkv_ref[:, vsl] += _dot(p.astype(bf16), dyh, TN)

    return pl.pallas_call(
        body, name="mem_bwd", grid=(S // tm,),
        in_specs=[pl.BlockSpec((tm, MEM_W), lambda i: (i, P_MEMQ // MEM_W)),
                  pl.BlockSpec((M, 2 * MEM_W), lambda i: (0, 0)),
                  pl.BlockSpec((tm, MEM_W), lambda i: (i, 0))],
        out_specs=[pl.BlockSpec((tm, MEM_W), lambda i: (i, 0)), pl.BlockSpec((M, 2 * MEM_W), lambda i: (0, 0))],
        out_shape=[jax.ShapeDtypeStruct((S, MEM_W), bf16), jax.ShapeDtypeStruct((M, 2 * MEM_W), f32)],
        compiler_params=_params(("arbitrary",)),
    )(proj, kv, dy)


def _merge_fwd(proj, t0, t1, t2, S, tm=512):
    tm = min(tm, S)

    def body(g_ref, t0_ref, t1_ref, t2_ref, o_ref):
        acc = jnp.zeros((tm, D), f32)
        for b, t_ref in enumerate((t0_ref, t1_ref, t2_ref)):
            acc = acc + _sigmoid(g_ref[:, b * D:(b + 1) * D]) * t_ref[...]
        o_ref[...] = acc.astype(bf16)

    row = pl.BlockSpec((tm, D), lambda i: (i, 0))
    return pl.pallas_call(
        body, name="merge_fwd", grid=(S // tm,),
        in_specs=[pl.BlockSpec((tm, 3 * D), lambda i: (i, P_GATE // (3 * D))), row, row, row],
        out_specs=row, out_shape=jax.ShapeDtypeStruct((S, D), bf16),
        compiler_params=_params(("parallel",)),
    )(proj, t0, t1, t2)


def _merge_bwd(proj, t0, t1, t2, dm, S, tm=512):
    tm = min(tm, S)

    def body(g_ref, t0_ref, t1_ref, t2_ref, dm_ref, d0_ref, d1_ref, d2_ref, dg_ref):
        dmv = dm_ref[...]
        for b, (t_ref, d_ref) in enumerate(((t0_ref, d0_ref), (t1_ref, d1_ref), (t2_ref, d2_ref))):
            sg = _sigmoid(g_ref[:, b * D:(b + 1) * D])
            d_ref[...] = (dmv * sg).astype(bf16)
            dg_ref[:, b * D:(b + 1) * D] = (dmv * t_ref[...] * sg * (1.0 - sg)).astype(bf16)

    row = pl.BlockSpec((tm, D), lambda i: (i, 0))
    return pl.pallas_call(
        body, name="merge_bwd", grid=(S // tm,),
        in_specs=[pl.BlockSpec((tm, 3 * D), lambda i: (i, P_GATE // (3 * D))), row, row, row, row],
        out_specs=[row, row, row, pl.BlockSpec((tm, 3 * D), lambda i: (i, 0))],
        out_shape=[jax.ShapeDtypeStruct((S, D), bf16)] * 3 + [jax.ShapeDtypeStruct((S, 3 * D), bf16)],
        compiler_params=_params(("parallel",)),
    )(proj, t0, t1, t2, dm)


def _loss_head(ff, g, h1, target, S, tm=512):
    tm = min(tm, S)

    def body(ff_ref, g_ref, h1_ref, t_ref, dh_ref, loss_ref):
        xv = ff_ref[...]
        r = lax.rsqrt(jnp.mean(xv * xv, axis=1, keepdims=True) + EPS)
        err = h1_ref[...] + xv * r * g_ref[...] - t_ref[...]
        dh_ref[...] = err * (1.0 / D)

        @pl.when(pl.program_id(0) == 0)
        def _():
            loss_ref[...] = jnp.zeros_like(loss_ref)

        loss_ref[...] += 0.5 * _sum_all(jnp.mean(err * err, axis=1, keepdims=True)) * jnp.ones((1, 128), f32)

    row = pl.BlockSpec((tm, D), lambda i: (i, 0))
    return pl.pallas_call(
        body, name="loss_head", grid=(S // tm,),
        in_specs=[row, pl.BlockSpec((1, D), lambda i: (0, 0)), row, row],
        out_specs=[row, pl.BlockSpec((1, 128), lambda i: (0, 0))],
        out_shape=[jax.ShapeDtypeStruct((S, D), f32), jax.ShapeDtypeStruct((1, 128), f32)],
        compiler_params=_params(("arbitrary",)),
    )(ff, g, h1, target)


def _local_step(x, mem, target, wts, late_rides, late_weights, small, rest_rides, w_in_ride):
    S = x.shape[0]
    M = mem.shape[0]
    pad = lambda v: jnp.pad(v, ((0, 0), (0, 128 - SSD_HEADS)))
    dtb_p, alog_p = pad(small["dt_bias"]), pad(small["a_log"])
    dsk_c = jnp.repeat(small["d_skip"], SB_HD, axis=1)

    u = _rms_fwd(x, small["norm_mix_pre"], name="norm_pre", out_dtype=bf16)
    rides = late_rides or (None, None, None, None)
    if late_rides:
        proj, lands_a = _mm(u, wts["w_main"], "nn", tm=1024, tn=1024, name="in_proj", ride=rides[0])
    else:
        proj, lands_a = _mm(u, wts["w_main"], "nn", tm=1024, tn=1024, name="in_proj"), []
    pdt = _mm(u, wts["w_dt"], "nn", tm=1024, tn=128, name="in_proj_dt")
    y_sb, tot_lk, lands_b = _sb_fwd(proj, S, rides[1])
    wts = dict(wts, **late_weights(0, lands_a))
    small = dict(small, conv_w=wts.pop("conv_w"))
    xc, xbc, lands_d = _conv_fwd(proj, small["conv_w"], small["conv_b"], S, rides[3])
    y_ssd, yn, hprev, lands_c = _ssd_fwd(xbc, proj, pdt, dtb_p, alog_p, dsk_c, small["ssd_norm"], S, rides[2])
    wts = dict(wts, **late_weights(1, lands_b), **late_weights(2, lands_c), **late_weights(3, lands_d))
    mn = _rms_fwd(mem, small["norm_mem"], name="norm_mem", out_dtype=bf16, tm=min(512, M))
    kv = _mm(mn, wts["w_mem_kv"], "nn", tm=M, tn=1024, name="mem_kv")
    y_mem = _mem_fwd(proj, kv, S)
    t0 = _mm(y_sb, wts["w_sb_out"], "nn", tm=1024, tn=1024, name="sb_out")
    t1 = _mm(yn, wts["w_ssd_out"], "nn", tm=1024, tn=1024, name="ssd_out")
    t2 = _mm(y_mem, wts["w_mem_out"], "nn", tm=1024, tn=1024, name="mem_out")
    merged = _merge_fwd(proj, t0, t1, t2, S)
    mix = _mm(merged, wts["w_o"], "nn", tm=1024, tn=1024, name="w_o")
    h1 = _rms_fwd(mix, small["norm_mix_post"], name="norm_mix_post", out_dtype=f32, residual=x)
    u2 = _rms_fwd(h1, small["norm_mlp_pre"], name="norm_mlp_pre", out_dtype=bf16)
    a_up, hrelu = _mm(u2, wts["w_up"], "nn", tm=1024, tn=1024, name="mlp_up", out_dtypes=(f32, bf16),
                      epi=lambda acc: (acc, jnp.square(jnp.maximum(acc, 0.0))))
    ff = _mm(hrelu, wts["w_down"], "nn", tm=1024, tn=1024, name="mlp_down")
    dh2, loss = _loss_head(ff, small["norm_mlp_post"], h1, target, S)

    g = {}
    dff, g["norm_mlp_post"] = _rms_bwd(ff, dh2, small["norm_mlp_post"], name="norm_mlp_post_bwd", out_dtype=bf16)
    da = _mm(dff, wts["w_down"], "nt", tm=1024, tn=1024, name="mlp_down_dx", out_dtypes=(bf16,),
             epi=lambda acc, a: (acc * (2.0 * jnp.maximum(a, 0.0)),), extras=(a_up,))
    g["w_down"] = _mm(hrelu, dff, "tn", tm=1024, tn=1024, name="mlp_down_dw")
    du2 = _mm(da, wts["w_up"], "nt", tm=1024, tn=1024, name="mlp_up_dx")
    g["w_up"] = _mm(u2, da, "tn", tm=1024, tn=1024, name="mlp_up_dw")
    dh1, g["norm_mlp_pre"] = _rms_bwd(h1, du2, small["norm_mlp_pre"], name="norm_mlp_pre_bwd", out_dtype=f32, add=dh2)
    dmix, g["norm_mix_post"] = _rms_bwd(mix, dh1, small["norm_mix_post"], name="norm_mix_post_bwd", out_dtype=bf16)
    dmerged = _mm(dmix, wts["w_o"], "nt", tm=1024, tn=1024, name="w_o_dx")
    g["w_o"] = _mm(merged, dmix, "tn", tm=1024, tn=1024, name="w_o_dw")
    dt0, dt1, dt2, dgl = _merge_bwd(proj, t0, t1, t2, dmerged, S)
    dy_sb = _mm(dt0, wts["w_sb_out"], "nt", tm=1024, tn=1024, name="sb_out_dx")
    g["w_sb_out"] = _mm(y_sb, dt0, "tn", tm=1024, tn=1024, name="sb_out_dw")
    dy_ssd = _mm(dt1, wts["w_ssd_out"], "nt", tm=1024, tn=1024, name="ssd_out_dx")
    g["w_ssd_out"] = _mm(yn, dt1, "tn", tm=1024, tn=1024, name="ssd_out_dw")
    dy_mem = _mm(dt2, wts["w_mem_out"], "nt", tm=1024, tn=1024, name="mem_out_dx")
    g["w_mem_out"] = _mm(y_mem, dt2, "tn", tm=1024, tn=1024, name="mem_out_dw")
    dmemq, dkv = _mem_bwd(proj, kv, dy_mem, S)
    g["w_mem_kv"] = _mm(mn, dkv, "tn", tm=1024, tn=1024, name="mem_kv_dw")
    dmn = _mm(dkv, wts["w_mem_kv"], "nt", tm=M, tn=1024, name="mem_kv_dx")
    _, g["norm_mem"] = _rms_bwd(mem, dmn, small["norm_mem"], name="norm_mem_bwd", out_dtype=bf16, tm=min(512, M))
    rides = rest_rides(g) if rest_rides else (None, None)
    dz, dxbc, ddt, g["ssd_norm"], dsk, dalog, ddtb, *lands_a = _ssd_bwd(
        dy_ssd, y_ssd, xbc, proj, pdt, hprev, dtb_p, alog_p, dsk_c, small["ssd_norm"], S, rides[0])
    g["d_skip"], g["a_log"], g["dt_bias"] = dsk[:, :SSD_HEADS], dalog[:, :SSD_HEADS], ddtb[:, :SSD_HEADS]
    dxbc_raw, dcw, g["conv_b"] = _conv_bwd(proj, xc, dxbc, small["conv_w"], S)
    g["conv_w"] = dcw[:CONV_K]
    dq, dk, dv, lands_b = _sb_bwd(proj, tot_lk, dy_sb, S, rides[1])
    g["rest_lands"] = lands_b + lands_a
    dproj = (dq, dk, dv, dxbc_raw, dgl, dmemq, dz)
    u_t = u.T
    g["w_main"] = [_mm(u_t, p, "nn", tm=512, tn=1024, name="in_proj_dw_%d" % i) for i, p in enumerate(dproj)]
    g["w_dt"] = _mm(u_t, ddt, "nn", tm=512, tn=128, name="in_proj_dt_dw")
    du_dt = _mm(ddt, wts["w_dt"], "nt", tm=1024, tn=1024, name="in_proj_dt_dx")
    du, g["w_in_lands"] = _mm_pieces_nt(dproj, wts["w_main"], du_dt, tm=512, tn=256, name="in_proj_dx",
                                        ride=w_in_ride(g) if w_in_ride else None)
    grad_x, g["norm_mix_pre"] = _rms_bwd(x, du, small["norm_mix_pre"], name="norm_pre_bwd", out_dtype=f32, add=dh1)
    return loss, grad_x, g


def _to_internal(w_in):
    sec = lambda r: w_in[:, r[0]:r[1]]
    w_main = jnp.concatenate([sec(R_QKV), sec(R_XBC), sec(R_GATE), sec(R_MEMQ), sec(R_Z)], axis=1)
    w_dt = jnp.pad(sec(R_DT), ((0, 0), (0, 128 - SSD_HEADS)))
    return w_main, w_dt


def _from_internal(pieces, g_dt):
    dq, dk, dv, dxbc, dgate, dmemq, dz = pieces
    return [dq, dk, dv, dz, dxbc, g_dt[:, :SSD_HEADS], dmemq, dgate]


def _w_in_slab(ordered, s, dtype):
    width = D_IN // N_SHARD
    lo, hi, off, parts = s * width, (s + 1) * width, 0, []
    for p in ordered:
        a, b = max(lo, off), min(hi, off + p.shape[1])
        if a < b:
            parts.append(p[:, a - off:b - off].astype(dtype))
        off += p.shape[1]
    return jnp.concatenate(parts, axis=1)


MESH = pl.DeviceIdType.MESH
ANY = pl.BlockSpec(memory_space=pl.ANY)


def _place():
    x, y, c = lax.axis_index("x"), lax.axis_index("y"), lax.axis_index("c")
    return (x, y, c), [(1 - x, y, c), (x, 1 - y, c), (1 - x, 1 - y, c)]


def _exchange_copy(mode, ins, lands, send, recv, a, k, me, peers, arriving):
    p = peers[k]
    theirs = 2 * p[0] + p[1]
    if mode == "gather":
        src, dst = ins[a], lands[a].at[theirs if arriving else me]
    else:
        src, dst = ins[a].at[theirs], lands[a].at[k]
    return pltpu.make_async_remote_copy(src_ref=src, dst_ref=dst, send_sem=send.at[a * 3 + k],
                                        recv_sem=recv.at[a * 3 + k], device_id=p, device_id_type=MESH)


class _Ride:
    def __init__(self, srcs, mode):
        self.srcs, self.mode, self.n = list(srcs), mode, len(srcs)
        n = self.n
        self.in_specs, self.out_specs = [ANY] * n, [ANY] * n
        self.out_shape = [
            jax.ShapeDtypeStruct((N_SHARD,) + s.shape if mode == "gather" else (3,) + s.shape[1:], s.dtype)
            for s in self.srcs]
        self.scratch = [pltpu.SemaphoreType.DMA((3 * n,)), pltpu.SemaphoreType.DMA((3 * n,)),
                        pltpu.SemaphoreType.DMA((n,))]

    def _own(self, ins, lnd, sems):
        if self.mode != "gather":
            return []
        me = 2 * lax.axis_index("x") + lax.axis_index("y")
        return [pltpu.make_async_copy(ins[a], lnd[a].at[me], sems[2].at[a]) for a in range(self.n)]

    def _far(self, ins, lnd, sems, arriving):
        (x, y, c), peers = _place()
        return [_exchange_copy(self.mode, ins, lnd, sems[0], sems[1], a, k, 2 * x + y, peers, arriving)
                for a in range(self.n) for k in range(3)]

    def start(self, ins, lnd, sems):
        for cp in self._own(ins, lnd, sems) + self._far(ins, lnd, sems, False):
            cp.start()

    def finish(self, ins, lnd, sems):
        for cp in self._far(ins, lnd, sems, True):
            cp.wait_recv()
        for cp in self._far(ins, lnd, sems, False):
            cp.wait_send()
        for cp in self._own(ins, lnd, sems):
            cp.wait()


def _gather_two_level(shards, name):
    n = len(shards)

    def body(*refs):
        ins, lnd = refs[:n], refs[n:2 * n]
        send, recv, loc = refs[2 * n:]
        (x, y, c), peers = _place()
        me = 2 * x + y

        def half(ref, a, core):
            rows = shards[a].shape[0] // 2
            return ref.at[pl.ds(core * rows, rows)]

        def copy(a, j, slot, core, to):
            return pltpu.make_async_remote_copy(
                src_ref=half(ins[a], a, core) if j < 3 else half(lnd[a].at[slot], a, core),
                dst_ref=half(lnd[a].at[slot], a, core), send_sem=send.at[6 * a + j], recv_sem=recv.at[6 * a + j],
                device_id=to, device_id_type=MESH)

        own = [pltpu.make_async_copy(ins[a], lnd[a].at[me], loc.at[a]) for a in range(n)]
        far = [copy(a, k, me, c, peers[k]) for a in range(n) for k in range(3)]
        for cp in own + far:
            cp.start()
        passed = []
        for a in range(n):
            for k, p in enumerate(peers):
                theirs = 2 * p[0] + p[1]
                copy(a, k, theirs, c, p).wait_recv()
                passed.append(copy(a, 3 + k, theirs, c, (x, y, 1 - c)))
                passed[-1].start()
        for a in range(n):
            for k, p in enumerate(peers):
                copy(a, 3 + k, 2 * p[0] + p[1], 1 - c, (x, y, 1 - c)).wait_recv()
        for cp in far + passed:
            cp.wait_send()
        for cp in own:
            cp.wait()

    return pl.pallas_call(
        body, name=name, in_specs=[ANY] * n, out_specs=[ANY] * n,
        out_shape=[jax.ShapeDtypeStruct((N_SHARD,) + s.shape, s.dtype) for s in shards],
        scratch_shapes=[pltpu.SemaphoreType.DMA((6 * n,)), pltpu.SemaphoreType.DMA((6 * n,)),
                        pltpu.SemaphoreType.DMA((n,))],
    )(*shards)


def _exchange_packets(packet):
    def body(pk, pk_out, send, recv, loc):
        x, y, c = lax.axis_index("x"), lax.axis_index("y"), lax.axis_index("c")
        lin = 4 * x + 2 * y + c
        own = pltpu.make_async_copy(pk, pk_out.at[lin], loc.at[0])
        own.start()

        def pk_copy(m, slot):
            dev = (x ^ ((m >> 2) & 1), y ^ ((m >> 1) & 1), c ^ (m & 1))
            return pltpu.make_async_remote_copy(
                src_ref=pk, dst_ref=pk_out.at[slot], send_sem=send.at[m - 1], recv_sem=recv.at[m - 1],
                device_id=dev, device_id_type=MESH)

        sent = [pk_copy(m, lin) for m in range(1, N_DEV)]
        for cp in sent:
            cp.start()
        for m in range(1, N_DEV):
            pk_copy(m, lin ^ m).wait_recv()
        for cp in sent:
            cp.wait_send()
        own.wait()

    return pl.pallas_call(
        body, name="exchange_packets", in_specs=[ANY], out_specs=ANY,
        out_shape=jax.ShapeDtypeStruct((N_DEV,) + packet.shape, packet.dtype),
        scratch_shapes=[pltpu.SemaphoreType.DMA((N_DEV - 1,)), pltpu.SemaphoreType.DMA((N_DEV - 1,)),
                        pltpu.SemaphoreType.DMA((1,))],
    )(packet)


def _swap_sibling(parts, name):
    n = len(parts)

    def body(*refs):
        ins, outs = refs[:n], refs[n:2 * n]
        send, recv = refs[2 * n:]
        x, y, c = lax.axis_index("x"), lax.axis_index("y"), lax.axis_index("c")
        cps = [pltpu.make_async_remote_copy(
            src_ref=ins[a], dst_ref=outs[a], send_sem=send.at[a], recv_sem=recv.at[a],
            device_id=(x, y, 1 - c), device_id_type=MESH) for a in range(n)]
        for cp in cps:
            cp.start()
        for cp in cps:
            cp.wait_recv()
        for cp in cps:
            cp.wait_send()

    return pl.pallas_call(
        body, name=name,
        in_specs=[ANY] * n, out_specs=[ANY] * n,
        out_shape=[jax.ShapeDtypeStruct(p.shape, p.dtype) for p in parts],
        scratch_shapes=[pltpu.SemaphoreType.DMA((n,)), pltpu.SemaphoreType.DMA((n,))],
    )(*parts)


BLOCK_ELEMS = 256 * 1024


def _row_tile(R, C):
    tr = max(8, (BLOCK_ELEMS // C) // 8 * 8)
    while R % tr:
        tr -= 8
    return min(tr, R)


def _sum_parts(own, stack, name, out_dtype=f32):
    k = stack.shape[0]
    R, C = stack.shape[1:]
    tr = _row_tile(R, C)

    def body(*refs):
        o_ref = refs[-1]
        acc = refs[0][...].astype(f32)
        for r in refs[1:-1]:
            acc = acc + r[...].astype(f32)
        o_ref[...] = acc.astype(out_dtype)

    row = pl.BlockSpec((tr, C), lambda i: (i, 0))
    specs = ([row] if own is not None else []) + [
        pl.BlockSpec((None, tr, C), functools.partial(lambda i, j: (j, i, 0), j=j)) for j in range(k)]
    args = ([own] if own is not None else []) + [stack] * k
    return pl.pallas_call(
        body, name=name, grid=(R // tr,), in_specs=specs, out_specs=row,
        out_shape=jax.ShapeDtypeStruct((R, C), out_dtype), compiler_params=_params(("parallel",)),
    )(*args)


def _adamw(w, m, v, g_parts, name):
    R, C = w.shape
    tr = _row_tile(R, C)
    n_g = len(g_parts)

    def body(w_ref, m_ref, v_ref, *rest):
        g = rest[0][...]
        for r in rest[1:n_g]:
            g = g + r[...]
        g_ref, d_ref, nm_ref, nv_ref = rest[n_g:]
        nm = ADAM_B1 * m_ref[...] + (1.0 - ADAM_B1) * g
        nv = ADAM_B2 * v_ref[...] + (1.0 - ADAM_B2) * jnp.square(g)
        m_hat = nm / (1.0 - ADAM_B1 ** ADAM_STEP)
        v_hat = nv / (1.0 - ADAM_B2 ** ADAM_STEP)
        g_ref[...] = g
        d_ref[...] = -ADAM_LR * (m_hat / (jnp.sqrt(v_hat) + ADAM_EPS) + ADAM_WD * w_ref[...])
        nm_ref[...] = nm
        nv_ref[...] = nv

    row = pl.BlockSpec((tr, C), lambda i: (i, 0))
    return pl.pallas_call(
        body, name=name, grid=(R // tr,), in_specs=[row] * (3 + n_g), out_specs=[row] * 4,
        out_shape=[jax.ShapeDtypeStruct((R, C), f32)] * 4, compiler_params=_params(("parallel",)),
    )(w, m, v, *g_parts)


BIG = ("w_in", "w_mem_kv", "w_sb_out", "w_ssd_out", "w_mem_out", "w_o", "w_up", "w_down")
LATE = ("w_sb_out", "w_ssd_out", "w_mem_out", "w_o", "w_up", "w_down")
REST = BIG[1:]
COL_SHARDED = ("w_in", "w_mem_kv", "w_up")
SMALL = ("norm_mix_pre", "conv_w", "conv_b", "dt_bias", "a_log", "d_skip", "ssd_norm", "norm_mem",
         "norm_mix_post", "norm_mlp_pre", "norm_mlp_post")
WEIGHTS = ("norm_mix_pre", "w_in", "conv_w", "conv_b", "dt_bias", "a_log", "d_skip", "ssd_norm", "norm_mem",
           "w_mem_kv", "w_sb_out", "w_ssd_out", "w_mem_out", "w_o", "norm_mix_post", "norm_mlp_pre", "w_up",
           "w_down", "norm_mlp_post")
PK_ROWS = 184


def _pack(vecs):
    flat = jnp.concatenate([v.reshape(-1) for v in vecs])
    return jnp.pad(flat, (0, PK_ROWS * 128 - flat.shape[0])).reshape(PK_ROWS, 128)


def _unpack(pk, shapes):
    flat = pk.reshape(-1)
    out, off = [], 0
    for s in shapes:
        n = 1
        for d in s:
            n *= d
        out.append(flat[off:off + n].reshape(s))
        off += n
    return out


def _full_from_slabs(name, slabs):
    if name in COL_SHARDED:
        return slabs.transpose(1, 0, 2).reshape(slabs.shape[1], -1)
    return slabs.reshape(-1, slabs.shape[2])


def _slabs_from_full(name, g):
    if name in COL_SHARDED:
        return g.reshape(g.shape[0], N_SHARD, -1).transpose(1, 0, 2)
    return g.reshape(N_SHARD, -1, g.shape[1])


def kernel(x, mem, norm_mix_pre, w_in, conv_w, conv_b, dt_bias, a_log, d_skip, ssd_norm, norm_mem, w_mem_kv, w_sb_out, w_ssd_out, w_mem_out, w_o, norm_mix_post, norm_mlp_pre, w_up, w_down, norm_mlp_post, loss_target, m_norm_mix_pre, m_w_in, m_conv_w, m_conv_b, m_dt_bias, m_a_log, m_d_skip, m_ssd_norm, m_norm_mem, m_w_mem_kv, m_w_sb_out, m_w_ssd_out, m_w_mem_out, m_w_o, m_norm_mix_post, m_norm_mlp_pre, m_w_up, m_w_down, m_norm_mlp_post, v_norm_mix_pre, v_w_in, v_conv_w, v_conv_b, v_dt_bias, v_a_log, v_d_skip, v_ssd_norm, v_norm_mem, v_w_mem_kv, v_w_sb_out, v_w_ssd_out, v_w_mem_out, v_w_o, v_norm_mix_post, v_norm_mlp_pre, v_w_up, v_w_down, v_norm_mlp_post):
    env = dict(locals())
    w = {n: env[n] for n in WEIGHTS}
    mo = {n: env["m_" + n] for n in WEIGHTS}
    vo = {n: env["v_" + n] for n in WEIGHTS}
    shard = 2 * lax.axis_index("x") + lax.axis_index("y")

    first = _gather_two_level([w["w_in"][0].astype(bf16)], "gather_first")
    w_main, w_dt = _to_internal(_full_from_slabs("w_in", first[0]))
    wts = dict(w_main=w_main, w_dt=w_dt)
    ride_names = (LATE[:4], LATE[4:5], LATE[5:], ("w_mem_kv",))
    late_rides = tuple(_Ride([w[n][0].astype(bf16) for n in names] + ([w["conv_w"][0]] if i == 0 else []), "gather")
                       for i, names in enumerate(ride_names))

    def late_weights(i, lands):
        full = {n: _full_from_slabs(n, s) for n, s in zip(ride_names[i], lands)}
        if i == 0:
            full["conv_w"] = lands[-1].transpose(1, 0, 2).reshape(CONV_K, CONV_DIM)
        return full

    def rest_rides(g):
        slabs = [_slabs_from_full(n, g[n]).astype(bf16) for n in REST]
        return _Ride(slabs[5:], "scatter"), _Ride(slabs[:5], "scatter")

    core = lax.axis_index("c")
    half = D // 2

    def w_in_ride(g):
        ordered = _from_internal(g["w_main"], g["w_dt"])
        stack = jnp.stack([_w_in_slab(ordered, s, bf16) for s in range(N_SHARD)])
        keep = lax.dynamic_slice_in_dim(stack, core * half, half, axis=1)
        away = lax.dynamic_slice_in_dim(stack, (1 - core) * half, half, axis=1)
        (got,) = _swap_sibling([away], "w_in_halves_out")
        wide = lambda a: a.reshape(N_SHARD * half, -1)
        chip = _sum_parts(wide(keep), wide(got)[None], "sum_cores_w_in", bf16).reshape(N_SHARD, half, -1)
        own = lax.switch(shard, [functools.partial(_w_in_slab, ordered, s, f32) for s in range(N_SHARD)])
        own = lax.dynamic_slice_in_dim(own, core * half, half, axis=0)
        g["w_in_own"] = _sum_parts(own, lax.dynamic_index_in_dim(got, shard, 0, keepdims=True), "sum_cores_w_in_own")
        return _Ride([chip], "scatter")

    small = {n: w[n] for n in SMALL if n != "conv_w"}
    loss, grad_x, g = _local_step(x[0], mem[0], loss_target[0], wts, late_rides, late_weights, small,
                                  rest_rides, w_in_ride)
    out_g, out_d, out_m, out_v = {}, {}, {}, {}

    def apply(n, g_parts):
        res = _adamw(w[n][0], mo[n][0], vo[n][0], g_parts, name="adamw_" + n)
        out_g[n], out_d[n], out_m[n], out_v[n] = [r[None] for r in res]

    mine = _sum_parts(g["w_in_own"], g["w_in_lands"][0], name="sum_chips_w_in")

    packets = _exchange_packets(_pack([g[n] for n in SMALL] + [loss[:, :1]]))
    partial = []
    for n, r in zip(REST, g["rest_lands"]):
        own = lax.dynamic_index_in_dim(_slabs_from_full(n, g[n]), shard, 0, keepdims=False)
        partial.append(_sum_parts(own, r, name="sum_chips_" + n))
    *other, theirs = _swap_sibling(partial + [mine], "swap_sibling")
    g_w_in = lax.dynamic_update_slice_in_dim(jnp.zeros((D, D_IN // N_SHARD), f32), mine, core * half, axis=0)
    apply("w_in", [lax.dynamic_update_slice_in_dim(g_w_in, theirs, (1 - core) * half, axis=0)])

    for n, p, q in zip(REST, partial, other):
        apply(n, [p, q])
    tot = _sum_parts(None, packets, name="sum_packets")
    shapes = [g[n].shape for n in SMALL] + [(1, 1)]
    sm = dict(zip(SMALL + ("loss",), _unpack(tot, shapes)))
    sm["conv_w"] = lax.dynamic_slice_in_dim(sm["conv_w"], shard * (CONV_DIM // N_SHARD), CONV_DIM // N_SHARD, axis=1)
    own_small = lambda d: _pack([d[n].reshape(sm[n].shape) for n in SMALL])
    res = _adamw(own_small(w), own_small(mo), own_small(vo), [own_small(sm)], name="adamw_small")
    own_shapes = [sm[n].shape for n in SMALL]
    for store, r in zip((out_g, out_d, out_m, out_v), res):
        for n, val in zip(SMALL, _unpack(r, own_shapes)):
            store[n] = val.reshape(w[n].shape)

    outs = [sm["loss"].reshape(()), grad_x[None]]
    for store in (out_g, out_d, out_m, out_v):
        outs += [store[n] for n in WEIGHTS]
    return tuple(outs)
```

```python
import functools

import jax
import jax.numpy as jnp
from jax import lax
from jax.experimental import pallas as pl
from jax.experimental.pallas import tpu as pltpu

f32 = jnp.float32
bf16 = jnp.bfloat16

D = 1024
EPS = 1e-6
SB_HD = 64
SSD_INNER = 2048
SSD_HEADS = 32
SSD_GROUPS = 4
SSD_N = 128
SSD_L = 128
CONV_K = 4
CONV_DIM = 3072
MEM_HEADS = 4
MEM_HD = 256
D_FF = 4096
D_IN = 12320
N_SHARD = 4
N_DEV = 8

P_QKV, P_XBC, P_GATE, P_MEMQ, P_Z, P_DT, P_TOT = 0, 3072, 6144, 9216, 10240, 12288, 12416
R_QKV, R_Z, R_XBC, R_DT, R_MEMQ, R_GATE = (0, 3072), (3072, 5120), (5120, 8192), (8192, 8224), (8224, 9248), (9248, 12320)

ADAM_LR = 0.001
ADAM_B1 = 0.9
ADAM_B2 = 0.999
ADAM_EPS = 1e-08
ADAM_WD = 0.01
ADAM_STEP = 10

VMEM_LIMIT = 56 * 1024 * 1024

NN = (((1,), (0,)), ((), ()))
NT = (((1,), (1,)), ((), ()))
TN = (((0,), (0,)), ((), ()))


def _dot(a, b, dims=NN):
    return lax.dot_general(a, b, dims, preferred_element_type=f32)


def _params(sem=None):
    return pltpu.CompilerParams(dimension_semantics=sem, vmem_limit_bytes=VMEM_LIMIT)


def _sigmoid(x):
    return 1.0 / (1.0 + jnp.exp(-x))


def _split2(x):
    hi = x.astype(bf16)
    lo = (x - hi.astype(f32)).astype(bf16)
    return hi, lo


def _split3(x):
    hi = x.astype(bf16)
    r = x - hi.astype(f32)
    mid = r.astype(bf16)
    lo = (r - mid.astype(f32)).astype(bf16)
    return hi, mid, lo


def _mm(a, b, mode, *, tm, tn, name, out_dtypes=(f32,), epi=None, extras=(), ride=None):
    M = a.shape[1] if mode == "tn" else a.shape[0]
    N = b.shape[0] if mode == "nt" else b.shape[1]
    tm, tn = min(tm, M), min(tn, N)
    if mode == "nn":
        (M, K), N = a.shape, b.shape[1]
        a_spec = pl.BlockSpec((tm, K), lambda i, j: (i, 0))
        b_spec = pl.BlockSpec((K, tn), lambda i, j: (0, j))
        dims = NN
    elif mode == "nt":
        (M, K), N = a.shape, b.shape[0]
        a_spec = pl.BlockSpec((tm, K), lambda i, j: (i, 0))
        b_spec = pl.BlockSpec((tn, K), lambda i, j: (j, 0))
        dims = NT
    else:
        (K, M), N = a.shape, b.shape[1]
        a_spec = pl.BlockSpec((K, tm), lambda i, j: (0, i))
        b_spec = pl.BlockSpec((K, tn), lambda i, j: (0, j))
        dims = TN
    assert M % tm == 0 and N % tn == 0, (name, M, N, tm, tn)
    n_ex, n_out = len(extras), len(out_dtypes)
    n_r = ride.n if ride else 0
    o_spec = pl.BlockSpec((tm, tn), lambda i, j: (i, j))
    grid = (M // tm, N // tn)

    def body(a_ref, b_ref, *rest):
        r_ins = rest[n_ex:n_ex + n_r]
        outs = rest[n_ex + n_r:n_ex + n_r + n_out]
        r_lnd, r_sems = rest[n_ex + n_r + n_out:n_ex + 2 * n_r + n_out], rest[n_ex + 2 * n_r + n_out:]
        i, j = pl.program_id(0), pl.program_id(1)
        if ride:
            pl.when((i == 0) & (j == 0))(lambda: ride.start(r_ins, r_lnd, r_sems))
        acc = _dot(a_ref[...].astype(bf16), b_ref[...].astype(bf16), dims)
        res = (acc,) if epi is None else epi(acc, *[e[...] for e in rest[:n_ex]])
        for o_ref, r in zip(outs, res):
            o_ref[...] = r.astype(o_ref.dtype)
        if ride:
            pl.when((i == grid[0] - 1) & (j == grid[1] - 1))(lambda: ride.finish(r_ins, r_lnd, r_sems))

    out = pl.pallas_call(
        body, name=name, grid=grid,
        in_specs=[a_spec, b_spec] + [o_spec] * n_ex + (ride.in_specs if ride else []),
        out_specs=[o_spec] * n_out + (ride.out_specs if ride else []),
        out_shape=[jax.ShapeDtypeStruct((M, N), dt) for dt in out_dtypes] + (ride.out_shape if ride else []),
        scratch_shapes=ride.scratch if ride else [],
        compiler_params=_params(("arbitrary", "arbitrary") if ride else ("parallel", "parallel")),
    )(a, b, *extras, *(ride.srcs if ride else []))
    if ride:
        return (out[0] if n_out == 1 else out[:n_out]), list(out[n_out:])
    return out[0] if n_out == 1 else out


def _mm_pieces_nt(pieces, b, add, *, tm, tn, name, ride):
    M, N = pieces[0].shape[0], b.shape[0]
    n_p, n_r = len(pieces), (ride.n if ride else 0)
    o_spec = pl.BlockSpec((tm, tn), lambda i, j: (i, j))
    grid = (M // tm, N // tn)

    def body(*refs):
        b_ref, add_ref = refs[n_p:n_p + 2]
        r_ins, o_ref = refs[n_p + 2:n_p + 2 + n_r], refs[n_p + 2 + n_r]
        r_lnd, r_sems = refs[n_p + 3 + n_r:n_p + 3 + 2 * n_r], refs[n_p + 3 + 2 * n_r:]
        i, j = pl.program_id(0), pl.program_id(1)
        if ride:
            pl.when((i == 0) & (j == 0))(lambda: ride.start(r_ins, r_lnd, r_sems))
        acc, off = add_ref[...], 0
        for r in refs[:n_p]:
            acc = acc + _dot(r[...], b_ref[:, off:off + r.shape[1]], NT)
            off += r.shape[1]
        o_ref[...] = acc
        if ride:
            pl.when((i == grid[0] - 1) & (j == grid[1] - 1))(lambda: ride.finish(r_ins, r_lnd, r_sems))

    out = pl.pallas_call(
        body, name=name, grid=grid,
        in_specs=[pl.BlockSpec((tm, p.shape[1]), lambda i, j: (i, 0)) for p in pieces]
        + [pl.BlockSpec((tn, b.shape[1]), lambda i, j: (j, 0)), o_spec] + (ride.in_specs if ride else []),
        out_specs=[o_spec] + (ride.out_specs if ride else []),
        out_shape=[jax.ShapeDtypeStruct((M, N), f32)] + (ride.out_shape if ride else []),
        scratch_shapes=ride.scratch if ride else [],
        compiler_params=_params(("arbitrary", "arbitrary")),
    )(*pieces, b, add, *(ride.srcs if ride else []))
    return out[0], list(out[1:])


def _rms_fwd(x, g, *, name, out_dtype, residual=None, tm=512):
    S, C = x.shape
    tm = min(tm, S)
    has_res = residual is not None

    def body(x_ref, g_ref, *rest):
        xv = x_ref[...]
        r = lax.rsqrt(jnp.mean(xv * xv, axis=1, keepdims=True) + EPS)
        y = xv * r * g_ref[...]
        if has_res:
            y = y + rest[0][...]
        rest[-1][...] = y.astype(out_dtype)

    row = pl.BlockSpec((tm, C), lambda i: (i, 0))
    vec = pl.BlockSpec((1, C), lambda i: (0, 0))
    args = (x, g) + ((residual,) if has_res else ())
    return pl.pallas_call(
        body, name=name, grid=(S // tm,),
        in_specs=[row, vec] + ([row] if has_res else []),
        out_specs=row, out_shape=jax.ShapeDtypeStruct((S, C), out_dtype),
        compiler_params=_params(("parallel",)),
    )(*args)


def _rms_bwd(x, dy, g, *, name, out_dtype, add=None, tm=512):
    S, C = x.shape
    tm = min(tm, S)
    has_add = add is not None

    def body(x_ref, dy_ref, g_ref, *rest):
        dx_ref, dg_ref = rest[-2], rest[-1]
        xv = x_ref[...]
        dyv = dy_ref[...].astype(f32)
        r = lax.rsqrt(jnp.mean(xv * xv, axis=1, keepdims=True) + EPS)
        xh = xv * r
        dxh = dyv * g_ref[...]
        dx = r * (dxh - xh * jnp.mean(dxh * xh, axis=1, keepdims=True))
        if has_add:
            dx = dx + rest[0][...]
        dx_ref[...] = dx.astype(out_dtype)

        @pl.when(pl.program_id(0) == 0)
        def _():
            dg_ref[...] = jnp.zeros_like(dg_ref)

        dg_ref[...] += jnp.sum(dyv * xh, axis=0, keepdims=True)

    row = pl.BlockSpec((tm, C), lambda i: (i, 0))
    vec = pl.BlockSpec((1, C), lambda i: (0, 0))
    args = (x, dy, g) + ((add,) if has_add else ())
    return pl.pallas_call(
        body, name=name, grid=(S // tm,),
        in_specs=[row, row, vec] + ([row] if has_add else []),
        out_specs=[row, vec],
        out_shape=[jax.ShapeDtypeStruct((S, C), out_dtype), jax.ShapeDtypeStruct((1, C), f32)],
        compiler_params=_params(("arbitrary",)),
    )(*args)


SB_T = 128
SB_SPENT = -120.0
SB_QB = 4
SB_TAIL = 3
SB_GROUPS = (4, 2, 1)
SB_GROUPS_BWD = (4, 2, 1)


def _sb_masks():
    lane = lax.broadcasted_iota(jnp.int32, (1, 128), 1)
    m_a = (lane < SB_HD).astype(f32)
    return m_a, 1.0 - m_a


def _chunks(a, n):
    return [a[:, u * SB_T:(u + 1) * SB_T] for u in range(n)]


def _cat(parts, axis):
    return parts[0] if len(parts) == 1 else jnp.concatenate(parts, axis=axis)


def _mask_last(a, n, mask):
    if mask is None:
        return a
    parts = _chunks(a, n)
    return _cat(parts[:-1] + [jnp.where(mask, parts[-1], 0.0)], 1)


def _sb_logits(z, n, mask):
    l1p = jnp.log(1.0 + jnp.exp(-jnp.abs(z)))
    lb = jnp.minimum(z, 0.0) - l1p
    return lb, _mask_last(lb - z, n, mask)


def _by_count(i, most, fn):
    return lax.switch(jnp.minimum(i, most - 1), [functools.partial(fn, n) for n in range(1, most + 1)])


def _chunk_matmul(parts_list, u_mat):
    out = _dot(_cat(parts_list, 0), u_mat)
    return [out[u * SB_T:(u + 1) * SB_T] for u in range(len(parts_list))]


def _chunk_cumsum(lk, n, u_mat):
    hi = lk.astype(bf16)
    lo = (lk - hi.astype(f32)).astype(bf16)
    out = _chunk_matmul(_chunks(hi, n) + _chunks(lo, n), u_mat)
    return [out[u] + out[n + u] for u in range(n)]


def _sb_fwd(proj, S, ride=None):
    nq = S // SB_T
    n_pairs = D // 128
    scale = SB_HD ** -0.5
    n_r = ride.n if ride else 0

    def body(q_ref, k_ref, v_ref, *rest):
        o_ref, t_ref = rest[n_r:n_r + 2]
        step_i = pl.program_id(1)
        if ride:
            pl.when((pl.program_id(0) == 0) & (step_i == 0))(
                lambda: ride.start(rest[:n_r], rest[n_r + 2:2 * n_r + 2], rest[2 * n_r + 2:]))
        m_a, m_b = _sb_masks()
        r_i = lax.broadcasted_iota(jnp.int32, (SB_T, SB_T), 0)
        c_i = lax.broadcasted_iota(jnp.int32, (SB_T, SB_T), 1)
        u_mat = (r_i > c_i).astype(bf16)
        causal = c_i < r_i
        lane_a = lax.broadcasted_iota(jnp.int32, (1, 128), 1) < SB_HD
        q_all = q_ref[...] * scale
        q_hs = [((q * m_a).astype(bf16), (q * m_b).astype(bf16))
                for q in (q_all[b * SB_T:(b + 1) * SB_T] for b in range(SB_QB))]

        def group(q_h, j_lo, n, carry, mask):
            acc, c_a, c_b = carry
            rows = pl.ds(pl.multiple_of(j_lo * SB_T, SB_T), n * SB_T)
            k = k_ref[rows, :].astype(bf16)
            v = v_ref[rows, :]
            zs = [_dot(q_b, k, NT) for q_b in q_h]
            lbk = [_sb_logits(z, n, mask) for z in zs]
            parts = [_chunk_cumsum(lk, n, u_mat) for _, lk in lbk]
            ws, cs = [], []
            for (lb, lk), part, c in zip(lbk, parts, (c_a, c_b)):
                lb_c, lk_c = _chunks(lb, n), _chunks(lk, n)
                w_c = [None] * n
                for u in reversed(range(n)):
                    w_c[u] = jnp.exp(lb_c[u] + c + part[u])
                    c = c + jnp.sum(lk_c[u], axis=1, keepdims=True)
                ws.append(_mask_last(_cat(w_c, 1), n, mask).astype(bf16))
                cs.append(c)
            v_b = v.astype(bf16)
            acc = acc + jnp.where(lane_a, _dot(ws[0], v_b), _dot(ws[1], v_b))
            return acc, cs[0], cs[1]

        zero_c = jnp.zeros((SB_T, 1), f32)
        init = (jnp.zeros((SB_T, 128), f32), zero_c, zero_c)
        blocks = [(step_i * SB_QB + b, q_hs[b]) for b in range(SB_QB)]

        def whole_tails():
            return tuple(group(q_h, i - SB_TAIL + 1, SB_TAIL, init, causal) for i, q_h in blocks)

        def short_tails():
            return tuple(_by_count(i, SB_TAIL, functools.partial(
                lambda n, i, q_h: group(q_h, i - n + 1, n, init, causal), i=i, q_h=q_h)) for i, q_h in blocks)

        carries = lax.cond(step_i * SB_QB >= SB_TAIL - 1, whole_tails, short_tails)

        def spent(cr):
            return (jnp.max(jnp.maximum(cr[1], cr[2])) < SB_SPENT).astype(jnp.int32)

        lane = lax.broadcasted_iota(jnp.int32, (1, 128), 1)
        for b, ((i, q_h), carry) in enumerate(zip(blocks, carries)):
            state = (i - jnp.minimum(i, SB_TAIL - 1), spent(carry), carry)
            for n in SB_GROUPS:
                def step(st, n=n, q_h=q_h):
                    left, _, cr = st
                    cr = group(q_h, left - n, n, cr, None)
                    return left - n, spent(cr), cr

                state = lax.while_loop(lambda st, n=n: (st[0] >= n) & (st[1] == 0), step, state)
            left, _, carry = state
            rows = slice(b * SB_T, (b + 1) * SB_T)
            o_ref[rows, :] = carry[0]
            t_ref[rows, :] = (jnp.where(lane == 0, carry[1], 0.0) + jnp.where(lane == SB_HD, carry[2], 0.0)
                              + jnp.where(lane == 1, left.astype(f32), 0.0))
        if ride:
            pl.when((pl.program_id(0) == n_pairs - 1) & (step_i == nq // SB_QB - 1))(
                lambda: ride.finish(rest[:n_r], rest[n_r + 2:2 * n_r + 2], rest[2 * n_r + 2:]))

    qs = pl.BlockSpec((SB_QB * SB_T, 128), lambda h, i: (i, h))
    out = pl.pallas_call(
        body, name="sb_fwd", grid=(n_pairs, nq // SB_QB),
        in_specs=[qs,
                  pl.BlockSpec((S, 128), lambda h, i: (0, n_pairs + h)),
                  pl.BlockSpec((S, 128), lambda h, i: (0, 2 * n_pairs + h))] + (ride.in_specs if ride else []),
        out_specs=[qs, qs] + (ride.out_specs if ride else []),
        out_shape=[jax.ShapeDtypeStruct((S, D), f32)] * 2 + (ride.out_shape if ride else []),
        scratch_shapes=ride.scratch if ride else [],
        compiler_params=_params(("arbitrary", "arbitrary")),
    )(proj, proj, proj, *(ride.srcs if ride else []))
    return out[0], out[1], list(out[2:])


def _sb_bwd(proj, tot_lk, do, S, ride=None):
    nq = S // SB_T
    n_pairs = D // 128
    scale = SB_HD ** -0.5
    n_r = ride.n if ride else 0

    def body(q_ref, k_ref, v_ref, t_ref, do_ref, *rest):
        dq_ref, dk_ref, dv_ref = rest[n_r:n_r + 3]
        dk_acc, dv_acc = rest[2 * n_r + 3:2 * n_r + 5]
        r_ins, r_lnd, r_sems = rest[:n_r], rest[n_r + 3:2 * n_r + 3], rest[2 * n_r + 5:]
        step_i = pl.program_id(1)
        if ride:
            pl.when((pl.program_id(0) == 0) & (step_i == 0))(lambda: ride.start(r_ins, r_lnd, r_sems))
        m_a, m_b = _sb_masks()
        r_i = lax.broadcasted_iota(jnp.int32, (SB_T, SB_T), 0)
        c_i = lax.broadcasted_iota(jnp.int32, (SB_T, SB_T), 1)
        u_inc = (r_i <= c_i).astype(bf16)
        u_exc = (r_i < c_i).astype(bf16)
        causal = c_i < r_i

        @pl.when(step_i == 0)
        def _():
            dk_acc[...] = jnp.zeros_like(dk_acc)
            dv_acc[...] = jnp.zeros_like(dv_acc)

        lane = lax.broadcasted_iota(jnp.int32, (1, 128), 1)
        blocks = []
        for b in range(SB_QB):
            rows_b = slice(b * SB_T, (b + 1) * SB_T)
            i = step_i * SB_QB + b
            q = q_ref[rows_b, :] * scale
            dov = do_ref[rows_b, :]
            tv = t_ref[rows_b, :]
            heads = []
            for m, first in ((m_a, 0), (m_b, SB_HD)):
                tot = jnp.sum(jnp.where(lane == first, tv, 0.0), axis=1, keepdims=True)
                heads.append(((q * m).astype(bf16), (dov * m).astype(bf16), tot, m))
            lowest = jnp.clip(jnp.max(jnp.where(lane == 1, tv, 0.0)).astype(jnp.int32), 0, i)
            blocks.append((i, heads, lowest))

        def group(heads, j_lo, n, carry, mask):
            dq_acc, cp_a, cp_b, ce_a, ce_b = carry
            rows = pl.ds(pl.multiple_of(j_lo * SB_T, SB_T), n * SB_T)
            k_f = k_ref[rows, :]
            k = k_f.astype(bf16)
            v = v_ref[rows, :].astype(bf16)
            zs = [_dot(h[0], k, NT) for h in heads]
            dws = [_dot(h[1], v, NT) for h in heads]
            lbk = [_sb_logits(z, n, mask) for z in zs]
            parts = [_chunk_cumsum(lk, n, u_inc) for _, lk in lbk]
            ws, es, cps = [], [], []
            for (lb, lk), part, dw, h, cp in zip(lbk, parts, dws, heads, (cp_a, cp_b)):
                lb_c, lk_c = _chunks(lb, n), _chunks(lk, n)
                w_c = []
                for u in range(n):
                    w_c.append(jnp.exp(lb_c[u] + (h[2] - cp) - part[u]))
                    cp = cp + jnp.sum(lk_c[u], axis=1, keepdims=True)
                w = _mask_last(_cat(w_c, 1), n, mask)
                ws.append(w)
                es.append(dw * w)
                cps.append(cp)
            e_parts = [_chunk_matmul(_chunks(e.astype(bf16), n), u_exc) for e in es]
            dzs, ces = [], []
            for (lb, _), e, e_part, ce in zip(lbk, es, e_parts, (ce_a, ce_b)):
                e_c = _chunks(e, n)
                big_c = []
                for u in range(n):
                    big_c.append(ce + e_part[u])
                    ce = ce + jnp.sum(e_c[u], axis=1, keepdims=True)
                sig = jnp.exp(lb)
                dz = _mask_last(e * (1.0 - sig) - _cat(big_c, 1) * sig, n, mask)
                dzs.append(dz.astype(bf16))
                ces.append(ce)
            dk_t = jnp.zeros((n * SB_T, 128), f32)
            dv_t = jnp.zeros((n * SB_T, 128), f32)
            dq_acc = dq_acc + jnp.where(lane < SB_HD, _dot(dzs[0], k), _dot(dzs[1], k))
            for dz_b, w, h in zip(dzs, ws, heads):
                dk_t = dk_t + _dot(dz_b, h[0], TN)
                dv_t = dv_t + _dot(w.astype(bf16), h[1], TN)
            dk_acc[rows, :] += dk_t
            dv_acc[rows, :] += dv_t
            return dq_acc, cps[0], cps[1], ces[0], ces[1]

        zc = jnp.zeros((SB_T, 1), f32)
        carries = []
        for i, heads, lowest in blocks:
            carry = (jnp.zeros((SB_T, 128), f32), zc, zc, zc, zc)
            done = lowest
            tail_lo = i - jnp.minimum(i, SB_TAIL - 1)
            for n in SB_GROUPS_BWD:
                trips = (tail_lo - done) // n
                carry = lax.fori_loop(
                    0, trips, functools.partial(
                        lambda gi, cr, n, done, heads: group(heads, done + gi * n, n, cr, None),
                        n=n, done=done, heads=heads),
                    carry)
                done = done + trips * n
            carries.append(carry)

        def whole_tails():
            return tuple(group(heads, i - SB_TAIL + 1, SB_TAIL, cr, causal)
                         for (i, heads, _), cr in zip(blocks, carries))

        def short_tails():
            return tuple(_by_count(i, SB_TAIL, functools.partial(
                lambda n, i, heads, cr: group(heads, i - n + 1, n, cr, causal), i=i, heads=heads, cr=cr))
                for (i, heads, _), cr in zip(blocks, carries))

        carries = lax.cond(step_i * SB_QB >= SB_TAIL - 1, whole_tails, short_tails)
        for b, carry in enumerate(carries):
            dq_ref[b * SB_T:(b + 1) * SB_T, :] = (carry[0] * scale).astype(bf16)

        @pl.when(step_i == nq // SB_QB - 1)
        def _():
            dk_ref[...] = dk_acc[...].astype(bf16)
            dv_ref[...] = dv_acc[...].astype(bf16)

        if ride:
            pl.when((pl.program_id(0) == n_pairs - 1) & (step_i == nq // SB_QB - 1))(
                lambda: ride.finish(r_ins, r_lnd, r_sems))

    qs = pl.BlockSpec((SB_QB * SB_T, 128), lambda h, i: (i, h))
    full = pl.BlockSpec((S, 128), lambda h, i: (0, h))
    out = pl.pallas_call(
        body, name="sb_bwd", grid=(n_pairs, nq // SB_QB),
        in_specs=[qs,
                  pl.BlockSpec((S, 128), lambda h, i: (0, n_pairs + h)),
                  pl.BlockSpec((S, 128), lambda h, i: (0, 2 * n_pairs + h)),
                  qs, qs] + (ride.in_specs if ride else []),
        out_specs=[qs, full, full] + (ride.out_specs if ride else []),
        out_shape=[jax.ShapeDtypeStruct((S, D), bf16)] * 3 + (ride.out_shape if ride else []),
        scratch_shapes=[pltpu.VMEM((S, 128), f32), pltpu.VMEM((S, 128), f32)] + (ride.scratch if ride else []),
        compiler_params=_params(("arbitrary", "arbitrary")),
    )(proj, proj, proj, tot_lk, do, *(ride.srcs if ride else []))
    return out[0], out[1], out[2], list(out[3:])


CONV_CB = 256
HALO = 8


def _conv_fwd(proj, conv_w, conv_b, S, ride=None):
    tr = min(512, S)
    n_r = ride.n if ride else 0
    n_c = CONV_DIM // CONV_CB

    def body(x_ref, w_ref, b_ref, *rest):
        xc_ref, xbc_ref = rest[n_r:n_r + 2]
        r_ins, r_lnd, r_sems = rest[:n_r], rest[n_r + 2:2 * n_r + 2], rest[2 * n_r + 2:]
        if ride:
            pl.when(pl.program_id(0) == 0)(lambda: ride.start(r_ins, r_lnd, r_sems))
        w = w_ref[...]
        for t in range(S // tr):
            cur = x_ref[t * tr:(t + 1) * tr, :]
            halo = x_ref[t * tr - HALO:t * tr, :] if t else jnp.zeros((HALO, CONV_CB), f32)
            win = jnp.concatenate([halo, cur], axis=0)
            acc = b_ref[...] + w[CONV_K - 1:CONV_K, :] * cur
            for k in range(CONV_K - 1):
                acc = acc + w[k:k + 1, :] * pltpu.roll(win, CONV_K - 1 - k, 0)[HALO:, :]
            xc_ref[t * tr:(t + 1) * tr, :] = acc
            xbc_ref[t * tr:(t + 1) * tr, :] = acc * _sigmoid(acc)
        if ride:
            pl.when(pl.program_id(0) == n_c - 1)(lambda: ride.finish(r_ins, r_lnd, r_sems))

    col = pl.BlockSpec((S, CONV_CB), lambda c: (0, c))
    out = pl.pallas_call(
        body, name="conv_fwd", grid=(n_c,),
        in_specs=[pl.BlockSpec((S, CONV_CB), lambda c: (0, P_XBC // CONV_CB + c)),
                  pl.BlockSpec((CONV_K, CONV_CB), lambda c: (0, c)),
                  pl.BlockSpec((1, CONV_CB), lambda c: (0, c))] + (ride.in_specs if ride else []),
        out_specs=[col, col] + (ride.out_specs if ride else []),
        out_shape=[jax.ShapeDtypeStruct((S, CONV_DIM), f32)] * 2 + (ride.out_shape if ride else []),
        scratch_shapes=ride.scratch if ride else [],
        compiler_params=_params(("arbitrary",)),
    )(proj, conv_w, conv_b, *(ride.srcs if ride else []))
    return out[0], out[1], list(out[2:])


def _conv_bwd(proj, xc, dxbc, conv_w, S):
    tr = min(512, S)

    def body(x_ref, xc_ref, dy_ref, w_ref, dx_ref, dw_ref, db_ref, dxc_s):
        w = w_ref[...]
        xcv = xc_ref[...]
        sg = _sigmoid(xcv)
        dxc_s[0:S, :] = dy_ref[...] * (sg * (1.0 + xcv * (1.0 - sg)))
        dxc_s[S:S + HALO, :] = jnp.zeros((HALO, CONV_CB), f32)
        dws = [jnp.zeros((1, CONV_CB), f32) for _ in range(CONV_K)]
        db = jnp.zeros((1, CONV_CB), f32)
        for t in range(S // tr):
            cur = x_ref[t * tr:(t + 1) * tr, :]
            halo = x_ref[t * tr - HALO:t * tr, :] if t else jnp.zeros((HALO, CONV_CB), f32)
            win = jnp.concatenate([halo, cur], axis=0)
            dwin = dxc_s[t * tr:(t + 1) * tr + HALO, :]
            dcur = dwin[0:tr, :]
            db = db + jnp.sum(dcur, axis=0, keepdims=True)
            dws[CONV_K - 1] = dws[CONV_K - 1] + jnp.sum(dcur * cur, axis=0, keepdims=True)
            dx = w[CONV_K - 1:CONV_K, :] * dcur
            for k in range(CONV_K - 1):
                sh = CONV_K - 1 - k
                dws[k] = dws[k] + jnp.sum(dcur * pltpu.roll(win, sh, 0)[HALO:, :], axis=0, keepdims=True)
                dx = dx + w[k:k + 1, :] * pltpu.roll(dwin, tr + HALO - sh, 0)[0:tr, :]
            dx_ref[t * tr:(t + 1) * tr, :] = dx.astype(bf16)
        dw_ref[...] = jnp.concatenate(dws + [jnp.zeros((8 - CONV_K, CONV_CB), f32)], axis=0)
        db_ref[...] = db

    col = pl.BlockSpec((S, CONV_CB), lambda c: (0, c))
    return pl.pallas_call(
        body, name="conv_bwd", grid=(CONV_DIM // CONV_CB,),
        in_specs=[pl.BlockSpec((S, CONV_CB), lambda c: (0, P_XBC // CONV_CB + c)), col, col,
                  pl.BlockSpec((CONV_K, CONV_CB), lambda c: (0, c))],
        out_specs=[col, pl.BlockSpec((8, CONV_CB), lambda c: (0, c)), pl.BlockSpec((1, CONV_CB), lambda c: (0, c))],
        out_shape=[jax.ShapeDtypeStruct((S, CONV_DIM), bf16), jax.ShapeDtypeStruct((8, CONV_DIM), f32),
                   jax.ShapeDtypeStruct((1, CONV_DIM), f32)],
        scratch_shapes=[pltpu.VMEM((S + HALO, CONV_CB), f32)],
        compiler_params=_params(("parallel",)),
    )(proj, xc, dxbc, conv_w)


N_PAIR = SSD_HEADS // 2
NEG = -1e30


def _softplus(x):
    return jnp.maximum(x, 0.0) + jnp.log(1.0 + jnp.exp(-jnp.abs(x)))


def _ssd_common(dtr, dtb, alog):
    L = SSD_L
    r_i = lax.broadcasted_iota(jnp.int32, (L, L), 0)
    c_i = lax.broadcasted_iota(jnp.int32, (L, L), 1)
    dt = _softplus(dtr + dtb)
    a = -jnp.exp(alog)
    da = dt * a
    lower = (r_i >= c_i).astype(bf16)
    upper = (r_i <= c_i).astype(bf16)
    parts = _split3(da)
    a_cs = sum(_dot(lower, p) for p in parts)
    a_cs_t = sum(_dot(p, upper, TN) for p in parts)
    return dt, a, a_cs, a_cs_t, r_i >= c_i


def _pair_vec(lane, v, h):
    return jnp.where(lane < SB_HD, v[:, h:h + 1], v[:, h + 1:h + 2])


def _decay_mat(a_cs, a_cs_t, h, tril):
    return jnp.exp(jnp.where(tril, a_cs[:, h:h + 1] - a_cs_t[h:h + 1, :], NEG))


def _ssd_fwd(xbc, proj, pdt, dt_bias_p, a_log_p, d_skip_c, ssd_norm, S, ride=None):
    L = SSD_L
    nc = S // L
    n_r = ride.n if ride else 0

    def body(xbc_ref, dt_ref, z_ref, dtb_ref, alog_ref, dsk_ref, gn_ref, *rest):
        y_ref, yn_ref, hp_ref = rest[n_r:n_r + 3]
        state = rest[2 * n_r + 3]
        r_ins, r_lnd, r_sems = rest[:n_r], rest[n_r + 3:2 * n_r + 3], rest[2 * n_r + 4:]
        c = pl.program_id(0)
        if ride:
            pl.when(c == 0)(lambda: ride.start(r_ins, r_lnd, r_sems))

        @pl.when(c == 0)
        def _():
            state[...] = jnp.zeros_like(state)

        hp_ref[0] = state[...]
        lane = lax.broadcasted_iota(jnp.int32, (1, 128), 1)
        row128 = lax.broadcasted_iota(jnp.int32, (128, 1), 0)
        m_a, m_b = _sb_masks()
        dt, a, a_cs, a_cs_t, tril = _ssd_common(dt_ref[...], dtb_ref[...], alog_ref[...])
        a_last = a_cs[L - 1:L, :]
        for g in range(SSD_GROUPS):
            b_g = xbc_ref[:, SSD_INNER + g * SSD_N:SSD_INNER + (g + 1) * SSD_N].astype(bf16)
            c_g = xbc_ref[:, SSD_INNER + (SSD_GROUPS + g) * SSD_N:SSD_INNER + (SSD_GROUPS + g + 1) * SSD_N].astype(bf16)
            cb = _dot(c_g, b_g, NT)
            for pr in range(4):
                h = 8 * g + 2 * pr
                pi = h // 2
                cols = slice(pi * 128, (pi + 1) * 128)
                xs = xbc_ref[:, cols]
                x = xs * _pair_vec(lane, dt, h)
                acs = _pair_vec(lane, a_cs, h)
                al = _pair_vec(lane, a_last, h)
                w_a = (cb * _decay_mat(a_cs, a_cs_t, h, tril)).astype(bf16)
                w_b = (cb * _decay_mat(a_cs, a_cs_t, h + 1, tril)).astype(bf16)
                yd = _dot(w_a, (x * m_a).astype(bf16)) + _dot(w_b, (x * m_b).astype(bf16))
                hp = state[pi]
                yo = _dot(c_g, hp.astype(bf16), NT) * jnp.exp(acs)
                y_ref[:, cols] = yd + yo + dsk_ref[:, cols] * xs
                dec = jnp.exp(jnp.where(row128 < SB_HD, a_last[:, h:h + 1], a_last[:, h + 1:h + 2]))
                state[pi] = hp * dec + _dot((x * jnp.exp(al - acs)).astype(bf16), b_g, TN)
        zz = z_ref[...]
        y2 = y_ref[...] * (zz * _sigmoid(zz))
        gw = SSD_INNER // SSD_GROUPS
        for g in range(SSD_GROUPS):
            yg = y2[:, g * gw:(g + 1) * gw]
            rg = lax.rsqrt(jnp.mean(yg * yg, axis=1, keepdims=True) + EPS)
            yn_ref[:, g * gw:(g + 1) * gw] = (yg * rg * gn_ref[:, g * gw:(g + 1) * gw]).astype(bf16)
        if ride:
            pl.when(c == nc - 1)(lambda: ride.finish(r_ins, r_lnd, r_sems))

    vec128 = pl.BlockSpec((1, 128), lambda c: (0, 0))
    vecin = pl.BlockSpec((1, SSD_INNER), lambda c: (0, 0))
    rows = pl.BlockSpec((L, SSD_INNER), lambda c: (c, 0))
    out = pl.pallas_call(
        body, name="ssd_fwd", grid=(nc,),
        in_specs=[pl.BlockSpec((L, CONV_DIM), lambda c: (c, 0)),
                  pl.BlockSpec((L, 128), lambda c: (c, 0)),
                  pl.BlockSpec((L, SSD_INNER), lambda c: (c, P_Z // SSD_INNER)),
                  vec128, vec128, vecin, vecin] + (ride.in_specs if ride else []),
        out_specs=[rows, rows, pl.BlockSpec((1, N_PAIR, 128, SSD_N), lambda c: (c, 0, 0, 0))]
        + (ride.out_specs if ride else []),
        out_shape=[jax.ShapeDtypeStruct((S, SSD_INNER), f32), jax.ShapeDtypeStruct((S, SSD_INNER), bf16),
                   jax.ShapeDtypeStruct((nc, N_PAIR, 128, SSD_N), f32)] + (ride.out_shape if ride else []),
        scratch_shapes=[pltpu.VMEM((N_PAIR, 128, SSD_N), f32)] + (ride.scratch if ride else []),
        compiler_params=_params(("arbitrary",)),
    )(xbc, pdt, proj, dt_bias_p, a_log_p, d_skip_c, ssd_norm, *(ride.srcs if ride else []))
    return out[0], out[1], out[2], list(out[3:])


def _sum_all(v):
    return jnp.sum(jnp.sum(v, axis=1, keepdims=True), axis=0, keepdims=True)


def _ssd_bwd(dyn, y, xbc, proj, pdt, hprev, dt_bias_p, a_log_p, d_skip_c, ssd_norm, S, ride=None):
    L = SSD_L
    nc = S // L
    n_r = ride.n if ride else 0

    col = lax.broadcasted_iota(jnp.int32, (2 * SSD_INNER, 128), 0)
    head = lax.broadcasted_iota(jnp.int32, (2 * SSD_INNER, 128), 1)
    sel_pair = (col[:SSD_INNER] // SB_HD == head[:SSD_INNER]).astype(bf16)
    sel_head = (col // 128 == head).astype(bf16)

    def body(*refs):
        (dyn_ref, y_ref, xbc_ref, dt_ref, z_ref, hp_ref, dtb_ref, alog_ref, dsk_ref, gn_ref,
         selp_ref, selh_ref) = refs[:12]
        dz_ref, dxbc_ref, ddt_ref, dgn_ref, dsk_out, dalog_ref, ddtb_ref = refs[12 + n_r:19 + n_r]
        dstate, dy_s, st_a, st_q, st_d, st_x, dat = refs[19 + 2 * n_r:26 + 2 * n_r]
        r_ins, r_lnd, r_sems = refs[12:12 + n_r], refs[19 + n_r:19 + 2 * n_r], refs[26 + 2 * n_r:]
        c = pl.program_id(0)
        if ride:
            pl.when(c == 0)(lambda: ride.start(r_ins, r_lnd, r_sems))

        @pl.when(c == 0)
        def _():
            dat[...] = jnp.zeros_like(dat)
            dstate[...] = jnp.zeros_like(dstate)
            dgn_ref[...] = jnp.zeros_like(dgn_ref)
            dsk_out[...] = jnp.zeros_like(dsk_out)
            dalog_ref[...] = jnp.zeros_like(dalog_ref)
            ddtb_ref[...] = jnp.zeros_like(ddtb_ref)

        lane = lax.broadcasted_iota(jnp.int32, (1, 128), 1)
        row128 = lax.broadcasted_iota(jnp.int32, (128, 1), 0)
        rowl = lax.broadcasted_iota(jnp.int32, (L, 1), 0)
        m_a, m_b = _sb_masks()
        dtr = dt_ref[...]
        dt, a, a_cs, a_cs_t, tril = _ssd_common(dtr, dtb_ref[...], alog_ref[...])
        a_last = a_cs[L - 1:L, :]

        zz = z_ref[...]
        sg = _sigmoid(zz)
        silu = zz * sg
        yv = y_ref[...]
        y2 = yv * silu
        gw = SSD_INNER // SSD_GROUPS
        for g in range(SSD_GROUPS):
            sl = slice(g * gw, (g + 1) * gw)
            yg = y2[:, sl]
            rg = lax.rsqrt(jnp.mean(yg * yg, axis=1, keepdims=True) + EPS)
            yh = yg * rg
            dyn_g = dyn_ref[:, sl]
            dgn_ref[:, sl] += jnp.sum(dyn_g * yh, axis=0, keepdims=True)
            dyh = dyn_g * gn_ref[:, sl]
            dy2 = rg * (dyh - yh * jnp.mean(dyh * yh, axis=1, keepdims=True))
            dy_s[:, sl] = dy2 * silu[:, sl]
            dz_ref[:, sl] = (dy2 * yv[:, sl] * (sg[:, sl] * (1.0 + zz[:, sl] * (1.0 - sg[:, sl])))).astype(bf16)

        last_row = jnp.zeros((1, 128), f32)
        dsk_acc = jnp.zeros((1, 128), f32)
        for g in range(SSD_GROUPS):
            bsl = slice(SSD_INNER + g * SSD_N, SSD_INNER + (g + 1) * SSD_N)
            csl = slice(SSD_INNER + (SSD_GROUPS + g) * SSD_N, SSD_INNER + (SSD_GROUPS + g + 1) * SSD_N)
            b_g = xbc_ref[:, bsl].astype(bf16)
            c_g = xbc_ref[:, csl].astype(bf16)
            cb = _dot(c_g, b_g, NT)
            dcb = jnp.zeros((L, L), f32)
            dc_g = jnp.zeros((L, SSD_N), f32)
            db_g = jnp.zeros((L, SSD_N), f32)
            for pr in range(4):
                h = 8 * g + 2 * pr
                pi = h // 2
                cols = slice(pi * 128, (pi + 1) * 128)
                xs = xbc_ref[:, cols]
                dt_p = _pair_vec(lane, dt, h)
                x = xs * dt_p
                acs = _pair_vec(lane, a_cs, h)
                al = _pair_vec(lane, a_last, h)
                e_a = jnp.exp(acs)
                dte = jnp.exp(al - acs)
                m_mat_a = _decay_mat(a_cs, a_cs_t, h, tril)
                m_mat_b = _decay_mat(a_cs, a_cs_t, h + 1, tril)
                dyp = dy_s[:, cols]
                dsk = dsk_ref[:, cols]
                d_hn = dstate[pi]
                hp = hp_ref[0, pi]
                dy_a = (dyp * m_a).astype(bf16)
                dy_b = (dyp * m_b).astype(bf16)
                x_b = x.astype(bf16)
                gm_a = _dot(dy_a, x_b, NT) * m_mat_a
                gm_b = _dot(dy_b, x_b, NT) * m_mat_b
                dcb = dcb + gm_a + gm_b
                dx_d = _dot((cb * m_mat_a).astype(bf16), dy_a, TN) + _dot((cb * m_mat_b).astype(bf16), dy_b, TN)
                dx_s = _dot(b_g, d_hn.astype(bf16), NT) * dte
                dx = dx_d + dx_s
                dxbc_ref[:, cols] = dx * dt_p + dsk * dyp
                xdxs = x * dx_s
                st_x[:, cols] = xdxs
                st_a[:, cols] = dyp * (_dot(c_g, hp.astype(bf16), NT) * e_a) - xdxs
                st_d[:, cols] = dx * xs
                hh = d_hn * hp
                dsk_row = jnp.sum(dyp * xs, axis=0, keepdims=True)
                dec = jnp.exp(jnp.where(row128 < SB_HD, a_last[:, h:h + 1], a_last[:, h + 1:h + 2]))
                for hd, m, gm in ((h, m_a, gm_a), (h + 1, m_b, gm_b)):
                    half = slice(0, SB_HD) if hd == h else slice(SB_HD, 128)
                    qm = gm * cb
                    st_q[:, hd * 128:(hd + 1) * 128] = qm
                    dat[hd:hd + 1, :] = jnp.sum(qm, axis=0, keepdims=True)
                    hh_sum = jnp.sum(jnp.sum(hh[half, :], axis=0, keepdims=True), axis=1, keepdims=True)
                    last_row = jnp.where(lane == hd, jnp.exp(a_last[:, hd:hd + 1]) * hh_sum, last_row)
                    dsk_acc = jnp.where(lane == hd, jnp.sum(dsk_row * m, axis=1, keepdims=True), dsk_acc)
                dye = (dyp * e_a).astype(bf16)
                dc_g = dc_g + _dot(dye, hp.astype(bf16))
                db_g = db_g + _dot((x * dte).astype(bf16), d_hn.astype(bf16))
                dstate[pi] = dec * d_hn + _dot(dye, c_g, TN)
            dcb_b = dcb.astype(bf16)
            dxbc_ref[:, csl] = dc_g + _dot(dcb_b, b_g)
            dxbc_ref[:, bsl] = db_g + _dot(dcb_b, c_g, TN)

        r_i = lax.broadcasted_iota(jnp.int32, (L, L), 0)
        c_i = lax.broadcasted_iota(jnp.int32, (L, L), 1)
        rev = (r_i <= c_i).astype(bf16)

        def head_sums(st, sel, split=_split2):
            return sum(_dot(p, sel[...]) for p in split(st[...]))

        last_row = last_row + jnp.sum(head_sums(st_x, selp_ref), axis=0, keepdims=True)
        d_acs = (head_sums(st_a, selp_ref) + head_sums(st_q, selh_ref, _split3)
                 + jnp.where(rowl == L - 1, last_row, 0.0))
        ddt_x = head_sums(st_d, selp_ref)
        dda = sum(_dot(rev, p) for p in _split3(d_acs)) - sum(_dot(rev, p, NT) for p in _split3(dat[...]))
        ddt = ddt_x + dda * a
        dalog_ref[...] += jnp.sum(dda * dt, axis=0, keepdims=True) * a
        ddtr = jnp.where(lane < SSD_HEADS, ddt * _sigmoid(dtr + dtb_ref[...]), 0.0)
        ddt_ref[...] = ddtr.astype(bf16)
        ddtb_ref[...] += jnp.sum(ddtr, axis=0, keepdims=True)
        dsk_out[...] += dsk_acc
        if ride:
            pl.when(c == nc - 1)(lambda: ride.finish(r_ins, r_lnd, r_sems))

    rv = lambda c: nc - 1 - c
    vec128 = pl.BlockSpec((1, 128), lambda c: (0, 0))
    vecin = pl.BlockSpec((1, SSD_INNER), lambda c: (0, 0))
    rows = pl.BlockSpec((L, SSD_INNER), lambda c: (rv(c), 0))
    return pl.pallas_call(
        body, name="ssd_bwd", grid=(nc,),
        in_specs=[rows, rows,
                  pl.BlockSpec((L, CONV_DIM), lambda c: (rv(c), 0)),
                  pl.BlockSpec((L, 128), lambda c: (rv(c), 0)),
                  pl.BlockSpec((L, SSD_INNER), lambda c: (rv(c), P_Z // SSD_INNER)),
                  pl.BlockSpec((1, N_PAIR, 128, SSD_N), lambda c: (rv(c), 0, 0, 0)),
                  vec128, vec128, vecin, vecin,
                  pl.BlockSpec((SSD_INNER, 128), lambda c: (0, 0)),
                  pl.BlockSpec((2 * SSD_INNER, 128), lambda c: (0, 0))] + (ride.in_specs if ride else []),
        out_specs=[rows, pl.BlockSpec((L, CONV_DIM), lambda c: (rv(c), 0)),
                   pl.BlockSpec((L, 128), lambda c: (rv(c), 0)), vecin, vec128, vec128, vec128]
        + (ride.out_specs if ride else []),
        out_shape=[jax.ShapeDtypeStruct((S, SSD_INNER), bf16), jax.ShapeDtypeStruct((S, CONV_DIM), f32),
                   jax.ShapeDtypeStruct((S, 128), bf16), jax.ShapeDtypeStruct((1, SSD_INNER), f32),
                   jax.ShapeDtypeStruct((1, 128), f32), jax.ShapeDtypeStruct((1, 128), f32),
                   jax.ShapeDtypeStruct((1, 128), f32)] + (ride.out_shape if ride else []),
        scratch_shapes=[pltpu.VMEM((N_PAIR, 128, SSD_N), f32), pltpu.VMEM((L, SSD_INNER), f32),
                        pltpu.VMEM((L, SSD_INNER), f32), pltpu.VMEM((L, 2 * SSD_INNER), f32),
                        pltpu.VMEM((L, SSD_INNER), f32), pltpu.VMEM((L, SSD_INNER), f32),
                        pltpu.VMEM((128, L), f32)]
        + (ride.scratch if ride else []),
        compiler_params=_params(("arbitrary",)),
    )(dyn, y, xbc, pdt, proj, hprev, dt_bias_p, a_log_p, d_skip_c, ssd_norm, sel_pair, sel_head,
      *(ride.srcs if ride else []))


MEM_W = MEM_HEADS * MEM_HD


def _mem_probs(q, k):
    s = _dot(q, k, NT) * (MEM_HD ** -0.5)
    s = s - jnp.max(s, axis=1, keepdims=True)
    p = jnp.exp(s)
    return p / jnp.sum(p, axis=1, keepdims=True)


def _mem_fwd(proj, kv, S, tm=512):
    tm = min(tm, S)
    M = kv.shape[0]

    def body(q_ref, kv_ref, o_ref):
        for h in range(MEM_HEADS):
            sl = slice(h * MEM_HD, (h + 1) * MEM_HD)
            vsl = slice(MEM_W + h * MEM_HD, MEM_W + (h + 1) * MEM_HD)
            p = _mem_probs(q_ref[:, sl].astype(bf16), kv_ref[:, sl].astype(bf16))
            o_ref[:, sl] = _dot(p.astype(bf16), kv_ref[:, vsl].astype(bf16)).astype(bf16)

    return pl.pallas_call(
        body, name="mem_fwd", grid=(S // tm,),
        in_specs=[pl.BlockSpec((tm, MEM_W), lambda i: (i, P_MEMQ // MEM_W)),
                  pl.BlockSpec((M, 2 * MEM_W), lambda i: (0, 0))],
        out_specs=pl.BlockSpec((tm, MEM_W), lambda i: (i, 0)),
        out_shape=jax.ShapeDtypeStruct((S, MEM_W), bf16),
        compiler_params=_params(("parallel",)),
    )(proj, kv)


def _mem_bwd(proj, kv, dy, S, tm=512):
    tm = min(tm, S)
    M = kv.shape[0]
    scale = MEM_HD ** -0.5

    def body(q_ref, kv_ref, dy_ref, dq_ref, dkv_ref):
        @pl.when(pl.program_id(0) == 0)
        def _():
            dkv_ref[...] = jnp.zeros_like(dkv_ref)

        for h in range(MEM_HEADS):
            sl = slice(h * MEM_HD, (h + 1) * MEM_HD)
            vsl = slice(MEM_W + h * MEM_HD, MEM_W + (h + 1) * MEM_HD)
            q = q_ref[:, sl].astype(bf16)
            k = kv_ref[:, sl].astype(bf16)
            v = kv_ref[:, vsl].astype(bf16)
            dyh = dy_ref[:, sl].astype(bf16)
            p = _mem_probs(q, k)
            dp = _dot(dyh, v, NT)
            ds = (p * (dp - jnp.sum(dp * p, axis=1, keepdims=True)) * scale).astype(bf16)
            dq_ref[:, sl] = _dot(ds, k).astype(bf16)
            dkv_ref[:, sl] += _dot(ds, q, TN)
            dkv_ref[:, vsl] += _dot(p.astype(bf16), dyh, TN)

    return pl.pallas_call(
        body, name="mem_bwd", grid=(S // tm,),
        in_specs=[pl.BlockSpec((tm, MEM_W), lambda i: (i, P_MEMQ // MEM_W)),
                  pl.BlockSpec((M, 2 * MEM_W), lambda i: (0, 0)),
                  pl.BlockSpec((tm, MEM_W), lambda i: (i, 0))],
        out_specs=[pl.BlockSpec((tm, MEM_W), lambda i: (i, 0)), pl.BlockSpec((M, 2 * MEM_W), lambda i: (0, 0))],
        out_shape=[jax.ShapeDtypeStruct((S, MEM_W), bf16), jax.ShapeDtypeStruct((M, 2 * MEM_W), f32)],
        compiler_params=_params(("arbitrary",)),
    )(proj, kv, dy)


def _merge_fwd(proj, t0, t1, t2, S, tm=512):
    tm = min(tm, S)

    def body(g_ref, t0_ref, t1_ref, t2_ref, o_ref):
        acc = jnp.zeros((tm, D), f32)
        for b, t_ref in enumerate((t0_ref, t1_ref, t2_ref)):
            acc = acc + _sigmoid(g_ref[:, b * D:(b + 1) * D]) * t_ref[...]
        o_ref[...] = acc.astype(bf16)

    row = pl.BlockSpec((tm, D), lambda i: (i, 0))
    return pl.pallas_call(
        body, name="merge_fwd", grid=(S // tm,),
        in_specs=[pl.BlockSpec((tm, 3 * D), lambda i: (i, P_GATE // (3 * D))), row, row, row],
        out_specs=row, out_shape=jax.ShapeDtypeStruct((S, D), bf16),
        compiler_params=_params(("parallel",)),
    )(proj, t0, t1, t2)


def _merge_bwd(proj, t0, t1, t2, dm, S, tm=512):
    tm = min(tm, S)

    def body(g_ref, t0_ref, t1_ref, t2_ref, dm_ref, d0_ref, d1_ref, d2_ref, dg_ref):
        dmv = dm_ref[...]
        for b, (t_ref, d_ref) in enumerate(((t0_ref, d0_ref), (t1_ref, d1_ref), (t2_ref, d2_ref))):
            sg = _sigmoid(g_ref[:, b * D:(b + 1) * D])
            d_ref[...] = (dmv * sg).astype(bf16)
            dg_ref[:, b * D:(b + 1) * D] = (dmv * t_ref[...] * sg * (1.0 - sg)).astype(bf16)

    row = pl.BlockSpec((tm, D), lambda i: (i, 0))
    return pl.pallas_call(
        body, name="merge_bwd", grid=(S // tm,),
        in_specs=[pl.BlockSpec((tm, 3 * D), lambda i: (i, P_GATE // (3 * D))), row, row, row, row],
        out_specs=[row, row, row, pl.BlockSpec((tm, 3 * D), lambda i: (i, 0))],
        out_shape=[jax.ShapeDtypeStruct((S, D), bf16)] * 3 + [jax.ShapeDtypeStruct((S, 3 * D), bf16)],
        compiler_params=_params(("parallel",)),
    )(proj, t0, t1, t2, dm)


def _loss_head(ff, g, h1, target, S, tm=512):
    tm = min(tm, S)

    def body(ff_ref, g_ref, h1_ref, t_ref, dh_ref, loss_ref):
        xv = ff_ref[...]
        r = lax.rsqrt(jnp.mean(xv * xv, axis=1, keepdims=True) + EPS)
        err = h1_ref[...] + xv * r * g_ref[...] - t_ref[...]
        dh_ref[...] = err * (1.0 / D)

        @pl.when(pl.program_id(0) == 0)
        def _():
            loss_ref[...] = jnp.zeros_like(loss_ref)

        loss_ref[...] += 0.5 * _sum_all(jnp.mean(err * err, axis=1, keepdims=True)) * jnp.ones((1, 128), f32)

    row = pl.BlockSpec((tm, D), lambda i: (i, 0))
    return pl.pallas_call(
        body, name="loss_head", grid=(S // tm,),
        in_specs=[row, pl.BlockSpec((1, D), lambda i: (0, 0)), row, row],
        out_specs=[row, pl.BlockSpec((1, 128), lambda i: (0, 0))],
        out_shape=[jax.ShapeDtypeStruct((S, D), f32), jax.ShapeDtypeStruct((1, 128), f32)],
        compiler_params=_params(("arbitrary",)),
    )(ff, g, h1, target)


def _local_step(x, mem, target, wts, late_rides, late_weights, small, rest_rides, w_in_ride):
    S = x.shape[0]
    M = mem.shape[0]
    pad = lambda v: jnp.pad(v, ((0, 0), (0, 128 - SSD_HEADS)))
    dtb_p, alog_p = pad(small["dt_bias"]), pad(small["a_log"])
    dsk_c = jnp.repeat(small["d_skip"], SB_HD, axis=1)

    u = _rms_fwd(x, small["norm_mix_pre"], name="norm_pre", out_dtype=bf16)
    rides = late_rides or (None, None, None, None)
    if late_rides:
        proj, lands_a = _mm(u, wts["w_main"], "nn", tm=1024, tn=1024, name="in_proj", ride=rides[0])
    else:
        proj, lands_a = _mm(u, wts["w_main"], "nn", tm=1024, tn=1024, name="in_proj"), []
    pdt = _mm(u, wts["w_dt"], "nn", tm=1024, tn=128, name="in_proj_dt")
    y_sb, tot_lk, lands_b = _sb_fwd(proj, S, rides[1])
    wts = dict(wts, **late_weights(0, lands_a))
    small = dict(small, conv_w=wts.pop("conv_w"))
    xc, xbc, lands_d = _conv_fwd(proj, small["conv_w"], small["conv_b"], S, rides[3])
    y_ssd, yn, hprev, lands_c = _ssd_fwd(xbc, proj, pdt, dtb_p, alog_p, dsk_c, small["ssd_norm"], S, rides[2])
    wts = dict(wts, **late_weights(1, lands_b), **late_weights(2, lands_c), **late_weights(3, lands_d))
    mn = _rms_fwd(mem, small["norm_mem"], name="norm_mem", out_dtype=bf16, tm=min(512, M))
    kv = _mm(mn, wts["w_mem_kv"], "nn", tm=M, tn=1024, name="mem_kv")
    y_mem = _mem_fwd(proj, kv, S)
    t0 = _mm(y_sb, wts["w_sb_out"], "nn", tm=1024, tn=1024, name="sb_out")
    t1 = _mm(yn, wts["w_ssd_out"], "nn", tm=1024, tn=1024, name="ssd_out")
    t2 = _mm(y_mem, wts["w_mem_out"], "nn", tm=1024, tn=1024, name="mem_out")
    merged = _merge_fwd(proj, t0, t1, t2, S)
    mix = _mm(merged, wts["w_o"], "nn", tm=1024, tn=1024, name="w_o")
    h1 = _rms_fwd(mix, small["norm_mix_post"], name="norm_mix_post", out_dtype=f32, residual=x)
    u2 = _rms_fwd(h1, small["norm_mlp_pre"], name="norm_mlp_pre", out_dtype=bf16)
    a_up, hrelu = _mm(u2, wts["w_up"], "nn", tm=1024, tn=1024, name="mlp_up", out_dtypes=(f32, bf16),
                      epi=lambda acc: (acc, jnp.square(jnp.maximum(acc, 0.0))))
    ff = _mm(hrelu, wts["w_down"], "nn", tm=1024, tn=1024, name="mlp_down")
    dh2, loss = _loss_head(ff, small["norm_mlp_post"], h1, target, S)

    g = {}
    dff, g["norm_mlp_post"] = _rms_bwd(ff, dh2, small["norm_mlp_post"], name="norm_mlp_post_bwd", out_dtype=bf16)
    da = _mm(dff, wts["w_down"], "nt", tm=1024, tn=1024, name="mlp_down_dx", out_dtypes=(bf16,),
             epi=lambda acc, a: (acc * (2.0 * jnp.maximum(a, 0.0)),), extras=(a_up,))
    g["w_down"] = _mm(hrelu, dff, "tn", tm=1024, tn=1024, name="mlp_down_dw")
    du2 = _mm(da, wts["w_up"], "nt", tm=1024, tn=1024, name="mlp_up_dx")
    g["w_up"] = _mm(u2, da, "tn", tm=1024, tn=1024, name="mlp_up_dw")
    dh1, g["norm_mlp_pre"] = _rms_bwd(h1, du2, small["norm_mlp_pre"], name="norm_mlp_pre_bwd", out_dtype=f32, add=dh2)
    dmix, g["norm_mix_post"] = _rms_bwd(mix, dh1, small["norm_mix_post"], name="norm_mix_post_bwd", out_dtype=bf16)
    dmerged = _mm(dmix, wts["w_o"], "nt", tm=1024, tn=1024, name="w_o_dx")
    g["w_o"] = _mm(merged, dmix, "tn", tm=1024, tn=1024, name="w_o_dw")
    dt0, dt1, dt2, dgl = _merge_bwd(proj, t0, t1, t2, dmerged, S)
    dy_sb = _mm(dt0, wts["w_sb_out"], "nt", tm=1024, tn=1024, name="sb_out_dx")
    g["w_sb_out"] = _mm(y_sb, dt0, "tn", tm=1024, tn=1024, name="sb_out_dw")
    dy_ssd = _mm(dt1, wts["w_ssd_out"], "nt", tm=1024, tn=1024, name="ssd_out_dx")
    g["w_ssd_out"] = _mm(yn, dt1, "tn", tm=1024, tn=1024, name="ssd_out_dw")
    dy_mem = _mm(dt2, wts["w_mem_out"], "nt", tm=1024, tn=1024, name="mem_out_dx")
    g["w_mem_out"] = _mm(y_mem, dt2, "tn", tm=1024, tn=1024, name="mem_out_dw")
    dmemq, dkv = _mem_bwd(proj, kv, dy_mem, S)
    g["w_mem_kv"] = _mm(mn, dkv, "tn", tm=1024, tn=1024, name="mem_kv_dw")
    dmn = _mm(dkv, wts["w_mem_kv"], "nt", tm=M, tn=1024, name="mem_kv_dx")
    _, g["norm_mem"] = _rms_bwd(mem, dmn, small["norm_mem"], name="norm_mem_bwd", out_dtype=bf16, tm=min(512, M))
    rides = rest_rides(g) if rest_rides else (None, None)
    dz, dxbc, ddt, g["ssd_norm"], dsk, dalog, ddtb, *lands_a = _ssd_bwd(
        dy_ssd, y_ssd, xbc, proj, pdt, hprev, dtb_p, alog_p, dsk_c, small["ssd_norm"], S, rides[0])
    g["d_skip"], g["a_log"], g["dt_bias"] = dsk[:, :SSD_HEADS], dalog[:, :SSD_HEADS], ddtb[:, :SSD_HEADS]
    dxbc_raw, dcw, g["conv_b"] = _conv_bwd(proj, xc, dxbc, small["conv_w"], S)
    g["conv_w"] = dcw[:CONV_K]
    dq, dk, dv, lands_b = _sb_bwd(proj, tot_lk, dy_sb, S, rides[1])
    g["rest_lands"] = lands_b + lands_a
    dproj = (dq, dk, dv, dxbc_raw, dgl, dmemq, dz)
    u_t = u.T
    g["w_main"] = [_mm(u_t, p, "nn", tm=512, tn=1024, name="in_proj_dw_%d" % i) for i, p in enumerate(dproj)]
    g["w_dt"] = _mm(u_t, ddt, "nn", tm=512, tn=128, name="in_proj_dt_dw")
    du_dt = _mm(ddt, wts["w_dt"], "nt", tm=1024, tn=1024, name="in_proj_dt_dx")
    du, g["w_in_lands"] = _mm_pieces_nt(dproj, wts["w_main"], du_dt, tm=512, tn=256, name="in_proj_dx",
                                        ride=w_in_ride(g) if w_in_ride else None)
    grad_x, g["norm_mix_pre"] = _rms_bwd(x, du, small["norm_mix_pre"], name="norm_pre_bwd", out_dtype=f32, add=dh1)
    return loss, grad_x, g


def _to_internal(w_in):
    sec = lambda r: w_in[:, r[0]:r[1]]
    w_main = jnp.concatenate([sec(R_QKV), sec(R_XBC), sec(R_GATE), sec(R_MEMQ), sec(R_Z)], axis=1)
    w_dt = jnp.pad(sec(R_DT), ((0, 0), (0, 128 - SSD_HEADS)))
    return w_main, w_dt


def _from_internal(pieces, g_dt):
    dq, dk, dv, dxbc, dgate, dmemq, dz = pieces
    return [dq, dk, dv, dz, dxbc, g_dt[:, :SSD_HEADS], dmemq, dgate]


def _w_in_slab(ordered, s, dtype):
    width = D_IN // N_SHARD
    lo, hi, off, parts = s * width, (s + 1) * width, 0, []
    for p in ordered:
        a, b = max(lo, off), min(hi, off + p.shape[1])
        if a < b:
            parts.append(p[:, a - off:b - off].astype(dtype))
        off += p.shape[1]
    return jnp.concatenate(parts, axis=1)


MESH = pl.DeviceIdType.MESH
ANY = pl.BlockSpec(memory_space=pl.ANY)


def _place():
    x, y, c = lax.axis_index("x"), lax.axis_index("y"), lax.axis_index("c")
    return (x, y, c), [(1 - x, y, c), (x, 1 - y, c), (1 - x, 1 - y, c)]


def _exchange_copy(mode, ins, lands, send, recv, a, k, me, peers, arriving):
    p = peers[k]
    theirs = 2 * p[0] + p[1]
    if mode == "gather":
        src, dst = ins[a], lands[a].at[theirs if arriving else me]
    else:
        src, dst = ins[a].at[theirs], lands[a].at[k]
    return pltpu.make_async_remote_copy(src_ref=src, dst_ref=dst, send_sem=send.at[a * 3 + k],
                                        recv_sem=recv.at[a * 3 + k], device_id=p, device_id_type=MESH)


class _Ride:
    def __init__(self, srcs, mode):
        self.srcs, self.mode, self.n = list(srcs), mode, len(srcs)
        n = self.n
        self.in_specs, self.out_specs = [ANY] * n, [ANY] * n
        self.out_shape = [
            jax.ShapeDtypeStruct((N_SHARD,) + s.shape if mode == "gather" else (3,) + s.shape[1:], s.dtype)
            for s in self.srcs]
        self.scratch = [pltpu.SemaphoreType.DMA((3 * n,)), pltpu.SemaphoreType.DMA((3 * n,)),
                        pltpu.SemaphoreType.DMA((n,))]

    def _own(self, ins, lnd, sems):
        if self.mode != "gather":
            return []
        me = 2 * lax.axis_index("x") + lax.axis_index("y")
        return [pltpu.make_async_copy(ins[a], lnd[a].at[me], sems[2].at[a]) for a in range(self.n)]

    def _far(self, ins, lnd, sems, arriving):
        (x, y, c), peers = _place()
        return [_exchange_copy(self.mode, ins, lnd, sems[0], sems[1], a, k, 2 * x + y, peers, arriving)
                for a in range(self.n) for k in range(3)]

    def start(self, ins, lnd, sems):
        for cp in self._own(ins, lnd, sems) + self._far(ins, lnd, sems, False):
            cp.start()

    def finish(self, ins, lnd, sems):
        for cp in self._far(ins, lnd, sems, True):
            cp.wait_recv()
        for cp in self._far(ins, lnd, sems, False):
            cp.wait_send()
        for cp in self._own(ins, lnd, sems):
            cp.wait()


def _gather_two_level(shards, name):
    n = len(shards)

    def body(*refs):
        ins, lnd = refs[:n], refs[n:2 * n]
        send, recv, loc = refs[2 * n:]
        (x, y, c), peers = _place()
        me = 2 * x + y

        def half(ref, a, core):
            rows = shards[a].shape[0] // 2
            return ref.at[pl.ds(core * rows, rows)]

        def copy(a, j, slot, core, to):
            return pltpu.make_async_remote_copy(
                src_ref=half(ins[a], a, core) if j < 3 else half(lnd[a].at[slot], a, core),
                dst_ref=half(lnd[a].at[slot], a, core), send_sem=send.at[6 * a + j], recv_sem=recv.at[6 * a + j],
                device_id=to, device_id_type=MESH)

        own = [pltpu.make_async_copy(ins[a], lnd[a].at[me], loc.at[a]) for a in range(n)]
        far = [copy(a, k, me, c, peers[k]) for a in range(n) for k in range(3)]
        for cp in own + far:
            cp.start()
        passed = []
        for a in range(n):
            for k, p in enumerate(peers):
                theirs = 2 * p[0] + p[1]
                copy(a, k, theirs, c, p).wait_recv()
                passed.append(copy(a, 3 + k, theirs, c, (x, y, 1 - c)))
                passed[-1].start()
        for a in range(n):
            for k, p in enumerate(peers):
                copy(a, 3 + k, 2 * p[0] + p[1], 1 - c, (x, y, 1 - c)).wait_recv()
        for cp in far + passed:
            cp.wait_send()
        for cp in own:
            cp.wait()

    return pl.pallas_call(
        body, name=name, in_specs=[ANY] * n, out_specs=[ANY] * n,
        out_shape=[jax.ShapeDtypeStruct((N_SHARD,) + s.shape, s.dtype) for s in shards],
        scratch_shapes=[pltpu.SemaphoreType.DMA((6 * n,)), pltpu.SemaphoreType.DMA((6 * n,)),
                        pltpu.SemaphoreType.DMA((n,))],
    )(*shards)


def _exchange_packets(packet):
    def body(pk, pk_out, send, recv, loc):
        x, y, c = lax.axis_index("x"), lax.axis_index("y"), lax.axis_index("c")
        lin = 4 * x + 2 * y + c
        own = pltpu.make_async_copy(pk, pk_out.at[lin], loc.at[0])
        own.start()

        def pk_copy(m, slot):
            dev = (x ^ ((m >> 2) & 1), y ^ ((m >> 1) & 1), c ^ (m & 1))
            return pltpu.make_async_remote_copy(
                src_ref=pk, dst_ref=pk_out.at[slot], send_sem=send.at[m - 1], recv_sem=recv.at[m - 1],
                device_id=dev, device_id_type=MESH)

        sent = [pk_copy(m, lin) for m in range(1, N_DEV)]
        for cp in sent:
            cp.start()
        for m in range(1, N_DEV):
            pk_copy(m, lin ^ m).wait_recv()
        for cp in sent:
            cp.wait_send()
        own.wait()

    return pl.pallas_call(
        body, name="exchange_packets", in_specs=[ANY], out_specs=ANY,
        out_shape=jax.ShapeDtypeStruct((N_DEV,) + packet.shape, packet.dtype),
        scratch_shapes=[pltpu.SemaphoreType.DMA((N_DEV - 1,)), pltpu.SemaphoreType.DMA((N_DEV - 1,)),
                        pltpu.SemaphoreType.DMA((1,))],
    )(packet)


def _swap_sibling(parts, name):
    n = len(parts)

    def body(*refs):
        ins, outs = refs[:n], refs[n:2 * n]
        send, recv = refs[2 * n:]
        x, y, c = lax.axis_index("x"), lax.axis_index("y"), lax.axis_index("c")
        cps = [pltpu.make_async_remote_copy(
            src_ref=ins[a], dst_ref=outs[a], send_sem=send.at[a], recv_sem=recv.at[a],
            device_id=(x, y, 1 - c), device_id_type=MESH) for a in range(n)]
        for cp in cps:
            cp.start()
        for cp in cps:
            cp.wait_recv()
        for cp in cps:
            cp.wait_send()

    return pl.pallas_call(
        body, name=name,
        in_specs=[ANY] * n, out_specs=[ANY] * n,
        out_shape=[jax.ShapeDtypeStruct(p.shape, p.dtype) for p in parts],
        scratch_shapes=[pltpu.SemaphoreType.DMA((n,)), pltpu.SemaphoreType.DMA((n,))],
    )(*parts)


BLOCK_ELEMS = 256 * 1024


def _row_tile(R, C):
    tr = max(8, (BLOCK_ELEMS // C) // 8 * 8)
    while R % tr:
        tr -= 8
    return min(tr, R)


def _sum_parts(own, stack, name, out_dtype=f32):
    k = stack.shape[0]
    R, C = stack.shape[1:]
    tr = _row_tile(R, C)

    def body(*refs):
        o_ref = refs[-1]
        acc = refs[0][...].astype(f32)
        for r in refs[1:-1]:
            acc = acc + r[...].astype(f32)
        o_ref[...] = acc.astype(out_dtype)

    row = pl.BlockSpec((tr, C), lambda i: (i, 0))
    specs = ([row] if own is not None else []) + [
        pl.BlockSpec((None, tr, C), functools.partial(lambda i, j: (j, i, 0), j=j)) for j in range(k)]
    args = ([own] if own is not None else []) + [stack] * k
    return pl.pallas_call(
        body, name=name, grid=(R // tr,), in_specs=specs, out_specs=row,
        out_shape=jax.ShapeDtypeStruct((R, C), out_dtype), compiler_params=_params(("parallel",)),
    )(*args)


def _adamw(w, m, v, g_parts, name):
    R, C = w.shape
    tr = _row_tile(R, C)
    n_g = len(g_parts)

    def body(w_ref, m_ref, v_ref, *rest):
        g = rest[0][...]
        for r in rest[1:n_g]:
            g = g + r[...]
        g_ref, d_ref, nm_ref, nv_ref = rest[n_g:]
        nm = ADAM_B1 * m_ref[...] + (1.0 - ADAM_B1) * g
        nv = ADAM_B2 * v_ref[...] + (1.0 - ADAM_B2) * jnp.square(g)
        m_hat = nm / (1.0 - ADAM_B1 ** ADAM_STEP)
        v_hat = nv / (1.0 - ADAM_B2 ** ADAM_STEP)
        g_ref[...] = g
        d_ref[...] = -ADAM_LR * (m_hat / (jnp.sqrt(v_hat) + ADAM_EPS) + ADAM_WD * w_ref[...])
        nm_ref[...] = nm
        nv_ref[...] = nv

    row = pl.BlockSpec((tr, C), lambda i: (i, 0))
    return pl.pallas_call(
        body, name=name, grid=(R // tr,), in_specs=[row] * (3 + n_g), out_specs=[row] * 4,
        out_shape=[jax.ShapeDtypeStruct((R, C), f32)] * 4, compiler_params=_params(("parallel",)),
    )(w, m, v, *g_parts)


BIG = ("w_in", "w_mem_kv", "w_sb_out", "w_ssd_out", "w_mem_out", "w_o", "w_up", "w_down")
LATE = ("w_sb_out", "w_ssd_out", "w_mem_out", "w_o", "w_up", "w_down")
REST = BIG[1:]
COL_SHARDED = ("w_in", "w_mem_kv", "w_up")
SMALL = ("norm_mix_pre", "conv_w", "conv_b", "dt_bias", "a_log", "d_skip", "ssd_norm", "norm_mem",
         "norm_mix_post", "norm_mlp_pre", "norm_mlp_post")
WEIGHTS = ("norm_mix_pre", "w_in", "conv_w", "conv_b", "dt_bias", "a_log", "d_skip", "ssd_norm", "norm_mem",
           "w_mem_kv", "w_sb_out", "w_ssd_out", "w_mem_out", "w_o", "norm_mix_post", "norm_mlp_pre", "w_up",
           "w_down", "norm_mlp_post")
PK_ROWS = 184


def _pack(vecs):
    flat = jnp.concatenate([v.reshape(-1) for v in vecs])
    return jnp.pad(flat, (0, PK_ROWS * 128 - flat.shape[0])).reshape(PK_ROWS, 128)


def _unpack(pk, shapes):
    flat = pk.reshape(-1)
    out, off = [], 0
    for s in shapes:
        n = 1
        for d in s:
            n *= d
        out.append(flat[off:off + n].reshape(s))
        off += n
    return out


def _full_from_slabs(name, slabs):
    if name in COL_SHARDED:
        return slabs.transpose(1, 0, 2).reshape(slabs.shape[1], -1)
    return slabs.reshape(-1, slabs.shape[2])


def _slabs_from_full(name, g):
    if name in COL_SHARDED:
        return g.reshape(g.shape[0], N_SHARD, -1).transpose(1, 0, 2)
    return g.reshape(N_SHARD, -1, g.shape[1])


def kernel(x, mem, norm_mix_pre, w_in, conv_w, conv_b, dt_bias, a_log, d_skip, ssd_norm, norm_mem, w_mem_kv, w_sb_out, w_ssd_out, w_mem_out, w_o, norm_mix_post, norm_mlp_pre, w_up, w_down, norm_mlp_post, loss_target, m_norm_mix_pre, m_w_in, m_conv_w, m_conv_b, m_dt_bias, m_a_log, m_d_skip, m_ssd_norm, m_norm_mem, m_w_mem_kv, m_w_sb_out, m_w_ssd_out, m_w_mem_out, m_w_o, m_norm_mix_post, m_norm_mlp_pre, m_w_up, m_w_down, m_norm_mlp_post, v_norm_mix_pre, v_w_in, v_conv_w, v_conv_b, v_dt_bias, v_a_log, v_d_skip, v_ssd_norm, v_norm_mem, v_w_mem_kv, v_w_sb_out, v_w_ssd_out, v_w_mem_out, v_w_o, v_norm_mix_post, v_norm_mlp_pre, v_w_up, v_w_down, v_norm_mlp_post):
    env = dict(locals())
    w = {n: env[n] for n in WEIGHTS}
    mo = {n: env["m_" + n] for n in WEIGHTS}
    vo = {n: env["v_" + n] for n in WEIGHTS}
    shard = 2 * lax.axis_index("x") + lax.axis_index("y")

    first = _gather_two_level([w["w_in"][0].astype(bf16)], "gather_first")
    w_main, w_dt = _to_internal(_full_from_slabs("w_in", first[0]))
    wts = dict(w_main=w_main, w_dt=w_dt)
    ride_names = (LATE[:4], LATE[4:5], LATE[5:], ("w_mem_kv",))
    late_rides = tuple(_Ride([w[n][0].astype(bf16) for n in names] + ([w["conv_w"][0]] if i == 0 else []), "gather")
                       for i, names in enumerate(ride_names))

    def late_weights(i, lands):
        full = {n: _full_from_slabs(n, s) for n, s in zip(ride_names[i], lands)}
        if i == 0:
            full["conv_w"] = lands[-1].transpose(1, 0, 2).reshape(CONV_K, CONV_DIM)
        return full

    def rest_rides(g):
        slabs = [_slabs_from_full(n, g[n]).astype(bf16) for n in REST]
        return _Ride(slabs[5:], "scatter"), _Ride(slabs[:5], "scatter")

    core = lax.axis_index("c")
    half = D // 2

    def w_in_ride(g):
        ordered = _from_internal(g["w_main"], g["w_dt"])
        stack = jnp.stack([_w_in_slab(ordered, s, bf16) for s in range(N_SHARD)])
        keep = lax.dynamic_slice_in_dim(stack, core * half, half, axis=1)
        away = lax.dynamic_slice_in_dim(stack, (1 - core) * half, half, axis=1)
        (got,) = _swap_sibling([away], "w_in_halves_out")
        wide = lambda a: a.reshape(N_SHARD * half, -1)
        chip = _sum_parts(wide(keep), wide(got)[None], "sum_cores_w_in", bf16).reshape(N_SHARD, half, -1)
        own = lax.switch(shard, [functools.partial(_w_in_slab, ordered, s, f32) for s in range(N_SHARD)])
        own = lax.dynamic_slice_in_dim(own, core * half, half, axis=0)
        g["w_in_own"] = _sum_parts(own, lax.dynamic_index_in_dim(got, shard, 0, keepdims=True), "sum_cores_w_in_own")
        return _Ride([chip], "scatter")

    small = {n: w[n] for n in SMALL if n != "conv_w"}
    loss, grad_x, g = _local_step(x[0], mem[0], loss_target[0], wts, late_rides, late_weights, small,
                                  rest_rides, w_in_ride)
    out_g, out_d, out_m, out_v = {}, {}, {}, {}

    def apply(n, g_parts):
        res = _adamw(w[n][0], mo[n][0], vo[n][0], g_parts, name="adamw_" + n)
        out_g[n], out_d[n], out_m[n], out_v[n] = [r[None] for r in res]

    mine = _sum_parts(g["w_in_own"], g["w_in_lands"][0], name="sum_chips_w_in")
    (theirs,) = _swap_sibling([mine], "w_in_halves_back")
    g_w_in = lax.dynamic_update_slice_in_dim(jnp.zeros((D, D_IN // N_SHARD), f32), mine, core * half, axis=0)
    apply("w_in", [lax.dynamic_update_slice_in_dim(g_w_in, theirs, (1 - core) * half, axis=0)])

    packets = _exchange_packets(_pack([g[n] for n in SMALL] + [loss[:, :1]]))
    partial = []
    for n, r in zip(REST, g["rest_lands"]):
        own = lax.dynamic_index_in_dim(_slabs_from_full(n, g[n]), shard, 0, keepdims=False)
        partial.append(_sum_parts(own, r, name="sum_chips_" + n))
    other = _swap_sibling(partial, "swap_sibling")

    for n, p, q in zip(REST, partial, other):
        apply(n, [p, q])
    tot = _sum_parts(None, packets, name="sum_packets")
    shapes = [g[n].shape for n in SMALL] + [(1, 1)]
    sm = dict(zip(SMALL + ("loss",), _unpack(tot, shapes)))
    sm["conv_w"] = lax.dynamic_slice_in_dim(sm["conv_w"], shard * (CONV_DIM // N_SHARD), CONV_DIM // N_SHARD, axis=1)
    own_small = lambda d: _pack([d[n].reshape(sm[n].shape) for n in SMALL])
    res = _adamw(own_small(w), own_small(mo), own_small(vo), [own_small(sm)], name="adamw_small")
    own_shapes = [sm[n].shape for n in SMALL]
    for store, r in zip((out_g, out_d, out_m, out_v), res):
        for n, val in zip(SMALL, _unpack(r, own_shapes)):
            store[n] = val.reshape(w[n].shape)

    outs = [sm["loss"].reshape(()), grad_x[None]]
    for store in (out_g, out_d, out_m, out_v):
        outs += [store[n] for n in WEIGHTS]
    return tuple(outs)
```

```python
import functools

import jax
import jax.numpy as jnp
from jax import lax
from jax.experimental import pallas as pl
from jax.experimental.pallas import tpu as pltpu

f32 = jnp.float32
bf16 = jnp.bfloat16

D = 1024
EPS = 1e-6
SB_HD = 64
SSD_INNER = 2048
SSD_HEADS = 32
SSD_GROUPS = 4
SSD_N = 128
SSD_L = 128
CONV_K = 4
CONV_DIM = 3072
MEM_HEADS = 4
MEM_HD = 256
D_FF = 4096
D_IN = 12320
N_SHARD = 4
N_DEV = 8

P_QKV, P_XBC, P_GATE, P_MEMQ, P_Z, P_DT, P_TOT = 0, 3072, 6144, 9216, 10240, 12288, 12416
R_QKV, R_Z, R_XBC, R_DT, R_MEMQ, R_GATE = (0, 3072), (3072, 5120), (5120, 8192), (8192, 8224), (8224, 9248), (9248, 12320)

ADAM_LR = 0.001
ADAM_B1 = 0.9
ADAM_B2 = 0.999
ADAM_EPS = 1e-08
ADAM_WD = 0.01
ADAM_STEP = 10

VMEM_LIMIT = 56 * 1024 * 1024

NN = (((1,), (0,)), ((), ()))
NT = (((1,), (1,)), ((), ()))
TN = (((0,), (0,)), ((), ()))


def _dot(a, b, dims=NN):
    return lax.dot_general(a, b, dims, preferred_element_type=f32)


def _params(sem=None):
    return pltpu.CompilerParams(dimension_semantics=sem, vmem_limit_bytes=VMEM_LIMIT)


def _sigmoid(x):
    return 1.0 / (1.0 + jnp.exp(-x))


def _split2(x):
    hi = x.astype(bf16)
    lo = (x - hi.astype(f32)).astype(bf16)
    return hi, lo


def _split3(x):
    hi = x.astype(bf16)
    r = x - hi.astype(f32)
    mid = r.astype(bf16)
    lo = (r - mid.astype(f32)).astype(bf16)
    return hi, mid, lo


def _mm(a, b, mode, *, tm, tn, name, out_dtypes=(f32,), epi=None, extras=(), ride=None):
    M = a.shape[1] if mode == "tn" else a.shape[0]
    N = b.shape[0] if mode == "nt" else b.shape[1]
    tm, tn = min(tm, M), min(tn, N)
    if mode == "nn":
        (M, K), N = a.shape, b.shape[1]
        a_spec = pl.BlockSpec((tm, K), lambda i, j: (i, 0))
        b_spec = pl.BlockSpec((K, tn), lambda i, j: (0, j))
        dims = NN
    elif mode == "nt":
        (M, K), N = a.shape, b.shape[0]
        a_spec = pl.BlockSpec((tm, K), lambda i, j: (i, 0))
        b_spec = pl.BlockSpec((tn, K), lambda i, j: (j, 0))
        dims = NT
    else:
        (K, M), N = a.shape, b.shape[1]
        a_spec = pl.BlockSpec((K, tm), lambda i, j: (0, i))
        b_spec = pl.BlockSpec((K, tn), lambda i, j: (0, j))
        dims = TN
    assert M % tm == 0 and N % tn == 0, (name, M, N, tm, tn)
    n_ex, n_out = len(extras), len(out_dtypes)
    n_r = ride.n if ride else 0
    o_spec = pl.BlockSpec((tm, tn), lambda i, j: (i, j))
    grid = (M // tm, N // tn)

    def body(a_ref, b_ref, *rest):
        r_ins = rest[n_ex:n_ex + n_r]
        outs = rest[n_ex + n_r:n_ex + n_r + n_out]
        r_lnd, r_sems = rest[n_ex + n_r + n_out:n_ex + 2 * n_r + n_out], rest[n_ex + 2 * n_r + n_out:]
        i, j = pl.program_id(0), pl.program_id(1)
        if ride:
            pl.when((i == 0) & (j == 0))(lambda: ride.start(r_ins, r_lnd, r_sems))
        acc = _dot(a_ref[...].astype(bf16), b_ref[...].astype(bf16), dims)
        res = (acc,) if epi is None else epi(acc, *[e[...] for e in rest[:n_ex]])
        for o_ref, r in zip(outs, res):
            o_ref[...] = r.astype(o_ref.dtype)
        if ride:
            pl.when((i == grid[0] - 1) & (j == grid[1] - 1))(lambda: ride.finish(r_ins, r_lnd, r_sems))

    out = pl.pallas_call(
        body, name=name, grid=grid,
        in_specs=[a_spec, b_spec] + [o_spec] * n_ex + (ride.in_specs if ride else []),
        out_specs=[o_spec] * n_out + (ride.out_specs if ride else []),
        out_shape=[jax.ShapeDtypeStruct((M, N), dt) for dt in out_dtypes] + (ride.out_shape if ride else []),
        scratch_shapes=ride.scratch if ride else [],
        compiler_params=_params(("arbitrary", "arbitrary") if ride else ("parallel", "parallel")),
    )(a, b, *extras, *(ride.srcs if ride else []))
    if ride:
        return (out[0] if n_out == 1 else out[:n_out]), list(out[n_out:])
    return out[0] if n_out == 1 else out


def _mm_pieces_nt(pieces, b, add, *, tm, tn, name, ride):
    M, N = pieces[0].shape[0], b.shape[0]
    n_p, n_r = len(pieces), (ride.n if ride else 0)
    o_spec = pl.BlockSpec((tm, tn), lambda i, j: (i, j))
    grid = (M // tm, N // tn)

    def body(*refs):
        b_ref, add_ref = refs[n_p:n_p + 2]
        r_ins, o_ref = refs[n_p + 2:n_p + 2 + n_r], refs[n_p + 2 + n_r]
        r_lnd, r_sems = refs[n_p + 3 + n_r:n_p + 3 + 2 * n_r], refs[n_p + 3 + 2 * n_r:]
        i, j = pl.program_id(0), pl.program_id(1)
        if ride:
            pl.when((i == 0) & (j == 0))(lambda: ride.start(r_ins, r_lnd, r_sems))
        acc, off = add_ref[...], 0
        for r in refs[:n_p]:
            acc = acc + _dot(r[...], b_ref[:, off:off + r.shape[1]], NT)
            off += r.shape[1]
        o_ref[...] = acc
        if ride:
            pl.when((i == grid[0] - 1) & (j == grid[1] - 1))(lambda: ride.finish(r_ins, r_lnd, r_sems))

    out = pl.pallas_call(
        body, name=name, grid=grid,
        in_specs=[pl.BlockSpec((tm, p.shape[1]), lambda i, j: (i, 0)) for p in pieces]
        + [pl.BlockSpec((tn, b.shape[1]), lambda i, j: (j, 0)), o_spec] + (ride.in_specs if ride else []),
        out_specs=[o_spec] + (ride.out_specs if ride else []),
        out_shape=[jax.ShapeDtypeStruct((M, N), f32)] + (ride.out_shape if ride else []),
        scratch_shapes=ride.scratch if ride else [],
        compiler_params=_params(("arbitrary", "arbitrary")),
    )(*pieces, b, add, *(ride.srcs if ride else []))
    return out[0], list(out[1:])


def _rms_fwd(x, g, *, name, out_dtype, residual=None, tm=512):
    S, C = x.shape
    tm = min(tm, S)
    has_res = residual is not None

    def body(x_ref, g_ref, *rest):
        xv = x_ref[...]
        r = lax.rsqrt(jnp.mean(xv * xv, axis=1, keepdims=True) + EPS)
        y = xv * r * g_ref[...]
        if has_res:
            y = y + rest[0][...]
        rest[-1][...] = y.astype(out_dtype)

    row = pl.BlockSpec((tm, C), lambda i: (i, 0))
    vec = pl.BlockSpec((1, C), lambda i: (0, 0))
    args = (x, g) + ((residual,) if has_res else ())
    return pl.pallas_call(
        body, name=name, grid=(S // tm,),
        in_specs=[row, vec] + ([row] if has_res else []),
        out_specs=row, out_shape=jax.ShapeDtypeStruct((S, C), out_dtype),
        compiler_params=_params(("parallel",)),
    )(*args)


def _rms_bwd(x, dy, g, *, name, out_dtype, add=None, tm=512):
    S, C = x.shape
    tm = min(tm, S)
    has_add = add is not None

    def body(x_ref, dy_ref, g_ref, *rest):
        dx_ref, dg_ref = rest[-2], rest[-1]
        xv = x_ref[...]
        dyv = dy_ref[...].astype(f32)
        r = lax.rsqrt(jnp.mean(xv * xv, axis=1, keepdims=True) + EPS)
        xh = xv * r
        dxh = dyv * g_ref[...]
        dx = r * (dxh - xh * jnp.mean(dxh * xh, axis=1, keepdims=True))
        if has_add:
            dx = dx + rest[0][...]
        dx_ref[...] = dx.astype(out_dtype)

        @pl.when(pl.program_id(0) == 0)
        def _():
            dg_ref[...] = jnp.zeros_like(dg_ref)

        dg_ref[...] += jnp.sum(dyv * xh, axis=0, keepdims=True)

    row = pl.BlockSpec((tm, C), lambda i: (i, 0))
    vec = pl.BlockSpec((1, C), lambda i: (0, 0))
    args = (x, dy, g) + ((add,) if has_add else ())
    return pl.pallas_call(
        body, name=name, grid=(S // tm,),
        in_specs=[row, row, vec] + ([row] if has_add else []),
        out_specs=[row, vec],
        out_shape=[jax.ShapeDtypeStruct((S, C), out_dtype), jax.ShapeDtypeStruct((1, C), f32)],
        compiler_params=_params(("arbitrary",)),
    )(*args)


SB_T = 128
SB_SPENT = -120.0
SB_QB = 4
SB_TAIL = 3
SB_GROUPS = (4, 2, 1)
SB_GROUPS_BWD = (4, 2, 1)


def _sb_masks():
    lane = lax.broadcasted_iota(jnp.int32, (1, 128), 1)
    m_a = (lane < SB_HD).astype(f32)
    return m_a, 1.0 - m_a


def _chunks(a, n):
    return [a[:, u * SB_T:(u + 1) * SB_T] for u in range(n)]


def _cat(parts, axis):
    return parts[0] if len(parts) == 1 else jnp.concatenate(parts, axis=axis)


def _mask_last(a, n, mask):
    if mask is None:
        return a
    parts = _chunks(a, n)
    return _cat(parts[:-1] + [jnp.where(mask, parts[-1], 0.0)], 1)


def _sb_logits(z, n, mask):
    l1p = jnp.log(1.0 + jnp.exp(-jnp.abs(z)))
    lb = jnp.minimum(z, 0.0) - l1p
    return lb, _mask_last(lb - z, n, mask)


def _by_count(i, most, fn):
    return lax.switch(jnp.minimum(i, most - 1), [functools.partial(fn, n) for n in range(1, most + 1)])


def _chunk_matmul(parts_list, u_mat):
    out = _dot(_cat(parts_list, 0), u_mat)
    return [out[u * SB_T:(u + 1) * SB_T] for u in range(len(parts_list))]


def _chunk_cumsum(lk, n, u_mat):
    hi = lk.astype(bf16)
    lo = (lk - hi.astype(f32)).astype(bf16)
    out = _chunk_matmul(_chunks(hi, n) + _chunks(lo, n), u_mat)
    return [out[u] + out[n + u] for u in range(n)]


def _sb_fwd(proj, S, ride=None):
    nq = S // SB_T
    n_pairs = D // 128
    scale = SB_HD ** -0.5
    n_r = ride.n if ride else 0

    def body(q_ref, k_ref, v_ref, *rest):
        o_ref, t_ref = rest[n_r:n_r + 2]
        step_i = pl.program_id(1)
        if ride:
            pl.when((pl.program_id(0) == 0) & (step_i == 0))(
                lambda: ride.start(rest[:n_r], rest[n_r + 2:2 * n_r + 2], rest[2 * n_r + 2:]))
        m_a, m_b = _sb_masks()
        r_i = lax.broadcasted_iota(jnp.int32, (SB_T, SB_T), 0)
        c_i = lax.broadcasted_iota(jnp.int32, (SB_T, SB_T), 1)
        u_mat = (r_i > c_i).astype(bf16)
        causal = c_i < r_i
        lane_a = lax.broadcasted_iota(jnp.int32, (1, 128), 1) < SB_HD
        q_all = q_ref[...] * scale
        q_hs = [((q * m_a).astype(bf16), (q * m_b).astype(bf16))
                for q in (q_all[b * SB_T:(b + 1) * SB_T] for b in range(SB_QB))]

        def group(q_h, j_lo, n, carry, mask):
            acc, c_a, c_b = carry
            rows = pl.ds(pl.multiple_of(j_lo * SB_T, SB_T), n * SB_T)
            k = k_ref[rows, :].astype(bf16)
            v = v_ref[rows, :]
            zs = [_dot(q_b, k, NT) for q_b in q_h]
            lbk = [_sb_logits(z, n, mask) for z in zs]
            parts = [_chunk_cumsum(lk, n, u_mat) for _, lk in lbk]
            ws, cs = [], []
            for (lb, lk), part, c in zip(lbk, parts, (c_a, c_b)):
                lb_c, lk_c = _chunks(lb, n), _chunks(lk, n)
                w_c = [None] * n
                for u in reversed(range(n)):
                    w_c[u] = jnp.exp(lb_c[u] + c + part[u])
                    c = c + jnp.sum(lk_c[u], axis=1, keepdims=True)
                ws.append(_mask_last(_cat(w_c, 1), n, mask).astype(bf16))
                cs.append(c)
            v_b = v.astype(bf16)
            acc = acc + jnp.where(lane_a, _dot(ws[0], v_b), _dot(ws[1], v_b))
            return acc, cs[0], cs[1]

        zero_c = jnp.zeros((SB_T, 1), f32)
        init = (jnp.zeros((SB_T, 128), f32), zero_c, zero_c)
        blocks = [(step_i * SB_QB + b, q_hs[b]) for b in range(SB_QB)]

        def whole_tails():
            return tuple(group(q_h, i - SB_TAIL + 1, SB_TAIL, init, causal) for i, q_h in blocks)

        def short_tails():
            return tuple(_by_count(i, SB_TAIL, functools.partial(
                lambda n, i, q_h: group(q_h, i - n + 1, n, init, causal), i=i, q_h=q_h)) for i, q_h in blocks)

        carries = lax.cond(step_i * SB_QB >= SB_TAIL - 1, whole_tails, short_tails)

        def spent(cr):
            return (jnp.max(jnp.maximum(cr[1], cr[2])) < SB_SPENT).astype(jnp.int32)

        lane = lax.broadcasted_iota(jnp.int32, (1, 128), 1)
        for b, ((i, q_h), carry) in enumerate(zip(blocks, carries)):
            state = (i - jnp.minimum(i, SB_TAIL - 1), spent(carry), carry)
            for n in SB_GROUPS:
                def step(st, n=n, q_h=q_h):
                    left, _, cr = st
                    cr = group(q_h, left - n, n, cr, None)
                    return left - n, spent(cr), cr

                state = lax.while_loop(lambda st, n=n: (st[0] >= n) & (st[1] == 0), step, state)
            left, _, carry = state
            rows = slice(b * SB_T, (b + 1) * SB_T)
            o_ref[rows, :] = carry[0]
            t_ref[rows, :] = (jnp.where(lane == 0, carry[1], 0.0) + jnp.where(lane == SB_HD, carry[2], 0.0)
                              + jnp.where(lane == 1, left.astype(f32), 0.0))
        if ride:
            pl.when((pl.program_id(0) == n_pairs - 1) & (step_i == nq // SB_QB - 1))(
                lambda: ride.finish(rest[:n_r], rest[n_r + 2:2 * n_r + 2], rest[2 * n_r + 2:]))

    qs = pl.BlockSpec((SB_QB * SB_T, 128), lambda h, i: (i, h))
    out = pl.pallas_call(
        body, name="sb_fwd", grid=(n_pairs, nq // SB_QB),
        in_specs=[qs,
                  pl.BlockSpec((S, 128), lambda h, i: (0, n_pairs + h)),
                  pl.BlockSpec((S, 128), lambda h, i: (0, 2 * n_pairs + h))] + (ride.in_specs if ride else []),
        out_specs=[qs, qs] + (ride.out_specs if ride else []),
        out_shape=[jax.ShapeDtypeStruct((S, D), f32)] * 2 + (ride.out_shape if ride else []),
        scratch_shapes=ride.scratch if ride else [],
        compiler_params=_params(("arbitrary", "arbitrary")),
    )(proj, proj, proj, *(ride.srcs if ride else []))
    return out[0], out[1], list(out[2:])


def _sb_bwd(proj, tot_lk, do, S, ride=None):
    nq = S // SB_T
    n_pairs = D // 128
    scale = SB_HD ** -0.5
    n_r = ride.n if ride else 0

    def body(q_ref, k_ref, v_ref, t_ref, do_ref, *rest):
        dq_ref, dk_ref, dv_ref = rest[n_r:n_r + 3]
        dk_acc, dv_acc = rest[2 * n_r + 3:2 * n_r + 5]
        r_ins, r_lnd, r_sems = rest[:n_r], rest[n_r + 3:2 * n_r + 3], rest[2 * n_r + 5:]
        step_i = pl.program_id(1)
        if ride:
            pl.when((pl.program_id(0) == 0) & (step_i == 0))(lambda: ride.start(r_ins, r_lnd, r_sems))
        m_a, m_b = _sb_masks()
        r_i = lax.broadcasted_iota(jnp.int32, (SB_T, SB_T), 0)
        c_i = lax.broadcasted_iota(jnp.int32, (SB_T, SB_T), 1)
        u_inc = (r_i <= c_i).astype(bf16)
        u_exc = (r_i < c_i).astype(bf16)
        causal = c_i < r_i

        @pl.when(step_i == 0)
        def _():
            dk_acc[...] = jnp.zeros_like(dk_acc)
            dv_acc[...] = jnp.zeros_like(dv_acc)

        lane = lax.broadcasted_iota(jnp.int32, (1, 128), 1)
        blocks = []
        for b in range(SB_QB):
            rows_b = slice(b * SB_T, (b + 1) * SB_T)
            i = step_i * SB_QB + b
            q = q_ref[rows_b, :] * scale
            dov = do_ref[rows_b, :]
            tv = t_ref[rows_b, :]
            heads = []
            for m, first in ((m_a, 0), (m_b, SB_HD)):
                tot = jnp.sum(jnp.where(lane == first, tv, 0.0), axis=1, keepdims=True)
                heads.append(((q * m).astype(bf16), (dov * m).astype(bf16), tot, m))
            lowest = jnp.clip(jnp.max(jnp.where(lane == 1, tv, 0.0)).astype(jnp.int32), 0, i)
            blocks.append((i, heads, lowest))

        def group(heads, j_lo, n, carry, mask):
            dq_acc, cp_a, cp_b, ce_a, ce_b = carry
            rows = pl.ds(pl.multiple_of(j_lo * SB_T, SB_T), n * SB_T)
            k_f = k_ref[rows, :]
            k = k_f.astype(bf16)
            v = v_ref[rows, :].astype(bf16)
            zs = [_dot(h[0], k, NT) for h in heads]
            dws = [_dot(h[1], v, NT) for h in heads]
            lbk = [_sb_logits(z, n, mask) for z in zs]
            parts = [_chunk_cumsum(lk, n, u_inc) for _, lk in lbk]
            ws, es, cps = [], [], []
            for (lb, lk), part, dw, h, cp in zip(lbk, parts, dws, heads, (cp_a, cp_b)):
                lb_c, lk_c = _chunks(lb, n), _chunks(lk, n)
                w_c = []
                for u in range(n):
                    w_c.append(jnp.exp(lb_c[u] + (h[2] - cp) - part[u]))
                    cp = cp + jnp.sum(lk_c[u], axis=1, keepdims=True)
                w = _mask_last(_cat(w_c, 1), n, mask)
                ws.append(w)
                es.append(dw * w)
                cps.append(cp)
            e_parts = [_chunk_matmul(_chunks(e.astype(bf16), n), u_exc) for e in es]
            dzs, ces = [], []
            for (lb, _), e, e_part, ce in zip(lbk, es, e_parts, (ce_a, ce_b)):
                e_c = _chunks(e, n)
                big_c = []
                for u in range(n):
                    big_c.append(ce + e_part[u])
                    ce = ce + jnp.sum(e_c[u], axis=1, keepdims=True)
                sig = jnp.exp(lb)
                dz = _mask_last(e * (1.0 - sig) - _cat(big_c, 1) * sig, n, mask)
                dzs.append(dz.astype(bf16))
                ces.append(ce)
            dk_t = jnp.zeros((n * SB_T, 128), f32)
            dv_t = jnp.zeros((n * SB_T, 128), f32)
            dq_acc = dq_acc + jnp.where(lane < SB_HD, _dot(dzs[0], k), _dot(dzs[1], k))
            for dz_b, w, h in zip(dzs, ws, heads):
                dk_t = dk_t + _dot(dz_b, h[0], TN)
                dv_t = dv_t + _dot(w.astype(bf16), h[1], TN)
            dk_acc[rows, :] += dk_t
            dv_acc[rows, :] += dv_t
            return dq_acc, cps[0], cps[1], ces[0], ces[1]

        zc = jnp.zeros((SB_T, 1), f32)
        carries = []
        for i, heads, lowest in blocks:
            carry = (jnp.zeros((SB_T, 128), f32), zc, zc, zc, zc)
            done = lowest
            tail_lo = i - jnp.minimum(i, SB_TAIL - 1)
            for n in SB_GROUPS_BWD:
                trips = (tail_lo - done) // n
                carry = lax.fori_loop(
                    0, trips, functools.partial(
                        lambda gi, cr, n, done, heads: group(heads, done + gi * n, n, cr, None),
                        n=n, done=done, heads=heads),
                    carry)
                done = done + trips * n
            carries.append(carry)

        def whole_tails():
            return tuple(group(heads, i - SB_TAIL + 1, SB_TAIL, cr, causal)
                         for (i, heads, _), cr in zip(blocks, carries))

        def short_tails():
            return tuple(_by_count(i, SB_TAIL, functools.partial(
                lambda n, i, heads, cr: group(heads, i - n + 1, n, cr, causal), i=i, heads=heads, cr=cr))
                for (i, heads, _), cr in zip(blocks, carries))

        carries = lax.cond(step_i * SB_QB >= SB_TAIL - 1, whole_tails, short_tails)
        for b, carry in enumerate(carries):
            dq_ref[b * SB_T:(b + 1) * SB_T, :] = (carry[0] * scale).astype(bf16)

        @pl.when(step_i == nq // SB_QB - 1)
        def _():
            dk_ref[...] = dk_acc[...].astype(bf16)
            dv_ref[...] = dv_acc[...].astype(bf16)

        if ride:
            pl.when((pl.program_id(0) == n_pairs - 1) & (step_i == nq // SB_QB - 1))(
                lambda: ride.finish(r_ins, r_lnd, r_sems))

    qs = pl.BlockSpec((SB_QB * SB_T, 128), lambda h, i: (i, h))
    full = pl.BlockSpec((S, 128), lambda h, i: (0, h))
    out = pl.pallas_call(
        body, name="sb_bwd", grid=(n_pairs, nq // SB_QB),
        in_specs=[qs,
                  pl.BlockSpec((S, 128), lambda h, i: (0, n_pairs + h)),
                  pl.BlockSpec((S, 128), lambda h, i: (0, 2 * n_pairs + h)),
                  qs, qs] + (ride.in_specs if ride else []),
        out_specs=[qs, full, full] + (ride.out_specs if ride else []),
        out_shape=[jax.ShapeDtypeStruct((S, D), bf16)] * 3 + (ride.out_shape if ride else []),
        scratch_shapes=[pltpu.VMEM((S, 128), f32), pltpu.VMEM((S, 128), f32)] + (ride.scratch if ride else []),
        compiler_params=_params(("arbitrary", "arbitrary")),
    )(proj, proj, proj, tot_lk, do, *(ride.srcs if ride else []))
    return out[0], out[1], out[2], list(out[3:])


CONV_CB = 256
HALO = 8


def _conv_fwd(proj, conv_w, conv_b, S, ride=None):
    tr = min(512, S)
    n_r = ride.n if ride else 0
    n_c = CONV_DIM // CONV_CB

    def body(x_ref, w_ref, b_ref, *rest):
        xc_ref, xbc_ref = rest[n_r:n_r + 2]
        r_ins, r_lnd, r_sems = rest[:n_r], rest[n_r + 2:2 * n_r + 2], rest[2 * n_r + 2:]
        if ride:
            pl.when(pl.program_id(0) == 0)(lambda: ride.start(r_ins, r_lnd, r_sems))
        w = w_ref[...]
        for t in range(S // tr):
            cur = x_ref[t * tr:(t + 1) * tr, :]
            halo = x_ref[t * tr - HALO:t * tr, :] if t else jnp.zeros((HALO, CONV_CB), f32)
            win = jnp.concatenate([halo, cur], axis=0)
            acc = b_ref[...] + w[CONV_K - 1:CONV_K, :] * cur
            for k in range(CONV_K - 1):
                acc = acc + w[k:k + 1, :] * pltpu.roll(win, CONV_K - 1 - k, 0)[HALO:, :]
            xc_ref[t * tr:(t + 1) * tr, :] = acc
            xbc_ref[t * tr:(t + 1) * tr, :] = acc * _sigmoid(acc)
        if ride:
            pl.when(pl.program_id(0) == n_c - 1)(lambda: ride.finish(r_ins, r_lnd, r_sems))

    col = pl.BlockSpec((S, CONV_CB), lambda c: (0, c))
    out = pl.pallas_call(
        body, name="conv_fwd", grid=(n_c,),
        in_specs=[pl.BlockSpec((S, CONV_CB), lambda c: (0, P_XBC // CONV_CB + c)),
                  pl.BlockSpec((CONV_K, CONV_CB), lambda c: (0, c)),
                  pl.BlockSpec((1, CONV_CB), lambda c: (0, c))] + (ride.in_specs if ride else []),
        out_specs=[col, col] + (ride.out_specs if ride else []),
        out_shape=[jax.ShapeDtypeStruct((S, CONV_DIM), f32)] * 2 + (ride.out_shape if ride else []),
        scratch_shapes=ride.scratch if ride else [],
        compiler_params=_params(("arbitrary",)),
    )(proj, conv_w, conv_b, *(ride.srcs if ride else []))
    return out[0], out[1], list(out[2:])


def _conv_bwd(proj, xc, dxbc, conv_w, S):
    tr = min(512, S)

    def body(x_ref, xc_ref, dy_ref, w_ref, dx_ref, dw_ref, db_ref, dxc_s):
        w = w_ref[...]
        xcv = xc_ref[...]
        sg = _sigmoid(xcv)
        dxc_s[0:S, :] = dy_ref[...] * (sg * (1.0 + xcv * (1.0 - sg)))
        dxc_s[S:S + HALO, :] = jnp.zeros((HALO, CONV_CB), f32)
        dws = [jnp.zeros((1, CONV_CB), f32) for _ in range(CONV_K)]
        db = jnp.zeros((1, CONV_CB), f32)
        for t in range(S // tr):
            cur = x_ref[t * tr:(t + 1) * tr, :]
            halo = x_ref[t * tr - HALO:t * tr, :] if t else jnp.zeros((HALO, CONV_CB), f32)
            win = jnp.concatenate([halo, cur], axis=0)
            dwin = dxc_s[t * tr:(t + 1) * tr + HALO, :]
            dcur = dwin[0:tr, :]
            db = db + jnp.sum(dcur, axis=0, keepdims=True)
            dws[CONV_K - 1] = dws[CONV_K - 1] + jnp.sum(dcur * cur, axis=0, keepdims=True)
            dx = w[CONV_K - 1:CONV_K, :] * dcur
            for k in range(CONV_K - 1):
                sh = CONV_K - 1 - k
                dws[k] = dws[k] + jnp.sum(dcur * pltpu.roll(win, sh, 0)[HALO:, :], axis=0, keepdims=True)
                dx = dx + w[k:k + 1, :] * pltpu.roll(dwin, tr + HALO - sh, 0)[0:tr, :]
            dx_ref[t * tr:(t + 1) * tr, :] = dx.astype(bf16)
        dw_ref[...] = jnp.concatenate(dws + [jnp.zeros((8 - CONV_K, CONV_CB), f32)], axis=0)
        db_ref[...] = db

    col = pl.BlockSpec((S, CONV_CB), lambda c: (0, c))
    return pl.pallas_call(
        body, name="conv_bwd", grid=(CONV_DIM // CONV_CB,),
        in_specs=[pl.BlockSpec((S, CONV_CB), lambda c: (0, P_XBC // CONV_CB + c)), col, col,
                  pl.BlockSpec((CONV_K, CONV_CB), lambda c: (0, c))],
        out_specs=[col, pl.BlockSpec((8, CONV_CB), lambda c: (0, c)), pl.BlockSpec((1, CONV_CB), lambda c: (0, c))],
        out_shape=[jax.ShapeDtypeStruct((S, CONV_DIM), bf16), jax.ShapeDtypeStruct((8, CONV_DIM), f32),
                   jax.ShapeDtypeStruct((1, CONV_DIM), f32)],
        scratch_shapes=[pltpu.VMEM((S + HALO, CONV_CB), f32)],
        compiler_params=_params(("parallel",)),
    )(proj, xc, dxbc, conv_w)


N_PAIR = SSD_HEADS // 2
NEG = -1e30


def _softplus(x):
    return jnp.maximum(x, 0.0) + jnp.log(1.0 + jnp.exp(-jnp.abs(x)))


def _ssd_common(dtr, dtb, alog):
    L = SSD_L
    r_i = lax.broadcasted_iota(jnp.int32, (L, L), 0)
    c_i = lax.broadcasted_iota(jnp.int32, (L, L), 1)
    dt = _softplus(dtr + dtb)
    a = -jnp.exp(alog)
    da = dt * a
    lower = (r_i >= c_i).astype(bf16)
    upper = (r_i <= c_i).astype(bf16)
    parts = _split3(da)
    a_cs = sum(_dot(lower, p) for p in parts)
    a_cs_t = sum(_dot(p, upper, TN) for p in parts)
    return dt, a, a_cs, a_cs_t, r_i >= c_i


def _pair_vec(lane, v, h):
    return jnp.where(lane < SB_HD, v[:, h:h + 1], v[:, h + 1:h + 2])


def _decay_mat(a_cs, a_cs_t, h, tril):
    return jnp.exp(jnp.where(tril, a_cs[:, h:h + 1] - a_cs_t[h:h + 1, :], NEG))


def _ssd_fwd(xbc, proj, pdt, dt_bias_p, a_log_p, d_skip_c, ssd_norm, S, ride=None):
    L = SSD_L
    nc = S // L
    n_r = ride.n if ride else 0

    def body(xbc_ref, dt_ref, z_ref, dtb_ref, alog_ref, dsk_ref, gn_ref, *rest):
        y_ref, yn_ref, hp_ref = rest[n_r:n_r + 3]
        state = rest[2 * n_r + 3]
        r_ins, r_lnd, r_sems = rest[:n_r], rest[n_r + 3:2 * n_r + 3], rest[2 * n_r + 4:]
        c = pl.program_id(0)
        if ride:
            pl.when(c == 0)(lambda: ride.start(r_ins, r_lnd, r_sems))

        @pl.when(c == 0)
        def _():
            state[...] = jnp.zeros_like(state)

        hp_ref[0] = state[...]
        lane = lax.broadcasted_iota(jnp.int32, (1, 128), 1)
        row128 = lax.broadcasted_iota(jnp.int32, (128, 1), 0)
        m_a, m_b = _sb_masks()
        dt, a, a_cs, a_cs_t, tril = _ssd_common(dt_ref[...], dtb_ref[...], alog_ref[...])
        a_last = a_cs[L - 1:L, :]
        for g in range(SSD_GROUPS):
            b_g = xbc_ref[:, SSD_INNER + g * SSD_N:SSD_INNER + (g + 1) * SSD_N].astype(bf16)
            c_g = xbc_ref[:, SSD_INNER + (SSD_GROUPS + g) * SSD_N:SSD_INNER + (SSD_GROUPS + g + 1) * SSD_N].astype(bf16)
            cb = _dot(c_g, b_g, NT)
            for pr in range(4):
                h = 8 * g + 2 * pr
                pi = h // 2
                cols = slice(pi * 128, (pi + 1) * 128)
                xs = xbc_ref[:, cols]
                x = xs * _pair_vec(lane, dt, h)
                acs = _pair_vec(lane, a_cs, h)
                al = _pair_vec(lane, a_last, h)
                w_a = (cb * _decay_mat(a_cs, a_cs_t, h, tril)).astype(bf16)
                w_b = (cb * _decay_mat(a_cs, a_cs_t, h + 1, tril)).astype(bf16)
                x_b = x.astype(bf16)
                yd = jnp.where(lane < SB_HD, _dot(w_a, x_b), _dot(w_b, x_b))
                hp = state[pi]
                yo = _dot(c_g, hp.astype(bf16), NT) * jnp.exp(acs)
                y_ref[:, cols] = yd + yo + dsk_ref[:, cols] * xs
                dec = jnp.exp(jnp.where(row128 < SB_HD, a_last[:, h:h + 1], a_last[:, h + 1:h + 2]))
                state[pi] = hp * dec + _dot((x * jnp.exp(al - acs)).astype(bf16), b_g, TN)
        zz = z_ref[...]
        y2 = y_ref[...] * (zz * _sigmoid(zz))
        gw = SSD_INNER // SSD_GROUPS
        for g in range(SSD_GROUPS):
            yg = y2[:, g * gw:(g + 1) * gw]
            rg = lax.rsqrt(jnp.mean(yg * yg, axis=1, keepdims=True) + EPS)
            yn_ref[:, g * gw:(g + 1) * gw] = (yg * rg * gn_ref[:, g * gw:(g + 1) * gw]).astype(bf16)
        if ride:
            pl.when(c == nc - 1)(lambda: ride.finish(r_ins, r_lnd, r_sems))

    vec128 = pl.BlockSpec((1, 128), lambda c: (0, 0))
    vecin = pl.BlockSpec((1, SSD_INNER), lambda c: (0, 0))
    rows = pl.BlockSpec((L, SSD_INNER), lambda c: (c, 0))
    out = pl.pallas_call(
        body, name="ssd_fwd", grid=(nc,),
        in_specs=[pl.BlockSpec((L, CONV_DIM), lambda c: (c, 0)),
                  pl.BlockSpec((L, 128), lambda c: (c, 0)),
                  pl.BlockSpec((L, SSD_INNER), lambda c: (c, P_Z // SSD_INNER)),
                  vec128, vec128, vecin, vecin] + (ride.in_specs if ride else []),
        out_specs=[rows, rows, pl.BlockSpec((1, N_PAIR, 128, SSD_N), lambda c: (c, 0, 0, 0))]
        + (ride.out_specs if ride else []),
        out_shape=[jax.ShapeDtypeStruct((S, SSD_INNER), f32), jax.ShapeDtypeStruct((S, SSD_INNER), bf16),
                   jax.ShapeDtypeStruct((nc, N_PAIR, 128, SSD_N), f32)] + (ride.out_shape if ride else []),
        scratch_shapes=[pltpu.VMEM((N_PAIR, 128, SSD_N), f32)] + (ride.scratch if ride else []),
        compiler_params=_params(("arbitrary",)),
    )(xbc, pdt, proj, dt_bias_p, a_log_p, d_skip_c, ssd_norm, *(ride.srcs if ride else []))
    return out[0], out[1], out[2], list(out[3:])


def _sum_all(v):
    return jnp.sum(jnp.sum(v, axis=1, keepdims=True), axis=0, keepdims=True)


def _ssd_bwd(dyn, y, xbc, proj, pdt, hprev, dt_bias_p, a_log_p, d_skip_c, ssd_norm, S, ride=None):
    L = SSD_L
    nc = S // L
    n_r = ride.n if ride else 0

    col = lax.broadcasted_iota(jnp.int32, (2 * SSD_INNER, 128), 0)
    head = lax.broadcasted_iota(jnp.int32, (2 * SSD_INNER, 128), 1)
    sel_pair = (col[:SSD_INNER] // SB_HD == head[:SSD_INNER]).astype(bf16)
    sel_head = (col // 128 == head).astype(bf16)

    def body(*refs):
        (dyn_ref, y_ref, xbc_ref, dt_ref, z_ref, hp_ref, dtb_ref, alog_ref, dsk_ref, gn_ref,
         selp_ref, selh_ref) = refs[:12]
        dz_ref, dxbc_ref, ddt_ref, dgn_ref, dsk_out, dalog_ref, ddtb_ref = refs[12 + n_r:19 + n_r]
        dstate, dy_s, st_a, st_q, st_d, st_x, dat = refs[19 + 2 * n_r:26 + 2 * n_r]
        r_ins, r_lnd, r_sems = refs[12:12 + n_r], refs[19 + n_r:19 + 2 * n_r], refs[26 + 2 * n_r:]
        c = pl.program_id(0)
        if ride:
            pl.when(c == 0)(lambda: ride.start(r_ins, r_lnd, r_sems))

        @pl.when(c == 0)
        def _():
            dat[...] = jnp.zeros_like(dat)
            dstate[...] = jnp.zeros_like(dstate)
            dgn_ref[...] = jnp.zeros_like(dgn_ref)
            dsk_out[...] = jnp.zeros_like(dsk_out)
            dalog_ref[...] = jnp.zeros_like(dalog_ref)
            ddtb_ref[...] = jnp.zeros_like(ddtb_ref)

        lane = lax.broadcasted_iota(jnp.int32, (1, 128), 1)
        row128 = lax.broadcasted_iota(jnp.int32, (128, 1), 0)
        rowl = lax.broadcasted_iota(jnp.int32, (L, 1), 0)
        m_a, m_b = _sb_masks()
        dtr = dt_ref[...]
        dt, a, a_cs, a_cs_t, tril = _ssd_common(dtr, dtb_ref[...], alog_ref[...])
        a_last = a_cs[L - 1:L, :]

        zz = z_ref[...]
        sg = _sigmoid(zz)
        silu = zz * sg
        yv = y_ref[...]
        y2 = yv * silu
        gw = SSD_INNER // SSD_GROUPS
        for g in range(SSD_GROUPS):
            sl = slice(g * gw, (g + 1) * gw)
            yg = y2[:, sl]
            rg = lax.rsqrt(jnp.mean(yg * yg, axis=1, keepdims=True) + EPS)
            yh = yg * rg
            dyn_g = dyn_ref[:, sl]
            dgn_ref[:, sl] += jnp.sum(dyn_g * yh, axis=0, keepdims=True)
            dyh = dyn_g * gn_ref[:, sl]
            dy2 = rg * (dyh - yh * jnp.mean(dyh * yh, axis=1, keepdims=True))
            dy_s[:, sl] = dy2 * silu[:, sl]
            dz_ref[:, sl] = (dy2 * yv[:, sl] * (sg[:, sl] * (1.0 + zz[:, sl] * (1.0 - sg[:, sl])))).astype(bf16)

        last_row = jnp.zeros((1, 128), f32)
        dsk_acc = jnp.zeros((1, 128), f32)
        for g in range(SSD_GROUPS):
            bsl = slice(SSD_INNER + g * SSD_N, SSD_INNER + (g + 1) * SSD_N)
            csl = slice(SSD_INNER + (SSD_GROUPS + g) * SSD_N, SSD_INNER + (SSD_GROUPS + g + 1) * SSD_N)
            b_g = xbc_ref[:, bsl].astype(bf16)
            c_g = xbc_ref[:, csl].astype(bf16)
            cb = _dot(c_g, b_g, NT)
            dcb = jnp.zeros((L, L), f32)
            dc_g = jnp.zeros((L, SSD_N), f32)
            db_g = jnp.zeros((L, SSD_N), f32)
            for pr in range(4):
                h = 8 * g + 2 * pr
                pi = h // 2
                cols = slice(pi * 128, (pi + 1) * 128)
                xs = xbc_ref[:, cols]
                dt_p = _pair_vec(lane, dt, h)
                x = xs * dt_p
                acs = _pair_vec(lane, a_cs, h)
                al = _pair_vec(lane, a_last, h)
                e_a = jnp.exp(acs)
                dte = jnp.exp(al - acs)
                m_mat_a = _decay_mat(a_cs, a_cs_t, h, tril)
                m_mat_b = _decay_mat(a_cs, a_cs_t, h + 1, tril)
                dyp = dy_s[:, cols]
                dsk = dsk_ref[:, cols]
                d_hn = dstate[pi]
                hp = hp_ref[0, pi]
                dy_a = (dyp * m_a).astype(bf16)
                dy_b = (dyp * m_b).astype(bf16)
                x_b = x.astype(bf16)
                gm_a = _dot(dy_a, x_b, NT) * m_mat_a
                gm_b = _dot(dy_b, x_b, NT) * m_mat_b
                dcb = dcb + gm_a + gm_b
                dx_d = _dot((cb * m_mat_a).astype(bf16), dy_a, TN) + _dot((cb * m_mat_b).astype(bf16), dy_b, TN)
                dx_s = _dot(b_g, d_hn.astype(bf16), NT) * dte
                dx = dx_d + dx_s
                dxbc_ref[:, cols] = dx * dt_p + dsk * dyp
                xdxs = x * dx_s
                st_x[:, cols] = xdxs
                st_a[:, cols] = dyp * (_dot(c_g, hp.astype(bf16), NT) * e_a) - xdxs
                st_d[:, cols] = dx * xs
                hh = d_hn * hp
                dsk_row = jnp.sum(dyp * xs, axis=0, keepdims=True)
                dec = jnp.exp(jnp.where(row128 < SB_HD, a_last[:, h:h + 1], a_last[:, h + 1:h + 2]))
                for hd, m, gm in ((h, m_a, gm_a), (h + 1, m_b, gm_b)):
                    half = slice(0, SB_HD) if hd == h else slice(SB_HD, 128)
                    qm = gm * cb
                    st_q[:, hd * 128:(hd + 1) * 128] = qm
                    dat[hd:hd + 1, :] = jnp.sum(qm, axis=0, keepdims=True)
                    hh_sum = jnp.sum(jnp.sum(hh[half, :], axis=0, keepdims=True), axis=1, keepdims=True)
                    last_row = jnp.where(lane == hd, jnp.exp(a_last[:, hd:hd + 1]) * hh_sum, last_row)
                    dsk_acc = jnp.where(lane == hd, jnp.sum(dsk_row * m, axis=1, keepdims=True), dsk_acc)
                dye = (dyp * e_a).astype(bf16)
                dc_g = dc_g + _dot(dye, hp.astype(bf16))
                db_g = db_g + _dot((x * dte).astype(bf16), d_hn.astype(bf16))
                dstate[pi] = dec * d_hn + _dot(dye, c_g, TN)
            dcb_b = dcb.astype(bf16)
            dxbc_ref[:, csl] = dc_g + _dot(dcb_b, b_g)
            dxbc_ref[:, bsl] = db_g + _dot(dcb_b, c_g, TN)

        r_i = lax.broadcasted_iota(jnp.int32, (L, L), 0)
        c_i = lax.broadcasted_iota(jnp.int32, (L, L), 1)
        rev = (r_i <= c_i).astype(bf16)

        def head_sums(st, sel, split=_split2):
            return sum(_dot(p, sel[...]) for p in split(st[...]))

        last_row = last_row + jnp.sum(head_sums(st_x, selp_ref), axis=0, keepdims=True)
        d_acs = (head_sums(st_a, selp_ref) + head_sums(st_q, selh_ref, _split3)
                 + jnp.where(rowl == L - 1, last_row, 0.0))
        ddt_x = head_sums(st_d, selp_ref)
        dda = sum(_dot(rev, p) for p in _split3(d_acs)) - sum(_dot(rev, p, NT) for p in _split3(dat[...]))
        ddt = ddt_x + dda * a
        dalog_ref[...] += jnp.sum(dda * dt, axis=0, keepdims=True) * a
        ddtr = jnp.where(lane < SSD_HEADS, ddt * _sigmoid(dtr + dtb_ref[...]), 0.0)
        ddt_ref[...] = ddtr.astype(bf16)
        ddtb_ref[...] += jnp.sum(ddtr, axis=0, keepdims=True)
        dsk_out[...] += dsk_acc
        if ride:
            pl.when(c == nc - 1)(lambda: ride.finish(r_ins, r_lnd, r_sems))

    rv = lambda c: nc - 1 - c
    vec128 = pl.BlockSpec((1, 128), lambda c: (0, 0))
    vecin = pl.BlockSpec((1, SSD_INNER), lambda c: (0, 0))
    rows = pl.BlockSpec((L, SSD_INNER), lambda c: (rv(c), 0))
    return pl.pallas_call(
        body, name="ssd_bwd", grid=(nc,),
        in_specs=[rows, rows,
                  pl.BlockSpec((L, CONV_DIM), lambda c: (rv(c), 0)),
                  pl.BlockSpec((L, 128), lambda c: (rv(c), 0)),
                  pl.BlockSpec((L, SSD_INNER), lambda c: (rv(c), P_Z // SSD_INNER)),
                  pl.BlockSpec((1, N_PAIR, 128, SSD_N), lambda c: (rv(c), 0, 0, 0)),
                  vec128, vec128, vecin, vecin,
                  pl.BlockSpec((SSD_INNER, 128), lambda c: (0, 0)),
                  pl.BlockSpec((2 * SSD_INNER, 128), lambda c: (0, 0))] + (ride.in_specs if ride else []),
        out_specs=[rows, pl.BlockSpec((L, CONV_DIM), lambda c: (rv(c), 0)),
                   pl.BlockSpec((L, 128), lambda c: (rv(c), 0)), vecin, vec128, vec128, vec128]
        + (ride.out_specs if ride else []),
        out_shape=[jax.ShapeDtypeStruct((S, SSD_INNER), bf16), jax.ShapeDtypeStruct((S, CONV_DIM), f32),
                   jax.ShapeDtypeStruct((S, 128), bf16), jax.ShapeDtypeStruct((1, SSD_INNER), f32),
                   jax.ShapeDtypeStruct((1, 128), f32), jax.ShapeDtypeStruct((1, 128), f32),
                   jax.ShapeDtypeStruct((1, 128), f32)] + (ride.out_shape if ride else []),
        scratch_shapes=[pltpu.VMEM((N_PAIR, 128, SSD_N), f32), pltpu.VMEM((L, SSD_INNER), f32),
                        pltpu.VMEM((L, SSD_INNER), f32), pltpu.VMEM((L, 2 * SSD_INNER), f32),
                        pltpu.VMEM((L, SSD_INNER), f32), pltpu.VMEM((L, SSD_INNER), f32),
                        pltpu.VMEM((128, L), f32)]
        + (ride.scratch if ride else []),
        compiler_params=_params(("arbitrary",)),
    )(dyn, y, xbc, pdt, proj, hprev, dt_bias_p, a_log_p, d_skip_c, ssd_norm, sel_pair, sel_head,
      *(ride.srcs if ride else []))


MEM_W = MEM_HEADS * MEM_HD


def _mem_probs(q, k):
    s = _dot(q, k, NT) * (MEM_HD ** -0.5)
    s = s - jnp.max(s, axis=1, keepdims=True)
    p = jnp.exp(s)
    return p / jnp.sum(p, axis=1, keepdims=True)


def _mem_fwd(proj, kv, S, tm=512):
    tm = min(tm, S)
    M = kv.shape[0]

    def body(q_ref, kv_ref, o_ref):
        for h in range(MEM_HEADS):
            sl = slice(h * MEM_HD, (h + 1) * MEM_HD)
            vsl = slice(MEM_W + h * MEM_HD, MEM_W + (h + 1) * MEM_HD)
            p = _mem_probs(q_ref[:, sl].astype(bf16), kv_ref[:, sl].astype(bf16))
            o_ref[:, sl] = _dot(p.astype(bf16), kv_ref[:, vsl].astype(bf16)).astype(bf16)

    return pl.pallas_call(
        body, name="mem_fwd", grid=(S // tm,),
        in_specs=[pl.BlockSpec((tm, MEM_W), lambda i: (i, P_MEMQ // MEM_W)),
                  pl.BlockSpec((M, 2 * MEM_W), lambda i: (0, 0))],
        out_specs=pl.BlockSpec((tm, MEM_W), lambda i: (i, 0)),
        out_shape=jax.ShapeDtypeStruct((S, MEM_W), bf16),
        compiler_params=_params(("parallel",)),
    )(proj, kv)


def _mem_bwd(proj, kv, dy, S, tm=512):
    tm = min(tm, S)
    M = kv.shape[0]
    scale = MEM_HD ** -0.5

    def body(q_ref, kv_ref, dy_ref, dq_ref, dkv_ref):
        @pl.when(pl.program_id(0) == 0)
        def _():
            dkv_ref[...] = jnp.zeros_like(dkv_ref)

        for h in range(MEM_HEADS):
            sl = slice(h * MEM_HD, (h + 1) * MEM_HD)
            vsl = slice(MEM_W + h * MEM_HD, MEM_W + (h + 1) * MEM_HD)
            q = q_ref[:, sl].astype(bf16)
            k = kv_ref[:, sl].astype(bf16)
            v = kv_ref[:, vsl].astype(bf16)
            dyh = dy_ref[:, sl].astype(bf16)
            p = _mem_probs(q, k)
            dp = _dot(dyh, v, NT)
            ds = (p * (dp - jnp.sum(dp * p, axis=1, keepdims=True)) * scale).astype(bf16)
            dq_ref[:, sl] = _dot(ds, k).astype(bf16)
            dkv_ref[:, sl] += _dot(ds, q, TN)
            dkv_ref[:, vsl] += _dot(p.astype(bf16), dyh, TN)

    return pl.pallas_call(
        body, name="mem_bwd", grid=(S // tm,),
        in_specs=[pl.BlockSpec((tm, MEM_W), lambda i: (i, P_MEMQ // MEM_W)),
                  pl.BlockSpec((M, 2 * MEM_W), lambda i: (0, 0)),
                  pl.BlockSpec((tm, MEM_W), lambda i: (i, 0))],
        out_specs=[pl.BlockSpec((tm, MEM_W), lambda i: (i, 0)), pl.BlockSpec((M, 2 * MEM_W), lambda i: (0, 0))],
        out_shape=[jax.ShapeDtypeStruct((S, MEM_W), bf16), jax.ShapeDtypeStruct((M, 2 * MEM_W), f32)],
        compiler_params=_params(("arbitrary",)),
    )(proj, kv, dy)


def _merge_fwd(proj, t0, t1, t2, S, tm=512):
    tm = min(tm, S)

    def body(g_ref, t0_ref, t1_ref, t2_ref, o_ref):
        acc = jnp.zeros((tm, D), f32)
        for b, t_ref in enumerate((t0_ref, t1_ref, t2_ref)):
            acc = acc + _sigmoid(g_ref[:, b * D:(b + 1) * D]) * t_ref[...]
        o_ref[...] = acc.astype(bf16)

    row = pl.BlockSpec((tm, D), lambda i: (i, 0))
    return pl.pallas_call(
        body, name="merge_fwd", grid=(S // tm,),
        in_specs=[pl.BlockSpec((tm, 3 * D), lambda i: (i, P_GATE // (3 * D))), row, row, row],
        out_specs=row, out_shape=jax.ShapeDtypeStruct((S, D), bf16),
        compiler_params=_params(("parallel",)),
    )(proj, t0, t1, t2)


def _merge_bwd(proj, t0, t1, t2, dm, S, tm=512):
    tm = min(tm, S)

    def body(g_ref, t0_ref, t1_ref, t2_ref, dm_ref, d0_ref, d1_ref, d2_ref, dg_ref):
        dmv = dm_ref[...]
        for b, (t_ref, d_ref) in enumerate(((t0_ref, d0_ref), (t1_ref, d1_ref), (t2_ref, d2_ref))):
            sg = _sigmoid(g_ref[:, b * D:(b + 1) * D])
            d_ref[...] = (dmv * sg).astype(bf16)
            dg_ref[:, b * D:(b + 1) * D] = (dmv * t_ref[...] * sg * (1.0 - sg)).astype(bf16)

    row = pl.BlockSpec((tm, D), lambda i: (i, 0))
    return pl.pallas_call(
        body, name="merge_bwd", grid=(S // tm,),
        in_specs=[pl.BlockSpec((tm, 3 * D), lambda i: (i, P_GATE // (3 * D))), row, row, row, row],
        out_specs=[row, row, row, pl.BlockSpec((tm, 3 * D), lambda i: (i, 0))],
        out_shape=[jax.ShapeDtypeStruct((S, D), bf16)] * 3 + [jax.ShapeDtypeStruct((S, 3 * D), bf16)],
        compiler_params=_params(("parallel",)),
    )(proj, t0, t1, t2, dm)


def _loss_head(ff, g, h1, target, S, tm=512):
    tm = min(tm, S)

    def body(ff_ref, g_ref, h1_ref, t_ref, dh_ref, loss_ref):
        xv = ff_ref[...]
        r = lax.rsqrt(jnp.mean(xv * xv, axis=1, keepdims=True) + EPS)
        err = h1_ref[...] + xv * r * g_ref[...] - t_ref[...]
        dh_ref[...] = err * (1.0 / D)

        @pl.when(pl.program_id(0) == 0)
        def _():
            loss_ref[...] = jnp.zeros_like(loss_ref)

        loss_ref[...] += 0.5 * _sum_all(jnp.mean(err * err, axis=1, keepdims=True)) * jnp.ones((1, 128), f32)

    row = pl.BlockSpec((tm, D), lambda i: (i, 0))
    return pl.pallas_call(
        body, name="loss_head", grid=(S // tm,),
        in_specs=[row, pl.BlockSpec((1, D), lambda i: (0, 0)), row, row],
        out_specs=[row, pl.BlockSpec((1, 128), lambda i: (0, 0))],
        out_shape=[jax.ShapeDtypeStruct((S, D), f32), jax.ShapeDtypeStruct((1, 128), f32)],
        compiler_params=_params(("arbitrary",)),
    )(ff, g, h1, target)


def _local_step(x, mem, target, wts, late_rides, late_weights, small, rest_rides, w_in_ride):
    S = x.shape[0]
    M = mem.shape[0]
    pad = lambda v: jnp.pad(v, ((0, 0), (0, 128 - SSD_HEADS)))
    dtb_p, alog_p = pad(small["dt_bias"]), pad(small["a_log"])
    dsk_c = jnp.repeat(small["d_skip"], SB_HD, axis=1)

    u = _rms_fwd(x, small["norm_mix_pre"], name="norm_pre", out_dtype=bf16)
    rides = late_rides or (None, None, None, None)
    if late_rides:
        proj, lands_a = _mm(u, wts["w_main"], "nn", tm=1024, tn=1024, name="in_proj", ride=rides[0])
    else:
        proj, lands_a = _mm(u, wts["w_main"], "nn", tm=1024, tn=1024, name="in_proj"), []
    pdt = _mm(u, wts["w_dt"], "nn", tm=1024, tn=128, name="in_proj_dt")
    y_sb, tot_lk, lands_b = _sb_fwd(proj, S, rides[1])
    wts = dict(wts, **late_weights(0, lands_a))
    small = dict(small, conv_w=wts.pop("conv_w"))
    xc, xbc, lands_d = _conv_fwd(proj, small["conv_w"], small["conv_b"], S, rides[3])
    y_ssd, yn, hprev, lands_c = _ssd_fwd(xbc, proj, pdt, dtb_p, alog_p, dsk_c, small["ssd_norm"], S, rides[2])
    wts = dict(wts, **late_weights(1, lands_b), **late_weights(2, lands_c), **late_weights(3, lands_d))
    mn = _rms_fwd(mem, small["norm_mem"], name="norm_mem", out_dtype=bf16, tm=min(512, M))
    kv = _mm(mn, wts["w_mem_kv"], "nn", tm=M, tn=1024, name="mem_kv")
    y_mem = _mem_fwd(proj, kv, S)
    t0 = _mm(y_sb, wts["w_sb_out"], "nn", tm=1024, tn=1024, name="sb_out")
    t1 = _mm(yn, wts["w_ssd_out"], "nn", tm=1024, tn=1024, name="ssd_out")
    t2 = _mm(y_mem, wts["w_mem_out"], "nn", tm=1024, tn=1024, name="mem_out")
    merged = _merge_fwd(proj, t0, t1, t2, S)
    mix = _mm(merged, wts["w_o"], "nn", tm=1024, tn=1024, name="w_o")
    h1 = _rms_fwd(mix, small["norm_mix_post"], name="norm_mix_post", out_dtype=f32, residual=x)
    u2 = _rms_fwd(h1, small["norm_mlp_pre"], name="norm_mlp_pre", out_dtype=bf16)
    a_up, hrelu = _mm(u2, wts["w_up"], "nn", tm=1024, tn=1024, name="mlp_up", out_dtypes=(f32, bf16),
                      epi=lambda acc: (acc, jnp.square(jnp.maximum(acc, 0.0))))
    ff = _mm(hrelu, wts["w_down"], "nn", tm=1024, tn=1024, name="mlp_down")
    dh2, loss = _loss_head(ff, small["norm_mlp_post"], h1, target, S)

    g = {}
    dff, g["norm_mlp_post"] = _rms_bwd(ff, dh2, small["norm_mlp_post"], name="norm_mlp_post_bwd", out_dtype=bf16)
    da = _mm(dff, wts["w_down"], "nt", tm=1024, tn=1024, name="mlp_down_dx", out_dtypes=(bf16,),
             epi=lambda acc, a: (acc * (2.0 * jnp.maximum(a, 0.0)),), extras=(a_up,))
    g["w_down"] = _mm(hrelu, dff, "tn", tm=1024, tn=1024, name="mlp_down_dw")
    du2 = _mm(da, wts["w_up"], "nt", tm=1024, tn=1024, name="mlp_up_dx")
    g["w_up"] = _mm(u2, da, "tn", tm=1024, tn=1024, name="mlp_up_dw")
    dh1, g["norm_mlp_pre"] = _rms_bwd(h1, du2, small["norm_mlp_pre"], name="norm_mlp_pre_bwd", out_dtype=f32, add=dh2)
    dmix, g["norm_mix_post"] = _rms_bwd(mix, dh1, small["norm_mix_post"], name="norm_mix_post_bwd", out_dtype=bf16)
    dmerged = _mm(dmix, wts["w_o"], "nt", tm=1024, tn=1024, name="w_o_dx")
    g["w_o"] = _mm(merged, dmix, "tn", tm=1024, tn=1024, name="w_o_dw")
    dt0, dt1, dt2, dgl = _merge_bwd(proj, t0, t1, t2, dmerged, S)
    dy_sb = _mm(dt0, wts["w_sb_out"], "nt", tm=1024, tn=1024, name="sb_out_dx")
    g["w_sb_out"] = _mm(y_sb, dt0, "tn", tm=1024, tn=1024, name="sb_out_dw")
    dy_ssd = _mm(dt1, wts["w_ssd_out"], "nt", tm=1024, tn=1024, name="ssd_out_dx")
    g["w_ssd_out"] = _mm(yn, dt1, "tn", tm=1024, tn=1024, name="ssd_out_dw")
    dy_mem = _mm(dt2, wts["w_mem_out"], "nt", tm=1024, tn=1024, name="mem_out_dx")
    g["w_mem_out"] = _mm(y_mem, dt2, "tn", tm=1024, tn=1024, name="mem_out_dw")
    dmemq, dkv = _mem_bwd(proj, kv, dy_mem, S)
    g["w_mem_kv"] = _mm(mn, dkv, "tn", tm=1024, tn=1024, name="mem_kv_dw")
    dmn = _mm(dkv, wts["w_mem_kv"], "nt", tm=M, tn=1024, name="mem_kv_dx")
    _, g["norm_mem"] = _rms_bwd(mem, dmn, small["norm_mem"], name="norm_mem_bwd", out_dtype=bf16, tm=min(512, M))
    rides = rest_rides(g) if rest_rides else (None, None)
    dz, dxbc, ddt, g["ssd_norm"], dsk, dalog, ddtb, *lands_a = _ssd_bwd(
        dy_ssd, y_ssd, xbc, proj, pdt, hprev, dtb_p, alog_p, dsk_c, small["ssd_norm"], S, rides[0])
    g["d_skip"], g["a_log"], g["dt_bias"] = dsk[:, :SSD_HEADS], dalog[:, :SSD_HEADS], ddtb[:, :SSD_HEADS]
    dxbc_raw, dcw, g["conv_b"] = _conv_bwd(proj, xc, dxbc, small["conv_w"], S)
    g["conv_w"] = dcw[:CONV_K]
    dq, dk, dv, lands_b = _sb_bwd(proj, tot_lk, dy_sb, S, rides[1])
    g["rest_lands"] = lands_b + lands_a
    dproj = (dq, dk, dv, dxbc_raw, dgl, dmemq, dz)
    u_t = u.T
    g["w_main"] = [_mm(u_t, p, "nn", tm=512, tn=1024, name="in_proj_dw_%d" % i) for i, p in enumerate(dproj)]
    g["w_dt"] = _mm(u_t, ddt, "nn", tm=512, tn=128, name="in_proj_dt_dw")
    du_dt = _mm(ddt, wts["w_dt"], "nt", tm=1024, tn=1024, name="in_proj_dt_dx")
    du, g["w_in_lands"] = _mm_pieces_nt(dproj, wts["w_main"], du_dt, tm=512, tn=256, name="in_proj_dx",
                                        ride=w_in_ride(g) if w_in_ride else None)
    grad_x, g["norm_mix_pre"] = _rms_bwd(x, du, small["norm_mix_pre"], name="norm_pre_bwd", out_dtype=f32, add=dh1)
    return loss, grad_x, g


def _to_internal(w_in):
    sec = lambda r: w_in[:, r[0]:r[1]]
    w_main = jnp.concatenate([sec(R_QKV), sec(R_XBC), sec(R_GATE), sec(R_MEMQ), sec(R_Z)], axis=1)
    w_dt = jnp.pad(sec(R_DT), ((0, 0), (0, 128 - SSD_HEADS)))
    return w_main, w_dt


def _from_internal(pieces, g_dt):
    dq, dk, dv, dxbc, dgate, dmemq, dz = pieces
    return [dq, dk, dv, dz, dxbc, g_dt[:, :SSD_HEADS], dmemq, dgate]


def _w_in_slab(ordered, s, dtype):
    width = D_IN // N_SHARD
    lo, hi, off, parts = s * width, (s + 1) * width, 0, []
    for p in ordered:
        a, b = max(lo, off), min(hi, off + p.shape[1])
        if a < b:
            parts.append(p[:, a - off:b - off].astype(dtype))
        off += p.shape[1]
    return jnp.concatenate(parts, axis=1)


MESH = pl.DeviceIdType.MESH
ANY = pl.BlockSpec(memory_space=pl.ANY)


def _place():
    x, y, c = lax.axis_index("x"), lax.axis_index("y"), lax.axis_index("c")
    return (x, y, c), [(1 - x, y, c), (x, 1 - y, c), (1 - x, 1 - y, c)]


def _exchange_copy(mode, ins, lands, send, recv, a, k, me, peers, arriving):
    p = peers[k]
    theirs = 2 * p[0] + p[1]
    if mode == "gather":
        src, dst = ins[a], lands[a].at[theirs if arriving else me]
    else:
        src, dst = ins[a].at[theirs], lands[a].at[k]
    return pltpu.make_async_remote_copy(src_ref=src, dst_ref=dst, send_sem=send.at[a * 3 + k],
                                        recv_sem=recv.at[a * 3 + k], device_id=p, device_id_type=MESH)


class _Ride:
    def __init__(self, srcs, mode):
        self.srcs, self.mode, self.n = list(srcs), mode, len(srcs)
        n = self.n
        self.in_specs, self.out_specs = [ANY] * n, [ANY] * n
        self.out_shape = [
            jax.ShapeDtypeStruct((N_SHARD,) + s.shape if mode == "gather" else (3,) + s.shape[1:], s.dtype)
            for s in self.srcs]
        self.scratch = [pltpu.SemaphoreType.DMA((3 * n,)), pltpu.SemaphoreType.DMA((3 * n,)),
                        pltpu.SemaphoreType.DMA((n,))]

    def _own(self, ins, lnd, sems):
        if self.mode != "gather":
            return []
        me = 2 * lax.axis_index("x") + lax.axis_index("y")
        return [pltpu.make_async_copy(ins[a], lnd[a].at[me], sems[2].at[a]) for a in range(self.n)]

    def _far(self, ins, lnd, sems, arriving):
        (x, y, c), peers = _place()
        return [_exchange_copy(self.mode, ins, lnd, sems[0], sems[1], a, k, 2 * x + y, peers, arriving)
                for a in range(self.n) for k in range(3)]

    def start(self, ins, lnd, sems):
        for cp in self._own(ins, lnd, sems) + self._far(ins, lnd, sems, False):
            cp.start()

    def finish(self, ins, lnd, sems):
        for cp in self._far(ins, lnd, sems, True):
            cp.wait_recv()
        for cp in self._far(ins, lnd, sems, False):
            cp.wait_send()
        for cp in self._own(ins, lnd, sems):
            cp.wait()


def _gather_two_level(shards, name):
    n = len(shards)

    def body(*refs):
        ins, lnd = refs[:n], refs[n:2 * n]
        send, recv, loc = refs[2 * n:]
        (x, y, c), peers = _place()
        me = 2 * x + y

        def half(ref, a, core):
            rows = shards[a].shape[0] // 2
            return ref.at[pl.ds(core * rows, rows)]

        def copy(a, j, slot, core, to):
            return pltpu.make_async_remote_copy(
                src_ref=half(ins[a], a, core) if j < 3 else half(lnd[a].at[slot], a, core),
                dst_ref=half(lnd[a].at[slot], a, core), send_sem=send.at[6 * a + j], recv_sem=recv.at[6 * a + j],
                device_id=to, device_id_type=MESH)

        own = [pltpu.make_async_copy(ins[a], lnd[a].at[me], loc.at[a]) for a in range(n)]
        far = [copy(a, k, me, c, peers[k]) for a in range(n) for k in range(3)]
        for cp in own + far:
            cp.start()
        passed = []
        for a in range(n):
            for k, p in enumerate(peers):
                theirs = 2 * p[0] + p[1]
                copy(a, k, theirs, c, p).wait_recv()
                passed.append(copy(a, 3 + k, theirs, c, (x, y, 1 - c)))
                passed[-1].start()
        for a in range(n):
            for k, p in enumerate(peers):
                copy(a, 3 + k, 2 * p[0] + p[1], 1 - c, (x, y, 1 - c)).wait_recv()
        for cp in far + passed:
            cp.wait_send()
        for cp in own:
            cp.wait()

    return pl.pallas_call(
        body, name=name, in_specs=[ANY] * n, out_specs=[ANY] * n,
        out_shape=[jax.ShapeDtypeStruct((N_SHARD,) + s.shape, s.dtype) for s in shards],
        scratch_shapes=[pltpu.SemaphoreType.DMA((6 * n,)), pltpu.SemaphoreType.DMA((6 * n,)),
                        pltpu.SemaphoreType.DMA((n,))],
    )(*shards)


def _exchange_packets(packet):
    def body(pk, pk_out, send, recv, loc):
        x, y, c = lax.axis_index("x"), lax.axis_index("y"), lax.axis_index("c")
        lin = 4 * x + 2 * y + c
        own = pltpu.make_async_copy(pk, pk_out.at[lin], loc.at[0])
        own.start()

        def pk_copy(m, slot):
            dev = (x ^ ((m >> 2) & 1), y ^ ((m >> 1) & 1), c ^ (m & 1))
            return pltpu.make_async_remote_copy(
                src_ref=pk, dst_ref=pk_out.at[slot], send_sem=send.at[m - 1], recv_sem=recv.at[m - 1],
                device_id=dev, device_id_type=MESH)

        sent = [pk_copy(m, lin) for m in range(1, N_DEV)]
        for cp in sent:
            cp.start()
        for m in range(1, N_DEV):
            pk_copy(m, lin ^ m).wait_recv()
        for cp in sent:
            cp.wait_send()
        own.wait()

    return pl.pallas_call(
        body, name="exchange_packets", in_specs=[ANY], out_specs=ANY,
        out_shape=jax.ShapeDtypeStruct((N_DEV,) + packet.shape, packet.dtype),
        scratch_shapes=[pltpu.SemaphoreType.DMA((N_DEV - 1,)), pltpu.SemaphoreType.DMA((N_DEV - 1,)),
                        pltpu.SemaphoreType.DMA((1,))],
    )(packet)


def _swap_sibling(parts, name):
    n = len(parts)

    def body(*refs):
        ins, outs = refs[:n], refs[n:2 * n]
        send, recv = refs[2 * n:]
        x, y, c = lax.axis_index("x"), lax.axis_index("y"), lax.axis_index("c")
        cps = [pltpu.make_async_remote_copy(
            src_ref=ins[a], dst_ref=outs[a], send_sem=send.at[a], recv_sem=recv.at[a],
            device_id=(x, y, 1 - c), device_id_type=MESH) for a in range(n)]
        for cp in cps:
            cp.start()
        for cp in cps:
            cp.wait_recv()
        for cp in cps:
            cp.wait_send()

    return pl.pallas_call(
        body, name=name,
        in_specs=[ANY] * n, out_specs=[ANY] * n,
        out_shape=[jax.ShapeDtypeStruct(p.shape, p.dtype) for p in parts],
        scratch_shapes=[pltpu.SemaphoreType.DMA((n,)), pltpu.SemaphoreType.DMA((n,))],
    )(*parts)


BLOCK_ELEMS = 256 * 1024


def _row_tile(R, C):
    tr = max(8, (BLOCK_ELEMS // C) // 8 * 8)
    while R % tr:
        tr -= 8
    return min(tr, R)


def _sum_parts(own, stack, name, out_dtype=f32):
    k = stack.shape[0]
    R, C = stack.shape[1:]
    tr = _row_tile(R, C)

    def body(*refs):
        o_ref = refs[-1]
        acc = refs[0][...].astype(f32)
        for r in refs[1:-1]:
            acc = acc + r[...].astype(f32)
        o_ref[...] = acc.astype(out_dtype)

    row = pl.BlockSpec((tr, C), lambda i: (i, 0))
    specs = ([row] if own is not None else []) + [
        pl.BlockSpec((None, tr, C), functools.partial(lambda i, j: (j, i, 0), j=j)) for j in range(k)]
    args = ([own] if own is not None else []) + [stack] * k
    return pl.pallas_call(
        body, name=name, grid=(R // tr,), in_specs=specs, out_specs=row,
        out_shape=jax.ShapeDtypeStruct((R, C), out_dtype), compiler_params=_params(("parallel",)),
    )(*args)


def _adamw(w, m, v, g_parts, name):
    R, C = w.shape
    tr = _row_tile(R, C)
    n_g = len(g_parts)

    def body(w_ref, m_ref, v_ref, *rest):
        g = rest[0][...]
        for r in rest[1:n_g]:
            g = g + r[...]
        g_ref, d_ref, nm_ref, nv_ref = rest[n_g:]
        nm = ADAM_B1 * m_ref[...] + (1.0 - ADAM_B1) * g
        nv = ADAM_B2 * v_ref[...] + (1.0 - ADAM_B2) * jnp.square(g)
        m_hat = nm / (1.0 - ADAM_B1 ** ADAM_STEP)
        v_hat = nv / (1.0 - ADAM_B2 ** ADAM_STEP)
        g_ref[...] = g
        d_ref[...] = -ADAM_LR * (m_hat / (jnp.sqrt(v_hat) + ADAM_EPS) + ADAM_WD * w_ref[...])
        nm_ref[...] = nm
        nv_ref[...] = nv

    row = pl.BlockSpec((tr, C), lambda i: (i, 0))
    return pl.pallas_call(
        body, name=name, grid=(R // tr,), in_specs=[row] * (3 + n_g), out_specs=[row] * 4,
        out_shape=[jax.ShapeDtypeStruct((R, C), f32)] * 4, compiler_params=_params(("parallel",)),
    )(w, m, v, *g_parts)


BIG = ("w_in", "w_mem_kv", "w_sb_out", "w_ssd_out", "w_mem_out", "w_o", "w_up", "w_down")
LATE = ("w_sb_out", "w_ssd_out", "w_mem_out", "w_o", "w_up", "w_down")
REST = BIG[1:]
COL_SHARDED = ("w_in", "w_mem_kv", "w_up")
SMALL = ("norm_mix_pre", "conv_w", "conv_b", "dt_bias", "a_log", "d_skip", "ssd_norm", "norm_mem",
         "norm_mix_post", "norm_mlp_pre", "norm_mlp_post")
WEIGHTS = ("norm_mix_pre", "w_in", "conv_w", "conv_b", "dt_bias", "a_log", "d_skip", "ssd_norm", "norm_mem",
           "w_mem_kv", "w_sb_out", "w_ssd_out", "w_mem_out", "w_o", "norm_mix_post", "norm_mlp_pre", "w_up",
           "w_down", "norm_mlp_post")
PK_ROWS = 184


def _pack(vecs):
    flat = jnp.concatenate([v.reshape(-1) for v in vecs])
    return jnp.pad(flat, (0, PK_ROWS * 128 - flat.shape[0])).reshape(PK_ROWS, 128)


def _unpack(pk, shapes):
    flat = pk.reshape(-1)
    out, off = [], 0
    for s in shapes:
        n = 1
        for d in s:
            n *= d
        out.append(flat[off:off + n].reshape(s))
        off += n
    return out


def _full_from_slabs(name, slabs):
    if name in COL_SHARDED:
        return slabs.transpose(1, 0, 2).reshape(slabs.shape[1], -1)
    return slabs.reshape(-1, slabs.shape[2])


def _slabs_from_full(name, g):
    if name in COL_SHARDED:
        return g.reshape(g.shape[0], N_SHARD, -1).transpose(1, 0, 2)
    return g.reshape(N_SHARD, -1, g.shape[1])


def kernel(x, mem, norm_mix_pre, w_in, conv_w, conv_b, dt_bias, a_log, d_skip, ssd_norm, norm_mem, w_mem_kv, w_sb_out, w_ssd_out, w_mem_out, w_o, norm_mix_post, norm_mlp_pre, w_up, w_down, norm_mlp_post, loss_target, m_norm_mix_pre, m_w_in, m_conv_w, m_conv_b, m_dt_bias, m_a_log, m_d_skip, m_ssd_norm, m_norm_mem, m_w_mem_kv, m_w_sb_out, m_w_ssd_out, m_w_mem_out, m_w_o, m_norm_mix_post, m_norm_mlp_pre, m_w_up, m_w_down, m_norm_mlp_post, v_norm_mix_pre, v_w_in, v_conv_w, v_conv_b, v_dt_bias, v_a_log, v_d_skip, v_ssd_norm, v_norm_mem, v_w_mem_kv, v_w_sb_out, v_w_ssd_out, v_w_mem_out, v_w_o, v_norm_mix_post, v_norm_mlp_pre, v_w_up, v_w_down, v_norm_mlp_post):
    env = dict(locals())
    w = {n: env[n] for n in WEIGHTS}
    mo = {n: env["m_" + n] for n in WEIGHTS}
    vo = {n: env["v_" + n] for n in WEIGHTS}
    shard = 2 * lax.axis_index("x") + lax.axis_index("y")

    first = _gather_two_level([w["w_in"][0].astype(bf16)], "gather_first")
    w_main, w_dt = _to_internal(_full_from_slabs("w_in", first[0]))
    wts = dict(w_main=w_main, w_dt=w_dt)
    ride_names = (LATE[:4], LATE[4:5], LATE[5:], ("w_mem_kv",))
    late_rides = tuple(_Ride([w[n][0].astype(bf16) for n in names] + ([w["conv_w"][0]] if i == 0 else []), "gather")
                       for i, names in enumerate(ride_names))

    def late_weights(i, lands):
        full = {n: _full_from_slabs(n, s) for n, s in zip(ride_names[i], lands)}
        if i == 0:
            full["conv_w"] = lands[-1].transpose(1, 0, 2).reshape(CONV_K, CONV_DIM)
        return full

    def rest_rides(g):
        slabs = [_slabs_from_full(n, g[n]).astype(bf16) for n in REST]
        return _Ride(slabs[5:], "scatter"), _Ride(slabs[:5], "scatter")

    core = lax.axis_index("c")
    half = D // 2

    def w_in_ride(g):
        ordered = _from_internal(g["w_main"], g["w_dt"])
        stack = jnp.stack([_w_in_slab(ordered, s, bf16) for s in range(N_SHARD)])
        keep = lax.dynamic_slice_in_dim(stack, core * half, half, axis=1)
        away = lax.dynamic_slice_in_dim(stack, (1 - core) * half, half, axis=1)
        (got,) = _swap_sibling([away], "w_in_halves_out")
        wide = lambda a: a.reshape(N_SHARD * half, -1)
        chip = _sum_parts(wide(keep), wide(got)[None], "sum_cores_w_in", bf16).reshape(N_SHARD, half, -1)
        own = lax.switch(shard, [functools.partial(_w_in_slab, ordered, s, f32) for s in range(N_SHARD)])
        own = lax.dynamic_slice_in_dim(own, core * half, half, axis=0)
        g["w_in_own"] = _sum_parts(own, lax.dynamic_index_in_dim(got, shard, 0, keepdims=True), "sum_cores_w_in_own")
        return _Ride([chip], "scatter")

    small = {n: w[n] for n in SMALL if n != "conv_w"}
    loss, grad_x, g = _local_step(x[0], mem[0], loss_target[0], wts, late_rides, late_weights, small,
                                  rest_rides, w_in_ride)
    out_g, out_d, out_m, out_v = {}, {}, {}, {}

    def apply(n, g_parts):
        res = _adamw(w[n][0], mo[n][0], vo[n][0], g_parts, name="adamw_" + n)
        out_g[n], out_d[n], out_m[n], out_v[n] = [r[None] for r in res]

    mine = _sum_parts(g["w_in_own"], g["w_in_lands"][0], name="sum_chips_w_in")
    (theirs,) = _swap_sibling([mine], "w_in_halves_back")
    g_w_in = lax.dynamic_update_slice_in_dim(jnp.zeros((D, D_IN // N_SHARD), f32), mine, core * half, axis=0)
    apply("w_in", [lax.dynamic_update_slice_in_dim(g_w_in, theirs, (1 - core) * half, axis=0)])

    packets = _exchange_packets(_pack([g[n] for n in SMALL] + [loss[:, :1]]))
    partial = []
    for n, r in zip(REST, g["rest_lands"]):
        own = lax.dynamic_index_in_dim(_slabs_from_full(n, g[n]), shard, 0, keepdims=False)
        partial.append(_sum_parts(own, r, name="sum_chips_" + n))
    other = _swap_sibling(partial, "swap_sibling")

    for n, p, q in zip(REST, partial, other):
        apply(n, [p, q])
    tot = _sum_parts(None, packets, name="sum_packets")
    shapes = [g[n].shape for n in SMALL] + [(1, 1)]
    sm = dict(zip(SMALL + ("loss",), _unpack(tot, shapes)))
    sm["conv_w"] = lax.dynamic_slice_in_dim(sm["conv_w"], shard * (CONV_DIM // N_SHARD), CONV_DIM // N_SHARD, axis=1)
    own_small = lambda d: _pack([d[n].reshape(sm[n].shape) for n in SMALL])
    res = _adamw(own_small(w), own_small(mo), own_small(vo), [own_small(sm)], name="adamw_small")
    own_shapes = [sm[n].shape for n in SMALL]
    for store, r in zip((out_g, out_d, out_m, out_v), res):
        for n, val in zip(SMALL, _unpack(r, own_shapes)):
            store[n] = val.reshape(w[n].shape)

    outs = [sm["loss"].reshape(()), grad_x[None]]
    for store in (out_g, out_d, out_m, out_v):
        outs += [store[n] for n in WEIGHTS]
    return tuple(outs)
```

```python
import functools

import jax
import jax.numpy as jnp
from jax import lax
from jax.experimental import pallas as pl
from jax.experimental.pallas import tpu as pltpu

f32 = jnp.float32
bf16 = jnp.bfloat16

D = 1024
EPS = 1e-6
SB_HD = 64
SSD_INNER = 2048
SSD_HEADS = 32
SSD_GROUPS = 4
SSD_N = 128
SSD_L = 128
CONV_K = 4
CONV_DIM = 3072
MEM_HEADS = 4
MEM_HD = 256
D_FF = 4096
D_IN = 12320
N_SHARD = 4
N_DEV = 8

P_QKV, P_XBC, P_GATE, P_MEMQ, P_Z, P_DT, P_TOT = 0, 3072, 6144, 9216, 10240, 12288, 12416
R_QKV, R_Z, R_XBC, R_DT, R_MEMQ, R_GATE = (0, 3072), (3072, 5120), (5120, 8192), (8192, 8224), (8224, 9248), (9248, 12320)

ADAM_LR = 0.001
ADAM_B1 = 0.9
ADAM_B2 = 0.999
ADAM_EPS = 1e-08
ADAM_WD = 0.01
ADAM_STEP = 10

VMEM_LIMIT = 56 * 1024 * 1024

NN = (((1,), (0,)), ((), ()))
NT = (((1,), (1,)), ((), ()))
TN = (((0,), (0,)), ((), ()))


def _dot(a, b, dims=NN):
    return lax.dot_general(a, b, dims, preferred_element_type=f32)


def _params(sem=None):
    return pltpu.CompilerParams(dimension_semantics=sem, vmem_limit_bytes=VMEM_LIMIT)


def _sigmoid(x):
    return 1.0 / (1.0 + jnp.exp(-x))


def _split2(x):
    hi = x.astype(bf16)
    lo = (x - hi.astype(f32)).astype(bf16)
    return hi, lo


def _split3(x):
    hi = x.astype(bf16)
    r = x - hi.astype(f32)
    mid = r.astype(bf16)
    lo = (r - mid.astype(f32)).astype(bf16)
    return hi, mid, lo


def _mm(a, b, mode, *, tm, tn, name, out_dtypes=(f32,), epi=None, extras=(), ride=None):
    M = a.shape[1] if mode == "tn" else a.shape[0]
    N = b.shape[0] if mode == "nt" else b.shape[1]
    tm, tn = min(tm, M), min(tn, N)
    if mode == "nn":
        (M, K), N = a.shape, b.shape[1]
        a_spec = pl.BlockSpec((tm, K), lambda i, j: (i, 0))
        b_spec = pl.BlockSpec((K, tn), lambda i, j: (0, j))
        dims = NN
    elif mode == "nt":
        (M, K), N = a.shape, b.shape[0]
        a_spec = pl.BlockSpec((tm, K), lambda i, j: (i, 0))
        b_spec = pl.BlockSpec((tn, K), lambda i, j: (j, 0))
        dims = NT
    else:
        (K, M), N = a.shape, b.shape[1]
        a_spec = pl.BlockSpec((K, tm), lambda i, j: (0, i))
        b_spec = pl.BlockSpec((K, tn), lambda i, j: (0, j))
        dims = TN
    assert M % tm == 0 and N % tn == 0, (name, M, N, tm, tn)
    n_ex, n_out = len(extras), len(out_dtypes)
    n_r = ride.n if ride else 0
    o_spec = pl.BlockSpec((tm, tn), lambda i, j: (i, j))
    grid = (M // tm, N // tn)

    def body(a_ref, b_ref, *rest):
        r_ins = rest[n_ex:n_ex + n_r]
        outs = rest[n_ex + n_r:n_ex + n_r + n_out]
        r_lnd, r_sems = rest[n_ex + n_r + n_out:n_ex + 2 * n_r + n_out], rest[n_ex + 2 * n_r + n_out:]
        i, j = pl.program_id(0), pl.program_id(1)
        if ride:
            pl.when((i == 0) & (j == 0))(lambda: ride.start(r_ins, r_lnd, r_sems))
        acc = _dot(a_ref[...].astype(bf16), b_ref[...].astype(bf16), dims)
        res = (acc,) if epi is None else epi(acc, *[e[...] for e in rest[:n_ex]])
        for o_ref, r in zip(outs, res):
            o_ref[...] = r.astype(o_ref.dtype)
        if ride:
            pl.when((i == grid[0] - 1) & (j == grid[1] - 1))(lambda: ride.finish(r_ins, r_lnd, r_sems))

    out = pl.pallas_call(
        body, name=name, grid=grid,
        in_specs=[a_spec, b_spec] + [o_spec] * n_ex + (ride.in_specs if ride else []),
        out_specs=[o_spec] * n_out + (ride.out_specs if ride else []),
        out_shape=[jax.ShapeDtypeStruct((M, N), dt) for dt in out_dtypes] + (ride.out_shape if ride else []),
        scratch_shapes=ride.scratch if ride else [],
        compiler_params=_params(("arbitrary", "arbitrary") if ride else ("parallel", "parallel")),
    )(a, b, *extras, *(ride.srcs if ride else []))
    if ride:
        return (out[0] if n_out == 1 else out[:n_out]), list(out[n_out:])
    return out[0] if n_out == 1 else out


def _mm_pieces_nt(pieces, b, add, *, tm, tn, name, ride):
    M, N = pieces[0].shape[0], b.shape[0]
    n_p, n_r = len(pieces), (ride.n if ride else 0)
    o_spec = pl.BlockSpec((tm, tn), lambda i, j: (i, j))
    grid = (M // tm, N // tn)

    def body(*refs):
        b_ref, add_ref = refs[n_p:n_p + 2]
        r_ins, o_ref = refs[n_p + 2:n_p + 2 + n_r], refs[n_p + 2 + n_r]
        r_lnd, r_sems = refs[n_p + 3 + n_r:n_p + 3 + 2 * n_r], refs[n_p + 3 + 2 * n_r:]
        i, j = pl.program_id(0), pl.program_id(1)
        if ride:
            pl.when((i == 0) & (j == 0))(lambda: ride.start(r_ins, r_lnd, r_sems))
        acc, off = add_ref[...], 0
        for r in refs[:n_p]:
            acc = acc + _dot(r[...], b_ref[:, off:off + r.shape[1]], NT)
            off += r.shape[1]
        o_ref[...] = acc
        if ride:
            pl.when((i == grid[0] - 1) & (j == grid[1] - 1))(lambda: ride.finish(r_ins, r_lnd, r_sems))

    out = pl.pallas_call(
        body, name=name, grid=grid,
        in_specs=[pl.BlockSpec((tm, p.shape[1]), lambda i, j: (i, 0)) for p in pieces]
        + [pl.BlockSpec((tn, b.shape[1]), lambda i, j: (j, 0)), o_spec] + (ride.in_specs if ride else []),
        out_specs=[o_spec] + (ride.out_specs if ride else []),
        out_shape=[jax.ShapeDtypeStruct((M, N), f32)] + (ride.out_shape if ride else []),
        scratch_shapes=ride.scratch if ride else [],
        compiler_params=_params(("arbitrary", "arbitrary")),
    )(*pieces, b, add, *(ride.srcs if ride else []))
    return out[0], list(out[1:])


def _rms_fwd(x, g, *, name, out_dtype, residual=None, tm=512, ride=None):
    S, C = x.shape
    tm = min(tm, S)
    has_res = residual is not None
    n_in = 1 if has_res else 0
    n_r = ride.n if ride else 0
    steps = S // tm

    def body(x_ref, g_ref, *rest):
        y_ref = rest[n_in + n_r]
        r_ins, r_lnd, r_sems = rest[n_in:n_in + n_r], rest[n_in + n_r + 1:n_in + 2 * n_r + 1], rest[n_in + 2 * n_r + 1:]
        if ride:
            pl.when(pl.program_id(0) == 0)(lambda: ride.start(r_ins, r_lnd, r_sems))
        xv = x_ref[...]
        r = lax.rsqrt(jnp.mean(xv * xv, axis=1, keepdims=True) + EPS)
        y = xv * r * g_ref[...]
        if has_res:
            y = y + rest[0][...]
        y_ref[...] = y.astype(out_dtype)
        if ride:
            pl.when(pl.program_id(0) == steps - 1)(lambda: ride.finish(r_ins, r_lnd, r_sems))

    row = pl.BlockSpec((tm, C), lambda i: (i, 0))
    vec = pl.BlockSpec((1, C), lambda i: (0, 0))
    args = (x, g) + ((residual,) if has_res else ()) + (tuple(ride.srcs) if ride else ())
    out = pl.pallas_call(
        body, name=name, grid=(steps,),
        in_specs=[row, vec] + ([row] if has_res else []) + (ride.in_specs if ride else []),
        out_specs=[row] + (ride.out_specs if ride else []),
        out_shape=[jax.ShapeDtypeStruct((S, C), out_dtype)] + (ride.out_shape if ride else []),
        scratch_shapes=ride.scratch if ride else [],
        compiler_params=_params(("arbitrary",) if ride else ("parallel",)),
    )(*args)
    return (out[0], list(out[1:])) if ride else out[0]


def _rms_bwd(x, dy, g, *, name, out_dtype, add=None, tm=512):
    S, C = x.shape
    tm = min(tm, S)
    has_add = add is not None

    def body(x_ref, dy_ref, g_ref, *rest):
        dx_ref, dg_ref = rest[-2], rest[-1]
        xv = x_ref[...]
        dyv = dy_ref[...].astype(f32)
        r = lax.rsqrt(jnp.mean(xv * xv, axis=1, keepdims=True) + EPS)
        xh = xv * r
        dxh = dyv * g_ref[...]
        dx = r * (dxh - xh * jnp.mean(dxh * xh, axis=1, keepdims=True))
        if has_add:
            dx = dx + rest[0][...]
        dx_ref[...] = dx.astype(out_dtype)

        @pl.when(pl.program_id(0) == 0)
        def _():
            dg_ref[...] = jnp.zeros_like(dg_ref)

        dg_ref[...] += jnp.sum(dyv * xh, axis=0, keepdims=True)

    row = pl.BlockSpec((tm, C), lambda i: (i, 0))
    vec = pl.BlockSpec((1, C), lambda i: (0, 0))
    args = (x, dy, g) + ((add,) if has_add else ())
    return pl.pallas_call(
        body, name=name, grid=(S // tm,),
        in_specs=[row, row, vec] + ([row] if has_add else []),
        out_specs=[row, vec],
        out_shape=[jax.ShapeDtypeStruct((S, C), out_dtype), jax.ShapeDtypeStruct((1, C), f32)],
        compiler_params=_params(("arbitrary",)),
    )(*args)


SB_T = 128
SB_SPENT = -120.0
SB_QB = 4
SB_TAIL = 3
SB_GROUPS = (4, 2, 1)
SB_GROUPS_BWD = (4, 2, 1)


def _sb_masks():
    lane = lax.broadcasted_iota(jnp.int32, (1, 128), 1)
    m_a = (lane < SB_HD).astype(f32)
    return m_a, 1.0 - m_a


def _chunks(a, n):
    return [a[:, u * SB_T:(u + 1) * SB_T] for u in range(n)]


def _cat(parts, axis):
    return parts[0] if len(parts) == 1 else jnp.concatenate(parts, axis=axis)


def _mask_last(a, n, mask):
    if mask is None:
        return a
    parts = _chunks(a, n)
    return _cat(parts[:-1] + [jnp.where(mask, parts[-1], 0.0)], 1)


def _sb_logits(z, n, mask):
    l1p = jnp.log(1.0 + jnp.exp(-jnp.abs(z)))
    lb = jnp.minimum(z, 0.0) - l1p
    return lb, _mask_last(lb - z, n, mask)


def _by_count(i, most, fn):
    return lax.switch(jnp.minimum(i, most - 1), [functools.partial(fn, n) for n in range(1, most + 1)])


def _chunk_matmul(parts_list, u_mat):
    out = _dot(_cat(parts_list, 0), u_mat)
    return [out[u * SB_T:(u + 1) * SB_T] for u in range(len(parts_list))]


def _chunk_cumsum(lk, n, u_mat):
    hi = lk.astype(bf16)
    lo = (lk - hi.astype(f32)).astype(bf16)
    out = _chunk_matmul(_chunks(hi, n) + _chunks(lo, n), u_mat)
    return [out[u] + out[n + u] for u in range(n)]


def _sb_fwd(proj, S, ride=None):
    nq = S // SB_T
    n_pairs = D // 128
    scale = SB_HD ** -0.5
    n_r = ride.n if ride else 0

    def body(q_ref, k_ref, v_ref, *rest):
        o_ref, t_ref = rest[n_r:n_r + 2]
        step_i = pl.program_id(1)
        if ride:
            pl.when((pl.program_id(0) == 0) & (step_i == 0))(
                lambda: ride.start(rest[:n_r], rest[n_r + 2:2 * n_r + 2], rest[2 * n_r + 2:]))
        m_a, m_b = _sb_masks()
        r_i = lax.broadcasted_iota(jnp.int32, (SB_T, SB_T), 0)
        c_i = lax.broadcasted_iota(jnp.int32, (SB_T, SB_T), 1)
        u_mat = (r_i > c_i).astype(bf16)
        causal = c_i < r_i
        lane_a = lax.broadcasted_iota(jnp.int32, (1, 128), 1) < SB_HD
        q_all = q_ref[...] * scale
        q_hs = [((q * m_a).astype(bf16), (q * m_b).astype(bf16))
                for q in (q_all[b * SB_T:(b + 1) * SB_T] for b in range(SB_QB))]

        def group(q_h, j_lo, n, carry, mask):
            acc, c_a, c_b = carry
            rows = pl.ds(pl.multiple_of(j_lo * SB_T, SB_T), n * SB_T)
            k = k_ref[rows, :].astype(bf16)
            v = v_ref[rows, :]
            zs = [_dot(q_b, k, NT) for q_b in q_h]
            lbk = [_sb_logits(z, n, mask) for z in zs]
            parts = [_chunk_cumsum(lk, n, u_mat) for _, lk in lbk]
            ws, cs = [], []
            for (lb, lk), part, c in zip(lbk, parts, (c_a, c_b)):
                lb_c, lk_c = _chunks(lb, n), _chunks(lk, n)
                w_c = [None] * n
                for u in reversed(range(n)):
                    w_c[u] = jnp.exp(lb_c[u] + c + part[u])
                    c = c + jnp.sum(lk_c[u], axis=1, keepdims=True)
                ws.append(_mask_last(_cat(w_c, 1), n, mask).astype(bf16))
                cs.append(c)
            v_b = v.astype(bf16)
            acc = acc + jnp.where(lane_a, _dot(ws[0], v_b), _dot(ws[1], v_b))
            return acc, cs[0], cs[1]

        zero_c = jnp.zeros((SB_T, 1), f32)
        init = (jnp.zeros((SB_T, 128), f32), zero_c, zero_c)
        blocks = [(step_i * SB_QB + b, q_hs[b]) for b in range(SB_QB)]

        def whole_tails():
            return tuple(group(q_h, i - SB_TAIL + 1, SB_TAIL, init, causal) for i, q_h in blocks)

        def short_tails():
            return tuple(_by_count(i, SB_TAIL, functools.partial(
                lambda n, i, q_h: group(q_h, i - n + 1, n, init, causal), i=i, q_h=q_h)) for i, q_h in blocks)

        carries = lax.cond(step_i * SB_QB >= SB_TAIL - 1, whole_tails, short_tails)

        def spent(cr):
            return (jnp.max(jnp.maximum(cr[1], cr[2])) < SB_SPENT).astype(jnp.int32)

        lane = lax.broadcasted_iota(jnp.int32, (1, 128), 1)
        for b, ((i, q_h), carry) in enumerate(zip(blocks, carries)):
            state = (i - jnp.minimum(i, SB_TAIL - 1), spent(carry), carry)
            for n in SB_GROUPS:
                def step(st, n=n, q_h=q_h):
                    left, _, cr = st
                    cr = group(q_h, left - n, n, cr, None)
                    return left - n, spent(cr), cr

                state = lax.while_loop(lambda st, n=n: (st[0] >= n) & (st[1] == 0), step, state)
            left, _, carry = state
            rows = slice(b * SB_T, (b + 1) * SB_T)
            o_ref[rows, :] = carry[0]
            t_ref[rows, :] = (jnp.where(lane == 0, carry[1], 0.0) + jnp.where(lane == SB_HD, carry[2], 0.0)
                              + jnp.where(lane == 1, left.astype(f32), 0.0))
        if ride:
            pl.when((pl.program_id(0) == n_pairs - 1) & (step_i == nq // SB_QB - 1))(
                lambda: ride.finish(rest[:n_r], rest[n_r + 2:2 * n_r + 2], rest[2 * n_r + 2:]))

    qs = pl.BlockSpec((SB_QB * SB_T, 128), lambda h, i: (i, h))
    out = pl.pallas_call(
        body, name="sb_fwd", grid=(n_pairs, nq // SB_QB),
        in_specs=[qs,
                  pl.BlockSpec((S, 128), lambda h, i: (0, n_pairs + h)),
                  pl.BlockSpec((S, 128), lambda h, i: (0, 2 * n_pairs + h))] + (ride.in_specs if ride else []),
        out_specs=[qs, qs] + (ride.out_specs if ride else []),
        out_shape=[jax.ShapeDtypeStruct((S, D), f32)] * 2 + (ride.out_shape if ride else []),
        scratch_shapes=ride.scratch if ride else [],
        compiler_params=_params(("arbitrary", "arbitrary")),
    )(proj, proj, proj, *(ride.srcs if ride else []))
    return out[0], out[1], list(out[2:])


def _sb_bwd(proj, tot_lk, do, S, ride=None):
    nq = S // SB_T
    n_pairs = D // 128
    scale = SB_HD ** -0.5
    n_r = ride.n if ride else 0

    def body(q_ref, k_ref, v_ref, t_ref, do_ref, *rest):
        dq_ref, dk_ref, dv_ref = rest[n_r:n_r + 3]
        dk_acc, dv_acc = rest[2 * n_r + 3:2 * n_r + 5]
        r_ins, r_lnd, r_sems = rest[:n_r], rest[n_r + 3:2 * n_r + 3], rest[2 * n_r + 5:]
        step_i = pl.program_id(1)
        if ride:
            pl.when((pl.program_id(0) == 0) & (step_i == 0))(lambda: ride.start(r_ins, r_lnd, r_sems))
        m_a, m_b = _sb_masks()
        r_i = lax.broadcasted_iota(jnp.int32, (SB_T, SB_T), 0)
        c_i = lax.broadcasted_iota(jnp.int32, (SB_T, SB_T), 1)
        u_inc = (r_i <= c_i).astype(bf16)
        u_exc = (r_i < c_i).astype(bf16)
        causal = c_i < r_i

        @pl.when(step_i == 0)
        def _():
            dk_acc[...] = jnp.zeros_like(dk_acc)
            dv_acc[...] = jnp.zeros_like(dv_acc)

        lane = lax.broadcasted_iota(jnp.int32, (1, 128), 1)
        blocks = []
        for b in range(SB_QB):
            rows_b = slice(b * SB_T, (b + 1) * SB_T)
            i = step_i * SB_QB + b
            q = q_ref[rows_b, :] * scale
            dov = do_ref[rows_b, :]
            tv = t_ref[rows_b, :]
            heads = []
            for m, first in ((m_a, 0), (m_b, SB_HD)):
                tot = jnp.sum(jnp.where(lane == first, tv, 0.0), axis=1, keepdims=True)
                heads.append(((q * m).astype(bf16), (dov * m).astype(bf16), tot, m))
            lowest = jnp.clip(jnp.max(jnp.where(lane == 1, tv, 0.0)).astype(jnp.int32), 0, i)
            blocks.append((i, heads, lowest))

        def group(heads, j_lo, n, carry, mask):
            dq_acc, cp_a, cp_b, ce_a, ce_b = carry
            rows = pl.ds(pl.multiple_of(j_lo * SB_T, SB_T), n * SB_T)
            k_f = k_ref[rows, :]
            k = k_f.astype(bf16)
            v = v_ref[rows, :].astype(bf16)
            zs = [_dot(h[0], k, NT) for h in heads]
            dws = [_dot(h[1], v, NT) for h in heads]
            lbk = [_sb_logits(z, n, mask) for z in zs]
            parts = [_chunk_cumsum(lk, n, u_inc) for _, lk in lbk]
            ws, es, cps = [], [], []
            for (lb, lk), part, dw, h, cp in zip(lbk, parts, dws, heads, (cp_a, cp_b)):
                lb_c, lk_c = _chunks(lb, n), _chunks(lk, n)
                w_c = []
                for u in range(n):
                    w_c.append(jnp.exp(lb_c[u] + (h[2] - cp) - part[u]))
                    cp = cp + jnp.sum(lk_c[u], axis=1, keepdims=True)
                w = _mask_last(_cat(w_c, 1), n, mask)
                ws.append(w)
                es.append(dw * w)
                cps.append(cp)
            e_parts = [_chunk_matmul(_chunks(e.astype(bf16), n), u_exc) for e in es]
            dzs, ces = [], []
            for (lb, _), e, e_part, ce in zip(lbk, es, e_parts, (ce_a, ce_b)):
                e_c = _chunks(e, n)
                big_c = []
                for u in range(n):
                    big_c.append(ce + e_part[u])
                    ce = ce + jnp.sum(e_c[u], axis=1, keepdims=True)
                sig = jnp.exp(lb)
                dz = _mask_last(e * (1.0 - sig) - _cat(big_c, 1) * sig, n, mask)
                dzs.append(dz.astype(bf16))
                ces.append(ce)
            dk_t = jnp.zeros((n * SB_T, 128), f32)
            dv_t = jnp.zeros((n * SB_T, 128), f32)
            dq_acc = dq_acc + jnp.where(lane < SB_HD, _dot(dzs[0], k), _dot(dzs[1], k))
            for dz_b, w, h in zip(dzs, ws, heads):
                dk_t = dk_t + _dot(dz_b, h[0], TN)
                dv_t = dv_t + _dot(w.astype(bf16), h[1], TN)
            dk_acc[rows, :] += dk_t
            dv_acc[rows, :] += dv_t
            return dq_acc, cps[0], cps[1], ces[0], ces[1]

        zc = jnp.zeros((SB_T, 1), f32)
        carries = []
        for i, heads, lowest in blocks:
            carry = (jnp.zeros((SB_T, 128), f32), zc, zc, zc, zc)
            done = lowest
            tail_lo = i - jnp.minimum(i, SB_TAIL - 1)
            for n in SB_GROUPS_BWD:
                trips = (tail_lo - done) // n
                carry = lax.fori_loop(
                    0, trips, functools.partial(
                        lambda gi, cr, n, done, heads: group(heads, done + gi * n, n, cr, None),
                        n=n, done=done, heads=heads),
                    carry)
                done = done + trips * n
            carries.append(carry)

        def whole_tails():
            return tuple(group(heads, i - SB_TAIL + 1, SB_TAIL, cr, causal)
                         for (i, heads, _), cr in zip(blocks, carries))

        def short_tails():
            return tuple(_by_count(i, SB_TAIL, functools.partial(
                lambda n, i, heads, cr: group(heads, i - n + 1, n, cr, causal), i=i, heads=heads, cr=cr))
                for (i, heads, _), cr in zip(blocks, carries))

        carries = lax.cond(step_i * SB_QB >= SB_TAIL - 1, whole_tails, short_tails)
        for b, carry in enumerate(carries):
            dq_ref[b * SB_T:(b + 1) * SB_T, :] = (carry[0] * scale).astype(bf16)

        @pl.when(step_i == nq // SB_QB - 1)
        def _():
            dk_ref[...] = dk_acc[...].astype(bf16)
            dv_ref[...] = dv_acc[...].astype(bf16)

        if ride:
            pl.when((pl.program_id(0) == n_pairs - 1) & (step_i == nq // SB_QB - 1))(
                lambda: ride.finish(r_ins, r_lnd, r_sems))

    qs = pl.BlockSpec((SB_QB * SB_T, 128), lambda h, i: (i, h))
    full = pl.BlockSpec((S, 128), lambda h, i: (0, h))
    out = pl.pallas_call(
        body, name="sb_bwd", grid=(n_pairs, nq // SB_QB),
        in_specs=[qs,
                  pl.BlockSpec((S, 128), lambda h, i: (0, n_pairs + h)),
                  pl.BlockSpec((S, 128), lambda h, i: (0, 2 * n_pairs + h)),
                  qs, qs] + (ride.in_specs if ride else []),
        out_specs=[qs, full, full] + (ride.out_specs if ride else []),
        out_shape=[jax.ShapeDtypeStruct((S, D), bf16)] * 3 + (ride.out_shape if ride else []),
        scratch_shapes=[pltpu.VMEM((S, 128), f32), pltpu.VMEM((S, 128), f32)] + (ride.scratch if ride else []),
        compiler_params=_params(("arbitrary", "arbitrary")),
    )(proj, proj, proj, tot_lk, do, *(ride.srcs if ride else []))
    return out[0], out[1], out[2], list(out[3:])


CONV_CB = 256
HALO = 8


def _conv_fwd(proj, conv_w, conv_b, S, ride=None):
    tr = min(512, S)
    n_r = ride.n if ride else 0
    n_c = CONV_DIM // CONV_CB

    def body(x_ref, w_ref, b_ref, *rest):
        xc_ref, xbc_ref = rest[n_r:n_r + 2]
        r_ins, r_lnd, r_sems = rest[:n_r], rest[n_r + 2:2 * n_r + 2], rest[2 * n_r + 2:]
        if ride:
            pl.when(pl.program_id(0) == 0)(lambda: ride.start(r_ins, r_lnd, r_sems))
        w = w_ref[...]
        for t in range(S // tr):
            cur = x_ref[t * tr:(t + 1) * tr, :]
            halo = x_ref[t * tr - HALO:t * tr, :] if t else jnp.zeros((HALO, CONV_CB), f32)
            win = jnp.concatenate([halo, cur], axis=0)
            acc = b_ref[...] + w[CONV_K - 1:CONV_K, :] * cur
            for k in range(CONV_K - 1):
                acc = acc + w[k:k + 1, :] * pltpu.roll(win, CONV_K - 1 - k, 0)[HALO:, :]
            xc_ref[t * tr:(t + 1) * tr, :] = acc
            xbc_ref[t * tr:(t + 1) * tr, :] = acc * _sigmoid(acc)
        if ride:
            pl.when(pl.program_id(0) == n_c - 1)(lambda: ride.finish(r_ins, r_lnd, r_sems))

    col = pl.BlockSpec((S, CONV_CB), lambda c: (0, c))
    out = pl.pallas_call(
        body, name="conv_fwd", grid=(n_c,),
        in_specs=[pl.BlockSpec((S, CONV_CB), lambda c: (0, P_XBC // CONV_CB + c)),
                  pl.BlockSpec((CONV_K, CONV_CB), lambda c: (0, c)),
                  pl.BlockSpec((1, CONV_CB), lambda c: (0, c))] + (ride.in_specs if ride else []),
        out_specs=[col, col] + (ride.out_specs if ride else []),
        out_shape=[jax.ShapeDtypeStruct((S, CONV_DIM), f32)] * 2 + (ride.out_shape if ride else []),
        scratch_shapes=ride.scratch if ride else [],
        compiler_params=_params(("arbitrary",)),
    )(proj, conv_w, conv_b, *(ride.srcs if ride else []))
    return out[0], out[1], list(out[2:])


def _conv_bwd(proj, xc, dxbc, conv_w, S):
    tr = min(512, S)

    def body(x_ref, xc_ref, dy_ref, w_ref, dx_ref, dw_ref, db_ref, dxc_s):
        w = w_ref[...]
        xcv = xc_ref[...]
        sg = _sigmoid(xcv)
        dxc_s[0:S, :] = dy_ref[...] * (sg * (1.0 + xcv * (1.0 - sg)))
        dxc_s[S:S + HALO, :] = jnp.zeros((HALO, CONV_CB), f32)
        dws = [jnp.zeros((1, CONV_CB), f32) for _ in range(CONV_K)]
        db = jnp.zeros((1, CONV_CB), f32)
        for t in range(S // tr):
            cur = x_ref[t * tr:(t + 1) * tr, :]
            halo = x_ref[t * tr - HALO:t * tr, :] if t else jnp.zeros((HALO, CONV_CB), f32)
            win = jnp.concatenate([halo, cur], axis=0)
            dwin = dxc_s[t * tr:(t + 1) * tr + HALO, :]
            dcur = dwin[0:tr, :]
            db = db + jnp.sum(dcur, axis=0, keepdims=True)
            dws[CONV_K - 1] = dws[CONV_K - 1] + jnp.sum(dcur * cur, axis=0, keepdims=True)
            dx = w[CONV_K - 1:CONV_K, :] * dcur
            for k in range(CONV_K - 1):
                sh = CONV_K - 1 - k
                dws[k] = dws[k] + jnp.sum(dcur * pltpu.roll(win, sh, 0)[HALO:, :], axis=0, keepdims=True)
                dx = dx + w[k:k + 1, :] * pltpu.roll(dwin, tr + HALO - sh, 0)[0:tr, :]
            dx_ref[t * tr:(t + 1) * tr, :] = dx.astype(bf16)
        dw_ref[...] = jnp.concatenate(dws + [jnp.zeros((8 - CONV_K, CONV_CB), f32)], axis=0)
        db_ref[...] = db

    col = pl.BlockSpec((S, CONV_CB), lambda c: (0, c))
    return pl.pallas_call(
        body, name="conv_bwd", grid=(CONV_DIM // CONV_CB,),
        in_specs=[pl.BlockSpec((S, CONV_CB), lambda c: (0, P_XBC // CONV_CB + c)), col, col,
                  pl.BlockSpec((CONV_K, CONV_CB), lambda c: (0, c))],
        out_specs=[col, pl.BlockSpec((8, CONV_CB), lambda c: (0, c)), pl.BlockSpec((1, CONV_CB), lambda c: (0, c))],
        out_shape=[jax.ShapeDtypeStruct((S, CONV_DIM), bf16), jax.ShapeDtypeStruct((8, CONV_DIM), f32),
                   jax.ShapeDtypeStruct((1, CONV_DIM), f32)],
        scratch_shapes=[pltpu.VMEM((S + HALO, CONV_CB), f32)],
        compiler_params=_params(("parallel",)),
    )(proj, xc, dxbc, conv_w)


N_PAIR = SSD_HEADS // 2
NEG = -1e30


def _softplus(x):
    return jnp.maximum(x, 0.0) + jnp.log(1.0 + jnp.exp(-jnp.abs(x)))


def _ssd_common(dtr, dtb, alog):
    L = SSD_L
    r_i = lax.broadcasted_iota(jnp.int32, (L, L), 0)
    c_i = lax.broadcasted_iota(jnp.int32, (L, L), 1)
    dt = _softplus(dtr + dtb)
    a = -jnp.exp(alog)
    da = dt * a
    lower = (r_i >= c_i).astype(bf16)
    upper = (r_i <= c_i).astype(bf16)
    parts = _split3(da)
    a_cs = sum(_dot(lower, p) for p in parts)
    a_cs_t = sum(_dot(p, upper, TN) for p in parts)
    return dt, a, a_cs, a_cs_t, r_i >= c_i


def _pair_vec(lane, v, h):
    return jnp.where(lane < SB_HD, v[:, h:h + 1], v[:, h + 1:h + 2])


def _decay_mat(a_cs, a_cs_t, h, tril):
    return jnp.exp(jnp.where(tril, a_cs[:, h:h + 1] - a_cs_t[h:h + 1, :], NEG))


def _ssd_fwd(xbc, proj, pdt, dt_bias_p, a_log_p, d_skip_c, ssd_norm, S, ride=None):
    L = SSD_L
    nc = S // L
    n_r = ride.n if ride else 0

    def body(xbc_ref, dt_ref, z_ref, dtb_ref, alog_ref, dsk_ref, gn_ref, *rest):
        y_ref, yn_ref, hp_ref = rest[n_r:n_r + 3]
        state = rest[2 * n_r + 3]
        r_ins, r_lnd, r_sems = rest[:n_r], rest[n_r + 3:2 * n_r + 3], rest[2 * n_r + 4:]
        c = pl.program_id(0)
        if ride:
            pl.when(c == 0)(lambda: ride.start(r_ins, r_lnd, r_sems))

        @pl.when(c == 0)
        def _():
            state[...] = jnp.zeros_like(state)

        hp_ref[0] = state[...]
        lane = lax.broadcasted_iota(jnp.int32, (1, 128), 1)
        row128 = lax.broadcasted_iota(jnp.int32, (128, 1), 0)
        m_a, m_b = _sb_masks()
        dt, a, a_cs, a_cs_t, tril = _ssd_common(dt_ref[...], dtb_ref[...], alog_ref[...])
        a_last = a_cs[L - 1:L, :]
        for g in range(SSD_GROUPS):
            b_g = xbc_ref[:, SSD_INNER + g * SSD_N:SSD_INNER + (g + 1) * SSD_N].astype(bf16)
            c_g = xbc_ref[:, SSD_INNER + (SSD_GROUPS + g) * SSD_N:SSD_INNER + (SSD_GROUPS + g + 1) * SSD_N].astype(bf16)
            cb = _dot(c_g, b_g, NT)
            for pr in range(4):
                h = 8 * g + 2 * pr
                pi = h // 2
                cols = slice(pi * 128, (pi + 1) * 128)
                xs = xbc_ref[:, cols]
                x = xs * _pair_vec(lane, dt, h)
                acs = _pair_vec(lane, a_cs, h)
                al = _pair_vec(lane, a_last, h)
                w_a = (cb * _decay_mat(a_cs, a_cs_t, h, tril)).astype(bf16)
                w_b = (cb * _decay_mat(a_cs, a_cs_t, h + 1, tril)).astype(bf16)
                yd = _dot(w_a, (x * m_a).astype(bf16)) + _dot(w_b, (x * m_b).astype(bf16))
                hp = state[pi]
                yo = _dot(c_g, hp.astype(bf16), NT) * jnp.exp(acs)
                y_ref[:, cols] = yd + yo + dsk_ref[:, cols] * xs
                dec = jnp.exp(jnp.where(row128 < SB_HD, a_last[:, h:h + 1], a_last[:, h + 1:h + 2]))
                state[pi] = hp * dec + _dot((x * jnp.exp(al - acs)).astype(bf16), b_g, TN)
        zz = z_ref[...]
        y2 = y_ref[...] * (zz * _sigmoid(zz))
        gw = SSD_INNER // SSD_GROUPS
        for g in range(SSD_GROUPS):
            yg = y2[:, g * gw:(g + 1) * gw]
            rg = lax.rsqrt(jnp.mean(yg * yg, axis=1, keepdims=True) + EPS)
            yn_ref[:, g * gw:(g + 1) * gw] = (yg * rg * gn_ref[:, g * gw:(g + 1) * gw]).astype(bf16)
        if ride:
            pl.when(c == nc - 1)(lambda: ride.finish(r_ins, r_lnd, r_sems))

    vec128 = pl.BlockSpec((1, 128), lambda c: (0, 0))
    vecin = pl.BlockSpec((1, SSD_INNER), lambda c: (0, 0))
    rows = pl.BlockSpec((L, SSD_INNER), lambda c: (c, 0))
    out = pl.pallas_call(
        body, name="ssd_fwd", grid=(nc,),
        in_specs=[pl.BlockSpec((L, CONV_DIM), lambda c: (c, 0)),
                  pl.BlockSpec((L, 128), lambda c: (c, 0)),
                  pl.BlockSpec((L, SSD_INNER), lambda c: (c, P_Z // SSD_INNER)),
                  vec128, vec128, vecin, vecin] + (ride.in_specs if ride else []),
        out_specs=[rows, rows, pl.BlockSpec((1, N_PAIR, 128, SSD_N), lambda c: (c, 0, 0, 0))]
        + (ride.out_specs if ride else []),
        out_shape=[jax.ShapeDtypeStruct((S, SSD_INNER), f32), jax.ShapeDtypeStruct((S, SSD_INNER), bf16),
                   jax.ShapeDtypeStruct((nc, N_PAIR, 128, SSD_N), f32)] + (ride.out_shape if ride else []),
        scratch_shapes=[pltpu.VMEM((N_PAIR, 128, SSD_N), f32)] + (ride.scratch if ride else []),
        compiler_params=_params(("arbitrary",)),
    )(xbc, pdt, proj, dt_bias_p, a_log_p, d_skip_c, ssd_norm, *(ride.srcs if ride else []))
    return out[0], out[1], out[2], list(out[3:])


def _sum_all(v):
    return jnp.sum(jnp.sum(v, axis=1, keepdims=True), axis=0, keepdims=True)


def _ssd_bwd(dyn, y, xbc, proj, pdt, hprev, dt_bias_p, a_log_p, d_skip_c, ssd_norm, S, ride=None):
    L = SSD_L
    nc = S // L
    n_r = ride.n if ride else 0

    col = lax.broadcasted_iota(jnp.int32, (2 * SSD_INNER, 128), 0)
    head = lax.broadcasted_iota(jnp.int32, (2 * SSD_INNER, 128), 1)
    sel_pair = (col[:SSD_INNER] // SB_HD == head[:SSD_INNER]).astype(bf16)
    sel_head = (col // 128 == head).astype(bf16)

    def body(*refs):
        (dyn_ref, y_ref, xbc_ref, dt_ref, z_ref, hp_ref, dtb_ref, alog_ref, dsk_ref, gn_ref,
         selp_ref, selh_ref) = refs[:12]
        dz_ref, dxbc_ref, ddt_ref, dgn_ref, dsk_out, dalog_ref, ddtb_ref = refs[12 + n_r:19 + n_r]
        dstate, dy_s, st_a, st_q, st_d, st_x, dat = refs[19 + 2 * n_r:26 + 2 * n_r]
        r_ins, r_lnd, r_sems = refs[12:12 + n_r], refs[19 + n_r:19 + 2 * n_r], refs[26 + 2 * n_r:]
        c = pl.program_id(0)
        if ride:
            pl.when(c == 0)(lambda: ride.start(r_ins, r_lnd, r_sems))

        @pl.when(c == 0)
        def _():
            dat[...] = jnp.zeros_like(dat)
            dstate[...] = jnp.zeros_like(dstate)
            dgn_ref[...] = jnp.zeros_like(dgn_ref)
            dsk_out[...] = jnp.zeros_like(dsk_out)
            dalog_ref[...] = jnp.zeros_like(dalog_ref)
            ddtb_ref[...] = jnp.zeros_like(ddtb_ref)

        lane = lax.broadcasted_iota(jnp.int32, (1, 128), 1)
        row128 = lax.broadcasted_iota(jnp.int32, (128, 1), 0)
        rowl = lax.broadcasted_iota(jnp.int32, (L, 1), 0)
        m_a, m_b = _sb_masks()
        dtr = dt_ref[...]
        dt, a, a_cs, a_cs_t, tril = _ssd_common(dtr, dtb_ref[...], alog_ref[...])
        a_last = a_cs[L - 1:L, :]

        zz = z_ref[...]
        sg = _sigmoid(zz)
        silu = zz * sg
        yv = y_ref[...]
        y2 = yv * silu
        gw = SSD_INNER // SSD_GROUPS
        for g in range(SSD_GROUPS):
            sl = slice(g * gw, (g + 1) * gw)
            yg = y2[:, sl]
            rg = lax.rsqrt(jnp.mean(yg * yg, axis=1, keepdims=True) + EPS)
            yh = yg * rg
            dyn_g = dyn_ref[:, sl]
            dgn_ref[:, sl] += jnp.sum(dyn_g * yh, axis=0, keepdims=True)
            dyh = dyn_g * gn_ref[:, sl]
            dy2 = rg * (dyh - yh * jnp.mean(dyh * yh, axis=1, keepdims=True))
            dy_s[:, sl] = dy2 * silu[:, sl]
            dz_ref[:, sl] = (dy2 * yv[:, sl] * (sg[:, sl] * (1.0 + zz[:, sl] * (1.0 - sg[:, sl])))).astype(bf16)

        last_row = jnp.zeros((1, 128), f32)
        dsk_acc = jnp.zeros((1, 128), f32)
        for g in range(SSD_GROUPS):
            bsl = slice(SSD_INNER + g * SSD_N, SSD_INNER + (g + 1) * SSD_N)
            csl = slice(SSD_INNER + (SSD_GROUPS + g) * SSD_N, SSD_INNER + (SSD_GROUPS + g + 1) * SSD_N)
            b_g = xbc_ref[:, bsl].astype(bf16)
            c_g = xbc_ref[:, csl].astype(bf16)
            cb = _dot(c_g, b_g, NT)
            dcb = jnp.zeros((L, L), f32)
            dc_g = jnp.zeros((L, SSD_N), f32)
            db_g = jnp.zeros((L, SSD_N), f32)
            for pr in range(4):
                h = 8 * g + 2 * pr
                pi = h // 2
                cols = slice(pi * 128, (pi + 1) * 128)
                xs = xbc_ref[:, cols]
                dt_p = _pair_vec(lane, dt, h)
                x = xs * dt_p
                acs = _pair_vec(lane, a_cs, h)
                al = _pair_vec(lane, a_last, h)
                e_a = jnp.exp(acs)
                dte = jnp.exp(al - acs)
                m_mat_a = _decay_mat(a_cs, a_cs_t, h, tril)
                m_mat_b = _decay_mat(a_cs, a_cs_t, h + 1, tril)
                dyp = dy_s[:, cols]
                dsk = dsk_ref[:, cols]
                d_hn = dstate[pi]
                hp = hp_ref[0, pi]
                dy_a = (dyp * m_a).astype(bf16)
                dy_b = (dyp * m_b).astype(bf16)
                x_b = x.astype(bf16)
                gm_a = _dot(dy_a, x_b, NT) * m_mat_a
                gm_b = _dot(dy_b, x_b, NT) * m_mat_b
                dcb = dcb + gm_a + gm_b
                dx_d = _dot((cb * m_mat_a).astype(bf16), dy_a, TN) + _dot((cb * m_mat_b).astype(bf16), dy_b, TN)
                dx_s = _dot(b_g, d_hn.astype(bf16), NT) * dte
                dx = dx_d + dx_s
                dxbc_ref[:, cols] = dx * dt_p + dsk * dyp
                xdxs = x * dx_s
                st_x[:, cols] = xdxs
                st_a[:, cols] = dyp * (_dot(c_g, hp.astype(bf16), NT) * e_a) - xdxs
                st_d[:, cols] = dx * xs
                hh = d_hn * hp
                dsk_row = jnp.sum(dyp * xs, axis=0, keepdims=True)
                dec = jnp.exp(jnp.where(row128 < SB_HD, a_last[:, h:h + 1], a_last[:, h + 1:h + 2]))
                for hd, m, gm in ((h, m_a, gm_a), (h + 1, m_b, gm_b)):
                    half = slice(0, SB_HD) if hd == h else slice(SB_HD, 128)
                    qm = gm * cb
                    st_q[:, hd * 128:(hd + 1) * 128] = qm
                    dat[hd:hd + 1, :] = jnp.sum(qm, axis=0, keepdims=True)
                    hh_sum = jnp.sum(jnp.sum(hh[half, :], axis=0, keepdims=True), axis=1, keepdims=True)
                    last_row = jnp.where(lane == hd, jnp.exp(a_last[:, hd:hd + 1]) * hh_sum, last_row)
                    dsk_acc = jnp.where(lane == hd, jnp.sum(dsk_row * m, axis=1, keepdims=True), dsk_acc)
                dye = (dyp * e_a).astype(bf16)
                dc_g = dc_g + _dot(dye, hp.astype(bf16))
                db_g = db_g + _dot((x * dte).astype(bf16), d_hn.astype(bf16))
                dstate[pi] = dec * d_hn + _dot(dye, c_g, TN)
            dcb_b = dcb.astype(bf16)
            dxbc_ref[:, csl] = dc_g + _dot(dcb_b, b_g)
            dxbc_ref[:, bsl] = db_g + _dot(dcb_b, c_g, TN)

        r_i = lax.broadcasted_iota(jnp.int32, (L, L), 0)
        c_i = lax.broadcasted_iota(jnp.int32, (L, L), 1)
        rev = (r_i <= c_i).astype(bf16)

        def head_sums(st, sel, split=_split2):
            return sum(_dot(p, sel[...]) for p in split(st[...]))

        last_row = last_row + jnp.sum(head_sums(st_x, selp_ref), axis=0, keepdims=True)
        d_acs = (head_sums(st_a, selp_ref) + head_sums(st_q, selh_ref, _split3)
                 + jnp.where(rowl == L - 1, last_row, 0.0))
        ddt_x = head_sums(st_d, selp_ref)
        dda = sum(_dot(rev, p) for p in _split3(d_acs)) - sum(_dot(rev, p, NT) for p in _split3(dat[...]))
        ddt = ddt_x + dda * a
        dalog_ref[...] += jnp.sum(dda * dt, axis=0, keepdims=True) * a
        ddtr = jnp.where(lane < SSD_HEADS, ddt * _sigmoid(dtr + dtb_ref[...]), 0.0)
        ddt_ref[...] = ddtr.astype(bf16)
        ddtb_ref[...] += jnp.sum(ddtr, axis=0, keepdims=True)
        dsk_out[...] += dsk_acc
        if ride:
            pl.when(c == nc - 1)(lambda: ride.finish(r_ins, r_lnd, r_sems))

    rv = lambda c: nc - 1 - c
    vec128 = pl.BlockSpec((1, 128), lambda c: (0, 0))
    vecin = pl.BlockSpec((1, SSD_INNER), lambda c: (0, 0))
    rows = pl.BlockSpec((L, SSD_INNER), lambda c: (rv(c), 0))
    return pl.pallas_call(
        body, name="ssd_bwd", grid=(nc,),
        in_specs=[rows, rows,
                  pl.BlockSpec((L, CONV_DIM), lambda c: (rv(c), 0)),
                  pl.BlockSpec((L, 128), lambda c: (rv(c), 0)),
                  pl.BlockSpec((L, SSD_INNER), lambda c: (rv(c), P_Z // SSD_INNER)),
                  pl.BlockSpec((1, N_PAIR, 128, SSD_N), lambda c: (rv(c), 0, 0, 0)),
                  vec128, vec128, vecin, vecin,
                  pl.BlockSpec((SSD_INNER, 128), lambda c: (0, 0)),
                  pl.BlockSpec((2 * SSD_INNER, 128), lambda c: (0, 0))] + (ride.in_specs if ride else []),
        out_specs=[rows, pl.BlockSpec((L, CONV_DIM), lambda c: (rv(c), 0)),
                   pl.BlockSpec((L, 128), lambda c: (rv(c), 0)), vecin, vec128, vec128, vec128]
        + (ride.out_specs if ride else []),
        out_shape=[jax.ShapeDtypeStruct((S, SSD_INNER), bf16), jax.ShapeDtypeStruct((S, CONV_DIM), f32),
                   jax.ShapeDtypeStruct((S, 128), bf16), jax.ShapeDtypeStruct((1, SSD_INNER), f32),
                   jax.ShapeDtypeStruct((1, 128), f32), jax.ShapeDtypeStruct((1, 128), f32),
                   jax.ShapeDtypeStruct((1, 128), f32)] + (ride.out_shape if ride else []),
        scratch_shapes=[pltpu.VMEM((N_PAIR, 128, SSD_N), f32), pltpu.VMEM((L, SSD_INNER), f32),
                        pltpu.VMEM((L, SSD_INNER), f32), pltpu.VMEM((L, 2 * SSD_INNER), f32),
                        pltpu.VMEM((L, SSD_INNER), f32), pltpu.VMEM((L, SSD_INNER), f32),
                        pltpu.VMEM((128, L), f32)]
        + (ride.scratch if ride else []),
        compiler_params=_params(("arbitrary",)),
    )(dyn, y, xbc, pdt, proj, hprev, dt_bias_p, a_log_p, d_skip_c, ssd_norm, sel_pair, sel_head,
      *(ride.srcs if ride else []))


MEM_W = MEM_HEADS * MEM_HD


def _mem_probs(q, k):
    s = _dot(q, k, NT) * (MEM_HD ** -0.5)
    s = s - jnp.max(s, axis=1, keepdims=True)
    p = jnp.exp(s)
    return p / jnp.sum(p, axis=1, keepdims=True)


def _mem_fwd(proj, kv, S, tm=512):
    tm = min(tm, S)
    M = kv.shape[0]

    def body(q_ref, kv_ref, o_ref):
        for h in range(MEM_HEADS):
            sl = slice(h * MEM_HD, (h + 1) * MEM_HD)
            vsl = slice(MEM_W + h * MEM_HD, MEM_W + (h + 1) * MEM_HD)
            p = _mem_probs(q_ref[:, sl].astype(bf16), kv_ref[:, sl].astype(bf16))
            o_ref[:, sl] = _dot(p.astype(bf16), kv_ref[:, vsl].astype(bf16)).astype(bf16)

    return pl.pallas_call(
        body, name="mem_fwd", grid=(S // tm,),
        in_specs=[pl.BlockSpec((tm, MEM_W), lambda i: (i, P_MEMQ // MEM_W)),
                  pl.BlockSpec((M, 2 * MEM_W), lambda i: (0, 0))],
        out_specs=pl.BlockSpec((tm, MEM_W), lambda i: (i, 0)),
        out_shape=jax.ShapeDtypeStruct((S, MEM_W), bf16),
        compiler_params=_params(("parallel",)),
    )(proj, kv)


def _mem_bwd(proj, kv, dy, S, tm=512):
    tm = min(tm, S)
    M = kv.shape[0]
    scale = MEM_HD ** -0.5

    def body(q_ref, kv_ref, dy_ref, dq_ref, dkv_ref):
        @pl.when(pl.program_id(0) == 0)
        def _():
            dkv_ref[...] = jnp.zeros_like(dkv_ref)

        for h in range(MEM_HEADS):
            sl = slice(h * MEM_HD, (h + 1) * MEM_HD)
            vsl = slice(MEM_W + h * MEM_HD, MEM_W + (h + 1) * MEM_HD)
            q = q_ref[:, sl].astype(bf16)
            k = kv_ref[:, sl].astype(bf16)
            v = kv_ref[:, vsl].astype(bf16)
            dyh = dy_ref[:, sl].astype(bf16)
            p = _mem_probs(q, k)
            dp = _dot(dyh, v, NT)
            ds = (p * (dp - jnp.sum(dp * p, axis=1, keepdims=True)) * scale).astype(bf16)
            dq_ref[:, sl] = _dot(ds, k).astype(bf16)
            dkv_ref[:, sl] += _dot(ds, q, TN)
            dkv_ref[:, vsl] += _dot(p.astype(bf16), dyh, TN)

    return pl.pallas_call(
        body, name="mem_bwd", grid=(S // tm,),
        in_specs=[pl.BlockSpec((tm, MEM_W), lambda i: (i, P_MEMQ // MEM_W)),
                  pl.BlockSpec((M, 2 * MEM_W), lambda i: (0, 0)),
                  pl.BlockSpec((tm, MEM_W), lambda i: (i, 0))],
        out_specs=[pl.BlockSpec((tm, MEM_W), lambda i: (i, 0)), pl.BlockSpec((M, 2 * MEM_W), lambda i: (0, 0))],
        out_shape=[jax.ShapeDtypeStruct((S, MEM_W), bf16), jax.ShapeDtypeStruct((M, 2 * MEM_W), f32)],
        compiler_params=_params(("arbitrary",)),
    )(proj, kv, dy)


def _merge_fwd(proj, t0, t1, t2, S, tm=512):
    tm = min(tm, S)

    def body(g_ref, t0_ref, t1_ref, t2_ref, o_ref):
        acc = jnp.zeros((tm, D), f32)
        for b, t_ref in enumerate((t0_ref, t1_ref, t2_ref)):
            acc = acc + _sigmoid(g_ref[:, b * D:(b + 1) * D]) * t_ref[...]
        o_ref[...] = acc.astype(bf16)

    row = pl.BlockSpec((tm, D), lambda i: (i, 0))
    return pl.pallas_call(
        body, name="merge_fwd", grid=(S // tm,),
        in_specs=[pl.BlockSpec((tm, 3 * D), lambda i: (i, P_GATE // (3 * D))), row, row, row],
        out_specs=row, out_shape=jax.ShapeDtypeStruct((S, D), bf16),
        compiler_params=_params(("parallel",)),
    )(proj, t0, t1, t2)


def _merge_bwd(proj, t0, t1, t2, dm, S, tm=512):
    tm = min(tm, S)

    def body(g_ref, t0_ref, t1_ref, t2_ref, dm_ref, d0_ref, d1_ref, d2_ref, dg_ref):
        dmv = dm_ref[...]
        for b, (t_ref, d_ref) in enumerate(((t0_ref, d0_ref), (t1_ref, d1_ref), (t2_ref, d2_ref))):
            sg = _sigmoid(g_ref[:, b * D:(b + 1) * D])
            d_ref[...] = (dmv * sg).astype(bf16)
            dg_ref[:, b * D:(b + 1) * D] = (dmv * t_ref[...] * sg * (1.0 - sg)).astype(bf16)

    row = pl.BlockSpec((tm, D), lambda i: (i, 0))
    return pl.pallas_call(
        body, name="merge_bwd", grid=(S // tm,),
        in_specs=[pl.BlockSpec((tm, 3 * D), lambda i: (i, P_GATE // (3 * D))), row, row, row, row],
        out_specs=[row, row, row, pl.BlockSpec((tm, 3 * D), lambda i: (i, 0))],
        out_shape=[jax.ShapeDtypeStruct((S, D), bf16)] * 3 + [jax.ShapeDtypeStruct((S, 3 * D), bf16)],
        compiler_params=_params(("parallel",)),
    )(proj, t0, t1, t2, dm)


def _loss_head(ff, g, h1, target, S, tm=512):
    tm = min(tm, S)

    def body(ff_ref, g_ref, h1_ref, t_ref, dh_ref, loss_ref):
        xv = ff_ref[...]
        r = lax.rsqrt(jnp.mean(xv * xv, axis=1, keepdims=True) + EPS)
        err = h1_ref[...] + xv * r * g_ref[...] - t_ref[...]
        dh_ref[...] = err * (1.0 / D)

        @pl.when(pl.program_id(0) == 0)
        def _():
            loss_ref[...] = jnp.zeros_like(loss_ref)

        loss_ref[...] += 0.5 * _sum_all(jnp.mean(err * err, axis=1, keepdims=True)) * jnp.ones((1, 128), f32)

    row = pl.BlockSpec((tm, D), lambda i: (i, 0))
    return pl.pallas_call(
        body, name="loss_head", grid=(S // tm,),
        in_specs=[row, pl.BlockSpec((1, D), lambda i: (0, 0)), row, row],
        out_specs=[row, pl.BlockSpec((1, 128), lambda i: (0, 0))],
        out_shape=[jax.ShapeDtypeStruct((S, D), f32), jax.ShapeDtypeStruct((1, 128), f32)],
        compiler_params=_params(("arbitrary",)),
    )(ff, g, h1, target)


def _local_step(x, mem, target, wts, late_rides, late_weights, small, rest_rides, w_in_ride):
    S = x.shape[0]
    M = mem.shape[0]
    pad = lambda v: jnp.pad(v, ((0, 0), (0, 128 - SSD_HEADS)))
    dtb_p, alog_p = pad(small["dt_bias"]), pad(small["a_log"])
    dsk_c = jnp.repeat(small["d_skip"], SB_HD, axis=1)

    if callable(wts):
        first_ride, first_weights = wts(None)
        u, lands_0 = _rms_fwd(x, small["norm_mix_pre"], name="norm_pre", out_dtype=bf16, ride=first_ride)
        wts = first_weights(lands_0)
    else:
        u = _rms_fwd(x, small["norm_mix_pre"], name="norm_pre", out_dtype=bf16)
    rides = late_rides or (None, None, None, None)
    if late_rides:
        proj, lands_a = _mm(u, wts["w_main"], "nn", tm=1024, tn=1024, name="in_proj", ride=rides[0])
    else:
        proj, lands_a = _mm(u, wts["w_main"], "nn", tm=1024, tn=1024, name="in_proj"), []
    pdt = _mm(u, wts["w_dt"], "nn", tm=1024, tn=128, name="in_proj_dt")
    y_sb, tot_lk, lands_b = _sb_fwd(proj, S, rides[1])
    wts = dict(wts, **late_weights(0, lands_a))
    small = dict(small, conv_w=wts.pop("conv_w"))
    xc, xbc, lands_d = _conv_fwd(proj, small["conv_w"], small["conv_b"], S, rides[3])
    y_ssd, yn, hprev, lands_c = _ssd_fwd(xbc, proj, pdt, dtb_p, alog_p, dsk_c, small["ssd_norm"], S, rides[2])
    wts = dict(wts, **late_weights(1, lands_b), **late_weights(2, lands_c), **late_weights(3, lands_d))
    mn = _rms_fwd(mem, small["norm_mem"], name="norm_mem", out_dtype=bf16, tm=min(512, M))
    kv = _mm(mn, wts["w_mem_kv"], "nn", tm=M, tn=1024, name="mem_kv")
    y_mem = _mem_fwd(proj, kv, S)
    t0 = _mm(y_sb, wts["w_sb_out"], "nn", tm=1024, tn=1024, name="sb_out")
    t1 = _mm(yn, wts["w_ssd_out"], "nn", tm=1024, tn=1024, name="ssd_out")
    t2 = _mm(y_mem, wts["w_mem_out"], "nn", tm=1024, tn=1024, name="mem_out")
    merged = _merge_fwd(proj, t0, t1, t2, S)
    mix = _mm(merged, wts["w_o"], "nn", tm=1024, tn=1024, name="w_o")
    h1 = _rms_fwd(mix, small["norm_mix_post"], name="norm_mix_post", out_dtype=f32, residual=x)
    u2 = _rms_fwd(h1, small["norm_mlp_pre"], name="norm_mlp_pre", out_dtype=bf16)
    a_up, hrelu = _mm(u2, wts["w_up"], "nn", tm=1024, tn=1024, name="mlp_up", out_dtypes=(f32, bf16),
                      epi=lambda acc: (acc, jnp.square(jnp.maximum(acc, 0.0))))
    ff = _mm(hrelu, wts["w_down"], "nn", tm=1024, tn=1024, name="mlp_down")
    dh2, loss = _loss_head(ff, small["norm_mlp_post"], h1, target, S)

    g = {}
    dff, g["norm_mlp_post"] = _rms_bwd(ff, dh2, small["norm_mlp_post"], name="norm_mlp_post_bwd", out_dtype=bf16)
    da = _mm(dff, wts["w_down"], "nt", tm=1024, tn=1024, name="mlp_down_dx", out_dtypes=(bf16,),
             epi=lambda acc, a: (acc * (2.0 * jnp.maximum(a, 0.0)),), extras=(a_up,))
    g["w_down"] = _mm(hrelu, dff, "tn", tm=1024, tn=1024, name="mlp_down_dw")
    du2 = _mm(da, wts["w_up"], "nt", tm=1024, tn=1024, name="mlp_up_dx")
    g["w_up"] = _mm(u2, da, "tn", tm=1024, tn=1024, name="mlp_up_dw")
    dh1, g["norm_mlp_pre"] = _rms_bwd(h1, du2, small["norm_mlp_pre"], name="norm_mlp_pre_bwd", out_dtype=f32, add=dh2)
    dmix, g["norm_mix_post"] = _rms_bwd(mix, dh1, small["norm_mix_post"], name="norm_mix_post_bwd", out_dtype=bf16)
    dmerged = _mm(dmix, wts["w_o"], "nt", tm=1024, tn=1024, name="w_o_dx")
    g["w_o"] = _mm(merged, dmix, "tn", tm=1024, tn=1024, name="w_o_dw")
    dt0, dt1, dt2, dgl = _merge_bwd(proj, t0, t1, t2, dmerged, S)
    dy_sb = _mm(dt0, wts["w_sb_out"], "nt", tm=1024, tn=1024, name="sb_out_dx")
    g["w_sb_out"] = _mm(y_sb, dt0, "tn", tm=1024, tn=1024, name="sb_out_dw")
    dy_ssd = _mm(dt1, wts["w_ssd_out"], "nt", tm=1024, tn=1024, name="ssd_out_dx")
    g["w_ssd_out"] = _mm(yn, dt1, "tn", tm=1024, tn=1024, name="ssd_out_dw")
    dy_mem = _mm(dt2, wts["w_mem_out"], "nt", tm=1024, tn=1024, name="mem_out_dx")
    g["w_mem_out"] = _mm(y_mem, dt2, "tn", tm=1024, tn=1024, name="mem_out_dw")
    dmemq, dkv = _mem_bwd(proj, kv, dy_mem, S)
    g["w_mem_kv"] = _mm(mn, dkv, "tn", tm=1024, tn=1024, name="mem_kv_dw")
    dmn = _mm(dkv, wts["w_mem_kv"], "nt", tm=M, tn=1024, name="mem_kv_dx")
    _, g["norm_mem"] = _rms_bwd(mem, dmn, small["norm_mem"], name="norm_mem_bwd", out_dtype=bf16, tm=min(512, M))
    rides = rest_rides(g) if rest_rides else (None, None)
    dz, dxbc, ddt, g["ssd_norm"], dsk, dalog, ddtb, *lands_a = _ssd_bwd(
        dy_ssd, y_ssd, xbc, proj, pdt, hprev, dtb_p, alog_p, dsk_c, small["ssd_norm"], S, rides[0])
    g["d_skip"], g["a_log"], g["dt_bias"] = dsk[:, :SSD_HEADS], dalog[:, :SSD_HEADS], ddtb[:, :SSD_HEADS]
    dxbc_raw, dcw, g["conv_b"] = _conv_bwd(proj, xc, dxbc, small["conv_w"], S)
    g["conv_w"] = dcw[:CONV_K]
    dq, dk, dv, lands_b = _sb_bwd(proj, tot_lk, dy_sb, S, rides[1])
    g["rest_lands"] = lands_b + lands_a
    dproj = (dq, dk, dv, dxbc_raw, dgl, dmemq, dz)
    u_t = u.T
    g["w_main"] = [_mm(u_t, p, "nn", tm=512, tn=1024, name="in_proj_dw_%d" % i) for i, p in enumerate(dproj)]
    g["w_dt"] = _mm(u_t, ddt, "nn", tm=512, tn=128, name="in_proj_dt_dw")
    du_dt = _mm(ddt, wts["w_dt"], "nt", tm=1024, tn=1024, name="in_proj_dt_dx")
    du, g["w_in_lands"] = _mm_pieces_nt(dproj, wts["w_main"], du_dt, tm=512, tn=256, name="in_proj_dx",
                                        ride=w_in_ride(g) if w_in_ride else None)
    grad_x, g["norm_mix_pre"] = _rms_bwd(x, du, small["norm_mix_pre"], name="norm_pre_bwd", out_dtype=f32, add=dh1)
    return loss, grad_x, g


def _to_internal(w_in):
    sec = lambda r: w_in[:, r[0]:r[1]]
    w_main = jnp.concatenate([sec(R_QKV), sec(R_XBC), sec(R_GATE), sec(R_MEMQ), sec(R_Z)], axis=1)
    w_dt = jnp.pad(sec(R_DT), ((0, 0), (0, 128 - SSD_HEADS)))
    return w_main, w_dt


def _from_internal(pieces, g_dt):
    dq, dk, dv, dxbc, dgate, dmemq, dz = pieces
    return [dq, dk, dv, dz, dxbc, g_dt[:, :SSD_HEADS], dmemq, dgate]


def _w_in_slab(ordered, s, dtype):
    width = D_IN // N_SHARD
    lo, hi, off, parts = s * width, (s + 1) * width, 0, []
    for p in ordered:
        a, b = max(lo, off), min(hi, off + p.shape[1])
        if a < b:
            parts.append(p[:, a - off:b - off].astype(dtype))
        off += p.shape[1]
    return jnp.concatenate(parts, axis=1)


MESH = pl.DeviceIdType.MESH
ANY = pl.BlockSpec(memory_space=pl.ANY)


def _place():
    x, y, c = lax.axis_index("x"), lax.axis_index("y"), lax.axis_index("c")
    return (x, y, c), [(1 - x, y, c), (x, 1 - y, c), (1 - x, 1 - y, c)]


def _exchange_copy(mode, ins, lands, send, recv, a, k, me, peers, arriving):
    p = peers[k]
    theirs = 2 * p[0] + p[1]
    if mode == "gather":
        src, dst = ins[a], lands[a].at[theirs if arriving else me]
    else:
        src, dst = ins[a].at[theirs], lands[a].at[k]
    return pltpu.make_async_remote_copy(src_ref=src, dst_ref=dst, send_sem=send.at[a * 3 + k],
                                        recv_sem=recv.at[a * 3 + k], device_id=p, device_id_type=MESH)


class _Ride:
    def __init__(self, srcs, mode):
        self.srcs, self.mode, self.n = list(srcs), mode, len(srcs)
        n = self.n
        self.in_specs, self.out_specs = [ANY] * n, [ANY] * n
        self.out_shape = [
            jax.ShapeDtypeStruct((N_SHARD,) + s.shape if mode == "gather" else (3,) + s.shape[1:], s.dtype)
            for s in self.srcs]
        self.scratch = [pltpu.SemaphoreType.DMA((3 * n,)), pltpu.SemaphoreType.DMA((3 * n,)),
                        pltpu.SemaphoreType.DMA((n,))]

    def _own(self, ins, lnd, sems):
        if self.mode != "gather":
            return []
        me = 2 * lax.axis_index("x") + lax.axis_index("y")
        return [pltpu.make_async_copy(ins[a], lnd[a].at[me], sems[2].at[a]) for a in range(self.n)]

    def _far(self, ins, lnd, sems, arriving):
        (x, y, c), peers = _place()
        return [_exchange_copy(self.mode, ins, lnd, sems[0], sems[1], a, k, 2 * x + y, peers, arriving)
                for a in range(self.n) for k in range(3)]

    def start(self, ins, lnd, sems):
        for cp in self._own(ins, lnd, sems) + self._far(ins, lnd, sems, False):
            cp.start()

    def finish(self, ins, lnd, sems):
        for cp in self._far(ins, lnd, sems, True):
            cp.wait_recv()
        for cp in self._far(ins, lnd, sems, False):
            cp.wait_send()
        for cp in self._own(ins, lnd, sems):
            cp.wait()


class _TwoLevelGather:
    def __init__(self, shard):
        self.srcs, self.n, self.rows = [shard], 1, shard.shape[0] // 2
        self.in_specs, self.out_specs = [ANY], [ANY]
        self.out_shape = [jax.ShapeDtypeStruct((N_SHARD,) + shard.shape, shard.dtype)]
        self.scratch = [pltpu.SemaphoreType.DMA((6,)), pltpu.SemaphoreType.DMA((6,)), pltpu.SemaphoreType.DMA((1,))]

    def _copy(self, ins, lnd, sems, j, slot, core, to):
        half = lambda ref: ref.at[pl.ds(core * self.rows, self.rows)]
        return pltpu.make_async_remote_copy(
            src_ref=half(ins[0]) if j < 3 else half(lnd[0].at[slot]), dst_ref=half(lnd[0].at[slot]),
            send_sem=sems[0].at[j], recv_sem=sems[1].at[j], device_id=to, device_id_type=MESH)

    def _own(self, ins, lnd, sems):
        me = 2 * lax.axis_index("x") + lax.axis_index("y")
        return pltpu.make_async_copy(ins[0], lnd[0].at[me], sems[2].at[0])

    def start(self, ins, lnd, sems):
        (x, y, c), peers = _place()
        self._own(ins, lnd, sems).start()
        for k in range(3):
            self._copy(ins, lnd, sems, k, 2 * x + y, c, peers[k]).start()

    def finish(self, ins, lnd, sems):
        (x, y, c), peers = _place()
        slots = [2 * p[0] + p[1] for p in peers]
        for k, p in enumerate(peers):
            self._copy(ins, lnd, sems, k, slots[k], c, p).wait_recv()
            self._copy(ins, lnd, sems, 3 + k, slots[k], c, (x, y, 1 - c)).start()
        for k in range(3):
            self._copy(ins, lnd, sems, 3 + k, slots[k], 1 - c, (x, y, 1 - c)).wait_recv()
        for k, p in enumerate(peers):
            self._copy(ins, lnd, sems, k, 2 * x + y, c, p).wait_send()
            self._copy(ins, lnd, sems, 3 + k, slots[k], c, (x, y, 1 - c)).wait_send()
        self._own(ins, lnd, sems).wait()


def _exchange_packets(packet):
    def body(pk, pk_out, send, recv, loc):
        x, y, c = lax.axis_index("x"), lax.axis_index("y"), lax.axis_index("c")
        lin = 4 * x + 2 * y + c
        own = pltpu.make_async_copy(pk, pk_out.at[lin], loc.at[0])
        own.start()

        def pk_copy(m, slot):
            dev = (x ^ ((m >> 2) & 1), y ^ ((m >> 1) & 1), c ^ (m & 1))
            return pltpu.make_async_remote_copy(
                src_ref=pk, dst_ref=pk_out.at[slot], send_sem=send.at[m - 1], recv_sem=recv.at[m - 1],
                device_id=dev, device_id_type=MESH)

        sent = [pk_copy(m, lin) for m in range(1, N_DEV)]
        for cp in sent:
            cp.start()
        for m in range(1, N_DEV):
            pk_copy(m, lin ^ m).wait_recv()
        for cp in sent:
            cp.wait_send()
        own.wait()

    return pl.pallas_call(
        body, name="exchange_packets", in_specs=[ANY], out_specs=ANY,
        out_shape=jax.ShapeDtypeStruct((N_DEV,) + packet.shape, packet.dtype),
        scratch_shapes=[pltpu.SemaphoreType.DMA((N_DEV - 1,)), pltpu.SemaphoreType.DMA((N_DEV - 1,)),
                        pltpu.SemaphoreType.DMA((1,))],
    )(packet)


def _swap_sibling(parts, name):
    n = len(parts)

    def body(*refs):
        ins, outs = refs[:n], refs[n:2 * n]
        send, recv = refs[2 * n:]
        x, y, c = lax.axis_index("x"), lax.axis_index("y"), lax.axis_index("c")
        cps = [pltpu.make_async_remote_copy(
            src_ref=ins[a], dst_ref=outs[a], send_sem=send.at[a], recv_sem=recv.at[a],
            device_id=(x, y, 1 - c), device_id_type=MESH) for a in range(n)]
        for cp in cps:
            cp.start()
        for cp in cps:
            cp.wait_recv()
        for cp in cps:
            cp.wait_send()

    return pl.pallas_call(
        body, name=name,
        in_specs=[ANY] * n, out_specs=[ANY] * n,
        out_shape=[jax.ShapeDtypeStruct(p.shape, p.dtype) for p in parts],
        scratch_shapes=[pltpu.SemaphoreType.DMA((n,)), pltpu.SemaphoreType.DMA((n,))],
    )(*parts)


BLOCK_ELEMS = 256 * 1024


def _row_tile(R, C):
    tr = max(8, (BLOCK_ELEMS // C) // 8 * 8)
    while R % tr:
        tr -= 8
    return min(tr, R)


def _sum_parts(own, stack, name, out_dtype=f32):
    k = stack.shape[0]
    R, C = stack.shape[1:]
    tr = _row_tile(R, C)

    def body(*refs):
        o_ref = refs[-1]
        acc = refs[0][...].astype(f32)
        for r in refs[1:-1]:
            acc = acc + r[...].astype(f32)
        o_ref[...] = acc.astype(out_dtype)

    row = pl.BlockSpec((tr, C), lambda i: (i, 0))
    specs = ([row] if own is not None else []) + [
        pl.BlockSpec((None, tr, C), functools.partial(lambda i, j: (j, i, 0), j=j)) for j in range(k)]
    args = ([own] if own is not None else []) + [stack] * k
    return pl.pallas_call(
        body, name=name, grid=(R // tr,), in_specs=specs, out_specs=row,
        out_shape=jax.ShapeDtypeStruct((R, C), out_dtype), compiler_params=_params(("parallel",)),
    )(*args)


def _adamw(w, m, v, g_parts, name):
    R, C = w.shape
    tr = _row_tile(R, C)
    n_g = len(g_parts)

    def body(w_ref, m_ref, v_ref, *rest):
        g = rest[0][...]
        for r in rest[1:n_g]:
            g = g + r[...]
        g_ref, d_ref, nm_ref, nv_ref = rest[n_g:]
        nm = ADAM_B1 * m_ref[...] + (1.0 - ADAM_B1) * g
        nv = ADAM_B2 * v_ref[...] + (1.0 - ADAM_B2) * jnp.square(g)
        m_hat = nm / (1.0 - ADAM_B1 ** ADAM_STEP)
        v_hat = nv / (1.0 - ADAM_B2 ** ADAM_STEP)
        g_ref[...] = g
        d_ref[...] = -ADAM_LR * (m_hat / (jnp.sqrt(v_hat) + ADAM_EPS) + ADAM_WD * w_ref[...])
        nm_ref[...] = nm
        nv_ref[...] = nv

    row = pl.BlockSpec((tr, C), lambda i: (i, 0))
    return pl.pallas_call(
        body, name=name, grid=(R // tr,), in_specs=[row] * (3 + n_g), out_specs=[row] * 4,
        out_shape=[jax.ShapeDtypeStruct((R, C), f32)] * 4, compiler_params=_params(("parallel",)),
    )(w, m, v, *g_parts)


BIG = ("w_in", "w_mem_kv", "w_sb_out", "w_ssd_out", "w_mem_out", "w_o", "w_up", "w_down")
LATE = ("w_sb_out", "w_ssd_out", "w_mem_out", "w_o", "w_up", "w_down")
REST = BIG[1:]
COL_SHARDED = ("w_in", "w_mem_kv", "w_up")
SMALL = ("norm_mix_pre", "conv_w", "conv_b", "dt_bias", "a_log", "d_skip", "ssd_norm", "norm_mem",
         "norm_mix_post", "norm_mlp_pre", "norm_mlp_post")
WEIGHTS = ("norm_mix_pre", "w_in", "conv_w", "conv_b", "dt_bias", "a_log", "d_skip", "ssd_norm", "norm_mem",
           "w_mem_kv", "w_sb_out", "w_ssd_out", "w_mem_out", "w_o", "norm_mix_post", "norm_mlp_pre", "w_up",
           "w_down", "norm_mlp_post")
PK_ROWS = 184


def _pack(vecs):
    flat = jnp.concatenate([v.reshape(-1) for v in vecs])
    return jnp.pad(flat, (0, PK_ROWS * 128 - flat.shape[0])).reshape(PK_ROWS, 128)


def _unpack(pk, shapes):
    flat = pk.reshape(-1)
    out, off = [], 0
    for s in shapes:
        n = 1
        for d in s:
            n *= d
        out.append(flat[off:off + n].reshape(s))
        off += n
    return out


def _full_from_slabs(name, slabs):
    if name in COL_SHARDED:
        return slabs.transpose(1, 0, 2).reshape(slabs.shape[1], -1)
    return slabs.reshape(-1, slabs.shape[2])


def _slabs_from_full(name, g):
    if name in COL_SHARDED:
        return g.reshape(g.shape[0], N_SHARD, -1).transpose(1, 0, 2)
    return g.reshape(N_SHARD, -1, g.shape[1])


def kernel(x, mem, norm_mix_pre, w_in, conv_w, conv_b, dt_bias, a_log, d_skip, ssd_norm, norm_mem, w_mem_kv, w_sb_out, w_ssd_out, w_mem_out, w_o, norm_mix_post, norm_mlp_pre, w_up, w_down, norm_mlp_post, loss_target, m_norm_mix_pre, m_w_in, m_conv_w, m_conv_b, m_dt_bias, m_a_log, m_d_skip, m_ssd_norm, m_norm_mem, m_w_mem_kv, m_w_sb_out, m_w_ssd_out, m_w_mem_out, m_w_o, m_norm_mix_post, m_norm_mlp_pre, m_w_up, m_w_down, m_norm_mlp_post, v_norm_mix_pre, v_w_in, v_conv_w, v_conv_b, v_dt_bias, v_a_log, v_d_skip, v_ssd_norm, v_norm_mem, v_w_mem_kv, v_w_sb_out, v_w_ssd_out, v_w_mem_out, v_w_o, v_norm_mix_post, v_norm_mlp_pre, v_w_up, v_w_down, v_norm_mlp_post):
    env = dict(locals())
    w = {n: env[n] for n in WEIGHTS}
    mo = {n: env["m_" + n] for n in WEIGHTS}
    vo = {n: env["v_" + n] for n in WEIGHTS}
    shard = 2 * lax.axis_index("x") + lax.axis_index("y")

    def first_weights(lands):
        w_main, w_dt = _to_internal(_full_from_slabs("w_in", lands[0]))
        return dict(w_main=w_main, w_dt=w_dt)

    wts = lambda _: (_TwoLevelGather(w["w_in"][0].astype(bf16)), first_weights)
    ride_names = (LATE[:4], LATE[4:5], LATE[5:], ("w_mem_kv",))
    late_rides = tuple(_Ride([w[n][0].astype(bf16) for n in names] + ([w["conv_w"][0]] if i == 0 else []), "gather")
                       for i, names in enumerate(ride_names))

    def late_weights(i, lands):
        full = {n: _full_from_slabs(n, s) for n, s in zip(ride_names[i], lands)}
        if i == 0:
            full["conv_w"] = lands[-1].transpose(1, 0, 2).reshape(CONV_K, CONV_DIM)
        return full

    def rest_rides(g):
        slabs = [_slabs_from_full(n, g[n]).astype(bf16) for n in REST]
        return _Ride(slabs[5:], "scatter"), _Ride(slabs[:5], "scatter")

    core = lax.axis_index("c")
    half = D // 2

    def w_in_ride(g):
        ordered = _from_internal(g["w_main"], g["w_dt"])
        stack = jnp.stack([_w_in_slab(ordered, s, bf16) for s in range(N_SHARD)])
        keep = lax.dynamic_slice_in_dim(stack, core * half, half, axis=1)
        away = lax.dynamic_slice_in_dim(stack, (1 - core) * half, half, axis=1)
        (got,) = _swap_sibling([away], "w_in_halves_out")
        wide = lambda a: a.reshape(N_SHARD * half, -1)
        chip = _sum_parts(wide(keep), wide(got)[None], "sum_cores_w_in", bf16).reshape(N_SHARD, half, -1)
        own = lax.switch(shard, [functools.partial(_w_in_slab, ordered, s, f32) for s in range(N_SHARD)])
        own = lax.dynamic_slice_in_dim(own, core * half, half, axis=0)
        g["w_in_own"] = _sum_parts(own, lax.dynamic_index_in_dim(got, shard, 0, keepdims=True), "sum_cores_w_in_own")
        return _Ride([chip], "scatter")

    small = {n: w[n] for n in SMALL if n != "conv_w"}
    loss, grad_x, g = _local_step(x[0], mem[0], loss_target[0], wts, late_rides, late_weights, small,
                                  rest_rides, w_in_ride)
    out_g, out_d, out_m, out_v = {}, {}, {}, {}

    def apply(n, g_parts):
        res = _adamw(w[n][0], mo[n][0], vo[n][0], g_parts, name="adamw_" + n)
        out_g[n], out_d[n], out_m[n], out_v[n] = [r[None] for r in res]

    mine = _sum_parts(g["w_in_own"], g["w_in_lands"][0], name="sum_chips_w_in")
    (theirs,) = _swap_sibling([mine], "w_in_halves_back")
    g_w_in = lax.dynamic_update_slice_in_dim(jnp.zeros((D, D_IN // N_SHARD), f32), mine, core * half, axis=0)
    apply("w_in", [lax.dynamic_update_slice_in_dim(g_w_in, theirs, (1 - core) * half, axis=0)])

    packets = _exchange_packets(_pack([g[n] for n in SMALL] + [loss[:, :1]]))
    partial = []
    for n, r in zip(REST, g["rest_lands"]):
        own = lax.dynamic_index_in_dim(_slabs_from_full(n, g[n]), shard, 0, keepdims=False)
        partial.append(_sum_parts(own, r, name="sum_chips_" + n))
    other = _swap_sibling(partial, "swap_sibling")

    for n, p, q in zip(REST, partial, other):
        apply(n, [p, q])
    tot = _sum_parts(None, packets, name="sum_packets")
    shapes = [g[n].shape for n in SMALL] + [(1, 1)]
    sm = dict(zip(SMALL + ("loss",), _unpack(tot, shapes)))
    sm["conv_w"] = lax.dynamic_slice_in_dim(sm["conv_w"], shard * (CONV_DIM // N_SHARD), CONV_DIM // N_SHARD, axis=1)
    own_small = lambda d: _pack([d[n].reshape(sm[n].shape) for n in SMALL])
    res = _adamw(own_small(w), own_small(mo), own_small(vo), [own_small(sm)], name="adamw_small")
    own_shapes = [sm[n].shape for n in SMALL]
    for store, r in zip((out_g, out_d, out_m, out_v), res):
        for n, val in zip(SMALL, _unpack(r, own_shapes)):
            store[n] = val.reshape(w[n].shape)

    outs = [sm["loss"].reshape(()), grad_x[None]]
    for store in (out_g, out_d, out_m, out_v):
        outs += [store[n] for n in WEIGHTS]
    return tuple(outs)
```
